```python
import jax, jax.numpy as jnp
from jax import lax
import numpy as np

D_MODEL = 1024
BATCH = 16
SEQ = 2048
DEPTH = 2

CHUNK = 64
N_MIXERS = 2
HEAD_DIM = 64
MEM_TOKENS = 256
MEM_HEADS = 4
MEM_WIDTH = MEM_HEADS * HEAD_DIM
TOK_WIDTH = D_MODEL - MEM_WIDTH
ATT_HEADS = TOK_WIDTH // HEAD_DIM
LEFT_CHUNKS = 8
BAND = (LEFT_CHUNKS + 1) * CHUNK
BAND_PAD = LEFT_CHUNKS * CHUNK
REL_CLIP = 128
N_REL = REL_CLIP + CHUNK
CONV_WIDTH = 31
CONV_CH = TOK_WIDTH
A_IN = 3 * TOK_WIDTH + MEM_WIDTH
B_IN = 2 * CONV_CH + MEM_WIDTH
D_FF = -(-8 * D_MODEL // (3 * 256)) * 256
EPS = 1e-6
NEG_INF = -1e30
ATTN_SCALE = HEAD_DIM ** -0.5

_DIST = np.arange(CHUNK)[:, None] - np.arange(BAND)[None, :] + BAND_PAD
REL_IDX = np.clip(_DIST, -(CHUNK - 1), REL_CLIP) + (CHUNK - 1)
BAND_OFF = np.arange(BAND) - BAND_PAD

kernel_name = "hybrid_chunkattn_conformerconv_memxattn"


def rms_norm(x, g):
    xf = x.astype(jnp.float32)
    y = xf * lax.rsqrt(jnp.mean(xf * xf, axis=-1, keepdims=True) + EPS)
    return (y * g.astype(jnp.float32)).astype(x.dtype)


def layer_norm(x, g, b):
    xf = x.astype(jnp.float32)
    mu = jnp.mean(xf, axis=-1, keepdims=True)
    xc = xf - mu
    y = xc * lax.rsqrt(jnp.mean(xc * xc, axis=-1, keepdims=True) + EPS)
    return (y * g.astype(jnp.float32) + b.astype(jnp.float32)).astype(x.dtype)


def chunk_relpos_attention(q, k, v, rel_bias):
    B, S, H, Dh = q.shape
    nc = S // CHUNK
    kp = jnp.pad(k, ((0, 0), (BAND_PAD, 0), (0, 0), (0, 0)))
    vp = jnp.pad(v, ((0, 0), (BAND_PAD, 0), (0, 0), (0, 0)))
    qc = q.reshape(B, nc, CHUNK, H, Dh).transpose(1, 0, 2, 3, 4)
    bias = rel_bias[:, REL_IDX].astype(jnp.float32)
    band_off = jnp.asarray(BAND_OFF, dtype=jnp.int32)

    def one_chunk(args):
        c, q_c = args
        start = c * CHUNK
        k_b = lax.dynamic_slice_in_dim(kp, start, BAND, axis=1)
        v_b = lax.dynamic_slice_in_dim(vp, start, BAND, axis=1)
        s = jnp.einsum('bqhd,bkhd->bhqk', q_c, k_b).astype(jnp.float32) * ATTN_SCALE + bias
        valid = (start + band_off) >= 0
        s = jnp.where(valid, s, NEG_INF)
        p = jax.nn.softmax(s, axis=-1).astype(v.dtype)
        return jnp.einsum('bhqk,bkhd->bqhd', p, v_b)

    out = lax.map(one_chunk, (jnp.arange(nc, dtype=jnp.int32), qc))
    return out.transpose(1, 0, 2, 3, 4).reshape(B, S, H * Dh)


def conformer_conv(u, conv_w, conv_b, ln_g, ln_b):
    a, gate = jnp.split(u, 2, axis=-1)
    h = a * jax.nn.sigmoid(gate)
    hp = jnp.pad(h, ((0, 0), (CONV_WIDTH - 1, 0), (0, 0)))
    y = lax.conv_general_dilated(
        hp, conv_w[:, None, :].astype(h.dtype), window_strides=(1,), padding='VALID',
        dimension_numbers=('NWC', 'WIO', 'NWC'), feature_group_count=CONV_CH)
    y = y + conv_b
    return jax.nn.silu(layer_norm(y, ln_g, ln_b))


def memory_attention(qm, mem_n, w_mem_kv, mq_g, mk_g):
    B, S, _ = qm.shape
    M = mem_n.shape[1]
    km, vm = jnp.split(mem_n @ w_mem_kv, 2, axis=-1)
    q = rms_norm(qm.reshape(B, S, MEM_HEADS, HEAD_DIM), mq_g)
    k = rms_norm(km.reshape(B, M, MEM_HEADS, HEAD_DIM), mk_g)
    v = vm.reshape(B, M, MEM_HEADS, HEAD_DIM)
    s = jnp.einsum('bshd,bmhd->bhsm', q, k).astype(jnp.float32) * ATTN_SCALE
    p = jax.nn.softmax(s, axis=-1).astype(v.dtype)
    return jnp.einsum('bhsm,bmhd->bshd', p, v).reshape(B, S, MEM_WIDTH)


def _fwd_setup_inputs(seed: int = 0) -> dict:
    key = jax.random.key(seed)
    ks = jax.random.split(key, 22)
    n_a = (DEPTH + N_MIXERS - 1) // N_MIXERS
    n_b = DEPTH // N_MIXERS
    f32 = jnp.float32

    def w(k, shape, fan_in):
        return jax.random.normal(k, shape, f32) * (fan_in ** -0.5)

    def gain(k, shape):
        return 1.0 + 0.05 * jax.random.normal(k, shape, f32)

    def small(k, shape, s=0.02):
        return s * jax.random.normal(k, shape, f32)

    return {
        "x": jax.random.normal(ks[0], (BATCH, SEQ, D_MODEL), f32),
        "mem": jax.random.normal(ks[1], (BATCH, MEM_TOKENS, D_MODEL), f32),
        "norm1_g": gain(ks[2], (DEPTH, D_MODEL)),
        "mem_norm_g": gain(ks[3], (DEPTH, D_MODEL)),
        "a_w_in": w(ks[4], (n_a, D_MODEL, A_IN), D_MODEL),
        "a_q_g": gain(ks[5], (n_a, HEAD_DIM)),
        "a_k_g": gain(ks[6], (n_a, HEAD_DIM)),
        "a_rel_bias": small(ks[7], (n_a, ATT_HEADS, N_REL), 0.3),
        "b_w_in": w(ks[8], (n_b, D_MODEL, B_IN), D_MODEL),
        "b_b_in": small(ks[9], (n_b, B_IN)),
        "b_conv_w": w(ks[10], (n_b, CONV_WIDTH, CONV_CH), CONV_WIDTH),
        "b_conv_b": small(ks[11], (n_b, CONV_CH)),
        "b_ln_g": gain(ks[12], (n_b, CONV_CH)),
        "b_ln_b": small(ks[13], (n_b, CONV_CH)),
        "mq_g": gain(ks[14], (DEPTH, HEAD_DIM)),
        "mk_g": gain(ks[15], (DEPTH, HEAD_DIM)),
        "w_mem_kv": w(ks[16], (DEPTH, D_MODEL, 2 * MEM_WIDTH), D_MODEL),
        "w_out": w(ks[17], (DEPTH, D_MODEL, D_MODEL), D_MODEL),
        "norm2_g": gain(ks[18], (DEPTH, D_MODEL)),
        "w_gate": w(ks[19], (DEPTH, D_MODEL, D_FF), D_MODEL),
        "w_up": w(ks[20], (DEPTH, D_MODEL, D_FF), D_MODEL),
        "w_down": w(ks[21], (DEPTH, D_FF, D_MODEL), D_FF),
    }


def _fwd_reference(x, mem, norm1_g, mem_norm_g, a_w_in, a_q_g, a_k_g, a_rel_bias,
              b_w_in, b_b_in, b_conv_w, b_conv_b, b_ln_g, b_ln_b,
              mq_g, mk_g, w_mem_kv, w_out, norm2_g, w_gate, w_up, w_down):
    B, S, _ = x.shape
    for i in range(DEPTH):
        j = i // N_MIXERS
        h = rms_norm(x, norm1_g[i])
        mem_n = rms_norm(mem, mem_norm_g[i])
        if i % N_MIXERS == 0:
            z = h @ a_w_in[j]
            q, k, v, qm = jnp.split(z, [TOK_WIDTH, 2 * TOK_WIDTH, 3 * TOK_WIDTH], axis=-1)
            q = rms_norm(q.reshape(B, S, ATT_HEADS, HEAD_DIM), a_q_g[j])
            k = rms_norm(k.reshape(B, S, ATT_HEADS, HEAD_DIM), a_k_g[j])
            v = v.reshape(B, S, ATT_HEADS, HEAD_DIM)
            tok = chunk_relpos_attention(q, k, v, a_rel_bias[j])
        else:
            z = h @ b_w_in[j] + b_b_in[j]
            u, qm = jnp.split(z, [2 * CONV_CH], axis=-1)
            tok = conformer_conv(u, b_conv_w[j], b_conv_b[j], b_ln_g[j], b_ln_b[j])
        memo = memory_attention(qm, mem_n, w_mem_kv[i], mq_g[i], mk_g[i])
        x = x + jnp.concatenate([tok, memo], axis=-1) @ w_out[i]
        h2 = rms_norm(x, norm2_g[i])
        x = x + (jax.nn.silu(h2 @ w_gate[i]) * (h2 @ w_up[i])) @ w_down[i]
    return x


import jax as _jax
import jax.numpy as _jnp

TWIN_FORMAT = 'train_step'
FWD_PARAMS = ['x', 'mem', 'norm1_g', 'mem_norm_g', 'a_w_in', 'a_q_g', 'a_k_g', 'a_rel_bias', 'b_w_in', 'b_b_in', 'b_conv_w', 'b_conv_b', 'b_ln_g', 'b_ln_b', 'mq_g', 'mk_g', 'w_mem_kv', 'w_out', 'norm2_g', 'w_gate', 'w_up', 'w_down']
TWIN_WEIGHTS = ['norm1_g', 'mem_norm_g', 'a_w_in', 'a_q_g', 'a_k_g', 'a_rel_bias', 'b_w_in', 'b_b_in', 'b_conv_w', 'b_conv_b', 'b_ln_g', 'b_ln_b', 'mq_g', 'mk_g', 'w_mem_kv', 'w_out', 'norm2_g', 'w_gate', 'w_up', 'w_down']
TWIN_DIFF_INPUT = 'x'
TWIN_INPUTS = ['x', 'mem', 'norm1_g', 'mem_norm_g', 'a_w_in', 'a_q_g', 'a_k_g', 'a_rel_bias', 'b_w_in', 'b_b_in', 'b_conv_w', 'b_conv_b', 'b_ln_g', 'b_ln_b', 'mq_g', 'mk_g', 'w_mem_kv', 'w_out', 'norm2_g', 'w_gate', 'w_up', 'w_down', 'loss_target', 'm_norm1_g', 'm_mem_norm_g', 'm_a_w_in', 'm_a_q_g', 'm_a_k_g', 'm_a_rel_bias', 'm_b_w_in', 'm_b_b_in', 'm_b_conv_w', 'm_b_conv_b', 'm_b_ln_g', 'm_b_ln_b', 'm_mq_g', 'm_mk_g', 'm_w_mem_kv', 'm_w_out', 'm_norm2_g', 'm_w_gate', 'm_w_up', 'm_w_down', 'v_norm1_g', 'v_mem_norm_g', 'v_a_w_in', 'v_a_q_g', 'v_a_k_g', 'v_a_rel_bias', 'v_b_w_in', 'v_b_b_in', 'v_b_conv_w', 'v_b_conv_b', 'v_b_ln_g', 'v_b_ln_b', 'v_mq_g', 'v_mk_g', 'v_w_mem_kv', 'v_w_out', 'v_norm2_g', 'v_w_gate', 'v_w_up', 'v_w_down']
TWIN_OUTPUTS = ['loss', 'grad_x', 'grad_norm1_g', 'grad_mem_norm_g', 'grad_a_w_in', 'grad_a_q_g', 'grad_a_k_g', 'grad_a_rel_bias', 'grad_b_w_in', 'grad_b_b_in', 'grad_b_conv_w', 'grad_b_conv_b', 'grad_b_ln_g', 'grad_b_ln_b', 'grad_mq_g', 'grad_mk_g', 'grad_w_mem_kv', 'grad_w_out', 'grad_norm2_g', 'grad_w_gate', 'grad_w_up', 'grad_w_down', 'delta_norm1_g', 'delta_mem_norm_g', 'delta_a_w_in', 'delta_a_q_g', 'delta_a_k_g', 'delta_a_rel_bias', 'delta_b_w_in', 'delta_b_b_in', 'delta_b_conv_w', 'delta_b_conv_b', 'delta_b_ln_g', 'delta_b_ln_b', 'delta_mq_g', 'delta_mk_g', 'delta_w_mem_kv', 'delta_w_out', 'delta_norm2_g', 'delta_w_gate', 'delta_w_up', 'delta_w_down', 'new_m_norm1_g', 'new_m_mem_norm_g', 'new_m_a_w_in', 'new_m_a_q_g', 'new_m_a_k_g', 'new_m_a_rel_bias', 'new_m_b_w_in', 'new_m_b_b_in', 'new_m_b_conv_w', 'new_m_b_conv_b', 'new_m_b_ln_g', 'new_m_b_ln_b', 'new_m_mq_g', 'new_m_mk_g', 'new_m_w_mem_kv', 'new_m_w_out', 'new_m_norm2_g', 'new_m_w_gate', 'new_m_w_up', 'new_m_w_down', 'new_v_norm1_g', 'new_v_mem_norm_g', 'new_v_a_w_in', 'new_v_a_q_g', 'new_v_a_k_g', 'new_v_a_rel_bias', 'new_v_b_w_in', 'new_v_b_b_in', 'new_v_b_conv_w', 'new_v_b_conv_b', 'new_v_b_ln_g', 'new_v_b_ln_b', 'new_v_mq_g', 'new_v_mk_g', 'new_v_w_mem_kv', 'new_v_w_out', 'new_v_norm2_g', 'new_v_w_gate', 'new_v_w_up', 'new_v_w_down']
TWIN_LEAF_KINDS = {'loss': 'loss', 'grad_x': 'grad_x', 'grad_norm1_g': 'grad_w', 'grad_mem_norm_g': 'grad_w', 'grad_a_w_in': 'grad_w', 'grad_a_q_g': 'grad_w', 'grad_a_k_g': 'grad_w', 'grad_a_rel_bias': 'grad_w', 'grad_b_w_in': 'grad_w', 'grad_b_b_in': 'grad_w', 'grad_b_conv_w': 'grad_w', 'grad_b_conv_b': 'grad_w', 'grad_b_ln_g': 'grad_w', 'grad_b_ln_b': 'grad_w', 'grad_mq_g': 'grad_w', 'grad_mk_g': 'grad_w', 'grad_w_mem_kv': 'grad_w', 'grad_w_out': 'grad_w', 'grad_norm2_g': 'grad_w', 'grad_w_gate': 'grad_w', 'grad_w_up': 'grad_w', 'grad_w_down': 'grad_w', 'delta_norm1_g': 'delta_w', 'delta_mem_norm_g': 'delta_w', 'delta_a_w_in': 'delta_w', 'delta_a_q_g': 'delta_w', 'delta_a_k_g': 'delta_w', 'delta_a_rel_bias': 'delta_w', 'delta_b_w_in': 'delta_w', 'delta_b_b_in': 'delta_w', 'delta_b_conv_w': 'delta_w', 'delta_b_conv_b': 'delta_w', 'delta_b_ln_g': 'delta_w', 'delta_b_ln_b': 'delta_w', 'delta_mq_g': 'delta_w', 'delta_mk_g': 'delta_w', 'delta_w_mem_kv': 'delta_w', 'delta_w_out': 'delta_w', 'delta_norm2_g': 'delta_w', 'delta_w_gate': 'delta_w', 'delta_w_up': 'delta_w', 'delta_w_down': 'delta_w', 'new_m_norm1_g': 'new_m', 'new_m_mem_norm_g': 'new_m', 'new_m_a_w_in': 'new_m', 'new_m_a_q_g': 'new_m', 'new_m_a_k_g': 'new_m', 'new_m_a_rel_bias': 'new_m', 'new_m_b_w_in': 'new_m', 'new_m_b_b_in': 'new_m', 'new_m_b_conv_w': 'new_m', 'new_m_b_conv_b': 'new_m', 'new_m_b_ln_g': 'new_m', 'new_m_b_ln_b': 'new_m', 'new_m_mq_g': 'new_m', 'new_m_mk_g': 'new_m', 'new_m_w_mem_kv': 'new_m', 'new_m_w_out': 'new_m', 'new_m_norm2_g': 'new_m', 'new_m_w_gate': 'new_m', 'new_m_w_up': 'new_m', 'new_m_w_down': 'new_m', 'new_v_norm1_g': 'new_v', 'new_v_mem_norm_g': 'new_v', 'new_v_a_w_in': 'new_v', 'new_v_a_q_g': 'new_v', 'new_v_a_k_g': 'new_v', 'new_v_a_rel_bias': 'new_v', 'new_v_b_w_in': 'new_v', 'new_v_b_b_in': 'new_v', 'new_v_b_conv_w': 'new_v', 'new_v_b_conv_b': 'new_v', 'new_v_b_ln_g': 'new_v', 'new_v_b_ln_b': 'new_v', 'new_v_mq_g': 'new_v', 'new_v_mk_g': 'new_v', 'new_v_w_mem_kv': 'new_v', 'new_v_w_out': 'new_v', 'new_v_norm2_g': 'new_v', 'new_v_w_gate': 'new_v', 'new_v_w_up': 'new_v', 'new_v_w_down': 'new_v'}


def _forward(args):
    return _fwd_reference(*[args[k] for k in FWD_PARAMS])


def _output_shape():
    out = _jax.eval_shape(lambda: _forward(_fwd_setup_inputs(0)))
    return out.shape, out.dtype

N_MICROBATCH = 1
ADAM_LR = 0.001
ADAM_B1 = 0.9
ADAM_B2 = 0.999
ADAM_EPS = 1e-08
ADAM_WD = 0.01
ADAM_STEP = 10
PER_EXAMPLE_BATCH_AXIS = {'x': 0, 'mem': 0, 'loss_target': 0}
SHARED_INPUTS = []
_WEIGHT_DTYPES = {'norm1_g': _jnp.float32, 'mem_norm_g': _jnp.float32, 'a_w_in': _jnp.float32, 'a_q_g': _jnp.float32, 'a_k_g': _jnp.float32, 'a_rel_bias': _jnp.float32, 'b_w_in': _jnp.float32, 'b_b_in': _jnp.float32, 'b_conv_w': _jnp.float32, 'b_conv_b': _jnp.float32, 'b_ln_g': _jnp.float32, 'b_ln_b': _jnp.float32, 'mq_g': _jnp.float32, 'mk_g': _jnp.float32, 'w_mem_kv': _jnp.float32, 'w_out': _jnp.float32, 'norm2_g': _jnp.float32, 'w_gate': _jnp.float32, 'w_up': _jnp.float32, 'w_down': _jnp.float32}
MOMENT_SCALE = {'norm1_g': 2.116497e-01, 'mem_norm_g': 1.244089e-01, 'a_w_in': 9.741065e-02, 'a_q_g': 1.351128e+00, 'a_k_g': 1.369862e+00, 'a_rel_bias': 5.103570e-02, 'b_w_in': 1.890975e-01, 'b_b_in': 2.721860e+00, 'b_conv_w': 4.330171e-01, 'b_conv_b': 7.107127e+00, 'b_ln_g': 1.417915e+01, 'b_ln_b': 9.705352e+00, 'mq_g': 1.256270e+00, 'mk_g': 1.258209e+00, 'w_mem_kv': 1.436314e-01, 'w_out': 8.031370e-01, 'norm2_g': 2.489698e+01, 'w_gate': 2.833126e-01, 'w_up': 2.042413e-01, 'w_down': 3.193203e-01}


def _to_microbatches(a, axis):
    t = _jnp.moveaxis(a, axis, 0)
    t = t.reshape((N_MICROBATCH, t.shape[0] // N_MICROBATCH) + t.shape[1:])
    return _jnp.moveaxis(t, 1, axis + 1)


def setup_inputs(seed: int = 0) -> dict:
    inp = _fwd_setup_inputs(seed)
    key = _jax.random.fold_in(_jax.random.key(seed), 7919)
    shape, _ = _output_shape()
    out = dict(inp)
    out["loss_target"] = _jax.random.normal(_jax.random.fold_in(key, 0), shape, _jnp.float32)
    for i, name in enumerate(TWIN_WEIGHTS):
        w = inp[name].astype(_jnp.float32)
        if MOMENT_SCALE is None:
            s = _jnp.sqrt(_jnp.mean(_jnp.square(w)) + 1e-30)
        else:
            s = MOMENT_SCALE[name]
        km, kv = _jax.random.split(_jax.random.fold_in(key, i + 1))
        out[name] = w
        out["m_" + name] = s * _jax.random.normal(km, w.shape, _jnp.float32)
        out["v_" + name] = (s * s) * _jax.random.uniform(kv, w.shape, _jnp.float32, 0.5, 1.5)
    if N_MICROBATCH > 1:
        for name, axis in PER_EXAMPLE_BATCH_AXIS.items():
            out[name] = _to_microbatches(out[name], axis)
    return {'x': out['x'], 'mem': out['mem'], 'norm1_g': out['norm1_g'], 'mem_norm_g': out['mem_norm_g'], 'a_w_in': out['a_w_in'], 'a_q_g': out['a_q_g'], 'a_k_g': out['a_k_g'], 'a_rel_bias': out['a_rel_bias'], 'b_w_in': out['b_w_in'], 'b_b_in': out['b_b_in'], 'b_conv_w': out['b_conv_w'], 'b_conv_b': out['b_conv_b'], 'b_ln_g': out['b_ln_g'], 'b_ln_b': out['b_ln_b'], 'mq_g': out['mq_g'], 'mk_g': out['mk_g'], 'w_mem_kv': out['w_mem_kv'], 'w_out': out['w_out'], 'norm2_g': out['norm2_g'], 'w_gate': out['w_gate'], 'w_up': out['w_up'], 'w_down': out['w_down'], 'loss_target': out['loss_target'], 'm_norm1_g': out['m_norm1_g'], 'm_mem_norm_g': out['m_mem_norm_g'], 'm_a_w_in': out['m_a_w_in'], 'm_a_q_g': out['m_a_q_g'], 'm_a_k_g': out['m_a_k_g'], 'm_a_rel_bias': out['m_a_rel_bias'], 'm_b_w_in': out['m_b_w_in'], 'm_b_b_in': out['m_b_b_in'], 'm_b_conv_w': out['m_b_conv_w'], 'm_b_conv_b': out['m_b_conv_b'], 'm_b_ln_g': out['m_b_ln_g'], 'm_b_ln_b': out['m_b_ln_b'], 'm_mq_g': out['m_mq_g'], 'm_mk_g': out['m_mk_g'], 'm_w_mem_kv': out['m_w_mem_kv'], 'm_w_out': out['m_w_out'], 'm_norm2_g': out['m_norm2_g'], 'm_w_gate': out['m_w_gate'], 'm_w_up': out['m_w_up'], 'm_w_down': out['m_w_down'], 'v_norm1_g': out['v_norm1_g'], 'v_mem_norm_g': out['v_mem_norm_g'], 'v_a_w_in': out['v_a_w_in'], 'v_a_q_g': out['v_a_q_g'], 'v_a_k_g': out['v_a_k_g'], 'v_a_rel_bias': out['v_a_rel_bias'], 'v_b_w_in': out['v_b_w_in'], 'v_b_b_in': out['v_b_b_in'], 'v_b_conv_w': out['v_b_conv_w'], 'v_b_conv_b': out['v_b_conv_b'], 'v_b_ln_g': out['v_b_ln_g'], 'v_b_ln_b': out['v_b_ln_b'], 'v_mq_g': out['v_mq_g'], 'v_mk_g': out['v_mk_g'], 'v_w_mem_kv': out['v_w_mem_kv'], 'v_w_out': out['v_w_out'], 'v_norm2_g': out['v_norm2_g'], 'v_w_gate': out['v_w_gate'], 'v_w_up': out['v_w_up'], 'v_w_down': out['v_w_down']}


def _loss(weights, diff, rest, loss_target):
    with _jax.named_scope("forward"):
        args = {**rest, TWIN_DIFF_INPUT: diff, **{k: w.astype(_WEIGHT_DTYPES[k]) for k, w in weights.items()}}
        y = _forward(args)
    with _jax.named_scope("loss_head"):
        err = _jnp.square(y.astype(_jnp.float32) - loss_target)
        return 0.5 * _jnp.sum(_jnp.mean(err, axis=-1)) if err.ndim else 0.5 * err


def _adamw(w, g, m, v):
    m = ADAM_B1 * m + (1.0 - ADAM_B1) * g
    v = ADAM_B2 * v + (1.0 - ADAM_B2) * _jnp.square(g)
    m_hat = m / (1.0 - ADAM_B1 ** ADAM_STEP)
    v_hat = v / (1.0 - ADAM_B2 ** ADAM_STEP)
    delta = -ADAM_LR * (m_hat / (_jnp.sqrt(v_hat) + ADAM_EPS) + ADAM_WD * w)
    return delta, m, v


def reference(x, mem, norm1_g, mem_norm_g, a_w_in, a_q_g, a_k_g, a_rel_bias, b_w_in, b_b_in, b_conv_w, b_conv_b, b_ln_g, b_ln_b, mq_g, mk_g, w_mem_kv, w_out, norm2_g, w_gate, w_up, w_down, loss_target, m_norm1_g, m_mem_norm_g, m_a_w_in, m_a_q_g, m_a_k_g, m_a_rel_bias, m_b_w_in, m_b_b_in, m_b_conv_w, m_b_conv_b, m_b_ln_g, m_b_ln_b, m_mq_g, m_mk_g, m_w_mem_kv, m_w_out, m_norm2_g, m_w_gate, m_w_up, m_w_down, v_norm1_g, v_mem_norm_g, v_a_w_in, v_a_q_g, v_a_k_g, v_a_rel_bias, v_b_w_in, v_b_b_in, v_b_conv_w, v_b_conv_b, v_b_ln_g, v_b_ln_b, v_mq_g, v_mk_g, v_w_mem_kv, v_w_out, v_norm2_g, v_w_gate, v_w_up, v_w_down):
    given = dict(x=x, mem=mem, norm1_g=norm1_g, mem_norm_g=mem_norm_g, a_w_in=a_w_in, a_q_g=a_q_g, a_k_g=a_k_g, a_rel_bias=a_rel_bias, b_w_in=b_w_in, b_b_in=b_b_in, b_conv_w=b_conv_w, b_conv_b=b_conv_b, b_ln_g=b_ln_g, b_ln_b=b_ln_b, mq_g=mq_g, mk_g=mk_g, w_mem_kv=w_mem_kv, w_out=w_out, norm2_g=norm2_g, w_gate=w_gate, w_up=w_up, w_down=w_down, loss_target=loss_target, m_norm1_g=m_norm1_g, m_mem_norm_g=m_mem_norm_g, m_a_w_in=m_a_w_in, m_a_q_g=m_a_q_g, m_a_k_g=m_a_k_g, m_a_rel_bias=m_a_rel_bias, m_b_w_in=m_b_w_in, m_b_b_in=m_b_b_in, m_b_conv_w=m_b_conv_w, m_b_conv_b=m_b_conv_b, m_b_ln_g=m_b_ln_g, m_b_ln_b=m_b_ln_b, m_mq_g=m_mq_g, m_mk_g=m_mk_g, m_w_mem_kv=m_w_mem_kv, m_w_out=m_w_out, m_norm2_g=m_norm2_g, m_w_gate=m_w_gate, m_w_up=m_w_up, m_w_down=m_w_down, v_norm1_g=v_norm1_g, v_mem_norm_g=v_mem_norm_g, v_a_w_in=v_a_w_in, v_a_q_g=v_a_q_g, v_a_k_g=v_a_k_g, v_a_rel_bias=v_a_rel_bias, v_b_w_in=v_b_w_in, v_b_b_in=v_b_b_in, v_b_conv_w=v_b_conv_w, v_b_conv_b=v_b_conv_b, v_b_ln_g=v_b_ln_g, v_b_ln_b=v_b_ln_b, v_mq_g=v_mq_g, v_mk_g=v_mk_g, v_w_mem_kv=v_w_mem_kv, v_w_out=v_w_out, v_norm2_g=v_norm2_g, v_w_gate=v_w_gate, v_w_up=v_w_up, v_w_down=v_w_down)
    weights = {n: given[n] for n in TWIN_WEIGHTS}
    shared = {n: given[n] for n in SHARED_INPUTS}
    per_example = {n: given[n] for n in ['x', 'mem']}
    grad_fn = _jax.value_and_grad(_loss, argnums=(0, 1))

    def one_microbatch(ex, loss_target):
        ex = dict(ex)
        diff = ex.pop(TWIN_DIFF_INPUT)
        return grad_fn(weights, diff, {**shared, **ex}, loss_target)

    if N_MICROBATCH == 1:
        loss, (grad_w, grad_x) = one_microbatch(per_example, given["loss_target"])
    else:
        def body(carry, xs):
            loss_sum, grad_sum = carry
            l_k, (gw_k, gx_k) = one_microbatch(xs[0], xs[1])
            with _jax.named_scope("update"):
                return (loss_sum + l_k, _jax.tree.map(_jnp.add, grad_sum, gw_k)), gx_k

        init = (_jnp.zeros((), _jnp.float32), _jax.tree.map(_jnp.zeros_like, weights))
        (loss, grad_w), grad_x = _jax.lax.scan(body, init, (per_example, given["loss_target"]))
    with _jax.named_scope("update"):
        delta_w, new_m, new_v = {}, {}, {}
        for n in TWIN_WEIGHTS:
            delta_w[n], new_m[n], new_v[n] = _adamw(weights[n], grad_w[n], given["m_" + n], given["v_" + n])
    return (loss, grad_x, *[grad_w[n] for n in TWIN_WEIGHTS], *[delta_w[n] for n in TWIN_WEIGHTS],
            *[new_m[n] for n in TWIN_WEIGHTS], *[new_v[n] for n in TWIN_WEIGHTS])
```

```python
import functools

import jax
import jax.numpy as jnp
from jax import lax
from jax.experimental import pallas as pl
from jax.experimental.pallas import tpu as pltpu

F32 = jnp.float32
BF16 = jnp.bfloat16
BS = pl.BlockSpec
ANY = pl.BlockSpec(memory_space=pl.ANY)
MESH = pl.DeviceIdType.MESH

D = 1024
SEQ = 2048
NB = 2
T = NB * SEQ
MEMT = 256
HD = 64
TOK = 768
MEMW = 256
NA = 3 * TOK + MEMW
NBW = 2 * TOK + MEMW
FF = 2816
NSH = 4
FS = FF // NSH
CONVW = 31
EPS = 1e-6
NEG = -1e30
SCALE = HD ** -0.5
QB = 256
KWIN = 768
KPAD = 512
TR = 512

ADAM_LR = 0.001
ADAM_B1 = 0.9
ADAM_B2 = 0.999
ADAM_EPS = 1e-08
ADAM_WD = 0.01
ADAM_STEP = 10

_DIMS = {
    "nn": (((1,), (0,)), ((), ())),
    "nt": (((1,), (1,)), ((), ())),
    "tn": (((0,), (0,)), ((), ())),
}


def _cp(sem=None, vmem_mb=48):
    return pltpu.CompilerParams(dimension_semantics=sem, vmem_limit_bytes=vmem_mb << 20)


def _sds(shape, dtype):
    return jax.ShapeDtypeStruct(tuple(shape), dtype)


def _mm(mode, a, b, *, grid, a_spec, b_spec, out_shape, out_spec, acc_shape, name,
        extras=(), extra_specs=(), epilogue=None, into=None):
    n_ex = len(extras)
    nk = grid[2]
    dims = _DIMS[mode]

    def body(a_ref, b_ref, *rest):
        ex = rest[:n_ex]
        o_ref = rest[n_ex + (1 if into is not None else 0)]
        acc = rest[-1]
        k = pl.program_id(2)
        prod = lax.dot_general(a_ref[...].astype(BF16), b_ref[...].astype(BF16), dims,
                               preferred_element_type=F32)

        def finish(val):
            if epilogue is None:
                o_ref[...] = val.astype(o_ref.dtype)
            else:
                epilogue(val, ex, o_ref)

        if nk == 1:
            finish(prod)
        else:
            @pl.when(k == 0)
            def _():
                acc[...] = prod

            @pl.when(k > 0)
            def _():
                acc[...] += prod

            @pl.when(k == nk - 1)
            def _():
                finish(acc[...])

    ins = [a, b, *extras]
    in_specs = [a_spec, b_spec, *extra_specs]
    aliases = {}
    if into is not None:
        ins.append(into)
        in_specs.append(ANY)
        aliases = {len(ins) - 1: 0}
    return pl.pallas_call(
        body, grid=grid, in_specs=in_specs, out_specs=out_spec, out_shape=out_shape,
        scratch_shapes=[pltpu.VMEM(acc_shape if nk > 1 else (8, 128), F32)],
        input_output_aliases=aliases,
        compiler_params=_cp(("parallel", "parallel", "arbitrary")), name=name,
    )(*ins)


def _rms_fwd(x, g, name):
    rows = x.shape[0]

    def body(x_ref, g_ref, o_ref):
        xv = x_ref[...]
        r = lax.rsqrt(jnp.mean(xv * xv, axis=-1, keepdims=True) + EPS)
        o_ref[...] = (xv * r * g_ref[...]).astype(BF16)

    return pl.pallas_call(
        body, grid=(rows // TR,),
        in_specs=[BS((TR, D), lambda i: (i, 0)), BS((1, D), lambda i: (0, 0))],
        out_specs=BS((TR, D), lambda i: (i, 0)), out_shape=_sds((rows, D), BF16),
        compiler_params=_cp(("arbitrary",)), name=name,
    )(x, g)


def _rms_bwd(dh, x, g, dres, name):
    rows = x.shape[0]
    has_res = dres is not None

    def body(*refs):
        if has_res:
            dh_ref, x_ref, g_ref, r_ref, dx_ref, dg_ref = refs
        else:
            dh_ref, x_ref, g_ref, dx_ref, dg_ref = refs
        xv = x_ref[...]
        dhv = dh_ref[...]
        r = lax.rsqrt(jnp.mean(xv * xv, axis=-1, keepdims=True) + EPS)
        xh = xv * r
        gy = dhv * g_ref[...]
        dx = r * (gy - xh * jnp.mean(gy * xh, axis=-1, keepdims=True))
        if has_res:
            dx = dx + r_ref[...]
        dx_ref[...] = dx

        @pl.when(pl.program_id(0) == 0)
        def _():
            dg_ref[...] = jnp.zeros_like(dg_ref)

        dg_ref[...] += jnp.sum(dhv * xh, axis=0, keepdims=True)

    row = BS((TR, D), lambda i: (i, 0))
    vec = BS((1, D), lambda i: (0, 0))
    ins = [dh, x, g] + ([dres] if has_res else [])
    return pl.pallas_call(
        body, grid=(rows // TR,),
        in_specs=[row, row, vec] + ([row] if has_res else []),
        out_specs=[row, vec], out_shape=[_sds((rows, D), F32), _sds((1, D), F32)],
        compiler_params=_cp(("arbitrary",)), name=name,
    )(*ins)


def _group_masks(width):
    lane = lax.broadcasted_iota(jnp.int32, (1, width), 1)
    return [(lane >= HD * h) & (lane < HD * (h + 1)) for h in range(width // HD)]


def _group_mean(v, masks):
    out = jnp.zeros_like(v)
    for m in masks:
        s = jnp.sum(jnp.where(m, v, 0.0), axis=-1, keepdims=True) * (1.0 / HD)
        out = jnp.where(m, s, out)
    return out


def _head_norm(zv, g, masks):
    r = lax.rsqrt(_group_mean(zv * zv, masks) + EPS)
    return zv * r * g


def _head_norm_bwd(dy, zv, g, masks):
    r = lax.rsqrt(_group_mean(zv * zv, masks) + EPS)
    zh = zv * r
    gy = dy * g
    dz = r * (gy - zh * _group_mean(gy * zh, masks))
    return dz, jnp.sum(dy * zh, axis=0, keepdims=True)


def _fold_heads(v, width):
    vb = jnp.broadcast_to(v, (8, width))
    out = vb
    for h in range(1, width // HD):
        out = out + pltpu.roll(vb, width - HD * h, axis=1)
    return out[0:1]


def _bias_expand(u):
    def body(u_ref, o_ref):
        x = jnp.broadcast_to(u_ref[...], (QB, 1024))
        rolled = pltpu.roll(x, 1024 - (QB - 1), axis=1, stride=1, stride_axis=0)[:, :KWIN]
        row = lax.broadcasted_iota(jnp.int32, (QB, 1), 0)
        col = lax.broadcasted_iota(jnp.int32, (1, KWIN), 1)
        lo = (row // 64) * 64
        ok = (col >= lo) & (col < lo + 576)
        o_ref[...] = jnp.where(ok, rolled, NEG)

    return pl.pallas_call(
        body, grid=(12,), in_specs=[BS((None, 1, 1024), lambda h: (h, 0, 0))],
        out_specs=BS((None, QB, KWIN), lambda h: (h, 0, 0)), out_shape=_sds((12, QB, KWIN), F32),
        compiler_params=_cp(("arbitrary",)), name="bias_expand",
    )(u)


def _bias_reduce(ds):
    def body(d_ref, o_ref):
        ri = lax.broadcasted_iota(jnp.int32, (QB, QB), 0)
        ci = lax.broadcasted_iota(jnp.int32, (QB, QB), 1)
        flip = (ri + ci == QB - 1).astype(F32)
        drev = jnp.dot(flip, d_ref[...], precision=lax.Precision.HIGHEST, preferred_element_type=F32)
        x = jnp.concatenate([drev, jnp.zeros((QB, 1024 - KWIN), F32)], axis=1)
        rolled = pltpu.roll(x, 0, axis=1, stride=1, stride_axis=0)
        o_ref[...] = jnp.sum(rolled, axis=0, keepdims=True)

    return pl.pallas_call(
        body, grid=(12,), in_specs=[BS((None, QB, KWIN), lambda h: (h, 0, 0))],
        out_specs=BS((None, 1, 1024), lambda h: (h, 0, 0)), out_shape=_sds((12, 1, 1024), F32),
        compiler_params=_cp(("arbitrary",)), name="bias_reduce",
    )(ds)


def _attn_softmax(qh, kw, bias, startadd):
    s = lax.dot_general(qh, kw, _DIMS["nt"], preferred_element_type=F32) * SCALE + bias + startadd
    m = jnp.max(s, axis=-1, keepdims=True)
    p = jnp.exp(s - m)
    return p / jnp.sum(p, axis=-1, keepdims=True)


def _attn_fwd(z, bias, gq2, gk2):
    def body(q_ref, k_ref, v_ref, b_ref, gq_ref, gk_ref, o_ref, qn_s, kn_s, v_s):
        masks = _group_masks(128)
        qn_s[...] = _head_norm(q_ref[...], gq_ref[...], masks).astype(BF16)
        kn_s[0:KPAD, :] = jnp.zeros((KPAD, 128), BF16)
        kn_s[KPAD:, :] = _head_norm(k_ref[...], gk_ref[...], masks).astype(BF16)
        v_s[0:KPAD, :] = jnp.zeros((KPAD, 128), BF16)
        v_s[KPAD:, :] = v_ref[...].astype(BF16)
        col = lax.broadcasted_iota(jnp.int32, (1, KWIN), 1)

        def blk(i, carry):
            r0 = pl.multiple_of(i * QB, QB)
            qb = qn_s[pl.ds(r0, QB), :]
            kw = kn_s[pl.ds(r0, KWIN), :]
            vw = v_s[pl.ds(r0, KWIN), :]
            startadd = jnp.where(col + r0 < KPAD, NEG, 0.0)
            o = jnp.zeros((QB, 128), F32)
            for h in range(2):
                qh = jnp.where(masks[h], qb, jnp.zeros_like(qb))
                vh = jnp.where(masks[h], vw, jnp.zeros_like(vw))
                p = _attn_softmax(qh, kw, b_ref[h], startadd).astype(BF16)
                o = o + jnp.dot(p, vh, preferred_element_type=F32)
            o_ref[pl.ds(r0, QB), :] = o.astype(BF16)
            return carry

        lax.fori_loop(0, SEQ // QB, blk, 0)

    vec = BS((1, 128), lambda b, hp: (0, 0))
    return pl.pallas_call(
        body, grid=(NB, 6),
        in_specs=[BS((SEQ, 128), lambda b, hp: (b, hp)),
                  BS((SEQ, 128), lambda b, hp: (b, 6 + hp)),
                  BS((SEQ, 128), lambda b, hp: (b, 12 + hp)),
                  BS((2, QB, KWIN), lambda b, hp: (hp, 0, 0)), vec, vec],
        out_specs=BS((SEQ, 128), lambda b, hp: (b, hp)),
        out_shape=_sds((T, D), BF16),
        scratch_shapes=[pltpu.VMEM((SEQ, 128), BF16), pltpu.VMEM((SEQ + KPAD, 128), BF16),
                        pltpu.VMEM((SEQ + KPAD, 128), BF16)],
        compiler_params=_cp(("arbitrary", "arbitrary")), name="attn_fwd",
    )(z, z, z, bias, gq2, gk2)


def _attn_bwd(z, dcat, bias, gq2, gk2):
    def body(q_ref, k_ref, v_ref, do_ref, b_ref, gq_ref, gk_ref,
             dq_ref, dk_ref, dv_ref, db_ref, dgq_ref, dgk_ref,
             qn_s, kn_s, v_s, dqn_s, dkn_s, dv_s):
        hp = pl.program_id(0)
        b = pl.program_id(1)
        masks = _group_masks(128)
        qn_s[...] = _head_norm(q_ref[...], gq_ref[...], masks).astype(BF16)
        kn_s[0:KPAD, :] = jnp.zeros((KPAD, 128), BF16)
        kn_s[KPAD:, :] = _head_norm(k_ref[...], gk_ref[...], masks).astype(BF16)
        v_s[0:KPAD, :] = jnp.zeros((KPAD, 128), BF16)
        v_s[KPAD:, :] = v_ref[...].astype(BF16)
        dkn_s[...] = jnp.zeros_like(dkn_s)
        dv_s[...] = jnp.zeros_like(dv_s)

        @pl.when(b == 0)
        def _():
            db_ref[...] = jnp.zeros_like(db_ref)

        @pl.when((b == 0) & (hp == 0))
        def _():
            dgq_ref[...] = jnp.zeros_like(dgq_ref)
            dgk_ref[...] = jnp.zeros_like(dgk_ref)

        col = lax.broadcasted_iota(jnp.int32, (1, KWIN), 1)

        def blk(i, carry):
            r0 = pl.multiple_of(i * QB, QB)
            qb = qn_s[pl.ds(r0, QB), :]
            kw = kn_s[pl.ds(r0, KWIN), :]
            vw = v_s[pl.ds(r0, KWIN), :]
            dob = do_ref[pl.ds(r0, QB), :].astype(BF16)
            startadd = jnp.where(col + r0 < KPAD, NEG, 0.0)
            dqn = jnp.zeros((QB, 128), F32)
            dkw = jnp.zeros((KWIN, 128), F32)
            dvw = jnp.zeros((KWIN, 128), F32)
            for h in range(2):
                qh = jnp.where(masks[h], qb, jnp.zeros_like(qb))
                kh = jnp.where(masks[h], kw, jnp.zeros_like(kw))
                doh = jnp.where(masks[h], dob, jnp.zeros_like(dob))
                p = _attn_softmax(qh, kw, b_ref[h], startadd)
                dvw = dvw + lax.dot_general(p.astype(BF16), doh, _DIMS["tn"],
                                            preferred_element_type=F32)
                dp = lax.dot_general(doh, vw, _DIMS["nt"], preferred_element_type=F32)
                ds = p * (dp - jnp.sum(dp * p, axis=-1, keepdims=True))
                db_ref[h] += ds
                dsb = (ds * SCALE).astype(BF16)
                dqn = dqn + jnp.dot(dsb, kh, preferred_element_type=F32)
                dkw = dkw + lax.dot_general(dsb, qh, _DIMS["tn"], preferred_element_type=F32)
            dqn_s[pl.ds(r0, QB), :] = dqn
            dkn_s[pl.ds(r0, KWIN), :] += dkw
            dv_s[pl.ds(r0, KWIN), :] += dvw
            return carry

        lax.fori_loop(0, SEQ // QB, blk, 0)

        dq, dgq = _head_norm_bwd(dqn_s[...], q_ref[...], gq_ref[...], masks)
        dk, dgk = _head_norm_bwd(dkn_s[KPAD:, :], k_ref[...], gk_ref[...], masks)
        dq_ref[...] = dq.astype(BF16)
        dk_ref[...] = dk.astype(BF16)
        dv_ref[...] = dv_s[KPAD:, :].astype(BF16)
        dgq_ref[...] += _fold_heads(dgq, 128)
        dgk_ref[...] += _fold_heads(dgk, 128)

    vec = BS((1, 128), lambda hp, b: (0, 0))
    row = BS((SEQ, 128), lambda hp, b: (b, hp))
    return pl.pallas_call(
        body, grid=(6, NB),
        in_specs=[row,
                  BS((SEQ, 128), lambda hp, b: (b, 6 + hp)),
                  BS((SEQ, 128), lambda hp, b: (b, 12 + hp)),
                  row,
                  BS((2, QB, KWIN), lambda hp, b: (hp, 0, 0)), vec, vec],
        out_specs=[row, row, row, BS((2, QB, KWIN), lambda hp, b: (hp, 0, 0)), vec, vec],
        out_shape=[_sds((T, TOK), BF16), _sds((T, TOK), BF16), _sds((T, TOK), BF16),
                   _sds((12, QB, KWIN), F32), _sds((1, 128), F32), _sds((1, 128), F32)],
        scratch_shapes=[pltpu.VMEM((SEQ, 128), BF16), pltpu.VMEM((SEQ + KPAD, 128), BF16),
                        pltpu.VMEM((SEQ + KPAD, 128), BF16), pltpu.VMEM((SEQ, 128), F32),
                        pltpu.VMEM((SEQ + KPAD, 128), F32), pltpu.VMEM((SEQ + KPAD, 128), F32)],
        compiler_params=_cp(("arbitrary", "arbitrary")), name="attn_bwd",
    )(z, z, z, dcat, bias, gq2, gk2)


def _mem_softmax(qh, kn):
    s = lax.dot_general(qh, kn, _DIMS["nt"], preferred_element_type=F32) * SCALE
    m = jnp.max(s, axis=-1, keepdims=True)
    p = jnp.exp(s - m)
    return p / jnp.sum(p, axis=-1, keepdims=True)


def _memattn_fwd(z, kv, cat, gq4, gk4, qcol, name):
    def body(q_ref, k_ref, v_ref, gq_ref, gk_ref, cat_ref, o_ref):
        del cat_ref
        masks = _group_masks(MEMW)
        qn = _head_norm(q_ref[...], gq_ref[...], masks).astype(BF16)
        kn = _head_norm(k_ref[...], gk_ref[...], masks).astype(BF16)
        vv = v_ref[...].astype(BF16)
        o = jnp.zeros((TR, MEMW), F32)
        for h in range(4):
            qh = jnp.where(masks[h], qn, jnp.zeros_like(qn))
            vh = jnp.where(masks[h], vv, jnp.zeros_like(vv))
            p = _mem_softmax(qh, kn).astype(BF16)
            o = o + jnp.dot(p, vh, preferred_element_type=F32)
        o_ref[...] = o.astype(BF16)

    nt = SEQ // TR
    vec = BS((1, MEMW), lambda b, t: (0, 0))
    return pl.pallas_call(
        body, grid=(NB, nt),
        in_specs=[BS((TR, MEMW), lambda b, t: (b * nt + t, qcol)),
                  BS((MEMT, MEMW), lambda b, t: (b, 0)),
                  BS((MEMT, MEMW), lambda b, t: (b, 1)), vec, vec, ANY],
        out_specs=BS((TR, MEMW), lambda b, t: (b * nt + t, 3)),
        out_shape=_sds((T, D), BF16), input_output_aliases={5: 0},
        compiler_params=_cp(("arbitrary", "arbitrary")), name=name,
    )(z, kv, kv, gq4, gk4, cat)


def _memattn_bwd(z, kv, dcat, gq4, gk4, qcol, name):
    nt = SEQ // TR

    def body(q_ref, k_ref, v_ref, do_ref, gq_ref, gk_ref,
             dq_ref, dkv_ref, dgq_ref, dgk_ref, dkn_s, dv_s):
        b = pl.program_id(0)
        t = pl.program_id(1)
        masks = _group_masks(MEMW)
        qz = q_ref[...]
        kz = k_ref[...]
        qn = _head_norm(qz, gq_ref[...], masks).astype(BF16)
        kn = _head_norm(kz, gk_ref[...], masks).astype(BF16)
        vv = v_ref[...].astype(BF16)
        dob = do_ref[...].astype(BF16)

        @pl.when(t == 0)
        def _():
            dkn_s[...] = jnp.zeros_like(dkn_s)
            dv_s[...] = jnp.zeros_like(dv_s)

        @pl.when((t == 0) & (b == 0))
        def _():
            dgq_ref[...] = jnp.zeros_like(dgq_ref)
            dgk_ref[...] = jnp.zeros_like(dgk_ref)

        dqn = jnp.zeros((TR, MEMW), F32)
        dkn = jnp.zeros((MEMT, MEMW), F32)
        dvv = jnp.zeros((MEMT, MEMW), F32)
        for h in range(4):
            qh = jnp.where(masks[h], qn, jnp.zeros_like(qn))
            kh = jnp.where(masks[h], kn, jnp.zeros_like(kn))
            doh = jnp.where(masks[h], dob, jnp.zeros_like(dob))
            p = _mem_softmax(qh, kn)
            dvv = dvv + lax.dot_general(p.astype(BF16), doh, _DIMS["tn"], preferred_element_type=F32)
            dp = lax.dot_general(doh, vv, _DIMS["nt"], preferred_element_type=F32)
            ds = p * (dp - jnp.sum(dp * p, axis=-1, keepdims=True))
            dsb = (ds * SCALE).astype(BF16)
            dqn = dqn + jnp.dot(dsb, kh, preferred_element_type=F32)
            dkn = dkn + lax.dot_general(dsb, qh, _DIMS["tn"], preferred_element_type=F32)
        dkn_s[...] += dkn
        dv_s[...] += dvv
        dq, dgq = _head_norm_bwd(dqn, qz, gq_ref[...], masks)
        dq_ref[...] = dq.astype(BF16)
        dgq_ref[...] += _fold_heads(dgq, MEMW)

        @pl.when(t == nt - 1)
        def _():
            dk, dgk = _head_norm_bwd(dkn_s[...], kz, gk_ref[...], masks)
            dkv_ref[:, 0:MEMW] = dk
            dkv_ref[:, MEMW:] = dv_s[...]
            dgk_ref[...] += _fold_heads(dgk, MEMW)

    vec = BS((1, MEMW), lambda b, t: (0, 0))
    return pl.pallas_call(
        body, grid=(NB, nt),
        in_specs=[BS((TR, MEMW), lambda b, t: (b * nt + t, qcol)),
                  BS((MEMT, MEMW), lambda b, t: (b, 0)),
                  BS((MEMT, MEMW), lambda b, t: (b, 1)),
                  BS((TR, MEMW), lambda b, t: (b * nt + t, 3)), vec, vec],
        out_specs=[BS((TR, MEMW), lambda b, t: (b * nt + t, 0)),
                   BS((MEMT, 2 * MEMW), lambda b, t: (b, 0)), vec, vec],
        out_shape=[_sds((T, MEMW), BF16), _sds((NB * MEMT, 2 * MEMW), F32),
                   _sds((1, MEMW), F32), _sds((1, MEMW), F32)],
        scratch_shapes=[pltpu.VMEM((MEMT, MEMW), F32), pltpu.VMEM((MEMT, MEMW), F32)],
        compiler_params=_cp(("arbitrary", "arbitrary")), name=name,
    )(z, kv, kv, dcat, gq4, gk4)


HALO = 32
NEXT = 64
RT = 64


def _glu(zz):
    return zz[:, :TOK] * jax.nn.sigmoid(zz[:, TOK:])


def _layer_norm_parts(y):
    mu = jnp.mean(y, axis=-1, keepdims=True)
    yc = y - mu
    rstd = lax.rsqrt(jnp.mean(yc * yc, axis=-1, keepdims=True) + EPS)
    return yc * rstd, rstd


def _conv_rows(w_ref, hbuf, r0, rows):
    y = jnp.zeros((rows, TOK), F32)
    for j in range(CONVW):
        y = y + w_ref[j:j + 1, :] * hbuf[r0 + (HALO - CONVW + 1) + j:r0 + (HALO - CONVW + 1) + j + rows, :]
    return y


def _conv_fwd(z, cw, cb, lg, lb):
    nt = SEQ // TR

    def body(zc_ref, zp_ref, w_ref, cb_ref, lg_ref, lb_ref, o_ref, hbuf):
        t = pl.program_id(1)
        hbuf[0:HALO, :] = jnp.where(t == 0, 0.0, _glu(zp_ref[...]))
        hbuf[HALO:, :] = _glu(zc_ref[...])
        for r0 in range(0, TR, RT):
            y = _conv_rows(w_ref, hbuf, r0, RT) + cb_ref[...]
            yh, _ = _layer_norm_parts(y)
            o = yh * lg_ref[...] + lb_ref[...]
            o_ref[r0:r0 + RT, :] = (o * jax.nn.sigmoid(o)).astype(BF16)

    vec = BS((1, TOK), lambda b, t: (0, 0))
    per = TR // HALO
    return pl.pallas_call(
        body, grid=(NB, nt),
        in_specs=[BS((TR, 2 * TOK), lambda b, t: (b * nt + t, 0)),
                  BS((HALO, 2 * TOK), lambda b, t: (jnp.maximum((b * nt + t) * per - 1, 0), 0)),
                  BS((32, TOK), lambda b, t: (0, 0)), vec, vec, vec],
        out_specs=BS((TR, TOK), lambda b, t: (b * nt + t, 0)),
        out_shape=_sds((T, D), BF16),
        scratch_shapes=[pltpu.VMEM((HALO + TR, TOK), F32)],
        compiler_params=_cp(("arbitrary", "arbitrary")), name="conv_fwd",
    )(z, z, cw, cb, lg, lb)


def _conv_bwd(z, dcat, cw, cb, lg, lb):
    nt = SEQ // TR
    ext = TR + NEXT

    def body(zc_ref, zp_ref, zn_ref, dc_ref, dn_ref, w_ref, cb_ref, lg_ref, lb_ref,
             du_ref, dw_ref, dcb_ref, dlg_ref, dlb_ref, dbin_ref, hbuf, dybuf):
        b = pl.program_id(0)
        t = pl.program_id(1)

        @pl.when((b == 0) & (t == 0))
        def _():
            dw_ref[...] = jnp.zeros_like(dw_ref)
            dcb_ref[...] = jnp.zeros_like(dcb_ref)
            dlg_ref[...] = jnp.zeros_like(dlg_ref)
            dlb_ref[...] = jnp.zeros_like(dlb_ref)
            dbin_ref[...] = jnp.zeros_like(dbin_ref)

        hbuf[0:HALO, :] = jnp.where(t == 0, 0.0, _glu(zp_ref[...]))
        hbuf[HALO:HALO + TR, :] = _glu(zc_ref[...])
        hbuf[HALO + TR:, :] = _glu(zn_ref[...])
        last = t == nt - 1
        for r0 in range(0, ext, RT):
            y = _conv_rows(w_ref, hbuf, r0, RT) + cb_ref[...]
            yh, rstd = _layer_norm_parts(y)
            o = yh * lg_ref[...] + lb_ref[...]
            sg = jax.nn.sigmoid(o)
            if r0 < TR:
                dtok = dc_ref[r0:r0 + RT, :]
            else:
                dtok = jnp.where(last, 0.0, dn_ref[...])
            do = dtok * (sg * (1.0 + o * (1.0 - sg)))
            dyh = do * lg_ref[...]
            dy = rstd * (dyh - jnp.mean(dyh, axis=-1, keepdims=True)
                         - yh * jnp.mean(dyh * yh, axis=-1, keepdims=True))
            dybuf[r0:r0 + RT, :] = dy
            if r0 < TR:
                dlg_ref[...] += jnp.sum(do * yh, axis=0, keepdims=True)
                dlb_ref[...] += jnp.sum(do, axis=0, keepdims=True)
                dcb_ref[...] += jnp.sum(dy, axis=0, keepdims=True)
        for r0 in range(0, TR, RT):
            dh = jnp.zeros((RT, TOK), F32)
            for j in range(CONVW):
                dh = dh + w_ref[j:j + 1, :] * dybuf[r0 + (CONVW - 1) - j:r0 + (CONVW - 1) - j + RT, :]
            a = zc_ref[r0:r0 + RT, 0:TOK]
            sg = jax.nn.sigmoid(zc_ref[r0:r0 + RT, TOK:])
            da = dh * sg
            dg = dh * a * (sg * (1.0 - sg))
            du_ref[r0:r0 + RT, 0:TOK] = da.astype(BF16)
            du_ref[r0:r0 + RT, TOK:] = dg.astype(BF16)
            dbin_ref[:, 0:TOK] += jnp.sum(da, axis=0, keepdims=True)
            dbin_ref[:, TOK:] += jnp.sum(dg, axis=0, keepdims=True)
        for j in range(CONVW):
            acc = jnp.zeros((8, TOK), F32)
            for r0 in range(0, TR, RT):
                prod = dybuf[r0:r0 + RT, :] * hbuf[r0 + (HALO - CONVW + 1) + j:r0 + (HALO - CONVW + 1) + j + RT, :]
                acc = acc + jnp.sum(prod.reshape(RT // 8, 8, TOK), axis=0)
            dw_ref[j:j + 1, :] += jnp.sum(acc, axis=0, keepdims=True)

    vec = BS((1, TOK), lambda b, t: (0, 0))
    perh = TR // HALO
    pern = TR // NEXT
    nlast_n = T // NEXT - 1
    return pl.pallas_call(
        body, grid=(NB, nt),
        in_specs=[BS((TR, 2 * TOK), lambda b, t: (b * nt + t, 0)),
                  BS((HALO, 2 * TOK), lambda b, t: (jnp.maximum((b * nt + t) * perh - 1, 0), 0)),
                  BS((NEXT, 2 * TOK), lambda b, t: (jnp.minimum((b * nt + t + 1) * pern, nlast_n), 0)),
                  BS((TR, TOK), lambda b, t: (b * nt + t, 0)),
                  BS((NEXT, TOK), lambda b, t: (jnp.minimum((b * nt + t + 1) * pern, nlast_n), 0)),
                  BS((32, TOK), lambda b, t: (0, 0)), vec, vec, vec],
        out_specs=[BS((TR, 2 * TOK), lambda b, t: (b * nt + t, 0)),
                   BS((32, TOK), lambda b, t: (0, 0)), vec, vec, vec,
                   BS((1, 2 * TOK), lambda b, t: (0, 0))],
        out_shape=[_sds((T, 2 * TOK), BF16), _sds((32, TOK), F32), _sds((1, TOK), F32),
                   _sds((1, TOK), F32), _sds((1, TOK), F32), _sds((1, 2 * TOK), F32)],
        scratch_shapes=[pltpu.VMEM((HALO + TR + NEXT, TOK), F32), pltpu.VMEM((ext, TOK), F32)],
        compiler_params=_cp(("arbitrary", "arbitrary")), name="conv_bwd",
    )(z, z, z, dcat, dcat, cw, cb, lg, lb)


def _ffn_up(h2, wgu, l):
    def body(h_ref, wg_ref, wu_ref, g_ref, u_ref, a_ref):
        hv = h_ref[...]
        g = jnp.dot(hv, wg_ref[...], preferred_element_type=F32)
        u = jnp.dot(hv, wu_ref[...], preferred_element_type=F32)
        g_ref[...] = g
        u_ref[...] = u
        a_ref[...] = (g * jax.nn.sigmoid(g) * u).astype(BF16)

    out = BS((None, TR, FS), lambda i, s: (s, i, 0))
    return pl.pallas_call(
        body, grid=(T // TR, NSH),
        in_specs=[BS((TR, D), lambda i, s: (i, 0)),
                  BS((None, None, D, FS), lambda i, s: (l, s, 0, 0)),
                  BS((None, None, D, FS), lambda i, s: (l, NSH + s, 0, 0))],
        out_specs=[out, out, out],
        out_shape=[_sds((NSH, T, FS), F32), _sds((NSH, T, FS), F32), _sds((NSH, T, FS), BF16)],
        compiler_params=_cp(("parallel", "arbitrary")), name=f"ffn_up_{l}",
    )(h2, wgu, wgu)


def _ffn_down_bwd(dx, wd, g, u, l):
    def epilogue(dact, ex, o_ref):
        gv = ex[0][...]
        uv = ex[1][...]
        sg = jax.nn.sigmoid(gv)
        o_ref[0] = (dact * uv * (sg * (1.0 + gv * (1.0 - sg)))).astype(BF16)
        o_ref[1] = (dact * (gv * sg)).astype(BF16)

    ex_spec = BS((None, TR, FS), lambda i, s, k: (s, i, 0))
    return _mm("nt", dx, wd, grid=(T // TR, NSH, 1),
               a_spec=BS((TR, D), lambda i, s, k: (i, 0)),
               b_spec=BS((None, None, FS, D), lambda i, s, k: (l, s, 0, 0)),
               out_shape=_sds((2, NSH, T, FS), BF16),
               out_spec=BS((2, None, TR, FS), lambda i, s, k: (0, s, i, 0)),
               acc_shape=(TR, FS), extras=(g, u), extra_specs=(ex_spec, ex_spec),
               epilogue=epilogue, name=f"ffn_down_bwd_{l}")


def _loss_head(y, target):
    def body(y_ref, t_ref, l_ref, dy_ref, acc):
        i = pl.program_id(0)
        e = y_ref[...] - t_ref[...]
        dy_ref[...] = e * (1.0 / D)

        @pl.when(i == 0)
        def _():
            acc[...] = jnp.zeros_like(acc)

        acc[...] += jnp.sum(jnp.mean(e * e, axis=-1, keepdims=True), axis=0, keepdims=True)

        @pl.when(i == T // TR - 1)
        def _():
            l_ref[...] = 0.5 * acc[...]

    row = BS((TR, D), lambda i: (i, 0))
    return pl.pallas_call(
        body, grid=(T // TR,), in_specs=[row, row],
        out_specs=[BS((1, 1), lambda i: (0, 0)), row],
        out_shape=[_sds((1, 1), F32), _sds((T, D), F32)],
        scratch_shapes=[pltpu.VMEM((1, 1), F32)],
        compiler_params=_cp(("arbitrary",)), name="loss_head",
    )(y, target)


def _row_tile(rows, cols, itemsize=4, limit=2 << 20):
    tr = rows
    while tr * cols * itemsize > limit and tr % 2 == 0 and (tr // 2) % 16 == 0:
        tr //= 2
    return tr


def _cast_bf16(arrs, name):
    n = len(arrs)
    rows, cols = arrs[0].shape
    tr = _row_tile(rows, cols)

    def body(*refs):
        o_ref = refs[n]
        k = pl.program_id(0)
        val = refs[0][...]
        for j in range(1, n):
            val = jnp.where(k == j, refs[j][...], val)
        o_ref[...] = val.astype(BF16)

    return pl.pallas_call(
        body, grid=(n, rows // tr),
        in_specs=[BS((tr, cols), lambda k, i: (i, 0))] * n,
        out_specs=BS((None, tr, cols), lambda k, i: (k, i, 0)),
        out_shape=_sds((n, rows, cols), BF16),
        compiler_params=_cp(("arbitrary", "arbitrary")), name=name,
    )(*arrs)


def _pair_sum(mine, got, name):
    n, rows, cols = mine.shape
    tr = _row_tile(rows, cols)

    def body(a_ref, b_ref, s_ref, sb_ref):
        s = a_ref[...] + b_ref[...]
        s_ref[...] = s
        sb_ref[...] = s.astype(BF16)

    spec = BS((None, tr, cols), lambda k, i: (k, i, 0))
    return pl.pallas_call(
        body, grid=(n, rows // tr), in_specs=[spec, spec], out_specs=[spec, spec],
        out_shape=[_sds((n, rows, cols), F32), _sds((n, rows, cols), BF16)],
        compiler_params=_cp(("arbitrary", "arbitrary")), name=name,
    )(mine, got)


def _quad_sum(own, got, name):
    n, rows, cols = own.shape
    tr = _row_tile(rows, cols)

    def body(a_ref, q_ref, o_ref):
        o_ref[...] = ((a_ref[...] + q_ref[0].astype(F32)) + q_ref[1].astype(F32)) + q_ref[2].astype(F32)

    spec = BS((None, tr, cols), lambda k, i: (k, i, 0))
    return pl.pallas_call(
        body, grid=(n, rows // tr),
        in_specs=[spec, BS((3, None, tr, cols), lambda k, i: (0, k, i, 0))], out_specs=spec,
        out_shape=_sds((n, rows, cols), F32),
        compiler_params=_cp(("arbitrary", "arbitrary")), name=name,
    )(own, got)


def _adam_math(w, g, m, v):
    m = ADAM_B1 * m + (1.0 - ADAM_B1) * g
    v = ADAM_B2 * v + (1.0 - ADAM_B2) * (g * g)
    m_hat = m / (1.0 - ADAM_B1 ** ADAM_STEP)
    v_hat = v / (1.0 - ADAM_B2 ** ADAM_STEP)
    delta = -ADAM_LR * (m_hat / (jnp.sqrt(v_hat) + ADAM_EPS) + ADAM_WD * w)
    return delta, m, v


def _adamw_big(w, g, m, v, name):
    shape = w.shape
    cols = shape[-1]
    rows = w.size // cols
    tr = _row_tile(rows, cols, limit=1 << 20)

    def body(w_ref, g_ref, m_ref, v_ref, d_ref, nm_ref, nv_ref):
        d, nm, nv = _adam_math(w_ref[...], g_ref[...], m_ref[...], v_ref[...])
        d_ref[...] = d
        nm_ref[...] = nm
        nv_ref[...] = nv

    spec = BS((tr, cols), lambda i: (i, 0))
    outs = pl.pallas_call(
        body, grid=(rows // tr,), in_specs=[spec] * 4, out_specs=[spec] * 3,
        out_shape=[_sds((rows, cols), F32)] * 3,
        compiler_params=_cp(("arbitrary",)), name=name,
    )(*[a.reshape(rows, cols) for a in (w, g, m, v)])
    return [o.reshape(shape) for o in outs]


def _adamw_small(ws, gs, ms, vs):
    n = len(ws)

    def body(*refs):
        for i in range(n):
            d, nm, nv = _adam_math(refs[i][...], refs[n + i][...], refs[2 * n + i][...],
                                   refs[3 * n + i][...])
            refs[4 * n + i][...] = d
            refs[5 * n + i][...] = nm
            refs[6 * n + i][...] = nv

    vm = BS(memory_space=pltpu.VMEM)
    outs = pl.pallas_call(
        body, in_specs=[vm] * (4 * n), out_specs=[vm] * (3 * n),
        out_shape=[_sds(w.shape, F32) for w in ws] * 3,
        compiler_params=_cp(), name="adamw_small",
    )(*ws, *gs, *ms, *vs)
    return outs[:n], outs[n:2 * n], outs[2 * n:]


def _place():
    x, y, c = lax.axis_index("x"), lax.axis_index("y"), lax.axis_index("c")
    chips = [(1 - x, y), (x, 1 - y), (1 - x, 1 - y)]
    return x, y, c, chips


def _gather_weights(srcs, dst_shapes, views, small):
    nu = len(srcs)
    nd = len(dst_shapes)

    def body(*refs):
        src = refs[:nu]
        small_ref = refs[nu]
        dst = refs[nu + 1:nu + 1 + nd]
        small_dst = refs[nu + 1 + nd]
        send, recv, fsend, frecv, lsem, ssend, srecv = refs[nu + 2 + nd:]
        x, y, c, chips = _place()
        s = 2 * x + y
        vw = [views[u](dst) for u in range(nu)]

        def ici(u, j, shard, to):
            return pltpu.make_async_remote_copy(
                src_ref=src[u].at[:, c], dst_ref=vw[u].at[:, shard, c],
                send_sem=send.at[3 * u + j], recv_sem=recv.at[3 * u + j],
                device_id=to, device_id_type=MESH)

        def fwd(u, j, shard, half):
            return pltpu.make_async_remote_copy(
                src_ref=vw[u].at[:, shard, half], dst_ref=vw[u].at[:, shard, half],
                send_sem=fsend.at[3 * u + j], recv_sem=frecv.at[3 * u + j],
                device_id=(x, y, 1 - c), device_id_type=MESH)

        def small_copy(j, shard, to):
            return pltpu.make_async_remote_copy(
                src_ref=small_ref, dst_ref=small_dst.at[shard],
                send_sem=ssend.at[j], recv_sem=srecv.at[j], device_id=to, device_id_type=MESH)

        local = [pltpu.make_async_copy(src[u], vw[u].at[:, s], lsem.at[u]) for u in range(nu)]
        local.append(pltpu.make_async_copy(small_ref, small_dst.at[s], lsem.at[nu]))
        for cp in local:
            cp.start()
        for j, chip in enumerate(chips):
            small_copy(j, s, (*chip, c)).start()
        for u in range(nu):
            for j, chip in enumerate(chips):
                ici(u, j, s, (*chip, c)).start()
        for u in range(nu):
            for j, chip in enumerate(chips):
                sj = 2 * chip[0] + chip[1]
                ici(u, j, sj, (x, y, c)).wait_recv()
                fwd(u, j, sj, c).start()
        for u in range(nu):
            for j, chip in enumerate(chips):
                sj = 2 * chip[0] + chip[1]
                fwd(u, j, sj, 1 - c).wait_recv()
        for j, chip in enumerate(chips):
            sj = 2 * chip[0] + chip[1]
            small_copy(j, sj, (x, y, c)).wait_recv()
        for u in range(nu):
            for j, chip in enumerate(chips):
                ici(u, j, s, (*chip, c)).wait_send()
                fwd(u, j, s, c).wait_send()
        for j, chip in enumerate(chips):
            small_copy(j, s, (*chip, c)).wait_send()
        for cp in local:
            cp.wait()

    dma = pltpu.SemaphoreType.DMA
    outs = pl.pallas_call(
        body, in_specs=[ANY] * (nu + 1), out_specs=[ANY] * (nd + 1),
        out_shape=[_sds(sh, BF16) for sh in dst_shapes] + [_sds((NSH,) + small.shape, F32)],
        scratch_shapes=[dma((3 * nu,)), dma((3 * nu,)), dma((3 * nu,)), dma((3 * nu,)),
                        dma((nu + 1,)), dma((3,)), dma((3,))],
        compiler_params=pltpu.CompilerParams(has_side_effects=True), name="gather_weights",
    )(*srcs, small)
    return outs[:nd], outs[nd]


def _pair_exchange(grads):
    nu = len(grads)

    def body(*refs):
        g = refs[:nu]
        mine = refs[nu:2 * nu]
        got = refs[2 * nu:3 * nu]
        send, recv, lsem = refs[3 * nu:]
        x, y, c, _ = _place()

        def push(u):
            return pltpu.make_async_remote_copy(
                src_ref=g[u].at[:, :, 1 - c], dst_ref=got[u], send_sem=send.at[u],
                recv_sem=recv.at[u], device_id=(x, y, 1 - c), device_id_type=MESH)

        local = [pltpu.make_async_copy(g[u].at[:, :, c], mine[u], lsem.at[u]) for u in range(nu)]
        for u in range(nu):
            push(u).start()
            local[u].start()
        for u in range(nu):
            push(u).wait()
            local[u].wait()

    dma = pltpu.SemaphoreType.DMA
    half_shapes = [(g.shape[0], NSH, g.shape[3], g.shape[4]) for g in grads]
    outs = pl.pallas_call(
        body, in_specs=[ANY] * nu, out_specs=[ANY] * (2 * nu),
        out_shape=[_sds(sh, F32) for sh in half_shapes] * 2,
        scratch_shapes=[dma((nu,)), dma((nu,)), dma((nu,))],
        compiler_params=pltpu.CompilerParams(has_side_effects=True), name="pair_exchange",
    )(*grads)
    return outs[:nu], outs[nu:]


def _chip_exchange(sums, sums_bf16):
    nu = len(sums)

    def body(*refs):
        sf = refs[:nu]
        sb = refs[nu:2 * nu]
        own = refs[2 * nu:3 * nu]
        got = refs[3 * nu:4 * nu]
        send, recv, lsem = refs[4 * nu:]
        x, y, c, chips = _place()
        s = 2 * x + y

        def push(u, j, shard, to):
            return pltpu.make_async_remote_copy(
                src_ref=sb[u].at[:, shard], dst_ref=got[u].at[j], send_sem=send.at[3 * u + j],
                recv_sem=recv.at[3 * u + j], device_id=to, device_id_type=MESH)

        local = [pltpu.make_async_copy(sf[u].at[:, s], own[u], lsem.at[u]) for u in range(nu)]
        for u in range(nu):
            local[u].start()
            for j, chip in enumerate(chips):
                push(u, j, 2 * chip[0] + chip[1], (*chip, c)).start()
        for u in range(nu):
            for j, chip in enumerate(chips):
                push(u, j, 2 * chip[0] + chip[1], (*chip, c)).wait()
            local[u].wait()

    dma = pltpu.SemaphoreType.DMA
    own_shapes = [(a.shape[0], a.shape[2], a.shape[3]) for a in sums]
    outs = pl.pallas_call(
        body, in_specs=[ANY] * (2 * nu), out_specs=[ANY] * (2 * nu),
        out_shape=[_sds(sh, F32) for sh in own_shapes] + [_sds((3,) + sh, BF16) for sh in own_shapes],
        scratch_shapes=[dma((3 * nu,)), dma((3 * nu,)), dma((nu,))],
        compiler_params=pltpu.CompilerParams(has_side_effects=True), name="chip_exchange",
    )(*sums, *sums_bf16)
    return outs[:nu], outs[nu:]


def _final_exchange(halves, out_shapes, targets):
    nu = len(halves)
    no = len(out_shapes)
    ncp = sum(len(t) for t in targets)

    def body(*refs):
        hv = refs[:nu]
        out = refs[nu:nu + no]
        send, recv, lsem = refs[nu + no:]
        x, y, c, _ = _place()
        copies = []
        k = 0
        for u in range(nu):
            rh = hv[u].shape[1]
            for (p, oi, li) in targets[u]:
                win = out[oi].at[li, pl.ds(c * rh, rh), :]
                copies.append(pltpu.make_async_remote_copy(
                    src_ref=hv[u].at[p], dst_ref=win, send_sem=send.at[k], recv_sem=recv.at[k],
                    device_id=(x, y, 1 - c), device_id_type=MESH))
                copies.append(pltpu.make_async_copy(hv[u].at[p], win, lsem.at[k]))
                k += 1
        for cp in copies:
            cp.start()
        k = 0
        for u in range(nu):
            rh = hv[u].shape[1]
            for (p, oi, li) in targets[u]:
                other = out[oi].at[li, pl.ds((1 - c) * rh, rh), :]
                pltpu.make_async_remote_copy(
                    src_ref=hv[u].at[p], dst_ref=other, send_sem=send.at[k], recv_sem=recv.at[k],
                    device_id=(x, y, 1 - c), device_id_type=MESH).wait()
                k += 1
        for cp in copies[1::2]:
            cp.wait()

    dma = pltpu.SemaphoreType.DMA
    return pl.pallas_call(
        body, in_specs=[ANY] * nu, out_specs=[ANY] * no,
        out_shape=[_sds(sh, F32) for sh in out_shapes],
        scratch_shapes=[dma((ncp,)), dma((ncp,)), dma((ncp,))],
        compiler_params=pltpu.CompilerParams(has_side_effects=True), name="final_exchange",
    )(*halves)


def _small_allreduce(pack):
    rows = pack.shape[0]

    def body(p_ref, o_ref, buf, send, recv):
        x, y, c, _ = _place()
        me = 4 * x + 2 * y + c
        buf[me] = p_ref[...]
        k = 0
        copies = []
        for dx in range(2):
            for dy in range(2):
                for dc in range(2):
                    if dx == 0 and dy == 0 and dc == 0:
                        continue
                    to = (jnp.where(dx, 1 - x, x), jnp.where(dy, 1 - y, y), jnp.where(dc, 1 - c, c))
                    src_slot = 4 * to[0] + 2 * to[1] + to[2]
                    copies.append((pltpu.make_async_remote_copy(
                        src_ref=p_ref, dst_ref=buf.at[me], send_sem=send.at[k], recv_sem=recv.at[k],
                        device_id=to, device_id_type=MESH), src_slot, k))
                    k += 1
        for cp, _, _ in copies:
            cp.start()
        for cp, src_slot, k in copies:
            pltpu.make_async_remote_copy(
                src_ref=p_ref, dst_ref=buf.at[src_slot], send_sem=send.at[k], recv_sem=recv.at[k],
                device_id=(x, y, c), device_id_type=MESH).wait()
        acc = buf[0]
        for d in range(1, 8):
            acc = acc + buf[d]
        o_ref[...] = acc

    dma = pltpu.SemaphoreType.DMA
    vm = BS(memory_space=pltpu.VMEM)
    return pl.pallas_call(
        body, in_specs=[vm], out_specs=vm, out_shape=_sds((rows, D), F32),
        scratch_shapes=[pltpu.VMEM((8, rows, D), F32), dma((7,)), dma((7,))],
        compiler_params=pltpu.CompilerParams(has_side_effects=True, vmem_limit_bytes=32 << 20),
        name="small_allreduce",
    )(pack)


def _in_proj(h, w, bias, name):
    n = w.shape[1]
    tn = 640 if n == NA else 896
    ep = None
    extras, especs = (), ()
    if bias is not None:
        def ep(acc, ex, o_ref):
            o_ref[...] = acc + ex[0][...]
        extras = (bias,)
        especs = (BS((1, tn), lambda i, j, k: (0, j)),)
    return _mm("nn", h, w, grid=(T // TR, n // tn, 1),
               a_spec=BS((TR, D), lambda i, j, k: (i, 0)), b_spec=BS((D, tn), lambda i, j, k: (0, j)),
               out_shape=_sds((T, n), F32), out_spec=BS((TR, tn), lambda i, j, k: (i, j)),
               acc_shape=(TR, tn), extras=extras, extra_specs=especs, epilogue=ep, name=name)


def _add_res(acc, ex, o_ref):
    o_ref[...] = acc + ex[0][...]


def _local_step(x, mem, target, w, p):
    row = lambda i, j, k: (i, 0)
    saved = []
    bias = _bias_expand(p["rel_u"])
    for l in range(2):
        type_a = l == 0
        h = _rms_fwd(x, p["norm1_g"][l:l + 1], f"rms1_{l}")
        memn = _rms_fwd(mem, p["mem_norm_g"][l:l + 1], f"rmsmem_{l}")
        kv = _mm("nn", memn, w["kv"], grid=(1, 1, 1),
                 a_spec=BS((NB * MEMT, D), lambda i, j, k: (0, 0)),
                 b_spec=BS((None, D, 2 * MEMW), lambda i, j, k: (l, 0, 0)),
                 out_shape=_sds((NB * MEMT, 2 * MEMW), F32),
                 out_spec=BS((NB * MEMT, 2 * MEMW), lambda i, j, k: (0, 0)),
                 acc_shape=(8, 128), name=f"memkv_{l}")
        if type_a:
            z = _in_proj(h, w["a"], None, "inproj_a")
            cat = _attn_fwd(z, bias, p["a_q_g2"], p["a_k_g2"])
            qcol = NA // MEMW - 1
        else:
            z = _in_proj(h, w["b"], p["b_b_in"], "inproj_b")
            cat = _conv_fwd(z, p["conv_w"], p["conv_b"], p["ln_g"], p["ln_b"])
            qcol = NBW // MEMW - 1
        cat = _memattn_fwd(z, kv, cat, p["mq_g4"][l:l + 1], p["mk_g4"][l:l + 1], qcol, f"memattn_fwd_{l}")
        x1 = _mm("nn", cat, w["wo"], grid=(T // TR, 1, 1), a_spec=BS((TR, D), row),
                 b_spec=BS((None, D, D), lambda i, j, k: (l, 0, 0)),
                 out_shape=_sds((T, D), F32), out_spec=BS((TR, D), row), acc_shape=(8, 128),
                 extras=(x,), extra_specs=(BS((TR, D), row),), epilogue=_add_res, name=f"outproj_{l}")
        h2 = _rms_fwd(x1, p["norm2_g"][l:l + 1], f"rms2_{l}")
        g, u, act = _ffn_up(h2, w["gu"], l)
        x2 = _mm("nn", act, w["wd"], grid=(T // TR, 1, NSH),
                 a_spec=BS((None, TR, FS), lambda i, j, k: (k, i, 0)),
                 b_spec=BS((None, None, FS, D), lambda i, j, k: (l, k, 0, 0)),
                 out_shape=_sds((T, D), F32), out_spec=BS((TR, D), row), acc_shape=(TR, D),
                 extras=(x1,), extra_specs=(BS((TR, D), row),), epilogue=_add_res, name=f"ffn_down_{l}")
        saved.append(dict(x=x, h=h, memn=memn, kv=kv, z=z, cat=cat, x1=x1, h2=h2, g=g, u=u, act=act,
                          qcol=qcol))
        x = x2

    loss, dx = _loss_head(x, target)

    big = dict(a=None, b=None, kv=None, wo=None, gu=None, wd=None)
    small = {}
    tk = TR
    nkt = T // tk
    for l in (1, 0):
        sv = saved[l]
        dgu = _ffn_down_bwd(dx, w["wd"], sv["g"], sv["u"], l)
        dgu8 = dgu.reshape(2 * NSH, T, FS)
        big["wd"] = _mm("tn", sv["act"], dx, grid=(NSH, 1, nkt),
                        a_spec=BS((None, tk, FS), lambda i, j, k: (i, k, 0)),
                        b_spec=BS((tk, D), lambda i, j, k: (k, 0)),
                        out_shape=_sds((2, NSH, FS, D), F32),
                        out_spec=BS((None, None, FS, D), lambda i, j, k: (l, i, 0, 0)),
                        acc_shape=(FS, D), into=big["wd"], name=f"dw_down_{l}")
        dh2 = _mm("nt", dgu8, w["gu"], grid=(T // TR, 1, 2 * NSH),
                  a_spec=BS((None, TR, FS), lambda i, j, k: (k, i, 0)),
                  b_spec=BS((None, None, D, FS), lambda i, j, k: (l, k, 0, 0)),
                  out_shape=_sds((T, D), F32), out_spec=BS((TR, D), row), acc_shape=(TR, D),
                  name=f"dh2_{l}")
        big["gu"] = _mm("tn", sv["h2"], dgu8, grid=(2 * NSH, 1, nkt),
                        a_spec=BS((tk, D), lambda i, j, k: (k, 0)),
                        b_spec=BS((None, tk, FS), lambda i, j, k: (i, k, 0)),
                        out_shape=_sds((2, 2 * NSH, D, FS), F32),
                        out_spec=BS((None, None, D, FS), lambda i, j, k: (l, i, 0, 0)),
                        acc_shape=(D, FS), into=big["gu"], name=f"dw_gu_{l}")
        dx1, small[f"norm2_g{l}"] = _rms_bwd(dh2, sv["x1"], p["norm2_g"][l:l + 1], dx, f"rms2_bwd_{l}")
        dcat = _mm("nt", dx1, w["wo"], grid=(T // TR, 1, 1), a_spec=BS((TR, D), row),
                   b_spec=BS((None, D, D), lambda i, j, k: (l, 0, 0)),
                   out_shape=_sds((T, D), F32), out_spec=BS((TR, D), row), acc_shape=(8, 128),
                   name=f"dcat_{l}")
        big["wo"] = _mm("tn", sv["cat"], dx1, grid=(1, 1, nkt),
                        a_spec=BS((tk, D), lambda i, j, k: (k, 0)), b_spec=BS((tk, D), lambda i, j, k: (k, 0)),
                        out_shape=_sds((2, D, D), F32), out_spec=BS((None, D, D), lambda i, j, k: (l, 0, 0)),
                        acc_shape=(D, D), into=big["wo"], name=f"dw_out_{l}")
        dqm, dkv, small[f"mq_g{l}"], small[f"mk_g{l}"] = _memattn_bwd(
            sv["z"], sv["kv"], dcat, p["mq_g4"][l:l + 1], p["mk_g4"][l:l + 1], sv["qcol"], f"memattn_bwd_{l}")
        if l == 0:
            dq, dk, dv, dbias, small["a_q_g"], small["a_k_g"] = _attn_bwd(
                sv["z"], dcat, bias, p["a_q_g2"], p["a_k_g2"])
            small["rel_u"] = _bias_reduce(dbias)
            dz = jnp.concatenate([dq, dk, dv, dqm], axis=1)
            w_in, key, n, tn = w["a"], "a", NA, 640
        else:
            du, small["conv_w"], small["conv_b"], small["ln_g"], small["ln_b"], dbin_u = _conv_bwd(
                sv["z"], dcat, p["conv_w"], p["conv_b"], p["ln_g"], p["ln_b"])
            dz = jnp.concatenate([du, dqm], axis=1)
            small["b_in_u"] = dbin_u
            w_in, key, n, tn = w["b"], "b", NBW, 896
        dh = _mm("nt", dz, w_in, grid=(T // TR, 1, n // tn),
                 a_spec=BS((TR, tn), lambda i, j, k: (i, k)), b_spec=BS((D, tn), lambda i, j, k: (0, k)),
                 out_shape=_sds((T, D), F32), out_spec=BS((TR, D), row), acc_shape=(TR, D),
                 name=f"dh_{l}")
        big[key] = _mm("tn", sv["h"], dz, grid=(1, n // tn, nkt),
                       a_spec=BS((tk, D), lambda i, j, k: (k, 0)), b_spec=BS((tk, tn), lambda i, j, k: (k, j)),
                       out_shape=_sds((D, n), F32), out_spec=BS((D, tn), lambda i, j, k: (0, j)),
                       acc_shape=(D, tn), name=f"dw_in_{l}")
        if l == 1:
            small["b_in_qm"] = _colsum(dqm, "colsum_dqm")
        dx, small[f"norm1_g{l}"] = _rms_bwd(dh, sv["x"], p["norm1_g"][l:l + 1], dx1, f"rms1_bwd_{l}")
        big["kv"] = _mm("tn", sv["memn"], dkv, grid=(1, 1, 1),
                        a_spec=BS((NB * MEMT, D), lambda i, j, k: (0, 0)),
                        b_spec=BS((NB * MEMT, 2 * MEMW), lambda i, j, k: (0, 0)),
                        out_shape=_sds((2, D, 2 * MEMW), F32),
                        out_spec=BS((None, D, 2 * MEMW), lambda i, j, k: (l, 0, 0)),
                        acc_shape=(8, 128), into=big["kv"], name=f"dw_kv_{l}")
        dmemn = _mm("nt", dkv, w["kv"], grid=(1, 1, 1),
                    a_spec=BS((NB * MEMT, 2 * MEMW), lambda i, j, k: (0, 0)),
                    b_spec=BS((None, D, 2 * MEMW), lambda i, j, k: (l, 0, 0)),
                    out_shape=_sds((NB * MEMT, D), F32), out_spec=BS((NB * MEMT, D), lambda i, j, k: (0, 0)),
                    acc_shape=(8, 128), name=f"dmemn_{l}")
        _, small[f"mem_norm_g{l}"] = _rms_bwd(dmemn, mem, p["mem_norm_g"][l:l + 1], None, f"rmsmem_bwd_{l}")
    return loss, dx, big, small


def _colsum(a, name):
    rows, cols = a.shape

    def body(a_ref, o_ref):
        @pl.when(pl.program_id(0) == 0)
        def _():
            o_ref[...] = jnp.zeros_like(o_ref)

        o_ref[...] += jnp.sum(a_ref[...].astype(F32), axis=0, keepdims=True)

    return pl.pallas_call(
        body, grid=(rows // TR,), in_specs=[BS((TR, cols), lambda i: (i, 0))],
        out_specs=BS((1, cols), lambda i: (0, 0)), out_shape=_sds((1, cols), F32),
        compiler_params=_cp(("arbitrary",)), name=name,
    )(a)


_PACK_ROWS = 64


def _pad_to(a, rows, cols=D):
    return jnp.pad(a, ((0, rows - a.shape[0]), (0, cols - a.shape[1])))


def _pack_small(sm):
    parts = [
        jnp.concatenate([sm["norm1_g0"], sm["norm1_g1"]], 0),
        jnp.concatenate([sm["mem_norm_g0"], sm["mem_norm_g1"]], 0),
        jnp.concatenate([sm["norm2_g0"], sm["norm2_g1"]], 0),
        _pad_to(sm["a_q_g"], 1), _pad_to(sm["a_k_g"], 1),
        _pad_to(jnp.concatenate([sm["mq_g0"], sm["mq_g1"]], 0), 2),
        _pad_to(jnp.concatenate([sm["mk_g0"], sm["mk_g1"]], 0), 2),
        _pad_to(sm["conv_b"], 1), _pad_to(sm["ln_g"], 1), _pad_to(sm["ln_b"], 1),
        _pad_to(sm["b_in_u"][:, :D], 1), _pad_to(sm["b_in_u"][:, D:], 1),
        _pad_to(sm["b_in_qm"], 1),
        _pad_to(sm["conv_w"][:CONVW], CONVW),
        sm["rel_u"].reshape(12, D),
    ]
    pack = jnp.concatenate(parts, 0)
    return jnp.pad(pack, ((0, _PACK_ROWS - pack.shape[0]), (0, 0)))


def _rel_table_to_u(rel_bias):
    flat = jnp.concatenate([jnp.broadcast_to(rel_bias[:, 191:192], (12, 447)), rel_bias[:, ::-1]], axis=1)
    return jnp.pad(flat, ((0, 0), (192, 1024 - 192 - 639))).reshape(12, 1, 1024)


def _u_to_rel_table(du):
    flat = du[:, 192:192 + 639]
    g = flat[:, 447:][:, ::-1]
    return g, flat[:, :447]


def kernel(x, mem, norm1_g, mem_norm_g, a_w_in, a_q_g, a_k_g, a_rel_bias, b_w_in, b_b_in, b_conv_w, b_conv_b, b_ln_g, b_ln_b, mq_g, mk_g, w_mem_kv, w_out, norm2_g, w_gate, w_up, w_down, loss_target, m_norm1_g, m_mem_norm_g, m_a_w_in, m_a_q_g, m_a_k_g, m_a_rel_bias, m_b_w_in, m_b_b_in, m_b_conv_w, m_b_conv_b, m_b_ln_g, m_b_ln_b, m_mq_g, m_mk_g, m_w_mem_kv, m_w_out, m_norm2_g, m_w_gate, m_w_up, m_w_down, v_norm1_g, v_mem_norm_g, v_a_w_in, v_a_q_g, v_a_k_g, v_a_rel_bias, v_b_w_in, v_b_b_in, v_b_conv_w, v_b_conv_b, v_b_ln_g, v_b_ln_b, v_mq_g, v_mk_g, v_w_mem_kv, v_w_out, v_norm2_g, v_w_gate, v_w_up, v_w_down):
    sx = 2 * lax.axis_index("x") + lax.axis_index("y")

    src_a = _cast_bf16([a_w_in[0]], "cast_a").reshape(1, 2, D // 2, NA // NSH)
    src_b = _cast_bf16([b_w_in[0]], "cast_b").reshape(1, 2, D // 2, NBW // NSH)
    src_kv = _cast_bf16([w_mem_kv[0], w_mem_kv[1]], "cast_kv").reshape(2, 2, 128, 2 * MEMW)
    src_wo = _cast_bf16([w_out[0], w_out[1]], "cast_wo").reshape(2, 2, 128, D)
    src_g = _cast_bf16([w_gate[0], w_gate[1]], "cast_gate").reshape(2, 2, D // 2, FS)
    src_u = _cast_bf16([w_up[0], w_up[1]], "cast_up").reshape(2, 2, D // 2, FS)
    src_wd = _cast_bf16([w_down[0], w_down[1]], "cast_wd").reshape(2, 2, FS // 2, D)
    small_src = jnp.concatenate([
        jnp.pad(b_b_in, ((0, 0), (0, 512 - 448))),
        jnp.pad(b_conv_w[0], ((0, 0), (0, 512 - 192))),
        jnp.pad(jnp.concatenate([b_conv_b, b_ln_g, b_ln_b], 0), ((0, 0), (0, 512 - 192))),
        jnp.zeros((5, 512), F32)], 0)
    dst_shapes = [(1, NSH, 2, D // 2, NA // NSH), (1, NSH, 2, D // 2, NBW // NSH),
                  (2, NSH, 2, 128, 2 * MEMW), (2, NSH, 2, 128, D),
                  (2, 2, NSH, 2, D // 2, FS), (2, NSH, 2, FS // 2, D)]
    views = [lambda d: d[0], lambda d: d[1], lambda d: d[2], lambda d: d[3],
             lambda d: d[4].at[:, 0], lambda d: d[4].at[:, 1], lambda d: d[5]]
    (ga, gb, gkv, gwo, ggu, gwd), small_all = _gather_weights(
        [src_a, src_b, src_kv, src_wo, src_g, src_u, src_wd], dst_shapes, views, small_src)
    w = dict(
        a=ga.reshape(NSH, D, NA // NSH).transpose(1, 0, 2).reshape(D, NA),
        b=gb.reshape(NSH, D, NBW // NSH).transpose(1, 0, 2).reshape(D, NBW),
        kv=gkv.reshape(2, D, 2 * MEMW), wo=gwo.reshape(2, D, D),
        gu=ggu.reshape(2, 2 * NSH, D, FS), wd=gwd.reshape(2, NSH, FS, D))

    conv_w_full = small_all[:, 1:1 + CONVW, :192].transpose(1, 0, 2).reshape(CONVW, TOK)
    vec3 = small_all[:, 32:35, :192].transpose(1, 0, 2).reshape(3, TOK)
    p = dict(
        norm1_g=norm1_g, mem_norm_g=mem_norm_g, norm2_g=norm2_g,
        a_q_g2=jnp.tile(a_q_g, (1, 2)), a_k_g2=jnp.tile(a_k_g, (1, 2)),
        mq_g4=jnp.tile(mq_g, (1, 4)), mk_g4=jnp.tile(mk_g, (1, 4)),
        rel_u=_rel_table_to_u(a_rel_bias[0]),
        b_b_in=small_all[:, 0, :448].reshape(1, NBW),
        conv_w=jnp.pad(conv_w_full, ((0, 1), (0, 0))),
        conv_b=vec3[0:1], ln_g=vec3[1:2], ln_b=vec3[2:3])

    loss, grad_x, big, small = _local_step(
        x.reshape(T, D), mem.reshape(NB * MEMT, D), loss_target.reshape(T, D), w, p)
    loss = lax.psum(loss[0, 0], ("x", "y", "c"))

    units = [
        big["a"].reshape(D, NSH, NA // NSH).transpose(1, 0, 2).reshape(1, NSH, 2, D // 2, NA // NSH),
        big["b"].reshape(D, NSH, NBW // NSH).transpose(1, 0, 2).reshape(1, NSH, 2, D // 2, NBW // NSH),
        big["kv"].reshape(2, NSH, 2, 128, 2 * MEMW),
        big["wo"].reshape(2, NSH, 2, 128, D),
        big["gu"].reshape(4, NSH, 2, D // 2, FS),
        big["wd"].reshape(2, NSH, 2, FS // 2, D),
    ]
    mine, got = _pair_exchange(units)
    sums, sums_b = [], []
    for u in range(len(units)):
        pn, _, rh, cc = mine[u].shape
        sf, sb = _pair_sum(mine[u].reshape(pn * NSH, rh, cc), got[u].reshape(pn * NSH, rh, cc), f"pair_sum_{u}")
        sums.append(sf.reshape(pn, NSH, rh, cc))
        sums_b.append(sb.reshape(pn, NSH, rh, cc))
    own, parts = _chip_exchange(sums, sums_b)
    halves = [_quad_sum(own[u], parts[u], f"quad_sum_{u}") for u in range(len(units))]
    out_shapes = [(1, D, NA // NSH), (1, D, NBW // NSH), (2, 2 * 128, 2 * MEMW), (2, 2 * 128, D),
                  (2, D, FS), (2, D, FS), (2, FS, D)]
    targets = [[(0, 0, 0)], [(0, 1, 0)], [(0, 2, 0), (1, 2, 1)], [(0, 3, 0), (1, 3, 1)],
               [(0, 4, 0), (1, 5, 0), (2, 4, 1), (3, 5, 1)], [(0, 6, 0), (1, 6, 1)]]
    g_a, g_b, g_kv, g_wo, g_gate, g_up, g_wd = _final_exchange(halves, out_shapes, targets)

    tot = _small_allreduce(_pack_small(small))
    g_rel, clip_part = _u_to_rel_table(tot[49:61])
    g_rel = jnp.concatenate([g_rel[:, :191], g_rel[:, 191:] + _rowsum(clip_part)], axis=1)
    b_in_full = jnp.concatenate([tot[15:16], tot[16:17, :512], tot[17:18, :MEMW]], axis=1)
    g_small = dict(
        norm1_g=tot[0:2], mem_norm_g=tot[2:4], norm2_g=tot[4:6],
        a_q_g=tot[6:7, :HD], a_k_g=tot[7:8, :HD], a_rel_bias=g_rel[None],
        b_b_in=lax.dynamic_slice(b_in_full, (0, sx * 448), (1, 448)),
        b_conv_w=lax.dynamic_slice(tot[18:49, :TOK], (0, sx * 192), (CONVW, 192))[None],
        b_conv_b=lax.dynamic_slice(tot[12:13, :TOK], (0, sx * 192), (1, 192)),
        b_ln_g=lax.dynamic_slice(tot[13:14, :TOK], (0, sx * 192), (1, 192)),
        b_ln_b=lax.dynamic_slice(tot[14:15, :TOK], (0, sx * 192), (1, 192)),
        mq_g=tot[8:10, :HD], mk_g=tot[10:12, :HD])

    names = ["norm1_g", "mem_norm_g", "a_w_in", "a_q_g", "a_k_g", "a_rel_bias", "b_w_in", "b_b_in",
             "b_conv_w", "b_conv_b", "b_ln_g", "b_ln_b", "mq_g", "mk_g", "w_mem_kv", "w_out",
             "norm2_g", "w_gate", "w_up", "w_down"]
    weights = dict(zip(names, [norm1_g, mem_norm_g, a_w_in, a_q_g, a_k_g, a_rel_bias, b_w_in, b_b_in,
                               b_conv_w, b_conv_b, b_ln_g, b_ln_b, mq_g, mk_g, w_mem_kv, w_out,
                               norm2_g, w_gate, w_up, w_down]))
    ms = dict(zip(names, [m_norm1_g, m_mem_norm_g, m_a_w_in, m_a_q_g, m_a_k_g, m_a_rel_bias, m_b_w_in,
                          m_b_b_in, m_b_conv_w, m_b_conv_b, m_b_ln_g, m_b_ln_b, m_mq_g, m_mk_g,
                          m_w_mem_kv, m_w_out, m_norm2_g, m_w_gate, m_w_up, m_w_down]))
    vs = dict(zip(names, [v_norm1_g, v_mem_norm_g, v_a_w_in, v_a_q_g, v_a_k_g, v_a_rel_bias, v_b_w_in,
                          v_b_b_in, v_b_conv_w, v_b_conv_b, v_b_ln_g, v_b_ln_b, v_mq_g, v_mk_g,
                          v_w_mem_kv, v_w_out, v_norm2_g, v_w_gate, v_w_up, v_w_down]))
    grads = dict(g_small)
    grads.update(a_w_in=g_a, b_w_in=g_b, w_mem_kv=g_kv, w_out=g_wo, w_gate=g_gate, w_up=g_up, w_down=g_wd)
    big_names = ["a_w_in", "b_w_in", "w_mem_kv", "w_out", "w_gate", "w_up", "w_down"]
    small_names = [n for n in names if n not in big_names]
    delta, new_m, new_v = {}, {}, {}
    for n in big_names:
        delta[n], new_m[n], new_v[n] = _adamw_big(weights[n], grads[n], ms[n], vs[n], f"adamw_{n}")
    as2d = lambda a: a.reshape(-1, a.shape[-1])
    d_s, m_s, v_s = _adamw_small([as2d(weights[n]) for n in small_names], [as2d(grads[n]) for n in small_names],
                                 [as2d(ms[n]) for n in small_names], [as2d(vs[n]) for n in small_names])
    for i, n in enumerate(small_names):
        delta[n] = d_s[i].reshape(weights[n].shape)
        new_m[n] = m_s[i].reshape(weights[n].shape)
        new_v[n] = v_s[i].reshape(weights[n].shape)

    return (loss, grad_x.reshape(NB, SEQ, D), *[grads[n] for n in names], *[delta[n] for n in names],
            *[new_m[n] for n in names], *[new_v[n] for n in names])


def _rowsum(a):
    def body(a_ref, o_ref):
        o_ref[...] = jnp.sum(a_ref[...], axis=1, keepdims=True)

    vm = BS(memory_space=pltpu.VMEM)
    return pl.pallas_call(body, in_specs=[vm], out_specs=vm, out_shape=_sds((a.shape[0], 1), F32),
                          compiler_params=_cp(), name="rowsum")(a)
```

```python
import functools

import jax
import jax.numpy as jnp
from jax import lax
from jax.experimental import pallas as pl
from jax.experimental.pallas import tpu as pltpu

F32 = jnp.float32
BF16 = jnp.bfloat16
BS = pl.BlockSpec
ANY = pl.BlockSpec(memory_space=pl.ANY)
MESH = pl.DeviceIdType.MESH

D = 1024
SEQ = 2048
NB = 2
T = NB * SEQ
MEMT = 256
HD = 64
TOK = 768
MEMW = 256
NA = 3 * TOK + MEMW
NBW = 2 * TOK + MEMW
FF = 2816
NSH = 4
FS = FF // NSH
CONVW = 31
EPS = 1e-6
NEG = -1e30
SCALE = HD ** -0.5
QB = 256
KWIN = 768
KPAD = 512
TR = 512

ADAM_LR = 0.001
ADAM_B1 = 0.9
ADAM_B2 = 0.999
ADAM_EPS = 1e-08
ADAM_WD = 0.01
ADAM_STEP = 10

_DIMS = {
    "nn": (((1,), (0,)), ((), ())),
    "nt": (((1,), (1,)), ((), ())),
    "tn": (((0,), (0,)), ((), ())),
}


def _cp(sem=None, vmem_mb=48):
    return pltpu.CompilerParams(dimension_semantics=sem, vmem_limit_bytes=vmem_mb << 20)


def _sds(shape, dtype):
    return jax.ShapeDtypeStruct(tuple(shape), dtype)


def _mm(mode, a, b, *, grid, a_spec, b_spec, out_shape, out_spec, acc_shape, name,
        extras=(), extra_specs=(), epilogue=None, into=None):
    n_ex = len(extras)
    nk = grid[2]
    dims = _DIMS[mode]

    def body(a_ref, b_ref, *rest):
        ex = rest[:n_ex]
        o_ref = rest[n_ex + (1 if into is not None else 0)]
        acc = rest[-1]
        k = pl.program_id(2)
        prod = lax.dot_general(a_ref[...].astype(BF16), b_ref[...].astype(BF16), dims,
                               preferred_element_type=F32)

        def finish(val):
            if epilogue is None:
                o_ref[...] = val.astype(o_ref.dtype)
            else:
                epilogue(val, ex, o_ref)

        if nk == 1:
            finish(prod)
        else:
            @pl.when(k == 0)
            def _():
                acc[...] = prod

            @pl.when(k > 0)
            def _():
                acc[...] += prod

            @pl.when(k == nk - 1)
            def _():
                finish(acc[...])

    ins = [a, b, *extras]
    in_specs = [a_spec, b_spec, *extra_specs]
    aliases = {}
    if into is not None:
        ins.append(into)
        in_specs.append(ANY)
        aliases = {len(ins) - 1: 0}
    return pl.pallas_call(
        body, grid=grid, in_specs=in_specs, out_specs=out_spec, out_shape=out_shape,
        scratch_shapes=[pltpu.VMEM(acc_shape if nk > 1 else (8, 128), F32)],
        input_output_aliases=aliases,
        compiler_params=_cp(("parallel", "parallel", "arbitrary")), name=name,
    )(*ins)


def _rms_fwd(x, g, name):
    rows = x.shape[0]

    def body(x_ref, g_ref, o_ref):
        xv = x_ref[...]
        r = lax.rsqrt(jnp.mean(xv * xv, axis=-1, keepdims=True) + EPS)
        o_ref[...] = (xv * r * g_ref[...]).astype(BF16)

    return pl.pallas_call(
        body, grid=(rows // TR,),
        in_specs=[BS((TR, D), lambda i: (i, 0)), BS((1, D), lambda i: (0, 0))],
        out_specs=BS((TR, D), lambda i: (i, 0)), out_shape=_sds((rows, D), BF16),
        compiler_params=_cp(("arbitrary",)), name=name,
    )(x, g)


def _rms_bwd(dh, x, g, dres, name):
    rows = x.shape[0]
    has_res = dres is not None

    def body(*refs):
        if has_res:
            dh_ref, x_ref, g_ref, r_ref, dx_ref, dg_ref = refs
        else:
            dh_ref, x_ref, g_ref, dx_ref, dg_ref = refs
        xv = x_ref[...]
        dhv = dh_ref[...]
        r = lax.rsqrt(jnp.mean(xv * xv, axis=-1, keepdims=True) + EPS)
        xh = xv * r
        gy = dhv * g_ref[...]
        dx = r * (gy - xh * jnp.mean(gy * xh, axis=-1, keepdims=True))
        if has_res:
            dx = dx + r_ref[...]
        dx_ref[...] = dx

        @pl.when(pl.program_id(0) == 0)
        def _():
            dg_ref[...] = jnp.zeros_like(dg_ref)

        dg_ref[...] += jnp.sum(dhv * xh, axis=0, keepdims=True)

    row = BS((TR, D), lambda i: (i, 0))
    vec = BS((1, D), lambda i: (0, 0))
    ins = [dh, x, g] + ([dres] if has_res else [])
    return pl.pallas_call(
        body, grid=(rows // TR,),
        in_specs=[row, row, vec] + ([row] if has_res else []),
        out_specs=[row, vec], out_shape=[_sds((rows, D), F32), _sds((1, D), F32)],
        compiler_params=_cp(("arbitrary",)), name=name,
    )(*ins)


def _group_masks(width):
    lane = lax.broadcasted_iota(jnp.int32, (1, width), 1)
    return [(lane >= HD * h) & (lane < HD * (h + 1)) for h in range(width // HD)]


def _group_mean(v, masks):
    out = jnp.zeros_like(v)
    for m in masks:
        s = jnp.sum(jnp.where(m, v, 0.0), axis=-1, keepdims=True) * (1.0 / HD)
        out = jnp.where(m, s, out)
    return out


def _head_norm(zv, g, masks):
    r = lax.rsqrt(_group_mean(zv * zv, masks) + EPS)
    return zv * r * g


def _head_norm_bwd(dy, zv, g, masks):
    r = lax.rsqrt(_group_mean(zv * zv, masks) + EPS)
    zh = zv * r
    gy = dy * g
    dz = r * (gy - zh * _group_mean(gy * zh, masks))
    return dz, jnp.sum(dy * zh, axis=0, keepdims=True)


def _fold_heads(v, width):
    vb = jnp.broadcast_to(v, (8, width))
    out = vb
    for h in range(1, width // HD):
        out = out + pltpu.roll(vb, width - HD * h, axis=1)
    return out[0:1]


def _bias_expand(u):
    def body(u_ref, o_ref):
        x = jnp.broadcast_to(u_ref[...], (QB, 1024))
        rolled = pltpu.roll(x, 1024 - (QB - 1), axis=1, stride=1, stride_axis=0)[:, :KWIN]
        row = lax.broadcasted_iota(jnp.int32, (QB, 1), 0)
        col = lax.broadcasted_iota(jnp.int32, (1, KWIN), 1)
        lo = (row // 64) * 64
        ok = (col >= lo) & (col < lo + 576)
        o_ref[...] = jnp.where(ok, rolled, NEG)

    return pl.pallas_call(
        body, grid=(12,), in_specs=[BS((None, 1, 1024), lambda h: (h, 0, 0))],
        out_specs=BS((None, QB, KWIN), lambda h: (h, 0, 0)), out_shape=_sds((12, QB, KWIN), F32),
        compiler_params=_cp(("arbitrary",)), name="bias_expand",
    )(u)


def _bias_reduce(ds):
    def body(d_ref, o_ref):
        ri = lax.broadcasted_iota(jnp.int32, (QB, QB), 0)
        ci = lax.broadcasted_iota(jnp.int32, (QB, QB), 1)
        flip = (ri + ci == QB - 1).astype(F32)
        drev = jnp.dot(flip, d_ref[...], precision=lax.Precision.HIGHEST, preferred_element_type=F32)
        x = jnp.concatenate([drev, jnp.zeros((QB, 1024 - KWIN), F32)], axis=1)
        rolled = pltpu.roll(x, 0, axis=1, stride=1, stride_axis=0)
        o_ref[...] = jnp.sum(rolled, axis=0, keepdims=True)

    return pl.pallas_call(
        body, grid=(12,), in_specs=[BS((None, QB, KWIN), lambda h: (h, 0, 0))],
        out_specs=BS((None, 1, 1024), lambda h: (h, 0, 0)), out_shape=_sds((12, 1, 1024), F32),
        compiler_params=_cp(("arbitrary",)), name="bias_reduce",
    )(ds)


def _attn_softmax(qh, kw, bias, startadd):
    s = lax.dot_general(qh, kw, _DIMS["nt"], preferred_element_type=F32) * SCALE + bias + startadd
    m = jnp.max(s, axis=-1, keepdims=True)
    p = jnp.exp(s - m)
    return p / jnp.sum(p, axis=-1, keepdims=True)


def _attn_fwd(z, bias, gq2, gk2):
    def body(q_ref, k_ref, v_ref, b_ref, gq_ref, gk_ref, o_ref, qn_s, kn_s, v_s):
        masks = _group_masks(128)
        qn_s[...] = _head_norm(q_ref[...], gq_ref[...], masks).astype(BF16)
        kn_s[0:KPAD, :] = jnp.zeros((KPAD, 128), BF16)
        kn_s[KPAD:, :] = _head_norm(k_ref[...], gk_ref[...], masks).astype(BF16)
        v_s[0:KPAD, :] = jnp.zeros((KPAD, 128), BF16)
        v_s[KPAD:, :] = v_ref[...].astype(BF16)
        col = lax.broadcasted_iota(jnp.int32, (1, KWIN), 1)

        def blk(i, carry):
            r0 = pl.multiple_of(i * QB, QB)
            qb = qn_s[pl.ds(r0, QB), :]
            kw = kn_s[pl.ds(r0, KWIN), :]
            vw = v_s[pl.ds(r0, KWIN), :]
            startadd = jnp.where(col + r0 < KPAD, NEG, 0.0)
            o = jnp.zeros((QB, 128), F32)
            for h in range(2):
                qh = jnp.where(masks[h], qb, jnp.zeros_like(qb))
                vh = jnp.where(masks[h], vw, jnp.zeros_like(vw))
                p = _attn_softmax(qh, kw, b_ref[h], startadd).astype(BF16)
                o = o + jnp.dot(p, vh, preferred_element_type=F32)
            o_ref[pl.ds(r0, QB), :] = o.astype(BF16)
            return carry

        lax.fori_loop(0, SEQ // QB, blk, 0)

    vec = BS((1, 128), lambda b, hp: (0, 0))
    return pl.pallas_call(
        body, grid=(NB, 6),
        in_specs=[BS((SEQ, 128), lambda b, hp: (b, hp)),
                  BS((SEQ, 128), lambda b, hp: (b, 6 + hp)),
                  BS((SEQ, 128), lambda b, hp: (b, 12 + hp)),
                  BS((2, QB, KWIN), lambda b, hp: (hp, 0, 0)), vec, vec],
        out_specs=BS((SEQ, 128), lambda b, hp: (b, hp)),
        out_shape=_sds((T, D), BF16),
        scratch_shapes=[pltpu.VMEM((SEQ, 128), BF16), pltpu.VMEM((SEQ + KPAD, 128), BF16),
                        pltpu.VMEM((SEQ + KPAD, 128), BF16)],
        compiler_params=_cp(("arbitrary", "arbitrary")), name="attn_fwd",
    )(z, z, z, bias, gq2, gk2)


def _attn_bwd(z, dcat, bias, gq2, gk2):
    def body(q_ref, k_ref, v_ref, do_ref, b_ref, gq_ref, gk_ref,
             dq_ref, dk_ref, dv_ref, db_ref, dgq_ref, dgk_ref,
             qn_s, kn_s, v_s, dqn_s, dkn_s, dv_s):
        hp = pl.program_id(0)
        b = pl.program_id(1)
        masks = _group_masks(128)
        qn_s[...] = _head_norm(q_ref[...], gq_ref[...], masks).astype(BF16)
        kn_s[0:KPAD, :] = jnp.zeros((KPAD, 128), BF16)
        kn_s[KPAD:, :] = _head_norm(k_ref[...], gk_ref[...], masks).astype(BF16)
        v_s[0:KPAD, :] = jnp.zeros((KPAD, 128), BF16)
        v_s[KPAD:, :] = v_ref[...].astype(BF16)
        dkn_s[...] = jnp.zeros_like(dkn_s)
        dv_s[...] = jnp.zeros_like(dv_s)

        @pl.when(b == 0)
        def _():
            db_ref[...] = jnp.zeros_like(db_ref)

        @pl.when((b == 0) & (hp == 0))
        def _():
            dgq_ref[...] = jnp.zeros_like(dgq_ref)
            dgk_ref[...] = jnp.zeros_like(dgk_ref)

        col = lax.broadcasted_iota(jnp.int32, (1, KWIN), 1)

        def blk(i, carry):
            r0 = pl.multiple_of(i * QB, QB)
            qb = qn_s[pl.ds(r0, QB), :]
            kw = kn_s[pl.ds(r0, KWIN), :]
            vw = v_s[pl.ds(r0, KWIN), :]
            dob = do_ref[pl.ds(r0, QB), :].astype(BF16)
            startadd = jnp.where(col + r0 < KPAD, NEG, 0.0)
            dqn = jnp.zeros((QB, 128), F32)
            dkw = jnp.zeros((KWIN, 128), F32)
            dvw = jnp.zeros((KWIN, 128), F32)
            for h in range(2):
                qh = jnp.where(masks[h], qb, jnp.zeros_like(qb))
                kh = jnp.where(masks[h], kw, jnp.zeros_like(kw))
                doh = jnp.where(masks[h], dob, jnp.zeros_like(dob))
                p = _attn_softmax(qh, kw, b_ref[h], startadd)
                dvw = dvw + lax.dot_general(p.astype(BF16), doh, _DIMS["tn"],
                                            preferred_element_type=F32)
                dp = lax.dot_general(doh, vw, _DIMS["nt"], preferred_element_type=F32)
                ds = p * (dp - jnp.sum(dp * p, axis=-1, keepdims=True))
                db_ref[h] += ds
                dsb = (ds * SCALE).astype(BF16)
                dqn = dqn + jnp.dot(dsb, kh, preferred_element_type=F32)
                dkw = dkw + lax.dot_general(dsb, qh, _DIMS["tn"], preferred_element_type=F32)
            dqn_s[pl.ds(r0, QB), :] = dqn
            dkn_s[pl.ds(r0, KWIN), :] += dkw
            dv_s[pl.ds(r0, KWIN), :] += dvw
            return carry

        lax.fori_loop(0, SEQ // QB, blk, 0)

        dq, dgq = _head_norm_bwd(dqn_s[...], q_ref[...], gq_ref[...], masks)
        dk, dgk = _head_norm_bwd(dkn_s[KPAD:, :], k_ref[...], gk_ref[...], masks)
        dq_ref[...] = dq.astype(BF16)
        dk_ref[...] = dk.astype(BF16)
        dv_ref[...] = dv_s[KPAD:, :].astype(BF16)
        dgq_ref[...] += _fold_heads(dgq, 128)
        dgk_ref[...] += _fold_heads(dgk, 128)

    vec = BS((1, 128), lambda hp, b: (0, 0))
    row = BS((SEQ, 128), lambda hp, b: (b, hp))
    return pl.pallas_call(
        body, grid=(6, NB),
        in_specs=[row,
                  BS((SEQ, 128), lambda hp, b: (b, 6 + hp)),
                  BS((SEQ, 128), lambda hp, b: (b, 12 + hp)),
                  row,
                  BS((2, QB, KWIN), lambda hp, b: (hp, 0, 0)), vec, vec],
        out_specs=[row, row, row, BS((2, QB, KWIN), lambda hp, b: (hp, 0, 0)), vec, vec],
        out_shape=[_sds((T, TOK), BF16), _sds((T, TOK), BF16), _sds((T, TOK), BF16),
                   _sds((12, QB, KWIN), F32), _sds((1, 128), F32), _sds((1, 128), F32)],
        scratch_shapes=[pltpu.VMEM((SEQ, 128), BF16), pltpu.VMEM((SEQ + KPAD, 128), BF16),
                        pltpu.VMEM((SEQ + KPAD, 128), BF16), pltpu.VMEM((SEQ, 128), F32),
                        pltpu.VMEM((SEQ + KPAD, 128), F32), pltpu.VMEM((SEQ + KPAD, 128), F32)],
        compiler_params=_cp(("arbitrary", "arbitrary")), name="attn_bwd",
    )(z, z, z, dcat, bias, gq2, gk2)


def _mem_softmax(qh, kn):
    s = lax.dot_general(qh, kn, _DIMS["nt"], preferred_element_type=F32) * SCALE
    m = jnp.max(s, axis=-1, keepdims=True)
    p = jnp.exp(s - m)
    return p / jnp.sum(p, axis=-1, keepdims=True)


def _memattn_fwd(z, kv, cat, gq4, gk4, qcol, name):
    def body(q_ref, k_ref, v_ref, gq_ref, gk_ref, cat_ref, o_ref):
        del cat_ref
        masks = _group_masks(MEMW)
        qn = _head_norm(q_ref[...], gq_ref[...], masks).astype(BF16)
        kn = _head_norm(k_ref[...], gk_ref[...], masks).astype(BF16)
        vv = v_ref[...].astype(BF16)
        o = jnp.zeros((TR, MEMW), F32)
        for h in range(4):
            qh = jnp.where(masks[h], qn, jnp.zeros_like(qn))
            vh = jnp.where(masks[h], vv, jnp.zeros_like(vv))
            p = _mem_softmax(qh, kn).astype(BF16)
            o = o + jnp.dot(p, vh, preferred_element_type=F32)
        o_ref[...] = o.astype(BF16)

    nt = SEQ // TR
    vec = BS((1, MEMW), lambda b, t: (0, 0))
    return pl.pallas_call(
        body, grid=(NB, nt),
        in_specs=[BS((TR, MEMW), lambda b, t: (b * nt + t, qcol)),
                  BS((MEMT, MEMW), lambda b, t: (b, 0)),
                  BS((MEMT, MEMW), lambda b, t: (b, 1)), vec, vec, ANY],
        out_specs=BS((TR, MEMW), lambda b, t: (b * nt + t, 3)),
        out_shape=_sds((T, D), BF16), input_output_aliases={5: 0},
        compiler_params=_cp(("arbitrary", "arbitrary")), name=name,
    )(z, kv, kv, gq4, gk4, cat)


def _memattn_bwd(z, kv, dcat, gq4, gk4, qcol, name):
    nt = SEQ // TR

    def body(q_ref, k_ref, v_ref, do_ref, gq_ref, gk_ref,
             dq_ref, dkv_ref, dgq_ref, dgk_ref, dkn_s, dv_s):
        b = pl.program_id(0)
        t = pl.program_id(1)
        masks = _group_masks(MEMW)
        qz = q_ref[...]
        kz = k_ref[...]
        qn = _head_norm(qz, gq_ref[...], masks).astype(BF16)
        kn = _head_norm(kz, gk_ref[...], masks).astype(BF16)
        vv = v_ref[...].astype(BF16)
        dob = do_ref[...].astype(BF16)

        @pl.when(t == 0)
        def _():
            dkn_s[...] = jnp.zeros_like(dkn_s)
            dv_s[...] = jnp.zeros_like(dv_s)

        @pl.when((t == 0) & (b == 0))
        def _():
            dgq_ref[...] = jnp.zeros_like(dgq_ref)
            dgk_ref[...] = jnp.zeros_like(dgk_ref)

        dqn = jnp.zeros((TR, MEMW), F32)
        dkn = jnp.zeros((MEMT, MEMW), F32)
        dvv = jnp.zeros((MEMT, MEMW), F32)
        for h in range(4):
            qh = jnp.where(masks[h], qn, jnp.zeros_like(qn))
            kh = jnp.where(masks[h], kn, jnp.zeros_like(kn))
            doh = jnp.where(masks[h], dob, jnp.zeros_like(dob))
            p = _mem_softmax(qh, kn)
            dvv = dvv + lax.dot_general(p.astype(BF16), doh, _DIMS["tn"], preferred_element_type=F32)
            dp = lax.dot_general(doh, vv, _DIMS["nt"], preferred_element_type=F32)
            ds = p * (dp - jnp.sum(dp * p, axis=-1, keepdims=True))
            dsb = (ds * SCALE).astype(BF16)
            dqn = dqn + jnp.dot(dsb, kh, preferred_element_type=F32)
            dkn = dkn + lax.dot_general(dsb, qh, _DIMS["tn"], preferred_element_type=F32)
        dkn_s[...] += dkn
        dv_s[...] += dvv
        dq, dgq = _head_norm_bwd(dqn, qz, gq_ref[...], masks)
        dq_ref[...] = dq.astype(BF16)
        dgq_ref[...] += _fold_heads(dgq, MEMW)

        @pl.when(t == nt - 1)
        def _():
            dk, dgk = _head_norm_bwd(dkn_s[...], kz, gk_ref[...], masks)
            dkv_ref[:, 0:MEMW] = dk
            dkv_ref[:, MEMW:] = dv_s[...]
            dgk_ref[...] += _fold_heads(dgk, MEMW)

    vec = BS((1, MEMW), lambda b, t: (0, 0))
    return pl.pallas_call(
        body, grid=(NB, nt),
        in_specs=[BS((TR, MEMW), lambda b, t: (b * nt + t, qcol)),
                  BS((MEMT, MEMW), lambda b, t: (b, 0)),
                  BS((MEMT, MEMW), lambda b, t: (b, 1)),
                  BS((TR, MEMW), lambda b, t: (b * nt + t, 3)), vec, vec],
        out_specs=[BS((TR, MEMW), lambda b, t: (b * nt + t, 0)),
                   BS((MEMT, 2 * MEMW), lambda b, t: (b, 0)), vec, vec],
        out_shape=[_sds((T, MEMW), BF16), _sds((NB * MEMT, 2 * MEMW), F32),
                   _sds((1, MEMW), F32), _sds((1, MEMW), F32)],
        scratch_shapes=[pltpu.VMEM((MEMT, MEMW), F32), pltpu.VMEM((MEMT, MEMW), F32)],
        compiler_params=_cp(("arbitrary", "arbitrary")), name=name,
    )(z, kv, kv, dcat, gq4, gk4)


HALO = 32
NEXT = 64
RT = 64


def _glu(zz):
    return zz[:, :TOK] * jax.nn.sigmoid(zz[:, TOK:])


def _layer_norm_parts(y):
    mu = jnp.mean(y, axis=-1, keepdims=True)
    yc = y - mu
    rstd = lax.rsqrt(jnp.mean(yc * yc, axis=-1, keepdims=True) + EPS)
    return yc * rstd, rstd


def _conv_rows(w_ref, hbuf, r0, rows):
    y = jnp.zeros((rows, TOK), F32)
    for j in range(CONVW):
        y = y + w_ref[j:j + 1, :] * hbuf[r0 + (HALO - CONVW + 1) + j:r0 + (HALO - CONVW + 1) + j + rows, :]
    return y


def _conv_fwd(z, cw, cb, lg, lb):
    nt = SEQ // TR

    def body(zc_ref, zp_ref, w_ref, cb_ref, lg_ref, lb_ref, o_ref, hbuf):
        t = pl.program_id(1)
        hbuf[0:HALO, :] = jnp.where(t == 0, 0.0, _glu(zp_ref[...]))
        hbuf[HALO:, :] = _glu(zc_ref[...])
        for r0 in range(0, TR, RT):
            y = _conv_rows(w_ref, hbuf, r0, RT) + cb_ref[...]
            yh, _ = _layer_norm_parts(y)
            o = yh * lg_ref[...] + lb_ref[...]
            o_ref[r0:r0 + RT, :] = (o * jax.nn.sigmoid(o)).astype(BF16)

    vec = BS((1, TOK), lambda b, t: (0, 0))
    per = TR // HALO
    return pl.pallas_call(
        body, grid=(NB, nt),
        in_specs=[BS((TR, 2 * TOK), lambda b, t: (b * nt + t, 0)),
                  BS((HALO, 2 * TOK), lambda b, t: (jnp.maximum((b * nt + t) * per - 1, 0), 0)),
                  BS((32, TOK), lambda b, t: (0, 0)), vec, vec, vec],
        out_specs=BS((TR, TOK), lambda b, t: (b * nt + t, 0)),
        out_shape=_sds((T, D), BF16),
        scratch_shapes=[pltpu.VMEM((HALO + TR, TOK), F32)],
        compiler_params=_cp(("arbitrary", "arbitrary")), name="conv_fwd",
    )(z, z, cw, cb, lg, lb)


def _conv_bwd(z, dcat, cw, cb, lg, lb):
    nt = SEQ // TR
    ext = TR + NEXT

    def body(zc_ref, zp_ref, zn_ref, dc_ref, dn_ref, w_ref, cb_ref, lg_ref, lb_ref,
             du_ref, dw_ref, dcb_ref, dlg_ref, dlb_ref, dbin_ref, hbuf, dybuf):
        b = pl.program_id(0)
        t = pl.program_id(1)

        @pl.when((b == 0) & (t == 0))
        def _():
            dw_ref[...] = jnp.zeros_like(dw_ref)
            dcb_ref[...] = jnp.zeros_like(dcb_ref)
            dlg_ref[...] = jnp.zeros_like(dlg_ref)
            dlb_ref[...] = jnp.zeros_like(dlb_ref)
            dbin_ref[...] = jnp.zeros_like(dbin_ref)

        hbuf[0:HALO, :] = jnp.where(t == 0, 0.0, _glu(zp_ref[...]))
        hbuf[HALO:HALO + TR, :] = _glu(zc_ref[...])
        hbuf[HALO + TR:, :] = _glu(zn_ref[...])
        last = t == nt - 1
        for r0 in range(0, ext, RT):
            y = _conv_rows(w_ref, hbuf, r0, RT) + cb_ref[...]
            yh, rstd = _layer_norm_parts(y)
            o = yh * lg_ref[...] + lb_ref[...]
            sg = jax.nn.sigmoid(o)
            if r0 < TR:
                dtok = dc_ref[r0:r0 + RT, :]
            else:
                dtok = jnp.where(last, 0.0, dn_ref[...])
            do = dtok * (sg * (1.0 + o * (1.0 - sg)))
            dyh = do * lg_ref[...]
            dy = rstd * (dyh - jnp.mean(dyh, axis=-1, keepdims=True)
                         - yh * jnp.mean(dyh * yh, axis=-1, keepdims=True))
            dybuf[r0:r0 + RT, :] = dy
            if r0 < TR:
                dlg_ref[...] += jnp.sum(do * yh, axis=0, keepdims=True)
                dlb_ref[...] += jnp.sum(do, axis=0, keepdims=True)
                dcb_ref[...] += jnp.sum(dy, axis=0, keepdims=True)
        for r0 in range(0, TR, RT):
            dh = jnp.zeros((RT, TOK), F32)
            for j in range(CONVW):
                dh = dh + w_ref[j:j + 1, :] * dybuf[r0 + (CONVW - 1) - j:r0 + (CONVW - 1) - j + RT, :]
            a = zc_ref[r0:r0 + RT, 0:TOK]
            sg = jax.nn.sigmoid(zc_ref[r0:r0 + RT, TOK:])
            da = dh * sg
            dg = dh * a * (sg * (1.0 - sg))
            du_ref[r0:r0 + RT, 0:TOK] = da.astype(BF16)
            du_ref[r0:r0 + RT, TOK:] = dg.astype(BF16)
            dbin_ref[:, 0:TOK] += jnp.sum(da, axis=0, keepdims=True)
            dbin_ref[:, TOK:] += jnp.sum(dg, axis=0, keepdims=True)
        for j in range(CONVW):
            acc = jnp.zeros((8, TOK), F32)
            for r0 in range(0, TR, RT):
                prod = dybuf[r0:r0 + RT, :] * hbuf[r0 + (HALO - CONVW + 1) + j:r0 + (HALO - CONVW + 1) + j + RT, :]
                acc = acc + jnp.sum(prod.reshape(RT // 8, 8, TOK), axis=0)
            dw_ref[j:j + 1, :] += jnp.sum(acc, axis=0, keepdims=True)

    vec = BS((1, TOK), lambda b, t: (0, 0))
    perh = TR // HALO
    pern = TR // NEXT
    nlast_n = T // NEXT - 1
    return pl.pallas_call(
        body, grid=(NB, nt),
        in_specs=[BS((TR, 2 * TOK), lambda b, t: (b * nt + t, 0)),
                  BS((HALO, 2 * TOK), lambda b, t: (jnp.maximum((b * nt + t) * perh - 1, 0), 0)),
                  BS((NEXT, 2 * TOK), lambda b, t: (jnp.minimum((b * nt + t + 1) * pern, nlast_n), 0)),
                  BS((TR, TOK), lambda b, t: (b * nt + t, 0)),
                  BS((NEXT, TOK), lambda b, t: (jnp.minimum((b * nt + t + 1) * pern, nlast_n), 0)),
                  BS((32, TOK), lambda b, t: (0, 0)), vec, vec, vec],
        out_specs=[BS((TR, 2 * TOK), lambda b, t: (b * nt + t, 0)),
                   BS((32, TOK), lambda b, t: (0, 0)), vec, vec, vec,
                   BS((1, 2 * TOK), lambda b, t: (0, 0))],
        out_shape=[_sds((T, 2 * TOK), BF16), _sds((32, TOK), F32), _sds((1, TOK), F32),
                   _sds((1, TOK), F32), _sds((1, TOK), F32), _sds((1, 2 * TOK), F32)],
        scratch_shapes=[pltpu.VMEM((HALO + TR + NEXT, TOK), F32), pltpu.VMEM((ext, TOK), F32)],
        compiler_params=_cp(("arbitrary", "arbitrary")), name="conv_bwd",
    )(z, z, z, dcat, dcat, cw, cb, lg, lb)


def _ffn_up(h2, wgu, l):
    def body(h_ref, wg_ref, wu_ref, g_ref, u_ref, a_ref):
        hv = h_ref[...]
        g = jnp.dot(hv, wg_ref[...], preferred_element_type=F32)
        u = jnp.dot(hv, wu_ref[...], preferred_element_type=F32)
        g_ref[...] = g
        u_ref[...] = u
        a_ref[...] = (g * jax.nn.sigmoid(g) * u).astype(BF16)

    out = BS((None, TR, FS), lambda i, s: (s, i, 0))
    return pl.pallas_call(
        body, grid=(T // TR, NSH),
        in_specs=[BS((TR, D), lambda i, s: (i, 0)),
                  BS((None, None, D, FS), lambda i, s: (l, s, 0, 0)),
                  BS((None, None, D, FS), lambda i, s: (l, NSH + s, 0, 0))],
        out_specs=[out, out, out],
        out_shape=[_sds((NSH, T, FS), F32), _sds((NSH, T, FS), F32), _sds((NSH, T, FS), BF16)],
        compiler_params=_cp(("parallel", "arbitrary")), name=f"ffn_up_{l}",
    )(h2, wgu, wgu)


def _ffn_down_bwd(dx, wd, g, u, l):
    def epilogue(dact, ex, o_ref):
        gv = ex[0][...]
        uv = ex[1][...]
        sg = jax.nn.sigmoid(gv)
        o_ref[0] = (dact * uv * (sg * (1.0 + gv * (1.0 - sg)))).astype(BF16)
        o_ref[1] = (dact * (gv * sg)).astype(BF16)

    ex_spec = BS((None, TR, FS), lambda i, s, k: (s, i, 0))
    return _mm("nt", dx, wd, grid=(T // TR, NSH, 1),
               a_spec=BS((TR, D), lambda i, s, k: (i, 0)),
               b_spec=BS((None, None, FS, D), lambda i, s, k: (l, s, 0, 0)),
               out_shape=_sds((2, NSH, T, FS), BF16),
               out_spec=BS((2, None, TR, FS), lambda i, s, k: (0, s, i, 0)),
               acc_shape=(TR, FS), extras=(g, u), extra_specs=(ex_spec, ex_spec),
               epilogue=epilogue, name=f"ffn_down_bwd_{l}")


def _loss_head(y, target):
    def body(y_ref, t_ref, l_ref, dy_ref, acc):
        i = pl.program_id(0)
        e = y_ref[...] - t_ref[...]
        dy_ref[...] = e * (1.0 / D)

        @pl.when(i == 0)
        def _():
            acc[...] = jnp.zeros_like(acc)

        acc[...] += jnp.sum(jnp.mean(e * e, axis=-1, keepdims=True), axis=0, keepdims=True)

        @pl.when(i == T // TR - 1)
        def _():
            l_ref[...] = 0.5 * acc[...]

    row = BS((TR, D), lambda i: (i, 0))
    return pl.pallas_call(
        body, grid=(T // TR,), in_specs=[row, row],
        out_specs=[BS((1, 1), lambda i: (0, 0)), row],
        out_shape=[_sds((1, 1), F32), _sds((T, D), F32)],
        scratch_shapes=[pltpu.VMEM((1, 1), F32)],
        compiler_params=_cp(("arbitrary",)), name="loss_head",
    )(y, target)


def _row_tile(rows, cols, itemsize=4, limit=2 << 20):
    tr = rows
    while tr * cols * itemsize > limit and tr % 2 == 0 and (tr // 2) % 16 == 0:
        tr //= 2
    return tr


def _cast_bf16(arrs, name):
    n = len(arrs)
    rows, cols = arrs[0].shape
    tr = _row_tile(rows, cols)

    def body(*refs):
        o_ref = refs[n]
        k = pl.program_id(0)
        val = refs[0][...]
        for j in range(1, n):
            val = jnp.where(k == j, refs[j][...], val)
        o_ref[...] = val.astype(BF16)

    return pl.pallas_call(
        body, grid=(n, rows // tr),
        in_specs=[BS((tr, cols), lambda k, i: (i, 0))] * n,
        out_specs=BS((None, tr, cols), lambda k, i: (k, i, 0)),
        out_shape=_sds((n, rows, cols), BF16),
        compiler_params=_cp(("arbitrary", "arbitrary")), name=name,
    )(*arrs)


def _quad_sum(own, got, name):
    n, rows, cols = own.shape
    tr = _row_tile(rows, cols)

    def body(a_ref, q_ref, o_ref):
        o_ref[...] = ((a_ref[...] + q_ref[0].astype(F32)) + q_ref[1].astype(F32)) + q_ref[2].astype(F32)

    spec = BS((None, tr, cols), lambda k, i: (k, i, 0))
    return pl.pallas_call(
        body, grid=(n, rows // tr),
        in_specs=[spec, BS((3, None, tr, cols), lambda k, i: (0, k, i, 0))], out_specs=spec,
        out_shape=_sds((n, rows, cols), F32),
        compiler_params=_cp(("arbitrary", "arbitrary")), name=name,
    )(own, got)


def _adam_math(w, g, m, v):
    m = ADAM_B1 * m + (1.0 - ADAM_B1) * g
    v = ADAM_B2 * v + (1.0 - ADAM_B2) * (g * g)
    m_hat = m / (1.0 - ADAM_B1 ** ADAM_STEP)
    v_hat = v / (1.0 - ADAM_B2 ** ADAM_STEP)
    delta = -ADAM_LR * (m_hat / (jnp.sqrt(v_hat) + ADAM_EPS) + ADAM_WD * w)
    return delta, m, v


def _adamw_big(w, g, m, v, name):
    shape = w.shape
    cols = shape[-1]
    rows = w.size // cols
    tr = _row_tile(rows, cols, limit=1 << 20)

    def body(w_ref, g_ref, m_ref, v_ref, d_ref, nm_ref, nv_ref):
        d, nm, nv = _adam_math(w_ref[...], g_ref[...], m_ref[...], v_ref[...])
        d_ref[...] = d
        nm_ref[...] = nm
        nv_ref[...] = nv

    spec = BS((tr, cols), lambda i: (i, 0))
    outs = pl.pallas_call(
        body, grid=(rows // tr,), in_specs=[spec] * 4, out_specs=[spec] * 3,
        out_shape=[_sds((rows, cols), F32)] * 3,
        compiler_params=_cp(("arbitrary",)), name=name,
    )(*[a.reshape(rows, cols) for a in (w, g, m, v)])
    return [o.reshape(shape) for o in outs]


def _adamw_small(ws, gs, ms, vs):
    n = len(ws)

    def body(*refs):
        for i in range(n):
            d, nm, nv = _adam_math(refs[i][...], refs[n + i][...], refs[2 * n + i][...],
                                   refs[3 * n + i][...])
            refs[4 * n + i][...] = d
            refs[5 * n + i][...] = nm
            refs[6 * n + i][...] = nv

    vm = BS(memory_space=pltpu.VMEM)
    outs = pl.pallas_call(
        body, in_specs=[vm] * (4 * n), out_specs=[vm] * (3 * n),
        out_shape=[_sds(w.shape, F32) for w in ws] * 3,
        compiler_params=_cp(), name="adamw_small",
    )(*ws, *gs, *ms, *vs)
    return outs[:n], outs[n:2 * n], outs[2 * n:]


def _place():
    x, y, c = lax.axis_index("x"), lax.axis_index("y"), lax.axis_index("c")
    chips = [(1 - x, y), (x, 1 - y), (1 - x, 1 - y)]
    return x, y, c, chips


def _gather_weights(srcs, dst_shapes, views, small):
    nu = len(srcs)
    nd = len(dst_shapes)

    def body(*refs):
        src = refs[:nu]
        small_ref = refs[nu]
        dst = refs[nu + 1:nu + 1 + nd]
        small_dst = refs[nu + 1 + nd]
        send, recv, fsend, frecv, lsem, ssend, srecv = refs[nu + 2 + nd:]
        x, y, c, chips = _place()
        s = 2 * x + y
        vw = [views[u](dst) for u in range(nu)]

        def ici(u, j, shard, to):
            return pltpu.make_async_remote_copy(
                src_ref=src[u].at[:, c], dst_ref=vw[u].at[:, shard, c],
                send_sem=send.at[3 * u + j], recv_sem=recv.at[3 * u + j],
                device_id=to, device_id_type=MESH)

        def fwd(u, j, shard, half):
            return pltpu.make_async_remote_copy(
                src_ref=vw[u].at[:, shard, half], dst_ref=vw[u].at[:, shard, half],
                send_sem=fsend.at[3 * u + j], recv_sem=frecv.at[3 * u + j],
                device_id=(x, y, 1 - c), device_id_type=MESH)

        def small_copy(j, shard, to):
            return pltpu.make_async_remote_copy(
                src_ref=small_ref, dst_ref=small_dst.at[shard],
                send_sem=ssend.at[j], recv_sem=srecv.at[j], device_id=to, device_id_type=MESH)

        local = [pltpu.make_async_copy(src[u], vw[u].at[:, s], lsem.at[u]) for u in range(nu)]
        local.append(pltpu.make_async_copy(small_ref, small_dst.at[s], lsem.at[nu]))
        for cp in local:
            cp.start()
        for j, chip in enumerate(chips):
            small_copy(j, s, (*chip, c)).start()
        for u in range(nu):
            for j, chip in enumerate(chips):
                ici(u, j, s, (*chip, c)).start()
        for u in range(nu):
            for j, chip in enumerate(chips):
                sj = 2 * chip[0] + chip[1]
                ici(u, j, sj, (x, y, c)).wait_recv()
                fwd(u, j, sj, c).start()
        for u in range(nu):
            for j, chip in enumerate(chips):
                sj = 2 * chip[0] + chip[1]
                fwd(u, j, sj, 1 - c).wait_recv()
        for j, chip in enumerate(chips):
            sj = 2 * chip[0] + chip[1]
            small_copy(j, sj, (x, y, c)).wait_recv()
        for u in range(nu):
            for j, chip in enumerate(chips):
                ici(u, j, s, (*chip, c)).wait_send()
                fwd(u, j, s, c).wait_send()
        for j, chip in enumerate(chips):
            small_copy(j, s, (*chip, c)).wait_send()
        for cp in local:
            cp.wait()

    dma = pltpu.SemaphoreType.DMA
    outs = pl.pallas_call(
        body, in_specs=[ANY] * (nu + 1), out_specs=[ANY] * (nd + 1),
        out_shape=[_sds(sh, BF16) for sh in dst_shapes] + [_sds((NSH,) + small.shape, F32)],
        scratch_shapes=[dma((3 * nu,)), dma((3 * nu,)), dma((3 * nu,)), dma((3 * nu,)),
                        dma((nu + 1,)), dma((3,)), dma((3,))],
        compiler_params=pltpu.CompilerParams(has_side_effects=True), name="gather_weights",
    )(*srcs, small)
    return outs[:nd], outs[nd]


def _pair_reduce(g, name):
    pn, _, _, rh, cc = g.shape
    n = pn * NSH

    def body(g_ref, sf_ref, sb_ref, sendb, recvb, stage, outf, outb, send, recv, lsem, osem):
        x, y, c, _ = _place()

        def load(k, half):
            return pltpu.make_async_copy(g_ref.at[k // NSH, k % NSH, half], stage.at[k % 2], lsem.at[k % 2])

        def push(k):
            return pltpu.make_async_remote_copy(
                src_ref=sendb.at[k], dst_ref=recvb.at[k], send_sem=send.at[k], recv_sem=recv.at[k],
                device_id=(x, y, 1 - c), device_id_type=MESH)

        def store(k):
            return (pltpu.make_async_copy(outf.at[k % 2], sf_ref.at[k // NSH, k % NSH], osem.at[k % 2]),
                    pltpu.make_async_copy(outb.at[k % 2], sb_ref.at[k // NSH, k % NSH], osem.at[2 + k % 2]))

        load(0, 1 - c).start()
        for k in range(n):
            if k + 1 < n:
                load(k + 1, 1 - c).start()
            load(k, 1 - c).wait()
            sendb[k] = stage[k % 2].astype(BF16)
            push(k).start()
        load(0, c).start()
        for k in range(n):
            if k + 1 < n:
                load(k + 1, c).start()
            load(k, c).wait()
            push(k).wait_recv()
            total = stage[k % 2] + recvb[k].astype(F32)
            if k >= 2:
                for cp in store(k - 2):
                    cp.wait()
            outf[k % 2] = total
            outb[k % 2] = total.astype(BF16)
            for cp in store(k):
                cp.start()
        for k in range(max(n - 2, 0), n):
            for cp in store(k):
                cp.wait()
        for k in range(n):
            push(k).wait_send()

    dma = pltpu.SemaphoreType.DMA
    return pl.pallas_call(
        body, in_specs=[ANY], out_specs=[ANY, ANY],
        out_shape=[_sds((pn, NSH, rh, cc), F32), _sds((pn, NSH, rh, cc), BF16)],
        scratch_shapes=[pltpu.VMEM((n, rh, cc), BF16), pltpu.VMEM((n, rh, cc), BF16),
                        pltpu.VMEM((2, rh, cc), F32), pltpu.VMEM((2, rh, cc), F32),
                        pltpu.VMEM((2, rh, cc), BF16), dma((n,)), dma((n,)), dma((2,)), dma((4,))],
        compiler_params=pltpu.CompilerParams(has_side_effects=True, vmem_limit_bytes=56 << 20),
        name=name,
    )(g)


def _chip_exchange(sums, sums_bf16):
    nu = len(sums)

    def body(*refs):
        sf = refs[:nu]
        sb = refs[nu:2 * nu]
        own = refs[2 * nu:3 * nu]
        got = refs[3 * nu:4 * nu]
        send, recv, lsem = refs[4 * nu:]
        x, y, c, chips = _place()
        s = 2 * x + y

        def push(u, j, shard, to):
            return pltpu.make_async_remote_copy(
                src_ref=sb[u].at[:, shard], dst_ref=got[u].at[j], send_sem=send.at[3 * u + j],
                recv_sem=recv.at[3 * u + j], device_id=to, device_id_type=MESH)

        local = [pltpu.make_async_copy(sf[u].at[:, s], own[u], lsem.at[u]) for u in range(nu)]
        for u in range(nu):
            local[u].start()
            for j, chip in enumerate(chips):
                push(u, j, 2 * chip[0] + chip[1], (*chip, c)).start()
        for u in range(nu):
            for j, chip in enumerate(chips):
                push(u, j, 2 * chip[0] + chip[1], (*chip, c)).wait()
            local[u].wait()

    dma = pltpu.SemaphoreType.DMA
    own_shapes = [(a.shape[0], a.shape[2], a.shape[3]) for a in sums]
    outs = pl.pallas_call(
        body, in_specs=[ANY] * (2 * nu), out_specs=[ANY] * (2 * nu),
        out_shape=[_sds(sh, F32) for sh in own_shapes] + [_sds((3,) + sh, BF16) for sh in own_shapes],
        scratch_shapes=[dma((3 * nu,)), dma((3 * nu,)), dma((nu,))],
        compiler_params=pltpu.CompilerParams(has_side_effects=True), name="chip_exchange",
    )(*sums, *sums_bf16)
    return outs[:nu], outs[nu:]


def _final_exchange(halves, out_shapes, targets):
    nu = len(halves)
    no = len(out_shapes)
    ncp = sum(len(t) for t in targets)

    def body(*refs):
        hv = refs[:nu]
        out = refs[nu:nu + no]
        sbuf = refs[nu + no:2 * nu + no]
        rbuf = refs[2 * nu + no:3 * nu + no]
        send, recv, lsem, osem, csem = refs[3 * nu + no:]
        x, y, c, _ = _place()
        stage = [pltpu.make_async_copy(hv[u], sbuf[u], lsem.at[u]) for u in range(nu)]
        push = [pltpu.make_async_remote_copy(
            src_ref=sbuf[u], dst_ref=rbuf[u], send_sem=send.at[u], recv_sem=recv.at[u],
            device_id=(x, y, 1 - c), device_id_type=MESH) for u in range(nu)]
        mine, theirs = [], []
        k = 0
        for u in range(nu):
            rh = hv[u].shape[1]
            for (p, oi, li) in targets[u]:
                mine.append(pltpu.make_async_copy(
                    hv[u].at[p], out[oi].at[li, pl.ds(c * rh, rh), :], csem.at[k]))
                theirs.append((u, pltpu.make_async_copy(
                    rbuf[u].at[p], out[oi].at[li, pl.ds((1 - c) * rh, rh), :], osem.at[k])))
                k += 1
        for cp in stage + mine:
            cp.start()
        for u in range(nu):
            stage[u].wait()
            push[u].start()
        for u in range(nu):
            push[u].wait_recv()
            for (v, cp) in theirs:
                if v == u:
                    cp.start()
        for (_, cp) in theirs:
            cp.wait()
        for cp in mine:
            cp.wait()
        for u in range(nu):
            push[u].wait_send()

    dma = pltpu.SemaphoreType.DMA
    bufs = [pltpu.VMEM(h.shape, F32) for h in halves]
    return pl.pallas_call(
        body, in_specs=[ANY] * nu, out_specs=[ANY] * no,
        out_shape=[_sds(sh, F32) for sh in out_shapes],
        scratch_shapes=bufs + bufs + [dma((nu,)), dma((nu,)), dma((nu,)), dma((ncp,)), dma((ncp,))],
        compiler_params=pltpu.CompilerParams(has_side_effects=True, vmem_limit_bytes=56 << 20),
        name="final_exchange",
    )(*halves)


def _small_allreduce(pack):
    rows = pack.shape[0]

    def body(p_ref, o_ref, buf, send, recv):
        x, y, c, _ = _place()
        me = 4 * x + 2 * y + c
        buf[me] = p_ref[...]
        k = 0
        copies = []
        for dx in range(2):
            for dy in range(2):
                for dc in range(2):
                    if dx == 0 and dy == 0 and dc == 0:
                        continue
                    to = (jnp.where(dx, 1 - x, x), jnp.where(dy, 1 - y, y), jnp.where(dc, 1 - c, c))
                    src_slot = 4 * to[0] + 2 * to[1] + to[2]
                    copies.append((pltpu.make_async_remote_copy(
                        src_ref=p_ref, dst_ref=buf.at[me], send_sem=send.at[k], recv_sem=recv.at[k],
                        device_id=to, device_id_type=MESH), src_slot, k))
                    k += 1
        for cp, _, _ in copies:
            cp.start()
        for cp, src_slot, k in copies:
            pltpu.make_async_remote_copy(
                src_ref=p_ref, dst_ref=buf.at[src_slot], send_sem=send.at[k], recv_sem=recv.at[k],
                device_id=(x, y, c), device_id_type=MESH).wait()
        acc = buf[0]
        for d in range(1, 8):
            acc = acc + buf[d]
        o_ref[...] = acc

    dma = pltpu.SemaphoreType.DMA
    vm = BS(memory_space=pltpu.VMEM)
    return pl.pallas_call(
        body, in_specs=[vm], out_specs=vm, out_shape=_sds((rows, D), F32),
        scratch_shapes=[pltpu.VMEM((8, rows, D), F32), dma((7,)), dma((7,))],
        compiler_params=pltpu.CompilerParams(has_side_effects=True, vmem_limit_bytes=32 << 20),
        name="small_allreduce",
    )(pack)


def _in_proj(h, w, bias, name):
    n = w.shape[1]
    tn = 640 if n == NA else 896
    ep = None
    extras, especs = (), ()
    if bias is not None:
        def ep(acc, ex, o_ref):
            o_ref[...] = acc + ex[0][...]
        extras = (bias,)
        especs = (BS((1, tn), lambda i, j, k: (0, j)),)
    return _mm("nn", h, w, grid=(T // TR, n // tn, 1),
               a_spec=BS((TR, D), lambda i, j, k: (i, 0)), b_spec=BS((D, tn), lambda i, j, k: (0, j)),
               out_shape=_sds((T, n), F32), out_spec=BS((TR, tn), lambda i, j, k: (i, j)),
               acc_shape=(TR, tn), extras=extras, extra_specs=especs, epilogue=ep, name=name)


def _add_res(acc, ex, o_ref):
    o_ref[...] = acc + ex[0][...]


def _local_step(x, mem, target, w, p):
    row = lambda i, j, k: (i, 0)
    saved = []
    bias = _bias_expand(p["rel_u"])
    for l in range(2):
        type_a = l == 0
        h = _rms_fwd(x, p["norm1_g"][l:l + 1], f"rms1_{l}")
        memn = _rms_fwd(mem, p["mem_norm_g"][l:l + 1], f"rmsmem_{l}")
        kv = _mm("nn", memn, w["kv"], grid=(1, 1, 1),
                 a_spec=BS((NB * MEMT, D), lambda i, j, k: (0, 0)),
                 b_spec=BS((None, D, 2 * MEMW), lambda i, j, k: (l, 0, 0)),
                 out_shape=_sds((NB * MEMT, 2 * MEMW), F32),
                 out_spec=BS((NB * MEMT, 2 * MEMW), lambda i, j, k: (0, 0)),
                 acc_shape=(8, 128), name=f"memkv_{l}")
        if type_a:
            z = _in_proj(h, w["a"], None, "inproj_a")
            cat = _attn_fwd(z, bias, p["a_q_g2"], p["a_k_g2"])
            qcol = NA // MEMW - 1
        else:
            z = _in_proj(h, w["b"], p["b_b_in"], "inproj_b")
            cat = _conv_fwd(z, p["conv_w"], p["conv_b"], p["ln_g"], p["ln_b"])
            qcol = NBW // MEMW - 1
        cat = _memattn_fwd(z, kv, cat, p["mq_g4"][l:l + 1], p["mk_g4"][l:l + 1], qcol, f"memattn_fwd_{l}")
        x1 = _mm("nn", cat, w["wo"], grid=(T // TR, 1, 1), a_spec=BS((TR, D), row),
                 b_spec=BS((None, D, D), lambda i, j, k: (l, 0, 0)),
                 out_shape=_sds((T, D), F32), out_spec=BS((TR, D), row), acc_shape=(8, 128),
                 extras=(x,), extra_specs=(BS((TR, D), row),), epilogue=_add_res, name=f"outproj_{l}")
        h2 = _rms_fwd(x1, p["norm2_g"][l:l + 1], f"rms2_{l}")
        g, u, act = _ffn_up(h2, w["gu"], l)
        x2 = _mm("nn", act, w["wd"], grid=(T // TR, 1, NSH),
                 a_spec=BS((None, TR, FS), lambda i, j, k: (k, i, 0)),
                 b_spec=BS((None, None, FS, D), lambda i, j, k: (l, k, 0, 0)),
                 out_shape=_sds((T, D), F32), out_spec=BS((TR, D), row), acc_shape=(TR, D),
                 extras=(x1,), extra_specs=(BS((TR, D), row),), epilogue=_add_res, name=f"ffn_down_{l}")
        saved.append(dict(x=x, h=h, memn=memn, kv=kv, z=z, cat=cat, x1=x1, h2=h2, g=g, u=u, act=act,
                          qcol=qcol))
        x = x2

    loss, dx = _loss_head(x, target)

    big = dict(a=None, b=None, kv=None, wo=None, gu=None, wd=None)
    small = {}
    tk = TR
    nkt = T // tk
    for l in (1, 0):
        sv = saved[l]
        dgu = _ffn_down_bwd(dx, w["wd"], sv["g"], sv["u"], l)
        dgu8 = dgu.reshape(2 * NSH, T, FS)
        big["wd"] = _mm("tn", sv["act"], dx, grid=(NSH, 1, nkt),
                        a_spec=BS((None, tk, FS), lambda i, j, k: (i, k, 0)),
                        b_spec=BS((tk, D), lambda i, j, k: (k, 0)),
                        out_shape=_sds((2, NSH, FS, D), F32),
                        out_spec=BS((None, None, FS, D), lambda i, j, k: (l, i, 0, 0)),
                        acc_shape=(FS, D), into=big["wd"], name=f"dw_down_{l}")
        dh2 = _mm("nt", dgu8, w["gu"], grid=(T // TR, 1, 2 * NSH),
                  a_spec=BS((None, TR, FS), lambda i, j, k: (k, i, 0)),
                  b_spec=BS((None, None, D, FS), lambda i, j, k: (l, k, 0, 0)),
                  out_shape=_sds((T, D), F32), out_spec=BS((TR, D), row), acc_shape=(TR, D),
                  name=f"dh2_{l}")
        big["gu"] = _mm("tn", sv["h2"], dgu8, grid=(2 * NSH, 1, nkt),
                        a_spec=BS((tk, D), lambda i, j, k: (k, 0)),
                        b_spec=BS((None, tk, FS), lambda i, j, k: (i, k, 0)),
                        out_shape=_sds((2, 2 * NSH, D, FS), F32),
                        out_spec=BS((None, None, D, FS), lambda i, j, k: (l, i, 0, 0)),
                        acc_shape=(D, FS), into=big["gu"], name=f"dw_gu_{l}")
        dx1, small[f"norm2_g{l}"] = _rms_bwd(dh2, sv["x1"], p["norm2_g"][l:l + 1], dx, f"rms2_bwd_{l}")
        dcat = _mm("nt", dx1, w["wo"], grid=(T // TR, 1, 1), a_spec=BS((TR, D), row),
                   b_spec=BS((None, D, D), lambda i, j, k: (l, 0, 0)),
                   out_shape=_sds((T, D), F32), out_spec=BS((TR, D), row), acc_shape=(8, 128),
                   name=f"dcat_{l}")
        big["wo"] = _mm("tn", sv["cat"], dx1, grid=(1, 1, nkt),
                        a_spec=BS((tk, D), lambda i, j, k: (k, 0)), b_spec=BS((tk, D), lambda i, j, k: (k, 0)),
                        out_shape=_sds((2, D, D), F32), out_spec=BS((None, D, D), lambda i, j, k: (l, 0, 0)),
                        acc_shape=(D, D), into=big["wo"], name=f"dw_out_{l}")
        dqm, dkv, small[f"mq_g{l}"], small[f"mk_g{l}"] = _memattn_bwd(
            sv["z"], sv["kv"], dcat, p["mq_g4"][l:l + 1], p["mk_g4"][l:l + 1], sv["qcol"], f"memattn_bwd_{l}")
        if l == 0:
            dq, dk, dv, dbias, small["a_q_g"], small["a_k_g"] = _attn_bwd(
                sv["z"], dcat, bias, p["a_q_g2"], p["a_k_g2"])
            small["rel_u"] = _bias_reduce(dbias)
            dz = jnp.concatenate([dq, dk, dv, dqm], axis=1)
            w_in, key, n, tn = w["a"], "a", NA, 640
        else:
            du, small["conv_w"], small["conv_b"], small["ln_g"], small["ln_b"], dbin_u = _conv_bwd(
                sv["z"], dcat, p["conv_w"], p["conv_b"], p["ln_g"], p["ln_b"])
            dz = jnp.concatenate([du, dqm], axis=1)
            small["b_in_u"] = dbin_u
            w_in, key, n, tn = w["b"], "b", NBW, 896
        dh = _mm("nt", dz, w_in, grid=(T // TR, 1, n // tn),
                 a_spec=BS((TR, tn), lambda i, j, k: (i, k)), b_spec=BS((D, tn), lambda i, j, k: (0, k)),
                 out_shape=_sds((T, D), F32), out_spec=BS((TR, D), row), acc_shape=(TR, D),
                 name=f"dh_{l}")
        big[key] = _mm("tn", sv["h"], dz, grid=(1, n // tn, nkt),
                       a_spec=BS((tk, D), lambda i, j, k: (k, 0)), b_spec=BS((tk, tn), lambda i, j, k: (k, j)),
                       out_shape=_sds((D, n), F32), out_spec=BS((D, tn), lambda i, j, k: (0, j)),
                       acc_shape=(D, tn), name=f"dw_in_{l}")
        if l == 1:
            small["b_in_qm"] = _colsum(dqm, "colsum_dqm")
        dx, small[f"norm1_g{l}"] = _rms_bwd(dh, sv["x"], p["norm1_g"][l:l + 1], dx1, f"rms1_bwd_{l}")
        big["kv"] = _mm("tn", sv["memn"], dkv, grid=(1, 1, 1),
                        a_spec=BS((NB * MEMT, D), lambda i, j, k: (0, 0)),
                        b_spec=BS((NB * MEMT, 2 * MEMW), lambda i, j, k: (0, 0)),
                        out_shape=_sds((2, D, 2 * MEMW), F32),
                        out_spec=BS((None, D, 2 * MEMW), lambda i, j, k: (l, 0, 0)),
                        acc_shape=(8, 128), into=big["kv"], name=f"dw_kv_{l}")
        dmemn = _mm("nt", dkv, w["kv"], grid=(1, 1, 1),
                    a_spec=BS((NB * MEMT, 2 * MEMW), lambda i, j, k: (0, 0)),
                    b_spec=BS((None, D, 2 * MEMW), lambda i, j, k: (l, 0, 0)),
                    out_shape=_sds((NB * MEMT, D), F32), out_spec=BS((NB * MEMT, D), lambda i, j, k: (0, 0)),
                    acc_shape=(8, 128), name=f"dmemn_{l}")
        _, small[f"mem_norm_g{l}"] = _rms_bwd(dmemn, mem, p["mem_norm_g"][l:l + 1], None, f"rmsmem_bwd_{l}")
    return loss, dx, big, small


def _colsum(a, name):
    rows, cols = a.shape

    def body(a_ref, o_ref):
        @pl.when(pl.program_id(0) == 0)
        def _():
            o_ref[...] = jnp.zeros_like(o_ref)

        o_ref[...] += jnp.sum(a_ref[...].astype(F32), axis=0, keepdims=True)

    return pl.pallas_call(
        body, grid=(rows // TR,), in_specs=[BS((TR, cols), lambda i: (i, 0))],
        out_specs=BS((1, cols), lambda i: (0, 0)), out_shape=_sds((1, cols), F32),
        compiler_params=_cp(("arbitrary",)), name=name,
    )(a)


_PACK_ROWS = 64


def _pad_to(a, rows, cols=D):
    return jnp.pad(a, ((0, rows - a.shape[0]), (0, cols - a.shape[1])))


def _pack_small(sm):
    parts = [
        jnp.concatenate([sm["norm1_g0"], sm["norm1_g1"]], 0),
        jnp.concatenate([sm["mem_norm_g0"], sm["mem_norm_g1"]], 0),
        jnp.concatenate([sm["norm2_g0"], sm["norm2_g1"]], 0),
        _pad_to(sm["a_q_g"], 1), _pad_to(sm["a_k_g"], 1),
        _pad_to(jnp.concatenate([sm["mq_g0"], sm["mq_g1"]], 0), 2),
        _pad_to(jnp.concatenate([sm["mk_g0"], sm["mk_g1"]], 0), 2),
        _pad_to(sm["conv_b"], 1), _pad_to(sm["ln_g"], 1), _pad_to(sm["ln_b"], 1),
        _pad_to(sm["b_in_u"][:, :D], 1), _pad_to(sm["b_in_u"][:, D:], 1),
        _pad_to(sm["b_in_qm"], 1),
        _pad_to(sm["conv_w"][:CONVW], CONVW),
        sm["rel_u"].reshape(12, D),
    ]
    pack = jnp.concatenate(parts, 0)
    return jnp.pad(pack, ((0, _PACK_ROWS - pack.shape[0]), (0, 0)))


def _rel_table_to_u(rel_bias):
    flat = jnp.concatenate([jnp.broadcast_to(rel_bias[:, 191:192], (12, 447)), rel_bias[:, ::-1]], axis=1)
    return jnp.pad(flat, ((0, 0), (192, 1024 - 192 - 639))).reshape(12, 1, 1024)


def _u_to_rel_table(du):
    flat = du[:, 192:192 + 639]
    g = flat[:, 447:][:, ::-1]
    return g, flat[:, :447]


def kernel(x, mem, norm1_g, mem_norm_g, a_w_in, a_q_g, a_k_g, a_rel_bias, b_w_in, b_b_in, b_conv_w, b_conv_b, b_ln_g, b_ln_b, mq_g, mk_g, w_mem_kv, w_out, norm2_g, w_gate, w_up, w_down, loss_target, m_norm1_g, m_mem_norm_g, m_a_w_in, m_a_q_g, m_a_k_g, m_a_rel_bias, m_b_w_in, m_b_b_in, m_b_conv_w, m_b_conv_b, m_b_ln_g, m_b_ln_b, m_mq_g, m_mk_g, m_w_mem_kv, m_w_out, m_norm2_g, m_w_gate, m_w_up, m_w_down, v_norm1_g, v_mem_norm_g, v_a_w_in, v_a_q_g, v_a_k_g, v_a_rel_bias, v_b_w_in, v_b_b_in, v_b_conv_w, v_b_conv_b, v_b_ln_g, v_b_ln_b, v_mq_g, v_mk_g, v_w_mem_kv, v_w_out, v_norm2_g, v_w_gate, v_w_up, v_w_down):
    sx = 2 * lax.axis_index("x") + lax.axis_index("y")

    src_a = _cast_bf16([a_w_in[0]], "cast_a").reshape(1, 2, D // 2, NA // NSH)
    src_b = _cast_bf16([b_w_in[0]], "cast_b").reshape(1, 2, D // 2, NBW // NSH)
    src_kv = _cast_bf16([w_mem_kv[0], w_mem_kv[1]], "cast_kv").reshape(2, 2, 128, 2 * MEMW)
    src_wo = _cast_bf16([w_out[0], w_out[1]], "cast_wo").reshape(2, 2, 128, D)
    src_g = _cast_bf16([w_gate[0], w_gate[1]], "cast_gate").reshape(2, 2, D // 2, FS)
    src_u = _cast_bf16([w_up[0], w_up[1]], "cast_up").reshape(2, 2, D // 2, FS)
    src_wd = _cast_bf16([w_down[0], w_down[1]], "cast_wd").reshape(2, 2, FS // 2, D)
    small_src = jnp.concatenate([
        jnp.pad(b_b_in, ((0, 0), (0, 512 - 448))),
        jnp.pad(b_conv_w[0], ((0, 0), (0, 512 - 192))),
        jnp.pad(jnp.concatenate([b_conv_b, b_ln_g, b_ln_b], 0), ((0, 0), (0, 512 - 192))),
        jnp.zeros((5, 512), F32)], 0)
    dst_shapes = [(1, NSH, 2, D // 2, NA // NSH), (1, NSH, 2, D // 2, NBW // NSH),
                  (2, NSH, 2, 128, 2 * MEMW), (2, NSH, 2, 128, D),
                  (2, 2, NSH, 2, D // 2, FS), (2, NSH, 2, FS // 2, D)]
    views = [lambda d: d[0], lambda d: d[1], lambda d: d[2], lambda d: d[3],
             lambda d: d[4].at[:, 0], lambda d: d[4].at[:, 1], lambda d: d[5]]
    (ga, gb, gkv, gwo, ggu, gwd), small_all = _gather_weights(
        [src_a, src_b, src_kv, src_wo, src_g, src_u, src_wd], dst_shapes, views, small_src)
    w = dict(
        a=ga.reshape(NSH, D, NA // NSH).transpose(1, 0, 2).reshape(D, NA),
        b=gb.reshape(NSH, D, NBW // NSH).transpose(1, 0, 2).reshape(D, NBW),
        kv=gkv.reshape(2, D, 2 * MEMW), wo=gwo.reshape(2, D, D),
        gu=ggu.reshape(2, 2 * NSH, D, FS), wd=gwd.reshape(2, NSH, FS, D))

    conv_w_full = small_all[:, 1:1 + CONVW, :192].transpose(1, 0, 2).reshape(CONVW, TOK)
    vec3 = small_all[:, 32:35, :192].transpose(1, 0, 2).reshape(3, TOK)
    p = dict(
        norm1_g=norm1_g, mem_norm_g=mem_norm_g, norm2_g=norm2_g,
        a_q_g2=jnp.tile(a_q_g, (1, 2)), a_k_g2=jnp.tile(a_k_g, (1, 2)),
        mq_g4=jnp.tile(mq_g, (1, 4)), mk_g4=jnp.tile(mk_g, (1, 4)),
        rel_u=_rel_table_to_u(a_rel_bias[0]),
        b_b_in=small_all[:, 0, :448].reshape(1, NBW),
        conv_w=jnp.pad(conv_w_full, ((0, 1), (0, 0))),
        conv_b=vec3[0:1], ln_g=vec3[1:2], ln_b=vec3[2:3])

    loss, grad_x, big, small = _local_step(
        x.reshape(T, D), mem.reshape(NB * MEMT, D), loss_target.reshape(T, D), w, p)
    loss = lax.psum(loss[0, 0], ("x", "y", "c"))

    units = [
        big["a"].reshape(D, NSH, NA // NSH).transpose(1, 0, 2).reshape(1, NSH, 2, D // 2, NA // NSH),
        big["b"].reshape(D, NSH, NBW // NSH).transpose(1, 0, 2).reshape(1, NSH, 2, D // 2, NBW // NSH),
        big["kv"].reshape(2, NSH, 2, 128, 2 * MEMW),
        big["wo"].reshape(2, NSH, 2, 128, D),
        big["gu"].reshape(4, NSH, 2, D // 2, FS),
        big["wd"].reshape(2, NSH, 2, FS // 2, D),
    ]
    sums, sums_b = [], []
    for u in range(len(units)):
        sf, sb = _pair_reduce(units[u], f"pair_reduce_{u}")
        sums.append(sf)
        sums_b.append(sb)
    own, parts = _chip_exchange(sums, sums_b)
    halves = [_quad_sum(own[u], parts[u], f"quad_sum_{u}") for u in range(len(units))]
    out_shapes = [(1, D, NA // NSH), (1, D, NBW // NSH), (2, 2 * 128, 2 * MEMW), (2, 2 * 128, D),
                  (2, D, FS), (2, D, FS), (2, FS, D)]
    targets = [[(0, 0, 0)], [(0, 1, 0)], [(0, 2, 0), (1, 2, 1)], [(0, 3, 0), (1, 3, 1)],
               [(0, 4, 0), (1, 5, 0), (2, 4, 1), (3, 5, 1)], [(0, 6, 0), (1, 6, 1)]]
    g_a, g_b, g_kv, g_wo, g_gate, g_up, g_wd = _final_exchange(halves, out_shapes, targets)

    tot = _small_allreduce(_pack_small(small))
    g_rel, clip_part = _u_to_rel_table(tot[49:61])
    g_rel = jnp.concatenate([g_rel[:, :191], g_rel[:, 191:] + _rowsum(clip_part)], axis=1)
    b_in_full = jnp.concatenate([tot[15:16], tot[16:17, :512], tot[17:18, :MEMW]], axis=1)
    g_small = dict(
        norm1_g=tot[0:2], mem_norm_g=tot[2:4], norm2_g=tot[4:6],
        a_q_g=tot[6:7, :HD], a_k_g=tot[7:8, :HD], a_rel_bias=g_rel[None],
        b_b_in=lax.dynamic_slice(b_in_full, (0, sx * 448), (1, 448)),
        b_conv_w=lax.dynamic_slice(tot[18:49, :TOK], (0, sx * 192), (CONVW, 192))[None],
        b_conv_b=lax.dynamic_slice(tot[12:13, :TOK], (0, sx * 192), (1, 192)),
        b_ln_g=lax.dynamic_slice(tot[13:14, :TOK], (0, sx * 192), (1, 192)),
        b_ln_b=lax.dynamic_slice(tot[14:15, :TOK], (0, sx * 192), (1, 192)),
        mq_g=tot[8:10, :HD], mk_g=tot[10:12, :HD])

    names = ["norm1_g", "mem_norm_g", "a_w_in", "a_q_g", "a_k_g", "a_rel_bias", "b_w_in", "b_b_in",
             "b_conv_w", "b_conv_b", "b_ln_g", "b_ln_b", "mq_g", "mk_g", "w_mem_kv", "w_out",
             "norm2_g", "w_gate", "w_up", "w_down"]
    weights = dict(zip(names, [norm1_g, mem_norm_g, a_w_in, a_q_g, a_k_g, a_rel_bias, b_w_in, b_b_in,
                               b_conv_w, b_conv_b, b_ln_g, b_ln_b, mq_g, mk_g, w_mem_kv, w_out,
                               norm2_g, w_gate, w_up, w_down]))
    ms = dict(zip(names, [m_norm1_g, m_mem_norm_g, m_a_w_in, m_a_q_g, m_a_k_g, m_a_rel_bias, m_b_w_in,
                          m_b_b_in, m_b_conv_w, m_b_conv_b, m_b_ln_g, m_b_ln_b, m_mq_g, m_mk_g,
                          m_w_mem_kv, m_w_out, m_norm2_g, m_w_gate, m_w_up, m_w_down]))
    vs = dict(zip(names, [v_norm1_g, v_mem_norm_g, v_a_w_in, v_a_q_g, v_a_k_g, v_a_rel_bias, v_b_w_in,
                          v_b_b_in, v_b_conv_w, v_b_conv_b, v_b_ln_g, v_b_ln_b, v_mq_g, v_mk_g,
                          v_w_mem_kv, v_w_out, v_norm2_g, v_w_gate, v_w_up, v_w_down]))
    grads = dict(g_small)
    grads.update(a_w_in=g_a, b_w_in=g_b, w_mem_kv=g_kv, w_out=g_wo, w_gate=g_gate, w_up=g_up, w_down=g_wd)
    big_names = ["a_w_in", "b_w_in", "w_mem_kv", "w_out", "w_gate", "w_up", "w_down"]
    small_names = [n for n in names if n not in big_names]
    delta, new_m, new_v = {}, {}, {}
    for n in big_names:
        delta[n], new_m[n], new_v[n] = _adamw_big(weights[n], grads[n], ms[n], vs[n], f"adamw_{n}")
    as2d = lambda a: a.reshape(-1, a.shape[-1])
    d_s, m_s, v_s = _adamw_small([as2d(weights[n]) for n in small_names], [as2d(grads[n]) for n in small_names],
                                 [as2d(ms[n]) for n in small_names], [as2d(vs[n]) for n in small_names])
    for i, n in enumerate(small_names):
        delta[n] = d_s[i].reshape(weights[n].shape)
        new_m[n] = m_s[i].reshape(weights[n].shape)
        new_v[n] = v_s[i].reshape(weights[n].shape)

    return (loss, grad_x.reshape(NB, SEQ, D), *[grads[n] for n in names], *[delta[n] for n in names],
            *[new_m[n] for n in names], *[new_v[n] for n in names])


def _rowsum(a):
    def body(a_ref, o_ref):
        o_ref[...] = jnp.sum(a_ref[...], axis=1, keepdims=True)

    vm = BS(memory_space=pltpu.VMEM)
    return pl.pallas_call(body, in_specs=[vm], out_specs=vm, out_shape=_sds((a.shape[0], 1), F32),
                          compiler_params=_cp(), name="rowsum")(a)
```

```python
import functools

import jax
import jax.numpy as jnp
from jax import lax
from jax.experimental import pallas as pl
from jax.experimental.pallas import tpu as pltpu

F32 = jnp.float32
BF16 = jnp.bfloat16
BS = pl.BlockSpec
ANY = pl.BlockSpec(memory_space=pl.ANY)
MESH = pl.DeviceIdType.MESH

D = 1024
SEQ = 2048
NB = 2
T = NB * SEQ
MEMT = 256
HD = 64
TOK = 768
MEMW = 256
NA = 3 * TOK + MEMW
NBW = 2 * TOK + MEMW
FF = 2816
NSH = 4
FS = FF // NSH
CONVW = 31
EPS = 1e-6
NEG = -1e30
SCALE = HD ** -0.5
QB = 256
KWIN = 768
KPAD = 512
TR = 512

ADAM_LR = 0.001
ADAM_B1 = 0.9
ADAM_B2 = 0.999
ADAM_EPS = 1e-08
ADAM_WD = 0.01
ADAM_STEP = 10

_DIMS = {
    "nn": (((1,), (0,)), ((), ())),
    "nt": (((1,), (1,)), ((), ())),
    "tn": (((0,), (0,)), ((), ())),
}


def _cp(sem=None, vmem_mb=48):
    return pltpu.CompilerParams(dimension_semantics=sem, vmem_limit_bytes=vmem_mb << 20)


def _sds(shape, dtype):
    return jax.ShapeDtypeStruct(tuple(shape), dtype)


def _mm(mode, a, b, *, grid, a_spec, b_spec, out_shape, out_spec, acc_shape, name,
        extras=(), extra_specs=(), epilogue=None, into=None):
    n_ex = len(extras)
    nk = grid[2]
    dims = _DIMS[mode]

    def body(a_ref, b_ref, *rest):
        ex = rest[:n_ex]
        o_ref = rest[n_ex + (1 if into is not None else 0)]
        acc = rest[-1]
        k = pl.program_id(2)
        prod = lax.dot_general(a_ref[...].astype(BF16), b_ref[...].astype(BF16), dims,
                               preferred_element_type=F32)

        def finish(val):
            if epilogue is None:
                o_ref[...] = val.astype(o_ref.dtype)
            else:
                epilogue(val, ex, o_ref)

        if nk == 1:
            finish(prod)
        else:
            @pl.when(k == 0)
            def _():
                acc[...] = prod

            @pl.when(k > 0)
            def _():
                acc[...] += prod

            @pl.when(k == nk - 1)
            def _():
                finish(acc[...])

    ins = [a, b, *extras]
    in_specs = [a_spec, b_spec, *extra_specs]
    aliases = {}
    if into is not None:
        ins.append(into)
        in_specs.append(ANY)
        aliases = {len(ins) - 1: 0}
    return pl.pallas_call(
        body, grid=grid, in_specs=in_specs, out_specs=out_spec, out_shape=out_shape,
        scratch_shapes=[pltpu.VMEM(acc_shape if nk > 1 else (8, 128), F32)],
        input_output_aliases=aliases,
        compiler_params=_cp(("parallel", "parallel", "arbitrary")), name=name,
    )(*ins)


def _rms_fwd(x, g, name):
    rows = x.shape[0]

    def body(x_ref, g_ref, o_ref):
        xv = x_ref[...]
        r = lax.rsqrt(jnp.mean(xv * xv, axis=-1, keepdims=True) + EPS)
        o_ref[...] = (xv * r * g_ref[...]).astype(BF16)

    return pl.pallas_call(
        body, grid=(rows // TR,),
        in_specs=[BS((TR, D), lambda i: (i, 0)), BS((1, D), lambda i: (0, 0))],
        out_specs=BS((TR, D), lambda i: (i, 0)), out_shape=_sds((rows, D), BF16),
        compiler_params=_cp(("arbitrary",)), name=name,
    )(x, g)


def _rms_bwd(dh, x, g, dres, name):
    rows = x.shape[0]
    has_res = dres is not None

    def body(*refs):
        if has_res:
            dh_ref, x_ref, g_ref, r_ref, dx_ref, dg_ref = refs
        else:
            dh_ref, x_ref, g_ref, dx_ref, dg_ref = refs
        xv = x_ref[...]
        dhv = dh_ref[...]
        r = lax.rsqrt(jnp.mean(xv * xv, axis=-1, keepdims=True) + EPS)
        xh = xv * r
        gy = dhv * g_ref[...]
        dx = r * (gy - xh * jnp.mean(gy * xh, axis=-1, keepdims=True))
        if has_res:
            dx = dx + r_ref[...]
        dx_ref[...] = dx

        @pl.when(pl.program_id(0) == 0)
        def _():
            dg_ref[...] = jnp.zeros_like(dg_ref)

        dg_ref[...] += jnp.sum(dhv * xh, axis=0, keepdims=True)

    row = BS((TR, D), lambda i: (i, 0))
    vec = BS((1, D), lambda i: (0, 0))
    ins = [dh, x, g] + ([dres] if has_res else [])
    return pl.pallas_call(
        body, grid=(rows // TR,),
        in_specs=[row, row, vec] + ([row] if has_res else []),
        out_specs=[row, vec], out_shape=[_sds((rows, D), F32), _sds((1, D), F32)],
        compiler_params=_cp(("arbitrary",)), name=name,
    )(*ins)


def _group_masks(width):
    lane = lax.broadcasted_iota(jnp.int32, (1, width), 1)
    return [(lane >= HD * h) & (lane < HD * (h + 1)) for h in range(width // HD)]


def _group_mean(v, masks):
    out = jnp.zeros_like(v)
    for m in masks:
        s = jnp.sum(jnp.where(m, v, 0.0), axis=-1, keepdims=True) * (1.0 / HD)
        out = jnp.where(m, s, out)
    return out


def _head_norm(zv, g, masks):
    r = lax.rsqrt(_group_mean(zv * zv, masks) + EPS)
    return zv * r * g


def _head_norm_bwd(dy, zv, g, masks):
    r = lax.rsqrt(_group_mean(zv * zv, masks) + EPS)
    zh = zv * r
    gy = dy * g
    dz = r * (gy - zh * _group_mean(gy * zh, masks))
    return dz, jnp.sum(dy * zh, axis=0, keepdims=True)


def _fold_heads(v, width):
    vb = jnp.broadcast_to(v, (8, width))
    out = vb
    for h in range(1, width // HD):
        out = out + pltpu.roll(vb, width - HD * h, axis=1)
    return out[0:1]


def _bias_expand(u):
    def body(u_ref, o_ref):
        x = jnp.broadcast_to(u_ref[...], (QB, 1024))
        rolled = pltpu.roll(x, 1024 - (QB - 1), axis=1, stride=1, stride_axis=0)[:, :KWIN]
        row = lax.broadcasted_iota(jnp.int32, (QB, 1), 0)
        col = lax.broadcasted_iota(jnp.int32, (1, KWIN), 1)
        lo = (row // 64) * 64
        ok = (col >= lo) & (col < lo + 576)
        o_ref[...] = jnp.where(ok, rolled, NEG)

    return pl.pallas_call(
        body, grid=(12,), in_specs=[BS((None, 1, 1024), lambda h: (h, 0, 0))],
        out_specs=BS((None, QB, KWIN), lambda h: (h, 0, 0)), out_shape=_sds((12, QB, KWIN), F32),
        compiler_params=_cp(("arbitrary",)), name="bias_expand",
    )(u)


def _bias_reduce(ds):
    def body(d_ref, o_ref):
        ri = lax.broadcasted_iota(jnp.int32, (QB, QB), 0)
        ci = lax.broadcasted_iota(jnp.int32, (QB, QB), 1)
        flip = (ri + ci == QB - 1).astype(F32)
        drev = jnp.dot(flip, d_ref[...], precision=lax.Precision.HIGHEST, preferred_element_type=F32)
        x = jnp.concatenate([drev, jnp.zeros((QB, 1024 - KWIN), F32)], axis=1)
        rolled = pltpu.roll(x, 0, axis=1, stride=1, stride_axis=0)
        o_ref[...] = jnp.sum(rolled, axis=0, keepdims=True)

    return pl.pallas_call(
        body, grid=(12,), in_specs=[BS((None, QB, KWIN), lambda h: (h, 0, 0))],
        out_specs=BS((None, 1, 1024), lambda h: (h, 0, 0)), out_shape=_sds((12, 1, 1024), F32),
        compiler_params=_cp(("arbitrary",)), name="bias_reduce",
    )(ds)


def _attn_softmax(qh, kw, bias, startadd):
    s = lax.dot_general(qh, kw, _DIMS["nt"], preferred_element_type=F32) * SCALE + bias + startadd
    m = jnp.max(s, axis=-1, keepdims=True)
    p = jnp.exp(s - m)
    return p / jnp.sum(p, axis=-1, keepdims=True)


def _attn_fwd(z, bias, gq2, gk2):
    def body(q_ref, k_ref, v_ref, b_ref, gq_ref, gk_ref, o_ref, qn_s, kn_s, v_s):
        masks = _group_masks(128)
        qn_s[...] = _head_norm(q_ref[...], gq_ref[...], masks).astype(BF16)
        kn_s[0:KPAD, :] = jnp.zeros((KPAD, 128), BF16)
        kn_s[KPAD:, :] = _head_norm(k_ref[...], gk_ref[...], masks).astype(BF16)
        v_s[0:KPAD, :] = jnp.zeros((KPAD, 128), BF16)
        v_s[KPAD:, :] = v_ref[...].astype(BF16)
        col = lax.broadcasted_iota(jnp.int32, (1, KWIN), 1)

        def blk(i, carry):
            r0 = pl.multiple_of(i * QB, QB)
            qb = qn_s[pl.ds(r0, QB), :]
            kw = kn_s[pl.ds(r0, KWIN), :]
            vw = v_s[pl.ds(r0, KWIN), :]
            startadd = jnp.where(col + r0 < KPAD, NEG, 0.0)
            o = jnp.zeros((QB, 128), F32)
            for h in range(2):
                qh = jnp.where(masks[h], qb, jnp.zeros_like(qb))
                vh = jnp.where(masks[h], vw, jnp.zeros_like(vw))
                p = _attn_softmax(qh, kw, b_ref[h], startadd).astype(BF16)
                o = o + jnp.dot(p, vh, preferred_element_type=F32)
            o_ref[pl.ds(r0, QB), :] = o.astype(BF16)
            return carry

        lax.fori_loop(0, SEQ // QB, blk, 0)

    vec = BS((1, 128), lambda b, hp: (0, 0))
    return pl.pallas_call(
        body, grid=(NB, 6),
        in_specs=[BS((SEQ, 128), lambda b, hp: (b, hp)),
                  BS((SEQ, 128), lambda b, hp: (b, 6 + hp)),
                  BS((SEQ, 128), lambda b, hp: (b, 12 + hp)),
                  BS((2, QB, KWIN), lambda b, hp: (hp, 0, 0)), vec, vec],
        out_specs=BS((SEQ, 128), lambda b, hp: (b, hp)),
        out_shape=_sds((T, D), BF16),
        scratch_shapes=[pltpu.VMEM((SEQ, 128), BF16), pltpu.VMEM((SEQ + KPAD, 128), BF16),
                        pltpu.VMEM((SEQ + KPAD, 128), BF16)],
        compiler_params=_cp(("arbitrary", "arbitrary")), name="attn_fwd",
    )(z, z, z, bias, gq2, gk2)


def _attn_bwd(z, dcat, bias, gq2, gk2):
    def body(q_ref, k_ref, v_ref, do_ref, b_ref, gq_ref, gk_ref,
             dq_ref, dk_ref, dv_ref, db_ref, dgq_ref, dgk_ref,
             qn_s, kn_s, v_s, dqn_s, dkn_s, dv_s):
        hp = pl.program_id(0)
        b = pl.program_id(1)
        masks = _group_masks(128)
        qn_s[...] = _head_norm(q_ref[...], gq_ref[...], masks).astype(BF16)
        kn_s[0:KPAD, :] = jnp.zeros((KPAD, 128), BF16)
        kn_s[KPAD:, :] = _head_norm(k_ref[...], gk_ref[...], masks).astype(BF16)
        v_s[0:KPAD, :] = jnp.zeros((KPAD, 128), BF16)
        v_s[KPAD:, :] = v_ref[...].astype(BF16)
        dkn_s[...] = jnp.zeros_like(dkn_s)
        dv_s[...] = jnp.zeros_like(dv_s)

        @pl.when(b == 0)
        def _():
            db_ref[...] = jnp.zeros_like(db_ref)

        @pl.when((b == 0) & (hp == 0))
        def _():
            dgq_ref[...] = jnp.zeros_like(dgq_ref)
            dgk_ref[...] = jnp.zeros_like(dgk_ref)

        col = lax.broadcasted_iota(jnp.int32, (1, KWIN), 1)

        def blk(i, carry):
            r0 = pl.multiple_of(i * QB, QB)
            qb = qn_s[pl.ds(r0, QB), :]
            kw = kn_s[pl.ds(r0, KWIN), :]
            vw = v_s[pl.ds(r0, KWIN), :]
            dob = do_ref[pl.ds(r0, QB), :].astype(BF16)
            startadd = jnp.where(col + r0 < KPAD, NEG, 0.0)
            dqn = jnp.zeros((QB, 128), F32)
            dkw = jnp.zeros((KWIN, 128), F32)
            dvw = jnp.zeros((KWIN, 128), F32)
            for h in range(2):
                qh = jnp.where(masks[h], qb, jnp.zeros_like(qb))
                kh = jnp.where(masks[h], kw, jnp.zeros_like(kw))
                doh = jnp.where(masks[h], dob, jnp.zeros_like(dob))
                p = _attn_softmax(qh, kw, b_ref[h], startadd)
                dvw = dvw + lax.dot_general(p.astype(BF16), doh, _DIMS["tn"],
                                            preferred_element_type=F32)
                dp = lax.dot_general(doh, vw, _DIMS["nt"], preferred_element_type=F32)
                ds = p * (dp - jnp.sum(dp * p, axis=-1, keepdims=True))
                db_ref[h] += ds
                dsb = (ds * SCALE).astype(BF16)
                dqn = dqn + jnp.dot(dsb, kh, preferred_element_type=F32)
                dkw = dkw + lax.dot_general(dsb, qh, _DIMS["tn"], preferred_element_type=F32)
            dqn_s[pl.ds(r0, QB), :] = dqn
            dkn_s[pl.ds(r0, KWIN), :] += dkw
            dv_s[pl.ds(r0, KWIN), :] += dvw
            return carry

        lax.fori_loop(0, SEQ // QB, blk, 0)

        dq, dgq = _head_norm_bwd(dqn_s[...], q_ref[...], gq_ref[...], masks)
        dk, dgk = _head_norm_bwd(dkn_s[KPAD:, :], k_ref[...], gk_ref[...], masks)
        dq_ref[...] = dq.astype(BF16)
        dk_ref[...] = dk.astype(BF16)
        dv_ref[...] = dv_s[KPAD:, :].astype(BF16)
        dgq_ref[...] += _fold_heads(dgq, 128)
        dgk_ref[...] += _fold_heads(dgk, 128)

    vec = BS((1, 128), lambda hp, b: (0, 0))
    row = BS((SEQ, 128), lambda hp, b: (b, hp))
    return pl.pallas_call(
        body, grid=(6, NB),
        in_specs=[row,
                  BS((SEQ, 128), lambda hp, b: (b, 6 + hp)),
                  BS((SEQ, 128), lambda hp, b: (b, 12 + hp)),
                  row,
                  BS((2, QB, KWIN), lambda hp, b: (hp, 0, 0)), vec, vec],
        out_specs=[row, row, row, BS((2, QB, KWIN), lambda hp, b: (hp, 0, 0)), vec, vec],
        out_shape=[_sds((T, TOK), BF16), _sds((T, TOK), BF16), _sds((T, TOK), BF16),
                   _sds((12, QB, KWIN), F32), _sds((1, 128), F32), _sds((1, 128), F32)],
        scratch_shapes=[pltpu.VMEM((SEQ, 128), BF16), pltpu.VMEM((SEQ + KPAD, 128), BF16),
                        pltpu.VMEM((SEQ + KPAD, 128), BF16), pltpu.VMEM((SEQ, 128), F32),
                        pltpu.VMEM((SEQ + KPAD, 128), F32), pltpu.VMEM((SEQ + KPAD, 128), F32)],
        compiler_params=_cp(("arbitrary", "arbitrary")), name="attn_bwd",
    )(z, z, z, dcat, bias, gq2, gk2)


def _mem_softmax(qh, kn):
    s = lax.dot_general(qh, kn, _DIMS["nt"], preferred_element_type=F32) * SCALE
    m = jnp.max(s, axis=-1, keepdims=True)
    p = jnp.exp(s - m)
    return p / jnp.sum(p, axis=-1, keepdims=True)


def _memattn_fwd(z, kv, cat, gq4, gk4, qcol, name):
    def body(q_ref, k_ref, v_ref, gq_ref, gk_ref, cat_ref, o_ref):
        del cat_ref
        masks = _group_masks(MEMW)
        qn = _head_norm(q_ref[...], gq_ref[...], masks).astype(BF16)
        kn = _head_norm(k_ref[...], gk_ref[...], masks).astype(BF16)
        vv = v_ref[...].astype(BF16)
        o = jnp.zeros((TR, MEMW), F32)
        for h in range(4):
            qh = jnp.where(masks[h], qn, jnp.zeros_like(qn))
            vh = jnp.where(masks[h], vv, jnp.zeros_like(vv))
            p = _mem_softmax(qh, kn).astype(BF16)
            o = o + jnp.dot(p, vh, preferred_element_type=F32)
        o_ref[...] = o.astype(BF16)

    nt = SEQ // TR
    vec = BS((1, MEMW), lambda b, t: (0, 0))
    return pl.pallas_call(
        body, grid=(NB, nt),
        in_specs=[BS((TR, MEMW), lambda b, t: (b * nt + t, qcol)),
                  BS((MEMT, MEMW), lambda b, t: (b, 0)),
                  BS((MEMT, MEMW), lambda b, t: (b, 1)), vec, vec, ANY],
        out_specs=BS((TR, MEMW), lambda b, t: (b * nt + t, 3)),
        out_shape=_sds((T, D), BF16), input_output_aliases={5: 0},
        compiler_params=_cp(("arbitrary", "arbitrary")), name=name,
    )(z, kv, kv, gq4, gk4, cat)


def _memattn_bwd(z, kv, dcat, gq4, gk4, qcol, name):
    nt = SEQ // TR

    def body(q_ref, k_ref, v_ref, do_ref, gq_ref, gk_ref,
             dq_ref, dkv_ref, dgq_ref, dgk_ref, dkn_s, dv_s):
        b = pl.program_id(0)
        t = pl.program_id(1)
        masks = _group_masks(MEMW)
        qz = q_ref[...]
        kz = k_ref[...]
        qn = _head_norm(qz, gq_ref[...], masks).astype(BF16)
        kn = _head_norm(kz, gk_ref[...], masks).astype(BF16)
        vv = v_ref[...].astype(BF16)
        dob = do_ref[...].astype(BF16)

        @pl.when(t == 0)
        def _():
            dkn_s[...] = jnp.zeros_like(dkn_s)
            dv_s[...] = jnp.zeros_like(dv_s)

        @pl.when((t == 0) & (b == 0))
        def _():
            dgq_ref[...] = jnp.zeros_like(dgq_ref)
            dgk_ref[...] = jnp.zeros_like(dgk_ref)

        dqn = jnp.zeros((TR, MEMW), F32)
        dkn = jnp.zeros((MEMT, MEMW), F32)
        dvv = jnp.zeros((MEMT, MEMW), F32)
        for h in range(4):
            qh = jnp.where(masks[h], qn, jnp.zeros_like(qn))
            kh = jnp.where(masks[h], kn, jnp.zeros_like(kn))
            doh = jnp.where(masks[h], dob, jnp.zeros_like(dob))
            p = _mem_softmax(qh, kn)
            dvv = dvv + lax.dot_general(p.astype(BF16), doh, _DIMS["tn"], preferred_element_type=F32)
            dp = lax.dot_general(doh, vv, _DIMS["nt"], preferred_element_type=F32)
            ds = p * (dp - jnp.sum(dp * p, axis=-1, keepdims=True))
            dsb = (ds * SCALE).astype(BF16)
            dqn = dqn + jnp.dot(dsb, kh, preferred_element_type=F32)
            dkn = dkn + lax.dot_general(dsb, qh, _DIMS["tn"], preferred_element_type=F32)
        dkn_s[...] += dkn
        dv_s[...] += dvv
        dq, dgq = _head_norm_bwd(dqn, qz, gq_ref[...], masks)
        dq_ref[...] = dq.astype(BF16)
        dgq_ref[...] += _fold_heads(dgq, MEMW)

        @pl.when(t == nt - 1)
        def _():
            dk, dgk = _head_norm_bwd(dkn_s[...], kz, gk_ref[...], masks)
            dkv_ref[:, 0:MEMW] = dk
            dkv_ref[:, MEMW:] = dv_s[...]
            dgk_ref[...] += _fold_heads(dgk, MEMW)

    vec = BS((1, MEMW), lambda b, t: (0, 0))
    return pl.pallas_call(
        body, grid=(NB, nt),
        in_specs=[BS((TR, MEMW), lambda b, t: (b * nt + t, qcol)),
                  BS((MEMT, MEMW), lambda b, t: (b, 0)),
                  BS((MEMT, MEMW), lambda b, t: (b, 1)),
                  BS((TR, MEMW), lambda b, t: (b * nt + t, 3)), vec, vec],
        out_specs=[BS((TR, MEMW), lambda b, t: (b * nt + t, 0)),
                   BS((MEMT, 2 * MEMW), lambda b, t: (b, 0)), vec, vec],
        out_shape=[_sds((T, MEMW), BF16), _sds((NB * MEMT, 2 * MEMW), F32),
                   _sds((1, MEMW), F32), _sds((1, MEMW), F32)],
        scratch_shapes=[pltpu.VMEM((MEMT, MEMW), F32), pltpu.VMEM((MEMT, MEMW), F32)],
        compiler_params=_cp(("arbitrary", "arbitrary")), name=name,
    )(z, kv, kv, dcat, gq4, gk4)


HALO = 32
NEXT = 64
RT = 64


def _glu(zz):
    return zz[:, :TOK] * jax.nn.sigmoid(zz[:, TOK:])


def _layer_norm_parts(y):
    mu = jnp.mean(y, axis=-1, keepdims=True)
    yc = y - mu
    rstd = lax.rsqrt(jnp.mean(yc * yc, axis=-1, keepdims=True) + EPS)
    return yc * rstd, rstd


def _conv_rows(w_ref, hbuf, r0, rows):
    y = jnp.zeros((rows, TOK), F32)
    for j in range(CONVW):
        y = y + w_ref[j:j + 1, :] * hbuf[r0 + (HALO - CONVW + 1) + j:r0 + (HALO - CONVW + 1) + j + rows, :]
    return y


def _conv_fwd(z, cw, cb, lg, lb):
    nt = SEQ // TR

    def body(zc_ref, zp_ref, w_ref, cb_ref, lg_ref, lb_ref, o_ref, hbuf):
        t = pl.program_id(1)
        hbuf[0:HALO, :] = jnp.where(t == 0, 0.0, _glu(zp_ref[...]))
        hbuf[HALO:, :] = _glu(zc_ref[...])
        for r0 in range(0, TR, RT):
            y = _conv_rows(w_ref, hbuf, r0, RT) + cb_ref[...]
            yh, _ = _layer_norm_parts(y)
            o = yh * lg_ref[...] + lb_ref[...]
            o_ref[r0:r0 + RT, :] = (o * jax.nn.sigmoid(o)).astype(BF16)

    vec = BS((1, TOK), lambda b, t: (0, 0))
    per = TR // HALO
    return pl.pallas_call(
        body, grid=(NB, nt),
        in_specs=[BS((TR, 2 * TOK), lambda b, t: (b * nt + t, 0)),
                  BS((HALO, 2 * TOK), lambda b, t: (jnp.maximum((b * nt + t) * per - 1, 0), 0)),
                  BS((32, TOK), lambda b, t: (0, 0)), vec, vec, vec],
        out_specs=BS((TR, TOK), lambda b, t: (b * nt + t, 0)),
        out_shape=_sds((T, D), BF16),
        scratch_shapes=[pltpu.VMEM((HALO + TR, TOK), F32)],
        compiler_params=_cp(("arbitrary", "arbitrary")), name="conv_fwd",
    )(z, z, cw, cb, lg, lb)


def _conv_bwd(z, dcat, cw, cb, lg, lb):
    nt = SEQ // TR
    ext = TR + NEXT

    def body(zc_ref, zp_ref, zn_ref, dc_ref, dn_ref, w_ref, cb_ref, lg_ref, lb_ref,
             du_ref, dw_ref, dcb_ref, dlg_ref, dlb_ref, dbin_ref, hbuf, dybuf):
        b = pl.program_id(0)
        t = pl.program_id(1)

        @pl.when((b == 0) & (t == 0))
        def _():
            dw_ref[...] = jnp.zeros_like(dw_ref)
            dcb_ref[...] = jnp.zeros_like(dcb_ref)
            dlg_ref[...] = jnp.zeros_like(dlg_ref)
            dlb_ref[...] = jnp.zeros_like(dlb_ref)
            dbin_ref[...] = jnp.zeros_like(dbin_ref)

        hbuf[0:HALO, :] = jnp.where(t == 0, 0.0, _glu(zp_ref[...]))
        hbuf[HALO:HALO + TR, :] = _glu(zc_ref[...])
        hbuf[HALO + TR:, :] = _glu(zn_ref[...])
        last = t == nt - 1
        for r0 in range(0, ext, RT):
            y = _conv_rows(w_ref, hbuf, r0, RT) + cb_ref[...]
            yh, rstd = _layer_norm_parts(y)
            o = yh * lg_ref[...] + lb_ref[...]
            sg = jax.nn.sigmoid(o)
            if r0 < TR:
                dtok = dc_ref[r0:r0 + RT, :]
            else:
                dtok = jnp.where(last, 0.0, dn_ref[...])
            do = dtok * (sg * (1.0 + o * (1.0 - sg)))
            dyh = do * lg_ref[...]
            dy = rstd * (dyh - jnp.mean(dyh, axis=-1, keepdims=True)
                         - yh * jnp.mean(dyh * yh, axis=-1, keepdims=True))
            dybuf[r0:r0 + RT, :] = dy
            if r0 < TR:
                dlg_ref[...] += jnp.sum(do * yh, axis=0, keepdims=True)
                dlb_ref[...] += jnp.sum(do, axis=0, keepdims=True)
                dcb_ref[...] += jnp.sum(dy, axis=0, keepdims=True)
        for r0 in range(0, TR, RT):
            dh = jnp.zeros((RT, TOK), F32)
            for j in range(CONVW):
                dh = dh + w_ref[j:j + 1, :] * dybuf[r0 + (CONVW - 1) - j:r0 + (CONVW - 1) - j + RT, :]
            a = zc_ref[r0:r0 + RT, 0:TOK]
            sg = jax.nn.sigmoid(zc_ref[r0:r0 + RT, TOK:])
            da = dh * sg
            dg = dh * a * (sg * (1.0 - sg))
            du_ref[r0:r0 + RT, 0:TOK] = da.astype(BF16)
            du_ref[r0:r0 + RT, TOK:] = dg.astype(BF16)
            dbin_ref[:, 0:TOK] += jnp.sum(da, axis=0, keepdims=True)
            dbin_ref[:, TOK:] += jnp.sum(dg, axis=0, keepdims=True)
        for j in range(CONVW):
            acc = jnp.zeros((8, TOK), F32)
            for r0 in range(0, TR, RT):
                prod = dybuf[r0:r0 + RT, :] * hbuf[r0 + (HALO - CONVW + 1) + j:r0 + (HALO - CONVW + 1) + j + RT, :]
                acc = acc + jnp.sum(prod.reshape(RT // 8, 8, TOK), axis=0)
            dw_ref[j:j + 1, :] += jnp.sum(acc, axis=0, keepdims=True)

    vec = BS((1, TOK), lambda b, t: (0, 0))
    perh = TR // HALO
    pern = TR // NEXT
    nlast_n = T // NEXT - 1
    return pl.pallas_call(
        body, grid=(NB, nt),
        in_specs=[BS((TR, 2 * TOK), lambda b, t: (b * nt + t, 0)),
                  BS((HALO, 2 * TOK), lambda b, t: (jnp.maximum((b * nt + t) * perh - 1, 0), 0)),
                  BS((NEXT, 2 * TOK), lambda b, t: (jnp.minimum((b * nt + t + 1) * pern, nlast_n), 0)),
                  BS((TR, TOK), lambda b, t: (b * nt + t, 0)),
                  BS((NEXT, TOK), lambda b, t: (jnp.minimum((b * nt + t + 1) * pern, nlast_n), 0)),
                  BS((32, TOK), lambda b, t: (0, 0)), vec, vec, vec],
        out_specs=[BS((TR, 2 * TOK), lambda b, t: (b * nt + t, 0)),
                   BS((32, TOK), lambda b, t: (0, 0)), vec, vec, vec,
                   BS((1, 2 * TOK), lambda b, t: (0, 0))],
        out_shape=[_sds((T, 2 * TOK), BF16), _sds((32, TOK), F32), _sds((1, TOK), F32),
                   _sds((1, TOK), F32), _sds((1, TOK), F32), _sds((1, 2 * TOK), F32)],
        scratch_shapes=[pltpu.VMEM((HALO + TR + NEXT, TOK), F32), pltpu.VMEM((ext, TOK), F32)],
        compiler_params=_cp(("arbitrary", "arbitrary")), name="conv_bwd",
    )(z, z, z, dcat, dcat, cw, cb, lg, lb)


def _ffn_up(h2, wgu, l):
    def body(h_ref, wg_ref, wu_ref, g_ref, u_ref, a_ref):
        hv = h_ref[...]
        g = jnp.dot(hv, wg_ref[...], preferred_element_type=F32)
        u = jnp.dot(hv, wu_ref[...], preferred_element_type=F32)
        g_ref[...] = g
        u_ref[...] = u
        a_ref[...] = (g * jax.nn.sigmoid(g) * u).astype(BF16)

    out = BS((None, TR, FS), lambda i, s: (s, i, 0))
    return pl.pallas_call(
        body, grid=(T // TR, NSH),
        in_specs=[BS((TR, D), lambda i, s: (i, 0)),
                  BS((None, None, D, FS), lambda i, s: (l, s, 0, 0)),
                  BS((None, None, D, FS), lambda i, s: (l, NSH + s, 0, 0))],
        out_specs=[out, out, out],
        out_shape=[_sds((NSH, T, FS), F32), _sds((NSH, T, FS), F32), _sds((NSH, T, FS), BF16)],
        compiler_params=_cp(("parallel", "arbitrary")), name=f"ffn_up_{l}",
    )(h2, wgu, wgu)


def _ffn_down_bwd(dx, wd, g, u, l):
    def epilogue(dact, ex, o_ref):
        gv = ex[0][...]
        uv = ex[1][...]
        sg = jax.nn.sigmoid(gv)
        o_ref[0] = (dact * uv * (sg * (1.0 + gv * (1.0 - sg)))).astype(BF16)
        o_ref[1] = (dact * (gv * sg)).astype(BF16)

    ex_spec = BS((None, TR, FS), lambda i, s, k: (s, i, 0))
    return _mm("nt", dx, wd, grid=(T // TR, NSH, 1),
               a_spec=BS((TR, D), lambda i, s, k: (i, 0)),
               b_spec=BS((None, None, FS, D), lambda i, s, k: (l, s, 0, 0)),
               out_shape=_sds((2, NSH, T, FS), BF16),
               out_spec=BS((2, None, TR, FS), lambda i, s, k: (0, s, i, 0)),
               acc_shape=(TR, FS), extras=(g, u), extra_specs=(ex_spec, ex_spec),
               epilogue=epilogue, name=f"ffn_down_bwd_{l}")


def _loss_head(y, target):
    def body(y_ref, t_ref, l_ref, dy_ref, acc):
        i = pl.program_id(0)
        e = y_ref[...] - t_ref[...]
        dy_ref[...] = e * (1.0 / D)

        @pl.when(i == 0)
        def _():
            acc[...] = jnp.zeros_like(acc)

        acc[...] += jnp.sum(jnp.mean(e * e, axis=-1, keepdims=True), axis=0, keepdims=True)

        @pl.when(i == T // TR - 1)
        def _():
            l_ref[...] = 0.5 * acc[...]

    row = BS((TR, D), lambda i: (i, 0))
    return pl.pallas_call(
        body, grid=(T // TR,), in_specs=[row, row],
        out_specs=[BS((1, 1), lambda i: (0, 0)), row],
        out_shape=[_sds((1, 1), F32), _sds((T, D), F32)],
        scratch_shapes=[pltpu.VMEM((1, 1), F32)],
        compiler_params=_cp(("arbitrary",)), name="loss_head",
    )(y, target)


def _row_tile(rows, cols, itemsize=4, limit=2 << 20):
    tr = rows
    while tr * cols * itemsize > limit and tr % 2 == 0 and (tr // 2) % 16 == 0:
        tr //= 2
    return tr


def _cast_bf16(arrs, name):
    n = len(arrs)
    rows, cols = arrs[0].shape
    tr = _row_tile(rows, cols)

    def body(*refs):
        o_ref = refs[n]
        k = pl.program_id(0)
        val = refs[0][...]
        for j in range(1, n):
            val = jnp.where(k == j, refs[j][...], val)
        o_ref[...] = val.astype(BF16)

    return pl.pallas_call(
        body, grid=(n, rows // tr),
        in_specs=[BS((tr, cols), lambda k, i: (i, 0))] * n,
        out_specs=BS((None, tr, cols), lambda k, i: (k, i, 0)),
        out_shape=_sds((n, rows, cols), BF16),
        compiler_params=_cp(("arbitrary", "arbitrary")), name=name,
    )(*arrs)


def _quad_sum(own, got, name):
    n, rows, cols = own.shape
    tr = _row_tile(rows, cols)

    def body(a_ref, q_ref, o_ref):
        o_ref[...] = ((a_ref[...] + q_ref[0].astype(F32)) + q_ref[1].astype(F32)) + q_ref[2].astype(F32)

    spec = BS((None, tr, cols), lambda k, i: (k, i, 0))
    return pl.pallas_call(
        body, grid=(n, rows // tr),
        in_specs=[spec, BS((3, None, tr, cols), lambda k, i: (0, k, i, 0))], out_specs=spec,
        out_shape=_sds((n, rows, cols), F32),
        compiler_params=_cp(("arbitrary", "arbitrary")), name=name,
    )(own, got)


def _adam_math(w, g, m, v):
    m = ADAM_B1 * m + (1.0 - ADAM_B1) * g
    v = ADAM_B2 * v + (1.0 - ADAM_B2) * (g * g)
    m_hat = m / (1.0 - ADAM_B1 ** ADAM_STEP)
    v_hat = v / (1.0 - ADAM_B2 ** ADAM_STEP)
    delta = -ADAM_LR * (m_hat / (jnp.sqrt(v_hat) + ADAM_EPS) + ADAM_WD * w)
    return delta, m, v


def _adamw_big(w, g, m, v, name):
    shape = w.shape
    cols = shape[-1]
    rows = w.size // cols
    tr = _row_tile(rows, cols, limit=1 << 20)

    def body(w_ref, g_ref, m_ref, v_ref, d_ref, nm_ref, nv_ref):
        d, nm, nv = _adam_math(w_ref[...], g_ref[...], m_ref[...], v_ref[...])
        d_ref[...] = d
        nm_ref[...] = nm
        nv_ref[...] = nv

    spec = BS((tr, cols), lambda i: (i, 0))
    outs = pl.pallas_call(
        body, grid=(rows // tr,), in_specs=[spec] * 4, out_specs=[spec] * 3,
        out_shape=[_sds((rows, cols), F32)] * 3,
        compiler_params=_cp(("arbitrary",)), name=name,
    )(*[a.reshape(rows, cols) for a in (w, g, m, v)])
    return [o.reshape(shape) for o in outs]


def _adamw_small(ws, gs, ms, vs):
    n = len(ws)

    def body(*refs):
        for i in range(n):
            d, nm, nv = _adam_math(refs[i][...], refs[n + i][...], refs[2 * n + i][...],
                                   refs[3 * n + i][...])
            refs[4 * n + i][...] = d
            refs[5 * n + i][...] = nm
            refs[6 * n + i][...] = nv

    vm = BS(memory_space=pltpu.VMEM)
    outs = pl.pallas_call(
        body, in_specs=[vm] * (4 * n), out_specs=[vm] * (3 * n),
        out_shape=[_sds(w.shape, F32) for w in ws] * 3,
        compiler_params=_cp(), name="adamw_small",
    )(*ws, *gs, *ms, *vs)
    return outs[:n], outs[n:2 * n], outs[2 * n:]


def _place():
    x, y, c = lax.axis_index("x"), lax.axis_index("y"), lax.axis_index("c")
    chips = [(1 - x, y), (x, 1 - y), (1 - x, 1 - y)]
    return x, y, c, chips


def _gather_weights(srcs, dst_shapes, views, small):
    nu = len(srcs)
    nd = len(dst_shapes)

    def body(*refs):
        src = refs[:nu]
        small_ref = refs[nu]
        dst = refs[nu + 1:nu + 1 + nd]
        small_dst = refs[nu + 1 + nd]
        vbuf = refs[nu + 2 + nd:2 * nu + 2 + nd]
        send, recv, fsend, frecv, lsem, ssend, srecv, vsem = refs[2 * nu + 2 + nd:]
        x, y, c, chips = _place()
        s = 2 * x + y
        vw = [views[u](dst) for u in range(nu)]

        def ici(u, j, shard, to):
            return pltpu.make_async_remote_copy(
                src_ref=vbuf[u].at[:, c], dst_ref=vw[u].at[:, shard, c],
                send_sem=send.at[3 * u + j], recv_sem=recv.at[3 * u + j],
                device_id=to, device_id_type=MESH)

        def fwd(u, j, shard, half):
            return pltpu.make_async_remote_copy(
                src_ref=vw[u].at[:, shard, half], dst_ref=vw[u].at[:, shard, half],
                send_sem=fsend.at[3 * u + j], recv_sem=frecv.at[3 * u + j],
                device_id=(x, y, 1 - c), device_id_type=MESH)

        def small_copy(j, shard, to):
            return pltpu.make_async_remote_copy(
                src_ref=small_ref, dst_ref=small_dst.at[shard],
                send_sem=ssend.at[j], recv_sem=srecv.at[j], device_id=to, device_id_type=MESH)

        stage = [pltpu.make_async_copy(src[u], vbuf[u], vsem.at[u]) for u in range(nu)]
        local = [pltpu.make_async_copy(vbuf[u], vw[u].at[:, s], lsem.at[u]) for u in range(nu)]
        local.append(pltpu.make_async_copy(small_ref, small_dst.at[s], lsem.at[nu]))
        for cp in stage:
            cp.start()
        local[nu].start()
        for j, chip in enumerate(chips):
            small_copy(j, s, (*chip, c)).start()
        for u in range(nu):
            stage[u].wait()
            for j, chip in enumerate(chips):
                ici(u, j, s, (*chip, c)).start()
            local[u].start()
        for u in range(nu):
            for j, chip in enumerate(chips):
                sj = 2 * chip[0] + chip[1]
                ici(u, j, sj, (x, y, c)).wait_recv()
                fwd(u, j, sj, c).start()
        for u in range(nu):
            for j, chip in enumerate(chips):
                sj = 2 * chip[0] + chip[1]
                fwd(u, j, sj, 1 - c).wait_recv()
        for j, chip in enumerate(chips):
            sj = 2 * chip[0] + chip[1]
            small_copy(j, sj, (x, y, c)).wait_recv()
        for u in range(nu):
            for j, chip in enumerate(chips):
                ici(u, j, s, (*chip, c)).wait_send()
                fwd(u, j, s, c).wait_send()
        for j, chip in enumerate(chips):
            small_copy(j, s, (*chip, c)).wait_send()
        for cp in local:
            cp.wait()

    dma = pltpu.SemaphoreType.DMA
    outs = pl.pallas_call(
        body, in_specs=[ANY] * nu + [BS(memory_space=pltpu.VMEM)], out_specs=[ANY] * (nd + 1),
        out_shape=[_sds(sh, BF16) for sh in dst_shapes] + [_sds((NSH,) + small.shape, F32)],
        scratch_shapes=[pltpu.VMEM(a.shape, BF16) for a in srcs]
        + [dma((3 * nu,)), dma((3 * nu,)), dma((3 * nu,)), dma((3 * nu,)),
           dma((nu + 1,)), dma((3,)), dma((3,)), dma((nu,))],
        compiler_params=pltpu.CompilerParams(has_side_effects=True, vmem_limit_bytes=40 << 20),
        name="gather_weights",
    )(*srcs, small)
    return outs[:nd], outs[nd]


def _pair_reduce(g, name):
    pn, _, _, rh, cc = g.shape
    n = pn * NSH

    def body(g_ref, own_ref, sb_ref, sendb, recvb, stage, outf, outb, send, recv, lsem, osem):
        x, y, c, _ = _place()
        s = 2 * x + y

        def load(k, half):
            return pltpu.make_async_copy(g_ref.at[k // NSH, k % NSH, half], stage.at[k % 2], lsem.at[k % 2])

        def push(k):
            return pltpu.make_async_remote_copy(
                src_ref=sendb.at[k], dst_ref=recvb.at[k], send_sem=send.at[k], recv_sem=recv.at[k],
                device_id=(x, y, 1 - c), device_id_type=MESH)

        def store(k):
            return pltpu.make_async_copy(outb.at[k % 2], sb_ref.at[k // NSH, k % NSH], osem.at[k % 2])

        load(0, 1 - c).start()
        for k in range(n):
            if k + 1 < n:
                load(k + 1, 1 - c).start()
            load(k, 1 - c).wait()
            sendb[k] = stage[k % 2].astype(BF16)
            push(k).start()
        load(0, c).start()
        for k in range(n):
            if k + 1 < n:
                load(k + 1, c).start()
            load(k, c).wait()
            push(k).wait_recv()
            total = stage[k % 2] + recvb[k].astype(F32)
            if k >= 2:
                store(k - 2).wait()
            outb[k % 2] = total.astype(BF16)
            store(k).start()

            @pl.when(s == k % NSH)
            def _():
                outf[...] = total
                keep = pltpu.make_async_copy(outf, own_ref.at[k // NSH], osem.at[2])
                keep.start()
                keep.wait()

        for k in range(max(n - 2, 0), n):
            store(k).wait()
        for k in range(n):
            push(k).wait_send()

    dma = pltpu.SemaphoreType.DMA
    return pl.pallas_call(
        body, in_specs=[ANY], out_specs=[ANY, ANY],
        out_shape=[_sds((pn, rh, cc), F32), _sds((pn, NSH, rh, cc), BF16)],
        scratch_shapes=[pltpu.VMEM((n, rh, cc), BF16), pltpu.VMEM((n, rh, cc), BF16),
                        pltpu.VMEM((2, rh, cc), F32), pltpu.VMEM((rh, cc), F32),
                        pltpu.VMEM((2, rh, cc), BF16), dma((n,)), dma((n,)), dma((2,)), dma((3,))],
        compiler_params=pltpu.CompilerParams(has_side_effects=True, vmem_limit_bytes=56 << 20),
        name=name,
    )(g)


def _chip_exchange(sums_bf16):
    nu = len(sums_bf16)

    def body(*refs):
        sb = refs[:nu]
        got = refs[nu:2 * nu]
        send, recv = refs[2 * nu:]
        x, y, c, chips = _place()

        def push(u, j, shard, to):
            return pltpu.make_async_remote_copy(
                src_ref=sb[u].at[:, shard], dst_ref=got[u].at[j], send_sem=send.at[3 * u + j],
                recv_sem=recv.at[3 * u + j], device_id=to, device_id_type=MESH)

        for u in range(nu):
            for j, chip in enumerate(chips):
                push(u, j, 2 * chip[0] + chip[1], (*chip, c)).start()
        for u in range(nu):
            for j, chip in enumerate(chips):
                push(u, j, 2 * chip[0] + chip[1], (*chip, c)).wait()

    dma = pltpu.SemaphoreType.DMA
    shapes = [(3, a.shape[0], a.shape[2], a.shape[3]) for a in sums_bf16]
    return pl.pallas_call(
        body, in_specs=[ANY] * nu, out_specs=[ANY] * nu,
        out_shape=[_sds(sh, BF16) for sh in shapes],
        scratch_shapes=[dma((3 * nu,)), dma((3 * nu,))],
        compiler_params=pltpu.CompilerParams(has_side_effects=True), name="chip_exchange",
    )(*sums_bf16)


def _final_exchange(halves, out_shapes, targets):
    nu = len(halves)
    no = len(out_shapes)
    ncp = sum(len(t) for t in targets)

    def body(*refs):
        hv = refs[:nu]
        out = refs[nu:nu + no]
        sbuf = refs[nu + no:2 * nu + no]
        rbuf = refs[2 * nu + no:3 * nu + no]
        send, recv, lsem, osem, csem = refs[3 * nu + no:]
        x, y, c, _ = _place()
        stage = [pltpu.make_async_copy(hv[u], sbuf[u], lsem.at[u]) for u in range(nu)]
        push = [pltpu.make_async_remote_copy(
            src_ref=sbuf[u], dst_ref=rbuf[u], send_sem=send.at[u], recv_sem=recv.at[u],
            device_id=(x, y, 1 - c), device_id_type=MESH) for u in range(nu)]
        mine, theirs = [], []
        k = 0
        for u in range(nu):
            rh = hv[u].shape[1]
            for (p, oi, li) in targets[u]:
                mine.append((u, pltpu.make_async_copy(
                    sbuf[u].at[p], out[oi].at[li, pl.ds(c * rh, rh), :], csem.at[k])))
                theirs.append((u, pltpu.make_async_copy(
                    rbuf[u].at[p], out[oi].at[li, pl.ds((1 - c) * rh, rh), :], osem.at[k])))
                k += 1
        for cp in stage:
            cp.start()
        for u in range(nu):
            stage[u].wait()
            push[u].start()
            for (v, cp) in mine:
                if v == u:
                    cp.start()
        for u in range(nu):
            push[u].wait_recv()
            for (v, cp) in theirs:
                if v == u:
                    cp.start()
        for (_, cp) in theirs + mine:
            cp.wait()
        for u in range(nu):
            push[u].wait_send()

    dma = pltpu.SemaphoreType.DMA
    bufs = [pltpu.VMEM(h.shape, F32) for h in halves]
    return pl.pallas_call(
        body, in_specs=[ANY] * nu, out_specs=[ANY] * no,
        out_shape=[_sds(sh, F32) for sh in out_shapes],
        scratch_shapes=bufs + bufs + [dma((nu,)), dma((nu,)), dma((nu,)), dma((ncp,)), dma((ncp,))],
        compiler_params=pltpu.CompilerParams(has_side_effects=True, vmem_limit_bytes=56 << 20),
        name="final_exchange",
    )(*halves)


def _small_allreduce(pack):
    rows = pack.shape[0]

    def body(p_ref, o_ref, buf, send, recv):
        x, y, c, _ = _place()
        me = 4 * x + 2 * y + c
        buf[me] = p_ref[...]
        k = 0
        copies = []
        for dx in range(2):
            for dy in range(2):
                for dc in range(2):
                    if dx == 0 and dy == 0 and dc == 0:
                        continue
                    to = (jnp.where(dx, 1 - x, x), jnp.where(dy, 1 - y, y), jnp.where(dc, 1 - c, c))
                    src_slot = 4 * to[0] + 2 * to[1] + to[2]
                    copies.append((pltpu.make_async_remote_copy(
                        src_ref=p_ref, dst_ref=buf.at[me], send_sem=send.at[k], recv_sem=recv.at[k],
                        device_id=to, device_id_type=MESH), src_slot, k))
                    k += 1
        for cp, _, _ in copies:
            cp.start()
        for cp, src_slot, k in copies:
            pltpu.make_async_remote_copy(
                src_ref=p_ref, dst_ref=buf.at[src_slot], send_sem=send.at[k], recv_sem=recv.at[k],
                device_id=(x, y, c), device_id_type=MESH).wait()
        acc = buf[0]
        for d in range(1, 8):
            acc = acc + buf[d]
        o_ref[...] = acc

    dma = pltpu.SemaphoreType.DMA
    vm = BS(memory_space=pltpu.VMEM)
    return pl.pallas_call(
        body, in_specs=[vm], out_specs=vm, out_shape=_sds((rows, D), F32),
        scratch_shapes=[pltpu.VMEM((8, rows, D), F32), dma((7,)), dma((7,))],
        compiler_params=pltpu.CompilerParams(has_side_effects=True, vmem_limit_bytes=32 << 20),
        name="small_allreduce",
    )(pack)


def _in_proj(h, w, bias, name):
    n = w.shape[1]
    tn = 640 if n == NA else 896
    ep = None
    extras, especs = (), ()
    if bias is not None:
        def ep(acc, ex, o_ref):
            o_ref[...] = acc + ex[0][...]
        extras = (bias,)
        especs = (BS((1, tn), lambda i, j, k: (0, j)),)
    return _mm("nn", h, w, grid=(T // TR, n // tn, 1),
               a_spec=BS((TR, D), lambda i, j, k: (i, 0)), b_spec=BS((D, tn), lambda i, j, k: (0, j)),
               out_shape=_sds((T, n), F32), out_spec=BS((TR, tn), lambda i, j, k: (i, j)),
               acc_shape=(TR, tn), extras=extras, extra_specs=especs, epilogue=ep, name=name)


def _add_res(acc, ex, o_ref):
    o_ref[...] = acc + ex[0][...]


def _local_step(x, mem, target, w, p):
    row = lambda i, j, k: (i, 0)
    saved = []
    bias = _bias_expand(p["rel_u"])
    for l in range(2):
        type_a = l == 0
        h = _rms_fwd(x, p["norm1_g"][l:l + 1], f"rms1_{l}")
        memn = _rms_fwd(mem, p["mem_norm_g"][l:l + 1], f"rmsmem_{l}")
        kv = _mm("nn", memn, w["kv"], grid=(1, 1, 1),
                 a_spec=BS((NB * MEMT, D), lambda i, j, k: (0, 0)),
                 b_spec=BS((None, D, 2 * MEMW), lambda i, j, k: (l, 0, 0)),
                 out_shape=_sds((NB * MEMT, 2 * MEMW), F32),
                 out_spec=BS((NB * MEMT, 2 * MEMW), lambda i, j, k: (0, 0)),
                 acc_shape=(8, 128), name=f"memkv_{l}")
        if type_a:
            z = _in_proj(h, w["a"], None, "inproj_a")
            cat = _attn_fwd(z, bias, p["a_q_g2"], p["a_k_g2"])
            qcol = NA // MEMW - 1
        else:
            z = _in_proj(h, w["b"], p["b_b_in"], "inproj_b")
            cat = _conv_fwd(z, p["conv_w"], p["conv_b"], p["ln_g"], p["ln_b"])
            qcol = NBW // MEMW - 1
        cat = _memattn_fwd(z, kv, cat, p["mq_g4"][l:l + 1], p["mk_g4"][l:l + 1], qcol, f"memattn_fwd_{l}")
        x1 = _mm("nn", cat, w["wo"], grid=(T // TR, 1, 1), a_spec=BS((TR, D), row),
                 b_spec=BS((None, D, D), lambda i, j, k: (l, 0, 0)),
                 out_shape=_sds((T, D), F32), out_spec=BS((TR, D), row), acc_shape=(8, 128),
                 extras=(x,), extra_specs=(BS((TR, D), row),), epilogue=_add_res, name=f"outproj_{l}")
        h2 = _rms_fwd(x1, p["norm2_g"][l:l + 1], f"rms2_{l}")
        g, u, act = _ffn_up(h2, w["gu"], l)
        x2 = _mm("nn", act, w["wd"], grid=(T // TR, 1, NSH),
                 a_spec=BS((None, TR, FS), lambda i, j, k: (k, i, 0)),
                 b_spec=BS((None, None, FS, D), lambda i, j, k: (l, k, 0, 0)),
                 out_shape=_sds((T, D), F32), out_spec=BS((TR, D), row), acc_shape=(TR, D),
                 extras=(x1,), extra_specs=(BS((TR, D), row),), epilogue=_add_res, name=f"ffn_down_{l}")
        saved.append(dict(x=x, h=h, memn=memn, kv=kv, z=z, cat=cat, x1=x1, h2=h2, g=g, u=u, act=act,
                          qcol=qcol))
        x = x2

    loss, dx = _loss_head(x, target)

    big = dict(a=None, b=None, kv=None, wo=None, gu=None, wd=None)
    small = {}
    tk = TR
    nkt = T // tk
    for l in (1, 0):
        sv = saved[l]
        dgu = _ffn_down_bwd(dx, w["wd"], sv["g"], sv["u"], l)
        dgu8 = dgu.reshape(2 * NSH, T, FS)
        big["wd"] = _mm("tn", sv["act"], dx, grid=(NSH, 1, nkt),
                        a_spec=BS((None, tk, FS), lambda i, j, k: (i, k, 0)),
                        b_spec=BS((tk, D), lambda i, j, k: (k, 0)),
                        out_shape=_sds((2, NSH, FS, D), F32),
                        out_spec=BS((None, None, FS, D), lambda i, j, k: (l, i, 0, 0)),
                        acc_shape=(FS, D), into=big["wd"], name=f"dw_down_{l}")
        dh2 = _mm("nt", dgu8, w["gu"], grid=(T // TR, 1, 2 * NSH),
                  a_spec=BS((None, TR, FS), lambda i, j, k: (k, i, 0)),
                  b_spec=BS((None, None, D, FS), lambda i, j, k: (l, k, 0, 0)),
                  out_shape=_sds((T, D), F32), out_spec=BS((TR, D), row), acc_shape=(TR, D),
                  name=f"dh2_{l}")
        big["gu"] = _mm("tn", sv["h2"], dgu8, grid=(2 * NSH, 1, nkt),
                        a_spec=BS((tk, D), lambda i, j, k: (k, 0)),
                        b_spec=BS((None, tk, FS), lambda i, j, k: (i, k, 0)),
                        out_shape=_sds((2, 2 * NSH, D, FS), F32),
                        out_spec=BS((None, None, D, FS), lambda i, j, k: (l, i, 0, 0)),
                        acc_shape=(D, FS), into=big["gu"], name=f"dw_gu_{l}")
        dx1, small[f"norm2_g{l}"] = _rms_bwd(dh2, sv["x1"], p["norm2_g"][l:l + 1], dx, f"rms2_bwd_{l}")
        dcat = _mm("nt", dx1, w["wo"], grid=(T // TR, 1, 1), a_spec=BS((TR, D), row),
                   b_spec=BS((None, D, D), lambda i, j, k: (l, 0, 0)),
                   out_shape=_sds((T, D), F32), out_spec=BS((TR, D), row), acc_shape=(8, 128),
                   name=f"dcat_{l}")
        big["wo"] = _mm("tn", sv["cat"], dx1, grid=(1, 1, nkt),
                        a_spec=BS((tk, D), lambda i, j, k: (k, 0)), b_spec=BS((tk, D), lambda i, j, k: (k, 0)),
                        out_shape=_sds((2, D, D), F32), out_spec=BS((None, D, D), lambda i, j, k: (l, 0, 0)),
                        acc_shape=(D, D), into=big["wo"], name=f"dw_out_{l}")
        dqm, dkv, small[f"mq_g{l}"], small[f"mk_g{l}"] = _memattn_bwd(
            sv["z"], sv["kv"], dcat, p["mq_g4"][l:l + 1], p["mk_g4"][l:l + 1], sv["qcol"], f"memattn_bwd_{l}")
        if l == 0:
            dq, dk, dv, dbias, small["a_q_g"], small["a_k_g"] = _attn_bwd(
                sv["z"], dcat, bias, p["a_q_g2"], p["a_k_g2"])
            small["rel_u"] = _bias_reduce(dbias)
            dz = jnp.concatenate([dq, dk, dv, dqm], axis=1)
            w_in, key, n, tn = w["a"], "a", NA, 640
        else:
            du, small["conv_w"], small["conv_b"], small["ln_g"], small["ln_b"], dbin_u = _conv_bwd(
                sv["z"], dcat, p["conv_w"], p["conv_b"], p["ln_g"], p["ln_b"])
            dz = jnp.concatenate([du, dqm], axis=1)
            small["b_in_u"] = dbin_u
            w_in, key, n, tn = w["b"], "b", NBW, 896
        dh = _mm("nt", dz, w_in, grid=(T // TR, 1, n // tn),
                 a_spec=BS((TR, tn), lambda i, j, k: (i, k)), b_spec=BS((D, tn), lambda i, j, k: (0, k)),
                 out_shape=_sds((T, D), F32), out_spec=BS((TR, D), row), acc_shape=(TR, D),
                 name=f"dh_{l}")
        big[key] = _mm("tn", sv["h"], dz, grid=(1, n // tn, nkt),
                       a_spec=BS((tk, D), lambda i, j, k: (k, 0)), b_spec=BS((tk, tn), lambda i, j, k: (k, j)),
                       out_shape=_sds((D, n), F32), out_spec=BS((D, tn), lambda i, j, k: (0, j)),
                       acc_shape=(D, tn), name=f"dw_in_{l}")
        if l == 1:
            small["b_in_qm"] = _colsum(dqm, "colsum_dqm")
        dx, small[f"norm1_g{l}"] = _rms_bwd(dh, sv["x"], p["norm1_g"][l:l + 1], dx1, f"rms1_bwd_{l}")
        big["kv"] = _mm("tn", sv["memn"], dkv, grid=(1, 1, 1),
                        a_spec=BS((NB * MEMT, D), lambda i, j, k: (0, 0)),
                        b_spec=BS((NB * MEMT, 2 * MEMW), lambda i, j, k: (0, 0)),
                        out_shape=_sds((2, D, 2 * MEMW), F32),
                        out_spec=BS((None, D, 2 * MEMW), lambda i, j, k: (l, 0, 0)),
                        acc_shape=(8, 128), into=big["kv"], name=f"dw_kv_{l}")
        dmemn = _mm("nt", dkv, w["kv"], grid=(1, 1, 1),
                    a_spec=BS((NB * MEMT, 2 * MEMW), lambda i, j, k: (0, 0)),
                    b_spec=BS((None, D, 2 * MEMW), lambda i, j, k: (l, 0, 0)),
                    out_shape=_sds((NB * MEMT, D), F32), out_spec=BS((NB * MEMT, D), lambda i, j, k: (0, 0)),
                    acc_shape=(8, 128), name=f"dmemn_{l}")
        _, small[f"mem_norm_g{l}"] = _rms_bwd(dmemn, mem, p["mem_norm_g"][l:l + 1], None, f"rmsmem_bwd_{l}")
    return loss, dx, big, small


def _colsum(a, name):
    rows, cols = a.shape

    def body(a_ref, o_ref):
        @pl.when(pl.program_id(0) == 0)
        def _():
            o_ref[...] = jnp.zeros_like(o_ref)

        o_ref[...] += jnp.sum(a_ref[...].astype(F32), axis=0, keepdims=True)

    return pl.pallas_call(
        body, grid=(rows // TR,), in_specs=[BS((TR, cols), lambda i: (i, 0))],
        out_specs=BS((1, cols), lambda i: (0, 0)), out_shape=_sds((1, cols), F32),
        compiler_params=_cp(("arbitrary",)), name=name,
    )(a)


_PACK_ROWS = 64


def _pad_to(a, rows, cols=D):
    return jnp.pad(a, ((0, rows - a.shape[0]), (0, cols - a.shape[1])))


def _pack_small(sm):
    parts = [
        jnp.concatenate([sm["norm1_g0"], sm["norm1_g1"]], 0),
        jnp.concatenate([sm["mem_norm_g0"], sm["mem_norm_g1"]], 0),
        jnp.concatenate([sm["norm2_g0"], sm["norm2_g1"]], 0),
        _pad_to(sm["a_q_g"], 1), _pad_to(sm["a_k_g"], 1),
        _pad_to(jnp.concatenate([sm["mq_g0"], sm["mq_g1"]], 0), 2),
        _pad_to(jnp.concatenate([sm["mk_g0"], sm["mk_g1"]], 0), 2),
        _pad_to(sm["conv_b"], 1), _pad_to(sm["ln_g"], 1), _pad_to(sm["ln_b"], 1),
        _pad_to(sm["b_in_u"][:, :D], 1), _pad_to(sm["b_in_u"][:, D:], 1),
        _pad_to(sm["b_in_qm"], 1),
        _pad_to(sm["conv_w"][:CONVW], CONVW),
        sm["rel_u"].reshape(12, D),
    ]
    pack = jnp.concatenate(parts, 0)
    return jnp.pad(pack, ((0, _PACK_ROWS - pack.shape[0]), (0, 0)))


def _rel_table_to_u(rel_bias):
    flat = jnp.concatenate([jnp.broadcast_to(rel_bias[:, 191:192], (12, 447)), rel_bias[:, ::-1]], axis=1)
    return jnp.pad(flat, ((0, 0), (192, 1024 - 192 - 639))).reshape(12, 1, 1024)


def _u_to_rel_table(du):
    flat = du[:, 192:192 + 639]
    g = flat[:, 447:][:, ::-1]
    return g, flat[:, :447]


def kernel(x, mem, norm1_g, mem_norm_g, a_w_in, a_q_g, a_k_g, a_rel_bias, b_w_in, b_b_in, b_conv_w, b_conv_b, b_ln_g, b_ln_b, mq_g, mk_g, w_mem_kv, w_out, norm2_g, w_gate, w_up, w_down, loss_target, m_norm1_g, m_mem_norm_g, m_a_w_in, m_a_q_g, m_a_k_g, m_a_rel_bias, m_b_w_in, m_b_b_in, m_b_conv_w, m_b_conv_b, m_b_ln_g, m_b_ln_b, m_mq_g, m_mk_g, m_w_mem_kv, m_w_out, m_norm2_g, m_w_gate, m_w_up, m_w_down, v_norm1_g, v_mem_norm_g, v_a_w_in, v_a_q_g, v_a_k_g, v_a_rel_bias, v_b_w_in, v_b_b_in, v_b_conv_w, v_b_conv_b, v_b_ln_g, v_b_ln_b, v_mq_g, v_mk_g, v_w_mem_kv, v_w_out, v_norm2_g, v_w_gate, v_w_up, v_w_down):
    sx = 2 * lax.axis_index("x") + lax.axis_index("y")

    src_a = _cast_bf16([a_w_in[0]], "cast_a").reshape(1, 2, D // 2, NA // NSH)
    src_b = _cast_bf16([b_w_in[0]], "cast_b").reshape(1, 2, D // 2, NBW // NSH)
    src_kv = _cast_bf16([w_mem_kv[0], w_mem_kv[1]], "cast_kv").reshape(2, 2, 128, 2 * MEMW)
    src_wo = _cast_bf16([w_out[0], w_out[1]], "cast_wo").reshape(2, 2, 128, D)
    src_g = _cast_bf16([w_gate[0], w_gate[1]], "cast_gate").reshape(2, 2, D // 2, FS)
    src_u = _cast_bf16([w_up[0], w_up[1]], "cast_up").reshape(2, 2, D // 2, FS)
    src_wd = _cast_bf16([w_down[0], w_down[1]], "cast_wd").reshape(2, 2, FS // 2, D)
    small_src = jnp.concatenate([
        jnp.pad(b_b_in, ((0, 0), (0, 512 - 448))),
        jnp.pad(b_conv_w[0], ((0, 0), (0, 512 - 192))),
        jnp.pad(jnp.concatenate([b_conv_b, b_ln_g, b_ln_b], 0), ((0, 0), (0, 512 - 192))),
        jnp.zeros((5, 512), F32)], 0)
    dst_shapes = [(1, NSH, 2, D // 2, NA // NSH), (1, NSH, 2, D // 2, NBW // NSH),
                  (2, NSH, 2, 128, 2 * MEMW), (2, NSH, 2, 128, D),
                  (2, 2, NSH, 2, D // 2, FS), (2, NSH, 2, FS // 2, D)]
    views = [lambda d: d[0], lambda d: d[1], lambda d: d[2], lambda d: d[3],
             lambda d: d[4].at[:, 0], lambda d: d[4].at[:, 1], lambda d: d[5]]
    (ga, gb, gkv, gwo, ggu, gwd), small_all = _gather_weights(
        [src_a, src_b, src_kv, src_wo, src_g, src_u, src_wd], dst_shapes, views, small_src)
    w = dict(
        a=ga.reshape(NSH, D, NA // NSH).transpose(1, 0, 2).reshape(D, NA),
        b=gb.reshape(NSH, D, NBW // NSH).transpose(1, 0, 2).reshape(D, NBW),
        kv=gkv.reshape(2, D, 2 * MEMW), wo=gwo.reshape(2, D, D),
        gu=ggu.reshape(2, 2 * NSH, D, FS), wd=gwd.reshape(2, NSH, FS, D))

    conv_w_full = small_all[:, 1:1 + CONVW, :192].transpose(1, 0, 2).reshape(CONVW, TOK)
    vec3 = small_all[:, 32:35, :192].transpose(1, 0, 2).reshape(3, TOK)
    p = dict(
        norm1_g=norm1_g, mem_norm_g=mem_norm_g, norm2_g=norm2_g,
        a_q_g2=jnp.tile(a_q_g, (1, 2)), a_k_g2=jnp.tile(a_k_g, (1, 2)),
        mq_g4=jnp.tile(mq_g, (1, 4)), mk_g4=jnp.tile(mk_g, (1, 4)),
        rel_u=_rel_table_to_u(a_rel_bias[0]),
        b_b_in=small_all[:, 0, :448].reshape(1, NBW),
        conv_w=jnp.pad(conv_w_full, ((0, 1), (0, 0))),
        conv_b=vec3[0:1], ln_g=vec3[1:2], ln_b=vec3[2:3])

    loss, grad_x, big, small = _local_step(
        x.reshape(T, D), mem.reshape(NB * MEMT, D), loss_target.reshape(T, D), w, p)
    loss = lax.psum(loss[0, 0], ("x", "y", "c"))

    units = [
        big["a"].reshape(D, NSH, NA // NSH).transpose(1, 0, 2).reshape(1, NSH, 2, D // 2, NA // NSH),
        big["b"].reshape(D, NSH, NBW // NSH).transpose(1, 0, 2).reshape(1, NSH, 2, D // 2, NBW // NSH),
        big["kv"].reshape(2, NSH, 2, 128, 2 * MEMW),
        big["wo"].reshape(2, NSH, 2, 128, D),
        big["gu"].reshape(4, NSH, 2, D // 2, FS),
        big["wd"].reshape(2, NSH, 2, FS // 2, D),
    ]
    own, sums_b = [], []
    for u in range(len(units)):
        of, sb = _pair_reduce(units[u], f"pair_reduce_{u}")
        own.append(of)
        sums_b.append(sb)
    parts = _chip_exchange(sums_b)
    halves = [_quad_sum(own[u], parts[u], f"quad_sum_{u}") for u in range(len(units))]
    out_shapes = [(1, D, NA // NSH), (1, D, NBW // NSH), (2, 2 * 128, 2 * MEMW), (2, 2 * 128, D),
                  (2, D, FS), (2, D, FS), (2, FS, D)]
    targets = [[(0, 0, 0)], [(0, 1, 0)], [(0, 2, 0), (1, 2, 1)], [(0, 3, 0), (1, 3, 1)],
               [(0, 4, 0), (1, 5, 0), (2, 4, 1), (3, 5, 1)], [(0, 6, 0), (1, 6, 1)]]
    g_a, g_b, g_kv, g_wo, g_gate, g_up, g_wd = _final_exchange(halves, out_shapes, targets)

    tot = _small_allreduce(_pack_small(small))
    g_rel, clip_part = _u_to_rel_table(tot[49:61])
    g_rel = jnp.concatenate([g_rel[:, :191], g_rel[:, 191:] + _rowsum(clip_part)], axis=1)
    b_in_full = jnp.concatenate([tot[15:16], tot[16:17, :512], tot[17:18, :MEMW]], axis=1)
    g_small = dict(
        norm1_g=tot[0:2], mem_norm_g=tot[2:4], norm2_g=tot[4:6],
        a_q_g=tot[6:7, :HD], a_k_g=tot[7:8, :HD], a_rel_bias=g_rel[None],
        b_b_in=lax.dynamic_slice(b_in_full, (0, sx * 448), (1, 448)),
        b_conv_w=lax.dynamic_slice(tot[18:49, :TOK], (0, sx * 192), (CONVW, 192))[None],
        b_conv_b=lax.dynamic_slice(tot[12:13, :TOK], (0, sx * 192), (1, 192)),
        b_ln_g=lax.dynamic_slice(tot[13:14, :TOK], (0, sx * 192), (1, 192)),
        b_ln_b=lax.dynamic_slice(tot[14:15, :TOK], (0, sx * 192), (1, 192)),
        mq_g=tot[8:10, :HD], mk_g=tot[10:12, :HD])

    names = ["norm1_g", "mem_norm_g", "a_w_in", "a_q_g", "a_k_g", "a_rel_bias", "b_w_in", "b_b_in",
             "b_conv_w", "b_conv_b", "b_ln_g", "b_ln_b", "mq_g", "mk_g", "w_mem_kv", "w_out",
             "norm2_g", "w_gate", "w_up", "w_down"]
    weights = dict(zip(names, [norm1_g, mem_norm_g, a_w_in, a_q_g, a_k_g, a_rel_bias, b_w_in, b_b_in,
                               b_conv_w, b_conv_b, b_ln_g, b_ln_b, mq_g, mk_g, w_mem_kv, w_out,
                               norm2_g, w_gate, w_up, w_down]))
    ms = dict(zip(names, [m_norm1_g, m_mem_norm_g, m_a_w_in, m_a_q_g, m_a_k_g, m_a_rel_bias, m_b_w_in,
                          m_b_b_in, m_b_conv_w, m_b_conv_b, m_b_ln_g, m_b_ln_b, m_mq_g, m_mk_g,
                          m_w_mem_kv, m_w_out, m_norm2_g, m_w_gate, m_w_up, m_w_down]))
    vs = dict(zip(names, [v_norm1_g, v_mem_norm_g, v_a_w_in, v_a_q_g, v_a_k_g, v_a_rel_bias, v_b_w_in,
                          v_b_b_in, v_b_conv_w, v_b_conv_b, v_b_ln_g, v_b_ln_b, v_mq_g, v_mk_g,
                          v_w_mem_kv, v_w_out, v_norm2_g, v_w_gate, v_w_up, v_w_down]))
    grads = dict(g_small)
    grads.update(a_w_in=g_a, b_w_in=g_b, w_mem_kv=g_kv, w_out=g_wo, w_gate=g_gate, w_up=g_up, w_down=g_wd)
    big_names = ["a_w_in", "b_w_in", "w_mem_kv", "w_out", "w_gate", "w_up", "w_down"]
    small_names = [n for n in names if n not in big_names]
    delta, new_m, new_v = {}, {}, {}
    for n in big_names:
        delta[n], new_m[n], new_v[n] = _adamw_big(weights[n], grads[n], ms[n], vs[n], f"adamw_{n}")
    as2d = lambda a: a.reshape(-1, a.shape[-1])
    d_s, m_s, v_s = _adamw_small([as2d(weights[n]) for n in small_names], [as2d(grads[n]) for n in small_names],
                                 [as2d(ms[n]) for n in small_names], [as2d(vs[n]) for n in small_names])
    for i, n in enumerate(small_names):
        delta[n] = d_s[i].reshape(weights[n].shape)
        new_m[n] = m_s[i].reshape(weights[n].shape)
        new_v[n] = v_s[i].reshape(weights[n].shape)

    return (loss, grad_x.reshape(NB, SEQ, D), *[grads[n] for n in names], *[delta[n] for n in names],
            *[new_m[n] for n in names], *[new_v[n] for n in names])


def _rowsum(a):
    def body(a_ref, o_ref):
        o_ref[...] = jnp.sum(a_ref[...], axis=1, keepdims=True)

    vm = BS(memory_space=pltpu.VMEM)
    return pl.pallas_call(body, in_specs=[vm], out_specs=vm, out_shape=_sds((a.shape[0], 1), F32),
                          compiler_params=_cp(), name="rowsum")(a)
```

```python
import functools

import jax
import jax.numpy as jnp
from jax import lax
from jax.experimental import pallas as pl
from jax.experimental.pallas import tpu as pltpu

F32 = jnp.float32
BF16 = jnp.bfloat16
BS = pl.BlockSpec
ANY = pl.BlockSpec(memory_space=pl.ANY)
MESH = pl.DeviceIdType.MESH

D = 1024
SEQ = 2048
NB = 2
T = NB * SEQ
MEMT = 256
HD = 64
TOK = 768
MEMW = 256
NA = 3 * TOK + MEMW
NBW = 2 * TOK + MEMW
FF = 2816
NSH = 4
FS = FF // NSH
CONVW = 31
EPS = 1e-6
NEG = -1e30
SCALE = HD ** -0.5
QB = 256
KWIN = 768
KPAD = 512
TR = 512

ADAM_LR = 0.001
ADAM_B1 = 0.9
ADAM_B2 = 0.999
ADAM_EPS = 1e-08
ADAM_WD = 0.01
ADAM_STEP = 10

_DIMS = {
    "nn": (((1,), (0,)), ((), ())),
    "nt": (((1,), (1,)), ((), ())),
    "tn": (((0,), (0,)), ((), ())),
}


def _cp(sem=None, vmem_mb=48):
    return pltpu.CompilerParams(dimension_semantics=sem, vmem_limit_bytes=vmem_mb << 20)


def _sds(shape, dtype):
    return jax.ShapeDtypeStruct(tuple(shape), dtype)


def _mm(mode, a, b, *, grid, a_spec, b_spec, out_shape, out_spec, acc_shape, name,
        extras=(), extra_specs=(), epilogue=None, into=None):
    n_ex = len(extras)
    nk = grid[2]
    dims = _DIMS[mode]

    def body(a_ref, b_ref, *rest):
        ex = rest[:n_ex]
        o_ref = rest[n_ex + (1 if into is not None else 0)]
        acc = rest[-1]
        k = pl.program_id(2)
        prod = lax.dot_general(a_ref[...].astype(BF16), b_ref[...].astype(BF16), dims,
                               preferred_element_type=F32)

        def finish(val):
            if epilogue is None:
                o_ref[...] = val.astype(o_ref.dtype)
            else:
                epilogue(val, ex, o_ref)

        if nk == 1:
            finish(prod)
        else:
            @pl.when(k == 0)
            def _():
                acc[...] = prod

            @pl.when(k > 0)
            def _():
                acc[...] += prod

            @pl.when(k == nk - 1)
            def _():
                finish(acc[...])

    ins = [a, b, *extras]
    in_specs = [a_spec, b_spec, *extra_specs]
    aliases = {}
    if into is not None:
        ins.append(into)
        in_specs.append(ANY)
        aliases = {len(ins) - 1: 0}
    return pl.pallas_call(
        body, grid=grid, in_specs=in_specs, out_specs=out_spec, out_shape=out_shape,
        scratch_shapes=[pltpu.VMEM(acc_shape if nk > 1 else (8, 128), F32)],
        input_output_aliases=aliases,
        compiler_params=_cp(("parallel", "parallel", "arbitrary")), name=name,
    )(*ins)


def _rms_fwd(x, g, name):
    rows = x.shape[0]

    def body(x_ref, g_ref, o_ref):
        xv = x_ref[...]
        r = lax.rsqrt(jnp.mean(xv * xv, axis=-1, keepdims=True) + EPS)
        o_ref[...] = (xv * r * g_ref[...]).astype(BF16)

    return pl.pallas_call(
        body, grid=(rows // TR,),
        in_specs=[BS((TR, D), lambda i: (i, 0)), BS((1, D), lambda i: (0, 0))],
        out_specs=BS((TR, D), lambda i: (i, 0)), out_shape=_sds((rows, D), BF16),
        compiler_params=_cp(("arbitrary",)), name=name,
    )(x, g)


def _rms_bwd(dh, x, g, dres, name):
    rows = x.shape[0]
    has_res = dres is not None

    def body(*refs):
        if has_res:
            dh_ref, x_ref, g_ref, r_ref, dx_ref, dg_ref = refs
        else:
            dh_ref, x_ref, g_ref, dx_ref, dg_ref = refs
        xv = x_ref[...]
        dhv = dh_ref[...]
        r = lax.rsqrt(jnp.mean(xv * xv, axis=-1, keepdims=True) + EPS)
        xh = xv * r
        gy = dhv * g_ref[...]
        dx = r * (gy - xh * jnp.mean(gy * xh, axis=-1, keepdims=True))
        if has_res:
            dx = dx + r_ref[...]
        dx_ref[...] = dx

        @pl.when(pl.program_id(0) == 0)
        def _():
            dg_ref[...] = jnp.zeros_like(dg_ref)

        dg_ref[...] += jnp.sum(dhv * xh, axis=0, keepdims=True)

    row = BS((TR, D), lambda i: (i, 0))
    vec = BS((1, D), lambda i: (0, 0))
    ins = [dh, x, g] + ([dres] if has_res else [])
    return pl.pallas_call(
        body, grid=(rows // TR,),
        in_specs=[row, row, vec] + ([row] if has_res else []),
        out_specs=[row, vec], out_shape=[_sds((rows, D), F32), _sds((1, D), F32)],
        compiler_params=_cp(("arbitrary",)), name=name,
    )(*ins)


def _group_masks(width):
    lane = lax.broadcasted_iota(jnp.int32, (1, width), 1)
    return [(lane >= HD * h) & (lane < HD * (h + 1)) for h in range(width // HD)]


def _group_mean(v, masks):
    out = jnp.zeros_like(v)
    for m in masks:
        s = jnp.sum(jnp.where(m, v, 0.0), axis=-1, keepdims=True) * (1.0 / HD)
        out = jnp.where(m, s, out)
    return out


def _head_norm(zv, g, masks):
    r = lax.rsqrt(_group_mean(zv * zv, masks) + EPS)
    return zv * r * g


def _head_norm_bwd(dy, zv, g, masks):
    r = lax.rsqrt(_group_mean(zv * zv, masks) + EPS)
    zh = zv * r
    gy = dy * g
    dz = r * (gy - zh * _group_mean(gy * zh, masks))
    return dz, jnp.sum(dy * zh, axis=0, keepdims=True)


def _fold_heads(v, width):
    vb = jnp.broadcast_to(v, (8, width))
    out = vb
    for h in range(1, width // HD):
        out = out + pltpu.roll(vb, width - HD * h, axis=1)
    return out[0:1]


def _bias_expand(u):
    def body(u_ref, o_ref):
        x = jnp.broadcast_to(u_ref[...], (QB, 1024))
        rolled = pltpu.roll(x, 1024 - (QB - 1), axis=1, stride=1, stride_axis=0)[:, :KWIN]
        row = lax.broadcasted_iota(jnp.int32, (QB, 1), 0)
        col = lax.broadcasted_iota(jnp.int32, (1, KWIN), 1)
        lo = (row // 64) * 64
        ok = (col >= lo) & (col < lo + 576)
        o_ref[...] = jnp.where(ok, rolled, NEG)

    return pl.pallas_call(
        body, grid=(12,), in_specs=[BS((None, 1, 1024), lambda h: (h, 0, 0))],
        out_specs=BS((None, QB, KWIN), lambda h: (h, 0, 0)), out_shape=_sds((12, QB, KWIN), F32),
        compiler_params=_cp(("arbitrary",)), name="bias_expand",
    )(u)


def _bias_reduce(ds):
    def body(d_ref, o_ref):
        ri = lax.broadcasted_iota(jnp.int32, (QB, QB), 0)
        ci = lax.broadcasted_iota(jnp.int32, (QB, QB), 1)
        flip = (ri + ci == QB - 1).astype(F32)
        drev = jnp.dot(flip, d_ref[...], precision=lax.Precision.HIGHEST, preferred_element_type=F32)
        x = jnp.concatenate([drev, jnp.zeros((QB, 1024 - KWIN), F32)], axis=1)
        rolled = pltpu.roll(x, 0, axis=1, stride=1, stride_axis=0)
        o_ref[...] = jnp.sum(rolled, axis=0, keepdims=True)

    return pl.pallas_call(
        body, grid=(12,), in_specs=[BS((None, QB, KWIN), lambda h: (h, 0, 0))],
        out_specs=BS((None, 1, 1024), lambda h: (h, 0, 0)), out_shape=_sds((12, 1, 1024), F32),
        compiler_params=_cp(("arbitrary",)), name="bias_reduce",
    )(ds)


def _attn_softmax(qh, kw, bias, startadd):
    s = lax.dot_general(qh, kw, _DIMS["nt"], preferred_element_type=F32) + bias + startadd
    m = jnp.max(s, axis=-1, keepdims=True)
    p = jnp.exp(s - m)
    return p * (1.0 / jnp.sum(p, axis=-1, keepdims=True))


def _attn_fwd(z, bias, gq2, gk2):
    def body(q_ref, k_ref, v_ref, b_ref, gq_ref, gk_ref, o_ref, qn_s, kn_s, v_s):
        masks = _group_masks(128)
        qn_s[...] = (_head_norm(q_ref[...], gq_ref[...], masks) * SCALE).astype(BF16)
        kn_s[0:KPAD, :] = jnp.zeros((KPAD, 128), BF16)
        kn_s[KPAD:, :] = _head_norm(k_ref[...], gk_ref[...], masks).astype(BF16)
        v_s[0:KPAD, :] = jnp.zeros((KPAD, 128), BF16)
        v_s[KPAD:, :] = v_ref[...].astype(BF16)
        col = lax.broadcasted_iota(jnp.int32, (1, KWIN), 1)

        def blk(i, carry):
            r0 = pl.multiple_of(i * QB, QB)
            qb = qn_s[pl.ds(r0, QB), :]
            kw = kn_s[pl.ds(r0, KWIN), :]
            vw = v_s[pl.ds(r0, KWIN), :]
            startadd = jnp.where(col + r0 < KPAD, NEG, 0.0)
            o = jnp.zeros((QB, 128), F32)
            for h in range(2):
                qh = jnp.where(masks[h], qb, jnp.zeros_like(qb))
                vh = jnp.where(masks[h], vw, jnp.zeros_like(vw))
                p = _attn_softmax(qh, kw, b_ref[h], startadd).astype(BF16)
                o = o + jnp.dot(p, vh, preferred_element_type=F32)
            o_ref[pl.ds(r0, QB), :] = o.astype(BF16)
            return carry

        lax.fori_loop(0, SEQ // QB, blk, 0)

    vec = BS((1, 128), lambda b, hp: (0, 0))
    return pl.pallas_call(
        body, grid=(NB, 6),
        in_specs=[BS((SEQ, 128), lambda b, hp: (b, hp)),
                  BS((SEQ, 128), lambda b, hp: (b, 6 + hp)),
                  BS((SEQ, 128), lambda b, hp: (b, 12 + hp)),
                  BS((2, QB, KWIN), lambda b, hp: (hp, 0, 0)), vec, vec],
        out_specs=BS((SEQ, 128), lambda b, hp: (b, hp)),
        out_shape=_sds((T, D), BF16),
        scratch_shapes=[pltpu.VMEM((SEQ, 128), BF16), pltpu.VMEM((SEQ + KPAD, 128), BF16),
                        pltpu.VMEM((SEQ + KPAD, 128), BF16)],
        compiler_params=_cp(("arbitrary", "arbitrary")), name="attn_fwd",
    )(z, z, z, bias, gq2, gk2)


def _attn_bwd(z, dcat, bias, gq2, gk2):
    def body(q_ref, k_ref, v_ref, do_ref, b_ref, gq_ref, gk_ref,
             dq_ref, dk_ref, dv_ref, db_ref, dgq_ref, dgk_ref,
             qn_s, kn_s, v_s, dqn_s, dkn_s, dv_s):
        hp = pl.program_id(0)
        b = pl.program_id(1)
        masks = _group_masks(128)
        qn_s[...] = (_head_norm(q_ref[...], gq_ref[...], masks) * SCALE).astype(BF16)
        kn_s[0:KPAD, :] = jnp.zeros((KPAD, 128), BF16)
        kn_s[KPAD:, :] = _head_norm(k_ref[...], gk_ref[...], masks).astype(BF16)
        v_s[0:KPAD, :] = jnp.zeros((KPAD, 128), BF16)
        v_s[KPAD:, :] = v_ref[...].astype(BF16)
        dkn_s[...] = jnp.zeros_like(dkn_s)
        dv_s[...] = jnp.zeros_like(dv_s)

        @pl.when(b == 0)
        def _():
            db_ref[...] = jnp.zeros_like(db_ref)

        @pl.when((b == 0) & (hp == 0))
        def _():
            dgq_ref[...] = jnp.zeros_like(dgq_ref)
            dgk_ref[...] = jnp.zeros_like(dgk_ref)

        col = lax.broadcasted_iota(jnp.int32, (1, KWIN), 1)

        def blk(i, carry):
            r0 = pl.multiple_of(i * QB, QB)
            qb = qn_s[pl.ds(r0, QB), :]
            kw = kn_s[pl.ds(r0, KWIN), :]
            vw = v_s[pl.ds(r0, KWIN), :]
            dob = do_ref[pl.ds(r0, QB), :].astype(BF16)
            startadd = jnp.where(col + r0 < KPAD, NEG, 0.0)
            dqn = jnp.zeros((QB, 128), F32)
            dkw = jnp.zeros((KWIN, 128), F32)
            dvw = jnp.zeros((KWIN, 128), F32)
            for h in range(2):
                qh = jnp.where(masks[h], qb, jnp.zeros_like(qb))
                kh = jnp.where(masks[h], kw, jnp.zeros_like(kw))
                doh = jnp.where(masks[h], dob, jnp.zeros_like(dob))
                p = _attn_softmax(qh, kw, b_ref[h], startadd)
                dvw = dvw + lax.dot_general(p.astype(BF16), doh, _DIMS["tn"],
                                            preferred_element_type=F32)
                dp = lax.dot_general(doh, vw, _DIMS["nt"], preferred_element_type=F32)
                ds = p * (dp - jnp.sum(dp * p, axis=-1, keepdims=True))
                db_ref[h] += ds
                dsb = ds.astype(BF16)
                dqn = dqn + jnp.dot(dsb, kh, preferred_element_type=F32)
                dkw = dkw + lax.dot_general(dsb, qh, _DIMS["tn"], preferred_element_type=F32)
            dqn_s[pl.ds(r0, QB), :] = dqn * SCALE
            dkn_s[pl.ds(r0, KWIN), :] += dkw
            dv_s[pl.ds(r0, KWIN), :] += dvw
            return carry

        lax.fori_loop(0, SEQ // QB, blk, 0)

        dq, dgq = _head_norm_bwd(dqn_s[...], q_ref[...], gq_ref[...], masks)
        dk, dgk = _head_norm_bwd(dkn_s[KPAD:, :], k_ref[...], gk_ref[...], masks)
        dq_ref[...] = dq.astype(BF16)
        dk_ref[...] = dk.astype(BF16)
        dv_ref[...] = dv_s[KPAD:, :].astype(BF16)
        dgq_ref[...] += _fold_heads(dgq, 128)
        dgk_ref[...] += _fold_heads(dgk, 128)

    vec = BS((1, 128), lambda hp, b: (0, 0))
    row = BS((SEQ, 128), lambda hp, b: (b, hp))
    return pl.pallas_call(
        body, grid=(6, NB),
        in_specs=[row,
                  BS((SEQ, 128), lambda hp, b: (b, 6 + hp)),
                  BS((SEQ, 128), lambda hp, b: (b, 12 + hp)),
                  row,
                  BS((2, QB, KWIN), lambda hp, b: (hp, 0, 0)), vec, vec],
        out_specs=[row, row, row, BS((2, QB, KWIN), lambda hp, b: (hp, 0, 0)), vec, vec],
        out_shape=[_sds((T, TOK), BF16), _sds((T, TOK), BF16), _sds((T, TOK), BF16),
                   _sds((12, QB, KWIN), F32), _sds((1, 128), F32), _sds((1, 128), F32)],
        scratch_shapes=[pltpu.VMEM((SEQ, 128), BF16), pltpu.VMEM((SEQ + KPAD, 128), BF16),
                        pltpu.VMEM((SEQ + KPAD, 128), BF16), pltpu.VMEM((SEQ, 128), F32),
                        pltpu.VMEM((SEQ + KPAD, 128), F32), pltpu.VMEM((SEQ + KPAD, 128), F32)],
        compiler_params=_cp(("arbitrary", "arbitrary")), name="attn_bwd",
    )(z, z, z, dcat, bias, gq2, gk2)


def _mem_softmax(qh, kn):
    s = lax.dot_general(qh, kn, _DIMS["nt"], preferred_element_type=F32)
    m = jnp.max(s, axis=-1, keepdims=True)
    p = jnp.exp(s - m)
    return p * (1.0 / jnp.sum(p, axis=-1, keepdims=True))


def _memattn_fwd(z, kv, cat, gq4, gk4, qcol, name):
    def body(q_ref, k_ref, v_ref, gq_ref, gk_ref, cat_ref, o_ref):
        del cat_ref
        masks = _group_masks(MEMW)
        qn = (_head_norm(q_ref[...], gq_ref[...], masks) * SCALE).astype(BF16)
        kn = _head_norm(k_ref[...], gk_ref[...], masks).astype(BF16)
        vv = v_ref[...].astype(BF16)
        o = jnp.zeros((TR, MEMW), F32)
        for h in range(4):
            qh = jnp.where(masks[h], qn, jnp.zeros_like(qn))
            vh = jnp.where(masks[h], vv, jnp.zeros_like(vv))
            p = _mem_softmax(qh, kn).astype(BF16)
            o = o + jnp.dot(p, vh, preferred_element_type=F32)
        o_ref[...] = o.astype(BF16)

    nt = SEQ // TR
    vec = BS((1, MEMW), lambda b, t: (0, 0))
    return pl.pallas_call(
        body, grid=(NB, nt),
        in_specs=[BS((TR, MEMW), lambda b, t: (b * nt + t, qcol)),
                  BS((MEMT, MEMW), lambda b, t: (b, 0)),
                  BS((MEMT, MEMW), lambda b, t: (b, 1)), vec, vec, ANY],
        out_specs=BS((TR, MEMW), lambda b, t: (b * nt + t, 3)),
        out_shape=_sds((T, D), BF16), input_output_aliases={5: 0},
        compiler_params=_cp(("arbitrary", "arbitrary")), name=name,
    )(z, kv, kv, gq4, gk4, cat)


def _memattn_bwd(z, kv, dcat, gq4, gk4, qcol, name):
    nt = SEQ // TR

    def body(q_ref, k_ref, v_ref, do_ref, gq_ref, gk_ref,
             dq_ref, dkv_ref, dgq_ref, dgk_ref, dkn_s, dv_s):
        b = pl.program_id(0)
        t = pl.program_id(1)
        masks = _group_masks(MEMW)
        qz = q_ref[...]
        kz = k_ref[...]
        qn = (_head_norm(qz, gq_ref[...], masks) * SCALE).astype(BF16)
        kn = _head_norm(kz, gk_ref[...], masks).astype(BF16)
        vv = v_ref[...].astype(BF16)
        dob = do_ref[...].astype(BF16)

        @pl.when(t == 0)
        def _():
            dkn_s[...] = jnp.zeros_like(dkn_s)
            dv_s[...] = jnp.zeros_like(dv_s)

        @pl.when((t == 0) & (b == 0))
        def _():
            dgq_ref[...] = jnp.zeros_like(dgq_ref)
            dgk_ref[...] = jnp.zeros_like(dgk_ref)

        dqn = jnp.zeros((TR, MEMW), F32)
        dkn = jnp.zeros((MEMT, MEMW), F32)
        dvv = jnp.zeros((MEMT, MEMW), F32)
        for h in range(4):
            qh = jnp.where(masks[h], qn, jnp.zeros_like(qn))
            kh = jnp.where(masks[h], kn, jnp.zeros_like(kn))
            doh = jnp.where(masks[h], dob, jnp.zeros_like(dob))
            p = _mem_softmax(qh, kn)
            dvv = dvv + lax.dot_general(p.astype(BF16), doh, _DIMS["tn"], preferred_element_type=F32)
            dp = lax.dot_general(doh, vv, _DIMS["nt"], preferred_element_type=F32)
            ds = p * (dp - jnp.sum(dp * p, axis=-1, keepdims=True))
            dsb = ds.astype(BF16)
            dqn = dqn + jnp.dot(dsb, kh, preferred_element_type=F32)
            dkn = dkn + lax.dot_general(dsb, qh, _DIMS["tn"], preferred_element_type=F32)
        dkn_s[...] += dkn
        dv_s[...] += dvv
        dq, dgq = _head_norm_bwd(dqn * SCALE, qz, gq_ref[...], masks)
        dq_ref[...] = dq.astype(BF16)
        dgq_ref[...] += _fold_heads(dgq, MEMW)

        @pl.when(t == nt - 1)
        def _():
            dk, dgk = _head_norm_bwd(dkn_s[...], kz, gk_ref[...], masks)
            dkv_ref[:, 0:MEMW] = dk
            dkv_ref[:, MEMW:] = dv_s[...]
            dgk_ref[...] += _fold_heads(dgk, MEMW)

    vec = BS((1, MEMW), lambda b, t: (0, 0))
    return pl.pallas_call(
        body, grid=(NB, nt),
        in_specs=[BS((TR, MEMW), lambda b, t: (b * nt + t, qcol)),
                  BS((MEMT, MEMW), lambda b, t: (b, 0)),
                  BS((MEMT, MEMW), lambda b, t: (b, 1)),
                  BS((TR, MEMW), lambda b, t: (b * nt + t, 3)), vec, vec],
        out_specs=[BS((TR, MEMW), lambda b, t: (b * nt + t, 0)),
                   BS((MEMT, 2 * MEMW), lambda b, t: (b, 0)), vec, vec],
        out_shape=[_sds((T, MEMW), BF16), _sds((NB * MEMT, 2 * MEMW), F32),
                   _sds((1, MEMW), F32), _sds((1, MEMW), F32)],
        scratch_shapes=[pltpu.VMEM((MEMT, MEMW), F32), pltpu.VMEM((MEMT, MEMW), F32)],
        compiler_params=_cp(("arbitrary", "arbitrary")), name=name,
    )(z, kv, kv, dcat, gq4, gk4)


HALO = 32
NEXT = 64
RT = 64


def _glu(zz):
    return zz[:, :TOK] * jax.nn.sigmoid(zz[:, TOK:])


def _layer_norm_parts(y):
    mu = jnp.mean(y, axis=-1, keepdims=True)
    yc = y - mu
    rstd = lax.rsqrt(jnp.mean(yc * yc, axis=-1, keepdims=True) + EPS)
    return yc * rstd, rstd


def _shifted_copies(src, dst, rows):
    for b in range(1, 8):
        dst[b - 1, 0:rows, :] = src[b:b + rows, :]


def _tap(src, shifted, off, r0, rows):
    b = off % 8
    if b == 0:
        return src[r0 + off:r0 + off + rows, :]
    return shifted[b - 1, r0 + off - b:r0 + off - b + rows, :]


def _conv_rows(w_ref, hbuf, hs, r0, rows):
    y = jnp.zeros((rows, TOK), F32)
    for j in range(CONVW):
        y = y + w_ref[j:j + 1, :] * _tap(hbuf, hs, (HALO - CONVW + 1) + j, r0, rows)
    return y


def _conv_fwd(z, cw, cb, lg, lb):
    nt = SEQ // TR

    def body(zc_ref, zp_ref, w_ref, cb_ref, lg_ref, lb_ref, o_ref, hbuf, hs):
        t = pl.program_id(1)
        hbuf[0:HALO, :] = jnp.where(t == 0, 0.0, _glu(zp_ref[...]))
        hbuf[HALO:, :] = _glu(zc_ref[...])
        _shifted_copies(hbuf, hs, HALO + TR - 8)
        for r0 in range(0, TR, RT):
            y = _conv_rows(w_ref, hbuf, hs, r0, RT) + cb_ref[...]
            yh, _ = _layer_norm_parts(y)
            o = yh * lg_ref[...] + lb_ref[...]
            o_ref[r0:r0 + RT, :] = (o * jax.nn.sigmoid(o)).astype(BF16)

    vec = BS((1, TOK), lambda b, t: (0, 0))
    per = TR // HALO
    return pl.pallas_call(
        body, grid=(NB, nt),
        in_specs=[BS((TR, 2 * TOK), lambda b, t: (b * nt + t, 0)),
                  BS((HALO, 2 * TOK), lambda b, t: (jnp.maximum((b * nt + t) * per - 1, 0), 0)),
                  BS((32, TOK), lambda b, t: (0, 0)), vec, vec, vec],
        out_specs=BS((TR, TOK), lambda b, t: (b * nt + t, 0)),
        out_shape=_sds((T, D), BF16),
        scratch_shapes=[pltpu.VMEM((HALO + TR, TOK), F32), pltpu.VMEM((7, HALO + TR, TOK), F32)],
        compiler_params=_cp(("arbitrary", "arbitrary")), name="conv_fwd",
    )(z, z, cw, cb, lg, lb)


def _conv_bwd(z, dcat, cw, cb, lg, lb):
    nt = SEQ // TR
    ext = TR + NEXT

    def body(zc_ref, zp_ref, zn_ref, dc_ref, dn_ref, w_ref, cb_ref, lg_ref, lb_ref,
             du_ref, dw_ref, dcb_ref, dlg_ref, dlb_ref, dbin_ref, hbuf, dybuf, hs, dys):
        b = pl.program_id(0)
        t = pl.program_id(1)

        @pl.when((b == 0) & (t == 0))
        def _():
            dw_ref[...] = jnp.zeros_like(dw_ref)
            dcb_ref[...] = jnp.zeros_like(dcb_ref)
            dlg_ref[...] = jnp.zeros_like(dlg_ref)
            dlb_ref[...] = jnp.zeros_like(dlb_ref)
            dbin_ref[...] = jnp.zeros_like(dbin_ref)

        hbuf[0:HALO, :] = jnp.where(t == 0, 0.0, _glu(zp_ref[...]))
        hbuf[HALO:HALO + TR, :] = _glu(zc_ref[...])
        hbuf[HALO + TR:, :] = _glu(zn_ref[...])
        _shifted_copies(hbuf, hs, HALO + TR + NEXT - 8)
        last = t == nt - 1
        for r0 in range(0, ext, RT):
            y = _conv_rows(w_ref, hbuf, hs, r0, RT) + cb_ref[...]
            yh, rstd = _layer_norm_parts(y)
            o = yh * lg_ref[...] + lb_ref[...]
            sg = jax.nn.sigmoid(o)
            if r0 < TR:
                dtok = dc_ref[r0:r0 + RT, :]
            else:
                dtok = jnp.where(last, 0.0, dn_ref[...])
            do = dtok * (sg * (1.0 + o * (1.0 - sg)))
            dyh = do * lg_ref[...]
            dy = rstd * (dyh - jnp.mean(dyh, axis=-1, keepdims=True)
                         - yh * jnp.mean(dyh * yh, axis=-1, keepdims=True))
            dybuf[r0:r0 + RT, :] = dy
            if r0 < TR:
                dlg_ref[...] += jnp.sum(do * yh, axis=0, keepdims=True)
                dlb_ref[...] += jnp.sum(do, axis=0, keepdims=True)
                dcb_ref[...] += jnp.sum(dy, axis=0, keepdims=True)
        _shifted_copies(dybuf, dys, ext - 8)
        for r0 in range(0, TR, RT):
            dh = jnp.zeros((RT, TOK), F32)
            for j in range(CONVW):
                dh = dh + w_ref[j:j + 1, :] * _tap(dybuf, dys, (CONVW - 1) - j, r0, RT)
            a = zc_ref[r0:r0 + RT, 0:TOK]
            sg = jax.nn.sigmoid(zc_ref[r0:r0 + RT, TOK:])
            da = dh * sg
            dg = dh * a * (sg * (1.0 - sg))
            du_ref[r0:r0 + RT, 0:TOK] = da.astype(BF16)
            du_ref[r0:r0 + RT, TOK:] = dg.astype(BF16)
            dbin_ref[:, 0:TOK] += jnp.sum(da, axis=0, keepdims=True)
            dbin_ref[:, TOK:] += jnp.sum(dg, axis=0, keepdims=True)
        for j in range(CONVW):
            acc = jnp.zeros((8, TOK), F32)
            for r0 in range(0, TR, RT):
                prod = dybuf[r0:r0 + RT, :] * _tap(hbuf, hs, (HALO - CONVW + 1) + j, r0, RT)
                acc = acc + jnp.sum(prod.reshape(RT // 8, 8, TOK), axis=0)
            dw_ref[j:j + 1, :] += jnp.sum(acc, axis=0, keepdims=True)

    vec = BS((1, TOK), lambda b, t: (0, 0))
    perh = TR // HALO
    pern = TR // NEXT
    nlast_n = T // NEXT - 1
    return pl.pallas_call(
        body, grid=(NB, nt),
        in_specs=[BS((TR, 2 * TOK), lambda b, t: (b * nt + t, 0)),
                  BS((HALO, 2 * TOK), lambda b, t: (jnp.maximum((b * nt + t) * perh - 1, 0), 0)),
                  BS((NEXT, 2 * TOK), lambda b, t: (jnp.minimum((b * nt + t + 1) * pern, nlast_n), 0)),
                  BS((TR, TOK), lambda b, t: (b * nt + t, 0)),
                  BS((NEXT, TOK), lambda b, t: (jnp.minimum((b * nt + t + 1) * pern, nlast_n), 0)),
                  BS((32, TOK), lambda b, t: (0, 0)), vec, vec, vec],
        out_specs=[BS((TR, 2 * TOK), lambda b, t: (b * nt + t, 0)),
                   BS((32, TOK), lambda b, t: (0, 0)), vec, vec, vec,
                   BS((1, 2 * TOK), lambda b, t: (0, 0))],
        out_shape=[_sds((T, 2 * TOK), BF16), _sds((32, TOK), F32), _sds((1, TOK), F32),
                   _sds((1, TOK), F32), _sds((1, TOK), F32), _sds((1, 2 * TOK), F32)],
        scratch_shapes=[pltpu.VMEM((HALO + TR + NEXT, TOK), F32), pltpu.VMEM((ext, TOK), F32),
                        pltpu.VMEM((7, HALO + TR + NEXT, TOK), F32), pltpu.VMEM((7, ext, TOK), F32)],
        compiler_params=_cp(("arbitrary", "arbitrary"), vmem_mb=56), name="conv_bwd",
    )(z, z, z, dcat, dcat, cw, cb, lg, lb)


def _ffn_up(h2, wgu, l):
    def body(h_ref, wg_ref, wu_ref, g_ref, u_ref, a_ref):
        hv = h_ref[...]
        g = jnp.dot(hv, wg_ref[...], preferred_element_type=F32)
        u = jnp.dot(hv, wu_ref[...], preferred_element_type=F32)
        g_ref[...] = g
        u_ref[...] = u
        a_ref[...] = (g * jax.nn.sigmoid(g) * u).astype(BF16)

    out = BS((None, TR, FS), lambda s, i: (s, i, 0))
    return pl.pallas_call(
        body, grid=(NSH, T // TR),
        in_specs=[BS((TR, D), lambda s, i: (i, 0)),
                  BS((None, None, D, FS), lambda s, i: (l, s, 0, 0)),
                  BS((None, None, D, FS), lambda s, i: (l, NSH + s, 0, 0))],
        out_specs=[out, out, out],
        out_shape=[_sds((NSH, T, FS), F32), _sds((NSH, T, FS), F32), _sds((NSH, T, FS), BF16)],
        compiler_params=_cp(("parallel", "arbitrary")), name=f"ffn_up_{l}",
    )(h2, wgu, wgu)


def _ffn_down_bwd(dx, wd, g, u, l):
    def epilogue(dact, ex, o_ref):
        gv = ex[0][...]
        uv = ex[1][...]
        sg = jax.nn.sigmoid(gv)
        o_ref[0] = (dact * uv * (sg * (1.0 + gv * (1.0 - sg)))).astype(BF16)
        o_ref[1] = (dact * (gv * sg)).astype(BF16)

    ex_spec = BS((None, TR, FS), lambda i, s, k: (s, i, 0))
    return _mm("nt", dx, wd, grid=(T // TR, NSH, 1),
               a_spec=BS((TR, D), lambda i, s, k: (i, 0)),
               b_spec=BS((None, None, FS, D), lambda i, s, k: (l, s, 0, 0)),
               out_shape=_sds((2, NSH, T, FS), BF16),
               out_spec=BS((2, None, TR, FS), lambda i, s, k: (0, s, i, 0)),
               acc_shape=(TR, FS), extras=(g, u), extra_specs=(ex_spec, ex_spec),
               epilogue=epilogue, name=f"ffn_down_bwd_{l}")


def _loss_head(y, target):
    def body(y_ref, t_ref, l_ref, dy_ref, acc):
        i = pl.program_id(0)
        e = y_ref[...] - t_ref[...]
        dy_ref[...] = e * (1.0 / D)

        @pl.when(i == 0)
        def _():
            acc[...] = jnp.zeros_like(acc)

        acc[...] += jnp.sum(jnp.mean(e * e, axis=-1, keepdims=True), axis=0, keepdims=True)

        @pl.when(i == T // TR - 1)
        def _():
            l_ref[...] = 0.5 * acc[...]

    row = BS((TR, D), lambda i: (i, 0))
    return pl.pallas_call(
        body, grid=(T // TR,), in_specs=[row, row],
        out_specs=[BS((1, 1), lambda i: (0, 0)), row],
        out_shape=[_sds((1, 1), F32), _sds((T, D), F32)],
        scratch_shapes=[pltpu.VMEM((1, 1), F32)],
        compiler_params=_cp(("arbitrary",)), name="loss_head",
    )(y, target)


def _row_tile(rows, cols, itemsize=4, limit=2 << 20):
    tr = rows
    while tr * cols * itemsize > limit and tr % 2 == 0 and (tr // 2) % 16 == 0:
        tr //= 2
    return tr


def _cast_bf16(arrs, name):
    n = len(arrs)
    rows, cols = arrs[0].shape
    tr = _row_tile(rows, cols)

    def body(*refs):
        o_ref = refs[n]
        k = pl.program_id(0)
        val = refs[0][...]
        for j in range(1, n):
            val = jnp.where(k == j, refs[j][...], val)
        o_ref[...] = val.astype(BF16)

    return pl.pallas_call(
        body, grid=(n, rows // tr),
        in_specs=[BS((tr, cols), lambda k, i: (i, 0))] * n,
        out_specs=BS((None, tr, cols), lambda k, i: (k, i, 0)),
        out_shape=_sds((n, rows, cols), BF16),
        compiler_params=_cp(("arbitrary", "arbitrary")), name=name,
    )(*arrs)


def _quad_sum(own, got, name):
    n, rows, cols = own.shape
    tr = _row_tile(rows, cols)

    def body(a_ref, q_ref, o_ref):
        o_ref[...] = ((a_ref[...] + q_ref[0].astype(F32)) + q_ref[1].astype(F32)) + q_ref[2].astype(F32)

    spec = BS((None, tr, cols), lambda k, i: (k, i, 0))
    return pl.pallas_call(
        body, grid=(n, rows // tr),
        in_specs=[spec, BS((3, None, tr, cols), lambda k, i: (0, k, i, 0))], out_specs=spec,
        out_shape=_sds((n, rows, cols), F32),
        compiler_params=_cp(("arbitrary", "arbitrary")), name=name,
    )(own, got)


def _adam_math(w, g, m, v):
    m = ADAM_B1 * m + (1.0 - ADAM_B1) * g
    v = ADAM_B2 * v + (1.0 - ADAM_B2) * (g * g)
    m_hat = m / (1.0 - ADAM_B1 ** ADAM_STEP)
    v_hat = v / (1.0 - ADAM_B2 ** ADAM_STEP)
    delta = -ADAM_LR * (m_hat / (jnp.sqrt(v_hat) + ADAM_EPS) + ADAM_WD * w)
    return delta, m, v


def _adamw_big(w, g, m, v, name):
    shape = w.shape
    cols = shape[-1]
    rows = w.size // cols
    tr = _row_tile(rows, cols, limit=1 << 20)

    def body(w_ref, g_ref, m_ref, v_ref, d_ref, nm_ref, nv_ref):
        d, nm, nv = _adam_math(w_ref[...], g_ref[...], m_ref[...], v_ref[...])
        d_ref[...] = d
        nm_ref[...] = nm
        nv_ref[...] = nv

    spec = BS((tr, cols), lambda i: (i, 0))
    outs = pl.pallas_call(
        body, grid=(rows // tr,), in_specs=[spec] * 4, out_specs=[spec] * 3,
        out_shape=[_sds((rows, cols), F32)] * 3,
        compiler_params=_cp(("arbitrary",)), name=name,
    )(*[a.reshape(rows, cols) for a in (w, g, m, v)])
    return [o.reshape(shape) for o in outs]


def _adamw_small(ws, gs, ms, vs):
    n = len(ws)

    def body(*refs):
        for i in range(n):
            d, nm, nv = _adam_math(refs[i][...], refs[n + i][...], refs[2 * n + i][...],
                                   refs[3 * n + i][...])
            refs[4 * n + i][...] = d
            refs[5 * n + i][...] = nm
            refs[6 * n + i][...] = nv

    vm = BS(memory_space=pltpu.VMEM)
    outs = pl.pallas_call(
        body, in_specs=[vm] * (4 * n), out_specs=[vm] * (3 * n),
        out_shape=[_sds(w.shape, F32) for w in ws] * 3,
        compiler_params=_cp(), name="adamw_small",
    )(*ws, *gs, *ms, *vs)
    return outs[:n], outs[n:2 * n], outs[2 * n:]


def _place():
    x, y, c = lax.axis_index("x"), lax.axis_index("y"), lax.axis_index("c")
    chips = [(1 - x, y), (x, 1 - y), (1 - x, 1 - y)]
    return x, y, c, chips


def _gather_weights(srcs, dst_shapes, views, small):
    nu = len(srcs)
    nd = len(dst_shapes)

    def body(*refs):
        src = refs[:nu]
        small_ref = refs[nu]
        dst = refs[nu + 1:nu + 1 + nd]
        small_dst = refs[nu + 1 + nd]
        vbuf = refs[nu + 2 + nd:2 * nu + 2 + nd]
        send, recv, fsend, frecv, lsem, ssend, srecv, vsem = refs[2 * nu + 2 + nd:]
        x, y, c, chips = _place()
        s = 2 * x + y
        vw = [views[u](dst) for u in range(nu)]

        def ici(u, j, shard, to):
            return pltpu.make_async_remote_copy(
                src_ref=vbuf[u].at[:, c], dst_ref=vw[u].at[:, shard, c],
                send_sem=send.at[3 * u + j], recv_sem=recv.at[3 * u + j],
                device_id=to, device_id_type=MESH)

        def fwd(u, j, shard, half):
            return pltpu.make_async_remote_copy(
                src_ref=vw[u].at[:, shard, half], dst_ref=vw[u].at[:, shard, half],
                send_sem=fsend.at[3 * u + j], recv_sem=frecv.at[3 * u + j],
                device_id=(x, y, 1 - c), device_id_type=MESH)

        def small_copy(j, shard, to):
            return pltpu.make_async_remote_copy(
                src_ref=small_ref, dst_ref=small_dst.at[shard],
                send_sem=ssend.at[j], recv_sem=srecv.at[j], device_id=to, device_id_type=MESH)

        stage = [pltpu.make_async_copy(src[u], vbuf[u], vsem.at[u]) for u in range(nu)]
        local = [pltpu.make_async_copy(vbuf[u], vw[u].at[:, s], lsem.at[u]) for u in range(nu)]
        local.append(pltpu.make_async_copy(small_ref, small_dst.at[s], lsem.at[nu]))
        for cp in stage:
            cp.start()
        local[nu].start()
        for j, chip in enumerate(chips):
            small_copy(j, s, (*chip, c)).start()
        for u in range(nu):
            stage[u].wait()
            for j, chip in enumerate(chips):
                ici(u, j, s, (*chip, c)).start()
            local[u].start()
        for u in range(nu):
            for j, chip in enumerate(chips):
                sj = 2 * chip[0] + chip[1]
                ici(u, j, sj, (x, y, c)).wait_recv()
                fwd(u, j, sj, c).start()
        for u in range(nu):
            for j, chip in enumerate(chips):
                sj = 2 * chip[0] + chip[1]
                fwd(u, j, sj, 1 - c).wait_recv()
        for j, chip in enumerate(chips):
            sj = 2 * chip[0] + chip[1]
            small_copy(j, sj, (x, y, c)).wait_recv()
        for u in range(nu):
            for j, chip in enumerate(chips):
                ici(u, j, s, (*chip, c)).wait_send()
                fwd(u, j, s, c).wait_send()
        for j, chip in enumerate(chips):
            small_copy(j, s, (*chip, c)).wait_send()
        for cp in local:
            cp.wait()

    dma = pltpu.SemaphoreType.DMA
    outs = pl.pallas_call(
        body, in_specs=[ANY] * nu + [BS(memory_space=pltpu.VMEM)], out_specs=[ANY] * (nd + 1),
        out_shape=[_sds(sh, BF16) for sh in dst_shapes] + [_sds((NSH,) + small.shape, F32)],
        scratch_shapes=[pltpu.VMEM(a.shape, BF16) for a in srcs]
        + [dma((3 * nu,)), dma((3 * nu,)), dma((3 * nu,)), dma((3 * nu,)),
           dma((nu + 1,)), dma((3,)), dma((3,)), dma((nu,))],
        compiler_params=pltpu.CompilerParams(has_side_effects=True, vmem_limit_bytes=40 << 20),
        name="gather_weights",
    )(*srcs, small)
    return outs[:nd], outs[nd]


def _pair_reduce(g, name):
    pn, _, _, rh, cc = g.shape
    n = pn * NSH

    def body(g_ref, own_ref, sb_ref, sendb, recvb, stage, outf, outb, send, recv, lsem, osem):
        x, y, c, _ = _place()
        s = 2 * x + y

        def load(k, half):
            return pltpu.make_async_copy(g_ref.at[k // NSH, k % NSH, half], stage.at[k % 2], lsem.at[k % 2])

        def push(k):
            return pltpu.make_async_remote_copy(
                src_ref=sendb.at[k], dst_ref=recvb.at[k], send_sem=send.at[k], recv_sem=recv.at[k],
                device_id=(x, y, 1 - c), device_id_type=MESH)

        def store(k):
            return pltpu.make_async_copy(outb.at[k % 2], sb_ref.at[k // NSH, k % NSH], osem.at[k % 2])

        load(0, 1 - c).start()
        for k in range(n):
            if k + 1 < n:
                load(k + 1, 1 - c).start()
            load(k, 1 - c).wait()
            sendb[k] = stage[k % 2].astype(BF16)
            push(k).start()
        load(0, c).start()
        for k in range(n):
            if k + 1 < n:
                load(k + 1, c).start()
            load(k, c).wait()
            push(k).wait_recv()
            total = stage[k % 2] + recvb[k].astype(F32)
            if k >= 2:
                store(k - 2).wait()
            outb[k % 2] = total.astype(BF16)
            store(k).start()

            @pl.when(s == k % NSH)
            def _():
                outf[...] = total
                keep = pltpu.make_async_copy(outf, own_ref.at[k // NSH], osem.at[2])
                keep.start()
                keep.wait()

        for k in range(max(n - 2, 0), n):
            store(k).wait()
        for k in range(n):
            push(k).wait_send()

    dma = pltpu.SemaphoreType.DMA
    return pl.pallas_call(
        body, in_specs=[ANY], out_specs=[ANY, ANY],
        out_shape=[_sds((pn, rh, cc), F32), _sds((pn, NSH, rh, cc), BF16)],
        scratch_shapes=[pltpu.VMEM((n, rh, cc), BF16), pltpu.VMEM((n, rh, cc), BF16),
                        pltpu.VMEM((2, rh, cc), F32), pltpu.VMEM((rh, cc), F32),
                        pltpu.VMEM((2, rh, cc), BF16), dma((n,)), dma((n,)), dma((2,)), dma((3,))],
        compiler_params=pltpu.CompilerParams(has_side_effects=True, vmem_limit_bytes=56 << 20),
        name=name,
    )(g)


def _chip_exchange(sums_bf16):
    nu = len(sums_bf16)

    def body(*refs):
        sb = refs[:nu]
        got = refs[nu:2 * nu]
        send, recv = refs[2 * nu:]
        x, y, c, chips = _place()

        def push(u, j, shard, to):
            return pltpu.make_async_remote_copy(
                src_ref=sb[u].at[:, shard], dst_ref=got[u].at[j], send_sem=send.at[3 * u + j],
                recv_sem=recv.at[3 * u + j], device_id=to, device_id_type=MESH)

        for u in range(nu):
            for j, chip in enumerate(chips):
                push(u, j, 2 * chip[0] + chip[1], (*chip, c)).start()
        for u in range(nu):
            for j, chip in enumerate(chips):
                push(u, j, 2 * chip[0] + chip[1], (*chip, c)).wait()

    dma = pltpu.SemaphoreType.DMA
    shapes = [(3, a.shape[0], a.shape[2], a.shape[3]) for a in sums_bf16]
    return pl.pallas_call(
        body, in_specs=[ANY] * nu, out_specs=[ANY] * nu,
        out_shape=[_sds(sh, BF16) for sh in shapes],
        scratch_shapes=[dma((3 * nu,)), dma((3 * nu,))],
        compiler_params=pltpu.CompilerParams(has_side_effects=True), name="chip_exchange",
    )(*sums_bf16)


def _final_exchange(halves, out_shapes, targets):
    nu = len(halves)
    no = len(out_shapes)
    ncp = sum(len(t) for t in targets)

    def body(*refs):
        hv = refs[:nu]
        out = refs[nu:nu + no]
        sbuf = refs[nu + no:2 * nu + no]
        rbuf = refs[2 * nu + no:3 * nu + no]
        send, recv, lsem, osem, csem = refs[3 * nu + no:]
        x, y, c, _ = _place()
        stage = [pltpu.make_async_copy(hv[u], sbuf[u], lsem.at[u]) for u in range(nu)]
        push = [pltpu.make_async_remote_copy(
            src_ref=sbuf[u], dst_ref=rbuf[u], send_sem=send.at[u], recv_sem=recv.at[u],
            device_id=(x, y, 1 - c), device_id_type=MESH) for u in range(nu)]
        mine, theirs = [], []
        k = 0
        for u in range(nu):
            rh = hv[u].shape[1]
            for (p, oi, li) in targets[u]:
                mine.append((u, pltpu.make_async_copy(
                    sbuf[u].at[p], out[oi].at[li, pl.ds(c * rh, rh), :], csem.at[k])))
                theirs.append((u, pltpu.make_async_copy(
                    rbuf[u].at[p], out[oi].at[li, pl.ds((1 - c) * rh, rh), :], osem.at[k])))
                k += 1
        for cp in stage:
            cp.start()
        for u in range(nu):
            stage[u].wait()
            push[u].start()
            for (v, cp) in mine:
                if v == u:
                    cp.start()
        for u in range(nu):
            push[u].wait_recv()
            for (v, cp) in theirs:
                if v == u:
                    cp.start()
        for (_, cp) in theirs + mine:
            cp.wait()
        for u in range(nu):
            push[u].wait_send()

    dma = pltpu.SemaphoreType.DMA
    bufs = [pltpu.VMEM(h.shape, F32) for h in halves]
    return pl.pallas_call(
        body, in_specs=[ANY] * nu, out_specs=[ANY] * no,
        out_shape=[_sds(sh, F32) for sh in out_shapes],
        scratch_shapes=bufs + bufs + [dma((nu,)), dma((nu,)), dma((nu,)), dma((ncp,)), dma((ncp,))],
        compiler_params=pltpu.CompilerParams(has_side_effects=True, vmem_limit_bytes=56 << 20),
        name="final_exchange",
    )(*halves)


def _small_allreduce(pack):
    rows = pack.shape[0]

    def body(p_ref, o_ref, buf, send, recv):
        x, y, c, _ = _place()
        me = 4 * x + 2 * y + c
        buf[me] = p_ref[...]
        k = 0
        copies = []
        for dx in range(2):
            for dy in range(2):
                for dc in range(2):
                    if dx == 0 and dy == 0 and dc == 0:
                        continue
                    to = (jnp.where(dx, 1 - x, x), jnp.where(dy, 1 - y, y), jnp.where(dc, 1 - c, c))
                    src_slot = 4 * to[0] + 2 * to[1] + to[2]
                    copies.append((pltpu.make_async_remote_copy(
                        src_ref=p_ref, dst_ref=buf.at[me], send_sem=send.at[k], recv_sem=recv.at[k],
                        device_id=to, device_id_type=MESH), src_slot, k))
                    k += 1
        for cp, _, _ in copies:
            cp.start()
        for cp, src_slot, k in copies:
            pltpu.make_async_remote_copy(
                src_ref=p_ref, dst_ref=buf.at[src_slot], send_sem=send.at[k], recv_sem=recv.at[k],
                device_id=(x, y, c), device_id_type=MESH).wait()
        acc = buf[0]
        for d in range(1, 8):
            acc = acc + buf[d]
        o_ref[...] = acc

    dma = pltpu.SemaphoreType.DMA
    vm = BS(memory_space=pltpu.VMEM)
    return pl.pallas_call(
        body, in_specs=[vm], out_specs=vm, out_shape=_sds((rows, D), F32),
        scratch_shapes=[pltpu.VMEM((8, rows, D), F32), dma((7,)), dma((7,))],
        compiler_params=pltpu.CompilerParams(has_side_effects=True, vmem_limit_bytes=32 << 20),
        name="small_allreduce",
    )(pack)


def _in_proj(h, w, bias, name):
    n = w.shape[1]
    tn = 640 if n == NA else 896
    ep = None
    extras, especs = (), ()
    if bias is not None:
        def ep(acc, ex, o_ref):
            o_ref[...] = acc + ex[0][...]
        extras = (bias,)
        especs = (BS((1, tn), lambda i, j, k: (0, j)),)
    return _mm("nn", h, w, grid=(T // TR, n // tn, 1),
               a_spec=BS((TR, D), lambda i, j, k: (i, 0)), b_spec=BS((D, tn), lambda i, j, k: (0, j)),
               out_shape=_sds((T, n), F32), out_spec=BS((TR, tn), lambda i, j, k: (i, j)),
               acc_shape=(TR, tn), extras=extras, extra_specs=especs, epilogue=ep, name=name)


def _add_res(acc, ex, o_ref):
    o_ref[...] = acc + ex[0][...]


def _local_step(x, mem, target, w, p):
    row = lambda i, j, k: (i, 0)
    saved = []
    bias = _bias_expand(p["rel_u"])
    for l in range(2):
        type_a = l == 0
        h = _rms_fwd(x, p["norm1_g"][l:l + 1], f"rms1_{l}")
        memn = _rms_fwd(mem, p["mem_norm_g"][l:l + 1], f"rmsmem_{l}")
        kv = _mm("nn", memn, w["kv"], grid=(1, 1, 1),
                 a_spec=BS((NB * MEMT, D), lambda i, j, k: (0, 0)),
                 b_spec=BS((None, D, 2 * MEMW), lambda i, j, k: (l, 0, 0)),
                 out_shape=_sds((NB * MEMT, 2 * MEMW), F32),
                 out_spec=BS((NB * MEMT, 2 * MEMW), lambda i, j, k: (0, 0)),
                 acc_shape=(8, 128), name=f"memkv_{l}")
        if type_a:
            z = _in_proj(h, w["a"], None, "inproj_a")
            cat = _attn_fwd(z, bias, p["a_q_g2"], p["a_k_g2"])
            qcol = NA // MEMW - 1
        else:
            z = _in_proj(h, w["b"], p["b_b_in"], "inproj_b")
            cat = _conv_fwd(z, p["conv_w"], p["conv_b"], p["ln_g"], p["ln_b"])
            qcol = NBW // MEMW - 1
        cat = _memattn_fwd(z, kv, cat, p["mq_g4"][l:l + 1], p["mk_g4"][l:l + 1], qcol, f"memattn_fwd_{l}")
        x1 = _mm("nn", cat, w["wo"], grid=(T // TR, 1, 1), a_spec=BS((TR, D), row),
                 b_spec=BS((None, D, D), lambda i, j, k: (l, 0, 0)),
                 out_shape=_sds((T, D), F32), out_spec=BS((TR, D), row), acc_shape=(8, 128),
                 extras=(x,), extra_specs=(BS((TR, D), row),), epilogue=_add_res, name=f"outproj_{l}")
        h2 = _rms_fwd(x1, p["norm2_g"][l:l + 1], f"rms2_{l}")
        g, u, act = _ffn_up(h2, w["gu"], l)
        x2 = _mm("nn", act, w["wd"], grid=(T // TR, 1, NSH),
                 a_spec=BS((None, TR, FS), lambda i, j, k: (k, i, 0)),
                 b_spec=BS((None, None, FS, D), lambda i, j, k: (l, k, 0, 0)),
                 out_shape=_sds((T, D), F32), out_spec=BS((TR, D), row), acc_shape=(TR, D),
                 extras=(x1,), extra_specs=(BS((TR, D), row),), epilogue=_add_res, name=f"ffn_down_{l}")
        saved.append(dict(x=x, h=h, memn=memn, kv=kv, z=z, cat=cat, x1=x1, h2=h2, g=g, u=u, act=act,
                          qcol=qcol))
        x = x2

    loss, dx = _loss_head(x, target)

    big = dict(a=None, b=None, kv=None, wo=None, gu=None, wd=None)
    small = {}
    tk = T // 2
    nkt = T // tk
    for l in (1, 0):
        sv = saved[l]
        dgu = _ffn_down_bwd(dx, w["wd"], sv["g"], sv["u"], l)
        dgu8 = dgu.reshape(2 * NSH, T, FS)
        big["wd"] = _mm("tn", sv["act"], dx, grid=(NSH, 1, nkt),
                        a_spec=BS((None, tk, FS), lambda i, j, k: (i, k, 0)),
                        b_spec=BS((tk, D), lambda i, j, k: (k, 0)),
                        out_shape=_sds((2, NSH, FS, D), F32),
                        out_spec=BS((None, None, FS, D), lambda i, j, k: (l, i, 0, 0)),
                        acc_shape=(FS, D), into=big["wd"], name=f"dw_down_{l}")
        dh2 = _mm("nt", dgu8, w["gu"], grid=(T // TR, 1, 2 * NSH),
                  a_spec=BS((None, TR, FS), lambda i, j, k: (k, i, 0)),
                  b_spec=BS((None, None, D, FS), lambda i, j, k: (l, k, 0, 0)),
                  out_shape=_sds((T, D), F32), out_spec=BS((TR, D), row), acc_shape=(TR, D),
                  name=f"dh2_{l}")
        big["gu"] = _mm("tn", sv["h2"], dgu8, grid=(2 * NSH, 1, nkt),
                        a_spec=BS((tk, D), lambda i, j, k: (k, 0)),
                        b_spec=BS((None, tk, FS), lambda i, j, k: (i, k, 0)),
                        out_shape=_sds((2, 2 * NSH, D, FS), F32),
                        out_spec=BS((None, None, D, FS), lambda i, j, k: (l, i, 0, 0)),
                        acc_shape=(D, FS), into=big["gu"], name=f"dw_gu_{l}")
        dx1, small[f"norm2_g{l}"] = _rms_bwd(dh2, sv["x1"], p["norm2_g"][l:l + 1], dx, f"rms2_bwd_{l}")
        dcat = _mm("nt", dx1, w["wo"], grid=(T // TR, 1, 1), a_spec=BS((TR, D), row),
                   b_spec=BS((None, D, D), lambda i, j, k: (l, 0, 0)),
                   out_shape=_sds((T, D), F32), out_spec=BS((TR, D), row), acc_shape=(8, 128),
                   name=f"dcat_{l}")
        big["wo"] = _mm("tn", sv["cat"], dx1, grid=(1, 1, nkt),
                        a_spec=BS((tk, D), lambda i, j, k: (k, 0)), b_spec=BS((tk, D), lambda i, j, k: (k, 0)),
                        out_shape=_sds((2, D, D), F32), out_spec=BS((None, D, D), lambda i, j, k: (l, 0, 0)),
                        acc_shape=(D, D), into=big["wo"], name=f"dw_out_{l}")
        dqm, dkv, small[f"mq_g{l}"], small[f"mk_g{l}"] = _memattn_bwd(
            sv["z"], sv["kv"], dcat, p["mq_g4"][l:l + 1], p["mk_g4"][l:l + 1], sv["qcol"], f"memattn_bwd_{l}")
        if l == 0:
            dq, dk, dv, dbias, small["a_q_g"], small["a_k_g"] = _attn_bwd(
                sv["z"], dcat, bias, p["a_q_g2"], p["a_k_g2"])
            small["rel_u"] = _bias_reduce(dbias)
            dz = jnp.concatenate([dq, dk, dv, dqm], axis=1)
            w_in, key, n, tn = w["a"], "a", NA, 640
        else:
            du, small["conv_w"], small["conv_b"], small["ln_g"], small["ln_b"], dbin_u = _conv_bwd(
                sv["z"], dcat, p["conv_w"], p["conv_b"], p["ln_g"], p["ln_b"])
            dz = jnp.concatenate([du, dqm], axis=1)
            small["b_in_u"] = dbin_u
            w_in, key, n, tn = w["b"], "b", NBW, 896
        dh = _mm("nt", dz, w_in, grid=(T // TR, 1, 1),
                 a_spec=BS((TR, n), lambda i, j, k: (i, 0)), b_spec=BS((D, n), lambda i, j, k: (0, 0)),
                 out_shape=_sds((T, D), F32), out_spec=BS((TR, D), row), acc_shape=(8, 128),
                 name=f"dh_{l}")
        big[key] = _mm("tn", sv["h"], dz, grid=(1, n // tn, nkt),
                       a_spec=BS((tk, D), lambda i, j, k: (k, 0)), b_spec=BS((tk, tn), lambda i, j, k: (k, j)),
                       out_shape=_sds((D, n), F32), out_spec=BS((D, tn), lambda i, j, k: (0, j)),
                       acc_shape=(D, tn), name=f"dw_in_{l}")
        if l == 1:
            small["b_in_qm"] = _colsum(dqm, "colsum_dqm")
        dx, small[f"norm1_g{l}"] = _rms_bwd(dh, sv["x"], p["norm1_g"][l:l + 1], dx1, f"rms1_bwd_{l}")
        big["kv"] = _mm("tn", sv["memn"], dkv, grid=(1, 1, 1),
                        a_spec=BS((NB * MEMT, D), lambda i, j, k: (0, 0)),
                        b_spec=BS((NB * MEMT, 2 * MEMW), lambda i, j, k: (0, 0)),
                        out_shape=_sds((2, D, 2 * MEMW), F32),
                        out_spec=BS((None, D, 2 * MEMW), lambda i, j, k: (l, 0, 0)),
                        acc_shape=(8, 128), into=big["kv"], name=f"dw_kv_{l}")
        dmemn = _mm("nt", dkv, w["kv"], grid=(1, 1, 1),
                    a_spec=BS((NB * MEMT, 2 * MEMW), lambda i, j, k: (0, 0)),
                    b_spec=BS((None, D, 2 * MEMW), lambda i, j, k: (l, 0, 0)),
                    out_shape=_sds((NB * MEMT, D), F32), out_spec=BS((NB * MEMT, D), lambda i, j, k: (0, 0)),
                    acc_shape=(8, 128), name=f"dmemn_{l}")
        _, small[f"mem_norm_g{l}"] = _rms_bwd(dmemn, mem, p["mem_norm_g"][l:l + 1], None, f"rmsmem_bwd_{l}")
    return loss, dx, big, small


def _colsum(a, name):
    rows, cols = a.shape

    def body(a_ref, o_ref):
        @pl.when(pl.program_id(0) == 0)
        def _():
            o_ref[...] = jnp.zeros_like(o_ref)

        o_ref[...] += jnp.sum(a_ref[...].astype(F32), axis=0, keepdims=True)

    return pl.pallas_call(
        body, grid=(rows // TR,), in_specs=[BS((TR, cols), lambda i: (i, 0))],
        out_specs=BS((1, cols), lambda i: (0, 0)), out_shape=_sds((1, cols), F32),
        compiler_params=_cp(("arbitrary",)), name=name,
    )(a)


_PACK_ROWS = 64


def _pad_to(a, rows, cols=D):
    return jnp.pad(a, ((0, rows - a.shape[0]), (0, cols - a.shape[1])))


def _pack_small(sm):
    parts = [
        jnp.concatenate([sm["norm1_g0"], sm["norm1_g1"]], 0),
        jnp.concatenate([sm["mem_norm_g0"], sm["mem_norm_g1"]], 0),
        jnp.concatenate([sm["norm2_g0"], sm["norm2_g1"]], 0),
        _pad_to(sm["a_q_g"], 1), _pad_to(sm["a_k_g"], 1),
        _pad_to(jnp.concatenate([sm["mq_g0"], sm["mq_g1"]], 0), 2),
        _pad_to(jnp.concatenate([sm["mk_g0"], sm["mk_g1"]], 0), 2),
        _pad_to(sm["conv_b"], 1), _pad_to(sm["ln_g"], 1), _pad_to(sm["ln_b"], 1),
        _pad_to(sm["b_in_u"][:, :D], 1), _pad_to(sm["b_in_u"][:, D:], 1),
        _pad_to(sm["b_in_qm"], 1),
        _pad_to(sm["conv_w"][:CONVW], CONVW),
        sm["rel_u"].reshape(12, D),
    ]
    pack = jnp.concatenate(parts, 0)
    return jnp.pad(pack, ((0, _PACK_ROWS - pack.shape[0]), (0, 0)))


def _rel_table_to_u(rel_bias):
    flat = jnp.concatenate([jnp.broadcast_to(rel_bias[:, 191:192], (12, 447)), rel_bias[:, ::-1]], axis=1)
    return jnp.pad(flat, ((0, 0), (192, 1024 - 192 - 639))).reshape(12, 1, 1024)


def _u_to_rel_table(du):
    flat = du[:, 192:192 + 639]
    g = flat[:, 447:][:, ::-1]
    return g, flat[:, :447]


def kernel(x, mem, norm1_g, mem_norm_g, a_w_in, a_q_g, a_k_g, a_rel_bias, b_w_in, b_b_in, b_conv_w, b_conv_b, b_ln_g, b_ln_b, mq_g, mk_g, w_mem_kv, w_out, norm2_g, w_gate, w_up, w_down, loss_target, m_norm1_g, m_mem_norm_g, m_a_w_in, m_a_q_g, m_a_k_g, m_a_rel_bias, m_b_w_in, m_b_b_in, m_b_conv_w, m_b_conv_b, m_b_ln_g, m_b_ln_b, m_mq_g, m_mk_g, m_w_mem_kv, m_w_out, m_norm2_g, m_w_gate, m_w_up, m_w_down, v_norm1_g, v_mem_norm_g, v_a_w_in, v_a_q_g, v_a_k_g, v_a_rel_bias, v_b_w_in, v_b_b_in, v_b_conv_w, v_b_conv_b, v_b_ln_g, v_b_ln_b, v_mq_g, v_mk_g, v_w_mem_kv, v_w_out, v_norm2_g, v_w_gate, v_w_up, v_w_down):
    sx = 2 * lax.axis_index("x") + lax.axis_index("y")

    src_a = _cast_bf16([a_w_in[0]], "cast_a").reshape(1, 2, D // 2, NA // NSH)
    src_b = _cast_bf16([b_w_in[0]], "cast_b").reshape(1, 2, D // 2, NBW // NSH)
    src_kv = _cast_bf16([w_mem_kv[0], w_mem_kv[1]], "cast_kv").reshape(2, 2, 128, 2 * MEMW)
    src_wo = _cast_bf16([w_out[0], w_out[1]], "cast_wo").reshape(2, 2, 128, D)
    src_g = _cast_bf16([w_gate[0], w_gate[1]], "cast_gate").reshape(2, 2, D // 2, FS)
    src_u = _cast_bf16([w_up[0], w_up[1]], "cast_up").reshape(2, 2, D // 2, FS)
    src_wd = _cast_bf16([w_down[0], w_down[1]], "cast_wd").reshape(2, 2, FS // 2, D)
    small_src = jnp.concatenate([
        jnp.pad(b_b_in, ((0, 0), (0, 512 - 448))),
        jnp.pad(b_conv_w[0], ((0, 0), (0, 512 - 192))),
        jnp.pad(jnp.concatenate([b_conv_b, b_ln_g, b_ln_b], 0), ((0, 0), (0, 512 - 192))),
        jnp.zeros((5, 512), F32)], 0)
    dst_shapes = [(1, NSH, 2, D // 2, NA // NSH), (1, NSH, 2, D // 2, NBW // NSH),
                  (2, NSH, 2, 128, 2 * MEMW), (2, NSH, 2, 128, D),
                  (2, 2, NSH, 2, D // 2, FS), (2, NSH, 2, FS // 2, D)]
    views = [lambda d: d[0], lambda d: d[1], lambda d: d[2], lambda d: d[3],
             lambda d: d[4].at[:, 0], lambda d: d[4].at[:, 1], lambda d: d[5]]
    (ga, gb, gkv, gwo, ggu, gwd), small_all = _gather_weights(
        [src_a, src_b, src_kv, src_wo, src_g, src_u, src_wd], dst_shapes, views, small_src)
    w = dict(
        a=ga.reshape(NSH, D, NA // NSH).transpose(1, 0, 2).reshape(D, NA),
        b=gb.reshape(NSH, D, NBW // NSH).transpose(1, 0, 2).reshape(D, NBW),
        kv=gkv.reshape(2, D, 2 * MEMW), wo=gwo.reshape(2, D, D),
        gu=ggu.reshape(2, 2 * NSH, D, FS), wd=gwd.reshape(2, NSH, FS, D))

    conv_w_full = small_all[:, 1:1 + CONVW, :192].transpose(1, 0, 2).reshape(CONVW, TOK)
    vec3 = small_all[:, 32:35, :192].transpose(1, 0, 2).reshape(3, TOK)
    p = dict(
        norm1_g=norm1_g, mem_norm_g=mem_norm_g, norm2_g=norm2_g,
        a_q_g2=jnp.tile(a_q_g, (1, 2)), a_k_g2=jnp.tile(a_k_g, (1, 2)),
        mq_g4=jnp.tile(mq_g, (1, 4)), mk_g4=jnp.tile(mk_g, (1, 4)),
        rel_u=_rel_table_to_u(a_rel_bias[0]),
        b_b_in=small_all[:, 0, :448].reshape(1, NBW),
        conv_w=jnp.pad(conv_w_full, ((0, 1), (0, 0))),
        conv_b=vec3[0:1], ln_g=vec3[1:2], ln_b=vec3[2:3])

    loss, grad_x, big, small = _local_step(
        x.reshape(T, D), mem.reshape(NB * MEMT, D), loss_target.reshape(T, D), w, p)
    loss = lax.psum(loss[0, 0], ("x", "y", "c"))

    units = [
        big["a"].reshape(D, NSH, NA // NSH).transpose(1, 0, 2).reshape(1, NSH, 2, D // 2, NA // NSH),
        big["b"].reshape(D, NSH, NBW // NSH).transpose(1, 0, 2).reshape(1, NSH, 2, D // 2, NBW // NSH),
        big["kv"].reshape(2, NSH, 2, 128, 2 * MEMW),
        big["wo"].reshape(2, NSH, 2, 128, D),
        big["gu"].reshape(4, NSH, 2, D // 2, FS),
        big["wd"].reshape(2, NSH, 2, FS // 2, D),
    ]
    own, sums_b = [], []
    for u in range(len(units)):
        of, sb = _pair_reduce(units[u], f"pair_reduce_{u}")
        own.append(of)
        sums_b.append(sb)
    parts = _chip_exchange(sums_b)
    halves = [_quad_sum(own[u], parts[u], f"quad_sum_{u}") for u in range(len(units))]
    out_shapes = [(1, D, NA // NSH), (1, D, NBW // NSH), (2, 2 * 128, 2 * MEMW), (2, 2 * 128, D),
                  (2, D, FS), (2, D, FS), (2, FS, D)]
    targets = [[(0, 0, 0)], [(0, 1, 0)], [(0, 2, 0), (1, 2, 1)], [(0, 3, 0), (1, 3, 1)],
               [(0, 4, 0), (1, 5, 0), (2, 4, 1), (3, 5, 1)], [(0, 6, 0), (1, 6, 1)]]
    g_a, g_b, g_kv, g_wo, g_gate, g_up, g_wd = _final_exchange(halves, out_shapes, targets)

    tot = _small_allreduce(_pack_small(small))
    g_rel, clip_part = _u_to_rel_table(tot[49:61])
    g_rel = jnp.concatenate([g_rel[:, :191], g_rel[:, 191:] + _rowsum(clip_part)], axis=1)
    b_in_full = jnp.concatenate([tot[15:16], tot[16:17, :512], tot[17:18, :MEMW]], axis=1)
    g_small = dict(
        norm1_g=tot[0:2], mem_norm_g=tot[2:4], norm2_g=tot[4:6],
        a_q_g=tot[6:7, :HD], a_k_g=tot[7:8, :HD], a_rel_bias=g_rel[None],
        b_b_in=lax.dynamic_slice(b_in_full, (0, sx * 448), (1, 448)),
        b_conv_w=lax.dynamic_slice(tot[18:49, :TOK], (0, sx * 192), (CONVW, 192))[None],
        b_conv_b=lax.dynamic_slice(tot[12:13, :TOK], (0, sx * 192), (1, 192)),
        b_ln_g=lax.dynamic_slice(tot[13:14, :TOK], (0, sx * 192), (1, 192)),
        b_ln_b=lax.dynamic_slice(tot[14:15, :TOK], (0, sx * 192), (1, 192)),
        mq_g=tot[8:10, :HD], mk_g=tot[10:12, :HD])

    names = ["norm1_g", "mem_norm_g", "a_w_in", "a_q_g", "a_k_g", "a_rel_bias", "b_w_in", "b_b_in",
             "b_conv_w", "b_conv_b", "b_ln_g", "b_ln_b", "mq_g", "mk_g", "w_mem_kv", "w_out",
             "norm2_g", "w_gate", "w_up", "w_down"]
    weights = dict(zip(names, [norm1_g, mem_norm_g, a_w_in, a_q_g, a_k_g, a_rel_bias, b_w_in, b_b_in,
                               b_conv_w, b_conv_b, b_ln_g, b_ln_b, mq_g, mk_g, w_mem_kv, w_out,
                               norm2_g, w_gate, w_up, w_down]))
    ms = dict(zip(names, [m_norm1_g, m_mem_norm_g, m_a_w_in, m_a_q_g, m_a_k_g, m_a_rel_bias, m_b_w_in,
                          m_b_b_in, m_b_conv_w, m_b_conv_b, m_b_ln_g, m_b_ln_b, m_mq_g, m_mk_g,
                          m_w_mem_kv, m_w_out, m_norm2_g, m_w_gate, m_w_up, m_w_down]))
    vs = dict(zip(names, [v_norm1_g, v_mem_norm_g, v_a_w_in, v_a_q_g, v_a_k_g, v_a_rel_bias, v_b_w_in,
                          v_b_b_in, v_b_conv_w, v_b_conv_b, v_b_ln_g, v_b_ln_b, v_mq_g, v_mk_g,
                          v_w_mem_kv, v_w_out, v_norm2_g, v_w_gate, v_w_up, v_w_down]))
    grads = dict(g_small)
    grads.update(a_w_in=g_a, b_w_in=g_b, w_mem_kv=g_kv, w_out=g_wo, w_gate=g_gate, w_up=g_up, w_down=g_wd)
    big_names = ["a_w_in", "b_w_in", "w_mem_kv", "w_out", "w_gate", "w_up", "w_down"]
    small_names = [n for n in names if n not in big_names]
    delta, new_m, new_v = {}, {}, {}
    for n in big_names:
        delta[n], new_m[n], new_v[n] = _adamw_big(weights[n], grads[n], ms[n], vs[n], f"adamw_{n}")
    as2d = lambda a: a.reshape(-1, a.shape[-1])
    d_s, m_s, v_s = _adamw_small([as2d(weights[n]) for n in small_names], [as2d(grads[n]) for n in small_names],
                                 [as2d(ms[n]) for n in small_names], [as2d(vs[n]) for n in small_names])
    for i, n in enumerate(small_names):
        delta[n] = d_s[i].reshape(weights[n].shape)
        new_m[n] = m_s[i].reshape(weights[n].shape)
        new_v[n] = v_s[i].reshape(weights[n].shape)

    return (loss, grad_x.reshape(NB, SEQ, D), *[grads[n] for n in names], *[delta[n] for n in names],
            *[new_m[n] for n in names], *[new_v[n] for n in names])


def _rowsum(a):
    def body(a_ref, o_ref):
        o_ref[...] = jnp.sum(a_ref[...], axis=1, keepdims=True)

    vm = BS(memory_space=pltpu.VMEM)
    return pl.pallas_call(body, in_specs=[vm], out_specs=vm, out_shape=_sds((a.shape[0], 1), F32),
                          compiler_params=_cp(), name="rowsum")(a)
```

```python
import functools

import jax
import jax.numpy as jnp
from jax import lax
from jax.experimental import pallas as pl
from jax.experimental.pallas import tpu as pltpu

F32 = jnp.float32
BF16 = jnp.bfloat16
BS = pl.BlockSpec
ANY = pl.BlockSpec(memory_space=pl.ANY)
MESH = pl.DeviceIdType.MESH

D = 1024
SEQ = 2048
NB = 2
T = NB * SEQ
MEMT = 256
HD = 64
TOK = 768
MEMW = 256
NA = 3 * TOK + MEMW
NBW = 2 * TOK + MEMW
FF = 2816
NSH = 4
FS = FF // NSH
CONVW = 31
EPS = 1e-6
NEG = -1e30
SCALE = HD ** -0.5
QB = 256
KWIN = 768
KPAD = 512
TR = 512

ADAM_LR = 0.001
ADAM_B1 = 0.9
ADAM_B2 = 0.999
ADAM_EPS = 1e-08
ADAM_WD = 0.01
ADAM_STEP = 10

_DIMS = {
    "nn": (((1,), (0,)), ((), ())),
    "nt": (((1,), (1,)), ((), ())),
    "tn": (((0,), (0,)), ((), ())),
}


def _cp(sem=None, vmem_mb=48):
    return pltpu.CompilerParams(dimension_semantics=sem, vmem_limit_bytes=vmem_mb << 20)


def _sds(shape, dtype):
    return jax.ShapeDtypeStruct(tuple(shape), dtype)


def _mm(mode, a, b, *, grid, a_spec, b_spec, out_shape, out_spec, acc_shape, name,
        extras=(), extra_specs=(), epilogue=None):
    n_ex = len(extras)
    nk = grid[2]
    dims = _DIMS[mode]

    def body(a_ref, b_ref, *rest):
        ex = rest[:n_ex]
        o_ref = rest[n_ex]
        acc = rest[-1]
        k = pl.program_id(2)
        prod = lax.dot_general(a_ref[...].astype(BF16), b_ref[...].astype(BF16), dims,
                               preferred_element_type=F32)

        def finish(val):
            if epilogue is None:
                o_ref[...] = val.astype(o_ref.dtype)
            else:
                epilogue(val, ex, o_ref)

        if nk == 1:
            finish(prod)
        else:
            @pl.when(k == 0)
            def _():
                acc[...] = prod

            @pl.when((k > 0) & (k < nk - 1))
            def _():
                acc[...] += prod

            @pl.when(k == nk - 1)
            def _():
                finish(acc[...] + prod)

    return pl.pallas_call(
        body, grid=grid, in_specs=[a_spec, b_spec, *extra_specs], out_specs=out_spec,
        out_shape=out_shape,
        scratch_shapes=[pltpu.VMEM(acc_shape if nk > 1 else (8, 128), F32)],
        compiler_params=_cp(("parallel", "parallel", "arbitrary")), name=name,
    )(a, b, *extras)


def _rms_fwd(x, g, name):
    rows = x.shape[0]

    def body(x_ref, g_ref, o_ref):
        xv = x_ref[...]
        r = lax.rsqrt(jnp.mean(xv * xv, axis=-1, keepdims=True) + EPS)
        o_ref[...] = (xv * r * g_ref[...]).astype(BF16)

    return pl.pallas_call(
        body, grid=(rows // TR,),
        in_specs=[BS((TR, D), lambda i: (i, 0)), BS((1, D), lambda i: (0, 0))],
        out_specs=BS((TR, D), lambda i: (i, 0)), out_shape=_sds((rows, D), BF16),
        compiler_params=_cp(("arbitrary",)), name=name,
    )(x, g)


def _rms_bwd(dh, x, g, dres, name):
    rows = x.shape[0]
    has_res = dres is not None

    def body(*refs):
        if has_res:
            dh_ref, x_ref, g_ref, r_ref, dx_ref, dg_ref = refs
        else:
            dh_ref, x_ref, g_ref, dx_ref, dg_ref = refs
        xv = x_ref[...]
        dhv = dh_ref[...]
        r = lax.rsqrt(jnp.mean(xv * xv, axis=-1, keepdims=True) + EPS)
        xh = xv * r
        gy = dhv * g_ref[...]
        dx = r * (gy - xh * jnp.mean(gy * xh, axis=-1, keepdims=True))
        if has_res:
            dx = dx + r_ref[...]
        dx_ref[...] = dx

        @pl.when(pl.program_id(0) == 0)
        def _():
            dg_ref[...] = jnp.zeros_like(dg_ref)

        dg_ref[...] += jnp.sum(dhv * xh, axis=0, keepdims=True)

    row = BS((TR, D), lambda i: (i, 0))
    vec = BS((1, D), lambda i: (0, 0))
    ins = [dh, x, g] + ([dres] if has_res else [])
    return pl.pallas_call(
        body, grid=(rows // TR,),
        in_specs=[row, row, vec] + ([row] if has_res else []),
        out_specs=[row, vec], out_shape=[_sds((rows, D), F32), _sds((1, D), F32)],
        compiler_params=_cp(("arbitrary",)), name=name,
    )(*ins)


def _group_masks(width):
    lane = lax.broadcasted_iota(jnp.int32, (1, width), 1)
    return [(lane >= HD * h) & (lane < HD * (h + 1)) for h in range(width // HD)]


def _group_mean(v, masks):
    out = jnp.zeros_like(v)
    for m in masks:
        s = jnp.sum(jnp.where(m, v, 0.0), axis=-1, keepdims=True) * (1.0 / HD)
        out = jnp.where(m, s, out)
    return out


def _head_norm(zv, g, masks):
    r = lax.rsqrt(_group_mean(zv * zv, masks) + EPS)
    return zv * r * g


def _head_norm_bwd(dy, zv, g, masks):
    r = lax.rsqrt(_group_mean(zv * zv, masks) + EPS)
    zh = zv * r
    gy = dy * g
    dz = r * (gy - zh * _group_mean(gy * zh, masks))
    return dz, jnp.sum(dy * zh, axis=0, keepdims=True)


def _fold_heads(v, width):
    vb = jnp.broadcast_to(v, (8, width))
    out = vb
    for h in range(1, width // HD):
        out = out + pltpu.roll(vb, width - HD * h, axis=1)
    return out[0:1]


def _bias_expand(u):
    def body(u_ref, o_ref):
        x = jnp.broadcast_to(u_ref[...], (QB, 1024))
        rolled = pltpu.roll(x, 1024 - (QB - 1), axis=1, stride=1, stride_axis=0)[:, :KWIN]
        row = lax.broadcasted_iota(jnp.int32, (QB, 1), 0)
        col = lax.broadcasted_iota(jnp.int32, (1, KWIN), 1)
        lo = (row // 64) * 64
        ok = (col >= lo) & (col < lo + 576)
        o_ref[...] = jnp.where(ok, rolled, NEG)

    return pl.pallas_call(
        body, grid=(12,), in_specs=[BS((None, 1, 1024), lambda h: (h, 0, 0))],
        out_specs=BS((None, QB, KWIN), lambda h: (h, 0, 0)), out_shape=_sds((12, QB, KWIN), F32),
        compiler_params=_cp(("arbitrary",)), name="bias_expand",
    )(u)


def _bias_reduce(ds):
    def body(d_ref, o_ref):
        ri = lax.broadcasted_iota(jnp.int32, (QB, QB), 0)
        ci = lax.broadcasted_iota(jnp.int32, (QB, QB), 1)
        flip = (ri + ci == QB - 1).astype(F32)
        drev = jnp.dot(flip, d_ref[...], precision=lax.Precision.HIGHEST, preferred_element_type=F32)
        x = jnp.concatenate([drev, jnp.zeros((QB, 1024 - KWIN), F32)], axis=1)
        rolled = pltpu.roll(x, 0, axis=1, stride=1, stride_axis=0)
        o_ref[...] = jnp.sum(rolled, axis=0, keepdims=True)

    return pl.pallas_call(
        body, grid=(12,), in_specs=[BS((None, QB, KWIN), lambda h: (h, 0, 0))],
        out_specs=BS((None, 1, 1024), lambda h: (h, 0, 0)), out_shape=_sds((12, 1, 1024), F32),
        compiler_params=_cp(("arbitrary",)), name="bias_reduce",
    )(ds)


def _attn_softmax(qh, kw, bias, startadd):
    s = lax.dot_general(qh, kw, _DIMS["nt"], preferred_element_type=F32) + bias + startadd
    m = jnp.max(s, axis=-1, keepdims=True)
    p = jnp.exp(s - m)
    return p * (1.0 / jnp.sum(p, axis=-1, keepdims=True))


def _attn_fwd(z, bias, gq2, gk2, carry=None):
    ni = len(carry.ins) if carry else 0
    no = len(carry.out_shapes) if carry else 0

    def body(q_ref, k_ref, v_ref, b_ref, gq_ref, gk_ref, *rest):
        cin, o_ref, cout = rest[:ni], rest[ni], rest[ni + 1:ni + 1 + no]
        qn_s, kn_s, v_s = rest[ni + 1 + no:ni + 4 + no]
        cscr = rest[ni + 4 + no:]
        if carry:
            @pl.when((pl.program_id(0) == 0) & (pl.program_id(1) == 0))
            def _():
                carry.start(cin, cout, cscr)

        masks = _group_masks(128)
        qn_s[...] = (_head_norm(q_ref[...], gq_ref[...], masks) * SCALE).astype(BF16)
        kn_s[0:KPAD, :] = jnp.zeros((KPAD, 128), BF16)
        kn_s[KPAD:, :] = _head_norm(k_ref[...], gk_ref[...], masks).astype(BF16)
        v_s[0:KPAD, :] = jnp.zeros((KPAD, 128), BF16)
        v_s[KPAD:, :] = v_ref[...].astype(BF16)
        col = lax.broadcasted_iota(jnp.int32, (1, KWIN), 1)

        def blk(i, carry):
            r0 = pl.multiple_of(i * QB, QB)
            qb = qn_s[pl.ds(r0, QB), :]
            kw = kn_s[pl.ds(r0, KWIN), :]
            vw = v_s[pl.ds(r0, KWIN), :]
            startadd = jnp.where(col + r0 < KPAD, NEG, 0.0)
            o = jnp.zeros((QB, 128), F32)
            for h in range(2):
                qh = jnp.where(masks[h], qb, jnp.zeros_like(qb))
                vh = jnp.where(masks[h], vw, jnp.zeros_like(vw))
                p = _attn_softmax(qh, kw, b_ref[h], startadd).astype(BF16)
                o = o + jnp.dot(p, vh, preferred_element_type=F32)
            o_ref[pl.ds(r0, QB), :] = o.astype(BF16)
            return carry

        lax.fori_loop(0, SEQ // QB, blk, 0)

        if carry:
            @pl.when((pl.program_id(0) == NB - 1) & (pl.program_id(1) == 5))
            def _():
                carry.finish(cin, cout, cscr)

    vec = BS((1, 128), lambda b, hp: (0, 0))
    outs = pl.pallas_call(
        body, grid=(NB, 6),
        in_specs=[BS((SEQ, 128), lambda b, hp: (b, hp)),
                  BS((SEQ, 128), lambda b, hp: (b, 6 + hp)),
                  BS((SEQ, 128), lambda b, hp: (b, 12 + hp)),
                  BS((2, QB, KWIN), lambda b, hp: (hp, 0, 0)), vec, vec]
        + (carry.in_specs if carry else []),
        out_specs=[BS((SEQ, 128), lambda b, hp: (b, hp))] + [ANY] * no,
        out_shape=[_sds((T, D), BF16)] + (carry.out_shapes if carry else []),
        scratch_shapes=[pltpu.VMEM((SEQ, 128), BF16), pltpu.VMEM((SEQ + KPAD, 128), BF16),
                        pltpu.VMEM((SEQ + KPAD, 128), BF16)] + (carry.scratch if carry else []),
        compiler_params=pltpu.CompilerParams(
            dimension_semantics=("arbitrary", "arbitrary"), vmem_limit_bytes=48 << 20,
            has_side_effects=bool(carry)), name="attn_fwd",
    )(z, z, z, bias, gq2, gk2, *(carry.ins if carry else []))
    return outs[0], outs[1:]


def _attn_bwd(z, dcat, bias, gq2, gk2, carry=None):
    ni = len(carry.ins) if carry else 0
    no = len(carry.out_shapes) if carry else 0

    def body(q_ref, k_ref, v_ref, do_ref, b_ref, gq_ref, gk_ref, *rest):
        cin = rest[:ni]
        dq_ref, dk_ref, dv_ref, db_ref, dgq_ref, dgk_ref = rest[ni:ni + 6]
        cout = rest[ni + 6:ni + 6 + no]
        qn_s, kn_s, v_s, dqn_s, dkn_s, dv_s = rest[ni + 6 + no:ni + 12 + no]
        cscr = rest[ni + 12 + no:]
        hp = pl.program_id(0)
        b = pl.program_id(1)
        if carry:
            @pl.when((hp == 0) & (b == 0))
            def _():
                carry.start(cin, cout, cscr)

        masks = _group_masks(128)
        qn_s[...] = (_head_norm(q_ref[...], gq_ref[...], masks) * SCALE).astype(BF16)
        kn_s[0:KPAD, :] = jnp.zeros((KPAD, 128), BF16)
        kn_s[KPAD:, :] = _head_norm(k_ref[...], gk_ref[...], masks).astype(BF16)
        v_s[0:KPAD, :] = jnp.zeros((KPAD, 128), BF16)
        v_s[KPAD:, :] = v_ref[...].astype(BF16)
        dkn_s[...] = jnp.zeros_like(dkn_s)
        dv_s[...] = jnp.zeros_like(dv_s)

        @pl.when(b == 0)
        def _():
            db_ref[...] = jnp.zeros_like(db_ref)

        @pl.when((b == 0) & (hp == 0))
        def _():
            dgq_ref[...] = jnp.zeros_like(dgq_ref)
            dgk_ref[...] = jnp.zeros_like(dgk_ref)

        col = lax.broadcasted_iota(jnp.int32, (1, KWIN), 1)

        def blk(i, carry):
            r0 = pl.multiple_of(i * QB, QB)
            qb = qn_s[pl.ds(r0, QB), :]
            kw = kn_s[pl.ds(r0, KWIN), :]
            vw = v_s[pl.ds(r0, KWIN), :]
            dob = do_ref[pl.ds(r0, QB), :].astype(BF16)
            startadd = jnp.where(col + r0 < KPAD, NEG, 0.0)
            dqn = jnp.zeros((QB, 128), F32)
            dkw = jnp.zeros((KWIN, 128), F32)
            dvw = jnp.zeros((KWIN, 128), F32)
            for h in range(2):
                qh = jnp.where(masks[h], qb, jnp.zeros_like(qb))
                kh = jnp.where(masks[h], kw, jnp.zeros_like(kw))
                doh = jnp.where(masks[h], dob, jnp.zeros_like(dob))
                p = _attn_softmax(qh, kw, b_ref[h], startadd)
                dvw = dvw + lax.dot_general(p.astype(BF16), doh, _DIMS["tn"],
                                            preferred_element_type=F32)
                dp = lax.dot_general(doh, vw, _DIMS["nt"], preferred_element_type=F32)
                ds = p * (dp - jnp.sum(dp * p, axis=-1, keepdims=True))
                db_ref[h] += ds
                dsb = ds.astype(BF16)
                dqn = dqn + jnp.dot(dsb, kh, preferred_element_type=F32)
                dkw = dkw + lax.dot_general(dsb, qh, _DIMS["tn"], preferred_element_type=F32)
            dqn_s[pl.ds(r0, QB), :] = dqn * SCALE
            dkn_s[pl.ds(r0, KWIN), :] += dkw
            dv_s[pl.ds(r0, KWIN), :] += dvw
            return carry

        lax.fori_loop(0, SEQ // QB, blk, 0)

        dq, dgq = _head_norm_bwd(dqn_s[...], q_ref[...], gq_ref[...], masks)
        dk, dgk = _head_norm_bwd(dkn_s[KPAD:, :], k_ref[...], gk_ref[...], masks)
        dq_ref[...] = dq.astype(BF16)
        dk_ref[...] = dk.astype(BF16)
        dv_ref[...] = dv_s[KPAD:, :].astype(BF16)
        dgq_ref[...] += _fold_heads(dgq, 128)
        dgk_ref[...] += _fold_heads(dgk, 128)

        if carry:
            @pl.when((hp == 5) & (b == NB - 1))
            def _():
                carry.finish(cin, cout, cscr)

    vec = BS((1, 128), lambda hp, b: (0, 0))
    row = BS((SEQ, 128), lambda hp, b: (b, hp))
    outs = pl.pallas_call(
        body, grid=(6, NB),
        in_specs=[row,
                  BS((SEQ, 128), lambda hp, b: (b, 6 + hp)),
                  BS((SEQ, 128), lambda hp, b: (b, 12 + hp)),
                  row,
                  BS((2, QB, KWIN), lambda hp, b: (hp, 0, 0)), vec, vec]
        + (carry.in_specs if carry else []),
        out_specs=[row, row, row, BS((2, QB, KWIN), lambda hp, b: (hp, 0, 0)), vec, vec] + [ANY] * no,
        out_shape=[_sds((T, TOK), BF16), _sds((T, TOK), BF16), _sds((T, TOK), BF16),
                   _sds((12, QB, KWIN), F32), _sds((1, 128), F32), _sds((1, 128), F32)]
        + (carry.out_shapes if carry else []),
        scratch_shapes=[pltpu.VMEM((SEQ, 128), BF16), pltpu.VMEM((SEQ + KPAD, 128), BF16),
                        pltpu.VMEM((SEQ + KPAD, 128), BF16), pltpu.VMEM((SEQ, 128), F32),
                        pltpu.VMEM((SEQ + KPAD, 128), F32), pltpu.VMEM((SEQ + KPAD, 128), F32)]
        + (carry.scratch if carry else []),
        compiler_params=pltpu.CompilerParams(
            dimension_semantics=("arbitrary", "arbitrary"), vmem_limit_bytes=48 << 20,
            has_side_effects=bool(carry)), name="attn_bwd",
    )(z, z, z, dcat, bias, gq2, gk2, *(carry.ins if carry else []))
    return outs[:6], outs[6:]


def _mem_softmax(qh, kn):
    s = lax.dot_general(qh, kn, _DIMS["nt"], preferred_element_type=F32)
    m = jnp.max(s, axis=-1, keepdims=True)
    p = jnp.exp(s - m)
    return p * (1.0 / jnp.sum(p, axis=-1, keepdims=True))


def _memattn_fwd(z, kv, cat, gq4, gk4, qcol, name):
    def body(q_ref, k_ref, v_ref, gq_ref, gk_ref, cat_ref, o_ref):
        del cat_ref
        masks = _group_masks(MEMW)
        qn = (_head_norm(q_ref[...], gq_ref[...], masks) * SCALE).astype(BF16)
        kn = _head_norm(k_ref[...], gk_ref[...], masks).astype(BF16)
        vv = v_ref[...].astype(BF16)
        o = jnp.zeros((TR, MEMW), F32)
        for h in range(4):
            qh = jnp.where(masks[h], qn, jnp.zeros_like(qn))
            vh = jnp.where(masks[h], vv, jnp.zeros_like(vv))
            p = _mem_softmax(qh, kn).astype(BF16)
            o = o + jnp.dot(p, vh, preferred_element_type=F32)
        o_ref[...] = o.astype(BF16)

    nt = SEQ // TR
    vec = BS((1, MEMW), lambda b, t: (0, 0))
    return pl.pallas_call(
        body, grid=(NB, nt),
        in_specs=[BS((TR, MEMW), lambda b, t: (b * nt + t, qcol)),
                  BS((MEMT, MEMW), lambda b, t: (b, 0)),
                  BS((MEMT, MEMW), lambda b, t: (b, 1)), vec, vec, ANY],
        out_specs=BS((TR, MEMW), lambda b, t: (b * nt + t, 3)),
        out_shape=_sds((T, D), BF16), input_output_aliases={5: 0},
        compiler_params=_cp(("arbitrary", "arbitrary")), name=name,
    )(z, kv, kv, gq4, gk4, cat)


def _memattn_bwd(z, kv, dcat, gq4, gk4, qcol, name):
    nt = SEQ // TR

    def body(q_ref, k_ref, v_ref, do_ref, gq_ref, gk_ref,
             dq_ref, dkv_ref, dgq_ref, dgk_ref, dkn_s, dv_s):
        b = pl.program_id(0)
        t = pl.program_id(1)
        masks = _group_masks(MEMW)
        qz = q_ref[...]
        kz = k_ref[...]
        qn = (_head_norm(qz, gq_ref[...], masks) * SCALE).astype(BF16)
        kn = _head_norm(kz, gk_ref[...], masks).astype(BF16)
        vv = v_ref[...].astype(BF16)
        dob = do_ref[...].astype(BF16)

        @pl.when(t == 0)
        def _():
            dkn_s[...] = jnp.zeros_like(dkn_s)
            dv_s[...] = jnp.zeros_like(dv_s)

        @pl.when((t == 0) & (b == 0))
        def _():
            dgq_ref[...] = jnp.zeros_like(dgq_ref)
            dgk_ref[...] = jnp.zeros_like(dgk_ref)

        dqn = jnp.zeros((TR, MEMW), F32)
        dkn = jnp.zeros((MEMT, MEMW), F32)
        dvv = jnp.zeros((MEMT, MEMW), F32)
        for h in range(4):
            qh = jnp.where(masks[h], qn, jnp.zeros_like(qn))
            kh = jnp.where(masks[h], kn, jnp.zeros_like(kn))
            doh = jnp.where(masks[h], dob, jnp.zeros_like(dob))
            p = _mem_softmax(qh, kn)
            dvv = dvv + lax.dot_general(p.astype(BF16), doh, _DIMS["tn"], preferred_element_type=F32)
            dp = lax.dot_general(doh, vv, _DIMS["nt"], preferred_element_type=F32)
            ds = p * (dp - jnp.sum(dp * p, axis=-1, keepdims=True))
            dsb = ds.astype(BF16)
            dqn = dqn + jnp.dot(dsb, kh, preferred_element_type=F32)
            dkn = dkn + lax.dot_general(dsb, qh, _DIMS["tn"], preferred_element_type=F32)
        dkn_s[...] += dkn
        dv_s[...] += dvv
        dq, dgq = _head_norm_bwd(dqn * SCALE, qz, gq_ref[...], masks)
        dq_ref[...] = dq.astype(BF16)
        dgq_ref[...] += _fold_heads(dgq, MEMW)

        @pl.when(t == nt - 1)
        def _():
            dk, dgk = _head_norm_bwd(dkn_s[...], kz, gk_ref[...], masks)
            dkv_ref[:, 0:MEMW] = dk
            dkv_ref[:, MEMW:] = dv_s[...]
            dgk_ref[...] += _fold_heads(dgk, MEMW)

    vec = BS((1, MEMW), lambda b, t: (0, 0))
    return pl.pallas_call(
        body, grid=(NB, nt),
        in_specs=[BS((TR, MEMW), lambda b, t: (b * nt + t, qcol)),
                  BS((MEMT, MEMW), lambda b, t: (b, 0)),
                  BS((MEMT, MEMW), lambda b, t: (b, 1)),
                  BS((TR, MEMW), lambda b, t: (b * nt + t, 3)), vec, vec],
        out_specs=[BS((TR, MEMW), lambda b, t: (b * nt + t, 0)),
                   BS((MEMT, 2 * MEMW), lambda b, t: (b, 0)), vec, vec],
        out_shape=[_sds((T, MEMW), BF16), _sds((NB * MEMT, 2 * MEMW), F32),
                   _sds((1, MEMW), F32), _sds((1, MEMW), F32)],
        scratch_shapes=[pltpu.VMEM((MEMT, MEMW), F32), pltpu.VMEM((MEMT, MEMW), F32)],
        compiler_params=_cp(("arbitrary", "arbitrary")), name=name,
    )(z, kv, kv, dcat, gq4, gk4)


HALO = 32
NEXT = 64
RT = 64


def _glu(zz):
    return zz[:, :TOK] * jax.nn.sigmoid(zz[:, TOK:])


def _layer_norm_parts(y):
    mu = jnp.mean(y, axis=-1, keepdims=True)
    yc = y - mu
    rstd = lax.rsqrt(jnp.mean(yc * yc, axis=-1, keepdims=True) + EPS)
    return yc * rstd, rstd


def _shifted_copies(src, dst, rows):
    for b in range(1, 8):
        dst[b - 1, 0:rows, :] = src[b:b + rows, :]


def _tap(src, shifted, off, r0, rows):
    b = off % 8
    if b == 0:
        return src[r0 + off:r0 + off + rows, :]
    return shifted[b - 1, r0 + off - b:r0 + off - b + rows, :]


def _conv_rows(w_ref, hbuf, hs, r0, rows):
    y = jnp.zeros((rows, TOK), F32)
    for j in range(CONVW):
        y = y + w_ref[j:j + 1, :] * _tap(hbuf, hs, (HALO - CONVW + 1) + j, r0, rows)
    return y


def _conv_fwd(z, cw, cb, lg, lb):
    nt = SEQ // TR

    def body(zc_ref, zp_ref, w_ref, cb_ref, lg_ref, lb_ref, o_ref, hbuf, hs):
        t = pl.program_id(1)
        hbuf[0:HALO, :] = jnp.where(t == 0, 0.0, _glu(zp_ref[...]))
        hbuf[HALO:, :] = _glu(zc_ref[...])
        _shifted_copies(hbuf, hs, HALO + TR - 8)
        for r0 in range(0, TR, RT):
            y = _conv_rows(w_ref, hbuf, hs, r0, RT) + cb_ref[...]
            yh, _ = _layer_norm_parts(y)
            o = yh * lg_ref[...] + lb_ref[...]
            o_ref[r0:r0 + RT, :] = (o * jax.nn.sigmoid(o)).astype(BF16)

    vec = BS((1, TOK), lambda b, t: (0, 0))
    per = TR // HALO
    return pl.pallas_call(
        body, grid=(NB, nt),
        in_specs=[BS((TR, 2 * TOK), lambda b, t: (b * nt + t, 0)),
                  BS((HALO, 2 * TOK), lambda b, t: (jnp.maximum((b * nt + t) * per - 1, 0), 0)),
                  BS((32, TOK), lambda b, t: (0, 0)), vec, vec, vec],
        out_specs=BS((TR, TOK), lambda b, t: (b * nt + t, 0)),
        out_shape=_sds((T, D), BF16),
        scratch_shapes=[pltpu.VMEM((HALO + TR, TOK), F32), pltpu.VMEM((7, HALO + TR, TOK), F32)],
        compiler_params=_cp(("arbitrary", "arbitrary")), name="conv_fwd",
    )(z, z, cw, cb, lg, lb)


def _conv_bwd(z, dcat, cw, cb, lg, lb):
    nt = SEQ // TR
    ext = TR + NEXT

    def body(zc_ref, zp_ref, zn_ref, dc_ref, dn_ref, w_ref, cb_ref, lg_ref, lb_ref,
             du_ref, dw_ref, dcb_ref, dlg_ref, dlb_ref, dbin_ref, hbuf, dybuf, hs, dys):
        b = pl.program_id(0)
        t = pl.program_id(1)

        @pl.when((b == 0) & (t == 0))
        def _():
            dw_ref[...] = jnp.zeros_like(dw_ref)
            dcb_ref[...] = jnp.zeros_like(dcb_ref)
            dlg_ref[...] = jnp.zeros_like(dlg_ref)
            dlb_ref[...] = jnp.zeros_like(dlb_ref)
            dbin_ref[...] = jnp.zeros_like(dbin_ref)

        hbuf[0:HALO, :] = jnp.where(t == 0, 0.0, _glu(zp_ref[...]))
        hbuf[HALO:HALO + TR, :] = _glu(zc_ref[...])
        hbuf[HALO + TR:, :] = _glu(zn_ref[...])
        _shifted_copies(hbuf, hs, HALO + TR + NEXT - 8)
        last = t == nt - 1
        for r0 in range(0, ext, RT):
            y = _conv_rows(w_ref, hbuf, hs, r0, RT) + cb_ref[...]
            yh, rstd = _layer_norm_parts(y)
            o = yh * lg_ref[...] + lb_ref[...]
            sg = jax.nn.sigmoid(o)
            if r0 < TR:
                dtok = dc_ref[r0:r0 + RT, :]
            else:
                dtok = jnp.where(last, 0.0, dn_ref[...])
            do = dtok * (sg * (1.0 + o * (1.0 - sg)))
            dyh = do * lg_ref[...]
            dy = rstd * (dyh - jnp.mean(dyh, axis=-1, keepdims=True)
                         - yh * jnp.mean(dyh * yh, axis=-1, keepdims=True))
            dybuf[r0:r0 + RT, :] = dy
            if r0 < TR:
                dlg_ref[...] += jnp.sum(do * yh, axis=0, keepdims=True)
                dlb_ref[...] += jnp.sum(do, axis=0, keepdims=True)
                dcb_ref[...] += jnp.sum(dy, axis=0, keepdims=True)
        _shifted_copies(dybuf, dys, ext - 8)
        for r0 in range(0, TR, RT):
            dh = jnp.zeros((RT, TOK), F32)
            for j in range(CONVW):
                dh = dh + w_ref[j:j + 1, :] * _tap(dybuf, dys, (CONVW - 1) - j, r0, RT)
            a = zc_ref[r0:r0 + RT, 0:TOK]
            sg = jax.nn.sigmoid(zc_ref[r0:r0 + RT, TOK:])
            da = dh * sg
            dg = dh * a * (sg * (1.0 - sg))
            du_ref[r0:r0 + RT, 0:TOK] = da.astype(BF16)
            du_ref[r0:r0 + RT, TOK:] = dg.astype(BF16)
            dbin_ref[:, 0:TOK] += jnp.sum(da, axis=0, keepdims=True)
            dbin_ref[:, TOK:] += jnp.sum(dg, axis=0, keepdims=True)
        for j in range(CONVW):
            acc = jnp.zeros((8, TOK), F32)
            for r0 in range(0, TR, RT):
                prod = dybuf[r0:r0 + RT, :] * _tap(hbuf, hs, (HALO - CONVW + 1) + j, r0, RT)
                acc = acc + jnp.sum(prod.reshape(RT // 8, 8, TOK), axis=0)
            dw_ref[j:j + 1, :] += jnp.sum(acc, axis=0, keepdims=True)

    vec = BS((1, TOK), lambda b, t: (0, 0))
    perh = TR // HALO
    pern = TR // NEXT
    nlast_n = T // NEXT - 1
    return pl.pallas_call(
        body, grid=(NB, nt),
        in_specs=[BS((TR, 2 * TOK), lambda b, t: (b * nt + t, 0)),
                  BS((HALO, 2 * TOK), lambda b, t: (jnp.maximum((b * nt + t) * perh - 1, 0), 0)),
                  BS((NEXT, 2 * TOK), lambda b, t: (jnp.minimum((b * nt + t + 1) * pern, nlast_n), 0)),
                  BS((TR, TOK), lambda b, t: (b * nt + t, 0)),
                  BS((NEXT, TOK), lambda b, t: (jnp.minimum((b * nt + t + 1) * pern, nlast_n), 0)),
                  BS((32, TOK), lambda b, t: (0, 0)), vec, vec, vec],
        out_specs=[BS((TR, 2 * TOK), lambda b, t: (b * nt + t, 0)),
                   BS((32, TOK), lambda b, t: (0, 0)), vec, vec, vec,
                   BS((1, 2 * TOK), lambda b, t: (0, 0))],
        out_shape=[_sds((T, 2 * TOK), BF16), _sds((32, TOK), F32), _sds((1, TOK), F32),
                   _sds((1, TOK), F32), _sds((1, TOK), F32), _sds((1, 2 * TOK), F32)],
        scratch_shapes=[pltpu.VMEM((HALO + TR + NEXT, TOK), F32), pltpu.VMEM((ext, TOK), F32),
                        pltpu.VMEM((7, HALO + TR + NEXT, TOK), F32), pltpu.VMEM((7, ext, TOK), F32)],
        compiler_params=_cp(("arbitrary", "arbitrary"), vmem_mb=56), name="conv_bwd",
    )(z, z, z, dcat, dcat, cw, cb, lg, lb)


def _ffn_up(h2, wgu, l):
    def body(h_ref, wg_ref, wu_ref, g_ref, u_ref, a_ref):
        hv = h_ref[...]
        g = jnp.dot(hv, wg_ref[...], preferred_element_type=F32)
        u = jnp.dot(hv, wu_ref[...], preferred_element_type=F32)
        g_ref[...] = g
        u_ref[...] = u
        a_ref[...] = (g * jax.nn.sigmoid(g) * u).astype(BF16)

    out = BS((None, TR, FS), lambda s, i: (s, i, 0))
    return pl.pallas_call(
        body, grid=(NSH, T // TR),
        in_specs=[BS((TR, D), lambda s, i: (i, 0)),
                  BS((None, D, FS), lambda s, i: (s, 0, 0)),
                  BS((None, D, FS), lambda s, i: (NSH + s, 0, 0))],
        out_specs=[out, out, out],
        out_shape=[_sds((NSH, T, FS), F32), _sds((NSH, T, FS), F32), _sds((NSH, T, FS), BF16)],
        compiler_params=_cp(("parallel", "arbitrary")), name=f"ffn_up_{l}",
    )(h2, wgu, wgu)


def _ffn_down_bwd(dx, wd, g, u, l):
    def epilogue(dact, ex, o_ref):
        gv = ex[0][...]
        uv = ex[1][...]
        sg = jax.nn.sigmoid(gv)
        o_ref[0] = (dact * uv * (sg * (1.0 + gv * (1.0 - sg)))).astype(BF16)
        o_ref[1] = (dact * (gv * sg)).astype(BF16)

    ex_spec = BS((None, TR, FS), lambda i, s, k: (s, i, 0))
    return _mm("nt", dx, wd, grid=(T // TR, NSH, 1),
               a_spec=BS((TR, D), lambda i, s, k: (i, 0)),
               b_spec=BS((None, FS, D), lambda i, s, k: (s, 0, 0)),
               out_shape=_sds((2, NSH, T, FS), BF16),
               out_spec=BS((2, None, TR, FS), lambda i, s, k: (0, s, i, 0)),
               acc_shape=(TR, FS), extras=(g, u), extra_specs=(ex_spec, ex_spec),
               epilogue=epilogue, name=f"ffn_down_bwd_{l}")


def _loss_head(y, target):
    def body(y_ref, t_ref, l_ref, dy_ref, acc):
        i = pl.program_id(0)
        e = y_ref[...] - t_ref[...]
        dy_ref[...] = e * (1.0 / D)

        @pl.when(i == 0)
        def _():
            acc[...] = jnp.zeros_like(acc)

        acc[...] += jnp.sum(jnp.mean(e * e, axis=-1, keepdims=True), axis=0, keepdims=True)

        @pl.when(i == T // TR - 1)
        def _():
            l_ref[...] = 0.5 * acc[...]

    row = BS((TR, D), lambda i: (i, 0))
    return pl.pallas_call(
        body, grid=(T // TR,), in_specs=[row, row],
        out_specs=[BS((1, 1), lambda i: (0, 0)), row],
        out_shape=[_sds((1, 1), F32), _sds((T, D), F32)],
        scratch_shapes=[pltpu.VMEM((1, 1), F32)],
        compiler_params=_cp(("arbitrary",)), name="loss_head",
    )(y, target)


def _row_tile(rows, cols, itemsize=4, limit=2 << 20):
    tr = rows
    while tr * cols * itemsize > limit and tr % 2 == 0 and (tr // 2) % 16 == 0:
        tr //= 2
    return tr


def _cast_bf16(arrs, name):
    n = len(arrs)
    rows, cols = arrs[0].shape
    tr = _row_tile(rows, cols)

    def body(*refs):
        o_ref = refs[n]
        k = pl.program_id(0)
        val = refs[0][...]
        for j in range(1, n):
            val = jnp.where(k == j, refs[j][...], val)
        o_ref[...] = val.astype(BF16)

    return pl.pallas_call(
        body, grid=(n, rows // tr),
        in_specs=[BS((tr, cols), lambda k, i: (i, 0))] * n,
        out_specs=BS((None, tr, cols), lambda k, i: (k, i, 0)),
        out_shape=_sds((n, rows, cols), BF16),
        compiler_params=_cp(("arbitrary", "arbitrary")), name=name,
    )(*arrs)


def _quad_sum(own, got, name):
    n, rows, cols = own.shape
    tr = _row_tile(rows, cols)

    def body(a_ref, q_ref, o_ref):
        o_ref[...] = ((a_ref[...] + q_ref[0].astype(F32)) + q_ref[1].astype(F32)) + q_ref[2].astype(F32)

    spec = BS((None, tr, cols), lambda k, i: (k, i, 0))
    return pl.pallas_call(
        body, grid=(n, rows // tr),
        in_specs=[spec, BS((3, None, tr, cols), lambda k, i: (0, k, i, 0))], out_specs=spec,
        out_shape=_sds((n, rows, cols), F32),
        compiler_params=_cp(("arbitrary", "arbitrary")), name=name,
    )(own, got)


def _adam_math(w, g, m, v):
    m = ADAM_B1 * m + (1.0 - ADAM_B1) * g
    v = ADAM_B2 * v + (1.0 - ADAM_B2) * (g * g)
    m_hat = m / (1.0 - ADAM_B1 ** ADAM_STEP)
    v_hat = v / (1.0 - ADAM_B2 ** ADAM_STEP)
    delta = -ADAM_LR * (m_hat / (jnp.sqrt(v_hat) + ADAM_EPS) + ADAM_WD * w)
    return delta, m, v


def _adamw_big(w, g, m, v, name):
    shape = w.shape
    cols = shape[-1]
    rows = w.size // cols
    tr = _row_tile(rows, cols, limit=1 << 20)

    def body(w_ref, g_ref, m_ref, v_ref, d_ref, nm_ref, nv_ref):
        d, nm, nv = _adam_math(w_ref[...], g_ref[...], m_ref[...], v_ref[...])
        d_ref[...] = d
        nm_ref[...] = nm
        nv_ref[...] = nv

    spec = BS((tr, cols), lambda i: (i, 0))
    outs = pl.pallas_call(
        body, grid=(rows // tr,), in_specs=[spec] * 4, out_specs=[spec] * 3,
        out_shape=[_sds((rows, cols), F32)] * 3,
        compiler_params=_cp(("arbitrary",)), name=name,
    )(*[a.reshape(rows, cols) for a in (w, g, m, v)])
    return [o.reshape(shape) for o in outs]


def _adamw_small(ws, gs, ms, vs):
    n = len(ws)

    def body(*refs):
        for i in range(n):
            d, nm, nv = _adam_math(refs[i][...], refs[n + i][...], refs[2 * n + i][...],
                                   refs[3 * n + i][...])
            refs[4 * n + i][...] = d
            refs[5 * n + i][...] = nm
            refs[6 * n + i][...] = nv

    vm = BS(memory_space=pltpu.VMEM)
    outs = pl.pallas_call(
        body, in_specs=[vm] * (4 * n), out_specs=[vm] * (3 * n),
        out_shape=[_sds(w.shape, F32) for w in ws] * 3,
        compiler_params=_cp(), name="adamw_small",
    )(*ws, *gs, *ms, *vs)
    return outs[:n], outs[n:2 * n], outs[2 * n:]


def _place():
    x, y, c = lax.axis_index("x"), lax.axis_index("y"), lax.axis_index("c")
    chips = [(1 - x, y), (x, 1 - y), (1 - x, 1 - y)]
    return x, y, c, chips


class _Exchange:
    def __init__(self, ins, in_specs, out_shapes, scratch, start, finish):
        self.ins, self.in_specs, self.out_shapes, self.scratch = ins, in_specs, out_shapes, scratch
        self.start, self.finish = start, finish


def _run_exchange(ex, name, vmem_mb=40):
    ni, no = len(ex.ins), len(ex.out_shapes)

    def body(*refs):
        ex.start(refs[:ni], refs[ni:ni + no], refs[ni + no:])
        ex.finish(refs[:ni], refs[ni:ni + no], refs[ni + no:])

    return pl.pallas_call(
        body, in_specs=ex.in_specs, out_specs=[ANY] * no, out_shape=ex.out_shapes,
        scratch_shapes=ex.scratch,
        compiler_params=pltpu.CompilerParams(has_side_effects=True, vmem_limit_bytes=vmem_mb << 20),
        name=name,
    )(*ex.ins)


def _gather_exchange(srcs, dst_shapes, views, small=None):
    nu = len(srcs)
    nd = len(dst_shapes)
    ns = 1 if small is not None else 0

    def unpack(ins, outs, scr):
        x, y, c, chips = _place()
        src = ins[:nu]
        vw = [views[u](outs[:nd]) for u in range(nu)]
        vbuf = scr[:nu]
        send, recv, fsend, frecv, lsem, ssend, srecv, vsem = scr[nu:]

        def ici(u, j, shard, to):
            return pltpu.make_async_remote_copy(
                src_ref=vbuf[u].at[:, c], dst_ref=vw[u].at[:, shard, c],
                send_sem=send.at[3 * u + j], recv_sem=recv.at[3 * u + j],
                device_id=to, device_id_type=MESH)

        def fwd(u, j, shard, half):
            return pltpu.make_async_remote_copy(
                src_ref=vw[u].at[:, shard, half], dst_ref=vw[u].at[:, shard, half],
                send_sem=fsend.at[3 * u + j], recv_sem=frecv.at[3 * u + j],
                device_id=(x, y, 1 - c), device_id_type=MESH)

        def small_copy(j, shard, to):
            return pltpu.make_async_remote_copy(
                src_ref=ins[nu], dst_ref=outs[nd].at[shard],
                send_sem=ssend.at[j], recv_sem=srecv.at[j], device_id=to, device_id_type=MESH)

        stage = [pltpu.make_async_copy(src[u], vbuf[u], vsem.at[u]) for u in range(nu)]
        local = [pltpu.make_async_copy(vbuf[u], vw[u].at[:, 2 * x + y], lsem.at[u]) for u in range(nu)]
        if ns:
            local.append(pltpu.make_async_copy(ins[nu], outs[nd].at[2 * x + y], lsem.at[nu]))
        return x, y, c, chips, ici, fwd, small_copy, stage, local

    def start(ins, outs, scr):
        x, y, c, chips, ici, fwd, small_copy, stage, local = unpack(ins, outs, scr)
        s = 2 * x + y
        for cp in stage:
            cp.start()
        if ns:
            local[nu].start()
            for j, chip in enumerate(chips):
                small_copy(j, s, (*chip, c)).start()
        for u in range(nu):
            stage[u].wait()
            for j, chip in enumerate(chips):
                ici(u, j, s, (*chip, c)).start()
            local[u].start()

    def finish(ins, outs, scr):
        x, y, c, chips, ici, fwd, small_copy, stage, local = unpack(ins, outs, scr)
        s = 2 * x + y
        for u in range(nu):
            for j, chip in enumerate(chips):
                sj = 2 * chip[0] + chip[1]
                ici(u, j, sj, (x, y, c)).wait_recv()
                fwd(u, j, sj, c).start()
        for u in range(nu):
            for j, chip in enumerate(chips):
                sj = 2 * chip[0] + chip[1]
                fwd(u, j, sj, 1 - c).wait_recv()
        for u in range(nu):
            for j, chip in enumerate(chips):
                ici(u, j, s, (*chip, c)).wait_send()
                fwd(u, j, s, c).wait_send()
        if ns:
            for j, chip in enumerate(chips):
                small_copy(j, 2 * chip[0] + chip[1], (x, y, c)).wait_recv()
                small_copy(j, s, (*chip, c)).wait_send()
        for cp in local:
            cp.wait()

    dma = pltpu.SemaphoreType.DMA
    return _Exchange(
        ins=list(srcs) + ([small] if ns else []),
        in_specs=[ANY] * nu + [BS(memory_space=pltpu.VMEM)] * ns,
        out_shapes=[_sds(sh, BF16) for sh in dst_shapes]
        + ([_sds((NSH,) + small.shape, F32)] if ns else []),
        scratch=[pltpu.VMEM(a.shape, BF16) for a in srcs]
        + [dma((3 * nu,)), dma((3 * nu,)), dma((3 * nu,)), dma((3 * nu,)),
           dma((nu + 1,)), dma((3,)), dma((3,)), dma((nu,))],
        start=start, finish=finish)


def _pair_reduce(g, name):
    pn, _, _, rh, cc = g.shape
    n = pn * NSH

    def body(g_ref, own_ref, sb_ref, sendb, recvb, stage, outf, outb, send, recv, lsem, osem):
        x, y, c, _ = _place()
        s = 2 * x + y

        def load(k, half):
            return pltpu.make_async_copy(g_ref.at[k // NSH, k % NSH, half], stage.at[k % 2], lsem.at[k % 2])

        def push(k):
            return pltpu.make_async_remote_copy(
                src_ref=sendb.at[k], dst_ref=recvb.at[k], send_sem=send.at[k], recv_sem=recv.at[k],
                device_id=(x, y, 1 - c), device_id_type=MESH)

        def store(k):
            return pltpu.make_async_copy(outb.at[k % 2], sb_ref.at[k // NSH, k % NSH], osem.at[k % 2])

        load(0, 1 - c).start()
        for k in range(n):
            if k + 1 < n:
                load(k + 1, 1 - c).start()
            load(k, 1 - c).wait()
            sendb[k] = stage[k % 2].astype(BF16)
            push(k).start()
        load(0, c).start()
        for k in range(n):
            if k + 1 < n:
                load(k + 1, c).start()
            load(k, c).wait()
            push(k).wait_recv()
            total = stage[k % 2] + recvb[k].astype(F32)
            if k >= 2:
                store(k - 2).wait()
            outb[k % 2] = total.astype(BF16)
            store(k).start()

            @pl.when(s == k % NSH)
            def _():
                outf[...] = total
                keep = pltpu.make_async_copy(outf, own_ref.at[k // NSH], osem.at[2])
                keep.start()
                keep.wait()

        for k in range(max(n - 2, 0), n):
            store(k).wait()
        for k in range(n):
            push(k).wait_send()

    dma = pltpu.SemaphoreType.DMA
    return pl.pallas_call(
        body, in_specs=[ANY], out_specs=[ANY, ANY],
        out_shape=[_sds((pn, rh, cc), F32), _sds((pn, NSH, rh, cc), BF16)],
        scratch_shapes=[pltpu.VMEM((n, rh, cc), BF16), pltpu.VMEM((n, rh, cc), BF16),
                        pltpu.VMEM((2, rh, cc), F32), pltpu.VMEM((rh, cc), F32),
                        pltpu.VMEM((2, rh, cc), BF16), dma((n,)), dma((n,)), dma((2,)), dma((3,))],
        compiler_params=pltpu.CompilerParams(has_side_effects=True, vmem_limit_bytes=56 << 20),
        name=name,
    )(g)


def _chip_exchange(sums_bf16):
    nu = len(sums_bf16)

    def pushes(ins, outs, scr):
        x, y, c, chips = _place()
        send, recv = scr
        return [pltpu.make_async_remote_copy(
            src_ref=ins[u].at[:, 2 * chip[0] + chip[1]], dst_ref=outs[u].at[j],
            send_sem=send.at[3 * u + j], recv_sem=recv.at[3 * u + j],
            device_id=(*chip, c), device_id_type=MESH)
            for u in range(nu) for j, chip in enumerate(chips)]

    def start(ins, outs, scr):
        for cp in pushes(ins, outs, scr):
            cp.start()

    def finish(ins, outs, scr):
        for cp in pushes(ins, outs, scr):
            cp.wait()

    dma = pltpu.SemaphoreType.DMA
    shapes = [(3, a.shape[0], a.shape[2], a.shape[3]) for a in sums_bf16]
    return _Exchange(ins=list(sums_bf16), in_specs=[ANY] * nu,
                     out_shapes=[_sds(sh, BF16) for sh in shapes],
                     scratch=[dma((3 * nu,)), dma((3 * nu,))], start=start, finish=finish)


def _final_exchange(halves, out_shapes, targets):
    nu = len(halves)
    no = len(out_shapes)
    ncp = sum(len(t) for t in targets)

    def body(*refs):
        hv = refs[:nu]
        out = refs[nu:nu + no]
        sbuf = refs[nu + no:2 * nu + no]
        rbuf = refs[2 * nu + no:3 * nu + no]
        send, recv, lsem, osem, csem = refs[3 * nu + no:]
        x, y, c, _ = _place()
        stage = [pltpu.make_async_copy(hv[u], sbuf[u], lsem.at[u]) for u in range(nu)]
        push = [pltpu.make_async_remote_copy(
            src_ref=sbuf[u], dst_ref=rbuf[u], send_sem=send.at[u], recv_sem=recv.at[u],
            device_id=(x, y, 1 - c), device_id_type=MESH) for u in range(nu)]
        mine, theirs = [], []
        k = 0
        for u in range(nu):
            rh = hv[u].shape[1]
            for (p, oi, li) in targets[u]:
                mine.append((u, pltpu.make_async_copy(
                    sbuf[u].at[p], out[oi].at[li, pl.ds(c * rh, rh), :], csem.at[k])))
                theirs.append((u, pltpu.make_async_copy(
                    rbuf[u].at[p], out[oi].at[li, pl.ds((1 - c) * rh, rh), :], osem.at[k])))
                k += 1
        for cp in stage:
            cp.start()
        for u in range(nu):
            stage[u].wait()
            push[u].start()
            for (v, cp) in mine:
                if v == u:
                    cp.start()
        for u in range(nu):
            push[u].wait_recv()
            for (v, cp) in theirs:
                if v == u:
                    cp.start()
        for (_, cp) in theirs + mine:
            cp.wait()
        for u in range(nu):
            push[u].wait_send()

    dma = pltpu.SemaphoreType.DMA
    bufs = [pltpu.VMEM(h.shape, F32) for h in halves]
    return pl.pallas_call(
        body, in_specs=[ANY] * nu, out_specs=[ANY] * no,
        out_shape=[_sds(sh, F32) for sh in out_shapes],
        scratch_shapes=bufs + bufs + [dma((nu,)), dma((nu,)), dma((nu,)), dma((ncp,)), dma((ncp,))],
        compiler_params=pltpu.CompilerParams(has_side_effects=True, vmem_limit_bytes=56 << 20),
        name="final_exchange",
    )(*halves)


def _small_allreduce(pack):
    rows = pack.shape[0]

    def body(p_ref, o_ref, buf, send, recv):
        x, y, c, _ = _place()
        me = 4 * x + 2 * y + c
        buf[me] = p_ref[...]
        k = 0
        copies = []
        for dx in range(2):
            for dy in range(2):
                for dc in range(2):
                    if dx == 0 and dy == 0 and dc == 0:
                        continue
                    to = (jnp.where(dx, 1 - x, x), jnp.where(dy, 1 - y, y), jnp.where(dc, 1 - c, c))
                    src_slot = 4 * to[0] + 2 * to[1] + to[2]
                    copies.append((pltpu.make_async_remote_copy(
                        src_ref=p_ref, dst_ref=buf.at[me], send_sem=send.at[k], recv_sem=recv.at[k],
                        device_id=to, device_id_type=MESH), src_slot, k))
                    k += 1
        for cp, _, _ in copies:
            cp.start()
        for cp, src_slot, k in copies:
            pltpu.make_async_remote_copy(
                src_ref=p_ref, dst_ref=buf.at[src_slot], send_sem=send.at[k], recv_sem=recv.at[k],
                device_id=(x, y, c), device_id_type=MESH).wait()
        acc = buf[0]
        for d in range(1, 8):
            acc = acc + buf[d]
        o_ref[...] = acc

    dma = pltpu.SemaphoreType.DMA
    vm = BS(memory_space=pltpu.VMEM)
    return pl.pallas_call(
        body, in_specs=[vm], out_specs=vm, out_shape=_sds((rows, D), F32),
        scratch_shapes=[pltpu.VMEM((8, rows, D), F32), dma((7,)), dma((7,))],
        compiler_params=pltpu.CompilerParams(has_side_effects=True, vmem_limit_bytes=32 << 20),
        name="small_allreduce",
    )(pack)


def _in_proj(h, w, bias, name):
    n = w.shape[1]
    tn = 640 if n == NA else 896
    ep = None
    extras, especs = (), ()
    if bias is not None:
        def ep(acc, ex, o_ref):
            o_ref[...] = acc + ex[0][...]
        extras = (bias,)
        especs = (BS((1, tn), lambda i, j, k: (0, j)),)
    return _mm("nn", h, w, grid=(T // TR, n // tn, 1),
               a_spec=BS((TR, D), lambda i, j, k: (i, 0)), b_spec=BS((D, tn), lambda i, j, k: (0, j)),
               out_shape=_sds((T, n), F32), out_spec=BS((TR, tn), lambda i, j, k: (i, j)),
               acc_shape=(TR, tn), extras=extras, extra_specs=especs, epilogue=ep, name=name)


def _add_res(acc, ex, o_ref):
    o_ref[...] = acc + ex[0][...]


def _local_step(x, mem, target, w, p, fwd_carry=None, fwd_carry_done=None, bwd_carry_fn=None):
    row = lambda i, j, k: (i, 0)
    whole = lambda i, j, k: (0, 0)
    w = dict(w)
    saved = []
    bias = _bias_expand(p["rel_u"])
    for l in range(2):
        type_a = l == 0
        h = _rms_fwd(x, p["norm1_g"][l:l + 1], f"rms1_{l}")
        memn = _rms_fwd(mem, p["mem_norm_g"][l:l + 1], f"rmsmem_{l}")
        if type_a:
            z = _in_proj(h, w["a"], None, "inproj_a")
            cat, carried = _attn_fwd(z, bias, p["a_q_g2"], p["a_k_g2"], fwd_carry)
            if fwd_carry is not None:
                w.update(fwd_carry_done(carried))
            qcol = NA // MEMW - 1
        else:
            z = _in_proj(h, w["b"], p["b_b_in"], "inproj_b")
            cat = _conv_fwd(z, p["conv_w"], p["conv_b"], p["ln_g"], p["ln_b"])
            qcol = NBW // MEMW - 1
        kv = _mm("nn", memn, w["kv"][l], grid=(1, 1, 1),
                 a_spec=BS((NB * MEMT, D), whole), b_spec=BS((D, 2 * MEMW), whole),
                 out_shape=_sds((NB * MEMT, 2 * MEMW), F32), out_spec=BS((NB * MEMT, 2 * MEMW), whole),
                 acc_shape=(8, 128), name=f"memkv_{l}")
        cat = _memattn_fwd(z, kv, cat, p["mq_g4"][l:l + 1], p["mk_g4"][l:l + 1], qcol, f"memattn_fwd_{l}")
        x1 = _mm("nn", cat, w["wo"][l], grid=(T // TR, 1, 1), a_spec=BS((TR, D), row),
                 b_spec=BS((D, D), whole),
                 out_shape=_sds((T, D), F32), out_spec=BS((TR, D), row), acc_shape=(8, 128),
                 extras=(x,), extra_specs=(BS((TR, D), row),), epilogue=_add_res, name=f"outproj_{l}")
        h2 = _rms_fwd(x1, p["norm2_g"][l:l + 1], f"rms2_{l}")
        g, u, act = _ffn_up(h2, w["gu"][l], l)
        x2 = _mm("nn", act, w["wd"][l], grid=(T // TR, 1, NSH),
                 a_spec=BS((None, TR, FS), lambda i, j, k: (k, i, 0)),
                 b_spec=BS((None, FS, D), lambda i, j, k: (k, 0, 0)),
                 out_shape=_sds((T, D), F32), out_spec=BS((TR, D), row), acc_shape=(TR, D),
                 extras=(x1,), extra_specs=(BS((TR, D), row),), epilogue=_add_res, name=f"ffn_down_{l}")
        saved.append(dict(x=x, h=h, memn=memn, kv=kv, z=z, cat=cat, x1=x1, h2=h2, g=g, u=u, act=act,
                          qcol=qcol))
        x = x2

    loss, dx = _loss_head(x, target)

    big = dict(a=None, b=None, kv=[None, None], wo=[None, None], gu=[None, None], wd=[None, None])
    small = {}
    bwd_carried = ()
    tk = T // 2
    nkt = T // tk
    for l in (1, 0):
        sv = saved[l]
        dgu = _ffn_down_bwd(dx, w["wd"][l], sv["g"], sv["u"], l)
        dgu8 = dgu.reshape(2 * NSH, T, FS)
        big["wd"][l] = _mm("tn", sv["act"], dx, grid=(NSH, 1, nkt),
                           a_spec=BS((None, tk, FS), lambda i, j, k: (i, k, 0)),
                           b_spec=BS((tk, D), lambda i, j, k: (k, 0)),
                           out_shape=_sds((NSH, FS, D), F32),
                           out_spec=BS((None, FS, D), lambda i, j, k: (i, 0, 0)),
                           acc_shape=(FS, D), name=f"dw_down_{l}")
        dh2 = _mm("nt", dgu8, w["gu"][l], grid=(T // TR, 1, 2 * NSH),
                  a_spec=BS((None, TR, FS), lambda i, j, k: (k, i, 0)),
                  b_spec=BS((None, D, FS), lambda i, j, k: (k, 0, 0)),
                  out_shape=_sds((T, D), F32), out_spec=BS((TR, D), row), acc_shape=(TR, D),
                  name=f"dh2_{l}")
        big["gu"][l] = _mm("tn", sv["h2"], dgu8, grid=(2 * NSH, 1, nkt),
                           a_spec=BS((tk, D), lambda i, j, k: (k, 0)),
                           b_spec=BS((None, tk, FS), lambda i, j, k: (i, k, 0)),
                           out_shape=_sds((2 * NSH, D, FS), F32),
                           out_spec=BS((None, D, FS), lambda i, j, k: (i, 0, 0)),
                           acc_shape=(D, FS), name=f"dw_gu_{l}")
        dx1, small[f"norm2_g{l}"] = _rms_bwd(dh2, sv["x1"], p["norm2_g"][l:l + 1], dx, f"rms2_bwd_{l}")
        dcat = _mm("nt", dx1, w["wo"][l], grid=(T // TR, 1, 1), a_spec=BS((TR, D), row),
                   b_spec=BS((D, D), whole),
                   out_shape=_sds((T, D), F32), out_spec=BS((TR, D), row), acc_shape=(8, 128),
                   name=f"dcat_{l}")
        big["wo"][l] = _mm("tn", sv["cat"], dx1, grid=(1, 1, nkt),
                           a_spec=BS((tk, D), lambda i, j, k: (k, 0)), b_spec=BS((tk, D), lambda i, j, k: (k, 0)),
                           out_shape=_sds((D, D), F32), out_spec=BS((D, D), whole),
                           acc_shape=(D, D), name=f"dw_out_{l}")
        dqm, dkv, small[f"mq_g{l}"], small[f"mk_g{l}"] = _memattn_bwd(
            sv["z"], sv["kv"], dcat, p["mq_g4"][l:l + 1], p["mk_g4"][l:l + 1], sv["qcol"], f"memattn_bwd_{l}")
        if l == 0:
            carry = bwd_carry_fn(big) if bwd_carry_fn is not None else None
            (dq, dk, dv, dbias, small["a_q_g"], small["a_k_g"]), bwd_carried = _attn_bwd(
                sv["z"], dcat, bias, p["a_q_g2"], p["a_k_g2"], carry)
            small["rel_u"] = _bias_reduce(dbias)
            dz = jnp.concatenate([dq, dk, dv, dqm], axis=1)
            w_in, key, n, tn = w["a"], "a", NA, 640
        else:
            du, small["conv_w"], small["conv_b"], small["ln_g"], small["ln_b"], dbin_u = _conv_bwd(
                sv["z"], dcat, p["conv_w"], p["conv_b"], p["ln_g"], p["ln_b"])
            dz = jnp.concatenate([du, dqm], axis=1)
            small["b_in_u"] = dbin_u
            w_in, key, n, tn = w["b"], "b", NBW, 896
        dh = _mm("nt", dz, w_in, grid=(T // TR, 1, 1),
                 a_spec=BS((TR, n), lambda i, j, k: (i, 0)), b_spec=BS((D, n), lambda i, j, k: (0, 0)),
                 out_shape=_sds((T, D), F32), out_spec=BS((TR, D), row), acc_shape=(8, 128),
                 name=f"dh_{l}")
        big[key] = _mm("tn", sv["h"], dz, grid=(1, n // tn, nkt),
                       a_spec=BS((tk, D), lambda i, j, k: (k, 0)), b_spec=BS((tk, tn), lambda i, j, k: (k, j)),
                       out_shape=_sds((D, n), F32), out_spec=BS((D, tn), lambda i, j, k: (0, j)),
                       acc_shape=(D, tn), name=f"dw_in_{l}")
        if l == 1:
            small["b_in_qm"] = _colsum(dqm, "colsum_dqm")
        dx, small[f"norm1_g{l}"] = _rms_bwd(dh, sv["x"], p["norm1_g"][l:l + 1], dx1, f"rms1_bwd_{l}")
        big["kv"][l] = _mm("tn", sv["memn"], dkv, grid=(1, 1, 1),
                           a_spec=BS((NB * MEMT, D), whole), b_spec=BS((NB * MEMT, 2 * MEMW), whole),
                           out_shape=_sds((D, 2 * MEMW), F32), out_spec=BS((D, 2 * MEMW), whole),
                           acc_shape=(8, 128), name=f"dw_kv_{l}")
        dmemn = _mm("nt", dkv, w["kv"][l], grid=(1, 1, 1),
                    a_spec=BS((NB * MEMT, 2 * MEMW), whole), b_spec=BS((D, 2 * MEMW), whole),
                    out_shape=_sds((NB * MEMT, D), F32), out_spec=BS((NB * MEMT, D), whole),
                    acc_shape=(8, 128), name=f"dmemn_{l}")
        _, small[f"mem_norm_g{l}"] = _rms_bwd(dmemn, mem, p["mem_norm_g"][l:l + 1], None, f"rmsmem_bwd_{l}")
    return loss, dx, big, small, bwd_carried


def _colsum(a, name):
    rows, cols = a.shape

    def body(a_ref, o_ref):
        @pl.when(pl.program_id(0) == 0)
        def _():
            o_ref[...] = jnp.zeros_like(o_ref)

        o_ref[...] += jnp.sum(a_ref[...].astype(F32), axis=0, keepdims=True)

    return pl.pallas_call(
        body, grid=(rows // TR,), in_specs=[BS((TR, cols), lambda i: (i, 0))],
        out_specs=BS((1, cols), lambda i: (0, 0)), out_shape=_sds((1, cols), F32),
        compiler_params=_cp(("arbitrary",)), name=name,
    )(a)


_PACK_ROWS = 64


def _pad_to(a, rows, cols=D):
    return jnp.pad(a, ((0, rows - a.shape[0]), (0, cols - a.shape[1])))


def _pack_small(sm):
    parts = [
        jnp.concatenate([sm["norm1_g0"], sm["norm1_g1"]], 0),
        jnp.concatenate([sm["mem_norm_g0"], sm["mem_norm_g1"]], 0),
        jnp.concatenate([sm["norm2_g0"], sm["norm2_g1"]], 0),
        _pad_to(sm["a_q_g"], 1), _pad_to(sm["a_k_g"], 1),
        _pad_to(jnp.concatenate([sm["mq_g0"], sm["mq_g1"]], 0), 2),
        _pad_to(jnp.concatenate([sm["mk_g0"], sm["mk_g1"]], 0), 2),
        _pad_to(sm["conv_b"], 1), _pad_to(sm["ln_g"], 1), _pad_to(sm["ln_b"], 1),
        _pad_to(sm["b_in_u"][:, :D], 1), _pad_to(sm["b_in_u"][:, D:], 1),
        _pad_to(sm["b_in_qm"], 1),
        _pad_to(sm["conv_w"][:CONVW], CONVW),
        sm["rel_u"].reshape(12, D),
    ]
    pack = jnp.concatenate(parts, 0)
    return jnp.pad(pack, ((0, _PACK_ROWS - pack.shape[0]), (0, 0)))


def _rel_table_to_u(rel_bias):
    flat = jnp.concatenate([jnp.broadcast_to(rel_bias[:, 191:192], (12, 447)), rel_bias[:, ::-1]], axis=1)
    return jnp.pad(flat, ((0, 0), (192, 1024 - 192 - 639))).reshape(12, 1, 1024)


def _u_to_rel_table(du):
    flat = du[:, 192:192 + 639]
    g = flat[:, 447:][:, ::-1]
    return g, flat[:, :447]


def kernel(x, mem, norm1_g, mem_norm_g, a_w_in, a_q_g, a_k_g, a_rel_bias, b_w_in, b_b_in, b_conv_w, b_conv_b, b_ln_g, b_ln_b, mq_g, mk_g, w_mem_kv, w_out, norm2_g, w_gate, w_up, w_down, loss_target, m_norm1_g, m_mem_norm_g, m_a_w_in, m_a_q_g, m_a_k_g, m_a_rel_bias, m_b_w_in, m_b_b_in, m_b_conv_w, m_b_conv_b, m_b_ln_g, m_b_ln_b, m_mq_g, m_mk_g, m_w_mem_kv, m_w_out, m_norm2_g, m_w_gate, m_w_up, m_w_down, v_norm1_g, v_mem_norm_g, v_a_w_in, v_a_q_g, v_a_k_g, v_a_rel_bias, v_b_w_in, v_b_b_in, v_b_conv_w, v_b_conv_b, v_b_ln_g, v_b_ln_b, v_mq_g, v_mk_g, v_w_mem_kv, v_w_out, v_norm2_g, v_w_gate, v_w_up, v_w_down):
    sx = 2 * lax.axis_index("x") + lax.axis_index("y")

    n_in = (NA // NSH, NBW // NSH)
    w_in = (a_w_in, b_w_in)
    small_src = jnp.concatenate([
        jnp.pad(b_b_in, ((0, 0), (0, 512 - 448))),
        jnp.pad(b_conv_w[0], ((0, 0), (0, 512 - 192))),
        jnp.pad(jnp.concatenate([b_conv_b, b_ln_g, b_ln_b], 0), ((0, 0), (0, 512 - 192))),
        jnp.zeros((5, 512), F32)], 0)
    views = [lambda d: d[0], lambda d: d[1], lambda d: d[2],
             lambda d: d[3].at[:, 0], lambda d: d[3].at[:, 1], lambda d: d[4]]

    def layer_exchange(l, small):
        srcs = [
            _cast_bf16([w_in[l][0]], f"cast_in_{l}").reshape(1, 2, D // 2, n_in[l]),
            _cast_bf16([w_mem_kv[l]], f"cast_kv_{l}").reshape(1, 2, 128, 2 * MEMW),
            _cast_bf16([w_out[l]], f"cast_wo_{l}").reshape(1, 2, 128, D),
            _cast_bf16([w_gate[l]], f"cast_gate_{l}").reshape(1, 2, D // 2, FS),
            _cast_bf16([w_up[l]], f"cast_up_{l}").reshape(1, 2, D // 2, FS),
            _cast_bf16([w_down[l]], f"cast_wd_{l}").reshape(1, 2, FS // 2, D)]
        dst_shapes = [(1, NSH, 2, D // 2, n_in[l]), (1, NSH, 2, 128, 2 * MEMW), (1, NSH, 2, 128, D),
                      (1, 2, NSH, 2, D // 2, FS), (1, NSH, 2, FS // 2, D)]
        return _gather_exchange(srcs, dst_shapes, views, small)

    def gathered(l, outs):
        g_in, gkv, gwo, ggu, gwd = outs[:5]
        n = NSH * n_in[l]
        return dict(w_in=g_in.reshape(NSH, D, n_in[l]).transpose(1, 0, 2).reshape(D, n),
                    kv=gkv.reshape(D, 2 * MEMW), wo=gwo.reshape(D, D),
                    gu=ggu.reshape(2 * NSH, D, FS), wd=gwd.reshape(NSH, FS, D))

    outs0 = _run_exchange(layer_exchange(0, small_src), "gather_weights_0")
    small_all = outs0[5]
    w0 = gathered(0, outs0)
    w = dict(a=w0["w_in"], kv=[w0["kv"], None], wo=[w0["wo"], None], gu=[w0["gu"], None],
             wd=[w0["wd"], None])

    def fwd_carry_done(outs):
        w1 = gathered(1, outs)
        return dict(b=w1["w_in"], kv=[w0["kv"], w1["kv"]], wo=[w0["wo"], w1["wo"]],
                    gu=[w0["gu"], w1["gu"]], wd=[w0["wd"], w1["wd"]])

    conv_w_full = small_all[:, 1:1 + CONVW, :192].transpose(1, 0, 2).reshape(CONVW, TOK)
    vec3 = small_all[:, 32:35, :192].transpose(1, 0, 2).reshape(3, TOK)
    p = dict(
        norm1_g=norm1_g, mem_norm_g=mem_norm_g, norm2_g=norm2_g,
        a_q_g2=jnp.tile(a_q_g, (1, 2)), a_k_g2=jnp.tile(a_k_g, (1, 2)),
        mq_g4=jnp.tile(mq_g, (1, 4)), mk_g4=jnp.tile(mk_g, (1, 4)),
        rel_u=_rel_table_to_u(a_rel_bias[0]),
        b_b_in=small_all[:, 0, :448].reshape(1, NBW),
        conv_w=jnp.pad(conv_w_full, ((0, 1), (0, 0))),
        conv_b=vec3[0:1], ln_g=vec3[1:2], ln_b=vec3[2:3])

    def pair_sums(l, big):
        g_in = big["b"] if l else big["a"]
        units = [
            g_in.reshape(D, NSH, n_in[l]).transpose(1, 0, 2).reshape(1, NSH, 2, D // 2, n_in[l]),
            big["kv"][l].reshape(1, NSH, 2, 128, 2 * MEMW),
            big["wo"][l].reshape(1, NSH, 2, 128, D),
            big["gu"][l].reshape(2, NSH, 2, D // 2, FS),
            big["wd"][l].reshape(1, NSH, 2, FS // 2, D)]
        pairs = [_pair_reduce(g, f"pair_reduce_{l}_{u}") for u, g in enumerate(units)]
        return [pr[0] for pr in pairs], [pr[1] for pr in pairs]

    own = [None, None]

    def bwd_carry_fn(big):
        own[1], sums_b = pair_sums(1, big)
        return _chip_exchange(sums_b)

    loss, grad_x, big, small, parts1 = _local_step(
        x.reshape(T, D), mem.reshape(NB * MEMT, D), loss_target.reshape(T, D), w, p,
        fwd_carry=layer_exchange(1, None), fwd_carry_done=fwd_carry_done, bwd_carry_fn=bwd_carry_fn)
    loss = lax.psum(loss[0, 0], ("x", "y", "c"))

    own[0], sums_b0 = pair_sums(0, big)
    parts0 = _run_exchange(_chip_exchange(sums_b0), "chip_exchange_0")
    halves = [_quad_sum(own[l][u], parts[u], f"quad_sum_{l}_{u}")
              for l, parts in ((0, parts0), (1, parts1)) for u in range(5)]
    out_shapes = [(1, D, NA // NSH), (1, D, NBW // NSH), (2, 2 * 128, 2 * MEMW), (2, 2 * 128, D),
                  (2, D, FS), (2, D, FS), (2, FS, D)]
    targets = [[(0, 0, 0)], [(0, 2, 0)], [(0, 3, 0)], [(0, 4, 0), (1, 5, 0)], [(0, 6, 0)],
               [(0, 1, 0)], [(0, 2, 1)], [(0, 3, 1)], [(0, 4, 1), (1, 5, 1)], [(0, 6, 1)]]
    g_a, g_b, g_kv, g_wo, g_gate, g_up, g_wd = _final_exchange(halves, out_shapes, targets)

    tot = _small_allreduce(_pack_small(small))
    g_rel, clip_part = _u_to_rel_table(tot[49:61])
    g_rel = jnp.concatenate([g_rel[:, :191], g_rel[:, 191:] + _rowsum(clip_part)], axis=1)
    b_in_full = jnp.concatenate([tot[15:16], tot[16:17, :512], tot[17:18, :MEMW]], axis=1)
    g_small = dict(
        norm1_g=tot[0:2], mem_norm_g=tot[2:4], norm2_g=tot[4:6],
        a_q_g=tot[6:7, :HD], a_k_g=tot[7:8, :HD], a_rel_bias=g_rel[None],
        b_b_in=lax.dynamic_slice(b_in_full, (0, sx * 448), (1, 448)),
        b_conv_w=lax.dynamic_slice(tot[18:49, :TOK], (0, sx * 192), (CONVW, 192))[None],
        b_conv_b=lax.dynamic_slice(tot[12:13, :TOK], (0, sx * 192), (1, 192)),
        b_ln_g=lax.dynamic_slice(tot[13:14, :TOK], (0, sx * 192), (1, 192)),
        b_ln_b=lax.dynamic_slice(tot[14:15, :TOK], (0, sx * 192), (1, 192)),
        mq_g=tot[8:10, :HD], mk_g=tot[10:12, :HD])

    names = ["norm1_g", "mem_norm_g", "a_w_in", "a_q_g", "a_k_g", "a_rel_bias", "b_w_in", "b_b_in",
             "b_conv_w", "b_conv_b", "b_ln_g", "b_ln_b", "mq_g", "mk_g", "w_mem_kv", "w_out",
             "norm2_g", "w_gate", "w_up", "w_down"]
    weights = dict(zip(names, [norm1_g, mem_norm_g, a_w_in, a_q_g, a_k_g, a_rel_bias, b_w_in, b_b_in,
                               b_conv_w, b_conv_b, b_ln_g, b_ln_b, mq_g, mk_g, w_mem_kv, w_out,
                               norm2_g, w_gate, w_up, w_down]))
    ms = dict(zip(names, [m_norm1_g, m_mem_norm_g, m_a_w_in, m_a_q_g, m_a_k_g, m_a_rel_bias, m_b_w_in,
                          m_b_b_in, m_b_conv_w, m_b_conv_b, m_b_ln_g, m_b_ln_b, m_mq_g, m_mk_g,
                          m_w_mem_kv, m_w_out, m_norm2_g, m_w_gate, m_w_up, m_w_down]))
    vs = dict(zip(names, [v_norm1_g, v_mem_norm_g, v_a_w_in, v_a_q_g, v_a_k_g, v_a_rel_bias, v_b_w_in,
                          v_b_b_in, v_b_conv_w, v_b_conv_b, v_b_ln_g, v_b_ln_b, v_mq_g, v_mk_g,
                          v_w_mem_kv, v_w_out, v_norm2_g, v_w_gate, v_w_up, v_w_down]))
    grads = dict(g_small)
    grads.update(a_w_in=g_a, b_w_in=g_b, w_mem_kv=g_kv, w_out=g_wo, w_gate=g_gate, w_up=g_up, w_down=g_wd)
    big_names = ["a_w_in", "b_w_in", "w_mem_kv", "w_out", "w_gate", "w_up", "w_down"]
    small_names = [n for n in names if n not in big_names]
    delta, new_m, new_v = {}, {}, {}
    for n in big_names:
        delta[n], new_m[n], new_v[n] = _adamw_big(weights[n], grads[n], ms[n], vs[n], f"adamw_{n}")
    as2d = lambda a: a.reshape(-1, a.shape[-1])
    d_s, m_s, v_s = _adamw_small([as2d(weights[n]) for n in small_names], [as2d(grads[n]) for n in small_names],
                                 [as2d(ms[n]) for n in small_names], [as2d(vs[n]) for n in small_names])
    for i, n in enumerate(small_names):
        delta[n] = d_s[i].reshape(weights[n].shape)
        new_m[n] = m_s[i].reshape(weights[n].shape)
        new_v[n] = v_s[i].reshape(weights[n].shape)

    return (loss, grad_x.reshape(NB, SEQ, D), *[grads[n] for n in names], *[delta[n] for n in names],
            *[new_m[n] for n in names], *[new_v[n] for n in names])


def _rowsum(a):
    def body(a_ref, o_ref):
        o_ref[...] = jnp.sum(a_ref[...], axis=1, keepdims=True)

    vm = BS(memory_space=pltpu.VMEM)
    return pl.pallas_call(body, in_specs=[vm], out_specs=vm, out_shape=_sds((a.shape[0], 1), F32),
                          compiler_params=_cp(), name="rowsum")(a)
```

```python
import functools

import jax
import jax.numpy as jnp
from jax import lax
from jax.experimental import pallas as pl
from jax.experimental.pallas import tpu as pltpu

F32 = jnp.float32
BF16 = jnp.bfloat16
BS = pl.BlockSpec
ANY = pl.BlockSpec(memory_space=pl.ANY)
MESH = pl.DeviceIdType.MESH

D = 1024
SEQ = 2048
NB = 2
T = NB * SEQ
MEMT = 256
HD = 64
TOK = 768
MEMW = 256
NA = 3 * TOK + MEMW
NBW = 2 * TOK + MEMW
FF = 2816
NSH = 4
FS = FF // NSH
CONVW = 31
EPS = 1e-6
NEG = -1e30
SCALE = HD ** -0.5
QB = 256
KWIN = 768
KPAD = 512
TR = 512

ADAM_LR = 0.001
ADAM_B1 = 0.9
ADAM_B2 = 0.999
ADAM_EPS = 1e-08
ADAM_WD = 0.01
ADAM_STEP = 10

_DIMS = {
    "nn": (((1,), (0,)), ((), ())),
    "nt": (((1,), (1,)), ((), ())),
    "tn": (((0,), (0,)), ((), ())),
}


def _cp(sem=None, vmem_mb=48):
    return pltpu.CompilerParams(dimension_semantics=sem, vmem_limit_bytes=vmem_mb << 20)


def _sds(shape, dtype):
    return jax.ShapeDtypeStruct(tuple(shape), dtype)


def _mm(mode, a, b, *, grid, a_spec, b_spec, out_shape, out_spec, acc_shape, name,
        extras=(), extra_specs=(), epilogue=None, carry=None):
    n_ex = len(extras)
    nk = grid[2]
    dims = _DIMS[mode]
    ni = len(carry.ins) if carry else 0
    no = len(carry.out_shapes) if carry else 0

    def body(a_ref, b_ref, *rest):
        ex = rest[:n_ex]
        cin = rest[n_ex:n_ex + ni]
        o_ref = rest[n_ex + ni]
        cout = rest[n_ex + ni + 1:n_ex + ni + 1 + no]
        acc = rest[n_ex + ni + 1 + no]
        cscr = rest[n_ex + ni + 2 + no:]
        k = pl.program_id(2)
        if carry:
            ids = [pl.program_id(d) for d in range(3)]

            @pl.when((ids[0] == 0) & (ids[1] == 0) & (ids[2] == 0))
            def _():
                carry.start(cin, cout, cscr)

        prod = lax.dot_general(a_ref[...].astype(BF16), b_ref[...].astype(BF16), dims,
                               preferred_element_type=F32)

        def finish(val):
            if epilogue is None:
                o_ref[...] = val.astype(o_ref.dtype)
            else:
                epilogue(val, ex, o_ref)

        if nk == 1:
            finish(prod)
        else:
            @pl.when(k == 0)
            def _():
                acc[...] = prod

            @pl.when((k > 0) & (k < nk - 1))
            def _():
                acc[...] += prod

            @pl.when(k == nk - 1)
            def _():
                finish(acc[...] + prod)

        if carry:
            @pl.when((ids[0] == grid[0] - 1) & (ids[1] == grid[1] - 1) & (ids[2] == grid[2] - 1))
            def _():
                carry.finish(cin, cout, cscr)

    acc_scratch = pltpu.VMEM(acc_shape if nk > 1 else (8, 128), F32)
    if not carry:
        return pl.pallas_call(
            body, grid=grid, in_specs=[a_spec, b_spec, *extra_specs], out_specs=out_spec,
            out_shape=out_shape, scratch_shapes=[acc_scratch],
            compiler_params=_cp(("parallel", "parallel", "arbitrary")), name=name,
        )(a, b, *extras)
    outs = pl.pallas_call(
        body, grid=grid, in_specs=[a_spec, b_spec, *extra_specs, *carry.in_specs],
        out_specs=[out_spec] + [ANY] * no, out_shape=[out_shape] + carry.out_shapes,
        scratch_shapes=[acc_scratch] + carry.scratch,
        compiler_params=pltpu.CompilerParams(
            dimension_semantics=("arbitrary", "arbitrary", "arbitrary"), vmem_limit_bytes=48 << 20,
            has_side_effects=True), name=name,
    )(a, b, *extras, *carry.ins)
    return outs[0], outs[1:]


def _rms_fwd(x, g, name):
    rows = x.shape[0]

    def body(x_ref, g_ref, o_ref):
        xv = x_ref[...]
        r = lax.rsqrt(jnp.mean(xv * xv, axis=-1, keepdims=True) + EPS)
        o_ref[...] = (xv * r * g_ref[...]).astype(BF16)

    return pl.pallas_call(
        body, grid=(rows // TR,),
        in_specs=[BS((TR, D), lambda i: (i, 0)), BS((1, D), lambda i: (0, 0))],
        out_specs=BS((TR, D), lambda i: (i, 0)), out_shape=_sds((rows, D), BF16),
        compiler_params=_cp(("arbitrary",)), name=name,
    )(x, g)


def _rms_bwd(dh, x, g, dres, name):
    rows = x.shape[0]
    has_res = dres is not None

    def body(*refs):
        if has_res:
            dh_ref, x_ref, g_ref, r_ref, dx_ref, dg_ref = refs
        else:
            dh_ref, x_ref, g_ref, dx_ref, dg_ref = refs
        xv = x_ref[...]
        dhv = dh_ref[...]
        r = lax.rsqrt(jnp.mean(xv * xv, axis=-1, keepdims=True) + EPS)
        xh = xv * r
        gy = dhv * g_ref[...]
        dx = r * (gy - xh * jnp.mean(gy * xh, axis=-1, keepdims=True))
        if has_res:
            dx = dx + r_ref[...]
        dx_ref[...] = dx

        @pl.when(pl.program_id(0) == 0)
        def _():
            dg_ref[...] = jnp.zeros_like(dg_ref)

        dg_ref[...] += jnp.sum(dhv * xh, axis=0, keepdims=True)

    row = BS((TR, D), lambda i: (i, 0))
    vec = BS((1, D), lambda i: (0, 0))
    ins = [dh, x, g] + ([dres] if has_res else [])
    return pl.pallas_call(
        body, grid=(rows // TR,),
        in_specs=[row, row, vec] + ([row] if has_res else []),
        out_specs=[row, vec], out_shape=[_sds((rows, D), F32), _sds((1, D), F32)],
        compiler_params=_cp(("arbitrary",)), name=name,
    )(*ins)


def _group_masks(width):
    lane = lax.broadcasted_iota(jnp.int32, (1, width), 1)
    return [(lane >= HD * h) & (lane < HD * (h + 1)) for h in range(width // HD)]


def _group_mean(v, masks):
    out = jnp.zeros_like(v)
    for m in masks:
        s = jnp.sum(jnp.where(m, v, 0.0), axis=-1, keepdims=True) * (1.0 / HD)
        out = jnp.where(m, s, out)
    return out


def _head_norm(zv, g, masks):
    r = lax.rsqrt(_group_mean(zv * zv, masks) + EPS)
    return zv * r * g


def _head_norm_bwd(dy, zv, g, masks):
    r = lax.rsqrt(_group_mean(zv * zv, masks) + EPS)
    zh = zv * r
    gy = dy * g
    dz = r * (gy - zh * _group_mean(gy * zh, masks))
    return dz, jnp.sum(dy * zh, axis=0, keepdims=True)


def _fold_heads(v, width):
    vb = jnp.broadcast_to(v, (8, width))
    out = vb
    for h in range(1, width // HD):
        out = out + pltpu.roll(vb, width - HD * h, axis=1)
    return out[0:1]


def _bias_expand(u):
    def body(u_ref, o_ref):
        x = jnp.broadcast_to(u_ref[...], (QB, 1024))
        rolled = pltpu.roll(x, 1024 - (QB - 1), axis=1, stride=1, stride_axis=0)[:, :KWIN]
        row = lax.broadcasted_iota(jnp.int32, (QB, 1), 0)
        col = lax.broadcasted_iota(jnp.int32, (1, KWIN), 1)
        lo = (row // 64) * 64
        ok = (col >= lo) & (col < lo + 576)
        o_ref[...] = jnp.where(ok, rolled, NEG)

    return pl.pallas_call(
        body, grid=(12,), in_specs=[BS((None, 1, 1024), lambda h: (h, 0, 0))],
        out_specs=BS((None, QB, KWIN), lambda h: (h, 0, 0)), out_shape=_sds((12, QB, KWIN), F32),
        compiler_params=_cp(("arbitrary",)), name="bias_expand",
    )(u)


def _bias_reduce(ds):
    def body(d_ref, o_ref):
        ri = lax.broadcasted_iota(jnp.int32, (QB, QB), 0)
        ci = lax.broadcasted_iota(jnp.int32, (QB, QB), 1)
        flip = (ri + ci == QB - 1).astype(F32)
        drev = jnp.dot(flip, d_ref[...], precision=lax.Precision.HIGHEST, preferred_element_type=F32)
        x = jnp.concatenate([drev, jnp.zeros((QB, 1024 - KWIN), F32)], axis=1)
        rolled = pltpu.roll(x, 0, axis=1, stride=1, stride_axis=0)
        o_ref[...] = jnp.sum(rolled, axis=0, keepdims=True)

    return pl.pallas_call(
        body, grid=(12,), in_specs=[BS((None, QB, KWIN), lambda h: (h, 0, 0))],
        out_specs=BS((None, 1, 1024), lambda h: (h, 0, 0)), out_shape=_sds((12, 1, 1024), F32),
        compiler_params=_cp(("arbitrary",)), name="bias_reduce",
    )(ds)


def _attn_softmax(qh, kw, bias, startadd):
    s = lax.dot_general(qh, kw, _DIMS["nt"], preferred_element_type=F32) + bias + startadd
    m = jnp.max(s, axis=-1, keepdims=True)
    p = jnp.exp(s - m)
    return p * (1.0 / jnp.sum(p, axis=-1, keepdims=True))


def _attn_fwd(z, bias, gq2, gk2, carry=None):
    ni = len(carry.ins) if carry else 0
    no = len(carry.out_shapes) if carry else 0

    def body(q_ref, k_ref, v_ref, b_ref, gq_ref, gk_ref, *rest):
        cin, o_ref, cout = rest[:ni], rest[ni], rest[ni + 1:ni + 1 + no]
        qn_s, kn_s, v_s = rest[ni + 1 + no:ni + 4 + no]
        cscr = rest[ni + 4 + no:]
        if carry:
            @pl.when((pl.program_id(0) == 0) & (pl.program_id(1) == 0))
            def _():
                carry.start(cin, cout, cscr)

        masks = _group_masks(128)
        qn_s[...] = (_head_norm(q_ref[...], gq_ref[...], masks) * SCALE).astype(BF16)
        kn_s[0:KPAD, :] = jnp.zeros((KPAD, 128), BF16)
        kn_s[KPAD:, :] = _head_norm(k_ref[...], gk_ref[...], masks).astype(BF16)
        v_s[0:KPAD, :] = jnp.zeros((KPAD, 128), BF16)
        v_s[KPAD:, :] = v_ref[...].astype(BF16)
        col = lax.broadcasted_iota(jnp.int32, (1, KWIN), 1)

        def blk(i, carry):
            r0 = pl.multiple_of(i * QB, QB)
            qb = qn_s[pl.ds(r0, QB), :]
            kw = kn_s[pl.ds(r0, KWIN), :]
            vw = v_s[pl.ds(r0, KWIN), :]
            startadd = jnp.where(col + r0 < KPAD, NEG, 0.0)
            o = jnp.zeros((QB, 128), F32)
            for h in range(2):
                qh = jnp.where(masks[h], qb, jnp.zeros_like(qb))
                vh = jnp.where(masks[h], vw, jnp.zeros_like(vw))
                p = _attn_softmax(qh, kw, b_ref[h], startadd).astype(BF16)
                o = o + jnp.dot(p, vh, preferred_element_type=F32)
            o_ref[pl.ds(r0, QB), :] = o.astype(BF16)
            return carry

        lax.fori_loop(0, SEQ // QB, blk, 0)

        if carry:
            @pl.when((pl.program_id(0) == NB - 1) & (pl.program_id(1) == 5))
            def _():
                carry.finish(cin, cout, cscr)

    vec = BS((1, 128), lambda b, hp: (0, 0))
    outs = pl.pallas_call(
        body, grid=(NB, 6),
        in_specs=[BS((SEQ, 128), lambda b, hp: (b, hp)),
                  BS((SEQ, 128), lambda b, hp: (b, 6 + hp)),
                  BS((SEQ, 128), lambda b, hp: (b, 12 + hp)),
                  BS((2, QB, KWIN), lambda b, hp: (hp, 0, 0)), vec, vec]
        + (carry.in_specs if carry else []),
        out_specs=[BS((SEQ, 128), lambda b, hp: (b, hp))] + [ANY] * no,
        out_shape=[_sds((T, D), BF16)] + (carry.out_shapes if carry else []),
        scratch_shapes=[pltpu.VMEM((SEQ, 128), BF16), pltpu.VMEM((SEQ + KPAD, 128), BF16),
                        pltpu.VMEM((SEQ + KPAD, 128), BF16)] + (carry.scratch if carry else []),
        compiler_params=pltpu.CompilerParams(
            dimension_semantics=("arbitrary", "arbitrary"), vmem_limit_bytes=48 << 20,
            has_side_effects=bool(carry)), name="attn_fwd",
    )(z, z, z, bias, gq2, gk2, *(carry.ins if carry else []))
    return outs[0], outs[1:]


def _attn_bwd(z, dcat, bias, gq2, gk2, carry=None):
    ni = len(carry.ins) if carry else 0
    no = len(carry.out_shapes) if carry else 0

    def body(q_ref, k_ref, v_ref, do_ref, b_ref, gq_ref, gk_ref, *rest):
        cin = rest[:ni]
        dq_ref, dk_ref, dv_ref, db_ref, dgq_ref, dgk_ref = rest[ni:ni + 6]
        cout = rest[ni + 6:ni + 6 + no]
        qn_s, kn_s, v_s, dqn_s, dkn_s, dv_s = rest[ni + 6 + no:ni + 12 + no]
        cscr = rest[ni + 12 + no:]
        hp = pl.program_id(0)
        b = pl.program_id(1)
        if carry:
            @pl.when((hp == 0) & (b == 0))
            def _():
                carry.start(cin, cout, cscr)

        masks = _group_masks(128)
        qn_s[...] = (_head_norm(q_ref[...], gq_ref[...], masks) * SCALE).astype(BF16)
        kn_s[0:KPAD, :] = jnp.zeros((KPAD, 128), BF16)
        kn_s[KPAD:, :] = _head_norm(k_ref[...], gk_ref[...], masks).astype(BF16)
        v_s[0:KPAD, :] = jnp.zeros((KPAD, 128), BF16)
        v_s[KPAD:, :] = v_ref[...].astype(BF16)
        dkn_s[...] = jnp.zeros_like(dkn_s)
        dv_s[...] = jnp.zeros_like(dv_s)

        @pl.when(b == 0)
        def _():
            db_ref[...] = jnp.zeros_like(db_ref)

        @pl.when((b == 0) & (hp == 0))
        def _():
            dgq_ref[...] = jnp.zeros_like(dgq_ref)
            dgk_ref[...] = jnp.zeros_like(dgk_ref)

        col = lax.broadcasted_iota(jnp.int32, (1, KWIN), 1)

        def blk(i, carry):
            r0 = pl.multiple_of(i * QB, QB)
            qb = qn_s[pl.ds(r0, QB), :]
            kw = kn_s[pl.ds(r0, KWIN), :]
            vw = v_s[pl.ds(r0, KWIN), :]
            dob = do_ref[pl.ds(r0, QB), :].astype(BF16)
            startadd = jnp.where(col + r0 < KPAD, NEG, 0.0)
            dqn = jnp.zeros((QB, 128), F32)
            dkw = jnp.zeros((KWIN, 128), F32)
            dvw = jnp.zeros((KWIN, 128), F32)
            for h in range(2):
                qh = jnp.where(masks[h], qb, jnp.zeros_like(qb))
                kh = jnp.where(masks[h], kw, jnp.zeros_like(kw))
                doh = jnp.where(masks[h], dob, jnp.zeros_like(dob))
                p = _attn_softmax(qh, kw, b_ref[h], startadd)
                dvw = dvw + lax.dot_general(p.astype(BF16), doh, _DIMS["tn"],
                                            preferred_element_type=F32)
                dp = lax.dot_general(doh, vw, _DIMS["nt"], preferred_element_type=F32)
                ds = p * (dp - jnp.sum(dp * p, axis=-1, keepdims=True))
                db_ref[h] += ds
                dsb = ds.astype(BF16)
                dqn = dqn + jnp.dot(dsb, kh, preferred_element_type=F32)
                dkw = dkw + lax.dot_general(dsb, qh, _DIMS["tn"], preferred_element_type=F32)
            dqn_s[pl.ds(r0, QB), :] = dqn * SCALE
            dkn_s[pl.ds(r0, KWIN), :] += dkw
            dv_s[pl.ds(r0, KWIN), :] += dvw
            return carry

        lax.fori_loop(0, SEQ // QB, blk, 0)

        dq, dgq = _head_norm_bwd(dqn_s[...], q_ref[...], gq_ref[...], masks)
        dk, dgk = _head_norm_bwd(dkn_s[KPAD:, :], k_ref[...], gk_ref[...], masks)
        dq_ref[...] = dq.astype(BF16)
        dk_ref[...] = dk.astype(BF16)
        dv_ref[...] = dv_s[KPAD:, :].astype(BF16)
        dgq_ref[...] += _fold_heads(dgq, 128)
        dgk_ref[...] += _fold_heads(dgk, 128)

        if carry:
            @pl.when((hp == 5) & (b == NB - 1))
            def _():
                carry.finish(cin, cout, cscr)

    vec = BS((1, 128), lambda hp, b: (0, 0))
    row = BS((SEQ, 128), lambda hp, b: (b, hp))
    outs = pl.pallas_call(
        body, grid=(6, NB),
        in_specs=[row,
                  BS((SEQ, 128), lambda hp, b: (b, 6 + hp)),
                  BS((SEQ, 128), lambda hp, b: (b, 12 + hp)),
                  row,
                  BS((2, QB, KWIN), lambda hp, b: (hp, 0, 0)), vec, vec]
        + (carry.in_specs if carry else []),
        out_specs=[row, row, row, BS((2, QB, KWIN), lambda hp, b: (hp, 0, 0)), vec, vec] + [ANY] * no,
        out_shape=[_sds((T, TOK), BF16), _sds((T, TOK), BF16), _sds((T, TOK), BF16),
                   _sds((12, QB, KWIN), F32), _sds((1, 128), F32), _sds((1, 128), F32)]
        + (carry.out_shapes if carry else []),
        scratch_shapes=[pltpu.VMEM((SEQ, 128), BF16), pltpu.VMEM((SEQ + KPAD, 128), BF16),
                        pltpu.VMEM((SEQ + KPAD, 128), BF16), pltpu.VMEM((SEQ, 128), F32),
                        pltpu.VMEM((SEQ + KPAD, 128), F32), pltpu.VMEM((SEQ + KPAD, 128), F32)]
        + (carry.scratch if carry else []),
        compiler_params=pltpu.CompilerParams(
            dimension_semantics=("arbitrary", "arbitrary"), vmem_limit_bytes=48 << 20,
            has_side_effects=bool(carry)), name="attn_bwd",
    )(z, z, z, dcat, bias, gq2, gk2, *(carry.ins if carry else []))
    return outs[:6], outs[6:]


def _mem_softmax(qh, kn):
    s = lax.dot_general(qh, kn, _DIMS["nt"], preferred_element_type=F32)
    m = jnp.max(s, axis=-1, keepdims=True)
    p = jnp.exp(s - m)
    return p * (1.0 / jnp.sum(p, axis=-1, keepdims=True))


def _memattn_fwd(z, kv, cat, gq4, gk4, qcol, name):
    def body(q_ref, k_ref, v_ref, gq_ref, gk_ref, cat_ref, o_ref):
        del cat_ref
        masks = _group_masks(MEMW)
        qn = (_head_norm(q_ref[...], gq_ref[...], masks) * SCALE).astype(BF16)
        kn = _head_norm(k_ref[...], gk_ref[...], masks).astype(BF16)
        vv = v_ref[...].astype(BF16)
        o = jnp.zeros((TR, MEMW), F32)
        for h in range(4):
            qh = jnp.where(masks[h], qn, jnp.zeros_like(qn))
            vh = jnp.where(masks[h], vv, jnp.zeros_like(vv))
            p = _mem_softmax(qh, kn).astype(BF16)
            o = o + jnp.dot(p, vh, preferred_element_type=F32)
        o_ref[...] = o.astype(BF16)

    nt = SEQ // TR
    vec = BS((1, MEMW), lambda b, t: (0, 0))
    return pl.pallas_call(
        body, grid=(NB, nt),
        in_specs=[BS((TR, MEMW), lambda b, t: (b * nt + t, qcol)),
                  BS((MEMT, MEMW), lambda b, t: (b, 0)),
                  BS((MEMT, MEMW), lambda b, t: (b, 1)), vec, vec, ANY],
        out_specs=BS((TR, MEMW), lambda b, t: (b * nt + t, 3)),
        out_shape=_sds((T, D), BF16), input_output_aliases={5: 0},
        compiler_params=_cp(("arbitrary", "arbitrary")), name=name,
    )(z, kv, kv, gq4, gk4, cat)


def _memattn_bwd(z, kv, dcat, gq4, gk4, qcol, name):
    nt = SEQ // TR

    def body(q_ref, k_ref, v_ref, do_ref, gq_ref, gk_ref,
             dq_ref, dkv_ref, dgq_ref, dgk_ref, dkn_s, dv_s):
        b = pl.program_id(0)
        t = pl.program_id(1)
        masks = _group_masks(MEMW)
        qz = q_ref[...]
        kz = k_ref[...]
        qn = (_head_norm(qz, gq_ref[...], masks) * SCALE).astype(BF16)
        kn = _head_norm(kz, gk_ref[...], masks).astype(BF16)
        vv = v_ref[...].astype(BF16)
        dob = do_ref[...].astype(BF16)

        @pl.when(t == 0)
        def _():
            dkn_s[...] = jnp.zeros_like(dkn_s)
            dv_s[...] = jnp.zeros_like(dv_s)

        @pl.when((t == 0) & (b == 0))
        def _():
            dgq_ref[...] = jnp.zeros_like(dgq_ref)
            dgk_ref[...] = jnp.zeros_like(dgk_ref)

        dqn = jnp.zeros((TR, MEMW), F32)
        dkn = jnp.zeros((MEMT, MEMW), F32)
        dvv = jnp.zeros((MEMT, MEMW), F32)
        for h in range(4):
            qh = jnp.where(masks[h], qn, jnp.zeros_like(qn))
            kh = jnp.where(masks[h], kn, jnp.zeros_like(kn))
            doh = jnp.where(masks[h], dob, jnp.zeros_like(dob))
            p = _mem_softmax(qh, kn)
            dvv = dvv + lax.dot_general(p.astype(BF16), doh, _DIMS["tn"], preferred_element_type=F32)
            dp = lax.dot_general(doh, vv, _DIMS["nt"], preferred_element_type=F32)
            ds = p * (dp - jnp.sum(dp * p, axis=-1, keepdims=True))
            dsb = ds.astype(BF16)
            dqn = dqn + jnp.dot(dsb, kh, preferred_element_type=F32)
            dkn = dkn + lax.dot_general(dsb, qh, _DIMS["tn"], preferred_element_type=F32)
        dkn_s[...] += dkn
        dv_s[...] += dvv
        dq, dgq = _head_norm_bwd(dqn * SCALE, qz, gq_ref[...], masks)
        dq_ref[...] = dq.astype(BF16)
        dgq_ref[...] += _fold_heads(dgq, MEMW)

        @pl.when(t == nt - 1)
        def _():
            dk, dgk = _head_norm_bwd(dkn_s[...], kz, gk_ref[...], masks)
            dkv_ref[:, 0:MEMW] = dk
            dkv_ref[:, MEMW:] = dv_s[...]
            dgk_ref[...] += _fold_heads(dgk, MEMW)

    vec = BS((1, MEMW), lambda b, t: (0, 0))
    return pl.pallas_call(
        body, grid=(NB, nt),
        in_specs=[BS((TR, MEMW), lambda b, t: (b * nt + t, qcol)),
                  BS((MEMT, MEMW), lambda b, t: (b, 0)),
                  BS((MEMT, MEMW), lambda b, t: (b, 1)),
                  BS((TR, MEMW), lambda b, t: (b * nt + t, 3)), vec, vec],
        out_specs=[BS((TR, MEMW), lambda b, t: (b * nt + t, 0)),
                   BS((MEMT, 2 * MEMW), lambda b, t: (b, 0)), vec, vec],
        out_shape=[_sds((T, MEMW), BF16), _sds((NB * MEMT, 2 * MEMW), F32),
                   _sds((1, MEMW), F32), _sds((1, MEMW), F32)],
        scratch_shapes=[pltpu.VMEM((MEMT, MEMW), F32), pltpu.VMEM((MEMT, MEMW), F32)],
        compiler_params=_cp(("arbitrary", "arbitrary")), name=name,
    )(z, kv, kv, dcat, gq4, gk4)


HALO = 32
NEXT = 64
RT = 64


def _glu(zz):
    return zz[:, :TOK] * jax.nn.sigmoid(zz[:, TOK:])


def _layer_norm_parts(y):
    mu = jnp.mean(y, axis=-1, keepdims=True)
    yc = y - mu
    rstd = lax.rsqrt(jnp.mean(yc * yc, axis=-1, keepdims=True) + EPS)
    return yc * rstd, rstd


def _shifted_copies(src, dst, rows):
    for b in range(1, 8):
        dst[b - 1, 0:rows, :] = src[b:b + rows, :]


def _tap(src, shifted, off, r0, rows):
    b = off % 8
    if b == 0:
        return src[r0 + off:r0 + off + rows, :]
    return shifted[b - 1, r0 + off - b:r0 + off - b + rows, :]


def _conv_rows(w_ref, hbuf, hs, r0, rows):
    y = jnp.zeros((rows, TOK), F32)
    for j in range(CONVW):
        y = y + w_ref[j:j + 1, :] * _tap(hbuf, hs, (HALO - CONVW + 1) + j, r0, rows)
    return y


def _conv_fwd(z, cw, cb, lg, lb):
    nt = SEQ // TR

    def body(zc_ref, zp_ref, w_ref, cb_ref, lg_ref, lb_ref, o_ref, hbuf, hs):
        t = pl.program_id(1)
        hbuf[0:HALO, :] = jnp.where(t == 0, 0.0, _glu(zp_ref[...]))
        hbuf[HALO:, :] = _glu(zc_ref[...])
        _shifted_copies(hbuf, hs, HALO + TR - 8)
        for r0 in range(0, TR, RT):
            y = _conv_rows(w_ref, hbuf, hs, r0, RT) + cb_ref[...]
            yh, _ = _layer_norm_parts(y)
            o = yh * lg_ref[...] + lb_ref[...]
            o_ref[r0:r0 + RT, :] = (o * jax.nn.sigmoid(o)).astype(BF16)

    vec = BS((1, TOK), lambda b, t: (0, 0))
    per = TR // HALO
    return pl.pallas_call(
        body, grid=(NB, nt),
        in_specs=[BS((TR, 2 * TOK), lambda b, t: (b * nt + t, 0)),
                  BS((HALO, 2 * TOK), lambda b, t: (jnp.maximum((b * nt + t) * per - 1, 0), 0)),
                  BS((32, TOK), lambda b, t: (0, 0)), vec, vec, vec],
        out_specs=BS((TR, TOK), lambda b, t: (b * nt + t, 0)),
        out_shape=_sds((T, D), BF16),
        scratch_shapes=[pltpu.VMEM((HALO + TR, TOK), F32), pltpu.VMEM((7, HALO + TR, TOK), F32)],
        compiler_params=_cp(("arbitrary", "arbitrary")), name="conv_fwd",
    )(z, z, cw, cb, lg, lb)


def _conv_bwd(z, dcat, cw, cb, lg, lb):
    nt = SEQ // TR
    ext = TR + NEXT

    def body(zc_ref, zp_ref, zn_ref, dc_ref, dn_ref, w_ref, cb_ref, lg_ref, lb_ref,
             du_ref, dw_ref, dcb_ref, dlg_ref, dlb_ref, dbin_ref, hbuf, dybuf, hs, dys):
        b = pl.program_id(0)
        t = pl.program_id(1)

        @pl.when((b == 0) & (t == 0))
        def _():
            dw_ref[...] = jnp.zeros_like(dw_ref)
            dcb_ref[...] = jnp.zeros_like(dcb_ref)
            dlg_ref[...] = jnp.zeros_like(dlg_ref)
            dlb_ref[...] = jnp.zeros_like(dlb_ref)
            dbin_ref[...] = jnp.zeros_like(dbin_ref)

        hbuf[0:HALO, :] = jnp.where(t == 0, 0.0, _glu(zp_ref[...]))
        hbuf[HALO:HALO + TR, :] = _glu(zc_ref[...])
        hbuf[HALO + TR:, :] = _glu(zn_ref[...])
        _shifted_copies(hbuf, hs, HALO + TR + NEXT - 8)
        last = t == nt - 1
        for r0 in range(0, ext, RT):
            y = _conv_rows(w_ref, hbuf, hs, r0, RT) + cb_ref[...]
            yh, rstd = _layer_norm_parts(y)
            o = yh * lg_ref[...] + lb_ref[...]
            sg = jax.nn.sigmoid(o)
            if r0 < TR:
                dtok = dc_ref[r0:r0 + RT, :]
            else:
                dtok = jnp.where(last, 0.0, dn_ref[...])
            do = dtok * (sg * (1.0 + o * (1.0 - sg)))
            dyh = do * lg_ref[...]
            dy = rstd * (dyh - jnp.mean(dyh, axis=-1, keepdims=True)
                         - yh * jnp.mean(dyh * yh, axis=-1, keepdims=True))
            dybuf[r0:r0 + RT, :] = dy
            if r0 < TR:
                dlg_ref[...] += jnp.sum(do * yh, axis=0, keepdims=True)
                dlb_ref[...] += jnp.sum(do, axis=0, keepdims=True)
                dcb_ref[...] += jnp.sum(dy, axis=0, keepdims=True)
        _shifted_copies(dybuf, dys, ext - 8)
        for r0 in range(0, TR, RT):
            dh = jnp.zeros((RT, TOK), F32)
            for j in range(CONVW):
                dh = dh + w_ref[j:j + 1, :] * _tap(dybuf, dys, (CONVW - 1) - j, r0, RT)
            a = zc_ref[r0:r0 + RT, 0:TOK]
            sg = jax.nn.sigmoid(zc_ref[r0:r0 + RT, TOK:])
            da = dh * sg
            dg = dh * a * (sg * (1.0 - sg))
            du_ref[r0:r0 + RT, 0:TOK] = da.astype(BF16)
            du_ref[r0:r0 + RT, TOK:] = dg.astype(BF16)
            dbin_ref[:, 0:TOK] += jnp.sum(da, axis=0, keepdims=True)
            dbin_ref[:, TOK:] += jnp.sum(dg, axis=0, keepdims=True)
        for j in range(CONVW):
            acc = jnp.zeros((8, TOK), F32)
            for r0 in range(0, TR, RT):
                prod = dybuf[r0:r0 + RT, :] * _tap(hbuf, hs, (HALO - CONVW + 1) + j, r0, RT)
                acc = acc + jnp.sum(prod.reshape(RT // 8, 8, TOK), axis=0)
            dw_ref[j:j + 1, :] += jnp.sum(acc, axis=0, keepdims=True)

    vec = BS((1, TOK), lambda b, t: (0, 0))
    perh = TR // HALO
    pern = TR // NEXT
    nlast_n = T // NEXT - 1
    return pl.pallas_call(
        body, grid=(NB, nt),
        in_specs=[BS((TR, 2 * TOK), lambda b, t: (b * nt + t, 0)),
                  BS((HALO, 2 * TOK), lambda b, t: (jnp.maximum((b * nt + t) * perh - 1, 0), 0)),
                  BS((NEXT, 2 * TOK), lambda b, t: (jnp.minimum((b * nt + t + 1) * pern, nlast_n), 0)),
                  BS((TR, TOK), lambda b, t: (b * nt + t, 0)),
                  BS((NEXT, TOK), lambda b, t: (jnp.minimum((b * nt + t + 1) * pern, nlast_n), 0)),
                  BS((32, TOK), lambda b, t: (0, 0)), vec, vec, vec],
        out_specs=[BS((TR, 2 * TOK), lambda b, t: (b * nt + t, 0)),
                   BS((32, TOK), lambda b, t: (0, 0)), vec, vec, vec,
                   BS((1, 2 * TOK), lambda b, t: (0, 0))],
        out_shape=[_sds((T, 2 * TOK), BF16), _sds((32, TOK), F32), _sds((1, TOK), F32),
                   _sds((1, TOK), F32), _sds((1, TOK), F32), _sds((1, 2 * TOK), F32)],
        scratch_shapes=[pltpu.VMEM((HALO + TR + NEXT, TOK), F32), pltpu.VMEM((ext, TOK), F32),
                        pltpu.VMEM((7, HALO + TR + NEXT, TOK), F32), pltpu.VMEM((7, ext, TOK), F32)],
        compiler_params=_cp(("arbitrary", "arbitrary"), vmem_mb=56), name="conv_bwd",
    )(z, z, z, dcat, dcat, cw, cb, lg, lb)


def _ffn_up(h2, wgu, l, carry=None):
    ni = len(carry.ins) if carry else 0
    no = len(carry.out_shapes) if carry else 0

    def body(h_ref, wg_ref, wu_ref, *rest):
        cin = rest[:ni]
        g_ref, u_ref, a_ref = rest[ni:ni + 3]
        cout, cscr = rest[ni + 3:ni + 3 + no], rest[ni + 3 + no:]
        if carry:
            @pl.when((pl.program_id(0) == 0) & (pl.program_id(1) == 0))
            def _():
                carry.start(cin, cout, cscr)

        hv = h_ref[...]
        g = jnp.dot(hv, wg_ref[...], preferred_element_type=F32)
        u = jnp.dot(hv, wu_ref[...], preferred_element_type=F32)
        g_ref[...] = g
        u_ref[...] = u
        a_ref[...] = (g * jax.nn.sigmoid(g) * u).astype(BF16)

        if carry:
            @pl.when((pl.program_id(0) == NSH - 1) & (pl.program_id(1) == T // TR - 1))
            def _():
                carry.finish(cin, cout, cscr)

    out = BS((None, TR, FS), lambda s, i: (s, i, 0))
    outs = pl.pallas_call(
        body, grid=(NSH, T // TR),
        in_specs=[BS((TR, D), lambda s, i: (i, 0)),
                  BS((None, D, FS), lambda s, i: (s, 0, 0)),
                  BS((None, D, FS), lambda s, i: (NSH + s, 0, 0))] + (carry.in_specs if carry else []),
        out_specs=[out, out, out] + [ANY] * no,
        out_shape=[_sds((NSH, T, FS), F32), _sds((NSH, T, FS), F32), _sds((NSH, T, FS), BF16)]
        + (carry.out_shapes if carry else []),
        scratch_shapes=carry.scratch if carry else [],
        compiler_params=pltpu.CompilerParams(
            dimension_semantics=("arbitrary", "arbitrary"), vmem_limit_bytes=48 << 20,
            has_side_effects=bool(carry)), name=f"ffn_up_{l}",
    )(h2, wgu, wgu, *(carry.ins if carry else []))
    return outs[:3], outs[3:]


def _ffn_down_bwd(dx, wd, g, u, l):
    def epilogue(dact, ex, o_ref):
        gv = ex[0][...]
        uv = ex[1][...]
        sg = jax.nn.sigmoid(gv)
        o_ref[0] = (dact * uv * (sg * (1.0 + gv * (1.0 - sg)))).astype(BF16)
        o_ref[1] = (dact * (gv * sg)).astype(BF16)

    ex_spec = BS((None, TR, FS), lambda i, s, k: (s, i, 0))
    return _mm("nt", dx, wd, grid=(T // TR, NSH, 1),
               a_spec=BS((TR, D), lambda i, s, k: (i, 0)),
               b_spec=BS((None, FS, D), lambda i, s, k: (s, 0, 0)),
               out_shape=_sds((2, NSH, T, FS), BF16),
               out_spec=BS((2, None, TR, FS), lambda i, s, k: (0, s, i, 0)),
               acc_shape=(TR, FS), extras=(g, u), extra_specs=(ex_spec, ex_spec),
               epilogue=epilogue, name=f"ffn_down_bwd_{l}")


def _loss_head(y, target):
    def body(y_ref, t_ref, l_ref, dy_ref, acc):
        i = pl.program_id(0)
        e = y_ref[...] - t_ref[...]
        dy_ref[...] = e * (1.0 / D)

        @pl.when(i == 0)
        def _():
            acc[...] = jnp.zeros_like(acc)

        acc[...] += jnp.sum(jnp.mean(e * e, axis=-1, keepdims=True), axis=0, keepdims=True)

        @pl.when(i == T // TR - 1)
        def _():
            l_ref[...] = 0.5 * acc[...]

    row = BS((TR, D), lambda i: (i, 0))
    return pl.pallas_call(
        body, grid=(T // TR,), in_specs=[row, row],
        out_specs=[BS((1, 1), lambda i: (0, 0)), row],
        out_shape=[_sds((1, 1), F32), _sds((T, D), F32)],
        scratch_shapes=[pltpu.VMEM((1, 1), F32)],
        compiler_params=_cp(("arbitrary",)), name="loss_head",
    )(y, target)


def _row_tile(rows, cols, itemsize=4, limit=2 << 20):
    tr = rows
    while tr * cols * itemsize > limit and tr % 2 == 0 and (tr // 2) % 16 == 0:
        tr //= 2
    return tr


def _cast_bf16(arrs, name):
    n = len(arrs)
    rows, cols = arrs[0].shape
    tr = _row_tile(rows, cols)

    def body(*refs):
        o_ref = refs[n]
        k = pl.program_id(0)
        val = refs[0][...]
        for j in range(1, n):
            val = jnp.where(k == j, refs[j][...], val)
        o_ref[...] = val.astype(BF16)

    return pl.pallas_call(
        body, grid=(n, rows // tr),
        in_specs=[BS((tr, cols), lambda k, i: (i, 0))] * n,
        out_specs=BS((None, tr, cols), lambda k, i: (k, i, 0)),
        out_shape=_sds((n, rows, cols), BF16),
        compiler_params=_cp(("arbitrary", "arbitrary")), name=name,
    )(*arrs)


def _quad_sum(own, got, name):
    n, rows, cols = own.shape
    tr = _row_tile(rows, cols)

    def body(a_ref, q_ref, o_ref):
        o_ref[...] = ((a_ref[...] + q_ref[0].astype(F32)) + q_ref[1].astype(F32)) + q_ref[2].astype(F32)

    spec = BS((None, tr, cols), lambda k, i: (k, i, 0))
    return pl.pallas_call(
        body, grid=(n, rows // tr),
        in_specs=[spec, BS((3, None, tr, cols), lambda k, i: (0, k, i, 0))], out_specs=spec,
        out_shape=_sds((n, rows, cols), F32),
        compiler_params=_cp(("arbitrary", "arbitrary")), name=name,
    )(own, got)


def _adam_math(w, g, m, v):
    m = ADAM_B1 * m + (1.0 - ADAM_B1) * g
    v = ADAM_B2 * v + (1.0 - ADAM_B2) * (g * g)
    m_hat = m / (1.0 - ADAM_B1 ** ADAM_STEP)
    v_hat = v / (1.0 - ADAM_B2 ** ADAM_STEP)
    delta = -ADAM_LR * (m_hat / (jnp.sqrt(v_hat) + ADAM_EPS) + ADAM_WD * w)
    return delta, m, v


def _adamw_big(w, g, m, v, name):
    shape = w.shape
    cols = shape[-1]
    rows = w.size // cols
    tr = _row_tile(rows, cols, limit=1 << 20)

    def body(w_ref, g_ref, m_ref, v_ref, d_ref, nm_ref, nv_ref):
        d, nm, nv = _adam_math(w_ref[...], g_ref[...], m_ref[...], v_ref[...])
        d_ref[...] = d
        nm_ref[...] = nm
        nv_ref[...] = nv

    spec = BS((tr, cols), lambda i: (i, 0))
    outs = pl.pallas_call(
        body, grid=(rows // tr,), in_specs=[spec] * 4, out_specs=[spec] * 3,
        out_shape=[_sds((rows, cols), F32)] * 3,
        compiler_params=_cp(("arbitrary",)), name=name,
    )(*[a.reshape(rows, cols) for a in (w, g, m, v)])
    return [o.reshape(shape) for o in outs]


def _adamw_small(ws, gs, ms, vs):
    n = len(ws)

    def body(*refs):
        for i in range(n):
            d, nm, nv = _adam_math(refs[i][...], refs[n + i][...], refs[2 * n + i][...],
                                   refs[3 * n + i][...])
            refs[4 * n + i][...] = d
            refs[5 * n + i][...] = nm
            refs[6 * n + i][...] = nv

    vm = BS(memory_space=pltpu.VMEM)
    outs = pl.pallas_call(
        body, in_specs=[vm] * (4 * n), out_specs=[vm] * (3 * n),
        out_shape=[_sds(w.shape, F32) for w in ws] * 3,
        compiler_params=_cp(), name="adamw_small",
    )(*ws, *gs, *ms, *vs)
    return outs[:n], outs[n:2 * n], outs[2 * n:]


def _place():
    x, y, c = lax.axis_index("x"), lax.axis_index("y"), lax.axis_index("c")
    chips = [(1 - x, y), (x, 1 - y), (1 - x, 1 - y)]
    return x, y, c, chips


class _Exchange:
    def __init__(self, ins, in_specs, out_shapes, scratch, start, finish):
        self.ins, self.in_specs, self.out_shapes, self.scratch = ins, in_specs, out_shapes, scratch
        self.start, self.finish = start, finish


def _run_exchange(ex, name, vmem_mb=40):
    ni, no = len(ex.ins), len(ex.out_shapes)

    def body(*refs):
        ex.start(refs[:ni], refs[ni:ni + no], refs[ni + no:])
        ex.finish(refs[:ni], refs[ni:ni + no], refs[ni + no:])

    return pl.pallas_call(
        body, in_specs=ex.in_specs, out_specs=[ANY] * no, out_shape=ex.out_shapes,
        scratch_shapes=ex.scratch,
        compiler_params=pltpu.CompilerParams(has_side_effects=True, vmem_limit_bytes=vmem_mb << 20),
        name=name,
    )(*ex.ins)


def _gather_exchange(srcs, dst_shapes, views, small=None):
    nu = len(srcs)
    nd = len(dst_shapes)
    ns = 1 if small is not None else 0

    def unpack(ins, outs, scr):
        x, y, c, chips = _place()
        src = ins[:nu]
        vw = [views[u](outs[:nd]) for u in range(nu)]
        vbuf = scr[:nu]
        send, recv, fsend, frecv, lsem, ssend, srecv, vsem = scr[nu:]

        def ici(u, j, shard, to):
            return pltpu.make_async_remote_copy(
                src_ref=vbuf[u].at[:, c], dst_ref=vw[u].at[:, shard, c],
                send_sem=send.at[3 * u + j], recv_sem=recv.at[3 * u + j],
                device_id=to, device_id_type=MESH)

        def fwd(u, j, shard, half):
            return pltpu.make_async_remote_copy(
                src_ref=vw[u].at[:, shard, half], dst_ref=vw[u].at[:, shard, half],
                send_sem=fsend.at[3 * u + j], recv_sem=frecv.at[3 * u + j],
                device_id=(x, y, 1 - c), device_id_type=MESH)

        def small_copy(j, shard, to):
            return pltpu.make_async_remote_copy(
                src_ref=ins[nu], dst_ref=outs[nd].at[shard],
                send_sem=ssend.at[j], recv_sem=srecv.at[j], device_id=to, device_id_type=MESH)

        stage = [pltpu.make_async_copy(src[u], vbuf[u], vsem.at[u]) for u in range(nu)]
        local = [pltpu.make_async_copy(vbuf[u], vw[u].at[:, 2 * x + y], lsem.at[u]) for u in range(nu)]
        if ns:
            local.append(pltpu.make_async_copy(ins[nu], outs[nd].at[2 * x + y], lsem.at[nu]))
        return x, y, c, chips, ici, fwd, small_copy, stage, local

    def start(ins, outs, scr):
        x, y, c, chips, ici, fwd, small_copy, stage, local = unpack(ins, outs, scr)
        s = 2 * x + y
        for cp in stage:
            cp.start()
        if ns:
            local[nu].start()
            for j, chip in enumerate(chips):
                small_copy(j, s, (*chip, c)).start()
        for u in range(nu):
            stage[u].wait()
            for j, chip in enumerate(chips):
                ici(u, j, s, (*chip, c)).start()
            local[u].start()

    def finish(ins, outs, scr):
        x, y, c, chips, ici, fwd, small_copy, stage, local = unpack(ins, outs, scr)
        s = 2 * x + y
        for u in range(nu):
            for j, chip in enumerate(chips):
                sj = 2 * chip[0] + chip[1]
                ici(u, j, sj, (x, y, c)).wait_recv()
                fwd(u, j, sj, c).start()
        for u in range(nu):
            for j, chip in enumerate(chips):
                sj = 2 * chip[0] + chip[1]
                fwd(u, j, sj, 1 - c).wait_recv()
        for u in range(nu):
            for j, chip in enumerate(chips):
                ici(u, j, s, (*chip, c)).wait_send()
                fwd(u, j, s, c).wait_send()
        if ns:
            for j, chip in enumerate(chips):
                small_copy(j, 2 * chip[0] + chip[1], (x, y, c)).wait_recv()
                small_copy(j, s, (*chip, c)).wait_send()
        for cp in local:
            cp.wait()

    dma = pltpu.SemaphoreType.DMA
    return _Exchange(
        ins=list(srcs) + ([small] if ns else []),
        in_specs=[ANY] * nu + [BS(memory_space=pltpu.VMEM)] * ns,
        out_shapes=[_sds(sh, BF16) for sh in dst_shapes]
        + ([_sds((NSH,) + small.shape, F32)] if ns else []),
        scratch=[pltpu.VMEM(a.shape, BF16) for a in srcs]
        + [dma((3 * nu,)), dma((3 * nu,)), dma((3 * nu,)), dma((3 * nu,)),
           dma((nu + 1,)), dma((3,)), dma((3,)), dma((nu,))],
        start=start, finish=finish)


def _pair_reduce(g, name):
    pn, _, _, rh, cc = g.shape
    n = pn * NSH

    def body(g_ref, own_ref, sb_ref, sendb, recvb, stage, outf, outb, send, recv, lsem, osem):
        x, y, c, _ = _place()
        s = 2 * x + y

        def load(k, half):
            return pltpu.make_async_copy(g_ref.at[k // NSH, k % NSH, half], stage.at[k % 2], lsem.at[k % 2])

        def push(k):
            return pltpu.make_async_remote_copy(
                src_ref=sendb.at[k], dst_ref=recvb.at[k], send_sem=send.at[k], recv_sem=recv.at[k],
                device_id=(x, y, 1 - c), device_id_type=MESH)

        def store(k):
            return pltpu.make_async_copy(outb.at[k % 2], sb_ref.at[k // NSH, k % NSH], osem.at[k % 2])

        load(0, 1 - c).start()
        for k in range(n):
            if k + 1 < n:
                load(k + 1, 1 - c).start()
            load(k, 1 - c).wait()
            sendb[k] = stage[k % 2].astype(BF16)
            push(k).start()
        load(0, c).start()
        for k in range(n):
            if k + 1 < n:
                load(k + 1, c).start()
            load(k, c).wait()
            push(k).wait_recv()
            total = stage[k % 2] + recvb[k].astype(F32)
            if k >= 2:
                store(k - 2).wait()
            outb[k % 2] = total.astype(BF16)
            store(k).start()

            @pl.when(s == k % NSH)
            def _():
                outf[...] = total
                keep = pltpu.make_async_copy(outf, own_ref.at[k // NSH], osem.at[2])
                keep.start()
                keep.wait()

        for k in range(max(n - 2, 0), n):
            store(k).wait()
        for k in range(n):
            push(k).wait_send()

    dma = pltpu.SemaphoreType.DMA
    return pl.pallas_call(
        body, in_specs=[ANY], out_specs=[ANY, ANY],
        out_shape=[_sds((pn, rh, cc), F32), _sds((pn, NSH, rh, cc), BF16)],
        scratch_shapes=[pltpu.VMEM((n, rh, cc), BF16), pltpu.VMEM((n, rh, cc), BF16),
                        pltpu.VMEM((2, rh, cc), F32), pltpu.VMEM((rh, cc), F32),
                        pltpu.VMEM((2, rh, cc), BF16), dma((n,)), dma((n,)), dma((2,)), dma((3,))],
        compiler_params=pltpu.CompilerParams(has_side_effects=True, vmem_limit_bytes=56 << 20),
        name=name,
    )(g)


def _chip_exchange(sums_bf16):
    nu = len(sums_bf16)

    def pushes(ins, outs, scr):
        x, y, c, chips = _place()
        send, recv = scr
        return [pltpu.make_async_remote_copy(
            src_ref=ins[u].at[:, 2 * chip[0] + chip[1]], dst_ref=outs[u].at[j],
            send_sem=send.at[3 * u + j], recv_sem=recv.at[3 * u + j],
            device_id=(*chip, c), device_id_type=MESH)
            for u in range(nu) for j, chip in enumerate(chips)]

    def start(ins, outs, scr):
        for cp in pushes(ins, outs, scr):
            cp.start()

    def finish(ins, outs, scr):
        for cp in pushes(ins, outs, scr):
            cp.wait()

    dma = pltpu.SemaphoreType.DMA
    shapes = [(3, a.shape[0], a.shape[2], a.shape[3]) for a in sums_bf16]
    return _Exchange(ins=list(sums_bf16), in_specs=[ANY] * nu,
                     out_shapes=[_sds(sh, BF16) for sh in shapes],
                     scratch=[dma((3 * nu,)), dma((3 * nu,))], start=start, finish=finish)


def _final_exchange(halves, out_shapes, targets):
    nu = len(halves)
    no = len(out_shapes)
    ncp = sum(len(t) for t in targets)

    def body(*refs):
        hv = refs[:nu]
        out = refs[nu:nu + no]
        sbuf = refs[nu + no:2 * nu + no]
        rbuf = refs[2 * nu + no:3 * nu + no]
        send, recv, lsem, osem, csem = refs[3 * nu + no:]
        x, y, c, _ = _place()
        stage = [pltpu.make_async_copy(hv[u], sbuf[u], lsem.at[u]) for u in range(nu)]
        push = [pltpu.make_async_remote_copy(
            src_ref=sbuf[u], dst_ref=rbuf[u], send_sem=send.at[u], recv_sem=recv.at[u],
            device_id=(x, y, 1 - c), device_id_type=MESH) for u in range(nu)]
        mine, theirs = [], []
        k = 0
        for u in range(nu):
            rh = hv[u].shape[1]
            for (p, oi, li) in targets[u]:
                mine.append((u, pltpu.make_async_copy(
                    sbuf[u].at[p], out[oi].at[li, pl.ds(c * rh, rh), :], csem.at[k])))
                theirs.append((u, pltpu.make_async_copy(
                    rbuf[u].at[p], out[oi].at[li, pl.ds((1 - c) * rh, rh), :], osem.at[k])))
                k += 1
        for cp in stage:
            cp.start()
        for u in range(nu):
            stage[u].wait()
            push[u].start()
            for (v, cp) in mine:
                if v == u:
                    cp.start()
        for u in range(nu):
            push[u].wait_recv()
            for (v, cp) in theirs:
                if v == u:
                    cp.start()
        for (_, cp) in theirs + mine:
            cp.wait()
        for u in range(nu):
            push[u].wait_send()

    dma = pltpu.SemaphoreType.DMA
    bufs = [pltpu.VMEM(h.shape, F32) for h in halves]
    return pl.pallas_call(
        body, in_specs=[ANY] * nu, out_specs=[ANY] * no,
        out_shape=[_sds(sh, F32) for sh in out_shapes],
        scratch_shapes=bufs + bufs + [dma((nu,)), dma((nu,)), dma((nu,)), dma((ncp,)), dma((ncp,))],
        compiler_params=pltpu.CompilerParams(has_side_effects=True, vmem_limit_bytes=56 << 20),
        name="final_exchange",
    )(*halves)


def _small_allreduce(pack):
    rows = pack.shape[0]

    def body(p_ref, o_ref, buf, send, recv):
        x, y, c, _ = _place()
        me = 4 * x + 2 * y + c
        buf[me] = p_ref[...]
        k = 0
        copies = []
        for dx in range(2):
            for dy in range(2):
                for dc in range(2):
                    if dx == 0 and dy == 0 and dc == 0:
                        continue
                    to = (jnp.where(dx, 1 - x, x), jnp.where(dy, 1 - y, y), jnp.where(dc, 1 - c, c))
                    src_slot = 4 * to[0] + 2 * to[1] + to[2]
                    copies.append((pltpu.make_async_remote_copy(
                        src_ref=p_ref, dst_ref=buf.at[me], send_sem=send.at[k], recv_sem=recv.at[k],
                        device_id=to, device_id_type=MESH), src_slot, k))
                    k += 1
        for cp, _, _ in copies:
            cp.start()
        for cp, src_slot, k in copies:
            pltpu.make_async_remote_copy(
                src_ref=p_ref, dst_ref=buf.at[src_slot], send_sem=send.at[k], recv_sem=recv.at[k],
                device_id=(x, y, c), device_id_type=MESH).wait()
        acc = buf[0]
        for d in range(1, 8):
            acc = acc + buf[d]
        o_ref[...] = acc

    dma = pltpu.SemaphoreType.DMA
    vm = BS(memory_space=pltpu.VMEM)
    return pl.pallas_call(
        body, in_specs=[vm], out_specs=vm, out_shape=_sds((rows, D), F32),
        scratch_shapes=[pltpu.VMEM((8, rows, D), F32), dma((7,)), dma((7,))],
        compiler_params=pltpu.CompilerParams(has_side_effects=True, vmem_limit_bytes=32 << 20),
        name="small_allreduce",
    )(pack)


def _in_proj(h, w, bias, name):
    n = w.shape[1]
    tn = 640 if n == NA else 896
    ep = None
    extras, especs = (), ()
    if bias is not None:
        def ep(acc, ex, o_ref):
            o_ref[...] = acc + ex[0][...]
        extras = (bias,)
        especs = (BS((1, tn), lambda i, j, k: (0, j)),)
    return _mm("nn", h, w, grid=(T // TR, n // tn, 1),
               a_spec=BS((TR, D), lambda i, j, k: (i, 0)), b_spec=BS((D, tn), lambda i, j, k: (0, j)),
               out_shape=_sds((T, n), F32), out_spec=BS((TR, tn), lambda i, j, k: (i, j)),
               acc_shape=(TR, tn), extras=extras, extra_specs=especs, epilogue=ep, name=name)


def _add_res(acc, ex, o_ref):
    o_ref[...] = acc + ex[0][...]


def _local_step(x, mem, target, w, p, carries=None, bwd_carry_fn=None):
    row = lambda i, j, k: (i, 0)
    whole = lambda i, j, k: (0, 0)
    w = {k: (list(v) if isinstance(v, list) else v) for k, v in w.items()}
    carries = carries or {}

    def carry_of(name):
        return carries[name][0] if name in carries else None

    def delivered(name, outs):
        if name in carries:
            carries[name][1](w, outs)

    saved = []
    bias = _bias_expand(p["rel_u"])
    for l in range(2):
        type_a = l == 0
        h = _rms_fwd(x, p["norm1_g"][l:l + 1], f"rms1_{l}")
        memn = _rms_fwd(mem, p["mem_norm_g"][l:l + 1], f"rmsmem_{l}")
        if type_a:
            z = _in_proj(h, w["a"], None, "inproj_a")
            cat, carried = _attn_fwd(z, bias, p["a_q_g2"], p["a_k_g2"], carry_of("attn_fwd"))
            delivered("attn_fwd", carried)
            qcol = NA // MEMW - 1
        else:
            z = _in_proj(h, w["b"], p["b_b_in"], "inproj_b")
            cat = _conv_fwd(z, p["conv_w"], p["conv_b"], p["ln_g"], p["ln_b"])
            qcol = NBW // MEMW - 1
        kv = _mm("nn", memn, w["kv"][l], grid=(1, 1, 1),
                 a_spec=BS((NB * MEMT, D), whole), b_spec=BS((D, 2 * MEMW), whole),
                 out_shape=_sds((NB * MEMT, 2 * MEMW), F32), out_spec=BS((NB * MEMT, 2 * MEMW), whole),
                 acc_shape=(8, 128), name=f"memkv_{l}")
        cat = _memattn_fwd(z, kv, cat, p["mq_g4"][l:l + 1], p["mk_g4"][l:l + 1], qcol, f"memattn_fwd_{l}")
        x1 = _mm("nn", cat, w["wo"][l], grid=(T // TR, 1, 1), a_spec=BS((TR, D), row),
                 b_spec=BS((D, D), whole),
                 out_shape=_sds((T, D), F32), out_spec=BS((TR, D), row), acc_shape=(8, 128),
                 extras=(x,), extra_specs=(BS((TR, D), row),), epilogue=_add_res, name=f"outproj_{l}")
        h2 = _rms_fwd(x1, p["norm2_g"][l:l + 1], f"rms2_{l}")
        (g, u, act), carried = _ffn_up(h2, w["gu"][l], l, carry_of(f"ffn_up_{l}"))
        delivered(f"ffn_up_{l}", carried)
        x2 = _mm("nn", act, w["wd"][l], grid=(T // TR, 1, NSH),
                 a_spec=BS((None, TR, FS), lambda i, j, k: (k, i, 0)),
                 b_spec=BS((None, FS, D), lambda i, j, k: (k, 0, 0)),
                 out_shape=_sds((T, D), F32), out_spec=BS((TR, D), row), acc_shape=(TR, D),
                 extras=(x1,), extra_specs=(BS((TR, D), row),), epilogue=_add_res, name=f"ffn_down_{l}",
                 carry=carry_of(f"ffn_down_{l}"))
        if carry_of(f"ffn_down_{l}") is not None:
            x2, carried = x2
            delivered(f"ffn_down_{l}", carried)
        saved.append(dict(x=x, h=h, memn=memn, kv=kv, z=z, cat=cat, x1=x1, h2=h2, g=g, u=u, act=act,
                          qcol=qcol))
        x = x2

    loss, dx = _loss_head(x, target)

    big = dict(a=None, b=None, kv=[None, None], wo=[None, None], gu=[None, None], wd=[None, None])
    small = {}
    bwd_carried = ()
    tk = T // 2
    nkt = T // tk
    for l in (1, 0):
        sv = saved[l]
        dgu = _ffn_down_bwd(dx, w["wd"][l], sv["g"], sv["u"], l)
        dgu8 = dgu.reshape(2 * NSH, T, FS)
        big["wd"][l] = _mm("tn", sv["act"], dx, grid=(NSH, 1, nkt),
                           a_spec=BS((None, tk, FS), lambda i, j, k: (i, k, 0)),
                           b_spec=BS((tk, D), lambda i, j, k: (k, 0)),
                           out_shape=_sds((NSH, FS, D), F32),
                           out_spec=BS((None, FS, D), lambda i, j, k: (i, 0, 0)),
                           acc_shape=(FS, D), name=f"dw_down_{l}")
        dh2 = _mm("nt", dgu8, w["gu"][l], grid=(T // TR, 1, 2 * NSH),
                  a_spec=BS((None, TR, FS), lambda i, j, k: (k, i, 0)),
                  b_spec=BS((None, D, FS), lambda i, j, k: (k, 0, 0)),
                  out_shape=_sds((T, D), F32), out_spec=BS((TR, D), row), acc_shape=(TR, D),
                  name=f"dh2_{l}")
        big["gu"][l] = _mm("tn", sv["h2"], dgu8, grid=(2 * NSH, 1, nkt),
                           a_spec=BS((tk, D), lambda i, j, k: (k, 0)),
                           b_spec=BS((None, tk, FS), lambda i, j, k: (i, k, 0)),
                           out_shape=_sds((2 * NSH, D, FS), F32),
                           out_spec=BS((None, D, FS), lambda i, j, k: (i, 0, 0)),
                           acc_shape=(D, FS), name=f"dw_gu_{l}")
        dx1, small[f"norm2_g{l}"] = _rms_bwd(dh2, sv["x1"], p["norm2_g"][l:l + 1], dx, f"rms2_bwd_{l}")
        dcat = _mm("nt", dx1, w["wo"][l], grid=(T // TR, 1, 1), a_spec=BS((TR, D), row),
                   b_spec=BS((D, D), whole),
                   out_shape=_sds((T, D), F32), out_spec=BS((TR, D), row), acc_shape=(8, 128),
                   name=f"dcat_{l}")
        big["wo"][l] = _mm("tn", sv["cat"], dx1, grid=(1, 1, nkt),
                           a_spec=BS((tk, D), lambda i, j, k: (k, 0)), b_spec=BS((tk, D), lambda i, j, k: (k, 0)),
                           out_shape=_sds((D, D), F32), out_spec=BS((D, D), whole),
                           acc_shape=(D, D), name=f"dw_out_{l}")
        dqm, dkv, small[f"mq_g{l}"], small[f"mk_g{l}"] = _memattn_bwd(
            sv["z"], sv["kv"], dcat, p["mq_g4"][l:l + 1], p["mk_g4"][l:l + 1], sv["qcol"], f"memattn_bwd_{l}")
        if l == 0:
            carry = bwd_carry_fn(big) if bwd_carry_fn is not None else None
            (dq, dk, dv, dbias, small["a_q_g"], small["a_k_g"]), bwd_carried = _attn_bwd(
                sv["z"], dcat, bias, p["a_q_g2"], p["a_k_g2"], carry)
            small["rel_u"] = _bias_reduce(dbias)
            dz = jnp.concatenate([dq, dk, dv, dqm], axis=1)
            w_in, key, n, tn = w["a"], "a", NA, 640
        else:
            du, small["conv_w"], small["conv_b"], small["ln_g"], small["ln_b"], dbin_u = _conv_bwd(
                sv["z"], dcat, p["conv_w"], p["conv_b"], p["ln_g"], p["ln_b"])
            dz = jnp.concatenate([du, dqm], axis=1)
            small["b_in_u"] = dbin_u
            w_in, key, n, tn = w["b"], "b", NBW, 896
        dh = _mm("nt", dz, w_in, grid=(T // TR, 1, 1),
                 a_spec=BS((TR, n), lambda i, j, k: (i, 0)), b_spec=BS((D, n), lambda i, j, k: (0, 0)),
                 out_shape=_sds((T, D), F32), out_spec=BS((TR, D), row), acc_shape=(8, 128),
                 name=f"dh_{l}")
        big[key] = _mm("tn", sv["h"], dz, grid=(1, n // tn, nkt),
                       a_spec=BS((tk, D), lambda i, j, k: (k, 0)), b_spec=BS((tk, tn), lambda i, j, k: (k, j)),
                       out_shape=_sds((D, n), F32), out_spec=BS((D, tn), lambda i, j, k: (0, j)),
                       acc_shape=(D, tn), name=f"dw_in_{l}")
        if l == 1:
            small["b_in_qm"] = _colsum(dqm, "colsum_dqm")
        dx, small[f"norm1_g{l}"] = _rms_bwd(dh, sv["x"], p["norm1_g"][l:l + 1], dx1, f"rms1_bwd_{l}")
        big["kv"][l] = _mm("tn", sv["memn"], dkv, grid=(1, 1, 1),
                           a_spec=BS((NB * MEMT, D), whole), b_spec=BS((NB * MEMT, 2 * MEMW), whole),
                           out_shape=_sds((D, 2 * MEMW), F32), out_spec=BS((D, 2 * MEMW), whole),
                           acc_shape=(8, 128), name=f"dw_kv_{l}")
        dmemn = _mm("nt", dkv, w["kv"][l], grid=(1, 1, 1),
                    a_spec=BS((NB * MEMT, 2 * MEMW), whole), b_spec=BS((D, 2 * MEMW), whole),
                    out_shape=_sds((NB * MEMT, D), F32), out_spec=BS((NB * MEMT, D), whole),
                    acc_shape=(8, 128), name=f"dmemn_{l}")
        _, small[f"mem_norm_g{l}"] = _rms_bwd(dmemn, mem, p["mem_norm_g"][l:l + 1], None, f"rmsmem_bwd_{l}")
    return loss, dx, big, small, bwd_carried


def _colsum(a, name):
    rows, cols = a.shape

    def body(a_ref, o_ref):
        @pl.when(pl.program_id(0) == 0)
        def _():
            o_ref[...] = jnp.zeros_like(o_ref)

        o_ref[...] += jnp.sum(a_ref[...].astype(F32), axis=0, keepdims=True)

    return pl.pallas_call(
        body, grid=(rows // TR,), in_specs=[BS((TR, cols), lambda i: (i, 0))],
        out_specs=BS((1, cols), lambda i: (0, 0)), out_shape=_sds((1, cols), F32),
        compiler_params=_cp(("arbitrary",)), name=name,
    )(a)


_PACK_ROWS = 64


def _pad_to(a, rows, cols=D):
    return jnp.pad(a, ((0, rows - a.shape[0]), (0, cols - a.shape[1])))


def _pack_small(sm):
    parts = [
        jnp.concatenate([sm["norm1_g0"], sm["norm1_g1"]], 0),
        jnp.concatenate([sm["mem_norm_g0"], sm["mem_norm_g1"]], 0),
        jnp.concatenate([sm["norm2_g0"], sm["norm2_g1"]], 0),
        _pad_to(sm["a_q_g"], 1), _pad_to(sm["a_k_g"], 1),
        _pad_to(jnp.concatenate([sm["mq_g0"], sm["mq_g1"]], 0), 2),
        _pad_to(jnp.concatenate([sm["mk_g0"], sm["mk_g1"]], 0), 2),
        _pad_to(sm["conv_b"], 1), _pad_to(sm["ln_g"], 1), _pad_to(sm["ln_b"], 1),
        _pad_to(sm["b_in_u"][:, :D], 1), _pad_to(sm["b_in_u"][:, D:], 1),
        _pad_to(sm["b_in_qm"], 1),
        _pad_to(sm["conv_w"][:CONVW], CONVW),
        sm["rel_u"].reshape(12, D),
    ]
    pack = jnp.concatenate(parts, 0)
    return jnp.pad(pack, ((0, _PACK_ROWS - pack.shape[0]), (0, 0)))


def _rel_table_to_u(rel_bias):
    flat = jnp.concatenate([jnp.broadcast_to(rel_bias[:, 191:192], (12, 447)), rel_bias[:, ::-1]], axis=1)
    return jnp.pad(flat, ((0, 0), (192, 1024 - 192 - 639))).reshape(12, 1, 1024)


def _u_to_rel_table(du):
    flat = du[:, 192:192 + 639]
    g = flat[:, 447:][:, ::-1]
    return g, flat[:, :447]


def kernel(x, mem, norm1_g, mem_norm_g, a_w_in, a_q_g, a_k_g, a_rel_bias, b_w_in, b_b_in, b_conv_w, b_conv_b, b_ln_g, b_ln_b, mq_g, mk_g, w_mem_kv, w_out, norm2_g, w_gate, w_up, w_down, loss_target, m_norm1_g, m_mem_norm_g, m_a_w_in, m_a_q_g, m_a_k_g, m_a_rel_bias, m_b_w_in, m_b_b_in, m_b_conv_w, m_b_conv_b, m_b_ln_g, m_b_ln_b, m_mq_g, m_mk_g, m_w_mem_kv, m_w_out, m_norm2_g, m_w_gate, m_w_up, m_w_down, v_norm1_g, v_mem_norm_g, v_a_w_in, v_a_q_g, v_a_k_g, v_a_rel_bias, v_b_w_in, v_b_b_in, v_b_conv_w, v_b_conv_b, v_b_ln_g, v_b_ln_b, v_mq_g, v_mk_g, v_w_mem_kv, v_w_out, v_norm2_g, v_w_gate, v_w_up, v_w_down):
    sx = 2 * lax.axis_index("x") + lax.axis_index("y")

    n_in = (NA // NSH, NBW // NSH)
    w_in = (a_w_in, b_w_in)
    small_src = jnp.concatenate([
        jnp.pad(b_b_in, ((0, 0), (0, 512 - 448))),
        jnp.pad(b_conv_w[0], ((0, 0), (0, 512 - 192))),
        jnp.pad(jnp.concatenate([b_conv_b, b_ln_g, b_ln_b], 0), ((0, 0), (0, 512 - 192))),
        jnp.zeros((5, 512), F32)], 0)

    def gather_group(l, names, small=None):
        src_of = {
            "in": lambda: [_cast_bf16([w_in[l][0]], f"cast_in_{l}").reshape(1, 2, D // 2, n_in[l])],
            "kv": lambda: [_cast_bf16([w_mem_kv[l]], f"cast_kv_{l}").reshape(1, 2, 128, 2 * MEMW)],
            "wo": lambda: [_cast_bf16([w_out[l]], f"cast_wo_{l}").reshape(1, 2, 128, D)],
            "gu": lambda: [_cast_bf16([w_gate[l]], f"cast_gate_{l}").reshape(1, 2, D // 2, FS),
                           _cast_bf16([w_up[l]], f"cast_up_{l}").reshape(1, 2, D // 2, FS)],
            "wd": lambda: [_cast_bf16([w_down[l]], f"cast_wd_{l}").reshape(1, 2, FS // 2, D)]}
        dst_of = {"in": (1, NSH, 2, D // 2, n_in[l]), "kv": (1, NSH, 2, 128, 2 * MEMW),
                  "wo": (1, NSH, 2, 128, D), "gu": (1, 2, NSH, 2, D // 2, FS), "wd": (1, NSH, 2, FS // 2, D)}
        srcs, views = [], []
        for k, name in enumerate(names):
            srcs += src_of[name]()
            if name == "gu":
                views += [lambda d, k=k: d[k].at[:, 0], lambda d, k=k: d[k].at[:, 1]]
            else:
                views.append(lambda d, k=k: d[k])

        def done(w, outs):
            for k, name in enumerate(names):
                if name == "in":
                    w["b" if l else "a"] = outs[k].reshape(NSH, D, n_in[l]).transpose(1, 0, 2).reshape(
                        D, NSH * n_in[l])
                else:
                    shape = {"kv": (D, 2 * MEMW), "wo": (D, D), "gu": (2 * NSH, D, FS), "wd": (NSH, FS, D)}
                    w[name][l] = outs[k].reshape(shape[name])

        return _gather_exchange(srcs, [dst_of[name] for name in names], views, small), done

    w = dict(a=None, b=None, kv=[None, None], wo=[None, None], gu=[None, None], wd=[None, None])
    first, first_done = gather_group(0, ["in", "kv"], small_src)
    outs0 = _run_exchange(first, "gather_weights_first")
    first_done(w, outs0)
    small_all = outs0[2]
    carries = {"attn_fwd": gather_group(0, ["wo", "gu", "wd"]),
               "ffn_up_0": gather_group(1, ["in", "kv", "wo", "wd"]),
               "ffn_down_0": gather_group(1, ["gu"])}

    conv_w_full = small_all[:, 1:1 + CONVW, :192].transpose(1, 0, 2).reshape(CONVW, TOK)
    vec3 = small_all[:, 32:35, :192].transpose(1, 0, 2).reshape(3, TOK)
    p = dict(
        norm1_g=norm1_g, mem_norm_g=mem_norm_g, norm2_g=norm2_g,
        a_q_g2=jnp.tile(a_q_g, (1, 2)), a_k_g2=jnp.tile(a_k_g, (1, 2)),
        mq_g4=jnp.tile(mq_g, (1, 4)), mk_g4=jnp.tile(mk_g, (1, 4)),
        rel_u=_rel_table_to_u(a_rel_bias[0]),
        b_b_in=small_all[:, 0, :448].reshape(1, NBW),
        conv_w=jnp.pad(conv_w_full, ((0, 1), (0, 0))),
        conv_b=vec3[0:1], ln_g=vec3[1:2], ln_b=vec3[2:3])

    def pair_sums(items, big):
        own, sums_b = [], []
        for l, name in items:
            if name == "in":
                g = (big["b"] if l else big["a"]).reshape(D, NSH, n_in[l]).transpose(1, 0, 2)
                g = g.reshape(1, NSH, 2, D // 2, n_in[l])
            else:
                shape = {"kv": (1, NSH, 2, 128, 2 * MEMW), "wo": (1, NSH, 2, 128, D),
                         "gu": (2, NSH, 2, D // 2, FS), "wd": (1, NSH, 2, FS // 2, D)}
                g = big[name][l].reshape(shape[name])
            of, sb = _pair_reduce(g, f"pair_reduce_{name}_{l}")
            own.append(of)
            sums_b.append(sb)
        return own, sums_b

    early = [(1, "in"), (1, "kv"), (1, "wo"), (1, "gu"), (1, "wd"), (0, "wo"), (0, "gu"), (0, "wd")]
    late = [(0, "in"), (0, "kv")]
    own_early = []

    def bwd_carry_fn(big):
        own, sums_b = pair_sums(early, big)
        own_early.extend(own)
        return _chip_exchange(sums_b)

    loss, grad_x, big, small, parts_early = _local_step(
        x.reshape(T, D), mem.reshape(NB * MEMT, D), loss_target.reshape(T, D), w, p,
        carries=carries, bwd_carry_fn=bwd_carry_fn)
    loss = lax.psum(loss[0, 0], ("x", "y", "c"))

    own_late, sums_b_late = pair_sums(late, big)
    parts_late = _run_exchange(_chip_exchange(sums_b_late), "chip_exchange_late")
    items = early + late
    halves = [_quad_sum(o, pt, f"quad_sum_{name}_{l}")
              for (l, name), o, pt in zip(items, own_early + own_late, list(parts_early) + list(parts_late))]
    out_shapes = [(1, D, NA // NSH), (1, D, NBW // NSH), (2, 2 * 128, 2 * MEMW), (2, 2 * 128, D),
                  (2, D, FS), (2, D, FS), (2, FS, D)]
    target_of = {"in": lambda l: [(0, l, 0)], "kv": lambda l: [(0, 2, l)], "wo": lambda l: [(0, 3, l)],
                 "gu": lambda l: [(0, 4, l), (1, 5, l)], "wd": lambda l: [(0, 6, l)]}
    targets = [target_of[name](l) for l, name in items]
    g_a, g_b, g_kv, g_wo, g_gate, g_up, g_wd = _final_exchange(halves, out_shapes, targets)

    tot = _small_allreduce(_pack_small(small))
    g_rel, clip_part = _u_to_rel_table(tot[49:61])
    g_rel = jnp.concatenate([g_rel[:, :191], g_rel[:, 191:] + _rowsum(clip_part)], axis=1)
    b_in_full = jnp.concatenate([tot[15:16], tot[16:17, :512], tot[17:18, :MEMW]], axis=1)
    g_small = dict(
        norm1_g=tot[0:2], mem_norm_g=tot[2:4], norm2_g=tot[4:6],
        a_q_g=tot[6:7, :HD], a_k_g=tot[7:8, :HD], a_rel_bias=g_rel[None],
        b_b_in=lax.dynamic_slice(b_in_full, (0, sx * 448), (1, 448)),
        b_conv_w=lax.dynamic_slice(tot[18:49, :TOK], (0, sx * 192), (CONVW, 192))[None],
        b_conv_b=lax.dynamic_slice(tot[12:13, :TOK], (0, sx * 192), (1, 192)),
        b_ln_g=lax.dynamic_slice(tot[13:14, :TOK], (0, sx * 192), (1, 192)),
        b_ln_b=lax.dynamic_slice(tot[14:15, :TOK], (0, sx * 192), (1, 192)),
        mq_g=tot[8:10, :HD], mk_g=tot[10:12, :HD])

    names = ["norm1_g", "mem_norm_g", "a_w_in", "a_q_g", "a_k_g", "a_rel_bias", "b_w_in", "b_b_in",
             "b_conv_w", "b_conv_b", "b_ln_g", "b_ln_b", "mq_g", "mk_g", "w_mem_kv", "w_out",
             "norm2_g", "w_gate", "w_up", "w_down"]
    weights = dict(zip(names, [norm1_g, mem_norm_g, a_w_in, a_q_g, a_k_g, a_rel_bias, b_w_in, b_b_in,
                               b_conv_w, b_conv_b, b_ln_g, b_ln_b, mq_g, mk_g, w_mem_kv, w_out,
                               norm2_g, w_gate, w_up, w_down]))
    ms = dict(zip(names, [m_norm1_g, m_mem_norm_g, m_a_w_in, m_a_q_g, m_a_k_g, m_a_rel_bias, m_b_w_in,
                          m_b_b_in, m_b_conv_w, m_b_conv_b, m_b_ln_g, m_b_ln_b, m_mq_g, m_mk_g,
                          m_w_mem_kv, m_w_out, m_norm2_g, m_w_gate, m_w_up, m_w_down]))
    vs = dict(zip(names, [v_norm1_g, v_mem_norm_g, v_a_w_in, v_a_q_g, v_a_k_g, v_a_rel_bias, v_b_w_in,
                          v_b_b_in, v_b_conv_w, v_b_conv_b, v_b_ln_g, v_b_ln_b, v_mq_g, v_mk_g,
                          v_w_mem_kv, v_w_out, v_norm2_g, v_w_gate, v_w_up, v_w_down]))
    grads = dict(g_small)
    grads.update(a_w_in=g_a, b_w_in=g_b, w_mem_kv=g_kv, w_out=g_wo, w_gate=g_gate, w_up=g_up, w_down=g_wd)
    big_names = ["a_w_in", "b_w_in", "w_mem_kv", "w_out", "w_gate", "w_up", "w_down"]
    small_names = [n for n in names if n not in big_names]
    delta, new_m, new_v = {}, {}, {}
    for n in big_names:
        delta[n], new_m[n], new_v[n] = _adamw_big(weights[n], grads[n], ms[n], vs[n], f"adamw_{n}")
    as2d = lambda a: a.reshape(-1, a.shape[-1])
    d_s, m_s, v_s = _adamw_small([as2d(weights[n]) for n in small_names], [as2d(grads[n]) for n in small_names],
                                 [as2d(ms[n]) for n in small_names], [as2d(vs[n]) for n in small_names])
    for i, n in enumerate(small_names):
        delta[n] = d_s[i].reshape(weights[n].shape)
        new_m[n] = m_s[i].reshape(weights[n].shape)
        new_v[n] = v_s[i].reshape(weights[n].shape)

    return (loss, grad_x.reshape(NB, SEQ, D), *[grads[n] for n in names], *[delta[n] for n in names],
            *[new_m[n] for n in names], *[new_v[n] for n in names])


def _rowsum(a):
    def body(a_ref, o_ref):
        o_ref[...] = jnp.sum(a_ref[...], axis=1, keepdims=True)

    vm = BS(memory_space=pltpu.VMEM)
    return pl.pallas_call(body, in_specs=[vm], out_specs=vm, out_shape=_sds((a.shape[0], 1), F32),
                          compiler_params=_cp(), name="rowsum")(a)
```

```python
import functools

import jax
import jax.numpy as jnp
from jax import lax
from jax.experimental import pallas as pl
from jax.experimental.pallas import tpu as pltpu

F32 = jnp.float32
BF16 = jnp.bfloat16
BS = pl.BlockSpec
ANY = pl.BlockSpec(memory_space=pl.ANY)
MESH = pl.DeviceIdType.MESH

D = 1024
SEQ = 2048
NB = 2
T = NB * SEQ
MEMT = 256
HD = 64
TOK = 768
MEMW = 256
NA = 3 * TOK + MEMW
NBW = 2 * TOK + MEMW
FF = 2816
NSH = 4
FS = FF // NSH
FT = FF // 2
CONVW = 31
EPS = 1e-6
NEG = -1e30
SCALE = HD ** -0.5
QB = 256
KWIN = 768
KPAD = 512
TR = 512

ADAM_LR = 0.001
ADAM_B1 = 0.9
ADAM_B2 = 0.999
ADAM_EPS = 1e-08
ADAM_WD = 0.01
ADAM_STEP = 10

_DIMS = {
    "nn": (((1,), (0,)), ((), ())),
    "nt": (((1,), (1,)), ((), ())),
    "tn": (((0,), (0,)), ((), ())),
}


def _cp(sem=None, vmem_mb=48):
    return pltpu.CompilerParams(dimension_semantics=sem, vmem_limit_bytes=vmem_mb << 20)


def _sds(shape, dtype):
    return jax.ShapeDtypeStruct(tuple(shape), dtype)


def _mm(mode, a, b, *, grid, a_spec, b_spec, out_shape, out_spec, acc_shape, name,
        extras=(), extra_specs=(), epilogue=None, carry=None, vmem_mb=48):
    n_ex = len(extras)
    nk = grid[2]
    dims = _DIMS[mode]
    ni = len(carry.ins) if carry else 0
    no = len(carry.out_shapes) if carry else 0

    def body(a_ref, b_ref, *rest):
        ex = rest[:n_ex]
        cin = rest[n_ex:n_ex + ni]
        o_ref = rest[n_ex + ni]
        cout = rest[n_ex + ni + 1:n_ex + ni + 1 + no]
        acc = rest[n_ex + ni + 1 + no]
        cscr = rest[n_ex + ni + 2 + no:]
        k = pl.program_id(2)
        if carry:
            ids = [pl.program_id(d) for d in range(3)]

            @pl.when((ids[0] == 0) & (ids[1] == 0) & (ids[2] == 0))
            def _():
                carry.start(cin, cout, cscr)

        prod = lax.dot_general(a_ref[...].astype(BF16), b_ref[...].astype(BF16), dims,
                               preferred_element_type=F32)

        def finish(val):
            if epilogue is None:
                o_ref[...] = val.astype(o_ref.dtype)
            else:
                epilogue(val, ex, o_ref)

        if nk == 1:
            finish(prod)
        else:
            @pl.when(k == 0)
            def _():
                acc[...] = prod

            @pl.when((k > 0) & (k < nk - 1))
            def _():
                acc[...] += prod

            @pl.when(k == nk - 1)
            def _():
                finish(acc[...] + prod)

        if carry:
            @pl.when((ids[0] == grid[0] - 1) & (ids[1] == grid[1] - 1) & (ids[2] == grid[2] - 1))
            def _():
                carry.finish(cin, cout, cscr)

    acc_scratch = pltpu.VMEM(acc_shape if nk > 1 else (8, 128), F32)
    if not carry:
        return pl.pallas_call(
            body, grid=grid, in_specs=[a_spec, b_spec, *extra_specs], out_specs=out_spec,
            out_shape=out_shape, scratch_shapes=[acc_scratch],
            compiler_params=_cp(("parallel", "parallel", "arbitrary"), vmem_mb), name=name,
        )(a, b, *extras)
    outs = pl.pallas_call(
        body, grid=grid, in_specs=[a_spec, b_spec, *extra_specs, *carry.in_specs],
        out_specs=[out_spec] + [ANY] * no, out_shape=[out_shape] + carry.out_shapes,
        scratch_shapes=[acc_scratch] + carry.scratch,
        compiler_params=pltpu.CompilerParams(
            dimension_semantics=("arbitrary", "arbitrary", "arbitrary"),
            vmem_limit_bytes=vmem_mb << 20, has_side_effects=True), name=name,
    )(a, b, *extras, *carry.ins)
    return outs[0], outs[1:]


def _rms_fwd(x, g, name):
    rows = x.shape[0]

    def body(x_ref, g_ref, o_ref):
        xv = x_ref[...]
        r = lax.rsqrt(jnp.mean(xv * xv, axis=-1, keepdims=True) + EPS)
        o_ref[...] = (xv * r * g_ref[...]).astype(BF16)

    return pl.pallas_call(
        body, grid=(rows // TR,),
        in_specs=[BS((TR, D), lambda i: (i, 0)), BS((1, D), lambda i: (0, 0))],
        out_specs=BS((TR, D), lambda i: (i, 0)), out_shape=_sds((rows, D), BF16),
        compiler_params=_cp(("arbitrary",)), name=name,
    )(x, g)


def _rms_bwd(dh, x, g, dres, name):
    rows = x.shape[0]
    has_res = dres is not None

    def body(*refs):
        if has_res:
            dh_ref, x_ref, g_ref, r_ref, dx_ref, dg_ref = refs
        else:
            dh_ref, x_ref, g_ref, dx_ref, dg_ref = refs
        xv = x_ref[...]
        dhv = dh_ref[...]
        r = lax.rsqrt(jnp.mean(xv * xv, axis=-1, keepdims=True) + EPS)
        xh = xv * r
        gy = dhv * g_ref[...]
        dx = r * (gy - xh * jnp.mean(gy * xh, axis=-1, keepdims=True))
        if has_res:
            dx = dx + r_ref[...]
        dx_ref[...] = dx

        @pl.when(pl.program_id(0) == 0)
        def _():
            dg_ref[...] = jnp.zeros_like(dg_ref)

        dg_ref[...] += jnp.sum(dhv * xh, axis=0, keepdims=True)

    row = BS((TR, D), lambda i: (i, 0))
    vec = BS((1, D), lambda i: (0, 0))
    ins = [dh, x, g] + ([dres] if has_res else [])
    return pl.pallas_call(
        body, grid=(rows // TR,),
        in_specs=[row, row, vec] + ([row] if has_res else []),
        out_specs=[row, vec], out_shape=[_sds((rows, D), F32), _sds((1, D), F32)],
        compiler_params=_cp(("arbitrary",)), name=name,
    )(*ins)


def _group_masks(width):
    lane = lax.broadcasted_iota(jnp.int32, (1, width), 1)
    return [(lane >= HD * h) & (lane < HD * (h + 1)) for h in range(width // HD)]


def _group_mean(v, masks):
    out = jnp.zeros_like(v)
    for m in masks:
        s = jnp.sum(jnp.where(m, v, 0.0), axis=-1, keepdims=True) * (1.0 / HD)
        out = jnp.where(m, s, out)
    return out


def _head_norm(zv, g, masks):
    r = lax.rsqrt(_group_mean(zv * zv, masks) + EPS)
    return zv * r * g


def _head_norm_bwd(dy, zv, g, masks):
    r = lax.rsqrt(_group_mean(zv * zv, masks) + EPS)
    zh = zv * r
    gy = dy * g
    dz = r * (gy - zh * _group_mean(gy * zh, masks))
    return dz, jnp.sum(dy * zh, axis=0, keepdims=True)


def _fold_heads(v, width):
    vb = jnp.broadcast_to(v, (8, width))
    out = vb
    for h in range(1, width // HD):
        out = out + pltpu.roll(vb, width - HD * h, axis=1)
    return out[0:1]


def _bias_expand(u):
    def body(u_ref, o_ref):
        x = jnp.broadcast_to(u_ref[...], (QB, 1024))
        rolled = pltpu.roll(x, 1024 - (QB - 1), axis=1, stride=1, stride_axis=0)[:, :KWIN]
        row = lax.broadcasted_iota(jnp.int32, (QB, 1), 0)
        col = lax.broadcasted_iota(jnp.int32, (1, KWIN), 1)
        lo = (row // 64) * 64
        ok = (col >= lo) & (col < lo + 576)
        o_ref[...] = jnp.where(ok, rolled, NEG)

    return pl.pallas_call(
        body, grid=(12,), in_specs=[BS((None, 1, 1024), lambda h: (h, 0, 0))],
        out_specs=BS((None, QB, KWIN), lambda h: (h, 0, 0)), out_shape=_sds((12, QB, KWIN), F32),
        compiler_params=_cp(("arbitrary",)), name="bias_expand",
    )(u)


def _bias_reduce(ds):
    def body(d_ref, o_ref):
        ri = lax.broadcasted_iota(jnp.int32, (QB, QB), 0)
        ci = lax.broadcasted_iota(jnp.int32, (QB, QB), 1)
        flip = (ri + ci == QB - 1).astype(F32)
        drev = jnp.dot(flip, d_ref[...], precision=lax.Precision.HIGHEST, preferred_element_type=F32)
        x = jnp.concatenate([drev, jnp.zeros((QB, 1024 - KWIN), F32)], axis=1)
        rolled = pltpu.roll(x, 0, axis=1, stride=1, stride_axis=0)
        o_ref[...] = jnp.sum(rolled, axis=0, keepdims=True)

    return pl.pallas_call(
        body, grid=(12,), in_specs=[BS((None, QB, KWIN), lambda h: (h, 0, 0))],
        out_specs=BS((None, 1, 1024), lambda h: (h, 0, 0)), out_shape=_sds((12, 1, 1024), F32),
        compiler_params=_cp(("arbitrary",)), name="bias_reduce",
    )(ds)


def _attn_softmax(qh, kw, bias, startadd):
    s = lax.dot_general(qh, kw, _DIMS["nt"], preferred_element_type=F32) + bias + startadd
    m = jnp.max(s, axis=-1, keepdims=True)
    p = jnp.exp(s - m)
    return p * (1.0 / jnp.sum(p, axis=-1, keepdims=True))


def _attn_fwd(z, bias, gq2, gk2, carry=None):
    ni = len(carry.ins) if carry else 0
    no = len(carry.out_shapes) if carry else 0

    def body(q_ref, k_ref, v_ref, b_ref, gq_ref, gk_ref, *rest):
        cin, o_ref, cout = rest[:ni], rest[ni], rest[ni + 1:ni + 1 + no]
        qn_s, kn_s, v_s = rest[ni + 1 + no:ni + 4 + no]
        cscr = rest[ni + 4 + no:]
        if carry:
            @pl.when((pl.program_id(0) == 0) & (pl.program_id(1) == 0))
            def _():
                carry.start(cin, cout, cscr)

        masks = _group_masks(128)
        qn_s[...] = (_head_norm(q_ref[...], gq_ref[...], masks) * SCALE).astype(BF16)
        kn_s[0:KPAD, :] = jnp.zeros((KPAD, 128), BF16)
        kn_s[KPAD:, :] = _head_norm(k_ref[...], gk_ref[...], masks).astype(BF16)
        v_s[0:KPAD, :] = jnp.zeros((KPAD, 128), BF16)
        v_s[KPAD:, :] = v_ref[...].astype(BF16)
        col = lax.broadcasted_iota(jnp.int32, (1, KWIN), 1)

        def blk(i, carry):
            r0 = pl.multiple_of(i * QB, QB)
            qb = qn_s[pl.ds(r0, QB), :]
            kw = kn_s[pl.ds(r0, KWIN), :]
            vw = v_s[pl.ds(r0, KWIN), :]
            startadd = jnp.where(col + r0 < KPAD, NEG, 0.0)
            o = jnp.zeros((QB, 128), F32)
            for h in range(2):
                qh = jnp.where(masks[h], qb, jnp.zeros_like(qb))
                vh = jnp.where(masks[h], vw, jnp.zeros_like(vw))
                p = _attn_softmax(qh, kw, b_ref[h], startadd).astype(BF16)
                o = o + jnp.dot(p, vh, preferred_element_type=F32)
            o_ref[pl.ds(r0, QB), :] = o.astype(BF16)
            return carry

        lax.fori_loop(0, SEQ // QB, blk, 0)

        if carry:
            @pl.when((pl.program_id(0) == NB - 1) & (pl.program_id(1) == 5))
            def _():
                carry.finish(cin, cout, cscr)

    vec = BS((1, 128), lambda b, hp: (0, 0))
    outs = pl.pallas_call(
        body, grid=(NB, 6),
        in_specs=[BS((SEQ, 128), lambda b, hp: (b, hp)),
                  BS((SEQ, 128), lambda b, hp: (b, 6 + hp)),
                  BS((SEQ, 128), lambda b, hp: (b, 12 + hp)),
                  BS((2, QB, KWIN), lambda b, hp: (hp, 0, 0)), vec, vec]
        + (carry.in_specs if carry else []),
        out_specs=[BS((SEQ, 128), lambda b, hp: (b, hp))] + [ANY] * no,
        out_shape=[_sds((T, D), BF16)] + (carry.out_shapes if carry else []),
        scratch_shapes=[pltpu.VMEM((SEQ, 128), BF16), pltpu.VMEM((SEQ + KPAD, 128), BF16),
                        pltpu.VMEM((SEQ + KPAD, 128), BF16)] + (carry.scratch if carry else []),
        compiler_params=pltpu.CompilerParams(
            dimension_semantics=("arbitrary", "arbitrary"), vmem_limit_bytes=48 << 20,
            has_side_effects=bool(carry)), name="attn_fwd",
    )(z, z, z, bias, gq2, gk2, *(carry.ins if carry else []))
    return outs[0], outs[1:]


def _attn_bwd(z, dcat, bias, gq2, gk2, carry=None):
    ni = len(carry.ins) if carry else 0
    no = len(carry.out_shapes) if carry else 0

    def body(q_ref, k_ref, v_ref, do_ref, b_ref, gq_ref, gk_ref, *rest):
        cin = rest[:ni]
        dq_ref, dk_ref, dv_ref, db_ref, dgq_ref, dgk_ref = rest[ni:ni + 6]
        cout = rest[ni + 6:ni + 6 + no]
        qn_s, kn_s, v_s, dqn_s, dkn_s, dv_s = rest[ni + 6 + no:ni + 12 + no]
        cscr = rest[ni + 12 + no:]
        hp = pl.program_id(0)
        b = pl.program_id(1)
        if carry:
            @pl.when((hp == 0) & (b == 0))
            def _():
                carry.start(cin, cout, cscr)

        masks = _group_masks(128)
        qn_s[...] = (_head_norm(q_ref[...], gq_ref[...], masks) * SCALE).astype(BF16)
        kn_s[0:KPAD, :] = jnp.zeros((KPAD, 128), BF16)
        kn_s[KPAD:, :] = _head_norm(k_ref[...], gk_ref[...], masks).astype(BF16)
        v_s[0:KPAD, :] = jnp.zeros((KPAD, 128), BF16)
        v_s[KPAD:, :] = v_ref[...].astype(BF16)
        dkn_s[...] = jnp.zeros_like(dkn_s)
        dv_s[...] = jnp.zeros_like(dv_s)

        @pl.when(b == 0)
        def _():
            db_ref[...] = jnp.zeros_like(db_ref)

        @pl.when((b == 0) & (hp == 0))
        def _():
            dgq_ref[...] = jnp.zeros_like(dgq_ref)
            dgk_ref[...] = jnp.zeros_like(dgk_ref)

        col = lax.broadcasted_iota(jnp.int32, (1, KWIN), 1)

        def blk(i, carry):
            r0 = pl.multiple_of(i * QB, QB)
            qb = qn_s[pl.ds(r0, QB), :]
            kw = kn_s[pl.ds(r0, KWIN), :]
            vw = v_s[pl.ds(r0, KWIN), :]
            dob = do_ref[pl.ds(r0, QB), :].astype(BF16)
            startadd = jnp.where(col + r0 < KPAD, NEG, 0.0)
            dqn = jnp.zeros((QB, 128), F32)
            dkw = jnp.zeros((KWIN, 128), F32)
            dvw = jnp.zeros((KWIN, 128), F32)
            for h in range(2):
                qh = jnp.where(masks[h], qb, jnp.zeros_like(qb))
                kh = jnp.where(masks[h], kw, jnp.zeros_like(kw))
                doh = jnp.where(masks[h], dob, jnp.zeros_like(dob))
                p = _attn_softmax(qh, kw, b_ref[h], startadd)
                dvw = dvw + lax.dot_general(p.astype(BF16), doh, _DIMS["tn"],
                                            preferred_element_type=F32)
                dp = lax.dot_general(doh, vw, _DIMS["nt"], preferred_element_type=F32)
                ds = p * (dp - jnp.sum(dp * p, axis=-1, keepdims=True))
                db_ref[h] += ds
                dsb = ds.astype(BF16)
                dqn = dqn + jnp.dot(dsb, kh, preferred_element_type=F32)
                dkw = dkw + lax.dot_general(dsb, qh, _DIMS["tn"], preferred_element_type=F32)
            dqn_s[pl.ds(r0, QB), :] = dqn * SCALE
            dkn_s[pl.ds(r0, KWIN), :] += dkw
            dv_s[pl.ds(r0, KWIN), :] += dvw
            return carry

        lax.fori_loop(0, SEQ // QB, blk, 0)

        dq, dgq = _head_norm_bwd(dqn_s[...], q_ref[...], gq_ref[...], masks)
        dk, dgk = _head_norm_bwd(dkn_s[KPAD:, :], k_ref[...], gk_ref[...], masks)
        dq_ref[...] = dq.astype(BF16)
        dk_ref[...] = dk.astype(BF16)
        dv_ref[...] = dv_s[KPAD:, :].astype(BF16)
        dgq_ref[...] += _fold_heads(dgq, 128)
        dgk_ref[...] += _fold_heads(dgk, 128)

        if carry:
            @pl.when((hp == 5) & (b == NB - 1))
            def _():
                carry.finish(cin, cout, cscr)

    vec = BS((1, 128), lambda hp, b: (0, 0))
    row = BS((SEQ, 128), lambda hp, b: (b, hp))
    outs = pl.pallas_call(
        body, grid=(6, NB),
        in_specs=[row,
                  BS((SEQ, 128), lambda hp, b: (b, 6 + hp)),
                  BS((SEQ, 128), lambda hp, b: (b, 12 + hp)),
                  row,
                  BS((2, QB, KWIN), lambda hp, b: (hp, 0, 0)), vec, vec]
        + (carry.in_specs if carry else []),
        out_specs=[row, row, row, BS((2, QB, KWIN), lambda hp, b: (hp, 0, 0)), vec, vec] + [ANY] * no,
        out_shape=[_sds((T, TOK), BF16), _sds((T, TOK), BF16), _sds((T, TOK), BF16),
                   _sds((12, QB, KWIN), F32), _sds((1, 128), F32), _sds((1, 128), F32)]
        + (carry.out_shapes if carry else []),
        scratch_shapes=[pltpu.VMEM((SEQ, 128), BF16), pltpu.VMEM((SEQ + KPAD, 128), BF16),
                        pltpu.VMEM((SEQ + KPAD, 128), BF16), pltpu.VMEM((SEQ, 128), F32),
                        pltpu.VMEM((SEQ + KPAD, 128), F32), pltpu.VMEM((SEQ + KPAD, 128), F32)]
        + (carry.scratch if carry else []),
        compiler_params=pltpu.CompilerParams(
            dimension_semantics=("arbitrary", "arbitrary"), vmem_limit_bytes=48 << 20,
            has_side_effects=bool(carry)), name="attn_bwd",
    )(z, z, z, dcat, bias, gq2, gk2, *(carry.ins if carry else []))
    return outs[:6], outs[6:]


def _mem_softmax(qh, kn):
    s = lax.dot_general(qh, kn, _DIMS["nt"], preferred_element_type=F32)
    m = jnp.max(s, axis=-1, keepdims=True)
    p = jnp.exp(s - m)
    return p * (1.0 / jnp.sum(p, axis=-1, keepdims=True))


def _memattn_fwd(z, kv, cat, gq4, gk4, qcol, name):
    def body(q_ref, k_ref, v_ref, gq_ref, gk_ref, cat_ref, o_ref):
        del cat_ref
        masks = _group_masks(MEMW)
        qn = (_head_norm(q_ref[...], gq_ref[...], masks) * SCALE).astype(BF16)
        kn = _head_norm(k_ref[...], gk_ref[...], masks).astype(BF16)
        vv = v_ref[...].astype(BF16)
        o = jnp.zeros((TR, MEMW), F32)
        for h in range(4):
            qh = jnp.where(masks[h], qn, jnp.zeros_like(qn))
            vh = jnp.where(masks[h], vv, jnp.zeros_like(vv))
            p = _mem_softmax(qh, kn).astype(BF16)
            o = o + jnp.dot(p, vh, preferred_element_type=F32)
        o_ref[...] = o.astype(BF16)

    nt = SEQ // TR
    vec = BS((1, MEMW), lambda b, t: (0, 0))
    return pl.pallas_call(
        body, grid=(NB, nt),
        in_specs=[BS((TR, MEMW), lambda b, t: (b * nt + t, qcol)),
                  BS((MEMT, MEMW), lambda b, t: (b, 0)),
                  BS((MEMT, MEMW), lambda b, t: (b, 1)), vec, vec, ANY],
        out_specs=BS((TR, MEMW), lambda b, t: (b * nt + t, 3)),
        out_shape=_sds((T, D), BF16), input_output_aliases={5: 0},
        compiler_params=_cp(("arbitrary", "arbitrary")), name=name,
    )(z, kv, kv, gq4, gk4, cat)


def _memattn_bwd(z, kv, dcat, gq4, gk4, qcol, name):
    nt = SEQ // TR

    def body(q_ref, k_ref, v_ref, do_ref, gq_ref, gk_ref,
             dq_ref, dkv_ref, dgq_ref, dgk_ref, dkn_s, dv_s):
        b = pl.program_id(0)
        t = pl.program_id(1)
        masks = _group_masks(MEMW)
        qz = q_ref[...]
        kz = k_ref[...]
        qn = (_head_norm(qz, gq_ref[...], masks) * SCALE).astype(BF16)
        kn = _head_norm(kz, gk_ref[...], masks).astype(BF16)
        vv = v_ref[...].astype(BF16)
        dob = do_ref[...].astype(BF16)

        @pl.when(t == 0)
        def _():
            dkn_s[...] = jnp.zeros_like(dkn_s)
            dv_s[...] = jnp.zeros_like(dv_s)

        @pl.when((t == 0) & (b == 0))
        def _():
            dgq_ref[...] = jnp.zeros_like(dgq_ref)
            dgk_ref[...] = jnp.zeros_like(dgk_ref)

        dqn = jnp.zeros((TR, MEMW), F32)
        dkn = jnp.zeros((MEMT, MEMW), F32)
        dvv = jnp.zeros((MEMT, MEMW), F32)
        for h in range(4):
            qh = jnp.where(masks[h], qn, jnp.zeros_like(qn))
            kh = jnp.where(masks[h], kn, jnp.zeros_like(kn))
            doh = jnp.where(masks[h], dob, jnp.zeros_like(dob))
            p = _mem_softmax(qh, kn)
            dvv = dvv + lax.dot_general(p.astype(BF16), doh, _DIMS["tn"], preferred_element_type=F32)
            dp = lax.dot_general(doh, vv, _DIMS["nt"], preferred_element_type=F32)
            ds = p * (dp - jnp.sum(dp * p, axis=-1, keepdims=True))
            dsb = ds.astype(BF16)
            dqn = dqn + jnp.dot(dsb, kh, preferred_element_type=F32)
            dkn = dkn + lax.dot_general(dsb, qh, _DIMS["tn"], preferred_element_type=F32)
        dkn_s[...] += dkn
        dv_s[...] += dvv
        dq, dgq = _head_norm_bwd(dqn * SCALE, qz, gq_ref[...], masks)
        dq_ref[...] = dq.astype(BF16)
        dgq_ref[...] += _fold_heads(dgq, MEMW)

        @pl.when(t == nt - 1)
        def _():
            dk, dgk = _head_norm_bwd(dkn_s[...], kz, gk_ref[...], masks)
            dkv_ref[:, 0:MEMW] = dk
            dkv_ref[:, MEMW:] = dv_s[...]
            dgk_ref[...] += _fold_heads(dgk, MEMW)

    vec = BS((1, MEMW), lambda b, t: (0, 0))
    return pl.pallas_call(
        body, grid=(NB, nt),
        in_specs=[BS((TR, MEMW), lambda b, t: (b * nt + t, qcol)),
                  BS((MEMT, MEMW), lambda b, t: (b, 0)),
                  BS((MEMT, MEMW), lambda b, t: (b, 1)),
                  BS((TR, MEMW), lambda b, t: (b * nt + t, 3)), vec, vec],
        out_specs=[BS((TR, MEMW), lambda b, t: (b * nt + t, 0)),
                   BS((MEMT, 2 * MEMW), lambda b, t: (b, 0)), vec, vec],
        out_shape=[_sds((T, MEMW), BF16), _sds((NB * MEMT, 2 * MEMW), F32),
                   _sds((1, MEMW), F32), _sds((1, MEMW), F32)],
        scratch_shapes=[pltpu.VMEM((MEMT, MEMW), F32), pltpu.VMEM((MEMT, MEMW), F32)],
        compiler_params=_cp(("arbitrary", "arbitrary")), name=name,
    )(z, kv, kv, dcat, gq4, gk4)


HALO = 32
NEXT = 64
RT = 64


def _glu(zz):
    return zz[:, :TOK] * jax.nn.sigmoid(zz[:, TOK:])


def _layer_norm_parts(y):
    mu = jnp.mean(y, axis=-1, keepdims=True)
    yc = y - mu
    rstd = lax.rsqrt(jnp.mean(yc * yc, axis=-1, keepdims=True) + EPS)
    return yc * rstd, rstd


def _shifted_copies(src, dst, rows):
    for b in range(1, 8):
        dst[b - 1, 0:rows, :] = src[b:b + rows, :]


def _tap(src, shifted, off, r0, rows):
    b = off % 8
    if b == 0:
        return src[r0 + off:r0 + off + rows, :]
    return shifted[b - 1, r0 + off - b:r0 + off - b + rows, :]


def _conv_rows(w_ref, hbuf, hs, r0, rows):
    y = jnp.zeros((rows, TOK), F32)
    for j in range(CONVW):
        y = y + w_ref[j:j + 1, :] * _tap(hbuf, hs, (HALO - CONVW + 1) + j, r0, rows)
    return y


def _conv_fwd(z, cw, cb, lg, lb):
    nt = SEQ // TR

    def body(zc_ref, zp_ref, w_ref, cb_ref, lg_ref, lb_ref, o_ref, hbuf, hs):
        t = pl.program_id(1)
        hbuf[0:HALO, :] = jnp.where(t == 0, 0.0, _glu(zp_ref[...]))
        hbuf[HALO:, :] = _glu(zc_ref[...])
        _shifted_copies(hbuf, hs, HALO + TR - 8)
        for r0 in range(0, TR, RT):
            y = _conv_rows(w_ref, hbuf, hs, r0, RT) + cb_ref[...]
            yh, _ = _layer_norm_parts(y)
            o = yh * lg_ref[...] + lb_ref[...]
            o_ref[r0:r0 + RT, :] = (o * jax.nn.sigmoid(o)).astype(BF16)

    vec = BS((1, TOK), lambda b, t: (0, 0))
    per = TR // HALO
    return pl.pallas_call(
        body, grid=(NB, nt),
        in_specs=[BS((TR, 2 * TOK), lambda b, t: (b * nt + t, 0)),
                  BS((HALO, 2 * TOK), lambda b, t: (jnp.maximum((b * nt + t) * per - 1, 0), 0)),
                  BS((32, TOK), lambda b, t: (0, 0)), vec, vec, vec],
        out_specs=BS((TR, TOK), lambda b, t: (b * nt + t, 0)),
        out_shape=_sds((T, D), BF16),
        scratch_shapes=[pltpu.VMEM((HALO + TR, TOK), F32), pltpu.VMEM((7, HALO + TR, TOK), F32)],
        compiler_params=_cp(("arbitrary", "arbitrary")), name="conv_fwd",
    )(z, z, cw, cb, lg, lb)


def _conv_bwd(z, dcat, cw, cb, lg, lb):
    nt = SEQ // TR
    ext = TR + NEXT

    def body(zc_ref, zp_ref, zn_ref, dc_ref, dn_ref, w_ref, cb_ref, lg_ref, lb_ref,
             du_ref, dw_ref, dcb_ref, dlg_ref, dlb_ref, dbin_ref, hbuf, dybuf, hs, dys):
        b = pl.program_id(0)
        t = pl.program_id(1)

        @pl.when((b == 0) & (t == 0))
        def _():
            dw_ref[...] = jnp.zeros_like(dw_ref)
            dcb_ref[...] = jnp.zeros_like(dcb_ref)
            dlg_ref[...] = jnp.zeros_like(dlg_ref)
            dlb_ref[...] = jnp.zeros_like(dlb_ref)
            dbin_ref[...] = jnp.zeros_like(dbin_ref)

        hbuf[0:HALO, :] = jnp.where(t == 0, 0.0, _glu(zp_ref[...]))
        hbuf[HALO:HALO + TR, :] = _glu(zc_ref[...])
        hbuf[HALO + TR:, :] = _glu(zn_ref[...])
        _shifted_copies(hbuf, hs, HALO + TR + NEXT - 8)
        last = t == nt - 1
        for r0 in range(0, ext, RT):
            y = _conv_rows(w_ref, hbuf, hs, r0, RT) + cb_ref[...]
            yh, rstd = _layer_norm_parts(y)
            o = yh * lg_ref[...] + lb_ref[...]
            sg = jax.nn.sigmoid(o)
            if r0 < TR:
                dtok = dc_ref[r0:r0 + RT, :]
            else:
                dtok = jnp.where(last, 0.0, dn_ref[...])
            do = dtok * (sg * (1.0 + o * (1.0 - sg)))
            dyh = do * lg_ref[...]
            dy = rstd * (dyh - jnp.mean(dyh, axis=-1, keepdims=True)
                         - yh * jnp.mean(dyh * yh, axis=-1, keepdims=True))
            dybuf[r0:r0 + RT, :] = dy
            if r0 < TR:
                dlg_ref[...] += jnp.sum(do * yh, axis=0, keepdims=True)
                dlb_ref[...] += jnp.sum(do, axis=0, keepdims=True)
                dcb_ref[...] += jnp.sum(dy, axis=0, keepdims=True)
        _shifted_copies(dybuf, dys, ext - 8)
        for r0 in range(0, TR, RT):
            dh = jnp.zeros((RT, TOK), F32)
            for j in range(CONVW):
                dh = dh + w_ref[j:j + 1, :] * _tap(dybuf, dys, (CONVW - 1) - j, r0, RT)
            a = zc_ref[r0:r0 + RT, 0:TOK]
            sg = jax.nn.sigmoid(zc_ref[r0:r0 + RT, TOK:])
            da = dh * sg
            dg = dh * a * (sg * (1.0 - sg))
            du_ref[r0:r0 + RT, 0:TOK] = da.astype(BF16)
            du_ref[r0:r0 + RT, TOK:] = dg.astype(BF16)
            dbin_ref[:, 0:TOK] += jnp.sum(da, axis=0, keepdims=True)
            dbin_ref[:, TOK:] += jnp.sum(dg, axis=0, keepdims=True)
        for j in range(CONVW):
            acc = jnp.zeros((8, TOK), F32)
            for r0 in range(0, TR, RT):
                prod = dybuf[r0:r0 + RT, :] * _tap(hbuf, hs, (HALO - CONVW + 1) + j, r0, RT)
                acc = acc + jnp.sum(prod.reshape(RT // 8, 8, TOK), axis=0)
            dw_ref[j:j + 1, :] += jnp.sum(acc, axis=0, keepdims=True)

    vec = BS((1, TOK), lambda b, t: (0, 0))
    perh = TR // HALO
    pern = TR // NEXT
    nlast_n = T // NEXT - 1
    return pl.pallas_call(
        body, grid=(NB, nt),
        in_specs=[BS((TR, 2 * TOK), lambda b, t: (b * nt + t, 0)),
                  BS((HALO, 2 * TOK), lambda b, t: (jnp.maximum((b * nt + t) * perh - 1, 0), 0)),
                  BS((NEXT, 2 * TOK), lambda b, t: (jnp.minimum((b * nt + t + 1) * pern, nlast_n), 0)),
                  BS((TR, TOK), lambda b, t: (b * nt + t, 0)),
                  BS((NEXT, TOK), lambda b, t: (jnp.minimum((b * nt + t + 1) * pern, nlast_n), 0)),
                  BS((32, TOK), lambda b, t: (0, 0)), vec, vec, vec],
        out_specs=[BS((TR, 2 * TOK), lambda b, t: (b * nt + t, 0)),
                   BS((32, TOK), lambda b, t: (0, 0)), vec, vec, vec,
                   BS((1, 2 * TOK), lambda b, t: (0, 0))],
        out_shape=[_sds((T, 2 * TOK), BF16), _sds((32, TOK), F32), _sds((1, TOK), F32),
                   _sds((1, TOK), F32), _sds((1, TOK), F32), _sds((1, 2 * TOK), F32)],
        scratch_shapes=[pltpu.VMEM((HALO + TR + NEXT, TOK), F32), pltpu.VMEM((ext, TOK), F32),
                        pltpu.VMEM((7, HALO + TR + NEXT, TOK), F32), pltpu.VMEM((7, ext, TOK), F32)],
        compiler_params=_cp(("arbitrary", "arbitrary"), vmem_mb=56), name="conv_bwd",
    )(z, z, z, dcat, dcat, cw, cb, lg, lb)


def _ffn_up(h2, wgu, l, carry=None):
    ni = len(carry.ins) if carry else 0
    no = len(carry.out_shapes) if carry else 0

    def body(h_ref, wg_ref, wu_ref, *rest):
        cin = rest[:ni]
        g_ref, u_ref, a_ref = rest[ni:ni + 3]
        cout, cscr = rest[ni + 3:ni + 3 + no], rest[ni + 3 + no:]
        if carry:
            @pl.when((pl.program_id(0) == 0) & (pl.program_id(1) == 0))
            def _():
                carry.start(cin, cout, cscr)

        hv = h_ref[...]
        g = jnp.dot(hv, wg_ref[...], preferred_element_type=F32)
        u = jnp.dot(hv, wu_ref[...], preferred_element_type=F32)
        g_ref[...] = g
        u_ref[...] = u
        a_ref[...] = (g * jax.nn.sigmoid(g) * u).astype(BF16)

        if carry:
            @pl.when((pl.program_id(0) == FF // FT - 1) & (pl.program_id(1) == T // TR - 1))
            def _():
                carry.finish(cin, cout, cscr)

    out = BS((TR, FT), lambda q, i: (i, q))
    outs = pl.pallas_call(
        body, grid=(FF // FT, T // TR),
        in_specs=[BS((TR, D), lambda q, i: (i, 0)),
                  BS((None, D, FT), lambda q, i: (0, 0, q)),
                  BS((None, D, FT), lambda q, i: (1, 0, q))] + (carry.in_specs if carry else []),
        out_specs=[out, out, out] + [ANY] * no,
        out_shape=[_sds((T, FF), F32), _sds((T, FF), F32), _sds((T, FF), BF16)]
        + (carry.out_shapes if carry else []),
        scratch_shapes=carry.scratch if carry else [],
        compiler_params=pltpu.CompilerParams(
            dimension_semantics=("arbitrary", "arbitrary"), vmem_limit_bytes=48 << 20,
            has_side_effects=bool(carry)), name=f"ffn_up_{l}",
    )(h2, wgu, wgu, *(carry.ins if carry else []))
    return outs[:3], outs[3:]


def _ffn_down_bwd(dx, wd, g, u, l):
    def epilogue(dact, ex, o_ref):
        gv = ex[0][...]
        uv = ex[1][...]
        sg = jax.nn.sigmoid(gv)
        o_ref[0] = (dact * uv * (sg * (1.0 + gv * (1.0 - sg)))).astype(BF16)
        o_ref[1] = (dact * (gv * sg)).astype(BF16)

    ex_spec = BS((TR, FT), lambda i, q, k: (i, q))
    return _mm("nt", dx, wd, grid=(T // TR, FF // FT, 1),
               a_spec=BS((TR, D), lambda i, q, k: (i, 0)),
               b_spec=BS((FT, D), lambda i, q, k: (q, 0)),
               out_shape=_sds((2, T, FF), BF16),
               out_spec=BS((2, TR, FT), lambda i, q, k: (0, i, q)),
               acc_shape=(TR, FT), extras=(g, u), extra_specs=(ex_spec, ex_spec),
               epilogue=epilogue, name=f"ffn_down_bwd_{l}")


def _loss_head(y, target):
    def body(y_ref, t_ref, l_ref, dy_ref, acc):
        i = pl.program_id(0)
        e = y_ref[...] - t_ref[...]
        dy_ref[...] = e * (1.0 / D)

        @pl.when(i == 0)
        def _():
            acc[...] = jnp.zeros_like(acc)

        acc[...] += jnp.sum(jnp.mean(e * e, axis=-1, keepdims=True), axis=0, keepdims=True)

        @pl.when(i == T // TR - 1)
        def _():
            l_ref[...] = 0.5 * acc[...]

    row = BS((TR, D), lambda i: (i, 0))
    return pl.pallas_call(
        body, grid=(T // TR,), in_specs=[row, row],
        out_specs=[BS((1, 1), lambda i: (0, 0)), row],
        out_shape=[_sds((1, 1), F32), _sds((T, D), F32)],
        scratch_shapes=[pltpu.VMEM((1, 1), F32)],
        compiler_params=_cp(("arbitrary",)), name="loss_head",
    )(y, target)


def _row_tile(rows, cols, itemsize=4, limit=2 << 20):
    tr = rows
    while tr * cols * itemsize > limit and tr % 2 == 0 and (tr // 2) % 16 == 0:
        tr //= 2
    return tr


def _cast_bf16(arrs, name):
    n = len(arrs)
    rows, cols = arrs[0].shape
    tr = _row_tile(rows, cols)

    def body(*refs):
        o_ref = refs[n]
        k = pl.program_id(0)
        val = refs[0][...]
        for j in range(1, n):
            val = jnp.where(k == j, refs[j][...], val)
        o_ref[...] = val.astype(BF16)

    return pl.pallas_call(
        body, grid=(n, rows // tr),
        in_specs=[BS((tr, cols), lambda k, i: (i, 0))] * n,
        out_specs=BS((None, tr, cols), lambda k, i: (k, i, 0)),
        out_shape=_sds((n, rows, cols), BF16),
        compiler_params=_cp(("arbitrary", "arbitrary")), name=name,
    )(*arrs)


def _quad_sum(own, got, name):
    n, rows, cols = own.shape
    tr = _row_tile(rows, cols)

    def body(a_ref, q_ref, o_ref):
        o_ref[...] = ((a_ref[...] + q_ref[0].astype(F32)) + q_ref[1].astype(F32)) + q_ref[2].astype(F32)

    spec = BS((None, tr, cols), lambda k, i: (k, i, 0))
    return pl.pallas_call(
        body, grid=(n, rows // tr),
        in_specs=[spec, BS((3, None, tr, cols), lambda k, i: (0, k, i, 0))], out_specs=spec,
        out_shape=_sds((n, rows, cols), F32),
        compiler_params=_cp(("arbitrary", "arbitrary")), name=name,
    )(own, got)


def _adam_math(w, g, m, v):
    m = ADAM_B1 * m + (1.0 - ADAM_B1) * g
    v = ADAM_B2 * v + (1.0 - ADAM_B2) * (g * g)
    m_hat = m / (1.0 - ADAM_B1 ** ADAM_STEP)
    v_hat = v / (1.0 - ADAM_B2 ** ADAM_STEP)
    delta = -ADAM_LR * (m_hat / (jnp.sqrt(v_hat) + ADAM_EPS) + ADAM_WD * w)
    return delta, m, v


def _adamw_big(w, g, m, v, name):
    shape = w.shape
    cols = shape[-1]
    rows = w.size // cols
    tr = _row_tile(rows, cols, limit=1 << 20)

    def body(w_ref, g_ref, m_ref, v_ref, d_ref, nm_ref, nv_ref):
        d, nm, nv = _adam_math(w_ref[...], g_ref[...], m_ref[...], v_ref[...])
        d_ref[...] = d
        nm_ref[...] = nm
        nv_ref[...] = nv

    spec = BS((tr, cols), lambda i: (i, 0))
    outs = pl.pallas_call(
        body, grid=(rows // tr,), in_specs=[spec] * 4, out_specs=[spec] * 3,
        out_shape=[_sds((rows, cols), F32)] * 3,
        compiler_params=_cp(("arbitrary",)), name=name,
    )(*[a.reshape(rows, cols) for a in (w, g, m, v)])
    return [o.reshape(shape) for o in outs]


def _adamw_small(ws, gs, ms, vs):
    n = len(ws)

    def body(*refs):
        for i in range(n):
            d, nm, nv = _adam_math(refs[i][...], refs[n + i][...], refs[2 * n + i][...],
                                   refs[3 * n + i][...])
            refs[4 * n + i][...] = d
            refs[5 * n + i][...] = nm
            refs[6 * n + i][...] = nv

    vm = BS(memory_space=pltpu.VMEM)
    outs = pl.pallas_call(
        body, in_specs=[vm] * (4 * n), out_specs=[vm] * (3 * n),
        out_shape=[_sds(w.shape, F32) for w in ws] * 3,
        compiler_params=_cp(), name="adamw_small",
    )(*ws, *gs, *ms, *vs)
    return outs[:n], outs[n:2 * n], outs[2 * n:]


def _place():
    x, y, c = lax.axis_index("x"), lax.axis_index("y"), lax.axis_index("c")
    chips = [(1 - x, y), (x, 1 - y), (1 - x, 1 - y)]
    return x, y, c, chips


class _Exchange:
    def __init__(self, ins, in_specs, out_shapes, scratch, start, finish):
        self.ins, self.in_specs, self.out_shapes, self.scratch = ins, in_specs, out_shapes, scratch
        self.start, self.finish = start, finish


def _run_exchange(ex, name, vmem_mb=40):
    ni, no = len(ex.ins), len(ex.out_shapes)

    def body(*refs):
        ex.start(refs[:ni], refs[ni:ni + no], refs[ni + no:])
        ex.finish(refs[:ni], refs[ni:ni + no], refs[ni + no:])

    return pl.pallas_call(
        body, in_specs=ex.in_specs, out_specs=[ANY] * no, out_shape=ex.out_shapes,
        scratch_shapes=ex.scratch,
        compiler_params=pltpu.CompilerParams(has_side_effects=True, vmem_limit_bytes=vmem_mb << 20),
        name=name,
    )(*ex.ins)


def _gather_exchange(srcs, dst_shapes, views, small=None):
    nu = len(srcs)
    nd = len(dst_shapes)
    ns = 1 if small is not None else 0

    def unpack(ins, outs, scr):
        x, y, c, chips = _place()
        src = ins[:nu]
        vw = [views[u](outs[:nd]) for u in range(nu)]
        vbuf = scr[:nu]
        send, recv, fsend, frecv, lsem, ssend, srecv, vsem = scr[nu:]

        def ici(u, j, shard, to):
            return pltpu.make_async_remote_copy(
                src_ref=vbuf[u].at[:, c], dst_ref=vw[u].at[:, shard, c],
                send_sem=send.at[3 * u + j], recv_sem=recv.at[3 * u + j],
                device_id=to, device_id_type=MESH)

        def fwd(u, j, shard, half):
            return pltpu.make_async_remote_copy(
                src_ref=vw[u].at[:, shard, half], dst_ref=vw[u].at[:, shard, half],
                send_sem=fsend.at[3 * u + j], recv_sem=frecv.at[3 * u + j],
                device_id=(x, y, 1 - c), device_id_type=MESH)

        def small_copy(j, shard, to):
            return pltpu.make_async_remote_copy(
                src_ref=ins[nu], dst_ref=outs[nd].at[shard],
                send_sem=ssend.at[j], recv_sem=srecv.at[j], device_id=to, device_id_type=MESH)

        stage = [pltpu.make_async_copy(src[u], vbuf[u], vsem.at[u]) for u in range(nu)]
        local = [pltpu.make_async_copy(vbuf[u], vw[u].at[:, 2 * x + y], lsem.at[u]) for u in range(nu)]
        if ns:
            local.append(pltpu.make_async_copy(ins[nu], outs[nd].at[2 * x + y], lsem.at[nu]))
        return x, y, c, chips, ici, fwd, small_copy, stage, local

    def start(ins, outs, scr):
        x, y, c, chips, ici, fwd, small_copy, stage, local = unpack(ins, outs, scr)
        s = 2 * x + y
        for cp in stage:
            cp.start()
        if ns:
            local[nu].start()
            for j, chip in enumerate(chips):
                small_copy(j, s, (*chip, c)).start()
        for u in range(nu):
            stage[u].wait()
            for j, chip in enumerate(chips):
                ici(u, j, s, (*chip, c)).start()
            local[u].start()

    def finish(ins, outs, scr):
        x, y, c, chips, ici, fwd, small_copy, stage, local = unpack(ins, outs, scr)
        s = 2 * x + y
        for u in range(nu):
            for j, chip in enumerate(chips):
                sj = 2 * chip[0] + chip[1]
                ici(u, j, sj, (x, y, c)).wait_recv()
                fwd(u, j, sj, c).start()
        for u in range(nu):
            for j, chip in enumerate(chips):
                sj = 2 * chip[0] + chip[1]
                fwd(u, j, sj, 1 - c).wait_recv()
        for u in range(nu):
            for j, chip in enumerate(chips):
                ici(u, j, s, (*chip, c)).wait_send()
                fwd(u, j, s, c).wait_send()
        if ns:
            for j, chip in enumerate(chips):
                small_copy(j, 2 * chip[0] + chip[1], (x, y, c)).wait_recv()
                small_copy(j, s, (*chip, c)).wait_send()
        for cp in local:
            cp.wait()

    dma = pltpu.SemaphoreType.DMA
    return _Exchange(
        ins=list(srcs) + ([small] if ns else []),
        in_specs=[ANY] * nu + [BS(memory_space=pltpu.VMEM)] * ns,
        out_shapes=[_sds(sh, BF16) for sh in dst_shapes]
        + ([_sds((NSH,) + small.shape, F32)] if ns else []),
        scratch=[pltpu.VMEM(a.shape, BF16) for a in srcs]
        + [dma((3 * nu,)), dma((3 * nu,)), dma((3 * nu,)), dma((3 * nu,)),
           dma((nu + 1,)), dma((3,)), dma((3,)), dma((nu,))],
        start=start, finish=finish)


def _pair_reduce(g, name):
    pn, _, _, rh, cc = g.shape
    n = pn * NSH

    def body(g_ref, own_ref, sb_ref, sendb, recvb, stage, outf, outb, send, recv, lsem, osem):
        x, y, c, _ = _place()
        s = 2 * x + y

        def load(k, half):
            return pltpu.make_async_copy(g_ref.at[k // NSH, k % NSH, half], stage.at[k % 2], lsem.at[k % 2])

        def push(k):
            return pltpu.make_async_remote_copy(
                src_ref=sendb.at[k], dst_ref=recvb.at[k], send_sem=send.at[k], recv_sem=recv.at[k],
                device_id=(x, y, 1 - c), device_id_type=MESH)

        def store(k):
            return pltpu.make_async_copy(outb.at[k % 2], sb_ref.at[k // NSH, k % NSH], osem.at[k % 2])

        load(0, 1 - c).start()
        for k in range(n):
            if k + 1 < n:
                load(k + 1, 1 - c).start()
            load(k, 1 - c).wait()
            sendb[k] = stage[k % 2].astype(BF16)
            push(k).start()
        load(0, c).start()
        for k in range(n):
            if k + 1 < n:
                load(k + 1, c).start()
            load(k, c).wait()
            push(k).wait_recv()
            total = stage[k % 2] + recvb[k].astype(F32)
            if k >= 2:
                store(k - 2).wait()
            outb[k % 2] = total.astype(BF16)
            store(k).start()

            @pl.when(s == k % NSH)
            def _():
                outf[...] = total
                keep = pltpu.make_async_copy(outf, own_ref.at[k // NSH], osem.at[2])
                keep.start()
                keep.wait()

        for k in range(max(n - 2, 0), n):
            store(k).wait()
        for k in range(n):
            push(k).wait_send()

    dma = pltpu.SemaphoreType.DMA
    return pl.pallas_call(
        body, in_specs=[ANY], out_specs=[ANY, ANY],
        out_shape=[_sds((pn, rh, cc), F32), _sds((pn, NSH, rh, cc), BF16)],
        scratch_shapes=[pltpu.VMEM((n, rh, cc), BF16), pltpu.VMEM((n, rh, cc), BF16),
                        pltpu.VMEM((2, rh, cc), F32), pltpu.VMEM((rh, cc), F32),
                        pltpu.VMEM((2, rh, cc), BF16), dma((n,)), dma((n,)), dma((2,)), dma((3,))],
        compiler_params=pltpu.CompilerParams(has_side_effects=True, vmem_limit_bytes=56 << 20),
        name=name,
    )(g)


def _chip_exchange(sums_bf16):
    nu = len(sums_bf16)

    def pushes(ins, outs, scr):
        x, y, c, chips = _place()
        send, recv = scr
        return [pltpu.make_async_remote_copy(
            src_ref=ins[u].at[:, 2 * chip[0] + chip[1]], dst_ref=outs[u].at[j],
            send_sem=send.at[3 * u + j], recv_sem=recv.at[3 * u + j],
            device_id=(*chip, c), device_id_type=MESH)
            for u in range(nu) for j, chip in enumerate(chips)]

    def start(ins, outs, scr):
        for cp in pushes(ins, outs, scr):
            cp.start()

    def finish(ins, outs, scr):
        for cp in pushes(ins, outs, scr):
            cp.wait()

    dma = pltpu.SemaphoreType.DMA
    shapes = [(3, a.shape[0], a.shape[2], a.shape[3]) for a in sums_bf16]
    return _Exchange(ins=list(sums_bf16), in_specs=[ANY] * nu,
                     out_shapes=[_sds(sh, BF16) for sh in shapes],
                     scratch=[dma((3 * nu,)), dma((3 * nu,))], start=start, finish=finish)


def _final_exchange(halves, out_shapes, targets):
    nu = len(halves)
    no = len(out_shapes)
    ncp = sum(len(t) for t in targets)

    def body(*refs):
        hv = refs[:nu]
        out = refs[nu:nu + no]
        sbuf = refs[nu + no:2 * nu + no]
        rbuf = refs[2 * nu + no:3 * nu + no]
        send, recv, lsem, osem, csem = refs[3 * nu + no:]
        x, y, c, _ = _place()
        stage = [pltpu.make_async_copy(hv[u], sbuf[u], lsem.at[u]) for u in range(nu)]
        push = [pltpu.make_async_remote_copy(
            src_ref=sbuf[u], dst_ref=rbuf[u], send_sem=send.at[u], recv_sem=recv.at[u],
            device_id=(x, y, 1 - c), device_id_type=MESH) for u in range(nu)]
        mine, theirs = [], []
        k = 0
        for u in range(nu):
            rh = hv[u].shape[1]
            for (p, oi, li) in targets[u]:
                mine.append((u, pltpu.make_async_copy(
                    sbuf[u].at[p], out[oi].at[li, pl.ds(c * rh, rh), :], csem.at[k])))
                theirs.append((u, pltpu.make_async_copy(
                    rbuf[u].at[p], out[oi].at[li, pl.ds((1 - c) * rh, rh), :], osem.at[k])))
                k += 1
        for cp in stage:
            cp.start()
        for u in range(nu):
            stage[u].wait()
            push[u].start()
            for (v, cp) in mine:
                if v == u:
                    cp.start()
        for u in range(nu):
            push[u].wait_recv()
            for (v, cp) in theirs:
                if v == u:
                    cp.start()
        for (_, cp) in theirs + mine:
            cp.wait()
        for u in range(nu):
            push[u].wait_send()

    dma = pltpu.SemaphoreType.DMA
    bufs = [pltpu.VMEM(h.shape, F32) for h in halves]
    return pl.pallas_call(
        body, in_specs=[ANY] * nu, out_specs=[ANY] * no,
        out_shape=[_sds(sh, F32) for sh in out_shapes],
        scratch_shapes=bufs + bufs + [dma((nu,)), dma((nu,)), dma((nu,)), dma((ncp,)), dma((ncp,))],
        compiler_params=pltpu.CompilerParams(has_side_effects=True, vmem_limit_bytes=56 << 20),
        name="final_exchange",
    )(*halves)


def _small_allreduce(pack):
    rows = pack.shape[0]

    def body(p_ref, o_ref, buf, send, recv):
        x, y, c, _ = _place()
        me = 4 * x + 2 * y + c
        buf[me] = p_ref[...]
        k = 0
        copies = []
        for dx in range(2):
            for dy in range(2):
                for dc in range(2):
                    if dx == 0 and dy == 0 and dc == 0:
                        continue
                    to = (jnp.where(dx, 1 - x, x), jnp.where(dy, 1 - y, y), jnp.where(dc, 1 - c, c))
                    src_slot = 4 * to[0] + 2 * to[1] + to[2]
                    copies.append((pltpu.make_async_remote_copy(
                        src_ref=p_ref, dst_ref=buf.at[me], send_sem=send.at[k], recv_sem=recv.at[k],
                        device_id=to, device_id_type=MESH), src_slot, k))
                    k += 1
        for cp, _, _ in copies:
            cp.start()
        for cp, src_slot, k in copies:
            pltpu.make_async_remote_copy(
                src_ref=p_ref, dst_ref=buf.at[src_slot], send_sem=send.at[k], recv_sem=recv.at[k],
                device_id=(x, y, c), device_id_type=MESH).wait()
        acc = buf[0]
        for d in range(1, 8):
            acc = acc + buf[d]
        o_ref[...] = acc

    dma = pltpu.SemaphoreType.DMA
    vm = BS(memory_space=pltpu.VMEM)
    return pl.pallas_call(
        body, in_specs=[vm], out_specs=vm, out_shape=_sds((rows, D), F32),
        scratch_shapes=[pltpu.VMEM((8, rows, D), F32), dma((7,)), dma((7,))],
        compiler_params=pltpu.CompilerParams(has_side_effects=True, vmem_limit_bytes=32 << 20),
        name="small_allreduce",
    )(pack)


def _in_proj(h, w, bias, name):
    n = w.shape[1]
    tn = 640 if n == NA else 896
    ep = None
    extras, especs = (), ()
    if bias is not None:
        def ep(acc, ex, o_ref):
            o_ref[...] = acc + ex[0][...]
        extras = (bias,)
        especs = (BS((1, tn), lambda i, j, k: (0, j)),)
    return _mm("nn", h, w, grid=(T // TR, n // tn, 1),
               a_spec=BS((TR, D), lambda i, j, k: (i, 0)), b_spec=BS((D, tn), lambda i, j, k: (0, j)),
               out_shape=_sds((T, n), F32), out_spec=BS((TR, tn), lambda i, j, k: (i, j)),
               acc_shape=(TR, tn), extras=extras, extra_specs=especs, epilogue=ep, name=name)


def _add_res(acc, ex, o_ref):
    o_ref[...] = acc + ex[0][...]


def _local_step(x, mem, target, w, p, carries=None, bwd_carry_fn=None):
    row = lambda i, j, k: (i, 0)
    whole = lambda i, j, k: (0, 0)
    w = {k: (list(v) if isinstance(v, list) else v) for k, v in w.items()}
    carries = carries or {}

    def carry_of(name):
        return carries[name][0] if name in carries else None

    def delivered(name, outs):
        if name in carries:
            carries[name][1](w, outs)

    saved = []
    bias = _bias_expand(p["rel_u"])
    for l in range(2):
        type_a = l == 0
        h = _rms_fwd(x, p["norm1_g"][l:l + 1], f"rms1_{l}")
        memn = _rms_fwd(mem, p["mem_norm_g"][l:l + 1], f"rmsmem_{l}")
        if type_a:
            z = _in_proj(h, w["a"], None, "inproj_a")
            cat, carried = _attn_fwd(z, bias, p["a_q_g2"], p["a_k_g2"], carry_of("attn_fwd"))
            delivered("attn_fwd", carried)
            qcol = NA // MEMW - 1
        else:
            z = _in_proj(h, w["b"], p["b_b_in"], "inproj_b")
            cat = _conv_fwd(z, p["conv_w"], p["conv_b"], p["ln_g"], p["ln_b"])
            qcol = NBW // MEMW - 1
        kv = _mm("nn", memn, w["kv"][l], grid=(1, 1, 1),
                 a_spec=BS((NB * MEMT, D), whole), b_spec=BS((D, 2 * MEMW), whole),
                 out_shape=_sds((NB * MEMT, 2 * MEMW), F32), out_spec=BS((NB * MEMT, 2 * MEMW), whole),
                 acc_shape=(8, 128), name=f"memkv_{l}")
        cat = _memattn_fwd(z, kv, cat, p["mq_g4"][l:l + 1], p["mk_g4"][l:l + 1], qcol, f"memattn_fwd_{l}")
        x1 = _mm("nn", cat, w["wo"][l], grid=(T // TR, 1, 1), a_spec=BS((TR, D), row),
                 b_spec=BS((D, D), whole),
                 out_shape=_sds((T, D), F32), out_spec=BS((TR, D), row), acc_shape=(8, 128),
                 extras=(x,), extra_specs=(BS((TR, D), row),), epilogue=_add_res, name=f"outproj_{l}")
        h2 = _rms_fwd(x1, p["norm2_g"][l:l + 1], f"rms2_{l}")
        (g, u, act), carried = _ffn_up(h2, w["gu"][l], l, carry_of(f"ffn_up_{l}"))
        delivered(f"ffn_up_{l}", carried)
        x2 = _mm("nn", act, w["wd"][l], grid=(T // TR, 1, 1),
                 a_spec=BS((TR, FF), row), b_spec=BS((FF, D), whole),
                 out_shape=_sds((T, D), F32), out_spec=BS((TR, D), row), acc_shape=(8, 128),
                 extras=(x1,), extra_specs=(BS((TR, D), row),), epilogue=_add_res, name=f"ffn_down_{l}",
                 carry=carry_of(f"ffn_down_{l}"))
        if carry_of(f"ffn_down_{l}") is not None:
            x2, carried = x2
            delivered(f"ffn_down_{l}", carried)
        saved.append(dict(x=x, h=h, memn=memn, kv=kv, z=z, cat=cat, x1=x1, h2=h2, g=g, u=u, act=act,
                          qcol=qcol))
        x = x2

    loss, dx = _loss_head(x, target)

    big = dict(a=None, b=None, kv=[None, None], wo=[None, None], gu=[None, None], wd=[None, None])
    small = {}
    bwd_carried = ()
    tk = T // 2
    nkt = T // tk
    for l in (1, 0):
        sv = saved[l]
        dgu = _ffn_down_bwd(dx, w["wd"][l], sv["g"], sv["u"], l)
        big["wd"][l] = _mm("tn", sv["act"], dx, grid=(FF // FT, 1, 2 * nkt),
                           a_spec=BS((tk // 2, FT), lambda i, j, k: (k, i)),
                           b_spec=BS((tk // 2, D), lambda i, j, k: (k, 0)),
                           out_shape=_sds((FF, D), F32), out_spec=BS((FT, D), lambda i, j, k: (i, 0)),
                           acc_shape=(FT, D), name=f"dw_down_{l}")
        dh2 = _mm("nt", dgu, w["gu"][l], grid=(T // TR, 1, 2),
                  a_spec=BS((None, TR, FF), lambda i, j, k: (k, i, 0)),
                  b_spec=BS((None, D, FF), lambda i, j, k: (k, 0, 0)),
                  out_shape=_sds((T, D), F32), out_spec=BS((TR, D), row), acc_shape=(TR, D),
                  name=f"dh2_{l}")
        big["gu"][l] = _mm("tn", sv["h2"], dgu, grid=(2 * FF // FT, 1, nkt),
                           a_spec=BS((tk, D), lambda i, j, k: (k, 0)),
                           b_spec=BS((None, tk, FT), lambda i, j, k: (i // 2, k, i % 2)),
                           out_shape=_sds((2, D, FF), F32),
                           out_spec=BS((None, D, FT), lambda i, j, k: (i // 2, 0, i % 2)),
                           acc_shape=(D, FT), vmem_mb=56, name=f"dw_gu_{l}")
        dx1, small[f"norm2_g{l}"] = _rms_bwd(dh2, sv["x1"], p["norm2_g"][l:l + 1], dx, f"rms2_bwd_{l}")
        dcat = _mm("nt", dx1, w["wo"][l], grid=(T // TR, 1, 1), a_spec=BS((TR, D), row),
                   b_spec=BS((D, D), whole),
                   out_shape=_sds((T, D), F32), out_spec=BS((TR, D), row), acc_shape=(8, 128),
                   name=f"dcat_{l}")
        big["wo"][l] = _mm("tn", sv["cat"], dx1, grid=(1, 1, nkt),
                           a_spec=BS((tk, D), lambda i, j, k: (k, 0)), b_spec=BS((tk, D), lambda i, j, k: (k, 0)),
                           out_shape=_sds((D, D), F32), out_spec=BS((D, D), whole),
                           acc_shape=(D, D), name=f"dw_out_{l}")
        dqm, dkv, small[f"mq_g{l}"], small[f"mk_g{l}"] = _memattn_bwd(
            sv["z"], sv["kv"], dcat, p["mq_g4"][l:l + 1], p["mk_g4"][l:l + 1], sv["qcol"], f"memattn_bwd_{l}")
        if l == 0:
            carry = bwd_carry_fn(big) if bwd_carry_fn is not None else None
            (dq, dk, dv, dbias, small["a_q_g"], small["a_k_g"]), bwd_carried = _attn_bwd(
                sv["z"], dcat, bias, p["a_q_g2"], p["a_k_g2"], carry)
            small["rel_u"] = _bias_reduce(dbias)
            dz = jnp.concatenate([dq, dk, dv, dqm], axis=1)
            w_in, key, n, tn = w["a"], "a", NA, 640
        else:
            du, small["conv_w"], small["conv_b"], small["ln_g"], small["ln_b"], dbin_u = _conv_bwd(
                sv["z"], dcat, p["conv_w"], p["conv_b"], p["ln_g"], p["ln_b"])
            dz = jnp.concatenate([du, dqm], axis=1)
            small["b_in_u"] = dbin_u
            w_in, key, n, tn = w["b"], "b", NBW, 896
        dh = _mm("nt", dz, w_in, grid=(T // TR, 1, 1),
                 a_spec=BS((TR, n), lambda i, j, k: (i, 0)), b_spec=BS((D, n), lambda i, j, k: (0, 0)),
                 out_shape=_sds((T, D), F32), out_spec=BS((TR, D), row), acc_shape=(8, 128),
                 name=f"dh_{l}")
        big[key] = _mm("tn", sv["h"], dz, grid=(1, n // tn, nkt),
                       a_spec=BS((tk, D), lambda i, j, k: (k, 0)), b_spec=BS((tk, tn), lambda i, j, k: (k, j)),
                       out_shape=_sds((D, n), F32), out_spec=BS((D, tn), lambda i, j, k: (0, j)),
                       acc_shape=(D, tn), name=f"dw_in_{l}")
        if l == 1:
            small["b_in_qm"] = _colsum(dqm, "colsum_dqm")
        dx, small[f"norm1_g{l}"] = _rms_bwd(dh, sv["x"], p["norm1_g"][l:l + 1], dx1, f"rms1_bwd_{l}")
        big["kv"][l] = _mm("tn", sv["memn"], dkv, grid=(1, 1, 1),
                           a_spec=BS((NB * MEMT, D), whole), b_spec=BS((NB * MEMT, 2 * MEMW), whole),
                           out_shape=_sds((D, 2 * MEMW), F32), out_spec=BS((D, 2 * MEMW), whole),
                           acc_shape=(8, 128), name=f"dw_kv_{l}")
        dmemn = _mm("nt", dkv, w["kv"][l], grid=(1, 1, 1),
                    a_spec=BS((NB * MEMT, 2 * MEMW), whole), b_spec=BS((D, 2 * MEMW), whole),
                    out_shape=_sds((NB * MEMT, D), F32), out_spec=BS((NB * MEMT, D), whole),
                    acc_shape=(8, 128), name=f"dmemn_{l}")
        _, small[f"mem_norm_g{l}"] = _rms_bwd(dmemn, mem, p["mem_norm_g"][l:l + 1], None, f"rmsmem_bwd_{l}")
    return loss, dx, big, small, bwd_carried


def _colsum(a, name):
    rows, cols = a.shape

    def body(a_ref, o_ref):
        @pl.when(pl.program_id(0) == 0)
        def _():
            o_ref[...] = jnp.zeros_like(o_ref)

        o_ref[...] += jnp.sum(a_ref[...].astype(F32), axis=0, keepdims=True)

    return pl.pallas_call(
        body, grid=(rows // TR,), in_specs=[BS((TR, cols), lambda i: (i, 0))],
        out_specs=BS((1, cols), lambda i: (0, 0)), out_shape=_sds((1, cols), F32),
        compiler_params=_cp(("arbitrary",)), name=name,
    )(a)


_PACK_ROWS = 64


def _pad_to(a, rows, cols=D):
    return jnp.pad(a, ((0, rows - a.shape[0]), (0, cols - a.shape[1])))


def _pack_small(sm):
    parts = [
        jnp.concatenate([sm["norm1_g0"], sm["norm1_g1"]], 0),
        jnp.concatenate([sm["mem_norm_g0"], sm["mem_norm_g1"]], 0),
        jnp.concatenate([sm["norm2_g0"], sm["norm2_g1"]], 0),
        _pad_to(sm["a_q_g"], 1), _pad_to(sm["a_k_g"], 1),
        _pad_to(jnp.concatenate([sm["mq_g0"], sm["mq_g1"]], 0), 2),
        _pad_to(jnp.concatenate([sm["mk_g0"], sm["mk_g1"]], 0), 2),
        _pad_to(sm["conv_b"], 1), _pad_to(sm["ln_g"], 1), _pad_to(sm["ln_b"], 1),
        _pad_to(sm["b_in_u"][:, :D], 1), _pad_to(sm["b_in_u"][:, D:], 1),
        _pad_to(sm["b_in_qm"], 1),
        _pad_to(sm["conv_w"][:CONVW], CONVW),
        sm["rel_u"].reshape(12, D),
    ]
    pack = jnp.concatenate(parts, 0)
    return jnp.pad(pack, ((0, _PACK_ROWS - pack.shape[0]), (0, 0)))


def _rel_table_to_u(rel_bias):
    flat = jnp.concatenate([jnp.broadcast_to(rel_bias[:, 191:192], (12, 447)), rel_bias[:, ::-1]], axis=1)
    return jnp.pad(flat, ((0, 0), (192, 1024 - 192 - 639))).reshape(12, 1, 1024)


def _u_to_rel_table(du):
    flat = du[:, 192:192 + 639]
    g = flat[:, 447:][:, ::-1]
    return g, flat[:, :447]


def kernel(x, mem, norm1_g, mem_norm_g, a_w_in, a_q_g, a_k_g, a_rel_bias, b_w_in, b_b_in, b_conv_w, b_conv_b, b_ln_g, b_ln_b, mq_g, mk_g, w_mem_kv, w_out, norm2_g, w_gate, w_up, w_down, loss_target, m_norm1_g, m_mem_norm_g, m_a_w_in, m_a_q_g, m_a_k_g, m_a_rel_bias, m_b_w_in, m_b_b_in, m_b_conv_w, m_b_conv_b, m_b_ln_g, m_b_ln_b, m_mq_g, m_mk_g, m_w_mem_kv, m_w_out, m_norm2_g, m_w_gate, m_w_up, m_w_down, v_norm1_g, v_mem_norm_g, v_a_w_in, v_a_q_g, v_a_k_g, v_a_rel_bias, v_b_w_in, v_b_b_in, v_b_conv_w, v_b_conv_b, v_b_ln_g, v_b_ln_b, v_mq_g, v_mk_g, v_w_mem_kv, v_w_out, v_norm2_g, v_w_gate, v_w_up, v_w_down):
    sx = 2 * lax.axis_index("x") + lax.axis_index("y")

    n_in = (NA // NSH, NBW // NSH)
    w_in = (a_w_in, b_w_in)
    small_src = jnp.concatenate([
        jnp.pad(b_b_in, ((0, 0), (0, 512 - 448))),
        jnp.pad(b_conv_w[0], ((0, 0), (0, 512 - 192))),
        jnp.pad(jnp.concatenate([b_conv_b, b_ln_g, b_ln_b], 0), ((0, 0), (0, 512 - 192))),
        jnp.zeros((5, 512), F32)], 0)

    def gather_group(l, names, small=None):
        src_of = {
            "in": lambda: [_cast_bf16([w_in[l][0]], f"cast_in_{l}").reshape(1, 2, D // 2, n_in[l])],
            "kv": lambda: [_cast_bf16([w_mem_kv[l]], f"cast_kv_{l}").reshape(1, 2, 128, 2 * MEMW)],
            "wo": lambda: [_cast_bf16([w_out[l]], f"cast_wo_{l}").reshape(1, 2, 128, D)],
            "gu": lambda: [_cast_bf16([w_gate[l]], f"cast_gate_{l}").reshape(1, 2, D // 2, FS),
                           _cast_bf16([w_up[l]], f"cast_up_{l}").reshape(1, 2, D // 2, FS)],
            "wd": lambda: [_cast_bf16([w_down[l]], f"cast_wd_{l}").reshape(1, 2, FS // 2, D)]}
        dst_of = {"in": (1, NSH, 2, D // 2, n_in[l]), "kv": (1, NSH, 2, 128, 2 * MEMW),
                  "wo": (1, NSH, 2, 128, D), "gu": (1, 2, NSH, 2, D // 2, FS), "wd": (1, NSH, 2, FS // 2, D)}
        srcs, views = [], []
        for k, name in enumerate(names):
            srcs += src_of[name]()
            if name == "gu":
                views += [lambda d, k=k: d[k].at[:, 0], lambda d, k=k: d[k].at[:, 1]]
            else:
                views.append(lambda d, k=k: d[k])

        def done(w, outs):
            for k, name in enumerate(names):
                if name == "in":
                    w["b" if l else "a"] = outs[k].reshape(NSH, D, n_in[l]).transpose(1, 0, 2).reshape(
                        D, NSH * n_in[l])
                elif name == "gu":
                    w["gu"][l] = outs[k].reshape(2, NSH, D, FS).transpose(0, 2, 1, 3).reshape(2, D, FF)
                else:
                    shape = {"kv": (D, 2 * MEMW), "wo": (D, D), "wd": (FF, D)}
                    w[name][l] = outs[k].reshape(shape[name])

        return _gather_exchange(srcs, [dst_of[name] for name in names], views, small), done

    w = dict(a=None, b=None, kv=[None, None], wo=[None, None], gu=[None, None], wd=[None, None])
    first, first_done = gather_group(0, ["in", "kv"], small_src)
    outs0 = _run_exchange(first, "gather_weights_first")
    first_done(w, outs0)
    small_all = outs0[2]
    carries = {"attn_fwd": gather_group(0, ["wo", "gu", "wd"]),
               "ffn_up_0": gather_group(1, ["in", "kv", "wo", "wd"]),
               "ffn_down_0": gather_group(1, ["gu"])}

    conv_w_full = small_all[:, 1:1 + CONVW, :192].transpose(1, 0, 2).reshape(CONVW, TOK)
    vec3 = small_all[:, 32:35, :192].transpose(1, 0, 2).reshape(3, TOK)
    p = dict(
        norm1_g=norm1_g, mem_norm_g=mem_norm_g, norm2_g=norm2_g,
        a_q_g2=jnp.tile(a_q_g, (1, 2)), a_k_g2=jnp.tile(a_k_g, (1, 2)),
        mq_g4=jnp.tile(mq_g, (1, 4)), mk_g4=jnp.tile(mk_g, (1, 4)),
        rel_u=_rel_table_to_u(a_rel_bias[0]),
        b_b_in=small_all[:, 0, :448].reshape(1, NBW),
        conv_w=jnp.pad(conv_w_full, ((0, 1), (0, 0))),
        conv_b=vec3[0:1], ln_g=vec3[1:2], ln_b=vec3[2:3])

    def pair_sums(items, big):
        own, sums_b = [], []
        for l, name in items:
            if name == "in":
                g = (big["b"] if l else big["a"]).reshape(D, NSH, n_in[l]).transpose(1, 0, 2)
                g = g.reshape(1, NSH, 2, D // 2, n_in[l])
            elif name == "gu":
                g = big["gu"][l].reshape(2, D, NSH, FS).transpose(0, 2, 1, 3).reshape(2, NSH, 2, D // 2, FS)
            else:
                shape = {"kv": (1, NSH, 2, 128, 2 * MEMW), "wo": (1, NSH, 2, 128, D),
                         "wd": (1, NSH, 2, FS // 2, D)}
                g = big[name][l].reshape(shape[name])
            of, sb = _pair_reduce(g, f"pair_reduce_{name}_{l}")
            own.append(of)
            sums_b.append(sb)
        return own, sums_b

    early = [(1, "in"), (1, "kv"), (1, "wo"), (1, "gu"), (1, "wd"), (0, "wo"), (0, "gu"), (0, "wd")]
    late = [(0, "in"), (0, "kv")]
    own_early = []

    def bwd_carry_fn(big):
        own, sums_b = pair_sums(early, big)
        own_early.extend(own)
        return _chip_exchange(sums_b)

    loss, grad_x, big, small, parts_early = _local_step(
        x.reshape(T, D), mem.reshape(NB * MEMT, D), loss_target.reshape(T, D), w, p,
        carries=carries, bwd_carry_fn=bwd_carry_fn)
    loss = lax.psum(loss[0, 0], ("x", "y", "c"))

    own_late, sums_b_late = pair_sums(late, big)
    parts_late = _run_exchange(_chip_exchange(sums_b_late), "chip_exchange_late")
    items = early + late
    halves = [_quad_sum(o, pt, f"quad_sum_{name}_{l}")
              for (l, name), o, pt in zip(items, own_early + own_late, list(parts_early) + list(parts_late))]
    out_shapes = [(1, D, NA // NSH), (1, D, NBW // NSH), (2, 2 * 128, 2 * MEMW), (2, 2 * 128, D),
                  (2, D, FS), (2, D, FS), (2, FS, D)]
    target_of = {"in": lambda l: [(0, l, 0)], "kv": lambda l: [(0, 2, l)], "wo": lambda l: [(0, 3, l)],
                 "gu": lambda l: [(0, 4, l), (1, 5, l)], "wd": lambda l: [(0, 6, l)]}
    targets = [target_of[name](l) for l, name in items]
    g_a, g_b, g_kv, g_wo, g_gate, g_up, g_wd = _final_exchange(halves, out_shapes, targets)

    tot = _small_allreduce(_pack_small(small))
    g_rel, clip_part = _u_to_rel_table(tot[49:61])
    g_rel = jnp.concatenate([g_rel[:, :191], g_rel[:, 191:] + _rowsum(clip_part)], axis=1)
    b_in_full = jnp.concatenate([tot[15:16], tot[16:17, :512], tot[17:18, :MEMW]], axis=1)
    g_small = dict(
        norm1_g=tot[0:2], mem_norm_g=tot[2:4], norm2_g=tot[4:6],
        a_q_g=tot[6:7, :HD], a_k_g=tot[7:8, :HD], a_rel_bias=g_rel[None],
        b_b_in=lax.dynamic_slice(b_in_full, (0, sx * 448), (1, 448)),
        b_conv_w=lax.dynamic_slice(tot[18:49, :TOK], (0, sx * 192), (CONVW, 192))[None],
        b_conv_b=lax.dynamic_slice(tot[12:13, :TOK], (0, sx * 192), (1, 192)),
        b_ln_g=lax.dynamic_slice(tot[13:14, :TOK], (0, sx * 192), (1, 192)),
        b_ln_b=lax.dynamic_slice(tot[14:15, :TOK], (0, sx * 192), (1, 192)),
        mq_g=tot[8:10, :HD], mk_g=tot[10:12, :HD])

    names = ["norm1_g", "mem_norm_g", "a_w_in", "a_q_g", "a_k_g", "a_rel_bias", "b_w_in", "b_b_in",
             "b_conv_w", "b_conv_b", "b_ln_g", "b_ln_b", "mq_g", "mk_g", "w_mem_kv", "w_out",
             "norm2_g", "w_gate", "w_up", "w_down"]
    weights = dict(zip(names, [norm1_g, mem_norm_g, a_w_in, a_q_g, a_k_g, a_rel_bias, b_w_in, b_b_in,
                               b_conv_w, b_conv_b, b_ln_g, b_ln_b, mq_g, mk_g, w_mem_kv, w_out,
                               norm2_g, w_gate, w_up, w_down]))
    ms = dict(zip(names, [m_norm1_g, m_mem_norm_g, m_a_w_in, m_a_q_g, m_a_k_g, m_a_rel_bias, m_b_w_in,
                          m_b_b_in, m_b_conv_w, m_b_conv_b, m_b_ln_g, m_b_ln_b, m_mq_g, m_mk_g,
                          m_w_mem_kv, m_w_out, m_norm2_g, m_w_gate, m_w_up, m_w_down]))
    vs = dict(zip(names, [v_norm1_g, v_mem_norm_g, v_a_w_in, v_a_q_g, v_a_k_g, v_a_rel_bias, v_b_w_in,
                          v_b_b_in, v_b_conv_w, v_b_conv_b, v_b_ln_g, v_b_ln_b, v_mq_g, v_mk_g,
                          v_w_mem_kv, v_w_out, v_norm2_g, v_w_gate, v_w_up, v_w_down]))
    grads = dict(g_small)
    grads.update(a_w_in=g_a, b_w_in=g_b, w_mem_kv=g_kv, w_out=g_wo, w_gate=g_gate, w_up=g_up, w_down=g_wd)
    big_names = ["a_w_in", "b_w_in", "w_mem_kv", "w_out", "w_gate", "w_up", "w_down"]
    small_names = [n for n in names if n not in big_names]
    delta, new_m, new_v = {}, {}, {}
    for n in big_names:
        delta[n], new_m[n], new_v[n] = _adamw_big(weights[n], grads[n], ms[n], vs[n], f"adamw_{n}")
    as2d = lambda a: a.reshape(-1, a.shape[-1])
    d_s, m_s, v_s = _adamw_small([as2d(weights[n]) for n in small_names], [as2d(grads[n]) for n in small_names],
                                 [as2d(ms[n]) for n in small_names], [as2d(vs[n]) for n in small_names])
    for i, n in enumerate(small_names):
        delta[n] = d_s[i].reshape(weights[n].shape)
        new_m[n] = m_s[i].reshape(weights[n].shape)
        new_v[n] = v_s[i].reshape(weights[n].shape)

    return (loss, grad_x.reshape(NB, SEQ, D), *[grads[n] for n in names], *[delta[n] for n in names],
            *[new_m[n] for n in names], *[new_v[n] for n in names])


def _rowsum(a):
    def body(a_ref, o_ref):
        o_ref[...] = jnp.sum(a_ref[...], axis=1, keepdims=True)

    vm = BS(memory_space=pltpu.VMEM)
    return pl.pallas_call(body, in_specs=[vm], out_specs=vm, out_shape=_sds((a.shape[0], 1), F32),
                          compiler_params=_cp(), name="rowsum")(a)
```

```python
import functools

import jax
import jax.numpy as jnp
from jax import lax
from jax.experimental import pallas as pl
from jax.experimental.pallas import tpu as pltpu

F32 = jnp.float32
BF16 = jnp.bfloat16
BS = pl.BlockSpec
ANY = pl.BlockSpec(memory_space=pl.ANY)
MESH = pl.DeviceIdType.MESH

D = 1024
SEQ = 2048
NB = 2
T = NB * SEQ
MEMT = 256
HD = 64
TOK = 768
MEMW = 256
NA = 3 * TOK + MEMW
NBW = 2 * TOK + MEMW
FF = 2816
NSH = 4
FS = FF // NSH
FT = FF // 2
CONVW = 31
EPS = 1e-6
NEG = -1e30
SCALE = HD ** -0.5
QB = 256
KWIN = 768
KPAD = 512
TR = 512

ADAM_LR = 0.001
ADAM_B1 = 0.9
ADAM_B2 = 0.999
ADAM_EPS = 1e-08
ADAM_WD = 0.01
ADAM_STEP = 10

_DIMS = {
    "nn": (((1,), (0,)), ((), ())),
    "nt": (((1,), (1,)), ((), ())),
    "tn": (((0,), (0,)), ((), ())),
}


def _cp(sem=None, vmem_mb=48):
    return pltpu.CompilerParams(dimension_semantics=sem, vmem_limit_bytes=vmem_mb << 20)


def _sds(shape, dtype):
    return jax.ShapeDtypeStruct(tuple(shape), dtype)


def _mm(mode, a, b, *, grid, a_spec, b_spec, out_shape, out_spec, acc_shape, name,
        extras=(), extra_specs=(), epilogue=None, carry=None, vmem_mb=48):
    n_ex = len(extras)
    nk = grid[2]
    dims = _DIMS[mode]
    ni = len(carry.ins) if carry else 0
    no = len(carry.out_shapes) if carry else 0

    def body(a_ref, b_ref, *rest):
        ex = rest[:n_ex]
        cin = rest[n_ex:n_ex + ni]
        o_ref = rest[n_ex + ni]
        cout = rest[n_ex + ni + 1:n_ex + ni + 1 + no]
        acc = rest[n_ex + ni + 1 + no]
        cscr = rest[n_ex + ni + 2 + no:]
        k = pl.program_id(2)
        if carry:
            ids = [pl.program_id(d) for d in range(3)]

            @pl.when((ids[0] == 0) & (ids[1] == 0) & (ids[2] == 0))
            def _():
                carry.start(cin, cout, cscr)

        prod = lax.dot_general(a_ref[...].astype(BF16), b_ref[...].astype(BF16), dims,
                               preferred_element_type=F32)

        def finish(val):
            if epilogue is None:
                o_ref[...] = val.astype(o_ref.dtype)
            else:
                epilogue(val, ex, o_ref)

        if nk == 1:
            finish(prod)
        else:
            @pl.when(k == 0)
            def _():
                acc[...] = prod

            @pl.when((k > 0) & (k < nk - 1))
            def _():
                acc[...] += prod

            @pl.when(k == nk - 1)
            def _():
                finish(acc[...] + prod)

        if carry:
            @pl.when((ids[0] == grid[0] - 1) & (ids[1] == grid[1] - 1) & (ids[2] == grid[2] - 1))
            def _():
                carry.finish(cin, cout, cscr)

    acc_scratch = pltpu.VMEM(acc_shape if nk > 1 else (8, 128), F32)
    if not carry:
        return pl.pallas_call(
            body, grid=grid, in_specs=[a_spec, b_spec, *extra_specs], out_specs=out_spec,
            out_shape=out_shape, scratch_shapes=[acc_scratch],
            compiler_params=_cp(("parallel", "parallel", "arbitrary"), vmem_mb), name=name,
        )(a, b, *extras)
    outs = pl.pallas_call(
        body, grid=grid, in_specs=[a_spec, b_spec, *extra_specs, *carry.in_specs],
        out_specs=[out_spec] + [ANY] * no, out_shape=[out_shape] + carry.out_shapes,
        scratch_shapes=[acc_scratch] + carry.scratch,
        compiler_params=pltpu.CompilerParams(
            dimension_semantics=("arbitrary", "arbitrary", "arbitrary"),
            vmem_limit_bytes=vmem_mb << 20, has_side_effects=True), name=name,
    )(a, b, *extras, *carry.ins)
    return outs[0], outs[1:]


def _rms_fwd(x, g, name):
    rows = x.shape[0]

    def body(x_ref, g_ref, o_ref):
        xv = x_ref[...]
        r = lax.rsqrt(jnp.mean(xv * xv, axis=-1, keepdims=True) + EPS)
        o_ref[...] = (xv * r * g_ref[...]).astype(BF16)

    return pl.pallas_call(
        body, grid=(rows // TR,),
        in_specs=[BS((TR, D), lambda i: (i, 0)), BS((1, D), lambda i: (0, 0))],
        out_specs=BS((TR, D), lambda i: (i, 0)), out_shape=_sds((rows, D), BF16),
        compiler_params=_cp(("arbitrary",)), name=name,
    )(x, g)


def _rms_bwd(dh, x, g, dres, name):
    rows = x.shape[0]
    has_res = dres is not None

    def body(*refs):
        if has_res:
            dh_ref, x_ref, g_ref, r_ref, dx_ref, dg_ref = refs
        else:
            dh_ref, x_ref, g_ref, dx_ref, dg_ref = refs
        xv = x_ref[...]
        dhv = dh_ref[...]
        r = lax.rsqrt(jnp.mean(xv * xv, axis=-1, keepdims=True) + EPS)
        xh = xv * r
        gy = dhv * g_ref[...]
        dx = r * (gy - xh * jnp.mean(gy * xh, axis=-1, keepdims=True))
        if has_res:
            dx = dx + r_ref[...]
        dx_ref[...] = dx

        @pl.when(pl.program_id(0) == 0)
        def _():
            dg_ref[...] = jnp.zeros_like(dg_ref)

        dg_ref[...] += jnp.sum(dhv * xh, axis=0, keepdims=True)

    row = BS((TR, D), lambda i: (i, 0))
    vec = BS((1, D), lambda i: (0, 0))
    ins = [dh, x, g] + ([dres] if has_res else [])
    return pl.pallas_call(
        body, grid=(rows // TR,),
        in_specs=[row, row, vec] + ([row] if has_res else []),
        out_specs=[row, vec], out_shape=[_sds((rows, D), F32), _sds((1, D), F32)],
        compiler_params=_cp(("arbitrary",)), name=name,
    )(*ins)


def _group_masks(width):
    lane = lax.broadcasted_iota(jnp.int32, (1, width), 1)
    return [(lane >= HD * h) & (lane < HD * (h + 1)) for h in range(width // HD)]


def _group_mean(v, masks):
    width = v.shape[-1]
    gi = lax.broadcasted_iota(jnp.int32, (width, width), 0) // HD
    gj = lax.broadcasted_iota(jnp.int32, (width, width), 1) // HD
    avg = jnp.where(gi == gj, 1.0 / HD, 0.0).astype(F32)
    return jnp.dot(v, avg, precision=lax.Precision.HIGHEST, preferred_element_type=F32)


def _head_norm(zv, g, masks):
    r = lax.rsqrt(_group_mean(zv * zv, masks) + EPS)
    return zv * r * g


def _head_norm_bwd(dy, zv, g, masks):
    r = lax.rsqrt(_group_mean(zv * zv, masks) + EPS)
    zh = zv * r
    gy = dy * g
    dz = r * (gy - zh * _group_mean(gy * zh, masks))
    return dz, jnp.sum(dy * zh, axis=0, keepdims=True)


def _fold_heads(v, width):
    vb = jnp.broadcast_to(v, (8, width))
    out = vb
    for h in range(1, width // HD):
        out = out + pltpu.roll(vb, width - HD * h, axis=1)
    return out[0:1]


def _bias_expand(u):
    def body(u_ref, o_ref):
        x = jnp.broadcast_to(u_ref[...], (QB, 1024))
        rolled = pltpu.roll(x, 1024 - (QB - 1), axis=1, stride=1, stride_axis=0)[:, :KWIN]
        row = lax.broadcasted_iota(jnp.int32, (QB, 1), 0)
        col = lax.broadcasted_iota(jnp.int32, (1, KWIN), 1)
        lo = (row // 64) * 64
        ok = (col >= lo) & (col < lo + 576)
        o_ref[...] = jnp.where(ok, rolled, NEG)

    return pl.pallas_call(
        body, grid=(12,), in_specs=[BS((None, 1, 1024), lambda h: (h, 0, 0))],
        out_specs=BS((None, QB, KWIN), lambda h: (h, 0, 0)), out_shape=_sds((12, QB, KWIN), F32),
        compiler_params=_cp(("arbitrary",)), name="bias_expand",
    )(u)


def _bias_reduce(ds):
    def body(d_ref, o_ref):
        ri = lax.broadcasted_iota(jnp.int32, (QB, QB), 0)
        ci = lax.broadcasted_iota(jnp.int32, (QB, QB), 1)
        flip = (ri + ci == QB - 1).astype(F32)
        drev = jnp.dot(flip, d_ref[...], precision=lax.Precision.HIGHEST, preferred_element_type=F32)
        x = jnp.concatenate([drev, jnp.zeros((QB, 1024 - KWIN), F32)], axis=1)
        rolled = pltpu.roll(x, 0, axis=1, stride=1, stride_axis=0)
        o_ref[...] = jnp.sum(rolled, axis=0, keepdims=True)

    return pl.pallas_call(
        body, grid=(12,), in_specs=[BS((None, QB, KWIN), lambda h: (h, 0, 0))],
        out_specs=BS((None, 1, 1024), lambda h: (h, 0, 0)), out_shape=_sds((12, 1, 1024), F32),
        compiler_params=_cp(("arbitrary",)), name="bias_reduce",
    )(ds)


AW = 256
AH = AW // HD
AG = TOK // AW
def _attn_softmax(qh, kw, bias, startadd):
    s = lax.dot_general(qh, kw, _DIMS["nt"], preferred_element_type=F32) + bias + startadd
    m = jnp.max(s, axis=-1, keepdims=True)
    p = jnp.exp(s - m)
    return p * (1.0 / jnp.sum(p, axis=-1, keepdims=True))


def _attn_prologue(q_ref, k_ref, v_ref, gq_ref, gk_ref, qn_s, kn_s, v_s, masks):
    kn_s[0:KPAD, :] = jnp.zeros((KPAD, AW), BF16)
    v_s[0:KPAD, :] = jnp.zeros((KPAD, AW), BF16)
    for r in range(0, SEQ, TR):
        qn_s[r:r + TR, :] = (_head_norm(q_ref[r:r + TR, :], gq_ref[...], masks) * SCALE).astype(BF16)
        kn_s[KPAD + r:KPAD + r + TR, :] = _head_norm(k_ref[r:r + TR, :], gk_ref[...], masks).astype(BF16)
        v_s[KPAD + r:KPAD + r + TR, :] = v_ref[r:r + TR, :].astype(BF16)


def _attn_fwd(z, bias, gq2, gk2, carry=None):
    ni = len(carry.ins) if carry else 0
    no = len(carry.out_shapes) if carry else 0

    def body(q_ref, k_ref, v_ref, b_ref, gq_ref, gk_ref, *rest):
        cin, o_ref, cout = rest[:ni], rest[ni], rest[ni + 1:ni + 1 + no]
        qn_s, kn_s, v_s = rest[ni + 1 + no:ni + 4 + no]
        cscr = rest[ni + 4 + no:]
        if carry:
            @pl.when((pl.program_id(0) == 0) & (pl.program_id(1) == 0))
            def _():
                carry.start(cin, cout, cscr)

        masks = _group_masks(AW)
        _attn_prologue(q_ref, k_ref, v_ref, gq_ref, gk_ref, qn_s, kn_s, v_s, masks)
        col = lax.broadcasted_iota(jnp.int32, (1, KWIN), 1)

        def blk(i, carry):
            r0 = pl.multiple_of(i * QB, QB)
            qb = qn_s[pl.ds(r0, QB), :]
            kw = kn_s[pl.ds(r0, KWIN), :]
            vw = v_s[pl.ds(r0, KWIN), :]
            startadd = jnp.where(col + r0 < KPAD, NEG, 0.0)
            o = jnp.zeros((QB, AW), F32)
            for h in range(AH):
                qh = jnp.where(masks[h], qb, jnp.zeros_like(qb))
                vh = jnp.where(masks[h], vw, jnp.zeros_like(vw))
                p = _attn_softmax(qh, kw, b_ref[h], startadd).astype(BF16)
                o = o + jnp.dot(p, vh, preferred_element_type=F32)
            o_ref[pl.ds(r0, QB), :] = o.astype(BF16)
            return carry

        lax.fori_loop(0, SEQ // QB, blk, 0)

        if carry:
            @pl.when((pl.program_id(0) == NB - 1) & (pl.program_id(1) == AG - 1))
            def _():
                carry.finish(cin, cout, cscr)

    vec = BS((1, AW), lambda b, hp: (0, 0))
    outs = pl.pallas_call(
        body, grid=(NB, AG),
        in_specs=[BS((SEQ, AW), lambda b, hp: (b, hp)),
                  BS((SEQ, AW), lambda b, hp: (b, AG + hp)),
                  BS((SEQ, AW), lambda b, hp: (b, 2 * AG + hp)),
                  BS((AH, QB, KWIN), lambda b, hp: (hp, 0, 0)), vec, vec]
        + (carry.in_specs if carry else []),
        out_specs=[BS((SEQ, AW), lambda b, hp: (b, hp))] + [ANY] * no,
        out_shape=[_sds((T, D), BF16)] + (carry.out_shapes if carry else []),
        scratch_shapes=[pltpu.VMEM((SEQ, AW), BF16), pltpu.VMEM((SEQ + KPAD, AW), BF16),
                        pltpu.VMEM((SEQ + KPAD, AW), BF16)] + (carry.scratch if carry else []),
        compiler_params=pltpu.CompilerParams(
            dimension_semantics=("arbitrary", "arbitrary"), vmem_limit_bytes=48 << 20,
            has_side_effects=bool(carry)), name="attn_fwd",
    )(z, z, z, bias, gq2, gk2, *(carry.ins if carry else []))
    return outs[0], outs[1:]


def _attn_bwd(z, dcat, bias, gq2, gk2, carry=None):
    ni = len(carry.ins) if carry else 0
    no = len(carry.out_shapes) if carry else 0

    def body(q_ref, k_ref, v_ref, do_ref, b_ref, gq_ref, gk_ref, *rest):
        cin = rest[:ni]
        dq_ref, dk_ref, dv_ref, db_ref, dgq_ref, dgk_ref = rest[ni:ni + 6]
        cout = rest[ni + 6:ni + 6 + no]
        qn_s, kn_s, v_s, dqn_s, dkn_s, dv_s = rest[ni + 6 + no:ni + 12 + no]
        cscr = rest[ni + 12 + no:]
        hp = pl.program_id(0)
        b = pl.program_id(1)
        if carry:
            @pl.when((hp == 0) & (b == 0))
            def _():
                carry.start(cin, cout, cscr)

        masks = _group_masks(AW)
        _attn_prologue(q_ref, k_ref, v_ref, gq_ref, gk_ref, qn_s, kn_s, v_s, masks)
        dkn_s[...] = jnp.zeros_like(dkn_s)
        dv_s[...] = jnp.zeros_like(dv_s)

        @pl.when(b == 0)
        def _():
            db_ref[...] = jnp.zeros_like(db_ref)

        @pl.when((b == 0) & (hp == 0))
        def _():
            dgq_ref[...] = jnp.zeros_like(dgq_ref)
            dgk_ref[...] = jnp.zeros_like(dgk_ref)

        col = lax.broadcasted_iota(jnp.int32, (1, KWIN), 1)

        def blk(i, carry):
            r0 = pl.multiple_of(i * QB, QB)
            qb = qn_s[pl.ds(r0, QB), :]
            kw = kn_s[pl.ds(r0, KWIN), :]
            vw = v_s[pl.ds(r0, KWIN), :]
            dob = do_ref[pl.ds(r0, QB), :].astype(BF16)
            startadd = jnp.where(col + r0 < KPAD, NEG, 0.0)
            dqn = jnp.zeros((QB, AW), F32)
            dkw = jnp.zeros((KWIN, AW), F32)
            dvw = jnp.zeros((KWIN, AW), F32)
            for h in range(AH):
                qh = jnp.where(masks[h], qb, jnp.zeros_like(qb))
                kh = jnp.where(masks[h], kw, jnp.zeros_like(kw))
                doh = jnp.where(masks[h], dob, jnp.zeros_like(dob))
                p = _attn_softmax(qh, kw, b_ref[h], startadd)
                dvw = dvw + lax.dot_general(p.astype(BF16), doh, _DIMS["tn"],
                                            preferred_element_type=F32)
                dp = lax.dot_general(doh, vw, _DIMS["nt"], preferred_element_type=F32)
                ds = p * (dp - jnp.sum(dp * p, axis=-1, keepdims=True))
                db_ref[h] += ds
                dsb = ds.astype(BF16)
                dqn = dqn + jnp.dot(dsb, kh, preferred_element_type=F32)
                dkw = dkw + lax.dot_general(dsb, qh, _DIMS["tn"], preferred_element_type=F32)
            dqn_s[pl.ds(r0, QB), :] = dqn * SCALE
            dkn_s[pl.ds(r0, KWIN), :] += dkw
            dv_s[pl.ds(r0, KWIN), :] += dvw
            return carry

        lax.fori_loop(0, SEQ // QB, blk, 0)

        dgq = jnp.zeros((1, AW), F32)
        dgk = jnp.zeros((1, AW), F32)
        for r in range(0, SEQ, TR):
            dq, dg = _head_norm_bwd(dqn_s[r:r + TR, :], q_ref[r:r + TR, :], gq_ref[...], masks)
            dq_ref[r:r + TR, :] = dq.astype(BF16)
            dgq = dgq + dg
            dk, dg = _head_norm_bwd(dkn_s[KPAD + r:KPAD + r + TR, :], k_ref[r:r + TR, :], gk_ref[...], masks)
            dk_ref[r:r + TR, :] = dk.astype(BF16)
            dgk = dgk + dg
            dv_ref[r:r + TR, :] = dv_s[KPAD + r:KPAD + r + TR, :].astype(BF16)
        dgq_ref[...] += _fold_heads(dgq, AW)
        dgk_ref[...] += _fold_heads(dgk, AW)

        if carry:
            @pl.when((hp == AG - 1) & (b == NB - 1))
            def _():
                carry.finish(cin, cout, cscr)

    vec = BS((1, AW), lambda hp, b: (0, 0))
    row = BS((SEQ, AW), lambda hp, b: (b, hp))
    outs = pl.pallas_call(
        body, grid=(AG, NB),
        in_specs=[row,
                  BS((SEQ, AW), lambda hp, b: (b, AG + hp)),
                  BS((SEQ, AW), lambda hp, b: (b, 2 * AG + hp)),
                  row,
                  BS((AH, QB, KWIN), lambda hp, b: (hp, 0, 0)), vec, vec]
        + (carry.in_specs if carry else []),
        out_specs=[row, row, row, BS((AH, QB, KWIN), lambda hp, b: (hp, 0, 0)), vec, vec] + [ANY] * no,
        out_shape=[_sds((T, TOK), BF16), _sds((T, TOK), BF16), _sds((T, TOK), BF16),
                   _sds((12, QB, KWIN), F32), _sds((1, AW), F32), _sds((1, AW), F32)]
        + (carry.out_shapes if carry else []),
        scratch_shapes=[pltpu.VMEM((SEQ, AW), BF16), pltpu.VMEM((SEQ + KPAD, AW), BF16),
                        pltpu.VMEM((SEQ + KPAD, AW), BF16), pltpu.VMEM((SEQ, AW), F32),
                        pltpu.VMEM((SEQ + KPAD, AW), F32), pltpu.VMEM((SEQ + KPAD, AW), F32)]
        + (carry.scratch if carry else []),
        compiler_params=pltpu.CompilerParams(
            dimension_semantics=("arbitrary", "arbitrary"), vmem_limit_bytes=58 << 20,
            has_side_effects=bool(carry)), name="attn_bwd",
    )(z, z, z, dcat, bias, gq2, gk2, *(carry.ins if carry else []))
    return outs[:6], outs[6:]


def _mem_softmax(qh, kn):
    s = lax.dot_general(qh, kn, _DIMS["nt"], preferred_element_type=F32)
    m = jnp.max(s, axis=-1, keepdims=True)
    p = jnp.exp(s - m)
    return p * (1.0 / jnp.sum(p, axis=-1, keepdims=True))


def _memattn_fwd(z, kv, cat, gq4, gk4, qcol, name):
    def body(q_ref, k_ref, v_ref, gq_ref, gk_ref, cat_ref, o_ref):
        del cat_ref
        masks = _group_masks(MEMW)
        qn = (_head_norm(q_ref[...], gq_ref[...], masks) * SCALE).astype(BF16)
        kn = _head_norm(k_ref[...], gk_ref[...], masks).astype(BF16)
        vv = v_ref[...].astype(BF16)
        o = jnp.zeros((TR, MEMW), F32)
        for h in range(4):
            qh = jnp.where(masks[h], qn, jnp.zeros_like(qn))
            vh = jnp.where(masks[h], vv, jnp.zeros_like(vv))
            p = _mem_softmax(qh, kn).astype(BF16)
            o = o + jnp.dot(p, vh, preferred_element_type=F32)
        o_ref[...] = o.astype(BF16)

    nt = SEQ // TR
    vec = BS((1, MEMW), lambda b, t: (0, 0))
    return pl.pallas_call(
        body, grid=(NB, nt),
        in_specs=[BS((TR, MEMW), lambda b, t: (b * nt + t, qcol)),
                  BS((MEMT, MEMW), lambda b, t: (b, 0)),
                  BS((MEMT, MEMW), lambda b, t: (b, 1)), vec, vec, ANY],
        out_specs=BS((TR, MEMW), lambda b, t: (b * nt + t, 3)),
        out_shape=_sds((T, D), BF16), input_output_aliases={5: 0},
        compiler_params=_cp(("arbitrary", "arbitrary")), name=name,
    )(z, kv, kv, gq4, gk4, cat)


def _memattn_bwd(z, kv, dcat, gq4, gk4, qcol, name):
    nt = SEQ // TR

    def body(q_ref, k_ref, v_ref, do_ref, gq_ref, gk_ref,
             dq_ref, dkv_ref, dgq_ref, dgk_ref, dkn_s, dv_s):
        b = pl.program_id(0)
        t = pl.program_id(1)
        masks = _group_masks(MEMW)
        qz = q_ref[...]
        kz = k_ref[...]
        qn = (_head_norm(qz, gq_ref[...], masks) * SCALE).astype(BF16)
        kn = _head_norm(kz, gk_ref[...], masks).astype(BF16)
        vv = v_ref[...].astype(BF16)
        dob = do_ref[...].astype(BF16)

        @pl.when(t == 0)
        def _():
            dkn_s[...] = jnp.zeros_like(dkn_s)
            dv_s[...] = jnp.zeros_like(dv_s)

        @pl.when((t == 0) & (b == 0))
        def _():
            dgq_ref[...] = jnp.zeros_like(dgq_ref)
            dgk_ref[...] = jnp.zeros_like(dgk_ref)

        dqn = jnp.zeros((TR, MEMW), F32)
        dkn = jnp.zeros((MEMT, MEMW), F32)
        dvv = jnp.zeros((MEMT, MEMW), F32)
        for h in range(4):
            qh = jnp.where(masks[h], qn, jnp.zeros_like(qn))
            kh = jnp.where(masks[h], kn, jnp.zeros_like(kn))
            doh = jnp.where(masks[h], dob, jnp.zeros_like(dob))
            p = _mem_softmax(qh, kn)
            dvv = dvv + lax.dot_general(p.astype(BF16), doh, _DIMS["tn"], preferred_element_type=F32)
            dp = lax.dot_general(doh, vv, _DIMS["nt"], preferred_element_type=F32)
            ds = p * (dp - jnp.sum(dp * p, axis=-1, keepdims=True))
            dsb = ds.astype(BF16)
            dqn = dqn + jnp.dot(dsb, kh, preferred_element_type=F32)
            dkn = dkn + lax.dot_general(dsb, qh, _DIMS["tn"], preferred_element_type=F32)
        dkn_s[...] += dkn
        dv_s[...] += dvv
        dq, dgq = _head_norm_bwd(dqn * SCALE, qz, gq_ref[...], masks)
        dq_ref[...] = dq.astype(BF16)
        dgq_ref[...] += _fold_heads(dgq, MEMW)

        @pl.when(t == nt - 1)
        def _():
            dk, dgk = _head_norm_bwd(dkn_s[...], kz, gk_ref[...], masks)
            dkv_ref[:, 0:MEMW] = dk
            dkv_ref[:, MEMW:] = dv_s[...]
            dgk_ref[...] += _fold_heads(dgk, MEMW)

    vec = BS((1, MEMW), lambda b, t: (0, 0))
    return pl.pallas_call(
        body, grid=(NB, nt),
        in_specs=[BS((TR, MEMW), lambda b, t: (b * nt + t, qcol)),
                  BS((MEMT, MEMW), lambda b, t: (b, 0)),
                  BS((MEMT, MEMW), lambda b, t: (b, 1)),
                  BS((TR, MEMW), lambda b, t: (b * nt + t, 3)), vec, vec],
        out_specs=[BS((TR, MEMW), lambda b, t: (b * nt + t, 0)),
                   BS((MEMT, 2 * MEMW), lambda b, t: (b, 0)), vec, vec],
        out_shape=[_sds((T, MEMW), BF16), _sds((NB * MEMT, 2 * MEMW), F32),
                   _sds((1, MEMW), F32), _sds((1, MEMW), F32)],
        scratch_shapes=[pltpu.VMEM((MEMT, MEMW), F32), pltpu.VMEM((MEMT, MEMW), F32)],
        compiler_params=_cp(("arbitrary", "arbitrary")), name=name,
    )(z, kv, kv, dcat, gq4, gk4)


HALO = 32
NEXT = 64
RT = 64


def _glu(zz):
    return zz[:, :TOK] * jax.nn.sigmoid(zz[:, TOK:])


def _layer_norm_parts(y):
    mu = jnp.mean(y, axis=-1, keepdims=True)
    yc = y - mu
    rstd = lax.rsqrt(jnp.mean(yc * yc, axis=-1, keepdims=True) + EPS)
    return yc * rstd, rstd


def _shifted_copies(src, dst, rows):
    for b in range(1, 8):
        dst[b - 1, 0:rows, :] = src[b:b + rows, :]


def _tap(src, shifted, off, r0, rows):
    b = off % 8
    if b == 0:
        return src[r0 + off:r0 + off + rows, :]
    return shifted[b - 1, r0 + off - b:r0 + off - b + rows, :]


def _conv_rows(w_ref, hbuf, hs, r0, rows):
    y = jnp.zeros((rows, TOK), F32)
    for j in range(CONVW):
        y = y + w_ref[j:j + 1, :] * _tap(hbuf, hs, (HALO - CONVW + 1) + j, r0, rows)
    return y


def _conv_fwd(z, cw, cb, lg, lb):
    nt = SEQ // TR

    def body(zc_ref, zp_ref, w_ref, cb_ref, lg_ref, lb_ref, o_ref, hbuf, hs):
        t = pl.program_id(1)
        hbuf[0:HALO, :] = jnp.where(t == 0, 0.0, _glu(zp_ref[...]))
        hbuf[HALO:, :] = _glu(zc_ref[...])
        _shifted_copies(hbuf, hs, HALO + TR - 8)
        for r0 in range(0, TR, RT):
            y = _conv_rows(w_ref, hbuf, hs, r0, RT) + cb_ref[...]
            yh, _ = _layer_norm_parts(y)
            o = yh * lg_ref[...] + lb_ref[...]
            o_ref[r0:r0 + RT, :] = (o * jax.nn.sigmoid(o)).astype(BF16)

    vec = BS((1, TOK), lambda b, t: (0, 0))
    per = TR // HALO
    return pl.pallas_call(
        body, grid=(NB, nt),
        in_specs=[BS((TR, 2 * TOK), lambda b, t: (b * nt + t, 0)),
                  BS((HALO, 2 * TOK), lambda b, t: (jnp.maximum((b * nt + t) * per - 1, 0), 0)),
                  BS((32, TOK), lambda b, t: (0, 0)), vec, vec, vec],
        out_specs=BS((TR, TOK), lambda b, t: (b * nt + t, 0)),
        out_shape=_sds((T, D), BF16),
        scratch_shapes=[pltpu.VMEM((HALO + TR, TOK), F32), pltpu.VMEM((7, HALO + TR, TOK), F32)],
        compiler_params=_cp(("arbitrary", "arbitrary")), name="conv_fwd",
    )(z, z, cw, cb, lg, lb)


def _conv_bwd(z, dcat, cw, cb, lg, lb):
    nt = SEQ // TR
    ext = TR + NEXT

    def body(zc_ref, zp_ref, zn_ref, dc_ref, dn_ref, w_ref, cb_ref, lg_ref, lb_ref,
             du_ref, dw_ref, dcb_ref, dlg_ref, dlb_ref, dbin_ref, hbuf, dybuf, hs, dys):
        b = pl.program_id(0)
        t = pl.program_id(1)

        @pl.when((b == 0) & (t == 0))
        def _():
            dw_ref[...] = jnp.zeros_like(dw_ref)
            dcb_ref[...] = jnp.zeros_like(dcb_ref)
            dlg_ref[...] = jnp.zeros_like(dlg_ref)
            dlb_ref[...] = jnp.zeros_like(dlb_ref)
            dbin_ref[...] = jnp.zeros_like(dbin_ref)

        hbuf[0:HALO, :] = jnp.where(t == 0, 0.0, _glu(zp_ref[...]))
        hbuf[HALO:HALO + TR, :] = _glu(zc_ref[...])
        hbuf[HALO + TR:, :] = _glu(zn_ref[...])
        _shifted_copies(hbuf, hs, HALO + TR + NEXT - 8)
        last = t == nt - 1
        for r0 in range(0, ext, RT):
            y = _conv_rows(w_ref, hbuf, hs, r0, RT) + cb_ref[...]
            yh, rstd = _layer_norm_parts(y)
            o = yh * lg_ref[...] + lb_ref[...]
            sg = jax.nn.sigmoid(o)
            if r0 < TR:
                dtok = dc_ref[r0:r0 + RT, :]
            else:
                dtok = jnp.where(last, 0.0, dn_ref[...])
            do = dtok * (sg * (1.0 + o * (1.0 - sg)))
            dyh = do * lg_ref[...]
            dy = rstd * (dyh - jnp.mean(dyh, axis=-1, keepdims=True)
                         - yh * jnp.mean(dyh * yh, axis=-1, keepdims=True))
            dybuf[r0:r0 + RT, :] = dy
            if r0 < TR:
                dlg_ref[...] += jnp.sum(do * yh, axis=0, keepdims=True)
                dlb_ref[...] += jnp.sum(do, axis=0, keepdims=True)
                dcb_ref[...] += jnp.sum(dy, axis=0, keepdims=True)
        _shifted_copies(dybuf, dys, ext - 8)
        for r0 in range(0, TR, RT):
            dh = jnp.zeros((RT, TOK), F32)
            for j in range(CONVW):
                dh = dh + w_ref[j:j + 1, :] * _tap(dybuf, dys, (CONVW - 1) - j, r0, RT)
            a = zc_ref[r0:r0 + RT, 0:TOK]
            sg = jax.nn.sigmoid(zc_ref[r0:r0 + RT, TOK:])
            da = dh * sg
            dg = dh * a * (sg * (1.0 - sg))
            du_ref[r0:r0 + RT, 0:TOK] = da.astype(BF16)
            du_ref[r0:r0 + RT, TOK:] = dg.astype(BF16)
            dbin_ref[:, 0:TOK] += jnp.sum(da, axis=0, keepdims=True)
            dbin_ref[:, TOK:] += jnp.sum(dg, axis=0, keepdims=True)
        for j in range(CONVW):
            acc = jnp.zeros((8, TOK), F32)
            for r0 in range(0, TR, RT):
                prod = dybuf[r0:r0 + RT, :] * _tap(hbuf, hs, (HALO - CONVW + 1) + j, r0, RT)
                acc = acc + jnp.sum(prod.reshape(RT // 8, 8, TOK), axis=0)
            dw_ref[j:j + 1, :] += jnp.sum(acc, axis=0, keepdims=True)

    vec = BS((1, TOK), lambda b, t: (0, 0))
    perh = TR // HALO
    pern = TR // NEXT
    nlast_n = T // NEXT - 1
    return pl.pallas_call(
        body, grid=(NB, nt),
        in_specs=[BS((TR, 2 * TOK), lambda b, t: (b * nt + t, 0)),
                  BS((HALO, 2 * TOK), lambda b, t: (jnp.maximum((b * nt + t) * perh - 1, 0), 0)),
                  BS((NEXT, 2 * TOK), lambda b, t: (jnp.minimum((b * nt + t + 1) * pern, nlast_n), 0)),
                  BS((TR, TOK), lambda b, t: (b * nt + t, 0)),
                  BS((NEXT, TOK), lambda b, t: (jnp.minimum((b * nt + t + 1) * pern, nlast_n), 0)),
                  BS((32, TOK), lambda b, t: (0, 0)), vec, vec, vec],
        out_specs=[BS((TR, 2 * TOK), lambda b, t: (b * nt + t, 0)),
                   BS((32, TOK), lambda b, t: (0, 0)), vec, vec, vec,
                   BS((1, 2 * TOK), lambda b, t: (0, 0))],
        out_shape=[_sds((T, 2 * TOK), BF16), _sds((32, TOK), F32), _sds((1, TOK), F32),
                   _sds((1, TOK), F32), _sds((1, TOK), F32), _sds((1, 2 * TOK), F32)],
        scratch_shapes=[pltpu.VMEM((HALO + TR + NEXT, TOK), F32), pltpu.VMEM((ext, TOK), F32),
                        pltpu.VMEM((7, HALO + TR + NEXT, TOK), F32), pltpu.VMEM((7, ext, TOK), F32)],
        compiler_params=_cp(("arbitrary", "arbitrary"), vmem_mb=56), name="conv_bwd",
    )(z, z, z, dcat, dcat, cw, cb, lg, lb)


def _ffn_up(h2, wgu, l, carry=None):
    ni = len(carry.ins) if carry else 0
    no = len(carry.out_shapes) if carry else 0

    def body(h_ref, wg_ref, wu_ref, *rest):
        cin = rest[:ni]
        g_ref, u_ref, a_ref = rest[ni:ni + 3]
        cout, cscr = rest[ni + 3:ni + 3 + no], rest[ni + 3 + no:]
        if carry:
            @pl.when((pl.program_id(0) == 0) & (pl.program_id(1) == 0))
            def _():
                carry.start(cin, cout, cscr)

        hv = h_ref[...]
        g = jnp.dot(hv, wg_ref[...], preferred_element_type=F32)
        u = jnp.dot(hv, wu_ref[...], preferred_element_type=F32)
        g_ref[...] = g
        u_ref[...] = u
        a_ref[...] = (g * jax.nn.sigmoid(g) * u).astype(BF16)

        if carry:
            @pl.when((pl.program_id(0) == FF // FT - 1) & (pl.program_id(1) == T // TR - 1))
            def _():
                carry.finish(cin, cout, cscr)

    out = BS((TR, FT), lambda q, i: (i, q))
    outs = pl.pallas_call(
        body, grid=(FF // FT, T // TR),
        in_specs=[BS((TR, D), lambda q, i: (i, 0)),
                  BS((None, D, FT), lambda q, i: (0, 0, q)),
                  BS((None, D, FT), lambda q, i: (1, 0, q))] + (carry.in_specs if carry else []),
        out_specs=[out, out, out] + [ANY] * no,
        out_shape=[_sds((T, FF), F32), _sds((T, FF), F32), _sds((T, FF), BF16)]
        + (carry.out_shapes if carry else []),
        scratch_shapes=carry.scratch if carry else [],
        compiler_params=pltpu.CompilerParams(
            dimension_semantics=("arbitrary", "arbitrary"), vmem_limit_bytes=48 << 20,
            has_side_effects=bool(carry)), name=f"ffn_up_{l}",
    )(h2, wgu, wgu, *(carry.ins if carry else []))
    return outs[:3], outs[3:]


def _ffn_down_bwd(dx, wd, g, u, l):
    def epilogue(dact, ex, o_ref):
        gv = ex[0][...]
        uv = ex[1][...]
        sg = jax.nn.sigmoid(gv)
        o_ref[0] = (dact * uv * (sg * (1.0 + gv * (1.0 - sg)))).astype(BF16)
        o_ref[1] = (dact * (gv * sg)).astype(BF16)

    ex_spec = BS((TR, FT), lambda i, q, k: (i, q))
    return _mm("nt", dx, wd, grid=(T // TR, FF // FT, 1),
               a_spec=BS((TR, D), lambda i, q, k: (i, 0)),
               b_spec=BS((FT, D), lambda i, q, k: (q, 0)),
               out_shape=_sds((2, T, FF), BF16),
               out_spec=BS((2, TR, FT), lambda i, q, k: (0, i, q)),
               acc_shape=(TR, FT), extras=(g, u), extra_specs=(ex_spec, ex_spec),
               epilogue=epilogue, name=f"ffn_down_bwd_{l}")


def _loss_head(y, target):
    def body(y_ref, t_ref, l_ref, dy_ref, acc):
        i = pl.program_id(0)
        e = y_ref[...] - t_ref[...]
        dy_ref[...] = e * (1.0 / D)

        @pl.when(i == 0)
        def _():
            acc[...] = jnp.zeros_like(acc)

        acc[...] += jnp.sum(jnp.mean(e * e, axis=-1, keepdims=True), axis=0, keepdims=True)

        @pl.when(i == T // TR - 1)
        def _():
            l_ref[...] = 0.5 * acc[...]

    row = BS((TR, D), lambda i: (i, 0))
    return pl.pallas_call(
        body, grid=(T // TR,), in_specs=[row, row],
        out_specs=[BS((1, 1), lambda i: (0, 0)), row],
        out_shape=[_sds((1, 1), F32), _sds((T, D), F32)],
        scratch_shapes=[pltpu.VMEM((1, 1), F32)],
        compiler_params=_cp(("arbitrary",)), name="loss_head",
    )(y, target)


def _row_tile(rows, cols, itemsize=4, limit=2 << 20):
    tr = rows
    while tr * cols * itemsize > limit and tr % 2 == 0 and (tr // 2) % 16 == 0:
        tr //= 2
    return tr


def _cast_bf16(arrs, name):
    n = len(arrs)
    rows, cols = arrs[0].shape
    tr = _row_tile(rows, cols)

    def body(*refs):
        o_ref = refs[n]
        k = pl.program_id(0)
        val = refs[0][...]
        for j in range(1, n):
            val = jnp.where(k == j, refs[j][...], val)
        o_ref[...] = val.astype(BF16)

    return pl.pallas_call(
        body, grid=(n, rows // tr),
        in_specs=[BS((tr, cols), lambda k, i: (i, 0))] * n,
        out_specs=BS((None, tr, cols), lambda k, i: (k, i, 0)),
        out_shape=_sds((n, rows, cols), BF16),
        compiler_params=_cp(("arbitrary", "arbitrary")), name=name,
    )(*arrs)


def _quad_sum(own, got, name):
    n, rows, cols = own.shape
    tr = _row_tile(rows, cols)

    def body(a_ref, q_ref, o_ref):
        o_ref[...] = ((a_ref[...] + q_ref[0].astype(F32)) + q_ref[1].astype(F32)) + q_ref[2].astype(F32)

    spec = BS((None, tr, cols), lambda k, i: (k, i, 0))
    return pl.pallas_call(
        body, grid=(n, rows // tr),
        in_specs=[spec, BS((3, None, tr, cols), lambda k, i: (0, k, i, 0))], out_specs=spec,
        out_shape=_sds((n, rows, cols), F32),
        compiler_params=_cp(("arbitrary", "arbitrary")), name=name,
    )(own, got)


def _adam_math(w, g, m, v):
    m = ADAM_B1 * m + (1.0 - ADAM_B1) * g
    v = ADAM_B2 * v + (1.0 - ADAM_B2) * (g * g)
    m_hat = m / (1.0 - ADAM_B1 ** ADAM_STEP)
    v_hat = v / (1.0 - ADAM_B2 ** ADAM_STEP)
    delta = -ADAM_LR * (m_hat / (jnp.sqrt(v_hat) + ADAM_EPS) + ADAM_WD * w)
    return delta, m, v


def _adamw_big(w, g, m, v, name):
    shape = w.shape
    cols = shape[-1]
    rows = w.size // cols
    tr = _row_tile(rows, cols, limit=1 << 20)

    def body(w_ref, g_ref, m_ref, v_ref, d_ref, nm_ref, nv_ref):
        d, nm, nv = _adam_math(w_ref[...], g_ref[...], m_ref[...], v_ref[...])
        d_ref[...] = d
        nm_ref[...] = nm
        nv_ref[...] = nv

    spec = BS((tr, cols), lambda i: (i, 0))
    outs = pl.pallas_call(
        body, grid=(rows // tr,), in_specs=[spec] * 4, out_specs=[spec] * 3,
        out_shape=[_sds((rows, cols), F32)] * 3,
        compiler_params=_cp(("arbitrary",)), name=name,
    )(*[a.reshape(rows, cols) for a in (w, g, m, v)])
    return [o.reshape(shape) for o in outs]


def _adamw_small(ws, gs, ms, vs):
    n = len(ws)

    def body(*refs):
        for i in range(n):
            d, nm, nv = _adam_math(refs[i][...], refs[n + i][...], refs[2 * n + i][...],
                                   refs[3 * n + i][...])
            refs[4 * n + i][...] = d
            refs[5 * n + i][...] = nm
            refs[6 * n + i][...] = nv

    specs = [BS(w.shape, lambda i: (0, 0)) for w in ws]
    outs = pl.pallas_call(
        body, grid=(1,), in_specs=specs * 4, out_specs=specs * 3,
        out_shape=[_sds(w.shape, F32) for w in ws] * 3,
        compiler_params=_cp(("arbitrary",)), name="adamw_small",
    )(*ws, *gs, *ms, *vs)
    return outs[:n], outs[n:2 * n], outs[2 * n:]


def _place():
    x, y, c = lax.axis_index("x"), lax.axis_index("y"), lax.axis_index("c")
    chips = [(1 - x, y), (x, 1 - y), (1 - x, 1 - y)]
    return x, y, c, chips


class _Exchange:
    def __init__(self, ins, in_specs, out_shapes, scratch, start, finish):
        self.ins, self.in_specs, self.out_shapes, self.scratch = ins, in_specs, out_shapes, scratch
        self.start, self.finish = start, finish


def _run_exchange(ex, name, vmem_mb=40):
    ni, no = len(ex.ins), len(ex.out_shapes)

    def body(*refs):
        ex.start(refs[:ni], refs[ni:ni + no], refs[ni + no:])
        ex.finish(refs[:ni], refs[ni:ni + no], refs[ni + no:])

    return pl.pallas_call(
        body, in_specs=ex.in_specs, out_specs=[ANY] * no, out_shape=ex.out_shapes,
        scratch_shapes=ex.scratch,
        compiler_params=pltpu.CompilerParams(has_side_effects=True, vmem_limit_bytes=vmem_mb << 20),
        name=name,
    )(*ex.ins)


def _gather_exchange(srcs, dst_shapes, views, small=None):
    nu = len(srcs)
    nd = len(dst_shapes)
    ns = 1 if small is not None else 0

    def unpack(ins, outs, scr):
        x, y, c, chips = _place()
        src = ins[:nu]
        vw = [views[u](outs[:nd]) for u in range(nu)]
        vbuf = scr[:nu]
        send, recv, fsend, frecv, lsem, ssend, srecv, vsem = scr[nu:]

        def ici(u, j, shard, to):
            return pltpu.make_async_remote_copy(
                src_ref=vbuf[u].at[:, c], dst_ref=vw[u].at[:, shard, c],
                send_sem=send.at[3 * u + j], recv_sem=recv.at[3 * u + j],
                device_id=to, device_id_type=MESH)

        def fwd(u, j, shard, half):
            return pltpu.make_async_remote_copy(
                src_ref=vw[u].at[:, shard, half], dst_ref=vw[u].at[:, shard, half],
                send_sem=fsend.at[3 * u + j], recv_sem=frecv.at[3 * u + j],
                device_id=(x, y, 1 - c), device_id_type=MESH)

        def small_copy(j, shard, to):
            return pltpu.make_async_remote_copy(
                src_ref=ins[nu], dst_ref=outs[nd].at[shard],
                send_sem=ssend.at[j], recv_sem=srecv.at[j], device_id=to, device_id_type=MESH)

        stage = [pltpu.make_async_copy(src[u], vbuf[u], vsem.at[u]) for u in range(nu)]
        local = [pltpu.make_async_copy(vbuf[u], vw[u].at[:, 2 * x + y], lsem.at[u]) for u in range(nu)]
        if ns:
            local.append(pltpu.make_async_copy(ins[nu], outs[nd].at[2 * x + y], lsem.at[nu]))
        return x, y, c, chips, ici, fwd, small_copy, stage, local

    def start(ins, outs, scr):
        x, y, c, chips, ici, fwd, small_copy, stage, local = unpack(ins, outs, scr)
        s = 2 * x + y
        for cp in stage:
            cp.start()
        if ns:
            local[nu].start()
            for j, chip in enumerate(chips):
                small_copy(j, s, (*chip, c)).start()
        for u in range(nu):
            stage[u].wait()
            for j, chip in enumerate(chips):
                ici(u, j, s, (*chip, c)).start()
            local[u].start()

    def finish(ins, outs, scr):
        x, y, c, chips, ici, fwd, small_copy, stage, local = unpack(ins, outs, scr)
        s = 2 * x + y
        for u in range(nu):
            for j, chip in enumerate(chips):
                sj = 2 * chip[0] + chip[1]
                ici(u, j, sj, (x, y, c)).wait_recv()
                fwd(u, j, sj, c).start()
        for u in range(nu):
            for j, chip in enumerate(chips):
                sj = 2 * chip[0] + chip[1]
                fwd(u, j, sj, 1 - c).wait_recv()
        for u in range(nu):
            for j, chip in enumerate(chips):
                ici(u, j, s, (*chip, c)).wait_send()
                fwd(u, j, s, c).wait_send()
        if ns:
            for j, chip in enumerate(chips):
                small_copy(j, 2 * chip[0] + chip[1], (x, y, c)).wait_recv()
                small_copy(j, s, (*chip, c)).wait_send()
        for cp in local:
            cp.wait()

    dma = pltpu.SemaphoreType.DMA
    return _Exchange(
        ins=list(srcs) + ([small] if ns else []),
        in_specs=[ANY] * nu + [BS(memory_space=pltpu.VMEM)] * ns,
        out_shapes=[_sds(sh, BF16) for sh in dst_shapes]
        + ([_sds((NSH,) + small.shape, F32)] if ns else []),
        scratch=[pltpu.VMEM(a.shape, BF16) for a in srcs]
        + [dma((3 * nu,)), dma((3 * nu,)), dma((3 * nu,)), dma((3 * nu,)),
           dma((nu + 1,)), dma((3,)), dma((3,)), dma((nu,))],
        start=start, finish=finish)


def _pair_reduce(g, name):
    pn, _, _, rh, cc = g.shape
    n = pn * NSH

    def body(g_ref, own_ref, sb_ref, sendb, recvb, stage, outf, outb, send, recv, lsem, osem):
        x, y, c, _ = _place()
        s = 2 * x + y

        def load(k, half):
            return pltpu.make_async_copy(g_ref.at[k // NSH, k % NSH, half], stage.at[k % 2], lsem.at[k % 2])

        def push(k):
            return pltpu.make_async_remote_copy(
                src_ref=sendb.at[k], dst_ref=recvb.at[k], send_sem=send.at[k], recv_sem=recv.at[k],
                device_id=(x, y, 1 - c), device_id_type=MESH)

        def store(k):
            return pltpu.make_async_copy(outb.at[k % 2], sb_ref.at[k // NSH, k % NSH], osem.at[k % 2])

        load(0, 1 - c).start()
        for k in range(n):
            if k + 1 < n:
                load(k + 1, 1 - c).start()
            load(k, 1 - c).wait()
            sendb[k] = stage[k % 2].astype(BF16)
            push(k).start()
        load(0, c).start()
        for k in range(n):
            if k + 1 < n:
                load(k + 1, c).start()
            load(k, c).wait()
            push(k).wait_recv()
            total = stage[k % 2] + recvb[k].astype(F32)
            if k >= 2:
                store(k - 2).wait()
            outb[k % 2] = total.astype(BF16)
            store(k).start()

            @pl.when(s == k % NSH)
            def _():
                outf[...] = total
                keep = pltpu.make_async_copy(outf, own_ref.at[k // NSH], osem.at[2])
                keep.start()
                keep.wait()

        for k in range(max(n - 2, 0), n):
            store(k).wait()
        for k in range(n):
            push(k).wait_send()

    dma = pltpu.SemaphoreType.DMA
    return pl.pallas_call(
        body, in_specs=[ANY], out_specs=[ANY, ANY],
        out_shape=[_sds((pn, rh, cc), F32), _sds((pn, NSH, rh, cc), BF16)],
        scratch_shapes=[pltpu.VMEM((n, rh, cc), BF16), pltpu.VMEM((n, rh, cc), BF16),
                        pltpu.VMEM((2, rh, cc), F32), pltpu.VMEM((rh, cc), F32),
                        pltpu.VMEM((2, rh, cc), BF16), dma((n,)), dma((n,)), dma((2,)), dma((3,))],
        compiler_params=pltpu.CompilerParams(has_side_effects=True, vmem_limit_bytes=56 << 20),
        name=name,
    )(g)


def _chip_exchange(sums_bf16):
    nu = len(sums_bf16)

    def pushes(ins, outs, scr):
        x, y, c, chips = _place()
        send, recv = scr
        return [pltpu.make_async_remote_copy(
            src_ref=ins[u].at[:, 2 * chip[0] + chip[1]], dst_ref=outs[u].at[j],
            send_sem=send.at[3 * u + j], recv_sem=recv.at[3 * u + j],
            device_id=(*chip, c), device_id_type=MESH)
            for u in range(nu) for j, chip in enumerate(chips)]

    def start(ins, outs, scr):
        for cp in pushes(ins, outs, scr):
            cp.start()

    def finish(ins, outs, scr):
        for cp in pushes(ins, outs, scr):
            cp.wait()

    dma = pltpu.SemaphoreType.DMA
    shapes = [(3, a.shape[0], a.shape[2], a.shape[3]) for a in sums_bf16]
    return _Exchange(ins=list(sums_bf16), in_specs=[ANY] * nu,
                     out_shapes=[_sds(sh, BF16) for sh in shapes],
                     scratch=[dma((3 * nu,)), dma((3 * nu,))], start=start, finish=finish)


def _final_exchange(halves, out_shapes, targets):
    nu = len(halves)
    no = len(out_shapes)
    ncp = sum(len(t) for t in targets)

    def body(*refs):
        hv = refs[:nu]
        out = refs[nu:nu + no]
        sbuf = refs[nu + no:2 * nu + no]
        rbuf = refs[2 * nu + no:3 * nu + no]
        send, recv, lsem, osem, csem = refs[3 * nu + no:]
        x, y, c, _ = _place()
        stage = [pltpu.make_async_copy(hv[u], sbuf[u], lsem.at[u]) for u in range(nu)]
        push = [pltpu.make_async_remote_copy(
            src_ref=sbuf[u], dst_ref=rbuf[u], send_sem=send.at[u], recv_sem=recv.at[u],
            device_id=(x, y, 1 - c), device_id_type=MESH) for u in range(nu)]
        mine, theirs = [], []
        k = 0
        for u in range(nu):
            rh = hv[u].shape[1]
            for (p, oi, li) in targets[u]:
                mine.append((u, pltpu.make_async_copy(
                    sbuf[u].at[p], out[oi].at[li, pl.ds(c * rh, rh), :], csem.at[k])))
                theirs.append((u, pltpu.make_async_copy(
                    rbuf[u].at[p], out[oi].at[li, pl.ds((1 - c) * rh, rh), :], osem.at[k])))
                k += 1
        for cp in stage:
            cp.start()
        for u in range(nu):
            stage[u].wait()
            push[u].start()
            for (v, cp) in mine:
                if v == u:
                    cp.start()
        for u in range(nu):
            push[u].wait_recv()
            for (v, cp) in theirs:
                if v == u:
                    cp.start()
        for (_, cp) in theirs + mine:
            cp.wait()
        for u in range(nu):
            push[u].wait_send()

    dma = pltpu.SemaphoreType.DMA
    bufs = [pltpu.VMEM(h.shape, F32) for h in halves]
    return pl.pallas_call(
        body, in_specs=[ANY] * nu, out_specs=[ANY] * no,
        out_shape=[_sds(sh, F32) for sh in out_shapes],
        scratch_shapes=bufs + bufs + [dma((nu,)), dma((nu,)), dma((nu,)), dma((ncp,)), dma((ncp,))],
        compiler_params=pltpu.CompilerParams(has_side_effects=True, vmem_limit_bytes=56 << 20),
        name="final_exchange",
    )(*halves)


def _small_allreduce(pack):
    rows = pack.shape[0]

    def body(p_ref, o_ref, buf, send, recv):
        x, y, c, _ = _place()
        me = 4 * x + 2 * y + c
        buf[me] = p_ref[...]
        k = 0
        copies = []
        for dx in range(2):
            for dy in range(2):
                for dc in range(2):
                    if dx == 0 and dy == 0 and dc == 0:
                        continue
                    to = (jnp.where(dx, 1 - x, x), jnp.where(dy, 1 - y, y), jnp.where(dc, 1 - c, c))
                    src_slot = 4 * to[0] + 2 * to[1] + to[2]
                    copies.append((pltpu.make_async_remote_copy(
                        src_ref=p_ref, dst_ref=buf.at[me], send_sem=send.at[k], recv_sem=recv.at[k],
                        device_id=to, device_id_type=MESH), src_slot, k))
                    k += 1
        for cp, _, _ in copies:
            cp.start()
        for cp, src_slot, k in copies:
            pltpu.make_async_remote_copy(
                src_ref=p_ref, dst_ref=buf.at[src_slot], send_sem=send.at[k], recv_sem=recv.at[k],
                device_id=(x, y, c), device_id_type=MESH).wait()
        acc = buf[0]
        for d in range(1, 8):
            acc = acc + buf[d]
        o_ref[...] = acc

    dma = pltpu.SemaphoreType.DMA
    vm = BS(memory_space=pltpu.VMEM)
    return pl.pallas_call(
        body, in_specs=[vm], out_specs=vm, out_shape=_sds((rows, D), F32),
        scratch_shapes=[pltpu.VMEM((8, rows, D), F32), dma((7,)), dma((7,))],
        compiler_params=pltpu.CompilerParams(has_side_effects=True, vmem_limit_bytes=32 << 20),
        name="small_allreduce",
    )(pack)


def _in_proj(h, w, bias, name):
    n = w.shape[1]
    tn = 640 if n == NA else 896
    ep = None
    extras, especs = (), ()
    if bias is not None:
        def ep(acc, ex, o_ref):
            o_ref[...] = acc + ex[0][...]
        extras = (bias,)
        especs = (BS((1, tn), lambda i, j, k: (0, j)),)
    return _mm("nn", h, w, grid=(T // TR, n // tn, 1),
               a_spec=BS((TR, D), lambda i, j, k: (i, 0)), b_spec=BS((D, tn), lambda i, j, k: (0, j)),
               out_shape=_sds((T, n), F32), out_spec=BS((TR, tn), lambda i, j, k: (i, j)),
               acc_shape=(TR, tn), extras=extras, extra_specs=especs, epilogue=ep, name=name)


def _add_res(acc, ex, o_ref):
    o_ref[...] = acc + ex[0][...]


def _split_shards(acc, ex, o_ref):
    o_ref[0] = acc[:, :FS]
    o_ref[1] = acc[:, FS:]


def _local_step(x, mem, target, w, p, carries=None, bwd_carry_fn=None):
    row = lambda i, j, k: (i, 0)
    whole = lambda i, j, k: (0, 0)
    w = {k: (list(v) if isinstance(v, list) else v) for k, v in w.items()}
    carries = carries or {}

    def carry_of(name):
        return carries[name][0] if name in carries else None

    def delivered(name, outs):
        if name in carries:
            carries[name][1](w, outs)

    saved = []
    bias = _bias_expand(p["rel_u"])
    for l in range(2):
        type_a = l == 0
        h = _rms_fwd(x, p["norm1_g"][l:l + 1], f"rms1_{l}")
        memn = _rms_fwd(mem, p["mem_norm_g"][l:l + 1], f"rmsmem_{l}")
        if type_a:
            z = _in_proj(h, w["a"], None, "inproj_a")
            cat, carried = _attn_fwd(z, bias, p["a_q_g2"], p["a_k_g2"], carry_of("attn_fwd"))
            delivered("attn_fwd", carried)
            qcol = NA // MEMW - 1
        else:
            z = _in_proj(h, w["b"], p["b_b_in"], "inproj_b")
            cat = _conv_fwd(z, p["conv_w"], p["conv_b"], p["ln_g"], p["ln_b"])
            qcol = NBW // MEMW - 1
        kv = _mm("nn", memn, w["kv"][l], grid=(1, 1, 1),
                 a_spec=BS((NB * MEMT, D), whole), b_spec=BS((D, 2 * MEMW), whole),
                 out_shape=_sds((NB * MEMT, 2 * MEMW), F32), out_spec=BS((NB * MEMT, 2 * MEMW), whole),
                 acc_shape=(8, 128), name=f"memkv_{l}")
        cat = _memattn_fwd(z, kv, cat, p["mq_g4"][l:l + 1], p["mk_g4"][l:l + 1], qcol, f"memattn_fwd_{l}")
        x1 = _mm("nn", cat, w["wo"][l], grid=(T // TR, 1, 1), a_spec=BS((TR, D), row),
                 b_spec=BS((D, D), whole),
                 out_shape=_sds((T, D), F32), out_spec=BS((TR, D), row), acc_shape=(8, 128),
                 extras=(x,), extra_specs=(BS((TR, D), row),), epilogue=_add_res, name=f"outproj_{l}")
        h2 = _rms_fwd(x1, p["norm2_g"][l:l + 1], f"rms2_{l}")
        (g, u, act), carried = _ffn_up(h2, w["gu"][l], l, carry_of(f"ffn_up_{l}"))
        delivered(f"ffn_up_{l}", carried)
        x2 = _mm("nn", act, w["wd"][l], grid=(T // TR, 1, 1),
                 a_spec=BS((TR, FF), row), b_spec=BS((FF, D), whole),
                 out_shape=_sds((T, D), F32), out_spec=BS((TR, D), row), acc_shape=(8, 128),
                 extras=(x1,), extra_specs=(BS((TR, D), row),), epilogue=_add_res, name=f"ffn_down_{l}",
                 carry=carry_of(f"ffn_down_{l}"))
        if carry_of(f"ffn_down_{l}") is not None:
            x2, carried = x2
            delivered(f"ffn_down_{l}", carried)
        saved.append(dict(x=x, h=h, memn=memn, kv=kv, z=z, cat=cat, x1=x1, h2=h2, g=g, u=u, act=act,
                          qcol=qcol))
        x = x2

    loss, dx = _loss_head(x, target)

    big = dict(a=None, b=None, kv=[None, None], wo=[None, None], gu=[None, None], wd=[None, None])
    small = {}
    bwd_carried = ()
    tk = T // 2
    nkt = T // tk
    for l in (1, 0):
        sv = saved[l]
        dgu = _ffn_down_bwd(dx, w["wd"][l], sv["g"], sv["u"], l)
        big["wd"][l] = _mm("tn", sv["act"], dx, grid=(FF // FT, 1, 2 * nkt),
                           a_spec=BS((tk // 2, FT), lambda i, j, k: (k, i)),
                           b_spec=BS((tk // 2, D), lambda i, j, k: (k, 0)),
                           out_shape=_sds((FF, D), F32), out_spec=BS((FT, D), lambda i, j, k: (i, 0)),
                           acc_shape=(FT, D), name=f"dw_down_{l}")
        dh2 = _mm("nt", dgu, w["gu"][l], grid=(T // TR, 1, 2),
                  a_spec=BS((None, TR, FF), lambda i, j, k: (k, i, 0)),
                  b_spec=BS((None, D, FF), lambda i, j, k: (k, 0, 0)),
                  out_shape=_sds((T, D), F32), out_spec=BS((TR, D), row), acc_shape=(TR, D),
                  name=f"dh2_{l}")
        big["gu"][l] = _mm("tn", sv["h2"], dgu, grid=(2 * FF // FT, 1, nkt),
                           a_spec=BS((tk, D), lambda i, j, k: (k, 0)),
                           b_spec=BS((None, tk, FT), lambda i, j, k: (i // 2, k, i % 2)),
                           out_shape=_sds((2, NSH, D, FS), F32),
                           out_spec=BS((None, 2, D, FS), lambda i, j, k: (i // 2, i % 2, 0, 0)),
                           acc_shape=(D, FT), vmem_mb=56, epilogue=_split_shards, name=f"dw_gu_{l}")
        dx1, small[f"norm2_g{l}"] = _rms_bwd(dh2, sv["x1"], p["norm2_g"][l:l + 1], dx, f"rms2_bwd_{l}")
        dcat = _mm("nt", dx1, w["wo"][l], grid=(T // TR, 1, 1), a_spec=BS((TR, D), row),
                   b_spec=BS((D, D), whole),
                   out_shape=_sds((T, D), F32), out_spec=BS((TR, D), row), acc_shape=(8, 128),
                   name=f"dcat_{l}")
        big["wo"][l] = _mm("tn", sv["cat"], dx1, grid=(1, 1, nkt),
                           a_spec=BS((tk, D), lambda i, j, k: (k, 0)), b_spec=BS((tk, D), lambda i, j, k: (k, 0)),
                           out_shape=_sds((D, D), F32), out_spec=BS((D, D), whole),
                           acc_shape=(D, D), name=f"dw_out_{l}")
        dqm, dkv, small[f"mq_g{l}"], small[f"mk_g{l}"] = _memattn_bwd(
            sv["z"], sv["kv"], dcat, p["mq_g4"][l:l + 1], p["mk_g4"][l:l + 1], sv["qcol"], f"memattn_bwd_{l}")
        if l == 0:
            carry = bwd_carry_fn(big) if bwd_carry_fn is not None else None
            (dq, dk, dv, dbias, small["a_q_g"], small["a_k_g"]), bwd_carried = _attn_bwd(
                sv["z"], dcat, bias, p["a_q_g2"], p["a_k_g2"], carry)
            small["rel_u"] = _bias_reduce(dbias)
            dz = jnp.concatenate([dq, dk, dv, dqm], axis=1)
            w_in, key, n, tn = w["a"], "a", NA, 640
        else:
            du, small["conv_w"], small["conv_b"], small["ln_g"], small["ln_b"], dbin_u = _conv_bwd(
                sv["z"], dcat, p["conv_w"], p["conv_b"], p["ln_g"], p["ln_b"])
            dz = jnp.concatenate([du, dqm], axis=1)
            small["b_in_u"] = dbin_u
            w_in, key, n, tn = w["b"], "b", NBW, 896
        dh = _mm("nt", dz, w_in, grid=(T // TR, 1, 1),
                 a_spec=BS((TR, n), lambda i, j, k: (i, 0)), b_spec=BS((D, n), lambda i, j, k: (0, 0)),
                 out_shape=_sds((T, D), F32), out_spec=BS((TR, D), row), acc_shape=(8, 128),
                 name=f"dh_{l}")
        big[key] = _mm("tn", sv["h"], dz, grid=(1, n // tn, nkt),
                       a_spec=BS((tk, D), lambda i, j, k: (k, 0)), b_spec=BS((tk, tn), lambda i, j, k: (k, j)),
                       out_shape=_sds((D, n), F32), out_spec=BS((D, tn), lambda i, j, k: (0, j)),
                       acc_shape=(D, tn), name=f"dw_in_{l}")
        if l == 1:
            small["b_in_qm"] = _colsum(dqm, "colsum_dqm")
        dx, small[f"norm1_g{l}"] = _rms_bwd(dh, sv["x"], p["norm1_g"][l:l + 1], dx1, f"rms1_bwd_{l}")
        big["kv"][l] = _mm("tn", sv["memn"], dkv, grid=(1, 1, 1),
                           a_spec=BS((NB * MEMT, D), whole), b_spec=BS((NB * MEMT, 2 * MEMW), whole),
                           out_shape=_sds((D, 2 * MEMW), F32), out_spec=BS((D, 2 * MEMW), whole),
                           acc_shape=(8, 128), name=f"dw_kv_{l}")
        dmemn = _mm("nt", dkv, w["kv"][l], grid=(1, 1, 1),
                    a_spec=BS((NB * MEMT, 2 * MEMW), whole), b_spec=BS((D, 2 * MEMW), whole),
                    out_shape=_sds((NB * MEMT, D), F32), out_spec=BS((NB * MEMT, D), whole),
                    acc_shape=(8, 128), name=f"dmemn_{l}")
        _, small[f"mem_norm_g{l}"] = _rms_bwd(dmemn, mem, p["mem_norm_g"][l:l + 1], None, f"rmsmem_bwd_{l}")
    return loss, dx, big, small, bwd_carried


def _colsum(a, name):
    rows, cols = a.shape

    def body(a_ref, o_ref):
        @pl.when(pl.program_id(0) == 0)
        def _():
            o_ref[...] = jnp.zeros_like(o_ref)

        o_ref[...] += jnp.sum(a_ref[...].astype(F32), axis=0, keepdims=True)

    return pl.pallas_call(
        body, grid=(rows // TR,), in_specs=[BS((TR, cols), lambda i: (i, 0))],
        out_specs=BS((1, cols), lambda i: (0, 0)), out_shape=_sds((1, cols), F32),
        compiler_params=_cp(("arbitrary",)), name=name,
    )(a)


_PACK_ROWS = 64


def _pad_to(a, rows, cols=D):
    return jnp.pad(a, ((0, rows - a.shape[0]), (0, cols - a.shape[1])))


def _pack_small(sm):
    parts = [
        jnp.concatenate([sm["norm1_g0"], sm["norm1_g1"]], 0),
        jnp.concatenate([sm["mem_norm_g0"], sm["mem_norm_g1"]], 0),
        jnp.concatenate([sm["norm2_g0"], sm["norm2_g1"]], 0),
        _pad_to(sm["a_q_g"], 1), _pad_to(sm["a_k_g"], 1),
        _pad_to(jnp.concatenate([sm["mq_g0"], sm["mq_g1"]], 0), 2),
        _pad_to(jnp.concatenate([sm["mk_g0"], sm["mk_g1"]], 0), 2),
        _pad_to(sm["conv_b"], 1), _pad_to(sm["ln_g"], 1), _pad_to(sm["ln_b"], 1),
        _pad_to(sm["b_in_u"][:, :D], 1), _pad_to(sm["b_in_u"][:, D:], 1),
        _pad_to(sm["b_in_qm"], 1),
        _pad_to(sm["conv_w"][:CONVW], CONVW),
        sm["rel_u"].reshape(12, D),
    ]
    pack = jnp.concatenate(parts, 0)
    return jnp.pad(pack, ((0, _PACK_ROWS - pack.shape[0]), (0, 0)))


def _rel_table_to_u(rel_bias):
    flat = jnp.concatenate([jnp.broadcast_to(rel_bias[:, 191:192], (12, 447)), rel_bias[:, ::-1]], axis=1)
    return jnp.pad(flat, ((0, 0), (192, 1024 - 192 - 639))).reshape(12, 1, 1024)


def _u_to_rel_table(du):
    flat = du[:, 192:192 + 639]
    g = flat[:, 447:][:, ::-1]
    return g, flat[:, :447]


def kernel(x, mem, norm1_g, mem_norm_g, a_w_in, a_q_g, a_k_g, a_rel_bias, b_w_in, b_b_in, b_conv_w, b_conv_b, b_ln_g, b_ln_b, mq_g, mk_g, w_mem_kv, w_out, norm2_g, w_gate, w_up, w_down, loss_target, m_norm1_g, m_mem_norm_g, m_a_w_in, m_a_q_g, m_a_k_g, m_a_rel_bias, m_b_w_in, m_b_b_in, m_b_conv_w, m_b_conv_b, m_b_ln_g, m_b_ln_b, m_mq_g, m_mk_g, m_w_mem_kv, m_w_out, m_norm2_g, m_w_gate, m_w_up, m_w_down, v_norm1_g, v_mem_norm_g, v_a_w_in, v_a_q_g, v_a_k_g, v_a_rel_bias, v_b_w_in, v_b_b_in, v_b_conv_w, v_b_conv_b, v_b_ln_g, v_b_ln_b, v_mq_g, v_mk_g, v_w_mem_kv, v_w_out, v_norm2_g, v_w_gate, v_w_up, v_w_down):
    sx = 2 * lax.axis_index("x") + lax.axis_index("y")

    n_in = (NA // NSH, NBW // NSH)
    w_in = (a_w_in, b_w_in)
    small_src = jnp.concatenate([
        jnp.pad(b_b_in, ((0, 0), (0, 512 - 448))),
        jnp.pad(b_conv_w[0], ((0, 0), (0, 512 - 192))),
        jnp.pad(jnp.concatenate([b_conv_b, b_ln_g, b_ln_b], 0), ((0, 0), (0, 512 - 192))),
        jnp.zeros((5, 512), F32)], 0)

    def gather_group(l, names, small=None):
        src_of = {
            "in": lambda: [_cast_bf16([w_in[l][0]], f"cast_in_{l}").reshape(1, 2, D // 2, n_in[l])],
            "kv": lambda: [_cast_bf16([w_mem_kv[l]], f"cast_kv_{l}").reshape(1, 2, 128, 2 * MEMW)],
            "wo": lambda: [_cast_bf16([w_out[l]], f"cast_wo_{l}").reshape(1, 2, 128, D)],
            "gu": lambda: [_cast_bf16([w_gate[l]], f"cast_gate_{l}").reshape(1, 2, D // 2, FS),
                           _cast_bf16([w_up[l]], f"cast_up_{l}").reshape(1, 2, D // 2, FS)],
            "wd": lambda: [_cast_bf16([w_down[l]], f"cast_wd_{l}").reshape(1, 2, FS // 2, D)]}
        dst_of = {"in": (1, NSH, 2, D // 2, n_in[l]), "kv": (1, NSH, 2, 128, 2 * MEMW),
                  "wo": (1, NSH, 2, 128, D), "gu": (1, 2, NSH, 2, D // 2, FS), "wd": (1, NSH, 2, FS // 2, D)}
        srcs, views = [], []
        for k, name in enumerate(names):
            srcs += src_of[name]()
            if name == "gu":
                views += [lambda d, k=k: d[k].at[:, 0], lambda d, k=k: d[k].at[:, 1]]
            else:
                views.append(lambda d, k=k: d[k])

        def done(w, outs):
            for k, name in enumerate(names):
                if name == "in":
                    w["b" if l else "a"] = outs[k].reshape(NSH, D, n_in[l]).transpose(1, 0, 2).reshape(
                        D, NSH * n_in[l])
                elif name == "gu":
                    w["gu"][l] = outs[k].reshape(2, NSH, D, FS).transpose(0, 2, 1, 3).reshape(2, D, FF)
                else:
                    shape = {"kv": (D, 2 * MEMW), "wo": (D, D), "wd": (FF, D)}
                    w[name][l] = outs[k].reshape(shape[name])

        return _gather_exchange(srcs, [dst_of[name] for name in names], views, small), done

    w = dict(a=None, b=None, kv=[None, None], wo=[None, None], gu=[None, None], wd=[None, None])
    first, first_done = gather_group(0, ["in", "kv"], small_src)
    outs0 = _run_exchange(first, "gather_weights_first")
    first_done(w, outs0)
    small_all = outs0[2]
    carries = {"attn_fwd": gather_group(0, ["wo", "gu", "wd"]),
               "ffn_up_0": gather_group(1, ["in", "kv", "wo", "wd"]),
               "ffn_down_0": gather_group(1, ["gu"])}

    conv_w_full = small_all[:, 1:1 + CONVW, :192].transpose(1, 0, 2).reshape(CONVW, TOK)
    vec3 = small_all[:, 32:35, :192].transpose(1, 0, 2).reshape(3, TOK)
    p = dict(
        norm1_g=norm1_g, mem_norm_g=mem_norm_g, norm2_g=norm2_g,
        a_q_g2=jnp.tile(a_q_g, (1, AH)), a_k_g2=jnp.tile(a_k_g, (1, AH)),
        mq_g4=jnp.tile(mq_g, (1, 4)), mk_g4=jnp.tile(mk_g, (1, 4)),
        rel_u=_rel_table_to_u(a_rel_bias[0]),
        b_b_in=small_all[:, 0, :448].reshape(1, NBW),
        conv_w=jnp.pad(conv_w_full, ((0, 1), (0, 0))),
        conv_b=vec3[0:1], ln_g=vec3[1:2], ln_b=vec3[2:3])

    def pair_sums(items, big):
        own, sums_b = [], []
        for l, name in items:
            if name == "in":
                g = (big["b"] if l else big["a"]).reshape(D, NSH, n_in[l]).transpose(1, 0, 2)
                g = g.reshape(1, NSH, 2, D // 2, n_in[l])
            elif name == "gu":
                g = big["gu"][l].reshape(2, NSH, 2, D // 2, FS)
            else:
                shape = {"kv": (1, NSH, 2, 128, 2 * MEMW), "wo": (1, NSH, 2, 128, D),
                         "wd": (1, NSH, 2, FS // 2, D)}
                g = big[name][l].reshape(shape[name])
            of, sb = _pair_reduce(g, f"pair_reduce_{name}_{l}")
            own.append(of)
            sums_b.append(sb)
        return own, sums_b

    early = [(1, "in"), (1, "kv"), (1, "wo"), (1, "gu"), (1, "wd"), (0, "wo"), (0, "gu"), (0, "wd")]
    late = [(0, "in"), (0, "kv")]
    own_early = []

    def bwd_carry_fn(big):
        own, sums_b = pair_sums(early, big)
        own_early.extend(own)
        return _chip_exchange(sums_b)

    loss, grad_x, big, small, parts_early = _local_step(
        x.reshape(T, D), mem.reshape(NB * MEMT, D), loss_target.reshape(T, D), w, p,
        carries=carries, bwd_carry_fn=bwd_carry_fn)
    loss = lax.psum(loss[0, 0], ("x", "y", "c"))

    own_late, sums_b_late = pair_sums(late, big)
    parts_late = _run_exchange(_chip_exchange(sums_b_late), "chip_exchange_late")
    items = early + late
    halves = [_quad_sum(o, pt, f"quad_sum_{name}_{l}")
              for (l, name), o, pt in zip(items, own_early + own_late, list(parts_early) + list(parts_late))]
    out_shapes = [(1, D, NA // NSH), (1, D, NBW // NSH), (2, 2 * 128, 2 * MEMW), (2, 2 * 128, D),
                  (2, D, FS), (2, D, FS), (2, FS, D)]
    target_of = {"in": lambda l: [(0, l, 0)], "kv": lambda l: [(0, 2, l)], "wo": lambda l: [(0, 3, l)],
                 "gu": lambda l: [(0, 4, l), (1, 5, l)], "wd": lambda l: [(0, 6, l)]}
    targets = [target_of[name](l) for l, name in items]
    g_a, g_b, g_kv, g_wo, g_gate, g_up, g_wd = _final_exchange(halves, out_shapes, targets)

    tot = _small_allreduce(_pack_small(small))
    g_rel, clip_part = _u_to_rel_table(tot[49:61])
    g_rel = jnp.concatenate([g_rel[:, :191], g_rel[:, 191:] + _rowsum(clip_part)], axis=1)
    b_in_full = jnp.concatenate([tot[15:16], tot[16:17, :512], tot[17:18, :MEMW]], axis=1)
    g_small = dict(
        norm1_g=tot[0:2], mem_norm_g=tot[2:4], norm2_g=tot[4:6],
        a_q_g=tot[6:7, :HD], a_k_g=tot[7:8, :HD], a_rel_bias=g_rel[None],
        b_b_in=lax.dynamic_slice(b_in_full, (0, sx * 448), (1, 448)),
        b_conv_w=lax.dynamic_slice(tot[18:49, :TOK], (0, sx * 192), (CONVW, 192))[None],
        b_conv_b=lax.dynamic_slice(tot[12:13, :TOK], (0, sx * 192), (1, 192)),
        b_ln_g=lax.dynamic_slice(tot[13:14, :TOK], (0, sx * 192), (1, 192)),
        b_ln_b=lax.dynamic_slice(tot[14:15, :TOK], (0, sx * 192), (1, 192)),
        mq_g=tot[8:10, :HD], mk_g=tot[10:12, :HD])

    names = ["norm1_g", "mem_norm_g", "a_w_in", "a_q_g", "a_k_g", "a_rel_bias", "b_w_in", "b_b_in",
             "b_conv_w", "b_conv_b", "b_ln_g", "b_ln_b", "mq_g", "mk_g", "w_mem_kv", "w_out",
             "norm2_g", "w_gate", "w_up", "w_down"]
    weights = dict(zip(names, [norm1_g, mem_norm_g, a_w_in, a_q_g, a_k_g, a_rel_bias, b_w_in, b_b_in,
                               b_conv_w, b_conv_b, b_ln_g, b_ln_b, mq_g, mk_g, w_mem_kv, w_out,
                               norm2_g, w_gate, w_up, w_down]))
    ms = dict(zip(names, [m_norm1_g, m_mem_norm_g, m_a_w_in, m_a_q_g, m_a_k_g, m_a_rel_bias, m_b_w_in,
                          m_b_b_in, m_b_conv_w, m_b_conv_b, m_b_ln_g, m_b_ln_b, m_mq_g, m_mk_g,
                          m_w_mem_kv, m_w_out, m_norm2_g, m_w_gate, m_w_up, m_w_down]))
    vs = dict(zip(names, [v_norm1_g, v_mem_norm_g, v_a_w_in, v_a_q_g, v_a_k_g, v_a_rel_bias, v_b_w_in,
                          v_b_b_in, v_b_conv_w, v_b_conv_b, v_b_ln_g, v_b_ln_b, v_mq_g, v_mk_g,
                          v_w_mem_kv, v_w_out, v_norm2_g, v_w_gate, v_w_up, v_w_down]))
    grads = dict(g_small)
    grads.update(a_w_in=g_a, b_w_in=g_b, w_mem_kv=g_kv, w_out=g_wo, w_gate=g_gate, w_up=g_up, w_down=g_wd)
    big_names = ["a_w_in", "b_w_in", "w_mem_kv", "w_out", "w_gate", "w_up", "w_down"]
    small_names = [n for n in names if n not in big_names]
    delta, new_m, new_v = {}, {}, {}
    for n in big_names:
        delta[n], new_m[n], new_v[n] = _adamw_big(weights[n], grads[n], ms[n], vs[n], f"adamw_{n}")
    as2d = lambda a: a.reshape(-1, a.shape[-1])
    d_s, m_s, v_s = _adamw_small([as2d(weights[n]) for n in small_names], [as2d(grads[n]) for n in small_names],
                                 [as2d(ms[n]) for n in small_names], [as2d(vs[n]) for n in small_names])
    for i, n in enumerate(small_names):
        delta[n] = d_s[i].reshape(weights[n].shape)
        new_m[n] = m_s[i].reshape(weights[n].shape)
        new_v[n] = v_s[i].reshape(weights[n].shape)

    return (loss, grad_x.reshape(NB, SEQ, D), *[grads[n] for n in names], *[delta[n] for n in names],
            *[new_m[n] for n in names], *[new_v[n] for n in names])


def _rowsum(a):
    def body(a_ref, o_ref):
        o_ref[...] = jnp.sum(a_ref[...], axis=1, keepdims=True)

    vm = BS(memory_space=pltpu.VMEM)
    return pl.pallas_call(body, in_specs=[vm], out_specs=vm, out_shape=_sds((a.shape[0], 1), F32),
                          compiler_params=_cp(), name="rowsum")(a)
```

```python
import functools

import jax
import jax.numpy as jnp
from jax import lax
from jax.experimental import pallas as pl
from jax.experimental.pallas import tpu as pltpu

F32 = jnp.float32
BF16 = jnp.bfloat16
BS = pl.BlockSpec
ANY = pl.BlockSpec(memory_space=pl.ANY)
MESH = pl.DeviceIdType.MESH

D = 1024
SEQ = 2048
NB = 2
T = NB * SEQ
MEMT = 256
HD = 64
TOK = 768
MEMW = 256
NA = 3 * TOK + MEMW
NBW = 2 * TOK + MEMW
FF = 2816
NSH = 4
FS = FF // NSH
FT = FF // 2
CONVW = 31
EPS = 1e-6
NEG = -1e30
SCALE = HD ** -0.5
QB = 256
KWIN = 768
KPAD = 512
TR = 512

ADAM_LR = 0.001
ADAM_B1 = 0.9
ADAM_B2 = 0.999
ADAM_EPS = 1e-08
ADAM_WD = 0.01
ADAM_STEP = 10

_DIMS = {
    "nn": (((1,), (0,)), ((), ())),
    "nt": (((1,), (1,)), ((), ())),
    "tn": (((0,), (0,)), ((), ())),
}


def _cp(sem=None, vmem_mb=48):
    return pltpu.CompilerParams(dimension_semantics=sem, vmem_limit_bytes=vmem_mb << 20)


def _sds(shape, dtype):
    return jax.ShapeDtypeStruct(tuple(shape), dtype)


def _mm(mode, a, b, *, grid, a_spec, b_spec, out_shape, out_spec, acc_shape, name,
        extras=(), extra_specs=(), epilogue=None, carry=None, vmem_mb=48):
    n_ex = len(extras)
    nk = grid[2]
    dims = _DIMS[mode]
    ni = len(carry.ins) if carry else 0
    no = len(carry.out_shapes) if carry else 0

    def body(a_ref, b_ref, *rest):
        ex = rest[:n_ex]
        cin = rest[n_ex:n_ex + ni]
        o_ref = rest[n_ex + ni]
        cout = rest[n_ex + ni + 1:n_ex + ni + 1 + no]
        acc = rest[n_ex + ni + 1 + no]
        cscr = rest[n_ex + ni + 2 + no:]
        k = pl.program_id(2)
        if carry:
            ids = [pl.program_id(d) for d in range(3)]

            @pl.when((ids[0] == 0) & (ids[1] == 0) & (ids[2] == 0))
            def _():
                carry.start(cin, cout, cscr)

        prod = lax.dot_general(a_ref[...].astype(BF16), b_ref[...].astype(BF16), dims,
                               preferred_element_type=F32)

        def finish(val):
            if epilogue is None:
                o_ref[...] = val.astype(o_ref.dtype)
            else:
                epilogue(val, ex, o_ref)

        if nk == 1:
            finish(prod)
        else:
            @pl.when(k == 0)
            def _():
                acc[...] = prod

            @pl.when((k > 0) & (k < nk - 1))
            def _():
                acc[...] += prod

            @pl.when(k == nk - 1)
            def _():
                finish(acc[...] + prod)

        if carry:
            @pl.when((ids[0] == grid[0] - 1) & (ids[1] == grid[1] - 1) & (ids[2] == grid[2] - 1))
            def _():
                carry.finish(cin, cout, cscr)

    acc_scratch = pltpu.VMEM(acc_shape if nk > 1 else (8, 128), F32)
    if not carry:
        return pl.pallas_call(
            body, grid=grid, in_specs=[a_spec, b_spec, *extra_specs], out_specs=out_spec,
            out_shape=out_shape, scratch_shapes=[acc_scratch],
            compiler_params=_cp(("parallel", "parallel", "arbitrary"), vmem_mb), name=name,
        )(a, b, *extras)
    outs = pl.pallas_call(
        body, grid=grid, in_specs=[a_spec, b_spec, *extra_specs, *carry.in_specs],
        out_specs=[out_spec] + [ANY] * no, out_shape=[out_shape] + carry.out_shapes,
        scratch_shapes=[acc_scratch] + carry.scratch,
        compiler_params=pltpu.CompilerParams(
            dimension_semantics=("arbitrary", "arbitrary", "arbitrary"),
            vmem_limit_bytes=vmem_mb << 20, has_side_effects=True), name=name,
    )(a, b, *extras, *carry.ins)
    return outs[0], outs[1:]


def _rms_fwd(x, g, name):
    rows = x.shape[0]

    def body(x_ref, g_ref, o_ref):
        xv = x_ref[...]
        r = lax.rsqrt(jnp.mean(xv * xv, axis=-1, keepdims=True) + EPS)
        o_ref[...] = (xv * r * g_ref[...]).astype(BF16)

    return pl.pallas_call(
        body, grid=(rows // TR,),
        in_specs=[BS((TR, D), lambda i: (i, 0)), BS((1, D), lambda i: (0, 0))],
        out_specs=BS((TR, D), lambda i: (i, 0)), out_shape=_sds((rows, D), BF16),
        compiler_params=_cp(("arbitrary",)), name=name,
    )(x, g)


def _rms_bwd(dh, x, g, dres, name):
    rows = x.shape[0]
    has_res = dres is not None

    def body(*refs):
        if has_res:
            dh_ref, x_ref, g_ref, r_ref, dx_ref, dg_ref = refs
        else:
            dh_ref, x_ref, g_ref, dx_ref, dg_ref = refs
        xv = x_ref[...]
        dhv = dh_ref[...]
        r = lax.rsqrt(jnp.mean(xv * xv, axis=-1, keepdims=True) + EPS)
        xh = xv * r
        gy = dhv * g_ref[...]
        dx = r * (gy - xh * jnp.mean(gy * xh, axis=-1, keepdims=True))
        if has_res:
            dx = dx + r_ref[...]
        dx_ref[...] = dx

        @pl.when(pl.program_id(0) == 0)
        def _():
            dg_ref[...] = jnp.zeros_like(dg_ref)

        dg_ref[...] += jnp.sum(dhv * xh, axis=0, keepdims=True)

    row = BS((TR, D), lambda i: (i, 0))
    vec = BS((1, D), lambda i: (0, 0))
    ins = [dh, x, g] + ([dres] if has_res else [])
    return pl.pallas_call(
        body, grid=(rows // TR,),
        in_specs=[row, row, vec] + ([row] if has_res else []),
        out_specs=[row, vec], out_shape=[_sds((rows, D), F32), _sds((1, D), F32)],
        compiler_params=_cp(("arbitrary",)), name=name,
    )(*ins)


def _group_masks(width):
    lane = lax.broadcasted_iota(jnp.int32, (1, width), 1)
    return [(lane >= HD * h) & (lane < HD * (h + 1)) for h in range(width // HD)]


def _group_mean(v, masks):
    out = jnp.zeros_like(v)
    for m in masks:
        s = jnp.sum(jnp.where(m, v, 0.0), axis=-1, keepdims=True) * (1.0 / HD)
        out = jnp.where(m, s, out)
    return out


def _head_norm(zv, g, masks):
    r = lax.rsqrt(_group_mean(zv * zv, masks) + EPS)
    return zv * r * g


def _head_norm_bwd(dy, zv, g, masks):
    r = lax.rsqrt(_group_mean(zv * zv, masks) + EPS)
    zh = zv * r
    gy = dy * g
    dz = r * (gy - zh * _group_mean(gy * zh, masks))
    return dz, jnp.sum(dy * zh, axis=0, keepdims=True)


def _fold_heads(v, width):
    vb = jnp.broadcast_to(v, (8, width))
    out = vb
    for h in range(1, width // HD):
        out = out + pltpu.roll(vb, width - HD * h, axis=1)
    return out[0:1]


def _bias_expand(u):
    def body(u_ref, o_ref):
        x = jnp.broadcast_to(u_ref[...], (QB, 1024))
        rolled = pltpu.roll(x, 1024 - (QB - 1), axis=1, stride=1, stride_axis=0)[:, :KWIN]
        row = lax.broadcasted_iota(jnp.int32, (QB, 1), 0)
        col = lax.broadcasted_iota(jnp.int32, (1, KWIN), 1)
        lo = (row // 64) * 64
        ok = (col >= lo) & (col < lo + 576)
        o_ref[...] = jnp.where(ok, rolled, NEG)

    return pl.pallas_call(
        body, grid=(12,), in_specs=[BS((None, 1, 1024), lambda h: (h, 0, 0))],
        out_specs=BS((None, QB, KWIN), lambda h: (h, 0, 0)), out_shape=_sds((12, QB, KWIN), F32),
        compiler_params=_cp(("arbitrary",)), name="bias_expand",
    )(u)


def _bias_reduce(ds):
    def body(d_ref, o_ref):
        ri = lax.broadcasted_iota(jnp.int32, (QB, QB), 0)
        ci = lax.broadcasted_iota(jnp.int32, (QB, QB), 1)
        flip = (ri + ci == QB - 1).astype(F32)
        drev = jnp.dot(flip, d_ref[...], precision=lax.Precision.HIGHEST, preferred_element_type=F32)
        x = jnp.concatenate([drev, jnp.zeros((QB, 1024 - KWIN), F32)], axis=1)
        rolled = pltpu.roll(x, 0, axis=1, stride=1, stride_axis=0)
        o_ref[...] = jnp.sum(rolled, axis=0, keepdims=True)

    return pl.pallas_call(
        body, grid=(12,), in_specs=[BS((None, QB, KWIN), lambda h: (h, 0, 0))],
        out_specs=BS((None, 1, 1024), lambda h: (h, 0, 0)), out_shape=_sds((12, 1, 1024), F32),
        compiler_params=_cp(("arbitrary",)), name="bias_reduce",
    )(ds)


AW = 256
AH = AW // HD
AG = TOK // AW
def _attn_softmax(qh, kw, bias, startadd):
    s = lax.dot_general(qh, kw, _DIMS["nt"], preferred_element_type=F32) + bias + startadd
    m = jnp.max(s, axis=-1, keepdims=True)
    p = jnp.exp(s - m)
    return p * (1.0 / jnp.sum(p, axis=-1, keepdims=True))


def _attn_prologue(q_ref, k_ref, v_ref, gq_ref, gk_ref, qn_s, kn_s, v_s, masks):
    kn_s[0:KPAD, :] = jnp.zeros((KPAD, AW), BF16)
    v_s[0:KPAD, :] = jnp.zeros((KPAD, AW), BF16)
    for r in range(0, SEQ, TR):
        qn_s[r:r + TR, :] = (_head_norm(q_ref[r:r + TR, :], gq_ref[...], masks) * SCALE).astype(BF16)
        kn_s[KPAD + r:KPAD + r + TR, :] = _head_norm(k_ref[r:r + TR, :], gk_ref[...], masks).astype(BF16)
        v_s[KPAD + r:KPAD + r + TR, :] = v_ref[r:r + TR, :].astype(BF16)


def _attn_fwd(z, bias, gq2, gk2, carry=None):
    ni = len(carry.ins) if carry else 0
    no = len(carry.out_shapes) if carry else 0

    def body(q_ref, k_ref, v_ref, b_ref, gq_ref, gk_ref, *rest):
        cin, o_ref, cout = rest[:ni], rest[ni], rest[ni + 1:ni + 1 + no]
        qn_s, kn_s, v_s = rest[ni + 1 + no:ni + 4 + no]
        cscr = rest[ni + 4 + no:]
        if carry:
            @pl.when((pl.program_id(0) == 0) & (pl.program_id(1) == 0))
            def _():
                carry.start(cin, cout, cscr)

        masks = _group_masks(AW)
        _attn_prologue(q_ref, k_ref, v_ref, gq_ref, gk_ref, qn_s, kn_s, v_s, masks)
        col = lax.broadcasted_iota(jnp.int32, (1, KWIN), 1)

        def blk(i, carry):
            r0 = pl.multiple_of(i * QB, QB)
            qb = qn_s[pl.ds(r0, QB), :]
            kw = kn_s[pl.ds(r0, KWIN), :]
            vw = v_s[pl.ds(r0, KWIN), :]
            startadd = jnp.where(col + r0 < KPAD, NEG, 0.0)
            o = jnp.zeros((QB, AW), F32)
            for h in range(AH):
                qh = jnp.where(masks[h], qb, jnp.zeros_like(qb))
                vh = jnp.where(masks[h], vw, jnp.zeros_like(vw))
                p = _attn_softmax(qh, kw, b_ref[h], startadd).astype(BF16)
                o = o + jnp.dot(p, vh, preferred_element_type=F32)
            o_ref[pl.ds(r0, QB), :] = o.astype(BF16)
            return carry

        lax.fori_loop(0, SEQ // QB, blk, 0)

        if carry:
            @pl.when((pl.program_id(0) == NB - 1) & (pl.program_id(1) == AG - 1))
            def _():
                carry.finish(cin, cout, cscr)

    vec = BS((1, AW), lambda b, hp: (0, 0))
    outs = pl.pallas_call(
        body, grid=(NB, AG),
        in_specs=[BS((SEQ, AW), lambda b, hp: (b, hp)),
                  BS((SEQ, AW), lambda b, hp: (b, AG + hp)),
                  BS((SEQ, AW), lambda b, hp: (b, 2 * AG + hp)),
                  BS((AH, QB, KWIN), lambda b, hp: (hp, 0, 0)), vec, vec]
        + (carry.in_specs if carry else []),
        out_specs=[BS((SEQ, AW), lambda b, hp: (b, hp))] + [ANY] * no,
        out_shape=[_sds((T, D), BF16)] + (carry.out_shapes if carry else []),
        scratch_shapes=[pltpu.VMEM((SEQ, AW), BF16), pltpu.VMEM((SEQ + KPAD, AW), BF16),
                        pltpu.VMEM((SEQ + KPAD, AW), BF16)] + (carry.scratch if carry else []),
        compiler_params=pltpu.CompilerParams(
            dimension_semantics=("arbitrary", "arbitrary"), vmem_limit_bytes=48 << 20,
            has_side_effects=bool(carry)), name="attn_fwd",
    )(z, z, z, bias, gq2, gk2, *(carry.ins if carry else []))
    return outs[0], outs[1:]


def _attn_bwd(z, dcat, bias, gq2, gk2, carry=None):
    ni = len(carry.ins) if carry else 0
    no = len(carry.out_shapes) if carry else 0

    def body(q_ref, k_ref, v_ref, do_ref, b_ref, gq_ref, gk_ref, *rest):
        cin = rest[:ni]
        dq_ref, dk_ref, dv_ref, db_ref, dgq_ref, dgk_ref = rest[ni:ni + 6]
        cout = rest[ni + 6:ni + 6 + no]
        qn_s, kn_s, v_s, dqn_s, dkn_s, dv_s = rest[ni + 6 + no:ni + 12 + no]
        cscr = rest[ni + 12 + no:]
        hp = pl.program_id(0)
        b = pl.program_id(1)
        if carry:
            @pl.when((hp == 0) & (b == 0))
            def _():
                carry.start(cin, cout, cscr)

        masks = _group_masks(AW)
        _attn_prologue(q_ref, k_ref, v_ref, gq_ref, gk_ref, qn_s, kn_s, v_s, masks)
        dkn_s[...] = jnp.zeros_like(dkn_s)
        dv_s[...] = jnp.zeros_like(dv_s)

        @pl.when(b == 0)
        def _():
            db_ref[...] = jnp.zeros_like(db_ref)

        @pl.when((b == 0) & (hp == 0))
        def _():
            dgq_ref[...] = jnp.zeros_like(dgq_ref)
            dgk_ref[...] = jnp.zeros_like(dgk_ref)

        col = lax.broadcasted_iota(jnp.int32, (1, KWIN), 1)

        def blk(i, carry):
            r0 = pl.multiple_of(i * QB, QB)
            qb = qn_s[pl.ds(r0, QB), :]
            kw = kn_s[pl.ds(r0, KWIN), :]
            vw = v_s[pl.ds(r0, KWIN), :]
            dob = do_ref[pl.ds(r0, QB), :].astype(BF16)
            startadd = jnp.where(col + r0 < KPAD, NEG, 0.0)
            dqn = jnp.zeros((QB, AW), F32)
            dkw = jnp.zeros((KWIN, AW), F32)
            dvw = jnp.zeros((KWIN, AW), F32)
            for h in range(AH):
                qh = jnp.where(masks[h], qb, jnp.zeros_like(qb))
                kh = jnp.where(masks[h], kw, jnp.zeros_like(kw))
                doh = jnp.where(masks[h], dob, jnp.zeros_like(dob))
                p = _attn_softmax(qh, kw, b_ref[h], startadd)
                dvw = dvw + lax.dot_general(p.astype(BF16), doh, _DIMS["tn"],
                                            preferred_element_type=F32)
                dp = lax.dot_general(doh, vw, _DIMS["nt"], preferred_element_type=F32)
                ds = p * (dp - jnp.sum(dp * p, axis=-1, keepdims=True))
                db_ref[h] += ds
                dsb = ds.astype(BF16)
                dqn = dqn + jnp.dot(dsb, kh, preferred_element_type=F32)
                dkw = dkw + lax.dot_general(dsb, qh, _DIMS["tn"], preferred_element_type=F32)
            dqn_s[pl.ds(r0, QB), :] = dqn * SCALE
            dkn_s[pl.ds(r0, KWIN), :] += dkw
            dv_s[pl.ds(r0, KWIN), :] += dvw
            return carry

        lax.fori_loop(0, SEQ // QB, blk, 0)

        dgq = jnp.zeros((1, AW), F32)
        dgk = jnp.zeros((1, AW), F32)
        for r in range(0, SEQ, TR):
            dq, dg = _head_norm_bwd(dqn_s[r:r + TR, :], q_ref[r:r + TR, :], gq_ref[...], masks)
            dq_ref[r:r + TR, :] = dq.astype(BF16)
            dgq = dgq + dg
            dk, dg = _head_norm_bwd(dkn_s[KPAD + r:KPAD + r + TR, :], k_ref[r:r + TR, :], gk_ref[...], masks)
            dk_ref[r:r + TR, :] = dk.astype(BF16)
            dgk = dgk + dg
            dv_ref[r:r + TR, :] = dv_s[KPAD + r:KPAD + r + TR, :].astype(BF16)
        dgq_ref[...] += _fold_heads(dgq, AW)
        dgk_ref[...] += _fold_heads(dgk, AW)

        if carry:
            @pl.when((hp == AG - 1) & (b == NB - 1))
            def _():
                carry.finish(cin, cout, cscr)

    vec = BS((1, AW), lambda hp, b: (0, 0))
    row = BS((SEQ, AW), lambda hp, b: (b, hp))
    outs = pl.pallas_call(
        body, grid=(AG, NB),
        in_specs=[row,
                  BS((SEQ, AW), lambda hp, b: (b, AG + hp)),
                  BS((SEQ, AW), lambda hp, b: (b, 2 * AG + hp)),
                  row,
                  BS((AH, QB, KWIN), lambda hp, b: (hp, 0, 0)), vec, vec]
        + (carry.in_specs if carry else []),
        out_specs=[row, row, row, BS((AH, QB, KWIN), lambda hp, b: (hp, 0, 0)), vec, vec] + [ANY] * no,
        out_shape=[_sds((T, TOK), BF16), _sds((T, TOK), BF16), _sds((T, TOK), BF16),
                   _sds((12, QB, KWIN), F32), _sds((1, AW), F32), _sds((1, AW), F32)]
        + (carry.out_shapes if carry else []),
        scratch_shapes=[pltpu.VMEM((SEQ, AW), BF16), pltpu.VMEM((SEQ + KPAD, AW), BF16),
                        pltpu.VMEM((SEQ + KPAD, AW), BF16), pltpu.VMEM((SEQ, AW), F32),
                        pltpu.VMEM((SEQ + KPAD, AW), F32), pltpu.VMEM((SEQ + KPAD, AW), F32)]
        + (carry.scratch if carry else []),
        compiler_params=pltpu.CompilerParams(
            dimension_semantics=("arbitrary", "arbitrary"), vmem_limit_bytes=58 << 20,
            has_side_effects=bool(carry)), name="attn_bwd",
    )(z, z, z, dcat, bias, gq2, gk2, *(carry.ins if carry else []))
    return outs[:6], outs[6:]


def _mem_softmax(qh, kn):
    s = lax.dot_general(qh, kn, _DIMS["nt"], preferred_element_type=F32)
    m = jnp.max(s, axis=-1, keepdims=True)
    p = jnp.exp(s - m)
    return p * (1.0 / jnp.sum(p, axis=-1, keepdims=True))


def _memattn_fwd(z, kv, cat, gq4, gk4, qcol, name):
    def body(q_ref, k_ref, v_ref, gq_ref, gk_ref, cat_ref, o_ref):
        del cat_ref
        masks = _group_masks(MEMW)
        qn = (_head_norm(q_ref[...], gq_ref[...], masks) * SCALE).astype(BF16)
        kn = _head_norm(k_ref[...], gk_ref[...], masks).astype(BF16)
        vv = v_ref[...].astype(BF16)
        o = jnp.zeros((TR, MEMW), F32)
        for h in range(4):
            qh = jnp.where(masks[h], qn, jnp.zeros_like(qn))
            vh = jnp.where(masks[h], vv, jnp.zeros_like(vv))
            p = _mem_softmax(qh, kn).astype(BF16)
            o = o + jnp.dot(p, vh, preferred_element_type=F32)
        o_ref[...] = o.astype(BF16)

    nt = SEQ // TR
    vec = BS((1, MEMW), lambda b, t: (0, 0))
    return pl.pallas_call(
        body, grid=(NB, nt),
        in_specs=[BS((TR, MEMW), lambda b, t: (b * nt + t, qcol)),
                  BS((MEMT, MEMW), lambda b, t: (b, 0)),
                  BS((MEMT, MEMW), lambda b, t: (b, 1)), vec, vec, ANY],
        out_specs=BS((TR, MEMW), lambda b, t: (b * nt + t, 3)),
        out_shape=_sds((T, D), BF16), input_output_aliases={5: 0},
        compiler_params=_cp(("arbitrary", "arbitrary")), name=name,
    )(z, kv, kv, gq4, gk4, cat)


def _memattn_bwd(z, kv, dcat, gq4, gk4, qcol, name):
    nt = SEQ // TR

    def body(q_ref, k_ref, v_ref, do_ref, gq_ref, gk_ref,
             dq_ref, dkv_ref, dgq_ref, dgk_ref, dkn_s, dv_s):
        b = pl.program_id(0)
        t = pl.program_id(1)
        masks = _group_masks(MEMW)
        qz = q_ref[...]
        kz = k_ref[...]
        qn = (_head_norm(qz, gq_ref[...], masks) * SCALE).astype(BF16)
        kn = _head_norm(kz, gk_ref[...], masks).astype(BF16)
        vv = v_ref[...].astype(BF16)
        dob = do_ref[...].astype(BF16)

        @pl.when(t == 0)
        def _():
            dkn_s[...] = jnp.zeros_like(dkn_s)
            dv_s[...] = jnp.zeros_like(dv_s)

        @pl.when((t == 0) & (b == 0))
        def _():
            dgq_ref[...] = jnp.zeros_like(dgq_ref)
            dgk_ref[...] = jnp.zeros_like(dgk_ref)

        dqn = jnp.zeros((TR, MEMW), F32)
        dkn = jnp.zeros((MEMT, MEMW), F32)
        dvv = jnp.zeros((MEMT, MEMW), F32)
        for h in range(4):
            qh = jnp.where(masks[h], qn, jnp.zeros_like(qn))
            kh = jnp.where(masks[h], kn, jnp.zeros_like(kn))
            doh = jnp.where(masks[h], dob, jnp.zeros_like(dob))
            p = _mem_softmax(qh, kn)
            dvv = dvv + lax.dot_general(p.astype(BF16), doh, _DIMS["tn"], preferred_element_type=F32)
            dp = lax.dot_general(doh, vv, _DIMS["nt"], preferred_element_type=F32)
            ds = p * (dp - jnp.sum(dp * p, axis=-1, keepdims=True))
            dsb = ds.astype(BF16)
            dqn = dqn + jnp.dot(dsb, kh, preferred_element_type=F32)
            dkn = dkn + lax.dot_general(dsb, qh, _DIMS["tn"], preferred_element_type=F32)
        dkn_s[...] += dkn
        dv_s[...] += dvv
        dq, dgq = _head_norm_bwd(dqn * SCALE, qz, gq_ref[...], masks)
        dq_ref[...] = dq.astype(BF16)
        dgq_ref[...] += _fold_heads(dgq, MEMW)

        @pl.when(t == nt - 1)
        def _():
            dk, dgk = _head_norm_bwd(dkn_s[...], kz, gk_ref[...], masks)
            dkv_ref[:, 0:MEMW] = dk
            dkv_ref[:, MEMW:] = dv_s[...]
            dgk_ref[...] += _fold_heads(dgk, MEMW)

    vec = BS((1, MEMW), lambda b, t: (0, 0))
    return pl.pallas_call(
        body, grid=(NB, nt),
        in_specs=[BS((TR, MEMW), lambda b, t: (b * nt + t, qcol)),
                  BS((MEMT, MEMW), lambda b, t: (b, 0)),
                  BS((MEMT, MEMW), lambda b, t: (b, 1)),
                  BS((TR, MEMW), lambda b, t: (b * nt + t, 3)), vec, vec],
        out_specs=[BS((TR, MEMW), lambda b, t: (b * nt + t, 0)),
                   BS((MEMT, 2 * MEMW), lambda b, t: (b, 0)), vec, vec],
        out_shape=[_sds((T, MEMW), BF16), _sds((NB * MEMT, 2 * MEMW), F32),
                   _sds((1, MEMW), F32), _sds((1, MEMW), F32)],
        scratch_shapes=[pltpu.VMEM((MEMT, MEMW), F32), pltpu.VMEM((MEMT, MEMW), F32)],
        compiler_params=_cp(("arbitrary", "arbitrary")), name=name,
    )(z, kv, kv, dcat, gq4, gk4)


HALO = 32
NEXT = 64
RT = 64


def _glu(zz):
    return zz[:, :TOK] * jax.nn.sigmoid(zz[:, TOK:])


def _layer_norm_parts(y):
    mu = jnp.mean(y, axis=-1, keepdims=True)
    yc = y - mu
    rstd = lax.rsqrt(jnp.mean(yc * yc, axis=-1, keepdims=True) + EPS)
    return yc * rstd, rstd


def _shifted_copies(src, dst, rows):
    for b in range(1, 8):
        dst[b - 1, 0:rows, :] = src[b:b + rows, :]


def _tap(src, shifted, off, r0, rows):
    b = off % 8
    if b == 0:
        return src[r0 + off:r0 + off + rows, :]
    return shifted[b - 1, r0 + off - b:r0 + off - b + rows, :]


def _conv_rows(w_ref, hbuf, hs, r0, rows):
    y = jnp.zeros((rows, TOK), F32)
    for j in range(CONVW):
        y = y + w_ref[j:j + 1, :] * _tap(hbuf, hs, (HALO - CONVW + 1) + j, r0, rows)
    return y


def _conv_fwd(z, cw, cb, lg, lb):
    nt = SEQ // TR

    def body(zc_ref, zp_ref, w_ref, cb_ref, lg_ref, lb_ref, o_ref, hbuf, hs):
        t = pl.program_id(1)
        hbuf[0:HALO, :] = jnp.where(t == 0, 0.0, _glu(zp_ref[...]))
        hbuf[HALO:, :] = _glu(zc_ref[...])
        _shifted_copies(hbuf, hs, HALO + TR - 8)
        for r0 in range(0, TR, RT):
            y = _conv_rows(w_ref, hbuf, hs, r0, RT) + cb_ref[...]
            yh, _ = _layer_norm_parts(y)
            o = yh * lg_ref[...] + lb_ref[...]
            o_ref[r0:r0 + RT, :] = (o * jax.nn.sigmoid(o)).astype(BF16)

    vec = BS((1, TOK), lambda b, t: (0, 0))
    per = TR // HALO
    return pl.pallas_call(
        body, grid=(NB, nt),
        in_specs=[BS((TR, 2 * TOK), lambda b, t: (b * nt + t, 0)),
                  BS((HALO, 2 * TOK), lambda b, t: (jnp.maximum((b * nt + t) * per - 1, 0), 0)),
                  BS((32, TOK), lambda b, t: (0, 0)), vec, vec, vec],
        out_specs=BS((TR, TOK), lambda b, t: (b * nt + t, 0)),
        out_shape=_sds((T, D), BF16),
        scratch_shapes=[pltpu.VMEM((HALO + TR, TOK), F32), pltpu.VMEM((7, HALO + TR, TOK), F32)],
        compiler_params=_cp(("arbitrary", "arbitrary")), name="conv_fwd",
    )(z, z, cw, cb, lg, lb)


def _conv_bwd(z, dcat, cw, cb, lg, lb):
    nt = SEQ // TR
    ext = TR + NEXT

    def body(zc_ref, zp_ref, zn_ref, dc_ref, dn_ref, w_ref, cb_ref, lg_ref, lb_ref,
             du_ref, dw_ref, dcb_ref, dlg_ref, dlb_ref, dbin_ref, hbuf, dybuf, hs, dys):
        b = pl.program_id(0)
        t = pl.program_id(1)

        @pl.when((b == 0) & (t == 0))
        def _():
            dw_ref[...] = jnp.zeros_like(dw_ref)
            dcb_ref[...] = jnp.zeros_like(dcb_ref)
            dlg_ref[...] = jnp.zeros_like(dlg_ref)
            dlb_ref[...] = jnp.zeros_like(dlb_ref)
            dbin_ref[...] = jnp.zeros_like(dbin_ref)

        hbuf[0:HALO, :] = jnp.where(t == 0, 0.0, _glu(zp_ref[...]))
        hbuf[HALO:HALO + TR, :] = _glu(zc_ref[...])
        hbuf[HALO + TR:, :] = _glu(zn_ref[...])
        _shifted_copies(hbuf, hs, HALO + TR + NEXT - 8)
        last = t == nt - 1
        for r0 in range(0, ext, RT):
            y = _conv_rows(w_ref, hbuf, hs, r0, RT) + cb_ref[...]
            yh, rstd = _layer_norm_parts(y)
            o = yh * lg_ref[...] + lb_ref[...]
            sg = jax.nn.sigmoid(o)
            if r0 < TR:
                dtok = dc_ref[r0:r0 + RT, :]
            else:
                dtok = jnp.where(last, 0.0, dn_ref[...])
            do = dtok * (sg * (1.0 + o * (1.0 - sg)))
            dyh = do * lg_ref[...]
            dy = rstd * (dyh - jnp.mean(dyh, axis=-1, keepdims=True)
                         - yh * jnp.mean(dyh * yh, axis=-1, keepdims=True))
            dybuf[r0:r0 + RT, :] = dy
            if r0 < TR:
                dlg_ref[...] += jnp.sum(do * yh, axis=0, keepdims=True)
                dlb_ref[...] += jnp.sum(do, axis=0, keepdims=True)
                dcb_ref[...] += jnp.sum(dy, axis=0, keepdims=True)
        _shifted_copies(dybuf, dys, ext - 8)
        for r0 in range(0, TR, RT):
            dh = jnp.zeros((RT, TOK), F32)
            for j in range(CONVW):
                dh = dh + w_ref[j:j + 1, :] * _tap(dybuf, dys, (CONVW - 1) - j, r0, RT)
            a = zc_ref[r0:r0 + RT, 0:TOK]
            sg = jax.nn.sigmoid(zc_ref[r0:r0 + RT, TOK:])
            da = dh * sg
            dg = dh * a * (sg * (1.0 - sg))
            du_ref[r0:r0 + RT, 0:TOK] = da.astype(BF16)
            du_ref[r0:r0 + RT, TOK:] = dg.astype(BF16)
            dbin_ref[:, 0:TOK] += jnp.sum(da, axis=0, keepdims=True)
            dbin_ref[:, TOK:] += jnp.sum(dg, axis=0, keepdims=True)
        for j in range(CONVW):
            acc = jnp.zeros((8, TOK), F32)
            for r0 in range(0, TR, RT):
                prod = dybuf[r0:r0 + RT, :] * _tap(hbuf, hs, (HALO - CONVW + 1) + j, r0, RT)
                acc = acc + jnp.sum(prod.reshape(RT // 8, 8, TOK), axis=0)
            dw_ref[j:j + 1, :] += jnp.sum(acc, axis=0, keepdims=True)

    vec = BS((1, TOK), lambda b, t: (0, 0))
    perh = TR // HALO
    pern = TR // NEXT
    nlast_n = T // NEXT - 1
    return pl.pallas_call(
        body, grid=(NB, nt),
        in_specs=[BS((TR, 2 * TOK), lambda b, t: (b * nt + t, 0)),
                  BS((HALO, 2 * TOK), lambda b, t: (jnp.maximum((b * nt + t) * perh - 1, 0), 0)),
                  BS((NEXT, 2 * TOK), lambda b, t: (jnp.minimum((b * nt + t + 1) * pern, nlast_n), 0)),
                  BS((TR, TOK), lambda b, t: (b * nt + t, 0)),
                  BS((NEXT, TOK), lambda b, t: (jnp.minimum((b * nt + t + 1) * pern, nlast_n), 0)),
                  BS((32, TOK), lambda b, t: (0, 0)), vec, vec, vec],
        out_specs=[BS((TR, 2 * TOK), lambda b, t: (b * nt + t, 0)),
                   BS((32, TOK), lambda b, t: (0, 0)), vec, vec, vec,
                   BS((1, 2 * TOK), lambda b, t: (0, 0))],
        out_shape=[_sds((T, 2 * TOK), BF16), _sds((32, TOK), F32), _sds((1, TOK), F32),
                   _sds((1, TOK), F32), _sds((1, TOK), F32), _sds((1, 2 * TOK), F32)],
        scratch_shapes=[pltpu.VMEM((HALO + TR + NEXT, TOK), F32), pltpu.VMEM((ext, TOK), F32),
                        pltpu.VMEM((7, HALO + TR + NEXT, TOK), F32), pltpu.VMEM((7, ext, TOK), F32)],
        compiler_params=_cp(("arbitrary", "arbitrary"), vmem_mb=56), name="conv_bwd",
    )(z, z, z, dcat, dcat, cw, cb, lg, lb)


def _ffn_up(h2, wgu, l, carry=None):
    ni = len(carry.ins) if carry else 0
    no = len(carry.out_shapes) if carry else 0

    def body(h_ref, wg_ref, wu_ref, *rest):
        cin = rest[:ni]
        g_ref, u_ref, a_ref = rest[ni:ni + 3]
        cout, cscr = rest[ni + 3:ni + 3 + no], rest[ni + 3 + no:]
        if carry:
            @pl.when((pl.program_id(0) == 0) & (pl.program_id(1) == 0))
            def _():
                carry.start(cin, cout, cscr)

        hv = h_ref[...]
        g = lax.dot_general(hv, wg_ref[...], _DIMS["nt"], preferred_element_type=F32)
        u = lax.dot_general(hv, wu_ref[...], _DIMS["nt"], preferred_element_type=F32)
        g_ref[...] = g
        u_ref[...] = u
        a_ref[...] = (g * jax.nn.sigmoid(g) * u).astype(BF16)

        if carry:
            @pl.when((pl.program_id(0) == FF // FT - 1) & (pl.program_id(1) == T // TR - 1))
            def _():
                carry.finish(cin, cout, cscr)

    out = BS((TR, FT), lambda q, i: (i, q))
    outs = pl.pallas_call(
        body, grid=(FF // FT, T // TR),
        in_specs=[BS((TR, D), lambda q, i: (i, 0)),
                  BS((None, FT, D), lambda q, i: (0, q, 0)),
                  BS((None, FT, D), lambda q, i: (1, q, 0))] + (carry.in_specs if carry else []),
        out_specs=[out, out, out] + [ANY] * no,
        out_shape=[_sds((T, FF), F32), _sds((T, FF), F32), _sds((T, FF), BF16)]
        + (carry.out_shapes if carry else []),
        scratch_shapes=carry.scratch if carry else [],
        compiler_params=pltpu.CompilerParams(
            dimension_semantics=("arbitrary", "arbitrary"), vmem_limit_bytes=48 << 20,
            has_side_effects=bool(carry)), name=f"ffn_up_{l}",
    )(h2, wgu, wgu, *(carry.ins if carry else []))
    return outs[:3], outs[3:]


def _ffn_down_bwd(dx, wd, g, u, l):
    def epilogue(dact, ex, o_ref):
        gv = ex[0][...]
        uv = ex[1][...]
        sg = jax.nn.sigmoid(gv)
        o_ref[0] = (dact * uv * (sg * (1.0 + gv * (1.0 - sg)))).astype(BF16)
        o_ref[1] = (dact * (gv * sg)).astype(BF16)

    ex_spec = BS((TR, FT), lambda i, q, k: (i, q))
    return _mm("nt", dx, wd, grid=(T // TR, FF // FT, 1),
               a_spec=BS((TR, D), lambda i, q, k: (i, 0)),
               b_spec=BS((FT, D), lambda i, q, k: (q, 0)),
               out_shape=_sds((2, T, FF), BF16),
               out_spec=BS((2, TR, FT), lambda i, q, k: (0, i, q)),
               acc_shape=(TR, FT), extras=(g, u), extra_specs=(ex_spec, ex_spec),
               epilogue=epilogue, name=f"ffn_down_bwd_{l}")


def _loss_head(y, target):
    def body(y_ref, t_ref, l_ref, dy_ref, acc):
        i = pl.program_id(0)
        e = y_ref[...] - t_ref[...]
        dy_ref[...] = e * (1.0 / D)

        @pl.when(i == 0)
        def _():
            acc[...] = jnp.zeros_like(acc)

        acc[...] += jnp.sum(jnp.mean(e * e, axis=-1, keepdims=True), axis=0, keepdims=True)

        @pl.when(i == T // TR - 1)
        def _():
            l_ref[...] = 0.5 * acc[...]

    row = BS((TR, D), lambda i: (i, 0))
    return pl.pallas_call(
        body, grid=(T // TR,), in_specs=[row, row],
        out_specs=[BS((1, 1), lambda i: (0, 0)), row],
        out_shape=[_sds((1, 1), F32), _sds((T, D), F32)],
        scratch_shapes=[pltpu.VMEM((1, 1), F32)],
        compiler_params=_cp(("arbitrary",)), name="loss_head",
    )(y, target)


def _row_tile(rows, cols, itemsize=4, limit=2 << 20):
    tr = rows
    while tr * cols * itemsize > limit and tr % 2 == 0 and (tr // 2) % 16 == 0:
        tr //= 2
    return tr


def _cast_bf16(arrs, name):
    n = len(arrs)
    rows, cols = arrs[0].shape
    tr = _row_tile(rows, cols)

    def body(*refs):
        o_ref = refs[n]
        k = pl.program_id(0)
        val = refs[0][...]
        for j in range(1, n):
            val = jnp.where(k == j, refs[j][...], val)
        o_ref[...] = val.astype(BF16)

    return pl.pallas_call(
        body, grid=(n, rows // tr),
        in_specs=[BS((tr, cols), lambda k, i: (i, 0))] * n,
        out_specs=BS((None, tr, cols), lambda k, i: (k, i, 0)),
        out_shape=_sds((n, rows, cols), BF16),
        compiler_params=_cp(("arbitrary", "arbitrary")), name=name,
    )(*arrs)


def _quad_sum(own, got, name):
    n, rows, cols = own.shape
    tr = _row_tile(rows, cols)

    def body(a_ref, q_ref, o_ref):
        o_ref[...] = ((a_ref[...] + q_ref[0].astype(F32)) + q_ref[1].astype(F32)) + q_ref[2].astype(F32)

    spec = BS((None, tr, cols), lambda k, i: (k, i, 0))
    return pl.pallas_call(
        body, grid=(n, rows // tr),
        in_specs=[spec, BS((3, None, tr, cols), lambda k, i: (0, k, i, 0))], out_specs=spec,
        out_shape=_sds((n, rows, cols), F32),
        compiler_params=_cp(("arbitrary", "arbitrary")), name=name,
    )(own, got)


def _adam_math(w, g, m, v):
    m = ADAM_B1 * m + (1.0 - ADAM_B1) * g
    v = ADAM_B2 * v + (1.0 - ADAM_B2) * (g * g)
    m_hat = m / (1.0 - ADAM_B1 ** ADAM_STEP)
    v_hat = v / (1.0 - ADAM_B2 ** ADAM_STEP)
    delta = -ADAM_LR * (m_hat / (jnp.sqrt(v_hat) + ADAM_EPS) + ADAM_WD * w)
    return delta, m, v


def _adamw_big(w, g, m, v, name):
    shape = w.shape
    cols = shape[-1]
    rows = w.size // cols
    tr = _row_tile(rows, cols, limit=1 << 20)

    def body(w_ref, g_ref, m_ref, v_ref, d_ref, nm_ref, nv_ref):
        d, nm, nv = _adam_math(w_ref[...], g_ref[...], m_ref[...], v_ref[...])
        d_ref[...] = d
        nm_ref[...] = nm
        nv_ref[...] = nv

    spec = BS((tr, cols), lambda i: (i, 0))
    outs = pl.pallas_call(
        body, grid=(rows // tr,), in_specs=[spec] * 4, out_specs=[spec] * 3,
        out_shape=[_sds((rows, cols), F32)] * 3,
        compiler_params=_cp(("arbitrary",)), name=name,
    )(*[a.reshape(rows, cols) for a in (w, g, m, v)])
    return [o.reshape(shape) for o in outs]


def _adamw_small(ws, gs, ms, vs):
    n = len(ws)

    def body(*refs):
        for i in range(n):
            d, nm, nv = _adam_math(refs[i][...], refs[n + i][...], refs[2 * n + i][...],
                                   refs[3 * n + i][...])
            refs[4 * n + i][...] = d
            refs[5 * n + i][...] = nm
            refs[6 * n + i][...] = nv

    specs = [BS(w.shape, lambda i: (0, 0)) for w in ws]
    outs = pl.pallas_call(
        body, grid=(1,), in_specs=specs * 4, out_specs=specs * 3,
        out_shape=[_sds(w.shape, F32) for w in ws] * 3,
        compiler_params=_cp(("arbitrary",)), name="adamw_small",
    )(*ws, *gs, *ms, *vs)
    return outs[:n], outs[n:2 * n], outs[2 * n:]


def _place():
    x, y, c = lax.axis_index("x"), lax.axis_index("y"), lax.axis_index("c")
    chips = [(1 - x, y), (x, 1 - y), (1 - x, 1 - y)]
    return x, y, c, chips


class _Exchange:
    def __init__(self, ins, in_specs, out_shapes, scratch, start, finish):
        self.ins, self.in_specs, self.out_shapes, self.scratch = ins, in_specs, out_shapes, scratch
        self.start, self.finish = start, finish


def _run_exchange(ex, name, vmem_mb=40):
    ni, no = len(ex.ins), len(ex.out_shapes)

    def body(*refs):
        ex.start(refs[:ni], refs[ni:ni + no], refs[ni + no:])
        ex.finish(refs[:ni], refs[ni:ni + no], refs[ni + no:])

    return pl.pallas_call(
        body, in_specs=ex.in_specs, out_specs=[ANY] * no, out_shape=ex.out_shapes,
        scratch_shapes=ex.scratch,
        compiler_params=pltpu.CompilerParams(has_side_effects=True, vmem_limit_bytes=vmem_mb << 20),
        name=name,
    )(*ex.ins)


def _gather_exchange(srcs, dst_shapes, views, small=None):
    nu = len(srcs)
    nd = len(dst_shapes)
    ns = 1 if small is not None else 0

    def unpack(ins, outs, scr):
        x, y, c, chips = _place()
        src = ins[:nu]
        vw = [views[u](outs[:nd]) for u in range(nu)]
        vbuf = scr[:nu]
        send, recv, fsend, frecv, lsem, ssend, srecv, vsem = scr[nu:]

        def ici(u, j, shard, to):
            return pltpu.make_async_remote_copy(
                src_ref=vbuf[u].at[:, c], dst_ref=vw[u].at[:, shard, c],
                send_sem=send.at[3 * u + j], recv_sem=recv.at[3 * u + j],
                device_id=to, device_id_type=MESH)

        def fwd(u, j, shard, half):
            return pltpu.make_async_remote_copy(
                src_ref=vw[u].at[:, shard, half], dst_ref=vw[u].at[:, shard, half],
                send_sem=fsend.at[3 * u + j], recv_sem=frecv.at[3 * u + j],
                device_id=(x, y, 1 - c), device_id_type=MESH)

        def small_copy(j, shard, to):
            return pltpu.make_async_remote_copy(
                src_ref=ins[nu], dst_ref=outs[nd].at[shard],
                send_sem=ssend.at[j], recv_sem=srecv.at[j], device_id=to, device_id_type=MESH)

        stage = [pltpu.make_async_copy(src[u], vbuf[u], vsem.at[u]) for u in range(nu)]
        local = [pltpu.make_async_copy(vbuf[u], vw[u].at[:, 2 * x + y], lsem.at[u]) for u in range(nu)]
        if ns:
            local.append(pltpu.make_async_copy(ins[nu], outs[nd].at[2 * x + y], lsem.at[nu]))
        return x, y, c, chips, ici, fwd, small_copy, stage, local

    def start(ins, outs, scr):
        x, y, c, chips, ici, fwd, small_copy, stage, local = unpack(ins, outs, scr)
        s = 2 * x + y
        for cp in stage:
            cp.start()
        if ns:
            local[nu].start()
            for j, chip in enumerate(chips):
                small_copy(j, s, (*chip, c)).start()
        for u in range(nu):
            stage[u].wait()
            for j, chip in enumerate(chips):
                ici(u, j, s, (*chip, c)).start()
            local[u].start()

    def finish(ins, outs, scr):
        x, y, c, chips, ici, fwd, small_copy, stage, local = unpack(ins, outs, scr)
        s = 2 * x + y
        for u in range(nu):
            for j, chip in enumerate(chips):
                sj = 2 * chip[0] + chip[1]
                ici(u, j, sj, (x, y, c)).wait_recv()
                fwd(u, j, sj, c).start()
        for u in range(nu):
            for j, chip in enumerate(chips):
                sj = 2 * chip[0] + chip[1]
                fwd(u, j, sj, 1 - c).wait_recv()
        for u in range(nu):
            for j, chip in enumerate(chips):
                ici(u, j, s, (*chip, c)).wait_send()
                fwd(u, j, s, c).wait_send()
        if ns:
            for j, chip in enumerate(chips):
                small_copy(j, 2 * chip[0] + chip[1], (x, y, c)).wait_recv()
                small_copy(j, s, (*chip, c)).wait_send()
        for cp in local:
            cp.wait()

    dma = pltpu.SemaphoreType.DMA
    return _Exchange(
        ins=list(srcs) + ([small] if ns else []),
        in_specs=[ANY] * nu + [BS(memory_space=pltpu.VMEM)] * ns,
        out_shapes=[_sds(sh, BF16) for sh in dst_shapes]
        + ([_sds((NSH,) + small.shape, F32)] if ns else []),
        scratch=[pltpu.VMEM(a.shape, BF16) for a in srcs]
        + [dma((3 * nu,)), dma((3 * nu,)), dma((3 * nu,)), dma((3 * nu,)),
           dma((nu + 1,)), dma((3,)), dma((3,)), dma((nu,))],
        start=start, finish=finish)


def _pair_reduce(g, name):
    pn, _, _, rh, cc = g.shape
    n = pn * NSH

    def body(g_ref, own_ref, sb_ref, sendb, recvb, stage, outf, outb, send, recv, lsem, osem):
        x, y, c, _ = _place()
        s = 2 * x + y

        def load(k, half):
            return pltpu.make_async_copy(g_ref.at[k // NSH, k % NSH, half], stage.at[k % 2], lsem.at[k % 2])

        def push(k):
            return pltpu.make_async_remote_copy(
                src_ref=sendb.at[k], dst_ref=recvb.at[k], send_sem=send.at[k], recv_sem=recv.at[k],
                device_id=(x, y, 1 - c), device_id_type=MESH)

        def store(k):
            return pltpu.make_async_copy(outb.at[k % 2], sb_ref.at[k // NSH, k % NSH], osem.at[k % 2])

        load(0, 1 - c).start()
        for k in range(n):
            if k + 1 < n:
                load(k + 1, 1 - c).start()
            load(k, 1 - c).wait()
            sendb[k] = stage[k % 2].astype(BF16)
            push(k).start()
        load(0, c).start()
        for k in range(n):
            if k + 1 < n:
                load(k + 1, c).start()
            load(k, c).wait()
            push(k).wait_recv()
            total = stage[k % 2] + recvb[k].astype(F32)
            if k >= 2:
                store(k - 2).wait()
            outb[k % 2] = total.astype(BF16)
            store(k).start()

            @pl.when(s == k % NSH)
            def _():
                outf[...] = total
                keep = pltpu.make_async_copy(outf, own_ref.at[k // NSH], osem.at[2])
                keep.start()
                keep.wait()

        for k in range(max(n - 2, 0), n):
            store(k).wait()
        for k in range(n):
            push(k).wait_send()

    dma = pltpu.SemaphoreType.DMA
    return pl.pallas_call(
        body, in_specs=[ANY], out_specs=[ANY, ANY],
        out_shape=[_sds((pn, rh, cc), F32), _sds((pn, NSH, rh, cc), BF16)],
        scratch_shapes=[pltpu.VMEM((n, rh, cc), BF16), pltpu.VMEM((n, rh, cc), BF16),
                        pltpu.VMEM((2, rh, cc), F32), pltpu.VMEM((rh, cc), F32),
                        pltpu.VMEM((2, rh, cc), BF16), dma((n,)), dma((n,)), dma((2,)), dma((3,))],
        compiler_params=pltpu.CompilerParams(has_side_effects=True, vmem_limit_bytes=56 << 20),
        name=name,
    )(g)


def _chip_exchange(sums_bf16):
    nu = len(sums_bf16)

    def pushes(ins, outs, scr):
        x, y, c, chips = _place()
        send, recv = scr
        return [pltpu.make_async_remote_copy(
            src_ref=ins[u].at[:, 2 * chip[0] + chip[1]], dst_ref=outs[u].at[j],
            send_sem=send.at[3 * u + j], recv_sem=recv.at[3 * u + j],
            device_id=(*chip, c), device_id_type=MESH)
            for u in range(nu) for j, chip in enumerate(chips)]

    def start(ins, outs, scr):
        for cp in pushes(ins, outs, scr):
            cp.start()

    def finish(ins, outs, scr):
        for cp in pushes(ins, outs, scr):
            cp.wait()

    dma = pltpu.SemaphoreType.DMA
    shapes = [(3, a.shape[0], a.shape[2], a.shape[3]) for a in sums_bf16]
    return _Exchange(ins=list(sums_bf16), in_specs=[ANY] * nu,
                     out_shapes=[_sds(sh, BF16) for sh in shapes],
                     scratch=[dma((3 * nu,)), dma((3 * nu,))], start=start, finish=finish)


def _final_exchange(halves, out_shapes, targets):
    nu = len(halves)
    no = len(out_shapes)
    ncp = sum(len(t) for t in targets)

    def body(*refs):
        hv = refs[:nu]
        out = refs[nu:nu + no]
        sbuf = refs[nu + no:2 * nu + no]
        rbuf = refs[2 * nu + no:3 * nu + no]
        send, recv, lsem, osem, csem = refs[3 * nu + no:]
        x, y, c, _ = _place()
        stage = [pltpu.make_async_copy(hv[u], sbuf[u], lsem.at[u]) for u in range(nu)]
        push = [pltpu.make_async_remote_copy(
            src_ref=sbuf[u], dst_ref=rbuf[u], send_sem=send.at[u], recv_sem=recv.at[u],
            device_id=(x, y, 1 - c), device_id_type=MESH) for u in range(nu)]
        mine, theirs = [], []
        k = 0
        for u in range(nu):
            rh = hv[u].shape[1]
            for (p, oi, li) in targets[u]:
                mine.append((u, pltpu.make_async_copy(
                    sbuf[u].at[p], out[oi].at[li, pl.ds(c * rh, rh), :], csem.at[k])))
                theirs.append((u, pltpu.make_async_copy(
                    rbuf[u].at[p], out[oi].at[li, pl.ds((1 - c) * rh, rh), :], osem.at[k])))
                k += 1
        for cp in stage:
            cp.start()
        for u in range(nu):
            stage[u].wait()
            push[u].start()
            for (v, cp) in mine:
                if v == u:
                    cp.start()
        for u in range(nu):
            push[u].wait_recv()
            for (v, cp) in theirs:
                if v == u:
                    cp.start()
        for (_, cp) in theirs + mine:
            cp.wait()
        for u in range(nu):
            push[u].wait_send()

    dma = pltpu.SemaphoreType.DMA
    bufs = [pltpu.VMEM(h.shape, F32) for h in halves]
    return pl.pallas_call(
        body, in_specs=[ANY] * nu, out_specs=[ANY] * no,
        out_shape=[_sds(sh, F32) for sh in out_shapes],
        scratch_shapes=bufs + bufs + [dma((nu,)), dma((nu,)), dma((nu,)), dma((ncp,)), dma((ncp,))],
        compiler_params=pltpu.CompilerParams(has_side_effects=True, vmem_limit_bytes=56 << 20),
        name="final_exchange",
    )(*halves)


def _small_allreduce(pack):
    rows = pack.shape[0]

    def body(p_ref, o_ref, buf, send, recv):
        x, y, c, _ = _place()
        me = 4 * x + 2 * y + c
        buf[me] = p_ref[...]
        k = 0
        copies = []
        for dx in range(2):
            for dy in range(2):
                for dc in range(2):
                    if dx == 0 and dy == 0 and dc == 0:
                        continue
                    to = (jnp.where(dx, 1 - x, x), jnp.where(dy, 1 - y, y), jnp.where(dc, 1 - c, c))
                    src_slot = 4 * to[0] + 2 * to[1] + to[2]
                    copies.append((pltpu.make_async_remote_copy(
                        src_ref=p_ref, dst_ref=buf.at[me], send_sem=send.at[k], recv_sem=recv.at[k],
                        device_id=to, device_id_type=MESH), src_slot, k))
                    k += 1
        for cp, _, _ in copies:
            cp.start()
        for cp, src_slot, k in copies:
            pltpu.make_async_remote_copy(
                src_ref=p_ref, dst_ref=buf.at[src_slot], send_sem=send.at[k], recv_sem=recv.at[k],
                device_id=(x, y, c), device_id_type=MESH).wait()
        acc = buf[0]
        for d in range(1, 8):
            acc = acc + buf[d]
        o_ref[...] = acc

    dma = pltpu.SemaphoreType.DMA
    vm = BS(memory_space=pltpu.VMEM)
    return pl.pallas_call(
        body, in_specs=[vm], out_specs=vm, out_shape=_sds((rows, D), F32),
        scratch_shapes=[pltpu.VMEM((8, rows, D), F32), dma((7,)), dma((7,))],
        compiler_params=pltpu.CompilerParams(has_side_effects=True, vmem_limit_bytes=32 << 20),
        name="small_allreduce",
    )(pack)


def _in_proj(h, w, bias, name, transposed=False):
    n = w.shape[0] if transposed else w.shape[1]
    tn = 640 if n == NA else 896
    ep = None
    extras, especs = (), ()
    if bias is not None:
        def ep(acc, ex, o_ref):
            o_ref[...] = acc + ex[0][...]
        extras = (bias,)
        especs = (BS((1, tn), lambda i, j, k: (0, j)),)
    b_spec = BS((tn, D), lambda i, j, k: (j, 0)) if transposed else BS((D, tn), lambda i, j, k: (0, j))
    return _mm("nt" if transposed else "nn", h, w, grid=(T // TR, n // tn, 1),
               a_spec=BS((TR, D), lambda i, j, k: (i, 0)), b_spec=b_spec,
               out_shape=_sds((T, n), F32), out_spec=BS((TR, tn), lambda i, j, k: (i, j)),
               acc_shape=(TR, tn), extras=extras, extra_specs=especs, epilogue=ep, name=name)


def _add_res(acc, ex, o_ref):
    o_ref[...] = acc + ex[0][...]


def _local_step(x, mem, target, w, p, carries=None, bwd_carry_fn=None):
    row = lambda i, j, k: (i, 0)
    whole = lambda i, j, k: (0, 0)
    w = {k: (list(v) if isinstance(v, list) else v) for k, v in w.items()}
    carries = carries or {}

    def carry_of(name):
        return carries[name][0] if name in carries else None

    def delivered(name, outs):
        if name in carries:
            carries[name][1](w, outs)

    saved = []
    bias = _bias_expand(p["rel_u"])
    for l in range(2):
        type_a = l == 0
        h = _rms_fwd(x, p["norm1_g"][l:l + 1], f"rms1_{l}")
        memn = _rms_fwd(mem, p["mem_norm_g"][l:l + 1], f"rmsmem_{l}")
        if type_a:
            z = _in_proj(h, w["a"], None, "inproj_a")
            cat, carried = _attn_fwd(z, bias, p["a_q_g2"], p["a_k_g2"], carry_of("attn_fwd"))
            delivered("attn_fwd", carried)
            qcol = NA // MEMW - 1
        else:
            z = _in_proj(h, w["b"], p["b_b_in"], "inproj_b", transposed=True)
            cat = _conv_fwd(z, p["conv_w"], p["conv_b"], p["ln_g"], p["ln_b"])
            qcol = NBW // MEMW - 1
        kv = _mm("nn", memn, w["kv"][l], grid=(1, 1, 1),
                 a_spec=BS((NB * MEMT, D), whole), b_spec=BS((D, 2 * MEMW), whole),
                 out_shape=_sds((NB * MEMT, 2 * MEMW), F32), out_spec=BS((NB * MEMT, 2 * MEMW), whole),
                 acc_shape=(8, 128), name=f"memkv_{l}")
        cat = _memattn_fwd(z, kv, cat, p["mq_g4"][l:l + 1], p["mk_g4"][l:l + 1], qcol, f"memattn_fwd_{l}")
        x1 = _mm("nn", cat, w["wo"][l], grid=(T // TR, 1, 1), a_spec=BS((TR, D), row),
                 b_spec=BS((D, D), whole),
                 out_shape=_sds((T, D), F32), out_spec=BS((TR, D), row), acc_shape=(8, 128),
                 extras=(x,), extra_specs=(BS((TR, D), row),), epilogue=_add_res, name=f"outproj_{l}")
        h2 = _rms_fwd(x1, p["norm2_g"][l:l + 1], f"rms2_{l}")
        (g, u, act), carried = _ffn_up(h2, w["gu"][l], l, carry_of(f"ffn_up_{l}"))
        delivered(f"ffn_up_{l}", carried)
        x2 = _mm("nn", act, w["wd"][l], grid=(T // TR, 1, 1),
                 a_spec=BS((TR, FF), row), b_spec=BS((FF, D), whole),
                 out_shape=_sds((T, D), F32), out_spec=BS((TR, D), row), acc_shape=(8, 128),
                 extras=(x1,), extra_specs=(BS((TR, D), row),), epilogue=_add_res, name=f"ffn_down_{l}",
                 carry=carry_of(f"ffn_down_{l}"))
        if carry_of(f"ffn_down_{l}") is not None:
            x2, carried = x2
            delivered(f"ffn_down_{l}", carried)
        saved.append(dict(x=x, h=h, memn=memn, kv=kv, z=z, cat=cat, x1=x1, h2=h2, g=g, u=u, act=act,
                          qcol=qcol))
        x = x2

    loss, dx = _loss_head(x, target)

    big = dict(a=None, b=None, kv=[None, None], wo=[None, None], gu=[None, None], wd=[None, None])
    small = {}
    bwd_carried = ()
    tk = T // 2
    nkt = T // tk
    for l in (1, 0):
        sv = saved[l]
        dgu = _ffn_down_bwd(dx, w["wd"][l], sv["g"], sv["u"], l)
        big["wd"][l] = _mm("tn", sv["act"], dx, grid=(FF // FT, 1, 2 * nkt),
                           a_spec=BS((tk // 2, FT), lambda i, j, k: (k, i)),
                           b_spec=BS((tk // 2, D), lambda i, j, k: (k, 0)),
                           out_shape=_sds((FF, D), F32), out_spec=BS((FT, D), lambda i, j, k: (i, 0)),
                           acc_shape=(FT, D), name=f"dw_down_{l}")
        dh2 = _mm("nn", dgu, w["gu"][l], grid=(T // TR, 1, 2),
                  a_spec=BS((None, TR, FF), lambda i, j, k: (k, i, 0)),
                  b_spec=BS((None, FF, D), lambda i, j, k: (k, 0, 0)),
                  out_shape=_sds((T, D), F32), out_spec=BS((TR, D), row), acc_shape=(TR, D),
                  name=f"dh2_{l}")
        big["gu"][l] = _mm("tn", dgu, sv["h2"], grid=(2 * FF // FT, 1, 2 * nkt),
                           a_spec=BS((None, tk // 2, FT), lambda i, j, k: (i // 2, k, i % 2)),
                           b_spec=BS((tk // 2, D), lambda i, j, k: (k, 0)),
                           out_shape=_sds((2, FF, D), F32),
                           out_spec=BS((None, FT, D), lambda i, j, k: (i // 2, i % 2, 0)),
                           acc_shape=(FT, D), name=f"dw_gu_{l}")
        dx1, small[f"norm2_g{l}"] = _rms_bwd(dh2, sv["x1"], p["norm2_g"][l:l + 1], dx, f"rms2_bwd_{l}")
        dcat = _mm("nt", dx1, w["wo"][l], grid=(T // TR, 1, 1), a_spec=BS((TR, D), row),
                   b_spec=BS((D, D), whole),
                   out_shape=_sds((T, D), F32), out_spec=BS((TR, D), row), acc_shape=(8, 128),
                   name=f"dcat_{l}")
        big["wo"][l] = _mm("tn", sv["cat"], dx1, grid=(1, 1, nkt),
                           a_spec=BS((tk, D), lambda i, j, k: (k, 0)), b_spec=BS((tk, D), lambda i, j, k: (k, 0)),
                           out_shape=_sds((D, D), F32), out_spec=BS((D, D), whole),
                           acc_shape=(D, D), name=f"dw_out_{l}")
        dqm, dkv, small[f"mq_g{l}"], small[f"mk_g{l}"] = _memattn_bwd(
            sv["z"], sv["kv"], dcat, p["mq_g4"][l:l + 1], p["mk_g4"][l:l + 1], sv["qcol"], f"memattn_bwd_{l}")
        if l == 0:
            carry = bwd_carry_fn(big) if bwd_carry_fn is not None else None
            (dq, dk, dv, dbias, small["a_q_g"], small["a_k_g"]), bwd_carried = _attn_bwd(
                sv["z"], dcat, bias, p["a_q_g2"], p["a_k_g2"], carry)
            small["rel_u"] = _bias_reduce(dbias)
            dz = jnp.concatenate([dq, dk, dv, dqm], axis=1)
            w_in, key, n, tn = w["a"], "a", NA, 640
        else:
            du, small["conv_w"], small["conv_b"], small["ln_g"], small["ln_b"], dbin_u = _conv_bwd(
                sv["z"], dcat, p["conv_w"], p["conv_b"], p["ln_g"], p["ln_b"])
            dz = jnp.concatenate([du, dqm], axis=1)
            small["b_in_u"] = dbin_u
            w_in, key, n, tn = w["b"], "b", NBW, 896
        if l == 0:
            dh = _mm("nt", dz, w_in, grid=(T // TR, 1, 1),
                     a_spec=BS((TR, n), row), b_spec=BS((D, n), whole),
                     out_shape=_sds((T, D), F32), out_spec=BS((TR, D), row), acc_shape=(8, 128),
                     name=f"dh_{l}")
            big[key] = _mm("tn", sv["h"], dz, grid=(1, n // tn, nkt),
                           a_spec=BS((tk, D), lambda i, j, k: (k, 0)), b_spec=BS((tk, tn), lambda i, j, k: (k, j)),
                           out_shape=_sds((D, n), F32), out_spec=BS((D, tn), lambda i, j, k: (0, j)),
                           acc_shape=(D, tn), name=f"dw_in_{l}")
        else:
            dh = _mm("nn", dz, w_in, grid=(T // TR, 1, 1),
                     a_spec=BS((TR, n), row), b_spec=BS((n, D), whole),
                     out_shape=_sds((T, D), F32), out_spec=BS((TR, D), row), acc_shape=(8, 128),
                     name=f"dh_{l}")
            big[key] = _mm("tn", dz, sv["h"], grid=(n // tn, 1, nkt),
                           a_spec=BS((tk, tn), lambda i, j, k: (k, i)), b_spec=BS((tk, D), lambda i, j, k: (k, 0)),
                           out_shape=_sds((n, D), F32), out_spec=BS((tn, D), lambda i, j, k: (i, 0)),
                           acc_shape=(tn, D), name=f"dw_in_{l}")
        if l == 1:
            small["b_in_qm"] = _colsum(dqm, "colsum_dqm")
        dx, small[f"norm1_g{l}"] = _rms_bwd(dh, sv["x"], p["norm1_g"][l:l + 1], dx1, f"rms1_bwd_{l}")
        big["kv"][l] = _mm("tn", sv["memn"], dkv, grid=(1, 1, 1),
                           a_spec=BS((NB * MEMT, D), whole), b_spec=BS((NB * MEMT, 2 * MEMW), whole),
                           out_shape=_sds((D, 2 * MEMW), F32), out_spec=BS((D, 2 * MEMW), whole),
                           acc_shape=(8, 128), name=f"dw_kv_{l}")
        dmemn = _mm("nt", dkv, w["kv"][l], grid=(1, 1, 1),
                    a_spec=BS((NB * MEMT, 2 * MEMW), whole), b_spec=BS((D, 2 * MEMW), whole),
                    out_shape=_sds((NB * MEMT, D), F32), out_spec=BS((NB * MEMT, D), whole),
                    acc_shape=(8, 128), name=f"dmemn_{l}")
        _, small[f"mem_norm_g{l}"] = _rms_bwd(dmemn, mem, p["mem_norm_g"][l:l + 1], None, f"rmsmem_bwd_{l}")
    return loss, dx, big, small, bwd_carried


def _colsum(a, name):
    rows, cols = a.shape

    def body(a_ref, o_ref):
        @pl.when(pl.program_id(0) == 0)
        def _():
            o_ref[...] = jnp.zeros_like(o_ref)

        o_ref[...] += jnp.sum(a_ref[...].astype(F32), axis=0, keepdims=True)

    return pl.pallas_call(
        body, grid=(rows // TR,), in_specs=[BS((TR, cols), lambda i: (i, 0))],
        out_specs=BS((1, cols), lambda i: (0, 0)), out_shape=_sds((1, cols), F32),
        compiler_params=_cp(("arbitrary",)), name=name,
    )(a)


_PACK_ROWS = 64


def _pad_to(a, rows, cols=D):
    return jnp.pad(a, ((0, rows - a.shape[0]), (0, cols - a.shape[1])))


def _pack_small(sm):
    parts = [
        jnp.concatenate([sm["norm1_g0"], sm["norm1_g1"]], 0),
        jnp.concatenate([sm["mem_norm_g0"], sm["mem_norm_g1"]], 0),
        jnp.concatenate([sm["norm2_g0"], sm["norm2_g1"]], 0),
        _pad_to(sm["a_q_g"], 1), _pad_to(sm["a_k_g"], 1),
        _pad_to(jnp.concatenate([sm["mq_g0"], sm["mq_g1"]], 0), 2),
        _pad_to(jnp.concatenate([sm["mk_g0"], sm["mk_g1"]], 0), 2),
        _pad_to(sm["conv_b"], 1), _pad_to(sm["ln_g"], 1), _pad_to(sm["ln_b"], 1),
        _pad_to(sm["b_in_u"][:, :D], 1), _pad_to(sm["b_in_u"][:, D:], 1),
        _pad_to(sm["b_in_qm"], 1),
        _pad_to(sm["conv_w"][:CONVW], CONVW),
        sm["rel_u"].reshape(12, D),
    ]
    pack = jnp.concatenate(parts, 0)
    return jnp.pad(pack, ((0, _PACK_ROWS - pack.shape[0]), (0, 0)))


def _rel_table_to_u(rel_bias):
    flat = jnp.concatenate([jnp.broadcast_to(rel_bias[:, 191:192], (12, 447)), rel_bias[:, ::-1]], axis=1)
    return jnp.pad(flat, ((0, 0), (192, 1024 - 192 - 639))).reshape(12, 1, 1024)


def _u_to_rel_table(du):
    flat = du[:, 192:192 + 639]
    g = flat[:, 447:][:, ::-1]
    return g, flat[:, :447]


def kernel(x, mem, norm1_g, mem_norm_g, a_w_in, a_q_g, a_k_g, a_rel_bias, b_w_in, b_b_in, b_conv_w, b_conv_b, b_ln_g, b_ln_b, mq_g, mk_g, w_mem_kv, w_out, norm2_g, w_gate, w_up, w_down, loss_target, m_norm1_g, m_mem_norm_g, m_a_w_in, m_a_q_g, m_a_k_g, m_a_rel_bias, m_b_w_in, m_b_b_in, m_b_conv_w, m_b_conv_b, m_b_ln_g, m_b_ln_b, m_mq_g, m_mk_g, m_w_mem_kv, m_w_out, m_norm2_g, m_w_gate, m_w_up, m_w_down, v_norm1_g, v_mem_norm_g, v_a_w_in, v_a_q_g, v_a_k_g, v_a_rel_bias, v_b_w_in, v_b_b_in, v_b_conv_w, v_b_conv_b, v_b_ln_g, v_b_ln_b, v_mq_g, v_mk_g, v_w_mem_kv, v_w_out, v_norm2_g, v_w_gate, v_w_up, v_w_down):
    sx = 2 * lax.axis_index("x") + lax.axis_index("y")

    n_in = (NA // NSH, NBW // NSH)
    tr = lambda a: jnp.swapaxes(a, -1, -2)
    small_src = jnp.concatenate([
        jnp.pad(b_b_in, ((0, 0), (0, 512 - 448))),
        jnp.pad(b_conv_w[0], ((0, 0), (0, 512 - 192))),
        jnp.pad(jnp.concatenate([b_conv_b, b_ln_g, b_ln_b], 0), ((0, 0), (0, 512 - 192))),
        jnp.zeros((5, 512), F32)], 0)

    def gather_group(l, names, small=None):
        src_of = {
            "in": lambda: [_cast_bf16([tr(b_w_in[0])], "cast_in_1").reshape(1, 2, n_in[1] // 2, D) if l else
                           _cast_bf16([a_w_in[0]], "cast_in_0").reshape(1, 2, D // 2, n_in[0])],
            "kv": lambda: [_cast_bf16([w_mem_kv[l]], f"cast_kv_{l}").reshape(1, 2, 128, 2 * MEMW)],
            "wo": lambda: [_cast_bf16([w_out[l]], f"cast_wo_{l}").reshape(1, 2, 128, D)],
            "gu": lambda: [_cast_bf16([tr(w_gate[l])], f"cast_gate_{l}").reshape(1, 2, FS // 2, D),
                           _cast_bf16([tr(w_up[l])], f"cast_up_{l}").reshape(1, 2, FS // 2, D)],
            "wd": lambda: [_cast_bf16([w_down[l]], f"cast_wd_{l}").reshape(1, 2, FS // 2, D)]}
        dst_of = {"in": (1, NSH, 2, n_in[1] // 2, D) if l else (1, NSH, 2, D // 2, n_in[0]),
                  "kv": (1, NSH, 2, 128, 2 * MEMW), "wo": (1, NSH, 2, 128, D),
                  "gu": (1, 2, NSH, 2, FS // 2, D), "wd": (1, NSH, 2, FS // 2, D)}
        srcs, views = [], []
        for k, name in enumerate(names):
            srcs += src_of[name]()
            if name == "gu":
                views += [lambda d, k=k: d[k].at[:, 0], lambda d, k=k: d[k].at[:, 1]]
            else:
                views.append(lambda d, k=k: d[k])

        def done(w, outs):
            for k, name in enumerate(names):
                if name == "in" and l == 0:
                    w["a"] = outs[k].reshape(NSH, D, n_in[0]).transpose(1, 0, 2).reshape(D, NA)
                elif name == "in":
                    w["b"] = outs[k].reshape(NBW, D)
                else:
                    shape = {"kv": (D, 2 * MEMW), "wo": (D, D), "gu": (2, FF, D), "wd": (FF, D)}
                    w[name][l] = outs[k].reshape(shape[name])

        return _gather_exchange(srcs, [dst_of[name] for name in names], views, small), done

    w = dict(a=None, b=None, kv=[None, None], wo=[None, None], gu=[None, None], wd=[None, None])
    first, first_done = gather_group(0, ["in", "kv"], small_src)
    outs0 = _run_exchange(first, "gather_weights_first")
    first_done(w, outs0)
    small_all = outs0[2]
    carries = {"attn_fwd": gather_group(0, ["wo", "gu", "wd"]),
               "ffn_up_0": gather_group(1, ["in", "kv", "wo", "wd"]),
               "ffn_down_0": gather_group(1, ["gu"])}

    conv_w_full = small_all[:, 1:1 + CONVW, :192].transpose(1, 0, 2).reshape(CONVW, TOK)
    vec3 = small_all[:, 32:35, :192].transpose(1, 0, 2).reshape(3, TOK)
    p = dict(
        norm1_g=norm1_g, mem_norm_g=mem_norm_g, norm2_g=norm2_g,
        a_q_g2=jnp.tile(a_q_g, (1, AH)), a_k_g2=jnp.tile(a_k_g, (1, AH)),
        mq_g4=jnp.tile(mq_g, (1, 4)), mk_g4=jnp.tile(mk_g, (1, 4)),
        rel_u=_rel_table_to_u(a_rel_bias[0]),
        b_b_in=small_all[:, 0, :448].reshape(1, NBW),
        conv_w=jnp.pad(conv_w_full, ((0, 1), (0, 0))),
        conv_b=vec3[0:1], ln_g=vec3[1:2], ln_b=vec3[2:3])

    def pair_sums(items, big):
        own, sums_b = [], []
        for l, name in items:
            if name == "in" and l == 0:
                g = big["a"].reshape(D, NSH, n_in[0]).transpose(1, 0, 2).reshape(1, NSH, 2, D // 2, n_in[0])
            elif name == "in":
                g = big["b"].reshape(1, NSH, 2, n_in[1] // 2, D)
            else:
                shape = {"kv": (1, NSH, 2, 128, 2 * MEMW), "wo": (1, NSH, 2, 128, D),
                         "gu": (2, NSH, 2, FS // 2, D), "wd": (1, NSH, 2, FS // 2, D)}
                g = big[name][l].reshape(shape[name])
            of, sb = _pair_reduce(g, f"pair_reduce_{name}_{l}")
            own.append(of)
            sums_b.append(sb)
        return own, sums_b

    early = [(1, "in"), (1, "kv"), (1, "wo"), (1, "gu"), (1, "wd"), (0, "wo"), (0, "gu"), (0, "wd")]
    late = [(0, "in"), (0, "kv")]
    own_early = []

    def bwd_carry_fn(big):
        own, sums_b = pair_sums(early, big)
        own_early.extend(own)
        return _chip_exchange(sums_b)

    loss, grad_x, big, small, parts_early = _local_step(
        x.reshape(T, D), mem.reshape(NB * MEMT, D), loss_target.reshape(T, D), w, p,
        carries=carries, bwd_carry_fn=bwd_carry_fn)
    loss = lax.psum(loss[0, 0], ("x", "y", "c"))

    own_late, sums_b_late = pair_sums(late, big)
    parts_late = _run_exchange(_chip_exchange(sums_b_late), "chip_exchange_late")
    items = early + late
    halves = [_quad_sum(o, pt, f"quad_sum_{name}_{l}")
              for (l, name), o, pt in zip(items, own_early + own_late, list(parts_early) + list(parts_late))]
    out_shapes = [(1, D, NA // NSH), (1, NBW // NSH, D), (2, 2 * 128, 2 * MEMW), (2, 2 * 128, D),
                  (2, FS, D), (2, FS, D), (2, FS, D)]
    target_of = {"in": lambda l: [(0, l, 0)], "kv": lambda l: [(0, 2, l)], "wo": lambda l: [(0, 3, l)],
                 "gu": lambda l: [(0, 4, l), (1, 5, l)], "wd": lambda l: [(0, 6, l)]}
    targets = [target_of[name](l) for l, name in items]
    g_a, g_b, g_kv, g_wo, g_gate, g_up, g_wd = _final_exchange(halves, out_shapes, targets)

    tot = _small_allreduce(_pack_small(small))
    g_rel, clip_part = _u_to_rel_table(tot[49:61])
    g_rel = jnp.concatenate([g_rel[:, :191], g_rel[:, 191:] + _rowsum(clip_part)], axis=1)
    b_in_full = jnp.concatenate([tot[15:16], tot[16:17, :512], tot[17:18, :MEMW]], axis=1)
    g_small = dict(
        norm1_g=tot[0:2], mem_norm_g=tot[2:4], norm2_g=tot[4:6],
        a_q_g=tot[6:7, :HD], a_k_g=tot[7:8, :HD], a_rel_bias=g_rel[None],
        b_b_in=lax.dynamic_slice(b_in_full, (0, sx * 448), (1, 448)),
        b_conv_w=lax.dynamic_slice(tot[18:49, :TOK], (0, sx * 192), (CONVW, 192))[None],
        b_conv_b=lax.dynamic_slice(tot[12:13, :TOK], (0, sx * 192), (1, 192)),
        b_ln_g=lax.dynamic_slice(tot[13:14, :TOK], (0, sx * 192), (1, 192)),
        b_ln_b=lax.dynamic_slice(tot[14:15, :TOK], (0, sx * 192), (1, 192)),
        mq_g=tot[8:10, :HD], mk_g=tot[10:12, :HD])

    names = ["norm1_g", "mem_norm_g", "a_w_in", "a_q_g", "a_k_g", "a_rel_bias", "b_w_in", "b_b_in",
             "b_conv_w", "b_conv_b", "b_ln_g", "b_ln_b", "mq_g", "mk_g", "w_mem_kv", "w_out",
             "norm2_g", "w_gate", "w_up", "w_down"]
    weights = dict(zip(names, [norm1_g, mem_norm_g, a_w_in, a_q_g, a_k_g, a_rel_bias, b_w_in, b_b_in,
                               b_conv_w, b_conv_b, b_ln_g, b_ln_b, mq_g, mk_g, w_mem_kv, w_out,
                               norm2_g, w_gate, w_up, w_down]))
    ms = dict(zip(names, [m_norm1_g, m_mem_norm_g, m_a_w_in, m_a_q_g, m_a_k_g, m_a_rel_bias, m_b_w_in,
                          m_b_b_in, m_b_conv_w, m_b_conv_b, m_b_ln_g, m_b_ln_b, m_mq_g, m_mk_g,
                          m_w_mem_kv, m_w_out, m_norm2_g, m_w_gate, m_w_up, m_w_down]))
    vs = dict(zip(names, [v_norm1_g, v_mem_norm_g, v_a_w_in, v_a_q_g, v_a_k_g, v_a_rel_bias, v_b_w_in,
                          v_b_b_in, v_b_conv_w, v_b_conv_b, v_b_ln_g, v_b_ln_b, v_mq_g, v_mk_g,
                          v_w_mem_kv, v_w_out, v_norm2_g, v_w_gate, v_w_up, v_w_down]))
    grads = dict(g_small)
    grads.update(a_w_in=g_a, b_w_in=g_b, w_mem_kv=g_kv, w_out=g_wo, w_gate=g_gate, w_up=g_up, w_down=g_wd)
    big_names = ["a_w_in", "b_w_in", "w_mem_kv", "w_out", "w_gate", "w_up", "w_down"]
    small_names = [n for n in names if n not in big_names]
    delta, new_m, new_v = {}, {}, {}
    for n in big_names:
        if n in ("b_w_in", "w_gate", "w_up"):
            outs = _adamw_big(tr(weights[n]), grads[n], tr(ms[n]), tr(vs[n]), f"adamw_{n}")
            delta[n], new_m[n], new_v[n] = [tr(o) for o in outs]
            grads[n] = tr(grads[n])
        else:
            delta[n], new_m[n], new_v[n] = _adamw_big(weights[n], grads[n], ms[n], vs[n], f"adamw_{n}")
    as2d = lambda a: a.reshape(-1, a.shape[-1])
    d_s, m_s, v_s = _adamw_small([as2d(weights[n]) for n in small_names], [as2d(grads[n]) for n in small_names],
                                 [as2d(ms[n]) for n in small_names], [as2d(vs[n]) for n in small_names])
    for i, n in enumerate(small_names):
        delta[n] = d_s[i].reshape(weights[n].shape)
        new_m[n] = m_s[i].reshape(weights[n].shape)
        new_v[n] = v_s[i].reshape(weights[n].shape)

    return (loss, grad_x.reshape(NB, SEQ, D), *[grads[n] for n in names], *[delta[n] for n in names],
            *[new_m[n] for n in names], *[new_v[n] for n in names])


def _rowsum(a):
    def body(a_ref, o_ref):
        o_ref[...] = jnp.sum(a_ref[...], axis=1, keepdims=True)

    vm = BS(memory_space=pltpu.VMEM)
    return pl.pallas_call(body, in_specs=[vm], out_specs=vm, out_shape=_sds((a.shape[0], 1), F32),
                          compiler_params=_cp(), name="rowsum")(a)
```

```python
import functools

import jax
import jax.numpy as jnp
from jax import lax
from jax.experimental import pallas as pl
from jax.experimental.pallas import tpu as pltpu

F32 = jnp.float32
BF16 = jnp.bfloat16
BS = pl.BlockSpec
ANY = pl.BlockSpec(memory_space=pl.ANY)
MESH = pl.DeviceIdType.MESH

D = 1024
SEQ = 2048
NB = 2
T = NB * SEQ
MEMT = 256
HD = 64
TOK = 768
MEMW = 256
NA = 3 * TOK + MEMW
NBW = 2 * TOK + MEMW
FF = 2816
NSH = 4
FS = FF // NSH
FT = FF // 2
CONVW = 31
EPS = 1e-6
NEG = -1e30
SCALE = HD ** -0.5
QB = 256
KWIN = 768
KPAD = 512
TR = 512

ADAM_LR = 0.001
ADAM_B1 = 0.9
ADAM_B2 = 0.999
ADAM_EPS = 1e-08
ADAM_WD = 0.01
ADAM_STEP = 10

_DIMS = {
    "nn": (((1,), (0,)), ((), ())),
    "nt": (((1,), (1,)), ((), ())),
    "tn": (((0,), (0,)), ((), ())),
}


def _cp(sem=None, vmem_mb=48):
    return pltpu.CompilerParams(dimension_semantics=sem, vmem_limit_bytes=vmem_mb << 20)


def _sds(shape, dtype):
    return jax.ShapeDtypeStruct(tuple(shape), dtype)


def _mm(mode, a, b, *, grid, a_spec, b_spec, out_shape, out_spec, acc_shape, name,
        extras=(), extra_specs=(), epilogue=None, carry=None, vmem_mb=48, sequential=False):
    n_ex = len(extras)
    nk = grid[2]
    dims = _DIMS[mode]
    ni = len(carry.ins) if carry else 0
    no = len(carry.out_shapes) if carry else 0
    multi = isinstance(out_shape, (list, tuple))
    out_shapes = list(out_shape) if multi else [out_shape]
    out_specs = list(out_spec) if multi else [out_spec]
    n_o = len(out_shapes)

    def body(a_ref, b_ref, *rest):
        ex = rest[:n_ex]
        cin = rest[n_ex:n_ex + ni]
        o_refs = rest[n_ex + ni:n_ex + ni + n_o]
        o_ref = o_refs if multi else o_refs[0]
        cout = rest[n_ex + ni + n_o:n_ex + ni + n_o + no]
        acc = rest[n_ex + ni + n_o + no]
        cscr = rest[n_ex + ni + n_o + no + 1:]
        ids = [pl.program_id(d) for d in range(3)]
        k = ids[2]
        if carry:
            @pl.when((ids[0] == 0) & (ids[1] == 0) & (ids[2] == 0))
            def _():
                carry.start(cin, cout, cscr)

        prod = lax.dot_general(a_ref[...].astype(BF16), b_ref[...].astype(BF16), dims,
                               preferred_element_type=F32)

        def finish(val):
            if epilogue is None:
                o_ref[...] = val.astype(o_ref.dtype)
            else:
                epilogue(val, ex, o_ref, ids[0])

        if nk == 1:
            finish(prod)
        else:
            @pl.when(k == 0)
            def _():
                acc[...] = prod

            @pl.when((k > 0) & (k < nk - 1))
            def _():
                acc[...] += prod

            @pl.when(k == nk - 1)
            def _():
                finish(acc[...] + prod)

        if carry:
            @pl.when((ids[0] == grid[0] - 1) & (ids[1] == grid[1] - 1) & (ids[2] == grid[2] - 1))
            def _():
                carry.finish(cin, cout, cscr)

    acc_scratch = pltpu.VMEM(acc_shape if nk > 1 else (8, 128), F32)
    ordered = sequential or bool(carry)
    outs = pl.pallas_call(
        body, grid=grid,
        in_specs=[a_spec, b_spec, *extra_specs] + (carry.in_specs if carry else []),
        out_specs=out_specs + [ANY] * no, out_shape=out_shapes + (carry.out_shapes if carry else []),
        scratch_shapes=[acc_scratch] + (carry.scratch if carry else []),
        compiler_params=pltpu.CompilerParams(
            dimension_semantics=("arbitrary",) * 3 if ordered else ("parallel", "parallel", "arbitrary"),
            vmem_limit_bytes=vmem_mb << 20, has_side_effects=bool(carry)), name=name,
    )(a, b, *extras, *(carry.ins if carry else []))
    mine = list(outs[:n_o]) if multi else outs[0]
    return (mine, outs[n_o:]) if carry else mine


def _rms_fwd(x, g, name):
    rows = x.shape[0]

    def body(x_ref, g_ref, o_ref):
        xv = x_ref[...]
        r = lax.rsqrt(jnp.mean(xv * xv, axis=-1, keepdims=True) + EPS)
        o_ref[...] = (xv * r * g_ref[...]).astype(BF16)

    return pl.pallas_call(
        body, grid=(rows // TR,),
        in_specs=[BS((TR, D), lambda i: (i, 0)), BS((1, D), lambda i: (0, 0))],
        out_specs=BS((TR, D), lambda i: (i, 0)), out_shape=_sds((rows, D), BF16),
        compiler_params=_cp(("arbitrary",)), name=name,
    )(x, g)


def _rms_bwd(dh, x, g, dres, name):
    rows = x.shape[0]
    has_res = dres is not None

    def body(*refs):
        if has_res:
            dh_ref, x_ref, g_ref, r_ref, dx_ref, dg_ref = refs
        else:
            dh_ref, x_ref, g_ref, dx_ref, dg_ref = refs
        xv = x_ref[...]
        dhv = dh_ref[...]
        r = lax.rsqrt(jnp.mean(xv * xv, axis=-1, keepdims=True) + EPS)
        xh = xv * r
        gy = dhv * g_ref[...]
        dx = r * (gy - xh * jnp.mean(gy * xh, axis=-1, keepdims=True))
        if has_res:
            dx = dx + r_ref[...]
        dx_ref[...] = dx

        @pl.when(pl.program_id(0) == 0)
        def _():
            dg_ref[...] = jnp.zeros_like(dg_ref)

        dg_ref[...] += jnp.sum(dhv * xh, axis=0, keepdims=True)

    row = BS((TR, D), lambda i: (i, 0))
    vec = BS((1, D), lambda i: (0, 0))
    ins = [dh, x, g] + ([dres] if has_res else [])
    return pl.pallas_call(
        body, grid=(rows // TR,),
        in_specs=[row, row, vec] + ([row] if has_res else []),
        out_specs=[row, vec], out_shape=[_sds((rows, D), F32), _sds((1, D), F32)],
        compiler_params=_cp(("arbitrary",)), name=name,
    )(*ins)


def _group_masks(width):
    lane = lax.broadcasted_iota(jnp.int32, (1, width), 1)
    return [(lane >= HD * h) & (lane < HD * (h + 1)) for h in range(width // HD)]


def _group_mean(v, masks):
    out = jnp.zeros_like(v)
    for m in masks:
        s = jnp.sum(jnp.where(m, v, 0.0), axis=-1, keepdims=True) * (1.0 / HD)
        out = jnp.where(m, s, out)
    return out


def _head_norm(zv, g, masks):
    r = lax.rsqrt(_group_mean(zv * zv, masks) + EPS)
    return zv * r * g


def _head_norm_bwd(dy, zv, g, masks):
    r = lax.rsqrt(_group_mean(zv * zv, masks) + EPS)
    zh = zv * r
    gy = dy * g
    dz = r * (gy - zh * _group_mean(gy * zh, masks))
    return dz, jnp.sum(dy * zh, axis=0, keepdims=True)


def _fold_heads(v, width):
    vb = jnp.broadcast_to(v, (8, width))
    out = vb
    for h in range(1, width // HD):
        out = out + pltpu.roll(vb, width - HD * h, axis=1)
    return out[0:1]


def _bias_expand(u):
    def body(u_ref, o_ref):
        x = jnp.broadcast_to(u_ref[...], (QB, 1024))
        rolled = pltpu.roll(x, 1024 - (QB - 1), axis=1, stride=1, stride_axis=0)[:, :KWIN]
        row = lax.broadcasted_iota(jnp.int32, (QB, 1), 0)
        col = lax.broadcasted_iota(jnp.int32, (1, KWIN), 1)
        lo = (row // 64) * 64
        ok = (col >= lo) & (col < lo + 576)
        o_ref[...] = jnp.where(ok, rolled, NEG)

    return pl.pallas_call(
        body, grid=(12,), in_specs=[BS((None, 1, 1024), lambda h: (h, 0, 0))],
        out_specs=BS((None, QB, KWIN), lambda h: (h, 0, 0)), out_shape=_sds((12, QB, KWIN), F32),
        compiler_params=_cp(("arbitrary",)), name="bias_expand",
    )(u)


def _bias_reduce(ds):
    def body(d_ref, o_ref):
        ri = lax.broadcasted_iota(jnp.int32, (QB, QB), 0)
        ci = lax.broadcasted_iota(jnp.int32, (QB, QB), 1)
        flip = (ri + ci == QB - 1).astype(F32)
        drev = jnp.dot(flip, d_ref[...], precision=lax.Precision.HIGHEST, preferred_element_type=F32)
        x = jnp.concatenate([drev, jnp.zeros((QB, 1024 - KWIN), F32)], axis=1)
        rolled = pltpu.roll(x, 0, axis=1, stride=1, stride_axis=0)
        o_ref[...] = jnp.sum(rolled, axis=0, keepdims=True)

    return pl.pallas_call(
        body, grid=(12,), in_specs=[BS((None, QB, KWIN), lambda h: (h, 0, 0))],
        out_specs=BS((None, 1, 1024), lambda h: (h, 0, 0)), out_shape=_sds((12, 1, 1024), F32),
        compiler_params=_cp(("arbitrary",)), name="bias_reduce",
    )(ds)


AW = 256
AH = AW // HD
AG = TOK // AW
def _attn_softmax(qh, kw, bias, startadd):
    s = lax.dot_general(qh, kw, _DIMS["nt"], preferred_element_type=F32) + bias + startadd
    m = jnp.max(s, axis=-1, keepdims=True)
    p = jnp.exp(s - m)
    return p * (1.0 / jnp.sum(p, axis=-1, keepdims=True))


def _attn_prologue(q_ref, k_ref, v_ref, gq_ref, gk_ref, qn_s, kn_s, v_s, masks):
    kn_s[0:KPAD, :] = jnp.zeros((KPAD, AW), BF16)
    v_s[0:KPAD, :] = jnp.zeros((KPAD, AW), BF16)
    for r in range(0, SEQ, TR):
        qn_s[r:r + TR, :] = (_head_norm(q_ref[r:r + TR, :], gq_ref[...], masks) * SCALE).astype(BF16)
        kn_s[KPAD + r:KPAD + r + TR, :] = _head_norm(k_ref[r:r + TR, :], gk_ref[...], masks).astype(BF16)
        v_s[KPAD + r:KPAD + r + TR, :] = v_ref[r:r + TR, :].astype(BF16)


def _attn_fwd(z, bias, gq2, gk2, carry=None):
    ni = len(carry.ins) if carry else 0
    no = len(carry.out_shapes) if carry else 0

    def body(q_ref, k_ref, v_ref, b_ref, gq_ref, gk_ref, *rest):
        cin, o_ref, cout = rest[:ni], rest[ni], rest[ni + 1:ni + 1 + no]
        qn_s, kn_s, v_s = rest[ni + 1 + no:ni + 4 + no]
        cscr = rest[ni + 4 + no:]
        if carry:
            @pl.when((pl.program_id(0) == 0) & (pl.program_id(1) == 0))
            def _():
                carry.start(cin, cout, cscr)

        masks = _group_masks(AW)
        _attn_prologue(q_ref, k_ref, v_ref, gq_ref, gk_ref, qn_s, kn_s, v_s, masks)
        col = lax.broadcasted_iota(jnp.int32, (1, KWIN), 1)

        def blk(i, carry):
            r0 = pl.multiple_of(i * QB, QB)
            qb = qn_s[pl.ds(r0, QB), :]
            kw = kn_s[pl.ds(r0, KWIN), :]
            vw = v_s[pl.ds(r0, KWIN), :]
            startadd = jnp.where(col + r0 < KPAD, NEG, 0.0)
            o = jnp.zeros((QB, AW), F32)
            for h in range(AH):
                qh = jnp.where(masks[h], qb, jnp.zeros_like(qb))
                vh = jnp.where(masks[h], vw, jnp.zeros_like(vw))
                p = _attn_softmax(qh, kw, b_ref[h], startadd).astype(BF16)
                o = o + jnp.dot(p, vh, preferred_element_type=F32)
            o_ref[pl.ds(r0, QB), :] = o.astype(BF16)
            return carry

        lax.fori_loop(0, SEQ // QB, blk, 0)

        if carry:
            @pl.when((pl.program_id(0) == NB - 1) & (pl.program_id(1) == AG - 1))
            def _():
                carry.finish(cin, cout, cscr)

    vec = BS((1, AW), lambda b, hp: (0, 0))
    outs = pl.pallas_call(
        body, grid=(NB, AG),
        in_specs=[BS((SEQ, AW), lambda b, hp: (b, hp)),
                  BS((SEQ, AW), lambda b, hp: (b, AG + hp)),
                  BS((SEQ, AW), lambda b, hp: (b, 2 * AG + hp)),
                  BS((AH, QB, KWIN), lambda b, hp: (hp, 0, 0)), vec, vec]
        + (carry.in_specs if carry else []),
        out_specs=[BS((SEQ, AW), lambda b, hp: (b, hp))] + [ANY] * no,
        out_shape=[_sds((T, D), BF16)] + (carry.out_shapes if carry else []),
        scratch_shapes=[pltpu.VMEM((SEQ, AW), BF16), pltpu.VMEM((SEQ + KPAD, AW), BF16),
                        pltpu.VMEM((SEQ + KPAD, AW), BF16)] + (carry.scratch if carry else []),
        compiler_params=pltpu.CompilerParams(
            dimension_semantics=("arbitrary", "arbitrary"), vmem_limit_bytes=48 << 20,
            has_side_effects=bool(carry)), name="attn_fwd",
    )(z, z, z, bias, gq2, gk2, *(carry.ins if carry else []))
    return outs[0], outs[1:]


def _attn_bwd(z, dcat, bias, gq2, gk2, carry=None):
    ni = len(carry.ins) if carry else 0
    no = len(carry.out_shapes) if carry else 0

    def body(q_ref, k_ref, v_ref, do_ref, b_ref, gq_ref, gk_ref, *rest):
        cin = rest[:ni]
        dq_ref, dk_ref, dv_ref, db_ref, dgq_ref, dgk_ref = rest[ni:ni + 6]
        cout = rest[ni + 6:ni + 6 + no]
        qn_s, kn_s, v_s, dqn_s, dkn_s, dv_s = rest[ni + 6 + no:ni + 12 + no]
        cscr = rest[ni + 12 + no:]
        hp = pl.program_id(0)
        b = pl.program_id(1)
        if carry:
            @pl.when((hp == 0) & (b == 0))
            def _():
                carry.start(cin, cout, cscr)

        masks = _group_masks(AW)
        _attn_prologue(q_ref, k_ref, v_ref, gq_ref, gk_ref, qn_s, kn_s, v_s, masks)
        dkn_s[...] = jnp.zeros_like(dkn_s)
        dv_s[...] = jnp.zeros_like(dv_s)

        @pl.when(b == 0)
        def _():
            db_ref[...] = jnp.zeros_like(db_ref)

        @pl.when((b == 0) & (hp == 0))
        def _():
            dgq_ref[...] = jnp.zeros_like(dgq_ref)
            dgk_ref[...] = jnp.zeros_like(dgk_ref)

        col = lax.broadcasted_iota(jnp.int32, (1, KWIN), 1)

        def blk(i, carry):
            r0 = pl.multiple_of(i * QB, QB)
            qb = qn_s[pl.ds(r0, QB), :]
            kw = kn_s[pl.ds(r0, KWIN), :]
            vw = v_s[pl.ds(r0, KWIN), :]
            dob = do_ref[pl.ds(r0, QB), :].astype(BF16)
            startadd = jnp.where(col + r0 < KPAD, NEG, 0.0)
            dqn = jnp.zeros((QB, AW), F32)
            dkw = jnp.zeros((KWIN, AW), F32)
            dvw = jnp.zeros((KWIN, AW), F32)
            for h in range(AH):
                qh = jnp.where(masks[h], qb, jnp.zeros_like(qb))
                kh = jnp.where(masks[h], kw, jnp.zeros_like(kw))
                doh = jnp.where(masks[h], dob, jnp.zeros_like(dob))
                p = _attn_softmax(qh, kw, b_ref[h], startadd)
                dvw = dvw + lax.dot_general(p.astype(BF16), doh, _DIMS["tn"],
                                            preferred_element_type=F32)
                dp = lax.dot_general(doh, vw, _DIMS["nt"], preferred_element_type=F32)
                ds = p * (dp - jnp.sum(dp * p, axis=-1, keepdims=True))
                db_ref[h] += ds
                dsb = ds.astype(BF16)
                dqn = dqn + jnp.dot(dsb, kh, preferred_element_type=F32)
                dkw = dkw + lax.dot_general(dsb, qh, _DIMS["tn"], preferred_element_type=F32)
            dqn_s[pl.ds(r0, QB), :] = dqn * SCALE
            dkn_s[pl.ds(r0, KWIN), :] += dkw
            dv_s[pl.ds(r0, KWIN), :] += dvw
            return carry

        lax.fori_loop(0, SEQ // QB, blk, 0)

        dgq = jnp.zeros((1, AW), F32)
        dgk = jnp.zeros((1, AW), F32)
        for r in range(0, SEQ, TR):
            dq, dg = _head_norm_bwd(dqn_s[r:r + TR, :], q_ref[r:r + TR, :], gq_ref[...], masks)
            dq_ref[r:r + TR, :] = dq.astype(BF16)
            dgq = dgq + dg
            dk, dg = _head_norm_bwd(dkn_s[KPAD + r:KPAD + r + TR, :], k_ref[r:r + TR, :], gk_ref[...], masks)
            dk_ref[r:r + TR, :] = dk.astype(BF16)
            dgk = dgk + dg
            dv_ref[r:r + TR, :] = dv_s[KPAD + r:KPAD + r + TR, :].astype(BF16)
        dgq_ref[...] += _fold_heads(dgq, AW)
        dgk_ref[...] += _fold_heads(dgk, AW)

        if carry:
            @pl.when((hp == AG - 1) & (b == NB - 1))
            def _():
                carry.finish(cin, cout, cscr)

    vec = BS((1, AW), lambda hp, b: (0, 0))
    row = BS((SEQ, AW), lambda hp, b: (b, hp))
    outs = pl.pallas_call(
        body, grid=(AG, NB),
        in_specs=[row,
                  BS((SEQ, AW), lambda hp, b: (b, AG + hp)),
                  BS((SEQ, AW), lambda hp, b: (b, 2 * AG + hp)),
                  row,
                  BS((AH, QB, KWIN), lambda hp, b: (hp, 0, 0)), vec, vec]
        + (carry.in_specs if carry else []),
        out_specs=[row, row, row, BS((AH, QB, KWIN), lambda hp, b: (hp, 0, 0)), vec, vec] + [ANY] * no,
        out_shape=[_sds((T, TOK), BF16), _sds((T, TOK), BF16), _sds((T, TOK), BF16),
                   _sds((12, QB, KWIN), F32), _sds((1, AW), F32), _sds((1, AW), F32)]
        + (carry.out_shapes if carry else []),
        scratch_shapes=[pltpu.VMEM((SEQ, AW), BF16), pltpu.VMEM((SEQ + KPAD, AW), BF16),
                        pltpu.VMEM((SEQ + KPAD, AW), BF16), pltpu.VMEM((SEQ, AW), F32),
                        pltpu.VMEM((SEQ + KPAD, AW), F32), pltpu.VMEM((SEQ + KPAD, AW), F32)]
        + (carry.scratch if carry else []),
        compiler_params=pltpu.CompilerParams(
            dimension_semantics=("arbitrary", "arbitrary"), vmem_limit_bytes=58 << 20,
            has_side_effects=bool(carry)), name="attn_bwd",
    )(z, z, z, dcat, bias, gq2, gk2, *(carry.ins if carry else []))
    return outs[:6], outs[6:]


def _mem_softmax(qh, kn):
    s = lax.dot_general(qh, kn, _DIMS["nt"], preferred_element_type=F32)
    m = jnp.max(s, axis=-1, keepdims=True)
    p = jnp.exp(s - m)
    return p * (1.0 / jnp.sum(p, axis=-1, keepdims=True))


def _memattn_fwd(z, kv, cat, gq4, gk4, qcol, name):
    def body(q_ref, k_ref, v_ref, gq_ref, gk_ref, cat_ref, o_ref):
        del cat_ref
        masks = _group_masks(MEMW)
        qn = (_head_norm(q_ref[...], gq_ref[...], masks) * SCALE).astype(BF16)
        kn = _head_norm(k_ref[...], gk_ref[...], masks).astype(BF16)
        vv = v_ref[...].astype(BF16)
        o = jnp.zeros((TR, MEMW), F32)
        for h in range(4):
            qh = jnp.where(masks[h], qn, jnp.zeros_like(qn))
            vh = jnp.where(masks[h], vv, jnp.zeros_like(vv))
            p = _mem_softmax(qh, kn).astype(BF16)
            o = o + jnp.dot(p, vh, preferred_element_type=F32)
        o_ref[...] = o.astype(BF16)

    nt = SEQ // TR
    vec = BS((1, MEMW), lambda b, t: (0, 0))
    return pl.pallas_call(
        body, grid=(NB, nt),
        in_specs=[BS((TR, MEMW), lambda b, t: (b * nt + t, qcol)),
                  BS((MEMT, MEMW), lambda b, t: (b, 0)),
                  BS((MEMT, MEMW), lambda b, t: (b, 1)), vec, vec, ANY],
        out_specs=BS((TR, MEMW), lambda b, t: (b * nt + t, 3)),
        out_shape=_sds((T, D), BF16), input_output_aliases={5: 0},
        compiler_params=_cp(("arbitrary", "arbitrary")), name=name,
    )(z, kv, kv, gq4, gk4, cat)


def _memattn_bwd(z, kv, dcat, gq4, gk4, qcol, name):
    nt = SEQ // TR

    def body(q_ref, k_ref, v_ref, do_ref, gq_ref, gk_ref,
             dq_ref, dkv_ref, dgq_ref, dgk_ref, dkn_s, dv_s):
        b = pl.program_id(0)
        t = pl.program_id(1)
        masks = _group_masks(MEMW)
        qz = q_ref[...]
        kz = k_ref[...]
        qn = (_head_norm(qz, gq_ref[...], masks) * SCALE).astype(BF16)
        kn = _head_norm(kz, gk_ref[...], masks).astype(BF16)
        vv = v_ref[...].astype(BF16)
        dob = do_ref[...].astype(BF16)

        @pl.when(t == 0)
        def _():
            dkn_s[...] = jnp.zeros_like(dkn_s)
            dv_s[...] = jnp.zeros_like(dv_s)

        @pl.when((t == 0) & (b == 0))
        def _():
            dgq_ref[...] = jnp.zeros_like(dgq_ref)
            dgk_ref[...] = jnp.zeros_like(dgk_ref)

        dqn = jnp.zeros((TR, MEMW), F32)
        dkn = jnp.zeros((MEMT, MEMW), F32)
        dvv = jnp.zeros((MEMT, MEMW), F32)
        for h in range(4):
            qh = jnp.where(masks[h], qn, jnp.zeros_like(qn))
            kh = jnp.where(masks[h], kn, jnp.zeros_like(kn))
            doh = jnp.where(masks[h], dob, jnp.zeros_like(dob))
            p = _mem_softmax(qh, kn)
            dvv = dvv + lax.dot_general(p.astype(BF16), doh, _DIMS["tn"], preferred_element_type=F32)
            dp = lax.dot_general(doh, vv, _DIMS["nt"], preferred_element_type=F32)
            ds = p * (dp - jnp.sum(dp * p, axis=-1, keepdims=True))
            dsb = ds.astype(BF16)
            dqn = dqn + jnp.dot(dsb, kh, preferred_element_type=F32)
            dkn = dkn + lax.dot_general(dsb, qh, _DIMS["tn"], preferred_element_type=F32)
        dkn_s[...] += dkn
        dv_s[...] += dvv
        dq, dgq = _head_norm_bwd(dqn * SCALE, qz, gq_ref[...], masks)
        dq_ref[...] = dq.astype(BF16)
        dgq_ref[...] += _fold_heads(dgq, MEMW)

        @pl.when(t == nt - 1)
        def _():
            dk, dgk = _head_norm_bwd(dkn_s[...], kz, gk_ref[...], masks)
            dkv_ref[:, 0:MEMW] = dk
            dkv_ref[:, MEMW:] = dv_s[...]
            dgk_ref[...] += _fold_heads(dgk, MEMW)

    vec = BS((1, MEMW), lambda b, t: (0, 0))
    return pl.pallas_call(
        body, grid=(NB, nt),
        in_specs=[BS((TR, MEMW), lambda b, t: (b * nt + t, qcol)),
                  BS((MEMT, MEMW), lambda b, t: (b, 0)),
                  BS((MEMT, MEMW), lambda b, t: (b, 1)),
                  BS((TR, MEMW), lambda b, t: (b * nt + t, 3)), vec, vec],
        out_specs=[BS((TR, MEMW), lambda b, t: (b * nt + t, 0)),
                   BS((MEMT, 2 * MEMW), lambda b, t: (b, 0)), vec, vec],
        out_shape=[_sds((T, MEMW), BF16), _sds((NB * MEMT, 2 * MEMW), F32),
                   _sds((1, MEMW), F32), _sds((1, MEMW), F32)],
        scratch_shapes=[pltpu.VMEM((MEMT, MEMW), F32), pltpu.VMEM((MEMT, MEMW), F32)],
        compiler_params=_cp(("arbitrary", "arbitrary")), name=name,
    )(z, kv, kv, dcat, gq4, gk4)


HALO = 32
NEXT = 64
RT = 64


def _glu(zz):
    return zz[:, :TOK] * jax.nn.sigmoid(zz[:, TOK:])


def _layer_norm_parts(y):
    mu = jnp.mean(y, axis=-1, keepdims=True)
    yc = y - mu
    rstd = lax.rsqrt(jnp.mean(yc * yc, axis=-1, keepdims=True) + EPS)
    return yc * rstd, rstd


def _shifted_copies(src, dst, rows):
    for b in range(1, 8):
        dst[b - 1, 0:rows, :] = src[b:b + rows, :]


def _tap(src, shifted, off, r0, rows):
    b = off % 8
    if b == 0:
        return src[r0 + off:r0 + off + rows, :]
    return shifted[b - 1, r0 + off - b:r0 + off - b + rows, :]


def _conv_rows(w_ref, hbuf, hs, r0, rows):
    y = jnp.zeros((rows, TOK), F32)
    for j in range(CONVW):
        y = y + w_ref[j:j + 1, :] * _tap(hbuf, hs, (HALO - CONVW + 1) + j, r0, rows)
    return y


def _conv_fwd(z, cw, cb, lg, lb):
    nt = SEQ // TR

    def body(zc_ref, zp_ref, w_ref, cb_ref, lg_ref, lb_ref, o_ref, hbuf, hs):
        t = pl.program_id(1)
        hbuf[0:HALO, :] = jnp.where(t == 0, 0.0, _glu(zp_ref[...]))
        hbuf[HALO:, :] = _glu(zc_ref[...])
        _shifted_copies(hbuf, hs, HALO + TR - 8)
        for r0 in range(0, TR, RT):
            y = _conv_rows(w_ref, hbuf, hs, r0, RT) + cb_ref[...]
            yh, _ = _layer_norm_parts(y)
            o = yh * lg_ref[...] + lb_ref[...]
            o_ref[r0:r0 + RT, :] = (o * jax.nn.sigmoid(o)).astype(BF16)

    vec = BS((1, TOK), lambda b, t: (0, 0))
    per = TR // HALO
    return pl.pallas_call(
        body, grid=(NB, nt),
        in_specs=[BS((TR, 2 * TOK), lambda b, t: (b * nt + t, 0)),
                  BS((HALO, 2 * TOK), lambda b, t: (jnp.maximum((b * nt + t) * per - 1, 0), 0)),
                  BS((32, TOK), lambda b, t: (0, 0)), vec, vec, vec],
        out_specs=BS((TR, TOK), lambda b, t: (b * nt + t, 0)),
        out_shape=_sds((T, D), BF16),
        scratch_shapes=[pltpu.VMEM((HALO + TR, TOK), F32), pltpu.VMEM((7, HALO + TR, TOK), F32)],
        compiler_params=_cp(("arbitrary", "arbitrary")), name="conv_fwd",
    )(z, z, cw, cb, lg, lb)


def _conv_bwd(z, dcat, cw, cb, lg, lb):
    nt = SEQ // TR
    ext = TR + NEXT

    def body(zc_ref, zp_ref, zn_ref, dc_ref, dn_ref, w_ref, cb_ref, lg_ref, lb_ref,
             du_ref, dw_ref, dcb_ref, dlg_ref, dlb_ref, dbin_ref, hbuf, dybuf, hs, dys):
        b = pl.program_id(0)
        t = pl.program_id(1)

        @pl.when((b == 0) & (t == 0))
        def _():
            dw_ref[...] = jnp.zeros_like(dw_ref)
            dcb_ref[...] = jnp.zeros_like(dcb_ref)
            dlg_ref[...] = jnp.zeros_like(dlg_ref)
            dlb_ref[...] = jnp.zeros_like(dlb_ref)
            dbin_ref[...] = jnp.zeros_like(dbin_ref)

        hbuf[0:HALO, :] = jnp.where(t == 0, 0.0, _glu(zp_ref[...]))
        hbuf[HALO:HALO + TR, :] = _glu(zc_ref[...])
        hbuf[HALO + TR:, :] = _glu(zn_ref[...])
        _shifted_copies(hbuf, hs, HALO + TR + NEXT - 8)
        last = t == nt - 1
        for r0 in range(0, ext, RT):
            y = _conv_rows(w_ref, hbuf, hs, r0, RT) + cb_ref[...]
            yh, rstd = _layer_norm_parts(y)
            o = yh * lg_ref[...] + lb_ref[...]
            sg = jax.nn.sigmoid(o)
            if r0 < TR:
                dtok = dc_ref[r0:r0 + RT, :]
            else:
                dtok = jnp.where(last, 0.0, dn_ref[...])
            do = dtok * (sg * (1.0 + o * (1.0 - sg)))
            dyh = do * lg_ref[...]
            dy = rstd * (dyh - jnp.mean(dyh, axis=-1, keepdims=True)
                         - yh * jnp.mean(dyh * yh, axis=-1, keepdims=True))
            dybuf[r0:r0 + RT, :] = dy
            if r0 < TR:
                dlg_ref[...] += jnp.sum(do * yh, axis=0, keepdims=True)
                dlb_ref[...] += jnp.sum(do, axis=0, keepdims=True)
                dcb_ref[...] += jnp.sum(dy, axis=0, keepdims=True)
        _shifted_copies(dybuf, dys, ext - 8)
        for r0 in range(0, TR, RT):
            dh = jnp.zeros((RT, TOK), F32)
            for j in range(CONVW):
                dh = dh + w_ref[j:j + 1, :] * _tap(dybuf, dys, (CONVW - 1) - j, r0, RT)
            a = zc_ref[r0:r0 + RT, 0:TOK]
            sg = jax.nn.sigmoid(zc_ref[r0:r0 + RT, TOK:])
            da = dh * sg
            dg = dh * a * (sg * (1.0 - sg))
            du_ref[r0:r0 + RT, 0:TOK] = da.astype(BF16)
            du_ref[r0:r0 + RT, TOK:] = dg.astype(BF16)
            dbin_ref[:, 0:TOK] += jnp.sum(da, axis=0, keepdims=True)
            dbin_ref[:, TOK:] += jnp.sum(dg, axis=0, keepdims=True)
        for j in range(CONVW):
            acc = jnp.zeros((8, TOK), F32)
            for r0 in range(0, TR, RT):
                prod = dybuf[r0:r0 + RT, :] * _tap(hbuf, hs, (HALO - CONVW + 1) + j, r0, RT)
                acc = acc + jnp.sum(prod.reshape(RT // 8, 8, TOK), axis=0)
            dw_ref[j:j + 1, :] += jnp.sum(acc, axis=0, keepdims=True)

    vec = BS((1, TOK), lambda b, t: (0, 0))
    perh = TR // HALO
    pern = TR // NEXT
    nlast_n = T // NEXT - 1
    return pl.pallas_call(
        body, grid=(NB, nt),
        in_specs=[BS((TR, 2 * TOK), lambda b, t: (b * nt + t, 0)),
                  BS((HALO, 2 * TOK), lambda b, t: (jnp.maximum((b * nt + t) * perh - 1, 0), 0)),
                  BS((NEXT, 2 * TOK), lambda b, t: (jnp.minimum((b * nt + t + 1) * pern, nlast_n), 0)),
                  BS((TR, TOK), lambda b, t: (b * nt + t, 0)),
                  BS((NEXT, TOK), lambda b, t: (jnp.minimum((b * nt + t + 1) * pern, nlast_n), 0)),
                  BS((32, TOK), lambda b, t: (0, 0)), vec, vec, vec],
        out_specs=[BS((TR, 2 * TOK), lambda b, t: (b * nt + t, 0)),
                   BS((32, TOK), lambda b, t: (0, 0)), vec, vec, vec,
                   BS((1, 2 * TOK), lambda b, t: (0, 0))],
        out_shape=[_sds((T, 2 * TOK), BF16), _sds((32, TOK), F32), _sds((1, TOK), F32),
                   _sds((1, TOK), F32), _sds((1, TOK), F32), _sds((1, 2 * TOK), F32)],
        scratch_shapes=[pltpu.VMEM((HALO + TR + NEXT, TOK), F32), pltpu.VMEM((ext, TOK), F32),
                        pltpu.VMEM((7, HALO + TR + NEXT, TOK), F32), pltpu.VMEM((7, ext, TOK), F32)],
        compiler_params=_cp(("arbitrary", "arbitrary"), vmem_mb=56), name="conv_bwd",
    )(z, z, z, dcat, dcat, cw, cb, lg, lb)


def _ffn_up(h2, wgu, l, carry=None):
    ni = len(carry.ins) if carry else 0
    no = len(carry.out_shapes) if carry else 0

    def body(h_ref, wg_ref, wu_ref, *rest):
        cin = rest[:ni]
        g_ref, u_ref, a_ref = rest[ni:ni + 3]
        cout, cscr = rest[ni + 3:ni + 3 + no], rest[ni + 3 + no:]
        if carry:
            @pl.when((pl.program_id(0) == 0) & (pl.program_id(1) == 0))
            def _():
                carry.start(cin, cout, cscr)

        hv = h_ref[...]
        g = lax.dot_general(hv, wg_ref[...], _DIMS["nt"], preferred_element_type=F32)
        u = lax.dot_general(hv, wu_ref[...], _DIMS["nt"], preferred_element_type=F32)
        sg = jax.nn.sigmoid(g)
        silu = g * sg
        g_ref[...] = (u * (sg * (1.0 + g * (1.0 - sg)))).astype(BF16)
        u_ref[...] = silu.astype(BF16)
        a_ref[...] = (silu * u).astype(BF16)

        if carry:
            @pl.when((pl.program_id(0) == FF // FT - 1) & (pl.program_id(1) == T // TR - 1))
            def _():
                carry.finish(cin, cout, cscr)

    out = BS((TR, FT), lambda q, i: (i, q))
    outs = pl.pallas_call(
        body, grid=(FF // FT, T // TR),
        in_specs=[BS((TR, D), lambda q, i: (i, 0)),
                  BS((None, FT, D), lambda q, i: (0, q, 0)),
                  BS((None, FT, D), lambda q, i: (1, q, 0))] + (carry.in_specs if carry else []),
        out_specs=[out, out, out] + [ANY] * no,
        out_shape=[_sds((T, FF), BF16), _sds((T, FF), BF16), _sds((T, FF), BF16)]
        + (carry.out_shapes if carry else []),
        scratch_shapes=carry.scratch if carry else [],
        compiler_params=pltpu.CompilerParams(
            dimension_semantics=("arbitrary", "arbitrary"), vmem_limit_bytes=48 << 20,
            has_side_effects=bool(carry)), name=f"ffn_up_{l}",
    )(h2, wgu, wgu, *(carry.ins if carry else []))
    return outs[:3], outs[3:]


def _ffn_down_bwd(dx, wd, g, u, l):
    def epilogue(dact, ex, o_ref, i):
        o_ref[0] = (dact * ex[0][...].astype(F32)).astype(BF16)
        o_ref[1] = (dact * ex[1][...].astype(F32)).astype(BF16)

    ex_spec = BS((TR, FT), lambda i, q, k: (i, q))
    return _mm("nt", dx, wd, grid=(T // TR, FF // FT, 1),
               a_spec=BS((TR, D), lambda i, q, k: (i, 0)),
               b_spec=BS((FT, D), lambda i, q, k: (q, 0)),
               out_shape=_sds((2, T, FF), BF16),
               out_spec=BS((2, TR, FT), lambda i, q, k: (0, i, q)),
               acc_shape=(TR, FT), extras=(g, u), extra_specs=(ex_spec, ex_spec),
               epilogue=epilogue, name=f"ffn_down_bwd_{l}")


def _row_tile(rows, cols, itemsize=4, limit=2 << 20):
    tr = rows
    while tr * cols * itemsize > limit and tr % 2 == 0 and (tr // 2) % 16 == 0:
        tr //= 2
    return tr


def _cast_bf16(arrs, name):
    n = len(arrs)
    rows, cols = arrs[0].shape
    tr = _row_tile(rows, cols)

    def body(*refs):
        o_ref = refs[n]
        k = pl.program_id(0)
        val = refs[0][...]
        for j in range(1, n):
            val = jnp.where(k == j, refs[j][...], val)
        o_ref[...] = val.astype(BF16)

    return pl.pallas_call(
        body, grid=(n, rows // tr),
        in_specs=[BS((tr, cols), lambda k, i: (i, 0))] * n,
        out_specs=BS((None, tr, cols), lambda k, i: (k, i, 0)),
        out_shape=_sds((n, rows, cols), BF16),
        compiler_params=_cp(("arbitrary", "arbitrary")), name=name,
    )(*arrs)


def _quad_sum(own, got, name):
    n, rows, cols = own.shape
    tr = _row_tile(rows, cols)

    def body(a_ref, q_ref, o_ref):
        o_ref[...] = ((a_ref[...] + q_ref[0].astype(F32)) + q_ref[1].astype(F32)) + q_ref[2].astype(F32)

    spec = BS((None, tr, cols), lambda k, i: (k, i, 0))
    return pl.pallas_call(
        body, grid=(n, rows // tr),
        in_specs=[spec, BS((3, None, tr, cols), lambda k, i: (0, k, i, 0))], out_specs=spec,
        out_shape=_sds((n, rows, cols), F32),
        compiler_params=_cp(("arbitrary", "arbitrary")), name=name,
    )(own, got)


def _adam_math(w, g, m, v):
    m = ADAM_B1 * m + (1.0 - ADAM_B1) * g
    v = ADAM_B2 * v + (1.0 - ADAM_B2) * (g * g)
    m_hat = m / (1.0 - ADAM_B1 ** ADAM_STEP)
    v_hat = v / (1.0 - ADAM_B2 ** ADAM_STEP)
    delta = -ADAM_LR * (m_hat / (jnp.sqrt(v_hat) + ADAM_EPS) + ADAM_WD * w)
    return delta, m, v


def _adamw_big(w, g, m, v, name):
    shape = w.shape
    cols = shape[-1]
    rows = w.size // cols
    tr = _row_tile(rows, cols, limit=1 << 20)

    def body(w_ref, g_ref, m_ref, v_ref, d_ref, nm_ref, nv_ref):
        d, nm, nv = _adam_math(w_ref[...], g_ref[...], m_ref[...], v_ref[...])
        d_ref[...] = d
        nm_ref[...] = nm
        nv_ref[...] = nv

    spec = BS((tr, cols), lambda i: (i, 0))
    outs = pl.pallas_call(
        body, grid=(rows // tr,), in_specs=[spec] * 4, out_specs=[spec] * 3,
        out_shape=[_sds((rows, cols), F32)] * 3,
        compiler_params=_cp(("arbitrary",)), name=name,
    )(*[a.reshape(rows, cols) for a in (w, g, m, v)])
    return [o.reshape(shape) for o in outs]


def _adamw_small(ws, gs, ms, vs):
    n = len(ws)

    def body(*refs):
        for i in range(n):
            d, nm, nv = _adam_math(refs[i][...], refs[n + i][...], refs[2 * n + i][...],
                                   refs[3 * n + i][...])
            refs[4 * n + i][...] = d
            refs[5 * n + i][...] = nm
            refs[6 * n + i][...] = nv

    specs = [BS(w.shape, lambda i: (0, 0)) for w in ws]
    outs = pl.pallas_call(
        body, grid=(1,), in_specs=specs * 4, out_specs=specs * 3,
        out_shape=[_sds(w.shape, F32) for w in ws] * 3,
        compiler_params=_cp(("arbitrary",)), name="adamw_small",
    )(*ws, *gs, *ms, *vs)
    return outs[:n], outs[n:2 * n], outs[2 * n:]


def _place():
    x, y, c = lax.axis_index("x"), lax.axis_index("y"), lax.axis_index("c")
    chips = [(1 - x, y), (x, 1 - y), (1 - x, 1 - y)]
    return x, y, c, chips


class _Exchange:
    def __init__(self, ins, in_specs, out_shapes, scratch, start, finish):
        self.ins, self.in_specs, self.out_shapes, self.scratch = ins, in_specs, out_shapes, scratch
        self.start, self.finish = start, finish


def _run_exchange(ex, name, vmem_mb=40):
    ni, no = len(ex.ins), len(ex.out_shapes)

    def body(*refs):
        ex.start(refs[:ni], refs[ni:ni + no], refs[ni + no:])
        ex.finish(refs[:ni], refs[ni:ni + no], refs[ni + no:])

    return pl.pallas_call(
        body, in_specs=ex.in_specs, out_specs=[ANY] * no, out_shape=ex.out_shapes,
        scratch_shapes=ex.scratch,
        compiler_params=pltpu.CompilerParams(has_side_effects=True, vmem_limit_bytes=vmem_mb << 20),
        name=name,
    )(*ex.ins)


def _gather_exchange(srcs, dst_shapes, views, small=None):
    nu = len(srcs)
    nd = len(dst_shapes)
    ns = 1 if small is not None else 0

    def unpack(ins, outs, scr):
        x, y, c, chips = _place()
        src = ins[:nu]
        vw = [views[u](outs[:nd]) for u in range(nu)]
        vbuf = scr[:nu]
        send, recv, fsend, frecv, lsem, ssend, srecv, vsem = scr[nu:]

        def ici(u, j, shard, to):
            return pltpu.make_async_remote_copy(
                src_ref=vbuf[u].at[:, c], dst_ref=vw[u].at[:, shard, c],
                send_sem=send.at[3 * u + j], recv_sem=recv.at[3 * u + j],
                device_id=to, device_id_type=MESH)

        def fwd(u, j, shard, half):
            return pltpu.make_async_remote_copy(
                src_ref=vw[u].at[:, shard, half], dst_ref=vw[u].at[:, shard, half],
                send_sem=fsend.at[3 * u + j], recv_sem=frecv.at[3 * u + j],
                device_id=(x, y, 1 - c), device_id_type=MESH)

        def small_copy(j, shard, to):
            return pltpu.make_async_remote_copy(
                src_ref=ins[nu], dst_ref=outs[nd].at[shard],
                send_sem=ssend.at[j], recv_sem=srecv.at[j], device_id=to, device_id_type=MESH)

        stage = [pltpu.make_async_copy(src[u], vbuf[u], vsem.at[u]) for u in range(nu)]
        local = [pltpu.make_async_copy(vbuf[u], vw[u].at[:, 2 * x + y], lsem.at[u]) for u in range(nu)]
        if ns:
            local.append(pltpu.make_async_copy(ins[nu], outs[nd].at[2 * x + y], lsem.at[nu]))
        return x, y, c, chips, ici, fwd, small_copy, stage, local

    def start(ins, outs, scr):
        x, y, c, chips, ici, fwd, small_copy, stage, local = unpack(ins, outs, scr)
        s = 2 * x + y
        for cp in stage:
            cp.start()
        if ns:
            local[nu].start()
            for j, chip in enumerate(chips):
                small_copy(j, s, (*chip, c)).start()
        for u in range(nu):
            stage[u].wait()
            for j, chip in enumerate(chips):
                ici(u, j, s, (*chip, c)).start()
            local[u].start()

    def finish(ins, outs, scr):
        x, y, c, chips, ici, fwd, small_copy, stage, local = unpack(ins, outs, scr)
        s = 2 * x + y
        for u in range(nu):
            for j, chip in enumerate(chips):
                sj = 2 * chip[0] + chip[1]
                ici(u, j, sj, (x, y, c)).wait_recv()
                fwd(u, j, sj, c).start()
        for u in range(nu):
            for j, chip in enumerate(chips):
                sj = 2 * chip[0] + chip[1]
                fwd(u, j, sj, 1 - c).wait_recv()
        for u in range(nu):
            for j, chip in enumerate(chips):
                ici(u, j, s, (*chip, c)).wait_send()
                fwd(u, j, s, c).wait_send()
        if ns:
            for j, chip in enumerate(chips):
                small_copy(j, 2 * chip[0] + chip[1], (x, y, c)).wait_recv()
                small_copy(j, s, (*chip, c)).wait_send()
        for cp in local:
            cp.wait()

    dma = pltpu.SemaphoreType.DMA
    return _Exchange(
        ins=list(srcs) + ([small] if ns else []),
        in_specs=[ANY] * nu + [BS(memory_space=pltpu.VMEM)] * ns,
        out_shapes=[_sds(sh, BF16) for sh in dst_shapes]
        + ([_sds((NSH,) + small.shape, F32)] if ns else []),
        scratch=[pltpu.VMEM(a.shape, BF16) for a in srcs]
        + [dma((3 * nu,)), dma((3 * nu,)), dma((3 * nu,)), dma((3 * nu,)),
           dma((nu + 1,)), dma((3,)), dma((3,)), dma((nu,))],
        start=start, finish=finish)


def _pair_reduce(g, name):
    pn, _, _, rh, cc = g.shape
    n = pn * NSH

    def body(g_ref, own_ref, sb_ref, sendb, recvb, stage, outf, outb, send, recv, lsem, osem):
        x, y, c, _ = _place()
        s = 2 * x + y

        def load(k, half):
            return pltpu.make_async_copy(g_ref.at[k // NSH, k % NSH, half], stage.at[k % 2], lsem.at[k % 2])

        def push(k):
            return pltpu.make_async_remote_copy(
                src_ref=sendb.at[k], dst_ref=recvb.at[k], send_sem=send.at[k], recv_sem=recv.at[k],
                device_id=(x, y, 1 - c), device_id_type=MESH)

        def store(k):
            return pltpu.make_async_copy(outb.at[k % 2], sb_ref.at[k // NSH, k % NSH], osem.at[k % 2])

        load(0, 1 - c).start()
        for k in range(n):
            if k + 1 < n:
                load(k + 1, 1 - c).start()
            load(k, 1 - c).wait()
            sendb[k] = stage[k % 2].astype(BF16)
            push(k).start()
        load(0, c).start()
        for k in range(n):
            if k + 1 < n:
                load(k + 1, c).start()
            load(k, c).wait()
            push(k).wait_recv()
            total = stage[k % 2] + recvb[k].astype(F32)
            if k >= 2:
                store(k - 2).wait()
            outb[k % 2] = total.astype(BF16)
            store(k).start()

            @pl.when(s == k % NSH)
            def _():
                outf[...] = total
                keep = pltpu.make_async_copy(outf, own_ref.at[k // NSH], osem.at[2])
                keep.start()
                keep.wait()

        for k in range(max(n - 2, 0), n):
            store(k).wait()
        for k in range(n):
            push(k).wait_send()

    dma = pltpu.SemaphoreType.DMA
    return pl.pallas_call(
        body, in_specs=[ANY], out_specs=[ANY, ANY],
        out_shape=[_sds((pn, rh, cc), F32), _sds((pn, NSH, rh, cc), BF16)],
        scratch_shapes=[pltpu.VMEM((n, rh, cc), BF16), pltpu.VMEM((n, rh, cc), BF16),
                        pltpu.VMEM((2, rh, cc), F32), pltpu.VMEM((rh, cc), F32),
                        pltpu.VMEM((2, rh, cc), BF16), dma((n,)), dma((n,)), dma((2,)), dma((3,))],
        compiler_params=pltpu.CompilerParams(has_side_effects=True, vmem_limit_bytes=56 << 20),
        name=name,
    )(g)


def _chip_exchange(sums_bf16):
    nu = len(sums_bf16)

    def pushes(ins, outs, scr):
        x, y, c, chips = _place()
        send, recv = scr
        return [pltpu.make_async_remote_copy(
            src_ref=ins[u].at[:, 2 * chip[0] + chip[1]], dst_ref=outs[u].at[j],
            send_sem=send.at[3 * u + j], recv_sem=recv.at[3 * u + j],
            device_id=(*chip, c), device_id_type=MESH)
            for u in range(nu) for j, chip in enumerate(chips)]

    def start(ins, outs, scr):
        for cp in pushes(ins, outs, scr):
            cp.start()

    def finish(ins, outs, scr):
        for cp in pushes(ins, outs, scr):
            cp.wait()

    dma = pltpu.SemaphoreType.DMA
    shapes = [(3, a.shape[0], a.shape[2], a.shape[3]) for a in sums_bf16]
    return _Exchange(ins=list(sums_bf16), in_specs=[ANY] * nu,
                     out_shapes=[_sds(sh, BF16) for sh in shapes],
                     scratch=[dma((3 * nu,)), dma((3 * nu,))], start=start, finish=finish)


def _final_exchange(halves, out_shapes, targets):
    nu = len(halves)
    no = len(out_shapes)
    ncp = sum(len(t) for t in targets)

    def body(*refs):
        hv = refs[:nu]
        out = refs[nu:nu + no]
        sbuf = refs[nu + no:2 * nu + no]
        rbuf = refs[2 * nu + no:3 * nu + no]
        send, recv, lsem, osem, csem = refs[3 * nu + no:]
        x, y, c, _ = _place()
        stage = [pltpu.make_async_copy(hv[u], sbuf[u], lsem.at[u]) for u in range(nu)]
        push = [pltpu.make_async_remote_copy(
            src_ref=sbuf[u], dst_ref=rbuf[u], send_sem=send.at[u], recv_sem=recv.at[u],
            device_id=(x, y, 1 - c), device_id_type=MESH) for u in range(nu)]
        mine, theirs = [], []
        k = 0
        for u in range(nu):
            rh = hv[u].shape[1]
            for (p, oi, li) in targets[u]:
                mine.append((u, pltpu.make_async_copy(
                    sbuf[u].at[p], out[oi].at[li, pl.ds(c * rh, rh), :], csem.at[k])))
                theirs.append((u, pltpu.make_async_copy(
                    rbuf[u].at[p], out[oi].at[li, pl.ds((1 - c) * rh, rh), :], osem.at[k])))
                k += 1
        for cp in stage:
            cp.start()
        for u in range(nu):
            stage[u].wait()
            push[u].start()
            for (v, cp) in mine:
                if v == u:
                    cp.start()
        for u in range(nu):
            push[u].wait_recv()
            for (v, cp) in theirs:
                if v == u:
                    cp.start()
        for (_, cp) in theirs + mine:
            cp.wait()
        for u in range(nu):
            push[u].wait_send()

    dma = pltpu.SemaphoreType.DMA
    bufs = [pltpu.VMEM(h.shape, F32) for h in halves]
    return pl.pallas_call(
        body, in_specs=[ANY] * nu, out_specs=[ANY] * no,
        out_shape=[_sds(sh, F32) for sh in out_shapes],
        scratch_shapes=bufs + bufs + [dma((nu,)), dma((nu,)), dma((nu,)), dma((ncp,)), dma((ncp,))],
        compiler_params=pltpu.CompilerParams(has_side_effects=True, vmem_limit_bytes=56 << 20),
        name="final_exchange",
    )(*halves)


def _small_allreduce(pack):
    rows = pack.shape[0]

    def body(p_ref, o_ref, buf, send, recv):
        x, y, c, _ = _place()
        me = 4 * x + 2 * y + c
        buf[me] = p_ref[...]
        k = 0
        copies = []
        for dx in range(2):
            for dy in range(2):
                for dc in range(2):
                    if dx == 0 and dy == 0 and dc == 0:
                        continue
                    to = (jnp.where(dx, 1 - x, x), jnp.where(dy, 1 - y, y), jnp.where(dc, 1 - c, c))
                    src_slot = 4 * to[0] + 2 * to[1] + to[2]
                    copies.append((pltpu.make_async_remote_copy(
                        src_ref=p_ref, dst_ref=buf.at[me], send_sem=send.at[k], recv_sem=recv.at[k],
                        device_id=to, device_id_type=MESH), src_slot, k))
                    k += 1
        for cp, _, _ in copies:
            cp.start()
        for cp, src_slot, k in copies:
            pltpu.make_async_remote_copy(
                src_ref=p_ref, dst_ref=buf.at[src_slot], send_sem=send.at[k], recv_sem=recv.at[k],
                device_id=(x, y, c), device_id_type=MESH).wait()
        acc = buf[0]
        for d in range(1, 8):
            acc = acc + buf[d]
        o_ref[...] = acc

    dma = pltpu.SemaphoreType.DMA
    vm = BS(memory_space=pltpu.VMEM)
    return pl.pallas_call(
        body, in_specs=[vm], out_specs=vm, out_shape=_sds((rows, D), F32),
        scratch_shapes=[pltpu.VMEM((8, rows, D), F32), dma((7,)), dma((7,))],
        compiler_params=pltpu.CompilerParams(has_side_effects=True, vmem_limit_bytes=32 << 20),
        name="small_allreduce",
    )(pack)


def _in_proj(h, w, bias, name, transposed=False):
    n = w.shape[0] if transposed else w.shape[1]
    tn = 640 if n == NA else 896
    ep = None
    extras, especs = (), ()
    if bias is not None:
        def ep(acc, ex, o_ref, i):
            o_ref[...] = acc + ex[0][...]
        extras = (bias,)
        especs = (BS((1, tn), lambda i, j, k: (0, j)),)
    b_spec = BS((tn, D), lambda i, j, k: (j, 0)) if transposed else BS((D, tn), lambda i, j, k: (0, j))
    return _mm("nt" if transposed else "nn", h, w, grid=(T // TR, n // tn, 1),
               a_spec=BS((TR, D), lambda i, j, k: (i, 0)), b_spec=b_spec,
               out_shape=_sds((T, n), F32), out_spec=BS((TR, tn), lambda i, j, k: (i, j)),
               acc_shape=(TR, tn), extras=extras, extra_specs=especs, epilogue=ep, name=name)


def _res_rms_epilogue(acc, ex, outs, i):
    y = acc + ex[0][...]
    outs[0][...] = y
    r = lax.rsqrt(jnp.mean(y * y, axis=-1, keepdims=True) + EPS)
    outs[1][...] = (y * r * ex[1][...]).astype(BF16)


def _res_loss_epilogue(acc, ex, outs, i):
    e = acc + ex[0][...] - ex[1][...]
    outs[1][...] = e * (1.0 / D)

    @pl.when(i == 0)
    def _():
        outs[0][...] = jnp.zeros_like(outs[0])

    outs[0][...] += 0.5 * jnp.sum(jnp.mean(e * e, axis=-1, keepdims=True), axis=0, keepdims=True)


def _rms_bwd_epilogue(dh, ex, outs, i):
    xv = ex[0][...]
    r = lax.rsqrt(jnp.mean(xv * xv, axis=-1, keepdims=True) + EPS)
    xh = xv * r
    gy = dh * ex[1][...]
    outs[0][...] = r * (gy - xh * jnp.mean(gy * xh, axis=-1, keepdims=True)) + ex[2][...]

    @pl.when(i == 0)
    def _():
        outs[1][...] = jnp.zeros_like(outs[1])

    outs[1][...] += jnp.sum(dh * xh, axis=0, keepdims=True)


def _local_step(x, mem, target, w, p, carries=None, bwd_carry_fn=None):
    row = lambda i, j, k: (i, 0)
    whole = lambda i, j, k: (0, 0)
    w = {k: (list(v) if isinstance(v, list) else v) for k, v in w.items()}
    carries = carries or {}

    def carry_of(name):
        return carries[name][0] if name in carries else None

    def delivered(name, outs):
        if name in carries:
            carries[name][1](w, outs)

    saved = []
    bias = _bias_expand(p["rel_u"])
    vec = BS((1, D), whole)
    h = _rms_fwd(x, p["norm1_g"][0:1], "rms1_0")
    for l in range(2):
        type_a = l == 0
        memn = _rms_fwd(mem, p["mem_norm_g"][l:l + 1], f"rmsmem_{l}")
        if type_a:
            z = _in_proj(h, w["a"], None, "inproj_a")
            cat, carried = _attn_fwd(z, bias, p["a_q_g2"], p["a_k_g2"], carry_of("attn_fwd"))
            delivered("attn_fwd", carried)
            qcol = NA // MEMW - 1
        else:
            z = _in_proj(h, w["b"], p["b_b_in"], "inproj_b", transposed=True)
            cat = _conv_fwd(z, p["conv_w"], p["conv_b"], p["ln_g"], p["ln_b"])
            qcol = NBW // MEMW - 1
        kv = _mm("nn", memn, w["kv"][l], grid=(1, 1, 1),
                 a_spec=BS((NB * MEMT, D), whole), b_spec=BS((D, 2 * MEMW), whole),
                 out_shape=_sds((NB * MEMT, 2 * MEMW), F32), out_spec=BS((NB * MEMT, 2 * MEMW), whole),
                 acc_shape=(8, 128), name=f"memkv_{l}")
        cat = _memattn_fwd(z, kv, cat, p["mq_g4"][l:l + 1], p["mk_g4"][l:l + 1], qcol, f"memattn_fwd_{l}")
        x1, h2 = _mm("nn", cat, w["wo"][l], grid=(T // TR, 1, 1), a_spec=BS((TR, D), row),
                     b_spec=BS((D, D), whole),
                     out_shape=[_sds((T, D), F32), _sds((T, D), BF16)],
                     out_spec=[BS((TR, D), row), BS((TR, D), row)], acc_shape=(8, 128),
                     extras=(x, p["norm2_g"][l:l + 1]), extra_specs=(BS((TR, D), row), vec),
                     epilogue=_res_rms_epilogue, name=f"outproj_{l}")
        (g, u, act), carried = _ffn_up(h2, w["gu"][l], l, carry_of(f"ffn_up_{l}"))
        delivered(f"ffn_up_{l}", carried)
        last = l == 1
        res = _mm("nn", act, w["wd"][l], grid=(T // TR, 1, 1),
                  a_spec=BS((TR, FF), row), b_spec=BS((FF, D), whole),
                  out_shape=[_sds((1, 1), F32), _sds((T, D), F32)] if last else
                  [_sds((T, D), F32), _sds((T, D), BF16)],
                  out_spec=[BS((1, 1), whole), BS((TR, D), row)] if last else
                  [BS((TR, D), row), BS((TR, D), row)],
                  acc_shape=(8, 128), extras=(x1, target if last else p["norm1_g"][1:2]),
                  extra_specs=(BS((TR, D), row), BS((TR, D), row) if last else vec),
                  epilogue=_res_loss_epilogue if last else _res_rms_epilogue, sequential=last,
                  name=f"ffn_down_{l}", carry=carry_of(f"ffn_down_{l}"))
        if carry_of(f"ffn_down_{l}") is not None:
            res, carried = res
            delivered(f"ffn_down_{l}", carried)
        saved.append(dict(x=x, h=h, memn=memn, kv=kv, z=z, cat=cat, x1=x1, h2=h2, g=g, u=u, act=act,
                          qcol=qcol))
        if last:
            loss, dx = res
        else:
            x, h = res

    big = dict(a=None, b=None, kv=[None, None], wo=[None, None], gu=[None, None], wd=[None, None])
    small = {}
    bwd_carried = ()
    tk = T // 2
    nkt = T // tk
    for l in (1, 0):
        sv = saved[l]
        dgu = _ffn_down_bwd(dx, w["wd"][l], sv["g"], sv["u"], l)
        big["wd"][l] = _mm("tn", sv["act"], dx, grid=(FF // FT, 1, 2 * nkt),
                           a_spec=BS((tk // 2, FT), lambda i, j, k: (k, i)),
                           b_spec=BS((tk // 2, D), lambda i, j, k: (k, 0)),
                           out_shape=_sds((FF, D), F32), out_spec=BS((FT, D), lambda i, j, k: (i, 0)),
                           acc_shape=(FT, D), name=f"dw_down_{l}")
        dx1, small[f"norm2_g{l}"] = _mm(
            "nn", dgu, w["gu"][l], grid=(T // TR, 1, 2),
            a_spec=BS((None, TR, FF), lambda i, j, k: (k, i, 0)),
            b_spec=BS((None, FF, D), lambda i, j, k: (k, 0, 0)),
            out_shape=[_sds((T, D), F32), _sds((1, D), F32)], out_spec=[BS((TR, D), row), vec],
            acc_shape=(TR, D), extras=(sv["x1"], p["norm2_g"][l:l + 1], dx),
            extra_specs=(BS((TR, D), row), vec, BS((TR, D), row)),
            epilogue=_rms_bwd_epilogue, sequential=True, name=f"dh2_{l}")
        big["gu"][l] = _mm("tn", dgu, sv["h2"], grid=(2 * FF // FT, 1, 2 * nkt),
                           a_spec=BS((None, tk // 2, FT), lambda i, j, k: (i // 2, k, i % 2)),
                           b_spec=BS((tk // 2, D), lambda i, j, k: (k, 0)),
                           out_shape=_sds((2, FF, D), F32),
                           out_spec=BS((None, FT, D), lambda i, j, k: (i // 2, i % 2, 0)),
                           acc_shape=(FT, D), name=f"dw_gu_{l}")
        dcat = _mm("nt", dx1, w["wo"][l], grid=(T // TR, 1, 1), a_spec=BS((TR, D), row),
                   b_spec=BS((D, D), whole),
                   out_shape=_sds((T, D), F32), out_spec=BS((TR, D), row), acc_shape=(8, 128),
                   name=f"dcat_{l}")
        big["wo"][l] = _mm("tn", sv["cat"], dx1, grid=(1, 1, nkt),
                           a_spec=BS((tk, D), lambda i, j, k: (k, 0)), b_spec=BS((tk, D), lambda i, j, k: (k, 0)),
                           out_shape=_sds((D, D), F32), out_spec=BS((D, D), whole),
                           acc_shape=(D, D), name=f"dw_out_{l}")
        dqm, dkv, small[f"mq_g{l}"], small[f"mk_g{l}"] = _memattn_bwd(
            sv["z"], sv["kv"], dcat, p["mq_g4"][l:l + 1], p["mk_g4"][l:l + 1], sv["qcol"], f"memattn_bwd_{l}")
        if l == 0:
            carry = bwd_carry_fn(big) if bwd_carry_fn is not None else None
            (dq, dk, dv, dbias, small["a_q_g"], small["a_k_g"]), bwd_carried = _attn_bwd(
                sv["z"], dcat, bias, p["a_q_g2"], p["a_k_g2"], carry)
            small["rel_u"] = _bias_reduce(dbias)
            dz = jnp.concatenate([dq, dk, dv, dqm], axis=1)
            w_in, key, n, tn = w["a"], "a", NA, 640
        else:
            du, small["conv_w"], small["conv_b"], small["ln_g"], small["ln_b"], dbin_u = _conv_bwd(
                sv["z"], dcat, p["conv_w"], p["conv_b"], p["ln_g"], p["ln_b"])
            dz = jnp.concatenate([du, dqm], axis=1)
            small["b_in_u"] = dbin_u
            w_in, key, n, tn = w["b"], "b", NBW, 896
        norm_bwd = dict(out_shape=[_sds((T, D), F32), _sds((1, D), F32)], out_spec=[BS((TR, D), row), vec],
                        acc_shape=(8, 128), extras=(sv["x"], p["norm1_g"][l:l + 1], dx1),
                        extra_specs=(BS((TR, D), row), vec, BS((TR, D), row)),
                        epilogue=_rms_bwd_epilogue, sequential=True, name=f"dh_{l}")
        if l == 0:
            dx, small[f"norm1_g{l}"] = _mm("nt", dz, w_in, grid=(T // TR, 1, 1),
                                           a_spec=BS((TR, n), row), b_spec=BS((D, n), whole), **norm_bwd)
            big[key] = _mm("tn", sv["h"], dz, grid=(1, n // tn, nkt),
                           a_spec=BS((tk, D), lambda i, j, k: (k, 0)), b_spec=BS((tk, tn), lambda i, j, k: (k, j)),
                           out_shape=_sds((D, n), F32), out_spec=BS((D, tn), lambda i, j, k: (0, j)),
                           acc_shape=(D, tn), name=f"dw_in_{l}")
        else:
            dx, small[f"norm1_g{l}"] = _mm("nn", dz, w_in, grid=(T // TR, 1, 1),
                                           a_spec=BS((TR, n), row), b_spec=BS((n, D), whole), **norm_bwd)
            big[key] = _mm("tn", dz, sv["h"], grid=(n // tn, 1, nkt),
                           a_spec=BS((tk, tn), lambda i, j, k: (k, i)), b_spec=BS((tk, D), lambda i, j, k: (k, 0)),
                           out_shape=_sds((n, D), F32), out_spec=BS((tn, D), lambda i, j, k: (i, 0)),
                           acc_shape=(tn, D), name=f"dw_in_{l}")
        if l == 1:
            small["b_in_qm"] = _colsum(dqm, "colsum_dqm")
        big["kv"][l] = _mm("tn", sv["memn"], dkv, grid=(1, 1, 1),
                           a_spec=BS((NB * MEMT, D), whole), b_spec=BS((NB * MEMT, 2 * MEMW), whole),
                           out_shape=_sds((D, 2 * MEMW), F32), out_spec=BS((D, 2 * MEMW), whole),
                           acc_shape=(8, 128), name=f"dw_kv_{l}")
        dmemn = _mm("nt", dkv, w["kv"][l], grid=(1, 1, 1),
                    a_spec=BS((NB * MEMT, 2 * MEMW), whole), b_spec=BS((D, 2 * MEMW), whole),
                    out_shape=_sds((NB * MEMT, D), F32), out_spec=BS((NB * MEMT, D), whole),
                    acc_shape=(8, 128), name=f"dmemn_{l}")
        _, small[f"mem_norm_g{l}"] = _rms_bwd(dmemn, mem, p["mem_norm_g"][l:l + 1], None, f"rmsmem_bwd_{l}")
    return loss, dx, big, small, bwd_carried


def _colsum(a, name):
    rows, cols = a.shape

    def body(a_ref, o_ref):
        @pl.when(pl.program_id(0) == 0)
        def _():
            o_ref[...] = jnp.zeros_like(o_ref)

        o_ref[...] += jnp.sum(a_ref[...].astype(F32), axis=0, keepdims=True)

    return pl.pallas_call(
        body, grid=(rows // TR,), in_specs=[BS((TR, cols), lambda i: (i, 0))],
        out_specs=BS((1, cols), lambda i: (0, 0)), out_shape=_sds((1, cols), F32),
        compiler_params=_cp(("arbitrary",)), name=name,
    )(a)


_PACK_ROWS = 64


def _pad_to(a, rows, cols=D):
    return jnp.pad(a, ((0, rows - a.shape[0]), (0, cols - a.shape[1])))


def _pack_small(sm):
    parts = [
        jnp.concatenate([sm["norm1_g0"], sm["norm1_g1"]], 0),
        jnp.concatenate([sm["mem_norm_g0"], sm["mem_norm_g1"]], 0),
        jnp.concatenate([sm["norm2_g0"], sm["norm2_g1"]], 0),
        _pad_to(sm["a_q_g"], 1), _pad_to(sm["a_k_g"], 1),
        _pad_to(jnp.concatenate([sm["mq_g0"], sm["mq_g1"]], 0), 2),
        _pad_to(jnp.concatenate([sm["mk_g0"], sm["mk_g1"]], 0), 2),
        _pad_to(sm["conv_b"], 1), _pad_to(sm["ln_g"], 1), _pad_to(sm["ln_b"], 1),
        _pad_to(sm["b_in_u"][:, :D], 1), _pad_to(sm["b_in_u"][:, D:], 1),
        _pad_to(sm["b_in_qm"], 1),
        _pad_to(sm["conv_w"][:CONVW], CONVW),
        sm["rel_u"].reshape(12, D),
    ]
    pack = jnp.concatenate(parts, 0)
    return jnp.pad(pack, ((0, _PACK_ROWS - pack.shape[0]), (0, 0)))


def _rel_table_to_u(rel_bias):
    flat = jnp.concatenate([jnp.broadcast_to(rel_bias[:, 191:192], (12, 447)), rel_bias[:, ::-1]], axis=1)
    return jnp.pad(flat, ((0, 0), (192, 1024 - 192 - 639))).reshape(12, 1, 1024)


def _u_to_rel_table(du):
    flat = du[:, 192:192 + 639]
    g = flat[:, 447:][:, ::-1]
    return g, flat[:, :447]


def kernel(x, mem, norm1_g, mem_norm_g, a_w_in, a_q_g, a_k_g, a_rel_bias, b_w_in, b_b_in, b_conv_w, b_conv_b, b_ln_g, b_ln_b, mq_g, mk_g, w_mem_kv, w_out, norm2_g, w_gate, w_up, w_down, loss_target, m_norm1_g, m_mem_norm_g, m_a_w_in, m_a_q_g, m_a_k_g, m_a_rel_bias, m_b_w_in, m_b_b_in, m_b_conv_w, m_b_conv_b, m_b_ln_g, m_b_ln_b, m_mq_g, m_mk_g, m_w_mem_kv, m_w_out, m_norm2_g, m_w_gate, m_w_up, m_w_down, v_norm1_g, v_mem_norm_g, v_a_w_in, v_a_q_g, v_a_k_g, v_a_rel_bias, v_b_w_in, v_b_b_in, v_b_conv_w, v_b_conv_b, v_b_ln_g, v_b_ln_b, v_mq_g, v_mk_g, v_w_mem_kv, v_w_out, v_norm2_g, v_w_gate, v_w_up, v_w_down):
    sx = 2 * lax.axis_index("x") + lax.axis_index("y")

    n_in = (NA // NSH, NBW // NSH)
    tr = lambda a: jnp.swapaxes(a, -1, -2)
    small_src = jnp.concatenate([
        jnp.pad(b_b_in, ((0, 0), (0, 512 - 448))),
        jnp.pad(b_conv_w[0], ((0, 0), (0, 512 - 192))),
        jnp.pad(jnp.concatenate([b_conv_b, b_ln_g, b_ln_b], 0), ((0, 0), (0, 512 - 192))),
        jnp.zeros((5, 512), F32)], 0)

    def gather_group(l, names, small=None):
        src_of = {
            "in": lambda: [_cast_bf16([tr(b_w_in[0])], "cast_in_1").reshape(1, 2, n_in[1] // 2, D) if l else
                           _cast_bf16([a_w_in[0]], "cast_in_0").reshape(1, 2, D // 2, n_in[0])],
            "kv": lambda: [_cast_bf16([w_mem_kv[l]], f"cast_kv_{l}").reshape(1, 2, 128, 2 * MEMW)],
            "wo": lambda: [_cast_bf16([w_out[l]], f"cast_wo_{l}").reshape(1, 2, 128, D)],
            "gu": lambda: [_cast_bf16([tr(w_gate[l])], f"cast_gate_{l}").reshape(1, 2, FS // 2, D),
                           _cast_bf16([tr(w_up[l])], f"cast_up_{l}").reshape(1, 2, FS // 2, D)],
            "wd": lambda: [_cast_bf16([w_down[l]], f"cast_wd_{l}").reshape(1, 2, FS // 2, D)]}
        dst_of = {"in": (1, NSH, 2, n_in[1] // 2, D) if l else (1, NSH, 2, D // 2, n_in[0]),
                  "kv": (1, NSH, 2, 128, 2 * MEMW), "wo": (1, NSH, 2, 128, D),
                  "gu": (1, 2, NSH, 2, FS // 2, D), "wd": (1, NSH, 2, FS // 2, D)}
        srcs, views = [], []
        for k, name in enumerate(names):
            srcs += src_of[name]()
            if name == "gu":
                views += [lambda d, k=k: d[k].at[:, 0], lambda d, k=k: d[k].at[:, 1]]
            else:
                views.append(lambda d, k=k: d[k])

        def done(w, outs):
            for k, name in enumerate(names):
                if name == "in" and l == 0:
                    w["a"] = outs[k].reshape(NSH, D, n_in[0]).transpose(1, 0, 2).reshape(D, NA)
                elif name == "in":
                    w["b"] = outs[k].reshape(NBW, D)
                else:
                    shape = {"kv": (D, 2 * MEMW), "wo": (D, D), "gu": (2, FF, D), "wd": (FF, D)}
                    w[name][l] = outs[k].reshape(shape[name])

        return _gather_exchange(srcs, [dst_of[name] for name in names], views, small), done

    w = dict(a=None, b=None, kv=[None, None], wo=[None, None], gu=[None, None], wd=[None, None])
    first, first_done = gather_group(0, ["in", "kv"], small_src)
    outs0 = _run_exchange(first, "gather_weights_first")
    first_done(w, outs0)
    small_all = outs0[2]
    carries = {"attn_fwd": gather_group(0, ["wo", "gu", "wd"]),
               "ffn_up_0": gather_group(1, ["in", "kv", "wo", "wd"]),
               "ffn_down_0": gather_group(1, ["gu"])}

    conv_w_full = small_all[:, 1:1 + CONVW, :192].transpose(1, 0, 2).reshape(CONVW, TOK)
    vec3 = small_all[:, 32:35, :192].transpose(1, 0, 2).reshape(3, TOK)
    p = dict(
        norm1_g=norm1_g, mem_norm_g=mem_norm_g, norm2_g=norm2_g,
        a_q_g2=jnp.tile(a_q_g, (1, AH)), a_k_g2=jnp.tile(a_k_g, (1, AH)),
        mq_g4=jnp.tile(mq_g, (1, 4)), mk_g4=jnp.tile(mk_g, (1, 4)),
        rel_u=_rel_table_to_u(a_rel_bias[0]),
        b_b_in=small_all[:, 0, :448].reshape(1, NBW),
        conv_w=jnp.pad(conv_w_full, ((0, 1), (0, 0))),
        conv_b=vec3[0:1], ln_g=vec3[1:2], ln_b=vec3[2:3])

    def pair_sums(items, big):
        own, sums_b = [], []
        for l, name in items:
            if name == "in" and l == 0:
                g = big["a"].reshape(D, NSH, n_in[0]).transpose(1, 0, 2).reshape(1, NSH, 2, D // 2, n_in[0])
            elif name == "in":
                g = big["b"].reshape(1, NSH, 2, n_in[1] // 2, D)
            else:
                shape = {"kv": (1, NSH, 2, 128, 2 * MEMW), "wo": (1, NSH, 2, 128, D),
                         "gu": (2, NSH, 2, FS // 2, D), "wd": (1, NSH, 2, FS // 2, D)}
                g = big[name][l].reshape(shape[name])
            of, sb = _pair_reduce(g, f"pair_reduce_{name}_{l}")
            own.append(of)
            sums_b.append(sb)
        return own, sums_b

    early = [(1, "in"), (1, "kv"), (1, "wo"), (1, "gu"), (1, "wd"), (0, "wo"), (0, "gu"), (0, "wd")]
    late = [(0, "in"), (0, "kv")]
    own_early = []

    def bwd_carry_fn(big):
        own, sums_b = pair_sums(early, big)
        own_early.extend(own)
        return _chip_exchange(sums_b)

    loss, grad_x, big, small, parts_early = _local_step(
        x.reshape(T, D), mem.reshape(NB * MEMT, D), loss_target.reshape(T, D), w, p,
        carries=carries, bwd_carry_fn=bwd_carry_fn)
    loss = lax.psum(loss[0, 0], ("x", "y", "c"))

    own_late, sums_b_late = pair_sums(late, big)
    parts_late = _run_exchange(_chip_exchange(sums_b_late), "chip_exchange_late")
    items = early + late
    halves = [_quad_sum(o, pt, f"quad_sum_{name}_{l}")
              for (l, name), o, pt in zip(items, own_early + own_late, list(parts_early) + list(parts_late))]
    out_shapes = [(1, D, NA // NSH), (1, NBW // NSH, D), (2, 2 * 128, 2 * MEMW), (2, 2 * 128, D),
                  (2, FS, D), (2, FS, D), (2, FS, D)]
    target_of = {"in": lambda l: [(0, l, 0)], "kv": lambda l: [(0, 2, l)], "wo": lambda l: [(0, 3, l)],
                 "gu": lambda l: [(0, 4, l), (1, 5, l)], "wd": lambda l: [(0, 6, l)]}
    targets = [target_of[name](l) for l, name in items]
    g_a, g_b, g_kv, g_wo, g_gate, g_up, g_wd = _final_exchange(halves, out_shapes, targets)

    tot = _small_allreduce(_pack_small(small))
    g_rel, clip_part = _u_to_rel_table(tot[49:61])
    g_rel = jnp.concatenate([g_rel[:, :191], g_rel[:, 191:] + _rowsum(clip_part)], axis=1)
    b_in_full = jnp.concatenate([tot[15:16], tot[16:17, :512], tot[17:18, :MEMW]], axis=1)
    g_small = dict(
        norm1_g=tot[0:2], mem_norm_g=tot[2:4], norm2_g=tot[4:6],
        a_q_g=tot[6:7, :HD], a_k_g=tot[7:8, :HD], a_rel_bias=g_rel[None],
        b_b_in=lax.dynamic_slice(b_in_full, (0, sx * 448), (1, 448)),
        b_conv_w=lax.dynamic_slice(tot[18:49, :TOK], (0, sx * 192), (CONVW, 192))[None],
        b_conv_b=lax.dynamic_slice(tot[12:13, :TOK], (0, sx * 192), (1, 192)),
        b_ln_g=lax.dynamic_slice(tot[13:14, :TOK], (0, sx * 192), (1, 192)),
        b_ln_b=lax.dynamic_slice(tot[14:15, :TOK], (0, sx * 192), (1, 192)),
        mq_g=tot[8:10, :HD], mk_g=tot[10:12, :HD])

    names = ["norm1_g", "mem_norm_g", "a_w_in", "a_q_g", "a_k_g", "a_rel_bias", "b_w_in", "b_b_in",
             "b_conv_w", "b_conv_b", "b_ln_g", "b_ln_b", "mq_g", "mk_g", "w_mem_kv", "w_out",
             "norm2_g", "w_gate", "w_up", "w_down"]
    weights = dict(zip(names, [norm1_g, mem_norm_g, a_w_in, a_q_g, a_k_g, a_rel_bias, b_w_in, b_b_in,
                               b_conv_w, b_conv_b, b_ln_g, b_ln_b, mq_g, mk_g, w_mem_kv, w_out,
                               norm2_g, w_gate, w_up, w_down]))
    ms = dict(zip(names, [m_norm1_g, m_mem_norm_g, m_a_w_in, m_a_q_g, m_a_k_g, m_a_rel_bias, m_b_w_in,
                          m_b_b_in, m_b_conv_w, m_b_conv_b, m_b_ln_g, m_b_ln_b, m_mq_g, m_mk_g,
                          m_w_mem_kv, m_w_out, m_norm2_g, m_w_gate, m_w_up, m_w_down]))
    vs = dict(zip(names, [v_norm1_g, v_mem_norm_g, v_a_w_in, v_a_q_g, v_a_k_g, v_a_rel_bias, v_b_w_in,
                          v_b_b_in, v_b_conv_w, v_b_conv_b, v_b_ln_g, v_b_ln_b, v_mq_g, v_mk_g,
                          v_w_mem_kv, v_w_out, v_norm2_g, v_w_gate, v_w_up, v_w_down]))
    grads = dict(g_small)
    grads.update(a_w_in=g_a, b_w_in=g_b, w_mem_kv=g_kv, w_out=g_wo, w_gate=g_gate, w_up=g_up, w_down=g_wd)
    big_names = ["a_w_in", "b_w_in", "w_mem_kv", "w_out", "w_gate", "w_up", "w_down"]
    small_names = [n for n in names if n not in big_names]
    delta, new_m, new_v = {}, {}, {}
    for n in big_names:
        if n in ("b_w_in", "w_gate", "w_up"):
            outs = _adamw_big(tr(weights[n]), grads[n], tr(ms[n]), tr(vs[n]), f"adamw_{n}")
            delta[n], new_m[n], new_v[n] = [tr(o) for o in outs]
            grads[n] = tr(grads[n])
        else:
            delta[n], new_m[n], new_v[n] = _adamw_big(weights[n], grads[n], ms[n], vs[n], f"adamw_{n}")
    as2d = lambda a: a.reshape(-1, a.shape[-1])
    d_s, m_s, v_s = _adamw_small([as2d(weights[n]) for n in small_names], [as2d(grads[n]) for n in small_names],
                                 [as2d(ms[n]) for n in small_names], [as2d(vs[n]) for n in small_names])
    for i, n in enumerate(small_names):
        delta[n] = d_s[i].reshape(weights[n].shape)
        new_m[n] = m_s[i].reshape(weights[n].shape)
        new_v[n] = v_s[i].reshape(weights[n].shape)

    return (loss, grad_x.reshape(NB, SEQ, D), *[grads[n] for n in names], *[delta[n] for n in names],
            *[new_m[n] for n in names], *[new_v[n] for n in names])


def _rowsum(a):
    def body(a_ref, o_ref):
        o_ref[...] = jnp.sum(a_ref[...], axis=1, keepdims=True)

    vm = BS(memory_space=pltpu.VMEM)
    return pl.pallas_call(body, in_specs=[vm], out_specs=vm, out_shape=_sds((a.shape[0], 1), F32),
                          compiler_params=_cp(), name="rowsum")(a)
```

```python
import functools

import jax
import jax.numpy as jnp
from jax import lax
from jax.experimental import pallas as pl
from jax.experimental.pallas import tpu as pltpu

F32 = jnp.float32
BF16 = jnp.bfloat16
BS = pl.BlockSpec
ANY = pl.BlockSpec(memory_space=pl.ANY)
MESH = pl.DeviceIdType.MESH

D = 1024
SEQ = 2048
NB = 2
T = NB * SEQ
MEMT = 256
HD = 64
TOK = 768
MEMW = 256
NA = 3 * TOK + MEMW
NBW = 2 * TOK + MEMW
FF = 2816
NSH = 4
FS = FF // NSH
FT = FF // 2
CONVW = 31
EPS = 1e-6
NEG = -1e30
SCALE = HD ** -0.5
QB = 256
KWIN = 768
KPAD = 512
TR = 512

ADAM_LR = 0.001
ADAM_B1 = 0.9
ADAM_B2 = 0.999
ADAM_EPS = 1e-08
ADAM_WD = 0.01
ADAM_STEP = 10

_DIMS = {
    "nn": (((1,), (0,)), ((), ())),
    "nt": (((1,), (1,)), ((), ())),
    "tn": (((0,), (0,)), ((), ())),
}


def _cp(sem=None, vmem_mb=48):
    return pltpu.CompilerParams(dimension_semantics=sem, vmem_limit_bytes=vmem_mb << 20)


def _sds(shape, dtype):
    return jax.ShapeDtypeStruct(tuple(shape), dtype)


def _mm(mode, a, b, *, grid, a_spec, b_spec, out_shape, out_spec, acc_shape, name,
        extras=(), extra_specs=(), epilogue=None, carry=None, vmem_mb=48, sequential=False):
    n_ex = len(extras)
    nk = grid[2]
    dims = _DIMS[mode]
    ni = len(carry.ins) if carry else 0
    no = len(carry.out_shapes) if carry else 0
    multi = isinstance(out_shape, (list, tuple))
    out_shapes = list(out_shape) if multi else [out_shape]
    out_specs = list(out_spec) if multi else [out_spec]
    n_o = len(out_shapes)

    def body(a_ref, b_ref, *rest):
        ex = rest[:n_ex]
        cin = rest[n_ex:n_ex + ni]
        o_refs = rest[n_ex + ni:n_ex + ni + n_o]
        o_ref = o_refs if multi else o_refs[0]
        cout = rest[n_ex + ni + n_o:n_ex + ni + n_o + no]
        acc = rest[n_ex + ni + n_o + no]
        cscr = rest[n_ex + ni + n_o + no + 1:]
        ids = [pl.program_id(d) for d in range(3)]
        k = ids[2]
        if carry:
            @pl.when((ids[0] == 0) & (ids[1] == 0) & (ids[2] == 0))
            def _():
                carry.start(cin, cout, cscr)

        prod = lax.dot_general(a_ref[...].astype(BF16), b_ref[...].astype(BF16), dims,
                               preferred_element_type=F32)

        def finish(val):
            if epilogue is None:
                o_ref[...] = val.astype(o_ref.dtype)
            else:
                epilogue(val, ex, o_ref, ids[0])

        if nk == 1:
            finish(prod)
        else:
            @pl.when(k == 0)
            def _():
                acc[...] = prod

            @pl.when((k > 0) & (k < nk - 1))
            def _():
                acc[...] += prod

            @pl.when(k == nk - 1)
            def _():
                finish(acc[...] + prod)

        if carry:
            @pl.when((ids[0] == grid[0] - 1) & (ids[1] == grid[1] - 1) & (ids[2] == grid[2] - 1))
            def _():
                carry.finish(cin, cout, cscr)

    acc_scratch = pltpu.VMEM(acc_shape if nk > 1 else (8, 128), F32)
    ordered = sequential or bool(carry)
    outs = pl.pallas_call(
        body, grid=grid,
        in_specs=[a_spec, b_spec, *extra_specs] + (carry.in_specs if carry else []),
        out_specs=out_specs + [ANY] * no, out_shape=out_shapes + (carry.out_shapes if carry else []),
        scratch_shapes=[acc_scratch] + (carry.scratch if carry else []),
        compiler_params=pltpu.CompilerParams(
            dimension_semantics=("arbitrary",) * 3 if ordered else ("parallel", "parallel", "arbitrary"),
            vmem_limit_bytes=vmem_mb << 20, has_side_effects=bool(carry)), name=name,
    )(a, b, *extras, *(carry.ins if carry else []))
    mine = list(outs[:n_o]) if multi else outs[0]
    return (mine, outs[n_o:]) if carry else mine


def _rms_fwd(x, g, name):
    rows = x.shape[0]

    def body(x_ref, g_ref, o_ref):
        xv = x_ref[...]
        r = lax.rsqrt(jnp.mean(xv * xv, axis=-1, keepdims=True) + EPS)
        o_ref[...] = (xv * r * g_ref[...]).astype(BF16)

    return pl.pallas_call(
        body, grid=(rows // TR,),
        in_specs=[BS((TR, D), lambda i: (i, 0)), BS((1, D), lambda i: (0, 0))],
        out_specs=BS((TR, D), lambda i: (i, 0)), out_shape=_sds((rows, D), BF16),
        compiler_params=_cp(("arbitrary",)), name=name,
    )(x, g)


def _rms_bwd(dh, x, g, dres, name):
    rows = x.shape[0]
    has_res = dres is not None

    def body(*refs):
        if has_res:
            dh_ref, x_ref, g_ref, r_ref, dx_ref, dg_ref = refs
        else:
            dh_ref, x_ref, g_ref, dx_ref, dg_ref = refs
        xv = x_ref[...]
        dhv = dh_ref[...]
        r = lax.rsqrt(jnp.mean(xv * xv, axis=-1, keepdims=True) + EPS)
        xh = xv * r
        gy = dhv * g_ref[...]
        dx = r * (gy - xh * jnp.mean(gy * xh, axis=-1, keepdims=True))
        if has_res:
            dx = dx + r_ref[...]
        dx_ref[...] = dx

        @pl.when(pl.program_id(0) == 0)
        def _():
            dg_ref[...] = jnp.zeros_like(dg_ref)

        dg_ref[...] += jnp.sum(dhv * xh, axis=0, keepdims=True)

    row = BS((TR, D), lambda i: (i, 0))
    vec = BS((1, D), lambda i: (0, 0))
    ins = [dh, x, g] + ([dres] if has_res else [])
    return pl.pallas_call(
        body, grid=(rows // TR,),
        in_specs=[row, row, vec] + ([row] if has_res else []),
        out_specs=[row, vec], out_shape=[_sds((rows, D), F32), _sds((1, D), F32)],
        compiler_params=_cp(("arbitrary",)), name=name,
    )(*ins)


def _group_masks(width):
    lane = lax.broadcasted_iota(jnp.int32, (1, width), 1)
    return [(lane >= HD * h) & (lane < HD * (h + 1)) for h in range(width // HD)]


def _group_mean(v, masks):
    out = jnp.zeros_like(v)
    for m in masks:
        s = jnp.sum(jnp.where(m, v, 0.0), axis=-1, keepdims=True) * (1.0 / HD)
        out = jnp.where(m, s, out)
    return out


def _head_norm(zv, g, masks):
    r = lax.rsqrt(_group_mean(zv * zv, masks) + EPS)
    return zv * r * g


def _head_norm_bwd(dy, zv, g, masks):
    r = lax.rsqrt(_group_mean(zv * zv, masks) + EPS)
    zh = zv * r
    gy = dy * g
    dz = r * (gy - zh * _group_mean(gy * zh, masks))
    return dz, jnp.sum(dy * zh, axis=0, keepdims=True)


def _fold_heads(v, width):
    vb = jnp.broadcast_to(v, (8, width))
    out = vb
    for h in range(1, width // HD):
        out = out + pltpu.roll(vb, width - HD * h, axis=1)
    return out[0:1]


def _bias_expand(u):
    def body(u_ref, o_ref):
        x = jnp.broadcast_to(u_ref[...], (QB, 1024))
        rolled = pltpu.roll(x, 1024 - (QB - 1), axis=1, stride=1, stride_axis=0)[:, :KWIN]
        row = lax.broadcasted_iota(jnp.int32, (QB, 1), 0)
        col = lax.broadcasted_iota(jnp.int32, (1, KWIN), 1)
        lo = (row // 64) * 64
        ok = (col >= lo) & (col < lo + 576)
        o_ref[...] = jnp.where(ok, rolled, NEG)

    return pl.pallas_call(
        body, grid=(12,), in_specs=[BS((None, 1, 1024), lambda h: (h, 0, 0))],
        out_specs=BS((None, QB, KWIN), lambda h: (h, 0, 0)), out_shape=_sds((12, QB, KWIN), F32),
        compiler_params=_cp(("arbitrary",)), name="bias_expand",
    )(u)


def _bias_reduce(ds):
    def body(d_ref, o_ref):
        ri = lax.broadcasted_iota(jnp.int32, (QB, QB), 0)
        ci = lax.broadcasted_iota(jnp.int32, (QB, QB), 1)
        flip = (ri + ci == QB - 1).astype(F32)
        drev = jnp.dot(flip, d_ref[...], precision=lax.Precision.HIGHEST, preferred_element_type=F32)
        x = jnp.concatenate([drev, jnp.zeros((QB, 1024 - KWIN), F32)], axis=1)
        rolled = pltpu.roll(x, 0, axis=1, stride=1, stride_axis=0)
        o_ref[...] = jnp.sum(rolled, axis=0, keepdims=True)

    return pl.pallas_call(
        body, grid=(12,), in_specs=[BS((None, QB, KWIN), lambda h: (h, 0, 0))],
        out_specs=BS((None, 1, 1024), lambda h: (h, 0, 0)), out_shape=_sds((12, 1, 1024), F32),
        compiler_params=_cp(("arbitrary",)), name="bias_reduce",
    )(ds)


AW = 256
AH = AW // HD
AG = TOK // AW
def _attn_softmax(qh, kw, bias, startadd):
    s = lax.dot_general(qh, kw, _DIMS["nt"], preferred_element_type=F32) + bias + startadd
    m = jnp.max(s, axis=-1, keepdims=True)
    p = jnp.exp(s - m)
    return p * (1.0 / jnp.sum(p, axis=-1, keepdims=True))


def _attn_prologue(q_ref, k_ref, v_ref, gq_ref, gk_ref, qn_s, kn_s, v_s, masks):
    kn_s[0:KPAD, :] = jnp.zeros((KPAD, AW), BF16)
    v_s[0:KPAD, :] = jnp.zeros((KPAD, AW), BF16)
    for r in range(0, SEQ, TR):
        qn_s[r:r + TR, :] = (_head_norm(q_ref[r:r + TR, :], gq_ref[...], masks) * SCALE).astype(BF16)
        kn_s[KPAD + r:KPAD + r + TR, :] = _head_norm(k_ref[r:r + TR, :], gk_ref[...], masks).astype(BF16)
        v_s[KPAD + r:KPAD + r + TR, :] = v_ref[r:r + TR, :].astype(BF16)


def _attn_fwd(z, bias, gq2, gk2, carry=None):
    ni = len(carry.ins) if carry else 0
    no = len(carry.out_shapes) if carry else 0

    def body(q_ref, k_ref, v_ref, b_ref, gq_ref, gk_ref, *rest):
        cin, o_ref, cout = rest[:ni], rest[ni], rest[ni + 1:ni + 1 + no]
        qn_s, kn_s, v_s = rest[ni + 1 + no:ni + 4 + no]
        cscr = rest[ni + 4 + no:]
        if carry:
            @pl.when((pl.program_id(0) == 0) & (pl.program_id(1) == 0))
            def _():
                carry.start(cin, cout, cscr)

        masks = _group_masks(AW)
        _attn_prologue(q_ref, k_ref, v_ref, gq_ref, gk_ref, qn_s, kn_s, v_s, masks)
        col = lax.broadcasted_iota(jnp.int32, (1, KWIN), 1)

        def blk(i, carry):
            r0 = pl.multiple_of(i * QB, QB)
            qb = qn_s[pl.ds(r0, QB), :]
            kw = kn_s[pl.ds(r0, KWIN), :]
            vw = v_s[pl.ds(r0, KWIN), :]
            startadd = jnp.where(col + r0 < KPAD, NEG, 0.0)
            o = jnp.zeros((QB, AW), F32)
            for h in range(AH):
                qh = jnp.where(masks[h], qb, jnp.zeros_like(qb))
                vh = jnp.where(masks[h], vw, jnp.zeros_like(vw))
                p = _attn_softmax(qh, kw, b_ref[h], startadd).astype(BF16)
                o = o + jnp.dot(p, vh, preferred_element_type=F32)
            o_ref[pl.ds(r0, QB), :] = o.astype(BF16)
            return carry

        lax.fori_loop(0, SEQ // QB, blk, 0)

        if carry:
            @pl.when((pl.program_id(0) == NB - 1) & (pl.program_id(1) == AG - 1))
            def _():
                carry.finish(cin, cout, cscr)

    vec = BS((1, AW), lambda b, hp: (0, 0))
    outs = pl.pallas_call(
        body, grid=(NB, AG),
        in_specs=[BS((SEQ, AW), lambda b, hp: (b, hp)),
                  BS((SEQ, AW), lambda b, hp: (b, AG + hp)),
                  BS((SEQ, AW), lambda b, hp: (b, 2 * AG + hp)),
                  BS((AH, QB, KWIN), lambda b, hp: (hp, 0, 0)), vec, vec]
        + (carry.in_specs if carry else []),
        out_specs=[BS((SEQ, AW), lambda b, hp: (b, hp))] + [ANY] * no,
        out_shape=[_sds((T, D), BF16)] + (carry.out_shapes if carry else []),
        scratch_shapes=[pltpu.VMEM((SEQ, AW), BF16), pltpu.VMEM((SEQ + KPAD, AW), BF16),
                        pltpu.VMEM((SEQ + KPAD, AW), BF16)] + (carry.scratch if carry else []),
        compiler_params=pltpu.CompilerParams(
            dimension_semantics=("arbitrary", "arbitrary"), vmem_limit_bytes=48 << 20,
            has_side_effects=bool(carry)), name="attn_fwd",
    )(z, z, z, bias, gq2, gk2, *(carry.ins if carry else []))
    return outs[0], outs[1:]


def _attn_bwd(z, dcat, bias, gq2, gk2, carry=None):
    ni = len(carry.ins) if carry else 0
    no = len(carry.out_shapes) if carry else 0

    def body(q_ref, k_ref, v_ref, do_ref, b_ref, gq_ref, gk_ref, *rest):
        cin = rest[:ni]
        dq_ref, dk_ref, dv_ref, db_ref, dgq_ref, dgk_ref = rest[ni:ni + 6]
        cout = rest[ni + 6:ni + 6 + no]
        qn_s, kn_s, v_s, dqn_s, dkn_s, dv_s = rest[ni + 6 + no:ni + 12 + no]
        cscr = rest[ni + 12 + no:]
        hp = pl.program_id(0)
        b = pl.program_id(1)
        if carry:
            @pl.when((hp == 0) & (b == 0))
            def _():
                carry.start(cin, cout, cscr)

        masks = _group_masks(AW)
        _attn_prologue(q_ref, k_ref, v_ref, gq_ref, gk_ref, qn_s, kn_s, v_s, masks)
        dkn_s[...] = jnp.zeros_like(dkn_s)
        dv_s[...] = jnp.zeros_like(dv_s)

        @pl.when(b == 0)
        def _():
            db_ref[...] = jnp.zeros_like(db_ref)

        @pl.when((b == 0) & (hp == 0))
        def _():
            dgq_ref[...] = jnp.zeros_like(dgq_ref)
            dgk_ref[...] = jnp.zeros_like(dgk_ref)

        col = lax.broadcasted_iota(jnp.int32, (1, KWIN), 1)

        def blk(i, carry):
            r0 = pl.multiple_of(i * QB, QB)
            qb = qn_s[pl.ds(r0, QB), :]
            kw = kn_s[pl.ds(r0, KWIN), :]
            vw = v_s[pl.ds(r0, KWIN), :]
            dob = do_ref[pl.ds(r0, QB), :].astype(BF16)
            startadd = jnp.where(col + r0 < KPAD, NEG, 0.0)
            dqn = jnp.zeros((QB, AW), F32)
            dkw = jnp.zeros((KWIN, AW), F32)
            dvw = jnp.zeros((KWIN, AW), F32)
            for h in range(AH):
                qh = jnp.where(masks[h], qb, jnp.zeros_like(qb))
                kh = jnp.where(masks[h], kw, jnp.zeros_like(kw))
                doh = jnp.where(masks[h], dob, jnp.zeros_like(dob))
                p = _attn_softmax(qh, kw, b_ref[h], startadd)
                dvw = dvw + lax.dot_general(p.astype(BF16), doh, _DIMS["tn"],
                                            preferred_element_type=F32)
                dp = lax.dot_general(doh, vw, _DIMS["nt"], preferred_element_type=F32)
                ds = p * (dp - jnp.sum(dp * p, axis=-1, keepdims=True))
                db_ref[h] += ds
                dsb = ds.astype(BF16)
                dqn = dqn + jnp.dot(dsb, kh, preferred_element_type=F32)
                dkw = dkw + lax.dot_general(dsb, qh, _DIMS["tn"], preferred_element_type=F32)
            dqn_s[pl.ds(r0, QB), :] = dqn * SCALE
            dkn_s[pl.ds(r0, KWIN), :] += dkw
            dv_s[pl.ds(r0, KWIN), :] += dvw
            return carry

        lax.fori_loop(0, SEQ // QB, blk, 0)

        dgq = jnp.zeros((1, AW), F32)
        dgk = jnp.zeros((1, AW), F32)
        for r in range(0, SEQ, TR):
            dq, dg = _head_norm_bwd(dqn_s[r:r + TR, :], q_ref[r:r + TR, :], gq_ref[...], masks)
            dq_ref[r:r + TR, :] = dq.astype(BF16)
            dgq = dgq + dg
            dk, dg = _head_norm_bwd(dkn_s[KPAD + r:KPAD + r + TR, :], k_ref[r:r + TR, :], gk_ref[...], masks)
            dk_ref[r:r + TR, :] = dk.astype(BF16)
            dgk = dgk + dg
            dv_ref[r:r + TR, :] = dv_s[KPAD + r:KPAD + r + TR, :].astype(BF16)
        dgq_ref[...] += _fold_heads(dgq, AW)
        dgk_ref[...] += _fold_heads(dgk, AW)

        if carry:
            @pl.when((hp == AG - 1) & (b == NB - 1))
            def _():
                carry.finish(cin, cout, cscr)

    vec = BS((1, AW), lambda hp, b: (0, 0))
    row = BS((SEQ, AW), lambda hp, b: (b, hp))
    outs = pl.pallas_call(
        body, grid=(AG, NB),
        in_specs=[row,
                  BS((SEQ, AW), lambda hp, b: (b, AG + hp)),
                  BS((SEQ, AW), lambda hp, b: (b, 2 * AG + hp)),
                  row,
                  BS((AH, QB, KWIN), lambda hp, b: (hp, 0, 0)), vec, vec]
        + (carry.in_specs if carry else []),
        out_specs=[row, row, row, BS((AH, QB, KWIN), lambda hp, b: (hp, 0, 0)), vec, vec] + [ANY] * no,
        out_shape=[_sds((T, TOK), BF16), _sds((T, TOK), BF16), _sds((T, TOK), BF16),
                   _sds((12, QB, KWIN), F32), _sds((1, AW), F32), _sds((1, AW), F32)]
        + (carry.out_shapes if carry else []),
        scratch_shapes=[pltpu.VMEM((SEQ, AW), BF16), pltpu.VMEM((SEQ + KPAD, AW), BF16),
                        pltpu.VMEM((SEQ + KPAD, AW), BF16), pltpu.VMEM((SEQ, AW), F32),
                        pltpu.VMEM((SEQ + KPAD, AW), F32), pltpu.VMEM((SEQ + KPAD, AW), F32)]
        + (carry.scratch if carry else []),
        compiler_params=pltpu.CompilerParams(
            dimension_semantics=("arbitrary", "arbitrary"), vmem_limit_bytes=58 << 20,
            has_side_effects=bool(carry)), name="attn_bwd",
    )(z, z, z, dcat, bias, gq2, gk2, *(carry.ins if carry else []))
    return outs[:6], outs[6:]


def _mem_softmax(qh, kn):
    s = lax.dot_general(qh, kn, _DIMS["nt"], preferred_element_type=F32)
    m = jnp.max(s, axis=-1, keepdims=True)
    p = jnp.exp(s - m)
    return p * (1.0 / jnp.sum(p, axis=-1, keepdims=True))


def _memattn_fwd(z, kv, cat, gq4, gk4, qcol, name):
    def body(q_ref, k_ref, v_ref, gq_ref, gk_ref, cat_ref, o_ref):
        del cat_ref
        masks = _group_masks(MEMW)
        qn = (_head_norm(q_ref[...], gq_ref[...], masks) * SCALE).astype(BF16)
        kn = _head_norm(k_ref[...], gk_ref[...], masks).astype(BF16)
        vv = v_ref[...].astype(BF16)
        o = jnp.zeros((TR, MEMW), F32)
        for h in range(4):
            qh = jnp.where(masks[h], qn, jnp.zeros_like(qn))
            vh = jnp.where(masks[h], vv, jnp.zeros_like(vv))
            p = _mem_softmax(qh, kn).astype(BF16)
            o = o + jnp.dot(p, vh, preferred_element_type=F32)
        o_ref[...] = o.astype(BF16)

    nt = SEQ // TR
    vec = BS((1, MEMW), lambda b, t: (0, 0))
    return pl.pallas_call(
        body, grid=(NB, nt),
        in_specs=[BS((TR, MEMW), lambda b, t: (b * nt + t, qcol)),
                  BS((MEMT, MEMW), lambda b, t: (b, 0)),
                  BS((MEMT, MEMW), lambda b, t: (b, 1)), vec, vec, ANY],
        out_specs=BS((TR, MEMW), lambda b, t: (b * nt + t, 3)),
        out_shape=_sds((T, D), BF16), input_output_aliases={5: 0},
        compiler_params=_cp(("arbitrary", "arbitrary")), name=name,
    )(z, kv, kv, gq4, gk4, cat)


def _memattn_bwd(z, kv, dcat, gq4, gk4, qcol, name):
    nt = SEQ // TR

    def body(q_ref, k_ref, v_ref, do_ref, gq_ref, gk_ref,
             dq_ref, dkv_ref, dgq_ref, dgk_ref, dkn_s, dv_s):
        b = pl.program_id(0)
        t = pl.program_id(1)
        masks = _group_masks(MEMW)
        qz = q_ref[...]
        kz = k_ref[...]
        qn = (_head_norm(qz, gq_ref[...], masks) * SCALE).astype(BF16)
        kn = _head_norm(kz, gk_ref[...], masks).astype(BF16)
        vv = v_ref[...].astype(BF16)
        dob = do_ref[...].astype(BF16)

        @pl.when(t == 0)
        def _():
            dkn_s[...] = jnp.zeros_like(dkn_s)
            dv_s[...] = jnp.zeros_like(dv_s)

        @pl.when((t == 0) & (b == 0))
        def _():
            dgq_ref[...] = jnp.zeros_like(dgq_ref)
            dgk_ref[...] = jnp.zeros_like(dgk_ref)

        dqn = jnp.zeros((TR, MEMW), F32)
        dkn = jnp.zeros((MEMT, MEMW), F32)
        dvv = jnp.zeros((MEMT, MEMW), F32)
        for h in range(4):
            qh = jnp.where(masks[h], qn, jnp.zeros_like(qn))
            kh = jnp.where(masks[h], kn, jnp.zeros_like(kn))
            doh = jnp.where(masks[h], dob, jnp.zeros_like(dob))
            p = _mem_softmax(qh, kn)
            dvv = dvv + lax.dot_general(p.astype(BF16), doh, _DIMS["tn"], preferred_element_type=F32)
            dp = lax.dot_general(doh, vv, _DIMS["nt"], preferred_element_type=F32)
            ds = p * (dp - jnp.sum(dp * p, axis=-1, keepdims=True))
            dsb = ds.astype(BF16)
            dqn = dqn + jnp.dot(dsb, kh, preferred_element_type=F32)
            dkn = dkn + lax.dot_general(dsb, qh, _DIMS["tn"], preferred_element_type=F32)
        dkn_s[...] += dkn
        dv_s[...] += dvv
        dq, dgq = _head_norm_bwd(dqn * SCALE, qz, gq_ref[...], masks)
        dq_ref[...] = dq.astype(BF16)
        dgq_ref[...] += _fold_heads(dgq, MEMW)

        @pl.when(t == nt - 1)
        def _():
            dk, dgk = _head_norm_bwd(dkn_s[...], kz, gk_ref[...], masks)
            dkv_ref[:, 0:MEMW] = dk
            dkv_ref[:, MEMW:] = dv_s[...]
            dgk_ref[...] += _fold_heads(dgk, MEMW)

    vec = BS((1, MEMW), lambda b, t: (0, 0))
    return pl.pallas_call(
        body, grid=(NB, nt),
        in_specs=[BS((TR, MEMW), lambda b, t: (b * nt + t, qcol)),
                  BS((MEMT, MEMW), lambda b, t: (b, 0)),
                  BS((MEMT, MEMW), lambda b, t: (b, 1)),
                  BS((TR, MEMW), lambda b, t: (b * nt + t, 3)), vec, vec],
        out_specs=[BS((TR, MEMW), lambda b, t: (b * nt + t, 0)),
                   BS((MEMT, 2 * MEMW), lambda b, t: (b, 0)), vec, vec],
        out_shape=[_sds((T, MEMW), BF16), _sds((NB * MEMT, 2 * MEMW), F32),
                   _sds((1, MEMW), F32), _sds((1, MEMW), F32)],
        scratch_shapes=[pltpu.VMEM((MEMT, MEMW), F32), pltpu.VMEM((MEMT, MEMW), F32)],
        compiler_params=_cp(("arbitrary", "arbitrary")), name=name,
    )(z, kv, kv, dcat, gq4, gk4)


HALO = 32
NEXT = 64
RT = 64


def _glu(zz):
    return zz[:, :TOK] * jax.nn.sigmoid(zz[:, TOK:])


def _layer_norm_parts(y):
    mu = jnp.mean(y, axis=-1, keepdims=True)
    yc = y - mu
    rstd = lax.rsqrt(jnp.mean(yc * yc, axis=-1, keepdims=True) + EPS)
    return yc * rstd, rstd


def _shifted_copies(src, dst, rows):
    for b in range(1, 8):
        dst[b - 1, 0:rows, :] = src[b:b + rows, :]


def _tap(src, shifted, off, r0, rows):
    b = off % 8
    if b == 0:
        return src[r0 + off:r0 + off + rows, :]
    return shifted[b - 1, r0 + off - b:r0 + off - b + rows, :]


def _conv_rows(w_ref, hbuf, hs, r0, rows):
    y = jnp.zeros((rows, TOK), F32)
    for j in range(CONVW):
        y = y + w_ref[j:j + 1, :] * _tap(hbuf, hs, (HALO - CONVW + 1) + j, r0, rows)
    return y


def _conv_fwd(z, cw, cb, lg, lb):
    nt = SEQ // TR

    def body(zc_ref, zp_ref, w_ref, cb_ref, lg_ref, lb_ref, o_ref, hbuf, hs):
        t = pl.program_id(1)
        hbuf[0:HALO, :] = jnp.where(t == 0, 0.0, _glu(zp_ref[...]))
        hbuf[HALO:, :] = _glu(zc_ref[...])
        _shifted_copies(hbuf, hs, HALO + TR - 8)
        for r0 in range(0, TR, RT):
            y = _conv_rows(w_ref, hbuf, hs, r0, RT) + cb_ref[...]
            yh, _ = _layer_norm_parts(y)
            o = yh * lg_ref[...] + lb_ref[...]
            o_ref[r0:r0 + RT, :] = (o * jax.nn.sigmoid(o)).astype(BF16)

    vec = BS((1, TOK), lambda b, t: (0, 0))
    per = TR // HALO
    return pl.pallas_call(
        body, grid=(NB, nt),
        in_specs=[BS((TR, 2 * TOK), lambda b, t: (b * nt + t, 0)),
                  BS((HALO, 2 * TOK), lambda b, t: (jnp.maximum((b * nt + t) * per - 1, 0), 0)),
                  BS((32, TOK), lambda b, t: (0, 0)), vec, vec, vec],
        out_specs=BS((TR, TOK), lambda b, t: (b * nt + t, 0)),
        out_shape=_sds((T, D), BF16),
        scratch_shapes=[pltpu.VMEM((HALO + TR, TOK), F32), pltpu.VMEM((7, HALO + TR, TOK), F32)],
        compiler_params=_cp(("arbitrary", "arbitrary")), name="conv_fwd",
    )(z, z, cw, cb, lg, lb)


def _conv_bwd(z, dcat, cw, cb, lg, lb):
    nt = SEQ // TR
    ext = TR + NEXT

    def body(zc_ref, zp_ref, zn_ref, dc_ref, dn_ref, w_ref, cb_ref, lg_ref, lb_ref,
             du_ref, dw_ref, dcb_ref, dlg_ref, dlb_ref, dbin_ref, hbuf, dybuf, hs, dys):
        b = pl.program_id(0)
        t = pl.program_id(1)

        @pl.when((b == 0) & (t == 0))
        def _():
            dw_ref[...] = jnp.zeros_like(dw_ref)
            dcb_ref[...] = jnp.zeros_like(dcb_ref)
            dlg_ref[...] = jnp.zeros_like(dlg_ref)
            dlb_ref[...] = jnp.zeros_like(dlb_ref)
            dbin_ref[...] = jnp.zeros_like(dbin_ref)

        hbuf[0:HALO, :] = jnp.where(t == 0, 0.0, _glu(zp_ref[...]))
        hbuf[HALO:HALO + TR, :] = _glu(zc_ref[...])
        hbuf[HALO + TR:, :] = _glu(zn_ref[...])
        _shifted_copies(hbuf, hs, HALO + TR + NEXT - 8)
        last = t == nt - 1
        for r0 in range(0, ext, RT):
            y = _conv_rows(w_ref, hbuf, hs, r0, RT) + cb_ref[...]
            yh, rstd = _layer_norm_parts(y)
            o = yh * lg_ref[...] + lb_ref[...]
            sg = jax.nn.sigmoid(o)
            if r0 < TR:
                dtok = dc_ref[r0:r0 + RT, :]
            else:
                dtok = jnp.where(last, 0.0, dn_ref[...])
            do = dtok * (sg * (1.0 + o * (1.0 - sg)))
            dyh = do * lg_ref[...]
            dy = rstd * (dyh - jnp.mean(dyh, axis=-1, keepdims=True)
                         - yh * jnp.mean(dyh * yh, axis=-1, keepdims=True))
            dybuf[r0:r0 + RT, :] = dy
            if r0 < TR:
                dlg_ref[...] += jnp.sum(do * yh, axis=0, keepdims=True)
                dlb_ref[...] += jnp.sum(do, axis=0, keepdims=True)
                dcb_ref[...] += jnp.sum(dy, axis=0, keepdims=True)
        _shifted_copies(dybuf, dys, ext - 8)
        for r0 in range(0, TR, RT):
            dh = jnp.zeros((RT, TOK), F32)
            for j in range(CONVW):
                dh = dh + w_ref[j:j + 1, :] * _tap(dybuf, dys, (CONVW - 1) - j, r0, RT)
            a = zc_ref[r0:r0 + RT, 0:TOK]
            sg = jax.nn.sigmoid(zc_ref[r0:r0 + RT, TOK:])
            da = dh * sg
            dg = dh * a * (sg * (1.0 - sg))
            du_ref[r0:r0 + RT, 0:TOK] = da.astype(BF16)
            du_ref[r0:r0 + RT, TOK:] = dg.astype(BF16)
            dbin_ref[:, 0:TOK] += jnp.sum(da, axis=0, keepdims=True)
            dbin_ref[:, TOK:] += jnp.sum(dg, axis=0, keepdims=True)
        for j in range(CONVW):
            acc = jnp.zeros((8, TOK), F32)
            for r0 in range(0, TR, RT):
                prod = dybuf[r0:r0 + RT, :] * _tap(hbuf, hs, (HALO - CONVW + 1) + j, r0, RT)
                acc = acc + jnp.sum(prod.reshape(RT // 8, 8, TOK), axis=0)
            dw_ref[j:j + 1, :] += jnp.sum(acc, axis=0, keepdims=True)

    vec = BS((1, TOK), lambda b, t: (0, 0))
    perh = TR // HALO
    pern = TR // NEXT
    nlast_n = T // NEXT - 1
    return pl.pallas_call(
        body, grid=(NB, nt),
        in_specs=[BS((TR, 2 * TOK), lambda b, t: (b * nt + t, 0)),
                  BS((HALO, 2 * TOK), lambda b, t: (jnp.maximum((b * nt + t) * perh - 1, 0), 0)),
                  BS((NEXT, 2 * TOK), lambda b, t: (jnp.minimum((b * nt + t + 1) * pern, nlast_n), 0)),
                  BS((TR, TOK), lambda b, t: (b * nt + t, 0)),
                  BS((NEXT, TOK), lambda b, t: (jnp.minimum((b * nt + t + 1) * pern, nlast_n), 0)),
                  BS((32, TOK), lambda b, t: (0, 0)), vec, vec, vec],
        out_specs=[BS((TR, 2 * TOK), lambda b, t: (b * nt + t, 0)),
                   BS((32, TOK), lambda b, t: (0, 0)), vec, vec, vec,
                   BS((1, 2 * TOK), lambda b, t: (0, 0))],
        out_shape=[_sds((T, 2 * TOK), BF16), _sds((32, TOK), F32), _sds((1, TOK), F32),
                   _sds((1, TOK), F32), _sds((1, TOK), F32), _sds((1, 2 * TOK), F32)],
        scratch_shapes=[pltpu.VMEM((HALO + TR + NEXT, TOK), F32), pltpu.VMEM((ext, TOK), F32),
                        pltpu.VMEM((7, HALO + TR + NEXT, TOK), F32), pltpu.VMEM((7, ext, TOK), F32)],
        compiler_params=_cp(("arbitrary", "arbitrary"), vmem_mb=56), name="conv_bwd",
    )(z, z, z, dcat, dcat, cw, cb, lg, lb)


def _ffn_up(h2, wgu, l, carry=None):
    ni = len(carry.ins) if carry else 0
    no = len(carry.out_shapes) if carry else 0

    def body(h_ref, wg_ref, wu_ref, *rest):
        cin = rest[:ni]
        g_ref, u_ref, a_ref = rest[ni:ni + 3]
        cout, cscr = rest[ni + 3:ni + 3 + no], rest[ni + 3 + no:]
        if carry:
            @pl.when((pl.program_id(0) == 0) & (pl.program_id(1) == 0))
            def _():
                carry.start(cin, cout, cscr)

        hv = h_ref[...]
        g = lax.dot_general(hv, wg_ref[...], _DIMS["nt"], preferred_element_type=F32)
        u = lax.dot_general(hv, wu_ref[...], _DIMS["nt"], preferred_element_type=F32)
        sg = jax.nn.sigmoid(g)
        silu = g * sg
        g_ref[...] = (u * (sg * (1.0 + g * (1.0 - sg)))).astype(BF16)
        u_ref[...] = silu.astype(BF16)
        a_ref[...] = (silu * u).astype(BF16)

        if carry:
            @pl.when((pl.program_id(0) == FF // FT - 1) & (pl.program_id(1) == T // TR - 1))
            def _():
                carry.finish(cin, cout, cscr)

    out = BS((TR, FT), lambda q, i: (i, q))
    outs = pl.pallas_call(
        body, grid=(FF // FT, T // TR),
        in_specs=[BS((TR, D), lambda q, i: (i, 0)),
                  BS((None, FT, D), lambda q, i: (0, q, 0)),
                  BS((None, FT, D), lambda q, i: (1, q, 0))] + (carry.in_specs if carry else []),
        out_specs=[out, out, out] + [ANY] * no,
        out_shape=[_sds((T, FF), BF16), _sds((T, FF), BF16), _sds((T, FF), BF16)]
        + (carry.out_shapes if carry else []),
        scratch_shapes=carry.scratch if carry else [],
        compiler_params=pltpu.CompilerParams(
            dimension_semantics=("arbitrary", "arbitrary"), vmem_limit_bytes=48 << 20,
            has_side_effects=bool(carry)), name=f"ffn_up_{l}",
    )(h2, wgu, wgu, *(carry.ins if carry else []))
    return outs[:3], outs[3:]


def _ffn_down_bwd(dx, wd, g, u, l):
    def epilogue(dact, ex, o_ref, i):
        o_ref[0] = (dact * ex[0][...].astype(F32)).astype(BF16)
        o_ref[1] = (dact * ex[1][...].astype(F32)).astype(BF16)

    ex_spec = BS((TR, FT), lambda i, q, k: (i, q))
    return _mm("nt", dx, wd, grid=(T // TR, FF // FT, 1),
               a_spec=BS((TR, D), lambda i, q, k: (i, 0)),
               b_spec=BS((FT, D), lambda i, q, k: (q, 0)),
               out_shape=_sds((2, T, FF), BF16),
               out_spec=BS((2, TR, FT), lambda i, q, k: (0, i, q)),
               acc_shape=(TR, FT), extras=(g, u), extra_specs=(ex_spec, ex_spec),
               epilogue=epilogue, name=f"ffn_down_bwd_{l}")


def _row_tile(rows, cols, itemsize=4, limit=2 << 20):
    tr = rows
    while tr * cols * itemsize > limit and tr % 2 == 0 and (tr // 2) % 16 == 0:
        tr //= 2
    return tr


def _cast_bf16(arrs, name):
    n = len(arrs)
    rows, cols = arrs[0].shape
    tr = _row_tile(rows, cols)

    def body(*refs):
        o_ref = refs[n]
        k = pl.program_id(0)
        val = refs[0][...]
        for j in range(1, n):
            val = jnp.where(k == j, refs[j][...], val)
        o_ref[...] = val.astype(BF16)

    return pl.pallas_call(
        body, grid=(n, rows // tr),
        in_specs=[BS((tr, cols), lambda k, i: (i, 0))] * n,
        out_specs=BS((None, tr, cols), lambda k, i: (k, i, 0)),
        out_shape=_sds((n, rows, cols), BF16),
        compiler_params=_cp(("arbitrary", "arbitrary")), name=name,
    )(*arrs)


def _quad_sum(own, got, name):
    n, rows, cols = own.shape
    tr = _row_tile(rows, cols)

    def body(a_ref, q_ref, o_ref):
        o_ref[...] = ((a_ref[...] + q_ref[0].astype(F32)) + q_ref[1].astype(F32)) + q_ref[2].astype(F32)

    spec = BS((None, tr, cols), lambda k, i: (k, i, 0))
    return pl.pallas_call(
        body, grid=(n, rows // tr),
        in_specs=[spec, BS((3, None, tr, cols), lambda k, i: (0, k, i, 0))], out_specs=spec,
        out_shape=_sds((n, rows, cols), F32),
        compiler_params=_cp(("arbitrary", "arbitrary")), name=name,
    )(own, got)


def _adam_math(w, g, m, v):
    m = ADAM_B1 * m + (1.0 - ADAM_B1) * g
    v = ADAM_B2 * v + (1.0 - ADAM_B2) * (g * g)
    m_hat = m / (1.0 - ADAM_B1 ** ADAM_STEP)
    v_hat = v / (1.0 - ADAM_B2 ** ADAM_STEP)
    delta = -ADAM_LR * (m_hat / (jnp.sqrt(v_hat) + ADAM_EPS) + ADAM_WD * w)
    return delta, m, v


def _adamw_big(w, g, m, v, name):
    shape = w.shape
    cols = shape[-1]
    rows = w.size // cols
    tr = _row_tile(rows, cols, limit=1 << 20)

    def body(w_ref, g_ref, m_ref, v_ref, d_ref, nm_ref, nv_ref):
        d, nm, nv = _adam_math(w_ref[...], g_ref[...], m_ref[...], v_ref[...])
        d_ref[...] = d
        nm_ref[...] = nm
        nv_ref[...] = nv

    spec = BS((tr, cols), lambda i: (i, 0))
    outs = pl.pallas_call(
        body, grid=(rows // tr,), in_specs=[spec] * 4, out_specs=[spec] * 3,
        out_shape=[_sds((rows, cols), F32)] * 3,
        compiler_params=_cp(("arbitrary",)), name=name,
    )(*[a.reshape(rows, cols) for a in (w, g, m, v)])
    return [o.reshape(shape) for o in outs]


def _adamw_small(ws, gs, ms, vs):
    n = len(ws)

    def body(*refs):
        for i in range(n):
            d, nm, nv = _adam_math(refs[i][...], refs[n + i][...], refs[2 * n + i][...],
                                   refs[3 * n + i][...])
            refs[4 * n + i][...] = d
            refs[5 * n + i][...] = nm
            refs[6 * n + i][...] = nv

    specs = [BS(w.shape, lambda i: (0, 0)) for w in ws]
    outs = pl.pallas_call(
        body, grid=(1,), in_specs=specs * 4, out_specs=specs * 3,
        out_shape=[_sds(w.shape, F32) for w in ws] * 3,
        compiler_params=_cp(("arbitrary",)), name="adamw_small",
    )(*ws, *gs, *ms, *vs)
    return outs[:n], outs[n:2 * n], outs[2 * n:]


def _place():
    x, y, c = lax.axis_index("x"), lax.axis_index("y"), lax.axis_index("c")
    chips = [(1 - x, y), (x, 1 - y), (1 - x, 1 - y)]
    return x, y, c, chips


class _Exchange:
    def __init__(self, ins, in_specs, out_shapes, scratch, start, finish):
        self.ins, self.in_specs, self.out_shapes, self.scratch = ins, in_specs, out_shapes, scratch
        self.start, self.finish = start, finish


def _run_exchange(ex, name, vmem_mb=40):
    ni, no = len(ex.ins), len(ex.out_shapes)

    def body(*refs):
        ex.start(refs[:ni], refs[ni:ni + no], refs[ni + no:])
        ex.finish(refs[:ni], refs[ni:ni + no], refs[ni + no:])

    return pl.pallas_call(
        body, in_specs=ex.in_specs, out_specs=[ANY] * no, out_shape=ex.out_shapes,
        scratch_shapes=ex.scratch,
        compiler_params=pltpu.CompilerParams(has_side_effects=True, vmem_limit_bytes=vmem_mb << 20),
        name=name,
    )(*ex.ins)


def _gather_exchange(srcs, dst_shapes, views, small=None):
    nu = len(srcs)
    nd = len(dst_shapes)
    ns = 1 if small is not None else 0

    def unpack(ins, outs, scr):
        x, y, c, chips = _place()
        src = ins[:nu]
        vw = [views[u](outs[:nd]) for u in range(nu)]
        vbuf = scr[:nu]
        send, recv, fsend, frecv, lsem, ssend, srecv, vsem = scr[nu:]

        def ici(u, j, shard, to):
            return pltpu.make_async_remote_copy(
                src_ref=vbuf[u].at[:, c], dst_ref=vw[u].at[:, shard, c],
                send_sem=send.at[3 * u + j], recv_sem=recv.at[3 * u + j],
                device_id=to, device_id_type=MESH)

        def fwd(u, j, shard, half):
            return pltpu.make_async_remote_copy(
                src_ref=vw[u].at[:, shard, half], dst_ref=vw[u].at[:, shard, half],
                send_sem=fsend.at[3 * u + j], recv_sem=frecv.at[3 * u + j],
                device_id=(x, y, 1 - c), device_id_type=MESH)

        def small_copy(j, shard, to):
            return pltpu.make_async_remote_copy(
                src_ref=ins[nu], dst_ref=outs[nd].at[shard],
                send_sem=ssend.at[j], recv_sem=srecv.at[j], device_id=to, device_id_type=MESH)

        stage = [pltpu.make_async_copy(src[u], vbuf[u], vsem.at[u]) for u in range(nu)]
        local = [pltpu.make_async_copy(vbuf[u], vw[u].at[:, 2 * x + y], lsem.at[u]) for u in range(nu)]
        if ns:
            local.append(pltpu.make_async_copy(ins[nu], outs[nd].at[2 * x + y], lsem.at[nu]))
        return x, y, c, chips, ici, fwd, small_copy, stage, local

    def start(ins, outs, scr):
        x, y, c, chips, ici, fwd, small_copy, stage, local = unpack(ins, outs, scr)
        s = 2 * x + y
        for cp in stage:
            cp.start()
        if ns:
            local[nu].start()
            for j, chip in enumerate(chips):
                small_copy(j, s, (*chip, c)).start()
        for u in range(nu):
            stage[u].wait()
            for j, chip in enumerate(chips):
                ici(u, j, s, (*chip, c)).start()
            local[u].start()

    def finish(ins, outs, scr):
        x, y, c, chips, ici, fwd, small_copy, stage, local = unpack(ins, outs, scr)
        s = 2 * x + y
        for u in range(nu):
            for j, chip in enumerate(chips):
                sj = 2 * chip[0] + chip[1]
                ici(u, j, sj, (x, y, c)).wait_recv()
                fwd(u, j, sj, c).start()
        for u in range(nu):
            for j, chip in enumerate(chips):
                sj = 2 * chip[0] + chip[1]
                fwd(u, j, sj, 1 - c).wait_recv()
        for u in range(nu):
            for j, chip in enumerate(chips):
                ici(u, j, s, (*chip, c)).wait_send()
                fwd(u, j, s, c).wait_send()
        if ns:
            for j, chip in enumerate(chips):
                small_copy(j, 2 * chip[0] + chip[1], (x, y, c)).wait_recv()
                small_copy(j, s, (*chip, c)).wait_send()
        for cp in local:
            cp.wait()

    dma = pltpu.SemaphoreType.DMA
    return _Exchange(
        ins=list(srcs) + ([small] if ns else []),
        in_specs=[ANY] * nu + [BS(memory_space=pltpu.VMEM)] * ns,
        out_shapes=[_sds(sh, BF16) for sh in dst_shapes]
        + ([_sds((NSH,) + small.shape, F32)] if ns else []),
        scratch=[pltpu.VMEM(a.shape, BF16) for a in srcs]
        + [dma((3 * nu,)), dma((3 * nu,)), dma((3 * nu,)), dma((3 * nu,)),
           dma((nu + 1,)), dma((3,)), dma((3,)), dma((nu,))],
        start=start, finish=finish)


def _pair_reduce(gs, name):
    nu = len(gs)
    ns = [g.shape[0] * NSH for g in gs]
    base = [sum(ns[:u]) for u in range(nu)]

    def body(*refs):
        g_refs, own_refs, sb_refs = refs[:nu], refs[nu:2 * nu], refs[2 * nu:3 * nu]
        bufs = refs[3 * nu:8 * nu]
        send, recv, lsem, osem = refs[8 * nu:]
        x, y, c, _ = _place()
        s = 2 * x + y

        def unit(u):
            sendb, recvb, stage, outf, outb = bufs[5 * u:5 * u + 5]

            def load(k, half):
                return pltpu.make_async_copy(g_refs[u].at[k // NSH, k % NSH, half], stage.at[k % 2],
                                             lsem.at[2 * u + k % 2])

            def push(k):
                return pltpu.make_async_remote_copy(
                    src_ref=sendb.at[k], dst_ref=recvb.at[k], send_sem=send.at[base[u] + k],
                    recv_sem=recv.at[base[u] + k], device_id=(x, y, 1 - c), device_id_type=MESH)

            def store(k):
                return pltpu.make_async_copy(outb.at[k % 2], sb_refs[u].at[k // NSH, k % NSH],
                                             osem.at[3 * u + k % 2])

            return sendb, recvb, stage, outf, outb, load, push, store

        for u in range(nu):
            sendb, recvb, stage, outf, outb, load, push, store = unit(u)
            load(0, 1 - c).start()
            for k in range(ns[u]):
                if k + 1 < ns[u]:
                    load(k + 1, 1 - c).start()
                load(k, 1 - c).wait()
                sendb[k] = stage[k % 2].astype(BF16)
                push(k).start()
        for u in range(nu):
            sendb, recvb, stage, outf, outb, load, push, store = unit(u)
            n = ns[u]
            load(0, c).start()
            for k in range(n):
                if k + 1 < n:
                    load(k + 1, c).start()
                load(k, c).wait()
                push(k).wait_recv()
                total = stage[k % 2] + recvb[k].astype(F32)
                if k >= 2:
                    store(k - 2).wait()
                outb[k % 2] = total.astype(BF16)
                store(k).start()

                @pl.when(s == k % NSH)
                def _():
                    outf[...] = total
                    keep = pltpu.make_async_copy(outf, own_refs[u].at[k // NSH], osem.at[3 * u + 2])
                    keep.start()
                    keep.wait()

            for k in range(max(n - 2, 0), n):
                store(k).wait()
        for u in range(nu):
            push = unit(u)[6]
            for k in range(ns[u]):
                push(k).wait_send()

    dma = pltpu.SemaphoreType.DMA
    scratch = []
    for g, n in zip(gs, ns):
        rh, cc = g.shape[3], g.shape[4]
        scratch += [pltpu.VMEM((n, rh, cc), BF16), pltpu.VMEM((n, rh, cc), BF16),
                    pltpu.VMEM((2, rh, cc), F32), pltpu.VMEM((rh, cc), F32), pltpu.VMEM((2, rh, cc), BF16)]
    outs = pl.pallas_call(
        body, in_specs=[ANY] * nu, out_specs=[ANY] * (2 * nu),
        out_shape=[_sds((g.shape[0], g.shape[3], g.shape[4]), F32) for g in gs]
        + [_sds((g.shape[0], NSH, g.shape[3], g.shape[4]), BF16) for g in gs],
        scratch_shapes=scratch + [dma((sum(ns),)), dma((sum(ns),)), dma((2 * nu,)), dma((3 * nu,))],
        compiler_params=pltpu.CompilerParams(has_side_effects=True, vmem_limit_bytes=56 << 20),
        name=name,
    )(*gs)
    return list(outs[:nu]), list(outs[nu:])


def _chip_exchange(sums_bf16):
    nu = len(sums_bf16)

    def pushes(ins, outs, scr):
        x, y, c, chips = _place()
        send, recv = scr
        return [pltpu.make_async_remote_copy(
            src_ref=ins[u].at[:, 2 * chip[0] + chip[1]], dst_ref=outs[u].at[j],
            send_sem=send.at[3 * u + j], recv_sem=recv.at[3 * u + j],
            device_id=(*chip, c), device_id_type=MESH)
            for u in range(nu) for j, chip in enumerate(chips)]

    def start(ins, outs, scr):
        for cp in pushes(ins, outs, scr):
            cp.start()

    def finish(ins, outs, scr):
        for cp in pushes(ins, outs, scr):
            cp.wait()

    dma = pltpu.SemaphoreType.DMA
    shapes = [(3, a.shape[0], a.shape[2], a.shape[3]) for a in sums_bf16]
    return _Exchange(ins=list(sums_bf16), in_specs=[ANY] * nu,
                     out_shapes=[_sds(sh, BF16) for sh in shapes],
                     scratch=[dma((3 * nu,)), dma((3 * nu,))], start=start, finish=finish)


def _final_exchange(halves, out_shapes, targets):
    nu = len(halves)
    no = len(out_shapes)
    ncp = sum(len(t) for t in targets)

    def body(*refs):
        hv = refs[:nu]
        out = refs[nu:nu + no]
        sbuf = refs[nu + no:2 * nu + no]
        rbuf = refs[2 * nu + no:3 * nu + no]
        send, recv, lsem, osem, csem = refs[3 * nu + no:]
        x, y, c, _ = _place()
        stage = [pltpu.make_async_copy(hv[u], sbuf[u], lsem.at[u]) for u in range(nu)]
        push = [pltpu.make_async_remote_copy(
            src_ref=sbuf[u], dst_ref=rbuf[u], send_sem=send.at[u], recv_sem=recv.at[u],
            device_id=(x, y, 1 - c), device_id_type=MESH) for u in range(nu)]
        mine, theirs = [], []
        k = 0
        for u in range(nu):
            rh = hv[u].shape[1]
            for (p, oi, li) in targets[u]:
                mine.append((u, pltpu.make_async_copy(
                    sbuf[u].at[p], out[oi].at[li, pl.ds(c * rh, rh), :], csem.at[k])))
                theirs.append((u, pltpu.make_async_copy(
                    rbuf[u].at[p], out[oi].at[li, pl.ds((1 - c) * rh, rh), :], osem.at[k])))
                k += 1
        for cp in stage:
            cp.start()
        for u in range(nu):
            stage[u].wait()
            push[u].start()
            for (v, cp) in mine:
                if v == u:
                    cp.start()
        for u in range(nu):
            push[u].wait_recv()
            for (v, cp) in theirs:
                if v == u:
                    cp.start()
        for (_, cp) in theirs + mine:
            cp.wait()
        for u in range(nu):
            push[u].wait_send()

    dma = pltpu.SemaphoreType.DMA
    bufs = [pltpu.VMEM(h.shape, F32) for h in halves]
    return pl.pallas_call(
        body, in_specs=[ANY] * nu, out_specs=[ANY] * no,
        out_shape=[_sds(sh, F32) for sh in out_shapes],
        scratch_shapes=bufs + bufs + [dma((nu,)), dma((nu,)), dma((nu,)), dma((ncp,)), dma((ncp,))],
        compiler_params=pltpu.CompilerParams(has_side_effects=True, vmem_limit_bytes=56 << 20),
        name="final_exchange",
    )(*halves)


def _small_allreduce(pack):
    rows = pack.shape[0]

    def body(p_ref, o_ref, buf, send, recv):
        x, y, c, _ = _place()
        me = 4 * x + 2 * y + c
        buf[me] = p_ref[...]
        k = 0
        copies = []
        for dx in range(2):
            for dy in range(2):
                for dc in range(2):
                    if dx == 0 and dy == 0 and dc == 0:
                        continue
                    to = (jnp.where(dx, 1 - x, x), jnp.where(dy, 1 - y, y), jnp.where(dc, 1 - c, c))
                    src_slot = 4 * to[0] + 2 * to[1] + to[2]
                    copies.append((pltpu.make_async_remote_copy(
                        src_ref=p_ref, dst_ref=buf.at[me], send_sem=send.at[k], recv_sem=recv.at[k],
                        device_id=to, device_id_type=MESH), src_slot, k))
                    k += 1
        for cp, _, _ in copies:
            cp.start()
        for cp, src_slot, k in copies:
            pltpu.make_async_remote_copy(
                src_ref=p_ref, dst_ref=buf.at[src_slot], send_sem=send.at[k], recv_sem=recv.at[k],
                device_id=(x, y, c), device_id_type=MESH).wait()
        acc = buf[0]
        for d in range(1, 8):
            acc = acc + buf[d]
        o_ref[...] = acc

    dma = pltpu.SemaphoreType.DMA
    vm = BS(memory_space=pltpu.VMEM)
    return pl.pallas_call(
        body, in_specs=[vm], out_specs=vm, out_shape=_sds((rows, D), F32),
        scratch_shapes=[pltpu.VMEM((8, rows, D), F32), dma((7,)), dma((7,))],
        compiler_params=pltpu.CompilerParams(has_side_effects=True, vmem_limit_bytes=32 << 20),
        name="small_allreduce",
    )(pack)


def _in_proj(h, w, bias, name, transposed=False):
    n = w.shape[0] if transposed else w.shape[1]
    tn = 640 if n == NA else 896
    ep = None
    extras, especs = (), ()
    if bias is not None:
        def ep(acc, ex, o_ref, i):
            o_ref[...] = acc + ex[0][...]
        extras = (bias,)
        especs = (BS((1, tn), lambda i, j, k: (0, j)),)
    b_spec = BS((tn, D), lambda i, j, k: (j, 0)) if transposed else BS((D, tn), lambda i, j, k: (0, j))
    return _mm("nt" if transposed else "nn", h, w, grid=(T // TR, n // tn, 1),
               a_spec=BS((TR, D), lambda i, j, k: (i, 0)), b_spec=b_spec,
               out_shape=_sds((T, n), F32), out_spec=BS((TR, tn), lambda i, j, k: (i, j)),
               acc_shape=(TR, tn), extras=extras, extra_specs=especs, epilogue=ep, name=name)


def _res_rms_epilogue(acc, ex, outs, i):
    y = acc + ex[0][...]
    outs[0][...] = y
    r = lax.rsqrt(jnp.mean(y * y, axis=-1, keepdims=True) + EPS)
    outs[1][...] = (y * r * ex[1][...]).astype(BF16)


def _res_loss_epilogue(acc, ex, outs, i):
    e = acc + ex[0][...] - ex[1][...]
    outs[1][...] = e * (1.0 / D)

    @pl.when(i == 0)
    def _():
        outs[0][...] = jnp.zeros_like(outs[0])

    outs[0][...] += 0.5 * jnp.sum(jnp.mean(e * e, axis=-1, keepdims=True), axis=0, keepdims=True)


def _rms_bwd_epilogue(dh, ex, outs, i):
    xv = ex[0][...]
    r = lax.rsqrt(jnp.mean(xv * xv, axis=-1, keepdims=True) + EPS)
    xh = xv * r
    gy = dh * ex[1][...]
    outs[0][...] = r * (gy - xh * jnp.mean(gy * xh, axis=-1, keepdims=True)) + ex[2][...]

    @pl.when(i == 0)
    def _():
        outs[1][...] = jnp.zeros_like(outs[1])

    outs[1][...] += jnp.sum(dh * xh, axis=0, keepdims=True)


def _local_step(x, mem, target, w, p, carries=None, bwd_carry_fn=None):
    row = lambda i, j, k: (i, 0)
    whole = lambda i, j, k: (0, 0)
    w = {k: (list(v) if isinstance(v, list) else v) for k, v in w.items()}
    carries = carries or {}

    def carry_of(name):
        return carries[name][0] if name in carries else None

    def delivered(name, outs):
        if name in carries:
            carries[name][1](w, outs)

    saved = []
    bias = _bias_expand(p["rel_u"])
    vec = BS((1, D), whole)
    h = _rms_fwd(x, p["norm1_g"][0:1], "rms1_0")
    for l in range(2):
        type_a = l == 0
        memn = _rms_fwd(mem, p["mem_norm_g"][l:l + 1], f"rmsmem_{l}")
        if type_a:
            z = _in_proj(h, w["a"], None, "inproj_a")
            cat, carried = _attn_fwd(z, bias, p["a_q_g2"], p["a_k_g2"], carry_of("attn_fwd"))
            delivered("attn_fwd", carried)
            qcol = NA // MEMW - 1
        else:
            z = _in_proj(h, w["b"], p["b_b_in"], "inproj_b", transposed=True)
            cat = _conv_fwd(z, p["conv_w"], p["conv_b"], p["ln_g"], p["ln_b"])
            qcol = NBW // MEMW - 1
        kv = _mm("nn", memn, w["kv"][l], grid=(1, 1, 1),
                 a_spec=BS((NB * MEMT, D), whole), b_spec=BS((D, 2 * MEMW), whole),
                 out_shape=_sds((NB * MEMT, 2 * MEMW), F32), out_spec=BS((NB * MEMT, 2 * MEMW), whole),
                 acc_shape=(8, 128), name=f"memkv_{l}")
        cat = _memattn_fwd(z, kv, cat, p["mq_g4"][l:l + 1], p["mk_g4"][l:l + 1], qcol, f"memattn_fwd_{l}")
        x1, h2 = _mm("nn", cat, w["wo"][l], grid=(T // TR, 1, 1), a_spec=BS((TR, D), row),
                     b_spec=BS((D, D), whole),
                     out_shape=[_sds((T, D), F32), _sds((T, D), BF16)],
                     out_spec=[BS((TR, D), row), BS((TR, D), row)], acc_shape=(8, 128),
                     extras=(x, p["norm2_g"][l:l + 1]), extra_specs=(BS((TR, D), row), vec),
                     epilogue=_res_rms_epilogue, name=f"outproj_{l}")
        (g, u, act), carried = _ffn_up(h2, w["gu"][l], l, carry_of(f"ffn_up_{l}"))
        delivered(f"ffn_up_{l}", carried)
        last = l == 1
        res = _mm("nn", act, w["wd"][l], grid=(T // TR, 1, 1),
                  a_spec=BS((TR, FF), row), b_spec=BS((FF, D), whole),
                  out_shape=[_sds((1, 1), F32), _sds((T, D), F32)] if last else
                  [_sds((T, D), F32), _sds((T, D), BF16)],
                  out_spec=[BS((1, 1), whole), BS((TR, D), row)] if last else
                  [BS((TR, D), row), BS((TR, D), row)],
                  acc_shape=(8, 128), extras=(x1, target if last else p["norm1_g"][1:2]),
                  extra_specs=(BS((TR, D), row), BS((TR, D), row) if last else vec),
                  epilogue=_res_loss_epilogue if last else _res_rms_epilogue, sequential=last,
                  name=f"ffn_down_{l}", carry=carry_of(f"ffn_down_{l}"))
        if carry_of(f"ffn_down_{l}") is not None:
            res, carried = res
            delivered(f"ffn_down_{l}", carried)
        saved.append(dict(x=x, h=h, memn=memn, kv=kv, z=z, cat=cat, x1=x1, h2=h2, g=g, u=u, act=act,
                          qcol=qcol))
        if last:
            loss, dx = res
        else:
            x, h = res

    big = dict(a=None, b=None, kv=[None, None], wo=[None, None], gu=[None, None], wd=[None, None])
    small = {}
    bwd_carried = ()
    tk = T // 2
    nkt = T // tk
    for l in (1, 0):
        sv = saved[l]
        dgu = _ffn_down_bwd(dx, w["wd"][l], sv["g"], sv["u"], l)
        big["wd"][l] = _mm("tn", sv["act"], dx, grid=(FF // FT, 1, 2 * nkt),
                           a_spec=BS((tk // 2, FT), lambda i, j, k: (k, i)),
                           b_spec=BS((tk // 2, D), lambda i, j, k: (k, 0)),
                           out_shape=_sds((FF, D), F32), out_spec=BS((FT, D), lambda i, j, k: (i, 0)),
                           acc_shape=(FT, D), name=f"dw_down_{l}")
        dx1, small[f"norm2_g{l}"] = _mm(
            "nn", dgu, w["gu"][l], grid=(T // TR, 1, 2),
            a_spec=BS((None, TR, FF), lambda i, j, k: (k, i, 0)),
            b_spec=BS((None, FF, D), lambda i, j, k: (k, 0, 0)),
            out_shape=[_sds((T, D), F32), _sds((1, D), F32)], out_spec=[BS((TR, D), row), vec],
            acc_shape=(TR, D), extras=(sv["x1"], p["norm2_g"][l:l + 1], dx),
            extra_specs=(BS((TR, D), row), vec, BS((TR, D), row)),
            epilogue=_rms_bwd_epilogue, sequential=True, name=f"dh2_{l}")
        big["gu"][l] = _mm("tn", dgu, sv["h2"], grid=(2 * FF // FT, 1, 2 * nkt),
                           a_spec=BS((None, tk // 2, FT), lambda i, j, k: (i // 2, k, i % 2)),
                           b_spec=BS((tk // 2, D), lambda i, j, k: (k, 0)),
                           out_shape=_sds((2, FF, D), F32),
                           out_spec=BS((None, FT, D), lambda i, j, k: (i // 2, i % 2, 0)),
                           acc_shape=(FT, D), name=f"dw_gu_{l}")
        dcat = _mm("nt", dx1, w["wo"][l], grid=(T // TR, 1, 1), a_spec=BS((TR, D), row),
                   b_spec=BS((D, D), whole),
                   out_shape=_sds((T, D), F32), out_spec=BS((TR, D), row), acc_shape=(8, 128),
                   name=f"dcat_{l}")
        big["wo"][l] = _mm("tn", sv["cat"], dx1, grid=(1, 1, nkt),
                           a_spec=BS((tk, D), lambda i, j, k: (k, 0)), b_spec=BS((tk, D), lambda i, j, k: (k, 0)),
                           out_shape=_sds((D, D), F32), out_spec=BS((D, D), whole),
                           acc_shape=(D, D), name=f"dw_out_{l}")
        dqm, dkv, small[f"mq_g{l}"], small[f"mk_g{l}"] = _memattn_bwd(
            sv["z"], sv["kv"], dcat, p["mq_g4"][l:l + 1], p["mk_g4"][l:l + 1], sv["qcol"], f"memattn_bwd_{l}")
        if l == 0:
            carry = bwd_carry_fn(big) if bwd_carry_fn is not None else None
            (dq, dk, dv, dbias, small["a_q_g"], small["a_k_g"]), bwd_carried = _attn_bwd(
                sv["z"], dcat, bias, p["a_q_g2"], p["a_k_g2"], carry)
            small["rel_u"] = _bias_reduce(dbias)
            dz = jnp.concatenate([dq, dk, dv, dqm], axis=1)
            w_in, key, n, tn = w["a"], "a", NA, 640
        else:
            du, small["conv_w"], small["conv_b"], small["ln_g"], small["ln_b"], dbin_u = _conv_bwd(
                sv["z"], dcat, p["conv_w"], p["conv_b"], p["ln_g"], p["ln_b"])
            dz = jnp.concatenate([du, dqm], axis=1)
            small["b_in_u"] = dbin_u
            w_in, key, n, tn = w["b"], "b", NBW, 896
        norm_bwd = dict(out_shape=[_sds((T, D), F32), _sds((1, D), F32)], out_spec=[BS((TR, D), row), vec],
                        acc_shape=(8, 128), extras=(sv["x"], p["norm1_g"][l:l + 1], dx1),
                        extra_specs=(BS((TR, D), row), vec, BS((TR, D), row)),
                        epilogue=_rms_bwd_epilogue, sequential=True, name=f"dh_{l}")
        if l == 0:
            dx, small[f"norm1_g{l}"] = _mm("nt", dz, w_in, grid=(T // TR, 1, 1),
                                           a_spec=BS((TR, n), row), b_spec=BS((D, n), whole), **norm_bwd)
            big[key] = _mm("tn", sv["h"], dz, grid=(1, n // tn, nkt),
                           a_spec=BS((tk, D), lambda i, j, k: (k, 0)), b_spec=BS((tk, tn), lambda i, j, k: (k, j)),
                           out_shape=_sds((D, n), F32), out_spec=BS((D, tn), lambda i, j, k: (0, j)),
                           acc_shape=(D, tn), name=f"dw_in_{l}")
        else:
            dx, small[f"norm1_g{l}"] = _mm("nn", dz, w_in, grid=(T // TR, 1, 1),
                                           a_spec=BS((TR, n), row), b_spec=BS((n, D), whole), **norm_bwd)
            big[key] = _mm("tn", dz, sv["h"], grid=(n // tn, 1, nkt),
                           a_spec=BS((tk, tn), lambda i, j, k: (k, i)), b_spec=BS((tk, D), lambda i, j, k: (k, 0)),
                           out_shape=_sds((n, D), F32), out_spec=BS((tn, D), lambda i, j, k: (i, 0)),
                           acc_shape=(tn, D), name=f"dw_in_{l}")
        if l == 1:
            small["b_in_qm"] = _colsum(dqm, "colsum_dqm")
        big["kv"][l] = _mm("tn", sv["memn"], dkv, grid=(1, 1, 1),
                           a_spec=BS((NB * MEMT, D), whole), b_spec=BS((NB * MEMT, 2 * MEMW), whole),
                           out_shape=_sds((D, 2 * MEMW), F32), out_spec=BS((D, 2 * MEMW), whole),
                           acc_shape=(8, 128), name=f"dw_kv_{l}")
        dmemn = _mm("nt", dkv, w["kv"][l], grid=(1, 1, 1),
                    a_spec=BS((NB * MEMT, 2 * MEMW), whole), b_spec=BS((D, 2 * MEMW), whole),
                    out_shape=_sds((NB * MEMT, D), F32), out_spec=BS((NB * MEMT, D), whole),
                    acc_shape=(8, 128), name=f"dmemn_{l}")
        _, small[f"mem_norm_g{l}"] = _rms_bwd(dmemn, mem, p["mem_norm_g"][l:l + 1], None, f"rmsmem_bwd_{l}")
    return loss, dx, big, small, bwd_carried


def _colsum(a, name):
    rows, cols = a.shape

    def body(a_ref, o_ref):
        @pl.when(pl.program_id(0) == 0)
        def _():
            o_ref[...] = jnp.zeros_like(o_ref)

        o_ref[...] += jnp.sum(a_ref[...].astype(F32), axis=0, keepdims=True)

    return pl.pallas_call(
        body, grid=(rows // TR,), in_specs=[BS((TR, cols), lambda i: (i, 0))],
        out_specs=BS((1, cols), lambda i: (0, 0)), out_shape=_sds((1, cols), F32),
        compiler_params=_cp(("arbitrary",)), name=name,
    )(a)


_PACK_ROWS = 64


def _pad_to(a, rows, cols=D):
    return jnp.pad(a, ((0, rows - a.shape[0]), (0, cols - a.shape[1])))


def _pack_small(sm):
    parts = [
        jnp.concatenate([sm["norm1_g0"], sm["norm1_g1"]], 0),
        jnp.concatenate([sm["mem_norm_g0"], sm["mem_norm_g1"]], 0),
        jnp.concatenate([sm["norm2_g0"], sm["norm2_g1"]], 0),
        _pad_to(sm["a_q_g"], 1), _pad_to(sm["a_k_g"], 1),
        _pad_to(jnp.concatenate([sm["mq_g0"], sm["mq_g1"]], 0), 2),
        _pad_to(jnp.concatenate([sm["mk_g0"], sm["mk_g1"]], 0), 2),
        _pad_to(sm["conv_b"], 1), _pad_to(sm["ln_g"], 1), _pad_to(sm["ln_b"], 1),
        _pad_to(sm["b_in_u"][:, :D], 1), _pad_to(sm["b_in_u"][:, D:], 1),
        _pad_to(sm["b_in_qm"], 1),
        _pad_to(sm["conv_w"][:CONVW], CONVW),
        sm["rel_u"].reshape(12, D),
    ]
    pack = jnp.concatenate(parts, 0)
    return jnp.pad(pack, ((0, _PACK_ROWS - pack.shape[0]), (0, 0)))


def _rel_table_to_u(rel_bias):
    flat = jnp.concatenate([jnp.broadcast_to(rel_bias[:, 191:192], (12, 447)), rel_bias[:, ::-1]], axis=1)
    return jnp.pad(flat, ((0, 0), (192, 1024 - 192 - 639))).reshape(12, 1, 1024)


def _u_to_rel_table(du):
    flat = du[:, 192:192 + 639]
    g = flat[:, 447:][:, ::-1]
    return g, flat[:, :447]


def kernel(x, mem, norm1_g, mem_norm_g, a_w_in, a_q_g, a_k_g, a_rel_bias, b_w_in, b_b_in, b_conv_w, b_conv_b, b_ln_g, b_ln_b, mq_g, mk_g, w_mem_kv, w_out, norm2_g, w_gate, w_up, w_down, loss_target, m_norm1_g, m_mem_norm_g, m_a_w_in, m_a_q_g, m_a_k_g, m_a_rel_bias, m_b_w_in, m_b_b_in, m_b_conv_w, m_b_conv_b, m_b_ln_g, m_b_ln_b, m_mq_g, m_mk_g, m_w_mem_kv, m_w_out, m_norm2_g, m_w_gate, m_w_up, m_w_down, v_norm1_g, v_mem_norm_g, v_a_w_in, v_a_q_g, v_a_k_g, v_a_rel_bias, v_b_w_in, v_b_b_in, v_b_conv_w, v_b_conv_b, v_b_ln_g, v_b_ln_b, v_mq_g, v_mk_g, v_w_mem_kv, v_w_out, v_norm2_g, v_w_gate, v_w_up, v_w_down):
    sx = 2 * lax.axis_index("x") + lax.axis_index("y")

    n_in = (NA // NSH, NBW // NSH)
    tr = lambda a: jnp.swapaxes(a, -1, -2)
    small_src = jnp.concatenate([
        jnp.pad(b_b_in, ((0, 0), (0, 512 - 448))),
        jnp.pad(b_conv_w[0], ((0, 0), (0, 512 - 192))),
        jnp.pad(jnp.concatenate([b_conv_b, b_ln_g, b_ln_b], 0), ((0, 0), (0, 512 - 192))),
        jnp.zeros((5, 512), F32)], 0)

    def gather_group(l, names, small=None):
        src_of = {
            "in": lambda: [_cast_bf16([tr(b_w_in[0])], "cast_in_1").reshape(1, 2, n_in[1] // 2, D) if l else
                           _cast_bf16([a_w_in[0]], "cast_in_0").reshape(1, 2, D // 2, n_in[0])],
            "kv": lambda: [_cast_bf16([w_mem_kv[l]], f"cast_kv_{l}").reshape(1, 2, 128, 2 * MEMW)],
            "wo": lambda: [_cast_bf16([w_out[l]], f"cast_wo_{l}").reshape(1, 2, 128, D)],
            "gu": lambda: [_cast_bf16([tr(w_gate[l])], f"cast_gate_{l}").reshape(1, 2, FS // 2, D),
                           _cast_bf16([tr(w_up[l])], f"cast_up_{l}").reshape(1, 2, FS // 2, D)],
            "wd": lambda: [_cast_bf16([w_down[l]], f"cast_wd_{l}").reshape(1, 2, FS // 2, D)]}
        dst_of = {"in": (1, NSH, 2, n_in[1] // 2, D) if l else (1, NSH, 2, D // 2, n_in[0]),
                  "kv": (1, NSH, 2, 128, 2 * MEMW), "wo": (1, NSH, 2, 128, D),
                  "gu": (1, 2, NSH, 2, FS // 2, D), "wd": (1, NSH, 2, FS // 2, D)}
        srcs, views = [], []
        for k, name in enumerate(names):
            srcs += src_of[name]()
            if name == "gu":
                views += [lambda d, k=k: d[k].at[:, 0], lambda d, k=k: d[k].at[:, 1]]
            else:
                views.append(lambda d, k=k: d[k])

        def done(w, outs):
            for k, name in enumerate(names):
                if name == "in" and l == 0:
                    w["a"] = outs[k].reshape(NSH, D, n_in[0]).transpose(1, 0, 2).reshape(D, NA)
                elif name == "in":
                    w["b"] = outs[k].reshape(NBW, D)
                else:
                    shape = {"kv": (D, 2 * MEMW), "wo": (D, D), "gu": (2, FF, D), "wd": (FF, D)}
                    w[name][l] = outs[k].reshape(shape[name])

        return _gather_exchange(srcs, [dst_of[name] for name in names], views, small), done

    w = dict(a=None, b=None, kv=[None, None], wo=[None, None], gu=[None, None], wd=[None, None])
    first, first_done = gather_group(0, ["in", "kv"], small_src)
    outs0 = _run_exchange(first, "gather_weights_first")
    first_done(w, outs0)
    small_all = outs0[2]
    carries = {"attn_fwd": gather_group(0, ["wo", "gu", "wd"]),
               "ffn_up_0": gather_group(1, ["in", "kv", "wo", "wd"]),
               "ffn_down_0": gather_group(1, ["gu"])}

    conv_w_full = small_all[:, 1:1 + CONVW, :192].transpose(1, 0, 2).reshape(CONVW, TOK)
    vec3 = small_all[:, 32:35, :192].transpose(1, 0, 2).reshape(3, TOK)
    p = dict(
        norm1_g=norm1_g, mem_norm_g=mem_norm_g, norm2_g=norm2_g,
        a_q_g2=jnp.tile(a_q_g, (1, AH)), a_k_g2=jnp.tile(a_k_g, (1, AH)),
        mq_g4=jnp.tile(mq_g, (1, 4)), mk_g4=jnp.tile(mk_g, (1, 4)),
        rel_u=_rel_table_to_u(a_rel_bias[0]),
        b_b_in=small_all[:, 0, :448].reshape(1, NBW),
        conv_w=jnp.pad(conv_w_full, ((0, 1), (0, 0))),
        conv_b=vec3[0:1], ln_g=vec3[1:2], ln_b=vec3[2:3])

    def pair_sums(items, big, name):
        units = []
        for l, tensor in items:
            if tensor == "in" and l == 0:
                g = big["a"].reshape(D, NSH, n_in[0]).transpose(1, 0, 2).reshape(1, NSH, 2, D // 2, n_in[0])
            elif tensor == "in":
                g = big["b"].reshape(1, NSH, 2, n_in[1] // 2, D)
            else:
                shape = {"kv": (1, NSH, 2, 128, 2 * MEMW), "wo": (1, NSH, 2, 128, D),
                         "gu": (2, NSH, 2, FS // 2, D), "wd": (1, NSH, 2, FS // 2, D)}
                g = big[tensor][l].reshape(shape[tensor])
            units.append(g)
        return _pair_reduce(units, name)

    early_groups = [[(1, "gu")], [(0, "gu")],
                    [(1, "in"), (1, "kv"), (1, "wo"), (1, "wd"), (0, "wo"), (0, "wd")]]
    early = [item for grp in early_groups for item in grp]
    late = [(0, "in"), (0, "kv")]
    own_early = []

    def bwd_carry_fn(big):
        sums_b = []
        for k, grp in enumerate(early_groups):
            own, sb = pair_sums(grp, big, f"pair_reduce_early_{k}")
            own_early.extend(own)
            sums_b.extend(sb)
        return _chip_exchange(sums_b)

    loss, grad_x, big, small, parts_early = _local_step(
        x.reshape(T, D), mem.reshape(NB * MEMT, D), loss_target.reshape(T, D), w, p,
        carries=carries, bwd_carry_fn=bwd_carry_fn)
    loss = lax.psum(loss[0, 0], ("x", "y", "c"))

    own_late, sums_b_late = pair_sums(late, big, "pair_reduce_late")
    parts_late = _run_exchange(_chip_exchange(sums_b_late), "chip_exchange_late")
    items = early + late
    halves = [_quad_sum(o, pt, f"quad_sum_{name}_{l}")
              for (l, name), o, pt in zip(items, own_early + own_late, list(parts_early) + list(parts_late))]
    out_shapes = [(1, D, NA // NSH), (1, NBW // NSH, D), (2, 2 * 128, 2 * MEMW), (2, 2 * 128, D),
                  (2, FS, D), (2, FS, D), (2, FS, D)]
    target_of = {"in": lambda l: [(0, l, 0)], "kv": lambda l: [(0, 2, l)], "wo": lambda l: [(0, 3, l)],
                 "gu": lambda l: [(0, 4, l), (1, 5, l)], "wd": lambda l: [(0, 6, l)]}
    targets = [target_of[name](l) for l, name in items]
    g_a, g_b, g_kv, g_wo, g_gate, g_up, g_wd = _final_exchange(halves, out_shapes, targets)

    tot = _small_allreduce(_pack_small(small))
    g_rel, clip_part = _u_to_rel_table(tot[49:61])
    g_rel = jnp.concatenate([g_rel[:, :191], g_rel[:, 191:] + _rowsum(clip_part)], axis=1)
    b_in_full = jnp.concatenate([tot[15:16], tot[16:17, :512], tot[17:18, :MEMW]], axis=1)
    g_small = dict(
        norm1_g=tot[0:2], mem_norm_g=tot[2:4], norm2_g=tot[4:6],
        a_q_g=tot[6:7, :HD], a_k_g=tot[7:8, :HD], a_rel_bias=g_rel[None],
        b_b_in=lax.dynamic_slice(b_in_full, (0, sx * 448), (1, 448)),
        b_conv_w=lax.dynamic_slice(tot[18:49, :TOK], (0, sx * 192), (CONVW, 192))[None],
        b_conv_b=lax.dynamic_slice(tot[12:13, :TOK], (0, sx * 192), (1, 192)),
        b_ln_g=lax.dynamic_slice(tot[13:14, :TOK], (0, sx * 192), (1, 192)),
        b_ln_b=lax.dynamic_slice(tot[14:15, :TOK], (0, sx * 192), (1, 192)),
        mq_g=tot[8:10, :HD], mk_g=tot[10:12, :HD])

    names = ["norm1_g", "mem_norm_g", "a_w_in", "a_q_g", "a_k_g", "a_rel_bias", "b_w_in", "b_b_in",
             "b_conv_w", "b_conv_b", "b_ln_g", "b_ln_b", "mq_g", "mk_g", "w_mem_kv", "w_out",
             "norm2_g", "w_gate", "w_up", "w_down"]
    weights = dict(zip(names, [norm1_g, mem_norm_g, a_w_in, a_q_g, a_k_g, a_rel_bias, b_w_in, b_b_in,
                               b_conv_w, b_conv_b, b_ln_g, b_ln_b, mq_g, mk_g, w_mem_kv, w_out,
                               norm2_g, w_gate, w_up, w_down]))
    ms = dict(zip(names, [m_norm1_g, m_mem_norm_g, m_a_w_in, m_a_q_g, m_a_k_g, m_a_rel_bias, m_b_w_in,
                          m_b_b_in, m_b_conv_w, m_b_conv_b, m_b_ln_g, m_b_ln_b, m_mq_g, m_mk_g,
                          m_w_mem_kv, m_w_out, m_norm2_g, m_w_gate, m_w_up, m_w_down]))
    vs = dict(zip(names, [v_norm1_g, v_mem_norm_g, v_a_w_in, v_a_q_g, v_a_k_g, v_a_rel_bias, v_b_w_in,
                          v_b_b_in, v_b_conv_w, v_b_conv_b, v_b_ln_g, v_b_ln_b, v_mq_g, v_mk_g,
                          v_w_mem_kv, v_w_out, v_norm2_g, v_w_gate, v_w_up, v_w_down]))
    grads = dict(g_small)
    grads.update(a_w_in=g_a, b_w_in=g_b, w_mem_kv=g_kv, w_out=g_wo, w_gate=g_gate, w_up=g_up, w_down=g_wd)
    big_names = ["a_w_in", "b_w_in", "w_mem_kv", "w_out", "w_gate", "w_up", "w_down"]
    small_names = [n for n in names if n not in big_names]
    delta, new_m, new_v = {}, {}, {}
    for n in big_names:
        if n in ("b_w_in", "w_gate", "w_up"):
            outs = _adamw_big(tr(weights[n]), grads[n], tr(ms[n]), tr(vs[n]), f"adamw_{n}")
            delta[n], new_m[n], new_v[n] = [tr(o) for o in outs]
            grads[n] = tr(grads[n])
        else:
            delta[n], new_m[n], new_v[n] = _adamw_big(weights[n], grads[n], ms[n], vs[n], f"adamw_{n}")
    as2d = lambda a: a.reshape(-1, a.shape[-1])
    d_s, m_s, v_s = _adamw_small([as2d(weights[n]) for n in small_names], [as2d(grads[n]) for n in small_names],
                                 [as2d(ms[n]) for n in small_names], [as2d(vs[n]) for n in small_names])
    for i, n in enumerate(small_names):
        delta[n] = d_s[i].reshape(weights[n].shape)
        new_m[n] = m_s[i].reshape(weights[n].shape)
        new_v[n] = v_s[i].reshape(weights[n].shape)

    return (loss, grad_x.reshape(NB, SEQ, D), *[grads[n] for n in names], *[delta[n] for n in names],
            *[new_m[n] for n in names], *[new_v[n] for n in names])


def _rowsum(a):
    def body(a_ref, o_ref):
        o_ref[...] = jnp.sum(a_ref[...], axis=1, keepdims=True)

    vm = BS(memory_space=pltpu.VMEM)
    return pl.pallas_call(body, in_specs=[vm], out_specs=vm, out_shape=_sds((a.shape[0], 1), F32),
                          compiler_params=_cp(), name="rowsum")(a)
```

```python
import functools

import jax
import jax.numpy as jnp
from jax import lax
from jax.experimental import pallas as pl
from jax.experimental.pallas import tpu as pltpu

F32 = jnp.float32
BF16 = jnp.bfloat16
BS = pl.BlockSpec
ANY = pl.BlockSpec(memory_space=pl.ANY)
MESH = pl.DeviceIdType.MESH

D = 1024
SEQ = 2048
NB = 2
T = NB * SEQ
MEMT = 256
HD = 64
TOK = 768
MEMW = 256
NA = 3 * TOK + MEMW
NBW = 2 * TOK + MEMW
FF = 2816
NSH = 4
FS = FF // NSH
FT = FF // 2
CONVW = 31
EPS = 1e-6
NEG = -1e30
SCALE = HD ** -0.5
QB = 256
KWIN = 768
KPAD = 512
TR = 512

ADAM_LR = 0.001
ADAM_B1 = 0.9
ADAM_B2 = 0.999
ADAM_EPS = 1e-08
ADAM_WD = 0.01
ADAM_STEP = 10

_DIMS = {
    "nn": (((1,), (0,)), ((), ())),
    "nt": (((1,), (1,)), ((), ())),
    "tn": (((0,), (0,)), ((), ())),
}


def _cp(sem=None, vmem_mb=48):
    return pltpu.CompilerParams(dimension_semantics=sem, vmem_limit_bytes=vmem_mb << 20)


def _sds(shape, dtype):
    return jax.ShapeDtypeStruct(tuple(shape), dtype)


def _mm(mode, a, b, *, grid, a_spec, b_spec, out_shape, out_spec, acc_shape, name,
        extras=(), extra_specs=(), epilogue=None, carry=None, vmem_mb=48, sequential=False):
    n_ex = len(extras)
    nk = grid[2]
    dims = _DIMS[mode]
    ni = len(carry.ins) if carry else 0
    no = len(carry.out_shapes) if carry else 0
    multi = isinstance(out_shape, (list, tuple))
    out_shapes = list(out_shape) if multi else [out_shape]
    out_specs = list(out_spec) if multi else [out_spec]
    n_o = len(out_shapes)

    def body(a_ref, b_ref, *rest):
        ex = rest[:n_ex]
        cin = rest[n_ex:n_ex + ni]
        o_refs = rest[n_ex + ni:n_ex + ni + n_o]
        o_ref = o_refs if multi else o_refs[0]
        cout = rest[n_ex + ni + n_o:n_ex + ni + n_o + no]
        acc = rest[n_ex + ni + n_o + no]
        cscr = rest[n_ex + ni + n_o + no + 1:]
        ids = [pl.program_id(d) for d in range(3)]
        k = ids[2]
        if carry:
            @pl.when((ids[0] == 0) & (ids[1] == 0) & (ids[2] == 0))
            def _():
                carry.start(cin, cout, cscr)

        prod = lax.dot_general(a_ref[...].astype(BF16), b_ref[...].astype(BF16), dims,
                               preferred_element_type=F32)

        def finish(val):
            if epilogue is None:
                o_ref[...] = val.astype(o_ref.dtype)
            else:
                epilogue(val, ex, o_ref, ids[0])

        if nk == 1:
            finish(prod)
        else:
            @pl.when(k == 0)
            def _():
                acc[...] = prod

            @pl.when((k > 0) & (k < nk - 1))
            def _():
                acc[...] += prod

            @pl.when(k == nk - 1)
            def _():
                finish(acc[...] + prod)

        if carry:
            @pl.when((ids[0] == grid[0] - 1) & (ids[1] == grid[1] - 1) & (ids[2] == grid[2] - 1))
            def _():
                carry.finish(cin, cout, cscr)

    acc_scratch = pltpu.VMEM(acc_shape if nk > 1 else (8, 128), F32)
    ordered = sequential or bool(carry)
    outs = pl.pallas_call(
        body, grid=grid,
        in_specs=[a_spec, b_spec, *extra_specs] + (carry.in_specs if carry else []),
        out_specs=out_specs + [ANY] * no, out_shape=out_shapes + (carry.out_shapes if carry else []),
        scratch_shapes=[acc_scratch] + (carry.scratch if carry else []),
        compiler_params=pltpu.CompilerParams(
            dimension_semantics=("arbitrary",) * 3 if ordered else ("parallel", "parallel", "arbitrary"),
            vmem_limit_bytes=vmem_mb << 20, has_side_effects=bool(carry)), name=name,
    )(a, b, *extras, *(carry.ins if carry else []))
    mine = list(outs[:n_o]) if multi else outs[0]
    return (mine, outs[n_o:]) if carry else mine


def _rms_fwd(x, g, name):
    rows = x.shape[0]

    def body(x_ref, g_ref, o_ref):
        xv = x_ref[...]
        r = lax.rsqrt(jnp.mean(xv * xv, axis=-1, keepdims=True) + EPS)
        o_ref[...] = (xv * r * g_ref[...]).astype(BF16)

    return pl.pallas_call(
        body, grid=(rows // TR,),
        in_specs=[BS((TR, D), lambda i: (i, 0)), BS((1, D), lambda i: (0, 0))],
        out_specs=BS((TR, D), lambda i: (i, 0)), out_shape=_sds((rows, D), BF16),
        compiler_params=_cp(("arbitrary",)), name=name,
    )(x, g)


def _rms_bwd(dh, x, g, dres, name):
    rows = x.shape[0]
    has_res = dres is not None

    def body(*refs):
        if has_res:
            dh_ref, x_ref, g_ref, r_ref, dx_ref, dg_ref = refs
        else:
            dh_ref, x_ref, g_ref, dx_ref, dg_ref = refs
        xv = x_ref[...]
        dhv = dh_ref[...]
        r = lax.rsqrt(jnp.mean(xv * xv, axis=-1, keepdims=True) + EPS)
        xh = xv * r
        gy = dhv * g_ref[...]
        dx = r * (gy - xh * jnp.mean(gy * xh, axis=-1, keepdims=True))
        if has_res:
            dx = dx + r_ref[...]
        dx_ref[...] = dx

        @pl.when(pl.program_id(0) == 0)
        def _():
            dg_ref[...] = jnp.zeros_like(dg_ref)

        dg_ref[...] += jnp.sum(dhv * xh, axis=0, keepdims=True)

    row = BS((TR, D), lambda i: (i, 0))
    vec = BS((1, D), lambda i: (0, 0))
    ins = [dh, x, g] + ([dres] if has_res else [])
    return pl.pallas_call(
        body, grid=(rows // TR,),
        in_specs=[row, row, vec] + ([row] if has_res else []),
        out_specs=[row, vec], out_shape=[_sds((rows, D), F32), _sds((1, D), F32)],
        compiler_params=_cp(("arbitrary",)), name=name,
    )(*ins)


def _group_masks(width):
    lane = lax.broadcasted_iota(jnp.int32, (1, width), 1)
    return [(lane >= HD * h) & (lane < HD * (h + 1)) for h in range(width // HD)]


def _group_mean(v, masks):
    out = jnp.zeros_like(v)
    for m in masks:
        s = jnp.sum(jnp.where(m, v, 0.0), axis=-1, keepdims=True) * (1.0 / HD)
        out = jnp.where(m, s, out)
    return out


def _head_norm(zv, g, masks):
    r = lax.rsqrt(_group_mean(zv * zv, masks) + EPS)
    return zv * r * g


def _head_norm_bwd(dy, zv, g, masks):
    r = lax.rsqrt(_group_mean(zv * zv, masks) + EPS)
    zh = zv * r
    gy = dy * g
    dz = r * (gy - zh * _group_mean(gy * zh, masks))
    return dz, jnp.sum(dy * zh, axis=0, keepdims=True)


def _fold_heads(v, width):
    vb = jnp.broadcast_to(v, (8, width))
    out = vb
    for h in range(1, width // HD):
        out = out + pltpu.roll(vb, width - HD * h, axis=1)
    return out[0:1]


def _bias_expand(u):
    def body(u_ref, o_ref):
        x = jnp.broadcast_to(u_ref[...], (QB, 1024))
        rolled = pltpu.roll(x, 1024 - (QB - 1), axis=1, stride=1, stride_axis=0)[:, :KWIN]
        row = lax.broadcasted_iota(jnp.int32, (QB, 1), 0)
        col = lax.broadcasted_iota(jnp.int32, (1, KWIN), 1)
        lo = (row // 64) * 64
        ok = (col >= lo) & (col < lo + 576)
        o_ref[...] = jnp.where(ok, rolled, NEG)

    return pl.pallas_call(
        body, grid=(12,), in_specs=[BS((None, 1, 1024), lambda h: (h, 0, 0))],
        out_specs=BS((None, QB, KWIN), lambda h: (h, 0, 0)), out_shape=_sds((12, QB, KWIN), F32),
        compiler_params=_cp(("arbitrary",)), name="bias_expand",
    )(u)


def _bias_reduce(ds):
    def body(d_ref, o_ref):
        ri = lax.broadcasted_iota(jnp.int32, (QB, QB), 0)
        ci = lax.broadcasted_iota(jnp.int32, (QB, QB), 1)
        flip = (ri + ci == QB - 1).astype(F32)
        drev = jnp.dot(flip, d_ref[...], precision=lax.Precision.HIGHEST, preferred_element_type=F32)
        x = jnp.concatenate([drev, jnp.zeros((QB, 1024 - KWIN), F32)], axis=1)
        rolled = pltpu.roll(x, 0, axis=1, stride=1, stride_axis=0)
        o_ref[...] = jnp.sum(rolled, axis=0, keepdims=True)

    return pl.pallas_call(
        body, grid=(12,), in_specs=[BS((None, QB, KWIN), lambda h: (h, 0, 0))],
        out_specs=BS((None, 1, 1024), lambda h: (h, 0, 0)), out_shape=_sds((12, 1, 1024), F32),
        compiler_params=_cp(("arbitrary",)), name="bias_reduce",
    )(ds)


AW = 256
AH = AW // HD
AG = TOK // AW
def _attn_softmax(qh, kw, bias, startadd):
    s = lax.dot_general(qh, kw, _DIMS["nt"], preferred_element_type=F32) + bias + startadd
    m = jnp.max(s, axis=-1, keepdims=True)
    p = jnp.exp(s - m)
    return p * (1.0 / jnp.sum(p, axis=-1, keepdims=True))


def _attn_prologue(q_ref, k_ref, v_ref, gq_ref, gk_ref, qn_s, kn_s, v_s, masks):
    kn_s[0:KPAD, :] = jnp.zeros((KPAD, AW), BF16)
    v_s[0:KPAD, :] = jnp.zeros((KPAD, AW), BF16)
    for r in range(0, SEQ, TR):
        qn_s[r:r + TR, :] = (_head_norm(q_ref[r:r + TR, :], gq_ref[...], masks) * SCALE).astype(BF16)
        kn_s[KPAD + r:KPAD + r + TR, :] = _head_norm(k_ref[r:r + TR, :], gk_ref[...], masks).astype(BF16)
        v_s[KPAD + r:KPAD + r + TR, :] = v_ref[r:r + TR, :].astype(BF16)


def _attn_fwd(z, bias, gq2, gk2, carry=None):
    ni = len(carry.ins) if carry else 0
    no = len(carry.out_shapes) if carry else 0

    def body(q_ref, k_ref, v_ref, b_ref, gq_ref, gk_ref, *rest):
        cin, o_ref, cout = rest[:ni], rest[ni], rest[ni + 1:ni + 1 + no]
        qn_s, kn_s, v_s = rest[ni + 1 + no:ni + 4 + no]
        cscr = rest[ni + 4 + no:]
        if carry:
            @pl.when((pl.program_id(0) == 0) & (pl.program_id(1) == 0))
            def _():
                carry.start(cin, cout, cscr)

        masks = _group_masks(AW)
        _attn_prologue(q_ref, k_ref, v_ref, gq_ref, gk_ref, qn_s, kn_s, v_s, masks)
        col = lax.broadcasted_iota(jnp.int32, (1, KWIN), 1)

        def blk(i, carry):
            r0 = pl.multiple_of(i * QB, QB)
            qb = qn_s[pl.ds(r0, QB), :]
            kw = kn_s[pl.ds(r0, KWIN), :]
            vw = v_s[pl.ds(r0, KWIN), :]
            startadd = jnp.where(col + r0 < KPAD, NEG, 0.0)
            o = jnp.zeros((QB, AW), F32)
            for h in range(AH):
                qh = jnp.where(masks[h], qb, jnp.zeros_like(qb))
                vh = jnp.where(masks[h], vw, jnp.zeros_like(vw))
                p = _attn_softmax(qh, kw, b_ref[h], startadd).astype(BF16)
                o = o + jnp.dot(p, vh, preferred_element_type=F32)
            o_ref[pl.ds(r0, QB), :] = o.astype(BF16)
            return carry

        lax.fori_loop(0, SEQ // QB, blk, 0)

        if carry:
            @pl.when((pl.program_id(0) == NB - 1) & (pl.program_id(1) == AG - 1))
            def _():
                carry.finish(cin, cout, cscr)

    vec = BS((1, AW), lambda b, hp: (0, 0))
    outs = pl.pallas_call(
        body, grid=(NB, AG),
        in_specs=[BS((SEQ, AW), lambda b, hp: (b, hp)),
                  BS((SEQ, AW), lambda b, hp: (b, AG + hp)),
                  BS((SEQ, AW), lambda b, hp: (b, 2 * AG + hp)),
                  BS((AH, QB, KWIN), lambda b, hp: (hp, 0, 0)), vec, vec]
        + (carry.in_specs if carry else []),
        out_specs=[BS((SEQ, AW), lambda b, hp: (b, hp))] + [ANY] * no,
        out_shape=[_sds((T, D), BF16)] + (carry.out_shapes if carry else []),
        scratch_shapes=[pltpu.VMEM((SEQ, AW), BF16), pltpu.VMEM((SEQ + KPAD, AW), BF16),
                        pltpu.VMEM((SEQ + KPAD, AW), BF16)] + (carry.scratch if carry else []),
        compiler_params=pltpu.CompilerParams(
            dimension_semantics=("arbitrary", "arbitrary"), vmem_limit_bytes=48 << 20,
            has_side_effects=bool(carry)), name="attn_fwd",
    )(z, z, z, bias, gq2, gk2, *(carry.ins if carry else []))
    return outs[0], outs[1:]


def _attn_bwd(z, dcat, bias, gq2, gk2, carry=None):
    ni = len(carry.ins) if carry else 0
    no = len(carry.out_shapes) if carry else 0

    def body(q_ref, k_ref, v_ref, do_ref, b_ref, gq_ref, gk_ref, *rest):
        cin = rest[:ni]
        dq_ref, dk_ref, dv_ref, db_ref, dgq_ref, dgk_ref = rest[ni:ni + 6]
        cout = rest[ni + 6:ni + 6 + no]
        qn_s, kn_s, v_s, dqn_s, dkn_s, dv_s = rest[ni + 6 + no:ni + 12 + no]
        cscr = rest[ni + 12 + no:]
        hp = pl.program_id(0)
        b = pl.program_id(1)
        if carry:
            @pl.when((hp == 0) & (b == 0))
            def _():
                carry.start(cin, cout, cscr)

        masks = _group_masks(AW)
        _attn_prologue(q_ref, k_ref, v_ref, gq_ref, gk_ref, qn_s, kn_s, v_s, masks)
        dkn_s[...] = jnp.zeros_like(dkn_s)
        dv_s[...] = jnp.zeros_like(dv_s)

        @pl.when(b == 0)
        def _():
            db_ref[...] = jnp.zeros_like(db_ref)

        @pl.when((b == 0) & (hp == 0))
        def _():
            dgq_ref[...] = jnp.zeros_like(dgq_ref)
            dgk_ref[...] = jnp.zeros_like(dgk_ref)

        col = lax.broadcasted_iota(jnp.int32, (1, KWIN), 1)

        def blk(i, carry):
            r0 = pl.multiple_of(i * QB, QB)
            qb = qn_s[pl.ds(r0, QB), :]
            kw = kn_s[pl.ds(r0, KWIN), :]
            vw = v_s[pl.ds(r0, KWIN), :]
            dob = do_ref[pl.ds(r0, QB), :].astype(BF16)
            startadd = jnp.where(col + r0 < KPAD, NEG, 0.0)
            dqn = jnp.zeros((QB, AW), F32)
            dkw = jnp.zeros((KWIN, AW), F32)
            dvw = jnp.zeros((KWIN, AW), F32)
            for h in range(AH):
                qh = jnp.where(masks[h], qb, jnp.zeros_like(qb))
                kh = jnp.where(masks[h], kw, jnp.zeros_like(kw))
                doh = jnp.where(masks[h], dob, jnp.zeros_like(dob))
                p = _attn_softmax(qh, kw, b_ref[h], startadd)
                dvw = dvw + lax.dot_general(p.astype(BF16), doh, _DIMS["tn"],
                                            preferred_element_type=F32)
                dp = lax.dot_general(doh, vw, _DIMS["nt"], preferred_element_type=F32)
                ds = p * (dp - jnp.sum(dp * p, axis=-1, keepdims=True))
                db_ref[h] += ds
                dsb = ds.astype(BF16)
                dqn = dqn + jnp.dot(dsb, kh, preferred_element_type=F32)
                dkw = dkw + lax.dot_general(dsb, qh, _DIMS["tn"], preferred_element_type=F32)
            dqn_s[pl.ds(r0, QB), :] = dqn * SCALE
            dkn_s[pl.ds(r0, KWIN), :] += dkw
            dv_s[pl.ds(r0, KWIN), :] += dvw
            return carry

        lax.fori_loop(0, SEQ // QB, blk, 0)

        dgq = jnp.zeros((1, AW), F32)
        dgk = jnp.zeros((1, AW), F32)
        for r in range(0, SEQ, TR):
            dq, dg = _head_norm_bwd(dqn_s[r:r + TR, :], q_ref[r:r + TR, :], gq_ref[...], masks)
            dq_ref[r:r + TR, :] = dq.astype(BF16)
            dgq = dgq + dg
            dk, dg = _head_norm_bwd(dkn_s[KPAD + r:KPAD + r + TR, :], k_ref[r:r + TR, :], gk_ref[...], masks)
            dk_ref[r:r + TR, :] = dk.astype(BF16)
            dgk = dgk + dg
            dv_ref[r:r + TR, :] = dv_s[KPAD + r:KPAD + r + TR, :].astype(BF16)
        dgq_ref[...] += _fold_heads(dgq, AW)
        dgk_ref[...] += _fold_heads(dgk, AW)

        if carry:
            @pl.when((hp == AG - 1) & (b == NB - 1))
            def _():
                carry.finish(cin, cout, cscr)

    vec = BS((1, AW), lambda hp, b: (0, 0))
    row = BS((SEQ, AW), lambda hp, b: (b, hp))
    outs = pl.pallas_call(
        body, grid=(AG, NB),
        in_specs=[row,
                  BS((SEQ, AW), lambda hp, b: (b, AG + hp)),
                  BS((SEQ, AW), lambda hp, b: (b, 2 * AG + hp)),
                  row,
                  BS((AH, QB, KWIN), lambda hp, b: (hp, 0, 0)), vec, vec]
        + (carry.in_specs if carry else []),
        out_specs=[row, row, row, BS((AH, QB, KWIN), lambda hp, b: (hp, 0, 0)), vec, vec] + [ANY] * no,
        out_shape=[_sds((T, TOK), BF16), _sds((T, TOK), BF16), _sds((T, TOK), BF16),
                   _sds((12, QB, KWIN), F32), _sds((1, AW), F32), _sds((1, AW), F32)]
        + (carry.out_shapes if carry else []),
        scratch_shapes=[pltpu.VMEM((SEQ, AW), BF16), pltpu.VMEM((SEQ + KPAD, AW), BF16),
                        pltpu.VMEM((SEQ + KPAD, AW), BF16), pltpu.VMEM((SEQ, AW), F32),
                        pltpu.VMEM((SEQ + KPAD, AW), F32), pltpu.VMEM((SEQ + KPAD, AW), F32)]
        + (carry.scratch if carry else []),
        compiler_params=pltpu.CompilerParams(
            dimension_semantics=("arbitrary", "arbitrary"), vmem_limit_bytes=58 << 20,
            has_side_effects=bool(carry)), name="attn_bwd",
    )(z, z, z, dcat, bias, gq2, gk2, *(carry.ins if carry else []))
    return outs[:6], outs[6:]


def _mem_softmax(qh, kn):
    s = lax.dot_general(qh, kn, _DIMS["nt"], preferred_element_type=F32)
    m = jnp.max(s, axis=-1, keepdims=True)
    p = jnp.exp(s - m)
    return p * (1.0 / jnp.sum(p, axis=-1, keepdims=True))


def _memattn_fwd(z, kv, cat, gq4, gk4, qcol, name):
    def body(q_ref, k_ref, v_ref, gq_ref, gk_ref, cat_ref, o_ref):
        del cat_ref
        masks = _group_masks(MEMW)
        qn = (_head_norm(q_ref[...], gq_ref[...], masks) * SCALE).astype(BF16)
        kn = _head_norm(k_ref[...], gk_ref[...], masks).astype(BF16)
        vv = v_ref[...].astype(BF16)
        o = jnp.zeros((TR, MEMW), F32)
        for h in range(4):
            qh = jnp.where(masks[h], qn, jnp.zeros_like(qn))
            vh = jnp.where(masks[h], vv, jnp.zeros_like(vv))
            p = _mem_softmax(qh, kn).astype(BF16)
            o = o + jnp.dot(p, vh, preferred_element_type=F32)
        o_ref[...] = o.astype(BF16)

    nt = SEQ // TR
    vec = BS((1, MEMW), lambda b, t: (0, 0))
    return pl.pallas_call(
        body, grid=(NB, nt),
        in_specs=[BS((TR, MEMW), lambda b, t: (b * nt + t, qcol)),
                  BS((MEMT, MEMW), lambda b, t: (b, 0)),
                  BS((MEMT, MEMW), lambda b, t: (b, 1)), vec, vec, ANY],
        out_specs=BS((TR, MEMW), lambda b, t: (b * nt + t, 3)),
        out_shape=_sds((T, D), BF16), input_output_aliases={5: 0},
        compiler_params=_cp(("arbitrary", "arbitrary")), name=name,
    )(z, kv, kv, gq4, gk4, cat)


def _memattn_bwd(z, kv, dcat, gq4, gk4, qcol, name):
    nt = SEQ // TR

    def body(q_ref, k_ref, v_ref, do_ref, gq_ref, gk_ref,
             dq_ref, dkv_ref, dgq_ref, dgk_ref, dkn_s, dv_s):
        b = pl.program_id(0)
        t = pl.program_id(1)
        masks = _group_masks(MEMW)
        qz = q_ref[...]
        kz = k_ref[...]
        qn = (_head_norm(qz, gq_ref[...], masks) * SCALE).astype(BF16)
        kn = _head_norm(kz, gk_ref[...], masks).astype(BF16)
        vv = v_ref[...].astype(BF16)
        dob = do_ref[...].astype(BF16)

        @pl.when(t == 0)
        def _():
            dkn_s[...] = jnp.zeros_like(dkn_s)
            dv_s[...] = jnp.zeros_like(dv_s)

        @pl.when((t == 0) & (b == 0))
        def _():
            dgq_ref[...] = jnp.zeros_like(dgq_ref)
            dgk_ref[...] = jnp.zeros_like(dgk_ref)

        dqn = jnp.zeros((TR, MEMW), F32)
        dkn = jnp.zeros((MEMT, MEMW), F32)
        dvv = jnp.zeros((MEMT, MEMW), F32)
        for h in range(4):
            qh = jnp.where(masks[h], qn, jnp.zeros_like(qn))
            kh = jnp.where(masks[h], kn, jnp.zeros_like(kn))
            doh = jnp.where(masks[h], dob, jnp.zeros_like(dob))
            p = _mem_softmax(qh, kn)
            dvv = dvv + lax.dot_general(p.astype(BF16), doh, _DIMS["tn"], preferred_element_type=F32)
            dp = lax.dot_general(doh, vv, _DIMS["nt"], preferred_element_type=F32)
            ds = p * (dp - jnp.sum(dp * p, axis=-1, keepdims=True))
            dsb = ds.astype(BF16)
            dqn = dqn + jnp.dot(dsb, kh, preferred_element_type=F32)
            dkn = dkn + lax.dot_general(dsb, qh, _DIMS["tn"], preferred_element_type=F32)
        dkn_s[...] += dkn
        dv_s[...] += dvv
        dq, dgq = _head_norm_bwd(dqn * SCALE, qz, gq_ref[...], masks)
        dq_ref[...] = dq.astype(BF16)
        dgq_ref[...] += _fold_heads(dgq, MEMW)

        @pl.when(t == nt - 1)
        def _():
            dk, dgk = _head_norm_bwd(dkn_s[...], kz, gk_ref[...], masks)
            dkv_ref[:, 0:MEMW] = dk
            dkv_ref[:, MEMW:] = dv_s[...]
            dgk_ref[...] += _fold_heads(dgk, MEMW)

    vec = BS((1, MEMW), lambda b, t: (0, 0))
    return pl.pallas_call(
        body, grid=(NB, nt),
        in_specs=[BS((TR, MEMW), lambda b, t: (b * nt + t, qcol)),
                  BS((MEMT, MEMW), lambda b, t: (b, 0)),
                  BS((MEMT, MEMW), lambda b, t: (b, 1)),
                  BS((TR, MEMW), lambda b, t: (b * nt + t, 3)), vec, vec],
        out_specs=[BS((TR, MEMW), lambda b, t: (b * nt + t, 0)),
                   BS((MEMT, 2 * MEMW), lambda b, t: (b, 0)), vec, vec],
        out_shape=[_sds((T, MEMW), BF16), _sds((NB * MEMT, 2 * MEMW), F32),
                   _sds((1, MEMW), F32), _sds((1, MEMW), F32)],
        scratch_shapes=[pltpu.VMEM((MEMT, MEMW), F32), pltpu.VMEM((MEMT, MEMW), F32)],
        compiler_params=_cp(("arbitrary", "arbitrary")), name=name,
    )(z, kv, kv, dcat, gq4, gk4)


HALO = 32
NEXT = 64
RT = 64


def _glu(zz):
    return zz[:, :TOK] * jax.nn.sigmoid(zz[:, TOK:])


def _layer_norm_parts(y):
    mu = jnp.mean(y, axis=-1, keepdims=True)
    yc = y - mu
    rstd = lax.rsqrt(jnp.mean(yc * yc, axis=-1, keepdims=True) + EPS)
    return yc * rstd, rstd


def _shifted_copies(src, dst, rows):
    for b in range(1, 8):
        dst[b - 1, 0:rows, :] = src[b:b + rows, :]


def _tap(src, shifted, off, r0, rows):
    b = off % 8
    if b == 0:
        return src[r0 + off:r0 + off + rows, :]
    return shifted[b - 1, r0 + off - b:r0 + off - b + rows, :]


def _conv_rows(w_ref, hbuf, hs, r0, rows):
    y = jnp.zeros((rows, TOK), F32)
    for j in range(CONVW):
        y = y + w_ref[j:j + 1, :] * _tap(hbuf, hs, (HALO - CONVW + 1) + j, r0, rows)
    return y


def _conv_fwd(z, cw, cb, lg, lb):
    nt = SEQ // TR

    def body(zc_ref, zp_ref, w_ref, cb_ref, lg_ref, lb_ref, o_ref, hbuf, hs):
        t = pl.program_id(1)
        hbuf[0:HALO, :] = jnp.where(t == 0, 0.0, _glu(zp_ref[...]))
        hbuf[HALO:, :] = _glu(zc_ref[...])
        _shifted_copies(hbuf, hs, HALO + TR - 8)
        for r0 in range(0, TR, RT):
            y = _conv_rows(w_ref, hbuf, hs, r0, RT) + cb_ref[...]
            yh, _ = _layer_norm_parts(y)
            o = yh * lg_ref[...] + lb_ref[...]
            o_ref[r0:r0 + RT, :] = (o * jax.nn.sigmoid(o)).astype(BF16)

    vec = BS((1, TOK), lambda b, t: (0, 0))
    per = TR // HALO
    return pl.pallas_call(
        body, grid=(NB, nt),
        in_specs=[BS((TR, 2 * TOK), lambda b, t: (b * nt + t, 0)),
                  BS((HALO, 2 * TOK), lambda b, t: (jnp.maximum((b * nt + t) * per - 1, 0), 0)),
                  BS((32, TOK), lambda b, t: (0, 0)), vec, vec, vec],
        out_specs=BS((TR, TOK), lambda b, t: (b * nt + t, 0)),
        out_shape=_sds((T, D), BF16),
        scratch_shapes=[pltpu.VMEM((HALO + TR, TOK), F32), pltpu.VMEM((7, HALO + TR, TOK), F32)],
        compiler_params=_cp(("arbitrary", "arbitrary")), name="conv_fwd",
    )(z, z, cw, cb, lg, lb)


def _conv_bwd(z, dcat, cw, cb, lg, lb):
    nt = SEQ // TR
    ext = TR + NEXT

    def body(zc_ref, zp_ref, zn_ref, dc_ref, dn_ref, w_ref, cb_ref, lg_ref, lb_ref,
             du_ref, dw_ref, dcb_ref, dlg_ref, dlb_ref, dbin_ref, hbuf, dybuf, hs, dys):
        b = pl.program_id(0)
        t = pl.program_id(1)

        @pl.when((b == 0) & (t == 0))
        def _():
            dw_ref[...] = jnp.zeros_like(dw_ref)
            dcb_ref[...] = jnp.zeros_like(dcb_ref)
            dlg_ref[...] = jnp.zeros_like(dlg_ref)
            dlb_ref[...] = jnp.zeros_like(dlb_ref)
            dbin_ref[...] = jnp.zeros_like(dbin_ref)

        hbuf[0:HALO, :] = jnp.where(t == 0, 0.0, _glu(zp_ref[...]))
        hbuf[HALO:HALO + TR, :] = _glu(zc_ref[...])
        hbuf[HALO + TR:, :] = _glu(zn_ref[...])
        _shifted_copies(hbuf, hs, HALO + TR + NEXT - 8)
        last = t == nt - 1
        for r0 in range(0, ext, RT):
            y = _conv_rows(w_ref, hbuf, hs, r0, RT) + cb_ref[...]
            yh, rstd = _layer_norm_parts(y)
            o = yh * lg_ref[...] + lb_ref[...]
            sg = jax.nn.sigmoid(o)
            if r0 < TR:
                dtok = dc_ref[r0:r0 + RT, :]
            else:
                dtok = jnp.where(last, 0.0, dn_ref[...])
            do = dtok * (sg * (1.0 + o * (1.0 - sg)))
            dyh = do * lg_ref[...]
            dy = rstd * (dyh - jnp.mean(dyh, axis=-1, keepdims=True)
                         - yh * jnp.mean(dyh * yh, axis=-1, keepdims=True))
            dybuf[r0:r0 + RT, :] = dy
            if r0 < TR:
                dlg_ref[...] += jnp.sum(do * yh, axis=0, keepdims=True)
                dlb_ref[...] += jnp.sum(do, axis=0, keepdims=True)
                dcb_ref[...] += jnp.sum(dy, axis=0, keepdims=True)
        _shifted_copies(dybuf, dys, ext - 8)
        for r0 in range(0, TR, RT):
            dh = jnp.zeros((RT, TOK), F32)
            for j in range(CONVW):
                dh = dh + w_ref[j:j + 1, :] * _tap(dybuf, dys, (CONVW - 1) - j, r0, RT)
            a = zc_ref[r0:r0 + RT, 0:TOK]
            sg = jax.nn.sigmoid(zc_ref[r0:r0 + RT, TOK:])
            da = dh * sg
            dg = dh * a * (sg * (1.0 - sg))
            du_ref[r0:r0 + RT, 0:TOK] = da.astype(BF16)
            du_ref[r0:r0 + RT, TOK:] = dg.astype(BF16)
            dbin_ref[:, 0:TOK] += jnp.sum(da, axis=0, keepdims=True)
            dbin_ref[:, TOK:] += jnp.sum(dg, axis=0, keepdims=True)
        for j in range(CONVW):
            acc = jnp.zeros((8, TOK), F32)
            for r0 in range(0, TR, RT):
                prod = dybuf[r0:r0 + RT, :] * _tap(hbuf, hs, (HALO - CONVW + 1) + j, r0, RT)
                acc = acc + jnp.sum(prod.reshape(RT // 8, 8, TOK), axis=0)
            dw_ref[j:j + 1, :] += jnp.sum(acc, axis=0, keepdims=True)

    vec = BS((1, TOK), lambda b, t: (0, 0))
    perh = TR // HALO
    pern = TR // NEXT
    nlast_n = T // NEXT - 1
    return pl.pallas_call(
        body, grid=(NB, nt),
        in_specs=[BS((TR, 2 * TOK), lambda b, t: (b * nt + t, 0)),
                  BS((HALO, 2 * TOK), lambda b, t: (jnp.maximum((b * nt + t) * perh - 1, 0), 0)),
                  BS((NEXT, 2 * TOK), lambda b, t: (jnp.minimum((b * nt + t + 1) * pern, nlast_n), 0)),
                  BS((TR, TOK), lambda b, t: (b * nt + t, 0)),
                  BS((NEXT, TOK), lambda b, t: (jnp.minimum((b * nt + t + 1) * pern, nlast_n), 0)),
                  BS((32, TOK), lambda b, t: (0, 0)), vec, vec, vec],
        out_specs=[BS((TR, 2 * TOK), lambda b, t: (b * nt + t, 0)),
                   BS((32, TOK), lambda b, t: (0, 0)), vec, vec, vec,
                   BS((1, 2 * TOK), lambda b, t: (0, 0))],
        out_shape=[_sds((T, 2 * TOK), BF16), _sds((32, TOK), F32), _sds((1, TOK), F32),
                   _sds((1, TOK), F32), _sds((1, TOK), F32), _sds((1, 2 * TOK), F32)],
        scratch_shapes=[pltpu.VMEM((HALO + TR + NEXT, TOK), F32), pltpu.VMEM((ext, TOK), F32),
                        pltpu.VMEM((7, HALO + TR + NEXT, TOK), F32), pltpu.VMEM((7, ext, TOK), F32)],
        compiler_params=_cp(("arbitrary", "arbitrary"), vmem_mb=56), name="conv_bwd",
    )(z, z, z, dcat, dcat, cw, cb, lg, lb)


def _ffn_up(h2, wgu, l, carry=None):
    ni = len(carry.ins) if carry else 0
    no = len(carry.out_shapes) if carry else 0

    def body(h_ref, wg_ref, wu_ref, *rest):
        cin = rest[:ni]
        g_ref, u_ref, a_ref = rest[ni:ni + 3]
        cout, cscr = rest[ni + 3:ni + 3 + no], rest[ni + 3 + no:]
        if carry:
            @pl.when((pl.program_id(0) == 0) & (pl.program_id(1) == 0))
            def _():
                carry.start(cin, cout, cscr)

        hv = h_ref[...]
        g = lax.dot_general(hv, wg_ref[...], _DIMS["nt"], preferred_element_type=F32)
        u = lax.dot_general(hv, wu_ref[...], _DIMS["nt"], preferred_element_type=F32)
        sg = jax.nn.sigmoid(g)
        silu = g * sg
        g_ref[...] = (u * (sg * (1.0 + g * (1.0 - sg)))).astype(BF16)
        u_ref[...] = silu.astype(BF16)
        a_ref[...] = (silu * u).astype(BF16)

        if carry:
            @pl.when((pl.program_id(0) == FF // FT - 1) & (pl.program_id(1) == T // TR - 1))
            def _():
                carry.finish(cin, cout, cscr)

    out = BS((TR, FT), lambda q, i: (i, q))
    outs = pl.pallas_call(
        body, grid=(FF // FT, T // TR),
        in_specs=[BS((TR, D), lambda q, i: (i, 0)),
                  BS((None, FT, D), lambda q, i: (0, q, 0)),
                  BS((None, FT, D), lambda q, i: (1, q, 0))] + (carry.in_specs if carry else []),
        out_specs=[out, out, out] + [ANY] * no,
        out_shape=[_sds((T, FF), BF16), _sds((T, FF), BF16), _sds((T, FF), BF16)]
        + (carry.out_shapes if carry else []),
        scratch_shapes=carry.scratch if carry else [],
        compiler_params=pltpu.CompilerParams(
            dimension_semantics=("arbitrary", "arbitrary"), vmem_limit_bytes=48 << 20,
            has_side_effects=bool(carry)), name=f"ffn_up_{l}",
    )(h2, wgu, wgu, *(carry.ins if carry else []))
    return outs[:3], outs[3:]


def _ffn_down_bwd(dx, wd, g, u, l):
    def epilogue(dact, ex, o_ref, i):
        o_ref[0] = (dact * ex[0][...].astype(F32)).astype(BF16)
        o_ref[1] = (dact * ex[1][...].astype(F32)).astype(BF16)

    ex_spec = BS((TR, FT), lambda i, q, k: (i, q))
    return _mm("nt", dx, wd, grid=(T // TR, FF // FT, 1),
               a_spec=BS((TR, D), lambda i, q, k: (i, 0)),
               b_spec=BS((FT, D), lambda i, q, k: (q, 0)),
               out_shape=_sds((2, T, FF), BF16),
               out_spec=BS((2, TR, FT), lambda i, q, k: (0, i, q)),
               acc_shape=(TR, FT), extras=(g, u), extra_specs=(ex_spec, ex_spec),
               epilogue=epilogue, name=f"ffn_down_bwd_{l}")


def _row_tile(rows, cols, itemsize=4, limit=2 << 20):
    tr = rows
    while tr * cols * itemsize > limit and tr % 2 == 0 and (tr // 2) % 16 == 0:
        tr //= 2
    return tr


def _cast_bf16(arrs, name):
    n = len(arrs)
    rows, cols = arrs[0].shape
    tr = _row_tile(rows, cols)

    def body(*refs):
        o_ref = refs[n]
        k = pl.program_id(0)
        val = refs[0][...]
        for j in range(1, n):
            val = jnp.where(k == j, refs[j][...], val)
        o_ref[...] = val.astype(BF16)

    return pl.pallas_call(
        body, grid=(n, rows // tr),
        in_specs=[BS((tr, cols), lambda k, i: (i, 0))] * n,
        out_specs=BS((None, tr, cols), lambda k, i: (k, i, 0)),
        out_shape=_sds((n, rows, cols), BF16),
        compiler_params=_cp(("arbitrary", "arbitrary")), name=name,
    )(*arrs)


def _quad_sum(own, got, name):
    n, rows, cols = own.shape
    tr = _row_tile(rows, cols)

    def body(a_ref, q_ref, o_ref):
        o_ref[...] = ((a_ref[...] + q_ref[0].astype(F32)) + q_ref[1].astype(F32)) + q_ref[2].astype(F32)

    spec = BS((None, tr, cols), lambda k, i: (k, i, 0))
    return pl.pallas_call(
        body, grid=(n, rows // tr),
        in_specs=[spec, BS((3, None, tr, cols), lambda k, i: (0, k, i, 0))], out_specs=spec,
        out_shape=_sds((n, rows, cols), F32),
        compiler_params=_cp(("arbitrary", "arbitrary")), name=name,
    )(own, got)


def _adam_math(w, g, m, v):
    m = ADAM_B1 * m + (1.0 - ADAM_B1) * g
    v = ADAM_B2 * v + (1.0 - ADAM_B2) * (g * g)
    m_hat = m / (1.0 - ADAM_B1 ** ADAM_STEP)
    v_hat = v / (1.0 - ADAM_B2 ** ADAM_STEP)
    delta = -ADAM_LR * (m_hat / (jnp.sqrt(v_hat) + ADAM_EPS) + ADAM_WD * w)
    return delta, m, v


def _adamw_big(w, g, m, v, name):
    shape = w.shape
    cols = shape[-1]
    rows = w.size // cols
    tr = _row_tile(rows, cols, limit=1 << 20)

    def body(w_ref, g_ref, m_ref, v_ref, d_ref, nm_ref, nv_ref):
        d, nm, nv = _adam_math(w_ref[...], g_ref[...], m_ref[...], v_ref[...])
        d_ref[...] = d
        nm_ref[...] = nm
        nv_ref[...] = nv

    spec = BS((tr, cols), lambda i: (i, 0))
    outs = pl.pallas_call(
        body, grid=(rows // tr,), in_specs=[spec] * 4, out_specs=[spec] * 3,
        out_shape=[_sds((rows, cols), F32)] * 3,
        compiler_params=_cp(("arbitrary",)), name=name,
    )(*[a.reshape(rows, cols) for a in (w, g, m, v)])
    return [o.reshape(shape) for o in outs]


def _adamw_small(ws, gs, ms, vs):
    n = len(ws)

    def body(*refs):
        for i in range(n):
            d, nm, nv = _adam_math(refs[i][...], refs[n + i][...], refs[2 * n + i][...],
                                   refs[3 * n + i][...])
            refs[4 * n + i][...] = d
            refs[5 * n + i][...] = nm
            refs[6 * n + i][...] = nv

    specs = [BS(w.shape, lambda i: (0, 0)) for w in ws]
    outs = pl.pallas_call(
        body, grid=(1,), in_specs=specs * 4, out_specs=specs * 3,
        out_shape=[_sds(w.shape, F32) for w in ws] * 3,
        compiler_params=_cp(("arbitrary",)), name="adamw_small",
    )(*ws, *gs, *ms, *vs)
    return outs[:n], outs[n:2 * n], outs[2 * n:]


def _place():
    x, y, c = lax.axis_index("x"), lax.axis_index("y"), lax.axis_index("c")
    chips = [(1 - x, y), (x, 1 - y), (1 - x, 1 - y)]
    return x, y, c, chips


class _Exchange:
    def __init__(self, ins, in_specs, out_shapes, scratch, start, finish):
        self.ins, self.in_specs, self.out_shapes, self.scratch = ins, in_specs, out_shapes, scratch
        self.start, self.finish = start, finish


def _run_exchange(ex, name, vmem_mb=40):
    ni, no = len(ex.ins), len(ex.out_shapes)

    def body(*refs):
        ex.start(refs[:ni], refs[ni:ni + no], refs[ni + no:])
        ex.finish(refs[:ni], refs[ni:ni + no], refs[ni + no:])

    return pl.pallas_call(
        body, in_specs=ex.in_specs, out_specs=[ANY] * no, out_shape=ex.out_shapes,
        scratch_shapes=ex.scratch,
        compiler_params=pltpu.CompilerParams(has_side_effects=True, vmem_limit_bytes=vmem_mb << 20),
        name=name,
    )(*ex.ins)


def _gather_exchange(srcs, dst_shapes, views, small=None):
    nu = len(srcs)
    nd = len(dst_shapes)
    ns = 1 if small is not None else 0

    def unpack(ins, outs, scr):
        x, y, c, chips = _place()
        src = ins[:nu]
        vw = [views[u](outs[:nd]) for u in range(nu)]
        vbuf = scr[:nu]
        send, recv, fsend, frecv, lsem, ssend, srecv, vsem = scr[nu:]

        def ici(u, j, shard, to):
            return pltpu.make_async_remote_copy(
                src_ref=vbuf[u].at[:, c], dst_ref=vw[u].at[:, shard, c],
                send_sem=send.at[3 * u + j], recv_sem=recv.at[3 * u + j],
                device_id=to, device_id_type=MESH)

        def fwd(u, j, shard, half):
            return pltpu.make_async_remote_copy(
                src_ref=vw[u].at[:, shard, half], dst_ref=vw[u].at[:, shard, half],
                send_sem=fsend.at[3 * u + j], recv_sem=frecv.at[3 * u + j],
                device_id=(x, y, 1 - c), device_id_type=MESH)

        def small_copy(j, shard, to):
            return pltpu.make_async_remote_copy(
                src_ref=ins[nu], dst_ref=outs[nd].at[shard],
                send_sem=ssend.at[j], recv_sem=srecv.at[j], device_id=to, device_id_type=MESH)

        stage = [pltpu.make_async_copy(src[u], vbuf[u], vsem.at[u]) for u in range(nu)]
        local = [pltpu.make_async_copy(vbuf[u], vw[u].at[:, 2 * x + y], lsem.at[u]) for u in range(nu)]
        if ns:
            local.append(pltpu.make_async_copy(ins[nu], outs[nd].at[2 * x + y], lsem.at[nu]))
        return x, y, c, chips, ici, fwd, small_copy, stage, local

    def start(ins, outs, scr):
        x, y, c, chips, ici, fwd, small_copy, stage, local = unpack(ins, outs, scr)
        s = 2 * x + y
        for cp in stage:
            cp.start()
        if ns:
            local[nu].start()
            for j, chip in enumerate(chips):
                small_copy(j, s, (*chip, c)).start()
        for u in range(nu):
            stage[u].wait()
            for j, chip in enumerate(chips):
                ici(u, j, s, (*chip, c)).start()
            local[u].start()

    def finish(ins, outs, scr):
        x, y, c, chips, ici, fwd, small_copy, stage, local = unpack(ins, outs, scr)
        s = 2 * x + y
        for u in range(nu):
            for j, chip in enumerate(chips):
                sj = 2 * chip[0] + chip[1]
                ici(u, j, sj, (x, y, c)).wait_recv()
                fwd(u, j, sj, c).start()
        for u in range(nu):
            for j, chip in enumerate(chips):
                sj = 2 * chip[0] + chip[1]
                fwd(u, j, sj, 1 - c).wait_recv()
        for u in range(nu):
            for j, chip in enumerate(chips):
                ici(u, j, s, (*chip, c)).wait_send()
                fwd(u, j, s, c).wait_send()
        if ns:
            for j, chip in enumerate(chips):
                small_copy(j, 2 * chip[0] + chip[1], (x, y, c)).wait_recv()
                small_copy(j, s, (*chip, c)).wait_send()
        for cp in local:
            cp.wait()

    dma = pltpu.SemaphoreType.DMA
    return _Exchange(
        ins=list(srcs) + ([small] if ns else []),
        in_specs=[ANY] * nu + [BS(memory_space=pltpu.VMEM)] * ns,
        out_shapes=[_sds(sh, BF16) for sh in dst_shapes]
        + ([_sds((NSH,) + small.shape, F32)] if ns else []),
        scratch=[pltpu.VMEM(a.shape, BF16) for a in srcs]
        + [dma((3 * nu,)), dma((3 * nu,)), dma((3 * nu,)), dma((3 * nu,)),
           dma((nu + 1,)), dma((3,)), dma((3,)), dma((nu,))],
        start=start, finish=finish)


def _pair_reduce(gs, name):
    nu = len(gs)
    ns = [g.shape[0] * NSH for g in gs]
    base = [sum(ns[:u]) for u in range(nu)]

    def body(*refs):
        g_refs, own_refs, sb_refs = refs[:nu], refs[nu:2 * nu], refs[2 * nu:3 * nu]
        bufs = refs[3 * nu:8 * nu]
        send, recv, lsem, osem = refs[8 * nu:]
        x, y, c, _ = _place()
        s = 2 * x + y

        def unit(u):
            sendb, recvb, stage, outf, outb = bufs[5 * u:5 * u + 5]

            def load(k, half):
                return pltpu.make_async_copy(g_refs[u].at[k // NSH, k % NSH, half], stage.at[k % 2],
                                             lsem.at[2 * u + k % 2])

            def push(k):
                return pltpu.make_async_remote_copy(
                    src_ref=sendb.at[k], dst_ref=recvb.at[k], send_sem=send.at[base[u] + k],
                    recv_sem=recv.at[base[u] + k], device_id=(x, y, 1 - c), device_id_type=MESH)

            def store(k):
                return pltpu.make_async_copy(outb.at[k % 2], sb_refs[u].at[k // NSH, k % NSH],
                                             osem.at[3 * u + k % 2])

            return sendb, recvb, stage, outf, outb, load, push, store

        for u in range(nu):
            sendb, recvb, stage, outf, outb, load, push, store = unit(u)
            load(0, 1 - c).start()
            for k in range(ns[u]):
                if k + 1 < ns[u]:
                    load(k + 1, 1 - c).start()
                load(k, 1 - c).wait()
                sendb[k] = stage[k % 2].astype(BF16)
                push(k).start()
        for u in range(nu):
            sendb, recvb, stage, outf, outb, load, push, store = unit(u)
            n = ns[u]
            load(0, c).start()
            for k in range(n):
                if k + 1 < n:
                    load(k + 1, c).start()
                load(k, c).wait()
                push(k).wait_recv()
                total = stage[k % 2] + recvb[k].astype(F32)
                if k >= 2:
                    store(k - 2).wait()
                outb[k % 2] = total.astype(BF16)
                store(k).start()

                @pl.when(s == k % NSH)
                def _():
                    outf[...] = total
                    keep = pltpu.make_async_copy(outf, own_refs[u].at[k // NSH], osem.at[3 * u + 2])
                    keep.start()
                    keep.wait()

            for k in range(max(n - 2, 0), n):
                store(k).wait()
        for u in range(nu):
            push = unit(u)[6]
            for k in range(ns[u]):
                push(k).wait_send()

    dma = pltpu.SemaphoreType.DMA
    scratch = []
    for g, n in zip(gs, ns):
        rh, cc = g.shape[3], g.shape[4]
        scratch += [pltpu.VMEM((n, rh, cc), BF16), pltpu.VMEM((n, rh, cc), BF16),
                    pltpu.VMEM((2, rh, cc), F32), pltpu.VMEM((rh, cc), F32), pltpu.VMEM((2, rh, cc), BF16)]
    outs = pl.pallas_call(
        body, in_specs=[ANY] * nu, out_specs=[ANY] * (2 * nu),
        out_shape=[_sds((g.shape[0], g.shape[3], g.shape[4]), F32) for g in gs]
        + [_sds((g.shape[0], NSH, g.shape[3], g.shape[4]), BF16) for g in gs],
        scratch_shapes=scratch + [dma((sum(ns),)), dma((sum(ns),)), dma((2 * nu,)), dma((3 * nu,))],
        compiler_params=pltpu.CompilerParams(has_side_effects=True, vmem_limit_bytes=56 << 20),
        name=name,
    )(*gs)
    return list(outs[:nu]), list(outs[nu:])


def _chip_exchange(sums_bf16):
    nu = len(sums_bf16)

    def pushes(ins, outs, scr):
        x, y, c, chips = _place()
        send, recv = scr
        return [pltpu.make_async_remote_copy(
            src_ref=ins[u].at[:, 2 * chip[0] + chip[1]], dst_ref=outs[u].at[j],
            send_sem=send.at[3 * u + j], recv_sem=recv.at[3 * u + j],
            device_id=(*chip, c), device_id_type=MESH)
            for u in range(nu) for j, chip in enumerate(chips)]

    def start(ins, outs, scr):
        for cp in pushes(ins, outs, scr):
            cp.start()

    def finish(ins, outs, scr):
        for cp in pushes(ins, outs, scr):
            cp.wait()

    dma = pltpu.SemaphoreType.DMA
    shapes = [(3, a.shape[0], a.shape[2], a.shape[3]) for a in sums_bf16]
    return _Exchange(ins=list(sums_bf16), in_specs=[ANY] * nu,
                     out_shapes=[_sds(sh, BF16) for sh in shapes],
                     scratch=[dma((3 * nu,)), dma((3 * nu,))], start=start, finish=finish)


def _final_exchange(halves, out_shapes, targets):
    nu = len(halves)
    no = len(out_shapes)
    ncp = sum(len(t) for t in targets)

    def body(*refs):
        hv = refs[:nu]
        out = refs[nu:nu + no]
        sbuf = refs[nu + no:2 * nu + no]
        rbuf = refs[2 * nu + no:3 * nu + no]
        send, recv, lsem, osem, csem = refs[3 * nu + no:]
        x, y, c, _ = _place()
        stage = [pltpu.make_async_copy(hv[u], sbuf[u], lsem.at[u]) for u in range(nu)]
        push = [pltpu.make_async_remote_copy(
            src_ref=sbuf[u], dst_ref=rbuf[u], send_sem=send.at[u], recv_sem=recv.at[u],
            device_id=(x, y, 1 - c), device_id_type=MESH) for u in range(nu)]
        mine, theirs = [], []
        k = 0
        for u in range(nu):
            rh = hv[u].shape[1]
            for (p, oi, li) in targets[u]:
                mine.append((u, pltpu.make_async_copy(
                    sbuf[u].at[p], out[oi].at[li, pl.ds(c * rh, rh), :], csem.at[k])))
                theirs.append((u, pltpu.make_async_copy(
                    rbuf[u].at[p], out[oi].at[li, pl.ds((1 - c) * rh, rh), :], osem.at[k])))
                k += 1
        for cp in stage:
            cp.start()
        for u in range(nu):
            stage[u].wait()
            push[u].start()
            for (v, cp) in mine:
                if v == u:
                    cp.start()
        for u in range(nu):
            push[u].wait_recv()
            for (v, cp) in theirs:
                if v == u:
                    cp.start()
        for (_, cp) in theirs + mine:
            cp.wait()
        for u in range(nu):
            push[u].wait_send()

    dma = pltpu.SemaphoreType.DMA
    bufs = [pltpu.VMEM(h.shape, F32) for h in halves]
    return pl.pallas_call(
        body, in_specs=[ANY] * nu, out_specs=[ANY] * no,
        out_shape=[_sds(sh, F32) for sh in out_shapes],
        scratch_shapes=bufs + bufs + [dma((nu,)), dma((nu,)), dma((nu,)), dma((ncp,)), dma((ncp,))],
        compiler_params=pltpu.CompilerParams(has_side_effects=True, vmem_limit_bytes=56 << 20),
        name="final_exchange",
    )(*halves)


def _small_allreduce(pack):
    rows = pack.shape[0]

    def body(p_ref, o_ref, buf, send, recv):
        x, y, c, _ = _place()
        me = 4 * x + 2 * y + c
        buf[me] = p_ref[...]
        k = 0
        copies = []
        for dx in range(2):
            for dy in range(2):
                for dc in range(2):
                    if dx == 0 and dy == 0 and dc == 0:
                        continue
                    to = (jnp.where(dx, 1 - x, x), jnp.where(dy, 1 - y, y), jnp.where(dc, 1 - c, c))
                    src_slot = 4 * to[0] + 2 * to[1] + to[2]
                    copies.append((pltpu.make_async_remote_copy(
                        src_ref=p_ref, dst_ref=buf.at[me], send_sem=send.at[k], recv_sem=recv.at[k],
                        device_id=to, device_id_type=MESH), src_slot, k))
                    k += 1
        for cp, _, _ in copies:
            cp.start()
        for cp, src_slot, k in copies:
            pltpu.make_async_remote_copy(
                src_ref=p_ref, dst_ref=buf.at[src_slot], send_sem=send.at[k], recv_sem=recv.at[k],
                device_id=(x, y, c), device_id_type=MESH).wait()
        acc = buf[0]
        for d in range(1, 8):
            acc = acc + buf[d]
        o_ref[...] = acc

    dma = pltpu.SemaphoreType.DMA
    vm = BS(memory_space=pltpu.VMEM)
    return pl.pallas_call(
        body, in_specs=[vm], out_specs=vm, out_shape=_sds((rows, D), F32),
        scratch_shapes=[pltpu.VMEM((8, rows, D), F32), dma((7,)), dma((7,))],
        compiler_params=pltpu.CompilerParams(has_side_effects=True, vmem_limit_bytes=32 << 20),
        name="small_allreduce",
    )(pack)


def _in_proj(h, w, bias, name, transposed=False):
    n = w.shape[0] if transposed else w.shape[1]
    tn = 640 if n == NA else 896
    ep = None
    extras, especs = (), ()
    if bias is not None:
        def ep(acc, ex, o_ref, i):
            o_ref[...] = acc + ex[0][...]
        extras = (bias,)
        especs = (BS((1, tn), lambda i, j, k: (0, j)),)
    b_spec = BS((tn, D), lambda i, j, k: (j, 0)) if transposed else BS((D, tn), lambda i, j, k: (0, j))
    return _mm("nt" if transposed else "nn", h, w, grid=(T // TR, n // tn, 1),
               a_spec=BS((TR, D), lambda i, j, k: (i, 0)), b_spec=b_spec,
               out_shape=_sds((T, n), F32), out_spec=BS((TR, tn), lambda i, j, k: (i, j)),
               acc_shape=(TR, tn), extras=extras, extra_specs=especs, epilogue=ep, name=name)


def _res_rms_epilogue(acc, ex, outs, i):
    y = acc + ex[0][...]
    outs[0][...] = y
    r = lax.rsqrt(jnp.mean(y * y, axis=-1, keepdims=True) + EPS)
    outs[1][...] = (y * r * ex[1][...]).astype(BF16)


def _res_loss_epilogue(acc, ex, outs, i):
    e = acc + ex[0][...] - ex[1][...]
    outs[1][...] = e * (1.0 / D)

    @pl.when(i == 0)
    def _():
        outs[0][...] = jnp.zeros_like(outs[0])

    outs[0][...] += 0.5 * jnp.sum(jnp.mean(e * e, axis=-1, keepdims=True), axis=0, keepdims=True)


def _rms_bwd_epilogue(dh, ex, outs, i):
    xv = ex[0][...]
    r = lax.rsqrt(jnp.mean(xv * xv, axis=-1, keepdims=True) + EPS)
    xh = xv * r
    gy = dh * ex[1][...]
    outs[0][...] = r * (gy - xh * jnp.mean(gy * xh, axis=-1, keepdims=True)) + ex[2][...]

    @pl.when(i == 0)
    def _():
        outs[1][...] = jnp.zeros_like(outs[1])

    outs[1][...] += jnp.sum(dh * xh, axis=0, keepdims=True)


def _local_step(x, mem, target, w, p, carries=None, bwd_carry_fn=None, late_carry_fn=None):
    row = lambda i, j, k: (i, 0)
    whole = lambda i, j, k: (0, 0)
    w = {k: (list(v) if isinstance(v, list) else v) for k, v in w.items()}
    carries = carries or {}

    def carry_of(name):
        return carries[name][0] if name in carries else None

    def delivered(name, outs):
        if name in carries:
            carries[name][1](w, outs)

    saved = []
    bias = _bias_expand(p["rel_u"])
    vec = BS((1, D), whole)
    h = _rms_fwd(x, p["norm1_g"][0:1], "rms1_0")
    for l in range(2):
        type_a = l == 0
        memn = _rms_fwd(mem, p["mem_norm_g"][l:l + 1], f"rmsmem_{l}")
        if type_a:
            z = _in_proj(h, w["a"], None, "inproj_a")
            cat, carried = _attn_fwd(z, bias, p["a_q_g2"], p["a_k_g2"], carry_of("attn_fwd"))
            delivered("attn_fwd", carried)
            qcol = NA // MEMW - 1
        else:
            z = _in_proj(h, w["b"], p["b_b_in"], "inproj_b", transposed=True)
            cat = _conv_fwd(z, p["conv_w"], p["conv_b"], p["ln_g"], p["ln_b"])
            qcol = NBW // MEMW - 1
        kv = _mm("nn", memn, w["kv"][l], grid=(1, 1, 1),
                 a_spec=BS((NB * MEMT, D), whole), b_spec=BS((D, 2 * MEMW), whole),
                 out_shape=_sds((NB * MEMT, 2 * MEMW), F32), out_spec=BS((NB * MEMT, 2 * MEMW), whole),
                 acc_shape=(8, 128), name=f"memkv_{l}")
        cat = _memattn_fwd(z, kv, cat, p["mq_g4"][l:l + 1], p["mk_g4"][l:l + 1], qcol, f"memattn_fwd_{l}")
        x1, h2 = _mm("nn", cat, w["wo"][l], grid=(T // TR, 1, 1), a_spec=BS((TR, D), row),
                     b_spec=BS((D, D), whole),
                     out_shape=[_sds((T, D), F32), _sds((T, D), BF16)],
                     out_spec=[BS((TR, D), row), BS((TR, D), row)], acc_shape=(8, 128),
                     extras=(x, p["norm2_g"][l:l + 1]), extra_specs=(BS((TR, D), row), vec),
                     epilogue=_res_rms_epilogue, name=f"outproj_{l}")
        (g, u, act), carried = _ffn_up(h2, w["gu"][l], l, carry_of(f"ffn_up_{l}"))
        delivered(f"ffn_up_{l}", carried)
        last = l == 1
        res = _mm("nn", act, w["wd"][l], grid=(T // TR, 1, 1),
                  a_spec=BS((TR, FF), row), b_spec=BS((FF, D), whole),
                  out_shape=[_sds((1, 1), F32), _sds((T, D), F32)] if last else
                  [_sds((T, D), F32), _sds((T, D), BF16)],
                  out_spec=[BS((1, 1), whole), BS((TR, D), row)] if last else
                  [BS((TR, D), row), BS((TR, D), row)],
                  acc_shape=(8, 128), extras=(x1, target if last else p["norm1_g"][1:2]),
                  extra_specs=(BS((TR, D), row), BS((TR, D), row) if last else vec),
                  epilogue=_res_loss_epilogue if last else _res_rms_epilogue, sequential=last,
                  name=f"ffn_down_{l}", carry=carry_of(f"ffn_down_{l}"))
        if carry_of(f"ffn_down_{l}") is not None:
            res, carried = res
            delivered(f"ffn_down_{l}", carried)
        saved.append(dict(x=x, h=h, memn=memn, kv=kv, z=z, cat=cat, x1=x1, h2=h2, g=g, u=u, act=act,
                          qcol=qcol))
        if last:
            loss, dx = res
        else:
            x, h = res

    big = dict(a=None, b=None, kv=[None, None], wo=[None, None], gu=[None, None], wd=[None, None])
    small = {}
    bwd_carried, late_carried = (), ()
    tk = T // 2
    nkt = T // tk
    for l in (1, 0):
        sv = saved[l]
        dgu = _ffn_down_bwd(dx, w["wd"][l], sv["g"], sv["u"], l)
        big["wd"][l] = _mm("tn", sv["act"], dx, grid=(FF // FT, 1, 2 * nkt),
                           a_spec=BS((tk // 2, FT), lambda i, j, k: (k, i)),
                           b_spec=BS((tk // 2, D), lambda i, j, k: (k, 0)),
                           out_shape=_sds((FF, D), F32), out_spec=BS((FT, D), lambda i, j, k: (i, 0)),
                           acc_shape=(FT, D), name=f"dw_down_{l}")
        dx1, small[f"norm2_g{l}"] = _mm(
            "nn", dgu, w["gu"][l], grid=(T // TR, 1, 2),
            a_spec=BS((None, TR, FF), lambda i, j, k: (k, i, 0)),
            b_spec=BS((None, FF, D), lambda i, j, k: (k, 0, 0)),
            out_shape=[_sds((T, D), F32), _sds((1, D), F32)], out_spec=[BS((TR, D), row), vec],
            acc_shape=(TR, D), extras=(sv["x1"], p["norm2_g"][l:l + 1], dx),
            extra_specs=(BS((TR, D), row), vec, BS((TR, D), row)),
            epilogue=_rms_bwd_epilogue, sequential=True, name=f"dh2_{l}")
        big["gu"][l] = _mm("tn", dgu, sv["h2"], grid=(2 * FF // FT, 1, nkt),
                           a_spec=BS((None, tk, FT), lambda i, j, k: (i // 2, k, i % 2)),
                           b_spec=BS((tk, D), lambda i, j, k: (k, 0)),
                           out_shape=_sds((2, FF, D), F32),
                           out_spec=BS((None, FT, D), lambda i, j, k: (i // 2, i % 2, 0)),
                           acc_shape=(FT, D), vmem_mb=58, name=f"dw_gu_{l}")
        dcat = _mm("nt", dx1, w["wo"][l], grid=(T // TR, 1, 1), a_spec=BS((TR, D), row),
                   b_spec=BS((D, D), whole),
                   out_shape=_sds((T, D), F32), out_spec=BS((TR, D), row), acc_shape=(8, 128),
                   name=f"dcat_{l}")
        big["wo"][l] = _mm("tn", sv["cat"], dx1, grid=(1, 1, nkt),
                           a_spec=BS((tk, D), lambda i, j, k: (k, 0)), b_spec=BS((tk, D), lambda i, j, k: (k, 0)),
                           out_shape=_sds((D, D), F32), out_spec=BS((D, D), whole),
                           acc_shape=(D, D), name=f"dw_out_{l}")
        dqm, dkv, small[f"mq_g{l}"], small[f"mk_g{l}"] = _memattn_bwd(
            sv["z"], sv["kv"], dcat, p["mq_g4"][l:l + 1], p["mk_g4"][l:l + 1], sv["qcol"], f"memattn_bwd_{l}")
        big["kv"][l] = _mm("tn", sv["memn"], dkv, grid=(1, 1, 1),
                           a_spec=BS((NB * MEMT, D), whole), b_spec=BS((NB * MEMT, 2 * MEMW), whole),
                           out_shape=_sds((D, 2 * MEMW), F32), out_spec=BS((D, 2 * MEMW), whole),
                           acc_shape=(8, 128), name=f"dw_kv_{l}")
        dmemn = _mm("nt", dkv, w["kv"][l], grid=(1, 1, 1),
                    a_spec=BS((NB * MEMT, 2 * MEMW), whole), b_spec=BS((D, 2 * MEMW), whole),
                    out_shape=_sds((NB * MEMT, D), F32), out_spec=BS((NB * MEMT, D), whole),
                    acc_shape=(8, 128), name=f"dmemn_{l}")
        _, small[f"mem_norm_g{l}"] = _rms_bwd(dmemn, mem, p["mem_norm_g"][l:l + 1], None, f"rmsmem_bwd_{l}")
        if l == 0:
            carry = bwd_carry_fn(big) if bwd_carry_fn is not None else None
            (dq, dk, dv, dbias, small["a_q_g"], small["a_k_g"]), bwd_carried = _attn_bwd(
                sv["z"], dcat, bias, p["a_q_g2"], p["a_k_g2"], carry)
            small["rel_u"] = _bias_reduce(dbias)
            dz = jnp.concatenate([dq, dk, dv, dqm], axis=1)
            w_in, key, n, tn = w["a"], "a", NA, 640
        else:
            du, small["conv_w"], small["conv_b"], small["ln_g"], small["ln_b"], dbin_u = _conv_bwd(
                sv["z"], dcat, p["conv_w"], p["conv_b"], p["ln_g"], p["ln_b"])
            dz = jnp.concatenate([du, dqm], axis=1)
            small["b_in_u"] = dbin_u
            w_in, key, n, tn = w["b"], "b", NBW, 896
        norm_bwd = dict(out_shape=[_sds((T, D), F32), _sds((1, D), F32)], out_spec=[BS((TR, D), row), vec],
                        acc_shape=(8, 128), extras=(sv["x"], p["norm1_g"][l:l + 1], dx1),
                        extra_specs=(BS((TR, D), row), vec, BS((TR, D), row)),
                        epilogue=_rms_bwd_epilogue, sequential=True, name=f"dh_{l}")
        if l == 0:
            big[key] = _mm("tn", sv["h"], dz, grid=(1, n // tn, nkt),
                           a_spec=BS((tk, D), lambda i, j, k: (k, 0)), b_spec=BS((tk, tn), lambda i, j, k: (k, j)),
                           out_shape=_sds((D, n), F32), out_spec=BS((D, tn), lambda i, j, k: (0, j)),
                           acc_shape=(D, tn), name=f"dw_in_{l}")
            carry = late_carry_fn(big) if late_carry_fn is not None else None
            res = _mm("nt", dz, w_in, grid=(T // TR, 1, 1), a_spec=BS((TR, n), row),
                      b_spec=BS((D, n), whole), carry=carry, **norm_bwd)
            if carry is not None:
                res, late_carried = res
            dx, small[f"norm1_g{l}"] = res
        else:
            dx, small[f"norm1_g{l}"] = _mm("nn", dz, w_in, grid=(T // TR, 1, 1),
                                           a_spec=BS((TR, n), row), b_spec=BS((n, D), whole), **norm_bwd)
            big[key] = _mm("tn", dz, sv["h"], grid=(n // tn, 1, nkt),
                           a_spec=BS((tk, tn), lambda i, j, k: (k, i)), b_spec=BS((tk, D), lambda i, j, k: (k, 0)),
                           out_shape=_sds((n, D), F32), out_spec=BS((tn, D), lambda i, j, k: (i, 0)),
                           acc_shape=(tn, D), name=f"dw_in_{l}")
        if l == 1:
            small["b_in_qm"] = _colsum(dqm, "colsum_dqm")
    return loss, dx, big, small, bwd_carried, late_carried


def _colsum(a, name):
    rows, cols = a.shape

    def body(a_ref, o_ref):
        @pl.when(pl.program_id(0) == 0)
        def _():
            o_ref[...] = jnp.zeros_like(o_ref)

        o_ref[...] += jnp.sum(a_ref[...].astype(F32), axis=0, keepdims=True)

    return pl.pallas_call(
        body, grid=(rows // TR,), in_specs=[BS((TR, cols), lambda i: (i, 0))],
        out_specs=BS((1, cols), lambda i: (0, 0)), out_shape=_sds((1, cols), F32),
        compiler_params=_cp(("arbitrary",)), name=name,
    )(a)


_PACK_ROWS = 64


def _pad_to(a, rows, cols=D):
    return jnp.pad(a, ((0, rows - a.shape[0]), (0, cols - a.shape[1])))


def _pack_small(sm):
    parts = [
        jnp.concatenate([sm["norm1_g0"], sm["norm1_g1"]], 0),
        jnp.concatenate([sm["mem_norm_g0"], sm["mem_norm_g1"]], 0),
        jnp.concatenate([sm["norm2_g0"], sm["norm2_g1"]], 0),
        _pad_to(sm["a_q_g"], 1), _pad_to(sm["a_k_g"], 1),
        _pad_to(jnp.concatenate([sm["mq_g0"], sm["mq_g1"]], 0), 2),
        _pad_to(jnp.concatenate([sm["mk_g0"], sm["mk_g1"]], 0), 2),
        _pad_to(sm["conv_b"], 1), _pad_to(sm["ln_g"], 1), _pad_to(sm["ln_b"], 1),
        _pad_to(sm["b_in_u"][:, :D], 1), _pad_to(sm["b_in_u"][:, D:], 1),
        _pad_to(sm["b_in_qm"], 1),
        _pad_to(sm["conv_w"][:CONVW], CONVW),
        sm["rel_u"].reshape(12, D),
    ]
    pack = jnp.concatenate(parts, 0)
    return jnp.pad(pack, ((0, _PACK_ROWS - pack.shape[0]), (0, 0)))


def _rel_table_to_u(rel_bias):
    flat = jnp.concatenate([jnp.broadcast_to(rel_bias[:, 191:192], (12, 447)), rel_bias[:, ::-1]], axis=1)
    return jnp.pad(flat, ((0, 0), (192, 1024 - 192 - 639))).reshape(12, 1, 1024)


def _u_to_rel_table(du):
    flat = du[:, 192:192 + 639]
    g = flat[:, 447:][:, ::-1]
    return g, flat[:, :447]


def kernel(x, mem, norm1_g, mem_norm_g, a_w_in, a_q_g, a_k_g, a_rel_bias, b_w_in, b_b_in, b_conv_w, b_conv_b, b_ln_g, b_ln_b, mq_g, mk_g, w_mem_kv, w_out, norm2_g, w_gate, w_up, w_down, loss_target, m_norm1_g, m_mem_norm_g, m_a_w_in, m_a_q_g, m_a_k_g, m_a_rel_bias, m_b_w_in, m_b_b_in, m_b_conv_w, m_b_conv_b, m_b_ln_g, m_b_ln_b, m_mq_g, m_mk_g, m_w_mem_kv, m_w_out, m_norm2_g, m_w_gate, m_w_up, m_w_down, v_norm1_g, v_mem_norm_g, v_a_w_in, v_a_q_g, v_a_k_g, v_a_rel_bias, v_b_w_in, v_b_b_in, v_b_conv_w, v_b_conv_b, v_b_ln_g, v_b_ln_b, v_mq_g, v_mk_g, v_w_mem_kv, v_w_out, v_norm2_g, v_w_gate, v_w_up, v_w_down):
    sx = 2 * lax.axis_index("x") + lax.axis_index("y")

    n_in = (NA // NSH, NBW // NSH)
    tr = lambda a: jnp.swapaxes(a, -1, -2)
    small_src = jnp.concatenate([
        jnp.pad(b_b_in, ((0, 0), (0, 512 - 448))),
        jnp.pad(b_conv_w[0], ((0, 0), (0, 512 - 192))),
        jnp.pad(jnp.concatenate([b_conv_b, b_ln_g, b_ln_b], 0), ((0, 0), (0, 512 - 192))),
        jnp.zeros((5, 512), F32)], 0)

    def gather_group(l, names, small=None):
        src_of = {
            "in": lambda: [_cast_bf16([tr(b_w_in[0])], "cast_in_1").reshape(1, 2, n_in[1] // 2, D) if l else
                           _cast_bf16([a_w_in[0]], "cast_in_0").reshape(1, 2, D // 2, n_in[0])],
            "kv": lambda: [_cast_bf16([w_mem_kv[l]], f"cast_kv_{l}").reshape(1, 2, 128, 2 * MEMW)],
            "wo": lambda: [_cast_bf16([w_out[l]], f"cast_wo_{l}").reshape(1, 2, 128, D)],
            "gu": lambda: [_cast_bf16([tr(w_gate[l])], f"cast_gate_{l}").reshape(1, 2, FS // 2, D),
                           _cast_bf16([tr(w_up[l])], f"cast_up_{l}").reshape(1, 2, FS // 2, D)],
            "wd": lambda: [_cast_bf16([w_down[l]], f"cast_wd_{l}").reshape(1, 2, FS // 2, D)]}
        dst_of = {"in": (1, NSH, 2, n_in[1] // 2, D) if l else (1, NSH, 2, D // 2, n_in[0]),
                  "kv": (1, NSH, 2, 128, 2 * MEMW), "wo": (1, NSH, 2, 128, D),
                  "gu": (1, 2, NSH, 2, FS // 2, D), "wd": (1, NSH, 2, FS // 2, D)}
        srcs, views = [], []
        for k, name in enumerate(names):
            srcs += src_of[name]()
            if name == "gu":
                views += [lambda d, k=k: d[k].at[:, 0], lambda d, k=k: d[k].at[:, 1]]
            else:
                views.append(lambda d, k=k: d[k])

        def done(w, outs):
            for k, name in enumerate(names):
                if name == "in" and l == 0:
                    w["a"] = outs[k].reshape(NSH, D, n_in[0]).transpose(1, 0, 2).reshape(D, NA)
                elif name == "in":
                    w["b"] = outs[k].reshape(NBW, D)
                else:
                    shape = {"kv": (D, 2 * MEMW), "wo": (D, D), "gu": (2, FF, D), "wd": (FF, D)}
                    w[name][l] = outs[k].reshape(shape[name])

        return _gather_exchange(srcs, [dst_of[name] for name in names], views, small), done

    w = dict(a=None, b=None, kv=[None, None], wo=[None, None], gu=[None, None], wd=[None, None])
    first, first_done = gather_group(0, ["in", "kv"], small_src)
    outs0 = _run_exchange(first, "gather_weights_first")
    first_done(w, outs0)
    small_all = outs0[2]
    carries = {"attn_fwd": gather_group(0, ["wo", "gu", "wd"]),
               "ffn_up_0": gather_group(1, ["in", "kv", "wo", "wd"]),
               "ffn_down_0": gather_group(1, ["gu"])}

    conv_w_full = small_all[:, 1:1 + CONVW, :192].transpose(1, 0, 2).reshape(CONVW, TOK)
    vec3 = small_all[:, 32:35, :192].transpose(1, 0, 2).reshape(3, TOK)
    p = dict(
        norm1_g=norm1_g, mem_norm_g=mem_norm_g, norm2_g=norm2_g,
        a_q_g2=jnp.tile(a_q_g, (1, AH)), a_k_g2=jnp.tile(a_k_g, (1, AH)),
        mq_g4=jnp.tile(mq_g, (1, 4)), mk_g4=jnp.tile(mk_g, (1, 4)),
        rel_u=_rel_table_to_u(a_rel_bias[0]),
        b_b_in=small_all[:, 0, :448].reshape(1, NBW),
        conv_w=jnp.pad(conv_w_full, ((0, 1), (0, 0))),
        conv_b=vec3[0:1], ln_g=vec3[1:2], ln_b=vec3[2:3])

    def pair_sums(items, big, name):
        units = []
        for l, tensor in items:
            if tensor == "in" and l == 0:
                g = big["a"].reshape(D, NSH, n_in[0]).transpose(1, 0, 2).reshape(1, NSH, 2, D // 2, n_in[0])
            elif tensor == "in":
                g = big["b"].reshape(1, NSH, 2, n_in[1] // 2, D)
            else:
                shape = {"kv": (1, NSH, 2, 128, 2 * MEMW), "wo": (1, NSH, 2, 128, D),
                         "gu": (2, NSH, 2, FS // 2, D), "wd": (1, NSH, 2, FS // 2, D)}
                g = big[tensor][l].reshape(shape[tensor])
            units.append(g)
        return _pair_reduce(units, name)

    early_groups = [[(1, "gu")], [(0, "gu")],
                    [(1, "in"), (1, "kv"), (1, "wo"), (1, "wd"), (0, "kv"), (0, "wo"), (0, "wd")]]
    early = [item for grp in early_groups for item in grp]
    late = [(0, "in")]
    own_early, own_late = [], []

    def bwd_carry_fn(big):
        sums_b = []
        for k, grp in enumerate(early_groups):
            own, sb = pair_sums(grp, big, f"pair_reduce_early_{k}")
            own_early.extend(own)
            sums_b.extend(sb)
        return _chip_exchange(sums_b)

    def late_carry_fn(big):
        own, sb = pair_sums(late, big, "pair_reduce_late")
        own_late.extend(own)
        return _chip_exchange(sb)

    loss, grad_x, big, small, parts_early, parts_late = _local_step(
        x.reshape(T, D), mem.reshape(NB * MEMT, D), loss_target.reshape(T, D), w, p,
        carries=carries, bwd_carry_fn=bwd_carry_fn, late_carry_fn=late_carry_fn)
    loss = lax.psum(loss[0, 0], ("x", "y", "c"))
    items = early + late
    halves = [_quad_sum(o, pt, f"quad_sum_{name}_{l}")
              for (l, name), o, pt in zip(items, own_early + own_late, list(parts_early) + list(parts_late))]
    out_shapes = [(1, D, NA // NSH), (1, NBW // NSH, D), (2, 2 * 128, 2 * MEMW), (2, 2 * 128, D),
                  (2, FS, D), (2, FS, D), (2, FS, D)]
    target_of = {"in": lambda l: [(0, l, 0)], "kv": lambda l: [(0, 2, l)], "wo": lambda l: [(0, 3, l)],
                 "gu": lambda l: [(0, 4, l), (1, 5, l)], "wd": lambda l: [(0, 6, l)]}
    targets = [target_of[name](l) for l, name in items]
    g_a, g_b, g_kv, g_wo, g_gate, g_up, g_wd = _final_exchange(halves, out_shapes, targets)

    tot = _small_allreduce(_pack_small(small))
    g_rel, clip_part = _u_to_rel_table(tot[49:61])
    g_rel = jnp.concatenate([g_rel[:, :191], g_rel[:, 191:] + _rowsum(clip_part)], axis=1)
    b_in_full = jnp.concatenate([tot[15:16], tot[16:17, :512], tot[17:18, :MEMW]], axis=1)
    g_small = dict(
        norm1_g=tot[0:2], mem_norm_g=tot[2:4], norm2_g=tot[4:6],
        a_q_g=tot[6:7, :HD], a_k_g=tot[7:8, :HD], a_rel_bias=g_rel[None],
        b_b_in=lax.dynamic_slice(b_in_full, (0, sx * 448), (1, 448)),
        b_conv_w=lax.dynamic_slice(tot[18:49, :TOK], (0, sx * 192), (CONVW, 192))[None],
        b_conv_b=lax.dynamic_slice(tot[12:13, :TOK], (0, sx * 192), (1, 192)),
        b_ln_g=lax.dynamic_slice(tot[13:14, :TOK], (0, sx * 192), (1, 192)),
        b_ln_b=lax.dynamic_slice(tot[14:15, :TOK], (0, sx * 192), (1, 192)),
        mq_g=tot[8:10, :HD], mk_g=tot[10:12, :HD])

    names = ["norm1_g", "mem_norm_g", "a_w_in", "a_q_g", "a_k_g", "a_rel_bias", "b_w_in", "b_b_in",
             "b_conv_w", "b_conv_b", "b_ln_g", "b_ln_b", "mq_g", "mk_g", "w_mem_kv", "w_out",
             "norm2_g", "w_gate", "w_up", "w_down"]
    weights = dict(zip(names, [norm1_g, mem_norm_g, a_w_in, a_q_g, a_k_g, a_rel_bias, b_w_in, b_b_in,
                               b_conv_w, b_conv_b, b_ln_g, b_ln_b, mq_g, mk_g, w_mem_kv, w_out,
                               norm2_g, w_gate, w_up, w_down]))
    ms = dict(zip(names, [m_norm1_g, m_mem_norm_g, m_a_w_in, m_a_q_g, m_a_k_g, m_a_rel_bias, m_b_w_in,
                          m_b_b_in, m_b_conv_w, m_b_conv_b, m_b_ln_g, m_b_ln_b, m_mq_g, m_mk_g,
                          m_w_mem_kv, m_w_out, m_norm2_g, m_w_gate, m_w_up, m_w_down]))
    vs = dict(zip(names, [v_norm1_g, v_mem_norm_g, v_a_w_in, v_a_q_g, v_a_k_g, v_a_rel_bias, v_b_w_in,
                          v_b_b_in, v_b_conv_w, v_b_conv_b, v_b_ln_g, v_b_ln_b, v_mq_g, v_mk_g,
                          v_w_mem_kv, v_w_out, v_norm2_g, v_w_gate, v_w_up, v_w_down]))
    grads = dict(g_small)
    grads.update(a_w_in=g_a, b_w_in=g_b, w_mem_kv=g_kv, w_out=g_wo, w_gate=g_gate, w_up=g_up, w_down=g_wd)
    big_names = ["a_w_in", "b_w_in", "w_mem_kv", "w_out", "w_gate", "w_up", "w_down"]
    small_names = [n for n in names if n not in big_names]
    delta, new_m, new_v = {}, {}, {}
    for n in big_names:
        if n in ("b_w_in", "w_gate", "w_up"):
            outs = _adamw_big(tr(weights[n]), grads[n], tr(ms[n]), tr(vs[n]), f"adamw_{n}")
            delta[n], new_m[n], new_v[n] = [tr(o) for o in outs]
            grads[n] = tr(grads[n])
        else:
            delta[n], new_m[n], new_v[n] = _adamw_big(weights[n], grads[n], ms[n], vs[n], f"adamw_{n}")
    as2d = lambda a: a.reshape(-1, a.shape[-1])
    d_s, m_s, v_s = _adamw_small([as2d(weights[n]) for n in small_names], [as2d(grads[n]) for n in small_names],
                                 [as2d(ms[n]) for n in small_names], [as2d(vs[n]) for n in small_names])
    for i, n in enumerate(small_names):
        delta[n] = d_s[i].reshape(weights[n].shape)
        new_m[n] = m_s[i].reshape(weights[n].shape)
        new_v[n] = v_s[i].reshape(weights[n].shape)

    return (loss, grad_x.reshape(NB, SEQ, D), *[grads[n] for n in names], *[delta[n] for n in names],
            *[new_m[n] for n in names], *[new_v[n] for n in names])


def _rowsum(a):
    def body(a_ref, o_ref):
        o_ref[...] = jnp.sum(a_ref[...], axis=1, keepdims=True)

    vm = BS(memory_space=pltpu.VMEM)
    return pl.pallas_call(body, in_specs=[vm], out_specs=vm, out_shape=_sds((a.shape[0], 1), F32),
                          compiler_params=_cp(), name="rowsum")(a)
```

```python
import functools

import jax
import jax.numpy as jnp
from jax import lax
from jax.experimental import pallas as pl
from jax.experimental.pallas import tpu as pltpu

F32 = jnp.float32
BF16 = jnp.bfloat16
BS = pl.BlockSpec
ANY = pl.BlockSpec(memory_space=pl.ANY)
MESH = pl.DeviceIdType.MESH

D = 1024
SEQ = 2048
NB = 2
T = NB * SEQ
MEMT = 256
HD = 64
TOK = 768
MEMW = 256
NA = 3 * TOK + MEMW
NBW = 2 * TOK + MEMW
FF = 2816
NSH = 4
FS = FF // NSH
FT = FF // 2
CONVW = 31
EPS = 1e-6
NEG = -1e30
SCALE = HD ** -0.5
QB = 256
KWIN = 768
KPAD = 512
TR = 512

ADAM_LR = 0.001
ADAM_B1 = 0.9
ADAM_B2 = 0.999
ADAM_EPS = 1e-08
ADAM_WD = 0.01
ADAM_STEP = 10

_DIMS = {
    "nn": (((1,), (0,)), ((), ())),
    "nt": (((1,), (1,)), ((), ())),
    "tn": (((0,), (0,)), ((), ())),
}


def _cp(sem=None, vmem_mb=48):
    return pltpu.CompilerParams(dimension_semantics=sem, vmem_limit_bytes=vmem_mb << 20)


def _sds(shape, dtype):
    return jax.ShapeDtypeStruct(tuple(shape), dtype)


def _mm(mode, a, b, *, grid, a_spec, b_spec, out_shape, out_spec, acc_shape, name,
        extras=(), extra_specs=(), epilogue=None, carry=None, vmem_mb=48, sequential=False):
    n_ex = len(extras)
    nk = grid[2]
    dims = _DIMS[mode]
    ni = len(carry.ins) if carry else 0
    no = len(carry.out_shapes) if carry else 0
    multi = isinstance(out_shape, (list, tuple))
    out_shapes = list(out_shape) if multi else [out_shape]
    out_specs = list(out_spec) if multi else [out_spec]
    n_o = len(out_shapes)

    def body(a_ref, b_ref, *rest):
        ex = rest[:n_ex]
        cin = rest[n_ex:n_ex + ni]
        o_refs = rest[n_ex + ni:n_ex + ni + n_o]
        o_ref = o_refs if multi else o_refs[0]
        cout = rest[n_ex + ni + n_o:n_ex + ni + n_o + no]
        acc = rest[n_ex + ni + n_o + no]
        cscr = rest[n_ex + ni + n_o + no + 1:]
        ids = [pl.program_id(d) for d in range(3)]
        k = ids[2]
        if carry:
            @pl.when((ids[0] == 0) & (ids[1] == 0) & (ids[2] == 0))
            def _():
                carry.start(cin, cout, cscr)

        prod = lax.dot_general(a_ref[...].astype(BF16), b_ref[...].astype(BF16), dims,
                               preferred_element_type=F32)

        def finish(val):
            if epilogue is None:
                o_ref[...] = val.astype(o_ref.dtype)
            else:
                epilogue(val, ex, o_ref, ids[0])

        if nk == 1:
            finish(prod)
        else:
            @pl.when(k == 0)
            def _():
                acc[...] = prod

            @pl.when((k > 0) & (k < nk - 1))
            def _():
                acc[...] += prod

            @pl.when(k == nk - 1)
            def _():
                finish(acc[...] + prod)

        if carry:
            @pl.when((ids[0] == grid[0] - 1) & (ids[1] == grid[1] - 1) & (ids[2] == grid[2] - 1))
            def _():
                carry.finish(cin, cout, cscr)

    acc_scratch = pltpu.VMEM(acc_shape if nk > 1 else (8, 128), F32)
    ordered = sequential or bool(carry)
    outs = pl.pallas_call(
        body, grid=grid,
        in_specs=[a_spec, b_spec, *extra_specs] + (carry.in_specs if carry else []),
        out_specs=out_specs + [ANY] * no, out_shape=out_shapes + (carry.out_shapes if carry else []),
        scratch_shapes=[acc_scratch] + (carry.scratch if carry else []),
        compiler_params=pltpu.CompilerParams(
            dimension_semantics=("arbitrary",) * 3 if ordered else ("parallel", "parallel", "arbitrary"),
            vmem_limit_bytes=vmem_mb << 20, has_side_effects=bool(carry)), name=name,
    )(a, b, *extras, *(carry.ins if carry else []))
    mine = list(outs[:n_o]) if multi else outs[0]
    return (mine, outs[n_o:]) if carry else mine


def _rms_fwd(x, g, name):
    rows = x.shape[0]

    def body(x_ref, g_ref, o_ref):
        xv = x_ref[...]
        r = lax.rsqrt(jnp.mean(xv * xv, axis=-1, keepdims=True) + EPS)
        o_ref[...] = (xv * r * g_ref[...]).astype(BF16)

    return pl.pallas_call(
        body, grid=(rows // TR,),
        in_specs=[BS((TR, D), lambda i: (i, 0)), BS((1, D), lambda i: (0, 0))],
        out_specs=BS((TR, D), lambda i: (i, 0)), out_shape=_sds((rows, D), BF16),
        compiler_params=_cp(("arbitrary",)), name=name,
    )(x, g)


def _rms_bwd(dh, x, g, dres, name):
    rows = x.shape[0]
    has_res = dres is not None

    def body(*refs):
        if has_res:
            dh_ref, x_ref, g_ref, r_ref, dx_ref, dg_ref = refs
        else:
            dh_ref, x_ref, g_ref, dx_ref, dg_ref = refs
        xv = x_ref[...]
        dhv = dh_ref[...]
        r = lax.rsqrt(jnp.mean(xv * xv, axis=-1, keepdims=True) + EPS)
        xh = xv * r
        gy = dhv * g_ref[...]
        dx = r * (gy - xh * jnp.mean(gy * xh, axis=-1, keepdims=True))
        if has_res:
            dx = dx + r_ref[...]
        dx_ref[...] = dx

        @pl.when(pl.program_id(0) == 0)
        def _():
            dg_ref[...] = jnp.zeros_like(dg_ref)

        dg_ref[...] += jnp.sum(dhv * xh, axis=0, keepdims=True)

    row = BS((TR, D), lambda i: (i, 0))
    vec = BS((1, D), lambda i: (0, 0))
    ins = [dh, x, g] + ([dres] if has_res else [])
    return pl.pallas_call(
        body, grid=(rows // TR,),
        in_specs=[row, row, vec] + ([row] if has_res else []),
        out_specs=[row, vec], out_shape=[_sds((rows, D), F32), _sds((1, D), F32)],
        compiler_params=_cp(("arbitrary",)), name=name,
    )(*ins)


def _group_masks(width):
    lane = lax.broadcasted_iota(jnp.int32, (1, width), 1)
    return [(lane >= HD * h) & (lane < HD * (h + 1)) for h in range(width // HD)]


def _group_mean(v, masks):
    out = jnp.zeros_like(v)
    for m in masks:
        s = jnp.sum(jnp.where(m, v, 0.0), axis=-1, keepdims=True) * (1.0 / HD)
        out = jnp.where(m, s, out)
    return out


def _head_norm(zv, g, masks):
    r = lax.rsqrt(_group_mean(zv * zv, masks) + EPS)
    return zv * r * g


def _head_norm_bwd(dy, zv, g, masks):
    r = lax.rsqrt(_group_mean(zv * zv, masks) + EPS)
    zh = zv * r
    gy = dy * g
    dz = r * (gy - zh * _group_mean(gy * zh, masks))
    return dz, jnp.sum(dy * zh, axis=0, keepdims=True)


def _fold_heads(v, width):
    vb = jnp.broadcast_to(v, (8, width))
    out = vb
    for h in range(1, width // HD):
        out = out + pltpu.roll(vb, width - HD * h, axis=1)
    return out[0:1]


def _bias_expand(u):
    def body(u_ref, o_ref):
        x = jnp.broadcast_to(u_ref[...], (QB, 1024))
        rolled = pltpu.roll(x, 1024 - (QB - 1), axis=1, stride=1, stride_axis=0)[:, :KWIN]
        row = lax.broadcasted_iota(jnp.int32, (QB, 1), 0)
        col = lax.broadcasted_iota(jnp.int32, (1, KWIN), 1)
        lo = (row // 64) * 64
        ok = (col >= lo) & (col < lo + 576)
        o_ref[...] = jnp.where(ok, rolled, NEG)

    return pl.pallas_call(
        body, grid=(12,), in_specs=[BS((None, 1, 1024), lambda h: (h, 0, 0))],
        out_specs=BS((None, QB, KWIN), lambda h: (h, 0, 0)), out_shape=_sds((12, QB, KWIN), F32),
        compiler_params=_cp(("arbitrary",)), name="bias_expand",
    )(u)


def _bias_reduce(ds):
    def body(d_ref, o_ref):
        ri = lax.broadcasted_iota(jnp.int32, (QB, QB), 0)
        ci = lax.broadcasted_iota(jnp.int32, (QB, QB), 1)
        flip = (ri + ci == QB - 1).astype(F32)
        drev = jnp.dot(flip, d_ref[...], precision=lax.Precision.HIGHEST, preferred_element_type=F32)
        x = jnp.concatenate([drev, jnp.zeros((QB, 1024 - KWIN), F32)], axis=1)
        rolled = pltpu.roll(x, 0, axis=1, stride=1, stride_axis=0)
        o_ref[...] = jnp.sum(rolled, axis=0, keepdims=True)

    return pl.pallas_call(
        body, grid=(12,), in_specs=[BS((None, QB, KWIN), lambda h: (h, 0, 0))],
        out_specs=BS((None, 1, 1024), lambda h: (h, 0, 0)), out_shape=_sds((12, 1, 1024), F32),
        compiler_params=_cp(("arbitrary",)), name="bias_reduce",
    )(ds)


AW = 256
AH = AW // HD
AG = TOK // AW
def _attn_softmax(qh, kw, bias, startadd):
    s = lax.dot_general(qh, kw, _DIMS["nt"], preferred_element_type=F32) + bias + startadd
    m = jnp.max(s, axis=-1, keepdims=True)
    p = jnp.exp(s - m)
    return p * (1.0 / jnp.sum(p, axis=-1, keepdims=True))


def _attn_prologue(q_ref, k_ref, v_ref, gq_ref, gk_ref, qn_s, kn_s, v_s, masks):
    kn_s[0:KPAD, :] = jnp.zeros((KPAD, AW), BF16)
    v_s[0:KPAD, :] = jnp.zeros((KPAD, AW), BF16)
    for r in range(0, SEQ, TR):
        qn_s[r:r + TR, :] = (_head_norm(q_ref[r:r + TR, :], gq_ref[...], masks) * SCALE).astype(BF16)
        kn_s[KPAD + r:KPAD + r + TR, :] = _head_norm(k_ref[r:r + TR, :], gk_ref[...], masks).astype(BF16)
        v_s[KPAD + r:KPAD + r + TR, :] = v_ref[r:r + TR, :].astype(BF16)


def _attn_fwd(z, bias, gq2, gk2, carry=None):
    ni = len(carry.ins) if carry else 0
    no = len(carry.out_shapes) if carry else 0

    def body(q_ref, k_ref, v_ref, b_ref, gq_ref, gk_ref, *rest):
        cin, o_ref, cout = rest[:ni], rest[ni], rest[ni + 1:ni + 1 + no]
        qn_s, kn_s, v_s = rest[ni + 1 + no:ni + 4 + no]
        cscr = rest[ni + 4 + no:]
        if carry:
            @pl.when((pl.program_id(0) == 0) & (pl.program_id(1) == 0))
            def _():
                carry.start(cin, cout, cscr)

        masks = _group_masks(AW)
        _attn_prologue(q_ref, k_ref, v_ref, gq_ref, gk_ref, qn_s, kn_s, v_s, masks)
        col = lax.broadcasted_iota(jnp.int32, (1, KWIN), 1)

        def blk(i, carry):
            r0 = pl.multiple_of(i * QB, QB)
            qb = qn_s[pl.ds(r0, QB), :]
            kw = kn_s[pl.ds(r0, KWIN), :]
            vw = v_s[pl.ds(r0, KWIN), :]
            startadd = jnp.where(col + r0 < KPAD, NEG, 0.0)
            o = jnp.zeros((QB, AW), F32)
            for h in range(AH):
                qh = jnp.where(masks[h], qb, jnp.zeros_like(qb))
                vh = jnp.where(masks[h], vw, jnp.zeros_like(vw))
                p = _attn_softmax(qh, kw, b_ref[h], startadd).astype(BF16)
                o = o + jnp.dot(p, vh, preferred_element_type=F32)
            o_ref[pl.ds(r0, QB), :] = o.astype(BF16)
            return carry

        lax.fori_loop(0, SEQ // QB, blk, 0)

        if carry:
            @pl.when((pl.program_id(0) == NB - 1) & (pl.program_id(1) == AG - 1))
            def _():
                carry.finish(cin, cout, cscr)

    vec = BS((1, AW), lambda b, hp: (0, 0))
    outs = pl.pallas_call(
        body, grid=(NB, AG),
        in_specs=[BS((SEQ, AW), lambda b, hp: (b, hp)),
                  BS((SEQ, AW), lambda b, hp: (b, AG + hp)),
                  BS((SEQ, AW), lambda b, hp: (b, 2 * AG + hp)),
                  BS((AH, QB, KWIN), lambda b, hp: (hp, 0, 0)), vec, vec]
        + (carry.in_specs if carry else []),
        out_specs=[BS((SEQ, AW), lambda b, hp: (b, hp))] + [ANY] * no,
        out_shape=[_sds((T, D), BF16)] + (carry.out_shapes if carry else []),
        scratch_shapes=[pltpu.VMEM((SEQ, AW), BF16), pltpu.VMEM((SEQ + KPAD, AW), BF16),
                        pltpu.VMEM((SEQ + KPAD, AW), BF16)] + (carry.scratch if carry else []),
        compiler_params=pltpu.CompilerParams(
            dimension_semantics=("arbitrary", "arbitrary"), vmem_limit_bytes=48 << 20,
            has_side_effects=bool(carry)), name="attn_fwd",
    )(z, z, z, bias, gq2, gk2, *(carry.ins if carry else []))
    return outs[0], outs[1:]


def _attn_bwd(z, dcat, bias, gq2, gk2, carry=None):
    ni = len(carry.ins) if carry else 0
    no = len(carry.out_shapes) if carry else 0

    def body(q_ref, k_ref, v_ref, do_ref, b_ref, gq_ref, gk_ref, *rest):
        cin = rest[:ni]
        dq_ref, dk_ref, dv_ref, db_ref, dgq_ref, dgk_ref = rest[ni:ni + 6]
        cout = rest[ni + 6:ni + 6 + no]
        qn_s, kn_s, v_s, dqn_s, dkn_s, dv_s = rest[ni + 6 + no:ni + 12 + no]
        cscr = rest[ni + 12 + no:]
        hp = pl.program_id(0)
        b = pl.program_id(1)
        if carry:
            @pl.when((hp == 0) & (b == 0))
            def _():
                carry.start(cin, cout, cscr)

        masks = _group_masks(AW)
        _attn_prologue(q_ref, k_ref, v_ref, gq_ref, gk_ref, qn_s, kn_s, v_s, masks)
        dkn_s[...] = jnp.zeros_like(dkn_s)
        dv_s[...] = jnp.zeros_like(dv_s)

        @pl.when(b == 0)
        def _():
            db_ref[...] = jnp.zeros_like(db_ref)

        @pl.when((b == 0) & (hp == 0))
        def _():
            dgq_ref[...] = jnp.zeros_like(dgq_ref)
            dgk_ref[...] = jnp.zeros_like(dgk_ref)

        col = lax.broadcasted_iota(jnp.int32, (1, KWIN), 1)

        def blk(i, carry):
            r0 = pl.multiple_of(i * QB, QB)
            qb = qn_s[pl.ds(r0, QB), :]
            kw = kn_s[pl.ds(r0, KWIN), :]
            vw = v_s[pl.ds(r0, KWIN), :]
            dob = do_ref[pl.ds(r0, QB), :].astype(BF16)
            startadd = jnp.where(col + r0 < KPAD, NEG, 0.0)
            dqn = jnp.zeros((QB, AW), F32)
            dkw = jnp.zeros((KWIN, AW), F32)
            dvw = jnp.zeros((KWIN, AW), F32)
            for h in range(AH):
                qh = jnp.where(masks[h], qb, jnp.zeros_like(qb))
                kh = jnp.where(masks[h], kw, jnp.zeros_like(kw))
                doh = jnp.where(masks[h], dob, jnp.zeros_like(dob))
                p = _attn_softmax(qh, kw, b_ref[h], startadd)
                dvw = dvw + lax.dot_general(p.astype(BF16), doh, _DIMS["tn"],
                                            preferred_element_type=F32)
                dp = lax.dot_general(doh, vw, _DIMS["nt"], preferred_element_type=F32)
                ds = p * (dp - jnp.sum(dp * p, axis=-1, keepdims=True))
                db_ref[h] += ds
                dsb = ds.astype(BF16)
                dqn = dqn + jnp.dot(dsb, kh, preferred_element_type=F32)
                dkw = dkw + lax.dot_general(dsb, qh, _DIMS["tn"], preferred_element_type=F32)
            dqn_s[pl.ds(r0, QB), :] = dqn * SCALE
            dkn_s[pl.ds(r0, KWIN), :] += dkw
            dv_s[pl.ds(r0, KWIN), :] += dvw
            return carry

        lax.fori_loop(0, SEQ // QB, blk, 0)

        dgq = jnp.zeros((1, AW), F32)
        dgk = jnp.zeros((1, AW), F32)
        for r in range(0, SEQ, TR):
            dq, dg = _head_norm_bwd(dqn_s[r:r + TR, :], q_ref[r:r + TR, :], gq_ref[...], masks)
            dq_ref[r:r + TR, :] = dq.astype(BF16)
            dgq = dgq + dg
            dk, dg = _head_norm_bwd(dkn_s[KPAD + r:KPAD + r + TR, :], k_ref[r:r + TR, :], gk_ref[...], masks)
            dk_ref[r:r + TR, :] = dk.astype(BF16)
            dgk = dgk + dg
            dv_ref[r:r + TR, :] = dv_s[KPAD + r:KPAD + r + TR, :].astype(BF16)
        dgq_ref[...] += _fold_heads(dgq, AW)
        dgk_ref[...] += _fold_heads(dgk, AW)

        if carry:
            @pl.when((hp == AG - 1) & (b == NB - 1))
            def _():
                carry.finish(cin, cout, cscr)

    vec = BS((1, AW), lambda hp, b: (0, 0))
    row = BS((SEQ, AW), lambda hp, b: (b, hp))
    outs = pl.pallas_call(
        body, grid=(AG, NB),
        in_specs=[row,
                  BS((SEQ, AW), lambda hp, b: (b, AG + hp)),
                  BS((SEQ, AW), lambda hp, b: (b, 2 * AG + hp)),
                  row,
                  BS((AH, QB, KWIN), lambda hp, b: (hp, 0, 0)), vec, vec]
        + (carry.in_specs if carry else []),
        out_specs=[row, row, row, BS((AH, QB, KWIN), lambda hp, b: (hp, 0, 0)), vec, vec] + [ANY] * no,
        out_shape=[_sds((T, TOK), BF16), _sds((T, TOK), BF16), _sds((T, TOK), BF16),
                   _sds((12, QB, KWIN), F32), _sds((1, AW), F32), _sds((1, AW), F32)]
        + (carry.out_shapes if carry else []),
        scratch_shapes=[pltpu.VMEM((SEQ, AW), BF16), pltpu.VMEM((SEQ + KPAD, AW), BF16),
                        pltpu.VMEM((SEQ + KPAD, AW), BF16), pltpu.VMEM((SEQ, AW), F32),
                        pltpu.VMEM((SEQ + KPAD, AW), F32), pltpu.VMEM((SEQ + KPAD, AW), F32)]
        + (carry.scratch if carry else []),
        compiler_params=pltpu.CompilerParams(
            dimension_semantics=("arbitrary", "arbitrary"), vmem_limit_bytes=58 << 20,
            has_side_effects=bool(carry)), name="attn_bwd",
    )(z, z, z, dcat, bias, gq2, gk2, *(carry.ins if carry else []))
    return outs[:6], outs[6:]


def _mem_softmax(qh, kn):
    s = lax.dot_general(qh, kn, _DIMS["nt"], preferred_element_type=F32)
    m = jnp.max(s, axis=-1, keepdims=True)
    p = jnp.exp(s - m)
    return p * (1.0 / jnp.sum(p, axis=-1, keepdims=True))


def _memattn_fwd(z, kv, cat, gq4, gk4, qcol, name):
    def body(q_ref, k_ref, v_ref, gq_ref, gk_ref, cat_ref, o_ref):
        del cat_ref
        masks = _group_masks(MEMW)
        qn = (_head_norm(q_ref[...], gq_ref[...], masks) * SCALE).astype(BF16)
        kn = _head_norm(k_ref[...], gk_ref[...], masks).astype(BF16)
        vv = v_ref[...].astype(BF16)
        o = jnp.zeros((TR, MEMW), F32)
        for h in range(4):
            qh = jnp.where(masks[h], qn, jnp.zeros_like(qn))
            vh = jnp.where(masks[h], vv, jnp.zeros_like(vv))
            p = _mem_softmax(qh, kn).astype(BF16)
            o = o + jnp.dot(p, vh, preferred_element_type=F32)
        o_ref[...] = o.astype(BF16)

    nt = SEQ // TR
    vec = BS((1, MEMW), lambda b, t: (0, 0))
    return pl.pallas_call(
        body, grid=(NB, nt),
        in_specs=[BS((TR, MEMW), lambda b, t: (b * nt + t, qcol)),
                  BS((MEMT, MEMW), lambda b, t: (b, 0)),
                  BS((MEMT, MEMW), lambda b, t: (b, 1)), vec, vec, ANY],
        out_specs=BS((TR, MEMW), lambda b, t: (b * nt + t, 3)),
        out_shape=_sds((T, D), BF16), input_output_aliases={5: 0},
        compiler_params=_cp(("arbitrary", "arbitrary")), name=name,
    )(z, kv, kv, gq4, gk4, cat)


def _memattn_bwd(z, kv, dcat, gq4, gk4, qcol, name):
    nt = SEQ // TR

    def body(q_ref, k_ref, v_ref, do_ref, gq_ref, gk_ref,
             dq_ref, dkv_ref, dgq_ref, dgk_ref, dkn_s, dv_s):
        b = pl.program_id(0)
        t = pl.program_id(1)
        masks = _group_masks(MEMW)
        qz = q_ref[...]
        kz = k_ref[...]
        qn = (_head_norm(qz, gq_ref[...], masks) * SCALE).astype(BF16)
        kn = _head_norm(kz, gk_ref[...], masks).astype(BF16)
        vv = v_ref[...].astype(BF16)
        dob = do_ref[...].astype(BF16)

        @pl.when(t == 0)
        def _():
            dkn_s[...] = jnp.zeros_like(dkn_s)
            dv_s[...] = jnp.zeros_like(dv_s)

        @pl.when((t == 0) & (b == 0))
        def _():
            dgq_ref[...] = jnp.zeros_like(dgq_ref)
            dgk_ref[...] = jnp.zeros_like(dgk_ref)

        dqn = jnp.zeros((TR, MEMW), F32)
        dkn = jnp.zeros((MEMT, MEMW), F32)
        dvv = jnp.zeros((MEMT, MEMW), F32)
        for h in range(4):
            qh = jnp.where(masks[h], qn, jnp.zeros_like(qn))
            kh = jnp.where(masks[h], kn, jnp.zeros_like(kn))
            doh = jnp.where(masks[h], dob, jnp.zeros_like(dob))
            p = _mem_softmax(qh, kn)
            dvv = dvv + lax.dot_general(p.astype(BF16), doh, _DIMS["tn"], preferred_element_type=F32)
            dp = lax.dot_general(doh, vv, _DIMS["nt"], preferred_element_type=F32)
            ds = p * (dp - jnp.sum(dp * p, axis=-1, keepdims=True))
            dsb = ds.astype(BF16)
            dqn = dqn + jnp.dot(dsb, kh, preferred_element_type=F32)
            dkn = dkn + lax.dot_general(dsb, qh, _DIMS["tn"], preferred_element_type=F32)
        dkn_s[...] += dkn
        dv_s[...] += dvv
        dq, dgq = _head_norm_bwd(dqn * SCALE, qz, gq_ref[...], masks)
        dq_ref[...] = dq.astype(BF16)
        dgq_ref[...] += _fold_heads(dgq, MEMW)

        @pl.when(t == nt - 1)
        def _():
            dk, dgk = _head_norm_bwd(dkn_s[...], kz, gk_ref[...], masks)
            dkv_ref[:, 0:MEMW] = dk
            dkv_ref[:, MEMW:] = dv_s[...]
            dgk_ref[...] += _fold_heads(dgk, MEMW)

    vec = BS((1, MEMW), lambda b, t: (0, 0))
    return pl.pallas_call(
        body, grid=(NB, nt),
        in_specs=[BS((TR, MEMW), lambda b, t: (b * nt + t, qcol)),
                  BS((MEMT, MEMW), lambda b, t: (b, 0)),
                  BS((MEMT, MEMW), lambda b, t: (b, 1)),
                  BS((TR, MEMW), lambda b, t: (b * nt + t, 3)), vec, vec],
        out_specs=[BS((TR, MEMW), lambda b, t: (b * nt + t, 0)),
                   BS((MEMT, 2 * MEMW), lambda b, t: (b, 0)), vec, vec],
        out_shape=[_sds((T, MEMW), BF16), _sds((NB * MEMT, 2 * MEMW), F32),
                   _sds((1, MEMW), F32), _sds((1, MEMW), F32)],
        scratch_shapes=[pltpu.VMEM((MEMT, MEMW), F32), pltpu.VMEM((MEMT, MEMW), F32)],
        compiler_params=_cp(("arbitrary", "arbitrary")), name=name,
    )(z, kv, kv, dcat, gq4, gk4)


HALO = 32
NEXT = 64
RT = 64


def _glu(zz):
    return zz[:, :TOK] * jax.nn.sigmoid(zz[:, TOK:])


def _layer_norm_parts(y):
    mu = jnp.mean(y, axis=-1, keepdims=True)
    yc = y - mu
    rstd = lax.rsqrt(jnp.mean(yc * yc, axis=-1, keepdims=True) + EPS)
    return yc * rstd, rstd


def _shifted_copies(src, dst, rows):
    for b in range(1, 8):
        dst[b - 1, 0:rows, :] = src[b:b + rows, :]


def _tap(src, shifted, off, r0, rows):
    b = off % 8
    if b == 0:
        return src[r0 + off:r0 + off + rows, :]
    return shifted[b - 1, r0 + off - b:r0 + off - b + rows, :]


def _conv_rows(w_ref, hbuf, hs, r0, rows):
    y = jnp.zeros((rows, TOK), F32)
    for j in range(CONVW):
        y = y + w_ref[j:j + 1, :] * _tap(hbuf, hs, (HALO - CONVW + 1) + j, r0, rows)
    return y


def _conv_fwd(z, cw, cb, lg, lb):
    nt = SEQ // TR

    def body(zc_ref, zp_ref, w_ref, cb_ref, lg_ref, lb_ref, o_ref, hbuf, hs):
        t = pl.program_id(1)
        hbuf[0:HALO, :] = jnp.where(t == 0, 0.0, _glu(zp_ref[...]))
        hbuf[HALO:, :] = _glu(zc_ref[...])
        _shifted_copies(hbuf, hs, HALO + TR - 8)
        for r0 in range(0, TR, RT):
            y = _conv_rows(w_ref, hbuf, hs, r0, RT) + cb_ref[...]
            yh, _ = _layer_norm_parts(y)
            o = yh * lg_ref[...] + lb_ref[...]
            o_ref[r0:r0 + RT, :] = (o * jax.nn.sigmoid(o)).astype(BF16)

    vec = BS((1, TOK), lambda b, t: (0, 0))
    per = TR // HALO
    return pl.pallas_call(
        body, grid=(NB, nt),
        in_specs=[BS((TR, 2 * TOK), lambda b, t: (b * nt + t, 0)),
                  BS((HALO, 2 * TOK), lambda b, t: (jnp.maximum((b * nt + t) * per - 1, 0), 0)),
                  BS((32, TOK), lambda b, t: (0, 0)), vec, vec, vec],
        out_specs=BS((TR, TOK), lambda b, t: (b * nt + t, 0)),
        out_shape=_sds((T, D), BF16),
        scratch_shapes=[pltpu.VMEM((HALO + TR, TOK), F32), pltpu.VMEM((7, HALO + TR, TOK), F32)],
        compiler_params=_cp(("arbitrary", "arbitrary")), name="conv_fwd",
    )(z, z, cw, cb, lg, lb)


def _conv_bwd(z, dcat, cw, cb, lg, lb):
    nt = SEQ // TR
    ext = TR + NEXT

    def body(zc_ref, zp_ref, zn_ref, dc_ref, dn_ref, w_ref, cb_ref, lg_ref, lb_ref,
             du_ref, dw_ref, dcb_ref, dlg_ref, dlb_ref, dbin_ref, hbuf, dybuf, hs, dys):
        b = pl.program_id(0)
        t = pl.program_id(1)

        @pl.when((b == 0) & (t == 0))
        def _():
            dw_ref[...] = jnp.zeros_like(dw_ref)
            dcb_ref[...] = jnp.zeros_like(dcb_ref)
            dlg_ref[...] = jnp.zeros_like(dlg_ref)
            dlb_ref[...] = jnp.zeros_like(dlb_ref)
            dbin_ref[...] = jnp.zeros_like(dbin_ref)

        hbuf[0:HALO, :] = jnp.where(t == 0, 0.0, _glu(zp_ref[...]))
        hbuf[HALO:HALO + TR, :] = _glu(zc_ref[...])
        hbuf[HALO + TR:, :] = _glu(zn_ref[...])
        _shifted_copies(hbuf, hs, HALO + TR + NEXT - 8)
        last = t == nt - 1
        for r0 in range(0, ext, RT):
            y = _conv_rows(w_ref, hbuf, hs, r0, RT) + cb_ref[...]
            yh, rstd = _layer_norm_parts(y)
            o = yh * lg_ref[...] + lb_ref[...]
            sg = jax.nn.sigmoid(o)
            if r0 < TR:
                dtok = dc_ref[r0:r0 + RT, :]
            else:
                dtok = jnp.where(last, 0.0, dn_ref[...])
            do = dtok * (sg * (1.0 + o * (1.0 - sg)))
            dyh = do * lg_ref[...]
            dy = rstd * (dyh - jnp.mean(dyh, axis=-1, keepdims=True)
                         - yh * jnp.mean(dyh * yh, axis=-1, keepdims=True))
            dybuf[r0:r0 + RT, :] = dy
            if r0 < TR:
                dlg_ref[...] += jnp.sum(do * yh, axis=0, keepdims=True)
                dlb_ref[...] += jnp.sum(do, axis=0, keepdims=True)
                dcb_ref[...] += jnp.sum(dy, axis=0, keepdims=True)
        _shifted_copies(dybuf, dys, ext - 8)
        for r0 in range(0, TR, RT):
            dh = jnp.zeros((RT, TOK), F32)
            for j in range(CONVW):
                dh = dh + w_ref[j:j + 1, :] * _tap(dybuf, dys, (CONVW - 1) - j, r0, RT)
            a = zc_ref[r0:r0 + RT, 0:TOK]
            sg = jax.nn.sigmoid(zc_ref[r0:r0 + RT, TOK:])
            da = dh * sg
            dg = dh * a * (sg * (1.0 - sg))
            du_ref[r0:r0 + RT, 0:TOK] = da.astype(BF16)
            du_ref[r0:r0 + RT, TOK:] = dg.astype(BF16)
            dbin_ref[:, 0:TOK] += jnp.sum(da, axis=0, keepdims=True)
            dbin_ref[:, TOK:] += jnp.sum(dg, axis=0, keepdims=True)
        for j in range(CONVW):
            acc = jnp.zeros((8, TOK), F32)
            for r0 in range(0, TR, RT):
                prod = dybuf[r0:r0 + RT, :] * _tap(hbuf, hs, (HALO - CONVW + 1) + j, r0, RT)
                acc = acc + jnp.sum(prod.reshape(RT // 8, 8, TOK), axis=0)
            dw_ref[j:j + 1, :] += jnp.sum(acc, axis=0, keepdims=True)

    vec = BS((1, TOK), lambda b, t: (0, 0))
    perh = TR // HALO
    pern = TR // NEXT
    nlast_n = T // NEXT - 1
    return pl.pallas_call(
        body, grid=(NB, nt),
        in_specs=[BS((TR, 2 * TOK), lambda b, t: (b * nt + t, 0)),
                  BS((HALO, 2 * TOK), lambda b, t: (jnp.maximum((b * nt + t) * perh - 1, 0), 0)),
                  BS((NEXT, 2 * TOK), lambda b, t: (jnp.minimum((b * nt + t + 1) * pern, nlast_n), 0)),
                  BS((TR, TOK), lambda b, t: (b * nt + t, 0)),
                  BS((NEXT, TOK), lambda b, t: (jnp.minimum((b * nt + t + 1) * pern, nlast_n), 0)),
                  BS((32, TOK), lambda b, t: (0, 0)), vec, vec, vec],
        out_specs=[BS((TR, 2 * TOK), lambda b, t: (b * nt + t, 0)),
                   BS((32, TOK), lambda b, t: (0, 0)), vec, vec, vec,
                   BS((1, 2 * TOK), lambda b, t: (0, 0))],
        out_shape=[_sds((T, 2 * TOK), BF16), _sds((32, TOK), F32), _sds((1, TOK), F32),
                   _sds((1, TOK), F32), _sds((1, TOK), F32), _sds((1, 2 * TOK), F32)],
        scratch_shapes=[pltpu.VMEM((HALO + TR + NEXT, TOK), F32), pltpu.VMEM((ext, TOK), F32),
                        pltpu.VMEM((7, HALO + TR + NEXT, TOK), F32), pltpu.VMEM((7, ext, TOK), F32)],
        compiler_params=_cp(("arbitrary", "arbitrary"), vmem_mb=56), name="conv_bwd",
    )(z, z, z, dcat, dcat, cw, cb, lg, lb)


def _ffn_up(h2, wgu, l, carry=None):
    ni = len(carry.ins) if carry else 0
    no = len(carry.out_shapes) if carry else 0

    def body(h_ref, wg_ref, wu_ref, *rest):
        cin = rest[:ni]
        g_ref, u_ref, a_ref = rest[ni:ni + 3]
        cout, cscr = rest[ni + 3:ni + 3 + no], rest[ni + 3 + no:]
        if carry:
            @pl.when((pl.program_id(0) == 0) & (pl.program_id(1) == 0))
            def _():
                carry.start(cin, cout, cscr)

        hv = h_ref[...]
        g = lax.dot_general(hv, wg_ref[...], _DIMS["nt"], preferred_element_type=F32)
        u = lax.dot_general(hv, wu_ref[...], _DIMS["nt"], preferred_element_type=F32)
        sg = jax.nn.sigmoid(g)
        silu = g * sg
        g_ref[...] = (u * (sg * (1.0 + g * (1.0 - sg)))).astype(BF16)
        u_ref[...] = silu.astype(BF16)
        a_ref[...] = (silu * u).astype(BF16)

        if carry:
            @pl.when((pl.program_id(0) == FF // FT - 1) & (pl.program_id(1) == T // TR - 1))
            def _():
                carry.finish(cin, cout, cscr)

    out = BS((TR, FT), lambda q, i: (i, q))
    outs = pl.pallas_call(
        body, grid=(FF // FT, T // TR),
        in_specs=[BS((TR, D), lambda q, i: (i, 0)),
                  BS((None, FT, D), lambda q, i: (0, q, 0)),
                  BS((None, FT, D), lambda q, i: (1, q, 0))] + (carry.in_specs if carry else []),
        out_specs=[out, out, out] + [ANY] * no,
        out_shape=[_sds((T, FF), BF16), _sds((T, FF), BF16), _sds((T, FF), BF16)]
        + (carry.out_shapes if carry else []),
        scratch_shapes=carry.scratch if carry else [],
        compiler_params=pltpu.CompilerParams(
            dimension_semantics=("arbitrary", "arbitrary"), vmem_limit_bytes=48 << 20,
            has_side_effects=bool(carry)), name=f"ffn_up_{l}",
    )(h2, wgu, wgu, *(carry.ins if carry else []))
    return outs[:3], outs[3:]


def _ffn_down_bwd(dx, wd, g, u, l):
    def epilogue(dact, ex, o_ref, i):
        o_ref[0] = (dact * ex[0][...].astype(F32)).astype(BF16)
        o_ref[1] = (dact * ex[1][...].astype(F32)).astype(BF16)

    ex_spec = BS((TR, FT), lambda i, q, k: (i, q))
    return _mm("nt", dx, wd, grid=(T // TR, FF // FT, 1),
               a_spec=BS((TR, D), lambda i, q, k: (i, 0)),
               b_spec=BS((FT, D), lambda i, q, k: (q, 0)),
               out_shape=_sds((2, T, FF), BF16),
               out_spec=BS((2, TR, FT), lambda i, q, k: (0, i, q)),
               acc_shape=(TR, FT), extras=(g, u), extra_specs=(ex_spec, ex_spec),
               epilogue=epilogue, name=f"ffn_down_bwd_{l}")


def _row_tile(rows, cols, itemsize=4, limit=2 << 20):
    tr = rows
    while tr * cols * itemsize > limit and tr % 2 == 0 and (tr // 2) % 16 == 0:
        tr //= 2
    return tr


def _cast_bf16(arrs, name):
    n = len(arrs)
    rows, cols = arrs[0].shape
    tr = _row_tile(rows, cols)

    def body(*refs):
        o_ref = refs[n]
        k = pl.program_id(0)
        val = refs[0][...]
        for j in range(1, n):
            val = jnp.where(k == j, refs[j][...], val)
        o_ref[...] = val.astype(BF16)

    return pl.pallas_call(
        body, grid=(n, rows // tr),
        in_specs=[BS((tr, cols), lambda k, i: (i, 0))] * n,
        out_specs=BS((None, tr, cols), lambda k, i: (k, i, 0)),
        out_shape=_sds((n, rows, cols), BF16),
        compiler_params=_cp(("arbitrary", "arbitrary")), name=name,
    )(*arrs)


def _quad_sum(own, got, name):
    n, rows, cols = own.shape
    tr = _row_tile(rows, cols)

    def body(a_ref, q_ref, o_ref):
        o_ref[...] = ((a_ref[...] + q_ref[0].astype(F32)) + q_ref[1].astype(F32)) + q_ref[2].astype(F32)

    spec = BS((None, tr, cols), lambda k, i: (k, i, 0))
    return pl.pallas_call(
        body, grid=(n, rows // tr),
        in_specs=[spec, BS((3, None, tr, cols), lambda k, i: (0, k, i, 0))], out_specs=spec,
        out_shape=_sds((n, rows, cols), F32),
        compiler_params=_cp(("arbitrary", "arbitrary")), name=name,
    )(own, got)


def _adam_math(w, g, m, v):
    m = ADAM_B1 * m + (1.0 - ADAM_B1) * g
    v = ADAM_B2 * v + (1.0 - ADAM_B2) * (g * g)
    m_hat = m / (1.0 - ADAM_B1 ** ADAM_STEP)
    v_hat = v / (1.0 - ADAM_B2 ** ADAM_STEP)
    delta = -ADAM_LR * (m_hat / (jnp.sqrt(v_hat) + ADAM_EPS) + ADAM_WD * w)
    return delta, m, v


def _adamw_big(w, g, m, v, name):
    shape = w.shape
    cols = shape[-1]
    rows = w.size // cols
    tr = _row_tile(rows, cols, limit=1 << 20)

    def body(w_ref, g_ref, m_ref, v_ref, d_ref, nm_ref, nv_ref):
        d, nm, nv = _adam_math(w_ref[...], g_ref[...], m_ref[...], v_ref[...])
        d_ref[...] = d
        nm_ref[...] = nm
        nv_ref[...] = nv

    spec = BS((tr, cols), lambda i: (i, 0))
    outs = pl.pallas_call(
        body, grid=(rows // tr,), in_specs=[spec] * 4, out_specs=[spec] * 3,
        out_shape=[_sds((rows, cols), F32)] * 3,
        compiler_params=_cp(("arbitrary",)), name=name,
    )(*[a.reshape(rows, cols) for a in (w, g, m, v)])
    return [o.reshape(shape) for o in outs]


def _adamw_small(ws, gs, ms, vs):
    n = len(ws)

    def body(*refs):
        for i in range(n):
            d, nm, nv = _adam_math(refs[i][...], refs[n + i][...], refs[2 * n + i][...],
                                   refs[3 * n + i][...])
            refs[4 * n + i][...] = d
            refs[5 * n + i][...] = nm
            refs[6 * n + i][...] = nv

    specs = [BS(w.shape, lambda i: (0, 0)) for w in ws]
    outs = pl.pallas_call(
        body, grid=(1,), in_specs=specs * 4, out_specs=specs * 3,
        out_shape=[_sds(w.shape, F32) for w in ws] * 3,
        compiler_params=_cp(("arbitrary",)), name="adamw_small",
    )(*ws, *gs, *ms, *vs)
    return outs[:n], outs[n:2 * n], outs[2 * n:]


def _place():
    x, y, c = lax.axis_index("x"), lax.axis_index("y"), lax.axis_index("c")
    chips = [(1 - x, y), (x, 1 - y), (1 - x, 1 - y)]
    return x, y, c, chips


class _Exchange:
    def __init__(self, ins, in_specs, out_shapes, scratch, start, finish):
        self.ins, self.in_specs, self.out_shapes, self.scratch = ins, in_specs, out_shapes, scratch
        self.start, self.finish = start, finish


def _run_exchange(ex, name, vmem_mb=40):
    ni, no = len(ex.ins), len(ex.out_shapes)

    def body(*refs):
        ex.start(refs[:ni], refs[ni:ni + no], refs[ni + no:])
        ex.finish(refs[:ni], refs[ni:ni + no], refs[ni + no:])

    return pl.pallas_call(
        body, in_specs=ex.in_specs, out_specs=[ANY] * no, out_shape=ex.out_shapes,
        scratch_shapes=ex.scratch,
        compiler_params=pltpu.CompilerParams(has_side_effects=True, vmem_limit_bytes=vmem_mb << 20),
        name=name,
    )(*ex.ins)


def _gather_exchange(srcs, dst_shapes, views, small=None):
    nu = len(srcs)
    nd = len(dst_shapes)
    ns = 1 if small is not None else 0

    def unpack(ins, outs, scr):
        x, y, c, chips = _place()
        src = ins[:nu]
        vw = [views[u](outs[:nd]) for u in range(nu)]
        vbuf = scr[:nu]
        send, recv, fsend, frecv, lsem, ssend, srecv, vsem = scr[nu:]

        def ici(u, j, shard, to):
            return pltpu.make_async_remote_copy(
                src_ref=vbuf[u].at[:, c], dst_ref=vw[u].at[:, shard, c],
                send_sem=send.at[3 * u + j], recv_sem=recv.at[3 * u + j],
                device_id=to, device_id_type=MESH)

        def fwd(u, j, shard, half):
            return pltpu.make_async_remote_copy(
                src_ref=vw[u].at[:, shard, half], dst_ref=vw[u].at[:, shard, half],
                send_sem=fsend.at[3 * u + j], recv_sem=frecv.at[3 * u + j],
                device_id=(x, y, 1 - c), device_id_type=MESH)

        def small_copy(j, shard, to):
            return pltpu.make_async_remote_copy(
                src_ref=ins[nu], dst_ref=outs[nd].at[shard],
                send_sem=ssend.at[j], recv_sem=srecv.at[j], device_id=to, device_id_type=MESH)

        stage = [pltpu.make_async_copy(src[u], vbuf[u], vsem.at[u]) for u in range(nu)]
        local = [pltpu.make_async_copy(vbuf[u], vw[u].at[:, 2 * x + y], lsem.at[u]) for u in range(nu)]
        if ns:
            local.append(pltpu.make_async_copy(ins[nu], outs[nd].at[2 * x + y], lsem.at[nu]))
        return x, y, c, chips, ici, fwd, small_copy, stage, local

    def start(ins, outs, scr):
        x, y, c, chips, ici, fwd, small_copy, stage, local = unpack(ins, outs, scr)
        s = 2 * x + y
        for cp in stage:
            cp.start()
        if ns:
            local[nu].start()
            for j, chip in enumerate(chips):
                small_copy(j, s, (*chip, c)).start()
        for u in range(nu):
            stage[u].wait()
            for j, chip in enumerate(chips):
                ici(u, j, s, (*chip, c)).start()
            local[u].start()

    def finish(ins, outs, scr):
        x, y, c, chips, ici, fwd, small_copy, stage, local = unpack(ins, outs, scr)
        s = 2 * x + y
        for u in range(nu):
            for j, chip in enumerate(chips):
                sj = 2 * chip[0] + chip[1]
                ici(u, j, sj, (x, y, c)).wait_recv()
                fwd(u, j, sj, c).start()
        for u in range(nu):
            for j, chip in enumerate(chips):
                sj = 2 * chip[0] + chip[1]
                fwd(u, j, sj, 1 - c).wait_recv()
        for u in range(nu):
            for j, chip in enumerate(chips):
                ici(u, j, s, (*chip, c)).wait_send()
                fwd(u, j, s, c).wait_send()
        if ns:
            for j, chip in enumerate(chips):
                small_copy(j, 2 * chip[0] + chip[1], (x, y, c)).wait_recv()
                small_copy(j, s, (*chip, c)).wait_send()
        for cp in local:
            cp.wait()

    dma = pltpu.SemaphoreType.DMA
    return _Exchange(
        ins=list(srcs) + ([small] if ns else []),
        in_specs=[ANY] * nu + [BS(memory_space=pltpu.VMEM)] * ns,
        out_shapes=[_sds(sh, BF16) for sh in dst_shapes]
        + ([_sds((NSH,) + small.shape, F32)] if ns else []),
        scratch=[pltpu.VMEM(a.shape, BF16) for a in srcs]
        + [dma((3 * nu,)), dma((3 * nu,)), dma((3 * nu,)), dma((3 * nu,)),
           dma((nu + 1,)), dma((3,)), dma((3,)), dma((nu,))],
        start=start, finish=finish)


def _pair_reduce(gs, name):
    nu = len(gs)
    ns = [g.shape[0] * NSH for g in gs]
    base = [sum(ns[:u]) for u in range(nu)]

    def body(*refs):
        g_refs, own_refs, sb_refs = refs[:nu], refs[nu:2 * nu], refs[2 * nu:3 * nu]
        bufs = refs[3 * nu:8 * nu]
        send, recv, lsem, osem = refs[8 * nu:]
        x, y, c, _ = _place()
        s = 2 * x + y

        def unit(u):
            sendb, recvb, stage, outf, outb = bufs[5 * u:5 * u + 5]

            def load(k, half):
                return pltpu.make_async_copy(g_refs[u].at[k // NSH, k % NSH, half], stage.at[k % 2],
                                             lsem.at[2 * u + k % 2])

            def push(k):
                return pltpu.make_async_remote_copy(
                    src_ref=sendb.at[k], dst_ref=recvb.at[k], send_sem=send.at[base[u] + k],
                    recv_sem=recv.at[base[u] + k], device_id=(x, y, 1 - c), device_id_type=MESH)

            def store(k):
                return pltpu.make_async_copy(outb.at[k % 2], sb_refs[u].at[k // NSH, k % NSH],
                                             osem.at[3 * u + k % 2])

            return sendb, recvb, stage, outf, outb, load, push, store

        for u in range(nu):
            sendb, recvb, stage, outf, outb, load, push, store = unit(u)
            load(0, 1 - c).start()
            for k in range(ns[u]):
                if k + 1 < ns[u]:
                    load(k + 1, 1 - c).start()
                load(k, 1 - c).wait()
                sendb[k] = stage[k % 2].astype(BF16)
                push(k).start()
        for u in range(nu):
            sendb, recvb, stage, outf, outb, load, push, store = unit(u)
            n = ns[u]
            load(0, c).start()
            for k in range(n):
                if k + 1 < n:
                    load(k + 1, c).start()
                load(k, c).wait()
                push(k).wait_recv()
                total = stage[k % 2] + recvb[k].astype(F32)
                if k >= 2:
                    store(k - 2).wait()
                outb[k % 2] = total.astype(BF16)
                store(k).start()

                @pl.when(s == k % NSH)
                def _():
                    outf[...] = total
                    keep = pltpu.make_async_copy(outf, own_refs[u].at[k // NSH], osem.at[3 * u + 2])
                    keep.start()
                    keep.wait()

            for k in range(max(n - 2, 0), n):
                store(k).wait()
        for u in range(nu):
            push = unit(u)[6]
            for k in range(ns[u]):
                push(k).wait_send()

    dma = pltpu.SemaphoreType.DMA
    scratch = []
    for g, n in zip(gs, ns):
        rh, cc = g.shape[3], g.shape[4]
        scratch += [pltpu.VMEM((n, rh, cc), BF16), pltpu.VMEM((n, rh, cc), BF16),
                    pltpu.VMEM((2, rh, cc), F32), pltpu.VMEM((rh, cc), F32), pltpu.VMEM((2, rh, cc), BF16)]
    outs = pl.pallas_call(
        body, in_specs=[ANY] * nu, out_specs=[ANY] * (2 * nu),
        out_shape=[_sds((g.shape[0], g.shape[3], g.shape[4]), F32) for g in gs]
        + [_sds((g.shape[0], NSH, g.shape[3], g.shape[4]), BF16) for g in gs],
        scratch_shapes=scratch + [dma((sum(ns),)), dma((sum(ns),)), dma((2 * nu,)), dma((3 * nu,))],
        compiler_params=pltpu.CompilerParams(has_side_effects=True, vmem_limit_bytes=56 << 20),
        name=name,
    )(*gs)
    return list(outs[:nu]), list(outs[nu:])


def _chip_exchange(sums_bf16):
    nu = len(sums_bf16)

    def pushes(ins, outs, scr):
        x, y, c, chips = _place()
        send, recv = scr
        return [pltpu.make_async_remote_copy(
            src_ref=ins[u].at[:, 2 * chip[0] + chip[1]], dst_ref=outs[u].at[j],
            send_sem=send.at[3 * u + j], recv_sem=recv.at[3 * u + j],
            device_id=(*chip, c), device_id_type=MESH)
            for u in range(nu) for j, chip in enumerate(chips)]

    def start(ins, outs, scr):
        for cp in pushes(ins, outs, scr):
            cp.start()

    def finish(ins, outs, scr):
        for cp in pushes(ins, outs, scr):
            cp.wait()

    dma = pltpu.SemaphoreType.DMA
    shapes = [(3, a.shape[0], a.shape[2], a.shape[3]) for a in sums_bf16]
    return _Exchange(ins=list(sums_bf16), in_specs=[ANY] * nu,
                     out_shapes=[_sds(sh, BF16) for sh in shapes],
                     scratch=[dma((3 * nu,)), dma((3 * nu,))], start=start, finish=finish)


def _final_exchange(halves, out_shapes, targets):
    nu = len(halves)
    no = len(out_shapes)
    ncp = sum(len(t) for t in targets)

    def body(*refs):
        hv = refs[:nu]
        out = refs[nu:nu + no]
        sbuf = refs[nu + no:2 * nu + no]
        rbuf = refs[2 * nu + no:3 * nu + no]
        send, recv, lsem, osem, csem = refs[3 * nu + no:]
        x, y, c, _ = _place()
        stage = [pltpu.make_async_copy(hv[u], sbuf[u], lsem.at[u]) for u in range(nu)]
        push = [pltpu.make_async_remote_copy(
            src_ref=sbuf[u], dst_ref=rbuf[u], send_sem=send.at[u], recv_sem=recv.at[u],
            device_id=(x, y, 1 - c), device_id_type=MESH) for u in range(nu)]
        mine, theirs = [], []
        k = 0
        for u in range(nu):
            rh = hv[u].shape[1]
            for (p, oi, li) in targets[u]:
                mine.append((u, pltpu.make_async_copy(
                    sbuf[u].at[p], out[oi].at[li, pl.ds(c * rh, rh), :], csem.at[k])))
                theirs.append((u, pltpu.make_async_copy(
                    rbuf[u].at[p], out[oi].at[li, pl.ds((1 - c) * rh, rh), :], osem.at[k])))
                k += 1
        for cp in stage:
            cp.start()
        for u in range(nu):
            stage[u].wait()
            push[u].start()
            for (v, cp) in mine:
                if v == u:
                    cp.start()
        for u in range(nu):
            push[u].wait_recv()
            for (v, cp) in theirs:
                if v == u:
                    cp.start()
        for (_, cp) in theirs + mine:
            cp.wait()
        for u in range(nu):
            push[u].wait_send()

    dma = pltpu.SemaphoreType.DMA
    bufs = [pltpu.VMEM(h.shape, F32) for h in halves]
    return pl.pallas_call(
        body, in_specs=[ANY] * nu, out_specs=[ANY] * no,
        out_shape=[_sds(sh, F32) for sh in out_shapes],
        scratch_shapes=bufs + bufs + [dma((nu,)), dma((nu,)), dma((nu,)), dma((ncp,)), dma((ncp,))],
        compiler_params=pltpu.CompilerParams(has_side_effects=True, vmem_limit_bytes=56 << 20),
        name="final_exchange",
    )(*halves)


def _small_allreduce(pack):
    rows = pack.shape[0]

    def body(p_ref, o_ref, buf, send, recv):
        x, y, c, _ = _place()
        me = 4 * x + 2 * y + c
        buf[me] = p_ref[...]
        k = 0
        copies = []
        for dx in range(2):
            for dy in range(2):
                for dc in range(2):
                    if dx == 0 and dy == 0 and dc == 0:
                        continue
                    to = (jnp.where(dx, 1 - x, x), jnp.where(dy, 1 - y, y), jnp.where(dc, 1 - c, c))
                    src_slot = 4 * to[0] + 2 * to[1] + to[2]
                    copies.append((pltpu.make_async_remote_copy(
                        src_ref=p_ref, dst_ref=buf.at[me], send_sem=send.at[k], recv_sem=recv.at[k],
                        device_id=to, device_id_type=MESH), src_slot, k))
                    k += 1
        for cp, _, _ in copies:
            cp.start()
        for cp, src_slot, k in copies:
            pltpu.make_async_remote_copy(
                src_ref=p_ref, dst_ref=buf.at[src_slot], send_sem=send.at[k], recv_sem=recv.at[k],
                device_id=(x, y, c), device_id_type=MESH).wait()
        acc = buf[0]
        for d in range(1, 8):
            acc = acc + buf[d]
        o_ref[...] = acc

    dma = pltpu.SemaphoreType.DMA
    vm = BS(memory_space=pltpu.VMEM)
    return pl.pallas_call(
        body, in_specs=[vm], out_specs=vm, out_shape=_sds((rows, D), F32),
        scratch_shapes=[pltpu.VMEM((8, rows, D), F32), dma((7,)), dma((7,))],
        compiler_params=pltpu.CompilerParams(has_side_effects=True, vmem_limit_bytes=32 << 20),
        name="small_allreduce",
    )(pack)


def _in_proj(h, w, bias, name, transposed=False, carry=None):
    n = w.shape[0] if transposed else w.shape[1]
    tn = 640 if n == NA else 896
    ep = None
    extras, especs = (), ()
    if bias is not None:
        def ep(acc, ex, o_ref, i):
            o_ref[...] = acc + ex[0][...]
        extras = (bias,)
        especs = (BS((1, tn), lambda i, j, k: (0, j)),)
    b_spec = BS((tn, D), lambda i, j, k: (j, 0)) if transposed else BS((D, tn), lambda i, j, k: (0, j))
    return _mm("nt" if transposed else "nn", h, w, grid=(T // TR, n // tn, 1),
               a_spec=BS((TR, D), lambda i, j, k: (i, 0)), b_spec=b_spec,
               out_shape=_sds((T, n), F32), out_spec=BS((TR, tn), lambda i, j, k: (i, j)),
               acc_shape=(TR, tn), extras=extras, extra_specs=especs, epilogue=ep, name=name, carry=carry)


def _res_rms_epilogue(acc, ex, outs, i):
    y = acc + ex[0][...]
    outs[0][...] = y
    r = lax.rsqrt(jnp.mean(y * y, axis=-1, keepdims=True) + EPS)
    outs[1][...] = (y * r * ex[1][...]).astype(BF16)


def _res_loss_epilogue(acc, ex, outs, i):
    e = acc + ex[0][...] - ex[1][...]
    outs[1][...] = e * (1.0 / D)

    @pl.when(i == 0)
    def _():
        outs[0][...] = jnp.zeros_like(outs[0])

    outs[0][...] += 0.5 * jnp.sum(jnp.mean(e * e, axis=-1, keepdims=True), axis=0, keepdims=True)


def _rms_bwd_epilogue(dh, ex, outs, i):
    xv = ex[0][...]
    r = lax.rsqrt(jnp.mean(xv * xv, axis=-1, keepdims=True) + EPS)
    xh = xv * r
    gy = dh * ex[1][...]
    outs[0][...] = r * (gy - xh * jnp.mean(gy * xh, axis=-1, keepdims=True)) + ex[2][...]

    @pl.when(i == 0)
    def _():
        outs[1][...] = jnp.zeros_like(outs[1])

    outs[1][...] += jnp.sum(dh * xh, axis=0, keepdims=True)


def _local_step(x, mem, target, w, p, carries=None, bwd_carry_fn=None, late_carry_fn=None):
    row = lambda i, j, k: (i, 0)
    whole = lambda i, j, k: (0, 0)
    w = {k: (list(v) if isinstance(v, list) else v) for k, v in w.items()}
    carries = carries or {}

    def carry_of(name):
        return carries[name][0] if name in carries else None

    def delivered(name, outs):
        if name in carries:
            carries[name][1](w, outs)

    saved = []
    bias = _bias_expand(p["rel_u"])
    vec = BS((1, D), whole)
    h = _rms_fwd(x, p["norm1_g"][0:1], "rms1_0")
    for l in range(2):
        type_a = l == 0
        memn = _rms_fwd(mem, p["mem_norm_g"][l:l + 1], f"rmsmem_{l}")
        if type_a:
            z = _in_proj(h, w["a"], None, "inproj_a", carry=carry_of("inproj_a"))
            if carry_of("inproj_a") is not None:
                z, carried = z
                delivered("inproj_a", carried)
            cat, carried = _attn_fwd(z, bias, p["a_q_g2"], p["a_k_g2"], carry_of("attn_fwd"))
            delivered("attn_fwd", carried)
            qcol = NA // MEMW - 1
        else:
            z = _in_proj(h, w["b"], p["b_b_in"], "inproj_b", transposed=True)
            cat = _conv_fwd(z, p["conv_w"], p["conv_b"], p["ln_g"], p["ln_b"])
            qcol = NBW // MEMW - 1
        kv = _mm("nn", memn, w["kv"][l], grid=(1, 1, 1),
                 a_spec=BS((NB * MEMT, D), whole), b_spec=BS((D, 2 * MEMW), whole),
                 out_shape=_sds((NB * MEMT, 2 * MEMW), F32), out_spec=BS((NB * MEMT, 2 * MEMW), whole),
                 acc_shape=(8, 128), name=f"memkv_{l}")
        cat = _memattn_fwd(z, kv, cat, p["mq_g4"][l:l + 1], p["mk_g4"][l:l + 1], qcol, f"memattn_fwd_{l}")
        x1, h2 = _mm("nn", cat, w["wo"][l], grid=(T // TR, 1, 1), a_spec=BS((TR, D), row),
                     b_spec=BS((D, D), whole),
                     out_shape=[_sds((T, D), F32), _sds((T, D), BF16)],
                     out_spec=[BS((TR, D), row), BS((TR, D), row)], acc_shape=(8, 128),
                     extras=(x, p["norm2_g"][l:l + 1]), extra_specs=(BS((TR, D), row), vec),
                     epilogue=_res_rms_epilogue, name=f"outproj_{l}")
        (g, u, act), carried = _ffn_up(h2, w["gu"][l], l, carry_of(f"ffn_up_{l}"))
        delivered(f"ffn_up_{l}", carried)
        last = l == 1
        res = _mm("nn", act, w["wd"][l], grid=(T // TR, 1, 1),
                  a_spec=BS((TR, FF), row), b_spec=BS((FF, D), whole),
                  out_shape=[_sds((1, 1), F32), _sds((T, D), F32)] if last else
                  [_sds((T, D), F32), _sds((T, D), BF16)],
                  out_spec=[BS((1, 1), whole), BS((TR, D), row)] if last else
                  [BS((TR, D), row), BS((TR, D), row)],
                  acc_shape=(8, 128), extras=(x1, target if last else p["norm1_g"][1:2]),
                  extra_specs=(BS((TR, D), row), BS((TR, D), row) if last else vec),
                  epilogue=_res_loss_epilogue if last else _res_rms_epilogue, sequential=last,
                  name=f"ffn_down_{l}", carry=carry_of(f"ffn_down_{l}"))
        if carry_of(f"ffn_down_{l}") is not None:
            res, carried = res
            delivered(f"ffn_down_{l}", carried)
        saved.append(dict(x=x, h=h, memn=memn, kv=kv, z=z, cat=cat, x1=x1, h2=h2, g=g, u=u, act=act,
                          qcol=qcol))
        if last:
            loss, dx = res
        else:
            x, h = res

    big = dict(a=None, b=None, kv=[None, None], wo=[None, None], gu=[None, None], wd=[None, None])
    small = {}
    bwd_carried, late_carried = (), ()
    tk = T // 2
    nkt = T // tk
    for l in (1, 0):
        sv = saved[l]
        dgu = _ffn_down_bwd(dx, w["wd"][l], sv["g"], sv["u"], l)
        big["wd"][l] = _mm("tn", sv["act"], dx, grid=(FF // FT, 1, 2 * nkt),
                           a_spec=BS((tk // 2, FT), lambda i, j, k: (k, i)),
                           b_spec=BS((tk // 2, D), lambda i, j, k: (k, 0)),
                           out_shape=_sds((FF, D), F32), out_spec=BS((FT, D), lambda i, j, k: (i, 0)),
                           acc_shape=(FT, D), name=f"dw_down_{l}")
        dx1, small[f"norm2_g{l}"] = _mm(
            "nn", dgu, w["gu"][l], grid=(T // TR, 1, 2),
            a_spec=BS((None, TR, FF), lambda i, j, k: (k, i, 0)),
            b_spec=BS((None, FF, D), lambda i, j, k: (k, 0, 0)),
            out_shape=[_sds((T, D), F32), _sds((1, D), F32)], out_spec=[BS((TR, D), row), vec],
            acc_shape=(TR, D), extras=(sv["x1"], p["norm2_g"][l:l + 1], dx),
            extra_specs=(BS((TR, D), row), vec, BS((TR, D), row)),
            epilogue=_rms_bwd_epilogue, sequential=True, name=f"dh2_{l}")
        big["gu"][l] = _mm("tn", dgu, sv["h2"], grid=(2 * FF // FT, 1, nkt),
                           a_spec=BS((None, tk, FT), lambda i, j, k: (i // 2, k, i % 2)),
                           b_spec=BS((tk, D), lambda i, j, k: (k, 0)),
                           out_shape=_sds((2, FF, D), F32),
                           out_spec=BS((None, FT, D), lambda i, j, k: (i // 2, i % 2, 0)),
                           acc_shape=(FT, D), vmem_mb=58, name=f"dw_gu_{l}")
        dcat = _mm("nt", dx1, w["wo"][l], grid=(T // TR, 1, 1), a_spec=BS((TR, D), row),
                   b_spec=BS((D, D), whole),
                   out_shape=_sds((T, D), F32), out_spec=BS((TR, D), row), acc_shape=(8, 128),
                   name=f"dcat_{l}")
        big["wo"][l] = _mm("tn", sv["cat"], dx1, grid=(1, 1, nkt),
                           a_spec=BS((tk, D), lambda i, j, k: (k, 0)), b_spec=BS((tk, D), lambda i, j, k: (k, 0)),
                           out_shape=_sds((D, D), F32), out_spec=BS((D, D), whole),
                           acc_shape=(D, D), name=f"dw_out_{l}")
        dqm, dkv, small[f"mq_g{l}"], small[f"mk_g{l}"] = _memattn_bwd(
            sv["z"], sv["kv"], dcat, p["mq_g4"][l:l + 1], p["mk_g4"][l:l + 1], sv["qcol"], f"memattn_bwd_{l}")
        big["kv"][l] = _mm("tn", sv["memn"], dkv, grid=(1, 1, 1),
                           a_spec=BS((NB * MEMT, D), whole), b_spec=BS((NB * MEMT, 2 * MEMW), whole),
                           out_shape=_sds((D, 2 * MEMW), F32), out_spec=BS((D, 2 * MEMW), whole),
                           acc_shape=(8, 128), name=f"dw_kv_{l}")
        dmemn = _mm("nt", dkv, w["kv"][l], grid=(1, 1, 1),
                    a_spec=BS((NB * MEMT, 2 * MEMW), whole), b_spec=BS((D, 2 * MEMW), whole),
                    out_shape=_sds((NB * MEMT, D), F32), out_spec=BS((NB * MEMT, D), whole),
                    acc_shape=(8, 128), name=f"dmemn_{l}")
        _, small[f"mem_norm_g{l}"] = _rms_bwd(dmemn, mem, p["mem_norm_g"][l:l + 1], None, f"rmsmem_bwd_{l}")
        if l == 0:
            carry = bwd_carry_fn(big) if bwd_carry_fn is not None else None
            (dq, dk, dv, dbias, small["a_q_g"], small["a_k_g"]), bwd_carried = _attn_bwd(
                sv["z"], dcat, bias, p["a_q_g2"], p["a_k_g2"], carry)
            small["rel_u"] = _bias_reduce(dbias)
            dz = jnp.concatenate([dq, dk, dv, dqm], axis=1)
            w_in, key, n, tn = w["a"], "a", NA, 640
        else:
            du, small["conv_w"], small["conv_b"], small["ln_g"], small["ln_b"], dbin_u = _conv_bwd(
                sv["z"], dcat, p["conv_w"], p["conv_b"], p["ln_g"], p["ln_b"])
            dz = jnp.concatenate([du, dqm], axis=1)
            small["b_in_u"] = dbin_u
            w_in, key, n, tn = w["b"], "b", NBW, 896
        norm_bwd = dict(out_shape=[_sds((T, D), F32), _sds((1, D), F32)], out_spec=[BS((TR, D), row), vec],
                        acc_shape=(8, 128), extras=(sv["x"], p["norm1_g"][l:l + 1], dx1),
                        extra_specs=(BS((TR, D), row), vec, BS((TR, D), row)),
                        epilogue=_rms_bwd_epilogue, sequential=True, name=f"dh_{l}")
        if l == 0:
            big[key] = _mm("tn", sv["h"], dz, grid=(1, n // tn, nkt),
                           a_spec=BS((tk, D), lambda i, j, k: (k, 0)), b_spec=BS((tk, tn), lambda i, j, k: (k, j)),
                           out_shape=_sds((D, n), F32), out_spec=BS((D, tn), lambda i, j, k: (0, j)),
                           acc_shape=(D, tn), name=f"dw_in_{l}")
            carry = late_carry_fn(big) if late_carry_fn is not None else None
            res = _mm("nt", dz, w_in, grid=(T // TR, 1, 1), a_spec=BS((TR, n), row),
                      b_spec=BS((D, n), whole), carry=carry, **norm_bwd)
            if carry is not None:
                res, late_carried = res
            dx, small[f"norm1_g{l}"] = res
        else:
            dx, small[f"norm1_g{l}"] = _mm("nn", dz, w_in, grid=(T // TR, 1, 1),
                                           a_spec=BS((TR, n), row), b_spec=BS((n, D), whole), **norm_bwd)
            big[key] = _mm("tn", dz, sv["h"], grid=(n // tn, 1, nkt),
                           a_spec=BS((tk, tn), lambda i, j, k: (k, i)), b_spec=BS((tk, D), lambda i, j, k: (k, 0)),
                           out_shape=_sds((n, D), F32), out_spec=BS((tn, D), lambda i, j, k: (i, 0)),
                           acc_shape=(tn, D), name=f"dw_in_{l}")
        if l == 1:
            small["b_in_qm"] = _colsum(dqm, "colsum_dqm")
    return loss, dx, big, small, bwd_carried, late_carried


def _colsum(a, name):
    rows, cols = a.shape

    def body(a_ref, o_ref):
        @pl.when(pl.program_id(0) == 0)
        def _():
            o_ref[...] = jnp.zeros_like(o_ref)

        o_ref[...] += jnp.sum(a_ref[...].astype(F32), axis=0, keepdims=True)

    return pl.pallas_call(
        body, grid=(rows // TR,), in_specs=[BS((TR, cols), lambda i: (i, 0))],
        out_specs=BS((1, cols), lambda i: (0, 0)), out_shape=_sds((1, cols), F32),
        compiler_params=_cp(("arbitrary",)), name=name,
    )(a)


_PACK_ROWS = 64


def _pad_to(a, rows, cols=D):
    return jnp.pad(a, ((0, rows - a.shape[0]), (0, cols - a.shape[1])))


def _pack_small(sm):
    parts = [
        jnp.concatenate([sm["norm1_g0"], sm["norm1_g1"]], 0),
        jnp.concatenate([sm["mem_norm_g0"], sm["mem_norm_g1"]], 0),
        jnp.concatenate([sm["norm2_g0"], sm["norm2_g1"]], 0),
        _pad_to(sm["a_q_g"], 1), _pad_to(sm["a_k_g"], 1),
        _pad_to(jnp.concatenate([sm["mq_g0"], sm["mq_g1"]], 0), 2),
        _pad_to(jnp.concatenate([sm["mk_g0"], sm["mk_g1"]], 0), 2),
        _pad_to(sm["conv_b"], 1), _pad_to(sm["ln_g"], 1), _pad_to(sm["ln_b"], 1),
        _pad_to(sm["b_in_u"][:, :D], 1), _pad_to(sm["b_in_u"][:, D:], 1),
        _pad_to(sm["b_in_qm"], 1),
        _pad_to(sm["conv_w"][:CONVW], CONVW),
        sm["rel_u"].reshape(12, D),
    ]
    pack = jnp.concatenate(parts, 0)
    return jnp.pad(pack, ((0, _PACK_ROWS - pack.shape[0]), (0, 0)))


def _rel_table_to_u(rel_bias):
    flat = jnp.concatenate([jnp.broadcast_to(rel_bias[:, 191:192], (12, 447)), rel_bias[:, ::-1]], axis=1)
    return jnp.pad(flat, ((0, 0), (192, 1024 - 192 - 639))).reshape(12, 1, 1024)


def _u_to_rel_table(du):
    flat = du[:, 192:192 + 639]
    g = flat[:, 447:][:, ::-1]
    return g, flat[:, :447]


def kernel(x, mem, norm1_g, mem_norm_g, a_w_in, a_q_g, a_k_g, a_rel_bias, b_w_in, b_b_in, b_conv_w, b_conv_b, b_ln_g, b_ln_b, mq_g, mk_g, w_mem_kv, w_out, norm2_g, w_gate, w_up, w_down, loss_target, m_norm1_g, m_mem_norm_g, m_a_w_in, m_a_q_g, m_a_k_g, m_a_rel_bias, m_b_w_in, m_b_b_in, m_b_conv_w, m_b_conv_b, m_b_ln_g, m_b_ln_b, m_mq_g, m_mk_g, m_w_mem_kv, m_w_out, m_norm2_g, m_w_gate, m_w_up, m_w_down, v_norm1_g, v_mem_norm_g, v_a_w_in, v_a_q_g, v_a_k_g, v_a_rel_bias, v_b_w_in, v_b_b_in, v_b_conv_w, v_b_conv_b, v_b_ln_g, v_b_ln_b, v_mq_g, v_mk_g, v_w_mem_kv, v_w_out, v_norm2_g, v_w_gate, v_w_up, v_w_down):
    sx = 2 * lax.axis_index("x") + lax.axis_index("y")

    n_in = (NA // NSH, NBW // NSH)
    tr = lambda a: jnp.swapaxes(a, -1, -2)
    small_src = jnp.concatenate([
        jnp.pad(b_b_in, ((0, 0), (0, 512 - 448))),
        jnp.pad(b_conv_w[0], ((0, 0), (0, 512 - 192))),
        jnp.pad(jnp.concatenate([b_conv_b, b_ln_g, b_ln_b], 0), ((0, 0), (0, 512 - 192))),
        jnp.zeros((5, 512), F32)], 0)

    def gather_groups(groups, small=None):
        def src_of(l, name):
            if name == "in":
                return [_cast_bf16([tr(b_w_in[0])], "cast_in_1").reshape(1, 2, n_in[1] // 2, D) if l else
                        _cast_bf16([a_w_in[0]], "cast_in_0").reshape(1, 2, D // 2, n_in[0])]
            if name == "gu":
                return [_cast_bf16([tr(w_gate[l])], f"cast_gate_{l}").reshape(1, 2, FS // 2, D),
                        _cast_bf16([tr(w_up[l])], f"cast_up_{l}").reshape(1, 2, FS // 2, D)]
            arr, shape = {"kv": (w_mem_kv, (1, 2, 128, 2 * MEMW)), "wo": (w_out, (1, 2, 128, D)),
                          "wd": (w_down, (1, 2, FS // 2, D))}[name]
            return [_cast_bf16([arr[l]], f"cast_{name}_{l}").reshape(shape)]

        def dst_of(l, name):
            if name == "in":
                return (1, NSH, 2, n_in[1] // 2, D) if l else (1, NSH, 2, D // 2, n_in[0])
            return {"kv": (1, NSH, 2, 128, 2 * MEMW), "wo": (1, NSH, 2, 128, D),
                    "gu": (1, 2, NSH, 2, FS // 2, D), "wd": (1, NSH, 2, FS // 2, D)}[name]

        items = [(l, name) for l, names in groups for name in names]
        srcs, views = [], []
        for k, (l, name) in enumerate(items):
            srcs += src_of(l, name)
            if name == "gu":
                views += [lambda d, k=k: d[k].at[:, 0], lambda d, k=k: d[k].at[:, 1]]
            else:
                views.append(lambda d, k=k: d[k])

        def done(w, outs):
            for k, (l, name) in enumerate(items):
                if name == "in" and l == 0:
                    w["a"] = outs[k].reshape(NSH, D, n_in[0]).transpose(1, 0, 2).reshape(D, NA)
                elif name == "in":
                    w["b"] = outs[k].reshape(NBW, D)
                else:
                    shape = {"kv": (D, 2 * MEMW), "wo": (D, D), "gu": (2, FF, D), "wd": (FF, D)}
                    w[name][l] = outs[k].reshape(shape[name])

        return _gather_exchange(srcs, [dst_of(l, name) for l, name in items], views, small), done

    w = dict(a=None, b=None, kv=[None, None], wo=[None, None], gu=[None, None], wd=[None, None])
    first, first_done = gather_groups([(0, ["in", "kv"])], small_src)
    outs0 = _run_exchange(first, "gather_weights_first")
    first_done(w, outs0)
    small_all = outs0[2]
    carries = {"inproj_a": gather_groups([(0, ["wo", "wd"])]),
               "attn_fwd": gather_groups([(0, ["gu"]), (1, ["in", "kv", "wo"])]),
               "ffn_up_0": gather_groups([(1, ["gu"])]),
               "ffn_down_0": gather_groups([(1, ["wd"])])}

    conv_w_full = small_all[:, 1:1 + CONVW, :192].transpose(1, 0, 2).reshape(CONVW, TOK)
    vec3 = small_all[:, 32:35, :192].transpose(1, 0, 2).reshape(3, TOK)
    p = dict(
        norm1_g=norm1_g, mem_norm_g=mem_norm_g, norm2_g=norm2_g,
        a_q_g2=jnp.tile(a_q_g, (1, AH)), a_k_g2=jnp.tile(a_k_g, (1, AH)),
        mq_g4=jnp.tile(mq_g, (1, 4)), mk_g4=jnp.tile(mk_g, (1, 4)),
        rel_u=_rel_table_to_u(a_rel_bias[0]),
        b_b_in=small_all[:, 0, :448].reshape(1, NBW),
        conv_w=jnp.pad(conv_w_full, ((0, 1), (0, 0))),
        conv_b=vec3[0:1], ln_g=vec3[1:2], ln_b=vec3[2:3])

    def pair_sums(items, big, name):
        units = []
        for l, tensor in items:
            if tensor == "in" and l == 0:
                g = big["a"].reshape(D, NSH, n_in[0]).transpose(1, 0, 2).reshape(1, NSH, 2, D // 2, n_in[0])
            elif tensor == "in":
                g = big["b"].reshape(1, NSH, 2, n_in[1] // 2, D)
            else:
                shape = {"kv": (1, NSH, 2, 128, 2 * MEMW), "wo": (1, NSH, 2, 128, D),
                         "gu": (2, NSH, 2, FS // 2, D), "wd": (1, NSH, 2, FS // 2, D)}
                g = big[tensor][l].reshape(shape[tensor])
            units.append(g)
        return _pair_reduce(units, name)

    early_groups = [[(1, "gu")], [(0, "gu")],
                    [(1, "in"), (1, "kv"), (1, "wo"), (1, "wd"), (0, "kv"), (0, "wo"), (0, "wd")]]
    early = [item for grp in early_groups for item in grp]
    late = [(0, "in")]
    own_early, own_late = [], []

    def bwd_carry_fn(big):
        sums_b = []
        for k, grp in enumerate(early_groups):
            own, sb = pair_sums(grp, big, f"pair_reduce_early_{k}")
            own_early.extend(own)
            sums_b.extend(sb)
        return _chip_exchange(sums_b)

    def late_carry_fn(big):
        own, sb = pair_sums(late, big, "pair_reduce_late")
        own_late.extend(own)
        return _chip_exchange(sb)

    loss, grad_x, big, small, parts_early, parts_late = _local_step(
        x.reshape(T, D), mem.reshape(NB * MEMT, D), loss_target.reshape(T, D), w, p,
        carries=carries, bwd_carry_fn=bwd_carry_fn, late_carry_fn=late_carry_fn)
    loss = lax.psum(loss[0, 0], ("x", "y", "c"))
    items = early + late
    halves = [_quad_sum(o, pt, f"quad_sum_{name}_{l}")
              for (l, name), o, pt in zip(items, own_early + own_late, list(parts_early) + list(parts_late))]
    out_shapes = [(1, D, NA // NSH), (1, NBW // NSH, D), (2, 2 * 128, 2 * MEMW), (2, 2 * 128, D),
                  (2, FS, D), (2, FS, D), (2, FS, D)]
    target_of = {"in": lambda l: [(0, l, 0)], "kv": lambda l: [(0, 2, l)], "wo": lambda l: [(0, 3, l)],
                 "gu": lambda l: [(0, 4, l), (1, 5, l)], "wd": lambda l: [(0, 6, l)]}
    targets = [target_of[name](l) for l, name in items]
    g_a, g_b, g_kv, g_wo, g_gate, g_up, g_wd = _final_exchange(halves, out_shapes, targets)

    tot = _small_allreduce(_pack_small(small))
    g_rel, clip_part = _u_to_rel_table(tot[49:61])
    g_rel = jnp.concatenate([g_rel[:, :191], g_rel[:, 191:] + _rowsum(clip_part)], axis=1)
    b_in_full = jnp.concatenate([tot[15:16], tot[16:17, :512], tot[17:18, :MEMW]], axis=1)
    g_small = dict(
        norm1_g=tot[0:2], mem_norm_g=tot[2:4], norm2_g=tot[4:6],
        a_q_g=tot[6:7, :HD], a_k_g=tot[7:8, :HD], a_rel_bias=g_rel[None],
        b_b_in=lax.dynamic_slice(b_in_full, (0, sx * 448), (1, 448)),
        b_conv_w=lax.dynamic_slice(tot[18:49, :TOK], (0, sx * 192), (CONVW, 192))[None],
        b_conv_b=lax.dynamic_slice(tot[12:13, :TOK], (0, sx * 192), (1, 192)),
        b_ln_g=lax.dynamic_slice(tot[13:14, :TOK], (0, sx * 192), (1, 192)),
        b_ln_b=lax.dynamic_slice(tot[14:15, :TOK], (0, sx * 192), (1, 192)),
        mq_g=tot[8:10, :HD], mk_g=tot[10:12, :HD])

    names = ["norm1_g", "mem_norm_g", "a_w_in", "a_q_g", "a_k_g", "a_rel_bias", "b_w_in", "b_b_in",
             "b_conv_w", "b_conv_b", "b_ln_g", "b_ln_b", "mq_g", "mk_g", "w_mem_kv", "w_out",
             "norm2_g", "w_gate", "w_up", "w_down"]
    weights = dict(zip(names, [norm1_g, mem_norm_g, a_w_in, a_q_g, a_k_g, a_rel_bias, b_w_in, b_b_in,
                               b_conv_w, b_conv_b, b_ln_g, b_ln_b, mq_g, mk_g, w_mem_kv, w_out,
                               norm2_g, w_gate, w_up, w_down]))
    ms = dict(zip(names, [m_norm1_g, m_mem_norm_g, m_a_w_in, m_a_q_g, m_a_k_g, m_a_rel_bias, m_b_w_in,
                          m_b_b_in, m_b_conv_w, m_b_conv_b, m_b_ln_g, m_b_ln_b, m_mq_g, m_mk_g,
                          m_w_mem_kv, m_w_out, m_norm2_g, m_w_gate, m_w_up, m_w_down]))
    vs = dict(zip(names, [v_norm1_g, v_mem_norm_g, v_a_w_in, v_a_q_g, v_a_k_g, v_a_rel_bias, v_b_w_in,
                          v_b_b_in, v_b_conv_w, v_b_conv_b, v_b_ln_g, v_b_ln_b, v_mq_g, v_mk_g,
                          v_w_mem_kv, v_w_out, v_norm2_g, v_w_gate, v_w_up, v_w_down]))
    grads = dict(g_small)
    grads.update(a_w_in=g_a, b_w_in=g_b, w_mem_kv=g_kv, w_out=g_wo, w_gate=g_gate, w_up=g_up, w_down=g_wd)
    big_names = ["a_w_in", "b_w_in", "w_mem_kv", "w_out", "w_gate", "w_up", "w_down"]
    small_names = [n for n in names if n not in big_names]
    delta, new_m, new_v = {}, {}, {}
    for n in big_names:
        if n in ("b_w_in", "w_gate", "w_up"):
            outs = _adamw_big(tr(weights[n]), grads[n], tr(ms[n]), tr(vs[n]), f"adamw_{n}")
            delta[n], new_m[n], new_v[n] = [tr(o) for o in outs]
            grads[n] = tr(grads[n])
        else:
            delta[n], new_m[n], new_v[n] = _adamw_big(weights[n], grads[n], ms[n], vs[n], f"adamw_{n}")
    as2d = lambda a: a.reshape(-1, a.shape[-1])
    d_s, m_s, v_s = _adamw_small([as2d(weights[n]) for n in small_names], [as2d(grads[n]) for n in small_names],
                                 [as2d(ms[n]) for n in small_names], [as2d(vs[n]) for n in small_names])
    for i, n in enumerate(small_names):
        delta[n] = d_s[i].reshape(weights[n].shape)
        new_m[n] = m_s[i].reshape(weights[n].shape)
        new_v[n] = v_s[i].reshape(weights[n].shape)

    return (loss, grad_x.reshape(NB, SEQ, D), *[grads[n] for n in names], *[delta[n] for n in names],
            *[new_m[n] for n in names], *[new_v[n] for n in names])


def _rowsum(a):
    def body(a_ref, o_ref):
        o_ref[...] = jnp.sum(a_ref[...], axis=1, keepdims=True)

    vm = BS(memory_space=pltpu.VMEM)
    return pl.pallas_call(body, in_specs=[vm], out_specs=vm, out_shape=_sds((a.shape[0], 1), F32),
                          compiler_params=_cp(), name="rowsum")(a)
```

```python
import functools

import jax
import jax.numpy as jnp
from jax import lax
from jax.experimental import pallas as pl
from jax.experimental.pallas import tpu as pltpu

F32 = jnp.float32
BF16 = jnp.bfloat16
BS = pl.BlockSpec
ANY = pl.BlockSpec(memory_space=pl.ANY)
MESH = pl.DeviceIdType.MESH

D = 1024
SEQ = 2048
NB = 2
T = NB * SEQ
MEMT = 256
HD = 64
TOK = 768
MEMW = 256
NA = 3 * TOK + MEMW
NBW = 2 * TOK + MEMW
FF = 2816
NSH = 4
FS = FF // NSH
FT = FF // 2
CONVW = 31
EPS = 1e-6
NEG = -1e30
SCALE = HD ** -0.5
QB = 256
KWIN = 768
KPAD = 512
TR = 512

ADAM_LR = 0.001
ADAM_B1 = 0.9
ADAM_B2 = 0.999
ADAM_EPS = 1e-08
ADAM_WD = 0.01
ADAM_STEP = 10

_DIMS = {
    "nn": (((1,), (0,)), ((), ())),
    "nt": (((1,), (1,)), ((), ())),
    "tn": (((0,), (0,)), ((), ())),
}


def _cp(sem=None, vmem_mb=48):
    return pltpu.CompilerParams(dimension_semantics=sem, vmem_limit_bytes=vmem_mb << 20)


def _sds(shape, dtype):
    return jax.ShapeDtypeStruct(tuple(shape), dtype)


def _mm(mode, a, b, *, grid, a_spec, b_spec, out_shape, out_spec, acc_shape, name,
        extras=(), extra_specs=(), epilogue=None, carry=None, vmem_mb=48, sequential=False):
    n_ex = len(extras)
    nk = grid[2]
    dims = _DIMS[mode]
    ni = len(carry.ins) if carry else 0
    no = len(carry.out_shapes) if carry else 0
    multi = isinstance(out_shape, (list, tuple))
    out_shapes = list(out_shape) if multi else [out_shape]
    out_specs = list(out_spec) if multi else [out_spec]
    n_o = len(out_shapes)

    def body(a_ref, b_ref, *rest):
        ex = rest[:n_ex]
        cin = rest[n_ex:n_ex + ni]
        o_refs = rest[n_ex + ni:n_ex + ni + n_o]
        o_ref = o_refs if multi else o_refs[0]
        cout = rest[n_ex + ni + n_o:n_ex + ni + n_o + no]
        acc = rest[n_ex + ni + n_o + no]
        cscr = rest[n_ex + ni + n_o + no + 1:]
        ids = [pl.program_id(d) for d in range(3)]
        k = ids[2]
        if carry:
            @pl.when((ids[0] == 0) & (ids[1] == 0) & (ids[2] == 0))
            def _():
                carry.start(cin, cout, cscr)

        prod = lax.dot_general(a_ref[...].astype(BF16), b_ref[...].astype(BF16), dims,
                               preferred_element_type=F32)

        def finish(val):
            if epilogue is None:
                o_ref[...] = val.astype(o_ref.dtype)
            else:
                epilogue(val, ex, o_ref, ids[0])

        if nk == 1:
            finish(prod)
        else:
            @pl.when(k == 0)
            def _():
                acc[...] = prod

            @pl.when((k > 0) & (k < nk - 1))
            def _():
                acc[...] += prod

            @pl.when(k == nk - 1)
            def _():
                finish(acc[...] + prod)

        if carry:
            @pl.when((ids[0] == grid[0] - 1) & (ids[1] == grid[1] - 1) & (ids[2] == grid[2] - 1))
            def _():
                carry.finish(cin, cout, cscr)

    acc_scratch = pltpu.VMEM(acc_shape if nk > 1 else (8, 128), F32)
    ordered = sequential or bool(carry)
    outs = pl.pallas_call(
        body, grid=grid,
        in_specs=[a_spec, b_spec, *extra_specs] + (carry.in_specs if carry else []),
        out_specs=out_specs + [ANY] * no, out_shape=out_shapes + (carry.out_shapes if carry else []),
        scratch_shapes=[acc_scratch] + (carry.scratch if carry else []),
        compiler_params=pltpu.CompilerParams(
            dimension_semantics=("arbitrary",) * 3 if ordered else ("parallel", "parallel", "arbitrary"),
            vmem_limit_bytes=vmem_mb << 20, has_side_effects=bool(carry)), name=name,
    )(a, b, *extras, *(carry.ins if carry else []))
    mine = list(outs[:n_o]) if multi else outs[0]
    return (mine, outs[n_o:]) if carry else mine


def _rms_fwd(x, g, name):
    rows = x.shape[0]

    def body(x_ref, g_ref, o_ref):
        xv = x_ref[...]
        r = lax.rsqrt(jnp.mean(xv * xv, axis=-1, keepdims=True) + EPS)
        o_ref[...] = (xv * r * g_ref[...]).astype(BF16)

    return pl.pallas_call(
        body, grid=(rows // TR,),
        in_specs=[BS((TR, D), lambda i: (i, 0)), BS((1, D), lambda i: (0, 0))],
        out_specs=BS((TR, D), lambda i: (i, 0)), out_shape=_sds((rows, D), BF16),
        compiler_params=_cp(("arbitrary",)), name=name,
    )(x, g)


def _rms_bwd(dh, x, g, dres, name):
    rows = x.shape[0]
    has_res = dres is not None

    def body(*refs):
        if has_res:
            dh_ref, x_ref, g_ref, r_ref, dx_ref, dg_ref = refs
        else:
            dh_ref, x_ref, g_ref, dx_ref, dg_ref = refs
        xv = x_ref[...]
        dhv = dh_ref[...]
        r = lax.rsqrt(jnp.mean(xv * xv, axis=-1, keepdims=True) + EPS)
        xh = xv * r
        gy = dhv * g_ref[...]
        dx = r * (gy - xh * jnp.mean(gy * xh, axis=-1, keepdims=True))
        if has_res:
            dx = dx + r_ref[...]
        dx_ref[...] = dx

        @pl.when(pl.program_id(0) == 0)
        def _():
            dg_ref[...] = jnp.zeros_like(dg_ref)

        dg_ref[...] += jnp.sum(dhv * xh, axis=0, keepdims=True)

    row = BS((TR, D), lambda i: (i, 0))
    vec = BS((1, D), lambda i: (0, 0))
    ins = [dh, x, g] + ([dres] if has_res else [])
    return pl.pallas_call(
        body, grid=(rows // TR,),
        in_specs=[row, row, vec] + ([row] if has_res else []),
        out_specs=[row, vec], out_shape=[_sds((rows, D), F32), _sds((1, D), F32)],
        compiler_params=_cp(("arbitrary",)), name=name,
    )(*ins)


def _group_masks(width):
    lane = lax.broadcasted_iota(jnp.int32, (1, width), 1)
    return [(lane >= HD * h) & (lane < HD * (h + 1)) for h in range(width // HD)]


def _group_mean(v, masks):
    out = jnp.zeros_like(v)
    for m in masks:
        s = jnp.sum(jnp.where(m, v, 0.0), axis=-1, keepdims=True) * (1.0 / HD)
        out = jnp.where(m, s, out)
    return out


def _head_norm(zv, g, masks):
    r = lax.rsqrt(_group_mean(zv * zv, masks) + EPS)
    return zv * r * g


def _head_norm_bwd(dy, zv, g, masks):
    r = lax.rsqrt(_group_mean(zv * zv, masks) + EPS)
    zh = zv * r
    gy = dy * g
    dz = r * (gy - zh * _group_mean(gy * zh, masks))
    return dz, jnp.sum(dy * zh, axis=0, keepdims=True)


def _fold_heads(v, width):
    vb = jnp.broadcast_to(v, (8, width))
    out = vb
    for h in range(1, width // HD):
        out = out + pltpu.roll(vb, width - HD * h, axis=1)
    return out[0:1]


def _bias_expand(u):
    def body(u_ref, o_ref):
        x = jnp.broadcast_to(u_ref[...], (QB, 1024))
        rolled = pltpu.roll(x, 1024 - (QB - 1), axis=1, stride=1, stride_axis=0)[:, :KWIN]
        row = lax.broadcasted_iota(jnp.int32, (QB, 1), 0)
        col = lax.broadcasted_iota(jnp.int32, (1, KWIN), 1)
        lo = (row // 64) * 64
        ok = (col >= lo) & (col < lo + 576)
        o_ref[...] = jnp.where(ok, rolled, NEG)

    return pl.pallas_call(
        body, grid=(12,), in_specs=[BS((None, 1, 1024), lambda h: (h, 0, 0))],
        out_specs=BS((None, QB, KWIN), lambda h: (h, 0, 0)), out_shape=_sds((12, QB, KWIN), F32),
        compiler_params=_cp(("arbitrary",)), name="bias_expand",
    )(u)


def _bias_reduce(ds):
    def body(d_ref, o_ref):
        ri = lax.broadcasted_iota(jnp.int32, (QB, QB), 0)
        ci = lax.broadcasted_iota(jnp.int32, (QB, QB), 1)
        flip = (ri + ci == QB - 1).astype(F32)
        drev = jnp.dot(flip, d_ref[...], precision=lax.Precision.HIGHEST, preferred_element_type=F32)
        x = jnp.concatenate([drev, jnp.zeros((QB, 1024 - KWIN), F32)], axis=1)
        rolled = pltpu.roll(x, 0, axis=1, stride=1, stride_axis=0)
        o_ref[...] = jnp.sum(rolled, axis=0, keepdims=True)

    return pl.pallas_call(
        body, grid=(12,), in_specs=[BS((None, QB, KWIN), lambda h: (h, 0, 0))],
        out_specs=BS((None, 1, 1024), lambda h: (h, 0, 0)), out_shape=_sds((12, 1, 1024), F32),
        compiler_params=_cp(("arbitrary",)), name="bias_reduce",
    )(ds)


AW = 256
AH = AW // HD
AG = TOK // AW
def _attn_softmax(qh, kw, bias, startadd):
    s = lax.dot_general(qh, kw, _DIMS["nt"], preferred_element_type=F32) + bias + startadd
    m = jnp.max(s, axis=-1, keepdims=True)
    p = jnp.exp(s - m)
    return p * (1.0 / jnp.sum(p, axis=-1, keepdims=True))


def _attn_prologue(q_ref, k_ref, v_ref, gq_ref, gk_ref, qn_s, kn_s, v_s, masks):
    kn_s[0:KPAD, :] = jnp.zeros((KPAD, AW), BF16)
    v_s[0:KPAD, :] = jnp.zeros((KPAD, AW), BF16)
    for r in range(0, SEQ, TR):
        qn_s[r:r + TR, :] = (_head_norm(q_ref[r:r + TR, :], gq_ref[...], masks) * SCALE).astype(BF16)
        kn_s[KPAD + r:KPAD + r + TR, :] = _head_norm(k_ref[r:r + TR, :], gk_ref[...], masks).astype(BF16)
        v_s[KPAD + r:KPAD + r + TR, :] = v_ref[r:r + TR, :].astype(BF16)


def _attn_fwd(z, bias, gq2, gk2, carry=None):
    ni = len(carry.ins) if carry else 0
    no = len(carry.out_shapes) if carry else 0

    def body(q_ref, k_ref, v_ref, b_ref, gq_ref, gk_ref, *rest):
        cin, o_ref, cout = rest[:ni], rest[ni], rest[ni + 1:ni + 1 + no]
        qn_s, kn_s, v_s = rest[ni + 1 + no:ni + 4 + no]
        cscr = rest[ni + 4 + no:]
        if carry:
            @pl.when((pl.program_id(0) == 0) & (pl.program_id(1) == 0))
            def _():
                carry.start(cin, cout, cscr)

        masks = _group_masks(AW)
        _attn_prologue(q_ref, k_ref, v_ref, gq_ref, gk_ref, qn_s, kn_s, v_s, masks)
        col = lax.broadcasted_iota(jnp.int32, (1, KWIN), 1)

        def blk(i, carry):
            r0 = pl.multiple_of(i * QB, QB)
            qb = qn_s[pl.ds(r0, QB), :]
            kw = kn_s[pl.ds(r0, KWIN), :]
            vw = v_s[pl.ds(r0, KWIN), :]
            startadd = jnp.where(col + r0 < KPAD, NEG, 0.0)
            o = jnp.zeros((QB, AW), F32)
            for h in range(AH):
                qh = jnp.where(masks[h], qb, jnp.zeros_like(qb))
                vh = jnp.where(masks[h], vw, jnp.zeros_like(vw))
                p = _attn_softmax(qh, kw, b_ref[h], startadd).astype(BF16)
                o = o + jnp.dot(p, vh, preferred_element_type=F32)
            o_ref[pl.ds(r0, QB), :] = o.astype(BF16)
            return carry

        lax.fori_loop(0, SEQ // QB, blk, 0)

        if carry:
            @pl.when((pl.program_id(0) == NB - 1) & (pl.program_id(1) == AG - 1))
            def _():
                carry.finish(cin, cout, cscr)

    vec = BS((1, AW), lambda b, hp: (0, 0))
    outs = pl.pallas_call(
        body, grid=(NB, AG),
        in_specs=[BS((SEQ, AW), lambda b, hp: (b, hp)),
                  BS((SEQ, AW), lambda b, hp: (b, AG + hp)),
                  BS((SEQ, AW), lambda b, hp: (b, 2 * AG + hp)),
                  BS((AH, QB, KWIN), lambda b, hp: (hp, 0, 0)), vec, vec]
        + (carry.in_specs if carry else []),
        out_specs=[BS((SEQ, AW), lambda b, hp: (b, hp))] + [ANY] * no,
        out_shape=[_sds((T, D), BF16)] + (carry.out_shapes if carry else []),
        scratch_shapes=[pltpu.VMEM((SEQ, AW), BF16), pltpu.VMEM((SEQ + KPAD, AW), BF16),
                        pltpu.VMEM((SEQ + KPAD, AW), BF16)] + (carry.scratch if carry else []),
        compiler_params=pltpu.CompilerParams(
            dimension_semantics=("arbitrary", "arbitrary"), vmem_limit_bytes=48 << 20,
            has_side_effects=bool(carry)), name="attn_fwd",
    )(z, z, z, bias, gq2, gk2, *(carry.ins if carry else []))
    return outs[0], outs[1:]


def _attn_bwd(z, dcat, bias, gq2, gk2, carry=None):
    ni = len(carry.ins) if carry else 0
    no = len(carry.out_shapes) if carry else 0

    def body(q_ref, k_ref, v_ref, do_ref, b_ref, gq_ref, gk_ref, *rest):
        cin = rest[:ni]
        dq_ref, dk_ref, dv_ref, db_ref, dgq_ref, dgk_ref = rest[ni:ni + 6]
        cout = rest[ni + 6:ni + 6 + no]
        qn_s, kn_s, v_s, dqn_s, dkn_s, dv_s = rest[ni + 6 + no:ni + 12 + no]
        cscr = rest[ni + 12 + no:]
        hp = pl.program_id(0)
        b = pl.program_id(1)
        if carry:
            @pl.when((hp == 0) & (b == 0))
            def _():
                carry.start(cin, cout, cscr)

        masks = _group_masks(AW)
        _attn_prologue(q_ref, k_ref, v_ref, gq_ref, gk_ref, qn_s, kn_s, v_s, masks)
        dkn_s[...] = jnp.zeros_like(dkn_s)
        dv_s[...] = jnp.zeros_like(dv_s)

        @pl.when(b == 0)
        def _():
            db_ref[...] = jnp.zeros_like(db_ref)

        @pl.when((b == 0) & (hp == 0))
        def _():
            dgq_ref[...] = jnp.zeros_like(dgq_ref)
            dgk_ref[...] = jnp.zeros_like(dgk_ref)

        col = lax.broadcasted_iota(jnp.int32, (1, KWIN), 1)

        def blk(i, carry):
            r0 = pl.multiple_of(i * QB, QB)
            qb = qn_s[pl.ds(r0, QB), :]
            kw = kn_s[pl.ds(r0, KWIN), :]
            vw = v_s[pl.ds(r0, KWIN), :]
            dob = do_ref[pl.ds(r0, QB), :].astype(BF16)
            startadd = jnp.where(col + r0 < KPAD, NEG, 0.0)
            dqn = jnp.zeros((QB, AW), F32)
            dkw = jnp.zeros((KWIN, AW), F32)
            dvw = jnp.zeros((KWIN, AW), F32)
            for h in range(AH):
                qh = jnp.where(masks[h], qb, jnp.zeros_like(qb))
                kh = jnp.where(masks[h], kw, jnp.zeros_like(kw))
                doh = jnp.where(masks[h], dob, jnp.zeros_like(dob))
                p = _attn_softmax(qh, kw, b_ref[h], startadd)
                dvw = dvw + lax.dot_general(p.astype(BF16), doh, _DIMS["tn"],
                                            preferred_element_type=F32)
                dp = lax.dot_general(doh, vw, _DIMS["nt"], preferred_element_type=F32)
                ds = p * (dp - jnp.sum(dp * p, axis=-1, keepdims=True))
                db_ref[h] += ds
                dsb = ds.astype(BF16)
                dqn = dqn + jnp.dot(dsb, kh, preferred_element_type=F32)
                dkw = dkw + lax.dot_general(dsb, qh, _DIMS["tn"], preferred_element_type=F32)
            dqn_s[pl.ds(r0, QB), :] = dqn * SCALE
            dkn_s[pl.ds(r0, KWIN), :] += dkw
            dv_s[pl.ds(r0, KWIN), :] += dvw
            return carry

        lax.fori_loop(0, SEQ // QB, blk, 0)

        dgq = jnp.zeros((1, AW), F32)
        dgk = jnp.zeros((1, AW), F32)
        for r in range(0, SEQ, TR):
            dq, dg = _head_norm_bwd(dqn_s[r:r + TR, :], q_ref[r:r + TR, :], gq_ref[...], masks)
            dq_ref[r:r + TR, :] = dq.astype(BF16)
            dgq = dgq + dg
            dk, dg = _head_norm_bwd(dkn_s[KPAD + r:KPAD + r + TR, :], k_ref[r:r + TR, :], gk_ref[...], masks)
            dk_ref[r:r + TR, :] = dk.astype(BF16)
            dgk = dgk + dg
            dv_ref[r:r + TR, :] = dv_s[KPAD + r:KPAD + r + TR, :].astype(BF16)
        dgq_ref[...] += _fold_heads(dgq, AW)
        dgk_ref[...] += _fold_heads(dgk, AW)

        if carry:
            @pl.when((hp == AG - 1) & (b == NB - 1))
            def _():
                carry.finish(cin, cout, cscr)

    vec = BS((1, AW), lambda hp, b: (0, 0))
    row = BS((SEQ, AW), lambda hp, b: (b, hp))
    outs = pl.pallas_call(
        body, grid=(AG, NB),
        in_specs=[row,
                  BS((SEQ, AW), lambda hp, b: (b, AG + hp)),
                  BS((SEQ, AW), lambda hp, b: (b, 2 * AG + hp)),
                  row,
                  BS((AH, QB, KWIN), lambda hp, b: (hp, 0, 0)), vec, vec]
        + (carry.in_specs if carry else []),
        out_specs=[row, row, row, BS((AH, QB, KWIN), lambda hp, b: (hp, 0, 0)), vec, vec] + [ANY] * no,
        out_shape=[_sds((T, TOK), BF16), _sds((T, TOK), BF16), _sds((T, TOK), BF16),
                   _sds((12, QB, KWIN), F32), _sds((1, AW), F32), _sds((1, AW), F32)]
        + (carry.out_shapes if carry else []),
        scratch_shapes=[pltpu.VMEM((SEQ, AW), BF16), pltpu.VMEM((SEQ + KPAD, AW), BF16),
                        pltpu.VMEM((SEQ + KPAD, AW), BF16), pltpu.VMEM((SEQ, AW), F32),
                        pltpu.VMEM((SEQ + KPAD, AW), F32), pltpu.VMEM((SEQ + KPAD, AW), F32)]
        + (carry.scratch if carry else []),
        compiler_params=pltpu.CompilerParams(
            dimension_semantics=("arbitrary", "arbitrary"), vmem_limit_bytes=58 << 20,
            has_side_effects=bool(carry)), name="attn_bwd",
    )(z, z, z, dcat, bias, gq2, gk2, *(carry.ins if carry else []))
    return outs[:6], outs[6:]


def _mem_softmax(qh, kn):
    s = lax.dot_general(qh, kn, _DIMS["nt"], preferred_element_type=F32)
    m = jnp.max(s, axis=-1, keepdims=True)
    p = jnp.exp(s - m)
    return p * (1.0 / jnp.sum(p, axis=-1, keepdims=True))


def _memattn_fwd(z, kv, cat, gq4, gk4, qcol, name):
    def body(q_ref, k_ref, v_ref, gq_ref, gk_ref, cat_ref, o_ref):
        del cat_ref
        masks = _group_masks(MEMW)
        qn = (_head_norm(q_ref[...], gq_ref[...], masks) * SCALE).astype(BF16)
        kn = _head_norm(k_ref[...], gk_ref[...], masks).astype(BF16)
        vv = v_ref[...].astype(BF16)
        o = jnp.zeros((TR, MEMW), F32)
        for h in range(4):
            qh = jnp.where(masks[h], qn, jnp.zeros_like(qn))
            vh = jnp.where(masks[h], vv, jnp.zeros_like(vv))
            p = _mem_softmax(qh, kn).astype(BF16)
            o = o + jnp.dot(p, vh, preferred_element_type=F32)
        o_ref[...] = o.astype(BF16)

    nt = SEQ // TR
    vec = BS((1, MEMW), lambda b, t: (0, 0))
    return pl.pallas_call(
        body, grid=(NB, nt),
        in_specs=[BS((TR, MEMW), lambda b, t: (b * nt + t, qcol)),
                  BS((MEMT, MEMW), lambda b, t: (b, 0)),
                  BS((MEMT, MEMW), lambda b, t: (b, 1)), vec, vec, ANY],
        out_specs=BS((TR, MEMW), lambda b, t: (b * nt + t, 3)),
        out_shape=_sds((T, D), BF16), input_output_aliases={5: 0},
        compiler_params=_cp(("arbitrary", "arbitrary")), name=name,
    )(z, kv, kv, gq4, gk4, cat)


def _memattn_bwd(z, kv, dcat, gq4, gk4, qcol, name):
    nt = SEQ // TR

    def body(q_ref, k_ref, v_ref, do_ref, gq_ref, gk_ref,
             dq_ref, dkv_ref, dgq_ref, dgk_ref, dkn_s, dv_s):
        b = pl.program_id(0)
        t = pl.program_id(1)
        masks = _group_masks(MEMW)
        qz = q_ref[...]
        kz = k_ref[...]
        qn = (_head_norm(qz, gq_ref[...], masks) * SCALE).astype(BF16)
        kn = _head_norm(kz, gk_ref[...], masks).astype(BF16)
        vv = v_ref[...].astype(BF16)
        dob = do_ref[...].astype(BF16)

        @pl.when(t == 0)
        def _():
            dkn_s[...] = jnp.zeros_like(dkn_s)
            dv_s[...] = jnp.zeros_like(dv_s)

        @pl.when((t == 0) & (b == 0))
        def _():
            dgq_ref[...] = jnp.zeros_like(dgq_ref)
            dgk_ref[...] = jnp.zeros_like(dgk_ref)

        dqn = jnp.zeros((TR, MEMW), F32)
        dkn = jnp.zeros((MEMT, MEMW), F32)
        dvv = jnp.zeros((MEMT, MEMW), F32)
        for h in range(4):
            qh = jnp.where(masks[h], qn, jnp.zeros_like(qn))
            kh = jnp.where(masks[h], kn, jnp.zeros_like(kn))
            doh = jnp.where(masks[h], dob, jnp.zeros_like(dob))
            p = _mem_softmax(qh, kn)
            dvv = dvv + lax.dot_general(p.astype(BF16), doh, _DIMS["tn"], preferred_element_type=F32)
            dp = lax.dot_general(doh, vv, _DIMS["nt"], preferred_element_type=F32)
            ds = p * (dp - jnp.sum(dp * p, axis=-1, keepdims=True))
            dsb = ds.astype(BF16)
            dqn = dqn + jnp.dot(dsb, kh, preferred_element_type=F32)
            dkn = dkn + lax.dot_general(dsb, qh, _DIMS["tn"], preferred_element_type=F32)
        dkn_s[...] += dkn
        dv_s[...] += dvv
        dq, dgq = _head_norm_bwd(dqn * SCALE, qz, gq_ref[...], masks)
        dq_ref[...] = dq.astype(BF16)
        dgq_ref[...] += _fold_heads(dgq, MEMW)

        @pl.when(t == nt - 1)
        def _():
            dk, dgk = _head_norm_bwd(dkn_s[...], kz, gk_ref[...], masks)
            dkv_ref[:, 0:MEMW] = dk
            dkv_ref[:, MEMW:] = dv_s[...]
            dgk_ref[...] += _fold_heads(dgk, MEMW)

    vec = BS((1, MEMW), lambda b, t: (0, 0))
    return pl.pallas_call(
        body, grid=(NB, nt),
        in_specs=[BS((TR, MEMW), lambda b, t: (b * nt + t, qcol)),
                  BS((MEMT, MEMW), lambda b, t: (b, 0)),
                  BS((MEMT, MEMW), lambda b, t: (b, 1)),
                  BS((TR, MEMW), lambda b, t: (b * nt + t, 3)), vec, vec],
        out_specs=[BS((TR, MEMW), lambda b, t: (b * nt + t, 0)),
                   BS((MEMT, 2 * MEMW), lambda b, t: (b, 0)), vec, vec],
        out_shape=[_sds((T, MEMW), BF16), _sds((NB * MEMT, 2 * MEMW), F32),
                   _sds((1, MEMW), F32), _sds((1, MEMW), F32)],
        scratch_shapes=[pltpu.VMEM((MEMT, MEMW), F32), pltpu.VMEM((MEMT, MEMW), F32)],
        compiler_params=_cp(("arbitrary", "arbitrary")), name=name,
    )(z, kv, kv, dcat, gq4, gk4)


HALO = 32
NEXT = 64
RT = 64


def _glu(zz):
    return zz[:, :TOK] * jax.nn.sigmoid(zz[:, TOK:])


def _layer_norm_parts(y):
    mu = jnp.mean(y, axis=-1, keepdims=True)
    yc = y - mu
    rstd = lax.rsqrt(jnp.mean(yc * yc, axis=-1, keepdims=True) + EPS)
    return yc * rstd, rstd


def _shifted_copies(src, dst, rows):
    for b in range(1, 8):
        dst[b - 1, 0:rows, :] = src[b:b + rows, :]


def _tap(src, shifted, off, r0, rows):
    b = off % 8
    if b == 0:
        return src[r0 + off:r0 + off + rows, :]
    return shifted[b - 1, r0 + off - b:r0 + off - b + rows, :]


def _conv_rows(w_ref, hbuf, hs, r0, rows):
    y = jnp.zeros((rows, TOK), F32)
    for j in range(CONVW):
        y = y + w_ref[j:j + 1, :] * _tap(hbuf, hs, (HALO - CONVW + 1) + j, r0, rows)
    return y


def _conv_fwd(z, cw, cb, lg, lb):
    nt = SEQ // TR

    def body(zc_ref, zp_ref, w_ref, cb_ref, lg_ref, lb_ref, o_ref, y_ref, hbuf, hs):
        t = pl.program_id(1)
        hbuf[0:HALO, :] = jnp.where(t == 0, 0.0, _glu(zp_ref[...]))
        hbuf[HALO:, :] = _glu(zc_ref[...])
        _shifted_copies(hbuf, hs, HALO + TR - 8)
        for r0 in range(0, TR, RT):
            y = _conv_rows(w_ref, hbuf, hs, r0, RT) + cb_ref[...]
            y_ref[r0:r0 + RT, :] = y
            yh, _ = _layer_norm_parts(y)
            o = yh * lg_ref[...] + lb_ref[...]
            o_ref[r0:r0 + RT, :] = (o * jax.nn.sigmoid(o)).astype(BF16)

    vec = BS((1, TOK), lambda b, t: (0, 0))
    per = TR // HALO
    return pl.pallas_call(
        body, grid=(NB, nt),
        in_specs=[BS((TR, 2 * TOK), lambda b, t: (b * nt + t, 0)),
                  BS((HALO, 2 * TOK), lambda b, t: (jnp.maximum((b * nt + t) * per - 1, 0), 0)),
                  BS((32, TOK), lambda b, t: (0, 0)), vec, vec, vec],
        out_specs=[BS((TR, TOK), lambda b, t: (b * nt + t, 0)), BS((TR, TOK), lambda b, t: (b * nt + t, 0))],
        out_shape=[_sds((T, D), BF16), _sds((T, TOK), F32)],
        scratch_shapes=[pltpu.VMEM((HALO + TR, TOK), F32), pltpu.VMEM((7, HALO + TR, TOK), F32)],
        compiler_params=_cp(("arbitrary", "arbitrary")), name="conv_fwd",
    )(z, z, cw, cb, lg, lb)


def _conv_bwd(z, y, dcat, cw, lg, lb):
    nt = SEQ // TR
    ext = TR + NEXT

    def body(zc_ref, zp_ref, yc_ref, yn_ref, dc_ref, dn_ref, w_ref, lg_ref, lb_ref,
             du_ref, dw_ref, dcb_ref, dlg_ref, dlb_ref, dbin_ref, hbuf, dybuf, hs, dys):
        b = pl.program_id(0)
        t = pl.program_id(1)

        @pl.when((b == 0) & (t == 0))
        def _():
            dw_ref[...] = jnp.zeros_like(dw_ref)
            dcb_ref[...] = jnp.zeros_like(dcb_ref)
            dlg_ref[...] = jnp.zeros_like(dlg_ref)
            dlb_ref[...] = jnp.zeros_like(dlb_ref)
            dbin_ref[...] = jnp.zeros_like(dbin_ref)

        hbuf[0:HALO, :] = jnp.where(t == 0, 0.0, _glu(zp_ref[...]))
        hbuf[HALO:, :] = _glu(zc_ref[...])
        _shifted_copies(hbuf, hs, HALO + TR - 8)
        last = t == nt - 1
        for r0 in range(0, ext, RT):
            yh, rstd = _layer_norm_parts(yc_ref[r0:r0 + RT, :] if r0 < TR else yn_ref[...])
            o = yh * lg_ref[...] + lb_ref[...]
            sg = jax.nn.sigmoid(o)
            if r0 < TR:
                dtok = dc_ref[r0:r0 + RT, :]
            else:
                dtok = jnp.where(last, 0.0, dn_ref[...])
            do = dtok * (sg * (1.0 + o * (1.0 - sg)))
            dyh = do * lg_ref[...]
            dy = rstd * (dyh - jnp.mean(dyh, axis=-1, keepdims=True)
                         - yh * jnp.mean(dyh * yh, axis=-1, keepdims=True))
            dybuf[r0:r0 + RT, :] = dy
            if r0 < TR:
                dlg_ref[...] += jnp.sum(do * yh, axis=0, keepdims=True)
                dlb_ref[...] += jnp.sum(do, axis=0, keepdims=True)
                dcb_ref[...] += jnp.sum(dy, axis=0, keepdims=True)
        _shifted_copies(dybuf, dys, ext - 8)
        for r0 in range(0, TR, RT):
            dh = jnp.zeros((RT, TOK), F32)
            for j in range(CONVW):
                dh = dh + w_ref[j:j + 1, :] * _tap(dybuf, dys, (CONVW - 1) - j, r0, RT)
            a = zc_ref[r0:r0 + RT, 0:TOK]
            sg = jax.nn.sigmoid(zc_ref[r0:r0 + RT, TOK:])
            da = dh * sg
            dg = dh * a * (sg * (1.0 - sg))
            du_ref[r0:r0 + RT, 0:TOK] = da.astype(BF16)
            du_ref[r0:r0 + RT, TOK:] = dg.astype(BF16)
            dbin_ref[:, 0:TOK] += jnp.sum(da, axis=0, keepdims=True)
            dbin_ref[:, TOK:] += jnp.sum(dg, axis=0, keepdims=True)
        for j in range(CONVW):
            acc = jnp.zeros((8, TOK), F32)
            for r0 in range(0, TR, RT):
                prod = dybuf[r0:r0 + RT, :] * _tap(hbuf, hs, (HALO - CONVW + 1) + j, r0, RT)
                acc = acc + jnp.sum(prod.reshape(RT // 8, 8, TOK), axis=0)
            dw_ref[j:j + 1, :] += jnp.sum(acc, axis=0, keepdims=True)

    vec = BS((1, TOK), lambda b, t: (0, 0))
    perh = TR // HALO
    pern = TR // NEXT
    nlast_n = T // NEXT - 1
    return pl.pallas_call(
        body, grid=(NB, nt),
        in_specs=[BS((TR, 2 * TOK), lambda b, t: (b * nt + t, 0)),
                  BS((HALO, 2 * TOK), lambda b, t: (jnp.maximum((b * nt + t) * perh - 1, 0), 0)),
                  BS((TR, TOK), lambda b, t: (b * nt + t, 0)),
                  BS((NEXT, TOK), lambda b, t: (jnp.minimum((b * nt + t + 1) * pern, nlast_n), 0)),
                  BS((TR, TOK), lambda b, t: (b * nt + t, 0)),
                  BS((NEXT, TOK), lambda b, t: (jnp.minimum((b * nt + t + 1) * pern, nlast_n), 0)),
                  BS((32, TOK), lambda b, t: (0, 0)), vec, vec],
        out_specs=[BS((TR, 2 * TOK), lambda b, t: (b * nt + t, 0)),
                   BS((32, TOK), lambda b, t: (0, 0)), vec, vec, vec,
                   BS((1, 2 * TOK), lambda b, t: (0, 0))],
        out_shape=[_sds((T, 2 * TOK), BF16), _sds((32, TOK), F32), _sds((1, TOK), F32),
                   _sds((1, TOK), F32), _sds((1, TOK), F32), _sds((1, 2 * TOK), F32)],
        scratch_shapes=[pltpu.VMEM((HALO + TR, TOK), F32), pltpu.VMEM((ext, TOK), F32),
                        pltpu.VMEM((7, HALO + TR, TOK), F32), pltpu.VMEM((7, ext, TOK), F32)],
        compiler_params=_cp(("arbitrary", "arbitrary"), vmem_mb=56), name="conv_bwd",
    )(z, z, y, y, dcat, dcat, cw, lg, lb)


def _ffn_up(h2, wgu, l, carry=None):
    ni = len(carry.ins) if carry else 0
    no = len(carry.out_shapes) if carry else 0

    def body(h_ref, wg_ref, wu_ref, *rest):
        cin = rest[:ni]
        g_ref, u_ref, a_ref = rest[ni:ni + 3]
        cout, cscr = rest[ni + 3:ni + 3 + no], rest[ni + 3 + no:]
        if carry:
            @pl.when((pl.program_id(0) == 0) & (pl.program_id(1) == 0))
            def _():
                carry.start(cin, cout, cscr)

        hv = h_ref[...]
        g = lax.dot_general(hv, wg_ref[...], _DIMS["nt"], preferred_element_type=F32)
        u = lax.dot_general(hv, wu_ref[...], _DIMS["nt"], preferred_element_type=F32)
        sg = jax.nn.sigmoid(g)
        silu = g * sg
        g_ref[...] = (u * (sg * (1.0 + g * (1.0 - sg)))).astype(BF16)
        u_ref[...] = silu.astype(BF16)
        a_ref[...] = (silu * u).astype(BF16)

        if carry:
            @pl.when((pl.program_id(0) == FF // FT - 1) & (pl.program_id(1) == T // TR - 1))
            def _():
                carry.finish(cin, cout, cscr)

    out = BS((TR, FT), lambda q, i: (i, q))
    outs = pl.pallas_call(
        body, grid=(FF // FT, T // TR),
        in_specs=[BS((TR, D), lambda q, i: (i, 0)),
                  BS((None, FT, D), lambda q, i: (0, q, 0)),
                  BS((None, FT, D), lambda q, i: (1, q, 0))] + (carry.in_specs if carry else []),
        out_specs=[out, out, out] + [ANY] * no,
        out_shape=[_sds((T, FF), BF16), _sds((T, FF), BF16), _sds((T, FF), BF16)]
        + (carry.out_shapes if carry else []),
        scratch_shapes=carry.scratch if carry else [],
        compiler_params=pltpu.CompilerParams(
            dimension_semantics=("arbitrary", "arbitrary"), vmem_limit_bytes=48 << 20,
            has_side_effects=bool(carry)), name=f"ffn_up_{l}",
    )(h2, wgu, wgu, *(carry.ins if carry else []))
    return outs[:3], outs[3:]


def _ffn_down_bwd(dx, wd, g, u, l):
    def epilogue(dact, ex, o_ref, i):
        o_ref[0] = (dact * ex[0][...].astype(F32)).astype(BF16)
        o_ref[1] = (dact * ex[1][...].astype(F32)).astype(BF16)

    ex_spec = BS((TR, FT), lambda i, q, k: (i, q))
    return _mm("nt", dx, wd, grid=(T // TR, FF // FT, 1),
               a_spec=BS((TR, D), lambda i, q, k: (i, 0)),
               b_spec=BS((FT, D), lambda i, q, k: (q, 0)),
               out_shape=_sds((2, T, FF), BF16),
               out_spec=BS((2, TR, FT), lambda i, q, k: (0, i, q)),
               acc_shape=(TR, FT), extras=(g, u), extra_specs=(ex_spec, ex_spec),
               epilogue=epilogue, name=f"ffn_down_bwd_{l}")


def _row_tile(rows, cols, itemsize=4, limit=2 << 20):
    tr = rows
    while tr * cols * itemsize > limit and tr % 2 == 0 and (tr // 2) % 16 == 0:
        tr //= 2
    return tr


def _cast_bf16(arrs, name):
    n = len(arrs)
    rows, cols = arrs[0].shape
    tr = _row_tile(rows, cols)

    def body(*refs):
        o_ref = refs[n]
        k = pl.program_id(0)
        val = refs[0][...]
        for j in range(1, n):
            val = jnp.where(k == j, refs[j][...], val)
        o_ref[...] = val.astype(BF16)

    return pl.pallas_call(
        body, grid=(n, rows // tr),
        in_specs=[BS((tr, cols), lambda k, i: (i, 0))] * n,
        out_specs=BS((None, tr, cols), lambda k, i: (k, i, 0)),
        out_shape=_sds((n, rows, cols), BF16),
        compiler_params=_cp(("arbitrary", "arbitrary")), name=name,
    )(*arrs)


def _quad_sum(own, got, name):
    n, rows, cols = own.shape
    tr = _row_tile(rows, cols)

    def body(a_ref, q_ref, o_ref):
        o_ref[...] = ((a_ref[...] + q_ref[0].astype(F32)) + q_ref[1].astype(F32)) + q_ref[2].astype(F32)

    spec = BS((None, tr, cols), lambda k, i: (k, i, 0))
    return pl.pallas_call(
        body, grid=(n, rows // tr),
        in_specs=[spec, BS((3, None, tr, cols), lambda k, i: (0, k, i, 0))], out_specs=spec,
        out_shape=_sds((n, rows, cols), F32),
        compiler_params=_cp(("arbitrary", "arbitrary")), name=name,
    )(own, got)


def _adam_math(w, g, m, v):
    m = ADAM_B1 * m + (1.0 - ADAM_B1) * g
    v = ADAM_B2 * v + (1.0 - ADAM_B2) * (g * g)
    m_hat = m / (1.0 - ADAM_B1 ** ADAM_STEP)
    v_hat = v / (1.0 - ADAM_B2 ** ADAM_STEP)
    delta = -ADAM_LR * (m_hat / (jnp.sqrt(v_hat) + ADAM_EPS) + ADAM_WD * w)
    return delta, m, v


def _adamw_big(w, g, m, v, name):
    shape = w.shape
    cols = shape[-1]
    rows = w.size // cols
    tr = _row_tile(rows, cols, limit=1 << 20)

    def body(w_ref, g_ref, m_ref, v_ref, d_ref, nm_ref, nv_ref):
        d, nm, nv = _adam_math(w_ref[...], g_ref[...], m_ref[...], v_ref[...])
        d_ref[...] = d
        nm_ref[...] = nm
        nv_ref[...] = nv

    spec = BS((tr, cols), lambda i: (i, 0))
    outs = pl.pallas_call(
        body, grid=(rows // tr,), in_specs=[spec] * 4, out_specs=[spec] * 3,
        out_shape=[_sds((rows, cols), F32)] * 3,
        compiler_params=_cp(("arbitrary",)), name=name,
    )(*[a.reshape(rows, cols) for a in (w, g, m, v)])
    return [o.reshape(shape) for o in outs]


def _adamw_small(ws, gs, ms, vs):
    n = len(ws)

    def body(*refs):
        for i in range(n):
            d, nm, nv = _adam_math(refs[i][...], refs[n + i][...], refs[2 * n + i][...],
                                   refs[3 * n + i][...])
            refs[4 * n + i][...] = d
            refs[5 * n + i][...] = nm
            refs[6 * n + i][...] = nv

    specs = [BS(w.shape, lambda i: (0, 0)) for w in ws]
    outs = pl.pallas_call(
        body, grid=(1,), in_specs=specs * 4, out_specs=specs * 3,
        out_shape=[_sds(w.shape, F32) for w in ws] * 3,
        compiler_params=_cp(("arbitrary",)), name="adamw_small",
    )(*ws, *gs, *ms, *vs)
    return outs[:n], outs[n:2 * n], outs[2 * n:]


def _place():
    x, y, c = lax.axis_index("x"), lax.axis_index("y"), lax.axis_index("c")
    chips = [(1 - x, y), (x, 1 - y), (1 - x, 1 - y)]
    return x, y, c, chips


class _Exchange:
    def __init__(self, ins, in_specs, out_shapes, scratch, start, finish):
        self.ins, self.in_specs, self.out_shapes, self.scratch = ins, in_specs, out_shapes, scratch
        self.start, self.finish = start, finish


def _run_exchange(ex, name, vmem_mb=40):
    ni, no = len(ex.ins), len(ex.out_shapes)

    def body(*refs):
        ex.start(refs[:ni], refs[ni:ni + no], refs[ni + no:])
        ex.finish(refs[:ni], refs[ni:ni + no], refs[ni + no:])

    return pl.pallas_call(
        body, in_specs=ex.in_specs, out_specs=[ANY] * no, out_shape=ex.out_shapes,
        scratch_shapes=ex.scratch,
        compiler_params=pltpu.CompilerParams(has_side_effects=True, vmem_limit_bytes=vmem_mb << 20),
        name=name,
    )(*ex.ins)


def _gather_exchange(srcs, dst_shapes, views, small=None):
    nu = len(srcs)
    nd = len(dst_shapes)
    ns = 1 if small is not None else 0

    def unpack(ins, outs, scr):
        x, y, c, chips = _place()
        src = ins[:nu]
        vw = [views[u](outs[:nd]) for u in range(nu)]
        vbuf = scr[:nu]
        send, recv, fsend, frecv, lsem, ssend, srecv, vsem = scr[nu:]

        def ici(u, j, shard, to):
            return pltpu.make_async_remote_copy(
                src_ref=vbuf[u].at[:, c], dst_ref=vw[u].at[:, shard, c],
                send_sem=send.at[3 * u + j], recv_sem=recv.at[3 * u + j],
                device_id=to, device_id_type=MESH)

        def fwd(u, j, shard, half):
            return pltpu.make_async_remote_copy(
                src_ref=vw[u].at[:, shard, half], dst_ref=vw[u].at[:, shard, half],
                send_sem=fsend.at[3 * u + j], recv_sem=frecv.at[3 * u + j],
                device_id=(x, y, 1 - c), device_id_type=MESH)

        def small_copy(j, shard, to):
            return pltpu.make_async_remote_copy(
                src_ref=ins[nu], dst_ref=outs[nd].at[shard],
                send_sem=ssend.at[j], recv_sem=srecv.at[j], device_id=to, device_id_type=MESH)

        stage = [pltpu.make_async_copy(src[u], vbuf[u], vsem.at[u]) for u in range(nu)]
        local = [pltpu.make_async_copy(vbuf[u], vw[u].at[:, 2 * x + y], lsem.at[u]) for u in range(nu)]
        if ns:
            local.append(pltpu.make_async_copy(ins[nu], outs[nd].at[2 * x + y], lsem.at[nu]))
        return x, y, c, chips, ici, fwd, small_copy, stage, local

    def start(ins, outs, scr):
        x, y, c, chips, ici, fwd, small_copy, stage, local = unpack(ins, outs, scr)
        s = 2 * x + y
        for cp in stage:
            cp.start()
        if ns:
            local[nu].start()
            for j, chip in enumerate(chips):
                small_copy(j, s, (*chip, c)).start()
        for u in range(nu):
            stage[u].wait()
            for j, chip in enumerate(chips):
                ici(u, j, s, (*chip, c)).start()
            local[u].start()

    def finish(ins, outs, scr):
        x, y, c, chips, ici, fwd, small_copy, stage, local = unpack(ins, outs, scr)
        s = 2 * x + y
        for u in range(nu):
            for j, chip in enumerate(chips):
                sj = 2 * chip[0] + chip[1]
                ici(u, j, sj, (x, y, c)).wait_recv()
                fwd(u, j, sj, c).start()
        for u in range(nu):
            for j, chip in enumerate(chips):
                sj = 2 * chip[0] + chip[1]
                fwd(u, j, sj, 1 - c).wait_recv()
        for u in range(nu):
            for j, chip in enumerate(chips):
                ici(u, j, s, (*chip, c)).wait_send()
                fwd(u, j, s, c).wait_send()
        if ns:
            for j, chip in enumerate(chips):
                small_copy(j, 2 * chip[0] + chip[1], (x, y, c)).wait_recv()
                small_copy(j, s, (*chip, c)).wait_send()
        for cp in local:
            cp.wait()

    dma = pltpu.SemaphoreType.DMA
    return _Exchange(
        ins=list(srcs) + ([small] if ns else []),
        in_specs=[ANY] * nu + [BS(memory_space=pltpu.VMEM)] * ns,
        out_shapes=[_sds(sh, BF16) for sh in dst_shapes]
        + ([_sds((NSH,) + small.shape, F32)] if ns else []),
        scratch=[pltpu.VMEM(a.shape, BF16) for a in srcs]
        + [dma((3 * nu,)), dma((3 * nu,)), dma((3 * nu,)), dma((3 * nu,)),
           dma((nu + 1,)), dma((3,)), dma((3,)), dma((nu,))],
        start=start, finish=finish)


def _pair_reduce(gs, name):
    nu = len(gs)
    ns = [g.shape[0] * NSH for g in gs]
    base = [sum(ns[:u]) for u in range(nu)]

    def body(*refs):
        g_refs, own_refs, sb_refs = refs[:nu], refs[nu:2 * nu], refs[2 * nu:3 * nu]
        bufs = refs[3 * nu:8 * nu]
        send, recv, lsem, osem = refs[8 * nu:]
        x, y, c, _ = _place()
        s = 2 * x + y

        def unit(u):
            sendb, recvb, stage, outf, outb = bufs[5 * u:5 * u + 5]

            def load(k, half):
                return pltpu.make_async_copy(g_refs[u].at[k // NSH, k % NSH, half], stage.at[k % 3],
                                             lsem.at[3 * u + k % 3])

            def push(k):
                return pltpu.make_async_remote_copy(
                    src_ref=sendb.at[k], dst_ref=recvb.at[k], send_sem=send.at[base[u] + k],
                    recv_sem=recv.at[base[u] + k], device_id=(x, y, 1 - c), device_id_type=MESH)

            def store(k):
                return pltpu.make_async_copy(outb.at[k % 2], sb_refs[u].at[k // NSH, k % NSH],
                                             osem.at[3 * u + k % 2])

            return sendb, recvb, stage, outf, outb, load, push, store

        for u in range(nu):
            sendb, recvb, stage, outf, outb, load, push, store = unit(u)
            for k in range(min(2, ns[u])):
                load(k, 1 - c).start()
            for k in range(ns[u]):
                if k + 2 < ns[u]:
                    load(k + 2, 1 - c).start()
                load(k, 1 - c).wait()
                sendb[k] = stage[k % 3].astype(BF16)
                push(k).start()
        for u in range(nu):
            sendb, recvb, stage, outf, outb, load, push, store = unit(u)
            n = ns[u]
            for k in range(min(2, n)):
                load(k, c).start()
            for k in range(n):
                if k + 2 < n:
                    load(k + 2, c).start()
                load(k, c).wait()
                push(k).wait_recv()
                total = stage[k % 3] + recvb[k].astype(F32)
                if k >= 2:
                    store(k - 2).wait()
                outb[k % 2] = total.astype(BF16)
                store(k).start()

                @pl.when(s == k % NSH)
                def _():
                    outf[...] = total
                    keep = pltpu.make_async_copy(outf, own_refs[u].at[k // NSH], osem.at[3 * u + 2])
                    keep.start()
                    keep.wait()

            for k in range(max(n - 2, 0), n):
                store(k).wait()
        for u in range(nu):
            push = unit(u)[6]
            for k in range(ns[u]):
                push(k).wait_send()

    dma = pltpu.SemaphoreType.DMA
    scratch = []
    for g, n in zip(gs, ns):
        rh, cc = g.shape[3], g.shape[4]
        scratch += [pltpu.VMEM((n, rh, cc), BF16), pltpu.VMEM((n, rh, cc), BF16),
                    pltpu.VMEM((3, rh, cc), F32), pltpu.VMEM((rh, cc), F32), pltpu.VMEM((2, rh, cc), BF16)]
    outs = pl.pallas_call(
        body, in_specs=[ANY] * nu, out_specs=[ANY] * (2 * nu),
        out_shape=[_sds((g.shape[0], g.shape[3], g.shape[4]), F32) for g in gs]
        + [_sds((g.shape[0], NSH, g.shape[3], g.shape[4]), BF16) for g in gs],
        scratch_shapes=scratch + [dma((sum(ns),)), dma((sum(ns),)), dma((3 * nu,)), dma((3 * nu,))],
        compiler_params=pltpu.CompilerParams(has_side_effects=True, vmem_limit_bytes=56 << 20),
        name=name,
    )(*gs)
    return list(outs[:nu]), list(outs[nu:])


def _chip_exchange(sums_bf16):
    nu = len(sums_bf16)

    def pushes(ins, outs, scr):
        x, y, c, chips = _place()
        send, recv = scr
        return [pltpu.make_async_remote_copy(
            src_ref=ins[u].at[:, 2 * chip[0] + chip[1]], dst_ref=outs[u].at[j],
            send_sem=send.at[3 * u + j], recv_sem=recv.at[3 * u + j],
            device_id=(*chip, c), device_id_type=MESH)
            for u in range(nu) for j, chip in enumerate(chips)]

    def start(ins, outs, scr):
        for cp in pushes(ins, outs, scr):
            cp.start()

    def finish(ins, outs, scr):
        for cp in pushes(ins, outs, scr):
            cp.wait()

    dma = pltpu.SemaphoreType.DMA
    shapes = [(3, a.shape[0], a.shape[2], a.shape[3]) for a in sums_bf16]
    return _Exchange(ins=list(sums_bf16), in_specs=[ANY] * nu,
                     out_shapes=[_sds(sh, BF16) for sh in shapes],
                     scratch=[dma((3 * nu,)), dma((3 * nu,))], start=start, finish=finish)


def _final_exchange(halves, out_shapes, targets):
    nu = len(halves)
    no = len(out_shapes)
    ncp = sum(len(t) for t in targets)

    def body(*refs):
        hv = refs[:nu]
        out = refs[nu:nu + no]
        sbuf = refs[nu + no:2 * nu + no]
        rbuf = refs[2 * nu + no:3 * nu + no]
        send, recv, lsem, osem, csem = refs[3 * nu + no:]
        x, y, c, _ = _place()
        stage = [pltpu.make_async_copy(hv[u], sbuf[u], lsem.at[u]) for u in range(nu)]
        push = [pltpu.make_async_remote_copy(
            src_ref=sbuf[u], dst_ref=rbuf[u], send_sem=send.at[u], recv_sem=recv.at[u],
            device_id=(x, y, 1 - c), device_id_type=MESH) for u in range(nu)]
        mine, theirs = [], []
        k = 0
        for u in range(nu):
            rh = hv[u].shape[1]
            for (p, oi, li) in targets[u]:
                mine.append((u, pltpu.make_async_copy(
                    sbuf[u].at[p], out[oi].at[li, pl.ds(c * rh, rh), :], csem.at[k])))
                theirs.append((u, pltpu.make_async_copy(
                    rbuf[u].at[p], out[oi].at[li, pl.ds((1 - c) * rh, rh), :], osem.at[k])))
                k += 1
        for cp in stage:
            cp.start()
        for u in range(nu):
            stage[u].wait()
            push[u].start()
            for (v, cp) in mine:
                if v == u:
                    cp.start()
        for u in range(nu):
            push[u].wait_recv()
            for (v, cp) in theirs:
                if v == u:
                    cp.start()
        for (_, cp) in theirs + mine:
            cp.wait()
        for u in range(nu):
            push[u].wait_send()

    dma = pltpu.SemaphoreType.DMA
    bufs = [pltpu.VMEM(h.shape, F32) for h in halves]
    return pl.pallas_call(
        body, in_specs=[ANY] * nu, out_specs=[ANY] * no,
        out_shape=[_sds(sh, F32) for sh in out_shapes],
        scratch_shapes=bufs + bufs + [dma((nu,)), dma((nu,)), dma((nu,)), dma((ncp,)), dma((ncp,))],
        compiler_params=pltpu.CompilerParams(has_side_effects=True, vmem_limit_bytes=56 << 20),
        name="final_exchange",
    )(*halves)


def _small_allreduce(pack):
    rows = pack.shape[0]

    def body(p_ref, o_ref, buf, send, recv):
        x, y, c, _ = _place()
        me = 4 * x + 2 * y + c
        buf[me] = p_ref[...]
        k = 0
        copies = []
        for dx in range(2):
            for dy in range(2):
                for dc in range(2):
                    if dx == 0 and dy == 0 and dc == 0:
                        continue
                    to = (jnp.where(dx, 1 - x, x), jnp.where(dy, 1 - y, y), jnp.where(dc, 1 - c, c))
                    src_slot = 4 * to[0] + 2 * to[1] + to[2]
                    copies.append((pltpu.make_async_remote_copy(
                        src_ref=p_ref, dst_ref=buf.at[me], send_sem=send.at[k], recv_sem=recv.at[k],
                        device_id=to, device_id_type=MESH), src_slot, k))
                    k += 1
        for cp, _, _ in copies:
            cp.start()
        for cp, src_slot, k in copies:
            pltpu.make_async_remote_copy(
                src_ref=p_ref, dst_ref=buf.at[src_slot], send_sem=send.at[k], recv_sem=recv.at[k],
                device_id=(x, y, c), device_id_type=MESH).wait()
        acc = buf[0]
        for d in range(1, 8):
            acc = acc + buf[d]
        o_ref[...] = acc

    dma = pltpu.SemaphoreType.DMA
    vm = BS(memory_space=pltpu.VMEM)
    return pl.pallas_call(
        body, in_specs=[vm], out_specs=vm, out_shape=_sds((rows, D), F32),
        scratch_shapes=[pltpu.VMEM((8, rows, D), F32), dma((7,)), dma((7,))],
        compiler_params=pltpu.CompilerParams(has_side_effects=True, vmem_limit_bytes=32 << 20),
        name="small_allreduce",
    )(pack)


def _in_proj(h, w, bias, name, transposed=False, carry=None):
    n = w.shape[0] if transposed else w.shape[1]
    tn = 640 if n == NA else 896
    ep = None
    extras, especs = (), ()
    if bias is not None:
        def ep(acc, ex, o_ref, i):
            o_ref[...] = acc + ex[0][...]
        extras = (bias,)
        especs = (BS((1, tn), lambda i, j, k: (0, j)),)
    b_spec = BS((tn, D), lambda i, j, k: (j, 0)) if transposed else BS((D, tn), lambda i, j, k: (0, j))
    return _mm("nt" if transposed else "nn", h, w, grid=(T // TR, n // tn, 1),
               a_spec=BS((TR, D), lambda i, j, k: (i, 0)), b_spec=b_spec,
               out_shape=_sds((T, n), F32), out_spec=BS((TR, tn), lambda i, j, k: (i, j)),
               acc_shape=(TR, tn), extras=extras, extra_specs=especs, epilogue=ep, name=name, carry=carry)


def _res_rms_epilogue(acc, ex, outs, i):
    y = acc + ex[0][...]
    outs[0][...] = y
    r = lax.rsqrt(jnp.mean(y * y, axis=-1, keepdims=True) + EPS)
    outs[1][...] = (y * r * ex[1][...]).astype(BF16)


def _res_loss_epilogue(acc, ex, outs, i):
    e = acc + ex[0][...] - ex[1][...]
    outs[1][...] = e * (1.0 / D)

    @pl.when(i == 0)
    def _():
        outs[0][...] = jnp.zeros_like(outs[0])

    outs[0][...] += 0.5 * jnp.sum(jnp.mean(e * e, axis=-1, keepdims=True), axis=0, keepdims=True)


def _rms_bwd_epilogue(dh, ex, outs, i):
    xv = ex[0][...]
    r = lax.rsqrt(jnp.mean(xv * xv, axis=-1, keepdims=True) + EPS)
    xh = xv * r
    gy = dh * ex[1][...]
    outs[0][...] = r * (gy - xh * jnp.mean(gy * xh, axis=-1, keepdims=True)) + ex[2][...]

    @pl.when(i == 0)
    def _():
        outs[1][...] = jnp.zeros_like(outs[1])

    outs[1][...] += jnp.sum(dh * xh, axis=0, keepdims=True)


def _local_step(x, mem, target, w, p, carries=None, bwd_carry_fn=None, late_carry_fn=None):
    row = lambda i, j, k: (i, 0)
    whole = lambda i, j, k: (0, 0)
    w = {k: (list(v) if isinstance(v, list) else v) for k, v in w.items()}
    carries = carries or {}

    def carry_of(name):
        return carries[name][0] if name in carries else None

    def delivered(name, outs):
        if name in carries:
            carries[name][1](w, outs)

    saved = []
    bias = _bias_expand(p["rel_u"])
    vec = BS((1, D), whole)
    h = _rms_fwd(x, p["norm1_g"][0:1], "rms1_0")
    for l in range(2):
        type_a = l == 0
        memn = _rms_fwd(mem, p["mem_norm_g"][l:l + 1], f"rmsmem_{l}")
        if type_a:
            z = _in_proj(h, w["a"], None, "inproj_a", carry=carry_of("inproj_a"))
            if carry_of("inproj_a") is not None:
                z, carried = z
                delivered("inproj_a", carried)
            cat, carried = _attn_fwd(z, bias, p["a_q_g2"], p["a_k_g2"], carry_of("attn_fwd"))
            delivered("attn_fwd", carried)
            qcol = NA // MEMW - 1
        else:
            z = _in_proj(h, w["b"], p["b_b_in"], "inproj_b", transposed=True)
            cat, conv_y = _conv_fwd(z, p["conv_w"], p["conv_b"], p["ln_g"], p["ln_b"])
            qcol = NBW // MEMW - 1
        kv = _mm("nn", memn, w["kv"][l], grid=(1, 1, 1),
                 a_spec=BS((NB * MEMT, D), whole), b_spec=BS((D, 2 * MEMW), whole),
                 out_shape=_sds((NB * MEMT, 2 * MEMW), F32), out_spec=BS((NB * MEMT, 2 * MEMW), whole),
                 acc_shape=(8, 128), name=f"memkv_{l}")
        cat = _memattn_fwd(z, kv, cat, p["mq_g4"][l:l + 1], p["mk_g4"][l:l + 1], qcol, f"memattn_fwd_{l}")
        x1, h2 = _mm("nn", cat, w["wo"][l], grid=(T // TR, 1, 1), a_spec=BS((TR, D), row),
                     b_spec=BS((D, D), whole),
                     out_shape=[_sds((T, D), F32), _sds((T, D), BF16)],
                     out_spec=[BS((TR, D), row), BS((TR, D), row)], acc_shape=(8, 128),
                     extras=(x, p["norm2_g"][l:l + 1]), extra_specs=(BS((TR, D), row), vec),
                     epilogue=_res_rms_epilogue, name=f"outproj_{l}")
        (g, u, act), carried = _ffn_up(h2, w["gu"][l], l, carry_of(f"ffn_up_{l}"))
        delivered(f"ffn_up_{l}", carried)
        last = l == 1
        res = _mm("nn", act, w["wd"][l], grid=(T // TR, 1, 1),
                  a_spec=BS((TR, FF), row), b_spec=BS((FF, D), whole),
                  out_shape=[_sds((1, 1), F32), _sds((T, D), F32)] if last else
                  [_sds((T, D), F32), _sds((T, D), BF16)],
                  out_spec=[BS((1, 1), whole), BS((TR, D), row)] if last else
                  [BS((TR, D), row), BS((TR, D), row)],
                  acc_shape=(8, 128), extras=(x1, target if last else p["norm1_g"][1:2]),
                  extra_specs=(BS((TR, D), row), BS((TR, D), row) if last else vec),
                  epilogue=_res_loss_epilogue if last else _res_rms_epilogue, sequential=last,
                  name=f"ffn_down_{l}", carry=carry_of(f"ffn_down_{l}"))
        if carry_of(f"ffn_down_{l}") is not None:
            res, carried = res
            delivered(f"ffn_down_{l}", carried)
        saved.append(dict(x=x, h=h, memn=memn, kv=kv, z=z, cat=cat, x1=x1, h2=h2, g=g, u=u, act=act,
                          qcol=qcol))
        if last:
            loss, dx = res
        else:
            x, h = res

    big = dict(a=None, b=None, kv=[None, None], wo=[None, None], gu=[None, None], wd=[None, None])
    small = {}
    bwd_carried, late_carried = (), ()
    tk = T // 2
    nkt = T // tk
    for l in (1, 0):
        sv = saved[l]
        dgu = _ffn_down_bwd(dx, w["wd"][l], sv["g"], sv["u"], l)
        big["wd"][l] = _mm("tn", sv["act"], dx, grid=(FF // FT, 1, 2 * nkt),
                           a_spec=BS((tk // 2, FT), lambda i, j, k: (k, i)),
                           b_spec=BS((tk // 2, D), lambda i, j, k: (k, 0)),
                           out_shape=_sds((FF, D), F32), out_spec=BS((FT, D), lambda i, j, k: (i, 0)),
                           acc_shape=(FT, D), name=f"dw_down_{l}")
        dx1, small[f"norm2_g{l}"] = _mm(
            "nn", dgu, w["gu"][l], grid=(T // TR, 1, 2),
            a_spec=BS((None, TR, FF), lambda i, j, k: (k, i, 0)),
            b_spec=BS((None, FF, D), lambda i, j, k: (k, 0, 0)),
            out_shape=[_sds((T, D), F32), _sds((1, D), F32)], out_spec=[BS((TR, D), row), vec],
            acc_shape=(TR, D), extras=(sv["x1"], p["norm2_g"][l:l + 1], dx),
            extra_specs=(BS((TR, D), row), vec, BS((TR, D), row)),
            epilogue=_rms_bwd_epilogue, sequential=True, name=f"dh2_{l}")
        big["gu"][l] = _mm("tn", dgu, sv["h2"], grid=(2 * FF // FT, 1, nkt),
                           a_spec=BS((None, tk, FT), lambda i, j, k: (i // 2, k, i % 2)),
                           b_spec=BS((tk, D), lambda i, j, k: (k, 0)),
                           out_shape=_sds((2, FF, D), F32),
                           out_spec=BS((None, FT, D), lambda i, j, k: (i // 2, i % 2, 0)),
                           acc_shape=(FT, D), vmem_mb=58, name=f"dw_gu_{l}")
        dcat = _mm("nt", dx1, w["wo"][l], grid=(T // TR, 1, 1), a_spec=BS((TR, D), row),
                   b_spec=BS((D, D), whole),
                   out_shape=_sds((T, D), F32), out_spec=BS((TR, D), row), acc_shape=(8, 128),
                   name=f"dcat_{l}")
        big["wo"][l] = _mm("tn", sv["cat"], dx1, grid=(1, 1, nkt),
                           a_spec=BS((tk, D), lambda i, j, k: (k, 0)), b_spec=BS((tk, D), lambda i, j, k: (k, 0)),
                           out_shape=_sds((D, D), F32), out_spec=BS((D, D), whole),
                           acc_shape=(D, D), name=f"dw_out_{l}")
        dqm, dkv, small[f"mq_g{l}"], small[f"mk_g{l}"] = _memattn_bwd(
            sv["z"], sv["kv"], dcat, p["mq_g4"][l:l + 1], p["mk_g4"][l:l + 1], sv["qcol"], f"memattn_bwd_{l}")
        big["kv"][l] = _mm("tn", sv["memn"], dkv, grid=(1, 1, 1),
                           a_spec=BS((NB * MEMT, D), whole), b_spec=BS((NB * MEMT, 2 * MEMW), whole),
                           out_shape=_sds((D, 2 * MEMW), F32), out_spec=BS((D, 2 * MEMW), whole),
                           acc_shape=(8, 128), name=f"dw_kv_{l}")
        dmemn = _mm("nt", dkv, w["kv"][l], grid=(1, 1, 1),
                    a_spec=BS((NB * MEMT, 2 * MEMW), whole), b_spec=BS((D, 2 * MEMW), whole),
                    out_shape=_sds((NB * MEMT, D), F32), out_spec=BS((NB * MEMT, D), whole),
                    acc_shape=(8, 128), name=f"dmemn_{l}")
        _, small[f"mem_norm_g{l}"] = _rms_bwd(dmemn, mem, p["mem_norm_g"][l:l + 1], None, f"rmsmem_bwd_{l}")
        if l == 0:
            carry = bwd_carry_fn(big) if bwd_carry_fn is not None else None
            (dq, dk, dv, dbias, small["a_q_g"], small["a_k_g"]), bwd_carried = _attn_bwd(
                sv["z"], dcat, bias, p["a_q_g2"], p["a_k_g2"], carry)
            small["rel_u"] = _bias_reduce(dbias)
            dz = jnp.concatenate([dq, dk, dv, dqm], axis=1)
            w_in, key, n, tn = w["a"], "a", NA, 640
        else:
            du, small["conv_w"], small["conv_b"], small["ln_g"], small["ln_b"], dbin_u = _conv_bwd(
                sv["z"], conv_y, dcat, p["conv_w"], p["ln_g"], p["ln_b"])
            dz = jnp.concatenate([du, dqm], axis=1)
            small["b_in_u"] = dbin_u
            w_in, key, n, tn = w["b"], "b", NBW, 896
        norm_bwd = dict(out_shape=[_sds((T, D), F32), _sds((1, D), F32)], out_spec=[BS((TR, D), row), vec],
                        acc_shape=(8, 128), extras=(sv["x"], p["norm1_g"][l:l + 1], dx1),
                        extra_specs=(BS((TR, D), row), vec, BS((TR, D), row)),
                        epilogue=_rms_bwd_epilogue, sequential=True, name=f"dh_{l}")
        if l == 0:
            big[key] = _mm("tn", sv["h"], dz, grid=(1, n // tn, nkt),
                           a_spec=BS((tk, D), lambda i, j, k: (k, 0)), b_spec=BS((tk, tn), lambda i, j, k: (k, j)),
                           out_shape=_sds((D, n), F32), out_spec=BS((D, tn), lambda i, j, k: (0, j)),
                           acc_shape=(D, tn), name=f"dw_in_{l}")
            carry = late_carry_fn(big) if late_carry_fn is not None else None
            res = _mm("nt", dz, w_in, grid=(T // TR, 1, 1), a_spec=BS((TR, n), row),
                      b_spec=BS((D, n), whole), carry=carry, **norm_bwd)
            if carry is not None:
                res, late_carried = res
            dx, small[f"norm1_g{l}"] = res
        else:
            dx, small[f"norm1_g{l}"] = _mm("nn", dz, w_in, grid=(T // TR, 1, 1),
                                           a_spec=BS((TR, n), row), b_spec=BS((n, D), whole), **norm_bwd)
            big[key] = _mm("tn", dz, sv["h"], grid=(n // tn, 1, nkt),
                           a_spec=BS((tk, tn), lambda i, j, k: (k, i)), b_spec=BS((tk, D), lambda i, j, k: (k, 0)),
                           out_shape=_sds((n, D), F32), out_spec=BS((tn, D), lambda i, j, k: (i, 0)),
                           acc_shape=(tn, D), name=f"dw_in_{l}")
        if l == 1:
            small["b_in_qm"] = _colsum(dqm, "colsum_dqm")
    return loss, dx, big, small, bwd_carried, late_carried


def _colsum(a, name):
    rows, cols = a.shape

    def body(a_ref, o_ref):
        @pl.when(pl.program_id(0) == 0)
        def _():
            o_ref[...] = jnp.zeros_like(o_ref)

        o_ref[...] += jnp.sum(a_ref[...].astype(F32), axis=0, keepdims=True)

    return pl.pallas_call(
        body, grid=(rows // TR,), in_specs=[BS((TR, cols), lambda i: (i, 0))],
        out_specs=BS((1, cols), lambda i: (0, 0)), out_shape=_sds((1, cols), F32),
        compiler_params=_cp(("arbitrary",)), name=name,
    )(a)


_PACK_ROWS = 64


def _pad_to(a, rows, cols=D):
    return jnp.pad(a, ((0, rows - a.shape[0]), (0, cols - a.shape[1])))


def _pack_small(sm):
    parts = [
        jnp.concatenate([sm["norm1_g0"], sm["norm1_g1"]], 0),
        jnp.concatenate([sm["mem_norm_g0"], sm["mem_norm_g1"]], 0),
        jnp.concatenate([sm["norm2_g0"], sm["norm2_g1"]], 0),
        _pad_to(sm["a_q_g"], 1), _pad_to(sm["a_k_g"], 1),
        _pad_to(jnp.concatenate([sm["mq_g0"], sm["mq_g1"]], 0), 2),
        _pad_to(jnp.concatenate([sm["mk_g0"], sm["mk_g1"]], 0), 2),
        _pad_to(sm["conv_b"], 1), _pad_to(sm["ln_g"], 1), _pad_to(sm["ln_b"], 1),
        _pad_to(sm["b_in_u"][:, :D], 1), _pad_to(sm["b_in_u"][:, D:], 1),
        _pad_to(sm["b_in_qm"], 1),
        _pad_to(sm["conv_w"][:CONVW], CONVW),
        sm["rel_u"].reshape(12, D),
    ]
    pack = jnp.concatenate(parts, 0)
    return jnp.pad(pack, ((0, _PACK_ROWS - pack.shape[0]), (0, 0)))


def _rel_table_to_u(rel_bias):
    flat = jnp.concatenate([jnp.broadcast_to(rel_bias[:, 191:192], (12, 447)), rel_bias[:, ::-1]], axis=1)
    return jnp.pad(flat, ((0, 0), (192, 1024 - 192 - 639))).reshape(12, 1, 1024)


def _u_to_rel_table(du):
    flat = du[:, 192:192 + 639]
    g = flat[:, 447:][:, ::-1]
    return g, flat[:, :447]


def kernel(x, mem, norm1_g, mem_norm_g, a_w_in, a_q_g, a_k_g, a_rel_bias, b_w_in, b_b_in, b_conv_w, b_conv_b, b_ln_g, b_ln_b, mq_g, mk_g, w_mem_kv, w_out, norm2_g, w_gate, w_up, w_down, loss_target, m_norm1_g, m_mem_norm_g, m_a_w_in, m_a_q_g, m_a_k_g, m_a_rel_bias, m_b_w_in, m_b_b_in, m_b_conv_w, m_b_conv_b, m_b_ln_g, m_b_ln_b, m_mq_g, m_mk_g, m_w_mem_kv, m_w_out, m_norm2_g, m_w_gate, m_w_up, m_w_down, v_norm1_g, v_mem_norm_g, v_a_w_in, v_a_q_g, v_a_k_g, v_a_rel_bias, v_b_w_in, v_b_b_in, v_b_conv_w, v_b_conv_b, v_b_ln_g, v_b_ln_b, v_mq_g, v_mk_g, v_w_mem_kv, v_w_out, v_norm2_g, v_w_gate, v_w_up, v_w_down):
    sx = 2 * lax.axis_index("x") + lax.axis_index("y")

    n_in = (NA // NSH, NBW // NSH)
    tr = lambda a: jnp.swapaxes(a, -1, -2)
    small_src = jnp.concatenate([
        jnp.pad(b_b_in, ((0, 0), (0, 512 - 448))),
        jnp.pad(b_conv_w[0], ((0, 0), (0, 512 - 192))),
        jnp.pad(jnp.concatenate([b_conv_b, b_ln_g, b_ln_b], 0), ((0, 0), (0, 512 - 192))),
        jnp.zeros((5, 512), F32)], 0)

    def gather_groups(groups, small=None):
        def src_of(l, name):
            if name == "in":
                return [_cast_bf16([tr(b_w_in[0])], "cast_in_1").reshape(1, 2, n_in[1] // 2, D) if l else
                        _cast_bf16([a_w_in[0]], "cast_in_0").reshape(1, 2, D // 2, n_in[0])]
            if name == "gu":
                return [_cast_bf16([tr(w_gate[l])], f"cast_gate_{l}").reshape(1, 2, FS // 2, D),
                        _cast_bf16([tr(w_up[l])], f"cast_up_{l}").reshape(1, 2, FS // 2, D)]
            arr, shape = {"kv": (w_mem_kv, (1, 2, 128, 2 * MEMW)), "wo": (w_out, (1, 2, 128, D)),
                          "wd": (w_down, (1, 2, FS // 2, D))}[name]
            return [_cast_bf16([arr[l]], f"cast_{name}_{l}").reshape(shape)]

        def dst_of(l, name):
            if name == "in":
                return (1, NSH, 2, n_in[1] // 2, D) if l else (1, NSH, 2, D // 2, n_in[0])
            return {"kv": (1, NSH, 2, 128, 2 * MEMW), "wo": (1, NSH, 2, 128, D),
                    "gu": (1, 2, NSH, 2, FS // 2, D), "wd": (1, NSH, 2, FS // 2, D)}[name]

        items = [(l, name) for l, names in groups for name in names]
        srcs, views = [], []
        for k, (l, name) in enumerate(items):
            srcs += src_of(l, name)
            if name == "gu":
                views += [lambda d, k=k: d[k].at[:, 0], lambda d, k=k: d[k].at[:, 1]]
            else:
                views.append(lambda d, k=k: d[k])

        def done(w, outs):
            for k, (l, name) in enumerate(items):
                if name == "in" and l == 0:
                    w["a"] = outs[k].reshape(NSH, D, n_in[0]).transpose(1, 0, 2).reshape(D, NA)
                elif name == "in":
                    w["b"] = outs[k].reshape(NBW, D)
                else:
                    shape = {"kv": (D, 2 * MEMW), "wo": (D, D), "gu": (2, FF, D), "wd": (FF, D)}
                    w[name][l] = outs[k].reshape(shape[name])

        return _gather_exchange(srcs, [dst_of(l, name) for l, name in items], views, small), done

    w = dict(a=None, b=None, kv=[None, None], wo=[None, None], gu=[None, None], wd=[None, None])
    first, first_done = gather_groups([(0, ["in", "kv"])], small_src)
    outs0 = _run_exchange(first, "gather_weights_first")
    first_done(w, outs0)
    small_all = outs0[2]
    carries = {"inproj_a": gather_groups([(0, ["wo", "wd"])]),
               "attn_fwd": gather_groups([(0, ["gu"]), (1, ["in", "kv", "wo"])]),
               "ffn_up_0": gather_groups([(1, ["gu"])]),
               "ffn_down_0": gather_groups([(1, ["wd"])])}

    conv_w_full = small_all[:, 1:1 + CONVW, :192].transpose(1, 0, 2).reshape(CONVW, TOK)
    vec3 = small_all[:, 32:35, :192].transpose(1, 0, 2).reshape(3, TOK)
    p = dict(
        norm1_g=norm1_g, mem_norm_g=mem_norm_g, norm2_g=norm2_g,
        a_q_g2=jnp.tile(a_q_g, (1, AH)), a_k_g2=jnp.tile(a_k_g, (1, AH)),
        mq_g4=jnp.tile(mq_g, (1, 4)), mk_g4=jnp.tile(mk_g, (1, 4)),
        rel_u=_rel_table_to_u(a_rel_bias[0]),
        b_b_in=small_all[:, 0, :448].reshape(1, NBW),
        conv_w=jnp.pad(conv_w_full, ((0, 1), (0, 0))),
        conv_b=vec3[0:1], ln_g=vec3[1:2], ln_b=vec3[2:3])

    def pair_sums(items, big, name):
        units = []
        for l, tensor in items:
            if tensor == "in" and l == 0:
                g = big["a"].reshape(D, NSH, n_in[0]).transpose(1, 0, 2).reshape(1, NSH, 2, D // 2, n_in[0])
            elif tensor == "in":
                g = big["b"].reshape(1, NSH, 2, n_in[1] // 2, D)
            else:
                shape = {"kv": (1, NSH, 2, 128, 2 * MEMW), "wo": (1, NSH, 2, 128, D),
                         "gu": (2, NSH, 2, FS // 2, D), "wd": (1, NSH, 2, FS // 2, D)}
                g = big[tensor][l].reshape(shape[tensor])
            units.append(g)
        return _pair_reduce(units, name)

    early_groups = [[(1, "gu")], [(0, "gu")],
                    [(1, "in"), (1, "kv"), (1, "wo"), (1, "wd"), (0, "kv"), (0, "wo"), (0, "wd")]]
    early = [item for grp in early_groups for item in grp]
    late = [(0, "in")]
    own_early, own_late = [], []

    def bwd_carry_fn(big):
        sums_b = []
        for k, grp in enumerate(early_groups):
            own, sb = pair_sums(grp, big, f"pair_reduce_early_{k}")
            own_early.extend(own)
            sums_b.extend(sb)
        return _chip_exchange(sums_b)

    def late_carry_fn(big):
        own, sb = pair_sums(late, big, "pair_reduce_late")
        own_late.extend(own)
        return _chip_exchange(sb)

    loss, grad_x, big, small, parts_early, parts_late = _local_step(
        x.reshape(T, D), mem.reshape(NB * MEMT, D), loss_target.reshape(T, D), w, p,
        carries=carries, bwd_carry_fn=bwd_carry_fn, late_carry_fn=late_carry_fn)
    loss = lax.psum(loss[0, 0], ("x", "y", "c"))
    items = early + late
    halves = [_quad_sum(o, pt, f"quad_sum_{name}_{l}")
              for (l, name), o, pt in zip(items, own_early + own_late, list(parts_early) + list(parts_late))]
    out_shapes = [(1, D, NA // NSH), (1, NBW // NSH, D), (2, 2 * 128, 2 * MEMW), (2, 2 * 128, D),
                  (2, FS, D), (2, FS, D), (2, FS, D)]
    target_of = {"in": lambda l: [(0, l, 0)], "kv": lambda l: [(0, 2, l)], "wo": lambda l: [(0, 3, l)],
                 "gu": lambda l: [(0, 4, l), (1, 5, l)], "wd": lambda l: [(0, 6, l)]}
    targets = [target_of[name](l) for l, name in items]
    g_a, g_b, g_kv, g_wo, g_gate, g_up, g_wd = _final_exchange(halves, out_shapes, targets)

    tot = _small_allreduce(_pack_small(small))
    g_rel, clip_part = _u_to_rel_table(tot[49:61])
    g_rel = jnp.concatenate([g_rel[:, :191], g_rel[:, 191:] + _rowsum(clip_part)], axis=1)
    b_in_full = jnp.concatenate([tot[15:16], tot[16:17, :512], tot[17:18, :MEMW]], axis=1)
    g_small = dict(
        norm1_g=tot[0:2], mem_norm_g=tot[2:4], norm2_g=tot[4:6],
        a_q_g=tot[6:7, :HD], a_k_g=tot[7:8, :HD], a_rel_bias=g_rel[None],
        b_b_in=lax.dynamic_slice(b_in_full, (0, sx * 448), (1, 448)),
        b_conv_w=lax.dynamic_slice(tot[18:49, :TOK], (0, sx * 192), (CONVW, 192))[None],
        b_conv_b=lax.dynamic_slice(tot[12:13, :TOK], (0, sx * 192), (1, 192)),
        b_ln_g=lax.dynamic_slice(tot[13:14, :TOK], (0, sx * 192), (1, 192)),
        b_ln_b=lax.dynamic_slice(tot[14:15, :TOK], (0, sx * 192), (1, 192)),
        mq_g=tot[8:10, :HD], mk_g=tot[10:12, :HD])

    names = ["norm1_g", "mem_norm_g", "a_w_in", "a_q_g", "a_k_g", "a_rel_bias", "b_w_in", "b_b_in",
             "b_conv_w", "b_conv_b", "b_ln_g", "b_ln_b", "mq_g", "mk_g", "w_mem_kv", "w_out",
             "norm2_g", "w_gate", "w_up", "w_down"]
    weights = dict(zip(names, [norm1_g, mem_norm_g, a_w_in, a_q_g, a_k_g, a_rel_bias, b_w_in, b_b_in,
                               b_conv_w, b_conv_b, b_ln_g, b_ln_b, mq_g, mk_g, w_mem_kv, w_out,
                               norm2_g, w_gate, w_up, w_down]))
    ms = dict(zip(names, [m_norm1_g, m_mem_norm_g, m_a_w_in, m_a_q_g, m_a_k_g, m_a_rel_bias, m_b_w_in,
                          m_b_b_in, m_b_conv_w, m_b_conv_b, m_b_ln_g, m_b_ln_b, m_mq_g, m_mk_g,
                          m_w_mem_kv, m_w_out, m_norm2_g, m_w_gate, m_w_up, m_w_down]))
    vs = dict(zip(names, [v_norm1_g, v_mem_norm_g, v_a_w_in, v_a_q_g, v_a_k_g, v_a_rel_bias, v_b_w_in,
                          v_b_b_in, v_b_conv_w, v_b_conv_b, v_b_ln_g, v_b_ln_b, v_mq_g, v_mk_g,
                          v_w_mem_kv, v_w_out, v_norm2_g, v_w_gate, v_w_up, v_w_down]))
    grads = dict(g_small)
    grads.update(a_w_in=g_a, b_w_in=g_b, w_mem_kv=g_kv, w_out=g_wo, w_gate=g_gate, w_up=g_up, w_down=g_wd)
    big_names = ["a_w_in", "b_w_in", "w_mem_kv", "w_out", "w_gate", "w_up", "w_down"]
    small_names = [n for n in names if n not in big_names]
    delta, new_m, new_v = {}, {}, {}
    for n in big_names:
        if n in ("b_w_in", "w_gate", "w_up"):
            outs = _adamw_big(tr(weights[n]), grads[n], tr(ms[n]), tr(vs[n]), f"adamw_{n}")
            delta[n], new_m[n], new_v[n] = [tr(o) for o in outs]
            grads[n] = tr(grads[n])
        else:
            delta[n], new_m[n], new_v[n] = _adamw_big(weights[n], grads[n], ms[n], vs[n], f"adamw_{n}")
    as2d = lambda a: a.reshape(-1, a.shape[-1])
    d_s, m_s, v_s = _adamw_small([as2d(weights[n]) for n in small_names], [as2d(grads[n]) for n in small_names],
                                 [as2d(ms[n]) for n in small_names], [as2d(vs[n]) for n in small_names])
    for i, n in enumerate(small_names):
        delta[n] = d_s[i].reshape(weights[n].shape)
        new_m[n] = m_s[i].reshape(weights[n].shape)
        new_v[n] = v_s[i].reshape(weights[n].shape)

    return (loss, grad_x.reshape(NB, SEQ, D), *[grads[n] for n in names], *[delta[n] for n in names],
            *[new_m[n] for n in names], *[new_v[n] for n in names])


def _rowsum(a):
    def body(a_ref, o_ref):
        o_ref[...] = jnp.sum(a_ref[...], axis=1, keepdims=True)

    vm = BS(memory_space=pltpu.VMEM)
    return pl.pallas_call(body, in_specs=[vm], out_specs=vm, out_shape=_sds((a.shape[0], 1), F32),
                          compiler_params=_cp(), name="rowsum")(a)
```

```python
import functools

import jax
import jax.numpy as jnp
from jax import lax
from jax.experimental import pallas as pl
from jax.experimental.pallas import tpu as pltpu

F32 = jnp.float32
BF16 = jnp.bfloat16
BS = pl.BlockSpec
ANY = pl.BlockSpec(memory_space=pl.ANY)
MESH = pl.DeviceIdType.MESH

D = 1024
SEQ = 2048
NB = 2
T = NB * SEQ
MEMT = 256
HD = 64
TOK = 768
MEMW = 256
NA = 3 * TOK + MEMW
NBW = 2 * TOK + MEMW
FF = 2816
NSH = 4
FS = FF // NSH
FT = FF // 2
CONVW = 31
EPS = 1e-6
NEG = -1e30
SCALE = HD ** -0.5
QB = 256
KWIN = 768
KPAD = 512
TR = 512

ADAM_LR = 0.001
ADAM_B1 = 0.9
ADAM_B2 = 0.999
ADAM_EPS = 1e-08
ADAM_WD = 0.01
ADAM_STEP = 10

_DIMS = {
    "nn": (((1,), (0,)), ((), ())),
    "nt": (((1,), (1,)), ((), ())),
    "tn": (((0,), (0,)), ((), ())),
}


def _cp(sem=None, vmem_mb=48):
    return pltpu.CompilerParams(dimension_semantics=sem, vmem_limit_bytes=vmem_mb << 20)


def _sds(shape, dtype):
    return jax.ShapeDtypeStruct(tuple(shape), dtype)


def _mm(mode, a, b, *, grid, a_spec, b_spec, out_shape, out_spec, acc_shape, name,
        extras=(), extra_specs=(), epilogue=None, carry=None, vmem_mb=48, sequential=False):
    n_ex = len(extras)
    nk = grid[2]
    dims = _DIMS[mode]
    ni = len(carry.ins) if carry else 0
    no = len(carry.out_shapes) if carry else 0
    multi = isinstance(out_shape, (list, tuple))
    out_shapes = list(out_shape) if multi else [out_shape]
    out_specs = list(out_spec) if multi else [out_spec]
    n_o = len(out_shapes)

    def body(a_ref, b_ref, *rest):
        ex = rest[:n_ex]
        cin = rest[n_ex:n_ex + ni]
        o_refs = rest[n_ex + ni:n_ex + ni + n_o]
        o_ref = o_refs if multi else o_refs[0]
        cout = rest[n_ex + ni + n_o:n_ex + ni + n_o + no]
        acc = rest[n_ex + ni + n_o + no]
        cscr = rest[n_ex + ni + n_o + no + 1:]
        ids = [pl.program_id(d) for d in range(3)]
        k = ids[2]
        if carry:
            @pl.when((ids[0] == 0) & (ids[1] == 0) & (ids[2] == 0))
            def _():
                carry.start(cin, cout, cscr)

        prod = lax.dot_general(a_ref[...].astype(BF16), b_ref[...].astype(BF16), dims,
                               preferred_element_type=F32)

        def finish(val):
            if epilogue is None:
                o_ref[...] = val.astype(o_ref.dtype)
            else:
                epilogue(val, ex, o_ref, ids[0])

        if nk == 1:
            finish(prod)
        else:
            @pl.when(k == 0)
            def _():
                acc[...] = prod

            @pl.when((k > 0) & (k < nk - 1))
            def _():
                acc[...] += prod

            @pl.when(k == nk - 1)
            def _():
                finish(acc[...] + prod)

        if carry:
            @pl.when((ids[0] == grid[0] - 1) & (ids[1] == grid[1] - 1) & (ids[2] == grid[2] - 1))
            def _():
                carry.finish(cin, cout, cscr)

    acc_scratch = pltpu.VMEM(acc_shape if nk > 1 else (8, 128), F32)
    ordered = sequential or bool(carry)
    outs = pl.pallas_call(
        body, grid=grid,
        in_specs=[a_spec, b_spec, *extra_specs] + (carry.in_specs if carry else []),
        out_specs=out_specs + [ANY] * no, out_shape=out_shapes + (carry.out_shapes if carry else []),
        scratch_shapes=[acc_scratch] + (carry.scratch if carry else []),
        compiler_params=pltpu.CompilerParams(
            dimension_semantics=("arbitrary",) * 3 if ordered else ("parallel", "parallel", "arbitrary"),
            vmem_limit_bytes=vmem_mb << 20, has_side_effects=bool(carry)), name=name,
    )(a, b, *extras, *(carry.ins if carry else []))
    mine = list(outs[:n_o]) if multi else outs[0]
    return (mine, outs[n_o:]) if carry else mine


def _rms_fwd(x, g, name):
    rows = x.shape[0]

    def body(x_ref, g_ref, o_ref):
        xv = x_ref[...]
        r = lax.rsqrt(jnp.mean(xv * xv, axis=-1, keepdims=True) + EPS)
        o_ref[...] = (xv * r * g_ref[...]).astype(BF16)

    return pl.pallas_call(
        body, grid=(rows // TR,),
        in_specs=[BS((TR, D), lambda i: (i, 0)), BS((1, D), lambda i: (0, 0))],
        out_specs=BS((TR, D), lambda i: (i, 0)), out_shape=_sds((rows, D), BF16),
        compiler_params=_cp(("arbitrary",)), name=name,
    )(x, g)


def _rms_bwd(dh, x, g, dres, name):
    rows = x.shape[0]
    has_res = dres is not None

    def body(*refs):
        if has_res:
            dh_ref, x_ref, g_ref, r_ref, dx_ref, dg_ref = refs
        else:
            dh_ref, x_ref, g_ref, dx_ref, dg_ref = refs
        xv = x_ref[...]
        dhv = dh_ref[...]
        r = lax.rsqrt(jnp.mean(xv * xv, axis=-1, keepdims=True) + EPS)
        xh = xv * r
        gy = dhv * g_ref[...]
        dx = r * (gy - xh * jnp.mean(gy * xh, axis=-1, keepdims=True))
        if has_res:
            dx = dx + r_ref[...]
        dx_ref[...] = dx

        @pl.when(pl.program_id(0) == 0)
        def _():
            dg_ref[...] = jnp.zeros_like(dg_ref)

        dg_ref[...] += jnp.sum(dhv * xh, axis=0, keepdims=True)

    row = BS((TR, D), lambda i: (i, 0))
    vec = BS((1, D), lambda i: (0, 0))
    ins = [dh, x, g] + ([dres] if has_res else [])
    return pl.pallas_call(
        body, grid=(rows // TR,),
        in_specs=[row, row, vec] + ([row] if has_res else []),
        out_specs=[row, vec], out_shape=[_sds((rows, D), F32), _sds((1, D), F32)],
        compiler_params=_cp(("arbitrary",)), name=name,
    )(*ins)


def _group_masks(width):
    lane = lax.broadcasted_iota(jnp.int32, (1, width), 1)
    return [(lane >= HD * h) & (lane < HD * (h + 1)) for h in range(width // HD)]


def _group_mean(v, masks):
    del masks
    low = lax.broadcasted_iota(jnp.int32, (1, 128), 1) < HD
    slabs = []
    for j in range(v.shape[-1] // 128):
        x = v[:, 128 * j:128 * (j + 1)]
        s0 = jnp.sum(jnp.where(low, x, 0.0), axis=-1, keepdims=True) * (1.0 / HD)
        s1 = jnp.sum(jnp.where(low, 0.0, x), axis=-1, keepdims=True) * (1.0 / HD)
        slabs.append(jnp.where(low, s0, s1))
    return slabs[0] if len(slabs) == 1 else jnp.concatenate(slabs, axis=-1)


def _head_norm(zv, g, masks):
    r = lax.rsqrt(_group_mean(zv * zv, masks) + EPS)
    return zv * r * g


def _head_norm_bwd(dy, zv, g, masks):
    r = lax.rsqrt(_group_mean(zv * zv, masks) + EPS)
    zh = zv * r
    gy = dy * g
    dz = r * (gy - zh * _group_mean(gy * zh, masks))
    return dz, jnp.sum(dy * zh, axis=0, keepdims=True)


def _fold_heads(v, width):
    vb = jnp.broadcast_to(v, (8, width))
    out = vb
    for h in range(1, width // HD):
        out = out + pltpu.roll(vb, width - HD * h, axis=1)
    return out[0:1]


def _bias_expand(u):
    def body(u_ref, o_ref):
        x = jnp.broadcast_to(u_ref[...], (QB, 1024))
        rolled = pltpu.roll(x, 1024 - (QB - 1), axis=1, stride=1, stride_axis=0)[:, :KWIN]
        row = lax.broadcasted_iota(jnp.int32, (QB, 1), 0)
        col = lax.broadcasted_iota(jnp.int32, (1, KWIN), 1)
        lo = (row // 64) * 64
        ok = (col >= lo) & (col < lo + 576)
        o_ref[...] = jnp.where(ok, rolled, NEG)

    return pl.pallas_call(
        body, grid=(12,), in_specs=[BS((None, 1, 1024), lambda h: (h, 0, 0))],
        out_specs=BS((None, QB, KWIN), lambda h: (h, 0, 0)), out_shape=_sds((12, QB, KWIN), F32),
        compiler_params=_cp(("arbitrary",)), name="bias_expand",
    )(u)


def _bias_reduce(ds):
    def body(d_ref, o_ref):
        ri = lax.broadcasted_iota(jnp.int32, (QB, QB), 0)
        ci = lax.broadcasted_iota(jnp.int32, (QB, QB), 1)
        flip = (ri + ci == QB - 1).astype(F32)
        drev = jnp.dot(flip, d_ref[...], precision=lax.Precision.HIGHEST, preferred_element_type=F32)
        x = jnp.concatenate([drev, jnp.zeros((QB, 1024 - KWIN), F32)], axis=1)
        rolled = pltpu.roll(x, 0, axis=1, stride=1, stride_axis=0)
        o_ref[...] = jnp.sum(rolled, axis=0, keepdims=True)

    return pl.pallas_call(
        body, grid=(12,), in_specs=[BS((None, QB, KWIN), lambda h: (h, 0, 0))],
        out_specs=BS((None, 1, 1024), lambda h: (h, 0, 0)), out_shape=_sds((12, 1, 1024), F32),
        compiler_params=_cp(("arbitrary",)), name="bias_reduce",
    )(ds)


AW = 256
AH = AW // HD
AG = TOK // AW
def _attn_softmax(qh, kw, bias, startadd):
    s = lax.dot_general(qh, kw, _DIMS["nt"], preferred_element_type=F32) + bias + startadd
    m = jnp.max(s, axis=-1, keepdims=True)
    p = jnp.exp(s - m)
    return p * (1.0 / jnp.sum(p, axis=-1, keepdims=True))


def _attn_prologue(q_ref, k_ref, v_ref, gq_ref, gk_ref, qn_s, kn_s, v_s, masks):
    kn_s[0:KPAD, :] = jnp.zeros((KPAD, AW), BF16)
    v_s[0:KPAD, :] = jnp.zeros((KPAD, AW), BF16)
    for r in range(0, SEQ, TR):
        qn_s[r:r + TR, :] = (_head_norm(q_ref[r:r + TR, :], gq_ref[...], masks) * SCALE).astype(BF16)
        kn_s[KPAD + r:KPAD + r + TR, :] = _head_norm(k_ref[r:r + TR, :], gk_ref[...], masks).astype(BF16)
        v_s[KPAD + r:KPAD + r + TR, :] = v_ref[r:r + TR, :].astype(BF16)


def _attn_fwd(z, bias, gq2, gk2, carry=None):
    ni = len(carry.ins) if carry else 0
    no = len(carry.out_shapes) if carry else 0

    def body(q_ref, k_ref, v_ref, b_ref, gq_ref, gk_ref, *rest):
        cin, o_ref, cout = rest[:ni], rest[ni], rest[ni + 1:ni + 1 + no]
        qn_s, kn_s, v_s = rest[ni + 1 + no:ni + 4 + no]
        cscr = rest[ni + 4 + no:]
        if carry:
            @pl.when((pl.program_id(0) == 0) & (pl.program_id(1) == 0))
            def _():
                carry.start(cin, cout, cscr)

        masks = _group_masks(AW)
        _attn_prologue(q_ref, k_ref, v_ref, gq_ref, gk_ref, qn_s, kn_s, v_s, masks)
        col = lax.broadcasted_iota(jnp.int32, (1, KWIN), 1)

        def blk(i, carry):
            r0 = pl.multiple_of(i * QB, QB)
            qb = qn_s[pl.ds(r0, QB), :]
            kw = kn_s[pl.ds(r0, KWIN), :]
            vw = v_s[pl.ds(r0, KWIN), :]
            startadd = jnp.where(col + r0 < KPAD, NEG, 0.0)
            o = jnp.zeros((QB, AW), F32)
            for h in range(AH):
                qh = jnp.where(masks[h], qb, jnp.zeros_like(qb))
                vh = jnp.where(masks[h], vw, jnp.zeros_like(vw))
                p = _attn_softmax(qh, kw, b_ref[h], startadd).astype(BF16)
                o = o + jnp.dot(p, vh, preferred_element_type=F32)
            o_ref[pl.ds(r0, QB), :] = o.astype(BF16)
            return carry

        lax.fori_loop(0, SEQ // QB, blk, 0)

        if carry:
            @pl.when((pl.program_id(0) == NB - 1) & (pl.program_id(1) == AG - 1))
            def _():
                carry.finish(cin, cout, cscr)

    vec = BS((1, AW), lambda b, hp: (0, 0))
    outs = pl.pallas_call(
        body, grid=(NB, AG),
        in_specs=[BS((SEQ, AW), lambda b, hp: (b, hp)),
                  BS((SEQ, AW), lambda b, hp: (b, AG + hp)),
                  BS((SEQ, AW), lambda b, hp: (b, 2 * AG + hp)),
                  BS((AH, QB, KWIN), lambda b, hp: (hp, 0, 0)), vec, vec]
        + (carry.in_specs if carry else []),
        out_specs=[BS((SEQ, AW), lambda b, hp: (b, hp))] + [ANY] * no,
        out_shape=[_sds((T, D), BF16)] + (carry.out_shapes if carry else []),
        scratch_shapes=[pltpu.VMEM((SEQ, AW), BF16), pltpu.VMEM((SEQ + KPAD, AW), BF16),
                        pltpu.VMEM((SEQ + KPAD, AW), BF16)] + (carry.scratch if carry else []),
        compiler_params=pltpu.CompilerParams(
            dimension_semantics=("arbitrary", "arbitrary"), vmem_limit_bytes=48 << 20,
            has_side_effects=bool(carry)), name="attn_fwd",
    )(z, z, z, bias, gq2, gk2, *(carry.ins if carry else []))
    return outs[0], outs[1:]


def _attn_bwd(z, dcat, bias, gq2, gk2, carry=None):
    ni = len(carry.ins) if carry else 0
    no = len(carry.out_shapes) if carry else 0

    def body(q_ref, k_ref, v_ref, do_ref, b_ref, gq_ref, gk_ref, *rest):
        cin = rest[:ni]
        dq_ref, dk_ref, dv_ref, db_ref, dgq_ref, dgk_ref = rest[ni:ni + 6]
        cout = rest[ni + 6:ni + 6 + no]
        qn_s, kn_s, v_s, dqn_s, dkn_s, dv_s = rest[ni + 6 + no:ni + 12 + no]
        cscr = rest[ni + 12 + no:]
        hp = pl.program_id(0)
        b = pl.program_id(1)
        if carry:
            @pl.when((hp == 0) & (b == 0))
            def _():
                carry.start(cin, cout, cscr)

        masks = _group_masks(AW)
        _attn_prologue(q_ref, k_ref, v_ref, gq_ref, gk_ref, qn_s, kn_s, v_s, masks)
        dkn_s[...] = jnp.zeros_like(dkn_s)
        dv_s[...] = jnp.zeros_like(dv_s)

        @pl.when(b == 0)
        def _():
            db_ref[...] = jnp.zeros_like(db_ref)

        @pl.when((b == 0) & (hp == 0))
        def _():
            dgq_ref[...] = jnp.zeros_like(dgq_ref)
            dgk_ref[...] = jnp.zeros_like(dgk_ref)

        col = lax.broadcasted_iota(jnp.int32, (1, KWIN), 1)

        def blk(i, carry):
            r0 = pl.multiple_of(i * QB, QB)
            qb = qn_s[pl.ds(r0, QB), :]
            kw = kn_s[pl.ds(r0, KWIN), :]
            vw = v_s[pl.ds(r0, KWIN), :]
            dob = do_ref[pl.ds(r0, QB), :].astype(BF16)
            startadd = jnp.where(col + r0 < KPAD, NEG, 0.0)
            dqn = jnp.zeros((QB, AW), F32)
            dkw = jnp.zeros((KWIN, AW), F32)
            dvw = jnp.zeros((KWIN, AW), F32)
            for h in range(AH):
                qh = jnp.where(masks[h], qb, jnp.zeros_like(qb))
                kh = jnp.where(masks[h], kw, jnp.zeros_like(kw))
                doh = jnp.where(masks[h], dob, jnp.zeros_like(dob))
                p = _attn_softmax(qh, kw, b_ref[h], startadd)
                dvw = dvw + lax.dot_general(p.astype(BF16), doh, _DIMS["tn"],
                                            preferred_element_type=F32)
                dp = lax.dot_general(doh, vw, _DIMS["nt"], preferred_element_type=F32)
                ds = p * (dp - jnp.sum(dp * p, axis=-1, keepdims=True))
                db_ref[h] += ds
                dsb = ds.astype(BF16)
                dqn = dqn + jnp.dot(dsb, kh, preferred_element_type=F32)
                dkw = dkw + lax.dot_general(dsb, qh, _DIMS["tn"], preferred_element_type=F32)
            dqn_s[pl.ds(r0, QB), :] = dqn * SCALE
            dkn_s[pl.ds(r0, KWIN), :] += dkw
            dv_s[pl.ds(r0, KWIN), :] += dvw
            return carry

        lax.fori_loop(0, SEQ // QB, blk, 0)

        dgq = jnp.zeros((1, AW), F32)
        dgk = jnp.zeros((1, AW), F32)
        for r in range(0, SEQ, TR):
            dq, dg = _head_norm_bwd(dqn_s[r:r + TR, :], q_ref[r:r + TR, :], gq_ref[...], masks)
            dq_ref[r:r + TR, :] = dq.astype(BF16)
            dgq = dgq + dg
            dk, dg = _head_norm_bwd(dkn_s[KPAD + r:KPAD + r + TR, :], k_ref[r:r + TR, :], gk_ref[...], masks)
            dk_ref[r:r + TR, :] = dk.astype(BF16)
            dgk = dgk + dg
            dv_ref[r:r + TR, :] = dv_s[KPAD + r:KPAD + r + TR, :].astype(BF16)
        dgq_ref[...] += _fold_heads(dgq, AW)
        dgk_ref[...] += _fold_heads(dgk, AW)

        if carry:
            @pl.when((hp == AG - 1) & (b == NB - 1))
            def _():
                carry.finish(cin, cout, cscr)

    vec = BS((1, AW), lambda hp, b: (0, 0))
    row = BS((SEQ, AW), lambda hp, b: (b, hp))
    outs = pl.pallas_call(
        body, grid=(AG, NB),
        in_specs=[row,
                  BS((SEQ, AW), lambda hp, b: (b, AG + hp)),
                  BS((SEQ, AW), lambda hp, b: (b, 2 * AG + hp)),
                  row,
                  BS((AH, QB, KWIN), lambda hp, b: (hp, 0, 0)), vec, vec]
        + (carry.in_specs if carry else []),
        out_specs=[row, row, row, BS((AH, QB, KWIN), lambda hp, b: (hp, 0, 0)), vec, vec] + [ANY] * no,
        out_shape=[_sds((T, TOK), BF16), _sds((T, TOK), BF16), _sds((T, TOK), BF16),
                   _sds((12, QB, KWIN), F32), _sds((1, AW), F32), _sds((1, AW), F32)]
        + (carry.out_shapes if carry else []),
        scratch_shapes=[pltpu.VMEM((SEQ, AW), BF16), pltpu.VMEM((SEQ + KPAD, AW), BF16),
                        pltpu.VMEM((SEQ + KPAD, AW), BF16), pltpu.VMEM((SEQ, AW), F32),
                        pltpu.VMEM((SEQ + KPAD, AW), F32), pltpu.VMEM((SEQ + KPAD, AW), F32)]
        + (carry.scratch if carry else []),
        compiler_params=pltpu.CompilerParams(
            dimension_semantics=("arbitrary", "arbitrary"), vmem_limit_bytes=58 << 20,
            has_side_effects=bool(carry)), name="attn_bwd",
    )(z, z, z, dcat, bias, gq2, gk2, *(carry.ins if carry else []))
    return outs[:6], outs[6:]


def _mem_softmax(qh, kn):
    s = lax.dot_general(qh, kn, _DIMS["nt"], preferred_element_type=F32)
    m = jnp.max(s, axis=-1, keepdims=True)
    p = jnp.exp(s - m)
    return p * (1.0 / jnp.sum(p, axis=-1, keepdims=True))


def _memattn_fwd(z, kv, cat, gq4, gk4, qcol, name):
    def body(q_ref, k_ref, v_ref, gq_ref, gk_ref, cat_ref, o_ref):
        del cat_ref
        masks = _group_masks(MEMW)
        qn = (_head_norm(q_ref[...], gq_ref[...], masks) * SCALE).astype(BF16)
        kn = _head_norm(k_ref[...], gk_ref[...], masks).astype(BF16)
        vv = v_ref[...].astype(BF16)
        o = jnp.zeros((TR, MEMW), F32)
        for h in range(4):
            qh = jnp.where(masks[h], qn, jnp.zeros_like(qn))
            vh = jnp.where(masks[h], vv, jnp.zeros_like(vv))
            p = _mem_softmax(qh, kn).astype(BF16)
            o = o + jnp.dot(p, vh, preferred_element_type=F32)
        o_ref[...] = o.astype(BF16)

    nt = SEQ // TR
    vec = BS((1, MEMW), lambda b, t: (0, 0))
    return pl.pallas_call(
        body, grid=(NB, nt),
        in_specs=[BS((TR, MEMW), lambda b, t: (b * nt + t, qcol)),
                  BS((MEMT, MEMW), lambda b, t: (b, 0)),
                  BS((MEMT, MEMW), lambda b, t: (b, 1)), vec, vec, ANY],
        out_specs=BS((TR, MEMW), lambda b, t: (b * nt + t, 3)),
        out_shape=_sds((T, D), BF16), input_output_aliases={5: 0},
        compiler_params=_cp(("arbitrary", "arbitrary")), name=name,
    )(z, kv, kv, gq4, gk4, cat)


def _memattn_bwd(z, kv, dcat, gq4, gk4, qcol, name):
    nt = SEQ // TR

    def body(q_ref, k_ref, v_ref, do_ref, gq_ref, gk_ref,
             dq_ref, dkv_ref, dgq_ref, dgk_ref, dkn_s, dv_s):
        b = pl.program_id(0)
        t = pl.program_id(1)
        masks = _group_masks(MEMW)
        qz = q_ref[...]
        kz = k_ref[...]
        qn = (_head_norm(qz, gq_ref[...], masks) * SCALE).astype(BF16)
        kn = _head_norm(kz, gk_ref[...], masks).astype(BF16)
        vv = v_ref[...].astype(BF16)
        dob = do_ref[...].astype(BF16)

        @pl.when(t == 0)
        def _():
            dkn_s[...] = jnp.zeros_like(dkn_s)
            dv_s[...] = jnp.zeros_like(dv_s)

        @pl.when((t == 0) & (b == 0))
        def _():
            dgq_ref[...] = jnp.zeros_like(dgq_ref)
            dgk_ref[...] = jnp.zeros_like(dgk_ref)

        dqn = jnp.zeros((TR, MEMW), F32)
        dkn = jnp.zeros((MEMT, MEMW), F32)
        dvv = jnp.zeros((MEMT, MEMW), F32)
        for h in range(4):
            qh = jnp.where(masks[h], qn, jnp.zeros_like(qn))
            kh = jnp.where(masks[h], kn, jnp.zeros_like(kn))
            doh = jnp.where(masks[h], dob, jnp.zeros_like(dob))
            p = _mem_softmax(qh, kn)
            dvv = dvv + lax.dot_general(p.astype(BF16), doh, _DIMS["tn"], preferred_element_type=F32)
            dp = lax.dot_general(doh, vv, _DIMS["nt"], preferred_element_type=F32)
            ds = p * (dp - jnp.sum(dp * p, axis=-1, keepdims=True))
            dsb = ds.astype(BF16)
            dqn = dqn + jnp.dot(dsb, kh, preferred_element_type=F32)
            dkn = dkn + lax.dot_general(dsb, qh, _DIMS["tn"], preferred_element_type=F32)
        dkn_s[...] += dkn
        dv_s[...] += dvv
        dq, dgq = _head_norm_bwd(dqn * SCALE, qz, gq_ref[...], masks)
        dq_ref[...] = dq.astype(BF16)
        dgq_ref[...] += _fold_heads(dgq, MEMW)

        @pl.when(t == nt - 1)
        def _():
            dk, dgk = _head_norm_bwd(dkn_s[...], kz, gk_ref[...], masks)
            dkv_ref[:, 0:MEMW] = dk
            dkv_ref[:, MEMW:] = dv_s[...]
            dgk_ref[...] += _fold_heads(dgk, MEMW)

    vec = BS((1, MEMW), lambda b, t: (0, 0))
    return pl.pallas_call(
        body, grid=(NB, nt),
        in_specs=[BS((TR, MEMW), lambda b, t: (b * nt + t, qcol)),
                  BS((MEMT, MEMW), lambda b, t: (b, 0)),
                  BS((MEMT, MEMW), lambda b, t: (b, 1)),
                  BS((TR, MEMW), lambda b, t: (b * nt + t, 3)), vec, vec],
        out_specs=[BS((TR, MEMW), lambda b, t: (b * nt + t, 0)),
                   BS((MEMT, 2 * MEMW), lambda b, t: (b, 0)), vec, vec],
        out_shape=[_sds((T, MEMW), BF16), _sds((NB * MEMT, 2 * MEMW), F32),
                   _sds((1, MEMW), F32), _sds((1, MEMW), F32)],
        scratch_shapes=[pltpu.VMEM((MEMT, MEMW), F32), pltpu.VMEM((MEMT, MEMW), F32)],
        compiler_params=_cp(("arbitrary", "arbitrary")), name=name,
    )(z, kv, kv, dcat, gq4, gk4)


HALO = 32
NEXT = 64
RT = 64


def _glu(zz):
    return zz[:, :TOK] * jax.nn.sigmoid(zz[:, TOK:])


def _layer_norm_parts(y):
    mu = jnp.mean(y, axis=-1, keepdims=True)
    yc = y - mu
    rstd = lax.rsqrt(jnp.mean(yc * yc, axis=-1, keepdims=True) + EPS)
    return yc * rstd, rstd


def _shifted_copies(src, dst, rows):
    for b in range(1, 8):
        dst[b - 1, 0:rows, :] = src[b:b + rows, :]


def _tap(src, shifted, off, r0, rows):
    b = off % 8
    if b == 0:
        return src[r0 + off:r0 + off + rows, :]
    return shifted[b - 1, r0 + off - b:r0 + off - b + rows, :]


def _conv_rows(w_ref, hbuf, hs, r0, rows):
    y = jnp.zeros((rows, TOK), F32)
    for j in range(CONVW):
        y = y + w_ref[j:j + 1, :] * _tap(hbuf, hs, (HALO - CONVW + 1) + j, r0, rows)
    return y


def _conv_fwd(z, cw, cb, lg, lb):
    nt = SEQ // TR

    def body(zc_ref, zp_ref, w_ref, cb_ref, lg_ref, lb_ref, o_ref, y_ref, hbuf, hs):
        t = pl.program_id(1)
        hbuf[0:HALO, :] = jnp.where(t == 0, 0.0, _glu(zp_ref[...]))
        hbuf[HALO:, :] = _glu(zc_ref[...])
        _shifted_copies(hbuf, hs, HALO + TR - 8)
        for r0 in range(0, TR, RT):
            y = _conv_rows(w_ref, hbuf, hs, r0, RT) + cb_ref[...]
            y_ref[r0:r0 + RT, :] = y
            yh, _ = _layer_norm_parts(y)
            o = yh * lg_ref[...] + lb_ref[...]
            o_ref[r0:r0 + RT, :] = (o * jax.nn.sigmoid(o)).astype(BF16)

    vec = BS((1, TOK), lambda b, t: (0, 0))
    per = TR // HALO
    return pl.pallas_call(
        body, grid=(NB, nt),
        in_specs=[BS((TR, 2 * TOK), lambda b, t: (b * nt + t, 0)),
                  BS((HALO, 2 * TOK), lambda b, t: (jnp.maximum((b * nt + t) * per - 1, 0), 0)),
                  BS((32, TOK), lambda b, t: (0, 0)), vec, vec, vec],
        out_specs=[BS((TR, TOK), lambda b, t: (b * nt + t, 0)), BS((TR, TOK), lambda b, t: (b * nt + t, 0))],
        out_shape=[_sds((T, D), BF16), _sds((T, TOK), F32)],
        scratch_shapes=[pltpu.VMEM((HALO + TR, TOK), F32), pltpu.VMEM((7, HALO + TR, TOK), F32)],
        compiler_params=_cp(("arbitrary", "arbitrary")), name="conv_fwd",
    )(z, z, cw, cb, lg, lb)


def _conv_bwd(z, y, dcat, cw, lg, lb):
    nt = SEQ // TR
    ext = TR + NEXT

    def body(zc_ref, zp_ref, yc_ref, yn_ref, dc_ref, dn_ref, w_ref, lg_ref, lb_ref,
             du_ref, dw_ref, dcb_ref, dlg_ref, dlb_ref, dbin_ref, hbuf, dybuf, hs, dys):
        b = pl.program_id(0)
        t = pl.program_id(1)

        @pl.when((b == 0) & (t == 0))
        def _():
            dw_ref[...] = jnp.zeros_like(dw_ref)
            dcb_ref[...] = jnp.zeros_like(dcb_ref)
            dlg_ref[...] = jnp.zeros_like(dlg_ref)
            dlb_ref[...] = jnp.zeros_like(dlb_ref)
            dbin_ref[...] = jnp.zeros_like(dbin_ref)

        hbuf[0:HALO, :] = jnp.where(t == 0, 0.0, _glu(zp_ref[...]))
        hbuf[HALO:, :] = _glu(zc_ref[...])
        _shifted_copies(hbuf, hs, HALO + TR - 8)
        last = t == nt - 1
        for r0 in range(0, ext, RT):
            yh, rstd = _layer_norm_parts(yc_ref[r0:r0 + RT, :] if r0 < TR else yn_ref[...])
            o = yh * lg_ref[...] + lb_ref[...]
            sg = jax.nn.sigmoid(o)
            if r0 < TR:
                dtok = dc_ref[r0:r0 + RT, :]
            else:
                dtok = jnp.where(last, 0.0, dn_ref[...])
            do = dtok * (sg * (1.0 + o * (1.0 - sg)))
            dyh = do * lg_ref[...]
            dy = rstd * (dyh - jnp.mean(dyh, axis=-1, keepdims=True)
                         - yh * jnp.mean(dyh * yh, axis=-1, keepdims=True))
            dybuf[r0:r0 + RT, :] = dy
            if r0 < TR:
                dlg_ref[...] += jnp.sum(do * yh, axis=0, keepdims=True)
                dlb_ref[...] += jnp.sum(do, axis=0, keepdims=True)
                dcb_ref[...] += jnp.sum(dy, axis=0, keepdims=True)
        _shifted_copies(dybuf, dys, ext - 8)
        for r0 in range(0, TR, RT):
            dh = jnp.zeros((RT, TOK), F32)
            for j in range(CONVW):
                dh = dh + w_ref[j:j + 1, :] * _tap(dybuf, dys, (CONVW - 1) - j, r0, RT)
            a = zc_ref[r0:r0 + RT, 0:TOK]
            sg = jax.nn.sigmoid(zc_ref[r0:r0 + RT, TOK:])
            da = dh * sg
            dg = dh * a * (sg * (1.0 - sg))
            du_ref[r0:r0 + RT, 0:TOK] = da.astype(BF16)
            du_ref[r0:r0 + RT, TOK:] = dg.astype(BF16)
            dbin_ref[:, 0:TOK] += jnp.sum(da, axis=0, keepdims=True)
            dbin_ref[:, TOK:] += jnp.sum(dg, axis=0, keepdims=True)
        for j in range(CONVW):
            acc = jnp.zeros((8, TOK), F32)
            for r0 in range(0, TR, RT):
                prod = dybuf[r0:r0 + RT, :] * _tap(hbuf, hs, (HALO - CONVW + 1) + j, r0, RT)
                acc = acc + jnp.sum(prod.reshape(RT // 8, 8, TOK), axis=0)
            dw_ref[j:j + 1, :] += jnp.sum(acc, axis=0, keepdims=True)

    vec = BS((1, TOK), lambda b, t: (0, 0))
    perh = TR // HALO
    pern = TR // NEXT
    nlast_n = T // NEXT - 1
    return pl.pallas_call(
        body, grid=(NB, nt),
        in_specs=[BS((TR, 2 * TOK), lambda b, t: (b * nt + t, 0)),
                  BS((HALO, 2 * TOK), lambda b, t: (jnp.maximum((b * nt + t) * perh - 1, 0), 0)),
                  BS((TR, TOK), lambda b, t: (b * nt + t, 0)),
                  BS((NEXT, TOK), lambda b, t: (jnp.minimum((b * nt + t + 1) * pern, nlast_n), 0)),
                  BS((TR, TOK), lambda b, t: (b * nt + t, 0)),
                  BS((NEXT, TOK), lambda b, t: (jnp.minimum((b * nt + t + 1) * pern, nlast_n), 0)),
                  BS((32, TOK), lambda b, t: (0, 0)), vec, vec],
        out_specs=[BS((TR, 2 * TOK), lambda b, t: (b * nt + t, 0)),
                   BS((32, TOK), lambda b, t: (0, 0)), vec, vec, vec,
                   BS((1, 2 * TOK), lambda b, t: (0, 0))],
        out_shape=[_sds((T, 2 * TOK), BF16), _sds((32, TOK), F32), _sds((1, TOK), F32),
                   _sds((1, TOK), F32), _sds((1, TOK), F32), _sds((1, 2 * TOK), F32)],
        scratch_shapes=[pltpu.VMEM((HALO + TR, TOK), F32), pltpu.VMEM((ext, TOK), F32),
                        pltpu.VMEM((7, HALO + TR, TOK), F32), pltpu.VMEM((7, ext, TOK), F32)],
        compiler_params=_cp(("arbitrary", "arbitrary"), vmem_mb=56), name="conv_bwd",
    )(z, z, y, y, dcat, dcat, cw, lg, lb)


def _ffn_up(h2, wgu, l, carry=None):
    ni = len(carry.ins) if carry else 0
    no = len(carry.out_shapes) if carry else 0

    def body(h_ref, wg_ref, wu_ref, *rest):
        cin = rest[:ni]
        g_ref, u_ref, a_ref = rest[ni:ni + 3]
        cout, cscr = rest[ni + 3:ni + 3 + no], rest[ni + 3 + no:]
        if carry:
            @pl.when((pl.program_id(0) == 0) & (pl.program_id(1) == 0))
            def _():
                carry.start(cin, cout, cscr)

        hv = h_ref[...]
        g = lax.dot_general(hv, wg_ref[...], _DIMS["nt"], preferred_element_type=F32)
        u = lax.dot_general(hv, wu_ref[...], _DIMS["nt"], preferred_element_type=F32)
        sg = jax.nn.sigmoid(g)
        silu = g * sg
        g_ref[...] = (u * (sg * (1.0 + g * (1.0 - sg)))).astype(BF16)
        u_ref[...] = silu.astype(BF16)
        a_ref[...] = (silu * u).astype(BF16)

        if carry:
            @pl.when((pl.program_id(0) == FF // FT - 1) & (pl.program_id(1) == T // TR - 1))
            def _():
                carry.finish(cin, cout, cscr)

    out = BS((TR, FT), lambda q, i: (i, q))
    outs = pl.pallas_call(
        body, grid=(FF // FT, T // TR),
        in_specs=[BS((TR, D), lambda q, i: (i, 0)),
                  BS((None, FT, D), lambda q, i: (0, q, 0)),
                  BS((None, FT, D), lambda q, i: (1, q, 0))] + (carry.in_specs if carry else []),
        out_specs=[out, out, out] + [ANY] * no,
        out_shape=[_sds((T, FF), BF16), _sds((T, FF), BF16), _sds((T, FF), BF16)]
        + (carry.out_shapes if carry else []),
        scratch_shapes=carry.scratch if carry else [],
        compiler_params=pltpu.CompilerParams(
            dimension_semantics=("arbitrary", "arbitrary"), vmem_limit_bytes=48 << 20,
            has_side_effects=bool(carry)), name=f"ffn_up_{l}",
    )(h2, wgu, wgu, *(carry.ins if carry else []))
    return outs[:3], outs[3:]


def _ffn_down_bwd(dx, wd, g, u, l):
    def epilogue(dact, ex, o_ref, i):
        o_ref[0] = (dact * ex[0][...].astype(F32)).astype(BF16)
        o_ref[1] = (dact * ex[1][...].astype(F32)).astype(BF16)

    ex_spec = BS((TR, FT), lambda i, q, k: (i, q))
    return _mm("nt", dx, wd, grid=(T // TR, FF // FT, 1),
               a_spec=BS((TR, D), lambda i, q, k: (i, 0)),
               b_spec=BS((FT, D), lambda i, q, k: (q, 0)),
               out_shape=_sds((2, T, FF), BF16),
               out_spec=BS((2, TR, FT), lambda i, q, k: (0, i, q)),
               acc_shape=(TR, FT), extras=(g, u), extra_specs=(ex_spec, ex_spec),
               epilogue=epilogue, name=f"ffn_down_bwd_{l}")


def _row_tile(rows, cols, itemsize=4, limit=2 << 20):
    tr = rows
    while tr * cols * itemsize > limit and tr % 2 == 0 and (tr // 2) % 16 == 0:
        tr //= 2
    return tr


def _cast_bf16(arrs, name):
    n = len(arrs)
    rows, cols = arrs[0].shape
    tr = _row_tile(rows, cols)

    def body(*refs):
        o_ref = refs[n]
        k = pl.program_id(0)
        val = refs[0][...]
        for j in range(1, n):
            val = jnp.where(k == j, refs[j][...], val)
        o_ref[...] = val.astype(BF16)

    return pl.pallas_call(
        body, grid=(n, rows // tr),
        in_specs=[BS((tr, cols), lambda k, i: (i, 0))] * n,
        out_specs=BS((None, tr, cols), lambda k, i: (k, i, 0)),
        out_shape=_sds((n, rows, cols), BF16),
        compiler_params=_cp(("arbitrary", "arbitrary")), name=name,
    )(*arrs)


def _quad_sum(own, got, name):
    n, rows, cols = own.shape
    tr = _row_tile(rows, cols)

    def body(a_ref, q_ref, o_ref):
        o_ref[...] = ((a_ref[...] + q_ref[0].astype(F32)) + q_ref[1].astype(F32)) + q_ref[2].astype(F32)

    spec = BS((None, tr, cols), lambda k, i: (k, i, 0))
    return pl.pallas_call(
        body, grid=(n, rows // tr),
        in_specs=[spec, BS((3, None, tr, cols), lambda k, i: (0, k, i, 0))], out_specs=spec,
        out_shape=_sds((n, rows, cols), F32),
        compiler_params=_cp(("arbitrary", "arbitrary")), name=name,
    )(own, got)


def _adam_math(w, g, m, v):
    m = ADAM_B1 * m + (1.0 - ADAM_B1) * g
    v = ADAM_B2 * v + (1.0 - ADAM_B2) * (g * g)
    m_hat = m / (1.0 - ADAM_B1 ** ADAM_STEP)
    v_hat = v / (1.0 - ADAM_B2 ** ADAM_STEP)
    delta = -ADAM_LR * (m_hat / (jnp.sqrt(v_hat) + ADAM_EPS) + ADAM_WD * w)
    return delta, m, v


def _adamw_big(w, g, m, v, name):
    shape = w.shape
    cols = shape[-1]
    rows = w.size // cols
    tr = _row_tile(rows, cols, limit=1 << 20)

    def body(w_ref, g_ref, m_ref, v_ref, d_ref, nm_ref, nv_ref):
        d, nm, nv = _adam_math(w_ref[...], g_ref[...], m_ref[...], v_ref[...])
        d_ref[...] = d
        nm_ref[...] = nm
        nv_ref[...] = nv

    spec = BS((tr, cols), lambda i: (i, 0))
    outs = pl.pallas_call(
        body, grid=(rows // tr,), in_specs=[spec] * 4, out_specs=[spec] * 3,
        out_shape=[_sds((rows, cols), F32)] * 3,
        compiler_params=_cp(("arbitrary",)), name=name,
    )(*[a.reshape(rows, cols) for a in (w, g, m, v)])
    return [o.reshape(shape) for o in outs]


def _adamw_small(ws, gs, ms, vs):
    n = len(ws)

    def body(*refs):
        for i in range(n):
            d, nm, nv = _adam_math(refs[i][...], refs[n + i][...], refs[2 * n + i][...],
                                   refs[3 * n + i][...])
            refs[4 * n + i][...] = d
            refs[5 * n + i][...] = nm
            refs[6 * n + i][...] = nv

    specs = [BS(w.shape, lambda i: (0, 0)) for w in ws]
    outs = pl.pallas_call(
        body, grid=(1,), in_specs=specs * 4, out_specs=specs * 3,
        out_shape=[_sds(w.shape, F32) for w in ws] * 3,
        compiler_params=_cp(("arbitrary",)), name="adamw_small",
    )(*ws, *gs, *ms, *vs)
    return outs[:n], outs[n:2 * n], outs[2 * n:]


def _place():
    x, y, c = lax.axis_index("x"), lax.axis_index("y"), lax.axis_index("c")
    chips = [(1 - x, y), (x, 1 - y), (1 - x, 1 - y)]
    return x, y, c, chips


class _Exchange:
    def __init__(self, ins, in_specs, out_shapes, scratch, start, finish):
        self.ins, self.in_specs, self.out_shapes, self.scratch = ins, in_specs, out_shapes, scratch
        self.start, self.finish = start, finish


def _run_exchange(ex, name, vmem_mb=40):
    ni, no = len(ex.ins), len(ex.out_shapes)

    def body(*refs):
        ex.start(refs[:ni], refs[ni:ni + no], refs[ni + no:])
        ex.finish(refs[:ni], refs[ni:ni + no], refs[ni + no:])

    return pl.pallas_call(
        body, in_specs=ex.in_specs, out_specs=[ANY] * no, out_shape=ex.out_shapes,
        scratch_shapes=ex.scratch,
        compiler_params=pltpu.CompilerParams(has_side_effects=True, vmem_limit_bytes=vmem_mb << 20),
        name=name,
    )(*ex.ins)


def _gather_exchange(srcs, dst_shapes, views, small=None):
    nu = len(srcs)
    nd = len(dst_shapes)
    ns = 1 if small is not None else 0

    def unpack(ins, outs, scr):
        x, y, c, chips = _place()
        src = ins[:nu]
        vw = [views[u](outs[:nd]) for u in range(nu)]
        vbuf = scr[:nu]
        send, recv, fsend, frecv, lsem, ssend, srecv, vsem = scr[nu:]

        def ici(u, j, shard, to):
            return pltpu.make_async_remote_copy(
                src_ref=vbuf[u].at[:, c], dst_ref=vw[u].at[:, shard, c],
                send_sem=send.at[3 * u + j], recv_sem=recv.at[3 * u + j],
                device_id=to, device_id_type=MESH)

        def fwd(u, j, shard, half):
            return pltpu.make_async_remote_copy(
                src_ref=vw[u].at[:, shard, half], dst_ref=vw[u].at[:, shard, half],
                send_sem=fsend.at[3 * u + j], recv_sem=frecv.at[3 * u + j],
                device_id=(x, y, 1 - c), device_id_type=MESH)

        def small_copy(j, shard, to):
            return pltpu.make_async_remote_copy(
                src_ref=ins[nu], dst_ref=outs[nd].at[shard],
                send_sem=ssend.at[j], recv_sem=srecv.at[j], device_id=to, device_id_type=MESH)

        stage = [pltpu.make_async_copy(src[u], vbuf[u], vsem.at[u]) for u in range(nu)]
        local = [pltpu.make_async_copy(vbuf[u], vw[u].at[:, 2 * x + y], lsem.at[u]) for u in range(nu)]
        if ns:
            local.append(pltpu.make_async_copy(ins[nu], outs[nd].at[2 * x + y], lsem.at[nu]))
        return x, y, c, chips, ici, fwd, small_copy, stage, local

    def start(ins, outs, scr):
        x, y, c, chips, ici, fwd, small_copy, stage, local = unpack(ins, outs, scr)
        s = 2 * x + y
        for cp in stage:
            cp.start()
        if ns:
            local[nu].start()
            for j, chip in enumerate(chips):
                small_copy(j, s, (*chip, c)).start()
        for u in range(nu):
            stage[u].wait()
            for j, chip in enumerate(chips):
                ici(u, j, s, (*chip, c)).start()
            local[u].start()

    def finish(ins, outs, scr):
        x, y, c, chips, ici, fwd, small_copy, stage, local = unpack(ins, outs, scr)
        s = 2 * x + y
        for u in range(nu):
            for j, chip in enumerate(chips):
                sj = 2 * chip[0] + chip[1]
                ici(u, j, sj, (x, y, c)).wait_recv()
                fwd(u, j, sj, c).start()
        for u in range(nu):
            for j, chip in enumerate(chips):
                sj = 2 * chip[0] + chip[1]
                fwd(u, j, sj, 1 - c).wait_recv()
        for u in range(nu):
            for j, chip in enumerate(chips):
                ici(u, j, s, (*chip, c)).wait_send()
                fwd(u, j, s, c).wait_send()
        if ns:
            for j, chip in enumerate(chips):
                small_copy(j, 2 * chip[0] + chip[1], (x, y, c)).wait_recv()
                small_copy(j, s, (*chip, c)).wait_send()
        for cp in local:
            cp.wait()

    dma = pltpu.SemaphoreType.DMA
    return _Exchange(
        ins=list(srcs) + ([small] if ns else []),
        in_specs=[ANY] * nu + [BS(memory_space=pltpu.VMEM)] * ns,
        out_shapes=[_sds(sh, BF16) for sh in dst_shapes]
        + ([_sds((NSH,) + small.shape, F32)] if ns else []),
        scratch=[pltpu.VMEM(a.shape, BF16) for a in srcs]
        + [dma((3 * nu,)), dma((3 * nu,)), dma((3 * nu,)), dma((3 * nu,)),
           dma((nu + 1,)), dma((3,)), dma((3,)), dma((nu,))],
        start=start, finish=finish)


def _pair_reduce(gs, name):
    nu = len(gs)
    ns = [g.shape[0] * NSH for g in gs]
    base = [sum(ns[:u]) for u in range(nu)]

    def body(*refs):
        g_refs, own_refs, sb_refs = refs[:nu], refs[nu:2 * nu], refs[2 * nu:3 * nu]
        bufs = refs[3 * nu:8 * nu]
        send, recv, lsem, osem = refs[8 * nu:]
        x, y, c, _ = _place()
        s = 2 * x + y

        def unit(u):
            sendb, recvb, stage, outf, outb = bufs[5 * u:5 * u + 5]

            def load(k, half):
                return pltpu.make_async_copy(g_refs[u].at[k // NSH, k % NSH, half], stage.at[k % 3],
                                             lsem.at[3 * u + k % 3])

            def push(k):
                return pltpu.make_async_remote_copy(
                    src_ref=sendb.at[k], dst_ref=recvb.at[k], send_sem=send.at[base[u] + k],
                    recv_sem=recv.at[base[u] + k], device_id=(x, y, 1 - c), device_id_type=MESH)

            def store(k):
                return pltpu.make_async_copy(outb.at[k % 2], sb_refs[u].at[k // NSH, k % NSH],
                                             osem.at[3 * u + k % 2])

            return sendb, recvb, stage, outf, outb, load, push, store

        for u in range(nu):
            sendb, recvb, stage, outf, outb, load, push, store = unit(u)
            for k in range(min(2, ns[u])):
                load(k, 1 - c).start()
            for k in range(ns[u]):
                if k + 2 < ns[u]:
                    load(k + 2, 1 - c).start()
                load(k, 1 - c).wait()
                sendb[k] = stage[k % 3].astype(BF16)
                push(k).start()
        for u in range(nu):
            sendb, recvb, stage, outf, outb, load, push, store = unit(u)
            n = ns[u]
            for k in range(min(2, n)):
                load(k, c).start()
            for k in range(n):
                if k + 2 < n:
                    load(k + 2, c).start()
                load(k, c).wait()
                push(k).wait_recv()
                total = stage[k % 3] + recvb[k].astype(F32)
                if k >= 2:
                    store(k - 2).wait()
                outb[k % 2] = total.astype(BF16)
                store(k).start()

                @pl.when(s == k % NSH)
                def _():
                    outf[...] = total
                    keep = pltpu.make_async_copy(outf, own_refs[u].at[k // NSH], osem.at[3 * u + 2])
                    keep.start()
                    keep.wait()

            for k in range(max(n - 2, 0), n):
                store(k).wait()
        for u in range(nu):
            push = unit(u)[6]
            for k in range(ns[u]):
                push(k).wait_send()

    dma = pltpu.SemaphoreType.DMA
    scratch = []
    for g, n in zip(gs, ns):
        rh, cc = g.shape[3], g.shape[4]
        scratch += [pltpu.VMEM((n, rh, cc), BF16), pltpu.VMEM((n, rh, cc), BF16),
                    pltpu.VMEM((3, rh, cc), F32), pltpu.VMEM((rh, cc), F32), pltpu.VMEM((2, rh, cc), BF16)]
    outs = pl.pallas_call(
        body, in_specs=[ANY] * nu, out_specs=[ANY] * (2 * nu),
        out_shape=[_sds((g.shape[0], g.shape[3], g.shape[4]), F32) for g in gs]
        + [_sds((g.shape[0], NSH, g.shape[3], g.shape[4]), BF16) for g in gs],
        scratch_shapes=scratch + [dma((sum(ns),)), dma((sum(ns),)), dma((3 * nu,)), dma((3 * nu,))],
        compiler_params=pltpu.CompilerParams(has_side_effects=True, vmem_limit_bytes=56 << 20),
        name=name,
    )(*gs)
    return list(outs[:nu]), list(outs[nu:])


def _chip_exchange(sums_bf16):
    nu = len(sums_bf16)

    def pushes(ins, outs, scr):
        x, y, c, chips = _place()
        send, recv = scr
        return [pltpu.make_async_remote_copy(
            src_ref=ins[u].at[:, 2 * chip[0] + chip[1]], dst_ref=outs[u].at[j],
            send_sem=send.at[3 * u + j], recv_sem=recv.at[3 * u + j],
            device_id=(*chip, c), device_id_type=MESH)
            for u in range(nu) for j, chip in enumerate(chips)]

    def start(ins, outs, scr):
        for cp in pushes(ins, outs, scr):
            cp.start()

    def finish(ins, outs, scr):
        for cp in pushes(ins, outs, scr):
            cp.wait()

    dma = pltpu.SemaphoreType.DMA
    shapes = [(3, a.shape[0], a.shape[2], a.shape[3]) for a in sums_bf16]
    return _Exchange(ins=list(sums_bf16), in_specs=[ANY] * nu,
                     out_shapes=[_sds(sh, BF16) for sh in shapes],
                     scratch=[dma((3 * nu,)), dma((3 * nu,))], start=start, finish=finish)


def _final_exchange(halves, out_shapes, targets):
    nu = len(halves)
    no = len(out_shapes)
    ncp = sum(len(t) for t in targets)

    def body(*refs):
        hv = refs[:nu]
        out = refs[nu:nu + no]
        sbuf = refs[nu + no:2 * nu + no]
        rbuf = refs[2 * nu + no:3 * nu + no]
        send, recv, lsem, osem, csem = refs[3 * nu + no:]
        x, y, c, _ = _place()
        stage = [pltpu.make_async_copy(hv[u], sbuf[u], lsem.at[u]) for u in range(nu)]
        push = [pltpu.make_async_remote_copy(
            src_ref=sbuf[u], dst_ref=rbuf[u], send_sem=send.at[u], recv_sem=recv.at[u],
            device_id=(x, y, 1 - c), device_id_type=MESH) for u in range(nu)]
        mine, theirs = [], []
        k = 0
        for u in range(nu):
            rh = hv[u].shape[1]
            for (p, oi, li) in targets[u]:
                mine.append((u, pltpu.make_async_copy(
                    sbuf[u].at[p], out[oi].at[li, pl.ds(c * rh, rh), :], csem.at[k])))
                theirs.append((u, pltpu.make_async_copy(
                    rbuf[u].at[p], out[oi].at[li, pl.ds((1 - c) * rh, rh), :], osem.at[k])))
                k += 1
        for cp in stage:
            cp.start()
        for u in range(nu):
            stage[u].wait()
            push[u].start()
            for (v, cp) in mine:
                if v == u:
                    cp.start()
        for u in range(nu):
            push[u].wait_recv()
            for (v, cp) in theirs:
                if v == u:
                    cp.start()
        for (_, cp) in theirs + mine:
            cp.wait()
        for u in range(nu):
            push[u].wait_send()

    dma = pltpu.SemaphoreType.DMA
    bufs = [pltpu.VMEM(h.shape, F32) for h in halves]
    return pl.pallas_call(
        body, in_specs=[ANY] * nu, out_specs=[ANY] * no,
        out_shape=[_sds(sh, F32) for sh in out_shapes],
        scratch_shapes=bufs + bufs + [dma((nu,)), dma((nu,)), dma((nu,)), dma((ncp,)), dma((ncp,))],
        compiler_params=pltpu.CompilerParams(has_side_effects=True, vmem_limit_bytes=56 << 20),
        name="final_exchange",
    )(*halves)


def _small_allreduce(pack):
    rows = pack.shape[0]

    def body(p_ref, o_ref, buf, send, recv):
        x, y, c, _ = _place()
        me = 4 * x + 2 * y + c
        buf[me] = p_ref[...]
        k = 0
        copies = []
        for dx in range(2):
            for dy in range(2):
                for dc in range(2):
                    if dx == 0 and dy == 0 and dc == 0:
                        continue
                    to = (jnp.where(dx, 1 - x, x), jnp.where(dy, 1 - y, y), jnp.where(dc, 1 - c, c))
                    src_slot = 4 * to[0] + 2 * to[1] + to[2]
                    copies.append((pltpu.make_async_remote_copy(
                        src_ref=p_ref, dst_ref=buf.at[me], send_sem=send.at[k], recv_sem=recv.at[k],
                        device_id=to, device_id_type=MESH), src_slot, k))
                    k += 1
        for cp, _, _ in copies:
            cp.start()
        for cp, src_slot, k in copies:
            pltpu.make_async_remote_copy(
                src_ref=p_ref, dst_ref=buf.at[src_slot], send_sem=send.at[k], recv_sem=recv.at[k],
                device_id=(x, y, c), device_id_type=MESH).wait()
        acc = buf[0]
        for d in range(1, 8):
            acc = acc + buf[d]
        o_ref[...] = acc

    dma = pltpu.SemaphoreType.DMA
    vm = BS(memory_space=pltpu.VMEM)
    return pl.pallas_call(
        body, in_specs=[vm], out_specs=vm, out_shape=_sds((rows, D), F32),
        scratch_shapes=[pltpu.VMEM((8, rows, D), F32), dma((7,)), dma((7,))],
        compiler_params=pltpu.CompilerParams(has_side_effects=True, vmem_limit_bytes=32 << 20),
        name="small_allreduce",
    )(pack)


def _in_proj(h, w, bias, name, transposed=False, carry=None):
    n = w.shape[0] if transposed else w.shape[1]
    tn = 1280 if n == NA else 1792
    ep = None
    extras, especs = (), ()
    if bias is not None:
        def ep(acc, ex, o_ref, i):
            o_ref[...] = acc + ex[0][...]
        extras = (bias,)
        especs = (BS((1, tn), lambda i, j, k: (0, j)),)
    b_spec = BS((tn, D), lambda i, j, k: (j, 0)) if transposed else BS((D, tn), lambda i, j, k: (0, j))
    return _mm("nt" if transposed else "nn", h, w, grid=(T // TR, n // tn, 1),
               a_spec=BS((TR, D), lambda i, j, k: (i, 0)), b_spec=b_spec,
               out_shape=_sds((T, n), F32), out_spec=BS((TR, tn), lambda i, j, k: (i, j)),
               acc_shape=(TR, tn), extras=extras, extra_specs=especs, epilogue=ep, name=name, carry=carry)


def _res_rms_epilogue(acc, ex, outs, i):
    y = acc + ex[0][...]
    outs[0][...] = y
    r = lax.rsqrt(jnp.mean(y * y, axis=-1, keepdims=True) + EPS)
    outs[1][...] = (y * r * ex[1][...]).astype(BF16)


def _res_loss_epilogue(acc, ex, outs, i):
    e = acc + ex[0][...] - ex[1][...]
    outs[1][...] = e * (1.0 / D)

    @pl.when(i == 0)
    def _():
        outs[0][...] = jnp.zeros_like(outs[0])

    outs[0][...] += 0.5 * jnp.sum(jnp.mean(e * e, axis=-1, keepdims=True), axis=0, keepdims=True)


def _rms_bwd_epilogue(dh, ex, outs, i):
    xv = ex[0][...]
    r = lax.rsqrt(jnp.mean(xv * xv, axis=-1, keepdims=True) + EPS)
    xh = xv * r
    gy = dh * ex[1][...]
    outs[0][...] = r * (gy - xh * jnp.mean(gy * xh, axis=-1, keepdims=True)) + ex[2][...]

    @pl.when(i == 0)
    def _():
        outs[1][...] = jnp.zeros_like(outs[1])

    outs[1][...] += jnp.sum(dh * xh, axis=0, keepdims=True)


def _local_step(x, mem, target, w, p, carries=None, bwd_carry_fn=None, late_carry_fn=None):
    row = lambda i, j, k: (i, 0)
    whole = lambda i, j, k: (0, 0)
    w = {k: (list(v) if isinstance(v, list) else v) for k, v in w.items()}
    carries = carries or {}

    def carry_of(name):
        return carries[name][0] if name in carries else None

    def delivered(name, outs):
        if name in carries:
            carries[name][1](w, outs)

    saved = []
    bias = _bias_expand(p["rel_u"])
    vec = BS((1, D), whole)
    h = _rms_fwd(x, p["norm1_g"][0:1], "rms1_0")
    for l in range(2):
        type_a = l == 0
        memn = _rms_fwd(mem, p["mem_norm_g"][l:l + 1], f"rmsmem_{l}")
        if type_a:
            z = _in_proj(h, w["a"], None, "inproj_a", carry=carry_of("inproj_a"))
            if carry_of("inproj_a") is not None:
                z, carried = z
                delivered("inproj_a", carried)
            cat, carried = _attn_fwd(z, bias, p["a_q_g2"], p["a_k_g2"], carry_of("attn_fwd"))
            delivered("attn_fwd", carried)
            qcol = NA // MEMW - 1
        else:
            z = _in_proj(h, w["b"], p["b_b_in"], "inproj_b", transposed=True)
            cat, conv_y = _conv_fwd(z, p["conv_w"], p["conv_b"], p["ln_g"], p["ln_b"])
            qcol = NBW // MEMW - 1
        kv = _mm("nn", memn, w["kv"][l], grid=(1, 1, 1),
                 a_spec=BS((NB * MEMT, D), whole), b_spec=BS((D, 2 * MEMW), whole),
                 out_shape=_sds((NB * MEMT, 2 * MEMW), F32), out_spec=BS((NB * MEMT, 2 * MEMW), whole),
                 acc_shape=(8, 128), name=f"memkv_{l}")
        cat = _memattn_fwd(z, kv, cat, p["mq_g4"][l:l + 1], p["mk_g4"][l:l + 1], qcol, f"memattn_fwd_{l}")
        x1, h2 = _mm("nn", cat, w["wo"][l], grid=(T // TR, 1, 1), a_spec=BS((TR, D), row),
                     b_spec=BS((D, D), whole),
                     out_shape=[_sds((T, D), F32), _sds((T, D), BF16)],
                     out_spec=[BS((TR, D), row), BS((TR, D), row)], acc_shape=(8, 128),
                     extras=(x, p["norm2_g"][l:l + 1]), extra_specs=(BS((TR, D), row), vec),
                     epilogue=_res_rms_epilogue, name=f"outproj_{l}")
        (g, u, act), carried = _ffn_up(h2, w["gu"][l], l, carry_of(f"ffn_up_{l}"))
        delivered(f"ffn_up_{l}", carried)
        last = l == 1
        res = _mm("nn", act, w["wd"][l], grid=(T // TR, 1, 1),
                  a_spec=BS((TR, FF), row), b_spec=BS((FF, D), whole),
                  out_shape=[_sds((1, 1), F32), _sds((T, D), F32)] if last else
                  [_sds((T, D), F32), _sds((T, D), BF16)],
                  out_spec=[BS((1, 1), whole), BS((TR, D), row)] if last else
                  [BS((TR, D), row), BS((TR, D), row)],
                  acc_shape=(8, 128), extras=(x1, target if last else p["norm1_g"][1:2]),
                  extra_specs=(BS((TR, D), row), BS((TR, D), row) if last else vec),
                  epilogue=_res_loss_epilogue if last else _res_rms_epilogue, sequential=last,
                  name=f"ffn_down_{l}", carry=carry_of(f"ffn_down_{l}"))
        if carry_of(f"ffn_down_{l}") is not None:
            res, carried = res
            delivered(f"ffn_down_{l}", carried)
        saved.append(dict(x=x, h=h, memn=memn, kv=kv, z=z, cat=cat, x1=x1, h2=h2, g=g, u=u, act=act,
                          qcol=qcol))
        if last:
            loss, dx = res
        else:
            x, h = res

    big = dict(a=None, b=None, kv=[None, None], wo=[None, None], gu=[None, None], wd=[None, None])
    small = {}
    bwd_carried, late_carried = (), ()
    tk = T // 2
    nkt = T // tk
    for l in (1, 0):
        sv = saved[l]
        dgu = _ffn_down_bwd(dx, w["wd"][l], sv["g"], sv["u"], l)
        big["wd"][l] = _mm("tn", sv["act"], dx, grid=(FF // FT, 1, 2 * nkt),
                           a_spec=BS((tk // 2, FT), lambda i, j, k: (k, i)),
                           b_spec=BS((tk // 2, D), lambda i, j, k: (k, 0)),
                           out_shape=_sds((FF, D), F32), out_spec=BS((FT, D), lambda i, j, k: (i, 0)),
                           acc_shape=(FT, D), name=f"dw_down_{l}")
        dx1, small[f"norm2_g{l}"] = _mm(
            "nn", dgu, w["gu"][l], grid=(T // TR, 1, 2),
            a_spec=BS((None, TR, FF), lambda i, j, k: (k, i, 0)),
            b_spec=BS((None, FF, D), lambda i, j, k: (k, 0, 0)),
            out_shape=[_sds((T, D), F32), _sds((1, D), F32)], out_spec=[BS((TR, D), row), vec],
            acc_shape=(TR, D), extras=(sv["x1"], p["norm2_g"][l:l + 1], dx),
            extra_specs=(BS((TR, D), row), vec, BS((TR, D), row)),
            epilogue=_rms_bwd_epilogue, sequential=True, name=f"dh2_{l}")
        big["gu"][l] = _mm("tn", dgu, sv["h2"], grid=(2 * FF // FT, 1, nkt),
                           a_spec=BS((None, tk, FT), lambda i, j, k: (i // 2, k, i % 2)),
                           b_spec=BS((tk, D), lambda i, j, k: (k, 0)),
                           out_shape=_sds((2, FF, D), F32),
                           out_spec=BS((None, FT, D), lambda i, j, k: (i // 2, i % 2, 0)),
                           acc_shape=(FT, D), vmem_mb=58, name=f"dw_gu_{l}")
        dcat = _mm("nt", dx1, w["wo"][l], grid=(T // TR, 1, 1), a_spec=BS((TR, D), row),
                   b_spec=BS((D, D), whole),
                   out_shape=_sds((T, D), F32), out_spec=BS((TR, D), row), acc_shape=(8, 128),
                   name=f"dcat_{l}")
        big["wo"][l] = _mm("tn", sv["cat"], dx1, grid=(1, 1, nkt),
                           a_spec=BS((tk, D), lambda i, j, k: (k, 0)), b_spec=BS((tk, D), lambda i, j, k: (k, 0)),
                           out_shape=_sds((D, D), F32), out_spec=BS((D, D), whole),
                           acc_shape=(D, D), name=f"dw_out_{l}")
        dqm, dkv, small[f"mq_g{l}"], small[f"mk_g{l}"] = _memattn_bwd(
            sv["z"], sv["kv"], dcat, p["mq_g4"][l:l + 1], p["mk_g4"][l:l + 1], sv["qcol"], f"memattn_bwd_{l}")
        big["kv"][l] = _mm("tn", sv["memn"], dkv, grid=(1, 1, 1),
                           a_spec=BS((NB * MEMT, D), whole), b_spec=BS((NB * MEMT, 2 * MEMW), whole),
                           out_shape=_sds((D, 2 * MEMW), F32), out_spec=BS((D, 2 * MEMW), whole),
                           acc_shape=(8, 128), name=f"dw_kv_{l}")
        dmemn = _mm("nt", dkv, w["kv"][l], grid=(1, 1, 1),
                    a_spec=BS((NB * MEMT, 2 * MEMW), whole), b_spec=BS((D, 2 * MEMW), whole),
                    out_shape=_sds((NB * MEMT, D), F32), out_spec=BS((NB * MEMT, D), whole),
                    acc_shape=(8, 128), name=f"dmemn_{l}")
        _, small[f"mem_norm_g{l}"] = _rms_bwd(dmemn, mem, p["mem_norm_g"][l:l + 1], None, f"rmsmem_bwd_{l}")
        if l == 0:
            carry = bwd_carry_fn(big) if bwd_carry_fn is not None else None
            (dq, dk, dv, dbias, small["a_q_g"], small["a_k_g"]), bwd_carried = _attn_bwd(
                sv["z"], dcat, bias, p["a_q_g2"], p["a_k_g2"], carry)
            small["rel_u"] = _bias_reduce(dbias)
            dz = jnp.concatenate([dq, dk, dv, dqm], axis=1)
            w_in, key, n, tn = w["a"], "a", NA, 1280
        else:
            du, small["conv_w"], small["conv_b"], small["ln_g"], small["ln_b"], dbin_u = _conv_bwd(
                sv["z"], conv_y, dcat, p["conv_w"], p["ln_g"], p["ln_b"])
            dz = jnp.concatenate([du, dqm], axis=1)
            small["b_in_u"] = dbin_u
            w_in, key, n, tn = w["b"], "b", NBW, 896
        norm_bwd = dict(out_shape=[_sds((T, D), F32), _sds((1, D), F32)], out_spec=[BS((TR, D), row), vec],
                        acc_shape=(8, 128), extras=(sv["x"], p["norm1_g"][l:l + 1], dx1),
                        extra_specs=(BS((TR, D), row), vec, BS((TR, D), row)),
                        epilogue=_rms_bwd_epilogue, sequential=True, name=f"dh_{l}")
        if l == 0:
            big[key] = _mm("tn", sv["h"], dz, grid=(1, n // tn, nkt),
                           a_spec=BS((tk, D), lambda i, j, k: (k, 0)), b_spec=BS((tk, tn), lambda i, j, k: (k, j)),
                           out_shape=_sds((D, n), F32), out_spec=BS((D, tn), lambda i, j, k: (0, j)),
                           acc_shape=(D, tn), name=f"dw_in_{l}")
            carry = late_carry_fn(big) if late_carry_fn is not None else None
            res = _mm("nt", dz, w_in, grid=(T // TR, 1, 1), a_spec=BS((TR, n), row),
                      b_spec=BS((D, n), whole), carry=carry, **norm_bwd)
            if carry is not None:
                res, late_carried = res
            dx, small[f"norm1_g{l}"] = res
        else:
            dx, small[f"norm1_g{l}"] = _mm("nn", dz, w_in, grid=(T // TR, 1, 1),
                                           a_spec=BS((TR, n), row), b_spec=BS((n, D), whole), **norm_bwd)
            big[key] = _mm("tn", dz, sv["h"], grid=(n // tn, 1, nkt),
                           a_spec=BS((tk, tn), lambda i, j, k: (k, i)), b_spec=BS((tk, D), lambda i, j, k: (k, 0)),
                           out_shape=_sds((n, D), F32), out_spec=BS((tn, D), lambda i, j, k: (i, 0)),
                           acc_shape=(tn, D), name=f"dw_in_{l}")
        if l == 1:
            small["b_in_qm"] = _colsum(dqm, "colsum_dqm")
    return loss, dx, big, small, bwd_carried, late_carried


def _colsum(a, name):
    rows, cols = a.shape

    def body(a_ref, o_ref):
        @pl.when(pl.program_id(0) == 0)
        def _():
            o_ref[...] = jnp.zeros_like(o_ref)

        o_ref[...] += jnp.sum(a_ref[...].astype(F32), axis=0, keepdims=True)

    return pl.pallas_call(
        body, grid=(rows // TR,), in_specs=[BS((TR, cols), lambda i: (i, 0))],
        out_specs=BS((1, cols), lambda i: (0, 0)), out_shape=_sds((1, cols), F32),
        compiler_params=_cp(("arbitrary",)), name=name,
    )(a)


_PACK_ROWS = 64


def _pad_to(a, rows, cols=D):
    return jnp.pad(a, ((0, rows - a.shape[0]), (0, cols - a.shape[1])))


def _pack_small(sm):
    parts = [
        jnp.concatenate([sm["norm1_g0"], sm["norm1_g1"]], 0),
        jnp.concatenate([sm["mem_norm_g0"], sm["mem_norm_g1"]], 0),
        jnp.concatenate([sm["norm2_g0"], sm["norm2_g1"]], 0),
        _pad_to(sm["a_q_g"], 1), _pad_to(sm["a_k_g"], 1),
        _pad_to(jnp.concatenate([sm["mq_g0"], sm["mq_g1"]], 0), 2),
        _pad_to(jnp.concatenate([sm["mk_g0"], sm["mk_g1"]], 0), 2),
        _pad_to(sm["conv_b"], 1), _pad_to(sm["ln_g"], 1), _pad_to(sm["ln_b"], 1),
        _pad_to(sm["b_in_u"][:, :D], 1), _pad_to(sm["b_in_u"][:, D:], 1),
        _pad_to(sm["b_in_qm"], 1),
        _pad_to(sm["conv_w"][:CONVW], CONVW),
        sm["rel_u"].reshape(12, D),
    ]
    pack = jnp.concatenate(parts, 0)
    return jnp.pad(pack, ((0, _PACK_ROWS - pack.shape[0]), (0, 0)))


def _rel_table_to_u(rel_bias):
    flat = jnp.concatenate([jnp.broadcast_to(rel_bias[:, 191:192], (12, 447)), rel_bias[:, ::-1]], axis=1)
    return jnp.pad(flat, ((0, 0), (192, 1024 - 192 - 639))).reshape(12, 1, 1024)


def _u_to_rel_table(du):
    flat = du[:, 192:192 + 639]
    g = flat[:, 447:][:, ::-1]
    return g, flat[:, :447]


def kernel(x, mem, norm1_g, mem_norm_g, a_w_in, a_q_g, a_k_g, a_rel_bias, b_w_in, b_b_in, b_conv_w, b_conv_b, b_ln_g, b_ln_b, mq_g, mk_g, w_mem_kv, w_out, norm2_g, w_gate, w_up, w_down, loss_target, m_norm1_g, m_mem_norm_g, m_a_w_in, m_a_q_g, m_a_k_g, m_a_rel_bias, m_b_w_in, m_b_b_in, m_b_conv_w, m_b_conv_b, m_b_ln_g, m_b_ln_b, m_mq_g, m_mk_g, m_w_mem_kv, m_w_out, m_norm2_g, m_w_gate, m_w_up, m_w_down, v_norm1_g, v_mem_norm_g, v_a_w_in, v_a_q_g, v_a_k_g, v_a_rel_bias, v_b_w_in, v_b_b_in, v_b_conv_w, v_b_conv_b, v_b_ln_g, v_b_ln_b, v_mq_g, v_mk_g, v_w_mem_kv, v_w_out, v_norm2_g, v_w_gate, v_w_up, v_w_down):
    sx = 2 * lax.axis_index("x") + lax.axis_index("y")

    n_in = (NA // NSH, NBW // NSH)
    tr = lambda a: jnp.swapaxes(a, -1, -2)
    small_src = jnp.concatenate([
        jnp.pad(b_b_in, ((0, 0), (0, 512 - 448))),
        jnp.pad(b_conv_w[0], ((0, 0), (0, 512 - 192))),
        jnp.pad(jnp.concatenate([b_conv_b, b_ln_g, b_ln_b], 0), ((0, 0), (0, 512 - 192))),
        jnp.zeros((5, 512), F32)], 0)

    def gather_groups(groups, small=None):
        def src_of(l, name):
            if name == "in":
                return [_cast_bf16([tr(b_w_in[0])], "cast_in_1").reshape(1, 2, n_in[1] // 2, D) if l else
                        _cast_bf16([a_w_in[0]], "cast_in_0").reshape(1, 2, D // 2, n_in[0])]
            if name == "gu":
                return [_cast_bf16([tr(w_gate[l])], f"cast_gate_{l}").reshape(1, 2, FS // 2, D),
                        _cast_bf16([tr(w_up[l])], f"cast_up_{l}").reshape(1, 2, FS // 2, D)]
            arr, shape = {"kv": (w_mem_kv, (1, 2, 128, 2 * MEMW)), "wo": (w_out, (1, 2, 128, D)),
                          "wd": (w_down, (1, 2, FS // 2, D))}[name]
            return [_cast_bf16([arr[l]], f"cast_{name}_{l}").reshape(shape)]

        def dst_of(l, name):
            if name == "in":
                return (1, NSH, 2, n_in[1] // 2, D) if l else (1, NSH, 2, D // 2, n_in[0])
            return {"kv": (1, NSH, 2, 128, 2 * MEMW), "wo": (1, NSH, 2, 128, D),
                    "gu": (1, 2, NSH, 2, FS // 2, D), "wd": (1, NSH, 2, FS // 2, D)}[name]

        items = [(l, name) for l, names in groups for name in names]
        srcs, views = [], []
        for k, (l, name) in enumerate(items):
            srcs += src_of(l, name)
            if name == "gu":
                views += [lambda d, k=k: d[k].at[:, 0], lambda d, k=k: d[k].at[:, 1]]
            else:
                views.append(lambda d, k=k: d[k])

        def done(w, outs):
            for k, (l, name) in enumerate(items):
                if name == "in" and l == 0:
                    w["a"] = outs[k].reshape(NSH, D, n_in[0]).transpose(1, 0, 2).reshape(D, NA)
                elif name == "in":
                    w["b"] = outs[k].reshape(NBW, D)
                else:
                    shape = {"kv": (D, 2 * MEMW), "wo": (D, D), "gu": (2, FF, D), "wd": (FF, D)}
                    w[name][l] = outs[k].reshape(shape[name])

        return _gather_exchange(srcs, [dst_of(l, name) for l, name in items], views, small), done

    w = dict(a=None, b=None, kv=[None, None], wo=[None, None], gu=[None, None], wd=[None, None])
    first, first_done = gather_groups([(0, ["in"])], small_src)
    outs0 = _run_exchange(first, "gather_weights_first")
    first_done(w, outs0)
    small_all = outs0[1]
    carries = {"inproj_a": gather_groups([(0, ["kv", "wo", "wd"])]),
               "attn_fwd": gather_groups([(0, ["gu"]), (1, ["in", "kv", "wo"])]),
               "ffn_up_0": gather_groups([(1, ["gu"])]),
               "ffn_down_0": gather_groups([(1, ["wd"])])}

    conv_w_full = small_all[:, 1:1 + CONVW, :192].transpose(1, 0, 2).reshape(CONVW, TOK)
    vec3 = small_all[:, 32:35, :192].transpose(1, 0, 2).reshape(3, TOK)
    p = dict(
        norm1_g=norm1_g, mem_norm_g=mem_norm_g, norm2_g=norm2_g,
        a_q_g2=jnp.tile(a_q_g, (1, AH)), a_k_g2=jnp.tile(a_k_g, (1, AH)),
        mq_g4=jnp.tile(mq_g, (1, 4)), mk_g4=jnp.tile(mk_g, (1, 4)),
        rel_u=_rel_table_to_u(a_rel_bias[0]),
        b_b_in=small_all[:, 0, :448].reshape(1, NBW),
        conv_w=jnp.pad(conv_w_full, ((0, 1), (0, 0))),
        conv_b=vec3[0:1], ln_g=vec3[1:2], ln_b=vec3[2:3])

    def pair_sums(items, big, name):
        units = []
        for l, tensor in items:
            if tensor == "in" and l == 0:
                g = big["a"].reshape(D, NSH, n_in[0]).transpose(1, 0, 2).reshape(1, NSH, 2, D // 2, n_in[0])
            elif tensor == "in":
                g = big["b"].reshape(1, NSH, 2, n_in[1] // 2, D)
            else:
                shape = {"kv": (1, NSH, 2, 128, 2 * MEMW), "wo": (1, NSH, 2, 128, D),
                         "gu": (2, NSH, 2, FS // 2, D), "wd": (1, NSH, 2, FS // 2, D)}
                g = big[tensor][l].reshape(shape[tensor])
            units.append(g)
        return _pair_reduce(units, name)

    early_groups = [[(1, "gu")], [(0, "gu")],
                    [(1, "in"), (1, "kv"), (1, "wo"), (1, "wd"), (0, "kv"), (0, "wo"), (0, "wd")]]
    early = [item for grp in early_groups for item in grp]
    late = [(0, "in")]
    own_early, own_late = [], []

    def bwd_carry_fn(big):
        sums_b = []
        for k, grp in enumerate(early_groups):
            own, sb = pair_sums(grp, big, f"pair_reduce_early_{k}")
            own_early.extend(own)
            sums_b.extend(sb)
        return _chip_exchange(sums_b)

    def late_carry_fn(big):
        own, sb = pair_sums(late, big, "pair_reduce_late")
        own_late.extend(own)
        return _chip_exchange(sb)

    loss, grad_x, big, small, parts_early, parts_late = _local_step(
        x.reshape(T, D), mem.reshape(NB * MEMT, D), loss_target.reshape(T, D), w, p,
        carries=carries, bwd_carry_fn=bwd_carry_fn, late_carry_fn=late_carry_fn)
    loss = lax.psum(loss[0, 0], ("x", "y", "c"))
    items = early + late
    halves = [_quad_sum(o, pt, f"quad_sum_{name}_{l}")
              for (l, name), o, pt in zip(items, own_early + own_late, list(parts_early) + list(parts_late))]
    out_shapes = [(1, D, NA // NSH), (1, NBW // NSH, D), (2, 2 * 128, 2 * MEMW), (2, 2 * 128, D),
                  (2, FS, D), (2, FS, D), (2, FS, D)]
    target_of = {"in": lambda l: [(0, l, 0)], "kv": lambda l: [(0, 2, l)], "wo": lambda l: [(0, 3, l)],
                 "gu": lambda l: [(0, 4, l), (1, 5, l)], "wd": lambda l: [(0, 6, l)]}
    targets = [target_of[name](l) for l, name in items]
    g_a, g_b, g_kv, g_wo, g_gate, g_up, g_wd = _final_exchange(halves, out_shapes, targets)

    tot = _small_allreduce(_pack_small(small))
    g_rel, clip_part = _u_to_rel_table(tot[49:61])
    g_rel = jnp.concatenate([g_rel[:, :191], g_rel[:, 191:] + _rowsum(clip_part)], axis=1)
    b_in_full = jnp.concatenate([tot[15:16], tot[16:17, :512], tot[17:18, :MEMW]], axis=1)
    g_small = dict(
        norm1_g=tot[0:2], mem_norm_g=tot[2:4], norm2_g=tot[4:6],
        a_q_g=tot[6:7, :HD], a_k_g=tot[7:8, :HD], a_rel_bias=g_rel[None],
        b_b_in=lax.dynamic_slice(b_in_full, (0, sx * 448), (1, 448)),
        b_conv_w=lax.dynamic_slice(tot[18:49, :TOK], (0, sx * 192), (CONVW, 192))[None],
        b_conv_b=lax.dynamic_slice(tot[12:13, :TOK], (0, sx * 192), (1, 192)),
        b_ln_g=lax.dynamic_slice(tot[13:14, :TOK], (0, sx * 192), (1, 192)),
        b_ln_b=lax.dynamic_slice(tot[14:15, :TOK], (0, sx * 192), (1, 192)),
        mq_g=tot[8:10, :HD], mk_g=tot[10:12, :HD])

    names = ["norm1_g", "mem_norm_g", "a_w_in", "a_q_g", "a_k_g", "a_rel_bias", "b_w_in", "b_b_in",
             "b_conv_w", "b_conv_b", "b_ln_g", "b_ln_b", "mq_g", "mk_g", "w_mem_kv", "w_out",
             "norm2_g", "w_gate", "w_up", "w_down"]
    weights = dict(zip(names, [norm1_g, mem_norm_g, a_w_in, a_q_g, a_k_g, a_rel_bias, b_w_in, b_b_in,
                               b_conv_w, b_conv_b, b_ln_g, b_ln_b, mq_g, mk_g, w_mem_kv, w_out,
                               norm2_g, w_gate, w_up, w_down]))
    ms = dict(zip(names, [m_norm1_g, m_mem_norm_g, m_a_w_in, m_a_q_g, m_a_k_g, m_a_rel_bias, m_b_w_in,
                          m_b_b_in, m_b_conv_w, m_b_conv_b, m_b_ln_g, m_b_ln_b, m_mq_g, m_mk_g,
                          m_w_mem_kv, m_w_out, m_norm2_g, m_w_gate, m_w_up, m_w_down]))
    vs = dict(zip(names, [v_norm1_g, v_mem_norm_g, v_a_w_in, v_a_q_g, v_a_k_g, v_a_rel_bias, v_b_w_in,
                          v_b_b_in, v_b_conv_w, v_b_conv_b, v_b_ln_g, v_b_ln_b, v_mq_g, v_mk_g,
                          v_w_mem_kv, v_w_out, v_norm2_g, v_w_gate, v_w_up, v_w_down]))
    grads = dict(g_small)
    grads.update(a_w_in=g_a, b_w_in=g_b, w_mem_kv=g_kv, w_out=g_wo, w_gate=g_gate, w_up=g_up, w_down=g_wd)
    big_names = ["a_w_in", "b_w_in", "w_mem_kv", "w_out", "w_gate", "w_up", "w_down"]
    small_names = [n for n in names if n not in big_names]
    delta, new_m, new_v = {}, {}, {}
    for n in big_names:
        if n in ("b_w_in", "w_gate", "w_up"):
            outs = _adamw_big(tr(weights[n]), grads[n], tr(ms[n]), tr(vs[n]), f"adamw_{n}")
            delta[n], new_m[n], new_v[n] = [tr(o) for o in outs]
            grads[n] = tr(grads[n])
        else:
            delta[n], new_m[n], new_v[n] = _adamw_big(weights[n], grads[n], ms[n], vs[n], f"adamw_{n}")
    as2d = lambda a: a.reshape(-1, a.shape[-1])
    d_s, m_s, v_s = _adamw_small([as2d(weights[n]) for n in small_names], [as2d(grads[n]) for n in small_names],
                                 [as2d(ms[n]) for n in small_names], [as2d(vs[n]) for n in small_names])
    for i, n in enumerate(small_names):
        delta[n] = d_s[i].reshape(weights[n].shape)
        new_m[n] = m_s[i].reshape(weights[n].shape)
        new_v[n] = v_s[i].reshape(weights[n].shape)

    return (loss, grad_x.reshape(NB, SEQ, D), *[grads[n] for n in names], *[delta[n] for n in names],
            *[new_m[n] for n in names], *[new_v[n] for n in names])


def _rowsum(a):
    def body(a_ref, o_ref):
        o_ref[...] = jnp.sum(a_ref[...], axis=1, keepdims=True)

    vm = BS(memory_space=pltpu.VMEM)
    return pl.pallas_call(body, in_specs=[vm], out_specs=vm, out_shape=_sds((a.shape[0], 1), F32),
                          compiler_params=_cp(), name="rowsum")(a)
```

```python
import functools

import jax
import jax.numpy as jnp
from jax import lax
from jax.experimental import pallas as pl
from jax.experimental.pallas import tpu as pltpu

F32 = jnp.float32
BF16 = jnp.bfloat16
BS = pl.BlockSpec
ANY = pl.BlockSpec(memory_space=pl.ANY)
MESH = pl.DeviceIdType.MESH

D = 1024
SEQ = 2048
NB = 2
T = NB * SEQ
MEMT = 256
HD = 64
TOK = 768
MEMW = 256
NA = 3 * TOK + MEMW
NBW = 2 * TOK + MEMW
FF = 2816
NSH = 4
FS = FF // NSH
FT = FF // 2
CONVW = 31
EPS = 1e-6
NEG = -1e30
SCALE = HD ** -0.5
QB = 256
KWIN = 768
KPAD = 512
TR = 512

ADAM_LR = 0.001
ADAM_B1 = 0.9
ADAM_B2 = 0.999
ADAM_EPS = 1e-08
ADAM_WD = 0.01
ADAM_STEP = 10

_DIMS = {
    "nn": (((1,), (0,)), ((), ())),
    "nt": (((1,), (1,)), ((), ())),
    "tn": (((0,), (0,)), ((), ())),
}


def _cp(sem=None, vmem_mb=48):
    return pltpu.CompilerParams(dimension_semantics=sem, vmem_limit_bytes=vmem_mb << 20)


def _sds(shape, dtype):
    return jax.ShapeDtypeStruct(tuple(shape), dtype)


def _mm(mode, a, b, *, grid, a_spec, b_spec, out_shape, out_spec, acc_shape, name,
        extras=(), extra_specs=(), epilogue=None, carry=None, vmem_mb=48, sequential=False):
    n_ex = len(extras)
    nk = grid[2]
    dims = _DIMS[mode]
    ni = len(carry.ins) if carry else 0
    no = len(carry.out_shapes) if carry else 0
    multi = isinstance(out_shape, (list, tuple))
    out_shapes = list(out_shape) if multi else [out_shape]
    out_specs = list(out_spec) if multi else [out_spec]
    n_o = len(out_shapes)

    def body(a_ref, b_ref, *rest):
        ex = rest[:n_ex]
        cin = rest[n_ex:n_ex + ni]
        o_refs = rest[n_ex + ni:n_ex + ni + n_o]
        o_ref = o_refs if multi else o_refs[0]
        cout = rest[n_ex + ni + n_o:n_ex + ni + n_o + no]
        acc = rest[n_ex + ni + n_o + no]
        cscr = rest[n_ex + ni + n_o + no + 1:]
        ids = [pl.program_id(d) for d in range(3)]
        k = ids[2]
        if carry:
            @pl.when((ids[0] == 0) & (ids[1] == 0) & (ids[2] == 0))
            def _():
                carry.start(cin, cout, cscr)

        prod = lax.dot_general(a_ref[...].astype(BF16), b_ref[...].astype(BF16), dims,
                               preferred_element_type=F32)

        def finish(val):
            if epilogue is None:
                o_ref[...] = val.astype(o_ref.dtype)
            else:
                epilogue(val, ex, o_ref, ids[0])

        if nk == 1:
            finish(prod)
        else:
            @pl.when(k == 0)
            def _():
                acc[...] = prod

            @pl.when((k > 0) & (k < nk - 1))
            def _():
                acc[...] += prod

            @pl.when(k == nk - 1)
            def _():
                finish(acc[...] + prod)

        if carry:
            @pl.when((ids[0] == grid[0] - 1) & (ids[1] == grid[1] - 1) & (ids[2] == grid[2] - 1))
            def _():
                carry.finish(cin, cout, cscr)

    acc_scratch = pltpu.VMEM(acc_shape if nk > 1 else (8, 128), F32)
    ordered = sequential or bool(carry)
    outs = pl.pallas_call(
        body, grid=grid,
        in_specs=[a_spec, b_spec, *extra_specs] + (carry.in_specs if carry else []),
        out_specs=out_specs + [ANY] * no, out_shape=out_shapes + (carry.out_shapes if carry else []),
        scratch_shapes=[acc_scratch] + (carry.scratch if carry else []),
        compiler_params=pltpu.CompilerParams(
            dimension_semantics=("arbitrary",) * 3 if ordered else ("parallel", "parallel", "arbitrary"),
            vmem_limit_bytes=vmem_mb << 20, has_side_effects=bool(carry)), name=name,
    )(a, b, *extras, *(carry.ins if carry else []))
    mine = list(outs[:n_o]) if multi else outs[0]
    return (mine, outs[n_o:]) if carry else mine


def _rms_fwd(x, g, name, carry=None):
    rows = x.shape[0]
    ni = len(carry.ins) if carry else 0
    no = len(carry.out_shapes) if carry else 0

    def body(x_ref, g_ref, *rest):
        cin, o_ref, cout, cscr = rest[:ni], rest[ni], rest[ni + 1:ni + 1 + no], rest[ni + 1 + no:]
        if carry:
            @pl.when(pl.program_id(0) == 0)
            def _():
                carry.start(cin, cout, cscr)

        xv = x_ref[...]
        r = lax.rsqrt(jnp.mean(xv * xv, axis=-1, keepdims=True) + EPS)
        o_ref[...] = (xv * r * g_ref[...]).astype(BF16)

        if carry:
            @pl.when(pl.program_id(0) == rows // TR - 1)
            def _():
                carry.finish(cin, cout, cscr)

    outs = pl.pallas_call(
        body, grid=(rows // TR,),
        in_specs=[BS((TR, D), lambda i: (i, 0)), BS((1, D), lambda i: (0, 0))]
        + (carry.in_specs if carry else []),
        out_specs=[BS((TR, D), lambda i: (i, 0))] + [ANY] * no,
        out_shape=[_sds((rows, D), BF16)] + (carry.out_shapes if carry else []),
        scratch_shapes=carry.scratch if carry else [],
        compiler_params=pltpu.CompilerParams(
            dimension_semantics=("arbitrary",), vmem_limit_bytes=48 << 20,
            has_side_effects=bool(carry)), name=name,
    )(x, g, *(carry.ins if carry else []))
    return (outs[0], outs[1:]) if carry else outs[0]


def _rms_bwd(dh, x, g, dres, name):
    rows = x.shape[0]
    has_res = dres is not None

    def body(*refs):
        if has_res:
            dh_ref, x_ref, g_ref, r_ref, dx_ref, dg_ref = refs
        else:
            dh_ref, x_ref, g_ref, dx_ref, dg_ref = refs
        xv = x_ref[...]
        dhv = dh_ref[...]
        r = lax.rsqrt(jnp.mean(xv * xv, axis=-1, keepdims=True) + EPS)
        xh = xv * r
        gy = dhv * g_ref[...]
        dx = r * (gy - xh * jnp.mean(gy * xh, axis=-1, keepdims=True))
        if has_res:
            dx = dx + r_ref[...]
        dx_ref[...] = dx

        @pl.when(pl.program_id(0) == 0)
        def _():
            dg_ref[...] = jnp.zeros_like(dg_ref)

        dg_ref[...] += jnp.sum(dhv * xh, axis=0, keepdims=True)

    row = BS((TR, D), lambda i: (i, 0))
    vec = BS((1, D), lambda i: (0, 0))
    ins = [dh, x, g] + ([dres] if has_res else [])
    return pl.pallas_call(
        body, grid=(rows // TR,),
        in_specs=[row, row, vec] + ([row] if has_res else []),
        out_specs=[row, vec], out_shape=[_sds((rows, D), F32), _sds((1, D), F32)],
        compiler_params=_cp(("arbitrary",)), name=name,
    )(*ins)


def _group_masks(width):
    lane = lax.broadcasted_iota(jnp.int32, (1, width), 1)
    return [(lane >= HD * h) & (lane < HD * (h + 1)) for h in range(width // HD)]


def _group_mean(v, masks):
    del masks
    low = lax.broadcasted_iota(jnp.int32, (1, 128), 1) < HD
    slabs = []
    for j in range(v.shape[-1] // 128):
        x = v[:, 128 * j:128 * (j + 1)]
        s0 = jnp.sum(jnp.where(low, x, 0.0), axis=-1, keepdims=True) * (1.0 / HD)
        s1 = jnp.sum(jnp.where(low, 0.0, x), axis=-1, keepdims=True) * (1.0 / HD)
        slabs.append(jnp.where(low, s0, s1))
    return slabs[0] if len(slabs) == 1 else jnp.concatenate(slabs, axis=-1)


def _head_norm(zv, g, masks):
    r = lax.rsqrt(_group_mean(zv * zv, masks) + EPS)
    return zv * r * g


def _head_norm_bwd(dy, zv, g, masks):
    r = lax.rsqrt(_group_mean(zv * zv, masks) + EPS)
    zh = zv * r
    gy = dy * g
    dz = r * (gy - zh * _group_mean(gy * zh, masks))
    return dz, jnp.sum(dy * zh, axis=0, keepdims=True)


def _fold_heads(v, width):
    vb = jnp.broadcast_to(v, (8, width))
    out = vb
    for h in range(1, width // HD):
        out = out + pltpu.roll(vb, width - HD * h, axis=1)
    return out[0:1]


def _bias_expand(u):
    def body(u_ref, o_ref):
        x = jnp.broadcast_to(u_ref[...], (QB, 1024))
        rolled = pltpu.roll(x, 1024 - (QB - 1), axis=1, stride=1, stride_axis=0)[:, :KWIN]
        row = lax.broadcasted_iota(jnp.int32, (QB, 1), 0)
        col = lax.broadcasted_iota(jnp.int32, (1, KWIN), 1)
        lo = (row // 64) * 64
        ok = (col >= lo) & (col < lo + 576)
        o_ref[...] = jnp.where(ok, rolled, NEG)

    return pl.pallas_call(
        body, grid=(12,), in_specs=[BS((None, 1, 1024), lambda h: (h, 0, 0))],
        out_specs=BS((None, QB, KWIN), lambda h: (h, 0, 0)), out_shape=_sds((12, QB, KWIN), F32),
        compiler_params=_cp(("arbitrary",)), name="bias_expand",
    )(u)


def _bias_reduce(ds):
    def body(d_ref, o_ref):
        ri = lax.broadcasted_iota(jnp.int32, (QB, QB), 0)
        ci = lax.broadcasted_iota(jnp.int32, (QB, QB), 1)
        flip = (ri + ci == QB - 1).astype(F32)
        drev = jnp.dot(flip, d_ref[...], precision=lax.Precision.HIGHEST, preferred_element_type=F32)
        x = jnp.concatenate([drev, jnp.zeros((QB, 1024 - KWIN), F32)], axis=1)
        rolled = pltpu.roll(x, 0, axis=1, stride=1, stride_axis=0)
        o_ref[...] = jnp.sum(rolled, axis=0, keepdims=True)

    return pl.pallas_call(
        body, grid=(12,), in_specs=[BS((None, QB, KWIN), lambda h: (h, 0, 0))],
        out_specs=BS((None, 1, 1024), lambda h: (h, 0, 0)), out_shape=_sds((12, 1, 1024), F32),
        compiler_params=_cp(("arbitrary",)), name="bias_reduce",
    )(ds)


AW = 256
AH = AW // HD
AG = TOK // AW
def _attn_softmax(qh, kw, bias, startadd):
    s = lax.dot_general(qh, kw, _DIMS["nt"], preferred_element_type=F32) + bias + startadd
    m = jnp.max(s, axis=-1, keepdims=True)
    p = jnp.exp(s - m)
    return p * (1.0 / jnp.sum(p, axis=-1, keepdims=True))


def _attn_prologue(q_ref, k_ref, v_ref, gq_ref, gk_ref, qn_s, kn_s, v_s, masks):
    kn_s[0:KPAD, :] = jnp.zeros((KPAD, AW), BF16)
    v_s[0:KPAD, :] = jnp.zeros((KPAD, AW), BF16)
    for r in range(0, SEQ, TR):
        qn_s[r:r + TR, :] = (_head_norm(q_ref[r:r + TR, :], gq_ref[...], masks) * SCALE).astype(BF16)
        kn_s[KPAD + r:KPAD + r + TR, :] = _head_norm(k_ref[r:r + TR, :], gk_ref[...], masks).astype(BF16)
        v_s[KPAD + r:KPAD + r + TR, :] = v_ref[r:r + TR, :].astype(BF16)


def _attn_fwd(z, bias, gq2, gk2, carry=None):
    ni = len(carry.ins) if carry else 0
    no = len(carry.out_shapes) if carry else 0

    def body(q_ref, k_ref, v_ref, b_ref, gq_ref, gk_ref, *rest):
        cin, o_ref, cout = rest[:ni], rest[ni], rest[ni + 1:ni + 1 + no]
        qn_s, kn_s, v_s = rest[ni + 1 + no:ni + 4 + no]
        cscr = rest[ni + 4 + no:]
        if carry:
            @pl.when((pl.program_id(0) == 0) & (pl.program_id(1) == 0))
            def _():
                carry.start(cin, cout, cscr)

        masks = _group_masks(AW)
        _attn_prologue(q_ref, k_ref, v_ref, gq_ref, gk_ref, qn_s, kn_s, v_s, masks)
        col = lax.broadcasted_iota(jnp.int32, (1, KWIN), 1)

        def blk(i, carry):
            r0 = pl.multiple_of(i * QB, QB)
            qb = qn_s[pl.ds(r0, QB), :]
            kw = kn_s[pl.ds(r0, KWIN), :]
            vw = v_s[pl.ds(r0, KWIN), :]
            startadd = jnp.where(col + r0 < KPAD, NEG, 0.0)
            o = jnp.zeros((QB, AW), F32)
            for h in range(AH):
                qh = jnp.where(masks[h], qb, jnp.zeros_like(qb))
                vh = jnp.where(masks[h], vw, jnp.zeros_like(vw))
                p = _attn_softmax(qh, kw, b_ref[h], startadd).astype(BF16)
                o = o + jnp.dot(p, vh, preferred_element_type=F32)
            o_ref[pl.ds(r0, QB), :] = o.astype(BF16)
            return carry

        lax.fori_loop(0, SEQ // QB, blk, 0)

        if carry:
            @pl.when((pl.program_id(0) == NB - 1) & (pl.program_id(1) == AG - 1))
            def _():
                carry.finish(cin, cout, cscr)

    vec = BS((1, AW), lambda b, hp: (0, 0))
    outs = pl.pallas_call(
        body, grid=(NB, AG),
        in_specs=[BS((SEQ, AW), lambda b, hp: (b, hp)),
                  BS((SEQ, AW), lambda b, hp: (b, AG + hp)),
                  BS((SEQ, AW), lambda b, hp: (b, 2 * AG + hp)),
                  BS((AH, QB, KWIN), lambda b, hp: (hp, 0, 0)), vec, vec]
        + (carry.in_specs if carry else []),
        out_specs=[BS((SEQ, AW), lambda b, hp: (b, hp))] + [ANY] * no,
        out_shape=[_sds((T, D), BF16)] + (carry.out_shapes if carry else []),
        scratch_shapes=[pltpu.VMEM((SEQ, AW), BF16), pltpu.VMEM((SEQ + KPAD, AW), BF16),
                        pltpu.VMEM((SEQ + KPAD, AW), BF16)] + (carry.scratch if carry else []),
        compiler_params=pltpu.CompilerParams(
            dimension_semantics=("arbitrary", "arbitrary"), vmem_limit_bytes=48 << 20,
            has_side_effects=bool(carry)), name="attn_fwd",
    )(z, z, z, bias, gq2, gk2, *(carry.ins if carry else []))
    return outs[0], outs[1:]


def _attn_bwd(z, dcat, bias, gq2, gk2, carry=None):
    ni = len(carry.ins) if carry else 0
    no = len(carry.out_shapes) if carry else 0

    def body(q_ref, k_ref, v_ref, do_ref, b_ref, gq_ref, gk_ref, *rest):
        cin = rest[:ni]
        dq_ref, dk_ref, dv_ref, db_ref, dgq_ref, dgk_ref = rest[ni:ni + 6]
        cout = rest[ni + 6:ni + 6 + no]
        qn_s, kn_s, v_s, dqn_s, dkn_s, dv_s = rest[ni + 6 + no:ni + 12 + no]
        cscr = rest[ni + 12 + no:]
        hp = pl.program_id(0)
        b = pl.program_id(1)
        if carry:
            @pl.when((hp == 0) & (b == 0))
            def _():
                carry.start(cin, cout, cscr)

        masks = _group_masks(AW)
        _attn_prologue(q_ref, k_ref, v_ref, gq_ref, gk_ref, qn_s, kn_s, v_s, masks)
        dkn_s[...] = jnp.zeros_like(dkn_s)
        dv_s[...] = jnp.zeros_like(dv_s)

        @pl.when(b == 0)
        def _():
            db_ref[...] = jnp.zeros_like(db_ref)

        @pl.when((b == 0) & (hp == 0))
        def _():
            dgq_ref[...] = jnp.zeros_like(dgq_ref)
            dgk_ref[...] = jnp.zeros_like(dgk_ref)

        col = lax.broadcasted_iota(jnp.int32, (1, KWIN), 1)

        def blk(i, carry):
            r0 = pl.multiple_of(i * QB, QB)
            qb = qn_s[pl.ds(r0, QB), :]
            kw = kn_s[pl.ds(r0, KWIN), :]
            vw = v_s[pl.ds(r0, KWIN), :]
            dob = do_ref[pl.ds(r0, QB), :].astype(BF16)
            startadd = jnp.where(col + r0 < KPAD, NEG, 0.0)
            dqn = jnp.zeros((QB, AW), F32)
            dkw = jnp.zeros((KWIN, AW), F32)
            dvw = jnp.zeros((KWIN, AW), F32)
            for h in range(AH):
                qh = jnp.where(masks[h], qb, jnp.zeros_like(qb))
                kh = jnp.where(masks[h], kw, jnp.zeros_like(kw))
                doh = jnp.where(masks[h], dob, jnp.zeros_like(dob))
                p = _attn_softmax(qh, kw, b_ref[h], startadd)
                dvw = dvw + lax.dot_general(p.astype(BF16), doh, _DIMS["tn"],
                                            preferred_element_type=F32)
                dp = lax.dot_general(doh, vw, _DIMS["nt"], preferred_element_type=F32)
                ds = p * (dp - jnp.sum(dp * p, axis=-1, keepdims=True))
                db_ref[h] += ds
                dsb = ds.astype(BF16)
                dqn = dqn + jnp.dot(dsb, kh, preferred_element_type=F32)
                dkw = dkw + lax.dot_general(dsb, qh, _DIMS["tn"], preferred_element_type=F32)
            dqn_s[pl.ds(r0, QB), :] = dqn * SCALE
            dkn_s[pl.ds(r0, KWIN), :] += dkw
            dv_s[pl.ds(r0, KWIN), :] += dvw
            return carry

        lax.fori_loop(0, SEQ // QB, blk, 0)

        dgq = jnp.zeros((1, AW), F32)
        dgk = jnp.zeros((1, AW), F32)
        for r in range(0, SEQ, TR):
            dq, dg = _head_norm_bwd(dqn_s[r:r + TR, :], q_ref[r:r + TR, :], gq_ref[...], masks)
            dq_ref[r:r + TR, :] = dq.astype(BF16)
            dgq = dgq + dg
            dk, dg = _head_norm_bwd(dkn_s[KPAD + r:KPAD + r + TR, :], k_ref[r:r + TR, :], gk_ref[...], masks)
            dk_ref[r:r + TR, :] = dk.astype(BF16)
            dgk = dgk + dg
            dv_ref[r:r + TR, :] = dv_s[KPAD + r:KPAD + r + TR, :].astype(BF16)
        dgq_ref[...] += _fold_heads(dgq, AW)
        dgk_ref[...] += _fold_heads(dgk, AW)

        if carry:
            @pl.when((hp == AG - 1) & (b == NB - 1))
            def _():
                carry.finish(cin, cout, cscr)

    vec = BS((1, AW), lambda hp, b: (0, 0))
    row = BS((SEQ, AW), lambda hp, b: (b, hp))
    outs = pl.pallas_call(
        body, grid=(AG, NB),
        in_specs=[row,
                  BS((SEQ, AW), lambda hp, b: (b, AG + hp)),
                  BS((SEQ, AW), lambda hp, b: (b, 2 * AG + hp)),
                  row,
                  BS((AH, QB, KWIN), lambda hp, b: (hp, 0, 0)), vec, vec]
        + (carry.in_specs if carry else []),
        out_specs=[row, row, row, BS((AH, QB, KWIN), lambda hp, b: (hp, 0, 0)), vec, vec] + [ANY] * no,
        out_shape=[_sds((T, TOK), BF16), _sds((T, TOK), BF16), _sds((T, TOK), BF16),
                   _sds((12, QB, KWIN), F32), _sds((1, AW), F32), _sds((1, AW), F32)]
        + (carry.out_shapes if carry else []),
        scratch_shapes=[pltpu.VMEM((SEQ, AW), BF16), pltpu.VMEM((SEQ + KPAD, AW), BF16),
                        pltpu.VMEM((SEQ + KPAD, AW), BF16), pltpu.VMEM((SEQ, AW), F32),
                        pltpu.VMEM((SEQ + KPAD, AW), F32), pltpu.VMEM((SEQ + KPAD, AW), F32)]
        + (carry.scratch if carry else []),
        compiler_params=pltpu.CompilerParams(
            dimension_semantics=("arbitrary", "arbitrary"), vmem_limit_bytes=58 << 20,
            has_side_effects=bool(carry)), name="attn_bwd",
    )(z, z, z, dcat, bias, gq2, gk2, *(carry.ins if carry else []))
    return outs[:6], outs[6:]


def _mem_softmax(qh, kn):
    s = lax.dot_general(qh, kn, _DIMS["nt"], preferred_element_type=F32)
    m = jnp.max(s, axis=-1, keepdims=True)
    p = jnp.exp(s - m)
    return p * (1.0 / jnp.sum(p, axis=-1, keepdims=True))


def _memattn_fwd(z, kv, cat, gq4, gk4, qcol, name):
    def body(q_ref, k_ref, v_ref, gq_ref, gk_ref, cat_ref, o_ref):
        del cat_ref
        masks = _group_masks(MEMW)
        qn = (_head_norm(q_ref[...], gq_ref[...], masks) * SCALE).astype(BF16)
        kn = _head_norm(k_ref[...], gk_ref[...], masks).astype(BF16)
        vv = v_ref[...].astype(BF16)
        o = jnp.zeros((TR, MEMW), F32)
        for h in range(4):
            qh = jnp.where(masks[h], qn, jnp.zeros_like(qn))
            vh = jnp.where(masks[h], vv, jnp.zeros_like(vv))
            p = _mem_softmax(qh, kn).astype(BF16)
            o = o + jnp.dot(p, vh, preferred_element_type=F32)
        o_ref[...] = o.astype(BF16)

    nt = SEQ // TR
    vec = BS((1, MEMW), lambda b, t: (0, 0))
    return pl.pallas_call(
        body, grid=(NB, nt),
        in_specs=[BS((TR, MEMW), lambda b, t: (b * nt + t, qcol)),
                  BS((MEMT, MEMW), lambda b, t: (b, 0)),
                  BS((MEMT, MEMW), lambda b, t: (b, 1)), vec, vec, ANY],
        out_specs=BS((TR, MEMW), lambda b, t: (b * nt + t, 3)),
        out_shape=_sds((T, D), BF16), input_output_aliases={5: 0},
        compiler_params=_cp(("arbitrary", "arbitrary")), name=name,
    )(z, kv, kv, gq4, gk4, cat)


def _memattn_bwd(z, kv, dcat, gq4, gk4, qcol, name):
    nt = SEQ // TR

    def body(q_ref, k_ref, v_ref, do_ref, gq_ref, gk_ref,
             dq_ref, dkv_ref, dgq_ref, dgk_ref, dkn_s, dv_s):
        b = pl.program_id(0)
        t = pl.program_id(1)
        masks = _group_masks(MEMW)
        qz = q_ref[...]
        kz = k_ref[...]
        qn = (_head_norm(qz, gq_ref[...], masks) * SCALE).astype(BF16)
        kn = _head_norm(kz, gk_ref[...], masks).astype(BF16)
        vv = v_ref[...].astype(BF16)
        dob = do_ref[...].astype(BF16)

        @pl.when(t == 0)
        def _():
            dkn_s[...] = jnp.zeros_like(dkn_s)
            dv_s[...] = jnp.zeros_like(dv_s)

        @pl.when((t == 0) & (b == 0))
        def _():
            dgq_ref[...] = jnp.zeros_like(dgq_ref)
            dgk_ref[...] = jnp.zeros_like(dgk_ref)

        dqn = jnp.zeros((TR, MEMW), F32)
        dkn = jnp.zeros((MEMT, MEMW), F32)
        dvv = jnp.zeros((MEMT, MEMW), F32)
        for h in range(4):
            qh = jnp.where(masks[h], qn, jnp.zeros_like(qn))
            kh = jnp.where(masks[h], kn, jnp.zeros_like(kn))
            doh = jnp.where(masks[h], dob, jnp.zeros_like(dob))
            p = _mem_softmax(qh, kn)
            dvv = dvv + lax.dot_general(p.astype(BF16), doh, _DIMS["tn"], preferred_element_type=F32)
            dp = lax.dot_general(doh, vv, _DIMS["nt"], preferred_element_type=F32)
            ds = p * (dp - jnp.sum(dp * p, axis=-1, keepdims=True))
            dsb = ds.astype(BF16)
            dqn = dqn + jnp.dot(dsb, kh, preferred_element_type=F32)
            dkn = dkn + lax.dot_general(dsb, qh, _DIMS["tn"], preferred_element_type=F32)
        dkn_s[...] += dkn
        dv_s[...] += dvv
        dq, dgq = _head_norm_bwd(dqn * SCALE, qz, gq_ref[...], masks)
        dq_ref[...] = dq.astype(BF16)
        dgq_ref[...] += _fold_heads(dgq, MEMW)

        @pl.when(t == nt - 1)
        def _():
            dk, dgk = _head_norm_bwd(dkn_s[...], kz, gk_ref[...], masks)
            dkv_ref[:, 0:MEMW] = dk
            dkv_ref[:, MEMW:] = dv_s[...]
            dgk_ref[...] += _fold_heads(dgk, MEMW)

    vec = BS((1, MEMW), lambda b, t: (0, 0))
    return pl.pallas_call(
        body, grid=(NB, nt),
        in_specs=[BS((TR, MEMW), lambda b, t: (b * nt + t, qcol)),
                  BS((MEMT, MEMW), lambda b, t: (b, 0)),
                  BS((MEMT, MEMW), lambda b, t: (b, 1)),
                  BS((TR, MEMW), lambda b, t: (b * nt + t, 3)), vec, vec],
        out_specs=[BS((TR, MEMW), lambda b, t: (b * nt + t, 0)),
                   BS((MEMT, 2 * MEMW), lambda b, t: (b, 0)), vec, vec],
        out_shape=[_sds((T, MEMW), BF16), _sds((NB * MEMT, 2 * MEMW), F32),
                   _sds((1, MEMW), F32), _sds((1, MEMW), F32)],
        scratch_shapes=[pltpu.VMEM((MEMT, MEMW), F32), pltpu.VMEM((MEMT, MEMW), F32)],
        compiler_params=_cp(("arbitrary", "arbitrary")), name=name,
    )(z, kv, kv, dcat, gq4, gk4)


HALO = 32
NEXT = 64
RT = 64


def _glu(zz):
    return zz[:, :TOK] * jax.nn.sigmoid(zz[:, TOK:])


def _layer_norm_parts(y):
    mu = jnp.mean(y, axis=-1, keepdims=True)
    yc = y - mu
    rstd = lax.rsqrt(jnp.mean(yc * yc, axis=-1, keepdims=True) + EPS)
    return yc * rstd, rstd


def _shifted_copies(src, dst, rows):
    for b in range(1, 8):
        dst[b - 1, 0:rows, :] = src[b:b + rows, :]


def _tap(src, shifted, off, r0, rows):
    b = off % 8
    if b == 0:
        return src[r0 + off:r0 + off + rows, :]
    return shifted[b - 1, r0 + off - b:r0 + off - b + rows, :]


def _conv_rows(w_ref, hbuf, hs, r0, rows):
    y = jnp.zeros((rows, TOK), F32)
    for j in range(CONVW):
        y = y + w_ref[j:j + 1, :] * _tap(hbuf, hs, (HALO - CONVW + 1) + j, r0, rows)
    return y


def _conv_fwd(z, cw, cb, lg, lb):
    nt = SEQ // TR

    def body(zc_ref, zp_ref, w_ref, cb_ref, lg_ref, lb_ref, o_ref, y_ref, hbuf, hs):
        t = pl.program_id(1)
        hbuf[0:HALO, :] = jnp.where(t == 0, 0.0, _glu(zp_ref[...]))
        hbuf[HALO:, :] = _glu(zc_ref[...])
        _shifted_copies(hbuf, hs, HALO + TR - 8)
        for r0 in range(0, TR, RT):
            y = _conv_rows(w_ref, hbuf, hs, r0, RT) + cb_ref[...]
            y_ref[r0:r0 + RT, :] = y
            yh, _ = _layer_norm_parts(y)
            o = yh * lg_ref[...] + lb_ref[...]
            o_ref[r0:r0 + RT, :] = (o * jax.nn.sigmoid(o)).astype(BF16)

    vec = BS((1, TOK), lambda b, t: (0, 0))
    per = TR // HALO
    return pl.pallas_call(
        body, grid=(NB, nt),
        in_specs=[BS((TR, 2 * TOK), lambda b, t: (b * nt + t, 0)),
                  BS((HALO, 2 * TOK), lambda b, t: (jnp.maximum((b * nt + t) * per - 1, 0), 0)),
                  BS((32, TOK), lambda b, t: (0, 0)), vec, vec, vec],
        out_specs=[BS((TR, TOK), lambda b, t: (b * nt + t, 0)), BS((TR, TOK), lambda b, t: (b * nt + t, 0))],
        out_shape=[_sds((T, D), BF16), _sds((T, TOK), F32)],
        scratch_shapes=[pltpu.VMEM((HALO + TR, TOK), F32), pltpu.VMEM((7, HALO + TR, TOK), F32)],
        compiler_params=_cp(("arbitrary", "arbitrary")), name="conv_fwd",
    )(z, z, cw, cb, lg, lb)


def _conv_bwd(z, y, dcat, cw, lg, lb):
    nt = SEQ // TR
    ext = TR + NEXT

    def body(zc_ref, zp_ref, yc_ref, yn_ref, dc_ref, dn_ref, w_ref, lg_ref, lb_ref,
             du_ref, dw_ref, dcb_ref, dlg_ref, dlb_ref, dbin_ref, hbuf, dybuf, hs, dys):
        b = pl.program_id(0)
        t = pl.program_id(1)

        @pl.when((b == 0) & (t == 0))
        def _():
            dw_ref[...] = jnp.zeros_like(dw_ref)
            dcb_ref[...] = jnp.zeros_like(dcb_ref)
            dlg_ref[...] = jnp.zeros_like(dlg_ref)
            dlb_ref[...] = jnp.zeros_like(dlb_ref)
            dbin_ref[...] = jnp.zeros_like(dbin_ref)

        hbuf[0:HALO, :] = jnp.where(t == 0, 0.0, _glu(zp_ref[...]))
        hbuf[HALO:, :] = _glu(zc_ref[...])
        _shifted_copies(hbuf, hs, HALO + TR - 8)
        last = t == nt - 1
        for r0 in range(0, ext, RT):
            yh, rstd = _layer_norm_parts(yc_ref[r0:r0 + RT, :] if r0 < TR else yn_ref[...])
            o = yh * lg_ref[...] + lb_ref[...]
            sg = jax.nn.sigmoid(o)
            if r0 < TR:
                dtok = dc_ref[r0:r0 + RT, :]
            else:
                dtok = jnp.where(last, 0.0, dn_ref[...])
            do = dtok * (sg * (1.0 + o * (1.0 - sg)))
            dyh = do * lg_ref[...]
            dy = rstd * (dyh - jnp.mean(dyh, axis=-1, keepdims=True)
                         - yh * jnp.mean(dyh * yh, axis=-1, keepdims=True))
            dybuf[r0:r0 + RT, :] = dy
            if r0 < TR:
                dlg_ref[...] += jnp.sum(do * yh, axis=0, keepdims=True)
                dlb_ref[...] += jnp.sum(do, axis=0, keepdims=True)
                dcb_ref[...] += jnp.sum(dy, axis=0, keepdims=True)
        _shifted_copies(dybuf, dys, ext - 8)
        for r0 in range(0, TR, RT):
            dh = jnp.zeros((RT, TOK), F32)
            for j in range(CONVW):
                dh = dh + w_ref[j:j + 1, :] * _tap(dybuf, dys, (CONVW - 1) - j, r0, RT)
            a = zc_ref[r0:r0 + RT, 0:TOK]
            sg = jax.nn.sigmoid(zc_ref[r0:r0 + RT, TOK:])
            da = dh * sg
            dg = dh * a * (sg * (1.0 - sg))
            du_ref[r0:r0 + RT, 0:TOK] = da.astype(BF16)
            du_ref[r0:r0 + RT, TOK:] = dg.astype(BF16)
            dbin_ref[:, 0:TOK] += jnp.sum(da, axis=0, keepdims=True)
            dbin_ref[:, TOK:] += jnp.sum(dg, axis=0, keepdims=True)
        for j in range(CONVW):
            acc = jnp.zeros((8, TOK), F32)
            for r0 in range(0, TR, RT):
                prod = dybuf[r0:r0 + RT, :] * _tap(hbuf, hs, (HALO - CONVW + 1) + j, r0, RT)
                acc = acc + jnp.sum(prod.reshape(RT // 8, 8, TOK), axis=0)
            dw_ref[j:j + 1, :] += jnp.sum(acc, axis=0, keepdims=True)

    vec = BS((1, TOK), lambda b, t: (0, 0))
    perh = TR // HALO
    pern = TR // NEXT
    nlast_n = T // NEXT - 1
    return pl.pallas_call(
        body, grid=(NB, nt),
        in_specs=[BS((TR, 2 * TOK), lambda b, t: (b * nt + t, 0)),
                  BS((HALO, 2 * TOK), lambda b, t: (jnp.maximum((b * nt + t) * perh - 1, 0), 0)),
                  BS((TR, TOK), lambda b, t: (b * nt + t, 0)),
                  BS((NEXT, TOK), lambda b, t: (jnp.minimum((b * nt + t + 1) * pern, nlast_n), 0)),
                  BS((TR, TOK), lambda b, t: (b * nt + t, 0)),
                  BS((NEXT, TOK), lambda b, t: (jnp.minimum((b * nt + t + 1) * pern, nlast_n), 0)),
                  BS((32, TOK), lambda b, t: (0, 0)), vec, vec],
        out_specs=[BS((TR, 2 * TOK), lambda b, t: (b * nt + t, 0)),
                   BS((32, TOK), lambda b, t: (0, 0)), vec, vec, vec,
                   BS((1, 2 * TOK), lambda b, t: (0, 0))],
        out_shape=[_sds((T, 2 * TOK), BF16), _sds((32, TOK), F32), _sds((1, TOK), F32),
                   _sds((1, TOK), F32), _sds((1, TOK), F32), _sds((1, 2 * TOK), F32)],
        scratch_shapes=[pltpu.VMEM((HALO + TR, TOK), F32), pltpu.VMEM((ext, TOK), F32),
                        pltpu.VMEM((7, HALO + TR, TOK), F32), pltpu.VMEM((7, ext, TOK), F32)],
        compiler_params=_cp(("arbitrary", "arbitrary"), vmem_mb=56), name="conv_bwd",
    )(z, z, y, y, dcat, dcat, cw, lg, lb)


def _ffn_up(h2, wgu, l, carry=None):
    ni = len(carry.ins) if carry else 0
    no = len(carry.out_shapes) if carry else 0

    def body(h_ref, wg_ref, wu_ref, *rest):
        cin = rest[:ni]
        g_ref, u_ref, a_ref = rest[ni:ni + 3]
        cout, cscr = rest[ni + 3:ni + 3 + no], rest[ni + 3 + no:]
        if carry:
            @pl.when((pl.program_id(0) == 0) & (pl.program_id(1) == 0))
            def _():
                carry.start(cin, cout, cscr)

        hv = h_ref[...]
        g = lax.dot_general(hv, wg_ref[...], _DIMS["nt"], preferred_element_type=F32)
        u = lax.dot_general(hv, wu_ref[...], _DIMS["nt"], preferred_element_type=F32)
        sg = jax.nn.sigmoid(g)
        silu = g * sg
        g_ref[...] = (u * (sg * (1.0 + g * (1.0 - sg)))).astype(BF16)
        u_ref[...] = silu.astype(BF16)
        a_ref[...] = (silu * u).astype(BF16)

        if carry:
            @pl.when((pl.program_id(0) == FF // FT - 1) & (pl.program_id(1) == T // TR - 1))
            def _():
                carry.finish(cin, cout, cscr)

    out = BS((TR, FT), lambda q, i: (i, q))
    outs = pl.pallas_call(
        body, grid=(FF // FT, T // TR),
        in_specs=[BS((TR, D), lambda q, i: (i, 0)),
                  BS((None, FT, D), lambda q, i: (0, q, 0)),
                  BS((None, FT, D), lambda q, i: (1, q, 0))] + (carry.in_specs if carry else []),
        out_specs=[out, out, out] + [ANY] * no,
        out_shape=[_sds((T, FF), BF16), _sds((T, FF), BF16), _sds((T, FF), BF16)]
        + (carry.out_shapes if carry else []),
        scratch_shapes=carry.scratch if carry else [],
        compiler_params=pltpu.CompilerParams(
            dimension_semantics=("arbitrary", "arbitrary"), vmem_limit_bytes=48 << 20,
            has_side_effects=bool(carry)), name=f"ffn_up_{l}",
    )(h2, wgu, wgu, *(carry.ins if carry else []))
    return outs[:3], outs[3:]


def _ffn_down_bwd(dx, wd, g, u, l):
    def epilogue(dact, ex, o_ref, i):
        o_ref[0] = (dact * ex[0][...].astype(F32)).astype(BF16)
        o_ref[1] = (dact * ex[1][...].astype(F32)).astype(BF16)

    ex_spec = BS((TR, FT), lambda i, q, k: (i, q))
    return _mm("nt", dx, wd, grid=(T // TR, FF // FT, 1),
               a_spec=BS((TR, D), lambda i, q, k: (i, 0)),
               b_spec=BS((FT, D), lambda i, q, k: (q, 0)),
               out_shape=_sds((2, T, FF), BF16),
               out_spec=BS((2, TR, FT), lambda i, q, k: (0, i, q)),
               acc_shape=(TR, FT), extras=(g, u), extra_specs=(ex_spec, ex_spec),
               epilogue=epilogue, name=f"ffn_down_bwd_{l}")


def _row_tile(rows, cols, itemsize=4, limit=2 << 20):
    tr = rows
    while tr * cols * itemsize > limit and tr % 2 == 0 and (tr // 2) % 16 == 0:
        tr //= 2
    return tr


def _cast_bf16(arrs, name):
    n = len(arrs)
    rows, cols = arrs[0].shape
    tr = _row_tile(rows, cols)

    def body(*refs):
        o_ref = refs[n]
        k = pl.program_id(0)
        val = refs[0][...]
        for j in range(1, n):
            val = jnp.where(k == j, refs[j][...], val)
        o_ref[...] = val.astype(BF16)

    return pl.pallas_call(
        body, grid=(n, rows // tr),
        in_specs=[BS((tr, cols), lambda k, i: (i, 0))] * n,
        out_specs=BS((None, tr, cols), lambda k, i: (k, i, 0)),
        out_shape=_sds((n, rows, cols), BF16),
        compiler_params=_cp(("arbitrary", "arbitrary")), name=name,
    )(*arrs)


def _quad_sum(own, got, name):
    n, rows, cols = own.shape
    tr = _row_tile(rows, cols)

    def body(a_ref, q_ref, o_ref):
        o_ref[...] = ((a_ref[...] + q_ref[0].astype(F32)) + q_ref[1].astype(F32)) + q_ref[2].astype(F32)

    spec = BS((None, tr, cols), lambda k, i: (k, i, 0))
    return pl.pallas_call(
        body, grid=(n, rows // tr),
        in_specs=[spec, BS((3, None, tr, cols), lambda k, i: (0, k, i, 0))], out_specs=spec,
        out_shape=_sds((n, rows, cols), F32),
        compiler_params=_cp(("arbitrary", "arbitrary")), name=name,
    )(own, got)


def _adam_math(w, g, m, v):
    m = ADAM_B1 * m + (1.0 - ADAM_B1) * g
    v = ADAM_B2 * v + (1.0 - ADAM_B2) * (g * g)
    m_hat = m / (1.0 - ADAM_B1 ** ADAM_STEP)
    v_hat = v / (1.0 - ADAM_B2 ** ADAM_STEP)
    delta = -ADAM_LR * (m_hat / (jnp.sqrt(v_hat) + ADAM_EPS) + ADAM_WD * w)
    return delta, m, v


def _adamw_big(w, g, m, v, name):
    shape = w.shape
    cols = shape[-1]
    rows = w.size // cols
    tr = _row_tile(rows, cols, limit=1 << 20)

    def body(w_ref, g_ref, m_ref, v_ref, d_ref, nm_ref, nv_ref):
        d, nm, nv = _adam_math(w_ref[...], g_ref[...], m_ref[...], v_ref[...])
        d_ref[...] = d
        nm_ref[...] = nm
        nv_ref[...] = nv

    spec = BS((tr, cols), lambda i: (i, 0))
    outs = pl.pallas_call(
        body, grid=(rows // tr,), in_specs=[spec] * 4, out_specs=[spec] * 3,
        out_shape=[_sds((rows, cols), F32)] * 3,
        compiler_params=_cp(("arbitrary",)), name=name,
    )(*[a.reshape(rows, cols) for a in (w, g, m, v)])
    return [o.reshape(shape) for o in outs]


def _adamw_small(ws, gs, ms, vs):
    n = len(ws)

    def body(*refs):
        for i in range(n):
            d, nm, nv = _adam_math(refs[i][...], refs[n + i][...], refs[2 * n + i][...],
                                   refs[3 * n + i][...])
            refs[4 * n + i][...] = d
            refs[5 * n + i][...] = nm
            refs[6 * n + i][...] = nv

    specs = [BS(w.shape, lambda i: (0, 0)) for w in ws]
    outs = pl.pallas_call(
        body, grid=(1,), in_specs=specs * 4, out_specs=specs * 3,
        out_shape=[_sds(w.shape, F32) for w in ws] * 3,
        compiler_params=_cp(("arbitrary",)), name="adamw_small",
    )(*ws, *gs, *ms, *vs)
    return outs[:n], outs[n:2 * n], outs[2 * n:]


def _place():
    x, y, c = lax.axis_index("x"), lax.axis_index("y"), lax.axis_index("c")
    chips = [(1 - x, y), (x, 1 - y), (1 - x, 1 - y)]
    return x, y, c, chips


class _Exchange:
    def __init__(self, ins, in_specs, out_shapes, scratch, start, finish):
        self.ins, self.in_specs, self.out_shapes, self.scratch = ins, in_specs, out_shapes, scratch
        self.start, self.finish = start, finish


def _run_exchange(ex, name, vmem_mb=40):
    ni, no = len(ex.ins), len(ex.out_shapes)

    def body(*refs):
        ex.start(refs[:ni], refs[ni:ni + no], refs[ni + no:])
        ex.finish(refs[:ni], refs[ni:ni + no], refs[ni + no:])

    return pl.pallas_call(
        body, in_specs=ex.in_specs, out_specs=[ANY] * no, out_shape=ex.out_shapes,
        scratch_shapes=ex.scratch,
        compiler_params=pltpu.CompilerParams(has_side_effects=True, vmem_limit_bytes=vmem_mb << 20),
        name=name,
    )(*ex.ins)


def _gather_exchange(srcs, dst_shapes, views, small=None):
    nu = len(srcs)
    nd = len(dst_shapes)
    ns = 1 if small is not None else 0

    def unpack(ins, outs, scr):
        x, y, c, chips = _place()
        src = ins[:nu]
        vw = [views[u](outs[:nd]) for u in range(nu)]
        vbuf = scr[:nu]
        send, recv, fsend, frecv, lsem, ssend, srecv, vsem = scr[nu:]

        def ici(u, j, shard, to):
            return pltpu.make_async_remote_copy(
                src_ref=vbuf[u].at[:, c], dst_ref=vw[u].at[:, shard, c],
                send_sem=send.at[3 * u + j], recv_sem=recv.at[3 * u + j],
                device_id=to, device_id_type=MESH)

        def fwd(u, j, shard, half):
            return pltpu.make_async_remote_copy(
                src_ref=vw[u].at[:, shard, half], dst_ref=vw[u].at[:, shard, half],
                send_sem=fsend.at[3 * u + j], recv_sem=frecv.at[3 * u + j],
                device_id=(x, y, 1 - c), device_id_type=MESH)

        def small_copy(j, shard, to):
            return pltpu.make_async_remote_copy(
                src_ref=ins[nu], dst_ref=outs[nd].at[shard],
                send_sem=ssend.at[j], recv_sem=srecv.at[j], device_id=to, device_id_type=MESH)

        stage = [pltpu.make_async_copy(src[u], vbuf[u], vsem.at[u]) for u in range(nu)]
        local = [pltpu.make_async_copy(vbuf[u], vw[u].at[:, 2 * x + y], lsem.at[u]) for u in range(nu)]
        if ns:
            local.append(pltpu.make_async_copy(ins[nu], outs[nd].at[2 * x + y], lsem.at[nu]))
        return x, y, c, chips, ici, fwd, small_copy, stage, local

    def start(ins, outs, scr):
        x, y, c, chips, ici, fwd, small_copy, stage, local = unpack(ins, outs, scr)
        s = 2 * x + y
        for cp in stage:
            cp.start()
        if ns:
            local[nu].start()
            for j, chip in enumerate(chips):
                small_copy(j, s, (*chip, c)).start()
        for u in range(nu):
            stage[u].wait()
            for j, chip in enumerate(chips):
                ici(u, j, s, (*chip, c)).start()
            local[u].start()

    def finish(ins, outs, scr):
        x, y, c, chips, ici, fwd, small_copy, stage, local = unpack(ins, outs, scr)
        s = 2 * x + y
        for u in range(nu):
            for j, chip in enumerate(chips):
                sj = 2 * chip[0] + chip[1]
                ici(u, j, sj, (x, y, c)).wait_recv()
                fwd(u, j, sj, c).start()
        for u in range(nu):
            for j, chip in enumerate(chips):
                sj = 2 * chip[0] + chip[1]
                fwd(u, j, sj, 1 - c).wait_recv()
        for u in range(nu):
            for j, chip in enumerate(chips):
                ici(u, j, s, (*chip, c)).wait_send()
                fwd(u, j, s, c).wait_send()
        if ns:
            for j, chip in enumerate(chips):
                small_copy(j, 2 * chip[0] + chip[1], (x, y, c)).wait_recv()
                small_copy(j, s, (*chip, c)).wait_send()
        for cp in local:
            cp.wait()

    dma = pltpu.SemaphoreType.DMA
    return _Exchange(
        ins=list(srcs) + ([small] if ns else []),
        in_specs=[ANY] * nu + [BS(memory_space=pltpu.VMEM)] * ns,
        out_shapes=[_sds(sh, BF16) for sh in dst_shapes]
        + ([_sds((NSH,) + small.shape, F32)] if ns else []),
        scratch=[pltpu.VMEM(a.shape, BF16) for a in srcs]
        + [dma((3 * nu,)), dma((3 * nu,)), dma((3 * nu,)), dma((3 * nu,)),
           dma((nu + 1,)), dma((3,)), dma((3,)), dma((nu,))],
        start=start, finish=finish)


def _pair_reduce(gs, name):
    nu = len(gs)
    ns = [g.shape[0] * NSH for g in gs]
    base = [sum(ns[:u]) for u in range(nu)]

    def body(*refs):
        g_refs, own_refs, sb_refs = refs[:nu], refs[nu:2 * nu], refs[2 * nu:3 * nu]
        bufs = refs[3 * nu:8 * nu]
        send, recv, lsem, osem = refs[8 * nu:]
        x, y, c, _ = _place()
        s = 2 * x + y

        def unit(u):
            sendb, recvb, stage, outf, outb = bufs[5 * u:5 * u + 5]

            def load(k, half):
                return pltpu.make_async_copy(g_refs[u].at[k // NSH, k % NSH, half], stage.at[k % 3],
                                             lsem.at[3 * u + k % 3])

            def push(k):
                return pltpu.make_async_remote_copy(
                    src_ref=sendb.at[k], dst_ref=recvb.at[k], send_sem=send.at[base[u] + k],
                    recv_sem=recv.at[base[u] + k], device_id=(x, y, 1 - c), device_id_type=MESH)

            def store(k):
                return pltpu.make_async_copy(outb.at[k % 2], sb_refs[u].at[k // NSH, k % NSH],
                                             osem.at[3 * u + k % 2])

            return sendb, recvb, stage, outf, outb, load, push, store

        for u in range(nu):
            sendb, recvb, stage, outf, outb, load, push, store = unit(u)
            for k in range(min(2, ns[u])):
                load(k, 1 - c).start()
            for k in range(ns[u]):
                if k + 2 < ns[u]:
                    load(k + 2, 1 - c).start()
                load(k, 1 - c).wait()
                sendb[k] = stage[k % 3].astype(BF16)
                push(k).start()
        for u in range(nu):
            sendb, recvb, stage, outf, outb, load, push, store = unit(u)
            n = ns[u]
            for k in range(min(2, n)):
                load(k, c).start()
            for k in range(n):
                if k + 2 < n:
                    load(k + 2, c).start()
                load(k, c).wait()
                push(k).wait_recv()
                total = stage[k % 3] + recvb[k].astype(F32)
                if k >= 2:
                    store(k - 2).wait()
                outb[k % 2] = total.astype(BF16)
                store(k).start()

                @pl.when(s == k % NSH)
                def _():
                    outf[...] = total
                    keep = pltpu.make_async_copy(outf, own_refs[u].at[k // NSH], osem.at[3 * u + 2])
                    keep.start()
                    keep.wait()

            for k in range(max(n - 2, 0), n):
                store(k).wait()
        for u in range(nu):
            push = unit(u)[6]
            for k in range(ns[u]):
                push(k).wait_send()

    dma = pltpu.SemaphoreType.DMA
    scratch = []
    for g, n in zip(gs, ns):
        rh, cc = g.shape[3], g.shape[4]
        scratch += [pltpu.VMEM((n, rh, cc), BF16), pltpu.VMEM((n, rh, cc), BF16),
                    pltpu.VMEM((3, rh, cc), F32), pltpu.VMEM((rh, cc), F32), pltpu.VMEM((2, rh, cc), BF16)]
    outs = pl.pallas_call(
        body, in_specs=[ANY] * nu, out_specs=[ANY] * (2 * nu),
        out_shape=[_sds((g.shape[0], g.shape[3], g.shape[4]), F32) for g in gs]
        + [_sds((g.shape[0], NSH, g.shape[3], g.shape[4]), BF16) for g in gs],
        scratch_shapes=scratch + [dma((sum(ns),)), dma((sum(ns),)), dma((3 * nu,)), dma((3 * nu,))],
        compiler_params=pltpu.CompilerParams(has_side_effects=True, vmem_limit_bytes=56 << 20),
        name=name,
    )(*gs)
    return list(outs[:nu]), list(outs[nu:])


def _chip_exchange(sums_bf16):
    nu = len(sums_bf16)

    def pushes(ins, outs, scr):
        x, y, c, chips = _place()
        send, recv = scr
        return [pltpu.make_async_remote_copy(
            src_ref=ins[u].at[:, 2 * chip[0] + chip[1]], dst_ref=outs[u].at[j],
            send_sem=send.at[3 * u + j], recv_sem=recv.at[3 * u + j],
            device_id=(*chip, c), device_id_type=MESH)
            for u in range(nu) for j, chip in enumerate(chips)]

    def start(ins, outs, scr):
        for cp in pushes(ins, outs, scr):
            cp.start()

    def finish(ins, outs, scr):
        for cp in pushes(ins, outs, scr):
            cp.wait()

    dma = pltpu.SemaphoreType.DMA
    shapes = [(3, a.shape[0], a.shape[2], a.shape[3]) for a in sums_bf16]
    return _Exchange(ins=list(sums_bf16), in_specs=[ANY] * nu,
                     out_shapes=[_sds(sh, BF16) for sh in shapes],
                     scratch=[dma((3 * nu,)), dma((3 * nu,))], start=start, finish=finish)


def _final_exchange(halves, out_shapes, targets):
    nu = len(halves)
    no = len(out_shapes)
    ncp = sum(len(t) for t in targets)

    def body(*refs):
        hv = refs[:nu]
        out = refs[nu:nu + no]
        sbuf = refs[nu + no:2 * nu + no]
        rbuf = refs[2 * nu + no:3 * nu + no]
        send, recv, lsem, osem, csem = refs[3 * nu + no:]
        x, y, c, _ = _place()
        stage = [pltpu.make_async_copy(hv[u], sbuf[u], lsem.at[u]) for u in range(nu)]
        push = [pltpu.make_async_remote_copy(
            src_ref=sbuf[u], dst_ref=rbuf[u], send_sem=send.at[u], recv_sem=recv.at[u],
            device_id=(x, y, 1 - c), device_id_type=MESH) for u in range(nu)]
        mine, theirs = [], []
        k = 0
        for u in range(nu):
            rh = hv[u].shape[1]
            for (p, oi, li) in targets[u]:
                mine.append((u, pltpu.make_async_copy(
                    sbuf[u].at[p], out[oi].at[li, pl.ds(c * rh, rh), :], csem.at[k])))
                theirs.append((u, pltpu.make_async_copy(
                    rbuf[u].at[p], out[oi].at[li, pl.ds((1 - c) * rh, rh), :], osem.at[k])))
                k += 1
        for cp in stage:
            cp.start()
        for u in range(nu):
            stage[u].wait()
            push[u].start()
            for (v, cp) in mine:
                if v == u:
                    cp.start()
        for u in range(nu):
            push[u].wait_recv()
            for (v, cp) in theirs:
                if v == u:
                    cp.start()
        for (_, cp) in theirs + mine:
            cp.wait()
        for u in range(nu):
            push[u].wait_send()

    dma = pltpu.SemaphoreType.DMA
    bufs = [pltpu.VMEM(h.shape, F32) for h in halves]
    return pl.pallas_call(
        body, in_specs=[ANY] * nu, out_specs=[ANY] * no,
        out_shape=[_sds(sh, F32) for sh in out_shapes],
        scratch_shapes=bufs + bufs + [dma((nu,)), dma((nu,)), dma((nu,)), dma((ncp,)), dma((ncp,))],
        compiler_params=pltpu.CompilerParams(has_side_effects=True, vmem_limit_bytes=56 << 20),
        name="final_exchange",
    )(*halves)


def _small_allreduce(pack):
    rows = pack.shape[0]

    def body(p_ref, o_ref, buf, send, recv):
        x, y, c, _ = _place()
        me = 4 * x + 2 * y + c
        buf[me] = p_ref[...]
        k = 0
        copies = []
        for dx in range(2):
            for dy in range(2):
                for dc in range(2):
                    if dx == 0 and dy == 0 and dc == 0:
                        continue
                    to = (jnp.where(dx, 1 - x, x), jnp.where(dy, 1 - y, y), jnp.where(dc, 1 - c, c))
                    src_slot = 4 * to[0] + 2 * to[1] + to[2]
                    copies.append((pltpu.make_async_remote_copy(
                        src_ref=p_ref, dst_ref=buf.at[me], send_sem=send.at[k], recv_sem=recv.at[k],
                        device_id=to, device_id_type=MESH), src_slot, k))
                    k += 1
        for cp, _, _ in copies:
            cp.start()
        for cp, src_slot, k in copies:
            pltpu.make_async_remote_copy(
                src_ref=p_ref, dst_ref=buf.at[src_slot], send_sem=send.at[k], recv_sem=recv.at[k],
                device_id=(x, y, c), device_id_type=MESH).wait()
        acc = buf[0]
        for d in range(1, 8):
            acc = acc + buf[d]
        o_ref[...] = acc

    dma = pltpu.SemaphoreType.DMA
    vm = BS(memory_space=pltpu.VMEM)
    return pl.pallas_call(
        body, in_specs=[vm], out_specs=vm, out_shape=_sds((rows, D), F32),
        scratch_shapes=[pltpu.VMEM((8, rows, D), F32), dma((7,)), dma((7,))],
        compiler_params=pltpu.CompilerParams(has_side_effects=True, vmem_limit_bytes=32 << 20),
        name="small_allreduce",
    )(pack)


def _in_proj(h, w, bias, name, transposed=False, carry=None):
    n = w.shape[0] if transposed else w.shape[1]
    tn = 1280 if n == NA else 1792
    ep = None
    extras, especs = (), ()
    if bias is not None:
        def ep(acc, ex, o_ref, i):
            o_ref[...] = acc + ex[0][...]
        extras = (bias,)
        especs = (BS((1, tn), lambda i, j, k: (0, j)),)
    b_spec = BS((tn, D), lambda i, j, k: (j, 0)) if transposed else BS((D, tn), lambda i, j, k: (0, j))
    return _mm("nt" if transposed else "nn", h, w, grid=(T // TR, n // tn, 1),
               a_spec=BS((TR, D), lambda i, j, k: (i, 0)), b_spec=b_spec,
               out_shape=_sds((T, n), F32), out_spec=BS((TR, tn), lambda i, j, k: (i, j)),
               acc_shape=(TR, tn), extras=extras, extra_specs=especs, epilogue=ep, name=name, carry=carry)


def _res_rms_epilogue(acc, ex, outs, i):
    y = acc + ex[0][...]
    outs[0][...] = y
    r = lax.rsqrt(jnp.mean(y * y, axis=-1, keepdims=True) + EPS)
    outs[1][...] = (y * r * ex[1][...]).astype(BF16)


def _res_loss_epilogue(acc, ex, outs, i):
    e = acc + ex[0][...] - ex[1][...]
    outs[1][...] = e * (1.0 / D)

    @pl.when(i == 0)
    def _():
        outs[0][...] = jnp.zeros_like(outs[0])

    outs[0][...] += 0.5 * jnp.sum(jnp.mean(e * e, axis=-1, keepdims=True), axis=0, keepdims=True)


def _rms_bwd_epilogue(dh, ex, outs, i):
    xv = ex[0][...]
    r = lax.rsqrt(jnp.mean(xv * xv, axis=-1, keepdims=True) + EPS)
    xh = xv * r
    gy = dh * ex[1][...]
    outs[0][...] = r * (gy - xh * jnp.mean(gy * xh, axis=-1, keepdims=True)) + ex[2][...]

    @pl.when(i == 0)
    def _():
        outs[1][...] = jnp.zeros_like(outs[1])

    outs[1][...] += jnp.sum(dh * xh, axis=0, keepdims=True)


def _local_step(x, mem, target, w, p, carries=None, bwd_carry_fn=None, late_carry_fn=None):
    row = lambda i, j, k: (i, 0)
    whole = lambda i, j, k: (0, 0)
    w = {k: (list(v) if isinstance(v, list) else v) for k, v in w.items()}
    carries = carries or {}

    def carry_of(name):
        return carries[name][0] if name in carries else None

    def delivered(name, outs):
        if name in carries:
            carries[name][1](w, outs)

    saved = []
    bias = _bias_expand(p["rel_u"])
    vec = BS((1, D), whole)
    h = _rms_fwd(x, p["norm1_g"][0:1], "rms1_0", carry=carry_of("rms1_0"))
    if carry_of("rms1_0") is not None:
        h, carried = h
        delivered("rms1_0", carried)
    for l in range(2):
        type_a = l == 0
        memn = _rms_fwd(mem, p["mem_norm_g"][l:l + 1], f"rmsmem_{l}")
        if type_a:
            z = _in_proj(h, w["a"], None, "inproj_a", carry=carry_of("inproj_a"))
            if carry_of("inproj_a") is not None:
                z, carried = z
                delivered("inproj_a", carried)
            cat, carried = _attn_fwd(z, bias, p["a_q_g2"], p["a_k_g2"], carry_of("attn_fwd"))
            delivered("attn_fwd", carried)
            qcol = NA // MEMW - 1
        else:
            z = _in_proj(h, w["b"], w["b_b_in"], "inproj_b", transposed=True)
            cat, conv_y = _conv_fwd(z, w["conv_w"], w["conv_b"], w["ln_g"], w["ln_b"])
            qcol = NBW // MEMW - 1
        kv = _mm("nn", memn, w["kv"][l], grid=(1, 1, 1),
                 a_spec=BS((NB * MEMT, D), whole), b_spec=BS((D, 2 * MEMW), whole),
                 out_shape=_sds((NB * MEMT, 2 * MEMW), F32), out_spec=BS((NB * MEMT, 2 * MEMW), whole),
                 acc_shape=(8, 128), name=f"memkv_{l}")
        cat = _memattn_fwd(z, kv, cat, p["mq_g4"][l:l + 1], p["mk_g4"][l:l + 1], qcol, f"memattn_fwd_{l}")
        x1, h2 = _mm("nn", cat, w["wo"][l], grid=(T // TR, 1, 1), a_spec=BS((TR, D), row),
                     b_spec=BS((D, D), whole),
                     out_shape=[_sds((T, D), F32), _sds((T, D), BF16)],
                     out_spec=[BS((TR, D), row), BS((TR, D), row)], acc_shape=(8, 128),
                     extras=(x, p["norm2_g"][l:l + 1]), extra_specs=(BS((TR, D), row), vec),
                     epilogue=_res_rms_epilogue, name=f"outproj_{l}")
        (g, u, act), carried = _ffn_up(h2, w["gu"][l], l, carry_of(f"ffn_up_{l}"))
        delivered(f"ffn_up_{l}", carried)
        last = l == 1
        res = _mm("nn", act, w["wd"][l], grid=(T // TR, 1, 1),
                  a_spec=BS((TR, FF), row), b_spec=BS((FF, D), whole),
                  out_shape=[_sds((1, 1), F32), _sds((T, D), F32)] if last else
                  [_sds((T, D), F32), _sds((T, D), BF16)],
                  out_spec=[BS((1, 1), whole), BS((TR, D), row)] if last else
                  [BS((TR, D), row), BS((TR, D), row)],
                  acc_shape=(8, 128), extras=(x1, target if last else p["norm1_g"][1:2]),
                  extra_specs=(BS((TR, D), row), BS((TR, D), row) if last else vec),
                  epilogue=_res_loss_epilogue if last else _res_rms_epilogue, sequential=last,
                  name=f"ffn_down_{l}", carry=carry_of(f"ffn_down_{l}"))
        if carry_of(f"ffn_down_{l}") is not None:
            res, carried = res
            delivered(f"ffn_down_{l}", carried)
        saved.append(dict(x=x, h=h, memn=memn, kv=kv, z=z, cat=cat, x1=x1, h2=h2, g=g, u=u, act=act,
                          qcol=qcol))
        if last:
            loss, dx = res
        else:
            x, h = res

    big = dict(a=None, b=None, kv=[None, None], wo=[None, None], gu=[None, None], wd=[None, None])
    small = {}
    bwd_carried, late_carried = (), ()
    tk = T // 2
    nkt = T // tk
    for l in (1, 0):
        sv = saved[l]
        dgu = _ffn_down_bwd(dx, w["wd"][l], sv["g"], sv["u"], l)
        big["wd"][l] = _mm("tn", sv["act"], dx, grid=(FF // FT, 1, 2 * nkt),
                           a_spec=BS((tk // 2, FT), lambda i, j, k: (k, i)),
                           b_spec=BS((tk // 2, D), lambda i, j, k: (k, 0)),
                           out_shape=_sds((FF, D), F32), out_spec=BS((FT, D), lambda i, j, k: (i, 0)),
                           acc_shape=(FT, D), name=f"dw_down_{l}")
        dx1, small[f"norm2_g{l}"] = _mm(
            "nn", dgu, w["gu"][l], grid=(T // TR, 1, 2),
            a_spec=BS((None, TR, FF), lambda i, j, k: (k, i, 0)),
            b_spec=BS((None, FF, D), lambda i, j, k: (k, 0, 0)),
            out_shape=[_sds((T, D), F32), _sds((1, D), F32)], out_spec=[BS((TR, D), row), vec],
            acc_shape=(TR, D), extras=(sv["x1"], p["norm2_g"][l:l + 1], dx),
            extra_specs=(BS((TR, D), row), vec, BS((TR, D), row)),
            epilogue=_rms_bwd_epilogue, sequential=True, name=f"dh2_{l}")
        big["gu"][l] = _mm("tn", dgu, sv["h2"], grid=(2 * FF // FT, 1, nkt),
                           a_spec=BS((None, tk, FT), lambda i, j, k: (i // 2, k, i % 2)),
                           b_spec=BS((tk, D), lambda i, j, k: (k, 0)),
                           out_shape=_sds((2, FF, D), F32),
                           out_spec=BS((None, FT, D), lambda i, j, k: (i // 2, i % 2, 0)),
                           acc_shape=(FT, D), vmem_mb=58, name=f"dw_gu_{l}")
        dcat = _mm("nt", dx1, w["wo"][l], grid=(T // TR, 1, 1), a_spec=BS((TR, D), row),
                   b_spec=BS((D, D), whole),
                   out_shape=_sds((T, D), F32), out_spec=BS((TR, D), row), acc_shape=(8, 128),
                   name=f"dcat_{l}")
        big["wo"][l] = _mm("tn", sv["cat"], dx1, grid=(1, 1, nkt),
                           a_spec=BS((tk, D), lambda i, j, k: (k, 0)), b_spec=BS((tk, D), lambda i, j, k: (k, 0)),
                           out_shape=_sds((D, D), F32), out_spec=BS((D, D), whole),
                           acc_shape=(D, D), name=f"dw_out_{l}")
        dqm, dkv, small[f"mq_g{l}"], small[f"mk_g{l}"] = _memattn_bwd(
            sv["z"], sv["kv"], dcat, p["mq_g4"][l:l + 1], p["mk_g4"][l:l + 1], sv["qcol"], f"memattn_bwd_{l}")
        big["kv"][l] = _mm("tn", sv["memn"], dkv, grid=(1, 1, 1),
                           a_spec=BS((NB * MEMT, D), whole), b_spec=BS((NB * MEMT, 2 * MEMW), whole),
                           out_shape=_sds((D, 2 * MEMW), F32), out_spec=BS((D, 2 * MEMW), whole),
                           acc_shape=(8, 128), name=f"dw_kv_{l}")
        dmemn = _mm("nt", dkv, w["kv"][l], grid=(1, 1, 1),
                    a_spec=BS((NB * MEMT, 2 * MEMW), whole), b_spec=BS((D, 2 * MEMW), whole),
                    out_shape=_sds((NB * MEMT, D), F32), out_spec=BS((NB * MEMT, D), whole),
                    acc_shape=(8, 128), name=f"dmemn_{l}")
        _, small[f"mem_norm_g{l}"] = _rms_bwd(dmemn, mem, p["mem_norm_g"][l:l + 1], None, f"rmsmem_bwd_{l}")
        if l == 0:
            carry = bwd_carry_fn(big) if bwd_carry_fn is not None else None
            (dq, dk, dv, dbias, small["a_q_g"], small["a_k_g"]), bwd_carried = _attn_bwd(
                sv["z"], dcat, bias, p["a_q_g2"], p["a_k_g2"], carry)
            small["rel_u"] = _bias_reduce(dbias)
            dz = jnp.concatenate([dq, dk, dv, dqm], axis=1)
            w_in, key, n, tn = w["a"], "a", NA, 1280
        else:
            du, small["conv_w"], small["conv_b"], small["ln_g"], small["ln_b"], dbin_u = _conv_bwd(
                sv["z"], conv_y, dcat, w["conv_w"], w["ln_g"], w["ln_b"])
            dz = jnp.concatenate([du, dqm], axis=1)
            small["b_in_u"] = dbin_u
            w_in, key, n, tn = w["b"], "b", NBW, 896
        norm_bwd = dict(out_shape=[_sds((T, D), F32), _sds((1, D), F32)], out_spec=[BS((TR, D), row), vec],
                        acc_shape=(8, 128), extras=(sv["x"], p["norm1_g"][l:l + 1], dx1),
                        extra_specs=(BS((TR, D), row), vec, BS((TR, D), row)),
                        epilogue=_rms_bwd_epilogue, sequential=True, name=f"dh_{l}")
        if l == 0:
            big[key] = _mm("tn", sv["h"], dz, grid=(1, n // tn, nkt),
                           a_spec=BS((tk, D), lambda i, j, k: (k, 0)), b_spec=BS((tk, tn), lambda i, j, k: (k, j)),
                           out_shape=_sds((D, n), F32), out_spec=BS((D, tn), lambda i, j, k: (0, j)),
                           acc_shape=(D, tn), name=f"dw_in_{l}")
            carry = late_carry_fn(big) if late_carry_fn is not None else None
            res = _mm("nt", dz, w_in, grid=(T // TR, 1, 1), a_spec=BS((TR, n), row),
                      b_spec=BS((D, n), whole), carry=carry, **norm_bwd)
            if carry is not None:
                res, late_carried = res
            dx, small[f"norm1_g{l}"] = res
        else:
            dx, small[f"norm1_g{l}"] = _mm("nn", dz, w_in, grid=(T // TR, 1, 1),
                                           a_spec=BS((TR, n), row), b_spec=BS((n, D), whole), **norm_bwd)
            big[key] = _mm("tn", dz, sv["h"], grid=(n // tn, 1, nkt),
                           a_spec=BS((tk, tn), lambda i, j, k: (k, i)), b_spec=BS((tk, D), lambda i, j, k: (k, 0)),
                           out_shape=_sds((n, D), F32), out_spec=BS((tn, D), lambda i, j, k: (i, 0)),
                           acc_shape=(tn, D), name=f"dw_in_{l}")
        if l == 1:
            small["b_in_qm"] = _colsum(dqm, "colsum_dqm")
    return loss, dx, big, small, bwd_carried, late_carried


def _colsum(a, name):
    rows, cols = a.shape

    def body(a_ref, o_ref):
        @pl.when(pl.program_id(0) == 0)
        def _():
            o_ref[...] = jnp.zeros_like(o_ref)

        o_ref[...] += jnp.sum(a_ref[...].astype(F32), axis=0, keepdims=True)

    return pl.pallas_call(
        body, grid=(rows // TR,), in_specs=[BS((TR, cols), lambda i: (i, 0))],
        out_specs=BS((1, cols), lambda i: (0, 0)), out_shape=_sds((1, cols), F32),
        compiler_params=_cp(("arbitrary",)), name=name,
    )(a)


_PACK_ROWS = 64


def _pack_small(sm):
    plan = [("norm1_g0", 0, 1, 0, D), ("norm1_g1", 1, 1, 0, D), ("mem_norm_g0", 2, 1, 0, D),
            ("mem_norm_g1", 3, 1, 0, D), ("norm2_g0", 4, 1, 0, D), ("norm2_g1", 5, 1, 0, D),
            ("a_q_g", 6, 1, 0, AW), ("a_k_g", 7, 1, 0, AW), ("mq_g0", 8, 1, 0, MEMW),
            ("mq_g1", 9, 1, 0, MEMW), ("mk_g0", 10, 1, 0, MEMW), ("mk_g1", 11, 1, 0, MEMW),
            ("conv_b", 12, 1, 0, TOK), ("ln_g", 13, 1, 0, TOK), ("ln_b", 14, 1, 0, TOK),
            ("b_in_u", 15, 1, 0, D), ("b_in_u", 16, 1, D, 2 * TOK - D), ("b_in_qm", 17, 1, 0, MEMW),
            ("conv_w", 18, CONVW, 0, TOK), ("rel_u", 49, 12, 0, D)]
    arrs = [sm[name].reshape(12, D) if name == "rel_u" else sm[name] for name, *_ in plan]

    def body(*refs):
        o_ref = refs[-1]
        o_ref[...] = jnp.zeros_like(o_ref)
        for ref, (_, r0, nr, c0, nc) in zip(refs, plan):
            o_ref[r0:r0 + nr, 0:nc] = ref[0:nr, c0:c0 + nc]

    return pl.pallas_call(
        body, grid=(1,), in_specs=[BS(a.shape, lambda i: (0, 0)) for a in arrs],
        out_specs=BS((_PACK_ROWS, D), lambda i: (0, 0)), out_shape=_sds((_PACK_ROWS, D), F32),
        compiler_params=_cp(("arbitrary",)), name="pack_small",
    )(*arrs)


def _rel_table_to_u(rel_bias):
    flat = jnp.concatenate([jnp.broadcast_to(rel_bias[:, 191:192], (12, 447)), rel_bias[:, ::-1]], axis=1)
    return jnp.pad(flat, ((0, 0), (192, 1024 - 192 - 639))).reshape(12, 1, 1024)


def _u_to_rel_table(du):
    flat = du[:, 192:192 + 639]
    g = flat[:, 447:][:, ::-1]
    return g, flat[:, :447]


def kernel(x, mem, norm1_g, mem_norm_g, a_w_in, a_q_g, a_k_g, a_rel_bias, b_w_in, b_b_in, b_conv_w, b_conv_b, b_ln_g, b_ln_b, mq_g, mk_g, w_mem_kv, w_out, norm2_g, w_gate, w_up, w_down, loss_target, m_norm1_g, m_mem_norm_g, m_a_w_in, m_a_q_g, m_a_k_g, m_a_rel_bias, m_b_w_in, m_b_b_in, m_b_conv_w, m_b_conv_b, m_b_ln_g, m_b_ln_b, m_mq_g, m_mk_g, m_w_mem_kv, m_w_out, m_norm2_g, m_w_gate, m_w_up, m_w_down, v_norm1_g, v_mem_norm_g, v_a_w_in, v_a_q_g, v_a_k_g, v_a_rel_bias, v_b_w_in, v_b_b_in, v_b_conv_w, v_b_conv_b, v_b_ln_g, v_b_ln_b, v_mq_g, v_mk_g, v_w_mem_kv, v_w_out, v_norm2_g, v_w_gate, v_w_up, v_w_down):
    sx = 2 * lax.axis_index("x") + lax.axis_index("y")

    n_in = (NA // NSH, NBW // NSH)
    tr = lambda a: jnp.swapaxes(a, -1, -2)
    small_src = jnp.concatenate([
        jnp.pad(b_b_in, ((0, 0), (0, 512 - 448))),
        jnp.pad(b_conv_w[0], ((0, 0), (0, 512 - 192))),
        jnp.pad(jnp.concatenate([b_conv_b, b_ln_g, b_ln_b], 0), ((0, 0), (0, 512 - 192))),
        jnp.zeros((5, 512), F32)], 0)

    def gather_groups(groups, small=None):
        def src_of(l, name):
            if name == "in":
                return [_cast_bf16([tr(b_w_in[0])], "cast_in_1").reshape(1, 2, n_in[1] // 2, D) if l else
                        _cast_bf16([a_w_in[0]], "cast_in_0").reshape(1, 2, D // 2, n_in[0])]
            if name == "gu":
                return [_cast_bf16([tr(w_gate[l])], f"cast_gate_{l}").reshape(1, 2, FS // 2, D),
                        _cast_bf16([tr(w_up[l])], f"cast_up_{l}").reshape(1, 2, FS // 2, D)]
            arr, shape = {"kv": (w_mem_kv, (1, 2, 128, 2 * MEMW)), "wo": (w_out, (1, 2, 128, D)),
                          "wd": (w_down, (1, 2, FS // 2, D))}[name]
            return [_cast_bf16([arr[l]], f"cast_{name}_{l}").reshape(shape)]

        def dst_of(l, name):
            if name == "in":
                return (1, NSH, 2, n_in[1] // 2, D) if l else (1, NSH, 2, D // 2, n_in[0])
            return {"kv": (1, NSH, 2, 128, 2 * MEMW), "wo": (1, NSH, 2, 128, D),
                    "gu": (1, 2, NSH, 2, FS // 2, D), "wd": (1, NSH, 2, FS // 2, D)}[name]

        items = [(l, name) for l, names in groups for name in names]
        srcs, views = [], []
        for k, (l, name) in enumerate(items):
            srcs += src_of(l, name)
            if name == "gu":
                views += [lambda d, k=k: d[k].at[:, 0], lambda d, k=k: d[k].at[:, 1]]
            else:
                views.append(lambda d, k=k: d[k])

        def done(w, outs):
            for k, (l, name) in enumerate(items):
                if name == "in" and l == 0:
                    w["a"] = outs[k].reshape(NSH, D, n_in[0]).transpose(1, 0, 2).reshape(D, NA)
                elif name == "in":
                    w["b"] = outs[k].reshape(NBW, D)
                else:
                    shape = {"kv": (D, 2 * MEMW), "wo": (D, D), "gu": (2, FF, D), "wd": (FF, D)}
                    w[name][l] = outs[k].reshape(shape[name])

        return _gather_exchange(srcs, [dst_of(l, name) for l, name in items], views, small), done

    w = dict(a=None, b=None, kv=[None, None], wo=[None, None], gu=[None, None], wd=[None, None])
    first, first_in_done = gather_groups([(0, ["in"])], small_src)

    def first_done(w, outs):
        first_in_done(w, outs)
        small_all = outs[1]
        conv_w_full = small_all[:, 1:1 + CONVW, :192].transpose(1, 0, 2).reshape(CONVW, TOK)
        vec3 = small_all[:, 32:35, :192].transpose(1, 0, 2).reshape(3, TOK)
        w.update(b_b_in=small_all[:, 0, :448].reshape(1, NBW), conv_w=jnp.pad(conv_w_full, ((0, 1), (0, 0))),
                 conv_b=vec3[0:1], ln_g=vec3[1:2], ln_b=vec3[2:3])

    carries = {"rms1_0": (first, first_done),
               "inproj_a": gather_groups([(0, ["kv", "wo", "wd"])]),
               "attn_fwd": gather_groups([(0, ["gu"]), (1, ["in", "kv", "wo"])]),
               "ffn_up_0": gather_groups([(1, ["gu"])]),
               "ffn_down_0": gather_groups([(1, ["wd"])])}
    p = dict(
        norm1_g=norm1_g, mem_norm_g=mem_norm_g, norm2_g=norm2_g,
        a_q_g2=jnp.tile(a_q_g, (1, AH)), a_k_g2=jnp.tile(a_k_g, (1, AH)),
        mq_g4=jnp.tile(mq_g, (1, 4)), mk_g4=jnp.tile(mk_g, (1, 4)),
        rel_u=_rel_table_to_u(a_rel_bias[0]))

    def pair_sums(items, big, name):
        units = []
        for l, tensor in items:
            if tensor == "in" and l == 0:
                g = big["a"].reshape(D, NSH, n_in[0]).transpose(1, 0, 2).reshape(1, NSH, 2, D // 2, n_in[0])
            elif tensor == "in":
                g = big["b"].reshape(1, NSH, 2, n_in[1] // 2, D)
            else:
                shape = {"kv": (1, NSH, 2, 128, 2 * MEMW), "wo": (1, NSH, 2, 128, D),
                         "gu": (2, NSH, 2, FS // 2, D), "wd": (1, NSH, 2, FS // 2, D)}
                g = big[tensor][l].reshape(shape[tensor])
            units.append(g)
        return _pair_reduce(units, name)

    early_groups = [[(1, "gu")], [(0, "gu")],
                    [(1, "in"), (1, "kv"), (1, "wo"), (1, "wd"), (0, "kv"), (0, "wo"), (0, "wd")]]
    early = [item for grp in early_groups for item in grp]
    late = [(0, "in")]
    own_early, own_late = [], []

    def bwd_carry_fn(big):
        sums_b = []
        for k, grp in enumerate(early_groups):
            own, sb = pair_sums(grp, big, f"pair_reduce_early_{k}")
            own_early.extend(own)
            sums_b.extend(sb)
        return _chip_exchange(sums_b)

    def late_carry_fn(big):
        own, sb = pair_sums(late, big, "pair_reduce_late")
        own_late.extend(own)
        return _chip_exchange(sb)

    loss, grad_x, big, small, parts_early, parts_late = _local_step(
        x.reshape(T, D), mem.reshape(NB * MEMT, D), loss_target.reshape(T, D), w, p,
        carries=carries, bwd_carry_fn=bwd_carry_fn, late_carry_fn=late_carry_fn)
    loss = lax.psum(loss[0, 0], ("x", "y", "c"))
    items = early + late
    halves = [_quad_sum(o, pt, f"quad_sum_{name}_{l}")
              for (l, name), o, pt in zip(items, own_early + own_late, list(parts_early) + list(parts_late))]
    out_shapes = [(1, D, NA // NSH), (1, NBW // NSH, D), (2, 2 * 128, 2 * MEMW), (2, 2 * 128, D),
                  (2, FS, D), (2, FS, D), (2, FS, D)]
    target_of = {"in": lambda l: [(0, l, 0)], "kv": lambda l: [(0, 2, l)], "wo": lambda l: [(0, 3, l)],
                 "gu": lambda l: [(0, 4, l), (1, 5, l)], "wd": lambda l: [(0, 6, l)]}
    targets = [target_of[name](l) for l, name in items]
    g_a, g_b, g_kv, g_wo, g_gate, g_up, g_wd = _final_exchange(halves, out_shapes, targets)

    tot = _small_allreduce(_pack_small(small))
    g_rel, clip_part = _u_to_rel_table(tot[49:61])
    g_rel = jnp.concatenate([g_rel[:, :191], g_rel[:, 191:] + _rowsum(clip_part)], axis=1)
    b_in_full = jnp.concatenate([tot[15:16], tot[16:17, :512], tot[17:18, :MEMW]], axis=1)
    g_small = dict(
        norm1_g=tot[0:2], mem_norm_g=tot[2:4], norm2_g=tot[4:6],
        a_q_g=tot[6:7, :HD], a_k_g=tot[7:8, :HD], a_rel_bias=g_rel[None],
        b_b_in=lax.dynamic_slice(b_in_full, (0, sx * 448), (1, 448)),
        b_conv_w=lax.dynamic_slice(tot[18:49, :TOK], (0, sx * 192), (CONVW, 192))[None],
        b_conv_b=lax.dynamic_slice(tot[12:13, :TOK], (0, sx * 192), (1, 192)),
        b_ln_g=lax.dynamic_slice(tot[13:14, :TOK], (0, sx * 192), (1, 192)),
        b_ln_b=lax.dynamic_slice(tot[14:15, :TOK], (0, sx * 192), (1, 192)),
        mq_g=tot[8:10, :HD], mk_g=tot[10:12, :HD])

    names = ["norm1_g", "mem_norm_g", "a_w_in", "a_q_g", "a_k_g", "a_rel_bias", "b_w_in", "b_b_in",
             "b_conv_w", "b_conv_b", "b_ln_g", "b_ln_b", "mq_g", "mk_g", "w_mem_kv", "w_out",
             "norm2_g", "w_gate", "w_up", "w_down"]
    weights = dict(zip(names, [norm1_g, mem_norm_g, a_w_in, a_q_g, a_k_g, a_rel_bias, b_w_in, b_b_in,
                               b_conv_w, b_conv_b, b_ln_g, b_ln_b, mq_g, mk_g, w_mem_kv, w_out,
                               norm2_g, w_gate, w_up, w_down]))
    ms = dict(zip(names, [m_norm1_g, m_mem_norm_g, m_a_w_in, m_a_q_g, m_a_k_g, m_a_rel_bias, m_b_w_in,
                          m_b_b_in, m_b_conv_w, m_b_conv_b, m_b_ln_g, m_b_ln_b, m_mq_g, m_mk_g,
                          m_w_mem_kv, m_w_out, m_norm2_g, m_w_gate, m_w_up, m_w_down]))
    vs = dict(zip(names, [v_norm1_g, v_mem_norm_g, v_a_w_in, v_a_q_g, v_a_k_g, v_a_rel_bias, v_b_w_in,
                          v_b_b_in, v_b_conv_w, v_b_conv_b, v_b_ln_g, v_b_ln_b, v_mq_g, v_mk_g,
                          v_w_mem_kv, v_w_out, v_norm2_g, v_w_gate, v_w_up, v_w_down]))
    grads = dict(g_small)
    grads.update(a_w_in=g_a, b_w_in=g_b, w_mem_kv=g_kv, w_out=g_wo, w_gate=g_gate, w_up=g_up, w_down=g_wd)
    big_names = ["a_w_in", "b_w_in", "w_mem_kv", "w_out", "w_gate", "w_up", "w_down"]
    small_names = [n for n in names if n not in big_names]
    delta, new_m, new_v = {}, {}, {}
    for n in big_names:
        if n in ("b_w_in", "w_gate", "w_up"):
            outs = _adamw_big(tr(weights[n]), grads[n], tr(ms[n]), tr(vs[n]), f"adamw_{n}")
            delta[n], new_m[n], new_v[n] = [tr(o) for o in outs]
            grads[n] = tr(grads[n])
        else:
            delta[n], new_m[n], new_v[n] = _adamw_big(weights[n], grads[n], ms[n], vs[n], f"adamw_{n}")
    as2d = lambda a: a.reshape(-1, a.shape[-1])
    d_s, m_s, v_s = _adamw_small([as2d(weights[n]) for n in small_names], [as2d(grads[n]) for n in small_names],
                                 [as2d(ms[n]) for n in small_names], [as2d(vs[n]) for n in small_names])
    for i, n in enumerate(small_names):
        delta[n] = d_s[i].reshape(weights[n].shape)
        new_m[n] = m_s[i].reshape(weights[n].shape)
        new_v[n] = v_s[i].reshape(weights[n].shape)

    return (loss, grad_x.reshape(NB, SEQ, D), *[grads[n] for n in names], *[delta[n] for n in names],
            *[new_m[n] for n in names], *[new_v[n] for n in names])


def _rowsum(a):
    def body(a_ref, o_ref):
        o_ref[...] = jnp.sum(a_ref[...], axis=1, keepdims=True)

    vm = BS(memory_space=pltpu.VMEM)
    return pl.pallas_call(body, in_specs=[vm], out_specs=vm, out_shape=_sds((a.shape[0], 1), F32),
                          compiler_params=_cp(), name="rowsum")(a)
```

```python
import functools

import jax
import jax.numpy as jnp
from jax import lax
from jax.experimental import pallas as pl
from jax.experimental.pallas import tpu as pltpu

F32 = jnp.float32
BF16 = jnp.bfloat16
BS = pl.BlockSpec
ANY = pl.BlockSpec(memory_space=pl.ANY)
MESH = pl.DeviceIdType.MESH

D = 1024
SEQ = 2048
NB = 2
T = NB * SEQ
MEMT = 256
HD = 64
TOK = 768
MEMW = 256
NA = 3 * TOK + MEMW
NBW = 2 * TOK + MEMW
FF = 2816
NSH = 4
FS = FF // NSH
FT = FF // 2
CONVW = 31
EPS = 1e-6
NEG = -1e30
SCALE = HD ** -0.5
QB = 256
KWIN = 768
KPAD = 512
TR = 512

ADAM_LR = 0.001
ADAM_B1 = 0.9
ADAM_B2 = 0.999
ADAM_EPS = 1e-08
ADAM_WD = 0.01
ADAM_STEP = 10

_DIMS = {
    "nn": (((1,), (0,)), ((), ())),
    "nt": (((1,), (1,)), ((), ())),
    "tn": (((0,), (0,)), ((), ())),
}


def _cp(sem=None, vmem_mb=48):
    return pltpu.CompilerParams(dimension_semantics=sem, vmem_limit_bytes=vmem_mb << 20)


def _sds(shape, dtype):
    return jax.ShapeDtypeStruct(tuple(shape), dtype)


def _mm(mode, a, b, *, grid, a_spec, b_spec, out_shape, out_spec, acc_shape, name,
        extras=(), extra_specs=(), epilogue=None, carry=None, vmem_mb=48, sequential=False):
    n_ex = len(extras)
    nk = grid[2]
    dims = _DIMS[mode]
    ni = len(carry.ins) if carry else 0
    no = len(carry.out_shapes) if carry else 0
    multi = isinstance(out_shape, (list, tuple))
    out_shapes = list(out_shape) if multi else [out_shape]
    out_specs = list(out_spec) if multi else [out_spec]
    n_o = len(out_shapes)

    def body(a_ref, b_ref, *rest):
        ex = rest[:n_ex]
        cin = rest[n_ex:n_ex + ni]
        o_refs = rest[n_ex + ni:n_ex + ni + n_o]
        o_ref = o_refs if multi else o_refs[0]
        cout = rest[n_ex + ni + n_o:n_ex + ni + n_o + no]
        acc = rest[n_ex + ni + n_o + no]
        cscr = rest[n_ex + ni + n_o + no + 1:]
        ids = [pl.program_id(d) for d in range(3)]
        k = ids[2]
        if carry:
            @pl.when((ids[0] == 0) & (ids[1] == 0) & (ids[2] == 0))
            def _():
                carry.start(cin, cout, cscr)

        prod = lax.dot_general(a_ref[...].astype(BF16), b_ref[...].astype(BF16), dims,
                               preferred_element_type=F32)

        def finish(val):
            if epilogue is None:
                o_ref[...] = val.astype(o_ref.dtype)
            else:
                epilogue(val, ex, o_ref, ids[0])

        if nk == 1:
            finish(prod)
        else:
            @pl.when(k == 0)
            def _():
                acc[...] = prod

            @pl.when((k > 0) & (k < nk - 1))
            def _():
                acc[...] += prod

            @pl.when(k == nk - 1)
            def _():
                finish(acc[...] + prod)

        if carry:
            @pl.when((ids[0] == grid[0] - 1) & (ids[1] == grid[1] - 1) & (ids[2] == grid[2] - 1))
            def _():
                carry.finish(cin, cout, cscr)

    acc_scratch = pltpu.VMEM(acc_shape if nk > 1 else (8, 128), F32)
    ordered = sequential or bool(carry)
    outs = pl.pallas_call(
        body, grid=grid,
        in_specs=[a_spec, b_spec, *extra_specs] + (carry.in_specs if carry else []),
        out_specs=out_specs + [ANY] * no, out_shape=out_shapes + (carry.out_shapes if carry else []),
        scratch_shapes=[acc_scratch] + (carry.scratch if carry else []),
        compiler_params=pltpu.CompilerParams(
            dimension_semantics=("arbitrary",) * 3 if ordered else ("parallel", "parallel", "arbitrary"),
            vmem_limit_bytes=vmem_mb << 20, has_side_effects=bool(carry)), name=name,
    )(a, b, *extras, *(carry.ins if carry else []))
    mine = list(outs[:n_o]) if multi else outs[0]
    return (mine, outs[n_o:]) if carry else mine


def _rms_fwd(x, g, name, carry=None):
    rows = x.shape[0]
    ni = len(carry.ins) if carry else 0
    no = len(carry.out_shapes) if carry else 0

    def body(x_ref, g_ref, *rest):
        cin, o_ref, cout, cscr = rest[:ni], rest[ni], rest[ni + 1:ni + 1 + no], rest[ni + 1 + no:]
        if carry:
            @pl.when(pl.program_id(0) == 0)
            def _():
                carry.start(cin, cout, cscr)

        xv = x_ref[...]
        r = lax.rsqrt(jnp.mean(xv * xv, axis=-1, keepdims=True) + EPS)
        o_ref[...] = (xv * r * g_ref[...]).astype(BF16)

        if carry:
            @pl.when(pl.program_id(0) == rows // TR - 1)
            def _():
                carry.finish(cin, cout, cscr)

    outs = pl.pallas_call(
        body, grid=(rows // TR,),
        in_specs=[BS((TR, D), lambda i: (i, 0)), BS((1, D), lambda i: (0, 0))]
        + (carry.in_specs if carry else []),
        out_specs=[BS((TR, D), lambda i: (i, 0))] + [ANY] * no,
        out_shape=[_sds((rows, D), BF16)] + (carry.out_shapes if carry else []),
        scratch_shapes=carry.scratch if carry else [],
        compiler_params=pltpu.CompilerParams(
            dimension_semantics=("arbitrary",), vmem_limit_bytes=48 << 20,
            has_side_effects=bool(carry)), name=name,
    )(x, g, *(carry.ins if carry else []))
    return (outs[0], outs[1:]) if carry else outs[0]


def _rms_bwd(dh, x, g, dres, name):
    rows = x.shape[0]
    has_res = dres is not None

    def body(*refs):
        if has_res:
            dh_ref, x_ref, g_ref, r_ref, dx_ref, dg_ref = refs
        else:
            dh_ref, x_ref, g_ref, dx_ref, dg_ref = refs
        xv = x_ref[...]
        dhv = dh_ref[...]
        r = lax.rsqrt(jnp.mean(xv * xv, axis=-1, keepdims=True) + EPS)
        xh = xv * r
        gy = dhv * g_ref[...]
        dx = r * (gy - xh * jnp.mean(gy * xh, axis=-1, keepdims=True))
        if has_res:
            dx = dx + r_ref[...]
        dx_ref[...] = dx

        @pl.when(pl.program_id(0) == 0)
        def _():
            dg_ref[...] = jnp.zeros_like(dg_ref)

        dg_ref[...] += jnp.sum(dhv * xh, axis=0, keepdims=True)

    row = BS((TR, D), lambda i: (i, 0))
    vec = BS((1, D), lambda i: (0, 0))
    ins = [dh, x, g] + ([dres] if has_res else [])
    return pl.pallas_call(
        body, grid=(rows // TR,),
        in_specs=[row, row, vec] + ([row] if has_res else []),
        out_specs=[row, vec], out_shape=[_sds((rows, D), F32), _sds((1, D), F32)],
        compiler_params=_cp(("arbitrary",)), name=name,
    )(*ins)


def _group_masks(width):
    lane = lax.broadcasted_iota(jnp.int32, (1, width), 1)
    return [(lane >= HD * h) & (lane < HD * (h + 1)) for h in range(width // HD)]


def _group_mean(v, masks):
    del masks
    low = lax.broadcasted_iota(jnp.int32, (1, 128), 1) < HD
    slabs = []
    for j in range(v.shape[-1] // 128):
        x = v[:, 128 * j:128 * (j + 1)]
        s0 = jnp.sum(jnp.where(low, x, 0.0), axis=-1, keepdims=True) * (1.0 / HD)
        s1 = jnp.sum(jnp.where(low, 0.0, x), axis=-1, keepdims=True) * (1.0 / HD)
        slabs.append(jnp.where(low, s0, s1))
    return slabs[0] if len(slabs) == 1 else jnp.concatenate(slabs, axis=-1)


def _head_norm(zv, g, masks):
    r = lax.rsqrt(_group_mean(zv * zv, masks) + EPS)
    return zv * r * g


def _head_norm_bwd(dy, zv, g, masks):
    r = lax.rsqrt(_group_mean(zv * zv, masks) + EPS)
    zh = zv * r
    gy = dy * g
    dz = r * (gy - zh * _group_mean(gy * zh, masks))
    return dz, jnp.sum(dy * zh, axis=0, keepdims=True)


def _fold_heads(v, width):
    vb = jnp.broadcast_to(v, (8, width))
    out = vb
    for h in range(1, width // HD):
        out = out + pltpu.roll(vb, width - HD * h, axis=1)
    return out[0:1]


def _bias_expand(u):
    def body(u_ref, o_ref):
        x = jnp.broadcast_to(u_ref[...], (QB, 1024))
        rolled = pltpu.roll(x, 1024 - (QB - 1), axis=1, stride=1, stride_axis=0)[:, :KWIN]
        row = lax.broadcasted_iota(jnp.int32, (QB, 1), 0)
        col = lax.broadcasted_iota(jnp.int32, (1, KWIN), 1)
        lo = (row // 64) * 64
        ok = (col >= lo) & (col < lo + 576)
        o_ref[...] = jnp.where(ok, rolled, NEG)

    return pl.pallas_call(
        body, grid=(12,), in_specs=[BS((None, 1, 1024), lambda h: (h, 0, 0))],
        out_specs=BS((None, QB, KWIN), lambda h: (h, 0, 0)), out_shape=_sds((12, QB, KWIN), F32),
        compiler_params=_cp(("arbitrary",)), name="bias_expand",
    )(u)


def _bias_reduce(ds):
    def body(d_ref, o_ref):
        ri = lax.broadcasted_iota(jnp.int32, (QB, QB), 0)
        ci = lax.broadcasted_iota(jnp.int32, (QB, QB), 1)
        flip = (ri + ci == QB - 1).astype(F32)
        drev = jnp.dot(flip, d_ref[...], precision=lax.Precision.HIGHEST, preferred_element_type=F32)
        x = jnp.concatenate([drev, jnp.zeros((QB, 1024 - KWIN), F32)], axis=1)
        rolled = pltpu.roll(x, 0, axis=1, stride=1, stride_axis=0)
        o_ref[...] = jnp.sum(rolled, axis=0, keepdims=True)

    return pl.pallas_call(
        body, grid=(12,), in_specs=[BS((None, QB, KWIN), lambda h: (h, 0, 0))],
        out_specs=BS((None, 1, 1024), lambda h: (h, 0, 0)), out_shape=_sds((12, 1, 1024), F32),
        compiler_params=_cp(("arbitrary",)), name="bias_reduce",
    )(ds)


AW = 256
AH = AW // HD
AG = TOK // AW
def _attn_softmax(qh, kw, bias, startadd):
    s = lax.dot_general(qh, kw, _DIMS["nt"], preferred_element_type=F32) + bias + startadd
    m = jnp.max(s, axis=-1, keepdims=True)
    p = jnp.exp(s - m)
    return p * (1.0 / jnp.sum(p, axis=-1, keepdims=True))


def _attn_prologue(z_ref, gq_ref, gk_ref, qn_s, kn_s, v_s, masks):
    kn_s[0:KPAD, :] = jnp.zeros((KPAD, AW), BF16)
    v_s[0:KPAD, :] = jnp.zeros((KPAD, AW), BF16)
    for r in range(0, SEQ, TR):
        qn_s[r:r + TR, :] = (_head_norm(z_ref[r:r + TR, 0:AW], gq_ref[...], masks) * SCALE).astype(BF16)
        kn_s[KPAD + r:KPAD + r + TR, :] = _head_norm(z_ref[r:r + TR, AW:2 * AW], gk_ref[...], masks).astype(BF16)
        v_s[KPAD + r:KPAD + r + TR, :] = z_ref[r:r + TR, 2 * AW:].astype(BF16)


def _attn_fwd(z, bias, gq2, gk2, carry=None):
    ni = len(carry.ins) if carry else 0
    no = len(carry.out_shapes) if carry else 0

    def body(z_ref, b_ref, gq_ref, gk_ref, *rest):
        cin, o_ref, cout = rest[:ni], rest[ni], rest[ni + 1:ni + 1 + no]
        qn_s, kn_s, v_s = rest[ni + 1 + no:ni + 4 + no]
        cscr = rest[ni + 4 + no:]
        if carry:
            @pl.when((pl.program_id(0) == 0) & (pl.program_id(1) == 0))
            def _():
                carry.start(cin, cout, cscr)

        masks = _group_masks(AW)
        _attn_prologue(z_ref, gq_ref, gk_ref, qn_s, kn_s, v_s, masks)
        col = lax.broadcasted_iota(jnp.int32, (1, KWIN), 1)

        def blk(i, carry):
            r0 = pl.multiple_of(i * QB, QB)
            qb = qn_s[pl.ds(r0, QB), :]
            kw = kn_s[pl.ds(r0, KWIN), :]
            vw = v_s[pl.ds(r0, KWIN), :]
            startadd = jnp.where(col + r0 < KPAD, NEG, 0.0)
            o = jnp.zeros((QB, AW), F32)
            for h in range(AH):
                qh = jnp.where(masks[h], qb, jnp.zeros_like(qb))
                vh = jnp.where(masks[h], vw, jnp.zeros_like(vw))
                p = _attn_softmax(qh, kw, b_ref[h], startadd).astype(BF16)
                o = o + jnp.dot(p, vh, preferred_element_type=F32)
            o_ref[pl.ds(r0, QB), :] = o.astype(BF16)
            return carry

        lax.fori_loop(0, SEQ // QB, blk, 0)

        if carry:
            @pl.when((pl.program_id(0) == NB - 1) & (pl.program_id(1) == AG - 1))
            def _():
                carry.finish(cin, cout, cscr)

    vec = BS((1, AW), lambda b, hp: (0, 0))
    outs = pl.pallas_call(
        body, grid=(NB, AG),
        in_specs=[BS((SEQ, 3 * AW), lambda b, hp: (b, hp)),
                  BS((AH, QB, KWIN), lambda b, hp: (hp, 0, 0)), vec, vec]
        + (carry.in_specs if carry else []),
        out_specs=[BS((SEQ, AW), lambda b, hp: (b, hp))] + [ANY] * no,
        out_shape=[_sds((T, D), BF16)] + (carry.out_shapes if carry else []),
        scratch_shapes=[pltpu.VMEM((SEQ, AW), BF16), pltpu.VMEM((SEQ + KPAD, AW), BF16),
                        pltpu.VMEM((SEQ + KPAD, AW), BF16)] + (carry.scratch if carry else []),
        compiler_params=pltpu.CompilerParams(
            dimension_semantics=("arbitrary", "arbitrary"), vmem_limit_bytes=48 << 20,
            has_side_effects=bool(carry)), name="attn_fwd",
    )(z, bias, gq2, gk2, *(carry.ins if carry else []))
    return outs[0], outs[1:]


def _attn_bwd(z, dcat, dz_into, bias, gq2, gk2, carry=None):
    ni = len(carry.ins) if carry else 0
    no = len(carry.out_shapes) if carry else 0

    def body(z_ref, do_ref, b_ref, gq_ref, gk_ref, into_ref, *rest):
        del into_ref
        cin = rest[:ni]
        dz_ref, db_ref, dgq_ref, dgk_ref = rest[ni:ni + 4]
        cout = rest[ni + 4:ni + 4 + no]
        qn_s, kn_s, v_s, dqn_s, dkn_s, dv_s = rest[ni + 4 + no:ni + 10 + no]
        cscr = rest[ni + 10 + no:]
        hp = pl.program_id(0)
        b = pl.program_id(1)
        if carry:
            @pl.when((hp == 0) & (b == 0))
            def _():
                carry.start(cin, cout, cscr)

        masks = _group_masks(AW)
        _attn_prologue(z_ref, gq_ref, gk_ref, qn_s, kn_s, v_s, masks)
        dkn_s[...] = jnp.zeros_like(dkn_s)
        dv_s[...] = jnp.zeros_like(dv_s)

        @pl.when(b == 0)
        def _():
            db_ref[...] = jnp.zeros_like(db_ref)

        @pl.when((b == 0) & (hp == 0))
        def _():
            dgq_ref[...] = jnp.zeros_like(dgq_ref)
            dgk_ref[...] = jnp.zeros_like(dgk_ref)

        col = lax.broadcasted_iota(jnp.int32, (1, KWIN), 1)

        def blk(i, carry):
            r0 = pl.multiple_of(i * QB, QB)
            qb = qn_s[pl.ds(r0, QB), :]
            kw = kn_s[pl.ds(r0, KWIN), :]
            vw = v_s[pl.ds(r0, KWIN), :]
            dob = do_ref[pl.ds(r0, QB), :].astype(BF16)
            startadd = jnp.where(col + r0 < KPAD, NEG, 0.0)
            dqn = jnp.zeros((QB, AW), F32)
            dkw = jnp.zeros((KWIN, AW), F32)
            dvw = jnp.zeros((KWIN, AW), F32)
            for h in range(AH):
                qh = jnp.where(masks[h], qb, jnp.zeros_like(qb))
                kh = jnp.where(masks[h], kw, jnp.zeros_like(kw))
                doh = jnp.where(masks[h], dob, jnp.zeros_like(dob))
                p = _attn_softmax(qh, kw, b_ref[h], startadd)
                dvw = dvw + lax.dot_general(p.astype(BF16), doh, _DIMS["tn"],
                                            preferred_element_type=F32)
                dp = lax.dot_general(doh, vw, _DIMS["nt"], preferred_element_type=F32)
                ds = p * (dp - jnp.sum(dp * p, axis=-1, keepdims=True))
                db_ref[h] += ds
                dsb = ds.astype(BF16)
                dqn = dqn + jnp.dot(dsb, kh, preferred_element_type=F32)
                dkw = dkw + lax.dot_general(dsb, qh, _DIMS["tn"], preferred_element_type=F32)
            dqn_s[pl.ds(r0, QB), :] = dqn * SCALE
            dkn_s[pl.ds(r0, KWIN), :] += dkw
            dv_s[pl.ds(r0, KWIN), :] += dvw
            return carry

        lax.fori_loop(0, SEQ // QB, blk, 0)

        dgq = jnp.zeros((1, AW), F32)
        dgk = jnp.zeros((1, AW), F32)
        for r in range(0, SEQ, TR):
            dq, dg = _head_norm_bwd(dqn_s[r:r + TR, :], z_ref[r:r + TR, 0:AW], gq_ref[...], masks)
            dz_ref[r:r + TR, 0:AW] = dq.astype(BF16)
            dgq = dgq + dg
            dk, dg = _head_norm_bwd(dkn_s[KPAD + r:KPAD + r + TR, :], z_ref[r:r + TR, AW:2 * AW],
                                    gk_ref[...], masks)
            dz_ref[r:r + TR, AW:2 * AW] = dk.astype(BF16)
            dgk = dgk + dg
            dz_ref[r:r + TR, 2 * AW:] = dv_s[KPAD + r:KPAD + r + TR, :].astype(BF16)
        dgq_ref[...] += _fold_heads(dgq, AW)
        dgk_ref[...] += _fold_heads(dgk, AW)

        if carry:
            @pl.when((hp == AG - 1) & (b == NB - 1))
            def _():
                carry.finish(cin, cout, cscr)

    vec = BS((1, AW), lambda hp, b: (0, 0))
    wide = BS((SEQ, 3 * AW), lambda hp, b: (b, hp))
    outs = pl.pallas_call(
        body, grid=(AG, NB),
        in_specs=[wide, BS((SEQ, AW), lambda hp, b: (b, hp)),
                  BS((AH, QB, KWIN), lambda hp, b: (hp, 0, 0)), vec, vec, ANY]
        + (carry.in_specs if carry else []),
        out_specs=[wide, BS((AH, QB, KWIN), lambda hp, b: (hp, 0, 0)), vec, vec] + [ANY] * no,
        out_shape=[_sds((T, NA), BF16), _sds((12, QB, KWIN), F32), _sds((1, AW), F32), _sds((1, AW), F32)]
        + (carry.out_shapes if carry else []),
        input_output_aliases={5: 0},
        scratch_shapes=[pltpu.VMEM((SEQ, AW), BF16), pltpu.VMEM((SEQ + KPAD, AW), BF16),
                        pltpu.VMEM((SEQ + KPAD, AW), BF16), pltpu.VMEM((SEQ, AW), F32),
                        pltpu.VMEM((SEQ + KPAD, AW), F32), pltpu.VMEM((SEQ + KPAD, AW), F32)]
        + (carry.scratch if carry else []),
        compiler_params=pltpu.CompilerParams(
            dimension_semantics=("arbitrary", "arbitrary"), vmem_limit_bytes=58 << 20,
            has_side_effects=bool(carry)), name="attn_bwd",
    )(z, dcat, bias, gq2, gk2, dz_into, *(carry.ins if carry else []))
    return outs[:4], outs[4:]


def _mem_softmax(qh, kn):
    s = lax.dot_general(qh, kn, _DIMS["nt"], preferred_element_type=F32)
    m = jnp.max(s, axis=-1, keepdims=True)
    p = jnp.exp(s - m)
    return p * (1.0 / jnp.sum(p, axis=-1, keepdims=True))


def _memattn_fwd(z, kv, cat, gq4, gk4, qcol, name):
    def body(q_ref, k_ref, v_ref, gq_ref, gk_ref, cat_ref, o_ref):
        del cat_ref
        masks = _group_masks(MEMW)
        qn = (_head_norm(q_ref[...], gq_ref[...], masks) * SCALE).astype(BF16)
        kn = _head_norm(k_ref[...], gk_ref[...], masks).astype(BF16)
        vv = v_ref[...].astype(BF16)
        o = jnp.zeros((TR, MEMW), F32)
        for h in range(4):
            qh = jnp.where(masks[h], qn, jnp.zeros_like(qn))
            vh = jnp.where(masks[h], vv, jnp.zeros_like(vv))
            p = _mem_softmax(qh, kn).astype(BF16)
            o = o + jnp.dot(p, vh, preferred_element_type=F32)
        o_ref[...] = o.astype(BF16)

    nt = SEQ // TR
    vec = BS((1, MEMW), lambda b, t: (0, 0))
    return pl.pallas_call(
        body, grid=(NB, nt),
        in_specs=[BS((TR, MEMW), lambda b, t: (b * nt + t, qcol)),
                  BS((MEMT, MEMW), lambda b, t: (b, 0)),
                  BS((MEMT, MEMW), lambda b, t: (b, 1)), vec, vec, ANY],
        out_specs=BS((TR, MEMW), lambda b, t: (b * nt + t, 3)),
        out_shape=_sds((T, D), BF16), input_output_aliases={5: 0},
        compiler_params=_cp(("arbitrary", "arbitrary")), name=name,
    )(z, kv, kv, gq4, gk4, cat)


def _memattn_bwd(z, kv, dcat, gq4, gk4, qcol, name, dz_cols=MEMW):
    nt = SEQ // TR

    def body(q_ref, k_ref, v_ref, do_ref, gq_ref, gk_ref,
             dq_ref, dkv_ref, dgq_ref, dgk_ref, dkn_s, dv_s):
        b = pl.program_id(0)
        t = pl.program_id(1)
        masks = _group_masks(MEMW)
        qz = q_ref[...]
        kz = k_ref[...]
        qn = (_head_norm(qz, gq_ref[...], masks) * SCALE).astype(BF16)
        kn = _head_norm(kz, gk_ref[...], masks).astype(BF16)
        vv = v_ref[...].astype(BF16)
        dob = do_ref[...].astype(BF16)

        @pl.when(t == 0)
        def _():
            dkn_s[...] = jnp.zeros_like(dkn_s)
            dv_s[...] = jnp.zeros_like(dv_s)

        @pl.when((t == 0) & (b == 0))
        def _():
            dgq_ref[...] = jnp.zeros_like(dgq_ref)
            dgk_ref[...] = jnp.zeros_like(dgk_ref)

        dqn = jnp.zeros((TR, MEMW), F32)
        dkn = jnp.zeros((MEMT, MEMW), F32)
        dvv = jnp.zeros((MEMT, MEMW), F32)
        for h in range(4):
            qh = jnp.where(masks[h], qn, jnp.zeros_like(qn))
            kh = jnp.where(masks[h], kn, jnp.zeros_like(kn))
            doh = jnp.where(masks[h], dob, jnp.zeros_like(dob))
            p = _mem_softmax(qh, kn)
            dvv = dvv + lax.dot_general(p.astype(BF16), doh, _DIMS["tn"], preferred_element_type=F32)
            dp = lax.dot_general(doh, vv, _DIMS["nt"], preferred_element_type=F32)
            ds = p * (dp - jnp.sum(dp * p, axis=-1, keepdims=True))
            dsb = ds.astype(BF16)
            dqn = dqn + jnp.dot(dsb, kh, preferred_element_type=F32)
            dkn = dkn + lax.dot_general(dsb, qh, _DIMS["tn"], preferred_element_type=F32)
        dkn_s[...] += dkn
        dv_s[...] += dvv
        dq, dgq = _head_norm_bwd(dqn * SCALE, qz, gq_ref[...], masks)
        dq_ref[...] = dq.astype(BF16)
        dgq_ref[...] += _fold_heads(dgq, MEMW)

        @pl.when(t == nt - 1)
        def _():
            dk, dgk = _head_norm_bwd(dkn_s[...], kz, gk_ref[...], masks)
            dkv_ref[:, 0:MEMW] = dk
            dkv_ref[:, MEMW:] = dv_s[...]
            dgk_ref[...] += _fold_heads(dgk, MEMW)

    vec = BS((1, MEMW), lambda b, t: (0, 0))
    return pl.pallas_call(
        body, grid=(NB, nt),
        in_specs=[BS((TR, MEMW), lambda b, t: (b * nt + t, qcol)),
                  BS((MEMT, MEMW), lambda b, t: (b, 0)),
                  BS((MEMT, MEMW), lambda b, t: (b, 1)),
                  BS((TR, MEMW), lambda b, t: (b * nt + t, 3)), vec, vec],
        out_specs=[BS((TR, MEMW), lambda b, t: (b * nt + t, dz_cols // MEMW - 1)),
                   BS((MEMT, 2 * MEMW), lambda b, t: (b, 0)), vec, vec],
        out_shape=[_sds((T, dz_cols), BF16), _sds((NB * MEMT, 2 * MEMW), F32),
                   _sds((1, MEMW), F32), _sds((1, MEMW), F32)],
        scratch_shapes=[pltpu.VMEM((MEMT, MEMW), F32), pltpu.VMEM((MEMT, MEMW), F32)],
        compiler_params=_cp(("arbitrary", "arbitrary")), name=name,
    )(z, kv, kv, dcat, gq4, gk4)


HALO = 32
NEXT = 64
RT = 64


def _glu(zz):
    return zz[:, :TOK] * jax.nn.sigmoid(zz[:, TOK:])


def _layer_norm_parts(y):
    mu = jnp.mean(y, axis=-1, keepdims=True)
    yc = y - mu
    rstd = lax.rsqrt(jnp.mean(yc * yc, axis=-1, keepdims=True) + EPS)
    return yc * rstd, rstd


def _shifted_copies(src, dst, rows):
    for b in range(1, 8):
        dst[b - 1, 0:rows, :] = src[b:b + rows, :]


def _tap(src, shifted, off, r0, rows):
    b = off % 8
    if b == 0:
        return src[r0 + off:r0 + off + rows, :]
    return shifted[b - 1, r0 + off - b:r0 + off - b + rows, :]


def _conv_rows(w_ref, hbuf, hs, r0, rows):
    y = jnp.zeros((rows, TOK), F32)
    for j in range(CONVW):
        y = y + w_ref[j:j + 1, :] * _tap(hbuf, hs, (HALO - CONVW + 1) + j, r0, rows)
    return y


def _conv_fwd(z, cw, cb, lg, lb):
    nt = SEQ // TR

    def body(zc_ref, zp_ref, w_ref, cb_ref, lg_ref, lb_ref, o_ref, y_ref, hbuf, hs):
        t = pl.program_id(1)
        hbuf[0:HALO, :] = jnp.where(t == 0, 0.0, _glu(zp_ref[...]))
        hbuf[HALO:, :] = _glu(zc_ref[...])
        _shifted_copies(hbuf, hs, HALO + TR - 8)
        for r0 in range(0, TR, RT):
            y = _conv_rows(w_ref, hbuf, hs, r0, RT) + cb_ref[...]
            y_ref[r0:r0 + RT, :] = y
            yh, _ = _layer_norm_parts(y)
            o = yh * lg_ref[...] + lb_ref[...]
            o_ref[r0:r0 + RT, :] = (o * jax.nn.sigmoid(o)).astype(BF16)

    vec = BS((1, TOK), lambda b, t: (0, 0))
    per = TR // HALO
    return pl.pallas_call(
        body, grid=(NB, nt),
        in_specs=[BS((TR, 2 * TOK), lambda b, t: (b * nt + t, 0)),
                  BS((HALO, 2 * TOK), lambda b, t: (jnp.maximum((b * nt + t) * per - 1, 0), 0)),
                  BS((32, TOK), lambda b, t: (0, 0)), vec, vec, vec],
        out_specs=[BS((TR, TOK), lambda b, t: (b * nt + t, 0)), BS((TR, TOK), lambda b, t: (b * nt + t, 0))],
        out_shape=[_sds((T, D), BF16), _sds((T, TOK), F32)],
        scratch_shapes=[pltpu.VMEM((HALO + TR, TOK), F32), pltpu.VMEM((7, HALO + TR, TOK), F32)],
        compiler_params=_cp(("arbitrary", "arbitrary")), name="conv_fwd",
    )(z, z, cw, cb, lg, lb)


def _conv_bwd(z, y, dcat, dz_into, cw, lg, lb):
    nt = SEQ // TR
    ext = TR + NEXT

    def body(zc_ref, zp_ref, yc_ref, yn_ref, dc_ref, dn_ref, w_ref, lg_ref, lb_ref, into_ref,
             du_ref, dw_ref, dcb_ref, dlg_ref, dlb_ref, dbin_ref, hbuf, dybuf, hs, dys):
        del into_ref
        b = pl.program_id(0)
        t = pl.program_id(1)

        @pl.when((b == 0) & (t == 0))
        def _():
            dw_ref[...] = jnp.zeros_like(dw_ref)
            dcb_ref[...] = jnp.zeros_like(dcb_ref)
            dlg_ref[...] = jnp.zeros_like(dlg_ref)
            dlb_ref[...] = jnp.zeros_like(dlb_ref)
            dbin_ref[...] = jnp.zeros_like(dbin_ref)

        hbuf[0:HALO, :] = jnp.where(t == 0, 0.0, _glu(zp_ref[...]))
        hbuf[HALO:, :] = _glu(zc_ref[...])
        _shifted_copies(hbuf, hs, HALO + TR - 8)
        last = t == nt - 1
        for r0 in range(0, ext, RT):
            yh, rstd = _layer_norm_parts(yc_ref[r0:r0 + RT, :] if r0 < TR else yn_ref[...])
            o = yh * lg_ref[...] + lb_ref[...]
            sg = jax.nn.sigmoid(o)
            if r0 < TR:
                dtok = dc_ref[r0:r0 + RT, :]
            else:
                dtok = jnp.where(last, 0.0, dn_ref[...])
            do = dtok * (sg * (1.0 + o * (1.0 - sg)))
            dyh = do * lg_ref[...]
            dy = rstd * (dyh - jnp.mean(dyh, axis=-1, keepdims=True)
                         - yh * jnp.mean(dyh * yh, axis=-1, keepdims=True))
            dybuf[r0:r0 + RT, :] = dy
            if r0 < TR:
                dlg_ref[...] += jnp.sum(do * yh, axis=0, keepdims=True)
                dlb_ref[...] += jnp.sum(do, axis=0, keepdims=True)
                dcb_ref[...] += jnp.sum(dy, axis=0, keepdims=True)
        _shifted_copies(dybuf, dys, ext - 8)
        for r0 in range(0, TR, RT):
            dh = jnp.zeros((RT, TOK), F32)
            for j in range(CONVW):
                dh = dh + w_ref[j:j + 1, :] * _tap(dybuf, dys, (CONVW - 1) - j, r0, RT)
            a = zc_ref[r0:r0 + RT, 0:TOK]
            sg = jax.nn.sigmoid(zc_ref[r0:r0 + RT, TOK:])
            da = dh * sg
            dg = dh * a * (sg * (1.0 - sg))
            du_ref[r0:r0 + RT, 0:TOK] = da.astype(BF16)
            du_ref[r0:r0 + RT, TOK:] = dg.astype(BF16)
            dbin_ref[:, 0:TOK] += jnp.sum(da, axis=0, keepdims=True)
            dbin_ref[:, TOK:] += jnp.sum(dg, axis=0, keepdims=True)
        for j in range(CONVW):
            acc = jnp.zeros((8, TOK), F32)
            for r0 in range(0, TR, RT):
                prod = dybuf[r0:r0 + RT, :] * _tap(hbuf, hs, (HALO - CONVW + 1) + j, r0, RT)
                acc = acc + jnp.sum(prod.reshape(RT // 8, 8, TOK), axis=0)
            dw_ref[j:j + 1, :] += jnp.sum(acc, axis=0, keepdims=True)

    vec = BS((1, TOK), lambda b, t: (0, 0))
    perh = TR // HALO
    pern = TR // NEXT
    nlast_n = T // NEXT - 1
    return pl.pallas_call(
        body, grid=(NB, nt),
        in_specs=[BS((TR, 2 * TOK), lambda b, t: (b * nt + t, 0)),
                  BS((HALO, 2 * TOK), lambda b, t: (jnp.maximum((b * nt + t) * perh - 1, 0), 0)),
                  BS((TR, TOK), lambda b, t: (b * nt + t, 0)),
                  BS((NEXT, TOK), lambda b, t: (jnp.minimum((b * nt + t + 1) * pern, nlast_n), 0)),
                  BS((TR, TOK), lambda b, t: (b * nt + t, 0)),
                  BS((NEXT, TOK), lambda b, t: (jnp.minimum((b * nt + t + 1) * pern, nlast_n), 0)),
                  BS((32, TOK), lambda b, t: (0, 0)), vec, vec, ANY],
        out_specs=[BS((TR, 2 * TOK), lambda b, t: (b * nt + t, 0)),
                   BS((32, TOK), lambda b, t: (0, 0)), vec, vec, vec,
                   BS((1, 2 * TOK), lambda b, t: (0, 0))],
        out_shape=[_sds((T, NBW), BF16), _sds((32, TOK), F32), _sds((1, TOK), F32),
                   _sds((1, TOK), F32), _sds((1, TOK), F32), _sds((1, 2 * TOK), F32)],
        input_output_aliases={9: 0},
        scratch_shapes=[pltpu.VMEM((HALO + TR, TOK), F32), pltpu.VMEM((ext, TOK), F32),
                        pltpu.VMEM((7, HALO + TR, TOK), F32), pltpu.VMEM((7, ext, TOK), F32)],
        compiler_params=_cp(("arbitrary", "arbitrary"), vmem_mb=56), name="conv_bwd",
    )(z, z, y, y, dcat, dcat, cw, lg, lb, dz_into)


def _ffn_up(h2, wgu, l, carry=None):
    ni = len(carry.ins) if carry else 0
    no = len(carry.out_shapes) if carry else 0

    def body(h_ref, wg_ref, wu_ref, *rest):
        cin = rest[:ni]
        g_ref, u_ref, a_ref = rest[ni:ni + 3]
        cout, cscr = rest[ni + 3:ni + 3 + no], rest[ni + 3 + no:]
        if carry:
            @pl.when((pl.program_id(0) == 0) & (pl.program_id(1) == 0))
            def _():
                carry.start(cin, cout, cscr)

        hv = h_ref[...]
        g = lax.dot_general(hv, wg_ref[...], _DIMS["nt"], preferred_element_type=F32)
        u = lax.dot_general(hv, wu_ref[...], _DIMS["nt"], preferred_element_type=F32)
        sg = jax.nn.sigmoid(g)
        silu = g * sg
        g_ref[...] = (u * (sg * (1.0 + g * (1.0 - sg)))).astype(BF16)
        u_ref[...] = silu.astype(BF16)
        a_ref[...] = (silu * u).astype(BF16)

        if carry:
            @pl.when((pl.program_id(0) == FF // FT - 1) & (pl.program_id(1) == T // TR - 1))
            def _():
                carry.finish(cin, cout, cscr)

    out = BS((TR, FT), lambda q, i: (i, q))
    outs = pl.pallas_call(
        body, grid=(FF // FT, T // TR),
        in_specs=[BS((TR, D), lambda q, i: (i, 0)),
                  BS((None, FT, D), lambda q, i: (0, q, 0)),
                  BS((None, FT, D), lambda q, i: (1, q, 0))] + (carry.in_specs if carry else []),
        out_specs=[out, out, out] + [ANY] * no,
        out_shape=[_sds((T, FF), BF16), _sds((T, FF), BF16), _sds((T, FF), BF16)]
        + (carry.out_shapes if carry else []),
        scratch_shapes=carry.scratch if carry else [],
        compiler_params=pltpu.CompilerParams(
            dimension_semantics=("arbitrary", "arbitrary"), vmem_limit_bytes=48 << 20,
            has_side_effects=bool(carry)), name=f"ffn_up_{l}",
    )(h2, wgu, wgu, *(carry.ins if carry else []))
    return outs[:3], outs[3:]


def _ffn_down_bwd(dx, wd, g, u, l):
    def epilogue(dact, ex, o_ref, i):
        o_ref[0] = (dact * ex[0][...].astype(F32)).astype(BF16)
        o_ref[1] = (dact * ex[1][...].astype(F32)).astype(BF16)

    ex_spec = BS((TR, FT), lambda i, q, k: (i, q))
    return _mm("nt", dx, wd, grid=(T // TR, FF // FT, 1),
               a_spec=BS((TR, D), lambda i, q, k: (i, 0)),
               b_spec=BS((FT, D), lambda i, q, k: (q, 0)),
               out_shape=_sds((2, T, FF), BF16),
               out_spec=BS((2, TR, FT), lambda i, q, k: (0, i, q)),
               acc_shape=(TR, FT), extras=(g, u), extra_specs=(ex_spec, ex_spec),
               epilogue=epilogue, name=f"ffn_down_bwd_{l}")


def _row_tile(rows, cols, itemsize=4, limit=2 << 20):
    tr = rows
    while tr * cols * itemsize > limit and tr % 2 == 0 and (tr // 2) % 16 == 0:
        tr //= 2
    return tr


def _cast_bf16(arrs, name):
    n = len(arrs)
    rows, cols = arrs[0].shape
    tr = _row_tile(rows, cols)

    def body(*refs):
        o_ref = refs[n]
        k = pl.program_id(0)
        val = refs[0][...]
        for j in range(1, n):
            val = jnp.where(k == j, refs[j][...], val)
        o_ref[...] = val.astype(BF16)

    return pl.pallas_call(
        body, grid=(n, rows // tr),
        in_specs=[BS((tr, cols), lambda k, i: (i, 0))] * n,
        out_specs=BS((None, tr, cols), lambda k, i: (k, i, 0)),
        out_shape=_sds((n, rows, cols), BF16),
        compiler_params=_cp(("arbitrary", "arbitrary")), name=name,
    )(*arrs)


def _quad_sum(own, got, name):
    n, rows, cols = own.shape
    tr = _row_tile(rows, cols)

    def body(a_ref, q_ref, o_ref):
        o_ref[...] = ((a_ref[...] + q_ref[0].astype(F32)) + q_ref[1].astype(F32)) + q_ref[2].astype(F32)

    spec = BS((None, tr, cols), lambda k, i: (k, i, 0))
    return pl.pallas_call(
        body, grid=(n, rows // tr),
        in_specs=[spec, BS((3, None, tr, cols), lambda k, i: (0, k, i, 0))], out_specs=spec,
        out_shape=_sds((n, rows, cols), F32),
        compiler_params=_cp(("arbitrary", "arbitrary")), name=name,
    )(own, got)


def _adam_math(w, g, m, v):
    m = ADAM_B1 * m + (1.0 - ADAM_B1) * g
    v = ADAM_B2 * v + (1.0 - ADAM_B2) * (g * g)
    m_hat = m / (1.0 - ADAM_B1 ** ADAM_STEP)
    v_hat = v / (1.0 - ADAM_B2 ** ADAM_STEP)
    delta = -ADAM_LR * (m_hat / (jnp.sqrt(v_hat) + ADAM_EPS) + ADAM_WD * w)
    return delta, m, v


def _adamw_big(w, g, m, v, name):
    shape = w.shape
    cols = shape[-1]
    rows = w.size // cols
    tr = _row_tile(rows, cols, limit=1 << 20)

    def body(w_ref, g_ref, m_ref, v_ref, d_ref, nm_ref, nv_ref):
        d, nm, nv = _adam_math(w_ref[...], g_ref[...], m_ref[...], v_ref[...])
        d_ref[...] = d
        nm_ref[...] = nm
        nv_ref[...] = nv

    spec = BS((tr, cols), lambda i: (i, 0))
    outs = pl.pallas_call(
        body, grid=(rows // tr,), in_specs=[spec] * 4, out_specs=[spec] * 3,
        out_shape=[_sds((rows, cols), F32)] * 3,
        compiler_params=_cp(("arbitrary",)), name=name,
    )(*[a.reshape(rows, cols) for a in (w, g, m, v)])
    return [o.reshape(shape) for o in outs]


def _adamw_small(ws, gs, ms, vs):
    n = len(ws)

    def body(*refs):
        for i in range(n):
            d, nm, nv = _adam_math(refs[i][...], refs[n + i][...], refs[2 * n + i][...],
                                   refs[3 * n + i][...])
            refs[4 * n + i][...] = d
            refs[5 * n + i][...] = nm
            refs[6 * n + i][...] = nv

    specs = [BS(w.shape, lambda i: (0, 0)) for w in ws]
    outs = pl.pallas_call(
        body, grid=(1,), in_specs=specs * 4, out_specs=specs * 3,
        out_shape=[_sds(w.shape, F32) for w in ws] * 3,
        compiler_params=_cp(("arbitrary",)), name="adamw_small",
    )(*ws, *gs, *ms, *vs)
    return outs[:n], outs[n:2 * n], outs[2 * n:]


def _place():
    x, y, c = lax.axis_index("x"), lax.axis_index("y"), lax.axis_index("c")
    chips = [(1 - x, y), (x, 1 - y), (1 - x, 1 - y)]
    return x, y, c, chips


class _Exchange:
    def __init__(self, ins, in_specs, out_shapes, scratch, start, finish):
        self.ins, self.in_specs, self.out_shapes, self.scratch = ins, in_specs, out_shapes, scratch
        self.start, self.finish = start, finish


def _run_exchange(ex, name, vmem_mb=40):
    ni, no = len(ex.ins), len(ex.out_shapes)

    def body(*refs):
        ex.start(refs[:ni], refs[ni:ni + no], refs[ni + no:])
        ex.finish(refs[:ni], refs[ni:ni + no], refs[ni + no:])

    return pl.pallas_call(
        body, in_specs=ex.in_specs, out_specs=[ANY] * no, out_shape=ex.out_shapes,
        scratch_shapes=ex.scratch,
        compiler_params=pltpu.CompilerParams(has_side_effects=True, vmem_limit_bytes=vmem_mb << 20),
        name=name,
    )(*ex.ins)


def _gather_exchange(srcs, dst_shapes, views, small=None):
    nu = len(srcs)
    nd = len(dst_shapes)
    ns = 1 if small is not None else 0

    def unpack(ins, outs, scr):
        x, y, c, chips = _place()
        src = ins[:nu]
        vw = [views[u](outs[:nd]) for u in range(nu)]
        vbuf = scr[:nu]
        send, recv, fsend, frecv, lsem, ssend, srecv, vsem = scr[nu:]

        def ici(u, j, shard, to):
            return pltpu.make_async_remote_copy(
                src_ref=vbuf[u].at[:, c], dst_ref=vw[u].at[:, shard, c],
                send_sem=send.at[3 * u + j], recv_sem=recv.at[3 * u + j],
                device_id=to, device_id_type=MESH)

        def fwd(u, j, shard, half):
            return pltpu.make_async_remote_copy(
                src_ref=vw[u].at[:, shard, half], dst_ref=vw[u].at[:, shard, half],
                send_sem=fsend.at[3 * u + j], recv_sem=frecv.at[3 * u + j],
                device_id=(x, y, 1 - c), device_id_type=MESH)

        def small_copy(j, shard, to):
            return pltpu.make_async_remote_copy(
                src_ref=ins[nu], dst_ref=outs[nd].at[shard],
                send_sem=ssend.at[j], recv_sem=srecv.at[j], device_id=to, device_id_type=MESH)

        stage = [pltpu.make_async_copy(src[u], vbuf[u], vsem.at[u]) for u in range(nu)]
        local = [pltpu.make_async_copy(vbuf[u], vw[u].at[:, 2 * x + y], lsem.at[u]) for u in range(nu)]
        if ns:
            local.append(pltpu.make_async_copy(ins[nu], outs[nd].at[2 * x + y], lsem.at[nu]))
        return x, y, c, chips, ici, fwd, small_copy, stage, local

    def start(ins, outs, scr):
        x, y, c, chips, ici, fwd, small_copy, stage, local = unpack(ins, outs, scr)
        s = 2 * x + y
        for cp in stage:
            cp.start()
        if ns:
            local[nu].start()
            for j, chip in enumerate(chips):
                small_copy(j, s, (*chip, c)).start()
        for u in range(nu):
            stage[u].wait()
            for j, chip in enumerate(chips):
                ici(u, j, s, (*chip, c)).start()
            local[u].start()

    def finish(ins, outs, scr):
        x, y, c, chips, ici, fwd, small_copy, stage, local = unpack(ins, outs, scr)
        s = 2 * x + y
        for u in range(nu):
            for j, chip in enumerate(chips):
                sj = 2 * chip[0] + chip[1]
                ici(u, j, sj, (x, y, c)).wait_recv()
                fwd(u, j, sj, c).start()
        for u in range(nu):
            for j, chip in enumerate(chips):
                sj = 2 * chip[0] + chip[1]
                fwd(u, j, sj, 1 - c).wait_recv()
        for u in range(nu):
            for j, chip in enumerate(chips):
                ici(u, j, s, (*chip, c)).wait_send()
                fwd(u, j, s, c).wait_send()
        if ns:
            for j, chip in enumerate(chips):
                small_copy(j, 2 * chip[0] + chip[1], (x, y, c)).wait_recv()
                small_copy(j, s, (*chip, c)).wait_send()
        for cp in local:
            cp.wait()

    dma = pltpu.SemaphoreType.DMA
    return _Exchange(
        ins=list(srcs) + ([small] if ns else []),
        in_specs=[ANY] * nu + [BS(memory_space=pltpu.VMEM)] * ns,
        out_shapes=[_sds(sh, BF16) for sh in dst_shapes]
        + ([_sds((NSH,) + small.shape, F32)] if ns else []),
        scratch=[pltpu.VMEM(a.shape, BF16) for a in srcs]
        + [dma((3 * nu,)), dma((3 * nu,)), dma((3 * nu,)), dma((3 * nu,)),
           dma((nu + 1,)), dma((3,)), dma((3,)), dma((nu,))],
        start=start, finish=finish)


def _pair_reduce(gs, name):
    nu = len(gs)
    ns = [g.shape[0] * NSH for g in gs]
    base = [sum(ns[:u]) for u in range(nu)]

    def body(*refs):
        g_refs, own_refs, sb_refs = refs[:nu], refs[nu:2 * nu], refs[2 * nu:3 * nu]
        bufs = refs[3 * nu:8 * nu]
        send, recv, lsem, osem = refs[8 * nu:]
        x, y, c, _ = _place()
        s = 2 * x + y

        def unit(u):
            sendb, recvb, stage, outf, outb = bufs[5 * u:5 * u + 5]

            def load(k, half):
                return pltpu.make_async_copy(g_refs[u].at[k // NSH, k % NSH, half], stage.at[k % 3],
                                             lsem.at[3 * u + k % 3])

            def push(k):
                return pltpu.make_async_remote_copy(
                    src_ref=sendb.at[k], dst_ref=recvb.at[k], send_sem=send.at[base[u] + k],
                    recv_sem=recv.at[base[u] + k], device_id=(x, y, 1 - c), device_id_type=MESH)

            def store(k):
                return pltpu.make_async_copy(outb.at[k % 2], sb_refs[u].at[k // NSH, k % NSH],
                                             osem.at[3 * u + k % 2])

            return sendb, recvb, stage, outf, outb, load, push, store

        for u in range(nu):
            sendb, recvb, stage, outf, outb, load, push, store = unit(u)
            for k in range(min(2, ns[u])):
                load(k, 1 - c).start()
            for k in range(ns[u]):
                if k + 2 < ns[u]:
                    load(k + 2, 1 - c).start()
                load(k, 1 - c).wait()
                sendb[k] = stage[k % 3].astype(BF16)
                push(k).start()
        for u in range(nu):
            sendb, recvb, stage, outf, outb, load, push, store = unit(u)
            n = ns[u]
            for k in range(min(2, n)):
                load(k, c).start()
            for k in range(n):
                if k + 2 < n:
                    load(k + 2, c).start()
                load(k, c).wait()
                push(k).wait_recv()
                total = stage[k % 3] + recvb[k].astype(F32)
                if k >= 2:
                    store(k - 2).wait()
                outb[k % 2] = total.astype(BF16)
                store(k).start()

                @pl.when(s == k % NSH)
                def _():
                    outf[...] = total
                    keep = pltpu.make_async_copy(outf, own_refs[u].at[k // NSH], osem.at[3 * u + 2])
                    keep.start()
                    keep.wait()

            for k in range(max(n - 2, 0), n):
                store(k).wait()
        for u in range(nu):
            push = unit(u)[6]
            for k in range(ns[u]):
                push(k).wait_send()

    dma = pltpu.SemaphoreType.DMA
    scratch = []
    for g, n in zip(gs, ns):
        rh, cc = g.shape[3], g.shape[4]
        scratch += [pltpu.VMEM((n, rh, cc), BF16), pltpu.VMEM((n, rh, cc), BF16),
                    pltpu.VMEM((3, rh, cc), F32), pltpu.VMEM((rh, cc), F32), pltpu.VMEM((2, rh, cc), BF16)]
    outs = pl.pallas_call(
        body, in_specs=[ANY] * nu, out_specs=[ANY] * (2 * nu),
        out_shape=[_sds((g.shape[0], g.shape[3], g.shape[4]), F32) for g in gs]
        + [_sds((g.shape[0], NSH, g.shape[3], g.shape[4]), BF16) for g in gs],
        scratch_shapes=scratch + [dma((sum(ns),)), dma((sum(ns),)), dma((3 * nu,)), dma((3 * nu,))],
        compiler_params=pltpu.CompilerParams(has_side_effects=True, vmem_limit_bytes=56 << 20),
        name=name,
    )(*gs)
    return list(outs[:nu]), list(outs[nu:])


def _chip_exchange(sums_bf16):
    nu = len(sums_bf16)

    def pushes(ins, outs, scr):
        x, y, c, chips = _place()
        send, recv = scr
        return [pltpu.make_async_remote_copy(
            src_ref=ins[u].at[:, 2 * chip[0] + chip[1]], dst_ref=outs[u].at[j],
            send_sem=send.at[3 * u + j], recv_sem=recv.at[3 * u + j],
            device_id=(*chip, c), device_id_type=MESH)
            for u in range(nu) for j, chip in enumerate(chips)]

    def start(ins, outs, scr):
        for cp in pushes(ins, outs, scr):
            cp.start()

    def finish(ins, outs, scr):
        for cp in pushes(ins, outs, scr):
            cp.wait()

    dma = pltpu.SemaphoreType.DMA
    shapes = [(3, a.shape[0], a.shape[2], a.shape[3]) for a in sums_bf16]
    return _Exchange(ins=list(sums_bf16), in_specs=[ANY] * nu,
                     out_shapes=[_sds(sh, BF16) for sh in shapes],
                     scratch=[dma((3 * nu,)), dma((3 * nu,))], start=start, finish=finish)


def _final_exchange(halves, out_shapes, targets):
    nu = len(halves)
    no = len(out_shapes)
    ncp = sum(len(t) for t in targets)

    def body(*refs):
        hv = refs[:nu]
        out = refs[nu:nu + no]
        sbuf = refs[nu + no:2 * nu + no]
        rbuf = refs[2 * nu + no:3 * nu + no]
        send, recv, lsem, osem, csem = refs[3 * nu + no:]
        x, y, c, _ = _place()
        stage = [pltpu.make_async_copy(hv[u], sbuf[u], lsem.at[u]) for u in range(nu)]
        push = [pltpu.make_async_remote_copy(
            src_ref=sbuf[u], dst_ref=rbuf[u], send_sem=send.at[u], recv_sem=recv.at[u],
            device_id=(x, y, 1 - c), device_id_type=MESH) for u in range(nu)]
        mine, theirs = [], []
        k = 0
        for u in range(nu):
            rh = hv[u].shape[1]
            for (p, oi, li) in targets[u]:
                mine.append((u, pltpu.make_async_copy(
                    sbuf[u].at[p], out[oi].at[li, pl.ds(c * rh, rh), :], csem.at[k])))
                theirs.append((u, pltpu.make_async_copy(
                    rbuf[u].at[p], out[oi].at[li, pl.ds((1 - c) * rh, rh), :], osem.at[k])))
                k += 1
        for cp in stage:
            cp.start()
        for u in range(nu):
            stage[u].wait()
            push[u].start()
            for (v, cp) in mine:
                if v == u:
                    cp.start()
        for u in range(nu):
            push[u].wait_recv()
            for (v, cp) in theirs:
                if v == u:
                    cp.start()
        for (_, cp) in theirs + mine:
            cp.wait()
        for u in range(nu):
            push[u].wait_send()

    dma = pltpu.SemaphoreType.DMA
    bufs = [pltpu.VMEM(h.shape, F32) for h in halves]
    return pl.pallas_call(
        body, in_specs=[ANY] * nu, out_specs=[ANY] * no,
        out_shape=[_sds(sh, F32) for sh in out_shapes],
        scratch_shapes=bufs + bufs + [dma((nu,)), dma((nu,)), dma((nu,)), dma((ncp,)), dma((ncp,))],
        compiler_params=pltpu.CompilerParams(has_side_effects=True, vmem_limit_bytes=56 << 20),
        name="final_exchange",
    )(*halves)


def _small_allreduce(pack):
    rows = pack.shape[0]

    def body(p_ref, o_ref, buf, send, recv):
        x, y, c, _ = _place()
        me = 4 * x + 2 * y + c
        buf[me] = p_ref[...]
        k = 0
        copies = []
        for dx in range(2):
            for dy in range(2):
                for dc in range(2):
                    if dx == 0 and dy == 0 and dc == 0:
                        continue
                    to = (jnp.where(dx, 1 - x, x), jnp.where(dy, 1 - y, y), jnp.where(dc, 1 - c, c))
                    src_slot = 4 * to[0] + 2 * to[1] + to[2]
                    copies.append((pltpu.make_async_remote_copy(
                        src_ref=p_ref, dst_ref=buf.at[me], send_sem=send.at[k], recv_sem=recv.at[k],
                        device_id=to, device_id_type=MESH), src_slot, k))
                    k += 1
        for cp, _, _ in copies:
            cp.start()
        for cp, src_slot, k in copies:
            pltpu.make_async_remote_copy(
                src_ref=p_ref, dst_ref=buf.at[src_slot], send_sem=send.at[k], recv_sem=recv.at[k],
                device_id=(x, y, c), device_id_type=MESH).wait()
        acc = buf[0]
        for d in range(1, 8):
            acc = acc + buf[d]
        o_ref[...] = acc

    dma = pltpu.SemaphoreType.DMA
    vm = BS(memory_space=pltpu.VMEM)
    return pl.pallas_call(
        body, in_specs=[vm], out_specs=vm, out_shape=_sds((rows, D), F32),
        scratch_shapes=[pltpu.VMEM((8, rows, D), F32), dma((7,)), dma((7,))],
        compiler_params=pltpu.CompilerParams(has_side_effects=True, vmem_limit_bytes=32 << 20),
        name="small_allreduce",
    )(pack)


def _in_proj(h, w, bias, name, transposed=False, carry=None):
    n = w.shape[0] if transposed else w.shape[1]
    tn = 1280 if n == NA else 1792
    ep = None
    extras, especs = (), ()
    if bias is not None:
        def ep(acc, ex, o_ref, i):
            o_ref[...] = acc + ex[0][...]
        extras = (bias,)
        especs = (BS((1, tn), lambda i, j, k: (0, j)),)
    b_spec = BS((tn, D), lambda i, j, k: (j, 0)) if transposed else BS((D, tn), lambda i, j, k: (0, j))
    return _mm("nt" if transposed else "nn", h, w, grid=(T // TR, n // tn, 1),
               a_spec=BS((TR, D), lambda i, j, k: (i, 0)), b_spec=b_spec,
               out_shape=_sds((T, n), F32), out_spec=BS((TR, tn), lambda i, j, k: (i, j)),
               acc_shape=(TR, tn), extras=extras, extra_specs=especs, epilogue=ep, name=name, carry=carry)


def _res_rms_epilogue(acc, ex, outs, i):
    y = acc + ex[0][...]
    outs[0][...] = y
    r = lax.rsqrt(jnp.mean(y * y, axis=-1, keepdims=True) + EPS)
    outs[1][...] = (y * r * ex[1][...]).astype(BF16)


def _res_loss_epilogue(acc, ex, outs, i):
    e = acc + ex[0][...] - ex[1][...]
    outs[1][...] = e * (1.0 / D)

    @pl.when(i == 0)
    def _():
        outs[0][...] = jnp.zeros_like(outs[0])

    outs[0][...] += 0.5 * jnp.sum(jnp.mean(e * e, axis=-1, keepdims=True), axis=0, keepdims=True)


def _rms_bwd_epilogue(dh, ex, outs, i):
    xv = ex[0][...]
    r = lax.rsqrt(jnp.mean(xv * xv, axis=-1, keepdims=True) + EPS)
    xh = xv * r
    gy = dh * ex[1][...]
    outs[0][...] = r * (gy - xh * jnp.mean(gy * xh, axis=-1, keepdims=True)) + ex[2][...]

    @pl.when(i == 0)
    def _():
        outs[1][...] = jnp.zeros_like(outs[1])

    outs[1][...] += jnp.sum(dh * xh, axis=0, keepdims=True)


def _local_step(x, mem, target, w, p, carries=None, bwd_carry_fn=None, late_carry_fn=None):
    row = lambda i, j, k: (i, 0)
    whole = lambda i, j, k: (0, 0)
    w = {k: (list(v) if isinstance(v, list) else v) for k, v in w.items()}
    carries = carries or {}

    def carry_of(name):
        return carries[name][0] if name in carries else None

    def delivered(name, outs):
        if name in carries:
            carries[name][1](w, outs)

    saved = []
    bias = _bias_expand(p["rel_u"])
    vec = BS((1, D), whole)
    h = _rms_fwd(x, p["norm1_g"][0:1], "rms1_0", carry=carry_of("rms1_0"))
    if carry_of("rms1_0") is not None:
        h, carried = h
        delivered("rms1_0", carried)
    for l in range(2):
        type_a = l == 0
        memn = _rms_fwd(mem, p["mem_norm_g"][l:l + 1], f"rmsmem_{l}")
        if type_a:
            z = _in_proj(h, w["a"], None, "inproj_a", carry=carry_of("inproj_a"))
            if carry_of("inproj_a") is not None:
                z, carried = z
                delivered("inproj_a", carried)
            cat, carried = _attn_fwd(z, bias, p["a_q_g2"], p["a_k_g2"], carry_of("attn_fwd"))
            delivered("attn_fwd", carried)
            qcol = NA // MEMW - 1
        else:
            z = _in_proj(h, w["b"], w["b_b_in"], "inproj_b", transposed=True)
            cat, conv_y = _conv_fwd(z, w["conv_w"], w["conv_b"], w["ln_g"], w["ln_b"])
            qcol = NBW // MEMW - 1
        kv = _mm("nn", memn, w["kv"][l], grid=(1, 1, 1),
                 a_spec=BS((NB * MEMT, D), whole), b_spec=BS((D, 2 * MEMW), whole),
                 out_shape=_sds((NB * MEMT, 2 * MEMW), F32), out_spec=BS((NB * MEMT, 2 * MEMW), whole),
                 acc_shape=(8, 128), name=f"memkv_{l}")
        cat = _memattn_fwd(z, kv, cat, p["mq_g4"][l:l + 1], p["mk_g4"][l:l + 1], qcol, f"memattn_fwd_{l}")
        x1, h2 = _mm("nn", cat, w["wo"][l], grid=(T // TR, 1, 1), a_spec=BS((TR, D), row),
                     b_spec=BS((D, D), whole),
                     out_shape=[_sds((T, D), F32), _sds((T, D), BF16)],
                     out_spec=[BS((TR, D), row), BS((TR, D), row)], acc_shape=(8, 128),
                     extras=(x, p["norm2_g"][l:l + 1]), extra_specs=(BS((TR, D), row), vec),
                     epilogue=_res_rms_epilogue, name=f"outproj_{l}")
        (g, u, act), carried = _ffn_up(h2, w["gu"][l], l, carry_of(f"ffn_up_{l}"))
        delivered(f"ffn_up_{l}", carried)
        last = l == 1
        res = _mm("nn", act, w["wd"][l], grid=(T // TR, 1, 1),
                  a_spec=BS((TR, FF), row), b_spec=BS((FF, D), whole),
                  out_shape=[_sds((1, 1), F32), _sds((T, D), F32)] if last else
                  [_sds((T, D), F32), _sds((T, D), BF16)],
                  out_spec=[BS((1, 1), whole), BS((TR, D), row)] if last else
                  [BS((TR, D), row), BS((TR, D), row)],
                  acc_shape=(8, 128), extras=(x1, target if last else p["norm1_g"][1:2]),
                  extra_specs=(BS((TR, D), row), BS((TR, D), row) if last else vec),
                  epilogue=_res_loss_epilogue if last else _res_rms_epilogue, sequential=last,
                  name=f"ffn_down_{l}", carry=carry_of(f"ffn_down_{l}"))
        if carry_of(f"ffn_down_{l}") is not None:
            res, carried = res
            delivered(f"ffn_down_{l}", carried)
        saved.append(dict(x=x, h=h, memn=memn, kv=kv, z=z, cat=cat, x1=x1, h2=h2, g=g, u=u, act=act,
                          qcol=qcol))
        if last:
            loss, dx = res
        else:
            x, h = res

    big = dict(a=None, b=None, kv=[None, None], wo=[None, None], gu=[None, None], wd=[None, None])
    small = {}
    bwd_carried, late_carried = (), ()
    tk = T // 2
    nkt = T // tk
    for l in (1, 0):
        sv = saved[l]
        dgu = _ffn_down_bwd(dx, w["wd"][l], sv["g"], sv["u"], l)
        big["wd"][l] = _mm("tn", sv["act"], dx, grid=(FF // FT, 1, 2 * nkt),
                           a_spec=BS((tk // 2, FT), lambda i, j, k: (k, i)),
                           b_spec=BS((tk // 2, D), lambda i, j, k: (k, 0)),
                           out_shape=_sds((FF, D), F32), out_spec=BS((FT, D), lambda i, j, k: (i, 0)),
                           acc_shape=(FT, D), name=f"dw_down_{l}")
        dx1, small[f"norm2_g{l}"] = _mm(
            "nn", dgu, w["gu"][l], grid=(T // TR, 1, 2),
            a_spec=BS((None, TR, FF), lambda i, j, k: (k, i, 0)),
            b_spec=BS((None, FF, D), lambda i, j, k: (k, 0, 0)),
            out_shape=[_sds((T, D), F32), _sds((1, D), F32)], out_spec=[BS((TR, D), row), vec],
            acc_shape=(TR, D), extras=(sv["x1"], p["norm2_g"][l:l + 1], dx),
            extra_specs=(BS((TR, D), row), vec, BS((TR, D), row)),
            epilogue=_rms_bwd_epilogue, sequential=True, name=f"dh2_{l}")
        big["gu"][l] = _mm("tn", dgu, sv["h2"], grid=(2 * FF // FT, 1, nkt),
                           a_spec=BS((None, tk, FT), lambda i, j, k: (i // 2, k, i % 2)),
                           b_spec=BS((tk, D), lambda i, j, k: (k, 0)),
                           out_shape=_sds((2, FF, D), F32),
                           out_spec=BS((None, FT, D), lambda i, j, k: (i // 2, i % 2, 0)),
                           acc_shape=(FT, D), vmem_mb=58, name=f"dw_gu_{l}")
        dcat = _mm("nt", dx1, w["wo"][l], grid=(T // TR, 1, 1), a_spec=BS((TR, D), row),
                   b_spec=BS((D, D), whole),
                   out_shape=_sds((T, D), F32), out_spec=BS((TR, D), row), acc_shape=(8, 128),
                   name=f"dcat_{l}")
        big["wo"][l] = _mm("tn", sv["cat"], dx1, grid=(1, 1, nkt),
                           a_spec=BS((tk, D), lambda i, j, k: (k, 0)), b_spec=BS((tk, D), lambda i, j, k: (k, 0)),
                           out_shape=_sds((D, D), F32), out_spec=BS((D, D), whole),
                           acc_shape=(D, D), name=f"dw_out_{l}")
        dqm, dkv, small[f"mq_g{l}"], small[f"mk_g{l}"] = _memattn_bwd(
            sv["z"], sv["kv"], dcat, p["mq_g4"][l:l + 1], p["mk_g4"][l:l + 1], sv["qcol"], f"memattn_bwd_{l}",
            dz_cols=NA if l == 0 else NBW)
        big["kv"][l] = _mm("tn", sv["memn"], dkv, grid=(1, 1, 1),
                           a_spec=BS((NB * MEMT, D), whole), b_spec=BS((NB * MEMT, 2 * MEMW), whole),
                           out_shape=_sds((D, 2 * MEMW), F32), out_spec=BS((D, 2 * MEMW), whole),
                           acc_shape=(8, 128), name=f"dw_kv_{l}")
        dmemn = _mm("nt", dkv, w["kv"][l], grid=(1, 1, 1),
                    a_spec=BS((NB * MEMT, 2 * MEMW), whole), b_spec=BS((D, 2 * MEMW), whole),
                    out_shape=_sds((NB * MEMT, D), F32), out_spec=BS((NB * MEMT, D), whole),
                    acc_shape=(8, 128), name=f"dmemn_{l}")
        _, small[f"mem_norm_g{l}"] = _rms_bwd(dmemn, mem, p["mem_norm_g"][l:l + 1], None, f"rmsmem_bwd_{l}")
        if l == 0:
            carry = bwd_carry_fn(big) if bwd_carry_fn is not None else None
            (dz, dbias, small["a_q_g"], small["a_k_g"]), bwd_carried = _attn_bwd(
                sv["z"], dcat, dqm, bias, p["a_q_g2"], p["a_k_g2"], carry)
            small["rel_u"] = _bias_reduce(dbias)
            w_in, key, n, tn = w["a"], "a", NA, 1280
        else:
            dz, small["conv_w"], small["conv_b"], small["ln_g"], small["ln_b"], dbin_u = _conv_bwd(
                sv["z"], conv_y, dcat, dqm, w["conv_w"], w["ln_g"], w["ln_b"])
            small["b_in_u"] = dbin_u
            small["b_in_qm"] = _colsum(dz, "colsum_dqm", cols=MEMW, block=NBW // MEMW - 1)
            w_in, key, n, tn = w["b"], "b", NBW, 896
        norm_bwd = dict(out_shape=[_sds((T, D), F32), _sds((1, D), F32)], out_spec=[BS((TR, D), row), vec],
                        acc_shape=(8, 128), extras=(sv["x"], p["norm1_g"][l:l + 1], dx1),
                        extra_specs=(BS((TR, D), row), vec, BS((TR, D), row)),
                        epilogue=_rms_bwd_epilogue, sequential=True, name=f"dh_{l}")
        if l == 0:
            big[key] = _mm("tn", sv["h"], dz, grid=(1, n // tn, nkt),
                           a_spec=BS((tk, D), lambda i, j, k: (k, 0)), b_spec=BS((tk, tn), lambda i, j, k: (k, j)),
                           out_shape=_sds((D, n), F32), out_spec=BS((D, tn), lambda i, j, k: (0, j)),
                           acc_shape=(D, tn), name=f"dw_in_{l}")
            carry = late_carry_fn(big) if late_carry_fn is not None else None
            res = _mm("nt", dz, w_in, grid=(T // TR, 1, 1), a_spec=BS((TR, n), row),
                      b_spec=BS((D, n), whole), carry=carry, **norm_bwd)
            if carry is not None:
                res, late_carried = res
            dx, small[f"norm1_g{l}"] = res
        else:
            dx, small[f"norm1_g{l}"] = _mm("nn", dz, w_in, grid=(T // TR, 1, 1),
                                           a_spec=BS((TR, n), row), b_spec=BS((n, D), whole), **norm_bwd)
            big[key] = _mm("tn", dz, sv["h"], grid=(n // tn, 1, nkt),
                           a_spec=BS((tk, tn), lambda i, j, k: (k, i)), b_spec=BS((tk, D), lambda i, j, k: (k, 0)),
                           out_shape=_sds((n, D), F32), out_spec=BS((tn, D), lambda i, j, k: (i, 0)),
                           acc_shape=(tn, D), name=f"dw_in_{l}")
    return loss, dx, big, small, bwd_carried, late_carried


def _colsum(a, name, cols=None, block=0):
    rows = a.shape[0]
    cols = cols or a.shape[1]

    def body(a_ref, o_ref):
        @pl.when(pl.program_id(0) == 0)
        def _():
            o_ref[...] = jnp.zeros_like(o_ref)

        o_ref[...] += jnp.sum(a_ref[...].astype(F32), axis=0, keepdims=True)

    return pl.pallas_call(
        body, grid=(rows // TR,), in_specs=[BS((TR, cols), lambda i: (i, block))],
        out_specs=BS((1, cols), lambda i: (0, 0)), out_shape=_sds((1, cols), F32),
        compiler_params=_cp(("arbitrary",)), name=name,
    )(a)


_PACK_ROWS = 64


def _pack_small(sm):
    plan = [("norm1_g0", 0, 1, 0, D), ("norm1_g1", 1, 1, 0, D), ("mem_norm_g0", 2, 1, 0, D),
            ("mem_norm_g1", 3, 1, 0, D), ("norm2_g0", 4, 1, 0, D), ("norm2_g1", 5, 1, 0, D),
            ("a_q_g", 6, 1, 0, AW), ("a_k_g", 7, 1, 0, AW), ("mq_g0", 8, 1, 0, MEMW),
            ("mq_g1", 9, 1, 0, MEMW), ("mk_g0", 10, 1, 0, MEMW), ("mk_g1", 11, 1, 0, MEMW),
            ("conv_b", 12, 1, 0, TOK), ("ln_g", 13, 1, 0, TOK), ("ln_b", 14, 1, 0, TOK),
            ("b_in_u", 15, 1, 0, D), ("b_in_u", 16, 1, D, 2 * TOK - D), ("b_in_qm", 17, 1, 0, MEMW),
            ("conv_w", 18, CONVW, 0, TOK), ("rel_u", 49, 12, 0, D)]
    arrs = [sm[name].reshape(12, D) if name == "rel_u" else sm[name] for name, *_ in plan]

    def body(*refs):
        o_ref = refs[-1]
        o_ref[...] = jnp.zeros_like(o_ref)
        for ref, (_, r0, nr, c0, nc) in zip(refs, plan):
            o_ref[r0:r0 + nr, 0:nc] = ref[0:nr, c0:c0 + nc]

    return pl.pallas_call(
        body, grid=(1,), in_specs=[BS(a.shape, lambda i: (0, 0)) for a in arrs],
        out_specs=BS((_PACK_ROWS, D), lambda i: (0, 0)), out_shape=_sds((_PACK_ROWS, D), F32),
        compiler_params=_cp(("arbitrary",)), name="pack_small",
    )(*arrs)


def _group_qkv(a, inverse=False):
    shape = (D, AG, 3, AW) if inverse else (D, 3, AG, AW)
    qkv = a[:, :3 * TOK].reshape(shape).transpose(0, 2, 1, 3).reshape(D, 3 * TOK)
    return jnp.concatenate([qkv, a[:, 3 * TOK:]], axis=1)


def _rel_table_to_u(rel_bias):
    flat = jnp.concatenate([jnp.broadcast_to(rel_bias[:, 191:192], (12, 447)), rel_bias[:, ::-1]], axis=1)
    return jnp.pad(flat, ((0, 0), (192, 1024 - 192 - 639))).reshape(12, 1, 1024)


def _u_to_rel_table(du):
    flat = du[:, 192:192 + 639]
    g = flat[:, 447:][:, ::-1]
    return g, flat[:, :447]


def kernel(x, mem, norm1_g, mem_norm_g, a_w_in, a_q_g, a_k_g, a_rel_bias, b_w_in, b_b_in, b_conv_w, b_conv_b, b_ln_g, b_ln_b, mq_g, mk_g, w_mem_kv, w_out, norm2_g, w_gate, w_up, w_down, loss_target, m_norm1_g, m_mem_norm_g, m_a_w_in, m_a_q_g, m_a_k_g, m_a_rel_bias, m_b_w_in, m_b_b_in, m_b_conv_w, m_b_conv_b, m_b_ln_g, m_b_ln_b, m_mq_g, m_mk_g, m_w_mem_kv, m_w_out, m_norm2_g, m_w_gate, m_w_up, m_w_down, v_norm1_g, v_mem_norm_g, v_a_w_in, v_a_q_g, v_a_k_g, v_a_rel_bias, v_b_w_in, v_b_b_in, v_b_conv_w, v_b_conv_b, v_b_ln_g, v_b_ln_b, v_mq_g, v_mk_g, v_w_mem_kv, v_w_out, v_norm2_g, v_w_gate, v_w_up, v_w_down):
    sx = 2 * lax.axis_index("x") + lax.axis_index("y")

    n_in = (NA // NSH, NBW // NSH)
    tr = lambda a: jnp.swapaxes(a, -1, -2)
    small_src = jnp.concatenate([
        jnp.pad(b_b_in, ((0, 0), (0, 512 - 448))),
        jnp.pad(b_conv_w[0], ((0, 0), (0, 512 - 192))),
        jnp.pad(jnp.concatenate([b_conv_b, b_ln_g, b_ln_b], 0), ((0, 0), (0, 512 - 192))),
        jnp.zeros((5, 512), F32)], 0)

    def gather_groups(groups, small=None):
        def src_of(l, name):
            if name == "in":
                return [_cast_bf16([tr(b_w_in[0])], "cast_in_1").reshape(1, 2, n_in[1] // 2, D) if l else
                        _cast_bf16([a_w_in[0]], "cast_in_0").reshape(1, 2, D // 2, n_in[0])]
            if name == "gu":
                return [_cast_bf16([tr(w_gate[l])], f"cast_gate_{l}").reshape(1, 2, FS // 2, D),
                        _cast_bf16([tr(w_up[l])], f"cast_up_{l}").reshape(1, 2, FS // 2, D)]
            arr, shape = {"kv": (w_mem_kv, (1, 2, 128, 2 * MEMW)), "wo": (w_out, (1, 2, 128, D)),
                          "wd": (w_down, (1, 2, FS // 2, D))}[name]
            return [_cast_bf16([arr[l]], f"cast_{name}_{l}").reshape(shape)]

        def dst_of(l, name):
            if name == "in":
                return (1, NSH, 2, n_in[1] // 2, D) if l else (1, NSH, 2, D // 2, n_in[0])
            return {"kv": (1, NSH, 2, 128, 2 * MEMW), "wo": (1, NSH, 2, 128, D),
                    "gu": (1, 2, NSH, 2, FS // 2, D), "wd": (1, NSH, 2, FS // 2, D)}[name]

        items = [(l, name) for l, names in groups for name in names]
        srcs, views = [], []
        for k, (l, name) in enumerate(items):
            srcs += src_of(l, name)
            if name == "gu":
                views += [lambda d, k=k: d[k].at[:, 0], lambda d, k=k: d[k].at[:, 1]]
            else:
                views.append(lambda d, k=k: d[k])

        def done(w, outs):
            for k, (l, name) in enumerate(items):
                if name == "in" and l == 0:
                    w["a"] = _group_qkv(outs[k].reshape(NSH, D, n_in[0]).transpose(1, 0, 2).reshape(D, NA))
                elif name == "in":
                    w["b"] = outs[k].reshape(NBW, D)
                else:
                    shape = {"kv": (D, 2 * MEMW), "wo": (D, D), "gu": (2, FF, D), "wd": (FF, D)}
                    w[name][l] = outs[k].reshape(shape[name])

        return _gather_exchange(srcs, [dst_of(l, name) for l, name in items], views, small), done

    w = dict(a=None, b=None, kv=[None, None], wo=[None, None], gu=[None, None], wd=[None, None])
    first, first_in_done = gather_groups([(0, ["in"])], small_src)

    def first_done(w, outs):
        first_in_done(w, outs)
        small_all = outs[1]
        conv_w_full = small_all[:, 1:1 + CONVW, :192].transpose(1, 0, 2).reshape(CONVW, TOK)
        vec3 = small_all[:, 32:35, :192].transpose(1, 0, 2).reshape(3, TOK)
        w.update(b_b_in=small_all[:, 0, :448].reshape(1, NBW), conv_w=jnp.pad(conv_w_full, ((0, 1), (0, 0))),
                 conv_b=vec3[0:1], ln_g=vec3[1:2], ln_b=vec3[2:3])

    carries = {"rms1_0": (first, first_done),
               "inproj_a": gather_groups([(0, ["kv", "wo", "wd"])]),
               "attn_fwd": gather_groups([(0, ["gu"]), (1, ["in", "kv", "wo"])]),
               "ffn_up_0": gather_groups([(1, ["gu"])]),
               "ffn_down_0": gather_groups([(1, ["wd"])])}
    p = dict(
        norm1_g=norm1_g, mem_norm_g=mem_norm_g, norm2_g=norm2_g,
        a_q_g2=jnp.tile(a_q_g, (1, AH)), a_k_g2=jnp.tile(a_k_g, (1, AH)),
        mq_g4=jnp.tile(mq_g, (1, 4)), mk_g4=jnp.tile(mk_g, (1, 4)),
        rel_u=_rel_table_to_u(a_rel_bias[0]))

    def pair_sums(items, big, name):
        units = []
        for l, tensor in items:
            if tensor == "in" and l == 0:
                g = _group_qkv(big["a"], inverse=True).reshape(D, NSH, n_in[0]).transpose(1, 0, 2)
                g = g.reshape(1, NSH, 2, D // 2, n_in[0])
            elif tensor == "in":
                g = big["b"].reshape(1, NSH, 2, n_in[1] // 2, D)
            else:
                shape = {"kv": (1, NSH, 2, 128, 2 * MEMW), "wo": (1, NSH, 2, 128, D),
                         "gu": (2, NSH, 2, FS // 2, D), "wd": (1, NSH, 2, FS // 2, D)}
                g = big[tensor][l].reshape(shape[tensor])
            units.append(g)
        return _pair_reduce(units, name)

    early_groups = [[(1, "gu")], [(0, "gu")],
                    [(1, "in"), (1, "kv"), (1, "wo"), (1, "wd"), (0, "kv"), (0, "wo"), (0, "wd")]]
    early = [item for grp in early_groups for item in grp]
    late = [(0, "in")]
    own_early, own_late = [], []

    def bwd_carry_fn(big):
        sums_b = []
        for k, grp in enumerate(early_groups):
            own, sb = pair_sums(grp, big, f"pair_reduce_early_{k}")
            own_early.extend(own)
            sums_b.extend(sb)
        return _chip_exchange(sums_b)

    def late_carry_fn(big):
        own, sb = pair_sums(late, big, "pair_reduce_late")
        own_late.extend(own)
        return _chip_exchange(sb)

    loss, grad_x, big, small, parts_early, parts_late = _local_step(
        x.reshape(T, D), mem.reshape(NB * MEMT, D), loss_target.reshape(T, D), w, p,
        carries=carries, bwd_carry_fn=bwd_carry_fn, late_carry_fn=late_carry_fn)
    loss = lax.psum(loss[0, 0], ("x", "y", "c"))
    items = early + late
    halves = [_quad_sum(o, pt, f"quad_sum_{name}_{l}")
              for (l, name), o, pt in zip(items, own_early + own_late, list(parts_early) + list(parts_late))]
    out_shapes = [(1, D, NA // NSH), (1, NBW // NSH, D), (2, 2 * 128, 2 * MEMW), (2, 2 * 128, D),
                  (2, FS, D), (2, FS, D), (2, FS, D)]
    target_of = {"in": lambda l: [(0, l, 0)], "kv": lambda l: [(0, 2, l)], "wo": lambda l: [(0, 3, l)],
                 "gu": lambda l: [(0, 4, l), (1, 5, l)], "wd": lambda l: [(0, 6, l)]}
    targets = [target_of[name](l) for l, name in items]
    g_a, g_b, g_kv, g_wo, g_gate, g_up, g_wd = _final_exchange(halves, out_shapes, targets)

    tot = _small_allreduce(_pack_small(small))
    g_rel, clip_part = _u_to_rel_table(tot[49:61])
    g_rel = jnp.concatenate([g_rel[:, :191], g_rel[:, 191:] + _rowsum(clip_part)], axis=1)
    b_in_full = jnp.concatenate([tot[15:16], tot[16:17, :512], tot[17:18, :MEMW]], axis=1)
    g_small = dict(
        norm1_g=tot[0:2], mem_norm_g=tot[2:4], norm2_g=tot[4:6],
        a_q_g=tot[6:7, :HD], a_k_g=tot[7:8, :HD], a_rel_bias=g_rel[None],
        b_b_in=lax.dynamic_slice(b_in_full, (0, sx * 448), (1, 448)),
        b_conv_w=lax.dynamic_slice(tot[18:49, :TOK], (0, sx * 192), (CONVW, 192))[None],
        b_conv_b=lax.dynamic_slice(tot[12:13, :TOK], (0, sx * 192), (1, 192)),
        b_ln_g=lax.dynamic_slice(tot[13:14, :TOK], (0, sx * 192), (1, 192)),
        b_ln_b=lax.dynamic_slice(tot[14:15, :TOK], (0, sx * 192), (1, 192)),
        mq_g=tot[8:10, :HD], mk_g=tot[10:12, :HD])

    names = ["norm1_g", "mem_norm_g", "a_w_in", "a_q_g", "a_k_g", "a_rel_bias", "b_w_in", "b_b_in",
             "b_conv_w", "b_conv_b", "b_ln_g", "b_ln_b", "mq_g", "mk_g", "w_mem_kv", "w_out",
             "norm2_g", "w_gate", "w_up", "w_down"]
    weights = dict(zip(names, [norm1_g, mem_norm_g, a_w_in, a_q_g, a_k_g, a_rel_bias, b_w_in, b_b_in,
                               b_conv_w, b_conv_b, b_ln_g, b_ln_b, mq_g, mk_g, w_mem_kv, w_out,
                               norm2_g, w_gate, w_up, w_down]))
    ms = dict(zip(names, [m_norm1_g, m_mem_norm_g, m_a_w_in, m_a_q_g, m_a_k_g, m_a_rel_bias, m_b_w_in,
                          m_b_b_in, m_b_conv_w, m_b_conv_b, m_b_ln_g, m_b_ln_b, m_mq_g, m_mk_g,
                          m_w_mem_kv, m_w_out, m_norm2_g, m_w_gate, m_w_up, m_w_down]))
    vs = dict(zip(names, [v_norm1_g, v_mem_norm_g, v_a_w_in, v_a_q_g, v_a_k_g, v_a_rel_bias, v_b_w_in,
                          v_b_b_in, v_b_conv_w, v_b_conv_b, v_b_ln_g, v_b_ln_b, v_mq_g, v_mk_g,
                          v_w_mem_kv, v_w_out, v_norm2_g, v_w_gate, v_w_up, v_w_down]))
    grads = dict(g_small)
    grads.update(a_w_in=g_a, b_w_in=g_b, w_mem_kv=g_kv, w_out=g_wo, w_gate=g_gate, w_up=g_up, w_down=g_wd)
    big_names = ["a_w_in", "b_w_in", "w_mem_kv", "w_out", "w_gate", "w_up", "w_down"]
    small_names = [n for n in names if n not in big_names]
    delta, new_m, new_v = {}, {}, {}
    for n in big_names:
        if n in ("b_w_in", "w_gate", "w_up"):
            outs = _adamw_big(tr(weights[n]), grads[n], tr(ms[n]), tr(vs[n]), f"adamw_{n}")
            delta[n], new_m[n], new_v[n] = [tr(o) for o in outs]
            grads[n] = tr(grads[n])
        else:
            delta[n], new_m[n], new_v[n] = _adamw_big(weights[n], grads[n], ms[n], vs[n], f"adamw_{n}")
    as2d = lambda a: a.reshape(-1, a.shape[-1])
    d_s, m_s, v_s = _adamw_small([as2d(weights[n]) for n in small_names], [as2d(grads[n]) for n in small_names],
                                 [as2d(ms[n]) for n in small_names], [as2d(vs[n]) for n in small_names])
    for i, n in enumerate(small_names):
        delta[n] = d_s[i].reshape(weights[n].shape)
        new_m[n] = m_s[i].reshape(weights[n].shape)
        new_v[n] = v_s[i].reshape(weights[n].shape)

    return (loss, grad_x.reshape(NB, SEQ, D), *[grads[n] for n in names], *[delta[n] for n in names],
            *[new_m[n] for n in names], *[new_v[n] for n in names])


def _rowsum(a):
    def body(a_ref, o_ref):
        o_ref[...] = jnp.sum(a_ref[...], axis=1, keepdims=True)

    vm = BS(memory_space=pltpu.VMEM)
    return pl.pallas_call(body, in_specs=[vm], out_specs=vm, out_shape=_sds((a.shape[0], 1), F32),
                          compiler_params=_cp(), name="rowsum")(a)
```

```python
import functools

import jax
import jax.numpy as jnp
from jax import lax
from jax.experimental import pallas as pl
from jax.experimental.pallas import tpu as pltpu

F32 = jnp.float32
BF16 = jnp.bfloat16
BS = pl.BlockSpec
ANY = pl.BlockSpec(memory_space=pl.ANY)
MESH = pl.DeviceIdType.MESH

D = 1024
SEQ = 2048
NB = 2
T = NB * SEQ
MEMT = 256
HD = 64
TOK = 768
MEMW = 256
NA = 3 * TOK + MEMW
NBW = 2 * TOK + MEMW
FF = 2816
NSH = 4
FS = FF // NSH
FT = FF // 2
CONVW = 31
EPS = 1e-6
NEG = -1e30
SCALE = HD ** -0.5
QB = 256
KWIN = 768
KPAD = 512
TR = 512

ADAM_LR = 0.001
ADAM_B1 = 0.9
ADAM_B2 = 0.999
ADAM_EPS = 1e-08
ADAM_WD = 0.01
ADAM_STEP = 10

_DIMS = {
    "nn": (((1,), (0,)), ((), ())),
    "nt": (((1,), (1,)), ((), ())),
    "tn": (((0,), (0,)), ((), ())),
}


def _cp(sem=None, vmem_mb=48):
    return pltpu.CompilerParams(dimension_semantics=sem, vmem_limit_bytes=vmem_mb << 20)


def _sds(shape, dtype):
    return jax.ShapeDtypeStruct(tuple(shape), dtype)


def _mm(mode, a, b, *, grid, a_spec, b_spec, out_shape, out_spec, acc_shape, name,
        extras=(), extra_specs=(), epilogue=None, carry=None, vmem_mb=48, sequential=False):
    n_ex = len(extras)
    nk = grid[2]
    dims = _DIMS[mode]
    ni = len(carry.ins) if carry else 0
    no = len(carry.out_shapes) if carry else 0
    multi = isinstance(out_shape, (list, tuple))
    out_shapes = list(out_shape) if multi else [out_shape]
    out_specs = list(out_spec) if multi else [out_spec]
    n_o = len(out_shapes)

    def body(a_ref, b_ref, *rest):
        ex = rest[:n_ex]
        cin = rest[n_ex:n_ex + ni]
        o_refs = rest[n_ex + ni:n_ex + ni + n_o]
        o_ref = o_refs if multi else o_refs[0]
        cout = rest[n_ex + ni + n_o:n_ex + ni + n_o + no]
        acc = rest[n_ex + ni + n_o + no]
        cscr = rest[n_ex + ni + n_o + no + 1:]
        ids = [pl.program_id(d) for d in range(3)]
        k = ids[2]
        if carry:
            @pl.when((ids[0] == 0) & (ids[1] == 0) & (ids[2] == 0))
            def _():
                carry.start(cin, cout, cscr)

        prod = lax.dot_general(a_ref[...].astype(BF16), b_ref[...].astype(BF16), dims,
                               preferred_element_type=F32)

        def finish(val):
            if epilogue is None:
                o_ref[...] = val.astype(o_ref.dtype)
            else:
                epilogue(val, ex, o_ref, ids[0])

        if nk == 1:
            finish(prod)
        else:
            @pl.when(k == 0)
            def _():
                acc[...] = prod

            @pl.when((k > 0) & (k < nk - 1))
            def _():
                acc[...] += prod

            @pl.when(k == nk - 1)
            def _():
                finish(acc[...] + prod)

        if carry:
            @pl.when((ids[0] == grid[0] - 1) & (ids[1] == grid[1] - 1) & (ids[2] == grid[2] - 1))
            def _():
                carry.finish(cin, cout, cscr)

    acc_scratch = pltpu.VMEM(acc_shape if nk > 1 else (8, 128), F32)
    ordered = sequential or bool(carry)
    outs = pl.pallas_call(
        body, grid=grid,
        in_specs=[a_spec, b_spec, *extra_specs] + (carry.in_specs if carry else []),
        out_specs=out_specs + [ANY] * no, out_shape=out_shapes + (carry.out_shapes if carry else []),
        scratch_shapes=[acc_scratch] + (carry.scratch if carry else []),
        compiler_params=pltpu.CompilerParams(
            dimension_semantics=("arbitrary",) * 3 if ordered else ("parallel", "parallel", "arbitrary"),
            vmem_limit_bytes=vmem_mb << 20, has_side_effects=bool(carry)), name=name,
    )(a, b, *extras, *(carry.ins if carry else []))
    mine = list(outs[:n_o]) if multi else outs[0]
    return (mine, outs[n_o:]) if carry else mine


def _rms_fwd(x, g, name, carry=None):
    rows = x.shape[0]
    ni = len(carry.ins) if carry else 0
    no = len(carry.out_shapes) if carry else 0

    def body(x_ref, g_ref, *rest):
        cin, o_ref, cout, cscr = rest[:ni], rest[ni], rest[ni + 1:ni + 1 + no], rest[ni + 1 + no:]
        if carry:
            @pl.when(pl.program_id(0) == 0)
            def _():
                carry.start(cin, cout, cscr)

        xv = x_ref[...]
        r = lax.rsqrt(jnp.mean(xv * xv, axis=-1, keepdims=True) + EPS)
        o_ref[...] = (xv * r * g_ref[...]).astype(BF16)

        if carry:
            @pl.when(pl.program_id(0) == rows // TR - 1)
            def _():
                carry.finish(cin, cout, cscr)

    outs = pl.pallas_call(
        body, grid=(rows // TR,),
        in_specs=[BS((TR, D), lambda i: (i, 0)), BS((1, D), lambda i: (0, 0))]
        + (carry.in_specs if carry else []),
        out_specs=[BS((TR, D), lambda i: (i, 0))] + [ANY] * no,
        out_shape=[_sds((rows, D), BF16)] + (carry.out_shapes if carry else []),
        scratch_shapes=carry.scratch if carry else [],
        compiler_params=pltpu.CompilerParams(
            dimension_semantics=("arbitrary",), vmem_limit_bytes=48 << 20,
            has_side_effects=bool(carry)), name=name,
    )(x, g, *(carry.ins if carry else []))
    return (outs[0], outs[1:]) if carry else outs[0]


def _rms_bwd(dh, x, g, dres, name):
    rows = x.shape[0]
    has_res = dres is not None

    def body(*refs):
        if has_res:
            dh_ref, x_ref, g_ref, r_ref, dx_ref, dg_ref = refs
        else:
            dh_ref, x_ref, g_ref, dx_ref, dg_ref = refs
        xv = x_ref[...]
        dhv = dh_ref[...]
        r = lax.rsqrt(jnp.mean(xv * xv, axis=-1, keepdims=True) + EPS)
        xh = xv * r
        gy = dhv * g_ref[...]
        dx = r * (gy - xh * jnp.mean(gy * xh, axis=-1, keepdims=True))
        if has_res:
            dx = dx + r_ref[...]
        dx_ref[...] = dx

        @pl.when(pl.program_id(0) == 0)
        def _():
            dg_ref[...] = jnp.zeros_like(dg_ref)

        dg_ref[...] += jnp.sum(dhv * xh, axis=0, keepdims=True)

    row = BS((TR, D), lambda i: (i, 0))
    vec = BS((1, D), lambda i: (0, 0))
    ins = [dh, x, g] + ([dres] if has_res else [])
    return pl.pallas_call(
        body, grid=(rows // TR,),
        in_specs=[row, row, vec] + ([row] if has_res else []),
        out_specs=[row, vec], out_shape=[_sds((rows, D), F32), _sds((1, D), F32)],
        compiler_params=_cp(("arbitrary",)), name=name,
    )(*ins)


def _group_masks(width):
    lane = lax.broadcasted_iota(jnp.int32, (1, width), 1)
    return [(lane >= HD * h) & (lane < HD * (h + 1)) for h in range(width // HD)]


def _group_mean(v, masks):
    del masks
    low = lax.broadcasted_iota(jnp.int32, (1, 128), 1) < HD
    slabs = []
    for j in range(v.shape[-1] // 128):
        x = v[:, 128 * j:128 * (j + 1)]
        s0 = jnp.sum(jnp.where(low, x, 0.0), axis=-1, keepdims=True) * (1.0 / HD)
        s1 = jnp.sum(jnp.where(low, 0.0, x), axis=-1, keepdims=True) * (1.0 / HD)
        slabs.append(jnp.where(low, s0, s1))
    return slabs[0] if len(slabs) == 1 else jnp.concatenate(slabs, axis=-1)


def _head_norm(zv, g, masks):
    r = lax.rsqrt(_group_mean(zv * zv, masks) + EPS)
    return zv * r * g


def _head_norm_bwd(dy, zv, g, masks):
    r = lax.rsqrt(_group_mean(zv * zv, masks) + EPS)
    zh = zv * r
    gy = dy * g
    dz = r * (gy - zh * _group_mean(gy * zh, masks))
    return dz, jnp.sum(dy * zh, axis=0, keepdims=True)


def _fold_heads(v, width):
    vb = jnp.broadcast_to(v, (8, width))
    out = vb
    for h in range(1, width // HD):
        out = out + pltpu.roll(vb, width - HD * h, axis=1)
    return out[0:1]


def _bias_expand(u):
    def body(u_ref, o_ref):
        x = jnp.broadcast_to(u_ref[...], (QB, 1024))
        rolled = pltpu.roll(x, 1024 - (QB - 1), axis=1, stride=1, stride_axis=0)[:, :KWIN]
        row = lax.broadcasted_iota(jnp.int32, (QB, 1), 0)
        col = lax.broadcasted_iota(jnp.int32, (1, KWIN), 1)
        lo = (row // 64) * 64
        ok = (col >= lo) & (col < lo + 576)
        o_ref[...] = jnp.where(ok, rolled, NEG)

    return pl.pallas_call(
        body, grid=(12,), in_specs=[BS((None, 1, 1024), lambda h: (h, 0, 0))],
        out_specs=BS((None, QB, KWIN), lambda h: (h, 0, 0)), out_shape=_sds((12, QB, KWIN), F32),
        compiler_params=_cp(("arbitrary",)), name="bias_expand",
    )(u)


def _bias_reduce(ds):
    def body(d_ref, o_ref):
        ri = lax.broadcasted_iota(jnp.int32, (QB, QB), 0)
        ci = lax.broadcasted_iota(jnp.int32, (QB, QB), 1)
        flip = (ri + ci == QB - 1).astype(F32)
        drev = jnp.dot(flip, d_ref[...], precision=lax.Precision.HIGHEST, preferred_element_type=F32)
        x = jnp.concatenate([drev, jnp.zeros((QB, 1024 - KWIN), F32)], axis=1)
        rolled = pltpu.roll(x, 0, axis=1, stride=1, stride_axis=0)
        o_ref[...] = jnp.sum(rolled, axis=0, keepdims=True)

    return pl.pallas_call(
        body, grid=(12,), in_specs=[BS((None, QB, KWIN), lambda h: (h, 0, 0))],
        out_specs=BS((None, 1, 1024), lambda h: (h, 0, 0)), out_shape=_sds((12, 1, 1024), F32),
        compiler_params=_cp(("arbitrary",)), name="bias_reduce",
    )(ds)


AW = 256
AH = AW // HD
AG = TOK // AW
def _attn_softmax(qh, kw, bias, startadd):
    s = lax.dot_general(qh, kw, _DIMS["nt"], preferred_element_type=F32) + bias + startadd
    m = jnp.max(s, axis=-1, keepdims=True)
    p = jnp.exp(s - m)
    return p * (1.0 / jnp.sum(p, axis=-1, keepdims=True))


def _attn_prologue(q_ref, k_ref, v_ref, gq_ref, gk_ref, qn_s, kn_s, v_s, masks):
    kn_s[0:KPAD, :] = jnp.zeros((KPAD, AW), BF16)
    v_s[0:KPAD, :] = jnp.zeros((KPAD, AW), BF16)
    for r in range(0, SEQ, TR):
        qn_s[r:r + TR, :] = (_head_norm(q_ref[r:r + TR, :], gq_ref[...], masks) * SCALE).astype(BF16)
        kn_s[KPAD + r:KPAD + r + TR, :] = _head_norm(k_ref[r:r + TR, :], gk_ref[...], masks).astype(BF16)
        v_s[KPAD + r:KPAD + r + TR, :] = v_ref[r:r + TR, :].astype(BF16)


def _attn_fwd(z, bias, gq2, gk2, carry=None):
    ni = len(carry.ins) if carry else 0
    no = len(carry.out_shapes) if carry else 0

    def body(q_ref, k_ref, v_ref, b_ref, gq_ref, gk_ref, *rest):
        cin, o_ref, cout = rest[:ni], rest[ni], rest[ni + 1:ni + 1 + no]
        qn_s, kn_s, v_s = rest[ni + 1 + no:ni + 4 + no]
        cscr = rest[ni + 4 + no:]
        if carry:
            @pl.when((pl.program_id(0) == 0) & (pl.program_id(1) == 0))
            def _():
                carry.start(cin, cout, cscr)

        masks = _group_masks(AW)
        _attn_prologue(q_ref, k_ref, v_ref, gq_ref, gk_ref, qn_s, kn_s, v_s, masks)
        col = lax.broadcasted_iota(jnp.int32, (1, KWIN), 1)

        def blk(i, carry):
            r0 = pl.multiple_of(i * QB, QB)
            qb = qn_s[pl.ds(r0, QB), :]
            kw = kn_s[pl.ds(r0, KWIN), :]
            vw = v_s[pl.ds(r0, KWIN), :]
            startadd = jnp.where(col + r0 < KPAD, NEG, 0.0)
            o = jnp.zeros((QB, AW), F32)
            for h in range(AH):
                qh = jnp.where(masks[h], qb, jnp.zeros_like(qb))
                vh = jnp.where(masks[h], vw, jnp.zeros_like(vw))
                p = _attn_softmax(qh, kw, b_ref[h], startadd).astype(BF16)
                o = o + jnp.dot(p, vh, preferred_element_type=F32)
            o_ref[pl.ds(r0, QB), :] = o.astype(BF16)
            return carry

        lax.fori_loop(0, SEQ // QB, blk, 0)

        if carry:
            @pl.when((pl.program_id(0) == NB - 1) & (pl.program_id(1) == AG - 1))
            def _():
                carry.finish(cin, cout, cscr)

    vec = BS((1, AW), lambda b, hp: (0, 0))
    outs = pl.pallas_call(
        body, grid=(NB, AG),
        in_specs=[BS((SEQ, AW), lambda b, hp: (b, hp)),
                  BS((SEQ, AW), lambda b, hp: (b, AG + hp)),
                  BS((SEQ, AW), lambda b, hp: (b, 2 * AG + hp)),
                  BS((AH, QB, KWIN), lambda b, hp: (hp, 0, 0)), vec, vec]
        + (carry.in_specs if carry else []),
        out_specs=[BS((SEQ, AW), lambda b, hp: (b, hp))] + [ANY] * no,
        out_shape=[_sds((T, D), BF16)] + (carry.out_shapes if carry else []),
        scratch_shapes=[pltpu.VMEM((SEQ, AW), BF16), pltpu.VMEM((SEQ + KPAD, AW), BF16),
                        pltpu.VMEM((SEQ + KPAD, AW), BF16)] + (carry.scratch if carry else []),
        compiler_params=pltpu.CompilerParams(
            dimension_semantics=("arbitrary", "arbitrary"), vmem_limit_bytes=48 << 20,
            has_side_effects=bool(carry)), name="attn_fwd",
    )(z, z, z, bias, gq2, gk2, *(carry.ins if carry else []))
    return outs[0], outs[1:]


def _attn_bwd(z, dcat, dz_into, bias, gq2, gk2, carry=None):
    ni = len(carry.ins) if carry else 0
    no = len(carry.out_shapes) if carry else 0

    def body(q_ref, k_ref, v_ref, do_ref, b_ref, gq_ref, gk_ref, into_ref, *rest):
        del into_ref
        cin = rest[:ni]
        dz_ref, db_ref, dgq_ref, dgk_ref = rest[ni:ni + 4]
        cout = rest[ni + 4:ni + 4 + no]
        qn_s, kn_s, v_s, dqn_s, dkn_s, dv_s, out_s, out_sem = rest[ni + 4 + no:ni + 12 + no]
        cscr = rest[ni + 12 + no:]
        hp = pl.program_id(0)
        b = pl.program_id(1)
        step = hp * NB + b

        def out_copy(j, step):
            rows = pl.ds(pl.multiple_of((step % NB) * SEQ, SEQ), SEQ)
            cols = pl.ds(pl.multiple_of((j * AG + step // NB) * AW, AW), AW)
            return pltpu.make_async_copy(out_s.at[j], dz_ref.at[rows, cols], out_sem.at[j])

        if carry:
            @pl.when((hp == 0) & (b == 0))
            def _():
                carry.start(cin, cout, cscr)

        masks = _group_masks(AW)
        _attn_prologue(q_ref, k_ref, v_ref, gq_ref, gk_ref, qn_s, kn_s, v_s, masks)
        dkn_s[...] = jnp.zeros_like(dkn_s)
        dv_s[...] = jnp.zeros_like(dv_s)

        @pl.when(b == 0)
        def _():
            db_ref[...] = jnp.zeros_like(db_ref)

        @pl.when((b == 0) & (hp == 0))
        def _():
            dgq_ref[...] = jnp.zeros_like(dgq_ref)
            dgk_ref[...] = jnp.zeros_like(dgk_ref)

        col = lax.broadcasted_iota(jnp.int32, (1, KWIN), 1)

        def blk(i, carry):
            r0 = pl.multiple_of(i * QB, QB)
            qb = qn_s[pl.ds(r0, QB), :]
            kw = kn_s[pl.ds(r0, KWIN), :]
            vw = v_s[pl.ds(r0, KWIN), :]
            dob = do_ref[pl.ds(r0, QB), :].astype(BF16)
            startadd = jnp.where(col + r0 < KPAD, NEG, 0.0)
            dqn = jnp.zeros((QB, AW), F32)
            dkw = jnp.zeros((KWIN, AW), F32)
            dvw = jnp.zeros((KWIN, AW), F32)
            for h in range(AH):
                qh = jnp.where(masks[h], qb, jnp.zeros_like(qb))
                kh = jnp.where(masks[h], kw, jnp.zeros_like(kw))
                doh = jnp.where(masks[h], dob, jnp.zeros_like(dob))
                p = _attn_softmax(qh, kw, b_ref[h], startadd)
                dvw = dvw + lax.dot_general(p.astype(BF16), doh, _DIMS["tn"],
                                            preferred_element_type=F32)
                dp = lax.dot_general(doh, vw, _DIMS["nt"], preferred_element_type=F32)
                ds = p * (dp - jnp.sum(dp * p, axis=-1, keepdims=True))
                db_ref[h] += ds
                dsb = ds.astype(BF16)
                dqn = dqn + jnp.dot(dsb, kh, preferred_element_type=F32)
                dkw = dkw + lax.dot_general(dsb, qh, _DIMS["tn"], preferred_element_type=F32)
            dqn_s[pl.ds(r0, QB), :] = dqn * SCALE
            dkn_s[pl.ds(r0, KWIN), :] += dkw
            dv_s[pl.ds(r0, KWIN), :] += dvw
            return carry

        lax.fori_loop(0, SEQ // QB, blk, 0)

        @pl.when(step > 0)
        def _():
            for j in range(3):
                out_copy(j, step - 1).wait()

        dgq = jnp.zeros((1, AW), F32)
        dgk = jnp.zeros((1, AW), F32)
        for r in range(0, SEQ, TR):
            dq, dg = _head_norm_bwd(dqn_s[r:r + TR, :], q_ref[r:r + TR, :], gq_ref[...], masks)
            out_s[0, r:r + TR, :] = dq.astype(BF16)
            dgq = dgq + dg
            dk, dg = _head_norm_bwd(dkn_s[KPAD + r:KPAD + r + TR, :], k_ref[r:r + TR, :], gk_ref[...], masks)
            out_s[1, r:r + TR, :] = dk.astype(BF16)
            dgk = dgk + dg
            out_s[2, r:r + TR, :] = dv_s[KPAD + r:KPAD + r + TR, :].astype(BF16)
        dgq_ref[...] += _fold_heads(dgq, AW)
        dgk_ref[...] += _fold_heads(dgk, AW)
        for j in range(3):
            out_copy(j, step).start()

        @pl.when(step == AG * NB - 1)
        def _():
            for j in range(3):
                out_copy(j, step).wait()

        if carry:
            @pl.when((hp == AG - 1) & (b == NB - 1))
            def _():
                carry.finish(cin, cout, cscr)

    vec = BS((1, AW), lambda hp, b: (0, 0))
    row = BS((SEQ, AW), lambda hp, b: (b, hp))
    outs = pl.pallas_call(
        body, grid=(AG, NB),
        in_specs=[row,
                  BS((SEQ, AW), lambda hp, b: (b, AG + hp)),
                  BS((SEQ, AW), lambda hp, b: (b, 2 * AG + hp)),
                  row,
                  BS((AH, QB, KWIN), lambda hp, b: (hp, 0, 0)), vec, vec, ANY]
        + (carry.in_specs if carry else []),
        out_specs=[ANY, BS((AH, QB, KWIN), lambda hp, b: (hp, 0, 0)), vec, vec] + [ANY] * no,
        out_shape=[_sds((T, NA), BF16), _sds((12, QB, KWIN), F32), _sds((1, AW), F32), _sds((1, AW), F32)]
        + (carry.out_shapes if carry else []),
        input_output_aliases={7: 0},
        scratch_shapes=[pltpu.VMEM((SEQ, AW), BF16), pltpu.VMEM((SEQ + KPAD, AW), BF16),
                        pltpu.VMEM((SEQ + KPAD, AW), BF16), pltpu.VMEM((SEQ, AW), F32),
                        pltpu.VMEM((SEQ + KPAD, AW), F32), pltpu.VMEM((SEQ + KPAD, AW), F32),
                        pltpu.VMEM((3, SEQ, AW), BF16), pltpu.SemaphoreType.DMA((3,))]
        + (carry.scratch if carry else []),
        compiler_params=pltpu.CompilerParams(
            dimension_semantics=("arbitrary", "arbitrary"), vmem_limit_bytes=58 << 20,
            has_side_effects=bool(carry)), name="attn_bwd",
    )(z, z, z, dcat, bias, gq2, gk2, dz_into, *(carry.ins if carry else []))
    return outs[:4], outs[4:]


def _mem_softmax(qh, kn):
    s = lax.dot_general(qh, kn, _DIMS["nt"], preferred_element_type=F32)
    m = jnp.max(s, axis=-1, keepdims=True)
    p = jnp.exp(s - m)
    return p * (1.0 / jnp.sum(p, axis=-1, keepdims=True))


def _memattn_fwd(z, kv, cat, gq4, gk4, qcol, name):
    def body(q_ref, k_ref, v_ref, gq_ref, gk_ref, cat_ref, o_ref):
        del cat_ref
        masks = _group_masks(MEMW)
        qn = (_head_norm(q_ref[...], gq_ref[...], masks) * SCALE).astype(BF16)
        kn = _head_norm(k_ref[...], gk_ref[...], masks).astype(BF16)
        vv = v_ref[...].astype(BF16)
        o = jnp.zeros((TR, MEMW), F32)
        for h in range(4):
            qh = jnp.where(masks[h], qn, jnp.zeros_like(qn))
            vh = jnp.where(masks[h], vv, jnp.zeros_like(vv))
            p = _mem_softmax(qh, kn).astype(BF16)
            o = o + jnp.dot(p, vh, preferred_element_type=F32)
        o_ref[...] = o.astype(BF16)

    nt = SEQ // TR
    vec = BS((1, MEMW), lambda b, t: (0, 0))
    return pl.pallas_call(
        body, grid=(NB, nt),
        in_specs=[BS((TR, MEMW), lambda b, t: (b * nt + t, qcol)),
                  BS((MEMT, MEMW), lambda b, t: (b, 0)),
                  BS((MEMT, MEMW), lambda b, t: (b, 1)), vec, vec, ANY],
        out_specs=BS((TR, MEMW), lambda b, t: (b * nt + t, 3)),
        out_shape=_sds((T, D), BF16), input_output_aliases={5: 0},
        compiler_params=_cp(("arbitrary", "arbitrary")), name=name,
    )(z, kv, kv, gq4, gk4, cat)


def _memattn_bwd(z, kv, dcat, gq4, gk4, qcol, name, dz_cols=MEMW):
    nt = SEQ // TR

    def body(q_ref, k_ref, v_ref, do_ref, gq_ref, gk_ref,
             dq_ref, dkv_ref, dgq_ref, dgk_ref, dkn_s, dv_s):
        b = pl.program_id(0)
        t = pl.program_id(1)
        masks = _group_masks(MEMW)
        qz = q_ref[...]
        kz = k_ref[...]
        qn = (_head_norm(qz, gq_ref[...], masks) * SCALE).astype(BF16)
        kn = _head_norm(kz, gk_ref[...], masks).astype(BF16)
        vv = v_ref[...].astype(BF16)
        dob = do_ref[...].astype(BF16)

        @pl.when(t == 0)
        def _():
            dkn_s[...] = jnp.zeros_like(dkn_s)
            dv_s[...] = jnp.zeros_like(dv_s)

        @pl.when((t == 0) & (b == 0))
        def _():
            dgq_ref[...] = jnp.zeros_like(dgq_ref)
            dgk_ref[...] = jnp.zeros_like(dgk_ref)

        dqn = jnp.zeros((TR, MEMW), F32)
        dkn = jnp.zeros((MEMT, MEMW), F32)
        dvv = jnp.zeros((MEMT, MEMW), F32)
        for h in range(4):
            qh = jnp.where(masks[h], qn, jnp.zeros_like(qn))
            kh = jnp.where(masks[h], kn, jnp.zeros_like(kn))
            doh = jnp.where(masks[h], dob, jnp.zeros_like(dob))
            p = _mem_softmax(qh, kn)
            dvv = dvv + lax.dot_general(p.astype(BF16), doh, _DIMS["tn"], preferred_element_type=F32)
            dp = lax.dot_general(doh, vv, _DIMS["nt"], preferred_element_type=F32)
            ds = p * (dp - jnp.sum(dp * p, axis=-1, keepdims=True))
            dsb = ds.astype(BF16)
            dqn = dqn + jnp.dot(dsb, kh, preferred_element_type=F32)
            dkn = dkn + lax.dot_general(dsb, qh, _DIMS["tn"], preferred_element_type=F32)
        dkn_s[...] += dkn
        dv_s[...] += dvv
        dq, dgq = _head_norm_bwd(dqn * SCALE, qz, gq_ref[...], masks)
        dq_ref[...] = dq.astype(BF16)
        dgq_ref[...] += _fold_heads(dgq, MEMW)

        @pl.when(t == nt - 1)
        def _():
            dk, dgk = _head_norm_bwd(dkn_s[...], kz, gk_ref[...], masks)
            dkv_ref[:, 0:MEMW] = dk
            dkv_ref[:, MEMW:] = dv_s[...]
            dgk_ref[...] += _fold_heads(dgk, MEMW)

    vec = BS((1, MEMW), lambda b, t: (0, 0))
    return pl.pallas_call(
        body, grid=(NB, nt),
        in_specs=[BS((TR, MEMW), lambda b, t: (b * nt + t, qcol)),
                  BS((MEMT, MEMW), lambda b, t: (b, 0)),
                  BS((MEMT, MEMW), lambda b, t: (b, 1)),
                  BS((TR, MEMW), lambda b, t: (b * nt + t, 3)), vec, vec],
        out_specs=[BS((TR, MEMW), lambda b, t: (b * nt + t, dz_cols // MEMW - 1)),
                   BS((MEMT, 2 * MEMW), lambda b, t: (b, 0)), vec, vec],
        out_shape=[_sds((T, dz_cols), BF16), _sds((NB * MEMT, 2 * MEMW), F32),
                   _sds((1, MEMW), F32), _sds((1, MEMW), F32)],
        scratch_shapes=[pltpu.VMEM((MEMT, MEMW), F32), pltpu.VMEM((MEMT, MEMW), F32)],
        compiler_params=_cp(("arbitrary", "arbitrary")), name=name,
    )(z, kv, kv, dcat, gq4, gk4)


HALO = 32
NEXT = 64
RT = 64


def _glu(zz):
    return zz[:, :TOK] * jax.nn.sigmoid(zz[:, TOK:])


def _layer_norm_parts(y):
    mu = jnp.mean(y, axis=-1, keepdims=True)
    yc = y - mu
    rstd = lax.rsqrt(jnp.mean(yc * yc, axis=-1, keepdims=True) + EPS)
    return yc * rstd, rstd


def _shifted_copies(src, dst, rows):
    for b in range(1, 8):
        dst[b - 1, 0:rows, :] = src[b:b + rows, :]


def _tap(src, shifted, off, r0, rows):
    b = off % 8
    if b == 0:
        return src[r0 + off:r0 + off + rows, :]
    return shifted[b - 1, r0 + off - b:r0 + off - b + rows, :]


def _conv_rows(w_ref, hbuf, hs, r0, rows):
    y = jnp.zeros((rows, TOK), F32)
    for j in range(CONVW):
        y = y + w_ref[j:j + 1, :] * _tap(hbuf, hs, (HALO - CONVW + 1) + j, r0, rows)
    return y


def _conv_fwd(z, cw, cb, lg, lb):
    nt = SEQ // TR

    def body(zc_ref, zp_ref, w_ref, cb_ref, lg_ref, lb_ref, o_ref, y_ref, hbuf, hs):
        t = pl.program_id(1)
        hbuf[0:HALO, :] = jnp.where(t == 0, 0.0, _glu(zp_ref[...]))
        hbuf[HALO:, :] = _glu(zc_ref[...])
        _shifted_copies(hbuf, hs, HALO + TR - 8)
        for r0 in range(0, TR, RT):
            y = _conv_rows(w_ref, hbuf, hs, r0, RT) + cb_ref[...]
            y_ref[r0:r0 + RT, :] = y
            yh, _ = _layer_norm_parts(y)
            o = yh * lg_ref[...] + lb_ref[...]
            o_ref[r0:r0 + RT, :] = (o * jax.nn.sigmoid(o)).astype(BF16)

    vec = BS((1, TOK), lambda b, t: (0, 0))
    per = TR // HALO
    return pl.pallas_call(
        body, grid=(NB, nt),
        in_specs=[BS((TR, 2 * TOK), lambda b, t: (b * nt + t, 0)),
                  BS((HALO, 2 * TOK), lambda b, t: (jnp.maximum((b * nt + t) * per - 1, 0), 0)),
                  BS((32, TOK), lambda b, t: (0, 0)), vec, vec, vec],
        out_specs=[BS((TR, TOK), lambda b, t: (b * nt + t, 0)), BS((TR, TOK), lambda b, t: (b * nt + t, 0))],
        out_shape=[_sds((T, D), BF16), _sds((T, TOK), F32)],
        scratch_shapes=[pltpu.VMEM((HALO + TR, TOK), F32), pltpu.VMEM((7, HALO + TR, TOK), F32)],
        compiler_params=_cp(("arbitrary", "arbitrary")), name="conv_fwd",
    )(z, z, cw, cb, lg, lb)


def _conv_bwd(z, y, dcat, cw, lg, lb):
    nt = SEQ // TR
    ext = TR + NEXT

    def body(zc_ref, zp_ref, yc_ref, yn_ref, dc_ref, dn_ref, w_ref, lg_ref, lb_ref,
             du_ref, dw_ref, dcb_ref, dlg_ref, dlb_ref, dbin_ref, hbuf, dybuf, hs, dys):
        b = pl.program_id(0)
        t = pl.program_id(1)

        @pl.when((b == 0) & (t == 0))
        def _():
            dw_ref[...] = jnp.zeros_like(dw_ref)
            dcb_ref[...] = jnp.zeros_like(dcb_ref)
            dlg_ref[...] = jnp.zeros_like(dlg_ref)
            dlb_ref[...] = jnp.zeros_like(dlb_ref)
            dbin_ref[...] = jnp.zeros_like(dbin_ref)

        hbuf[0:HALO, :] = jnp.where(t == 0, 0.0, _glu(zp_ref[...]))
        hbuf[HALO:, :] = _glu(zc_ref[...])
        _shifted_copies(hbuf, hs, HALO + TR - 8)
        last = t == nt - 1
        for r0 in range(0, ext, RT):
            yh, rstd = _layer_norm_parts(yc_ref[r0:r0 + RT, :] if r0 < TR else yn_ref[...])
            o = yh * lg_ref[...] + lb_ref[...]
            sg = jax.nn.sigmoid(o)
            if r0 < TR:
                dtok = dc_ref[r0:r0 + RT, :]
            else:
                dtok = jnp.where(last, 0.0, dn_ref[...])
            do = dtok * (sg * (1.0 + o * (1.0 - sg)))
            dyh = do * lg_ref[...]
            dy = rstd * (dyh - jnp.mean(dyh, axis=-1, keepdims=True)
                         - yh * jnp.mean(dyh * yh, axis=-1, keepdims=True))
            dybuf[r0:r0 + RT, :] = dy
            if r0 < TR:
                dlg_ref[...] += jnp.sum(do * yh, axis=0, keepdims=True)
                dlb_ref[...] += jnp.sum(do, axis=0, keepdims=True)
                dcb_ref[...] += jnp.sum(dy, axis=0, keepdims=True)
        _shifted_copies(dybuf, dys, ext - 8)
        for r0 in range(0, TR, RT):
            dh = jnp.zeros((RT, TOK), F32)
            for j in range(CONVW):
                dh = dh + w_ref[j:j + 1, :] * _tap(dybuf, dys, (CONVW - 1) - j, r0, RT)
            a = zc_ref[r0:r0 + RT, 0:TOK]
            sg = jax.nn.sigmoid(zc_ref[r0:r0 + RT, TOK:])
            da = dh * sg
            dg = dh * a * (sg * (1.0 - sg))
            du_ref[r0:r0 + RT, 0:TOK] = da.astype(BF16)
            du_ref[r0:r0 + RT, TOK:] = dg.astype(BF16)
            dbin_ref[:, 0:TOK] += jnp.sum(da, axis=0, keepdims=True)
            dbin_ref[:, TOK:] += jnp.sum(dg, axis=0, keepdims=True)
        for j in range(CONVW):
            acc = jnp.zeros((8, TOK), F32)
            for r0 in range(0, TR, RT):
                prod = dybuf[r0:r0 + RT, :] * _tap(hbuf, hs, (HALO - CONVW + 1) + j, r0, RT)
                acc = acc + jnp.sum(prod.reshape(RT // 8, 8, TOK), axis=0)
            dw_ref[j:j + 1, :] += jnp.sum(acc, axis=0, keepdims=True)

    vec = BS((1, TOK), lambda b, t: (0, 0))
    perh = TR // HALO
    pern = TR // NEXT
    nlast_n = T // NEXT - 1
    return pl.pallas_call(
        body, grid=(NB, nt),
        in_specs=[BS((TR, 2 * TOK), lambda b, t: (b * nt + t, 0)),
                  BS((HALO, 2 * TOK), lambda b, t: (jnp.maximum((b * nt + t) * perh - 1, 0), 0)),
                  BS((TR, TOK), lambda b, t: (b * nt + t, 0)),
                  BS((NEXT, TOK), lambda b, t: (jnp.minimum((b * nt + t + 1) * pern, nlast_n), 0)),
                  BS((TR, TOK), lambda b, t: (b * nt + t, 0)),
                  BS((NEXT, TOK), lambda b, t: (jnp.minimum((b * nt + t + 1) * pern, nlast_n), 0)),
                  BS((32, TOK), lambda b, t: (0, 0)), vec, vec],
        out_specs=[BS((TR, 2 * TOK), lambda b, t: (b * nt + t, 0)),
                   BS((32, TOK), lambda b, t: (0, 0)), vec, vec, vec,
                   BS((1, 2 * TOK), lambda b, t: (0, 0))],
        out_shape=[_sds((T, 2 * TOK), BF16), _sds((32, TOK), F32), _sds((1, TOK), F32),
                   _sds((1, TOK), F32), _sds((1, TOK), F32), _sds((1, 2 * TOK), F32)],
        scratch_shapes=[pltpu.VMEM((HALO + TR, TOK), F32), pltpu.VMEM((ext, TOK), F32),
                        pltpu.VMEM((7, HALO + TR, TOK), F32), pltpu.VMEM((7, ext, TOK), F32)],
        compiler_params=_cp(("arbitrary", "arbitrary"), vmem_mb=56), name="conv_bwd",
    )(z, z, y, y, dcat, dcat, cw, lg, lb)


def _ffn_up(h2, wgu, l, carry=None):
    ni = len(carry.ins) if carry else 0
    no = len(carry.out_shapes) if carry else 0

    def body(h_ref, wg_ref, wu_ref, *rest):
        cin = rest[:ni]
        g_ref, u_ref, a_ref = rest[ni:ni + 3]
        cout, cscr = rest[ni + 3:ni + 3 + no], rest[ni + 3 + no:]
        if carry:
            @pl.when((pl.program_id(0) == 0) & (pl.program_id(1) == 0))
            def _():
                carry.start(cin, cout, cscr)

        hv = h_ref[...]
        g = lax.dot_general(hv, wg_ref[...], _DIMS["nt"], preferred_element_type=F32)
        u = lax.dot_general(hv, wu_ref[...], _DIMS["nt"], preferred_element_type=F32)
        sg = jax.nn.sigmoid(g)
        silu = g * sg
        g_ref[...] = (u * (sg * (1.0 + g * (1.0 - sg)))).astype(BF16)
        u_ref[...] = silu.astype(BF16)
        a_ref[...] = (silu * u).astype(BF16)

        if carry:
            @pl.when((pl.program_id(0) == FF // FT - 1) & (pl.program_id(1) == T // TR - 1))
            def _():
                carry.finish(cin, cout, cscr)

    out = BS((TR, FT), lambda q, i: (i, q))
    outs = pl.pallas_call(
        body, grid=(FF // FT, T // TR),
        in_specs=[BS((TR, D), lambda q, i: (i, 0)),
                  BS((None, FT, D), lambda q, i: (0, q, 0)),
                  BS((None, FT, D), lambda q, i: (1, q, 0))] + (carry.in_specs if carry else []),
        out_specs=[out, out, out] + [ANY] * no,
        out_shape=[_sds((T, FF), BF16), _sds((T, FF), BF16), _sds((T, FF), BF16)]
        + (carry.out_shapes if carry else []),
        scratch_shapes=carry.scratch if carry else [],
        compiler_params=pltpu.CompilerParams(
            dimension_semantics=("arbitrary", "arbitrary"), vmem_limit_bytes=48 << 20,
            has_side_effects=bool(carry)), name=f"ffn_up_{l}",
    )(h2, wgu, wgu, *(carry.ins if carry else []))
    return outs[:3], outs[3:]


def _ffn_down_bwd(dx, wd, g, u, l):
    def epilogue(dact, ex, o_ref, i):
        o_ref[0] = (dact * ex[0][...].astype(F32)).astype(BF16)
        o_ref[1] = (dact * ex[1][...].astype(F32)).astype(BF16)

    ex_spec = BS((TR, FT), lambda i, q, k: (i, q))
    return _mm("nt", dx, wd, grid=(T // TR, FF // FT, 1),
               a_spec=BS((TR, D), lambda i, q, k: (i, 0)),
               b_spec=BS((FT, D), lambda i, q, k: (q, 0)),
               out_shape=_sds((2, T, FF), BF16),
               out_spec=BS((2, TR, FT), lambda i, q, k: (0, i, q)),
               acc_shape=(TR, FT), extras=(g, u), extra_specs=(ex_spec, ex_spec),
               epilogue=epilogue, name=f"ffn_down_bwd_{l}")


def _row_tile(rows, cols, itemsize=4, limit=2 << 20):
    tr = rows
    while tr * cols * itemsize > limit and tr % 2 == 0 and (tr // 2) % 16 == 0:
        tr //= 2
    return tr


def _cast_bf16(arrs, name):
    n = len(arrs)
    rows, cols = arrs[0].shape
    tr = _row_tile(rows, cols)

    def body(*refs):
        o_ref = refs[n]
        k = pl.program_id(0)
        val = refs[0][...]
        for j in range(1, n):
            val = jnp.where(k == j, refs[j][...], val)
        o_ref[...] = val.astype(BF16)

    return pl.pallas_call(
        body, grid=(n, rows // tr),
        in_specs=[BS((tr, cols), lambda k, i: (i, 0))] * n,
        out_specs=BS((None, tr, cols), lambda k, i: (k, i, 0)),
        out_shape=_sds((n, rows, cols), BF16),
        compiler_params=_cp(("arbitrary", "arbitrary")), name=name,
    )(*arrs)


def _quad_sum(own, got, name):
    n, rows, cols = own.shape
    tr = _row_tile(rows, cols)

    def body(a_ref, q_ref, o_ref):
        o_ref[...] = ((a_ref[...] + q_ref[0].astype(F32)) + q_ref[1].astype(F32)) + q_ref[2].astype(F32)

    spec = BS((None, tr, cols), lambda k, i: (k, i, 0))
    return pl.pallas_call(
        body, grid=(n, rows // tr),
        in_specs=[spec, BS((3, None, tr, cols), lambda k, i: (0, k, i, 0))], out_specs=spec,
        out_shape=_sds((n, rows, cols), F32),
        compiler_params=_cp(("arbitrary", "arbitrary")), name=name,
    )(own, got)


def _adam_math(w, g, m, v):
    m = ADAM_B1 * m + (1.0 - ADAM_B1) * g
    v = ADAM_B2 * v + (1.0 - ADAM_B2) * (g * g)
    m_hat = m / (1.0 - ADAM_B1 ** ADAM_STEP)
    v_hat = v / (1.0 - ADAM_B2 ** ADAM_STEP)
    delta = -ADAM_LR * (m_hat / (jnp.sqrt(v_hat) + ADAM_EPS) + ADAM_WD * w)
    return delta, m, v


def _adamw_big(w, g, m, v, name):
    shape = w.shape
    cols = shape[-1]
    rows = w.size // cols
    tr = _row_tile(rows, cols, limit=1 << 20)

    def body(w_ref, g_ref, m_ref, v_ref, d_ref, nm_ref, nv_ref):
        d, nm, nv = _adam_math(w_ref[...], g_ref[...], m_ref[...], v_ref[...])
        d_ref[...] = d
        nm_ref[...] = nm
        nv_ref[...] = nv

    spec = BS((tr, cols), lambda i: (i, 0))
    outs = pl.pallas_call(
        body, grid=(rows // tr,), in_specs=[spec] * 4, out_specs=[spec] * 3,
        out_shape=[_sds((rows, cols), F32)] * 3,
        compiler_params=_cp(("arbitrary",)), name=name,
    )(*[a.reshape(rows, cols) for a in (w, g, m, v)])
    return [o.reshape(shape) for o in outs]


def _adamw_small(ws, gs, ms, vs):
    n = len(ws)

    def body(*refs):
        for i in range(n):
            d, nm, nv = _adam_math(refs[i][...], refs[n + i][...], refs[2 * n + i][...],
                                   refs[3 * n + i][...])
            refs[4 * n + i][...] = d
            refs[5 * n + i][...] = nm
            refs[6 * n + i][...] = nv

    specs = [BS(w.shape, lambda i: (0, 0)) for w in ws]
    outs = pl.pallas_call(
        body, grid=(1,), in_specs=specs * 4, out_specs=specs * 3,
        out_shape=[_sds(w.shape, F32) for w in ws] * 3,
        compiler_params=_cp(("arbitrary",)), name="adamw_small",
    )(*ws, *gs, *ms, *vs)
    return outs[:n], outs[n:2 * n], outs[2 * n:]


def _place():
    x, y, c = lax.axis_index("x"), lax.axis_index("y"), lax.axis_index("c")
    chips = [(1 - x, y), (x, 1 - y), (1 - x, 1 - y)]
    return x, y, c, chips


class _Exchange:
    def __init__(self, ins, in_specs, out_shapes, scratch, start, finish):
        self.ins, self.in_specs, self.out_shapes, self.scratch = ins, in_specs, out_shapes, scratch
        self.start, self.finish = start, finish


def _run_exchange(ex, name, vmem_mb=40):
    ni, no = len(ex.ins), len(ex.out_shapes)

    def body(*refs):
        ex.start(refs[:ni], refs[ni:ni + no], refs[ni + no:])
        ex.finish(refs[:ni], refs[ni:ni + no], refs[ni + no:])

    return pl.pallas_call(
        body, in_specs=ex.in_specs, out_specs=[ANY] * no, out_shape=ex.out_shapes,
        scratch_shapes=ex.scratch,
        compiler_params=pltpu.CompilerParams(has_side_effects=True, vmem_limit_bytes=vmem_mb << 20),
        name=name,
    )(*ex.ins)


def _gather_exchange(srcs, dst_shapes, views, small=None):
    nu = len(srcs)
    nd = len(dst_shapes)
    ns = 1 if small is not None else 0

    def unpack(ins, outs, scr):
        x, y, c, chips = _place()
        src = ins[:nu]
        vw = [views[u](outs[:nd]) for u in range(nu)]
        vbuf = scr[:nu]
        send, recv, fsend, frecv, lsem, ssend, srecv, vsem = scr[nu:]

        def ici(u, j, shard, to):
            return pltpu.make_async_remote_copy(
                src_ref=vbuf[u].at[:, c], dst_ref=vw[u].at[:, shard, c],
                send_sem=send.at[3 * u + j], recv_sem=recv.at[3 * u + j],
                device_id=to, device_id_type=MESH)

        def fwd(u, j, shard, half):
            return pltpu.make_async_remote_copy(
                src_ref=vw[u].at[:, shard, half], dst_ref=vw[u].at[:, shard, half],
                send_sem=fsend.at[3 * u + j], recv_sem=frecv.at[3 * u + j],
                device_id=(x, y, 1 - c), device_id_type=MESH)

        def small_copy(j, shard, to):
            return pltpu.make_async_remote_copy(
                src_ref=ins[nu], dst_ref=outs[nd].at[shard],
                send_sem=ssend.at[j], recv_sem=srecv.at[j], device_id=to, device_id_type=MESH)

        stage = [pltpu.make_async_copy(src[u], vbuf[u], vsem.at[u]) for u in range(nu)]
        local = [pltpu.make_async_copy(vbuf[u], vw[u].at[:, 2 * x + y], lsem.at[u]) for u in range(nu)]
        if ns:
            local.append(pltpu.make_async_copy(ins[nu], outs[nd].at[2 * x + y], lsem.at[nu]))
        return x, y, c, chips, ici, fwd, small_copy, stage, local

    def start(ins, outs, scr):
        x, y, c, chips, ici, fwd, small_copy, stage, local = unpack(ins, outs, scr)
        s = 2 * x + y
        for cp in stage:
            cp.start()
        if ns:
            local[nu].start()
            for j, chip in enumerate(chips):
                small_copy(j, s, (*chip, c)).start()
        for u in range(nu):
            stage[u].wait()
            for j, chip in enumerate(chips):
                ici(u, j, s, (*chip, c)).start()
            local[u].start()

    def finish(ins, outs, scr):
        x, y, c, chips, ici, fwd, small_copy, stage, local = unpack(ins, outs, scr)
        s = 2 * x + y
        for u in range(nu):
            for j, chip in enumerate(chips):
                sj = 2 * chip[0] + chip[1]
                ici(u, j, sj, (x, y, c)).wait_recv()
                fwd(u, j, sj, c).start()
        for u in range(nu):
            for j, chip in enumerate(chips):
                sj = 2 * chip[0] + chip[1]
                fwd(u, j, sj, 1 - c).wait_recv()
        for u in range(nu):
            for j, chip in enumerate(chips):
                ici(u, j, s, (*chip, c)).wait_send()
                fwd(u, j, s, c).wait_send()
        if ns:
            for j, chip in enumerate(chips):
                small_copy(j, 2 * chip[0] + chip[1], (x, y, c)).wait_recv()
                small_copy(j, s, (*chip, c)).wait_send()
        for cp in local:
            cp.wait()

    dma = pltpu.SemaphoreType.DMA
    return _Exchange(
        ins=list(srcs) + ([small] if ns else []),
        in_specs=[ANY] * nu + [BS(memory_space=pltpu.VMEM)] * ns,
        out_shapes=[_sds(sh, BF16) for sh in dst_shapes]
        + ([_sds((NSH,) + small.shape, F32)] if ns else []),
        scratch=[pltpu.VMEM(a.shape, BF16) for a in srcs]
        + [dma((3 * nu,)), dma((3 * nu,)), dma((3 * nu,)), dma((3 * nu,)),
           dma((nu + 1,)), dma((3,)), dma((3,)), dma((nu,))],
        start=start, finish=finish)


def _pair_reduce(gs, name):
    nu = len(gs)
    ns = [g.shape[0] * NSH for g in gs]
    base = [sum(ns[:u]) for u in range(nu)]

    def body(*refs):
        g_refs, own_refs, sb_refs = refs[:nu], refs[nu:2 * nu], refs[2 * nu:3 * nu]
        bufs = refs[3 * nu:8 * nu]
        send, recv, lsem, osem = refs[8 * nu:]
        x, y, c, _ = _place()
        s = 2 * x + y

        def unit(u):
            sendb, recvb, stage, outf, outb = bufs[5 * u:5 * u + 5]

            def load(k, half):
                return pltpu.make_async_copy(g_refs[u].at[k // NSH, k % NSH, half], stage.at[k % 3],
                                             lsem.at[3 * u + k % 3])

            def push(k):
                return pltpu.make_async_remote_copy(
                    src_ref=sendb.at[k], dst_ref=recvb.at[k], send_sem=send.at[base[u] + k],
                    recv_sem=recv.at[base[u] + k], device_id=(x, y, 1 - c), device_id_type=MESH)

            def store(k):
                return pltpu.make_async_copy(outb.at[k % 2], sb_refs[u].at[k // NSH, k % NSH],
                                             osem.at[3 * u + k % 2])

            return sendb, recvb, stage, outf, outb, load, push, store

        for u in range(nu):
            sendb, recvb, stage, outf, outb, load, push, store = unit(u)
            for k in range(min(2, ns[u])):
                load(k, 1 - c).start()
            for k in range(ns[u]):
                if k + 2 < ns[u]:
                    load(k + 2, 1 - c).start()
                load(k, 1 - c).wait()
                sendb[k] = stage[k % 3].astype(BF16)
                push(k).start()
        for u in range(nu):
            sendb, recvb, stage, outf, outb, load, push, store = unit(u)
            n = ns[u]
            for k in range(min(2, n)):
                load(k, c).start()
            for k in range(n):
                if k + 2 < n:
                    load(k + 2, c).start()
                load(k, c).wait()
                push(k).wait_recv()
                total = stage[k % 3] + recvb[k].astype(F32)
                if k >= 2:
                    store(k - 2).wait()
                outb[k % 2] = total.astype(BF16)
                store(k).start()

                @pl.when(s == k % NSH)
                def _():
                    outf[...] = total
                    keep = pltpu.make_async_copy(outf, own_refs[u].at[k // NSH], osem.at[3 * u + 2])
                    keep.start()
                    keep.wait()

            for k in range(max(n - 2, 0), n):
                store(k).wait()
        for u in range(nu):
            push = unit(u)[6]
            for k in range(ns[u]):
                push(k).wait_send()

    dma = pltpu.SemaphoreType.DMA
    scratch = []
    for g, n in zip(gs, ns):
        rh, cc = g.shape[3], g.shape[4]
        scratch += [pltpu.VMEM((n, rh, cc), BF16), pltpu.VMEM((n, rh, cc), BF16),
                    pltpu.VMEM((3, rh, cc), F32), pltpu.VMEM((rh, cc), F32), pltpu.VMEM((2, rh, cc), BF16)]
    outs = pl.pallas_call(
        body, in_specs=[ANY] * nu, out_specs=[ANY] * (2 * nu),
        out_shape=[_sds((g.shape[0], g.shape[3], g.shape[4]), F32) for g in gs]
        + [_sds((g.shape[0], NSH, g.shape[3], g.shape[4]), BF16) for g in gs],
        scratch_shapes=scratch + [dma((sum(ns),)), dma((sum(ns),)), dma((3 * nu,)), dma((3 * nu,))],
        compiler_params=pltpu.CompilerParams(has_side_effects=True, vmem_limit_bytes=56 << 20),
        name=name,
    )(*gs)
    return list(outs[:nu]), list(outs[nu:])


def _chip_exchange(sums_bf16):
    nu = len(sums_bf16)

    def pushes(ins, outs, scr):
        x, y, c, chips = _place()
        send, recv = scr
        return [pltpu.make_async_remote_copy(
            src_ref=ins[u].at[:, 2 * chip[0] + chip[1]], dst_ref=outs[u].at[j],
            send_sem=send.at[3 * u + j], recv_sem=recv.at[3 * u + j],
            device_id=(*chip, c), device_id_type=MESH)
            for u in range(nu) for j, chip in enumerate(chips)]

    def start(ins, outs, scr):
        for cp in pushes(ins, outs, scr):
            cp.start()

    def finish(ins, outs, scr):
        for cp in pushes(ins, outs, scr):
            cp.wait()

    dma = pltpu.SemaphoreType.DMA
    shapes = [(3, a.shape[0], a.shape[2], a.shape[3]) for a in sums_bf16]
    return _Exchange(ins=list(sums_bf16), in_specs=[ANY] * nu,
                     out_shapes=[_sds(sh, BF16) for sh in shapes],
                     scratch=[dma((3 * nu,)), dma((3 * nu,))], start=start, finish=finish)


def _final_exchange(halves, out_shapes, targets):
    nu = len(halves)
    no = len(out_shapes)
    ncp = sum(len(t) for t in targets)

    def body(*refs):
        hv = refs[:nu]
        out = refs[nu:nu + no]
        sbuf = refs[nu + no:2 * nu + no]
        rbuf = refs[2 * nu + no:3 * nu + no]
        send, recv, lsem, osem, csem = refs[3 * nu + no:]
        x, y, c, _ = _place()
        stage = [pltpu.make_async_copy(hv[u], sbuf[u], lsem.at[u]) for u in range(nu)]
        push = [pltpu.make_async_remote_copy(
            src_ref=sbuf[u], dst_ref=rbuf[u], send_sem=send.at[u], recv_sem=recv.at[u],
            device_id=(x, y, 1 - c), device_id_type=MESH) for u in range(nu)]
        mine, theirs = [], []
        k = 0
        for u in range(nu):
            rh = hv[u].shape[1]
            for (p, oi, li) in targets[u]:
                mine.append((u, pltpu.make_async_copy(
                    sbuf[u].at[p], out[oi].at[li, pl.ds(c * rh, rh), :], csem.at[k])))
                theirs.append((u, pltpu.make_async_copy(
                    rbuf[u].at[p], out[oi].at[li, pl.ds((1 - c) * rh, rh), :], osem.at[k])))
                k += 1
        for cp in stage:
            cp.start()
        for u in range(nu):
            stage[u].wait()
            push[u].start()
            for (v, cp) in mine:
                if v == u:
                    cp.start()
        for u in range(nu):
            push[u].wait_recv()
            for (v, cp) in theirs:
                if v == u:
                    cp.start()
        for (_, cp) in theirs + mine:
            cp.wait()
        for u in range(nu):
            push[u].wait_send()

    dma = pltpu.SemaphoreType.DMA
    bufs = [pltpu.VMEM(h.shape, F32) for h in halves]
    return pl.pallas_call(
        body, in_specs=[ANY] * nu, out_specs=[ANY] * no,
        out_shape=[_sds(sh, F32) for sh in out_shapes],
        scratch_shapes=bufs + bufs + [dma((nu,)), dma((nu,)), dma((nu,)), dma((ncp,)), dma((ncp,))],
        compiler_params=pltpu.CompilerParams(has_side_effects=True, vmem_limit_bytes=56 << 20),
        name="final_exchange",
    )(*halves)


def _small_allreduce(pack):
    rows = pack.shape[0]

    def body(p_ref, o_ref, buf, send, recv):
        x, y, c, _ = _place()
        me = 4 * x + 2 * y + c
        buf[me] = p_ref[...]
        k = 0
        copies = []
        for dx in range(2):
            for dy in range(2):
                for dc in range(2):
                    if dx == 0 and dy == 0 and dc == 0:
                        continue
                    to = (jnp.where(dx, 1 - x, x), jnp.where(dy, 1 - y, y), jnp.where(dc, 1 - c, c))
                    src_slot = 4 * to[0] + 2 * to[1] + to[2]
                    copies.append((pltpu.make_async_remote_copy(
                        src_ref=p_ref, dst_ref=buf.at[me], send_sem=send.at[k], recv_sem=recv.at[k],
                        device_id=to, device_id_type=MESH), src_slot, k))
                    k += 1
        for cp, _, _ in copies:
            cp.start()
        for cp, src_slot, k in copies:
            pltpu.make_async_remote_copy(
                src_ref=p_ref, dst_ref=buf.at[src_slot], send_sem=send.at[k], recv_sem=recv.at[k],
                device_id=(x, y, c), device_id_type=MESH).wait()
        acc = buf[0]
        for d in range(1, 8):
            acc = acc + buf[d]
        o_ref[...] = acc

    dma = pltpu.SemaphoreType.DMA
    vm = BS(memory_space=pltpu.VMEM)
    return pl.pallas_call(
        body, in_specs=[vm], out_specs=vm, out_shape=_sds((rows, D), F32),
        scratch_shapes=[pltpu.VMEM((8, rows, D), F32), dma((7,)), dma((7,))],
        compiler_params=pltpu.CompilerParams(has_side_effects=True, vmem_limit_bytes=32 << 20),
        name="small_allreduce",
    )(pack)


def _in_proj(h, w, bias, name, transposed=False, carry=None):
    n = w.shape[0] if transposed else w.shape[1]
    tn = 1280 if n == NA else 1792
    ep = None
    extras, especs = (), ()
    if bias is not None:
        def ep(acc, ex, o_ref, i):
            o_ref[...] = acc + ex[0][...]
        extras = (bias,)
        especs = (BS((1, tn), lambda i, j, k: (0, j)),)
    b_spec = BS((tn, D), lambda i, j, k: (j, 0)) if transposed else BS((D, tn), lambda i, j, k: (0, j))
    return _mm("nt" if transposed else "nn", h, w, grid=(T // TR, n // tn, 1),
               a_spec=BS((TR, D), lambda i, j, k: (i, 0)), b_spec=b_spec,
               out_shape=_sds((T, n), F32), out_spec=BS((TR, tn), lambda i, j, k: (i, j)),
               acc_shape=(TR, tn), extras=extras, extra_specs=especs, epilogue=ep, name=name, carry=carry)


def _res_rms_epilogue(acc, ex, outs, i):
    y = acc + ex[0][...]
    outs[0][...] = y
    r = lax.rsqrt(jnp.mean(y * y, axis=-1, keepdims=True) + EPS)
    outs[1][...] = (y * r * ex[1][...]).astype(BF16)


def _res_loss_epilogue(acc, ex, outs, i):
    e = acc + ex[0][...] - ex[1][...]
    outs[1][...] = e * (1.0 / D)

    @pl.when(i == 0)
    def _():
        outs[0][...] = jnp.zeros_like(outs[0])

    outs[0][...] += 0.5 * jnp.sum(jnp.mean(e * e, axis=-1, keepdims=True), axis=0, keepdims=True)


def _rms_bwd_epilogue(dh, ex, outs, i):
    xv = ex[0][...]
    r = lax.rsqrt(jnp.mean(xv * xv, axis=-1, keepdims=True) + EPS)
    xh = xv * r
    gy = dh * ex[1][...]
    outs[0][...] = r * (gy - xh * jnp.mean(gy * xh, axis=-1, keepdims=True)) + ex[2][...]

    @pl.when(i == 0)
    def _():
        outs[1][...] = jnp.zeros_like(outs[1])

    outs[1][...] += jnp.sum(dh * xh, axis=0, keepdims=True)


def _local_step(x, mem, target, w, p, carries=None, bwd_carry_fn=None, late_carry_fn=None):
    row = lambda i, j, k: (i, 0)
    whole = lambda i, j, k: (0, 0)
    w = {k: (list(v) if isinstance(v, list) else v) for k, v in w.items()}
    carries = carries or {}

    def carry_of(name):
        return carries[name][0] if name in carries else None

    def delivered(name, outs):
        if name in carries:
            carries[name][1](w, outs)

    saved = []
    bias = _bias_expand(p["rel_u"])
    vec = BS((1, D), whole)
    h = _rms_fwd(x, p["norm1_g"][0:1], "rms1_0", carry=carry_of("rms1_0"))
    if carry_of("rms1_0") is not None:
        h, carried = h
        delivered("rms1_0", carried)
    for l in range(2):
        type_a = l == 0
        memn = _rms_fwd(mem, p["mem_norm_g"][l:l + 1], f"rmsmem_{l}")
        if type_a:
            z = _in_proj(h, w["a"], None, "inproj_a", carry=carry_of("inproj_a"))
            if carry_of("inproj_a") is not None:
                z, carried = z
                delivered("inproj_a", carried)
            cat, carried = _attn_fwd(z, bias, p["a_q_g2"], p["a_k_g2"], carry_of("attn_fwd"))
            delivered("attn_fwd", carried)
            qcol = NA // MEMW - 1
        else:
            z = _in_proj(h, w["b"], w["b_b_in"], "inproj_b", transposed=True)
            cat, conv_y = _conv_fwd(z, w["conv_w"], w["conv_b"], w["ln_g"], w["ln_b"])
            qcol = NBW // MEMW - 1
        kv = _mm("nn", memn, w["kv"][l], grid=(1, 1, 1),
                 a_spec=BS((NB * MEMT, D), whole), b_spec=BS((D, 2 * MEMW), whole),
                 out_shape=_sds((NB * MEMT, 2 * MEMW), F32), out_spec=BS((NB * MEMT, 2 * MEMW), whole),
                 acc_shape=(8, 128), name=f"memkv_{l}")
        cat = _memattn_fwd(z, kv, cat, p["mq_g4"][l:l + 1], p["mk_g4"][l:l + 1], qcol, f"memattn_fwd_{l}")
        x1, h2 = _mm("nn", cat, w["wo"][l], grid=(T // TR, 1, 1), a_spec=BS((TR, D), row),
                     b_spec=BS((D, D), whole),
                     out_shape=[_sds((T, D), F32), _sds((T, D), BF16)],
                     out_spec=[BS((TR, D), row), BS((TR, D), row)], acc_shape=(8, 128),
                     extras=(x, p["norm2_g"][l:l + 1]), extra_specs=(BS((TR, D), row), vec),
                     epilogue=_res_rms_epilogue, name=f"outproj_{l}")
        (g, u, act), carried = _ffn_up(h2, w["gu"][l], l, carry_of(f"ffn_up_{l}"))
        delivered(f"ffn_up_{l}", carried)
        last = l == 1
        res = _mm("nn", act, w["wd"][l], grid=(T // TR, 1, 1),
                  a_spec=BS((TR, FF), row), b_spec=BS((FF, D), whole),
                  out_shape=[_sds((1, 1), F32), _sds((T, D), F32)] if last else
                  [_sds((T, D), F32), _sds((T, D), BF16)],
                  out_spec=[BS((1, 1), whole), BS((TR, D), row)] if last else
                  [BS((TR, D), row), BS((TR, D), row)],
                  acc_shape=(8, 128), extras=(x1, target if last else p["norm1_g"][1:2]),
                  extra_specs=(BS((TR, D), row), BS((TR, D), row) if last else vec),
                  epilogue=_res_loss_epilogue if last else _res_rms_epilogue, sequential=last,
                  name=f"ffn_down_{l}", carry=carry_of(f"ffn_down_{l}"))
        if carry_of(f"ffn_down_{l}") is not None:
            res, carried = res
            delivered(f"ffn_down_{l}", carried)
        saved.append(dict(x=x, h=h, memn=memn, kv=kv, z=z, cat=cat, x1=x1, h2=h2, g=g, u=u, act=act,
                          qcol=qcol))
        if last:
            loss, dx = res
        else:
            x, h = res

    big = dict(a=None, b=None, kv=[None, None], wo=[None, None], gu=[None, None], wd=[None, None])
    small = {}
    bwd_carried, late_carried = (), ()
    tk = T // 2
    nkt = T // tk
    for l in (1, 0):
        sv = saved[l]
        dgu = _ffn_down_bwd(dx, w["wd"][l], sv["g"], sv["u"], l)
        big["wd"][l] = _mm("tn", sv["act"], dx, grid=(FF // FT, 1, 2 * nkt),
                           a_spec=BS((tk // 2, FT), lambda i, j, k: (k, i)),
                           b_spec=BS((tk // 2, D), lambda i, j, k: (k, 0)),
                           out_shape=_sds((FF, D), F32), out_spec=BS((FT, D), lambda i, j, k: (i, 0)),
                           acc_shape=(FT, D), name=f"dw_down_{l}")
        dx1, small[f"norm2_g{l}"] = _mm(
            "nn", dgu, w["gu"][l], grid=(T // TR, 1, 2),
            a_spec=BS((None, TR, FF), lambda i, j, k: (k, i, 0)),
            b_spec=BS((None, FF, D), lambda i, j, k: (k, 0, 0)),
            out_shape=[_sds((T, D), F32), _sds((1, D), F32)], out_spec=[BS((TR, D), row), vec],
            acc_shape=(TR, D), extras=(sv["x1"], p["norm2_g"][l:l + 1], dx),
            extra_specs=(BS((TR, D), row), vec, BS((TR, D), row)),
            epilogue=_rms_bwd_epilogue, sequential=True, name=f"dh2_{l}")
        big["gu"][l] = _mm("tn", dgu, sv["h2"], grid=(2 * FF // FT, 1, nkt),
                           a_spec=BS((None, tk, FT), lambda i, j, k: (i // 2, k, i % 2)),
                           b_spec=BS((tk, D), lambda i, j, k: (k, 0)),
                           out_shape=_sds((2, FF, D), F32),
                           out_spec=BS((None, FT, D), lambda i, j, k: (i // 2, i % 2, 0)),
                           acc_shape=(FT, D), vmem_mb=58, name=f"dw_gu_{l}")
        dcat = _mm("nt", dx1, w["wo"][l], grid=(T // TR, 1, 1), a_spec=BS((TR, D), row),
                   b_spec=BS((D, D), whole),
                   out_shape=_sds((T, D), F32), out_spec=BS((TR, D), row), acc_shape=(8, 128),
                   name=f"dcat_{l}")
        big["wo"][l] = _mm("tn", sv["cat"], dx1, grid=(1, 1, nkt),
                           a_spec=BS((tk, D), lambda i, j, k: (k, 0)), b_spec=BS((tk, D), lambda i, j, k: (k, 0)),
                           out_shape=_sds((D, D), F32), out_spec=BS((D, D), whole),
                           acc_shape=(D, D), name=f"dw_out_{l}")
        dqm, dkv, small[f"mq_g{l}"], small[f"mk_g{l}"] = _memattn_bwd(
            sv["z"], sv["kv"], dcat, p["mq_g4"][l:l + 1], p["mk_g4"][l:l + 1], sv["qcol"], f"memattn_bwd_{l}",
            dz_cols=NA if l == 0 else MEMW)
        big["kv"][l] = _mm("tn", sv["memn"], dkv, grid=(1, 1, 1),
                           a_spec=BS((NB * MEMT, D), whole), b_spec=BS((NB * MEMT, 2 * MEMW), whole),
                           out_shape=_sds((D, 2 * MEMW), F32), out_spec=BS((D, 2 * MEMW), whole),
                           acc_shape=(8, 128), name=f"dw_kv_{l}")
        dmemn = _mm("nt", dkv, w["kv"][l], grid=(1, 1, 1),
                    a_spec=BS((NB * MEMT, 2 * MEMW), whole), b_spec=BS((D, 2 * MEMW), whole),
                    out_shape=_sds((NB * MEMT, D), F32), out_spec=BS((NB * MEMT, D), whole),
                    acc_shape=(8, 128), name=f"dmemn_{l}")
        _, small[f"mem_norm_g{l}"] = _rms_bwd(dmemn, mem, p["mem_norm_g"][l:l + 1], None, f"rmsmem_bwd_{l}")
        if l == 0:
            carry = bwd_carry_fn(big) if bwd_carry_fn is not None else None
            (dz, dbias, small["a_q_g"], small["a_k_g"]), bwd_carried = _attn_bwd(
                sv["z"], dcat, dqm, bias, p["a_q_g2"], p["a_k_g2"], carry)
            small["rel_u"] = _bias_reduce(dbias)
            w_in, key, n, tn = w["a"], "a", NA, 1280
        else:
            du, small["conv_w"], small["conv_b"], small["ln_g"], small["ln_b"], dbin_u = _conv_bwd(
                sv["z"], conv_y, dcat, w["conv_w"], w["ln_g"], w["ln_b"])
            dz = jnp.concatenate([du, dqm], axis=1)
            small["b_in_u"] = dbin_u
            w_in, key, n, tn = w["b"], "b", NBW, 896
        norm_bwd = dict(out_shape=[_sds((T, D), F32), _sds((1, D), F32)], out_spec=[BS((TR, D), row), vec],
                        acc_shape=(8, 128), extras=(sv["x"], p["norm1_g"][l:l + 1], dx1),
                        extra_specs=(BS((TR, D), row), vec, BS((TR, D), row)),
                        epilogue=_rms_bwd_epilogue, sequential=True, name=f"dh_{l}")
        if l == 0:
            big[key] = _mm("tn", sv["h"], dz, grid=(1, n // tn, nkt),
                           a_spec=BS((tk, D), lambda i, j, k: (k, 0)), b_spec=BS((tk, tn), lambda i, j, k: (k, j)),
                           out_shape=_sds((D, n), F32), out_spec=BS((D, tn), lambda i, j, k: (0, j)),
                           acc_shape=(D, tn), name=f"dw_in_{l}")
            carry = late_carry_fn(big) if late_carry_fn is not None else None
            res = _mm("nt", dz, w_in, grid=(T // TR, 1, 1), a_spec=BS((TR, n), row),
                      b_spec=BS((D, n), whole), carry=carry, **norm_bwd)
            if carry is not None:
                res, late_carried = res
            dx, small[f"norm1_g{l}"] = res
        else:
            dx, small[f"norm1_g{l}"] = _mm("nn", dz, w_in, grid=(T // TR, 1, 1),
                                           a_spec=BS((TR, n), row), b_spec=BS((n, D), whole), **norm_bwd)
            big[key] = _mm("tn", dz, sv["h"], grid=(n // tn, 1, nkt),
                           a_spec=BS((tk, tn), lambda i, j, k: (k, i)), b_spec=BS((tk, D), lambda i, j, k: (k, 0)),
                           out_shape=_sds((n, D), F32), out_spec=BS((tn, D), lambda i, j, k: (i, 0)),
                           acc_shape=(tn, D), name=f"dw_in_{l}")
        if l == 1:
            small["b_in_qm"] = _colsum(dqm, "colsum_dqm")
    return loss, dx, big, small, bwd_carried, late_carried


def _colsum(a, name):
    rows, cols = a.shape

    def body(a_ref, o_ref):
        @pl.when(pl.program_id(0) == 0)
        def _():
            o_ref[...] = jnp.zeros_like(o_ref)

        o_ref[...] += jnp.sum(a_ref[...].astype(F32), axis=0, keepdims=True)

    return pl.pallas_call(
        body, grid=(rows // TR,), in_specs=[BS((TR, cols), lambda i: (i, 0))],
        out_specs=BS((1, cols), lambda i: (0, 0)), out_shape=_sds((1, cols), F32),
        compiler_params=_cp(("arbitrary",)), name=name,
    )(a)


_PACK_ROWS = 64


def _pack_small(sm):
    plan = [("norm1_g0", 0, 1, 0, D), ("norm1_g1", 1, 1, 0, D), ("mem_norm_g0", 2, 1, 0, D),
            ("mem_norm_g1", 3, 1, 0, D), ("norm2_g0", 4, 1, 0, D), ("norm2_g1", 5, 1, 0, D),
            ("a_q_g", 6, 1, 0, AW), ("a_k_g", 7, 1, 0, AW), ("mq_g0", 8, 1, 0, MEMW),
            ("mq_g1", 9, 1, 0, MEMW), ("mk_g0", 10, 1, 0, MEMW), ("mk_g1", 11, 1, 0, MEMW),
            ("conv_b", 12, 1, 0, TOK), ("ln_g", 13, 1, 0, TOK), ("ln_b", 14, 1, 0, TOK),
            ("b_in_u", 15, 1, 0, D), ("b_in_u", 16, 1, D, 2 * TOK - D), ("b_in_qm", 17, 1, 0, MEMW),
            ("conv_w", 18, CONVW, 0, TOK), ("rel_u", 49, 12, 0, D)]
    arrs = [sm[name].reshape(12, D) if name == "rel_u" else sm[name] for name, *_ in plan]

    def body(*refs):
        o_ref = refs[-1]
        o_ref[...] = jnp.zeros_like(o_ref)
        for ref, (_, r0, nr, c0, nc) in zip(refs, plan):
            o_ref[r0:r0 + nr, 0:nc] = ref[0:nr, c0:c0 + nc]

    return pl.pallas_call(
        body, grid=(1,), in_specs=[BS(a.shape, lambda i: (0, 0)) for a in arrs],
        out_specs=BS((_PACK_ROWS, D), lambda i: (0, 0)), out_shape=_sds((_PACK_ROWS, D), F32),
        compiler_params=_cp(("arbitrary",)), name="pack_small",
    )(*arrs)


def _rel_table_to_u(rel_bias):
    flat = jnp.concatenate([jnp.broadcast_to(rel_bias[:, 191:192], (12, 447)), rel_bias[:, ::-1]], axis=1)
    return jnp.pad(flat, ((0, 0), (192, 1024 - 192 - 639))).reshape(12, 1, 1024)


def _u_to_rel_table(du):
    flat = du[:, 192:192 + 639]
    g = flat[:, 447:][:, ::-1]
    return g, flat[:, :447]


def kernel(x, mem, norm1_g, mem_norm_g, a_w_in, a_q_g, a_k_g, a_rel_bias, b_w_in, b_b_in, b_conv_w, b_conv_b, b_ln_g, b_ln_b, mq_g, mk_g, w_mem_kv, w_out, norm2_g, w_gate, w_up, w_down, loss_target, m_norm1_g, m_mem_norm_g, m_a_w_in, m_a_q_g, m_a_k_g, m_a_rel_bias, m_b_w_in, m_b_b_in, m_b_conv_w, m_b_conv_b, m_b_ln_g, m_b_ln_b, m_mq_g, m_mk_g, m_w_mem_kv, m_w_out, m_norm2_g, m_w_gate, m_w_up, m_w_down, v_norm1_g, v_mem_norm_g, v_a_w_in, v_a_q_g, v_a_k_g, v_a_rel_bias, v_b_w_in, v_b_b_in, v_b_conv_w, v_b_conv_b, v_b_ln_g, v_b_ln_b, v_mq_g, v_mk_g, v_w_mem_kv, v_w_out, v_norm2_g, v_w_gate, v_w_up, v_w_down):
    sx = 2 * lax.axis_index("x") + lax.axis_index("y")

    n_in = (NA // NSH, NBW // NSH)
    tr = lambda a: jnp.swapaxes(a, -1, -2)
    small_src = jnp.concatenate([
        jnp.pad(b_b_in, ((0, 0), (0, 512 - 448))),
        jnp.pad(b_conv_w[0], ((0, 0), (0, 512 - 192))),
        jnp.pad(jnp.concatenate([b_conv_b, b_ln_g, b_ln_b], 0), ((0, 0), (0, 512 - 192))),
        jnp.zeros((5, 512), F32)], 0)

    def gather_groups(groups, small=None):
        def src_of(l, name):
            if name == "in":
                return [_cast_bf16([tr(b_w_in[0])], "cast_in_1").reshape(1, 2, n_in[1] // 2, D) if l else
                        _cast_bf16([a_w_in[0]], "cast_in_0").reshape(1, 2, D // 2, n_in[0])]
            if name == "gu":
                return [_cast_bf16([tr(w_gate[l])], f"cast_gate_{l}").reshape(1, 2, FS // 2, D),
                        _cast_bf16([tr(w_up[l])], f"cast_up_{l}").reshape(1, 2, FS // 2, D)]
            arr, shape = {"kv": (w_mem_kv, (1, 2, 128, 2 * MEMW)), "wo": (w_out, (1, 2, 128, D)),
                          "wd": (w_down, (1, 2, FS // 2, D))}[name]
            return [_cast_bf16([arr[l]], f"cast_{name}_{l}").reshape(shape)]

        def dst_of(l, name):
            if name == "in":
                return (1, NSH, 2, n_in[1] // 2, D) if l else (1, NSH, 2, D // 2, n_in[0])
            return {"kv": (1, NSH, 2, 128, 2 * MEMW), "wo": (1, NSH, 2, 128, D),
                    "gu": (1, 2, NSH, 2, FS // 2, D), "wd": (1, NSH, 2, FS // 2, D)}[name]

        items = [(l, name) for l, names in groups for name in names]
        srcs, views = [], []
        for k, (l, name) in enumerate(items):
            srcs += src_of(l, name)
            if name == "gu":
                views += [lambda d, k=k: d[k].at[:, 0], lambda d, k=k: d[k].at[:, 1]]
            else:
                views.append(lambda d, k=k: d[k])

        def done(w, outs):
            for k, (l, name) in enumerate(items):
                if name == "in" and l == 0:
                    w["a"] = outs[k].reshape(NSH, D, n_in[0]).transpose(1, 0, 2).reshape(D, NA)
                elif name == "in":
                    w["b"] = outs[k].reshape(NBW, D)
                else:
                    shape = {"kv": (D, 2 * MEMW), "wo": (D, D), "gu": (2, FF, D), "wd": (FF, D)}
                    w[name][l] = outs[k].reshape(shape[name])

        return _gather_exchange(srcs, [dst_of(l, name) for l, name in items], views, small), done

    w = dict(a=None, b=None, kv=[None, None], wo=[None, None], gu=[None, None], wd=[None, None])
    first, first_in_done = gather_groups([(0, ["in"])], small_src)

    def first_done(w, outs):
        first_in_done(w, outs)
        small_all = outs[1]
        conv_w_full = small_all[:, 1:1 + CONVW, :192].transpose(1, 0, 2).reshape(CONVW, TOK)
        vec3 = small_all[:, 32:35, :192].transpose(1, 0, 2).reshape(3, TOK)
        w.update(b_b_in=small_all[:, 0, :448].reshape(1, NBW), conv_w=jnp.pad(conv_w_full, ((0, 1), (0, 0))),
                 conv_b=vec3[0:1], ln_g=vec3[1:2], ln_b=vec3[2:3])

    carries = {"rms1_0": (first, first_done),
               "inproj_a": gather_groups([(0, ["kv", "wo", "wd"])]),
               "attn_fwd": gather_groups([(0, ["gu"]), (1, ["in", "kv", "wo"])]),
               "ffn_up_0": gather_groups([(1, ["gu"])]),
               "ffn_down_0": gather_groups([(1, ["wd"])])}
    p = dict(
        norm1_g=norm1_g, mem_norm_g=mem_norm_g, norm2_g=norm2_g,
        a_q_g2=jnp.tile(a_q_g, (1, AH)), a_k_g2=jnp.tile(a_k_g, (1, AH)),
        mq_g4=jnp.tile(mq_g, (1, 4)), mk_g4=jnp.tile(mk_g, (1, 4)),
        rel_u=_rel_table_to_u(a_rel_bias[0]))

    def pair_sums(items, big, name):
        units = []
        for l, tensor in items:
            if tensor == "in" and l == 0:
                g = big["a"].reshape(D, NSH, n_in[0]).transpose(1, 0, 2).reshape(1, NSH, 2, D // 2, n_in[0])
            elif tensor == "in":
                g = big["b"].reshape(1, NSH, 2, n_in[1] // 2, D)
            else:
                shape = {"kv": (1, NSH, 2, 128, 2 * MEMW), "wo": (1, NSH, 2, 128, D),
                         "gu": (2, NSH, 2, FS // 2, D), "wd": (1, NSH, 2, FS // 2, D)}
                g = big[tensor][l].reshape(shape[tensor])
            units.append(g)
        return _pair_reduce(units, name)

    early_groups = [[(1, "gu")], [(0, "gu")],
                    [(1, "in"), (1, "kv"), (1, "wo"), (1, "wd"), (0, "kv"), (0, "wo"), (0, "wd")]]
    early = [item for grp in early_groups for item in grp]
    late = [(0, "in")]
    own_early, own_late = [], []

    def bwd_carry_fn(big):
        sums_b = []
        for k, grp in enumerate(early_groups):
            own, sb = pair_sums(grp, big, f"pair_reduce_early_{k}")
            own_early.extend(own)
            sums_b.extend(sb)
        return _chip_exchange(sums_b)

    def late_carry_fn(big):
        own, sb = pair_sums(late, big, "pair_reduce_late")
        own_late.extend(own)
        return _chip_exchange(sb)

    loss, grad_x, big, small, parts_early, parts_late = _local_step(
        x.reshape(T, D), mem.reshape(NB * MEMT, D), loss_target.reshape(T, D), w, p,
        carries=carries, bwd_carry_fn=bwd_carry_fn, late_carry_fn=late_carry_fn)
    loss = lax.psum(loss[0, 0], ("x", "y", "c"))
    items = early + late
    halves = [_quad_sum(o, pt, f"quad_sum_{name}_{l}")
              for (l, name), o, pt in zip(items, own_early + own_late, list(parts_early) + list(parts_late))]
    out_shapes = [(1, D, NA // NSH), (1, NBW // NSH, D), (2, 2 * 128, 2 * MEMW), (2, 2 * 128, D),
                  (2, FS, D), (2, FS, D), (2, FS, D)]
    target_of = {"in": lambda l: [(0, l, 0)], "kv": lambda l: [(0, 2, l)], "wo": lambda l: [(0, 3, l)],
                 "gu": lambda l: [(0, 4, l), (1, 5, l)], "wd": lambda l: [(0, 6, l)]}
    targets = [target_of[name](l) for l, name in items]
    g_a, g_b, g_kv, g_wo, g_gate, g_up, g_wd = _final_exchange(halves, out_shapes, targets)

    tot = _small_allreduce(_pack_small(small))
    g_rel, clip_part = _u_to_rel_table(tot[49:61])
    g_rel = jnp.concatenate([g_rel[:, :191], g_rel[:, 191:] + _rowsum(clip_part)], axis=1)
    b_in_full = jnp.concatenate([tot[15:16], tot[16:17, :512], tot[17:18, :MEMW]], axis=1)
    g_small = dict(
        norm1_g=tot[0:2], mem_norm_g=tot[2:4], norm2_g=tot[4:6],
        a_q_g=tot[6:7, :HD], a_k_g=tot[7:8, :HD], a_rel_bias=g_rel[None],
        b_b_in=lax.dynamic_slice(b_in_full, (0, sx * 448), (1, 448)),
        b_conv_w=lax.dynamic_slice(tot[18:49, :TOK], (0, sx * 192), (CONVW, 192))[None],
        b_conv_b=lax.dynamic_slice(tot[12:13, :TOK], (0, sx * 192), (1, 192)),
        b_ln_g=lax.dynamic_slice(tot[13:14, :TOK], (0, sx * 192), (1, 192)),
        b_ln_b=lax.dynamic_slice(tot[14:15, :TOK], (0, sx * 192), (1, 192)),
        mq_g=tot[8:10, :HD], mk_g=tot[10:12, :HD])

    names = ["norm1_g", "mem_norm_g", "a_w_in", "a_q_g", "a_k_g", "a_rel_bias", "b_w_in", "b_b_in",
             "b_conv_w", "b_conv_b", "b_ln_g", "b_ln_b", "mq_g", "mk_g", "w_mem_kv", "w_out",
             "norm2_g", "w_gate", "w_up", "w_down"]
    weights = dict(zip(names, [norm1_g, mem_norm_g, a_w_in, a_q_g, a_k_g, a_rel_bias, b_w_in, b_b_in,
                               b_conv_w, b_conv_b, b_ln_g, b_ln_b, mq_g, mk_g, w_mem_kv, w_out,
                               norm2_g, w_gate, w_up, w_down]))
    ms = dict(zip(names, [m_norm1_g, m_mem_norm_g, m_a_w_in, m_a_q_g, m_a_k_g, m_a_rel_bias, m_b_w_in,
                          m_b_b_in, m_b_conv_w, m_b_conv_b, m_b_ln_g, m_b_ln_b, m_mq_g, m_mk_g,
                          m_w_mem_kv, m_w_out, m_norm2_g, m_w_gate, m_w_up, m_w_down]))
    vs = dict(zip(names, [v_norm1_g, v_mem_norm_g, v_a_w_in, v_a_q_g, v_a_k_g, v_a_rel_bias, v_b_w_in,
                          v_b_b_in, v_b_conv_w, v_b_conv_b, v_b_ln_g, v_b_ln_b, v_mq_g, v_mk_g,
                          v_w_mem_kv, v_w_out, v_norm2_g, v_w_gate, v_w_up, v_w_down]))
    grads = dict(g_small)
    grads.update(a_w_in=g_a, b_w_in=g_b, w_mem_kv=g_kv, w_out=g_wo, w_gate=g_gate, w_up=g_up, w_down=g_wd)
    big_names = ["a_w_in", "b_w_in", "w_mem_kv", "w_out", "w_gate", "w_up", "w_down"]
    small_names = [n for n in names if n not in big_names]
    delta, new_m, new_v = {}, {}, {}
    for n in big_names:
        if n in ("b_w_in", "w_gate", "w_up"):
            outs = _adamw_big(tr(weights[n]), grads[n], tr(ms[n]), tr(vs[n]), f"adamw_{n}")
            delta[n], new_m[n], new_v[n] = [tr(o) for o in outs]
            grads[n] = tr(grads[n])
        else:
            delta[n], new_m[n], new_v[n] = _adamw_big(weights[n], grads[n], ms[n], vs[n], f"adamw_{n}")
    as2d = lambda a: a.reshape(-1, a.shape[-1])
    d_s, m_s, v_s = _adamw_small([as2d(weights[n]) for n in small_names], [as2d(grads[n]) for n in small_names],
                                 [as2d(ms[n]) for n in small_names], [as2d(vs[n]) for n in small_names])
    for i, n in enumerate(small_names):
        delta[n] = d_s[i].reshape(weights[n].shape)
        new_m[n] = m_s[i].reshape(weights[n].shape)
        new_v[n] = v_s[i].reshape(weights[n].shape)

    return (loss, grad_x.reshape(NB, SEQ, D), *[grads[n] for n in names], *[delta[n] for n in names],
            *[new_m[n] for n in names], *[new_v[n] for n in names])


def _rowsum(a):
    def body(a_ref, o_ref):
        o_ref[...] = jnp.sum(a_ref[...], axis=1, keepdims=True)

    vm = BS(memory_space=pltpu.VMEM)
    return pl.pallas_call(body, in_specs=[vm], out_specs=vm, out_shape=_sds((a.shape[0], 1), F32),
                          compiler_params=_cp(), name="rowsum")(a)
```

```python
import functools

import jax
import jax.numpy as jnp
from jax import lax
from jax.experimental import pallas as pl
from jax.experimental.pallas import tpu as pltpu

F32 = jnp.float32
BF16 = jnp.bfloat16
BS = pl.BlockSpec
ANY = pl.BlockSpec(memory_space=pl.ANY)
MESH = pl.DeviceIdType.MESH

D = 1024
SEQ = 2048
NB = 2
T = NB * SEQ
MEMT = 256
HD = 64
TOK = 768
MEMW = 256
NA = 3 * TOK + MEMW
NBW = 2 * TOK + MEMW
FF = 2816
NSH = 4
FS = FF // NSH
FT = FF // 2
CONVW = 31
EPS = 1e-6
NEG = -1e30
SCALE = HD ** -0.5
QB = 256
KWIN = 768
KPAD = 512
TR = 512

ADAM_LR = 0.001
ADAM_B1 = 0.9
ADAM_B2 = 0.999
ADAM_EPS = 1e-08
ADAM_WD = 0.01
ADAM_STEP = 10

_DIMS = {
    "nn": (((1,), (0,)), ((), ())),
    "nt": (((1,), (1,)), ((), ())),
    "tn": (((0,), (0,)), ((), ())),
}


def _cp(sem=None, vmem_mb=48):
    return pltpu.CompilerParams(dimension_semantics=sem, vmem_limit_bytes=vmem_mb << 20)


def _sds(shape, dtype):
    return jax.ShapeDtypeStruct(tuple(shape), dtype)


def _mm(mode, a, b, *, grid, a_spec, b_spec, out_shape, out_spec, acc_shape, name,
        extras=(), extra_specs=(), epilogue=None, carry=None, vmem_mb=48, sequential=False):
    n_ex = len(extras)
    nk = grid[2]
    dims = _DIMS[mode]
    ni = len(carry.ins) if carry else 0
    no = len(carry.out_shapes) if carry else 0
    multi = isinstance(out_shape, (list, tuple))
    out_shapes = list(out_shape) if multi else [out_shape]
    out_specs = list(out_spec) if multi else [out_spec]
    n_o = len(out_shapes)

    def body(a_ref, b_ref, *rest):
        ex = rest[:n_ex]
        cin = rest[n_ex:n_ex + ni]
        o_refs = rest[n_ex + ni:n_ex + ni + n_o]
        o_ref = o_refs if multi else o_refs[0]
        cout = rest[n_ex + ni + n_o:n_ex + ni + n_o + no]
        acc = rest[n_ex + ni + n_o + no]
        cscr = rest[n_ex + ni + n_o + no + 1:]
        ids = [pl.program_id(d) for d in range(3)]
        k = ids[2]
        if carry:
            @pl.when((ids[0] == 0) & (ids[1] == 0) & (ids[2] == 0))
            def _():
                carry.start(cin, cout, cscr)

        prod = lax.dot_general(a_ref[...].astype(BF16), b_ref[...].astype(BF16), dims,
                               preferred_element_type=F32)

        def finish(val):
            if epilogue is None:
                o_ref[...] = val.astype(o_ref.dtype)
            else:
                epilogue(val, ex, o_ref, ids[0])

        if nk == 1:
            finish(prod)
        else:
            @pl.when(k == 0)
            def _():
                acc[...] = prod

            @pl.when((k > 0) & (k < nk - 1))
            def _():
                acc[...] += prod

            @pl.when(k == nk - 1)
            def _():
                finish(acc[...] + prod)

        if carry:
            @pl.when((ids[0] == grid[0] - 1) & (ids[1] == grid[1] - 1) & (ids[2] == grid[2] - 1))
            def _():
                carry.finish(cin, cout, cscr)

    acc_scratch = pltpu.VMEM(acc_shape if nk > 1 else (8, 128), F32)
    ordered = sequential or bool(carry)
    outs = pl.pallas_call(
        body, grid=grid,
        in_specs=[a_spec, b_spec, *extra_specs] + (carry.in_specs if carry else []),
        out_specs=out_specs + [ANY] * no, out_shape=out_shapes + (carry.out_shapes if carry else []),
        scratch_shapes=[acc_scratch] + (carry.scratch if carry else []),
        compiler_params=pltpu.CompilerParams(
            dimension_semantics=("arbitrary",) * 3 if ordered else ("parallel", "parallel", "arbitrary"),
            vmem_limit_bytes=vmem_mb << 20, has_side_effects=bool(carry)), name=name,
    )(a, b, *extras, *(carry.ins if carry else []))
    mine = list(outs[:n_o]) if multi else outs[0]
    return (mine, outs[n_o:]) if carry else mine


def _rms_fwd(x, g, name, carry=None):
    rows = x.shape[0]
    ni = len(carry.ins) if carry else 0
    no = len(carry.out_shapes) if carry else 0

    def body(x_ref, g_ref, *rest):
        cin, o_ref, cout, cscr = rest[:ni], rest[ni], rest[ni + 1:ni + 1 + no], rest[ni + 1 + no:]
        if carry:
            @pl.when(pl.program_id(0) == 0)
            def _():
                carry.start(cin, cout, cscr)

        xv = x_ref[...]
        r = lax.rsqrt(jnp.mean(xv * xv, axis=-1, keepdims=True) + EPS)
        o_ref[...] = (xv * r * g_ref[...]).astype(BF16)

        if carry:
            @pl.when(pl.program_id(0) == rows // TR - 1)
            def _():
                carry.finish(cin, cout, cscr)

    outs = pl.pallas_call(
        body, grid=(rows // TR,),
        in_specs=[BS((TR, D), lambda i: (i, 0)), BS((1, D), lambda i: (0, 0))]
        + (carry.in_specs if carry else []),
        out_specs=[BS((TR, D), lambda i: (i, 0))] + [ANY] * no,
        out_shape=[_sds((rows, D), BF16)] + (carry.out_shapes if carry else []),
        scratch_shapes=carry.scratch if carry else [],
        compiler_params=pltpu.CompilerParams(
            dimension_semantics=("arbitrary",), vmem_limit_bytes=48 << 20,
            has_side_effects=bool(carry)), name=name,
    )(x, g, *(carry.ins if carry else []))
    return (outs[0], outs[1:]) if carry else outs[0]


def _rms_bwd(dh, x, g, dres, name):
    rows = x.shape[0]
    has_res = dres is not None

    def body(*refs):
        if has_res:
            dh_ref, x_ref, g_ref, r_ref, dx_ref, dg_ref = refs
        else:
            dh_ref, x_ref, g_ref, dx_ref, dg_ref = refs
        xv = x_ref[...]
        dhv = dh_ref[...]
        r = lax.rsqrt(jnp.mean(xv * xv, axis=-1, keepdims=True) + EPS)
        xh = xv * r
        gy = dhv * g_ref[...]
        dx = r * (gy - xh * jnp.mean(gy * xh, axis=-1, keepdims=True))
        if has_res:
            dx = dx + r_ref[...]
        dx_ref[...] = dx

        @pl.when(pl.program_id(0) == 0)
        def _():
            dg_ref[...] = jnp.zeros_like(dg_ref)

        dg_ref[...] += jnp.sum(dhv * xh, axis=0, keepdims=True)

    row = BS((TR, D), lambda i: (i, 0))
    vec = BS((1, D), lambda i: (0, 0))
    ins = [dh, x, g] + ([dres] if has_res else [])
    return pl.pallas_call(
        body, grid=(rows // TR,),
        in_specs=[row, row, vec] + ([row] if has_res else []),
        out_specs=[row, vec], out_shape=[_sds((rows, D), F32), _sds((1, D), F32)],
        compiler_params=_cp(("arbitrary",)), name=name,
    )(*ins)


def _group_masks(width):
    lane = lax.broadcasted_iota(jnp.int32, (1, width), 1)
    return [(lane >= HD * h) & (lane < HD * (h + 1)) for h in range(width // HD)]


def _group_mean(v, masks):
    del masks
    low = lax.broadcasted_iota(jnp.int32, (1, 128), 1) < HD
    slabs = []
    for j in range(v.shape[-1] // 128):
        x = v[:, 128 * j:128 * (j + 1)]
        s0 = jnp.sum(jnp.where(low, x, 0.0), axis=-1, keepdims=True) * (1.0 / HD)
        s1 = jnp.sum(jnp.where(low, 0.0, x), axis=-1, keepdims=True) * (1.0 / HD)
        slabs.append(jnp.where(low, s0, s1))
    return slabs[0] if len(slabs) == 1 else jnp.concatenate(slabs, axis=-1)


def _head_norm(zv, g, masks):
    r = lax.rsqrt(_group_mean(zv * zv, masks) + EPS)
    return zv * r * g


def _head_norm_bwd(dy, zv, g, masks):
    r = lax.rsqrt(_group_mean(zv * zv, masks) + EPS)
    zh = zv * r
    gy = dy * g
    dz = r * (gy - zh * _group_mean(gy * zh, masks))
    return dz, jnp.sum(dy * zh, axis=0, keepdims=True)


def _fold_heads(v, width):
    vb = jnp.broadcast_to(v, (8, width))
    out = vb
    for h in range(1, width // HD):
        out = out + pltpu.roll(vb, width - HD * h, axis=1)
    return out[0:1]


def _bias_expand(u):
    def body(u_ref, o_ref):
        x = jnp.broadcast_to(u_ref[...], (QB, 1024))
        rolled = pltpu.roll(x, 1024 - (QB - 1), axis=1, stride=1, stride_axis=0)[:, :KWIN]
        row = lax.broadcasted_iota(jnp.int32, (QB, 1), 0)
        col = lax.broadcasted_iota(jnp.int32, (1, KWIN), 1)
        lo = (row // 64) * 64
        ok = (col >= lo) & (col < lo + 576)
        o_ref[...] = jnp.where(ok, rolled, NEG)

    return pl.pallas_call(
        body, grid=(12,), in_specs=[BS((None, 1, 1024), lambda h: (h, 0, 0))],
        out_specs=BS((None, QB, KWIN), lambda h: (h, 0, 0)), out_shape=_sds((12, QB, KWIN), F32),
        compiler_params=_cp(("arbitrary",)), name="bias_expand",
    )(u)


def _bias_reduce(ds):
    def body(d_ref, o_ref):
        ri = lax.broadcasted_iota(jnp.int32, (QB, QB), 0)
        ci = lax.broadcasted_iota(jnp.int32, (QB, QB), 1)
        flip = (ri + ci == QB - 1).astype(F32)
        drev = jnp.dot(flip, d_ref[...], precision=lax.Precision.HIGHEST, preferred_element_type=F32)
        x = jnp.concatenate([drev, jnp.zeros((QB, 1024 - KWIN), F32)], axis=1)
        rolled = pltpu.roll(x, 0, axis=1, stride=1, stride_axis=0)
        o_ref[...] = jnp.sum(rolled, axis=0, keepdims=True)

    return pl.pallas_call(
        body, grid=(12,), in_specs=[BS((None, QB, KWIN), lambda h: (h, 0, 0))],
        out_specs=BS((None, 1, 1024), lambda h: (h, 0, 0)), out_shape=_sds((12, 1, 1024), F32),
        compiler_params=_cp(("arbitrary",)), name="bias_reduce",
    )(ds)


AW = 256
AH = AW // HD
AG = TOK // AW
def _attn_softmax(qh, kw, bias, startadd):
    s = lax.dot_general(qh, kw, _DIMS["nt"], preferred_element_type=F32) + bias + startadd
    m = jnp.max(s, axis=-1, keepdims=True)
    p = jnp.exp(s - m)
    return p * (1.0 / jnp.sum(p, axis=-1, keepdims=True))


def _attn_prologue(q_ref, k_ref, v_ref, gq_ref, gk_ref, qn_s, kn_s, v_s, masks):
    kn_s[0:KPAD, :] = jnp.zeros((KPAD, AW), BF16)
    v_s[0:KPAD, :] = jnp.zeros((KPAD, AW), BF16)
    for r in range(0, SEQ, TR):
        qn_s[r:r + TR, :] = (_head_norm(q_ref[r:r + TR, :], gq_ref[...], masks) * SCALE).astype(BF16)
        kn_s[KPAD + r:KPAD + r + TR, :] = _head_norm(k_ref[r:r + TR, :], gk_ref[...], masks).astype(BF16)
        v_s[KPAD + r:KPAD + r + TR, :] = v_ref[r:r + TR, :].astype(BF16)


def _attn_fwd(z, bias, gq2, gk2, carry=None):
    ni = len(carry.ins) if carry else 0
    no = len(carry.out_shapes) if carry else 0

    def body(q_ref, k_ref, v_ref, b_ref, gq_ref, gk_ref, *rest):
        cin, o_ref, cout = rest[:ni], rest[ni], rest[ni + 1:ni + 1 + no]
        qn_s, kn_s, v_s = rest[ni + 1 + no:ni + 4 + no]
        cscr = rest[ni + 4 + no:]
        if carry:
            @pl.when((pl.program_id(0) == 0) & (pl.program_id(1) == 0))
            def _():
                carry.start(cin, cout, cscr)

        masks = _group_masks(AW)
        _attn_prologue(q_ref, k_ref, v_ref, gq_ref, gk_ref, qn_s, kn_s, v_s, masks)
        col = lax.broadcasted_iota(jnp.int32, (1, KWIN), 1)

        def blk(i, carry):
            r0 = pl.multiple_of(i * QB, QB)
            qb = qn_s[pl.ds(r0, QB), :]
            kw = kn_s[pl.ds(r0, KWIN), :]
            vw = v_s[pl.ds(r0, KWIN), :]
            startadd = jnp.where(col + r0 < KPAD, NEG, 0.0)
            o = jnp.zeros((QB, AW), F32)
            for h in range(AH):
                qh = jnp.where(masks[h], qb, jnp.zeros_like(qb))
                vh = jnp.where(masks[h], vw, jnp.zeros_like(vw))
                p = _attn_softmax(qh, kw, b_ref[h], startadd).astype(BF16)
                o = o + jnp.dot(p, vh, preferred_element_type=F32)
            o_ref[pl.ds(r0, QB), :] = o.astype(BF16)
            return carry

        lax.fori_loop(0, SEQ // QB, blk, 0)

        if carry:
            @pl.when((pl.program_id(0) == NB - 1) & (pl.program_id(1) == AG - 1))
            def _():
                carry.finish(cin, cout, cscr)

    vec = BS((1, AW), lambda b, hp: (0, 0))
    outs = pl.pallas_call(
        body, grid=(NB, AG),
        in_specs=[BS((SEQ, AW), lambda b, hp: (b, hp)),
                  BS((SEQ, AW), lambda b, hp: (b, AG + hp)),
                  BS((SEQ, AW), lambda b, hp: (b, 2 * AG + hp)),
                  BS((AH, QB, KWIN), lambda b, hp: (hp, 0, 0)), vec, vec]
        + (carry.in_specs if carry else []),
        out_specs=[BS((SEQ, AW), lambda b, hp: (b, hp))] + [ANY] * no,
        out_shape=[_sds((T, D), BF16)] + (carry.out_shapes if carry else []),
        scratch_shapes=[pltpu.VMEM((SEQ, AW), BF16), pltpu.VMEM((SEQ + KPAD, AW), BF16),
                        pltpu.VMEM((SEQ + KPAD, AW), BF16)] + (carry.scratch if carry else []),
        compiler_params=pltpu.CompilerParams(
            dimension_semantics=("arbitrary", "arbitrary"), vmem_limit_bytes=48 << 20,
            has_side_effects=bool(carry)), name="attn_fwd",
    )(z, z, z, bias, gq2, gk2, *(carry.ins if carry else []))
    return outs[0], outs[1:]


def _attn_bwd(z, dcat, dz_into, bias, gq2, gk2, carry=None):
    ni = len(carry.ins) if carry else 0
    no = len(carry.out_shapes) if carry else 0

    def body(q_ref, k_ref, v_ref, do_ref, b_ref, gq_ref, gk_ref, into_ref, *rest):
        del into_ref
        cin = rest[:ni]
        dz_ref, db_ref, dgq_ref, dgk_ref = rest[ni:ni + 4]
        cout = rest[ni + 4:ni + 4 + no]
        qn_s, kn_s, v_s, dqn_s, dkn_s, dv_s, out_s, out_sem = rest[ni + 4 + no:ni + 12 + no]
        cscr = rest[ni + 12 + no:]
        hp = pl.program_id(0)
        b = pl.program_id(1)
        step = hp * NB + b

        def out_copy(j, step):
            rows = pl.ds(pl.multiple_of((step % NB) * SEQ, SEQ), SEQ)
            cols = pl.ds(pl.multiple_of((j * AG + step // NB) * AW, AW), AW)
            return pltpu.make_async_copy(out_s.at[j], dz_ref.at[rows, cols], out_sem.at[j])

        if carry:
            @pl.when((hp == 0) & (b == 0))
            def _():
                carry.start(cin, cout, cscr)

        masks = _group_masks(AW)
        _attn_prologue(q_ref, k_ref, v_ref, gq_ref, gk_ref, qn_s, kn_s, v_s, masks)
        dkn_s[...] = jnp.zeros_like(dkn_s)
        dv_s[...] = jnp.zeros_like(dv_s)

        @pl.when(b == 0)
        def _():
            db_ref[...] = jnp.zeros_like(db_ref)

        @pl.when((b == 0) & (hp == 0))
        def _():
            dgq_ref[...] = jnp.zeros_like(dgq_ref)
            dgk_ref[...] = jnp.zeros_like(dgk_ref)

        col = lax.broadcasted_iota(jnp.int32, (1, KWIN), 1)

        def blk(i, carry):
            r0 = pl.multiple_of(i * QB, QB)
            qb = qn_s[pl.ds(r0, QB), :]
            kw = kn_s[pl.ds(r0, KWIN), :]
            vw = v_s[pl.ds(r0, KWIN), :]
            dob = do_ref[pl.ds(r0, QB), :].astype(BF16)
            startadd = jnp.where(col + r0 < KPAD, NEG, 0.0)
            dqn = jnp.zeros((QB, AW), F32)
            dkw = jnp.zeros((KWIN, AW), F32)
            dvw = jnp.zeros((KWIN, AW), F32)
            for h in range(AH):
                qh = jnp.where(masks[h], qb, jnp.zeros_like(qb))
                kh = jnp.where(masks[h], kw, jnp.zeros_like(kw))
                doh = jnp.where(masks[h], dob, jnp.zeros_like(dob))
                p = _attn_softmax(qh, kw, b_ref[h], startadd)
                dvw = dvw + lax.dot_general(p.astype(BF16), doh, _DIMS["tn"],
                                            preferred_element_type=F32)
                dp = lax.dot_general(doh, vw, _DIMS["nt"], preferred_element_type=F32)
                ds = p * (dp - jnp.sum(dp * p, axis=-1, keepdims=True))
                db_ref[h] += ds
                dsb = ds.astype(BF16)
                dqn = dqn + jnp.dot(dsb, kh, preferred_element_type=F32)
                dkw = dkw + lax.dot_general(dsb, qh, _DIMS["tn"], preferred_element_type=F32)
            dqn_s[pl.ds(r0, QB), :] = dqn * SCALE
            dkn_s[pl.ds(r0, KWIN), :] += dkw
            dv_s[pl.ds(r0, KWIN), :] += dvw
            return carry

        lax.fori_loop(0, SEQ // QB, blk, 0)

        @pl.when(step > 0)
        def _():
            for j in range(3):
                out_copy(j, step - 1).wait()

        dgq = jnp.zeros((1, AW), F32)
        dgk = jnp.zeros((1, AW), F32)
        for r in range(0, SEQ, TR):
            dq, dg = _head_norm_bwd(dqn_s[r:r + TR, :], q_ref[r:r + TR, :], gq_ref[...], masks)
            out_s[0, r:r + TR, :] = dq.astype(BF16)
            dgq = dgq + dg
            dk, dg = _head_norm_bwd(dkn_s[KPAD + r:KPAD + r + TR, :], k_ref[r:r + TR, :], gk_ref[...], masks)
            out_s[1, r:r + TR, :] = dk.astype(BF16)
            dgk = dgk + dg
            out_s[2, r:r + TR, :] = dv_s[KPAD + r:KPAD + r + TR, :].astype(BF16)
        dgq_ref[...] += _fold_heads(dgq, AW)
        dgk_ref[...] += _fold_heads(dgk, AW)
        for j in range(3):
            out_copy(j, step).start()

        @pl.when(step == AG * NB - 1)
        def _():
            for j in range(3):
                out_copy(j, step).wait()

        if carry:
            @pl.when((hp == AG - 1) & (b == NB - 1))
            def _():
                carry.finish(cin, cout, cscr)

    vec = BS((1, AW), lambda hp, b: (0, 0))
    row = BS((SEQ, AW), lambda hp, b: (b, hp))
    outs = pl.pallas_call(
        body, grid=(AG, NB),
        in_specs=[row,
                  BS((SEQ, AW), lambda hp, b: (b, AG + hp)),
                  BS((SEQ, AW), lambda hp, b: (b, 2 * AG + hp)),
                  row,
                  BS((AH, QB, KWIN), lambda hp, b: (hp, 0, 0)), vec, vec, ANY]
        + (carry.in_specs if carry else []),
        out_specs=[ANY, BS((AH, QB, KWIN), lambda hp, b: (hp, 0, 0)), vec, vec] + [ANY] * no,
        out_shape=[_sds((T, NA), BF16), _sds((12, QB, KWIN), F32), _sds((1, AW), F32), _sds((1, AW), F32)]
        + (carry.out_shapes if carry else []),
        input_output_aliases={7: 0},
        scratch_shapes=[pltpu.VMEM((SEQ, AW), BF16), pltpu.VMEM((SEQ + KPAD, AW), BF16),
                        pltpu.VMEM((SEQ + KPAD, AW), BF16), pltpu.VMEM((SEQ, AW), F32),
                        pltpu.VMEM((SEQ + KPAD, AW), F32), pltpu.VMEM((SEQ + KPAD, AW), F32),
                        pltpu.VMEM((3, SEQ, AW), BF16), pltpu.SemaphoreType.DMA((3,))]
        + (carry.scratch if carry else []),
        compiler_params=pltpu.CompilerParams(
            dimension_semantics=("arbitrary", "arbitrary"), vmem_limit_bytes=58 << 20,
            has_side_effects=bool(carry)), name="attn_bwd",
    )(z, z, z, dcat, bias, gq2, gk2, dz_into, *(carry.ins if carry else []))
    return outs[:4], outs[4:]


def _mem_softmax(qh, kn):
    s = lax.dot_general(qh, kn, _DIMS["nt"], preferred_element_type=F32)
    m = jnp.max(s, axis=-1, keepdims=True)
    p = jnp.exp(s - m)
    return p * (1.0 / jnp.sum(p, axis=-1, keepdims=True))


def _memattn_fwd(z, kv, cat, gq4, gk4, qcol, name):
    def body(q_ref, k_ref, v_ref, gq_ref, gk_ref, cat_ref, o_ref):
        del cat_ref
        masks = _group_masks(MEMW)
        qn = (_head_norm(q_ref[...], gq_ref[...], masks) * SCALE).astype(BF16)
        kn = _head_norm(k_ref[...], gk_ref[...], masks).astype(BF16)
        vv = v_ref[...].astype(BF16)
        o = jnp.zeros((TR, MEMW), F32)
        for h in range(4):
            qh = jnp.where(masks[h], qn, jnp.zeros_like(qn))
            vh = jnp.where(masks[h], vv, jnp.zeros_like(vv))
            p = _mem_softmax(qh, kn).astype(BF16)
            o = o + jnp.dot(p, vh, preferred_element_type=F32)
        o_ref[...] = o.astype(BF16)

    nt = SEQ // TR
    vec = BS((1, MEMW), lambda b, t: (0, 0))
    return pl.pallas_call(
        body, grid=(NB, nt),
        in_specs=[BS((TR, MEMW), lambda b, t: (b * nt + t, qcol)),
                  BS((MEMT, MEMW), lambda b, t: (b, 0)),
                  BS((MEMT, MEMW), lambda b, t: (b, 1)), vec, vec, ANY],
        out_specs=BS((TR, MEMW), lambda b, t: (b * nt + t, 3)),
        out_shape=_sds((T, D), BF16), input_output_aliases={5: 0},
        compiler_params=_cp(("arbitrary", "arbitrary")), name=name,
    )(z, kv, kv, gq4, gk4, cat)


def _memattn_bwd(z, kv, dcat, gq4, gk4, qcol, name, dz_cols=MEMW):
    nt = SEQ // TR

    def body(q_ref, k_ref, v_ref, do_ref, gq_ref, gk_ref,
             dq_ref, dkv_ref, dgq_ref, dgk_ref, dkn_s, dv_s):
        b = pl.program_id(0)
        t = pl.program_id(1)
        masks = _group_masks(MEMW)
        qz = q_ref[...]
        kz = k_ref[...]
        qn = (_head_norm(qz, gq_ref[...], masks) * SCALE).astype(BF16)
        kn = _head_norm(kz, gk_ref[...], masks).astype(BF16)
        vv = v_ref[...].astype(BF16)
        dob = do_ref[...].astype(BF16)

        @pl.when(t == 0)
        def _():
            dkn_s[...] = jnp.zeros_like(dkn_s)
            dv_s[...] = jnp.zeros_like(dv_s)

        @pl.when((t == 0) & (b == 0))
        def _():
            dgq_ref[...] = jnp.zeros_like(dgq_ref)
            dgk_ref[...] = jnp.zeros_like(dgk_ref)

        dqn = jnp.zeros((TR, MEMW), F32)
        dkn = jnp.zeros((MEMT, MEMW), F32)
        dvv = jnp.zeros((MEMT, MEMW), F32)
        for h in range(4):
            qh = jnp.where(masks[h], qn, jnp.zeros_like(qn))
            kh = jnp.where(masks[h], kn, jnp.zeros_like(kn))
            doh = jnp.where(masks[h], dob, jnp.zeros_like(dob))
            p = _mem_softmax(qh, kn)
            dvv = dvv + lax.dot_general(p.astype(BF16), doh, _DIMS["tn"], preferred_element_type=F32)
            dp = lax.dot_general(doh, vv, _DIMS["nt"], preferred_element_type=F32)
            ds = p * (dp - jnp.sum(dp * p, axis=-1, keepdims=True))
            dsb = ds.astype(BF16)
            dqn = dqn + jnp.dot(dsb, kh, preferred_element_type=F32)
            dkn = dkn + lax.dot_general(dsb, qh, _DIMS["tn"], preferred_element_type=F32)
        dkn_s[...] += dkn
        dv_s[...] += dvv
        dq, dgq = _head_norm_bwd(dqn * SCALE, qz, gq_ref[...], masks)
        dq_ref[...] = dq.astype(BF16)
        dgq_ref[...] += _fold_heads(dgq, MEMW)

        @pl.when(t == nt - 1)
        def _():
            dk, dgk = _head_norm_bwd(dkn_s[...], kz, gk_ref[...], masks)
            dkv_ref[:, 0:MEMW] = dk
            dkv_ref[:, MEMW:] = dv_s[...]
            dgk_ref[...] += _fold_heads(dgk, MEMW)

    vec = BS((1, MEMW), lambda b, t: (0, 0))
    return pl.pallas_call(
        body, grid=(NB, nt),
        in_specs=[BS((TR, MEMW), lambda b, t: (b * nt + t, qcol)),
                  BS((MEMT, MEMW), lambda b, t: (b, 0)),
                  BS((MEMT, MEMW), lambda b, t: (b, 1)),
                  BS((TR, MEMW), lambda b, t: (b * nt + t, 3)), vec, vec],
        out_specs=[BS((TR, MEMW), lambda b, t: (b * nt + t, dz_cols // MEMW - 1)),
                   BS((MEMT, 2 * MEMW), lambda b, t: (b, 0)), vec, vec],
        out_shape=[_sds((T, dz_cols), BF16), _sds((NB * MEMT, 2 * MEMW), F32),
                   _sds((1, MEMW), F32), _sds((1, MEMW), F32)],
        scratch_shapes=[pltpu.VMEM((MEMT, MEMW), F32), pltpu.VMEM((MEMT, MEMW), F32)],
        compiler_params=_cp(("arbitrary", "arbitrary")), name=name,
    )(z, kv, kv, dcat, gq4, gk4)


HALO = 32
NEXT = 64
RT = 64


def _glu(zz):
    return zz[:, :TOK] * jax.nn.sigmoid(zz[:, TOK:])


def _layer_norm_parts(y):
    mu = jnp.mean(y, axis=-1, keepdims=True)
    yc = y - mu
    rstd = lax.rsqrt(jnp.mean(yc * yc, axis=-1, keepdims=True) + EPS)
    return yc * rstd, rstd


def _shifted_copies(src, dst, rows):
    for b in range(1, 8):
        dst[b - 1, 0:rows, :] = src[b:b + rows, :]


def _tap(src, shifted, off, r0, rows):
    b = off % 8
    if b == 0:
        return src[r0 + off:r0 + off + rows, :]
    return shifted[b - 1, r0 + off - b:r0 + off - b + rows, :]


def _conv_rows(w_ref, hbuf, hs, r0, rows):
    y = jnp.zeros((rows, TOK), F32)
    for j in range(CONVW):
        y = y + w_ref[j:j + 1, :] * _tap(hbuf, hs, (HALO - CONVW + 1) + j, r0, rows)
    return y


def _conv_fwd(z, cw, cb, lg, lb):
    nt = SEQ // TR

    def body(zc_ref, zp_ref, w_ref, cb_ref, lg_ref, lb_ref, o_ref, y_ref, hbuf, hs):
        t = pl.program_id(1)
        hbuf[0:HALO, :] = jnp.where(t == 0, 0.0, _glu(zp_ref[...]))
        hbuf[HALO:, :] = _glu(zc_ref[...])
        _shifted_copies(hbuf, hs, HALO + TR - 8)
        for r0 in range(0, TR, RT):
            y = _conv_rows(w_ref, hbuf, hs, r0, RT) + cb_ref[...]
            y_ref[r0:r0 + RT, :] = y
            yh, _ = _layer_norm_parts(y)
            o = yh * lg_ref[...] + lb_ref[...]
            o_ref[r0:r0 + RT, :] = (o * jax.nn.sigmoid(o)).astype(BF16)

    vec = BS((1, TOK), lambda b, t: (0, 0))
    per = TR // HALO
    return pl.pallas_call(
        body, grid=(NB, nt),
        in_specs=[BS((TR, 2 * TOK), lambda b, t: (b * nt + t, 0)),
                  BS((HALO, 2 * TOK), lambda b, t: (jnp.maximum((b * nt + t) * per - 1, 0), 0)),
                  BS((32, TOK), lambda b, t: (0, 0)), vec, vec, vec],
        out_specs=[BS((TR, TOK), lambda b, t: (b * nt + t, 0)), BS((TR, TOK), lambda b, t: (b * nt + t, 0))],
        out_shape=[_sds((T, D), BF16), _sds((T, TOK), F32)],
        scratch_shapes=[pltpu.VMEM((HALO + TR, TOK), F32), pltpu.VMEM((7, HALO + TR, TOK), F32)],
        compiler_params=_cp(("arbitrary", "arbitrary")), name="conv_fwd",
    )(z, z, cw, cb, lg, lb)


def _conv_bwd(z, y, dcat, dz_into, cw, lg, lb):
    nt = SEQ // TR
    ext = TR + NEXT

    def body(zc_ref, zp_ref, yc_ref, yn_ref, dc_ref, dn_ref, w_ref, lg_ref, lb_ref, into_ref,
             du_ref, dw_ref, dcb_ref, dlg_ref, dlb_ref, dbin_ref, hbuf, dybuf, hs, dys):
        del into_ref
        b = pl.program_id(0)
        t = pl.program_id(1)

        @pl.when((b == 0) & (t == 0))
        def _():
            dw_ref[...] = jnp.zeros_like(dw_ref)
            dcb_ref[...] = jnp.zeros_like(dcb_ref)
            dlg_ref[...] = jnp.zeros_like(dlg_ref)
            dlb_ref[...] = jnp.zeros_like(dlb_ref)
            dbin_ref[...] = jnp.zeros_like(dbin_ref)

        hbuf[0:HALO, :] = jnp.where(t == 0, 0.0, _glu(zp_ref[...]))
        hbuf[HALO:, :] = _glu(zc_ref[...])
        _shifted_copies(hbuf, hs, HALO + TR - 8)
        last = t == nt - 1
        for r0 in range(0, ext, RT):
            yh, rstd = _layer_norm_parts(yc_ref[r0:r0 + RT, :] if r0 < TR else yn_ref[...])
            o = yh * lg_ref[...] + lb_ref[...]
            sg = jax.nn.sigmoid(o)
            if r0 < TR:
                dtok = dc_ref[r0:r0 + RT, :]
            else:
                dtok = jnp.where(last, 0.0, dn_ref[...])
            do = dtok * (sg * (1.0 + o * (1.0 - sg)))
            dyh = do * lg_ref[...]
            dy = rstd * (dyh - jnp.mean(dyh, axis=-1, keepdims=True)
                         - yh * jnp.mean(dyh * yh, axis=-1, keepdims=True))
            dybuf[r0:r0 + RT, :] = dy
            if r0 < TR:
                dlg_ref[...] += jnp.sum(do * yh, axis=0, keepdims=True)
                dlb_ref[...] += jnp.sum(do, axis=0, keepdims=True)
                dcb_ref[...] += jnp.sum(dy, axis=0, keepdims=True)
        _shifted_copies(dybuf, dys, ext - 8)
        for r0 in range(0, TR, RT):
            dh = jnp.zeros((RT, TOK), F32)
            for j in range(CONVW):
                dh = dh + w_ref[j:j + 1, :] * _tap(dybuf, dys, (CONVW - 1) - j, r0, RT)
            a = zc_ref[r0:r0 + RT, 0:TOK]
            sg = jax.nn.sigmoid(zc_ref[r0:r0 + RT, TOK:])
            da = dh * sg
            dg = dh * a * (sg * (1.0 - sg))
            du_ref[r0:r0 + RT, 0:TOK] = da.astype(BF16)
            du_ref[r0:r0 + RT, TOK:] = dg.astype(BF16)
            dbin_ref[:, 0:TOK] += jnp.sum(da, axis=0, keepdims=True)
            dbin_ref[:, TOK:] += jnp.sum(dg, axis=0, keepdims=True)
        for j in range(CONVW):
            acc = jnp.zeros((8, TOK), F32)
            for r0 in range(0, TR, RT):
                prod = dybuf[r0:r0 + RT, :] * _tap(hbuf, hs, (HALO - CONVW + 1) + j, r0, RT)
                acc = acc + jnp.sum(prod.reshape(RT // 8, 8, TOK), axis=0)
            dw_ref[j:j + 1, :] += jnp.sum(acc, axis=0, keepdims=True)

    vec = BS((1, TOK), lambda b, t: (0, 0))
    perh = TR // HALO
    pern = TR // NEXT
    nlast_n = T // NEXT - 1
    return pl.pallas_call(
        body, grid=(NB, nt),
        in_specs=[BS((TR, 2 * TOK), lambda b, t: (b * nt + t, 0)),
                  BS((HALO, 2 * TOK), lambda b, t: (jnp.maximum((b * nt + t) * perh - 1, 0), 0)),
                  BS((TR, TOK), lambda b, t: (b * nt + t, 0)),
                  BS((NEXT, TOK), lambda b, t: (jnp.minimum((b * nt + t + 1) * pern, nlast_n), 0)),
                  BS((TR, TOK), lambda b, t: (b * nt + t, 0)),
                  BS((NEXT, TOK), lambda b, t: (jnp.minimum((b * nt + t + 1) * pern, nlast_n), 0)),
                  BS((32, TOK), lambda b, t: (0, 0)), vec, vec, ANY],
        out_specs=[BS((TR, 2 * TOK), lambda b, t: (b * nt + t, 0)),
                   BS((32, TOK), lambda b, t: (0, 0)), vec, vec, vec,
                   BS((1, 2 * TOK), lambda b, t: (0, 0))],
        out_shape=[_sds((T, NBW), BF16), _sds((32, TOK), F32), _sds((1, TOK), F32),
                   _sds((1, TOK), F32), _sds((1, TOK), F32), _sds((1, 2 * TOK), F32)],
        input_output_aliases={9: 0},
        scratch_shapes=[pltpu.VMEM((HALO + TR, TOK), F32), pltpu.VMEM((ext, TOK), F32),
                        pltpu.VMEM((7, HALO + TR, TOK), F32), pltpu.VMEM((7, ext, TOK), F32)],
        compiler_params=_cp(("arbitrary", "arbitrary"), vmem_mb=56), name="conv_bwd",
    )(z, z, y, y, dcat, dcat, cw, lg, lb, dz_into)


def _ffn_up(h2, wgu, l, carry=None):
    ni = len(carry.ins) if carry else 0
    no = len(carry.out_shapes) if carry else 0

    def body(h_ref, wg_ref, wu_ref, *rest):
        cin = rest[:ni]
        g_ref, u_ref, a_ref = rest[ni:ni + 3]
        cout, cscr = rest[ni + 3:ni + 3 + no], rest[ni + 3 + no:]
        if carry:
            @pl.when((pl.program_id(0) == 0) & (pl.program_id(1) == 0))
            def _():
                carry.start(cin, cout, cscr)

        hv = h_ref[...]
        g = lax.dot_general(hv, wg_ref[...], _DIMS["nt"], preferred_element_type=F32)
        u = lax.dot_general(hv, wu_ref[...], _DIMS["nt"], preferred_element_type=F32)
        sg = jax.nn.sigmoid(g)
        silu = g * sg
        g_ref[...] = (u * (sg * (1.0 + g * (1.0 - sg)))).astype(BF16)
        u_ref[...] = silu.astype(BF16)
        a_ref[...] = (silu * u).astype(BF16)

        if carry:
            @pl.when((pl.program_id(0) == FF // FT - 1) & (pl.program_id(1) == T // TR - 1))
            def _():
                carry.finish(cin, cout, cscr)

    out = BS((TR, FT), lambda q, i: (i, q))
    outs = pl.pallas_call(
        body, grid=(FF // FT, T // TR),
        in_specs=[BS((TR, D), lambda q, i: (i, 0)),
                  BS((None, FT, D), lambda q, i: (0, q, 0)),
                  BS((None, FT, D), lambda q, i: (1, q, 0))] + (carry.in_specs if carry else []),
        out_specs=[out, out, out] + [ANY] * no,
        out_shape=[_sds((T, FF), BF16), _sds((T, FF), BF16), _sds((T, FF), BF16)]
        + (carry.out_shapes if carry else []),
        scratch_shapes=carry.scratch if carry else [],
        compiler_params=pltpu.CompilerParams(
            dimension_semantics=("arbitrary", "arbitrary"), vmem_limit_bytes=48 << 20,
            has_side_effects=bool(carry)), name=f"ffn_up_{l}",
    )(h2, wgu, wgu, *(carry.ins if carry else []))
    return outs[:3], outs[3:]


def _ffn_down_bwd(dx, wd, g, u, l):
    def epilogue(dact, ex, o_ref, i):
        o_ref[0] = (dact * ex[0][...].astype(F32)).astype(BF16)
        o_ref[1] = (dact * ex[1][...].astype(F32)).astype(BF16)

    ex_spec = BS((TR, FT), lambda i, q, k: (i, q))
    return _mm("nt", dx, wd, grid=(T // TR, FF // FT, 1),
               a_spec=BS((TR, D), lambda i, q, k: (i, 0)),
               b_spec=BS((FT, D), lambda i, q, k: (q, 0)),
               out_shape=_sds((2, T, FF), BF16),
               out_spec=BS((2, TR, FT), lambda i, q, k: (0, i, q)),
               acc_shape=(TR, FT), extras=(g, u), extra_specs=(ex_spec, ex_spec),
               epilogue=epilogue, name=f"ffn_down_bwd_{l}")


def _row_tile(rows, cols, itemsize=4, limit=2 << 20):
    tr = rows
    while tr * cols * itemsize > limit and tr % 2 == 0 and (tr // 2) % 16 == 0:
        tr //= 2
    return tr


def _cast_bf16(arrs, name):
    n = len(arrs)
    rows, cols = arrs[0].shape
    tr = _row_tile(rows, cols)

    def body(*refs):
        o_ref = refs[n]
        k = pl.program_id(0)
        val = refs[0][...]
        for j in range(1, n):
            val = jnp.where(k == j, refs[j][...], val)
        o_ref[...] = val.astype(BF16)

    return pl.pallas_call(
        body, grid=(n, rows // tr),
        in_specs=[BS((tr, cols), lambda k, i: (i, 0))] * n,
        out_specs=BS((None, tr, cols), lambda k, i: (k, i, 0)),
        out_shape=_sds((n, rows, cols), BF16),
        compiler_params=_cp(("arbitrary", "arbitrary")), name=name,
    )(*arrs)


def _quad_sum(own, got, name):
    n, rows, cols = own.shape
    tr = _row_tile(rows, cols)

    def body(a_ref, q_ref, o_ref):
        o_ref[...] = ((a_ref[...] + q_ref[0].astype(F32)) + q_ref[1].astype(F32)) + q_ref[2].astype(F32)

    spec = BS((None, tr, cols), lambda k, i: (k, i, 0))
    return pl.pallas_call(
        body, grid=(n, rows // tr),
        in_specs=[spec, BS((3, None, tr, cols), lambda k, i: (0, k, i, 0))], out_specs=spec,
        out_shape=_sds((n, rows, cols), F32),
        compiler_params=_cp(("arbitrary", "arbitrary")), name=name,
    )(own, got)


def _adam_math(w, g, m, v):
    m = ADAM_B1 * m + (1.0 - ADAM_B1) * g
    v = ADAM_B2 * v + (1.0 - ADAM_B2) * (g * g)
    m_hat = m / (1.0 - ADAM_B1 ** ADAM_STEP)
    v_hat = v / (1.0 - ADAM_B2 ** ADAM_STEP)
    delta = -ADAM_LR * (m_hat / (jnp.sqrt(v_hat) + ADAM_EPS) + ADAM_WD * w)
    return delta, m, v


def _adamw_big(w, g, m, v, name):
    shape = w.shape
    cols = shape[-1]
    rows = w.size // cols
    tr = _row_tile(rows, cols, limit=1 << 20)

    def body(w_ref, g_ref, m_ref, v_ref, d_ref, nm_ref, nv_ref):
        d, nm, nv = _adam_math(w_ref[...], g_ref[...], m_ref[...], v_ref[...])
        d_ref[...] = d
        nm_ref[...] = nm
        nv_ref[...] = nv

    spec = BS((tr, cols), lambda i: (i, 0))
    outs = pl.pallas_call(
        body, grid=(rows // tr,), in_specs=[spec] * 4, out_specs=[spec] * 3,
        out_shape=[_sds((rows, cols), F32)] * 3,
        compiler_params=_cp(("arbitrary",)), name=name,
    )(*[a.reshape(rows, cols) for a in (w, g, m, v)])
    return [o.reshape(shape) for o in outs]


def _adamw_small(ws, gs, ms, vs):
    n = len(ws)

    def body(*refs):
        for i in range(n):
            d, nm, nv = _adam_math(refs[i][...], refs[n + i][...], refs[2 * n + i][...],
                                   refs[3 * n + i][...])
            refs[4 * n + i][...] = d
            refs[5 * n + i][...] = nm
            refs[6 * n + i][...] = nv

    specs = [BS(w.shape, lambda i: (0, 0)) for w in ws]
    outs = pl.pallas_call(
        body, grid=(1,), in_specs=specs * 4, out_specs=specs * 3,
        out_shape=[_sds(w.shape, F32) for w in ws] * 3,
        compiler_params=_cp(("arbitrary",)), name="adamw_small",
    )(*ws, *gs, *ms, *vs)
    return outs[:n], outs[n:2 * n], outs[2 * n:]


def _place():
    x, y, c = lax.axis_index("x"), lax.axis_index("y"), lax.axis_index("c")
    chips = [(1 - x, y), (x, 1 - y), (1 - x, 1 - y)]
    return x, y, c, chips


class _Exchange:
    def __init__(self, ins, in_specs, out_shapes, scratch, start, finish):
        self.ins, self.in_specs, self.out_shapes, self.scratch = ins, in_specs, out_shapes, scratch
        self.start, self.finish = start, finish


def _run_exchange(ex, name, vmem_mb=40):
    ni, no = len(ex.ins), len(ex.out_shapes)

    def body(*refs):
        ex.start(refs[:ni], refs[ni:ni + no], refs[ni + no:])
        ex.finish(refs[:ni], refs[ni:ni + no], refs[ni + no:])

    return pl.pallas_call(
        body, in_specs=ex.in_specs, out_specs=[ANY] * no, out_shape=ex.out_shapes,
        scratch_shapes=ex.scratch,
        compiler_params=pltpu.CompilerParams(has_side_effects=True, vmem_limit_bytes=vmem_mb << 20),
        name=name,
    )(*ex.ins)


def _gather_exchange(srcs, dst_shapes, views, small=None):
    nu = len(srcs)
    nd = len(dst_shapes)
    ns = 1 if small is not None else 0

    def unpack(ins, outs, scr):
        x, y, c, chips = _place()
        src = ins[:nu]
        vw = [views[u](outs[:nd]) for u in range(nu)]
        vbuf = scr[:nu]
        send, recv, fsend, frecv, lsem, ssend, srecv, vsem = scr[nu:]

        def ici(u, j, shard, to):
            return pltpu.make_async_remote_copy(
                src_ref=vbuf[u].at[:, c], dst_ref=vw[u].at[:, shard, c],
                send_sem=send.at[3 * u + j], recv_sem=recv.at[3 * u + j],
                device_id=to, device_id_type=MESH)

        def fwd(u, j, shard, half):
            return pltpu.make_async_remote_copy(
                src_ref=vw[u].at[:, shard, half], dst_ref=vw[u].at[:, shard, half],
                send_sem=fsend.at[3 * u + j], recv_sem=frecv.at[3 * u + j],
                device_id=(x, y, 1 - c), device_id_type=MESH)

        def small_copy(j, shard, to):
            return pltpu.make_async_remote_copy(
                src_ref=ins[nu], dst_ref=outs[nd].at[shard],
                send_sem=ssend.at[j], recv_sem=srecv.at[j], device_id=to, device_id_type=MESH)

        stage = [pltpu.make_async_copy(src[u], vbuf[u], vsem.at[u]) for u in range(nu)]
        local = [pltpu.make_async_copy(vbuf[u], vw[u].at[:, 2 * x + y], lsem.at[u]) for u in range(nu)]
        if ns:
            local.append(pltpu.make_async_copy(ins[nu], outs[nd].at[2 * x + y], lsem.at[nu]))
        return x, y, c, chips, ici, fwd, small_copy, stage, local

    def start(ins, outs, scr):
        x, y, c, chips, ici, fwd, small_copy, stage, local = unpack(ins, outs, scr)
        s = 2 * x + y
        for cp in stage:
            cp.start()
        if ns:
            local[nu].start()
            for j, chip in enumerate(chips):
                small_copy(j, s, (*chip, c)).start()
        for u in range(nu):
            stage[u].wait()
            for j, chip in enumerate(chips):
                ici(u, j, s, (*chip, c)).start()
            local[u].start()

    def finish(ins, outs, scr):
        x, y, c, chips, ici, fwd, small_copy, stage, local = unpack(ins, outs, scr)
        s = 2 * x + y
        for u in range(nu):
            for j, chip in enumerate(chips):
                sj = 2 * chip[0] + chip[1]
                ici(u, j, sj, (x, y, c)).wait_recv()
                fwd(u, j, sj, c).start()
        for u in range(nu):
            for j, chip in enumerate(chips):
                sj = 2 * chip[0] + chip[1]
                fwd(u, j, sj, 1 - c).wait_recv()
        for u in range(nu):
            for j, chip in enumerate(chips):
                ici(u, j, s, (*chip, c)).wait_send()
                fwd(u, j, s, c).wait_send()
        if ns:
            for j, chip in enumerate(chips):
                small_copy(j, 2 * chip[0] + chip[1], (x, y, c)).wait_recv()
                small_copy(j, s, (*chip, c)).wait_send()
        for cp in local:
            cp.wait()

    dma = pltpu.SemaphoreType.DMA
    return _Exchange(
        ins=list(srcs) + ([small] if ns else []),
        in_specs=[ANY] * nu + [BS(memory_space=pltpu.VMEM)] * ns,
        out_shapes=[_sds(sh, BF16) for sh in dst_shapes]
        + ([_sds((NSH,) + small.shape, F32)] if ns else []),
        scratch=[pltpu.VMEM(a.shape, BF16) for a in srcs]
        + [dma((3 * nu,)), dma((3 * nu,)), dma((3 * nu,)), dma((3 * nu,)),
           dma((nu + 1,)), dma((3,)), dma((3,)), dma((nu,))],
        start=start, finish=finish)


def _pair_reduce(gs, name):
    nu = len(gs)
    ns = [g.shape[0] * NSH for g in gs]
    base = [sum(ns[:u]) for u in range(nu)]

    def body(*refs):
        g_refs, own_refs, sb_refs = refs[:nu], refs[nu:2 * nu], refs[2 * nu:3 * nu]
        bufs = refs[3 * nu:8 * nu]
        send, recv, lsem, osem = refs[8 * nu:]
        x, y, c, _ = _place()
        s = 2 * x + y

        def unit(u):
            sendb, recvb, stage, outf, outb = bufs[5 * u:5 * u + 5]

            def load(k, half):
                return pltpu.make_async_copy(g_refs[u].at[k // NSH, k % NSH, half], stage.at[k % 3],
                                             lsem.at[3 * u + k % 3])

            def push(k):
                return pltpu.make_async_remote_copy(
                    src_ref=sendb.at[k], dst_ref=recvb.at[k], send_sem=send.at[base[u] + k],
                    recv_sem=recv.at[base[u] + k], device_id=(x, y, 1 - c), device_id_type=MESH)

            def store(k):
                return pltpu.make_async_copy(outb.at[k % 2], sb_refs[u].at[k // NSH, k % NSH],
                                             osem.at[3 * u + k % 2])

            return sendb, recvb, stage, outf, outb, load, push, store

        for u in range(nu):
            sendb, recvb, stage, outf, outb, load, push, store = unit(u)
            for k in range(min(2, ns[u])):
                load(k, 1 - c).start()
            for k in range(ns[u]):
                if k + 2 < ns[u]:
                    load(k + 2, 1 - c).start()
                load(k, 1 - c).wait()
                sendb[k] = stage[k % 3].astype(BF16)
                push(k).start()
        for u in range(nu):
            sendb, recvb, stage, outf, outb, load, push, store = unit(u)
            n = ns[u]
            for k in range(min(2, n)):
                load(k, c).start()
            for k in range(n):
                if k + 2 < n:
                    load(k + 2, c).start()
                load(k, c).wait()
                push(k).wait_recv()
                total = stage[k % 3] + recvb[k].astype(F32)
                if k >= 2:
                    store(k - 2).wait()
                outb[k % 2] = total.astype(BF16)
                store(k).start()

                @pl.when(s == k % NSH)
                def _():
                    outf[...] = total
                    keep = pltpu.make_async_copy(outf, own_refs[u].at[k // NSH], osem.at[3 * u + 2])
                    keep.start()
                    keep.wait()

            for k in range(max(n - 2, 0), n):
                store(k).wait()
        for u in range(nu):
            push = unit(u)[6]
            for k in range(ns[u]):
                push(k).wait_send()

    dma = pltpu.SemaphoreType.DMA
    scratch = []
    for g, n in zip(gs, ns):
        rh, cc = g.shape[3], g.shape[4]
        scratch += [pltpu.VMEM((n, rh, cc), BF16), pltpu.VMEM((n, rh, cc), BF16),
                    pltpu.VMEM((3, rh, cc), F32), pltpu.VMEM((rh, cc), F32), pltpu.VMEM((2, rh, cc), BF16)]
    outs = pl.pallas_call(
        body, in_specs=[ANY] * nu, out_specs=[ANY] * (2 * nu),
        out_shape=[_sds((g.shape[0], g.shape[3], g.shape[4]), F32) for g in gs]
        + [_sds((g.shape[0], NSH, g.shape[3], g.shape[4]), BF16) for g in gs],
        scratch_shapes=scratch + [dma((sum(ns),)), dma((sum(ns),)), dma((3 * nu,)), dma((3 * nu,))],
        compiler_params=pltpu.CompilerParams(has_side_effects=True, vmem_limit_bytes=56 << 20),
        name=name,
    )(*gs)
    return list(outs[:nu]), list(outs[nu:])


def _chip_exchange(sums_bf16):
    nu = len(sums_bf16)

    def pushes(ins, outs, scr):
        x, y, c, chips = _place()
        send, recv = scr
        return [pltpu.make_async_remote_copy(
            src_ref=ins[u].at[:, 2 * chip[0] + chip[1]], dst_ref=outs[u].at[j],
            send_sem=send.at[3 * u + j], recv_sem=recv.at[3 * u + j],
            device_id=(*chip, c), device_id_type=MESH)
            for u in range(nu) for j, chip in enumerate(chips)]

    def start(ins, outs, scr):
        for cp in pushes(ins, outs, scr):
            cp.start()

    def finish(ins, outs, scr):
        for cp in pushes(ins, outs, scr):
            cp.wait()

    dma = pltpu.SemaphoreType.DMA
    shapes = [(3, a.shape[0], a.shape[2], a.shape[3]) for a in sums_bf16]
    return _Exchange(ins=list(sums_bf16), in_specs=[ANY] * nu,
                     out_shapes=[_sds(sh, BF16) for sh in shapes],
                     scratch=[dma((3 * nu,)), dma((3 * nu,))], start=start, finish=finish)


def _final_exchange(halves, out_shapes, targets):
    nu = len(halves)
    no = len(out_shapes)
    ncp = sum(len(t) for t in targets)

    def body(*refs):
        hv = refs[:nu]
        out = refs[nu:nu + no]
        sbuf = refs[nu + no:2 * nu + no]
        rbuf = refs[2 * nu + no:3 * nu + no]
        send, recv, lsem, osem, csem = refs[3 * nu + no:]
        x, y, c, _ = _place()
        stage = [pltpu.make_async_copy(hv[u], sbuf[u], lsem.at[u]) for u in range(nu)]
        push = [pltpu.make_async_remote_copy(
            src_ref=sbuf[u], dst_ref=rbuf[u], send_sem=send.at[u], recv_sem=recv.at[u],
            device_id=(x, y, 1 - c), device_id_type=MESH) for u in range(nu)]
        mine, theirs = [], []
        k = 0
        for u in range(nu):
            rh = hv[u].shape[1]
            for (p, oi, li) in targets[u]:
                mine.append((u, pltpu.make_async_copy(
                    sbuf[u].at[p], out[oi].at[li, pl.ds(c * rh, rh), :], csem.at[k])))
                theirs.append((u, pltpu.make_async_copy(
                    rbuf[u].at[p], out[oi].at[li, pl.ds((1 - c) * rh, rh), :], osem.at[k])))
                k += 1
        for cp in stage:
            cp.start()
        for u in range(nu):
            stage[u].wait()
            push[u].start()
            for (v, cp) in mine:
                if v == u:
                    cp.start()
        for u in range(nu):
            push[u].wait_recv()
            for (v, cp) in theirs:
                if v == u:
                    cp.start()
        for (_, cp) in theirs + mine:
            cp.wait()
        for u in range(nu):
            push[u].wait_send()

    dma = pltpu.SemaphoreType.DMA
    bufs = [pltpu.VMEM(h.shape, F32) for h in halves]
    return pl.pallas_call(
        body, in_specs=[ANY] * nu, out_specs=[ANY] * no,
        out_shape=[_sds(sh, F32) for sh in out_shapes],
        scratch_shapes=bufs + bufs + [dma((nu,)), dma((nu,)), dma((nu,)), dma((ncp,)), dma((ncp,))],
        compiler_params=pltpu.CompilerParams(has_side_effects=True, vmem_limit_bytes=56 << 20),
        name="final_exchange",
    )(*halves)


def _small_allreduce(pack):
    rows = pack.shape[0]

    def body(p_ref, o_ref, buf, send, recv):
        x, y, c, _ = _place()
        me = 4 * x + 2 * y + c
        buf[me] = p_ref[...]
        k = 0
        copies = []
        for dx in range(2):
            for dy in range(2):
                for dc in range(2):
                    if dx == 0 and dy == 0 and dc == 0:
                        continue
                    to = (jnp.where(dx, 1 - x, x), jnp.where(dy, 1 - y, y), jnp.where(dc, 1 - c, c))
                    src_slot = 4 * to[0] + 2 * to[1] + to[2]
                    copies.append((pltpu.make_async_remote_copy(
                        src_ref=p_ref, dst_ref=buf.at[me], send_sem=send.at[k], recv_sem=recv.at[k],
                        device_id=to, device_id_type=MESH), src_slot, k))
                    k += 1
        for cp, _, _ in copies:
            cp.start()
        for cp, src_slot, k in copies:
            pltpu.make_async_remote_copy(
                src_ref=p_ref, dst_ref=buf.at[src_slot], send_sem=send.at[k], recv_sem=recv.at[k],
                device_id=(x, y, c), device_id_type=MESH).wait()
        acc = buf[0]
        for d in range(1, 8):
            acc = acc + buf[d]
        o_ref[...] = acc

    dma = pltpu.SemaphoreType.DMA
    vm = BS(memory_space=pltpu.VMEM)
    return pl.pallas_call(
        body, in_specs=[vm], out_specs=vm, out_shape=_sds((rows, D), F32),
        scratch_shapes=[pltpu.VMEM((8, rows, D), F32), dma((7,)), dma((7,))],
        compiler_params=pltpu.CompilerParams(has_side_effects=True, vmem_limit_bytes=32 << 20),
        name="small_allreduce",
    )(pack)


def _in_proj(h, w, bias, name, transposed=False, carry=None):
    n = w.shape[0] if transposed else w.shape[1]
    tn = 1280 if n == NA else 1792
    ep = None
    extras, especs = (), ()
    if bias is not None:
        def ep(acc, ex, o_ref, i):
            o_ref[...] = acc + ex[0][...]
        extras = (bias,)
        especs = (BS((1, tn), lambda i, j, k: (0, j)),)
    b_spec = BS((tn, D), lambda i, j, k: (j, 0)) if transposed else BS((D, tn), lambda i, j, k: (0, j))
    return _mm("nt" if transposed else "nn", h, w, grid=(T // TR, n // tn, 1),
               a_spec=BS((TR, D), lambda i, j, k: (i, 0)), b_spec=b_spec,
               out_shape=_sds((T, n), F32), out_spec=BS((TR, tn), lambda i, j, k: (i, j)),
               acc_shape=(TR, tn), extras=extras, extra_specs=especs, epilogue=ep, name=name, carry=carry)


def _res_rms_epilogue(acc, ex, outs, i):
    y = acc + ex[0][...]
    outs[0][...] = y
    r = lax.rsqrt(jnp.mean(y * y, axis=-1, keepdims=True) + EPS)
    outs[1][...] = (y * r * ex[1][...]).astype(BF16)


def _res_loss_epilogue(acc, ex, outs, i):
    e = acc + ex[0][...] - ex[1][...]
    outs[1][...] = e * (1.0 / D)

    @pl.when(i == 0)
    def _():
        outs[0][...] = jnp.zeros_like(outs[0])

    outs[0][...] += 0.5 * jnp.sum(jnp.mean(e * e, axis=-1, keepdims=True), axis=0, keepdims=True)


def _rms_bwd_epilogue(dh, ex, outs, i):
    xv = ex[0][...]
    r = lax.rsqrt(jnp.mean(xv * xv, axis=-1, keepdims=True) + EPS)
    xh = xv * r
    gy = dh * ex[1][...]
    outs[0][...] = r * (gy - xh * jnp.mean(gy * xh, axis=-1, keepdims=True)) + ex[2][...]

    @pl.when(i == 0)
    def _():
        outs[1][...] = jnp.zeros_like(outs[1])

    outs[1][...] += jnp.sum(dh * xh, axis=0, keepdims=True)


def _local_step(x, mem, target, w, p, carries=None, bwd_carry_fn=None, late_carry_fn=None):
    row = lambda i, j, k: (i, 0)
    whole = lambda i, j, k: (0, 0)
    w = {k: (list(v) if isinstance(v, list) else v) for k, v in w.items()}
    carries = carries or {}

    def carry_of(name):
        return carries[name][0] if name in carries else None

    def delivered(name, outs):
        if name in carries:
            carries[name][1](w, outs)

    saved = []
    bias = _bias_expand(p["rel_u"])
    vec = BS((1, D), whole)
    h = _rms_fwd(x, p["norm1_g"][0:1], "rms1_0", carry=carry_of("rms1_0"))
    if carry_of("rms1_0") is not None:
        h, carried = h
        delivered("rms1_0", carried)
    for l in range(2):
        type_a = l == 0
        memn = _rms_fwd(mem, p["mem_norm_g"][l:l + 1], f"rmsmem_{l}")
        if type_a:
            z = _in_proj(h, w["a"], None, "inproj_a", carry=carry_of("inproj_a"))
            if carry_of("inproj_a") is not None:
                z, carried = z
                delivered("inproj_a", carried)
            cat, carried = _attn_fwd(z, bias, p["a_q_g2"], p["a_k_g2"], carry_of("attn_fwd"))
            delivered("attn_fwd", carried)
            qcol = NA // MEMW - 1
        else:
            z = _in_proj(h, w["b"], w["b_b_in"], "inproj_b", transposed=True)
            cat, conv_y = _conv_fwd(z, w["conv_w"], w["conv_b"], w["ln_g"], w["ln_b"])
            qcol = NBW // MEMW - 1
        kv = _mm("nn", memn, w["kv"][l], grid=(1, 1, 1),
                 a_spec=BS((NB * MEMT, D), whole), b_spec=BS((D, 2 * MEMW), whole),
                 out_shape=_sds((NB * MEMT, 2 * MEMW), F32), out_spec=BS((NB * MEMT, 2 * MEMW), whole),
                 acc_shape=(8, 128), name=f"memkv_{l}")
        cat = _memattn_fwd(z, kv, cat, p["mq_g4"][l:l + 1], p["mk_g4"][l:l + 1], qcol, f"memattn_fwd_{l}")
        x1, h2 = _mm("nn", cat, w["wo"][l], grid=(T // TR, 1, 1), a_spec=BS((TR, D), row),
                     b_spec=BS((D, D), whole),
                     out_shape=[_sds((T, D), F32), _sds((T, D), BF16)],
                     out_spec=[BS((TR, D), row), BS((TR, D), row)], acc_shape=(8, 128),
                     extras=(x, p["norm2_g"][l:l + 1]), extra_specs=(BS((TR, D), row), vec),
                     epilogue=_res_rms_epilogue, name=f"outproj_{l}")
        (g, u, act), carried = _ffn_up(h2, w["gu"][l], l, carry_of(f"ffn_up_{l}"))
        delivered(f"ffn_up_{l}", carried)
        last = l == 1
        res = _mm("nn", act, w["wd"][l], grid=(T // TR, 1, 1),
                  a_spec=BS((TR, FF), row), b_spec=BS((FF, D), whole),
                  out_shape=[_sds((1, 1), F32), _sds((T, D), F32)] if last else
                  [_sds((T, D), F32), _sds((T, D), BF16)],
                  out_spec=[BS((1, 1), whole), BS((TR, D), row)] if last else
                  [BS((TR, D), row), BS((TR, D), row)],
                  acc_shape=(8, 128), extras=(x1, target if last else p["norm1_g"][1:2]),
                  extra_specs=(BS((TR, D), row), BS((TR, D), row) if last else vec),
                  epilogue=_res_loss_epilogue if last else _res_rms_epilogue, sequential=last,
                  name=f"ffn_down_{l}", carry=carry_of(f"ffn_down_{l}"))
        if carry_of(f"ffn_down_{l}") is not None:
            res, carried = res
            delivered(f"ffn_down_{l}", carried)
        saved.append(dict(x=x, h=h, memn=memn, kv=kv, z=z, cat=cat, x1=x1, h2=h2, g=g, u=u, act=act,
                          qcol=qcol))
        if last:
            loss, dx = res
        else:
            x, h = res

    big = dict(a=None, b=None, kv=[None, None], wo=[None, None], gu=[None, None], wd=[None, None])
    small = {}
    bwd_carried, late_carried = (), ()
    tk = T // 2
    nkt = T // tk
    for l in (1, 0):
        sv = saved[l]
        dgu = _ffn_down_bwd(dx, w["wd"][l], sv["g"], sv["u"], l)
        big["wd"][l] = _mm("tn", sv["act"], dx, grid=(FF // FT, 1, 2 * nkt),
                           a_spec=BS((tk // 2, FT), lambda i, j, k: (k, i)),
                           b_spec=BS((tk // 2, D), lambda i, j, k: (k, 0)),
                           out_shape=_sds((FF, D), F32), out_spec=BS((FT, D), lambda i, j, k: (i, 0)),
                           acc_shape=(FT, D), name=f"dw_down_{l}")
        dx1, small[f"norm2_g{l}"] = _mm(
            "nn", dgu, w["gu"][l], grid=(T // TR, 1, 2),
            a_spec=BS((None, TR, FF), lambda i, j, k: (k, i, 0)),
            b_spec=BS((None, FF, D), lambda i, j, k: (k, 0, 0)),
            out_shape=[_sds((T, D), F32), _sds((1, D), F32)], out_spec=[BS((TR, D), row), vec],
            acc_shape=(TR, D), extras=(sv["x1"], p["norm2_g"][l:l + 1], dx),
            extra_specs=(BS((TR, D), row), vec, BS((TR, D), row)),
            epilogue=_rms_bwd_epilogue, sequential=True, name=f"dh2_{l}")
        big["gu"][l] = _mm("tn", dgu, sv["h2"], grid=(2 * FF // FT, 1, nkt),
                           a_spec=BS((None, tk, FT), lambda i, j, k: (i // 2, k, i % 2)),
                           b_spec=BS((tk, D), lambda i, j, k: (k, 0)),
                           out_shape=_sds((2, FF, D), F32),
                           out_spec=BS((None, FT, D), lambda i, j, k: (i // 2, i % 2, 0)),
                           acc_shape=(FT, D), vmem_mb=58, name=f"dw_gu_{l}")
        dcat = _mm("nt", dx1, w["wo"][l], grid=(T // TR, 1, 1), a_spec=BS((TR, D), row),
                   b_spec=BS((D, D), whole),
                   out_shape=_sds((T, D), F32), out_spec=BS((TR, D), row), acc_shape=(8, 128),
                   name=f"dcat_{l}")
        big["wo"][l] = _mm("tn", sv["cat"], dx1, grid=(1, 1, nkt),
                           a_spec=BS((tk, D), lambda i, j, k: (k, 0)), b_spec=BS((tk, D), lambda i, j, k: (k, 0)),
                           out_shape=_sds((D, D), F32), out_spec=BS((D, D), whole),
                           acc_shape=(D, D), name=f"dw_out_{l}")
        dqm, dkv, small[f"mq_g{l}"], small[f"mk_g{l}"] = _memattn_bwd(
            sv["z"], sv["kv"], dcat, p["mq_g4"][l:l + 1], p["mk_g4"][l:l + 1], sv["qcol"], f"memattn_bwd_{l}",
            dz_cols=NA if l == 0 else NBW)
        big["kv"][l] = _mm("tn", sv["memn"], dkv, grid=(1, 1, 1),
                           a_spec=BS((NB * MEMT, D), whole), b_spec=BS((NB * MEMT, 2 * MEMW), whole),
                           out_shape=_sds((D, 2 * MEMW), F32), out_spec=BS((D, 2 * MEMW), whole),
                           acc_shape=(8, 128), name=f"dw_kv_{l}")
        dmemn = _mm("nt", dkv, w["kv"][l], grid=(1, 1, 1),
                    a_spec=BS((NB * MEMT, 2 * MEMW), whole), b_spec=BS((D, 2 * MEMW), whole),
                    out_shape=_sds((NB * MEMT, D), F32), out_spec=BS((NB * MEMT, D), whole),
                    acc_shape=(8, 128), name=f"dmemn_{l}")
        _, small[f"mem_norm_g{l}"] = _rms_bwd(dmemn, mem, p["mem_norm_g"][l:l + 1], None, f"rmsmem_bwd_{l}")
        if l == 0:
            carry = bwd_carry_fn(big) if bwd_carry_fn is not None else None
            (dz, dbias, small["a_q_g"], small["a_k_g"]), bwd_carried = _attn_bwd(
                sv["z"], dcat, dqm, bias, p["a_q_g2"], p["a_k_g2"], carry)
            small["rel_u"] = _bias_reduce(dbias)
            w_in, key, n, tn = w["a"], "a", NA, 1280
        else:
            dz, small["conv_w"], small["conv_b"], small["ln_g"], small["ln_b"], dbin_u = _conv_bwd(
                sv["z"], conv_y, dcat, dqm, w["conv_w"], w["ln_g"], w["ln_b"])
            small["b_in_u"] = dbin_u
            small["b_in_qm"] = _colsum(dz, "colsum_dqm", cols=MEMW, block=NBW // MEMW - 1)
            w_in, key, n, tn = w["b"], "b", NBW, 896
        norm_bwd = dict(out_shape=[_sds((T, D), F32), _sds((1, D), F32)], out_spec=[BS((TR, D), row), vec],
                        acc_shape=(8, 128), extras=(sv["x"], p["norm1_g"][l:l + 1], dx1),
                        extra_specs=(BS((TR, D), row), vec, BS((TR, D), row)),
                        epilogue=_rms_bwd_epilogue, sequential=True, name=f"dh_{l}")
        if l == 0:
            big[key] = _mm("tn", sv["h"], dz, grid=(1, n // tn, nkt),
                           a_spec=BS((tk, D), lambda i, j, k: (k, 0)), b_spec=BS((tk, tn), lambda i, j, k: (k, j)),
                           out_shape=_sds((D, n), F32), out_spec=BS((D, tn), lambda i, j, k: (0, j)),
                           acc_shape=(D, tn), name=f"dw_in_{l}")
            carry = late_carry_fn(big) if late_carry_fn is not None else None
            res = _mm("nt", dz, w_in, grid=(T // TR, 1, 1), a_spec=BS((TR, n), row),
                      b_spec=BS((D, n), whole), carry=carry, **norm_bwd)
            if carry is not None:
                res, late_carried = res
            dx, small[f"norm1_g{l}"] = res
        else:
            dx, small[f"norm1_g{l}"] = _mm("nn", dz, w_in, grid=(T // TR, 1, 1),
                                           a_spec=BS((TR, n), row), b_spec=BS((n, D), whole), **norm_bwd)
            big[key] = _mm("tn", dz, sv["h"], grid=(n // tn, 1, nkt),
                           a_spec=BS((tk, tn), lambda i, j, k: (k, i)), b_spec=BS((tk, D), lambda i, j, k: (k, 0)),
                           out_shape=_sds((n, D), F32), out_spec=BS((tn, D), lambda i, j, k: (i, 0)),
                           acc_shape=(tn, D), name=f"dw_in_{l}")
    return loss, dx, big, small, bwd_carried, late_carried


def _colsum(a, name, cols=None, block=0):
    rows = a.shape[0]
    cols = cols or a.shape[1]

    def body(a_ref, o_ref):
        @pl.when(pl.program_id(0) == 0)
        def _():
            o_ref[...] = jnp.zeros_like(o_ref)

        o_ref[...] += jnp.sum(a_ref[...].astype(F32), axis=0, keepdims=True)

    return pl.pallas_call(
        body, grid=(rows // TR,), in_specs=[BS((TR, cols), lambda i: (i, block))],
        out_specs=BS((1, cols), lambda i: (0, 0)), out_shape=_sds((1, cols), F32),
        compiler_params=_cp(("arbitrary",)), name=name,
    )(a)


_PACK_ROWS = 64


def _pack_small(sm):
    plan = [("norm1_g0", 0, 1, 0, D), ("norm1_g1", 1, 1, 0, D), ("mem_norm_g0", 2, 1, 0, D),
            ("mem_norm_g1", 3, 1, 0, D), ("norm2_g0", 4, 1, 0, D), ("norm2_g1", 5, 1, 0, D),
            ("a_q_g", 6, 1, 0, AW), ("a_k_g", 7, 1, 0, AW), ("mq_g0", 8, 1, 0, MEMW),
            ("mq_g1", 9, 1, 0, MEMW), ("mk_g0", 10, 1, 0, MEMW), ("mk_g1", 11, 1, 0, MEMW),
            ("conv_b", 12, 1, 0, TOK), ("ln_g", 13, 1, 0, TOK), ("ln_b", 14, 1, 0, TOK),
            ("b_in_u", 15, 1, 0, D), ("b_in_u", 16, 1, D, 2 * TOK - D), ("b_in_qm", 17, 1, 0, MEMW),
            ("conv_w", 18, CONVW, 0, TOK), ("rel_u", 49, 12, 0, D)]
    arrs = [sm[name].reshape(12, D) if name == "rel_u" else sm[name] for name, *_ in plan]

    def body(*refs):
        o_ref = refs[-1]
        o_ref[...] = jnp.zeros_like(o_ref)
        for ref, (_, r0, nr, c0, nc) in zip(refs, plan):
            o_ref[r0:r0 + nr, 0:nc] = ref[0:nr, c0:c0 + nc]

    return pl.pallas_call(
        body, grid=(1,), in_specs=[BS(a.shape, lambda i: (0, 0)) for a in arrs],
        out_specs=BS((_PACK_ROWS, D), lambda i: (0, 0)), out_shape=_sds((_PACK_ROWS, D), F32),
        compiler_params=_cp(("arbitrary",)), name="pack_small",
    )(*arrs)


def _rel_table_to_u(rel_bias):
    flat = jnp.concatenate([jnp.broadcast_to(rel_bias[:, 191:192], (12, 447)), rel_bias[:, ::-1]], axis=1)
    return jnp.pad(flat, ((0, 0), (192, 1024 - 192 - 639))).reshape(12, 1, 1024)


def _u_to_rel_table(du):
    flat = du[:, 192:192 + 639]
    g = flat[:, 447:][:, ::-1]
    return g, flat[:, :447]


def kernel(x, mem, norm1_g, mem_norm_g, a_w_in, a_q_g, a_k_g, a_rel_bias, b_w_in, b_b_in, b_conv_w, b_conv_b, b_ln_g, b_ln_b, mq_g, mk_g, w_mem_kv, w_out, norm2_g, w_gate, w_up, w_down, loss_target, m_norm1_g, m_mem_norm_g, m_a_w_in, m_a_q_g, m_a_k_g, m_a_rel_bias, m_b_w_in, m_b_b_in, m_b_conv_w, m_b_conv_b, m_b_ln_g, m_b_ln_b, m_mq_g, m_mk_g, m_w_mem_kv, m_w_out, m_norm2_g, m_w_gate, m_w_up, m_w_down, v_norm1_g, v_mem_norm_g, v_a_w_in, v_a_q_g, v_a_k_g, v_a_rel_bias, v_b_w_in, v_b_b_in, v_b_conv_w, v_b_conv_b, v_b_ln_g, v_b_ln_b, v_mq_g, v_mk_g, v_w_mem_kv, v_w_out, v_norm2_g, v_w_gate, v_w_up, v_w_down):
    sx = 2 * lax.axis_index("x") + lax.axis_index("y")

    n_in = (NA // NSH, NBW // NSH)
    tr = lambda a: jnp.swapaxes(a, -1, -2)
    small_src = jnp.concatenate([
        jnp.pad(b_b_in, ((0, 0), (0, 512 - 448))),
        jnp.pad(b_conv_w[0], ((0, 0), (0, 512 - 192))),
        jnp.pad(jnp.concatenate([b_conv_b, b_ln_g, b_ln_b], 0), ((0, 0), (0, 512 - 192))),
        jnp.zeros((5, 512), F32)], 0)

    def gather_groups(groups, small=None):
        def src_of(l, name):
            if name == "in":
                return [_cast_bf16([tr(b_w_in[0])], "cast_in_1").reshape(1, 2, n_in[1] // 2, D) if l else
                        _cast_bf16([a_w_in[0]], "cast_in_0").reshape(1, 2, D // 2, n_in[0])]
            if name == "gu":
                return [_cast_bf16([tr(w_gate[l])], f"cast_gate_{l}").reshape(1, 2, FS // 2, D),
                        _cast_bf16([tr(w_up[l])], f"cast_up_{l}").reshape(1, 2, FS // 2, D)]
            arr, shape = {"kv": (w_mem_kv, (1, 2, 128, 2 * MEMW)), "wo": (w_out, (1, 2, 128, D)),
                          "wd": (w_down, (1, 2, FS // 2, D))}[name]
            return [_cast_bf16([arr[l]], f"cast_{name}_{l}").reshape(shape)]

        def dst_of(l, name):
            if name == "in":
                return (1, NSH, 2, n_in[1] // 2, D) if l else (1, NSH, 2, D // 2, n_in[0])
            return {"kv": (1, NSH, 2, 128, 2 * MEMW), "wo": (1, NSH, 2, 128, D),
                    "gu": (1, 2, NSH, 2, FS // 2, D), "wd": (1, NSH, 2, FS // 2, D)}[name]

        items = [(l, name) for l, names in groups for name in names]
        srcs, views = [], []
        for k, (l, name) in enumerate(items):
            srcs += src_of(l, name)
            if name == "gu":
                views += [lambda d, k=k: d[k].at[:, 0], lambda d, k=k: d[k].at[:, 1]]
            else:
                views.append(lambda d, k=k: d[k])

        def done(w, outs):
            for k, (l, name) in enumerate(items):
                if name == "in" and l == 0:
                    w["a"] = outs[k].reshape(NSH, D, n_in[0]).transpose(1, 0, 2).reshape(D, NA)
                elif name == "in":
                    w["b"] = outs[k].reshape(NBW, D)
                else:
                    shape = {"kv": (D, 2 * MEMW), "wo": (D, D), "gu": (2, FF, D), "wd": (FF, D)}
                    w[name][l] = outs[k].reshape(shape[name])

        return _gather_exchange(srcs, [dst_of(l, name) for l, name in items], views, small), done

    w = dict(a=None, b=None, kv=[None, None], wo=[None, None], gu=[None, None], wd=[None, None])
    first, first_in_done = gather_groups([(0, ["in"])], small_src)

    def first_done(w, outs):
        first_in_done(w, outs)
        small_all = outs[1]
        conv_w_full = small_all[:, 1:1 + CONVW, :192].transpose(1, 0, 2).reshape(CONVW, TOK)
        vec3 = small_all[:, 32:35, :192].transpose(1, 0, 2).reshape(3, TOK)
        w.update(b_b_in=small_all[:, 0, :448].reshape(1, NBW), conv_w=jnp.pad(conv_w_full, ((0, 1), (0, 0))),
                 conv_b=vec3[0:1], ln_g=vec3[1:2], ln_b=vec3[2:3])

    carries = {"rms1_0": (first, first_done),
               "inproj_a": gather_groups([(0, ["kv", "wo", "wd"])]),
               "attn_fwd": gather_groups([(0, ["gu"]), (1, ["in", "kv", "wo"])]),
               "ffn_up_0": gather_groups([(1, ["gu"])]),
               "ffn_down_0": gather_groups([(1, ["wd"])])}
    p = dict(
        norm1_g=norm1_g, mem_norm_g=mem_norm_g, norm2_g=norm2_g,
        a_q_g2=jnp.tile(a_q_g, (1, AH)), a_k_g2=jnp.tile(a_k_g, (1, AH)),
        mq_g4=jnp.tile(mq_g, (1, 4)), mk_g4=jnp.tile(mk_g, (1, 4)),
        rel_u=_rel_table_to_u(a_rel_bias[0]))

    def pair_sums(items, big, name):
        units = []
        for l, tensor in items:
            if tensor == "in" and l == 0:
                g = big["a"].reshape(D, NSH, n_in[0]).transpose(1, 0, 2).reshape(1, NSH, 2, D // 2, n_in[0])
            elif tensor == "in":
                g = big["b"].reshape(1, NSH, 2, n_in[1] // 2, D)
            else:
                shape = {"kv": (1, NSH, 2, 128, 2 * MEMW), "wo": (1, NSH, 2, 128, D),
                         "gu": (2, NSH, 2, FS // 2, D), "wd": (1, NSH, 2, FS // 2, D)}
                g = big[tensor][l].reshape(shape[tensor])
            units.append(g)
        return _pair_reduce(units, name)

    early_groups = [[(1, "gu")], [(0, "gu")],
                    [(1, "in"), (1, "kv"), (1, "wo"), (1, "wd"), (0, "kv"), (0, "wo"), (0, "wd")]]
    early = [item for grp in early_groups for item in grp]
    late = [(0, "in")]
    own_early, own_late = [], []

    def bwd_carry_fn(big):
        sums_b = []
        for k, grp in enumerate(early_groups):
            own, sb = pair_sums(grp, big, f"pair_reduce_early_{k}")
            own_early.extend(own)
            sums_b.extend(sb)
        return _chip_exchange(sums_b)

    def late_carry_fn(big):
        own, sb = pair_sums(late, big, "pair_reduce_late")
        own_late.extend(own)
        return _chip_exchange(sb)

    loss, grad_x, big, small, parts_early, parts_late = _local_step(
        x.reshape(T, D), mem.reshape(NB * MEMT, D), loss_target.reshape(T, D), w, p,
        carries=carries, bwd_carry_fn=bwd_carry_fn, late_carry_fn=late_carry_fn)
    loss = lax.psum(loss[0, 0], ("x", "y", "c"))
    items = early + late
    halves = [_quad_sum(o, pt, f"quad_sum_{name}_{l}")
              for (l, name), o, pt in zip(items, own_early + own_late, list(parts_early) + list(parts_late))]
    out_shapes = [(1, D, NA // NSH), (1, NBW // NSH, D), (2, 2 * 128, 2 * MEMW), (2, 2 * 128, D),
                  (2, FS, D), (2, FS, D), (2, FS, D)]
    target_of = {"in": lambda l: [(0, l, 0)], "kv": lambda l: [(0, 2, l)], "wo": lambda l: [(0, 3, l)],
                 "gu": lambda l: [(0, 4, l), (1, 5, l)], "wd": lambda l: [(0, 6, l)]}
    targets = [target_of[name](l) for l, name in items]
    g_a, g_b, g_kv, g_wo, g_gate, g_up, g_wd = _final_exchange(halves, out_shapes, targets)

    tot = _small_allreduce(_pack_small(small))
    g_rel, clip_part = _u_to_rel_table(tot[49:61])
    g_rel = jnp.concatenate([g_rel[:, :191], g_rel[:, 191:] + _rowsum(clip_part)], axis=1)
    b_in_full = jnp.concatenate([tot[15:16], tot[16:17, :512], tot[17:18, :MEMW]], axis=1)
    g_small = dict(
        norm1_g=tot[0:2], mem_norm_g=tot[2:4], norm2_g=tot[4:6],
        a_q_g=tot[6:7, :HD], a_k_g=tot[7:8, :HD], a_rel_bias=g_rel[None],
        b_b_in=lax.dynamic_slice(b_in_full, (0, sx * 448), (1, 448)),
        b_conv_w=lax.dynamic_slice(tot[18:49, :TOK], (0, sx * 192), (CONVW, 192))[None],
        b_conv_b=lax.dynamic_slice(tot[12:13, :TOK], (0, sx * 192), (1, 192)),
        b_ln_g=lax.dynamic_slice(tot[13:14, :TOK], (0, sx * 192), (1, 192)),
        b_ln_b=lax.dynamic_slice(tot[14:15, :TOK], (0, sx * 192), (1, 192)),
        mq_g=tot[8:10, :HD], mk_g=tot[10:12, :HD])

    names = ["norm1_g", "mem_norm_g", "a_w_in", "a_q_g", "a_k_g", "a_rel_bias", "b_w_in", "b_b_in",
             "b_conv_w", "b_conv_b", "b_ln_g", "b_ln_b", "mq_g", "mk_g", "w_mem_kv", "w_out",
             "norm2_g", "w_gate", "w_up", "w_down"]
    weights = dict(zip(names, [norm1_g, mem_norm_g, a_w_in, a_q_g, a_k_g, a_rel_bias, b_w_in, b_b_in,
                               b_conv_w, b_conv_b, b_ln_g, b_ln_b, mq_g, mk_g, w_mem_kv, w_out,
                               norm2_g, w_gate, w_up, w_down]))
    ms = dict(zip(names, [m_norm1_g, m_mem_norm_g, m_a_w_in, m_a_q_g, m_a_k_g, m_a_rel_bias, m_b_w_in,
                          m_b_b_in, m_b_conv_w, m_b_conv_b, m_b_ln_g, m_b_ln_b, m_mq_g, m_mk_g,
                          m_w_mem_kv, m_w_out, m_norm2_g, m_w_gate, m_w_up, m_w_down]))
    vs = dict(zip(names, [v_norm1_g, v_mem_norm_g, v_a_w_in, v_a_q_g, v_a_k_g, v_a_rel_bias, v_b_w_in,
                          v_b_b_in, v_b_conv_w, v_b_conv_b, v_b_ln_g, v_b_ln_b, v_mq_g, v_mk_g,
                          v_w_mem_kv, v_w_out, v_norm2_g, v_w_gate, v_w_up, v_w_down]))
    grads = dict(g_small)
    grads.update(a_w_in=g_a, b_w_in=g_b, w_mem_kv=g_kv, w_out=g_wo, w_gate=g_gate, w_up=g_up, w_down=g_wd)
    big_names = ["a_w_in", "b_w_in", "w_mem_kv", "w_out", "w_gate", "w_up", "w_down"]
    small_names = [n for n in names if n not in big_names]
    delta, new_m, new_v = {}, {}, {}
    for n in big_names:
        if n in ("b_w_in", "w_gate", "w_up"):
            outs = _adamw_big(tr(weights[n]), grads[n], tr(ms[n]), tr(vs[n]), f"adamw_{n}")
            delta[n], new_m[n], new_v[n] = [tr(o) for o in outs]
            grads[n] = tr(grads[n])
        else:
            delta[n], new_m[n], new_v[n] = _adamw_big(weights[n], grads[n], ms[n], vs[n], f"adamw_{n}")
    as2d = lambda a: a.reshape(-1, a.shape[-1])
    d_s, m_s, v_s = _adamw_small([as2d(weights[n]) for n in small_names], [as2d(grads[n]) for n in small_names],
                                 [as2d(ms[n]) for n in small_names], [as2d(vs[n]) for n in small_names])
    for i, n in enumerate(small_names):
        delta[n] = d_s[i].reshape(weights[n].shape)
        new_m[n] = m_s[i].reshape(weights[n].shape)
        new_v[n] = v_s[i].reshape(weights[n].shape)

    return (loss, grad_x.reshape(NB, SEQ, D), *[grads[n] for n in names], *[delta[n] for n in names],
            *[new_m[n] for n in names], *[new_v[n] for n in names])


def _rowsum(a):
    def body(a_ref, o_ref):
        o_ref[...] = jnp.sum(a_ref[...], axis=1, keepdims=True)

    vm = BS(memory_space=pltpu.VMEM)
    return pl.pallas_call(body, in_specs=[vm], out_specs=vm, out_shape=_sds((a.shape[0], 1), F32),
                          compiler_params=_cp(), name="rowsum")(a)
```

```python
import functools

import jax
import jax.numpy as jnp
from jax import lax
from jax.experimental import pallas as pl
from jax.experimental.pallas import tpu as pltpu

F32 = jnp.float32
BF16 = jnp.bfloat16
BS = pl.BlockSpec
ANY = pl.BlockSpec(memory_space=pl.ANY)
MESH = pl.DeviceIdType.MESH

D = 1024
SEQ = 2048
NB = 2
T = NB * SEQ
MEMT = 256
HD = 64
TOK = 768
MEMW = 256
NA = 3 * TOK + MEMW
NBW = 2 * TOK + MEMW
FF = 2816
NSH = 4
FS = FF // NSH
FT = FF // 2
CONVW = 31
EPS = 1e-6
NEG = -1e30
SCALE = HD ** -0.5
QB = 256
KWIN = 768
KPAD = 512
TR = 512

ADAM_LR = 0.001
ADAM_B1 = 0.9
ADAM_B2 = 0.999
ADAM_EPS = 1e-08
ADAM_WD = 0.01
ADAM_STEP = 10

_DIMS = {
    "nn": (((1,), (0,)), ((), ())),
    "nt": (((1,), (1,)), ((), ())),
    "tn": (((0,), (0,)), ((), ())),
}


def _cp(sem=None, vmem_mb=48):
    return pltpu.CompilerParams(dimension_semantics=sem, vmem_limit_bytes=vmem_mb << 20)


def _sds(shape, dtype):
    return jax.ShapeDtypeStruct(tuple(shape), dtype)


def _mm(mode, a, b, *, grid, a_spec, b_spec, out_shape, out_spec, acc_shape, name,
        extras=(), extra_specs=(), epilogue=None, carry=None, vmem_mb=48, sequential=False):
    n_ex = len(extras)
    nk = grid[2]
    dims = _DIMS[mode]
    ni = len(carry.ins) if carry else 0
    no = len(carry.out_shapes) if carry else 0
    multi = isinstance(out_shape, (list, tuple))
    out_shapes = list(out_shape) if multi else [out_shape]
    out_specs = list(out_spec) if multi else [out_spec]
    n_o = len(out_shapes)

    def body(a_ref, b_ref, *rest):
        ex = rest[:n_ex]
        cin = rest[n_ex:n_ex + ni]
        o_refs = rest[n_ex + ni:n_ex + ni + n_o]
        o_ref = o_refs if multi else o_refs[0]
        cout = rest[n_ex + ni + n_o:n_ex + ni + n_o + no]
        acc = rest[n_ex + ni + n_o + no]
        cscr = rest[n_ex + ni + n_o + no + 1:]
        ids = [pl.program_id(d) for d in range(3)]
        k = ids[2]
        if carry:
            @pl.when((ids[0] == 0) & (ids[1] == 0) & (ids[2] == 0))
            def _():
                carry.start(cin, cout, cscr)

        prod = lax.dot_general(a_ref[...].astype(BF16), b_ref[...].astype(BF16), dims,
                               preferred_element_type=F32)

        def finish(val):
            if epilogue is None:
                o_ref[...] = val.astype(o_ref.dtype)
            else:
                epilogue(val, ex, o_ref, ids[0])

        if nk == 1:
            finish(prod)
        else:
            @pl.when(k == 0)
            def _():
                acc[...] = prod

            @pl.when((k > 0) & (k < nk - 1))
            def _():
                acc[...] += prod

            @pl.when(k == nk - 1)
            def _():
                finish(acc[...] + prod)

        if carry:
            @pl.when((ids[0] == grid[0] - 1) & (ids[1] == grid[1] - 1) & (ids[2] == grid[2] - 1))
            def _():
                carry.finish(cin, cout, cscr)

    acc_scratch = pltpu.VMEM(acc_shape if nk > 1 else (8, 128), F32)
    ordered = sequential or bool(carry)
    outs = pl.pallas_call(
        body, grid=grid,
        in_specs=[a_spec, b_spec, *extra_specs] + (carry.in_specs if carry else []),
        out_specs=out_specs + [ANY] * no, out_shape=out_shapes + (carry.out_shapes if carry else []),
        scratch_shapes=[acc_scratch] + (carry.scratch if carry else []),
        compiler_params=pltpu.CompilerParams(
            dimension_semantics=("arbitrary",) * 3 if ordered else ("parallel", "parallel", "arbitrary"),
            vmem_limit_bytes=vmem_mb << 20, has_side_effects=bool(carry)), name=name,
    )(a, b, *extras, *(carry.ins if carry else []))
    mine = list(outs[:n_o]) if multi else outs[0]
    return (mine, outs[n_o:]) if carry else mine


def _rms_fwd(x, g, name, carry=None):
    rows = x.shape[0]
    ni = len(carry.ins) if carry else 0
    no = len(carry.out_shapes) if carry else 0

    def body(x_ref, g_ref, *rest):
        cin, o_ref, cout, cscr = rest[:ni], rest[ni], rest[ni + 1:ni + 1 + no], rest[ni + 1 + no:]
        if carry:
            @pl.when(pl.program_id(0) == 0)
            def _():
                carry.start(cin, cout, cscr)

        xv = x_ref[...]
        r = lax.rsqrt(jnp.mean(xv * xv, axis=-1, keepdims=True) + EPS)
        o_ref[...] = (xv * r * g_ref[...]).astype(BF16)

        if carry:
            @pl.when(pl.program_id(0) == rows // TR - 1)
            def _():
                carry.finish(cin, cout, cscr)

    outs = pl.pallas_call(
        body, grid=(rows // TR,),
        in_specs=[BS((TR, D), lambda i: (i, 0)), BS((1, D), lambda i: (0, 0))]
        + (carry.in_specs if carry else []),
        out_specs=[BS((TR, D), lambda i: (i, 0))] + [ANY] * no,
        out_shape=[_sds((rows, D), BF16)] + (carry.out_shapes if carry else []),
        scratch_shapes=carry.scratch if carry else [],
        compiler_params=pltpu.CompilerParams(
            dimension_semantics=("arbitrary",), vmem_limit_bytes=48 << 20,
            has_side_effects=bool(carry)), name=name,
    )(x, g, *(carry.ins if carry else []))
    return (outs[0], outs[1:]) if carry else outs[0]


def _rms_bwd(dh, x, g, dres, name):
    rows = x.shape[0]
    has_res = dres is not None

    def body(*refs):
        if has_res:
            dh_ref, x_ref, g_ref, r_ref, dx_ref, dg_ref = refs
        else:
            dh_ref, x_ref, g_ref, dx_ref, dg_ref = refs
        xv = x_ref[...]
        dhv = dh_ref[...]
        r = lax.rsqrt(jnp.mean(xv * xv, axis=-1, keepdims=True) + EPS)
        xh = xv * r
        gy = dhv * g_ref[...]
        dx = r * (gy - xh * jnp.mean(gy * xh, axis=-1, keepdims=True))
        if has_res:
            dx = dx + r_ref[...]
        dx_ref[...] = dx

        @pl.when(pl.program_id(0) == 0)
        def _():
            dg_ref[...] = jnp.zeros_like(dg_ref)

        dg_ref[...] += jnp.sum(dhv * xh, axis=0, keepdims=True)

    row = BS((TR, D), lambda i: (i, 0))
    vec = BS((1, D), lambda i: (0, 0))
    ins = [dh, x, g] + ([dres] if has_res else [])
    return pl.pallas_call(
        body, grid=(rows // TR,),
        in_specs=[row, row, vec] + ([row] if has_res else []),
        out_specs=[row, vec], out_shape=[_sds((rows, D), F32), _sds((1, D), F32)],
        compiler_params=_cp(("arbitrary",)), name=name,
    )(*ins)


def _group_masks(width):
    lane = lax.broadcasted_iota(jnp.int32, (1, width), 1)
    return [(lane >= HD * h) & (lane < HD * (h + 1)) for h in range(width // HD)]


def _group_mean(v, masks):
    del masks
    low = lax.broadcasted_iota(jnp.int32, (1, 128), 1) < HD
    slabs = []
    for j in range(v.shape[-1] // 128):
        x = v[:, 128 * j:128 * (j + 1)]
        s0 = jnp.sum(jnp.where(low, x, 0.0), axis=-1, keepdims=True) * (1.0 / HD)
        s1 = jnp.sum(jnp.where(low, 0.0, x), axis=-1, keepdims=True) * (1.0 / HD)
        slabs.append(jnp.where(low, s0, s1))
    return slabs[0] if len(slabs) == 1 else jnp.concatenate(slabs, axis=-1)


def _head_norm(zv, g, masks):
    r = lax.rsqrt(_group_mean(zv * zv, masks) + EPS)
    return zv * r * g


def _head_norm_bwd(dy, zv, g, masks):
    r = lax.rsqrt(_group_mean(zv * zv, masks) + EPS)
    zh = zv * r
    gy = dy * g
    dz = r * (gy - zh * _group_mean(gy * zh, masks))
    return dz, jnp.sum(dy * zh, axis=0, keepdims=True)


def _fold_heads(v, width):
    vb = jnp.broadcast_to(v, (8, width))
    out = vb
    for h in range(1, width // HD):
        out = out + pltpu.roll(vb, width - HD * h, axis=1)
    return out[0:1]


def _bias_expand(u):
    def body(u_ref, o_ref):
        x = jnp.broadcast_to(u_ref[...], (QB, 1024))
        rolled = pltpu.roll(x, 1024 - (QB - 1), axis=1, stride=1, stride_axis=0)[:, :KWIN]
        row = lax.broadcasted_iota(jnp.int32, (QB, 1), 0)
        col = lax.broadcasted_iota(jnp.int32, (1, KWIN), 1)
        lo = (row // 64) * 64
        ok = (col >= lo) & (col < lo + 576)
        o_ref[...] = jnp.where(ok, rolled, NEG)

    return pl.pallas_call(
        body, grid=(12,), in_specs=[BS((None, 1, 1024), lambda h: (h, 0, 0))],
        out_specs=BS((None, QB, KWIN), lambda h: (h, 0, 0)), out_shape=_sds((12, QB, KWIN), F32),
        compiler_params=_cp(("arbitrary",)), name="bias_expand",
    )(u)


def _bias_reduce(ds):
    def body(d_ref, o_ref):
        ri = lax.broadcasted_iota(jnp.int32, (QB, QB), 0)
        ci = lax.broadcasted_iota(jnp.int32, (QB, QB), 1)
        flip = (ri + ci == QB - 1).astype(F32)
        drev = jnp.dot(flip, d_ref[...], precision=lax.Precision.HIGHEST, preferred_element_type=F32)
        x = jnp.concatenate([drev, jnp.zeros((QB, 1024 - KWIN), F32)], axis=1)
        rolled = pltpu.roll(x, 0, axis=1, stride=1, stride_axis=0)
        o_ref[...] = jnp.sum(rolled, axis=0, keepdims=True)

    return pl.pallas_call(
        body, grid=(12,), in_specs=[BS((None, QB, KWIN), lambda h: (h, 0, 0))],
        out_specs=BS((None, 1, 1024), lambda h: (h, 0, 0)), out_shape=_sds((12, 1, 1024), F32),
        compiler_params=_cp(("arbitrary",)), name="bias_reduce",
    )(ds)


AW = 256
AH = AW // HD
AG = TOK // AW
def _attn_softmax(qh, kw, bias, startadd):
    s = lax.dot_general(qh, kw, _DIMS["nt"], preferred_element_type=F32) + bias + startadd
    m = jnp.max(s, axis=-1, keepdims=True)
    p = jnp.exp(s - m)
    return p * (1.0 / jnp.sum(p, axis=-1, keepdims=True))


def _attn_prologue(q_ref, k_ref, v_ref, gq_ref, gk_ref, qn_s, kn_s, v_s, masks):
    kn_s[0:KPAD, :] = jnp.zeros((KPAD, AW), BF16)
    v_s[0:KPAD, :] = jnp.zeros((KPAD, AW), BF16)
    for r in range(0, SEQ, TR):
        qn_s[r:r + TR, :] = (_head_norm(q_ref[r:r + TR, :], gq_ref[...], masks) * SCALE).astype(BF16)
        kn_s[KPAD + r:KPAD + r + TR, :] = _head_norm(k_ref[r:r + TR, :], gk_ref[...], masks).astype(BF16)
        v_s[KPAD + r:KPAD + r + TR, :] = v_ref[r:r + TR, :].astype(BF16)


def _attn_fwd(z, bias, gq2, gk2, carry=None):
    ni = len(carry.ins) if carry else 0
    no = len(carry.out_shapes) if carry else 0

    def body(q_ref, k_ref, v_ref, b_ref, gq_ref, gk_ref, *rest):
        cin, o_ref, cout = rest[:ni], rest[ni], rest[ni + 1:ni + 1 + no]
        qn_s, kn_s, v_s = rest[ni + 1 + no:ni + 4 + no]
        cscr = rest[ni + 4 + no:]
        if carry:
            @pl.when((pl.program_id(0) == 0) & (pl.program_id(1) == 0))
            def _():
                carry.start(cin, cout, cscr)

        masks = _group_masks(AW)
        _attn_prologue(q_ref, k_ref, v_ref, gq_ref, gk_ref, qn_s, kn_s, v_s, masks)
        col = lax.broadcasted_iota(jnp.int32, (1, KWIN), 1)

        def blk(i, carry):
            r0 = pl.multiple_of(i * QB, QB)
            qb = qn_s[pl.ds(r0, QB), :]
            kw = kn_s[pl.ds(r0, KWIN), :]
            vw = v_s[pl.ds(r0, KWIN), :]
            startadd = jnp.where(col + r0 < KPAD, NEG, 0.0)
            o = jnp.zeros((QB, AW), F32)
            for h in range(AH):
                qh = jnp.where(masks[h], qb, jnp.zeros_like(qb))
                vh = jnp.where(masks[h], vw, jnp.zeros_like(vw))
                p = _attn_softmax(qh, kw, b_ref[h], startadd).astype(BF16)
                o = o + jnp.dot(p, vh, preferred_element_type=F32)
            o_ref[pl.ds(r0, QB), :] = o.astype(BF16)
            return carry

        lax.fori_loop(0, SEQ // QB, blk, 0)

        if carry:
            @pl.when((pl.program_id(0) == NB - 1) & (pl.program_id(1) == AG - 1))
            def _():
                carry.finish(cin, cout, cscr)

    vec = BS((1, AW), lambda b, hp: (0, 0))
    outs = pl.pallas_call(
        body, grid=(NB, AG),
        in_specs=[BS((SEQ, AW), lambda b, hp: (b, hp)),
                  BS((SEQ, AW), lambda b, hp: (b, AG + hp)),
                  BS((SEQ, AW), lambda b, hp: (b, 2 * AG + hp)),
                  BS((AH, QB, KWIN), lambda b, hp: (hp, 0, 0)), vec, vec]
        + (carry.in_specs if carry else []),
        out_specs=[BS((SEQ, AW), lambda b, hp: (b, hp))] + [ANY] * no,
        out_shape=[_sds((T, D), BF16)] + (carry.out_shapes if carry else []),
        scratch_shapes=[pltpu.VMEM((SEQ, AW), BF16), pltpu.VMEM((SEQ + KPAD, AW), BF16),
                        pltpu.VMEM((SEQ + KPAD, AW), BF16)] + (carry.scratch if carry else []),
        compiler_params=pltpu.CompilerParams(
            dimension_semantics=("arbitrary", "arbitrary"), vmem_limit_bytes=48 << 20,
            has_side_effects=bool(carry)), name="attn_fwd",
    )(z, z, z, bias, gq2, gk2, *(carry.ins if carry else []))
    return outs[0], outs[1:]


def _attn_bwd(z, dcat, dz_into, bias, gq2, gk2, carry=None):
    ni = len(carry.ins) if carry else 0
    no = len(carry.out_shapes) if carry else 0

    def body(q_ref, k_ref, v_ref, do_ref, b_ref, gq_ref, gk_ref, into_ref, *rest):
        del into_ref
        cin = rest[:ni]
        dz_ref, db_ref, dgq_ref, dgk_ref = rest[ni:ni + 4]
        cout = rest[ni + 4:ni + 4 + no]
        qn_s, kn_s, v_s, dqn_s, dkn_s, dv_s, out_s, out_sem = rest[ni + 4 + no:ni + 12 + no]
        cscr = rest[ni + 12 + no:]
        hp = pl.program_id(0)
        b = pl.program_id(1)
        step = hp * NB + b

        def out_copy(j, step):
            rows = pl.ds(pl.multiple_of((step % NB) * SEQ, SEQ), SEQ)
            cols = pl.ds(pl.multiple_of((j * AG + step // NB) * AW, AW), AW)
            return pltpu.make_async_copy(out_s.at[j], dz_ref.at[rows, cols], out_sem.at[j])

        if carry:
            @pl.when((hp == 0) & (b == 0))
            def _():
                carry.start(cin, cout, cscr)

        masks = _group_masks(AW)
        _attn_prologue(q_ref, k_ref, v_ref, gq_ref, gk_ref, qn_s, kn_s, v_s, masks)
        dkn_s[...] = jnp.zeros_like(dkn_s)
        dv_s[...] = jnp.zeros_like(dv_s)

        @pl.when(b == 0)
        def _():
            db_ref[...] = jnp.zeros_like(db_ref)

        @pl.when((b == 0) & (hp == 0))
        def _():
            dgq_ref[...] = jnp.zeros_like(dgq_ref)
            dgk_ref[...] = jnp.zeros_like(dgk_ref)

        col = lax.broadcasted_iota(jnp.int32, (1, KWIN), 1)

        def blk(i, carry):
            r0 = pl.multiple_of(i * QB, QB)
            qb = qn_s[pl.ds(r0, QB), :]
            kw = kn_s[pl.ds(r0, KWIN), :]
            vw = v_s[pl.ds(r0, KWIN), :]
            dob = do_ref[pl.ds(r0, QB), :].astype(BF16)
            startadd = jnp.where(col + r0 < KPAD, NEG, 0.0)
            dqn = jnp.zeros((QB, AW), F32)
            dkw = jnp.zeros((KWIN, AW), F32)
            dvw = jnp.zeros((KWIN, AW), F32)
            for h in range(AH):
                qh = jnp.where(masks[h], qb, jnp.zeros_like(qb))
                kh = jnp.where(masks[h], kw, jnp.zeros_like(kw))
                doh = jnp.where(masks[h], dob, jnp.zeros_like(dob))
                p = _attn_softmax(qh, kw, b_ref[h], startadd)
                dvw = dvw + lax.dot_general(p.astype(BF16), doh, _DIMS["tn"],
                                            preferred_element_type=F32)
                dp = lax.dot_general(doh, vw, _DIMS["nt"], preferred_element_type=F32)
                ds = p * (dp - jnp.sum(dp * p, axis=-1, keepdims=True))
                db_ref[h] += ds
                dsb = ds.astype(BF16)
                dqn = dqn + jnp.dot(dsb, kh, preferred_element_type=F32)
                dkw = dkw + lax.dot_general(dsb, qh, _DIMS["tn"], preferred_element_type=F32)
            dqn_s[pl.ds(r0, QB), :] = dqn * SCALE
            dkn_s[pl.ds(r0, KWIN), :] += dkw
            dv_s[pl.ds(r0, KWIN), :] += dvw
            return carry

        lax.fori_loop(0, SEQ // QB, blk, 0)

        @pl.when(step > 0)
        def _():
            for j in range(3):
                out_copy(j, step - 1).wait()

        dgq = jnp.zeros((1, AW), F32)
        dgk = jnp.zeros((1, AW), F32)
        for r in range(0, SEQ, TR):
            dq, dg = _head_norm_bwd(dqn_s[r:r + TR, :], q_ref[r:r + TR, :], gq_ref[...], masks)
            out_s[0, r:r + TR, :] = dq.astype(BF16)
            dgq = dgq + dg
            dk, dg = _head_norm_bwd(dkn_s[KPAD + r:KPAD + r + TR, :], k_ref[r:r + TR, :], gk_ref[...], masks)
            out_s[1, r:r + TR, :] = dk.astype(BF16)
            dgk = dgk + dg
            out_s[2, r:r + TR, :] = dv_s[KPAD + r:KPAD + r + TR, :].astype(BF16)
        dgq_ref[...] += _fold_heads(dgq, AW)
        dgk_ref[...] += _fold_heads(dgk, AW)
        for j in range(3):
            out_copy(j, step).start()

        @pl.when(step == AG * NB - 1)
        def _():
            for j in range(3):
                out_copy(j, step).wait()

        if carry:
            @pl.when((hp == AG - 1) & (b == NB - 1))
            def _():
                carry.finish(cin, cout, cscr)

    vec = BS((1, AW), lambda hp, b: (0, 0))
    row = BS((SEQ, AW), lambda hp, b: (b, hp))
    outs = pl.pallas_call(
        body, grid=(AG, NB),
        in_specs=[row,
                  BS((SEQ, AW), lambda hp, b: (b, AG + hp)),
                  BS((SEQ, AW), lambda hp, b: (b, 2 * AG + hp)),
                  row,
                  BS((AH, QB, KWIN), lambda hp, b: (hp, 0, 0)), vec, vec, ANY]
        + (carry.in_specs if carry else []),
        out_specs=[ANY, BS((AH, QB, KWIN), lambda hp, b: (hp, 0, 0)), vec, vec] + [ANY] * no,
        out_shape=[_sds((T, NA), BF16), _sds((12, QB, KWIN), F32), _sds((1, AW), F32), _sds((1, AW), F32)]
        + (carry.out_shapes if carry else []),
        input_output_aliases={7: 0},
        scratch_shapes=[pltpu.VMEM((SEQ, AW), BF16), pltpu.VMEM((SEQ + KPAD, AW), BF16),
                        pltpu.VMEM((SEQ + KPAD, AW), BF16), pltpu.VMEM((SEQ, AW), F32),
                        pltpu.VMEM((SEQ + KPAD, AW), F32), pltpu.VMEM((SEQ + KPAD, AW), F32),
                        pltpu.VMEM((3, SEQ, AW), BF16), pltpu.SemaphoreType.DMA((3,))]
        + (carry.scratch if carry else []),
        compiler_params=pltpu.CompilerParams(
            dimension_semantics=("arbitrary", "arbitrary"), vmem_limit_bytes=58 << 20,
            has_side_effects=bool(carry)), name="attn_bwd",
    )(z, z, z, dcat, bias, gq2, gk2, dz_into, *(carry.ins if carry else []))
    return outs[:4], outs[4:]


def _mem_softmax(qh, kn):
    s = lax.dot_general(qh, kn, _DIMS["nt"], preferred_element_type=F32)
    m = jnp.max(s, axis=-1, keepdims=True)
    p = jnp.exp(s - m)
    return p * (1.0 / jnp.sum(p, axis=-1, keepdims=True))


def _memattn_fwd(z, kv, cat, gq4, gk4, qcol, name):
    def body(q_ref, k_ref, v_ref, gq_ref, gk_ref, cat_ref, o_ref):
        del cat_ref
        masks = _group_masks(MEMW)
        qn = (_head_norm(q_ref[...], gq_ref[...], masks) * SCALE).astype(BF16)
        kn = _head_norm(k_ref[...], gk_ref[...], masks).astype(BF16)
        vv = v_ref[...].astype(BF16)
        o = jnp.zeros((TR, MEMW), F32)
        for h in range(4):
            qh = jnp.where(masks[h], qn, jnp.zeros_like(qn))
            vh = jnp.where(masks[h], vv, jnp.zeros_like(vv))
            p = _mem_softmax(qh, kn).astype(BF16)
            o = o + jnp.dot(p, vh, preferred_element_type=F32)
        o_ref[...] = o.astype(BF16)

    nt = SEQ // TR
    vec = BS((1, MEMW), lambda b, t: (0, 0))
    return pl.pallas_call(
        body, grid=(NB, nt),
        in_specs=[BS((TR, MEMW), lambda b, t: (b * nt + t, qcol)),
                  BS((MEMT, MEMW), lambda b, t: (b, 0)),
                  BS((MEMT, MEMW), lambda b, t: (b, 1)), vec, vec, ANY],
        out_specs=BS((TR, MEMW), lambda b, t: (b * nt + t, 3)),
        out_shape=_sds((T, D), BF16), input_output_aliases={5: 0},
        compiler_params=_cp(("arbitrary", "arbitrary")), name=name,
    )(z, kv, kv, gq4, gk4, cat)


def _memattn_bwd(z, kv, dcat, gq4, gk4, qcol, name, dz_cols=MEMW):
    nt = SEQ // TR

    def body(q_ref, k_ref, v_ref, do_ref, gq_ref, gk_ref,
             dq_ref, dkv_ref, dgq_ref, dgk_ref, dkn_s, dv_s):
        b = pl.program_id(0)
        t = pl.program_id(1)
        masks = _group_masks(MEMW)
        qz = q_ref[...]
        kz = k_ref[...]
        qn = (_head_norm(qz, gq_ref[...], masks) * SCALE).astype(BF16)
        kn = _head_norm(kz, gk_ref[...], masks).astype(BF16)
        vv = v_ref[...].astype(BF16)
        dob = do_ref[...].astype(BF16)

        @pl.when(t == 0)
        def _():
            dkn_s[...] = jnp.zeros_like(dkn_s)
            dv_s[...] = jnp.zeros_like(dv_s)

        @pl.when((t == 0) & (b == 0))
        def _():
            dgq_ref[...] = jnp.zeros_like(dgq_ref)
            dgk_ref[...] = jnp.zeros_like(dgk_ref)

        dqn = jnp.zeros((TR, MEMW), F32)
        dkn = jnp.zeros((MEMT, MEMW), F32)
        dvv = jnp.zeros((MEMT, MEMW), F32)
        for h in range(4):
            qh = jnp.where(masks[h], qn, jnp.zeros_like(qn))
            kh = jnp.where(masks[h], kn, jnp.zeros_like(kn))
            doh = jnp.where(masks[h], dob, jnp.zeros_like(dob))
            p = _mem_softmax(qh, kn)
            dvv = dvv + lax.dot_general(p.astype(BF16), doh, _DIMS["tn"], preferred_element_type=F32)
            dp = lax.dot_general(doh, vv, _DIMS["nt"], preferred_element_type=F32)
            ds = p * (dp - jnp.sum(dp * p, axis=-1, keepdims=True))
            dsb = ds.astype(BF16)
            dqn = dqn + jnp.dot(dsb, kh, preferred_element_type=F32)
            dkn = dkn + lax.dot_general(dsb, qh, _DIMS["tn"], preferred_element_type=F32)
        dkn_s[...] += dkn
        dv_s[...] += dvv
        dq, dgq = _head_norm_bwd(dqn * SCALE, qz, gq_ref[...], masks)
        dq_ref[...] = dq.astype(BF16)
        dgq_ref[...] += _fold_heads(dgq, MEMW)

        @pl.when(t == nt - 1)
        def _():
            dk, dgk = _head_norm_bwd(dkn_s[...], kz, gk_ref[...], masks)
            dkv_ref[:, 0:MEMW] = dk
            dkv_ref[:, MEMW:] = dv_s[...]
            dgk_ref[...] += _fold_heads(dgk, MEMW)

    vec = BS((1, MEMW), lambda b, t: (0, 0))
    return pl.pallas_call(
        body, grid=(NB, nt),
        in_specs=[BS((TR, MEMW), lambda b, t: (b * nt + t, qcol)),
                  BS((MEMT, MEMW), lambda b, t: (b, 0)),
                  BS((MEMT, MEMW), lambda b, t: (b, 1)),
                  BS((TR, MEMW), lambda b, t: (b * nt + t, 3)), vec, vec],
        out_specs=[BS((TR, MEMW), lambda b, t: (b * nt + t, dz_cols // MEMW - 1)),
                   BS((MEMT, 2 * MEMW), lambda b, t: (b, 0)), vec, vec],
        out_shape=[_sds((T, dz_cols), BF16), _sds((NB * MEMT, 2 * MEMW), F32),
                   _sds((1, MEMW), F32), _sds((1, MEMW), F32)],
        scratch_shapes=[pltpu.VMEM((MEMT, MEMW), F32), pltpu.VMEM((MEMT, MEMW), F32)],
        compiler_params=_cp(("arbitrary", "arbitrary")), name=name,
    )(z, kv, kv, dcat, gq4, gk4)


HALO = 32
NEXT = 64
RT = 64


def _glu(zz):
    return zz[:, :TOK] * jax.nn.sigmoid(zz[:, TOK:])


def _layer_norm_parts(y):
    mu = jnp.mean(y, axis=-1, keepdims=True)
    yc = y - mu
    rstd = lax.rsqrt(jnp.mean(yc * yc, axis=-1, keepdims=True) + EPS)
    return yc * rstd, rstd


def _shifted_copies(src, dst, rows):
    for b in range(1, 8):
        dst[b - 1, 0:rows, :] = src[b:b + rows, :]


def _tap(src, shifted, off, r0, rows):
    b = off % 8
    if b == 0:
        return src[r0 + off:r0 + off + rows, :]
    return shifted[b - 1, r0 + off - b:r0 + off - b + rows, :]


def _conv_rows(w_ref, hbuf, hs, r0, rows):
    y = jnp.zeros((rows, TOK), F32)
    for j in range(CONVW):
        y = y + w_ref[j:j + 1, :] * _tap(hbuf, hs, (HALO - CONVW + 1) + j, r0, rows)
    return y


def _conv_fwd(z, cw, cb, lg, lb):
    nt = SEQ // TR

    def body(zc_ref, zp_ref, w_ref, cb_ref, lg_ref, lb_ref, o_ref, y_ref, hbuf, hs):
        t = pl.program_id(1)
        hbuf[0:HALO, :] = jnp.where(t == 0, 0.0, _glu(zp_ref[...]))
        hbuf[HALO:, :] = _glu(zc_ref[...])
        _shifted_copies(hbuf, hs, HALO + TR - 8)
        for r0 in range(0, TR, RT):
            y = _conv_rows(w_ref, hbuf, hs, r0, RT) + cb_ref[...]
            y_ref[r0:r0 + RT, :] = y
            yh, _ = _layer_norm_parts(y)
            o = yh * lg_ref[...] + lb_ref[...]
            o_ref[r0:r0 + RT, :] = (o * jax.nn.sigmoid(o)).astype(BF16)

    vec = BS((1, TOK), lambda b, t: (0, 0))
    per = TR // HALO
    return pl.pallas_call(
        body, grid=(NB, nt),
        in_specs=[BS((TR, 2 * TOK), lambda b, t: (b * nt + t, 0)),
                  BS((HALO, 2 * TOK), lambda b, t: (jnp.maximum((b * nt + t) * per - 1, 0), 0)),
                  BS((32, TOK), lambda b, t: (0, 0)), vec, vec, vec],
        out_specs=[BS((TR, TOK), lambda b, t: (b * nt + t, 0)), BS((TR, TOK), lambda b, t: (b * nt + t, 0))],
        out_shape=[_sds((T, D), BF16), _sds((T, TOK), F32)],
        scratch_shapes=[pltpu.VMEM((HALO + TR, TOK), F32), pltpu.VMEM((7, HALO + TR, TOK), F32)],
        compiler_params=_cp(("arbitrary", "arbitrary")), name="conv_fwd",
    )(z, z, cw, cb, lg, lb)


def _conv_bwd(z, y, dcat, dz_into, cw, lg, lb):
    nt = SEQ // TR
    ext = TR + NEXT

    def body(zc_ref, zp_ref, yc_ref, yn_ref, dc_ref, dn_ref, w_ref, lg_ref, lb_ref, into_ref,
             du_ref, dw_ref, dcb_ref, dlg_ref, dlb_ref, dbin_ref, hbuf, dybuf, hs, dys):
        del into_ref
        b = pl.program_id(0)
        t = pl.program_id(1)

        @pl.when((b == 0) & (t == 0))
        def _():
            dw_ref[...] = jnp.zeros_like(dw_ref)
            dcb_ref[...] = jnp.zeros_like(dcb_ref)
            dlg_ref[...] = jnp.zeros_like(dlg_ref)
            dlb_ref[...] = jnp.zeros_like(dlb_ref)
            dbin_ref[...] = jnp.zeros_like(dbin_ref)

        hbuf[0:HALO, :] = jnp.where(t == 0, 0.0, _glu(zp_ref[...]))
        hbuf[HALO:, :] = _glu(zc_ref[...])
        _shifted_copies(hbuf, hs, HALO + TR - 8)
        last = t == nt - 1
        for r0 in range(0, ext, RT):
            yh, rstd = _layer_norm_parts(yc_ref[r0:r0 + RT, :] if r0 < TR else yn_ref[...])
            o = yh * lg_ref[...] + lb_ref[...]
            sg = jax.nn.sigmoid(o)
            if r0 < TR:
                dtok = dc_ref[r0:r0 + RT, :]
            else:
                dtok = jnp.where(last, 0.0, dn_ref[...])
            do = dtok * (sg * (1.0 + o * (1.0 - sg)))
            dyh = do * lg_ref[...]
            dy = rstd * (dyh - jnp.mean(dyh, axis=-1, keepdims=True)
                         - yh * jnp.mean(dyh * yh, axis=-1, keepdims=True))
            dybuf[r0:r0 + RT, :] = dy
            if r0 < TR:
                dlg_ref[...] += jnp.sum(do * yh, axis=0, keepdims=True)
                dlb_ref[...] += jnp.sum(do, axis=0, keepdims=True)
                dcb_ref[...] += jnp.sum(dy, axis=0, keepdims=True)
        _shifted_copies(dybuf, dys, ext - 8)
        for r0 in range(0, TR, RT):
            dh = jnp.zeros((RT, TOK), F32)
            for j in range(CONVW):
                dh = dh + w_ref[j:j + 1, :] * _tap(dybuf, dys, (CONVW - 1) - j, r0, RT)
            a = zc_ref[r0:r0 + RT, 0:TOK]
            sg = jax.nn.sigmoid(zc_ref[r0:r0 + RT, TOK:])
            da = dh * sg
            dg = dh * a * (sg * (1.0 - sg))
            du_ref[r0:r0 + RT, 0:TOK] = da.astype(BF16)
            du_ref[r0:r0 + RT, TOK:] = dg.astype(BF16)
            dbin_ref[:, 0:TOK] += jnp.sum(da, axis=0, keepdims=True)
            dbin_ref[:, TOK:] += jnp.sum(dg, axis=0, keepdims=True)
        for j in range(CONVW):
            acc = jnp.zeros((8, TOK), F32)
            for r0 in range(0, TR, RT):
                prod = dybuf[r0:r0 + RT, :] * _tap(hbuf, hs, (HALO - CONVW + 1) + j, r0, RT)
                acc = acc + jnp.sum(prod.reshape(RT // 8, 8, TOK), axis=0)
            dw_ref[j:j + 1, :] += jnp.sum(acc, axis=0, keepdims=True)

    vec = BS((1, TOK), lambda b, t: (0, 0))
    perh = TR // HALO
    pern = TR // NEXT
    nlast_n = T // NEXT - 1
    return pl.pallas_call(
        body, grid=(NB, nt),
        in_specs=[BS((TR, 2 * TOK), lambda b, t: (b * nt + t, 0)),
                  BS((HALO, 2 * TOK), lambda b, t: (jnp.maximum((b * nt + t) * perh - 1, 0), 0)),
                  BS((TR, TOK), lambda b, t: (b * nt + t, 0)),
                  BS((NEXT, TOK), lambda b, t: (jnp.minimum((b * nt + t + 1) * pern, nlast_n), 0)),
                  BS((TR, TOK), lambda b, t: (b * nt + t, 0)),
                  BS((NEXT, TOK), lambda b, t: (jnp.minimum((b * nt + t + 1) * pern, nlast_n), 0)),
                  BS((32, TOK), lambda b, t: (0, 0)), vec, vec, ANY],
        out_specs=[BS((TR, 2 * TOK), lambda b, t: (b * nt + t, 0)),
                   BS((32, TOK), lambda b, t: (0, 0)), vec, vec, vec,
                   BS((1, 2 * TOK), lambda b, t: (0, 0))],
        out_shape=[_sds((T, NBW), BF16), _sds((32, TOK), F32), _sds((1, TOK), F32),
                   _sds((1, TOK), F32), _sds((1, TOK), F32), _sds((1, 2 * TOK), F32)],
        input_output_aliases={9: 0},
        scratch_shapes=[pltpu.VMEM((HALO + TR, TOK), F32), pltpu.VMEM((ext, TOK), F32),
                        pltpu.VMEM((7, HALO + TR, TOK), F32), pltpu.VMEM((7, ext, TOK), F32)],
        compiler_params=_cp(("arbitrary", "arbitrary"), vmem_mb=56), name="conv_bwd",
    )(z, z, y, y, dcat, dcat, cw, lg, lb, dz_into)


def _ffn_up(h2, wgu, l, carry=None):
    ni = len(carry.ins) if carry else 0
    no = len(carry.out_shapes) if carry else 0

    def body(h_ref, wg_ref, wu_ref, *rest):
        cin = rest[:ni]
        g_ref, u_ref, a_ref = rest[ni:ni + 3]
        cout, cscr = rest[ni + 3:ni + 3 + no], rest[ni + 3 + no:]
        if carry:
            @pl.when((pl.program_id(0) == 0) & (pl.program_id(1) == 0))
            def _():
                carry.start(cin, cout, cscr)

        hv = h_ref[...]
        g = lax.dot_general(hv, wg_ref[...], _DIMS["nt"], preferred_element_type=F32)
        u = lax.dot_general(hv, wu_ref[...], _DIMS["nt"], preferred_element_type=F32)
        sg = jax.nn.sigmoid(g)
        silu = g * sg
        g_ref[...] = (u * (sg * (1.0 + g * (1.0 - sg)))).astype(BF16)
        u_ref[...] = silu.astype(BF16)
        a_ref[...] = (silu * u).astype(BF16)

        if carry:
            @pl.when((pl.program_id(0) == FF // FT - 1) & (pl.program_id(1) == T // TR - 1))
            def _():
                carry.finish(cin, cout, cscr)

    out = BS((TR, FT), lambda q, i: (i, q))
    outs = pl.pallas_call(
        body, grid=(FF // FT, T // TR),
        in_specs=[BS((TR, D), lambda q, i: (i, 0)),
                  BS((None, FT, D), lambda q, i: (0, q, 0)),
                  BS((None, FT, D), lambda q, i: (1, q, 0))] + (carry.in_specs if carry else []),
        out_specs=[out, out, out] + [ANY] * no,
        out_shape=[_sds((T, FF), BF16), _sds((T, FF), BF16), _sds((T, FF), BF16)]
        + (carry.out_shapes if carry else []),
        scratch_shapes=carry.scratch if carry else [],
        compiler_params=pltpu.CompilerParams(
            dimension_semantics=("arbitrary", "arbitrary"), vmem_limit_bytes=48 << 20,
            has_side_effects=bool(carry)), name=f"ffn_up_{l}",
    )(h2, wgu, wgu, *(carry.ins if carry else []))
    return outs[:3], outs[3:]


def _ffn_down_bwd(dx, wd, g, u, l):
    def epilogue(dact, ex, o_ref, i):
        o_ref[0] = (dact * ex[0][...].astype(F32)).astype(BF16)
        o_ref[1] = (dact * ex[1][...].astype(F32)).astype(BF16)

    ex_spec = BS((TR, FT), lambda i, q, k: (i, q))
    return _mm("nt", dx, wd, grid=(T // TR, FF // FT, 1),
               a_spec=BS((TR, D), lambda i, q, k: (i, 0)),
               b_spec=BS((FT, D), lambda i, q, k: (q, 0)),
               out_shape=_sds((2, T, FF), BF16),
               out_spec=BS((2, TR, FT), lambda i, q, k: (0, i, q)),
               acc_shape=(TR, FT), extras=(g, u), extra_specs=(ex_spec, ex_spec),
               epilogue=epilogue, name=f"ffn_down_bwd_{l}")


def _row_tile(rows, cols, itemsize=4, limit=2 << 20):
    tr = rows
    while tr * cols * itemsize > limit and tr % 2 == 0 and (tr // 2) % 16 == 0:
        tr //= 2
    return tr


def _cast_bf16(arrs, name):
    n = len(arrs)
    rows, cols = arrs[0].shape
    tr = _row_tile(rows, cols)

    def body(*refs):
        o_ref = refs[n]
        k = pl.program_id(0)
        val = refs[0][...]
        for j in range(1, n):
            val = jnp.where(k == j, refs[j][...], val)
        o_ref[...] = val.astype(BF16)

    return pl.pallas_call(
        body, grid=(n, rows // tr),
        in_specs=[BS((tr, cols), lambda k, i: (i, 0))] * n,
        out_specs=BS((None, tr, cols), lambda k, i: (k, i, 0)),
        out_shape=_sds((n, rows, cols), BF16),
        compiler_params=_cp(("arbitrary", "arbitrary")), name=name,
    )(*arrs)


def _quad_sum(own, got, name):
    n, rows, cols = own.shape
    tr = _row_tile(rows, cols)

    def body(a_ref, q_ref, o_ref):
        o_ref[...] = ((a_ref[...] + q_ref[0].astype(F32)) + q_ref[1].astype(F32)) + q_ref[2].astype(F32)

    spec = BS((None, tr, cols), lambda k, i: (k, i, 0))
    return pl.pallas_call(
        body, grid=(n, rows // tr),
        in_specs=[spec, BS((3, None, tr, cols), lambda k, i: (0, k, i, 0))], out_specs=spec,
        out_shape=_sds((n, rows, cols), F32),
        compiler_params=_cp(("arbitrary", "arbitrary")), name=name,
    )(own, got)


def _adam_math(w, g, m, v):
    m = ADAM_B1 * m + (1.0 - ADAM_B1) * g
    v = ADAM_B2 * v + (1.0 - ADAM_B2) * (g * g)
    m_hat = m / (1.0 - ADAM_B1 ** ADAM_STEP)
    v_hat = v / (1.0 - ADAM_B2 ** ADAM_STEP)
    delta = -ADAM_LR * (m_hat / (jnp.sqrt(v_hat) + ADAM_EPS) + ADAM_WD * w)
    return delta, m, v


def _adamw_big(w, g, m, v, name):
    shape = w.shape
    cols = shape[-1]
    rows = w.size // cols
    tr = _row_tile(rows, cols, limit=1 << 20)

    def body(w_ref, g_ref, m_ref, v_ref, d_ref, nm_ref, nv_ref):
        d, nm, nv = _adam_math(w_ref[...], g_ref[...], m_ref[...], v_ref[...])
        d_ref[...] = d
        nm_ref[...] = nm
        nv_ref[...] = nv

    spec = BS((tr, cols), lambda i: (i, 0))
    outs = pl.pallas_call(
        body, grid=(rows // tr,), in_specs=[spec] * 4, out_specs=[spec] * 3,
        out_shape=[_sds((rows, cols), F32)] * 3,
        compiler_params=_cp(("arbitrary",)), name=name,
    )(*[a.reshape(rows, cols) for a in (w, g, m, v)])
    return [o.reshape(shape) for o in outs]


def _adamw_small(ws, gs, ms, vs):
    n = len(ws)

    def body(*refs):
        for i in range(n):
            d, nm, nv = _adam_math(refs[i][...], refs[n + i][...], refs[2 * n + i][...],
                                   refs[3 * n + i][...])
            refs[4 * n + i][...] = d
            refs[5 * n + i][...] = nm
            refs[6 * n + i][...] = nv

    specs = [BS(w.shape, lambda i: (0, 0)) for w in ws]
    outs = pl.pallas_call(
        body, grid=(1,), in_specs=specs * 4, out_specs=specs * 3,
        out_shape=[_sds(w.shape, F32) for w in ws] * 3,
        compiler_params=_cp(("arbitrary",)), name="adamw_small",
    )(*ws, *gs, *ms, *vs)
    return outs[:n], outs[n:2 * n], outs[2 * n:]


def _place():
    x, y, c = lax.axis_index("x"), lax.axis_index("y"), lax.axis_index("c")
    chips = [(1 - x, y), (x, 1 - y), (1 - x, 1 - y)]
    return x, y, c, chips


class _Exchange:
    def __init__(self, ins, in_specs, out_shapes, scratch, start, finish):
        self.ins, self.in_specs, self.out_shapes, self.scratch = ins, in_specs, out_shapes, scratch
        self.start, self.finish = start, finish


def _run_exchange(ex, name, vmem_mb=40):
    ni, no = len(ex.ins), len(ex.out_shapes)

    def body(*refs):
        ex.start(refs[:ni], refs[ni:ni + no], refs[ni + no:])
        ex.finish(refs[:ni], refs[ni:ni + no], refs[ni + no:])

    return pl.pallas_call(
        body, in_specs=ex.in_specs, out_specs=[ANY] * no, out_shape=ex.out_shapes,
        scratch_shapes=ex.scratch,
        compiler_params=pltpu.CompilerParams(has_side_effects=True, vmem_limit_bytes=vmem_mb << 20),
        name=name,
    )(*ex.ins)


def _gather_exchange(srcs, dst_shapes, views, small=None):
    nu = len(srcs)
    nd = len(dst_shapes)
    ns = 1 if small is not None else 0

    def unpack(ins, outs, scr):
        x, y, c, chips = _place()
        src = ins[:nu]
        vw = [views[u](outs[:nd]) for u in range(nu)]
        vbuf = scr[:nu]
        send, recv, fsend, frecv, lsem, ssend, srecv, vsem = scr[nu:]

        def ici(u, j, shard, to):
            return pltpu.make_async_remote_copy(
                src_ref=vbuf[u].at[:, c], dst_ref=vw[u].at[:, shard, c],
                send_sem=send.at[3 * u + j], recv_sem=recv.at[3 * u + j],
                device_id=to, device_id_type=MESH)

        def fwd(u, j, shard, half):
            return pltpu.make_async_remote_copy(
                src_ref=vw[u].at[:, shard, half], dst_ref=vw[u].at[:, shard, half],
                send_sem=fsend.at[3 * u + j], recv_sem=frecv.at[3 * u + j],
                device_id=(x, y, 1 - c), device_id_type=MESH)

        def small_copy(j, shard, to):
            return pltpu.make_async_remote_copy(
                src_ref=ins[nu], dst_ref=outs[nd].at[shard],
                send_sem=ssend.at[j], recv_sem=srecv.at[j], device_id=to, device_id_type=MESH)

        stage = [pltpu.make_async_copy(src[u], vbuf[u], vsem.at[u]) for u in range(nu)]
        local = [pltpu.make_async_copy(vbuf[u], vw[u].at[:, 2 * x + y], lsem.at[u]) for u in range(nu)]
        if ns:
            local.append(pltpu.make_async_copy(ins[nu], outs[nd].at[2 * x + y], lsem.at[nu]))
        return x, y, c, chips, ici, fwd, small_copy, stage, local

    def start(ins, outs, scr):
        x, y, c, chips, ici, fwd, small_copy, stage, local = unpack(ins, outs, scr)
        s = 2 * x + y
        for cp in stage:
            cp.start()
        if ns:
            local[nu].start()
            for j, chip in enumerate(chips):
                small_copy(j, s, (*chip, c)).start()
        for u in range(nu):
            stage[u].wait()
            for j, chip in enumerate(chips):
                ici(u, j, s, (*chip, c)).start()
            local[u].start()

    def finish(ins, outs, scr):
        x, y, c, chips, ici, fwd, small_copy, stage, local = unpack(ins, outs, scr)
        s = 2 * x + y
        for u in range(nu):
            for j, chip in enumerate(chips):
                sj = 2 * chip[0] + chip[1]
                ici(u, j, sj, (x, y, c)).wait_recv()
                fwd(u, j, sj, c).start()
        for u in range(nu):
            for j, chip in enumerate(chips):
                sj = 2 * chip[0] + chip[1]
                fwd(u, j, sj, 1 - c).wait_recv()
        for u in range(nu):
            for j, chip in enumerate(chips):
                ici(u, j, s, (*chip, c)).wait_send()
                fwd(u, j, s, c).wait_send()
        if ns:
            for j, chip in enumerate(chips):
                small_copy(j, 2 * chip[0] + chip[1], (x, y, c)).wait_recv()
                small_copy(j, s, (*chip, c)).wait_send()
        for cp in local:
            cp.wait()

    dma = pltpu.SemaphoreType.DMA
    return _Exchange(
        ins=list(srcs) + ([small] if ns else []),
        in_specs=[ANY] * nu + [BS(memory_space=pltpu.VMEM)] * ns,
        out_shapes=[_sds(sh, BF16) for sh in dst_shapes]
        + ([_sds((NSH,) + small.shape, F32)] if ns else []),
        scratch=[pltpu.VMEM(a.shape, BF16) for a in srcs]
        + [dma((3 * nu,)), dma((3 * nu,)), dma((3 * nu,)), dma((3 * nu,)),
           dma((nu + 1,)), dma((3,)), dma((3,)), dma((nu,))],
        start=start, finish=finish)


def _pair_reduce(gs, name):
    nu = len(gs)
    ns = [g.shape[0] * NSH for g in gs]
    base = [sum(ns[:u]) for u in range(nu)]

    def body(*refs):
        g_refs, own_refs, sb_refs = refs[:nu], refs[nu:2 * nu], refs[2 * nu:3 * nu]
        bufs = refs[3 * nu:8 * nu]
        send, recv, lsem, osem = refs[8 * nu:]
        x, y, c, _ = _place()
        s = 2 * x + y

        def unit(u):
            sendb, recvb, stage, outf, outb = bufs[5 * u:5 * u + 5]

            def load(k, half):
                return pltpu.make_async_copy(g_refs[u].at[k // NSH, k % NSH, half], stage.at[k % 3],
                                             lsem.at[3 * u + k % 3])

            def push(k):
                return pltpu.make_async_remote_copy(
                    src_ref=sendb.at[k], dst_ref=recvb.at[k], send_sem=send.at[base[u] + k],
                    recv_sem=recv.at[base[u] + k], device_id=(x, y, 1 - c), device_id_type=MESH)

            def store(k):
                return pltpu.make_async_copy(outb.at[k % 2], sb_refs[u].at[k // NSH, k % NSH],
                                             osem.at[3 * u + k % 2])

            return sendb, recvb, stage, outf, outb, load, push, store

        for u in range(nu):
            sendb, recvb, stage, outf, outb, load, push, store = unit(u)
            for k in range(min(2, ns[u])):
                load(k, 1 - c).start()
            for k in range(ns[u]):
                if k + 2 < ns[u]:
                    load(k + 2, 1 - c).start()
                load(k, 1 - c).wait()
                sendb[k] = stage[k % 3].astype(BF16)
                push(k).start()
        for u in range(nu):
            sendb, recvb, stage, outf, outb, load, push, store = unit(u)
            n = ns[u]
            for k in range(min(2, n)):
                load(k, c).start()
            for k in range(n):
                if k + 2 < n:
                    load(k + 2, c).start()
                load(k, c).wait()
                push(k).wait_recv()
                total = stage[k % 3] + recvb[k].astype(F32)
                if k >= 2:
                    store(k - 2).wait()
                outb[k % 2] = total.astype(BF16)
                store(k).start()

                @pl.when(s == k % NSH)
                def _():
                    outf[...] = total
                    keep = pltpu.make_async_copy(outf, own_refs[u].at[k // NSH], osem.at[3 * u + 2])
                    keep.start()
                    keep.wait()

            for k in range(max(n - 2, 0), n):
                store(k).wait()
        for u in range(nu):
            push = unit(u)[6]
            for k in range(ns[u]):
                push(k).wait_send()

    dma = pltpu.SemaphoreType.DMA
    scratch = []
    for g, n in zip(gs, ns):
        rh, cc = g.shape[3], g.shape[4]
        scratch += [pltpu.VMEM((n, rh, cc), BF16), pltpu.VMEM((n, rh, cc), BF16),
                    pltpu.VMEM((3, rh, cc), F32), pltpu.VMEM((rh, cc), F32), pltpu.VMEM((2, rh, cc), BF16)]
    outs = pl.pallas_call(
        body, in_specs=[ANY] * nu, out_specs=[ANY] * (2 * nu),
        out_shape=[_sds((g.shape[0], g.shape[3], g.shape[4]), F32) for g in gs]
        + [_sds((g.shape[0], NSH, g.shape[3], g.shape[4]), BF16) for g in gs],
        scratch_shapes=scratch + [dma((sum(ns),)), dma((sum(ns),)), dma((3 * nu,)), dma((3 * nu,))],
        compiler_params=pltpu.CompilerParams(has_side_effects=True, vmem_limit_bytes=56 << 20),
        name=name,
    )(*gs)
    return list(outs[:nu]), list(outs[nu:])


def _chip_exchange(sums_bf16):
    nu = len(sums_bf16)

    def pushes(ins, outs, scr):
        x, y, c, chips = _place()
        send, recv = scr
        return [pltpu.make_async_remote_copy(
            src_ref=ins[u].at[:, 2 * chip[0] + chip[1]], dst_ref=outs[u].at[j],
            send_sem=send.at[3 * u + j], recv_sem=recv.at[3 * u + j],
            device_id=(*chip, c), device_id_type=MESH)
            for u in range(nu) for j, chip in enumerate(chips)]

    def start(ins, outs, scr):
        for cp in pushes(ins, outs, scr):
            cp.start()

    def finish(ins, outs, scr):
        for cp in pushes(ins, outs, scr):
            cp.wait()

    dma = pltpu.SemaphoreType.DMA
    shapes = [(3, a.shape[0], a.shape[2], a.shape[3]) for a in sums_bf16]
    return _Exchange(ins=list(sums_bf16), in_specs=[ANY] * nu,
                     out_shapes=[_sds(sh, BF16) for sh in shapes],
                     scratch=[dma((3 * nu,)), dma((3 * nu,))], start=start, finish=finish)


def _final_exchange(halves, out_shapes, targets):
    nu = len(halves)
    no = len(out_shapes)
    ncp = sum(len(t) for t in targets)

    def body(*refs):
        hv = refs[:nu]
        out = refs[nu:nu + no]
        sbuf = refs[nu + no:2 * nu + no]
        rbuf = refs[2 * nu + no:3 * nu + no]
        send, recv, lsem, osem, csem = refs[3 * nu + no:]
        x, y, c, _ = _place()
        stage = [pltpu.make_async_copy(hv[u], sbuf[u], lsem.at[u]) for u in range(nu)]
        push = [pltpu.make_async_remote_copy(
            src_ref=sbuf[u], dst_ref=rbuf[u], send_sem=send.at[u], recv_sem=recv.at[u],
            device_id=(x, y, 1 - c), device_id_type=MESH) for u in range(nu)]
        mine, theirs = [], []
        k = 0
        for u in range(nu):
            rh = hv[u].shape[1]
            for (p, oi, li) in targets[u]:
                mine.append((u, pltpu.make_async_copy(
                    sbuf[u].at[p], out[oi].at[li, pl.ds(c * rh, rh), :], csem.at[k])))
                theirs.append((u, pltpu.make_async_copy(
                    rbuf[u].at[p], out[oi].at[li, pl.ds((1 - c) * rh, rh), :], osem.at[k])))
                k += 1
        for cp in stage:
            cp.start()
        for u in range(nu):
            stage[u].wait()
            push[u].start()
            for (v, cp) in mine:
                if v == u:
                    cp.start()
        for u in range(nu):
            push[u].wait_recv()
            for (v, cp) in theirs:
                if v == u:
                    cp.start()
        for (_, cp) in theirs + mine:
            cp.wait()
        for u in range(nu):
            push[u].wait_send()

    dma = pltpu.SemaphoreType.DMA
    bufs = [pltpu.VMEM(h.shape, F32) for h in halves]
    return pl.pallas_call(
        body, in_specs=[ANY] * nu, out_specs=[ANY] * no,
        out_shape=[_sds(sh, F32) for sh in out_shapes],
        scratch_shapes=bufs + bufs + [dma((nu,)), dma((nu,)), dma((nu,)), dma((ncp,)), dma((ncp,))],
        compiler_params=pltpu.CompilerParams(has_side_effects=True, vmem_limit_bytes=56 << 20),
        name="final_exchange",
    )(*halves)


def _small_allreduce(pack):
    rows = pack.shape[0]

    def body(p_ref, o_ref, buf, send, recv):
        x, y, c, _ = _place()
        me = 4 * x + 2 * y + c
        buf[me] = p_ref[...]
        k = 0
        copies = []
        for dx in range(2):
            for dy in range(2):
                for dc in range(2):
                    if dx == 0 and dy == 0 and dc == 0:
                        continue
                    to = (jnp.where(dx, 1 - x, x), jnp.where(dy, 1 - y, y), jnp.where(dc, 1 - c, c))
                    src_slot = 4 * to[0] + 2 * to[1] + to[2]
                    copies.append((pltpu.make_async_remote_copy(
                        src_ref=p_ref, dst_ref=buf.at[me], send_sem=send.at[k], recv_sem=recv.at[k],
                        device_id=to, device_id_type=MESH), src_slot, k))
                    k += 1
        for cp, _, _ in copies:
            cp.start()
        for cp, src_slot, k in copies:
            pltpu.make_async_remote_copy(
                src_ref=p_ref, dst_ref=buf.at[src_slot], send_sem=send.at[k], recv_sem=recv.at[k],
                device_id=(x, y, c), device_id_type=MESH).wait()
        acc = buf[0]
        for d in range(1, 8):
            acc = acc + buf[d]
        o_ref[...] = acc

    dma = pltpu.SemaphoreType.DMA
    vm = BS(memory_space=pltpu.VMEM)
    return pl.pallas_call(
        body, in_specs=[vm], out_specs=vm, out_shape=_sds((rows, D), F32),
        scratch_shapes=[pltpu.VMEM((8, rows, D), F32), dma((7,)), dma((7,))],
        compiler_params=pltpu.CompilerParams(has_side_effects=True, vmem_limit_bytes=32 << 20),
        name="small_allreduce",
    )(pack)


def _in_proj(h, w, bias, name, transposed=False, carry=None):
    n = w.shape[0] if transposed else w.shape[1]
    tn = 1280 if n == NA else 1792
    ep = None
    extras, especs = (), ()
    if bias is not None:
        def ep(acc, ex, o_ref, i):
            o_ref[...] = acc + ex[0][...]
        extras = (bias,)
        especs = (BS((1, tn), lambda i, j, k: (0, j)),)
    b_spec = BS((tn, D), lambda i, j, k: (j, 0)) if transposed else BS((D, tn), lambda i, j, k: (0, j))
    return _mm("nt" if transposed else "nn", h, w, grid=(T // TR, n // tn, 1),
               a_spec=BS((TR, D), lambda i, j, k: (i, 0)), b_spec=b_spec,
               out_shape=_sds((T, n), F32), out_spec=BS((TR, tn), lambda i, j, k: (i, j)),
               acc_shape=(TR, tn), extras=extras, extra_specs=especs, epilogue=ep, name=name, carry=carry)


def _res_rms_epilogue(acc, ex, outs, i):
    y = acc + ex[0][...]
    outs[0][...] = y
    r = lax.rsqrt(jnp.mean(y * y, axis=-1, keepdims=True) + EPS)
    outs[1][...] = (y * r * ex[1][...]).astype(BF16)


def _res_loss_epilogue(acc, ex, outs, i):
    e = acc + ex[0][...] - ex[1][...]
    outs[1][...] = e * (1.0 / D)

    @pl.when(i == 0)
    def _():
        outs[0][...] = jnp.zeros_like(outs[0])

    outs[0][...] += 0.5 * jnp.sum(jnp.mean(e * e, axis=-1, keepdims=True), axis=0, keepdims=True)


def _rms_bwd_epilogue(dh, ex, outs, i):
    xv = ex[0][...]
    r = lax.rsqrt(jnp.mean(xv * xv, axis=-1, keepdims=True) + EPS)
    xh = xv * r
    gy = dh * ex[1][...]
    outs[0][...] = r * (gy - xh * jnp.mean(gy * xh, axis=-1, keepdims=True)) + ex[2][...]

    @pl.when(i == 0)
    def _():
        outs[1][...] = jnp.zeros_like(outs[1])

    outs[1][...] += jnp.sum(dh * xh, axis=0, keepdims=True)


def _local_step(x, mem, target, w, p, carries=None, bwd_carry_fn=None, late_carry_fn=None):
    row = lambda i, j, k: (i, 0)
    whole = lambda i, j, k: (0, 0)
    w = {k: (list(v) if isinstance(v, list) else v) for k, v in w.items()}
    carries = carries or {}

    def carry_of(name):
        return carries[name][0] if name in carries else None

    def delivered(name, outs):
        if name in carries:
            carries[name][1](w, outs)

    saved = []
    bias = _bias_expand(p["rel_u"])
    vec = BS((1, D), whole)
    h = _rms_fwd(x, p["norm1_g"][0:1], "rms1_0", carry=carry_of("rms1_0"))
    if carry_of("rms1_0") is not None:
        h, carried = h
        delivered("rms1_0", carried)
    for l in range(2):
        type_a = l == 0
        memn = _rms_fwd(mem, p["mem_norm_g"][l:l + 1], f"rmsmem_{l}")
        if type_a:
            z = _in_proj(h, w["a"], None, "inproj_a", carry=carry_of("inproj_a"))
            if carry_of("inproj_a") is not None:
                z, carried = z
                delivered("inproj_a", carried)
            cat, carried = _attn_fwd(z, bias, p["a_q_g2"], p["a_k_g2"], carry_of("attn_fwd"))
            delivered("attn_fwd", carried)
            qcol = NA // MEMW - 1
        else:
            z = _in_proj(h, w["b"], w["b_b_in"], "inproj_b", transposed=True)
            cat, conv_y = _conv_fwd(z, w["conv_w"], w["conv_b"], w["ln_g"], w["ln_b"])
            qcol = NBW // MEMW - 1
        kv = _mm("nn", memn, w["kv"][l], grid=(1, 1, 1),
                 a_spec=BS((NB * MEMT, D), whole), b_spec=BS((D, 2 * MEMW), whole),
                 out_shape=_sds((NB * MEMT, 2 * MEMW), F32), out_spec=BS((NB * MEMT, 2 * MEMW), whole),
                 acc_shape=(8, 128), name=f"memkv_{l}")
        cat = _memattn_fwd(z, kv, cat, p["mq_g4"][l:l + 1], p["mk_g4"][l:l + 1], qcol, f"memattn_fwd_{l}")
        x1, h2 = _mm("nn", cat, w["wo"][l], grid=(T // TR, 1, 1), a_spec=BS((TR, D), row),
                     b_spec=BS((D, D), whole),
                     out_shape=[_sds((T, D), F32), _sds((T, D), BF16)],
                     out_spec=[BS((TR, D), row), BS((TR, D), row)], acc_shape=(8, 128),
                     extras=(x, p["norm2_g"][l:l + 1]), extra_specs=(BS((TR, D), row), vec),
                     epilogue=_res_rms_epilogue, name=f"outproj_{l}")
        (g, u, act), carried = _ffn_up(h2, w["gu"][l], l, carry_of(f"ffn_up_{l}"))
        delivered(f"ffn_up_{l}", carried)
        last = l == 1
        res = _mm("nn", act, w["wd"][l], grid=(T // TR, 1, 1),
                  a_spec=BS((TR, FF), row), b_spec=BS((FF, D), whole),
                  out_shape=[_sds((8, 128), F32), _sds((T, D), F32)] if last else
                  [_sds((T, D), F32), _sds((T, D), BF16)],
                  out_spec=[BS((8, 128), whole), BS((TR, D), row)] if last else
                  [BS((TR, D), row), BS((TR, D), row)],
                  acc_shape=(8, 128), extras=(x1, target if last else p["norm1_g"][1:2]),
                  extra_specs=(BS((TR, D), row), BS((TR, D), row) if last else vec),
                  epilogue=_res_loss_epilogue if last else _res_rms_epilogue, sequential=last,
                  name=f"ffn_down_{l}", carry=carry_of(f"ffn_down_{l}"))
        if carry_of(f"ffn_down_{l}") is not None:
            res, carried = res
            delivered(f"ffn_down_{l}", carried)
        saved.append(dict(x=x, h=h, memn=memn, kv=kv, z=z, cat=cat, x1=x1, h2=h2, g=g, u=u, act=act,
                          qcol=qcol))
        if last:
            loss, dx = res
        else:
            x, h = res

    big = dict(a=None, b=None, kv=[None, None], wo=[None, None], gu=[None, None], wd=[None, None])
    small = {}
    bwd_carried, late_carried = (), ()
    tk = T // 2
    nkt = T // tk
    for l in (1, 0):
        sv = saved[l]
        dgu = _ffn_down_bwd(dx, w["wd"][l], sv["g"], sv["u"], l)
        big["wd"][l] = _mm("tn", sv["act"], dx, grid=(FF // FT, 1, 2 * nkt),
                           a_spec=BS((tk // 2, FT), lambda i, j, k: (k, i)),
                           b_spec=BS((tk // 2, D), lambda i, j, k: (k, 0)),
                           out_shape=_sds((FF, D), F32), out_spec=BS((FT, D), lambda i, j, k: (i, 0)),
                           acc_shape=(FT, D), name=f"dw_down_{l}")
        dx1, small[f"norm2_g{l}"] = _mm(
            "nn", dgu, w["gu"][l], grid=(T // TR, 1, 2),
            a_spec=BS((None, TR, FF), lambda i, j, k: (k, i, 0)),
            b_spec=BS((None, FF, D), lambda i, j, k: (k, 0, 0)),
            out_shape=[_sds((T, D), F32), _sds((1, D), F32)], out_spec=[BS((TR, D), row), vec],
            acc_shape=(TR, D), extras=(sv["x1"], p["norm2_g"][l:l + 1], dx),
            extra_specs=(BS((TR, D), row), vec, BS((TR, D), row)),
            epilogue=_rms_bwd_epilogue, sequential=True, name=f"dh2_{l}")
        big["gu"][l] = _mm("tn", dgu, sv["h2"], grid=(2 * FF // FT, 1, nkt),
                           a_spec=BS((None, tk, FT), lambda i, j, k: (i // 2, k, i % 2)),
                           b_spec=BS((tk, D), lambda i, j, k: (k, 0)),
                           out_shape=_sds((2, FF, D), F32),
                           out_spec=BS((None, FT, D), lambda i, j, k: (i // 2, i % 2, 0)),
                           acc_shape=(FT, D), vmem_mb=58, name=f"dw_gu_{l}")
        dcat = _mm("nt", dx1, w["wo"][l], grid=(T // TR, 1, 1), a_spec=BS((TR, D), row),
                   b_spec=BS((D, D), whole),
                   out_shape=_sds((T, D), F32), out_spec=BS((TR, D), row), acc_shape=(8, 128),
                   name=f"dcat_{l}")
        big["wo"][l] = _mm("tn", sv["cat"], dx1, grid=(1, 1, nkt),
                           a_spec=BS((tk, D), lambda i, j, k: (k, 0)), b_spec=BS((tk, D), lambda i, j, k: (k, 0)),
                           out_shape=_sds((D, D), F32), out_spec=BS((D, D), whole),
                           acc_shape=(D, D), name=f"dw_out_{l}")
        dqm, dkv, small[f"mq_g{l}"], small[f"mk_g{l}"] = _memattn_bwd(
            sv["z"], sv["kv"], dcat, p["mq_g4"][l:l + 1], p["mk_g4"][l:l + 1], sv["qcol"], f"memattn_bwd_{l}",
            dz_cols=NA if l == 0 else NBW)
        big["kv"][l] = _mm("tn", sv["memn"], dkv, grid=(1, 1, 1),
                           a_spec=BS((NB * MEMT, D), whole), b_spec=BS((NB * MEMT, 2 * MEMW), whole),
                           out_shape=_sds((D, 2 * MEMW), F32), out_spec=BS((D, 2 * MEMW), whole),
                           acc_shape=(8, 128), name=f"dw_kv_{l}")
        dmemn = _mm("nt", dkv, w["kv"][l], grid=(1, 1, 1),
                    a_spec=BS((NB * MEMT, 2 * MEMW), whole), b_spec=BS((D, 2 * MEMW), whole),
                    out_shape=_sds((NB * MEMT, D), F32), out_spec=BS((NB * MEMT, D), whole),
                    acc_shape=(8, 128), name=f"dmemn_{l}")
        _, small[f"mem_norm_g{l}"] = _rms_bwd(dmemn, mem, p["mem_norm_g"][l:l + 1], None, f"rmsmem_bwd_{l}")
        if l == 0:
            carry = bwd_carry_fn(big) if bwd_carry_fn is not None else None
            (dz, dbias, small["a_q_g"], small["a_k_g"]), bwd_carried = _attn_bwd(
                sv["z"], dcat, dqm, bias, p["a_q_g2"], p["a_k_g2"], carry)
            small["rel_u"] = _bias_reduce(dbias)
            w_in, key, n, tn = w["a"], "a", NA, 1280
        else:
            dz, small["conv_w"], small["conv_b"], small["ln_g"], small["ln_b"], dbin_u = _conv_bwd(
                sv["z"], conv_y, dcat, dqm, w["conv_w"], w["ln_g"], w["ln_b"])
            small["b_in_u"] = dbin_u
            small["b_in_qm"] = _colsum(dz, "colsum_dqm", cols=MEMW, block=NBW // MEMW - 1)
            w_in, key, n, tn = w["b"], "b", NBW, 896
        norm_bwd = dict(out_shape=[_sds((T, D), F32), _sds((1, D), F32)], out_spec=[BS((TR, D), row), vec],
                        acc_shape=(8, 128), extras=(sv["x"], p["norm1_g"][l:l + 1], dx1),
                        extra_specs=(BS((TR, D), row), vec, BS((TR, D), row)),
                        epilogue=_rms_bwd_epilogue, sequential=True, name=f"dh_{l}")
        if l == 0:
            big[key] = _mm("tn", sv["h"], dz, grid=(1, n // tn, nkt),
                           a_spec=BS((tk, D), lambda i, j, k: (k, 0)), b_spec=BS((tk, tn), lambda i, j, k: (k, j)),
                           out_shape=_sds((D, n), F32), out_spec=BS((D, tn), lambda i, j, k: (0, j)),
                           acc_shape=(D, tn), name=f"dw_in_{l}")
            carry = late_carry_fn(big) if late_carry_fn is not None else None
            res = _mm("nt", dz, w_in, grid=(T // TR, 1, 1), a_spec=BS((TR, n), row),
                      b_spec=BS((D, n), whole), carry=carry, **norm_bwd)
            if carry is not None:
                res, late_carried = res
            dx, small[f"norm1_g{l}"] = res
        else:
            dx, small[f"norm1_g{l}"] = _mm("nn", dz, w_in, grid=(T // TR, 1, 1),
                                           a_spec=BS((TR, n), row), b_spec=BS((n, D), whole), **norm_bwd)
            big[key] = _mm("tn", dz, sv["h"], grid=(n // tn, 1, nkt),
                           a_spec=BS((tk, tn), lambda i, j, k: (k, i)), b_spec=BS((tk, D), lambda i, j, k: (k, 0)),
                           out_shape=_sds((n, D), F32), out_spec=BS((tn, D), lambda i, j, k: (i, 0)),
                           acc_shape=(tn, D), name=f"dw_in_{l}")
    return loss, dx, big, small, bwd_carried, late_carried


def _colsum(a, name, cols=None, block=0):
    rows = a.shape[0]
    cols = cols or a.shape[1]

    def body(a_ref, o_ref):
        @pl.when(pl.program_id(0) == 0)
        def _():
            o_ref[...] = jnp.zeros_like(o_ref)

        o_ref[...] += jnp.sum(a_ref[...].astype(F32), axis=0, keepdims=True)

    return pl.pallas_call(
        body, grid=(rows // TR,), in_specs=[BS((TR, cols), lambda i: (i, block))],
        out_specs=BS((1, cols), lambda i: (0, 0)), out_shape=_sds((1, cols), F32),
        compiler_params=_cp(("arbitrary",)), name=name,
    )(a)


_PACK_ROWS = 64


def _pack_small(sm):
    plan = [("norm1_g0", 0, 1, 0, D), ("norm1_g1", 1, 1, 0, D), ("mem_norm_g0", 2, 1, 0, D),
            ("mem_norm_g1", 3, 1, 0, D), ("norm2_g0", 4, 1, 0, D), ("norm2_g1", 5, 1, 0, D),
            ("a_q_g", 6, 1, 0, AW), ("a_k_g", 7, 1, 0, AW), ("mq_g0", 8, 1, 0, MEMW),
            ("mq_g1", 9, 1, 0, MEMW), ("mk_g0", 10, 1, 0, MEMW), ("mk_g1", 11, 1, 0, MEMW),
            ("conv_b", 12, 1, 0, TOK), ("ln_g", 13, 1, 0, TOK), ("ln_b", 14, 1, 0, TOK),
            ("b_in_u", 15, 1, 0, D), ("b_in_u", 16, 1, D, 2 * TOK - D), ("b_in_qm", 17, 1, 0, MEMW),
            ("conv_w", 18, CONVW, 0, TOK), ("rel_u", 49, 12, 0, D), ("loss", 61, 1, 0, 128)]
    arrs = [sm[name].reshape(12, D) if name == "rel_u" else sm[name] for name, *_ in plan]

    def body(*refs):
        o_ref = refs[-1]
        o_ref[...] = jnp.zeros_like(o_ref)
        for ref, (_, r0, nr, c0, nc) in zip(refs, plan):
            o_ref[r0:r0 + nr, 0:nc] = ref[0:nr, c0:c0 + nc]

    return pl.pallas_call(
        body, grid=(1,), in_specs=[BS(a.shape, lambda i: (0, 0)) for a in arrs],
        out_specs=BS((_PACK_ROWS, D), lambda i: (0, 0)), out_shape=_sds((_PACK_ROWS, D), F32),
        compiler_params=_cp(("arbitrary",)), name="pack_small",
    )(*arrs)


def _rel_table_to_u(rel_bias):
    flat = jnp.concatenate([jnp.broadcast_to(rel_bias[:, 191:192], (12, 447)), rel_bias[:, ::-1]], axis=1)
    return jnp.pad(flat, ((0, 0), (192, 1024 - 192 - 639))).reshape(12, 1, 1024)


def _u_to_rel_table(du):
    flat = du[:, 192:192 + 639]
    g = flat[:, 447:][:, ::-1]
    return g, flat[:, :447]


def kernel(x, mem, norm1_g, mem_norm_g, a_w_in, a_q_g, a_k_g, a_rel_bias, b_w_in, b_b_in, b_conv_w, b_conv_b, b_ln_g, b_ln_b, mq_g, mk_g, w_mem_kv, w_out, norm2_g, w_gate, w_up, w_down, loss_target, m_norm1_g, m_mem_norm_g, m_a_w_in, m_a_q_g, m_a_k_g, m_a_rel_bias, m_b_w_in, m_b_b_in, m_b_conv_w, m_b_conv_b, m_b_ln_g, m_b_ln_b, m_mq_g, m_mk_g, m_w_mem_kv, m_w_out, m_norm2_g, m_w_gate, m_w_up, m_w_down, v_norm1_g, v_mem_norm_g, v_a_w_in, v_a_q_g, v_a_k_g, v_a_rel_bias, v_b_w_in, v_b_b_in, v_b_conv_w, v_b_conv_b, v_b_ln_g, v_b_ln_b, v_mq_g, v_mk_g, v_w_mem_kv, v_w_out, v_norm2_g, v_w_gate, v_w_up, v_w_down):
    sx = 2 * lax.axis_index("x") + lax.axis_index("y")

    n_in = (NA // NSH, NBW // NSH)
    tr = lambda a: jnp.swapaxes(a, -1, -2)
    small_src = jnp.concatenate([
        jnp.pad(b_b_in, ((0, 0), (0, 512 - 448))),
        jnp.pad(b_conv_w[0], ((0, 0), (0, 512 - 192))),
        jnp.pad(jnp.concatenate([b_conv_b, b_ln_g, b_ln_b], 0), ((0, 0), (0, 512 - 192))),
        jnp.zeros((5, 512), F32)], 0)

    def gather_groups(groups, small=None):
        def src_of(l, name):
            if name == "in":
                return [_cast_bf16([tr(b_w_in[0])], "cast_in_1").reshape(1, 2, n_in[1] // 2, D) if l else
                        _cast_bf16([a_w_in[0]], "cast_in_0").reshape(1, 2, D // 2, n_in[0])]
            if name == "gu":
                return [_cast_bf16([tr(w_gate[l])], f"cast_gate_{l}").reshape(1, 2, FS // 2, D),
                        _cast_bf16([tr(w_up[l])], f"cast_up_{l}").reshape(1, 2, FS // 2, D)]
            arr, shape = {"kv": (w_mem_kv, (1, 2, 128, 2 * MEMW)), "wo": (w_out, (1, 2, 128, D)),
                          "wd": (w_down, (1, 2, FS // 2, D))}[name]
            return [_cast_bf16([arr[l]], f"cast_{name}_{l}").reshape(shape)]

        def dst_of(l, name):
            if name == "in":
                return (1, NSH, 2, n_in[1] // 2, D) if l else (1, NSH, 2, D // 2, n_in[0])
            return {"kv": (1, NSH, 2, 128, 2 * MEMW), "wo": (1, NSH, 2, 128, D),
                    "gu": (1, 2, NSH, 2, FS // 2, D), "wd": (1, NSH, 2, FS // 2, D)}[name]

        items = [(l, name) for l, names in groups for name in names]
        srcs, views = [], []
        for k, (l, name) in enumerate(items):
            srcs += src_of(l, name)
            if name == "gu":
                views += [lambda d, k=k: d[k].at[:, 0], lambda d, k=k: d[k].at[:, 1]]
            else:
                views.append(lambda d, k=k: d[k])

        def done(w, outs):
            for k, (l, name) in enumerate(items):
                if name == "in" and l == 0:
                    w["a"] = outs[k].reshape(NSH, D, n_in[0]).transpose(1, 0, 2).reshape(D, NA)
                elif name == "in":
                    w["b"] = outs[k].reshape(NBW, D)
                else:
                    shape = {"kv": (D, 2 * MEMW), "wo": (D, D), "gu": (2, FF, D), "wd": (FF, D)}
                    w[name][l] = outs[k].reshape(shape[name])

        return _gather_exchange(srcs, [dst_of(l, name) for l, name in items], views, small), done

    w = dict(a=None, b=None, kv=[None, None], wo=[None, None], gu=[None, None], wd=[None, None])
    first, first_in_done = gather_groups([(0, ["in"])], small_src)

    def first_done(w, outs):
        first_in_done(w, outs)
        small_all = outs[1]
        conv_w_full = small_all[:, 1:1 + CONVW, :192].transpose(1, 0, 2).reshape(CONVW, TOK)
        vec3 = small_all[:, 32:35, :192].transpose(1, 0, 2).reshape(3, TOK)
        w.update(b_b_in=small_all[:, 0, :448].reshape(1, NBW), conv_w=jnp.pad(conv_w_full, ((0, 1), (0, 0))),
                 conv_b=vec3[0:1], ln_g=vec3[1:2], ln_b=vec3[2:3])

    carries = {"rms1_0": (first, first_done),
               "inproj_a": gather_groups([(0, ["kv", "wo", "wd"])]),
               "attn_fwd": gather_groups([(0, ["gu"]), (1, ["in", "kv", "wo"])]),
               "ffn_up_0": gather_groups([(1, ["gu"])]),
               "ffn_down_0": gather_groups([(1, ["wd"])])}
    p = dict(
        norm1_g=norm1_g, mem_norm_g=mem_norm_g, norm2_g=norm2_g,
        a_q_g2=jnp.tile(a_q_g, (1, AH)), a_k_g2=jnp.tile(a_k_g, (1, AH)),
        mq_g4=jnp.tile(mq_g, (1, 4)), mk_g4=jnp.tile(mk_g, (1, 4)),
        rel_u=_rel_table_to_u(a_rel_bias[0]))

    def pair_sums(items, big, name):
        units = []
        for l, tensor in items:
            if tensor == "in" and l == 0:
                g = big["a"].reshape(D, NSH, n_in[0]).transpose(1, 0, 2).reshape(1, NSH, 2, D // 2, n_in[0])
            elif tensor == "in":
                g = big["b"].reshape(1, NSH, 2, n_in[1] // 2, D)
            else:
                shape = {"kv": (1, NSH, 2, 128, 2 * MEMW), "wo": (1, NSH, 2, 128, D),
                         "gu": (2, NSH, 2, FS // 2, D), "wd": (1, NSH, 2, FS // 2, D)}
                g = big[tensor][l].reshape(shape[tensor])
            units.append(g)
        return _pair_reduce(units, name)

    early_groups = [[(1, "gu")], [(0, "gu")],
                    [(1, "in"), (1, "kv"), (1, "wo"), (1, "wd"), (0, "kv"), (0, "wo"), (0, "wd")]]
    early = [item for grp in early_groups for item in grp]
    late = [(0, "in")]
    own_early, own_late = [], []

    def bwd_carry_fn(big):
        sums_b = []
        for k, grp in enumerate(early_groups):
            own, sb = pair_sums(grp, big, f"pair_reduce_early_{k}")
            own_early.extend(own)
            sums_b.extend(sb)
        return _chip_exchange(sums_b)

    def late_carry_fn(big):
        own, sb = pair_sums(late, big, "pair_reduce_late")
        own_late.extend(own)
        return _chip_exchange(sb)

    loss, grad_x, big, small, parts_early, parts_late = _local_step(
        x.reshape(T, D), mem.reshape(NB * MEMT, D), loss_target.reshape(T, D), w, p,
        carries=carries, bwd_carry_fn=bwd_carry_fn, late_carry_fn=late_carry_fn)
    items = early + late
    halves = [_quad_sum(o, pt, f"quad_sum_{name}_{l}")
              for (l, name), o, pt in zip(items, own_early + own_late, list(parts_early) + list(parts_late))]
    out_shapes = [(1, D, NA // NSH), (1, NBW // NSH, D), (2, 2 * 128, 2 * MEMW), (2, 2 * 128, D),
                  (2, FS, D), (2, FS, D), (2, FS, D)]
    target_of = {"in": lambda l: [(0, l, 0)], "kv": lambda l: [(0, 2, l)], "wo": lambda l: [(0, 3, l)],
                 "gu": lambda l: [(0, 4, l), (1, 5, l)], "wd": lambda l: [(0, 6, l)]}
    targets = [target_of[name](l) for l, name in items]
    g_a, g_b, g_kv, g_wo, g_gate, g_up, g_wd = _final_exchange(halves, out_shapes, targets)

    tot = _small_allreduce(_pack_small(dict(small, loss=loss)))
    loss = tot[61, 0]
    g_rel, clip_part = _u_to_rel_table(tot[49:61])
    g_rel = jnp.concatenate([g_rel[:, :191], g_rel[:, 191:] + _rowsum(clip_part)], axis=1)
    b_in_full = jnp.concatenate([tot[15:16], tot[16:17, :512], tot[17:18, :MEMW]], axis=1)
    g_small = dict(
        norm1_g=tot[0:2], mem_norm_g=tot[2:4], norm2_g=tot[4:6],
        a_q_g=tot[6:7, :HD], a_k_g=tot[7:8, :HD], a_rel_bias=g_rel[None],
        b_b_in=lax.dynamic_slice(b_in_full, (0, sx * 448), (1, 448)),
        b_conv_w=lax.dynamic_slice(tot[18:49, :TOK], (0, sx * 192), (CONVW, 192))[None],
        b_conv_b=lax.dynamic_slice(tot[12:13, :TOK], (0, sx * 192), (1, 192)),
        b_ln_g=lax.dynamic_slice(tot[13:14, :TOK], (0, sx * 192), (1, 192)),
        b_ln_b=lax.dynamic_slice(tot[14:15, :TOK], (0, sx * 192), (1, 192)),
        mq_g=tot[8:10, :HD], mk_g=tot[10:12, :HD])

    names = ["norm1_g", "mem_norm_g", "a_w_in", "a_q_g", "a_k_g", "a_rel_bias", "b_w_in", "b_b_in",
             "b_conv_w", "b_conv_b", "b_ln_g", "b_ln_b", "mq_g", "mk_g", "w_mem_kv", "w_out",
             "norm2_g", "w_gate", "w_up", "w_down"]
    weights = dict(zip(names, [norm1_g, mem_norm_g, a_w_in, a_q_g, a_k_g, a_rel_bias, b_w_in, b_b_in,
                               b_conv_w, b_conv_b, b_ln_g, b_ln_b, mq_g, mk_g, w_mem_kv, w_out,
                               norm2_g, w_gate, w_up, w_down]))
    ms = dict(zip(names, [m_norm1_g, m_mem_norm_g, m_a_w_in, m_a_q_g, m_a_k_g, m_a_rel_bias, m_b_w_in,
                          m_b_b_in, m_b_conv_w, m_b_conv_b, m_b_ln_g, m_b_ln_b, m_mq_g, m_mk_g,
                          m_w_mem_kv, m_w_out, m_norm2_g, m_w_gate, m_w_up, m_w_down]))
    vs = dict(zip(names, [v_norm1_g, v_mem_norm_g, v_a_w_in, v_a_q_g, v_a_k_g, v_a_rel_bias, v_b_w_in,
                          v_b_b_in, v_b_conv_w, v_b_conv_b, v_b_ln_g, v_b_ln_b, v_mq_g, v_mk_g,
                          v_w_mem_kv, v_w_out, v_norm2_g, v_w_gate, v_w_up, v_w_down]))
    grads = dict(g_small)
    grads.update(a_w_in=g_a, b_w_in=g_b, w_mem_kv=g_kv, w_out=g_wo, w_gate=g_gate, w_up=g_up, w_down=g_wd)
    big_names = ["a_w_in", "b_w_in", "w_mem_kv", "w_out", "w_gate", "w_up", "w_down"]
    small_names = [n for n in names if n not in big_names]
    delta, new_m, new_v = {}, {}, {}
    for n in big_names:
        if n in ("b_w_in", "w_gate", "w_up"):
            outs = _adamw_big(tr(weights[n]), grads[n], tr(ms[n]), tr(vs[n]), f"adamw_{n}")
            delta[n], new_m[n], new_v[n] = [tr(o) for o in outs]
            grads[n] = tr(grads[n])
        else:
            delta[n], new_m[n], new_v[n] = _adamw_big(weights[n], grads[n], ms[n], vs[n], f"adamw_{n}")
    as2d = lambda a: a.reshape(-1, a.shape[-1])
    d_s, m_s, v_s = _adamw_small([as2d(weights[n]) for n in small_names], [as2d(grads[n]) for n in small_names],
                                 [as2d(ms[n]) for n in small_names], [as2d(vs[n]) for n in small_names])
    for i, n in enumerate(small_names):
        delta[n] = d_s[i].reshape(weights[n].shape)
        new_m[n] = m_s[i].reshape(weights[n].shape)
        new_v[n] = v_s[i].reshape(weights[n].shape)

    return (loss, grad_x.reshape(NB, SEQ, D), *[grads[n] for n in names], *[delta[n] for n in names],
            *[new_m[n] for n in names], *[new_v[n] for n in names])


def _rowsum(a):
    def body(a_ref, o_ref):
        o_ref[...] = jnp.sum(a_ref[...], axis=1, keepdims=True)

    vm = BS(memory_space=pltpu.VMEM)
    return pl.pallas_call(body, in_specs=[vm], out_specs=vm, out_shape=_sds((a.shape[0], 1), F32),
                          compiler_params=_cp(), name="rowsum")(a)
```

```python
import functools

import jax
import jax.numpy as jnp
from jax import lax
from jax.experimental import pallas as pl
from jax.experimental.pallas import tpu as pltpu

F32 = jnp.float32
BF16 = jnp.bfloat16
BS = pl.BlockSpec
ANY = pl.BlockSpec(memory_space=pl.ANY)
MESH = pl.DeviceIdType.MESH

D = 1024
SEQ = 2048
NB = 2
T = NB * SEQ
MEMT = 256
HD = 64
TOK = 768
MEMW = 256
NA = 3 * TOK + MEMW
NBW = 2 * TOK + MEMW
FF = 2816
NSH = 4
FS = FF // NSH
FT = FF // 2
CONVW = 31
EPS = 1e-6
NEG = -1e30
SCALE = HD ** -0.5
QB = 256
KWIN = 768
KPAD = 512
TR = 512

ADAM_LR = 0.001
ADAM_B1 = 0.9
ADAM_B2 = 0.999
ADAM_EPS = 1e-08
ADAM_WD = 0.01
ADAM_STEP = 10

_DIMS = {
    "nn": (((1,), (0,)), ((), ())),
    "nt": (((1,), (1,)), ((), ())),
    "tn": (((0,), (0,)), ((), ())),
}


def _cp(sem=None, vmem_mb=48):
    return pltpu.CompilerParams(dimension_semantics=sem, vmem_limit_bytes=vmem_mb << 20)


def _sds(shape, dtype):
    return jax.ShapeDtypeStruct(tuple(shape), dtype)


def _mm(mode, a, b, *, grid, a_spec, b_spec, out_shape, out_spec, acc_shape, name,
        extras=(), extra_specs=(), epilogue=None, carry=None, vmem_mb=48, sequential=False):
    n_ex = len(extras)
    nk = grid[2]
    dims = _DIMS[mode]
    ni = len(carry.ins) if carry else 0
    no = len(carry.out_shapes) if carry else 0
    multi = isinstance(out_shape, (list, tuple))
    out_shapes = list(out_shape) if multi else [out_shape]
    out_specs = list(out_spec) if multi else [out_spec]
    n_o = len(out_shapes)

    def body(a_ref, b_ref, *rest):
        ex = rest[:n_ex]
        cin = rest[n_ex:n_ex + ni]
        o_refs = rest[n_ex + ni:n_ex + ni + n_o]
        o_ref = o_refs if multi else o_refs[0]
        cout = rest[n_ex + ni + n_o:n_ex + ni + n_o + no]
        acc = rest[n_ex + ni + n_o + no]
        cscr = rest[n_ex + ni + n_o + no + 1:]
        ids = [pl.program_id(d) for d in range(3)]
        k = ids[2]
        if carry:
            @pl.when((ids[0] == 0) & (ids[1] == 0) & (ids[2] == 0))
            def _():
                carry.start(cin, cout, cscr)

        prod = lax.dot_general(a_ref[...].astype(BF16), b_ref[...].astype(BF16), dims,
                               preferred_element_type=F32)

        def finish(val):
            if epilogue is None:
                o_ref[...] = val.astype(o_ref.dtype)
            else:
                epilogue(val, ex, o_ref, ids[0])

        if nk == 1:
            finish(prod)
        else:
            @pl.when(k == 0)
            def _():
                acc[...] = prod

            @pl.when((k > 0) & (k < nk - 1))
            def _():
                acc[...] += prod

            @pl.when(k == nk - 1)
            def _():
                finish(acc[...] + prod)

        if carry:
            @pl.when((ids[0] == grid[0] - 1) & (ids[1] == grid[1] - 1) & (ids[2] == grid[2] - 1))
            def _():
                carry.finish(cin, cout, cscr)

    acc_scratch = pltpu.VMEM(acc_shape if nk > 1 else (8, 128), F32)
    ordered = sequential or bool(carry)
    outs = pl.pallas_call(
        body, grid=grid,
        in_specs=[a_spec, b_spec, *extra_specs] + (carry.in_specs if carry else []),
        out_specs=out_specs + [ANY] * no, out_shape=out_shapes + (carry.out_shapes if carry else []),
        scratch_shapes=[acc_scratch] + (carry.scratch if carry else []),
        compiler_params=pltpu.CompilerParams(
            dimension_semantics=("arbitrary",) * 3 if ordered else ("parallel", "parallel", "arbitrary"),
            vmem_limit_bytes=vmem_mb << 20, has_side_effects=bool(carry)), name=name,
    )(a, b, *extras, *(carry.ins if carry else []))
    mine = list(outs[:n_o]) if multi else outs[0]
    return (mine, outs[n_o:]) if carry else mine


def _rms_fwd(x, g, name, carry=None):
    rows = x.shape[0]
    ni = len(carry.ins) if carry else 0
    no = len(carry.out_shapes) if carry else 0

    def body(x_ref, g_ref, *rest):
        cin, o_ref, cout, cscr = rest[:ni], rest[ni], rest[ni + 1:ni + 1 + no], rest[ni + 1 + no:]
        if carry:
            @pl.when(pl.program_id(0) == 0)
            def _():
                carry.start(cin, cout, cscr)

        xv = x_ref[...]
        r = lax.rsqrt(jnp.mean(xv * xv, axis=-1, keepdims=True) + EPS)
        o_ref[...] = (xv * r * g_ref[...]).astype(BF16)

        if carry:
            @pl.when(pl.program_id(0) == rows // TR - 1)
            def _():
                carry.finish(cin, cout, cscr)

    outs = pl.pallas_call(
        body, grid=(rows // TR,),
        in_specs=[BS((TR, D), lambda i: (i, 0)), BS((1, D), lambda i: (0, 0))]
        + (carry.in_specs if carry else []),
        out_specs=[BS((TR, D), lambda i: (i, 0))] + [ANY] * no,
        out_shape=[_sds((rows, D), BF16)] + (carry.out_shapes if carry else []),
        scratch_shapes=carry.scratch if carry else [],
        compiler_params=pltpu.CompilerParams(
            dimension_semantics=("arbitrary",), vmem_limit_bytes=48 << 20,
            has_side_effects=bool(carry)), name=name,
    )(x, g, *(carry.ins if carry else []))
    return (outs[0], outs[1:]) if carry else outs[0]


def _rms_bwd(dh, x, g, dres, name):
    rows = x.shape[0]
    has_res = dres is not None

    def body(*refs):
        if has_res:
            dh_ref, x_ref, g_ref, r_ref, dx_ref, dg_ref = refs
        else:
            dh_ref, x_ref, g_ref, dx_ref, dg_ref = refs
        xv = x_ref[...]
        dhv = dh_ref[...]
        r = lax.rsqrt(jnp.mean(xv * xv, axis=-1, keepdims=True) + EPS)
        xh = xv * r
        gy = dhv * g_ref[...]
        dx = r * (gy - xh * jnp.mean(gy * xh, axis=-1, keepdims=True))
        if has_res:
            dx = dx + r_ref[...]
        dx_ref[...] = dx

        @pl.when(pl.program_id(0) == 0)
        def _():
            dg_ref[...] = jnp.zeros_like(dg_ref)

        dg_ref[...] += jnp.sum(dhv * xh, axis=0, keepdims=True)

    row = BS((TR, D), lambda i: (i, 0))
    vec = BS((1, D), lambda i: (0, 0))
    ins = [dh, x, g] + ([dres] if has_res else [])
    return pl.pallas_call(
        body, grid=(rows // TR,),
        in_specs=[row, row, vec] + ([row] if has_res else []),
        out_specs=[row, vec], out_shape=[_sds((rows, D), F32), _sds((1, D), F32)],
        compiler_params=_cp(("arbitrary",)), name=name,
    )(*ins)


def _group_masks(width):
    lane = lax.broadcasted_iota(jnp.int32, (1, width), 1)
    return [(lane >= HD * h) & (lane < HD * (h + 1)) for h in range(width // HD)]


def _group_mean(v, masks):
    del masks
    low = lax.broadcasted_iota(jnp.int32, (1, 128), 1) < HD
    slabs = []
    for j in range(v.shape[-1] // 128):
        x = v[:, 128 * j:128 * (j + 1)]
        s0 = jnp.sum(jnp.where(low, x, 0.0), axis=-1, keepdims=True) * (1.0 / HD)
        s1 = jnp.sum(jnp.where(low, 0.0, x), axis=-1, keepdims=True) * (1.0 / HD)
        slabs.append(jnp.where(low, s0, s1))
    return slabs[0] if len(slabs) == 1 else jnp.concatenate(slabs, axis=-1)


def _head_norm(zv, g, masks):
    r = lax.rsqrt(_group_mean(zv * zv, masks) + EPS)
    return zv * r * g


def _head_norm_bwd(dy, zv, g, masks):
    r = lax.rsqrt(_group_mean(zv * zv, masks) + EPS)
    zh = zv * r
    gy = dy * g
    dz = r * (gy - zh * _group_mean(gy * zh, masks))
    return dz, jnp.sum(dy * zh, axis=0, keepdims=True)


def _fold_heads(v, width):
    vb = jnp.broadcast_to(v, (8, width))
    out = vb
    for h in range(1, width // HD):
        out = out + pltpu.roll(vb, width - HD * h, axis=1)
    return out[0:1]


def _bias_expand(u):
    def body(u_ref, o_ref):
        x = jnp.broadcast_to(u_ref[...], (QB, 1024))
        rolled = pltpu.roll(x, 1024 - (QB - 1), axis=1, stride=1, stride_axis=0)[:, :KWIN]
        row = lax.broadcasted_iota(jnp.int32, (QB, 1), 0)
        col = lax.broadcasted_iota(jnp.int32, (1, KWIN), 1)
        lo = (row // 64) * 64
        ok = (col >= lo) & (col < lo + 576)
        o_ref[...] = jnp.where(ok, rolled, NEG)

    return pl.pallas_call(
        body, grid=(12,), in_specs=[BS((None, 1, 1024), lambda h: (h, 0, 0))],
        out_specs=BS((None, QB, KWIN), lambda h: (h, 0, 0)), out_shape=_sds((12, QB, KWIN), F32),
        compiler_params=_cp(("arbitrary",)), name="bias_expand",
    )(u)


def _bias_reduce(ds):
    def body(d_ref, o_ref):
        ri = lax.broadcasted_iota(jnp.int32, (QB, QB), 0)
        ci = lax.broadcasted_iota(jnp.int32, (QB, QB), 1)
        flip = (ri + ci == QB - 1).astype(F32)
        drev = jnp.dot(flip, d_ref[...], precision=lax.Precision.HIGHEST, preferred_element_type=F32)
        x = jnp.concatenate([drev, jnp.zeros((QB, 1024 - KWIN), F32)], axis=1)
        rolled = pltpu.roll(x, 0, axis=1, stride=1, stride_axis=0)
        o_ref[...] = jnp.sum(rolled, axis=0, keepdims=True)

    return pl.pallas_call(
        body, grid=(12,), in_specs=[BS((None, QB, KWIN), lambda h: (h, 0, 0))],
        out_specs=BS((None, 1, 1024), lambda h: (h, 0, 0)), out_shape=_sds((12, 1, 1024), F32),
        compiler_params=_cp(("arbitrary",)), name="bias_reduce",
    )(ds)


AW = 256
AH = AW // HD
AG = TOK // AW
def _attn_softmax(qh, kw, bias, startadd):
    s = lax.dot_general(qh, kw, _DIMS["nt"], preferred_element_type=F32) + bias + startadd
    m = jnp.max(s, axis=-1, keepdims=True)
    p = jnp.exp(s - m)
    return p * (1.0 / jnp.sum(p, axis=-1, keepdims=True))


def _attn_prologue(q_ref, k_ref, v_ref, gq_ref, gk_ref, qn_s, kn_s, v_s, masks):
    kn_s[0:KPAD, :] = jnp.zeros((KPAD, AW), BF16)
    v_s[0:KPAD, :] = jnp.zeros((KPAD, AW), BF16)
    for r in range(0, SEQ, TR):
        qn_s[r:r + TR, :] = (_head_norm(q_ref[r:r + TR, :], gq_ref[...], masks) * SCALE).astype(BF16)
        kn_s[KPAD + r:KPAD + r + TR, :] = _head_norm(k_ref[r:r + TR, :], gk_ref[...], masks).astype(BF16)
        v_s[KPAD + r:KPAD + r + TR, :] = v_ref[r:r + TR, :].astype(BF16)


def _attn_fwd(z, bias, gq2, gk2, carry=None):
    ni = len(carry.ins) if carry else 0
    no = len(carry.out_shapes) if carry else 0

    def body(q_ref, k_ref, v_ref, b_ref, gq_ref, gk_ref, *rest):
        cin, o_ref, cout = rest[:ni], rest[ni], rest[ni + 1:ni + 1 + no]
        qn_s, kn_s, v_s = rest[ni + 1 + no:ni + 4 + no]
        cscr = rest[ni + 4 + no:]
        if carry:
            @pl.when((pl.program_id(0) == 0) & (pl.program_id(1) == 0))
            def _():
                carry.start(cin, cout, cscr)

        masks = _group_masks(AW)
        _attn_prologue(q_ref, k_ref, v_ref, gq_ref, gk_ref, qn_s, kn_s, v_s, masks)
        col = lax.broadcasted_iota(jnp.int32, (1, KWIN), 1)

        def blk(i, carry):
            r0 = pl.multiple_of(i * QB, QB)
            qb = qn_s[pl.ds(r0, QB), :]
            kw = kn_s[pl.ds(r0, KWIN), :]
            vw = v_s[pl.ds(r0, KWIN), :]
            startadd = jnp.where(col + r0 < KPAD, NEG, 0.0)
            o = jnp.zeros((QB, AW), F32)
            for h in range(AH):
                qh = jnp.where(masks[h], qb, jnp.zeros_like(qb))
                vh = jnp.where(masks[h], vw, jnp.zeros_like(vw))
                p = _attn_softmax(qh, kw, b_ref[h], startadd).astype(BF16)
                o = o + jnp.dot(p, vh, preferred_element_type=F32)
            o_ref[pl.ds(r0, QB), :] = o.astype(BF16)
            return carry

        lax.fori_loop(0, SEQ // QB, blk, 0)

        if carry:
            @pl.when((pl.program_id(0) == NB - 1) & (pl.program_id(1) == AG - 1))
            def _():
                carry.finish(cin, cout, cscr)

    vec = BS((1, AW), lambda b, hp: (0, 0))
    outs = pl.pallas_call(
        body, grid=(NB, AG),
        in_specs=[BS((SEQ, AW), lambda b, hp: (b, hp)),
                  BS((SEQ, AW), lambda b, hp: (b, AG + hp)),
                  BS((SEQ, AW), lambda b, hp: (b, 2 * AG + hp)),
                  BS((AH, QB, KWIN), lambda b, hp: (hp, 0, 0)), vec, vec]
        + (carry.in_specs if carry else []),
        out_specs=[BS((SEQ, AW), lambda b, hp: (b, hp))] + [ANY] * no,
        out_shape=[_sds((T, D), BF16)] + (carry.out_shapes if carry else []),
        scratch_shapes=[pltpu.VMEM((SEQ, AW), BF16), pltpu.VMEM((SEQ + KPAD, AW), BF16),
                        pltpu.VMEM((SEQ + KPAD, AW), BF16)] + (carry.scratch if carry else []),
        compiler_params=pltpu.CompilerParams(
            dimension_semantics=("arbitrary", "arbitrary"), vmem_limit_bytes=48 << 20,
            has_side_effects=bool(carry)), name="attn_fwd",
    )(z, z, z, bias, gq2, gk2, *(carry.ins if carry else []))
    return outs[0], outs[1:]


def _attn_bwd(z, dcat, dz_into, bias, gq2, gk2, carry=None):
    ni = len(carry.ins) if carry else 0
    no = len(carry.out_shapes) if carry else 0

    def body(q_ref, k_ref, v_ref, do_ref, b_ref, gq_ref, gk_ref, into_ref, *rest):
        del into_ref
        cin = rest[:ni]
        dz_ref, db_ref, dgq_ref, dgk_ref = rest[ni:ni + 4]
        cout = rest[ni + 4:ni + 4 + no]
        qn_s, kn_s, v_s, dqn_s, dkn_s, dv_s, out_s, out_sem = rest[ni + 4 + no:ni + 12 + no]
        cscr = rest[ni + 12 + no:]
        hp = pl.program_id(0)
        b = pl.program_id(1)
        step = hp * NB + b

        def out_copy(j, step):
            rows = pl.ds(pl.multiple_of((step % NB) * SEQ, SEQ), SEQ)
            cols = pl.ds(pl.multiple_of((j * AG + step // NB) * AW, AW), AW)
            return pltpu.make_async_copy(out_s.at[j], dz_ref.at[rows, cols], out_sem.at[j])

        if carry:
            @pl.when((hp == 0) & (b == 0))
            def _():
                carry.start(cin, cout, cscr)

        masks = _group_masks(AW)
        _attn_prologue(q_ref, k_ref, v_ref, gq_ref, gk_ref, qn_s, kn_s, v_s, masks)
        dkn_s[...] = jnp.zeros_like(dkn_s)
        dv_s[...] = jnp.zeros_like(dv_s)

        @pl.when(b == 0)
        def _():
            db_ref[...] = jnp.zeros_like(db_ref)

        @pl.when((b == 0) & (hp == 0))
        def _():
            dgq_ref[...] = jnp.zeros_like(dgq_ref)
            dgk_ref[...] = jnp.zeros_like(dgk_ref)

        col = lax.broadcasted_iota(jnp.int32, (1, KWIN), 1)

        def blk(i, carry):
            r0 = pl.multiple_of(i * QB, QB)
            qb = qn_s[pl.ds(r0, QB), :]
            kw = kn_s[pl.ds(r0, KWIN), :]
            vw = v_s[pl.ds(r0, KWIN), :]
            dob = do_ref[pl.ds(r0, QB), :].astype(BF16)
            startadd = jnp.where(col + r0 < KPAD, NEG, 0.0)
            dqn = jnp.zeros((QB, AW), F32)
            dkw = jnp.zeros((KWIN, AW), F32)
            dvw = jnp.zeros((KWIN, AW), F32)
            for h in range(AH):
                qh = jnp.where(masks[h], qb, jnp.zeros_like(qb))
                kh = jnp.where(masks[h], kw, jnp.zeros_like(kw))
                doh = jnp.where(masks[h], dob, jnp.zeros_like(dob))
                p = _attn_softmax(qh, kw, b_ref[h], startadd)
                dvw = dvw + lax.dot_general(p.astype(BF16), doh, _DIMS["tn"],
                                            preferred_element_type=F32)
                dp = lax.dot_general(doh, vw, _DIMS["nt"], preferred_element_type=F32)
                ds = p * (dp - jnp.sum(dp * p, axis=-1, keepdims=True))
                db_ref[h] += ds
                dsb = ds.astype(BF16)
                dqn = dqn + jnp.dot(dsb, kh, preferred_element_type=F32)
                dkw = dkw + lax.dot_general(dsb, qh, _DIMS["tn"], preferred_element_type=F32)
            dqn_s[pl.ds(r0, QB), :] = dqn * SCALE
            dkn_s[pl.ds(r0, KWIN), :] += dkw
            dv_s[pl.ds(r0, KWIN), :] += dvw
            return carry

        lax.fori_loop(0, SEQ // QB, blk, 0)

        @pl.when(step > 0)
        def _():
            for j in range(3):
                out_copy(j, step - 1).wait()

        dgq = jnp.zeros((1, AW), F32)
        dgk = jnp.zeros((1, AW), F32)
        for r in range(0, SEQ, TR):
            dq, dg = _head_norm_bwd(dqn_s[r:r + TR, :], q_ref[r:r + TR, :], gq_ref[...], masks)
            out_s[0, r:r + TR, :] = dq.astype(BF16)
            dgq = dgq + dg
            dk, dg = _head_norm_bwd(dkn_s[KPAD + r:KPAD + r + TR, :], k_ref[r:r + TR, :], gk_ref[...], masks)
            out_s[1, r:r + TR, :] = dk.astype(BF16)
            dgk = dgk + dg
            out_s[2, r:r + TR, :] = dv_s[KPAD + r:KPAD + r + TR, :].astype(BF16)
        dgq_ref[...] += _fold_heads(dgq, AW)
        dgk_ref[...] += _fold_heads(dgk, AW)
        for j in range(3):
            out_copy(j, step).start()

        @pl.when(step == AG * NB - 1)
        def _():
            for j in range(3):
                out_copy(j, step).wait()

        if carry:
            @pl.when((hp == AG - 1) & (b == NB - 1))
            def _():
                carry.finish(cin, cout, cscr)

    vec = BS((1, AW), lambda hp, b: (0, 0))
    row = BS((SEQ, AW), lambda hp, b: (b, hp))
    outs = pl.pallas_call(
        body, grid=(AG, NB),
        in_specs=[row,
                  BS((SEQ, AW), lambda hp, b: (b, AG + hp)),
                  BS((SEQ, AW), lambda hp, b: (b, 2 * AG + hp)),
                  row,
                  BS((AH, QB, KWIN), lambda hp, b: (hp, 0, 0)), vec, vec, ANY]
        + (carry.in_specs if carry else []),
        out_specs=[ANY, BS((AH, QB, KWIN), lambda hp, b: (hp, 0, 0)), vec, vec] + [ANY] * no,
        out_shape=[_sds((T, NA), BF16), _sds((12, QB, KWIN), F32), _sds((1, AW), F32), _sds((1, AW), F32)]
        + (carry.out_shapes if carry else []),
        input_output_aliases={7: 0},
        scratch_shapes=[pltpu.VMEM((SEQ, AW), BF16), pltpu.VMEM((SEQ + KPAD, AW), BF16),
                        pltpu.VMEM((SEQ + KPAD, AW), BF16), pltpu.VMEM((SEQ, AW), F32),
                        pltpu.VMEM((SEQ + KPAD, AW), F32), pltpu.VMEM((SEQ + KPAD, AW), F32),
                        pltpu.VMEM((3, SEQ, AW), BF16), pltpu.SemaphoreType.DMA((3,))]
        + (carry.scratch if carry else []),
        compiler_params=pltpu.CompilerParams(
            dimension_semantics=("arbitrary", "arbitrary"), vmem_limit_bytes=58 << 20,
            has_side_effects=bool(carry)), name="attn_bwd",
    )(z, z, z, dcat, bias, gq2, gk2, dz_into, *(carry.ins if carry else []))
    return outs[:4], outs[4:]


def _mem_softmax(qh, kn):
    s = lax.dot_general(qh, kn, _DIMS["nt"], preferred_element_type=F32)
    m = jnp.max(s, axis=-1, keepdims=True)
    p = jnp.exp(s - m)
    return p * (1.0 / jnp.sum(p, axis=-1, keepdims=True))


def _memattn_fwd(z, kv, cat, gq4, gk4, qcol, name):
    def body(q_ref, k_ref, v_ref, gq_ref, gk_ref, cat_ref, o_ref):
        del cat_ref
        masks = _group_masks(MEMW)
        qn = (_head_norm(q_ref[...], gq_ref[...], masks) * SCALE).astype(BF16)
        kn = _head_norm(k_ref[...], gk_ref[...], masks).astype(BF16)
        vv = v_ref[...].astype(BF16)
        o = jnp.zeros((TR, MEMW), F32)
        for h in range(4):
            qh = jnp.where(masks[h], qn, jnp.zeros_like(qn))
            vh = jnp.where(masks[h], vv, jnp.zeros_like(vv))
            p = _mem_softmax(qh, kn).astype(BF16)
            o = o + jnp.dot(p, vh, preferred_element_type=F32)
        o_ref[...] = o.astype(BF16)

    nt = SEQ // TR
    vec = BS((1, MEMW), lambda b, t: (0, 0))
    return pl.pallas_call(
        body, grid=(NB, nt),
        in_specs=[BS((TR, MEMW), lambda b, t: (b * nt + t, qcol)),
                  BS((MEMT, MEMW), lambda b, t: (b, 0)),
                  BS((MEMT, MEMW), lambda b, t: (b, 1)), vec, vec, ANY],
        out_specs=BS((TR, MEMW), lambda b, t: (b * nt + t, 3)),
        out_shape=_sds((T, D), BF16), input_output_aliases={5: 0},
        compiler_params=_cp(("arbitrary", "arbitrary")), name=name,
    )(z, kv, kv, gq4, gk4, cat)


def _memattn_bwd(z, kv, dcat, gq4, gk4, qcol, name, dz_cols=MEMW):
    nt = SEQ // TR

    def body(q_ref, k_ref, v_ref, do_ref, gq_ref, gk_ref,
             dq_ref, dkv_ref, dgq_ref, dgk_ref, dqsum_ref, dkn_s, dv_s):
        b = pl.program_id(0)
        t = pl.program_id(1)
        masks = _group_masks(MEMW)
        qz = q_ref[...]
        kz = k_ref[...]
        qn = (_head_norm(qz, gq_ref[...], masks) * SCALE).astype(BF16)
        kn = _head_norm(kz, gk_ref[...], masks).astype(BF16)
        vv = v_ref[...].astype(BF16)
        dob = do_ref[...].astype(BF16)

        @pl.when(t == 0)
        def _():
            dkn_s[...] = jnp.zeros_like(dkn_s)
            dv_s[...] = jnp.zeros_like(dv_s)

        @pl.when((t == 0) & (b == 0))
        def _():
            dgq_ref[...] = jnp.zeros_like(dgq_ref)
            dgk_ref[...] = jnp.zeros_like(dgk_ref)
            dqsum_ref[...] = jnp.zeros_like(dqsum_ref)

        dqn = jnp.zeros((TR, MEMW), F32)
        dkn = jnp.zeros((MEMT, MEMW), F32)
        dvv = jnp.zeros((MEMT, MEMW), F32)
        for h in range(4):
            qh = jnp.where(masks[h], qn, jnp.zeros_like(qn))
            kh = jnp.where(masks[h], kn, jnp.zeros_like(kn))
            doh = jnp.where(masks[h], dob, jnp.zeros_like(dob))
            p = _mem_softmax(qh, kn)
            dvv = dvv + lax.dot_general(p.astype(BF16), doh, _DIMS["tn"], preferred_element_type=F32)
            dp = lax.dot_general(doh, vv, _DIMS["nt"], preferred_element_type=F32)
            ds = p * (dp - jnp.sum(dp * p, axis=-1, keepdims=True))
            dsb = ds.astype(BF16)
            dqn = dqn + jnp.dot(dsb, kh, preferred_element_type=F32)
            dkn = dkn + lax.dot_general(dsb, qh, _DIMS["tn"], preferred_element_type=F32)
        dkn_s[...] += dkn
        dv_s[...] += dvv
        dq, dgq = _head_norm_bwd(dqn * SCALE, qz, gq_ref[...], masks)
        dq_ref[...] = dq.astype(BF16)
        dgq_ref[...] += _fold_heads(dgq, MEMW)
        dqsum_ref[...] += jnp.sum(dq, axis=0, keepdims=True)

        @pl.when(t == nt - 1)
        def _():
            dk, dgk = _head_norm_bwd(dkn_s[...], kz, gk_ref[...], masks)
            dkv_ref[:, 0:MEMW] = dk
            dkv_ref[:, MEMW:] = dv_s[...]
            dgk_ref[...] += _fold_heads(dgk, MEMW)

    vec = BS((1, MEMW), lambda b, t: (0, 0))
    return pl.pallas_call(
        body, grid=(NB, nt),
        in_specs=[BS((TR, MEMW), lambda b, t: (b * nt + t, qcol)),
                  BS((MEMT, MEMW), lambda b, t: (b, 0)),
                  BS((MEMT, MEMW), lambda b, t: (b, 1)),
                  BS((TR, MEMW), lambda b, t: (b * nt + t, 3)), vec, vec],
        out_specs=[BS((TR, MEMW), lambda b, t: (b * nt + t, dz_cols // MEMW - 1)),
                   BS((MEMT, 2 * MEMW), lambda b, t: (b, 0)), vec, vec, vec],
        out_shape=[_sds((T, dz_cols), BF16), _sds((NB * MEMT, 2 * MEMW), F32),
                   _sds((1, MEMW), F32), _sds((1, MEMW), F32), _sds((1, MEMW), F32)],
        scratch_shapes=[pltpu.VMEM((MEMT, MEMW), F32), pltpu.VMEM((MEMT, MEMW), F32)],
        compiler_params=_cp(("arbitrary", "arbitrary")), name=name,
    )(z, kv, kv, dcat, gq4, gk4)


HALO = 32
NEXT = 64
RT = 64


def _glu(zz):
    return zz[:, :TOK] * jax.nn.sigmoid(zz[:, TOK:])


def _layer_norm_parts(y):
    mu = jnp.mean(y, axis=-1, keepdims=True)
    yc = y - mu
    rstd = lax.rsqrt(jnp.mean(yc * yc, axis=-1, keepdims=True) + EPS)
    return yc * rstd, rstd


def _shifted_copies(src, dst, rows):
    for b in range(1, 8):
        dst[b - 1, 0:rows, :] = src[b:b + rows, :]


def _tap(src, shifted, off, r0, rows):
    b = off % 8
    if b == 0:
        return src[r0 + off:r0 + off + rows, :]
    return shifted[b - 1, r0 + off - b:r0 + off - b + rows, :]


def _conv_rows(w_ref, hbuf, hs, r0, rows):
    y = jnp.zeros((rows, TOK), F32)
    for j in range(CONVW):
        y = y + w_ref[j:j + 1, :] * _tap(hbuf, hs, (HALO - CONVW + 1) + j, r0, rows)
    return y


def _conv_fwd(z, cw, cb, lg, lb):
    nt = SEQ // TR

    def body(zc_ref, zp_ref, w_ref, cb_ref, lg_ref, lb_ref, o_ref, y_ref, hbuf, hs):
        t = pl.program_id(1)
        hbuf[0:HALO, :] = jnp.where(t == 0, 0.0, _glu(zp_ref[...]))
        hbuf[HALO:, :] = _glu(zc_ref[...])
        _shifted_copies(hbuf, hs, HALO + TR - 8)
        for r0 in range(0, TR, RT):
            y = _conv_rows(w_ref, hbuf, hs, r0, RT) + cb_ref[...]
            y_ref[r0:r0 + RT, :] = y
            yh, _ = _layer_norm_parts(y)
            o = yh * lg_ref[...] + lb_ref[...]
            o_ref[r0:r0 + RT, :] = (o * jax.nn.sigmoid(o)).astype(BF16)

    vec = BS((1, TOK), lambda b, t: (0, 0))
    per = TR // HALO
    return pl.pallas_call(
        body, grid=(NB, nt),
        in_specs=[BS((TR, 2 * TOK), lambda b, t: (b * nt + t, 0)),
                  BS((HALO, 2 * TOK), lambda b, t: (jnp.maximum((b * nt + t) * per - 1, 0), 0)),
                  BS((32, TOK), lambda b, t: (0, 0)), vec, vec, vec],
        out_specs=[BS((TR, TOK), lambda b, t: (b * nt + t, 0)), BS((TR, TOK), lambda b, t: (b * nt + t, 0))],
        out_shape=[_sds((T, D), BF16), _sds((T, TOK), F32)],
        scratch_shapes=[pltpu.VMEM((HALO + TR, TOK), F32), pltpu.VMEM((7, HALO + TR, TOK), F32)],
        compiler_params=_cp(("arbitrary", "arbitrary")), name="conv_fwd",
    )(z, z, cw, cb, lg, lb)


def _conv_bwd(z, y, dcat, dz_into, cw, lg, lb):
    nt = SEQ // TR
    ext = TR + NEXT

    def body(zc_ref, zp_ref, yc_ref, yn_ref, dc_ref, dn_ref, w_ref, lg_ref, lb_ref, into_ref,
             du_ref, dw_ref, dcb_ref, dlg_ref, dlb_ref, dbin_ref, hbuf, dybuf, hs, dys):
        del into_ref
        b = pl.program_id(0)
        t = pl.program_id(1)

        @pl.when((b == 0) & (t == 0))
        def _():
            dw_ref[...] = jnp.zeros_like(dw_ref)
            dcb_ref[...] = jnp.zeros_like(dcb_ref)
            dlg_ref[...] = jnp.zeros_like(dlg_ref)
            dlb_ref[...] = jnp.zeros_like(dlb_ref)
            dbin_ref[...] = jnp.zeros_like(dbin_ref)

        hbuf[0:HALO, :] = jnp.where(t == 0, 0.0, _glu(zp_ref[...]))
        hbuf[HALO:, :] = _glu(zc_ref[...])
        _shifted_copies(hbuf, hs, HALO + TR - 8)
        last = t == nt - 1
        for r0 in range(0, ext, RT):
            yh, rstd = _layer_norm_parts(yc_ref[r0:r0 + RT, :] if r0 < TR else yn_ref[...])
            o = yh * lg_ref[...] + lb_ref[...]
            sg = jax.nn.sigmoid(o)
            if r0 < TR:
                dtok = dc_ref[r0:r0 + RT, :]
            else:
                dtok = jnp.where(last, 0.0, dn_ref[...])
            do = dtok * (sg * (1.0 + o * (1.0 - sg)))
            dyh = do * lg_ref[...]
            dy = rstd * (dyh - jnp.mean(dyh, axis=-1, keepdims=True)
                         - yh * jnp.mean(dyh * yh, axis=-1, keepdims=True))
            dybuf[r0:r0 + RT, :] = dy
            if r0 < TR:
                dlg_ref[...] += jnp.sum(do * yh, axis=0, keepdims=True)
                dlb_ref[...] += jnp.sum(do, axis=0, keepdims=True)
                dcb_ref[...] += jnp.sum(dy, axis=0, keepdims=True)
        _shifted_copies(dybuf, dys, ext - 8)
        for r0 in range(0, TR, RT):
            dh = jnp.zeros((RT, TOK), F32)
            for j in range(CONVW):
                dh = dh + w_ref[j:j + 1, :] * _tap(dybuf, dys, (CONVW - 1) - j, r0, RT)
            a = zc_ref[r0:r0 + RT, 0:TOK]
            sg = jax.nn.sigmoid(zc_ref[r0:r0 + RT, TOK:])
            da = dh * sg
            dg = dh * a * (sg * (1.0 - sg))
            du_ref[r0:r0 + RT, 0:TOK] = da.astype(BF16)
            du_ref[r0:r0 + RT, TOK:] = dg.astype(BF16)
            dbin_ref[:, 0:TOK] += jnp.sum(da, axis=0, keepdims=True)
            dbin_ref[:, TOK:] += jnp.sum(dg, axis=0, keepdims=True)
        for j in range(CONVW):
            acc = jnp.zeros((8, TOK), F32)
            for r0 in range(0, TR, RT):
                prod = dybuf[r0:r0 + RT, :] * _tap(hbuf, hs, (HALO - CONVW + 1) + j, r0, RT)
                acc = acc + jnp.sum(prod.reshape(RT // 8, 8, TOK), axis=0)
            dw_ref[j:j + 1, :] += jnp.sum(acc, axis=0, keepdims=True)

    vec = BS((1, TOK), lambda b, t: (0, 0))
    perh = TR // HALO
    pern = TR // NEXT
    nlast_n = T // NEXT - 1
    return pl.pallas_call(
        body, grid=(NB, nt),
        in_specs=[BS((TR, 2 * TOK), lambda b, t: (b * nt + t, 0)),
                  BS((HALO, 2 * TOK), lambda b, t: (jnp.maximum((b * nt + t) * perh - 1, 0), 0)),
                  BS((TR, TOK), lambda b, t: (b * nt + t, 0)),
                  BS((NEXT, TOK), lambda b, t: (jnp.minimum((b * nt + t + 1) * pern, nlast_n), 0)),
                  BS((TR, TOK), lambda b, t: (b * nt + t, 0)),
                  BS((NEXT, TOK), lambda b, t: (jnp.minimum((b * nt + t + 1) * pern, nlast_n), 0)),
                  BS((32, TOK), lambda b, t: (0, 0)), vec, vec, ANY],
        out_specs=[BS((TR, 2 * TOK), lambda b, t: (b * nt + t, 0)),
                   BS((32, TOK), lambda b, t: (0, 0)), vec, vec, vec,
                   BS((1, 2 * TOK), lambda b, t: (0, 0))],
        out_shape=[_sds((T, NBW), BF16), _sds((32, TOK), F32), _sds((1, TOK), F32),
                   _sds((1, TOK), F32), _sds((1, TOK), F32), _sds((1, 2 * TOK), F32)],
        input_output_aliases={9: 0},
        scratch_shapes=[pltpu.VMEM((HALO + TR, TOK), F32), pltpu.VMEM((ext, TOK), F32),
                        pltpu.VMEM((7, HALO + TR, TOK), F32), pltpu.VMEM((7, ext, TOK), F32)],
        compiler_params=_cp(("arbitrary", "arbitrary"), vmem_mb=56), name="conv_bwd",
    )(z, z, y, y, dcat, dcat, cw, lg, lb, dz_into)


def _ffn_up(h2, wgu, l, carry=None):
    ni = len(carry.ins) if carry else 0
    no = len(carry.out_shapes) if carry else 0

    def body(h_ref, wg_ref, wu_ref, *rest):
        cin = rest[:ni]
        g_ref, u_ref, a_ref = rest[ni:ni + 3]
        cout, cscr = rest[ni + 3:ni + 3 + no], rest[ni + 3 + no:]
        if carry:
            @pl.when((pl.program_id(0) == 0) & (pl.program_id(1) == 0))
            def _():
                carry.start(cin, cout, cscr)

        hv = h_ref[...]
        g = lax.dot_general(hv, wg_ref[...], _DIMS["nt"], preferred_element_type=F32)
        u = lax.dot_general(hv, wu_ref[...], _DIMS["nt"], preferred_element_type=F32)
        sg = jax.nn.sigmoid(g)
        silu = g * sg
        g_ref[...] = (u * (sg * (1.0 + g * (1.0 - sg)))).astype(BF16)
        u_ref[...] = silu.astype(BF16)
        a_ref[...] = (silu * u).astype(BF16)

        if carry:
            @pl.when((pl.program_id(0) == FF // FT - 1) & (pl.program_id(1) == T // TR - 1))
            def _():
                carry.finish(cin, cout, cscr)

    out = BS((TR, FT), lambda q, i: (i, q))
    outs = pl.pallas_call(
        body, grid=(FF // FT, T // TR),
        in_specs=[BS((TR, D), lambda q, i: (i, 0)),
                  BS((None, FT, D), lambda q, i: (0, q, 0)),
                  BS((None, FT, D), lambda q, i: (1, q, 0))] + (carry.in_specs if carry else []),
        out_specs=[out, out, out] + [ANY] * no,
        out_shape=[_sds((T, FF), BF16), _sds((T, FF), BF16), _sds((T, FF), BF16)]
        + (carry.out_shapes if carry else []),
        scratch_shapes=carry.scratch if carry else [],
        compiler_params=pltpu.CompilerParams(
            dimension_semantics=("arbitrary", "arbitrary"), vmem_limit_bytes=48 << 20,
            has_side_effects=bool(carry)), name=f"ffn_up_{l}",
    )(h2, wgu, wgu, *(carry.ins if carry else []))
    return outs[:3], outs[3:]


def _ffn_down_bwd(dx, wd, g, u, l):
    def epilogue(dact, ex, o_ref, i):
        o_ref[0] = (dact * ex[0][...].astype(F32)).astype(BF16)
        o_ref[1] = (dact * ex[1][...].astype(F32)).astype(BF16)

    ex_spec = BS((TR, FT), lambda i, q, k: (i, q))
    return _mm("nt", dx, wd, grid=(T // TR, FF // FT, 1),
               a_spec=BS((TR, D), lambda i, q, k: (i, 0)),
               b_spec=BS((FT, D), lambda i, q, k: (q, 0)),
               out_shape=_sds((2, T, FF), BF16),
               out_spec=BS((2, TR, FT), lambda i, q, k: (0, i, q)),
               acc_shape=(TR, FT), extras=(g, u), extra_specs=(ex_spec, ex_spec),
               epilogue=epilogue, name=f"ffn_down_bwd_{l}")


def _row_tile(rows, cols, itemsize=4, limit=2 << 20):
    tr = rows
    while tr * cols * itemsize > limit and tr % 2 == 0 and (tr // 2) % 16 == 0:
        tr //= 2
    return tr


def _cast_bf16(arrs, name):
    n = len(arrs)
    rows, cols = arrs[0].shape
    tr = _row_tile(rows, cols)

    def body(*refs):
        o_ref = refs[n]
        k = pl.program_id(0)
        val = refs[0][...]
        for j in range(1, n):
            val = jnp.where(k == j, refs[j][...], val)
        o_ref[...] = val.astype(BF16)

    return pl.pallas_call(
        body, grid=(n, rows // tr),
        in_specs=[BS((tr, cols), lambda k, i: (i, 0))] * n,
        out_specs=BS((None, tr, cols), lambda k, i: (k, i, 0)),
        out_shape=_sds((n, rows, cols), BF16),
        compiler_params=_cp(("arbitrary", "arbitrary")), name=name,
    )(*arrs)


def _quad_sum(own, got, name):
    n, rows, cols = own.shape
    tr = _row_tile(rows, cols)

    def body(a_ref, q_ref, o_ref):
        o_ref[...] = ((a_ref[...] + q_ref[0].astype(F32)) + q_ref[1].astype(F32)) + q_ref[2].astype(F32)

    spec = BS((None, tr, cols), lambda k, i: (k, i, 0))
    return pl.pallas_call(
        body, grid=(n, rows // tr),
        in_specs=[spec, BS((3, None, tr, cols), lambda k, i: (0, k, i, 0))], out_specs=spec,
        out_shape=_sds((n, rows, cols), F32),
        compiler_params=_cp(("arbitrary", "arbitrary")), name=name,
    )(own, got)


def _adam_math(w, g, m, v):
    m = ADAM_B1 * m + (1.0 - ADAM_B1) * g
    v = ADAM_B2 * v + (1.0 - ADAM_B2) * (g * g)
    m_hat = m / (1.0 - ADAM_B1 ** ADAM_STEP)
    v_hat = v / (1.0 - ADAM_B2 ** ADAM_STEP)
    delta = -ADAM_LR * (m_hat / (jnp.sqrt(v_hat) + ADAM_EPS) + ADAM_WD * w)
    return delta, m, v


def _adamw_big(w, g, m, v, name):
    shape = w.shape
    cols = shape[-1]
    rows = w.size // cols
    tr = _row_tile(rows, cols, limit=1 << 20)

    def body(w_ref, g_ref, m_ref, v_ref, d_ref, nm_ref, nv_ref):
        d, nm, nv = _adam_math(w_ref[...], g_ref[...], m_ref[...], v_ref[...])
        d_ref[...] = d
        nm_ref[...] = nm
        nv_ref[...] = nv

    spec = BS((tr, cols), lambda i: (i, 0))
    outs = pl.pallas_call(
        body, grid=(rows // tr,), in_specs=[spec] * 4, out_specs=[spec] * 3,
        out_shape=[_sds((rows, cols), F32)] * 3,
        compiler_params=_cp(("arbitrary",)), name=name,
    )(*[a.reshape(rows, cols) for a in (w, g, m, v)])
    return [o.reshape(shape) for o in outs]


def _adamw_small(ws, gs, ms, vs):
    n = len(ws)

    def body(*refs):
        for i in range(n):
            d, nm, nv = _adam_math(refs[i][...], refs[n + i][...], refs[2 * n + i][...],
                                   refs[3 * n + i][...])
            refs[4 * n + i][...] = d
            refs[5 * n + i][...] = nm
            refs[6 * n + i][...] = nv

    specs = [BS(w.shape, lambda i: (0, 0)) for w in ws]
    outs = pl.pallas_call(
        body, grid=(1,), in_specs=specs * 4, out_specs=specs * 3,
        out_shape=[_sds(w.shape, F32) for w in ws] * 3,
        compiler_params=_cp(("arbitrary",)), name="adamw_small",
    )(*ws, *gs, *ms, *vs)
    return outs[:n], outs[n:2 * n], outs[2 * n:]


def _place():
    x, y, c = lax.axis_index("x"), lax.axis_index("y"), lax.axis_index("c")
    chips = [(1 - x, y), (x, 1 - y), (1 - x, 1 - y)]
    return x, y, c, chips


class _Exchange:
    def __init__(self, ins, in_specs, out_shapes, scratch, start, finish):
        self.ins, self.in_specs, self.out_shapes, self.scratch = ins, in_specs, out_shapes, scratch
        self.start, self.finish = start, finish


def _run_exchange(ex, name, vmem_mb=40):
    ni, no = len(ex.ins), len(ex.out_shapes)

    def body(*refs):
        ex.start(refs[:ni], refs[ni:ni + no], refs[ni + no:])
        ex.finish(refs[:ni], refs[ni:ni + no], refs[ni + no:])

    return pl.pallas_call(
        body, in_specs=ex.in_specs, out_specs=[ANY] * no, out_shape=ex.out_shapes,
        scratch_shapes=ex.scratch,
        compiler_params=pltpu.CompilerParams(has_side_effects=True, vmem_limit_bytes=vmem_mb << 20),
        name=name,
    )(*ex.ins)


def _gather_exchange(srcs, dst_shapes, views, small=None):
    nu = len(srcs)
    nd = len(dst_shapes)
    ns = 1 if small is not None else 0

    def unpack(ins, outs, scr):
        x, y, c, chips = _place()
        src = ins[:nu]
        vw = [views[u](outs[:nd]) for u in range(nu)]
        vbuf = scr[:nu]
        send, recv, fsend, frecv, lsem, ssend, srecv, vsem = scr[nu:]

        def ici(u, j, shard, to):
            return pltpu.make_async_remote_copy(
                src_ref=vbuf[u].at[:, c], dst_ref=vw[u].at[:, shard, c],
                send_sem=send.at[3 * u + j], recv_sem=recv.at[3 * u + j],
                device_id=to, device_id_type=MESH)

        def fwd(u, j, shard, half):
            return pltpu.make_async_remote_copy(
                src_ref=vw[u].at[:, shard, half], dst_ref=vw[u].at[:, shard, half],
                send_sem=fsend.at[3 * u + j], recv_sem=frecv.at[3 * u + j],
                device_id=(x, y, 1 - c), device_id_type=MESH)

        def small_copy(j, shard, to):
            return pltpu.make_async_remote_copy(
                src_ref=ins[nu], dst_ref=outs[nd].at[shard],
                send_sem=ssend.at[j], recv_sem=srecv.at[j], device_id=to, device_id_type=MESH)

        stage = [pltpu.make_async_copy(src[u], vbuf[u], vsem.at[u]) for u in range(nu)]
        local = [pltpu.make_async_copy(vbuf[u], vw[u].at[:, 2 * x + y], lsem.at[u]) for u in range(nu)]
        if ns:
            local.append(pltpu.make_async_copy(ins[nu], outs[nd].at[2 * x + y], lsem.at[nu]))
        return x, y, c, chips, ici, fwd, small_copy, stage, local

    def start(ins, outs, scr):
        x, y, c, chips, ici, fwd, small_copy, stage, local = unpack(ins, outs, scr)
        s = 2 * x + y
        for cp in stage:
            cp.start()
        if ns:
            local[nu].start()
            for j, chip in enumerate(chips):
                small_copy(j, s, (*chip, c)).start()
        for u in range(nu):
            stage[u].wait()
            for j, chip in enumerate(chips):
                ici(u, j, s, (*chip, c)).start()
            local[u].start()

    def finish(ins, outs, scr):
        x, y, c, chips, ici, fwd, small_copy, stage, local = unpack(ins, outs, scr)
        s = 2 * x + y
        for u in range(nu):
            for j, chip in enumerate(chips):
                sj = 2 * chip[0] + chip[1]
                ici(u, j, sj, (x, y, c)).wait_recv()
                fwd(u, j, sj, c).start()
        for u in range(nu):
            for j, chip in enumerate(chips):
                sj = 2 * chip[0] + chip[1]
                fwd(u, j, sj, 1 - c).wait_recv()
        for u in range(nu):
            for j, chip in enumerate(chips):
                ici(u, j, s, (*chip, c)).wait_send()
                fwd(u, j, s, c).wait_send()
        if ns:
            for j, chip in enumerate(chips):
                small_copy(j, 2 * chip[0] + chip[1], (x, y, c)).wait_recv()
                small_copy(j, s, (*chip, c)).wait_send()
        for cp in local:
            cp.wait()

    dma = pltpu.SemaphoreType.DMA
    return _Exchange(
        ins=list(srcs) + ([small] if ns else []),
        in_specs=[ANY] * nu + [BS(memory_space=pltpu.VMEM)] * ns,
        out_shapes=[_sds(sh, BF16) for sh in dst_shapes]
        + ([_sds((NSH,) + small.shape, F32)] if ns else []),
        scratch=[pltpu.VMEM(a.shape, BF16) for a in srcs]
        + [dma((3 * nu,)), dma((3 * nu,)), dma((3 * nu,)), dma((3 * nu,)),
           dma((nu + 1,)), dma((3,)), dma((3,)), dma((nu,))],
        start=start, finish=finish)


def _pair_reduce(gs, name):
    nu = len(gs)
    ns = [g.shape[0] * NSH for g in gs]
    base = [sum(ns[:u]) for u in range(nu)]

    def body(*refs):
        g_refs, own_refs, sb_refs = refs[:nu], refs[nu:2 * nu], refs[2 * nu:3 * nu]
        bufs = refs[3 * nu:8 * nu]
        send, recv, lsem, osem = refs[8 * nu:]
        x, y, c, _ = _place()
        s = 2 * x + y

        def unit(u):
            sendb, recvb, stage, outf, outb = bufs[5 * u:5 * u + 5]

            def load(k, half):
                return pltpu.make_async_copy(g_refs[u].at[k // NSH, k % NSH, half], stage.at[k % 3],
                                             lsem.at[3 * u + k % 3])

            def push(k):
                return pltpu.make_async_remote_copy(
                    src_ref=sendb.at[k], dst_ref=recvb.at[k], send_sem=send.at[base[u] + k],
                    recv_sem=recv.at[base[u] + k], device_id=(x, y, 1 - c), device_id_type=MESH)

            def store(k):
                return pltpu.make_async_copy(outb.at[k % 2], sb_refs[u].at[k // NSH, k % NSH],
                                             osem.at[3 * u + k % 2])

            return sendb, recvb, stage, outf, outb, load, push, store

        for u in range(nu):
            sendb, recvb, stage, outf, outb, load, push, store = unit(u)
            for k in range(min(2, ns[u])):
                load(k, 1 - c).start()
            for k in range(ns[u]):
                if k + 2 < ns[u]:
                    load(k + 2, 1 - c).start()
                load(k, 1 - c).wait()
                sendb[k] = stage[k % 3].astype(BF16)
                push(k).start()
        for u in range(nu):
            sendb, recvb, stage, outf, outb, load, push, store = unit(u)
            n = ns[u]
            for k in range(min(2, n)):
                load(k, c).start()
            for k in range(n):
                if k + 2 < n:
                    load(k + 2, c).start()
                load(k, c).wait()
                push(k).wait_recv()
                total = stage[k % 3] + recvb[k].astype(F32)
                if k >= 2:
                    store(k - 2).wait()
                outb[k % 2] = total.astype(BF16)
                store(k).start()

                @pl.when(s == k % NSH)
                def _():
                    outf[...] = total
                    keep = pltpu.make_async_copy(outf, own_refs[u].at[k // NSH], osem.at[3 * u + 2])
                    keep.start()
                    keep.wait()

            for k in range(max(n - 2, 0), n):
                store(k).wait()
        for u in range(nu):
            push = unit(u)[6]
            for k in range(ns[u]):
                push(k).wait_send()

    dma = pltpu.SemaphoreType.DMA
    scratch = []
    for g, n in zip(gs, ns):
        rh, cc = g.shape[3], g.shape[4]
        scratch += [pltpu.VMEM((n, rh, cc), BF16), pltpu.VMEM((n, rh, cc), BF16),
                    pltpu.VMEM((3, rh, cc), F32), pltpu.VMEM((rh, cc), F32), pltpu.VMEM((2, rh, cc), BF16)]
    outs = pl.pallas_call(
        body, in_specs=[ANY] * nu, out_specs=[ANY] * (2 * nu),
        out_shape=[_sds((g.shape[0], g.shape[3], g.shape[4]), F32) for g in gs]
        + [_sds((g.shape[0], NSH, g.shape[3], g.shape[4]), BF16) for g in gs],
        scratch_shapes=scratch + [dma((sum(ns),)), dma((sum(ns),)), dma((3 * nu,)), dma((3 * nu,))],
        compiler_params=pltpu.CompilerParams(has_side_effects=True, vmem_limit_bytes=56 << 20),
        name=name,
    )(*gs)
    return list(outs[:nu]), list(outs[nu:])


def _chip_exchange(sums_bf16):
    nu = len(sums_bf16)

    def pushes(ins, outs, scr):
        x, y, c, chips = _place()
        send, recv = scr
        return [pltpu.make_async_remote_copy(
            src_ref=ins[u].at[:, 2 * chip[0] + chip[1]], dst_ref=outs[u].at[j],
            send_sem=send.at[3 * u + j], recv_sem=recv.at[3 * u + j],
            device_id=(*chip, c), device_id_type=MESH)
            for u in range(nu) for j, chip in enumerate(chips)]

    def start(ins, outs, scr):
        for cp in pushes(ins, outs, scr):
            cp.start()

    def finish(ins, outs, scr):
        for cp in pushes(ins, outs, scr):
            cp.wait()

    dma = pltpu.SemaphoreType.DMA
    shapes = [(3, a.shape[0], a.shape[2], a.shape[3]) for a in sums_bf16]
    return _Exchange(ins=list(sums_bf16), in_specs=[ANY] * nu,
                     out_shapes=[_sds(sh, BF16) for sh in shapes],
                     scratch=[dma((3 * nu,)), dma((3 * nu,))], start=start, finish=finish)


def _final_exchange(halves, out_shapes, targets):
    nu = len(halves)
    no = len(out_shapes)
    ncp = sum(len(t) for t in targets)

    def body(*refs):
        hv = refs[:nu]
        out = refs[nu:nu + no]
        sbuf = refs[nu + no:2 * nu + no]
        rbuf = refs[2 * nu + no:3 * nu + no]
        send, recv, lsem, osem, csem = refs[3 * nu + no:]
        x, y, c, _ = _place()
        stage = [pltpu.make_async_copy(hv[u], sbuf[u], lsem.at[u]) for u in range(nu)]
        push = [pltpu.make_async_remote_copy(
            src_ref=sbuf[u], dst_ref=rbuf[u], send_sem=send.at[u], recv_sem=recv.at[u],
            device_id=(x, y, 1 - c), device_id_type=MESH) for u in range(nu)]
        mine, theirs = [], []
        k = 0
        for u in range(nu):
            rh = hv[u].shape[1]
            for (p, oi, li) in targets[u]:
                mine.append((u, pltpu.make_async_copy(
                    sbuf[u].at[p], out[oi].at[li, pl.ds(c * rh, rh), :], csem.at[k])))
                theirs.append((u, pltpu.make_async_copy(
                    rbuf[u].at[p], out[oi].at[li, pl.ds((1 - c) * rh, rh), :], osem.at[k])))
                k += 1
        for cp in stage:
            cp.start()
        for u in range(nu):
            stage[u].wait()
            push[u].start()
            for (v, cp) in mine:
                if v == u:
                    cp.start()
        for u in range(nu):
            push[u].wait_recv()
            for (v, cp) in theirs:
                if v == u:
                    cp.start()
        for (_, cp) in theirs + mine:
            cp.wait()
        for u in range(nu):
            push[u].wait_send()

    dma = pltpu.SemaphoreType.DMA
    bufs = [pltpu.VMEM(h.shape, F32) for h in halves]
    return pl.pallas_call(
        body, in_specs=[ANY] * nu, out_specs=[ANY] * no,
        out_shape=[_sds(sh, F32) for sh in out_shapes],
        scratch_shapes=bufs + bufs + [dma((nu,)), dma((nu,)), dma((nu,)), dma((ncp,)), dma((ncp,))],
        compiler_params=pltpu.CompilerParams(has_side_effects=True, vmem_limit_bytes=56 << 20),
        name="final_exchange",
    )(*halves)


def _small_allreduce(pack):
    rows = pack.shape[0]

    def body(p_ref, o_ref, buf, send, recv):
        x, y, c, _ = _place()
        me = 4 * x + 2 * y + c
        buf[me] = p_ref[...]
        k = 0
        copies = []
        for dx in range(2):
            for dy in range(2):
                for dc in range(2):
                    if dx == 0 and dy == 0 and dc == 0:
                        continue
                    to = (jnp.where(dx, 1 - x, x), jnp.where(dy, 1 - y, y), jnp.where(dc, 1 - c, c))
                    src_slot = 4 * to[0] + 2 * to[1] + to[2]
                    copies.append((pltpu.make_async_remote_copy(
                        src_ref=p_ref, dst_ref=buf.at[me], send_sem=send.at[k], recv_sem=recv.at[k],
                        device_id=to, device_id_type=MESH), src_slot, k))
                    k += 1
        for cp, _, _ in copies:
            cp.start()
        for cp, src_slot, k in copies:
            pltpu.make_async_remote_copy(
                src_ref=p_ref, dst_ref=buf.at[src_slot], send_sem=send.at[k], recv_sem=recv.at[k],
                device_id=(x, y, c), device_id_type=MESH).wait()
        acc = buf[0]
        for d in range(1, 8):
            acc = acc + buf[d]
        o_ref[...] = acc

    dma = pltpu.SemaphoreType.DMA
    vm = BS(memory_space=pltpu.VMEM)
    return pl.pallas_call(
        body, in_specs=[vm], out_specs=vm, out_shape=_sds((rows, D), F32),
        scratch_shapes=[pltpu.VMEM((8, rows, D), F32), dma((7,)), dma((7,))],
        compiler_params=pltpu.CompilerParams(has_side_effects=True, vmem_limit_bytes=32 << 20),
        name="small_allreduce",
    )(pack)


def _in_proj(h, w, bias, name, transposed=False, carry=None):
    n = w.shape[0] if transposed else w.shape[1]
    tn = 1280 if n == NA else 1792
    ep = None
    extras, especs = (), ()
    if bias is not None:
        def ep(acc, ex, o_ref, i):
            o_ref[...] = acc + ex[0][...]
        extras = (bias,)
        especs = (BS((1, tn), lambda i, j, k: (0, j)),)
    b_spec = BS((tn, D), lambda i, j, k: (j, 0)) if transposed else BS((D, tn), lambda i, j, k: (0, j))
    return _mm("nt" if transposed else "nn", h, w, grid=(T // TR, n // tn, 1),
               a_spec=BS((TR, D), lambda i, j, k: (i, 0)), b_spec=b_spec,
               out_shape=_sds((T, n), F32), out_spec=BS((TR, tn), lambda i, j, k: (i, j)),
               acc_shape=(TR, tn), extras=extras, extra_specs=especs, epilogue=ep, name=name, carry=carry)


def _res_rms_epilogue(acc, ex, outs, i):
    y = acc + ex[0][...]
    outs[0][...] = y
    r = lax.rsqrt(jnp.mean(y * y, axis=-1, keepdims=True) + EPS)
    outs[1][...] = (y * r * ex[1][...]).astype(BF16)


def _res_loss_epilogue(acc, ex, outs, i):
    e = acc + ex[0][...] - ex[1][...]
    outs[1][...] = e * (1.0 / D)

    @pl.when(i == 0)
    def _():
        outs[0][...] = jnp.zeros_like(outs[0])

    outs[0][...] += 0.5 * jnp.sum(jnp.mean(e * e, axis=-1, keepdims=True), axis=0, keepdims=True)


def _rms_bwd_epilogue(dh, ex, outs, i):
    xv = ex[0][...]
    r = lax.rsqrt(jnp.mean(xv * xv, axis=-1, keepdims=True) + EPS)
    xh = xv * r
    gy = dh * ex[1][...]
    outs[0][...] = r * (gy - xh * jnp.mean(gy * xh, axis=-1, keepdims=True)) + ex[2][...]

    @pl.when(i == 0)
    def _():
        outs[1][...] = jnp.zeros_like(outs[1])

    outs[1][...] += jnp.sum(dh * xh, axis=0, keepdims=True)


def _local_step(x, mem, target, w, p, carries=None, bwd_carry_fn=None, late_carry_fn=None):
    row = lambda i, j, k: (i, 0)
    whole = lambda i, j, k: (0, 0)
    w = {k: (list(v) if isinstance(v, list) else v) for k, v in w.items()}
    carries = carries or {}

    def carry_of(name):
        return carries[name][0] if name in carries else None

    def delivered(name, outs):
        if name in carries:
            carries[name][1](w, outs)

    saved = []
    bias = _bias_expand(p["rel_u"])
    vec = BS((1, D), whole)
    h = _rms_fwd(x, p["norm1_g"][0:1], "rms1_0", carry=carry_of("rms1_0"))
    if carry_of("rms1_0") is not None:
        h, carried = h
        delivered("rms1_0", carried)
    for l in range(2):
        type_a = l == 0
        memn = _rms_fwd(mem, p["mem_norm_g"][l:l + 1], f"rmsmem_{l}")
        if type_a:
            z = _in_proj(h, w["a"], None, "inproj_a", carry=carry_of("inproj_a"))
            if carry_of("inproj_a") is not None:
                z, carried = z
                delivered("inproj_a", carried)
            cat, carried = _attn_fwd(z, bias, p["a_q_g2"], p["a_k_g2"], carry_of("attn_fwd"))
            delivered("attn_fwd", carried)
            qcol = NA // MEMW - 1
        else:
            z = _in_proj(h, w["b"], w["b_b_in"], "inproj_b", transposed=True)
            cat, conv_y = _conv_fwd(z, w["conv_w"], w["conv_b"], w["ln_g"], w["ln_b"])
            qcol = NBW // MEMW - 1
        kv = _mm("nn", memn, w["kv"][l], grid=(1, 1, 1),
                 a_spec=BS((NB * MEMT, D), whole), b_spec=BS((D, 2 * MEMW), whole),
                 out_shape=_sds((NB * MEMT, 2 * MEMW), F32), out_spec=BS((NB * MEMT, 2 * MEMW), whole),
                 acc_shape=(8, 128), name=f"memkv_{l}")
        cat = _memattn_fwd(z, kv, cat, p["mq_g4"][l:l + 1], p["mk_g4"][l:l + 1], qcol, f"memattn_fwd_{l}")
        x1, h2 = _mm("nn", cat, w["wo"][l], grid=(T // TR, 1, 1), a_spec=BS((TR, D), row),
                     b_spec=BS((D, D), whole),
                     out_shape=[_sds((T, D), F32), _sds((T, D), BF16)],
                     out_spec=[BS((TR, D), row), BS((TR, D), row)], acc_shape=(8, 128),
                     extras=(x, p["norm2_g"][l:l + 1]), extra_specs=(BS((TR, D), row), vec),
                     epilogue=_res_rms_epilogue, name=f"outproj_{l}")
        (g, u, act), carried = _ffn_up(h2, w["gu"][l], l, carry_of(f"ffn_up_{l}"))
        delivered(f"ffn_up_{l}", carried)
        last = l == 1
        res = _mm("nn", act, w["wd"][l], grid=(T // TR, 1, 1),
                  a_spec=BS((TR, FF), row), b_spec=BS((FF, D), whole),
                  out_shape=[_sds((8, 128), F32), _sds((T, D), F32)] if last else
                  [_sds((T, D), F32), _sds((T, D), BF16)],
                  out_spec=[BS((8, 128), whole), BS((TR, D), row)] if last else
                  [BS((TR, D), row), BS((TR, D), row)],
                  acc_shape=(8, 128), extras=(x1, target if last else p["norm1_g"][1:2]),
                  extra_specs=(BS((TR, D), row), BS((TR, D), row) if last else vec),
                  epilogue=_res_loss_epilogue if last else _res_rms_epilogue, sequential=last,
                  name=f"ffn_down_{l}", carry=carry_of(f"ffn_down_{l}"))
        if carry_of(f"ffn_down_{l}") is not None:
            res, carried = res
            delivered(f"ffn_down_{l}", carried)
        saved.append(dict(x=x, h=h, memn=memn, kv=kv, z=z, cat=cat, x1=x1, h2=h2, g=g, u=u, act=act,
                          qcol=qcol))
        if last:
            loss, dx = res
        else:
            x, h = res

    big = dict(a=None, b=None, kv=[None, None], wo=[None, None], gu=[None, None], wd=[None, None])
    small = {}
    bwd_carried, late_carried = (), ()
    tk = T // 2
    nkt = T // tk
    for l in (1, 0):
        sv = saved[l]
        dgu = _ffn_down_bwd(dx, w["wd"][l], sv["g"], sv["u"], l)
        big["wd"][l] = _mm("tn", sv["act"], dx, grid=(FF // FT, 1, 2 * nkt),
                           a_spec=BS((tk // 2, FT), lambda i, j, k: (k, i)),
                           b_spec=BS((tk // 2, D), lambda i, j, k: (k, 0)),
                           out_shape=_sds((FF, D), F32), out_spec=BS((FT, D), lambda i, j, k: (i, 0)),
                           acc_shape=(FT, D), name=f"dw_down_{l}")
        dx1, small[f"norm2_g{l}"] = _mm(
            "nn", dgu, w["gu"][l], grid=(T // TR, 1, 2),
            a_spec=BS((None, TR, FF), lambda i, j, k: (k, i, 0)),
            b_spec=BS((None, FF, D), lambda i, j, k: (k, 0, 0)),
            out_shape=[_sds((T, D), F32), _sds((1, D), F32)], out_spec=[BS((TR, D), row), vec],
            acc_shape=(TR, D), extras=(sv["x1"], p["norm2_g"][l:l + 1], dx),
            extra_specs=(BS((TR, D), row), vec, BS((TR, D), row)),
            epilogue=_rms_bwd_epilogue, sequential=True, name=f"dh2_{l}")
        big["gu"][l] = _mm("tn", dgu, sv["h2"], grid=(2 * FF // FT, 1, nkt),
                           a_spec=BS((None, tk, FT), lambda i, j, k: (i // 2, k, i % 2)),
                           b_spec=BS((tk, D), lambda i, j, k: (k, 0)),
                           out_shape=_sds((2, FF, D), F32),
                           out_spec=BS((None, FT, D), lambda i, j, k: (i // 2, i % 2, 0)),
                           acc_shape=(FT, D), vmem_mb=58, name=f"dw_gu_{l}")
        dcat = _mm("nt", dx1, w["wo"][l], grid=(T // TR, 1, 1), a_spec=BS((TR, D), row),
                   b_spec=BS((D, D), whole),
                   out_shape=_sds((T, D), F32), out_spec=BS((TR, D), row), acc_shape=(8, 128),
                   name=f"dcat_{l}")
        big["wo"][l] = _mm("tn", sv["cat"], dx1, grid=(1, 1, nkt),
                           a_spec=BS((tk, D), lambda i, j, k: (k, 0)), b_spec=BS((tk, D), lambda i, j, k: (k, 0)),
                           out_shape=_sds((D, D), F32), out_spec=BS((D, D), whole),
                           acc_shape=(D, D), name=f"dw_out_{l}")
        dqm, dkv, small[f"mq_g{l}"], small[f"mk_g{l}"], dqm_colsum = _memattn_bwd(
            sv["z"], sv["kv"], dcat, p["mq_g4"][l:l + 1], p["mk_g4"][l:l + 1], sv["qcol"], f"memattn_bwd_{l}",
            dz_cols=NA if l == 0 else NBW)
        big["kv"][l] = _mm("tn", sv["memn"], dkv, grid=(1, 1, 1),
                           a_spec=BS((NB * MEMT, D), whole), b_spec=BS((NB * MEMT, 2 * MEMW), whole),
                           out_shape=_sds((D, 2 * MEMW), F32), out_spec=BS((D, 2 * MEMW), whole),
                           acc_shape=(8, 128), name=f"dw_kv_{l}")
        dmemn = _mm("nt", dkv, w["kv"][l], grid=(1, 1, 1),
                    a_spec=BS((NB * MEMT, 2 * MEMW), whole), b_spec=BS((D, 2 * MEMW), whole),
                    out_shape=_sds((NB * MEMT, D), F32), out_spec=BS((NB * MEMT, D), whole),
                    acc_shape=(8, 128), name=f"dmemn_{l}")
        _, small[f"mem_norm_g{l}"] = _rms_bwd(dmemn, mem, p["mem_norm_g"][l:l + 1], None, f"rmsmem_bwd_{l}")
        if l == 0:
            carry = bwd_carry_fn(big) if bwd_carry_fn is not None else None
            (dz, dbias, small["a_q_g"], small["a_k_g"]), bwd_carried = _attn_bwd(
                sv["z"], dcat, dqm, bias, p["a_q_g2"], p["a_k_g2"], carry)
            small["rel_u"] = _bias_reduce(dbias)
            w_in, key, n, tn = w["a"], "a", NA, 1280
        else:
            dz, small["conv_w"], small["conv_b"], small["ln_g"], small["ln_b"], dbin_u = _conv_bwd(
                sv["z"], conv_y, dcat, dqm, w["conv_w"], w["ln_g"], w["ln_b"])
            small["b_in_u"] = dbin_u
            small["b_in_qm"] = dqm_colsum
            w_in, key, n, tn = w["b"], "b", NBW, 896
        norm_bwd = dict(out_shape=[_sds((T, D), F32), _sds((1, D), F32)], out_spec=[BS((TR, D), row), vec],
                        acc_shape=(8, 128), extras=(sv["x"], p["norm1_g"][l:l + 1], dx1),
                        extra_specs=(BS((TR, D), row), vec, BS((TR, D), row)),
                        epilogue=_rms_bwd_epilogue, sequential=True, name=f"dh_{l}")
        if l == 0:
            big[key] = _mm("tn", sv["h"], dz, grid=(1, n // tn, nkt),
                           a_spec=BS((tk, D), lambda i, j, k: (k, 0)), b_spec=BS((tk, tn), lambda i, j, k: (k, j)),
                           out_shape=_sds((D, n), F32), out_spec=BS((D, tn), lambda i, j, k: (0, j)),
                           acc_shape=(D, tn), name=f"dw_in_{l}")
            carry = late_carry_fn(big) if late_carry_fn is not None else None
            res = _mm("nt", dz, w_in, grid=(T // TR, 1, 1), a_spec=BS((TR, n), row),
                      b_spec=BS((D, n), whole), carry=carry, **norm_bwd)
            if carry is not None:
                res, late_carried = res
            dx, small[f"norm1_g{l}"] = res
        else:
            dx, small[f"norm1_g{l}"] = _mm("nn", dz, w_in, grid=(T // TR, 1, 1),
                                           a_spec=BS((TR, n), row), b_spec=BS((n, D), whole), **norm_bwd)
            big[key] = _mm("tn", dz, sv["h"], grid=(n // tn, 1, nkt),
                           a_spec=BS((tk, tn), lambda i, j, k: (k, i)), b_spec=BS((tk, D), lambda i, j, k: (k, 0)),
                           out_shape=_sds((n, D), F32), out_spec=BS((tn, D), lambda i, j, k: (i, 0)),
                           acc_shape=(tn, D), name=f"dw_in_{l}")
    return loss, dx, big, small, bwd_carried, late_carried


_PACK_ROWS = 64


def _pack_small(sm):
    plan = [("norm1_g0", 0, 1, 0, D), ("norm1_g1", 1, 1, 0, D), ("mem_norm_g0", 2, 1, 0, D),
            ("mem_norm_g1", 3, 1, 0, D), ("norm2_g0", 4, 1, 0, D), ("norm2_g1", 5, 1, 0, D),
            ("a_q_g", 6, 1, 0, AW), ("a_k_g", 7, 1, 0, AW), ("mq_g0", 8, 1, 0, MEMW),
            ("mq_g1", 9, 1, 0, MEMW), ("mk_g0", 10, 1, 0, MEMW), ("mk_g1", 11, 1, 0, MEMW),
            ("conv_b", 12, 1, 0, TOK), ("ln_g", 13, 1, 0, TOK), ("ln_b", 14, 1, 0, TOK),
            ("b_in_u", 15, 1, 0, D), ("b_in_u", 16, 1, D, 2 * TOK - D), ("b_in_qm", 17, 1, 0, MEMW),
            ("conv_w", 18, CONVW, 0, TOK), ("rel_u", 49, 12, 0, D), ("loss", 61, 1, 0, 128)]
    arrs = [sm[name].reshape(12, D) if name == "rel_u" else sm[name] for name, *_ in plan]

    def body(*refs):
        o_ref = refs[-1]
        o_ref[...] = jnp.zeros_like(o_ref)
        for ref, (_, r0, nr, c0, nc) in zip(refs, plan):
            o_ref[r0:r0 + nr, 0:nc] = ref[0:nr, c0:c0 + nc]

    return pl.pallas_call(
        body, grid=(1,), in_specs=[BS(a.shape, lambda i: (0, 0)) for a in arrs],
        out_specs=BS((_PACK_ROWS, D), lambda i: (0, 0)), out_shape=_sds((_PACK_ROWS, D), F32),
        compiler_params=_cp(("arbitrary",)), name="pack_small",
    )(*arrs)


def _rel_table_to_u(rel_bias):
    flat = jnp.concatenate([jnp.broadcast_to(rel_bias[:, 191:192], (12, 447)), rel_bias[:, ::-1]], axis=1)
    return jnp.pad(flat, ((0, 0), (192, 1024 - 192 - 639))).reshape(12, 1, 1024)


def _u_to_rel_table(du):
    flat = du[:, 192:192 + 639]
    g = flat[:, 447:][:, ::-1]
    return g, flat[:, :447]


def kernel(x, mem, norm1_g, mem_norm_g, a_w_in, a_q_g, a_k_g, a_rel_bias, b_w_in, b_b_in, b_conv_w, b_conv_b, b_ln_g, b_ln_b, mq_g, mk_g, w_mem_kv, w_out, norm2_g, w_gate, w_up, w_down, loss_target, m_norm1_g, m_mem_norm_g, m_a_w_in, m_a_q_g, m_a_k_g, m_a_rel_bias, m_b_w_in, m_b_b_in, m_b_conv_w, m_b_conv_b, m_b_ln_g, m_b_ln_b, m_mq_g, m_mk_g, m_w_mem_kv, m_w_out, m_norm2_g, m_w_gate, m_w_up, m_w_down, v_norm1_g, v_mem_norm_g, v_a_w_in, v_a_q_g, v_a_k_g, v_a_rel_bias, v_b_w_in, v_b_b_in, v_b_conv_w, v_b_conv_b, v_b_ln_g, v_b_ln_b, v_mq_g, v_mk_g, v_w_mem_kv, v_w_out, v_norm2_g, v_w_gate, v_w_up, v_w_down):
    sx = 2 * lax.axis_index("x") + lax.axis_index("y")

    n_in = (NA // NSH, NBW // NSH)
    tr = lambda a: jnp.swapaxes(a, -1, -2)
    small_src = jnp.concatenate([
        jnp.pad(b_b_in, ((0, 0), (0, 512 - 448))),
        jnp.pad(b_conv_w[0], ((0, 0), (0, 512 - 192))),
        jnp.pad(jnp.concatenate([b_conv_b, b_ln_g, b_ln_b], 0), ((0, 0), (0, 512 - 192))),
        jnp.zeros((5, 512), F32)], 0)

    def gather_groups(groups, small=None):
        def src_of(l, name):
            if name == "in":
                return [_cast_bf16([tr(b_w_in[0])], "cast_in_1").reshape(1, 2, n_in[1] // 2, D) if l else
                        _cast_bf16([a_w_in[0]], "cast_in_0").reshape(1, 2, D // 2, n_in[0])]
            if name == "gu":
                return [_cast_bf16([tr(w_gate[l])], f"cast_gate_{l}").reshape(1, 2, FS // 2, D),
                        _cast_bf16([tr(w_up[l])], f"cast_up_{l}").reshape(1, 2, FS // 2, D)]
            arr, shape = {"kv": (w_mem_kv, (1, 2, 128, 2 * MEMW)), "wo": (w_out, (1, 2, 128, D)),
                          "wd": (w_down, (1, 2, FS // 2, D))}[name]
            return [_cast_bf16([arr[l]], f"cast_{name}_{l}").reshape(shape)]

        def dst_of(l, name):
            if name == "in":
                return (1, NSH, 2, n_in[1] // 2, D) if l else (1, NSH, 2, D // 2, n_in[0])
            return {"kv": (1, NSH, 2, 128, 2 * MEMW), "wo": (1, NSH, 2, 128, D),
                    "gu": (1, 2, NSH, 2, FS // 2, D), "wd": (1, NSH, 2, FS // 2, D)}[name]

        items = [(l, name) for l, names in groups for name in names]
        srcs, views = [], []
        for k, (l, name) in enumerate(items):
            srcs += src_of(l, name)
            if name == "gu":
                views += [lambda d, k=k: d[k].at[:, 0], lambda d, k=k: d[k].at[:, 1]]
            else:
                views.append(lambda d, k=k: d[k])

        def done(w, outs):
            for k, (l, name) in enumerate(items):
                if name == "in" and l == 0:
                    w["a"] = outs[k].reshape(NSH, D, n_in[0]).transpose(1, 0, 2).reshape(D, NA)
                elif name == "in":
                    w["b"] = outs[k].reshape(NBW, D)
                else:
                    shape = {"kv": (D, 2 * MEMW), "wo": (D, D), "gu": (2, FF, D), "wd": (FF, D)}
                    w[name][l] = outs[k].reshape(shape[name])

        return _gather_exchange(srcs, [dst_of(l, name) for l, name in items], views, small), done

    w = dict(a=None, b=None, kv=[None, None], wo=[None, None], gu=[None, None], wd=[None, None])
    first, first_in_done = gather_groups([(0, ["in"])], small_src)

    def first_done(w, outs):
        first_in_done(w, outs)
        small_all = outs[1]
        conv_w_full = small_all[:, 1:1 + CONVW, :192].transpose(1, 0, 2).reshape(CONVW, TOK)
        vec3 = small_all[:, 32:35, :192].transpose(1, 0, 2).reshape(3, TOK)
        w.update(b_b_in=small_all[:, 0, :448].reshape(1, NBW), conv_w=jnp.pad(conv_w_full, ((0, 1), (0, 0))),
                 conv_b=vec3[0:1], ln_g=vec3[1:2], ln_b=vec3[2:3])

    carries = {"rms1_0": (first, first_done),
               "inproj_a": gather_groups([(0, ["kv", "wo", "wd"])]),
               "attn_fwd": gather_groups([(0, ["gu"]), (1, ["in", "kv", "wo"])]),
               "ffn_up_0": gather_groups([(1, ["gu"])]),
               "ffn_down_0": gather_groups([(1, ["wd"])])}
    p = dict(
        norm1_g=norm1_g, mem_norm_g=mem_norm_g, norm2_g=norm2_g,
        a_q_g2=jnp.tile(a_q_g, (1, AH)), a_k_g2=jnp.tile(a_k_g, (1, AH)),
        mq_g4=jnp.tile(mq_g, (1, 4)), mk_g4=jnp.tile(mk_g, (1, 4)),
        rel_u=_rel_table_to_u(a_rel_bias[0]))

    def pair_sums(items, big, name):
        units = []
        for l, tensor in items:
            if tensor == "in" and l == 0:
                g = big["a"].reshape(D, NSH, n_in[0]).transpose(1, 0, 2).reshape(1, NSH, 2, D // 2, n_in[0])
            elif tensor == "in":
                g = big["b"].reshape(1, NSH, 2, n_in[1] // 2, D)
            else:
                shape = {"kv": (1, NSH, 2, 128, 2 * MEMW), "wo": (1, NSH, 2, 128, D),
                         "gu": (2, NSH, 2, FS // 2, D), "wd": (1, NSH, 2, FS // 2, D)}
                g = big[tensor][l].reshape(shape[tensor])
            units.append(g)
        return _pair_reduce(units, name)

    early_groups = [[(1, "gu")], [(0, "gu")],
                    [(1, "in"), (1, "kv"), (1, "wo"), (1, "wd"), (0, "kv"), (0, "wo"), (0, "wd")]]
    early = [item for grp in early_groups for item in grp]
    late = [(0, "in")]
    own_early, own_late = [], []

    def bwd_carry_fn(big):
        sums_b = []
        for k, grp in enumerate(early_groups):
            own, sb = pair_sums(grp, big, f"pair_reduce_early_{k}")
            own_early.extend(own)
            sums_b.extend(sb)
        return _chip_exchange(sums_b)

    def late_carry_fn(big):
        own, sb = pair_sums(late, big, "pair_reduce_late")
        own_late.extend(own)
        return _chip_exchange(sb)

    loss, grad_x, big, small, parts_early, parts_late = _local_step(
        x.reshape(T, D), mem.reshape(NB * MEMT, D), loss_target.reshape(T, D), w, p,
        carries=carries, bwd_carry_fn=bwd_carry_fn, late_carry_fn=late_carry_fn)
    items = early + late
    halves = [_quad_sum(o, pt, f"quad_sum_{name}_{l}")
              for (l, name), o, pt in zip(items, own_early + own_late, list(parts_early) + list(parts_late))]
    out_shapes = [(1, D, NA // NSH), (1, NBW // NSH, D), (2, 2 * 128, 2 * MEMW), (2, 2 * 128, D),
                  (2, FS, D), (2, FS, D), (2, FS, D)]
    target_of = {"in": lambda l: [(0, l, 0)], "kv": lambda l: [(0, 2, l)], "wo": lambda l: [(0, 3, l)],
                 "gu": lambda l: [(0, 4, l), (1, 5, l)], "wd": lambda l: [(0, 6, l)]}
    targets = [target_of[name](l) for l, name in items]
    g_a, g_b, g_kv, g_wo, g_gate, g_up, g_wd = _final_exchange(halves, out_shapes, targets)

    tot = _small_allreduce(_pack_small(dict(small, loss=loss)))
    loss = tot[61, 0]
    g_rel, clip_part = _u_to_rel_table(tot[49:61])
    g_rel = jnp.concatenate([g_rel[:, :191], g_rel[:, 191:] + _rowsum(clip_part)], axis=1)
    b_in_full = jnp.concatenate([tot[15:16], tot[16:17, :512], tot[17:18, :MEMW]], axis=1)
    g_small = dict(
        norm1_g=tot[0:2], mem_norm_g=tot[2:4], norm2_g=tot[4:6],
        a_q_g=tot[6:7, :HD], a_k_g=tot[7:8, :HD], a_rel_bias=g_rel[None],
        b_b_in=lax.dynamic_slice(b_in_full, (0, sx * 448), (1, 448)),
        b_conv_w=lax.dynamic_slice(tot[18:49, :TOK], (0, sx * 192), (CONVW, 192))[None],
        b_conv_b=lax.dynamic_slice(tot[12:13, :TOK], (0, sx * 192), (1, 192)),
        b_ln_g=lax.dynamic_slice(tot[13:14, :TOK], (0, sx * 192), (1, 192)),
        b_ln_b=lax.dynamic_slice(tot[14:15, :TOK], (0, sx * 192), (1, 192)),
        mq_g=tot[8:10, :HD], mk_g=tot[10:12, :HD])

    names = ["norm1_g", "mem_norm_g", "a_w_in", "a_q_g", "a_k_g", "a_rel_bias", "b_w_in", "b_b_in",
             "b_conv_w", "b_conv_b", "b_ln_g", "b_ln_b", "mq_g", "mk_g", "w_mem_kv", "w_out",
             "norm2_g", "w_gate", "w_up", "w_down"]
    weights = dict(zip(names, [norm1_g, mem_norm_g, a_w_in, a_q_g, a_k_g, a_rel_bias, b_w_in, b_b_in,
                               b_conv_w, b_conv_b, b_ln_g, b_ln_b, mq_g, mk_g, w_mem_kv, w_out,
                               norm2_g, w_gate, w_up, w_down]))
    ms = dict(zip(names, [m_norm1_g, m_mem_norm_g, m_a_w_in, m_a_q_g, m_a_k_g, m_a_rel_bias, m_b_w_in,
                          m_b_b_in, m_b_conv_w, m_b_conv_b, m_b_ln_g, m_b_ln_b, m_mq_g, m_mk_g,
                          m_w_mem_kv, m_w_out, m_norm2_g, m_w_gate, m_w_up, m_w_down]))
    vs = dict(zip(names, [v_norm1_g, v_mem_norm_g, v_a_w_in, v_a_q_g, v_a_k_g, v_a_rel_bias, v_b_w_in,
                          v_b_b_in, v_b_conv_w, v_b_conv_b, v_b_ln_g, v_b_ln_b, v_mq_g, v_mk_g,
                          v_w_mem_kv, v_w_out, v_norm2_g, v_w_gate, v_w_up, v_w_down]))
    grads = dict(g_small)
    grads.update(a_w_in=g_a, b_w_in=g_b, w_mem_kv=g_kv, w_out=g_wo, w_gate=g_gate, w_up=g_up, w_down=g_wd)
    big_names = ["a_w_in", "b_w_in", "w_mem_kv", "w_out", "w_gate", "w_up", "w_down"]
    small_names = [n for n in names if n not in big_names]
    delta, new_m, new_v = {}, {}, {}
    for n in big_names:
        if n in ("b_w_in", "w_gate", "w_up"):
            outs = _adamw_big(tr(weights[n]), grads[n], tr(ms[n]), tr(vs[n]), f"adamw_{n}")
            delta[n], new_m[n], new_v[n] = [tr(o) for o in outs]
            grads[n] = tr(grads[n])
        else:
            delta[n], new_m[n], new_v[n] = _adamw_big(weights[n], grads[n], ms[n], vs[n], f"adamw_{n}")
    as2d = lambda a: a.reshape(-1, a.shape[-1])
    d_s, m_s, v_s = _adamw_small([as2d(weights[n]) for n in small_names], [as2d(grads[n]) for n in small_names],
                                 [as2d(ms[n]) for n in small_names], [as2d(vs[n]) for n in small_names])
    for i, n in enumerate(small_names):
        delta[n] = d_s[i].reshape(weights[n].shape)
        new_m[n] = m_s[i].reshape(weights[n].shape)
        new_v[n] = v_s[i].reshape(weights[n].shape)

    return (loss, grad_x.reshape(NB, SEQ, D), *[grads[n] for n in names], *[delta[n] for n in names],
            *[new_m[n] for n in names], *[new_v[n] for n in names])


def _rowsum(a):
    def body(a_ref, o_ref):
        o_ref[...] = jnp.sum(a_ref[...], axis=1, keepdims=True)

    vm = BS(memory_space=pltpu.VMEM)
    return pl.pallas_call(body, in_specs=[vm], out_specs=vm, out_shape=_sds((a.shape[0], 1), F32),
                          compiler_params=_cp(), name="rowsum")(a)
```

```python
import functools

import jax
import jax.numpy as jnp
from jax import lax
from jax.experimental import pallas as pl
from jax.experimental.pallas import tpu as pltpu

F32 = jnp.float32
BF16 = jnp.bfloat16
BS = pl.BlockSpec
ANY = pl.BlockSpec(memory_space=pl.ANY)
MESH = pl.DeviceIdType.MESH

D = 1024
SEQ = 2048
NB = 2
T = NB * SEQ
MEMT = 256
HD = 64
TOK = 768
MEMW = 256
NA = 3 * TOK + MEMW
NBW = 2 * TOK + MEMW
FF = 2816
NSH = 4
FS = FF // NSH
FT = FF // 2
CONVW = 31
EPS = 1e-6
NEG = -1e30
SCALE = HD ** -0.5
QB = 256
KWIN = 768
KPAD = 512
TR = 512

ADAM_LR = 0.001
ADAM_B1 = 0.9
ADAM_B2 = 0.999
ADAM_EPS = 1e-08
ADAM_WD = 0.01
ADAM_STEP = 10

_DIMS = {
    "nn": (((1,), (0,)), ((), ())),
    "nt": (((1,), (1,)), ((), ())),
    "tn": (((0,), (0,)), ((), ())),
}


def _cp(sem=None, vmem_mb=48):
    return pltpu.CompilerParams(dimension_semantics=sem, vmem_limit_bytes=vmem_mb << 20)


def _sds(shape, dtype):
    return jax.ShapeDtypeStruct(tuple(shape), dtype)


def _mm(mode, a, b, *, grid, a_spec, b_spec, out_shape, out_spec, acc_shape, name,
        extras=(), extra_specs=(), epilogue=None, carry=None, vmem_mb=48, sequential=False):
    n_ex = len(extras)
    nk = grid[2]
    dims = _DIMS[mode]
    ni = len(carry.ins) if carry else 0
    no = len(carry.out_shapes) if carry else 0
    multi = isinstance(out_shape, (list, tuple))
    out_shapes = list(out_shape) if multi else [out_shape]
    out_specs = list(out_spec) if multi else [out_spec]
    n_o = len(out_shapes)

    def body(a_ref, b_ref, *rest):
        ex = rest[:n_ex]
        cin = rest[n_ex:n_ex + ni]
        o_refs = rest[n_ex + ni:n_ex + ni + n_o]
        o_ref = o_refs if multi else o_refs[0]
        cout = rest[n_ex + ni + n_o:n_ex + ni + n_o + no]
        acc = rest[n_ex + ni + n_o + no]
        cscr = rest[n_ex + ni + n_o + no + 1:]
        ids = [pl.program_id(d) for d in range(3)]
        k = ids[2]
        if carry:
            @pl.when((ids[0] == 0) & (ids[1] == 0) & (ids[2] == 0))
            def _():
                carry.start(cin, cout, cscr)

        prod = lax.dot_general(a_ref[...].astype(BF16), b_ref[...].astype(BF16), dims,
                               preferred_element_type=F32)

        def finish(val):
            if epilogue is None:
                o_ref[...] = val.astype(o_ref.dtype)
            else:
                epilogue(val, ex, o_ref, ids[0])

        if nk == 1:
            finish(prod)
        else:
            @pl.when(k == 0)
            def _():
                acc[...] = prod

            @pl.when((k > 0) & (k < nk - 1))
            def _():
                acc[...] += prod

            @pl.when(k == nk - 1)
            def _():
                finish(acc[...] + prod)

        if carry:
            @pl.when((ids[0] == grid[0] - 1) & (ids[1] == grid[1] - 1) & (ids[2] == grid[2] - 1))
            def _():
                carry.finish(cin, cout, cscr)

    acc_scratch = pltpu.VMEM(acc_shape if nk > 1 else (8, 128), F32)
    ordered = sequential or bool(carry)
    outs = pl.pallas_call(
        body, grid=grid,
        in_specs=[a_spec, b_spec, *extra_specs] + (carry.in_specs if carry else []),
        out_specs=out_specs + [ANY] * no, out_shape=out_shapes + (carry.out_shapes if carry else []),
        scratch_shapes=[acc_scratch] + (carry.scratch if carry else []),
        compiler_params=pltpu.CompilerParams(
            dimension_semantics=("arbitrary",) * 3 if ordered else ("parallel", "parallel", "arbitrary"),
            vmem_limit_bytes=vmem_mb << 20), name=name,
    )(a, b, *extras, *(carry.ins if carry else []))
    mine = list(outs[:n_o]) if multi else outs[0]
    return (mine, outs[n_o:]) if carry else mine


def _rms_fwd(x, g, name, carry=None):
    rows = x.shape[0]
    ni = len(carry.ins) if carry else 0
    no = len(carry.out_shapes) if carry else 0

    def body(x_ref, g_ref, *rest):
        cin, o_ref, cout, cscr = rest[:ni], rest[ni], rest[ni + 1:ni + 1 + no], rest[ni + 1 + no:]
        if carry:
            @pl.when(pl.program_id(0) == 0)
            def _():
                carry.start(cin, cout, cscr)

        xv = x_ref[...]
        r = lax.rsqrt(jnp.mean(xv * xv, axis=-1, keepdims=True) + EPS)
        o_ref[...] = (xv * r * g_ref[...]).astype(BF16)

        if carry:
            @pl.when(pl.program_id(0) == rows // TR - 1)
            def _():
                carry.finish(cin, cout, cscr)

    outs = pl.pallas_call(
        body, grid=(rows // TR,),
        in_specs=[BS((TR, D), lambda i: (i, 0)), BS((1, D), lambda i: (0, 0))]
        + (carry.in_specs if carry else []),
        out_specs=[BS((TR, D), lambda i: (i, 0))] + [ANY] * no,
        out_shape=[_sds((rows, D), BF16)] + (carry.out_shapes if carry else []),
        scratch_shapes=carry.scratch if carry else [],
        compiler_params=pltpu.CompilerParams(
            dimension_semantics=("arbitrary",), vmem_limit_bytes=48 << 20,
            has_side_effects=bool(carry)), name=name,
    )(x, g, *(carry.ins if carry else []))
    return (outs[0], outs[1:]) if carry else outs[0]


def _rms_bwd(dh, x, g, dres, name):
    rows = x.shape[0]
    has_res = dres is not None

    def body(*refs):
        if has_res:
            dh_ref, x_ref, g_ref, r_ref, dx_ref, dg_ref = refs
        else:
            dh_ref, x_ref, g_ref, dx_ref, dg_ref = refs
        xv = x_ref[...]
        dhv = dh_ref[...]
        r = lax.rsqrt(jnp.mean(xv * xv, axis=-1, keepdims=True) + EPS)
        xh = xv * r
        gy = dhv * g_ref[...]
        dx = r * (gy - xh * jnp.mean(gy * xh, axis=-1, keepdims=True))
        if has_res:
            dx = dx + r_ref[...]
        dx_ref[...] = dx

        @pl.when(pl.program_id(0) == 0)
        def _():
            dg_ref[...] = jnp.zeros_like(dg_ref)

        dg_ref[...] += jnp.sum(dhv * xh, axis=0, keepdims=True)

    row = BS((TR, D), lambda i: (i, 0))
    vec = BS((1, D), lambda i: (0, 0))
    ins = [dh, x, g] + ([dres] if has_res else [])
    return pl.pallas_call(
        body, grid=(rows // TR,),
        in_specs=[row, row, vec] + ([row] if has_res else []),
        out_specs=[row, vec], out_shape=[_sds((rows, D), F32), _sds((1, D), F32)],
        compiler_params=_cp(("arbitrary",)), name=name,
    )(*ins)


def _group_masks(width):
    lane = lax.broadcasted_iota(jnp.int32, (1, width), 1)
    return [(lane >= HD * h) & (lane < HD * (h + 1)) for h in range(width // HD)]


def _group_mean(v, masks):
    del masks
    low = lax.broadcasted_iota(jnp.int32, (1, 128), 1) < HD
    slabs = []
    for j in range(v.shape[-1] // 128):
        x = v[:, 128 * j:128 * (j + 1)]
        s0 = jnp.sum(jnp.where(low, x, 0.0), axis=-1, keepdims=True) * (1.0 / HD)
        s1 = jnp.sum(jnp.where(low, 0.0, x), axis=-1, keepdims=True) * (1.0 / HD)
        slabs.append(jnp.where(low, s0, s1))
    return slabs[0] if len(slabs) == 1 else jnp.concatenate(slabs, axis=-1)


def _head_norm(zv, g, masks):
    r = lax.rsqrt(_group_mean(zv * zv, masks) + EPS)
    return zv * r * g


def _head_norm_bwd(dy, zv, g, masks):
    r = lax.rsqrt(_group_mean(zv * zv, masks) + EPS)
    zh = zv * r
    gy = dy * g
    dz = r * (gy - zh * _group_mean(gy * zh, masks))
    return dz, jnp.sum(dy * zh, axis=0, keepdims=True)


def _fold_heads(v, width):
    vb = jnp.broadcast_to(v, (8, width))
    out = vb
    for h in range(1, width // HD):
        out = out + pltpu.roll(vb, width - HD * h, axis=1)
    return out[0:1]


def _bias_expand(u):
    def body(u_ref, o_ref):
        x = jnp.broadcast_to(u_ref[...], (QB, 1024))
        rolled = pltpu.roll(x, 1024 - (QB - 1), axis=1, stride=1, stride_axis=0)[:, :KWIN]
        row = lax.broadcasted_iota(jnp.int32, (QB, 1), 0)
        col = lax.broadcasted_iota(jnp.int32, (1, KWIN), 1)
        lo = (row // 64) * 64
        ok = (col >= lo) & (col < lo + 576)
        o_ref[...] = jnp.where(ok, rolled, NEG)

    return pl.pallas_call(
        body, grid=(12,), in_specs=[BS((None, 1, 1024), lambda h: (h, 0, 0))],
        out_specs=BS((None, QB, KWIN), lambda h: (h, 0, 0)), out_shape=_sds((12, QB, KWIN), F32),
        compiler_params=_cp(("arbitrary",)), name="bias_expand",
    )(u)


def _bias_reduce(ds):
    def body(d_ref, o_ref):
        ri = lax.broadcasted_iota(jnp.int32, (QB, QB), 0)
        ci = lax.broadcasted_iota(jnp.int32, (QB, QB), 1)
        flip = (ri + ci == QB - 1).astype(F32)
        drev = jnp.dot(flip, d_ref[...], precision=lax.Precision.HIGHEST, preferred_element_type=F32)
        x = jnp.concatenate([drev, jnp.zeros((QB, 1024 - KWIN), F32)], axis=1)
        rolled = pltpu.roll(x, 0, axis=1, stride=1, stride_axis=0)
        o_ref[...] = jnp.sum(rolled, axis=0, keepdims=True)

    return pl.pallas_call(
        body, grid=(12,), in_specs=[BS((None, QB, KWIN), lambda h: (h, 0, 0))],
        out_specs=BS((None, 1, 1024), lambda h: (h, 0, 0)), out_shape=_sds((12, 1, 1024), F32),
        compiler_params=_cp(("arbitrary",)), name="bias_reduce",
    )(ds)


AW = 256
AH = AW // HD
AG = TOK // AW
def _attn_softmax(qh, kw, bias, startadd):
    s = lax.dot_general(qh, kw, _DIMS["nt"], preferred_element_type=F32) + bias + startadd
    m = jnp.max(s, axis=-1, keepdims=True)
    p = jnp.exp(s - m)
    return p * (1.0 / jnp.sum(p, axis=-1, keepdims=True))


def _attn_prologue(q_ref, k_ref, v_ref, gq_ref, gk_ref, qn_s, kn_s, v_s, masks):
    kn_s[0:KPAD, :] = jnp.zeros((KPAD, AW), BF16)
    v_s[0:KPAD, :] = jnp.zeros((KPAD, AW), BF16)
    for r in range(0, SEQ, TR):
        qn_s[r:r + TR, :] = (_head_norm(q_ref[r:r + TR, :], gq_ref[...], masks) * SCALE).astype(BF16)
        kn_s[KPAD + r:KPAD + r + TR, :] = _head_norm(k_ref[r:r + TR, :], gk_ref[...], masks).astype(BF16)
        v_s[KPAD + r:KPAD + r + TR, :] = v_ref[r:r + TR, :].astype(BF16)


def _attn_fwd(z, bias, gq2, gk2, carry=None):
    ni = len(carry.ins) if carry else 0
    no = len(carry.out_shapes) if carry else 0

    def body(q_ref, k_ref, v_ref, b_ref, gq_ref, gk_ref, *rest):
        cin, o_ref, cout = rest[:ni], rest[ni], rest[ni + 1:ni + 1 + no]
        qn_s, kn_s, v_s = rest[ni + 1 + no:ni + 4 + no]
        cscr = rest[ni + 4 + no:]
        if carry:
            @pl.when((pl.program_id(0) == 0) & (pl.program_id(1) == 0))
            def _():
                carry.start(cin, cout, cscr)

        masks = _group_masks(AW)
        _attn_prologue(q_ref, k_ref, v_ref, gq_ref, gk_ref, qn_s, kn_s, v_s, masks)
        col = lax.broadcasted_iota(jnp.int32, (1, KWIN), 1)

        def blk(i, carry):
            r0 = pl.multiple_of(i * QB, QB)
            qb = qn_s[pl.ds(r0, QB), :]
            kw = kn_s[pl.ds(r0, KWIN), :]
            vw = v_s[pl.ds(r0, KWIN), :]
            startadd = jnp.where(col + r0 < KPAD, NEG, 0.0)
            o = jnp.zeros((QB, AW), F32)
            for h in range(AH):
                qh = jnp.where(masks[h], qb, jnp.zeros_like(qb))
                vh = jnp.where(masks[h], vw, jnp.zeros_like(vw))
                p = _attn_softmax(qh, kw, b_ref[h], startadd).astype(BF16)
                o = o + jnp.dot(p, vh, preferred_element_type=F32)
            o_ref[pl.ds(r0, QB), :] = o.astype(BF16)
            return carry

        lax.fori_loop(0, SEQ // QB, blk, 0)

        if carry:
            @pl.when((pl.program_id(0) == NB - 1) & (pl.program_id(1) == AG - 1))
            def _():
                carry.finish(cin, cout, cscr)

    vec = BS((1, AW), lambda b, hp: (0, 0))
    outs = pl.pallas_call(
        body, grid=(NB, AG),
        in_specs=[BS((SEQ, AW), lambda b, hp: (b, hp)),
                  BS((SEQ, AW), lambda b, hp: (b, AG + hp)),
                  BS((SEQ, AW), lambda b, hp: (b, 2 * AG + hp)),
                  BS((AH, QB, KWIN), lambda b, hp: (hp, 0, 0)), vec, vec]
        + (carry.in_specs if carry else []),
        out_specs=[BS((SEQ, AW), lambda b, hp: (b, hp))] + [ANY] * no,
        out_shape=[_sds((T, D), BF16)] + (carry.out_shapes if carry else []),
        scratch_shapes=[pltpu.VMEM((SEQ, AW), BF16), pltpu.VMEM((SEQ + KPAD, AW), BF16),
                        pltpu.VMEM((SEQ + KPAD, AW), BF16)] + (carry.scratch if carry else []),
        compiler_params=pltpu.CompilerParams(
            dimension_semantics=("arbitrary", "arbitrary"), vmem_limit_bytes=48 << 20,
            has_side_effects=bool(carry)), name="attn_fwd",
    )(z, z, z, bias, gq2, gk2, *(carry.ins if carry else []))
    return outs[0], outs[1:]


def _attn_bwd(z, dcat, dz_into, bias, gq2, gk2, carry=None):
    ni = len(carry.ins) if carry else 0
    no = len(carry.out_shapes) if carry else 0

    def body(q_ref, k_ref, v_ref, do_ref, b_ref, gq_ref, gk_ref, into_ref, *rest):
        del into_ref
        cin = rest[:ni]
        dz_ref, db_ref, dgq_ref, dgk_ref = rest[ni:ni + 4]
        cout = rest[ni + 4:ni + 4 + no]
        qn_s, kn_s, v_s, dqn_s, dkn_s, dv_s, out_s, out_sem = rest[ni + 4 + no:ni + 12 + no]
        cscr = rest[ni + 12 + no:]
        hp = pl.program_id(0)
        b = pl.program_id(1)
        step = hp * NB + b

        def out_copy(j, step):
            rows = pl.ds(pl.multiple_of((step % NB) * SEQ, SEQ), SEQ)
            cols = pl.ds(pl.multiple_of((j * AG + step // NB) * AW, AW), AW)
            return pltpu.make_async_copy(out_s.at[j], dz_ref.at[rows, cols], out_sem.at[j])

        if carry:
            @pl.when((hp == 0) & (b == 0))
            def _():
                carry.start(cin, cout, cscr)

        masks = _group_masks(AW)
        _attn_prologue(q_ref, k_ref, v_ref, gq_ref, gk_ref, qn_s, kn_s, v_s, masks)
        dkn_s[...] = jnp.zeros_like(dkn_s)
        dv_s[...] = jnp.zeros_like(dv_s)

        @pl.when(b == 0)
        def _():
            db_ref[...] = jnp.zeros_like(db_ref)

        @pl.when((b == 0) & (hp == 0))
        def _():
            dgq_ref[...] = jnp.zeros_like(dgq_ref)
            dgk_ref[...] = jnp.zeros_like(dgk_ref)

        col = lax.broadcasted_iota(jnp.int32, (1, KWIN), 1)

        def blk(i, carry):
            r0 = pl.multiple_of(i * QB, QB)
            qb = qn_s[pl.ds(r0, QB), :]
            kw = kn_s[pl.ds(r0, KWIN), :]
            vw = v_s[pl.ds(r0, KWIN), :]
            dob = do_ref[pl.ds(r0, QB), :].astype(BF16)
            startadd = jnp.where(col + r0 < KPAD, NEG, 0.0)
            dqn = jnp.zeros((QB, AW), F32)
            dkw = jnp.zeros((KWIN, AW), F32)
            dvw = jnp.zeros((KWIN, AW), F32)
            for h in range(AH):
                qh = jnp.where(masks[h], qb, jnp.zeros_like(qb))
                kh = jnp.where(masks[h], kw, jnp.zeros_like(kw))
                doh = jnp.where(masks[h], dob, jnp.zeros_like(dob))
                p = _attn_softmax(qh, kw, b_ref[h], startadd)
                dvw = dvw + lax.dot_general(p.astype(BF16), doh, _DIMS["tn"],
                                            preferred_element_type=F32)
                dp = lax.dot_general(doh, vw, _DIMS["nt"], preferred_element_type=F32)
                ds = p * (dp - jnp.sum(dp * p, axis=-1, keepdims=True))
                db_ref[h] += ds
                dsb = ds.astype(BF16)
                dqn = dqn + jnp.dot(dsb, kh, preferred_element_type=F32)
                dkw = dkw + lax.dot_general(dsb, qh, _DIMS["tn"], preferred_element_type=F32)
            dqn_s[pl.ds(r0, QB), :] = dqn * SCALE
            dkn_s[pl.ds(r0, KWIN), :] += dkw
            dv_s[pl.ds(r0, KWIN), :] += dvw
            return carry

        lax.fori_loop(0, SEQ // QB, blk, 0)

        @pl.when(step > 0)
        def _():
            for j in range(3):
                out_copy(j, step - 1).wait()

        dgq = jnp.zeros((1, AW), F32)
        dgk = jnp.zeros((1, AW), F32)
        for r in range(0, SEQ, TR):
            dq, dg = _head_norm_bwd(dqn_s[r:r + TR, :], q_ref[r:r + TR, :], gq_ref[...], masks)
            out_s[0, r:r + TR, :] = dq.astype(BF16)
            dgq = dgq + dg
            dk, dg = _head_norm_bwd(dkn_s[KPAD + r:KPAD + r + TR, :], k_ref[r:r + TR, :], gk_ref[...], masks)
            out_s[1, r:r + TR, :] = dk.astype(BF16)
            dgk = dgk + dg
            out_s[2, r:r + TR, :] = dv_s[KPAD + r:KPAD + r + TR, :].astype(BF16)
        dgq_ref[...] += _fold_heads(dgq, AW)
        dgk_ref[...] += _fold_heads(dgk, AW)
        for j in range(3):
            out_copy(j, step).start()

        @pl.when(step == AG * NB - 1)
        def _():
            for j in range(3):
                out_copy(j, step).wait()

        if carry:
            @pl.when((hp == AG - 1) & (b == NB - 1))
            def _():
                carry.finish(cin, cout, cscr)

    vec = BS((1, AW), lambda hp, b: (0, 0))
    row = BS((SEQ, AW), lambda hp, b: (b, hp))
    outs = pl.pallas_call(
        body, grid=(AG, NB),
        in_specs=[row,
                  BS((SEQ, AW), lambda hp, b: (b, AG + hp)),
                  BS((SEQ, AW), lambda hp, b: (b, 2 * AG + hp)),
                  row,
                  BS((AH, QB, KWIN), lambda hp, b: (hp, 0, 0)), vec, vec, ANY]
        + (carry.in_specs if carry else []),
        out_specs=[ANY, BS((AH, QB, KWIN), lambda hp, b: (hp, 0, 0)), vec, vec] + [ANY] * no,
        out_shape=[_sds((T, NA), BF16), _sds((12, QB, KWIN), F32), _sds((1, AW), F32), _sds((1, AW), F32)]
        + (carry.out_shapes if carry else []),
        input_output_aliases={7: 0},
        scratch_shapes=[pltpu.VMEM((SEQ, AW), BF16), pltpu.VMEM((SEQ + KPAD, AW), BF16),
                        pltpu.VMEM((SEQ + KPAD, AW), BF16), pltpu.VMEM((SEQ, AW), F32),
                        pltpu.VMEM((SEQ + KPAD, AW), F32), pltpu.VMEM((SEQ + KPAD, AW), F32),
                        pltpu.VMEM((3, SEQ, AW), BF16), pltpu.SemaphoreType.DMA((3,))]
        + (carry.scratch if carry else []),
        compiler_params=pltpu.CompilerParams(
            dimension_semantics=("arbitrary", "arbitrary"), vmem_limit_bytes=58 << 20,
            has_side_effects=bool(carry)), name="attn_bwd",
    )(z, z, z, dcat, bias, gq2, gk2, dz_into, *(carry.ins if carry else []))
    return outs[:4], outs[4:]


def _mem_softmax(qh, kn):
    s = lax.dot_general(qh, kn, _DIMS["nt"], preferred_element_type=F32)
    m = jnp.max(s, axis=-1, keepdims=True)
    p = jnp.exp(s - m)
    return p * (1.0 / jnp.sum(p, axis=-1, keepdims=True))


def _memattn_fwd(z, kv, cat, gq4, gk4, qcol, name):
    def body(q_ref, k_ref, v_ref, gq_ref, gk_ref, cat_ref, o_ref):
        del cat_ref
        masks = _group_masks(MEMW)
        qn = (_head_norm(q_ref[...], gq_ref[...], masks) * SCALE).astype(BF16)
        kn = _head_norm(k_ref[...], gk_ref[...], masks).astype(BF16)
        vv = v_ref[...].astype(BF16)
        o = jnp.zeros((TR, MEMW), F32)
        for h in range(4):
            qh = jnp.where(masks[h], qn, jnp.zeros_like(qn))
            vh = jnp.where(masks[h], vv, jnp.zeros_like(vv))
            p = _mem_softmax(qh, kn).astype(BF16)
            o = o + jnp.dot(p, vh, preferred_element_type=F32)
        o_ref[...] = o.astype(BF16)

    nt = SEQ // TR
    vec = BS((1, MEMW), lambda b, t: (0, 0))
    return pl.pallas_call(
        body, grid=(NB, nt),
        in_specs=[BS((TR, MEMW), lambda b, t: (b * nt + t, qcol)),
                  BS((MEMT, MEMW), lambda b, t: (b, 0)),
                  BS((MEMT, MEMW), lambda b, t: (b, 1)), vec, vec, ANY],
        out_specs=BS((TR, MEMW), lambda b, t: (b * nt + t, 3)),
        out_shape=_sds((T, D), BF16), input_output_aliases={5: 0},
        compiler_params=_cp(("arbitrary", "arbitrary")), name=name,
    )(z, kv, kv, gq4, gk4, cat)


def _memattn_bwd(z, kv, dcat, gq4, gk4, qcol, name, dz_cols=MEMW):
    nt = SEQ // TR

    def body(q_ref, k_ref, v_ref, do_ref, gq_ref, gk_ref,
             dq_ref, dkv_ref, dgq_ref, dgk_ref, dqsum_ref, dkn_s, dv_s):
        b = pl.program_id(0)
        t = pl.program_id(1)
        masks = _group_masks(MEMW)
        qz = q_ref[...]
        kz = k_ref[...]
        qn = (_head_norm(qz, gq_ref[...], masks) * SCALE).astype(BF16)
        kn = _head_norm(kz, gk_ref[...], masks).astype(BF16)
        vv = v_ref[...].astype(BF16)
        dob = do_ref[...].astype(BF16)

        @pl.when(t == 0)
        def _():
            dkn_s[...] = jnp.zeros_like(dkn_s)
            dv_s[...] = jnp.zeros_like(dv_s)

        @pl.when((t == 0) & (b == 0))
        def _():
            dgq_ref[...] = jnp.zeros_like(dgq_ref)
            dgk_ref[...] = jnp.zeros_like(dgk_ref)
            dqsum_ref[...] = jnp.zeros_like(dqsum_ref)

        dqn = jnp.zeros((TR, MEMW), F32)
        dkn = jnp.zeros((MEMT, MEMW), F32)
        dvv = jnp.zeros((MEMT, MEMW), F32)
        for h in range(4):
            qh = jnp.where(masks[h], qn, jnp.zeros_like(qn))
            kh = jnp.where(masks[h], kn, jnp.zeros_like(kn))
            doh = jnp.where(masks[h], dob, jnp.zeros_like(dob))
            p = _mem_softmax(qh, kn)
            dvv = dvv + lax.dot_general(p.astype(BF16), doh, _DIMS["tn"], preferred_element_type=F32)
            dp = lax.dot_general(doh, vv, _DIMS["nt"], preferred_element_type=F32)
            ds = p * (dp - jnp.sum(dp * p, axis=-1, keepdims=True))
            dsb = ds.astype(BF16)
            dqn = dqn + jnp.dot(dsb, kh, preferred_element_type=F32)
            dkn = dkn + lax.dot_general(dsb, qh, _DIMS["tn"], preferred_element_type=F32)
        dkn_s[...] += dkn
        dv_s[...] += dvv
        dq, dgq = _head_norm_bwd(dqn * SCALE, qz, gq_ref[...], masks)
        dq_ref[...] = dq.astype(BF16)
        dgq_ref[...] += _fold_heads(dgq, MEMW)
        dqsum_ref[...] += jnp.sum(dq, axis=0, keepdims=True)

        @pl.when(t == nt - 1)
        def _():
            dk, dgk = _head_norm_bwd(dkn_s[...], kz, gk_ref[...], masks)
            dkv_ref[:, 0:MEMW] = dk
            dkv_ref[:, MEMW:] = dv_s[...]
            dgk_ref[...] += _fold_heads(dgk, MEMW)

    vec = BS((1, MEMW), lambda b, t: (0, 0))
    return pl.pallas_call(
        body, grid=(NB, nt),
        in_specs=[BS((TR, MEMW), lambda b, t: (b * nt + t, qcol)),
                  BS((MEMT, MEMW), lambda b, t: (b, 0)),
                  BS((MEMT, MEMW), lambda b, t: (b, 1)),
                  BS((TR, MEMW), lambda b, t: (b * nt + t, 3)), vec, vec],
        out_specs=[BS((TR, MEMW), lambda b, t: (b * nt + t, dz_cols // MEMW - 1)),
                   BS((MEMT, 2 * MEMW), lambda b, t: (b, 0)), vec, vec, vec],
        out_shape=[_sds((T, dz_cols), BF16), _sds((NB * MEMT, 2 * MEMW), F32),
                   _sds((1, MEMW), F32), _sds((1, MEMW), F32), _sds((1, MEMW), F32)],
        scratch_shapes=[pltpu.VMEM((MEMT, MEMW), F32), pltpu.VMEM((MEMT, MEMW), F32)],
        compiler_params=_cp(("arbitrary", "arbitrary")), name=name,
    )(z, kv, kv, dcat, gq4, gk4)


HALO = 32
NEXT = 64
RT = 64


def _glu(zz):
    return zz[:, :TOK] * jax.nn.sigmoid(zz[:, TOK:])


def _layer_norm_parts(y):
    mu = jnp.mean(y, axis=-1, keepdims=True)
    yc = y - mu
    rstd = lax.rsqrt(jnp.mean(yc * yc, axis=-1, keepdims=True) + EPS)
    return yc * rstd, rstd


def _shifted_copies(src, dst, rows):
    for b in range(1, 8):
        dst[b - 1, 0:rows, :] = src[b:b + rows, :]


def _tap(src, shifted, off, r0, rows):
    b = off % 8
    if b == 0:
        return src[r0 + off:r0 + off + rows, :]
    return shifted[b - 1, r0 + off - b:r0 + off - b + rows, :]


def _conv_rows(w_ref, hbuf, hs, r0, rows):
    y = jnp.zeros((rows, TOK), F32)
    for j in range(CONVW):
        y = y + w_ref[j:j + 1, :] * _tap(hbuf, hs, (HALO - CONVW + 1) + j, r0, rows)
    return y


def _conv_fwd(z, cw, cb, lg, lb):
    nt = SEQ // TR

    def body(zc_ref, zp_ref, w_ref, cb_ref, lg_ref, lb_ref, o_ref, y_ref, hbuf, hs):
        t = pl.program_id(1)
        hbuf[0:HALO, :] = jnp.where(t == 0, 0.0, _glu(zp_ref[...]))
        hbuf[HALO:, :] = _glu(zc_ref[...])
        _shifted_copies(hbuf, hs, HALO + TR - 8)
        for r0 in range(0, TR, RT):
            y = _conv_rows(w_ref, hbuf, hs, r0, RT) + cb_ref[...]
            y_ref[r0:r0 + RT, :] = y
            yh, _ = _layer_norm_parts(y)
            o = yh * lg_ref[...] + lb_ref[...]
            o_ref[r0:r0 + RT, :] = (o * jax.nn.sigmoid(o)).astype(BF16)

    vec = BS((1, TOK), lambda b, t: (0, 0))
    per = TR // HALO
    return pl.pallas_call(
        body, grid=(NB, nt),
        in_specs=[BS((TR, 2 * TOK), lambda b, t: (b * nt + t, 0)),
                  BS((HALO, 2 * TOK), lambda b, t: (jnp.maximum((b * nt + t) * per - 1, 0), 0)),
                  BS((32, TOK), lambda b, t: (0, 0)), vec, vec, vec],
        out_specs=[BS((TR, TOK), lambda b, t: (b * nt + t, 0)), BS((TR, TOK), lambda b, t: (b * nt + t, 0))],
        out_shape=[_sds((T, D), BF16), _sds((T, TOK), F32)],
        scratch_shapes=[pltpu.VMEM((HALO + TR, TOK), F32), pltpu.VMEM((7, HALO + TR, TOK), F32)],
        compiler_params=_cp(("arbitrary", "arbitrary")), name="conv_fwd",
    )(z, z, cw, cb, lg, lb)


def _conv_bwd(z, y, dcat, dz_into, cw, lg, lb):
    nt = SEQ // TR
    ext = TR + NEXT

    def body(zc_ref, zp_ref, yc_ref, yn_ref, dc_ref, dn_ref, w_ref, lg_ref, lb_ref, into_ref,
             du_ref, dw_ref, dcb_ref, dlg_ref, dlb_ref, dbin_ref, hbuf, dybuf, hs, dys):
        del into_ref
        b = pl.program_id(0)
        t = pl.program_id(1)

        @pl.when((b == 0) & (t == 0))
        def _():
            dw_ref[...] = jnp.zeros_like(dw_ref)
            dcb_ref[...] = jnp.zeros_like(dcb_ref)
            dlg_ref[...] = jnp.zeros_like(dlg_ref)
            dlb_ref[...] = jnp.zeros_like(dlb_ref)
            dbin_ref[...] = jnp.zeros_like(dbin_ref)

        hbuf[0:HALO, :] = jnp.where(t == 0, 0.0, _glu(zp_ref[...]))
        hbuf[HALO:, :] = _glu(zc_ref[...])
        _shifted_copies(hbuf, hs, HALO + TR - 8)
        last = t == nt - 1
        for r0 in range(0, ext, RT):
            yh, rstd = _layer_norm_parts(yc_ref[r0:r0 + RT, :] if r0 < TR else yn_ref[...])
            o = yh * lg_ref[...] + lb_ref[...]
            sg = jax.nn.sigmoid(o)
            if r0 < TR:
                dtok = dc_ref[r0:r0 + RT, :]
            else:
                dtok = jnp.where(last, 0.0, dn_ref[...])
            do = dtok * (sg * (1.0 + o * (1.0 - sg)))
            dyh = do * lg_ref[...]
            dy = rstd * (dyh - jnp.mean(dyh, axis=-1, keepdims=True)
                         - yh * jnp.mean(dyh * yh, axis=-1, keepdims=True))
            dybuf[r0:r0 + RT, :] = dy
            if r0 < TR:
                dlg_ref[...] += jnp.sum(do * yh, axis=0, keepdims=True)
                dlb_ref[...] += jnp.sum(do, axis=0, keepdims=True)
                dcb_ref[...] += jnp.sum(dy, axis=0, keepdims=True)
        _shifted_copies(dybuf, dys, ext - 8)
        for r0 in range(0, TR, RT):
            dh = jnp.zeros((RT, TOK), F32)
            for j in range(CONVW):
                dh = dh + w_ref[j:j + 1, :] * _tap(dybuf, dys, (CONVW - 1) - j, r0, RT)
            a = zc_ref[r0:r0 + RT, 0:TOK]
            sg = jax.nn.sigmoid(zc_ref[r0:r0 + RT, TOK:])
            da = dh * sg
            dg = dh * a * (sg * (1.0 - sg))
            du_ref[r0:r0 + RT, 0:TOK] = da.astype(BF16)
            du_ref[r0:r0 + RT, TOK:] = dg.astype(BF16)
            dbin_ref[:, 0:TOK] += jnp.sum(da, axis=0, keepdims=True)
            dbin_ref[:, TOK:] += jnp.sum(dg, axis=0, keepdims=True)
        for j in range(CONVW):
            acc = jnp.zeros((8, TOK), F32)
            for r0 in range(0, TR, RT):
                prod = dybuf[r0:r0 + RT, :] * _tap(hbuf, hs, (HALO - CONVW + 1) + j, r0, RT)
                acc = acc + jnp.sum(prod.reshape(RT // 8, 8, TOK), axis=0)
            dw_ref[j:j + 1, :] += jnp.sum(acc, axis=0, keepdims=True)

    vec = BS((1, TOK), lambda b, t: (0, 0))
    perh = TR // HALO
    pern = TR // NEXT
    nlast_n = T // NEXT - 1
    return pl.pallas_call(
        body, grid=(NB, nt),
        in_specs=[BS((TR, 2 * TOK), lambda b, t: (b * nt + t, 0)),
                  BS((HALO, 2 * TOK), lambda b, t: (jnp.maximum((b * nt + t) * perh - 1, 0), 0)),
                  BS((TR, TOK), lambda b, t: (b * nt + t, 0)),
                  BS((NEXT, TOK), lambda b, t: (jnp.minimum((b * nt + t + 1) * pern, nlast_n), 0)),
                  BS((TR, TOK), lambda b, t: (b * nt + t, 0)),
                  BS((NEXT, TOK), lambda b, t: (jnp.minimum((b * nt + t + 1) * pern, nlast_n), 0)),
                  BS((32, TOK), lambda b, t: (0, 0)), vec, vec, ANY],
        out_specs=[BS((TR, 2 * TOK), lambda b, t: (b * nt + t, 0)),
                   BS((32, TOK), lambda b, t: (0, 0)), vec, vec, vec,
                   BS((1, 2 * TOK), lambda b, t: (0, 0))],
        out_shape=[_sds((T, NBW), BF16), _sds((32, TOK), F32), _sds((1, TOK), F32),
                   _sds((1, TOK), F32), _sds((1, TOK), F32), _sds((1, 2 * TOK), F32)],
        input_output_aliases={9: 0},
        scratch_shapes=[pltpu.VMEM((HALO + TR, TOK), F32), pltpu.VMEM((ext, TOK), F32),
                        pltpu.VMEM((7, HALO + TR, TOK), F32), pltpu.VMEM((7, ext, TOK), F32)],
        compiler_params=_cp(("arbitrary", "arbitrary"), vmem_mb=56), name="conv_bwd",
    )(z, z, y, y, dcat, dcat, cw, lg, lb, dz_into)


def _ffn_up(h2, wgu, l, carry=None):
    ni = len(carry.ins) if carry else 0
    no = len(carry.out_shapes) if carry else 0

    def body(h_ref, wg_ref, wu_ref, *rest):
        cin = rest[:ni]
        g_ref, u_ref, a_ref = rest[ni:ni + 3]
        cout, cscr = rest[ni + 3:ni + 3 + no], rest[ni + 3 + no:]
        if carry:
            @pl.when((pl.program_id(0) == 0) & (pl.program_id(1) == 0))
            def _():
                carry.start(cin, cout, cscr)

        hv = h_ref[...]
        g = lax.dot_general(hv, wg_ref[...], _DIMS["nt"], preferred_element_type=F32)
        u = lax.dot_general(hv, wu_ref[...], _DIMS["nt"], preferred_element_type=F32)
        sg = jax.nn.sigmoid(g)
        silu = g * sg
        g_ref[...] = (u * (sg * (1.0 + g * (1.0 - sg)))).astype(BF16)
        u_ref[...] = silu.astype(BF16)
        a_ref[...] = (silu * u).astype(BF16)

        if carry:
            @pl.when((pl.program_id(0) == FF // FT - 1) & (pl.program_id(1) == T // TR - 1))
            def _():
                carry.finish(cin, cout, cscr)

    out = BS((TR, FT), lambda q, i: (i, q))
    outs = pl.pallas_call(
        body, grid=(FF // FT, T // TR),
        in_specs=[BS((TR, D), lambda q, i: (i, 0)),
                  BS((None, FT, D), lambda q, i: (0, q, 0)),
                  BS((None, FT, D), lambda q, i: (1, q, 0))] + (carry.in_specs if carry else []),
        out_specs=[out, out, out] + [ANY] * no,
        out_shape=[_sds((T, FF), BF16), _sds((T, FF), BF16), _sds((T, FF), BF16)]
        + (carry.out_shapes if carry else []),
        scratch_shapes=carry.scratch if carry else [],
        compiler_params=pltpu.CompilerParams(
            dimension_semantics=("arbitrary", "arbitrary"), vmem_limit_bytes=48 << 20,
            has_side_effects=bool(carry)), name=f"ffn_up_{l}",
    )(h2, wgu, wgu, *(carry.ins if carry else []))
    return outs[:3], outs[3:]


def _ffn_down_bwd(dx, wd, g, u, l):
    def epilogue(dact, ex, o_ref, i):
        o_ref[0] = (dact * ex[0][...].astype(F32)).astype(BF16)
        o_ref[1] = (dact * ex[1][...].astype(F32)).astype(BF16)

    ex_spec = BS((TR, FT), lambda i, q, k: (i, q))
    return _mm("nt", dx, wd, grid=(T // TR, FF // FT, 1),
               a_spec=BS((TR, D), lambda i, q, k: (i, 0)),
               b_spec=BS((FT, D), lambda i, q, k: (q, 0)),
               out_shape=_sds((2, T, FF), BF16),
               out_spec=BS((2, TR, FT), lambda i, q, k: (0, i, q)),
               acc_shape=(TR, FT), extras=(g, u), extra_specs=(ex_spec, ex_spec),
               epilogue=epilogue, name=f"ffn_down_bwd_{l}")


def _row_tile(rows, cols, itemsize=4, limit=2 << 20):
    tr = rows
    while tr * cols * itemsize > limit and tr % 2 == 0 and (tr // 2) % 16 == 0:
        tr //= 2
    return tr


def _cast_bf16(arrs, name):
    n = len(arrs)
    rows, cols = arrs[0].shape
    tr = _row_tile(rows, cols)

    def body(*refs):
        o_ref = refs[n]
        k = pl.program_id(0)
        val = refs[0][...]
        for j in range(1, n):
            val = jnp.where(k == j, refs[j][...], val)
        o_ref[...] = val.astype(BF16)

    return pl.pallas_call(
        body, grid=(n, rows // tr),
        in_specs=[BS((tr, cols), lambda k, i: (i, 0))] * n,
        out_specs=BS((None, tr, cols), lambda k, i: (k, i, 0)),
        out_shape=_sds((n, rows, cols), BF16),
        compiler_params=_cp(("arbitrary", "arbitrary")), name=name,
    )(*arrs)


def _quad_sum(own, got, name):
    n, rows, cols = own.shape
    tr = _row_tile(rows, cols)

    def body(a_ref, q_ref, o_ref):
        o_ref[...] = ((a_ref[...] + q_ref[0].astype(F32)) + q_ref[1].astype(F32)) + q_ref[2].astype(F32)

    spec = BS((None, tr, cols), lambda k, i: (k, i, 0))
    return pl.pallas_call(
        body, grid=(n, rows // tr),
        in_specs=[spec, BS((3, None, tr, cols), lambda k, i: (0, k, i, 0))], out_specs=spec,
        out_shape=_sds((n, rows, cols), F32),
        compiler_params=_cp(("arbitrary", "arbitrary")), name=name,
    )(own, got)


def _adam_math(w, g, m, v):
    m = ADAM_B1 * m + (1.0 - ADAM_B1) * g
    v = ADAM_B2 * v + (1.0 - ADAM_B2) * (g * g)
    m_hat = m / (1.0 - ADAM_B1 ** ADAM_STEP)
    v_hat = v / (1.0 - ADAM_B2 ** ADAM_STEP)
    delta = -ADAM_LR * (m_hat / (jnp.sqrt(v_hat) + ADAM_EPS) + ADAM_WD * w)
    return delta, m, v


def _adamw_big(w, g, m, v, name):
    shape = w.shape
    cols = shape[-1]
    rows = w.size // cols
    tr = _row_tile(rows, cols, limit=1 << 20)

    def body(w_ref, g_ref, m_ref, v_ref, d_ref, nm_ref, nv_ref):
        d, nm, nv = _adam_math(w_ref[...], g_ref[...], m_ref[...], v_ref[...])
        d_ref[...] = d
        nm_ref[...] = nm
        nv_ref[...] = nv

    spec = BS((tr, cols), lambda i: (i, 0))
    outs = pl.pallas_call(
        body, grid=(rows // tr,), in_specs=[spec] * 4, out_specs=[spec] * 3,
        out_shape=[_sds((rows, cols), F32)] * 3,
        compiler_params=_cp(("arbitrary",)), name=name,
    )(*[a.reshape(rows, cols) for a in (w, g, m, v)])
    return [o.reshape(shape) for o in outs]


def _adamw_small(ws, gs, ms, vs):
    n = len(ws)

    def body(*refs):
        for i in range(n):
            d, nm, nv = _adam_math(refs[i][...], refs[n + i][...], refs[2 * n + i][...],
                                   refs[3 * n + i][...])
            refs[4 * n + i][...] = d
            refs[5 * n + i][...] = nm
            refs[6 * n + i][...] = nv

    specs = [BS(w.shape, lambda i: (0, 0)) for w in ws]
    outs = pl.pallas_call(
        body, grid=(1,), in_specs=specs * 4, out_specs=specs * 3,
        out_shape=[_sds(w.shape, F32) for w in ws] * 3,
        compiler_params=_cp(("arbitrary",)), name="adamw_small",
    )(*ws, *gs, *ms, *vs)
    return outs[:n], outs[n:2 * n], outs[2 * n:]


def _place():
    x, y, c = lax.axis_index("x"), lax.axis_index("y"), lax.axis_index("c")
    chips = [(1 - x, y), (x, 1 - y), (1 - x, 1 - y)]
    return x, y, c, chips


class _Exchange:
    def __init__(self, ins, in_specs, out_shapes, scratch, start, finish):
        self.ins, self.in_specs, self.out_shapes, self.scratch = ins, in_specs, out_shapes, scratch
        self.start, self.finish = start, finish


def _run_exchange(ex, name, vmem_mb=40):
    ni, no = len(ex.ins), len(ex.out_shapes)

    def body(*refs):
        ex.start(refs[:ni], refs[ni:ni + no], refs[ni + no:])
        ex.finish(refs[:ni], refs[ni:ni + no], refs[ni + no:])

    return pl.pallas_call(
        body, in_specs=ex.in_specs, out_specs=[ANY] * no, out_shape=ex.out_shapes,
        scratch_shapes=ex.scratch,
        compiler_params=pltpu.CompilerParams(has_side_effects=True, vmem_limit_bytes=vmem_mb << 20),
        name=name,
    )(*ex.ins)


def _gather_exchange(srcs, dst_shapes, views, small=None):
    nu = len(srcs)
    nd = len(dst_shapes)
    ns = 1 if small is not None else 0

    def unpack(ins, outs, scr):
        x, y, c, chips = _place()
        src = ins[:nu]
        vw = [views[u](outs[:nd]) for u in range(nu)]
        vbuf = scr[:nu]
        send, recv, fsend, frecv, lsem, ssend, srecv, vsem = scr[nu:]

        def ici(u, j, shard, to):
            return pltpu.make_async_remote_copy(
                src_ref=vbuf[u].at[:, c], dst_ref=vw[u].at[:, shard, c],
                send_sem=send.at[3 * u + j], recv_sem=recv.at[3 * u + j],
                device_id=to, device_id_type=MESH)

        def fwd(u, j, shard, half):
            return pltpu.make_async_remote_copy(
                src_ref=vw[u].at[:, shard, half], dst_ref=vw[u].at[:, shard, half],
                send_sem=fsend.at[3 * u + j], recv_sem=frecv.at[3 * u + j],
                device_id=(x, y, 1 - c), device_id_type=MESH)

        def small_copy(j, shard, to):
            return pltpu.make_async_remote_copy(
                src_ref=ins[nu], dst_ref=outs[nd].at[shard],
                send_sem=ssend.at[j], recv_sem=srecv.at[j], device_id=to, device_id_type=MESH)

        stage = [pltpu.make_async_copy(src[u], vbuf[u], vsem.at[u]) for u in range(nu)]
        local = [pltpu.make_async_copy(vbuf[u], vw[u].at[:, 2 * x + y], lsem.at[u]) for u in range(nu)]
        if ns:
            local.append(pltpu.make_async_copy(ins[nu], outs[nd].at[2 * x + y], lsem.at[nu]))
        return x, y, c, chips, ici, fwd, small_copy, stage, local

    def start(ins, outs, scr):
        x, y, c, chips, ici, fwd, small_copy, stage, local = unpack(ins, outs, scr)
        s = 2 * x + y
        for cp in stage:
            cp.start()
        if ns:
            local[nu].start()
            for j, chip in enumerate(chips):
                small_copy(j, s, (*chip, c)).start()
        for u in range(nu):
            stage[u].wait()
            for j, chip in enumerate(chips):
                ici(u, j, s, (*chip, c)).start()
            local[u].start()

    def finish(ins, outs, scr):
        x, y, c, chips, ici, fwd, small_copy, stage, local = unpack(ins, outs, scr)
        s = 2 * x + y
        for u in range(nu):
            for j, chip in enumerate(chips):
                sj = 2 * chip[0] + chip[1]
                ici(u, j, sj, (x, y, c)).wait_recv()
                fwd(u, j, sj, c).start()
        for u in range(nu):
            for j, chip in enumerate(chips):
                sj = 2 * chip[0] + chip[1]
                fwd(u, j, sj, 1 - c).wait_recv()
        for u in range(nu):
            for j, chip in enumerate(chips):
                ici(u, j, s, (*chip, c)).wait_send()
                fwd(u, j, s, c).wait_send()
        if ns:
            for j, chip in enumerate(chips):
                small_copy(j, 2 * chip[0] + chip[1], (x, y, c)).wait_recv()
                small_copy(j, s, (*chip, c)).wait_send()
        for cp in local:
            cp.wait()

    dma = pltpu.SemaphoreType.DMA
    return _Exchange(
        ins=list(srcs) + ([small] if ns else []),
        in_specs=[ANY] * nu + [BS(memory_space=pltpu.VMEM)] * ns,
        out_shapes=[_sds(sh, BF16) for sh in dst_shapes]
        + ([_sds((NSH,) + small.shape, F32)] if ns else []),
        scratch=[pltpu.VMEM(a.shape, BF16) for a in srcs]
        + [dma((3 * nu,)), dma((3 * nu,)), dma((3 * nu,)), dma((3 * nu,)),
           dma((nu + 1,)), dma((3,)), dma((3,)), dma((nu,))],
        start=start, finish=finish)


def _pair_reduce(gs, name):
    nu = len(gs)
    ns = [g.shape[0] * NSH for g in gs]
    base = [sum(ns[:u]) for u in range(nu)]

    def body(*refs):
        g_refs, own_refs, sb_refs = refs[:nu], refs[nu:2 * nu], refs[2 * nu:3 * nu]
        bufs = refs[3 * nu:8 * nu]
        send, recv, lsem, osem = refs[8 * nu:]
        x, y, c, _ = _place()
        s = 2 * x + y

        def unit(u):
            sendb, recvb, stage, outf, outb = bufs[5 * u:5 * u + 5]

            def load(k, half):
                return pltpu.make_async_copy(g_refs[u].at[k // NSH, k % NSH, half], stage.at[k % 3],
                                             lsem.at[3 * u + k % 3])

            def push(k):
                return pltpu.make_async_remote_copy(
                    src_ref=sendb.at[k], dst_ref=recvb.at[k], send_sem=send.at[base[u] + k],
                    recv_sem=recv.at[base[u] + k], device_id=(x, y, 1 - c), device_id_type=MESH)

            def store(k):
                return pltpu.make_async_copy(outb.at[k % 2], sb_refs[u].at[k // NSH, k % NSH],
                                             osem.at[3 * u + k % 2])

            return sendb, recvb, stage, outf, outb, load, push, store

        for u in range(nu):
            sendb, recvb, stage, outf, outb, load, push, store = unit(u)
            for k in range(min(2, ns[u])):
                load(k, 1 - c).start()
            for k in range(ns[u]):
                if k + 2 < ns[u]:
                    load(k + 2, 1 - c).start()
                load(k, 1 - c).wait()
                sendb[k] = stage[k % 3].astype(BF16)
                push(k).start()
        for u in range(nu):
            sendb, recvb, stage, outf, outb, load, push, store = unit(u)
            n = ns[u]
            for k in range(min(2, n)):
                load(k, c).start()
            for k in range(n):
                if k + 2 < n:
                    load(k + 2, c).start()
                load(k, c).wait()
                push(k).wait_recv()
                total = stage[k % 3] + recvb[k].astype(F32)
                if k >= 2:
                    store(k - 2).wait()
                outb[k % 2] = total.astype(BF16)
                store(k).start()

                @pl.when(s == k % NSH)
                def _():
                    outf[...] = total
                    keep = pltpu.make_async_copy(outf, own_refs[u].at[k // NSH], osem.at[3 * u + 2])
                    keep.start()
                    keep.wait()

            for k in range(max(n - 2, 0), n):
                store(k).wait()
        for u in range(nu):
            push = unit(u)[6]
            for k in range(ns[u]):
                push(k).wait_send()

    dma = pltpu.SemaphoreType.DMA
    scratch = []
    for g, n in zip(gs, ns):
        rh, cc = g.shape[3], g.shape[4]
        scratch += [pltpu.VMEM((n, rh, cc), BF16), pltpu.VMEM((n, rh, cc), BF16),
                    pltpu.VMEM((3, rh, cc), F32), pltpu.VMEM((rh, cc), F32), pltpu.VMEM((2, rh, cc), BF16)]
    outs = pl.pallas_call(
        body, in_specs=[ANY] * nu, out_specs=[ANY] * (2 * nu),
        out_shape=[_sds((g.shape[0], g.shape[3], g.shape[4]), F32) for g in gs]
        + [_sds((g.shape[0], NSH, g.shape[3], g.shape[4]), BF16) for g in gs],
        scratch_shapes=scratch + [dma((sum(ns),)), dma((sum(ns),)), dma((3 * nu,)), dma((3 * nu,))],
        compiler_params=pltpu.CompilerParams(has_side_effects=True, vmem_limit_bytes=56 << 20),
        name=name,
    )(*gs)
    return list(outs[:nu]), list(outs[nu:])


def _chip_exchange(sums_bf16):
    nu = len(sums_bf16)

    def pushes(ins, outs, scr):
        x, y, c, chips = _place()
        send, recv = scr
        return [pltpu.make_async_remote_copy(
            src_ref=ins[u].at[:, 2 * chip[0] + chip[1]], dst_ref=outs[u].at[j],
            send_sem=send.at[3 * u + j], recv_sem=recv.at[3 * u + j],
            device_id=(*chip, c), device_id_type=MESH)
            for u in range(nu) for j, chip in enumerate(chips)]

    def start(ins, outs, scr):
        for cp in pushes(ins, outs, scr):
            cp.start()

    def finish(ins, outs, scr):
        for cp in pushes(ins, outs, scr):
            cp.wait()

    dma = pltpu.SemaphoreType.DMA
    shapes = [(3, a.shape[0], a.shape[2], a.shape[3]) for a in sums_bf16]
    return _Exchange(ins=list(sums_bf16), in_specs=[ANY] * nu,
                     out_shapes=[_sds(sh, BF16) for sh in shapes],
                     scratch=[dma((3 * nu,)), dma((3 * nu,))], start=start, finish=finish)


def _final_exchange(halves, out_shapes, targets):
    nu = len(halves)
    no = len(out_shapes)
    ncp = sum(len(t) for t in targets)

    def body(*refs):
        hv = refs[:nu]
        out = refs[nu:nu + no]
        sbuf = refs[nu + no:2 * nu + no]
        rbuf = refs[2 * nu + no:3 * nu + no]
        send, recv, lsem, osem, csem = refs[3 * nu + no:]
        x, y, c, _ = _place()
        stage = [pltpu.make_async_copy(hv[u], sbuf[u], lsem.at[u]) for u in range(nu)]
        push = [pltpu.make_async_remote_copy(
            src_ref=sbuf[u], dst_ref=rbuf[u], send_sem=send.at[u], recv_sem=recv.at[u],
            device_id=(x, y, 1 - c), device_id_type=MESH) for u in range(nu)]
        mine, theirs = [], []
        k = 0
        for u in range(nu):
            rh = hv[u].shape[1]
            for (p, oi, li) in targets[u]:
                mine.append((u, pltpu.make_async_copy(
                    sbuf[u].at[p], out[oi].at[li, pl.ds(c * rh, rh), :], csem.at[k])))
                theirs.append((u, pltpu.make_async_copy(
                    rbuf[u].at[p], out[oi].at[li, pl.ds((1 - c) * rh, rh), :], osem.at[k])))
                k += 1
        for cp in stage:
            cp.start()
        for u in range(nu):
            stage[u].wait()
            push[u].start()
            for (v, cp) in mine:
                if v == u:
                    cp.start()
        for u in range(nu):
            push[u].wait_recv()
            for (v, cp) in theirs:
                if v == u:
                    cp.start()
        for (_, cp) in theirs + mine:
            cp.wait()
        for u in range(nu):
            push[u].wait_send()

    dma = pltpu.SemaphoreType.DMA
    bufs = [pltpu.VMEM(h.shape, F32) for h in halves]
    return pl.pallas_call(
        body, in_specs=[ANY] * nu, out_specs=[ANY] * no,
        out_shape=[_sds(sh, F32) for sh in out_shapes],
        scratch_shapes=bufs + bufs + [dma((nu,)), dma((nu,)), dma((nu,)), dma((ncp,)), dma((ncp,))],
        compiler_params=pltpu.CompilerParams(has_side_effects=True, vmem_limit_bytes=56 << 20),
        name="final_exchange",
    )(*halves)


def _small_allreduce(pack):
    rows = pack.shape[0]

    def body(p_ref, o_ref, buf, send, recv):
        x, y, c, _ = _place()
        me = 4 * x + 2 * y + c
        buf[me] = p_ref[...]
        k = 0
        copies = []
        for dx in range(2):
            for dy in range(2):
                for dc in range(2):
                    if dx == 0 and dy == 0 and dc == 0:
                        continue
                    to = (jnp.where(dx, 1 - x, x), jnp.where(dy, 1 - y, y), jnp.where(dc, 1 - c, c))
                    src_slot = 4 * to[0] + 2 * to[1] + to[2]
                    copies.append((pltpu.make_async_remote_copy(
                        src_ref=p_ref, dst_ref=buf.at[me], send_sem=send.at[k], recv_sem=recv.at[k],
                        device_id=to, device_id_type=MESH), src_slot, k))
                    k += 1
        for cp, _, _ in copies:
            cp.start()
        for cp, src_slot, k in copies:
            pltpu.make_async_remote_copy(
                src_ref=p_ref, dst_ref=buf.at[src_slot], send_sem=send.at[k], recv_sem=recv.at[k],
                device_id=(x, y, c), device_id_type=MESH).wait()
        acc = buf[0]
        for d in range(1, 8):
            acc = acc + buf[d]
        o_ref[...] = acc

    dma = pltpu.SemaphoreType.DMA
    vm = BS(memory_space=pltpu.VMEM)
    return pl.pallas_call(
        body, in_specs=[vm], out_specs=vm, out_shape=_sds((rows, D), F32),
        scratch_shapes=[pltpu.VMEM((8, rows, D), F32), dma((7,)), dma((7,))],
        compiler_params=pltpu.CompilerParams(has_side_effects=True, vmem_limit_bytes=32 << 20),
        name="small_allreduce",
    )(pack)


def _in_proj(h, w, bias, name, transposed=False, carry=None):
    n = w.shape[0] if transposed else w.shape[1]
    tn = 1280 if n == NA else 1792
    ep = None
    extras, especs = (), ()
    if bias is not None:
        def ep(acc, ex, o_ref, i):
            o_ref[...] = acc + ex[0][...]
        extras = (bias,)
        especs = (BS((1, tn), lambda i, j, k: (0, j)),)
    b_spec = BS((tn, D), lambda i, j, k: (j, 0)) if transposed else BS((D, tn), lambda i, j, k: (0, j))
    return _mm("nt" if transposed else "nn", h, w, grid=(T // TR, n // tn, 1),
               a_spec=BS((TR, D), lambda i, j, k: (i, 0)), b_spec=b_spec,
               out_shape=_sds((T, n), F32), out_spec=BS((TR, tn), lambda i, j, k: (i, j)),
               acc_shape=(TR, tn), extras=extras, extra_specs=especs, epilogue=ep, name=name, carry=carry)


def _res_rms_epilogue(acc, ex, outs, i):
    y = acc + ex[0][...]
    outs[0][...] = y
    r = lax.rsqrt(jnp.mean(y * y, axis=-1, keepdims=True) + EPS)
    outs[1][...] = (y * r * ex[1][...]).astype(BF16)


def _res_loss_epilogue(acc, ex, outs, i):
    e = acc + ex[0][...] - ex[1][...]
    outs[1][...] = e * (1.0 / D)

    @pl.when(i == 0)
    def _():
        outs[0][...] = jnp.zeros_like(outs[0])

    outs[0][...] += 0.5 * jnp.sum(jnp.mean(e * e, axis=-1, keepdims=True), axis=0, keepdims=True)


def _rms_bwd_epilogue(dh, ex, outs, i):
    xv = ex[0][...]
    r = lax.rsqrt(jnp.mean(xv * xv, axis=-1, keepdims=True) + EPS)
    xh = xv * r
    gy = dh * ex[1][...]
    outs[0][...] = r * (gy - xh * jnp.mean(gy * xh, axis=-1, keepdims=True)) + ex[2][...]

    @pl.when(i == 0)
    def _():
        outs[1][...] = jnp.zeros_like(outs[1])

    outs[1][...] += jnp.sum(dh * xh, axis=0, keepdims=True)


def _local_step(x, mem, target, w, p, carries=None, bwd_carry_fn=None, late_carry_fn=None):
    row = lambda i, j, k: (i, 0)
    whole = lambda i, j, k: (0, 0)
    w = {k: (list(v) if isinstance(v, list) else v) for k, v in w.items()}
    carries = carries or {}

    def carry_of(name):
        return carries[name][0] if name in carries else None

    def delivered(name, outs):
        if name in carries:
            carries[name][1](w, outs)

    saved = []
    bias = _bias_expand(p["rel_u"])
    vec = BS((1, D), whole)
    h = _rms_fwd(x, p["norm1_g"][0:1], "rms1_0", carry=carry_of("rms1_0"))
    if carry_of("rms1_0") is not None:
        h, carried = h
        delivered("rms1_0", carried)
    for l in range(2):
        type_a = l == 0
        memn = _rms_fwd(mem, p["mem_norm_g"][l:l + 1], f"rmsmem_{l}")
        if type_a:
            z = _in_proj(h, w["a"], None, "inproj_a", carry=carry_of("inproj_a"))
            if carry_of("inproj_a") is not None:
                z, carried = z
                delivered("inproj_a", carried)
            cat, carried = _attn_fwd(z, bias, p["a_q_g2"], p["a_k_g2"], carry_of("attn_fwd"))
            delivered("attn_fwd", carried)
            qcol = NA // MEMW - 1
        else:
            z = _in_proj(h, w["b"], w["b_b_in"], "inproj_b", transposed=True)
            cat, conv_y = _conv_fwd(z, w["conv_w"], w["conv_b"], w["ln_g"], w["ln_b"])
            qcol = NBW // MEMW - 1
        kv = _mm("nn", memn, w["kv"][l], grid=(1, 1, 1),
                 a_spec=BS((NB * MEMT, D), whole), b_spec=BS((D, 2 * MEMW), whole),
                 out_shape=_sds((NB * MEMT, 2 * MEMW), F32), out_spec=BS((NB * MEMT, 2 * MEMW), whole),
                 acc_shape=(8, 128), name=f"memkv_{l}")
        cat = _memattn_fwd(z, kv, cat, p["mq_g4"][l:l + 1], p["mk_g4"][l:l + 1], qcol, f"memattn_fwd_{l}")
        x1, h2 = _mm("nn", cat, w["wo"][l], grid=(T // TR, 1, 1), a_spec=BS((TR, D), row),
                     b_spec=BS((D, D), whole),
                     out_shape=[_sds((T, D), F32), _sds((T, D), BF16)],
                     out_spec=[BS((TR, D), row), BS((TR, D), row)], acc_shape=(8, 128),
                     extras=(x, p["norm2_g"][l:l + 1]), extra_specs=(BS((TR, D), row), vec),
                     epilogue=_res_rms_epilogue, name=f"outproj_{l}")
        (g, u, act), carried = _ffn_up(h2, w["gu"][l], l, carry_of(f"ffn_up_{l}"))
        delivered(f"ffn_up_{l}", carried)
        last = l == 1
        res = _mm("nn", act, w["wd"][l], grid=(T // TR, 1, 1),
                  a_spec=BS((TR, FF), row), b_spec=BS((FF, D), whole),
                  out_shape=[_sds((8, 128), F32), _sds((T, D), F32)] if last else
                  [_sds((T, D), F32), _sds((T, D), BF16)],
                  out_spec=[BS((8, 128), whole), BS((TR, D), row)] if last else
                  [BS((TR, D), row), BS((TR, D), row)],
                  acc_shape=(8, 128), extras=(x1, target if last else p["norm1_g"][1:2]),
                  extra_specs=(BS((TR, D), row), BS((TR, D), row) if last else vec),
                  epilogue=_res_loss_epilogue if last else _res_rms_epilogue, sequential=last,
                  name=f"ffn_down_{l}", carry=carry_of(f"ffn_down_{l}"))
        if carry_of(f"ffn_down_{l}") is not None:
            res, carried = res
            delivered(f"ffn_down_{l}", carried)
        saved.append(dict(x=x, h=h, memn=memn, kv=kv, z=z, cat=cat, x1=x1, h2=h2, g=g, u=u, act=act,
                          qcol=qcol))
        if last:
            loss, dx = res
        else:
            x, h = res

    big = dict(a=None, b=None, kv=[None, None], wo=[None, None], gu=[None, None], wd=[None, None])
    small = {}
    bwd_carried, late_carried = (), ()
    tk = T // 2
    nkt = T // tk
    for l in (1, 0):
        sv = saved[l]
        dgu = _ffn_down_bwd(dx, w["wd"][l], sv["g"], sv["u"], l)
        big["wd"][l] = _mm("tn", sv["act"], dx, grid=(FF // FT, 1, 2 * nkt),
                           a_spec=BS((tk // 2, FT), lambda i, j, k: (k, i)),
                           b_spec=BS((tk // 2, D), lambda i, j, k: (k, 0)),
                           out_shape=_sds((FF, D), F32), out_spec=BS((FT, D), lambda i, j, k: (i, 0)),
                           acc_shape=(FT, D), name=f"dw_down_{l}")
        dx1, small[f"norm2_g{l}"] = _mm(
            "nn", dgu, w["gu"][l], grid=(T // TR, 1, 2),
            a_spec=BS((None, TR, FF), lambda i, j, k: (k, i, 0)),
            b_spec=BS((None, FF, D), lambda i, j, k: (k, 0, 0)),
            out_shape=[_sds((T, D), F32), _sds((1, D), F32)], out_spec=[BS((TR, D), row), vec],
            acc_shape=(TR, D), extras=(sv["x1"], p["norm2_g"][l:l + 1], dx),
            extra_specs=(BS((TR, D), row), vec, BS((TR, D), row)),
            epilogue=_rms_bwd_epilogue, sequential=True, name=f"dh2_{l}")
        big["gu"][l] = _mm("tn", dgu, sv["h2"], grid=(2 * FF // FT, 1, nkt),
                           a_spec=BS((None, tk, FT), lambda i, j, k: (i // 2, k, i % 2)),
                           b_spec=BS((tk, D), lambda i, j, k: (k, 0)),
                           out_shape=_sds((2, FF, D), F32),
                           out_spec=BS((None, FT, D), lambda i, j, k: (i // 2, i % 2, 0)),
                           acc_shape=(FT, D), vmem_mb=58, name=f"dw_gu_{l}")
        dcat = _mm("nt", dx1, w["wo"][l], grid=(T // TR, 1, 1), a_spec=BS((TR, D), row),
                   b_spec=BS((D, D), whole),
                   out_shape=_sds((T, D), F32), out_spec=BS((TR, D), row), acc_shape=(8, 128),
                   name=f"dcat_{l}")
        big["wo"][l] = _mm("tn", sv["cat"], dx1, grid=(1, 1, nkt),
                           a_spec=BS((tk, D), lambda i, j, k: (k, 0)), b_spec=BS((tk, D), lambda i, j, k: (k, 0)),
                           out_shape=_sds((D, D), F32), out_spec=BS((D, D), whole),
                           acc_shape=(D, D), name=f"dw_out_{l}")
        dqm, dkv, small[f"mq_g{l}"], small[f"mk_g{l}"], dqm_colsum = _memattn_bwd(
            sv["z"], sv["kv"], dcat, p["mq_g4"][l:l + 1], p["mk_g4"][l:l + 1], sv["qcol"], f"memattn_bwd_{l}",
            dz_cols=NA if l == 0 else NBW)
        big["kv"][l] = _mm("tn", sv["memn"], dkv, grid=(1, 1, 1),
                           a_spec=BS((NB * MEMT, D), whole), b_spec=BS((NB * MEMT, 2 * MEMW), whole),
                           out_shape=_sds((D, 2 * MEMW), F32), out_spec=BS((D, 2 * MEMW), whole),
                           acc_shape=(8, 128), name=f"dw_kv_{l}")
        dmemn = _mm("nt", dkv, w["kv"][l], grid=(1, 1, 1),
                    a_spec=BS((NB * MEMT, 2 * MEMW), whole), b_spec=BS((D, 2 * MEMW), whole),
                    out_shape=_sds((NB * MEMT, D), F32), out_spec=BS((NB * MEMT, D), whole),
                    acc_shape=(8, 128), name=f"dmemn_{l}")
        _, small[f"mem_norm_g{l}"] = _rms_bwd(dmemn, mem, p["mem_norm_g"][l:l + 1], None, f"rmsmem_bwd_{l}")
        if l == 0:
            carry = bwd_carry_fn(big) if bwd_carry_fn is not None else None
            (dz, dbias, small["a_q_g"], small["a_k_g"]), bwd_carried = _attn_bwd(
                sv["z"], dcat, dqm, bias, p["a_q_g2"], p["a_k_g2"], carry)
            small["rel_u"] = _bias_reduce(dbias)
            w_in, key, n, tn = w["a"], "a", NA, 1280
        else:
            dz, small["conv_w"], small["conv_b"], small["ln_g"], small["ln_b"], dbin_u = _conv_bwd(
                sv["z"], conv_y, dcat, dqm, w["conv_w"], w["ln_g"], w["ln_b"])
            small["b_in_u"] = dbin_u
            small["b_in_qm"] = dqm_colsum
            w_in, key, n, tn = w["b"], "b", NBW, 896
        norm_bwd = dict(out_shape=[_sds((T, D), F32), _sds((1, D), F32)], out_spec=[BS((TR, D), row), vec],
                        acc_shape=(8, 128), extras=(sv["x"], p["norm1_g"][l:l + 1], dx1),
                        extra_specs=(BS((TR, D), row), vec, BS((TR, D), row)),
                        epilogue=_rms_bwd_epilogue, sequential=True, name=f"dh_{l}")
        if l == 0:
            big[key] = _mm("tn", sv["h"], dz, grid=(1, n // tn, nkt),
                           a_spec=BS((tk, D), lambda i, j, k: (k, 0)), b_spec=BS((tk, tn), lambda i, j, k: (k, j)),
                           out_shape=_sds((D, n), F32), out_spec=BS((D, tn), lambda i, j, k: (0, j)),
                           acc_shape=(D, tn), name=f"dw_in_{l}")
            carry = late_carry_fn(big) if late_carry_fn is not None else None
            res = _mm("nt", dz, w_in, grid=(T // TR, 1, 1), a_spec=BS((TR, n), row),
                      b_spec=BS((D, n), whole), carry=carry, **norm_bwd)
            if carry is not None:
                res, late_carried = res
            dx, small[f"norm1_g{l}"] = res
        else:
            dx, small[f"norm1_g{l}"] = _mm("nn", dz, w_in, grid=(T // TR, 1, 1),
                                           a_spec=BS((TR, n), row), b_spec=BS((n, D), whole), **norm_bwd)
            big[key] = _mm("tn", dz, sv["h"], grid=(n // tn, 1, nkt),
                           a_spec=BS((tk, tn), lambda i, j, k: (k, i)), b_spec=BS((tk, D), lambda i, j, k: (k, 0)),
                           out_shape=_sds((n, D), F32), out_spec=BS((tn, D), lambda i, j, k: (i, 0)),
                           acc_shape=(tn, D), name=f"dw_in_{l}")
    return loss, dx, big, small, bwd_carried, late_carried


_PACK_ROWS = 64


def _pack_small(sm):
    plan = [("norm1_g0", 0, 1, 0, D), ("norm1_g1", 1, 1, 0, D), ("mem_norm_g0", 2, 1, 0, D),
            ("mem_norm_g1", 3, 1, 0, D), ("norm2_g0", 4, 1, 0, D), ("norm2_g1", 5, 1, 0, D),
            ("a_q_g", 6, 1, 0, AW), ("a_k_g", 7, 1, 0, AW), ("mq_g0", 8, 1, 0, MEMW),
            ("mq_g1", 9, 1, 0, MEMW), ("mk_g0", 10, 1, 0, MEMW), ("mk_g1", 11, 1, 0, MEMW),
            ("conv_b", 12, 1, 0, TOK), ("ln_g", 13, 1, 0, TOK), ("ln_b", 14, 1, 0, TOK),
            ("b_in_u", 15, 1, 0, D), ("b_in_u", 16, 1, D, 2 * TOK - D), ("b_in_qm", 17, 1, 0, MEMW),
            ("conv_w", 18, CONVW, 0, TOK), ("rel_u", 49, 12, 0, D), ("loss", 61, 1, 0, 128)]
    arrs = [sm[name].reshape(12, D) if name == "rel_u" else sm[name] for name, *_ in plan]

    def body(*refs):
        o_ref = refs[-1]
        o_ref[...] = jnp.zeros_like(o_ref)
        for ref, (_, r0, nr, c0, nc) in zip(refs, plan):
            o_ref[r0:r0 + nr, 0:nc] = ref[0:nr, c0:c0 + nc]

    return pl.pallas_call(
        body, grid=(1,), in_specs=[BS(a.shape, lambda i: (0, 0)) for a in arrs],
        out_specs=BS((_PACK_ROWS, D), lambda i: (0, 0)), out_shape=_sds((_PACK_ROWS, D), F32),
        compiler_params=_cp(("arbitrary",)), name="pack_small",
    )(*arrs)


def _rel_table_to_u(rel_bias):
    flat = jnp.concatenate([jnp.broadcast_to(rel_bias[:, 191:192], (12, 447)), rel_bias[:, ::-1]], axis=1)
    return jnp.pad(flat, ((0, 0), (192, 1024 - 192 - 639))).reshape(12, 1, 1024)


def _u_to_rel_table(du):
    flat = du[:, 192:192 + 639]
    g = flat[:, 447:][:, ::-1]
    return g, flat[:, :447]


def kernel(x, mem, norm1_g, mem_norm_g, a_w_in, a_q_g, a_k_g, a_rel_bias, b_w_in, b_b_in, b_conv_w, b_conv_b, b_ln_g, b_ln_b, mq_g, mk_g, w_mem_kv, w_out, norm2_g, w_gate, w_up, w_down, loss_target, m_norm1_g, m_mem_norm_g, m_a_w_in, m_a_q_g, m_a_k_g, m_a_rel_bias, m_b_w_in, m_b_b_in, m_b_conv_w, m_b_conv_b, m_b_ln_g, m_b_ln_b, m_mq_g, m_mk_g, m_w_mem_kv, m_w_out, m_norm2_g, m_w_gate, m_w_up, m_w_down, v_norm1_g, v_mem_norm_g, v_a_w_in, v_a_q_g, v_a_k_g, v_a_rel_bias, v_b_w_in, v_b_b_in, v_b_conv_w, v_b_conv_b, v_b_ln_g, v_b_ln_b, v_mq_g, v_mk_g, v_w_mem_kv, v_w_out, v_norm2_g, v_w_gate, v_w_up, v_w_down):
    sx = 2 * lax.axis_index("x") + lax.axis_index("y")

    n_in = (NA // NSH, NBW // NSH)
    tr = lambda a: jnp.swapaxes(a, -1, -2)
    small_src = jnp.concatenate([
        jnp.pad(b_b_in, ((0, 0), (0, 512 - 448))),
        jnp.pad(b_conv_w[0], ((0, 0), (0, 512 - 192))),
        jnp.pad(jnp.concatenate([b_conv_b, b_ln_g, b_ln_b], 0), ((0, 0), (0, 512 - 192))),
        jnp.zeros((5, 512), F32)], 0)

    def gather_groups(groups, small=None):
        def src_of(l, name):
            if name == "in":
                return [_cast_bf16([tr(b_w_in[0])], "cast_in_1").reshape(1, 2, n_in[1] // 2, D) if l else
                        _cast_bf16([a_w_in[0]], "cast_in_0").reshape(1, 2, D // 2, n_in[0])]
            if name == "gu":
                return [_cast_bf16([tr(w_gate[l])], f"cast_gate_{l}").reshape(1, 2, FS // 2, D),
                        _cast_bf16([tr(w_up[l])], f"cast_up_{l}").reshape(1, 2, FS // 2, D)]
            arr, shape = {"kv": (w_mem_kv, (1, 2, 128, 2 * MEMW)), "wo": (w_out, (1, 2, 128, D)),
                          "wd": (w_down, (1, 2, FS // 2, D))}[name]
            return [_cast_bf16([arr[l]], f"cast_{name}_{l}").reshape(shape)]

        def dst_of(l, name):
            if name == "in":
                return (1, NSH, 2, n_in[1] // 2, D) if l else (1, NSH, 2, D // 2, n_in[0])
            return {"kv": (1, NSH, 2, 128, 2 * MEMW), "wo": (1, NSH, 2, 128, D),
                    "gu": (1, 2, NSH, 2, FS // 2, D), "wd": (1, NSH, 2, FS // 2, D)}[name]

        items = [(l, name) for l, names in groups for name in names]
        srcs, views = [], []
        for k, (l, name) in enumerate(items):
            srcs += src_of(l, name)
            if name == "gu":
                views += [lambda d, k=k: d[k].at[:, 0], lambda d, k=k: d[k].at[:, 1]]
            else:
                views.append(lambda d, k=k: d[k])

        def done(w, outs):
            for k, (l, name) in enumerate(items):
                if name == "in" and l == 0:
                    w["a"] = outs[k].reshape(NSH, D, n_in[0]).transpose(1, 0, 2).reshape(D, NA)
                elif name == "in":
                    w["b"] = outs[k].reshape(NBW, D)
                else:
                    shape = {"kv": (D, 2 * MEMW), "wo": (D, D), "gu": (2, FF, D), "wd": (FF, D)}
                    w[name][l] = outs[k].reshape(shape[name])

        return _gather_exchange(srcs, [dst_of(l, name) for l, name in items], views, small), done

    w = dict(a=None, b=None, kv=[None, None], wo=[None, None], gu=[None, None], wd=[None, None])
    first, first_in_done = gather_groups([(0, ["in"])], small_src)

    def first_done(w, outs):
        first_in_done(w, outs)
        small_all = outs[1]
        conv_w_full = small_all[:, 1:1 + CONVW, :192].transpose(1, 0, 2).reshape(CONVW, TOK)
        vec3 = small_all[:, 32:35, :192].transpose(1, 0, 2).reshape(3, TOK)
        w.update(b_b_in=small_all[:, 0, :448].reshape(1, NBW), conv_w=jnp.pad(conv_w_full, ((0, 1), (0, 0))),
                 conv_b=vec3[0:1], ln_g=vec3[1:2], ln_b=vec3[2:3])

    carries = {"rms1_0": (first, first_done),
               "inproj_a": gather_groups([(0, ["kv", "wo", "wd"])]),
               "attn_fwd": gather_groups([(0, ["gu"]), (1, ["in", "kv", "wo"])]),
               "ffn_up_0": gather_groups([(1, ["gu"])]),
               "ffn_down_0": gather_groups([(1, ["wd"])])}
    p = dict(
        norm1_g=norm1_g, mem_norm_g=mem_norm_g, norm2_g=norm2_g,
        a_q_g2=jnp.tile(a_q_g, (1, AH)), a_k_g2=jnp.tile(a_k_g, (1, AH)),
        mq_g4=jnp.tile(mq_g, (1, 4)), mk_g4=jnp.tile(mk_g, (1, 4)),
        rel_u=_rel_table_to_u(a_rel_bias[0]))

    def pair_sums(items, big, name):
        units = []
        for l, tensor in items:
            if tensor == "in" and l == 0:
                g = big["a"].reshape(D, NSH, n_in[0]).transpose(1, 0, 2).reshape(1, NSH, 2, D // 2, n_in[0])
            elif tensor == "in":
                g = big["b"].reshape(1, NSH, 2, n_in[1] // 2, D)
            else:
                shape = {"kv": (1, NSH, 2, 128, 2 * MEMW), "wo": (1, NSH, 2, 128, D),
                         "gu": (2, NSH, 2, FS // 2, D), "wd": (1, NSH, 2, FS // 2, D)}
                g = big[tensor][l].reshape(shape[tensor])
            units.append(g)
        return _pair_reduce(units, name)

    early_groups = [[(1, "gu")], [(0, "gu")],
                    [(1, "in"), (1, "kv"), (1, "wo"), (1, "wd"), (0, "kv"), (0, "wo"), (0, "wd")]]
    early = [item for grp in early_groups for item in grp]
    late = [(0, "in")]
    own_early, own_late = [], []

    def bwd_carry_fn(big):
        sums_b = []
        for k, grp in enumerate(early_groups):
            own, sb = pair_sums(grp, big, f"pair_reduce_early_{k}")
            own_early.extend(own)
            sums_b.extend(sb)
        return _chip_exchange(sums_b)

    def late_carry_fn(big):
        own, sb = pair_sums(late, big, "pair_reduce_late")
        own_late.extend(own)
        return _chip_exchange(sb)

    loss, grad_x, big, small, parts_early, parts_late = _local_step(
        x.reshape(T, D), mem.reshape(NB * MEMT, D), loss_target.reshape(T, D), w, p,
        carries=carries, bwd_carry_fn=bwd_carry_fn, late_carry_fn=late_carry_fn)
    items = early + late
    halves = [_quad_sum(o, pt, f"quad_sum_{name}_{l}")
              for (l, name), o, pt in zip(items, own_early + own_late, list(parts_early) + list(parts_late))]
    out_shapes = [(1, D, NA // NSH), (1, NBW // NSH, D), (2, 2 * 128, 2 * MEMW), (2, 2 * 128, D),
                  (2, FS, D), (2, FS, D), (2, FS, D)]
    target_of = {"in": lambda l: [(0, l, 0)], "kv": lambda l: [(0, 2, l)], "wo": lambda l: [(0, 3, l)],
                 "gu": lambda l: [(0, 4, l), (1, 5, l)], "wd": lambda l: [(0, 6, l)]}
    targets = [target_of[name](l) for l, name in items]
    g_a, g_b, g_kv, g_wo, g_gate, g_up, g_wd = _final_exchange(halves, out_shapes, targets)

    tot = _small_allreduce(_pack_small(dict(small, loss=loss)))
    loss = tot[61, 0]
    g_rel, clip_part = _u_to_rel_table(tot[49:61])
    g_rel = jnp.concatenate([g_rel[:, :191], g_rel[:, 191:] + _rowsum(clip_part)], axis=1)
    b_in_full = jnp.concatenate([tot[15:16], tot[16:17, :512], tot[17:18, :MEMW]], axis=1)
    g_small = dict(
        norm1_g=tot[0:2], mem_norm_g=tot[2:4], norm2_g=tot[4:6],
        a_q_g=tot[6:7, :HD], a_k_g=tot[7:8, :HD], a_rel_bias=g_rel[None],
        b_b_in=lax.dynamic_slice(b_in_full, (0, sx * 448), (1, 448)),
        b_conv_w=lax.dynamic_slice(tot[18:49, :TOK], (0, sx * 192), (CONVW, 192))[None],
        b_conv_b=lax.dynamic_slice(tot[12:13, :TOK], (0, sx * 192), (1, 192)),
        b_ln_g=lax.dynamic_slice(tot[13:14, :TOK], (0, sx * 192), (1, 192)),
        b_ln_b=lax.dynamic_slice(tot[14:15, :TOK], (0, sx * 192), (1, 192)),
        mq_g=tot[8:10, :HD], mk_g=tot[10:12, :HD])

    names = ["norm1_g", "mem_norm_g", "a_w_in", "a_q_g", "a_k_g", "a_rel_bias", "b_w_in", "b_b_in",
             "b_conv_w", "b_conv_b", "b_ln_g", "b_ln_b", "mq_g", "mk_g", "w_mem_kv", "w_out",
             "norm2_g", "w_gate", "w_up", "w_down"]
    weights = dict(zip(names, [norm1_g, mem_norm_g, a_w_in, a_q_g, a_k_g, a_rel_bias, b_w_in, b_b_in,
                               b_conv_w, b_conv_b, b_ln_g, b_ln_b, mq_g, mk_g, w_mem_kv, w_out,
                               norm2_g, w_gate, w_up, w_down]))
    ms = dict(zip(names, [m_norm1_g, m_mem_norm_g, m_a_w_in, m_a_q_g, m_a_k_g, m_a_rel_bias, m_b_w_in,
                          m_b_b_in, m_b_conv_w, m_b_conv_b, m_b_ln_g, m_b_ln_b, m_mq_g, m_mk_g,
                          m_w_mem_kv, m_w_out, m_norm2_g, m_w_gate, m_w_up, m_w_down]))
    vs = dict(zip(names, [v_norm1_g, v_mem_norm_g, v_a_w_in, v_a_q_g, v_a_k_g, v_a_rel_bias, v_b_w_in,
                          v_b_b_in, v_b_conv_w, v_b_conv_b, v_b_ln_g, v_b_ln_b, v_mq_g, v_mk_g,
                          v_w_mem_kv, v_w_out, v_norm2_g, v_w_gate, v_w_up, v_w_down]))
    grads = dict(g_small)
    grads.update(a_w_in=g_a, b_w_in=g_b, w_mem_kv=g_kv, w_out=g_wo, w_gate=g_gate, w_up=g_up, w_down=g_wd)
    big_names = ["a_w_in", "b_w_in", "w_mem_kv", "w_out", "w_gate", "w_up", "w_down"]
    small_names = [n for n in names if n not in big_names]
    delta, new_m, new_v = {}, {}, {}
    for n in big_names:
        if n in ("b_w_in", "w_gate", "w_up"):
            outs = _adamw_big(tr(weights[n]), grads[n], tr(ms[n]), tr(vs[n]), f"adamw_{n}")
            delta[n], new_m[n], new_v[n] = [tr(o) for o in outs]
            grads[n] = tr(grads[n])
        else:
            delta[n], new_m[n], new_v[n] = _adamw_big(weights[n], grads[n], ms[n], vs[n], f"adamw_{n}")
    as2d = lambda a: a.reshape(-1, a.shape[-1])
    d_s, m_s, v_s = _adamw_small([as2d(weights[n]) for n in small_names], [as2d(grads[n]) for n in small_names],
                                 [as2d(ms[n]) for n in small_names], [as2d(vs[n]) for n in small_names])
    for i, n in enumerate(small_names):
        delta[n] = d_s[i].reshape(weights[n].shape)
        new_m[n] = m_s[i].reshape(weights[n].shape)
        new_v[n] = v_s[i].reshape(weights[n].shape)

    return (loss, grad_x.reshape(NB, SEQ, D), *[grads[n] for n in names], *[delta[n] for n in names],
            *[new_m[n] for n in names], *[new_v[n] for n in names])


def _rowsum(a):
    def body(a_ref, o_ref):
        o_ref[...] = jnp.sum(a_ref[...], axis=1, keepdims=True)

    vm = BS(memory_space=pltpu.VMEM)
    return pl.pallas_call(body, in_specs=[vm], out_specs=vm, out_shape=_sds((a.shape[0], 1), F32),
                          compiler_params=_cp(), name="rowsum")(a)
```
